```python
import jax
import jax.numpy as jnp
from jax import lax
import numpy as np


D_MODEL = 2048
BATCH = 8
SEQ = 4096
DEPTH = 1

MEM_LEN = 256
BLOCK = 128
EPS = 1e-6
NEG_INF = -1e30

SWA_Q_HEADS = 16
SWA_KV_HEADS = 2
SWA_HEAD_DIM = 64
WINDOW = 128

MLA_HEADS = 4
MLA_Q_RANK = 512
MLA_KV_RANK = 512
MLA_NOPE_DIM = 128
MLA_ROPE_DIM = 64
MLA_V_DIM = 128
ROPE_THETA = 10000.0

MEM_HEADS = 4
MEM_HEAD_DIM = 128

MIX_WIDTH = SWA_Q_HEADS * SWA_HEAD_DIM + MLA_HEADS * MLA_V_DIM + MEM_HEADS * MEM_HEAD_DIM
IN_SIZES = (SWA_Q_HEADS * SWA_HEAD_DIM, SWA_KV_HEADS * SWA_HEAD_DIM, SWA_KV_HEADS * SWA_HEAD_DIM,
            MLA_Q_RANK, MLA_KV_RANK, MLA_ROPE_DIM, MEM_HEADS * MEM_HEAD_DIM)
IN_WIDTH = sum(IN_SIZES)
IN_SPLITS = tuple(int(v) for v in np.cumsum(IN_SIZES)[:-1])

D_FF = ((8 * D_MODEL + 3 * 256 - 1) // (3 * 256)) * 256

kernel_name = "hymba_swa_sink_mla_memory_swiglu"


def rms_norm(x, g):
    xf = x.astype(jnp.float32)
    y = xf * lax.rsqrt(jnp.mean(xf * xf, axis=-1, keepdims=True) + EPS)
    return (y * g.astype(jnp.float32)).astype(x.dtype)


def alibi_slopes(n):
    return 2.0 ** (-8.0 * jnp.arange(1, n + 1, dtype=jnp.float32) / n)


def apply_rope(x, cos, sin):
    x1, x2 = jnp.split(x.astype(jnp.float32), 2, axis=-1)
    return jnp.concatenate([x1 * cos - x2 * sin, x1 * sin + x2 * cos], axis=-1).astype(x.dtype)


def swa_sink_attention(q, k, v, pos, sinks):
    b, s, _, d = q.shape
    nb = s // BLOCK
    g = SWA_Q_HEADS // SWA_KV_HEADS
    qb = q.reshape(b, nb, BLOCK, SWA_KV_HEADS, g, d)

    def with_prev(t):
        tb = t.reshape((b, nb, BLOCK) + t.shape[2:])
        prev = jnp.concatenate([jnp.zeros_like(tb[:, :1]), tb[:, :-1]], axis=1)
        return jnp.concatenate([prev, tb], axis=2)

    kb, vb, pk = with_prev(k), with_prev(v), with_prev(pos)
    pq = pos.reshape(b, nb, BLOCK)
    scores = jnp.einsum('bnqhgd,bnkhd->bnhgqk', qb, kb,
                        preferred_element_type=jnp.float32) * (d ** -0.5)
    dist = jnp.abs(pq[:, :, :, None] - pk[:, :, None, :]).astype(jnp.float32)
    slopes = alibi_slopes(SWA_Q_HEADS).reshape(SWA_KV_HEADS, g)
    scores = scores - slopes[None, None, :, :, None, None] * dist[:, :, None, None]
    qi = jnp.arange(BLOCK)[:, None] + BLOCK
    ki = jnp.arange(2 * BLOCK)[None, :]
    band = (ki <= qi) & (qi - ki < WINDOW)
    not_first = jnp.arange(nb)[:, None, None] > 0
    valid = band[None] & (not_first | (ki >= BLOCK)[None])
    scores = jnp.where(valid[None, :, None, None], scores, NEG_INF)
    sink = sinks.astype(jnp.float32).reshape(SWA_KV_HEADS, g)[None, None, :, :, None, None]
    m = jnp.maximum(jnp.max(scores, axis=-1, keepdims=True), sink)
    p = jnp.exp(scores - m)
    p = p / (jnp.sum(p, axis=-1, keepdims=True) + jnp.exp(sink - m))
    out = jnp.einsum('bnhgqk,bnkhd->bnqhgd', p.astype(v.dtype), vb)
    return out.reshape(b, s, SWA_Q_HEADS * d)


def mla_causal_attention(q_nope, q_rope, k_nope, k_rope, v):
    b, s, h, _ = q_nope.shape
    nb = s // BLOCK
    scale = (MLA_NOPE_DIM + MLA_ROPE_DIM) ** -0.5
    k_idx = jnp.arange(s)

    def to_blocks(t):
        return jnp.moveaxis(t.reshape((b, nb, BLOCK) + t.shape[2:]), 1, 0)

    def one_block(args):
        qn, qr, i = args
        sc = (jnp.einsum('bqhd,bkhd->bhqk', qn, k_nope, preferred_element_type=jnp.float32)
              + jnp.einsum('bqhd,bkd->bhqk', qr, k_rope, preferred_element_type=jnp.float32)) * scale
        q_idx = i * BLOCK + jnp.arange(BLOCK)
        sc = jnp.where(k_idx[None, :] <= q_idx[:, None], sc, NEG_INF)
        p = jax.nn.softmax(sc, axis=-1).astype(v.dtype)
        return jnp.einsum('bhqk,bkhd->bqhd', p, v)

    out = lax.map(one_block, (to_blocks(q_nope), to_blocks(q_rope), jnp.arange(nb)))
    return jnp.moveaxis(out, 0, 1).reshape(b, s, h * MLA_V_DIM)


def memory_cross_attention(q, k, v):
    b, s, h, d = q.shape
    sc = jnp.einsum('bshd,bmhd->bhsm', q, k, preferred_element_type=jnp.float32) * (d ** -0.5)
    p = jax.nn.softmax(sc, axis=-1).astype(v.dtype)
    return jnp.einsum('bhsm,bmhd->bshd', p, v).reshape(b, s, h * d)


def _fwd_setup_inputs(seed: int = 0) -> dict:
    key = jax.random.key(seed)
    ks = iter(jax.random.split(key, 32))

    def nrm(shape, fan_in):
        return jax.random.normal(next(ks), shape, jnp.float32) * (fan_in ** -0.5)

    def gain(n):
        return 1.0 + 0.02 * jax.random.normal(next(ks), (DEPTH, n), jnp.float32)

    x = jax.random.normal(next(ks), (BATCH, SEQ, D_MODEL), jnp.float32)
    mem = jax.random.normal(next(ks), (BATCH, MEM_LEN, D_MODEL), jnp.float32)
    offsets = jax.random.randint(next(ks), (BATCH, 1), 0, 1024, dtype=jnp.int32)
    positions = offsets + jnp.arange(SEQ, dtype=jnp.int32)[None, :]
    return {
        "x": x,
        "mem": mem,
        "positions": positions,
        "attn_norm_g": gain(D_MODEL),
        "w_in": nrm((DEPTH, D_MODEL, IN_WIDTH), D_MODEL),
        "swa_q_norm_g": gain(SWA_HEAD_DIM),
        "swa_k_norm_g": gain(SWA_HEAD_DIM),
        "swa_sinks": 0.5 * jax.random.normal(next(ks), (DEPTH, SWA_Q_HEADS), jnp.float32),
        "mla_cq_norm_g": gain(MLA_Q_RANK),
        "mla_ckv_norm_g": gain(MLA_KV_RANK),
        "w_uq": nrm((DEPTH, MLA_Q_RANK, MLA_HEADS * (MLA_NOPE_DIM + MLA_ROPE_DIM)), MLA_Q_RANK),
        "w_ukv": nrm((DEPTH, MLA_KV_RANK, MLA_HEADS * (MLA_NOPE_DIM + MLA_V_DIM)), MLA_KV_RANK),
        "mla_qn_norm_g": gain(MLA_NOPE_DIM),
        "mla_qr_norm_g": gain(MLA_ROPE_DIM),
        "mla_kn_norm_g": gain(MLA_NOPE_DIM),
        "mla_kr_norm_g": gain(MLA_ROPE_DIM),
        "mem_norm_g": gain(D_MODEL),
        "w_mem_kv": nrm((DEPTH, D_MODEL, 2 * MEM_HEADS * MEM_HEAD_DIM), D_MODEL),
        "mem_q_norm_g": gain(MEM_HEAD_DIM),
        "mem_k_norm_g": gain(MEM_HEAD_DIM),
        "w_out": nrm((DEPTH, MIX_WIDTH, D_MODEL), MIX_WIDTH),
        "ffn_norm_g": gain(D_MODEL),
        "w_gate": nrm((DEPTH, D_MODEL, D_FF), D_MODEL),
        "w_up": nrm((DEPTH, D_MODEL, D_FF), D_MODEL),
        "w_down": nrm((DEPTH, D_FF, D_MODEL), D_FF),
    }


def _fwd_reference(x, mem, positions, attn_norm_g, w_in, swa_q_norm_g, swa_k_norm_g, swa_sinks,
              mla_cq_norm_g, mla_ckv_norm_g, w_uq, w_ukv, mla_qn_norm_g, mla_qr_norm_g,
              mla_kn_norm_g, mla_kr_norm_g, mem_norm_g, w_mem_kv, mem_q_norm_g, mem_k_norm_g,
              w_out, ffn_norm_g, w_gate, w_up, w_down):
    b, s, _ = x.shape
    m_len = mem.shape[1]
    inv_freq = ROPE_THETA ** (-jnp.arange(0, MLA_ROPE_DIM, 2, dtype=jnp.float32) / MLA_ROPE_DIM)
    ang = positions.astype(jnp.float32)[..., None] * inv_freq
    cos, sin = jnp.cos(ang), jnp.sin(ang)
    h = x
    for l in range(DEPTH):
        hn = rms_norm(h, attn_norm_g[l])
        proj = hn @ w_in[l]
        q_a, k_a, v_a, c_q, c_kv, k_r, q_m = jnp.split(proj, IN_SPLITS, axis=-1)

        q_a = rms_norm(q_a.reshape(b, s, SWA_Q_HEADS, SWA_HEAD_DIM), swa_q_norm_g[l])
        k_a = rms_norm(k_a.reshape(b, s, SWA_KV_HEADS, SWA_HEAD_DIM), swa_k_norm_g[l])
        v_a = v_a.reshape(b, s, SWA_KV_HEADS, SWA_HEAD_DIM)
        y_a = swa_sink_attention(q_a, k_a, v_a, positions, swa_sinks[l])

        q_b = (rms_norm(c_q, mla_cq_norm_g[l]) @ w_uq[l]).reshape(
            b, s, MLA_HEADS, MLA_NOPE_DIM + MLA_ROPE_DIM)
        kv_b = (rms_norm(c_kv, mla_ckv_norm_g[l]) @ w_ukv[l]).reshape(
            b, s, MLA_HEADS, MLA_NOPE_DIM + MLA_V_DIM)
        q_nope = rms_norm(q_b[..., :MLA_NOPE_DIM], mla_qn_norm_g[l])
        q_rope = apply_rope(rms_norm(q_b[..., MLA_NOPE_DIM:], mla_qr_norm_g[l]),
                            cos[:, :, None], sin[:, :, None])
        k_nope = rms_norm(kv_b[..., :MLA_NOPE_DIM], mla_kn_norm_g[l])
        v_b = kv_b[..., MLA_NOPE_DIM:]
        k_rope = apply_rope(rms_norm(k_r, mla_kr_norm_g[l]), cos, sin)
        y_b = mla_causal_attention(q_nope, q_rope, k_nope, k_rope, v_b)

        q_m = rms_norm(q_m.reshape(b, s, MEM_HEADS, MEM_HEAD_DIM), mem_q_norm_g[l])
        kv_m = (rms_norm(mem, mem_norm_g[l]) @ w_mem_kv[l]).reshape(
            b, m_len, 2, MEM_HEADS, MEM_HEAD_DIM)
        k_m = rms_norm(kv_m[:, :, 0], mem_k_norm_g[l])
        v_m = kv_m[:, :, 1]
        y_m = memory_cross_attention(q_m, k_m, v_m)

        h = h + jnp.concatenate([y_a, y_b, y_m], axis=-1) @ w_out[l]

        fn = rms_norm(h, ffn_norm_g[l])
        h = h + (jax.nn.silu(fn @ w_gate[l]) * (fn @ w_up[l])) @ w_down[l]
    return h


import jax as _jax
import jax.numpy as _jnp

TWIN_FORMAT = 'train_step'
FWD_PARAMS = ['x', 'mem', 'positions', 'attn_norm_g', 'w_in', 'swa_q_norm_g', 'swa_k_norm_g', 'swa_sinks', 'mla_cq_norm_g', 'mla_ckv_norm_g', 'w_uq', 'w_ukv', 'mla_qn_norm_g', 'mla_qr_norm_g', 'mla_kn_norm_g', 'mla_kr_norm_g', 'mem_norm_g', 'w_mem_kv', 'mem_q_norm_g', 'mem_k_norm_g', 'w_out', 'ffn_norm_g', 'w_gate', 'w_up', 'w_down']
TWIN_WEIGHTS = ['attn_norm_g', 'w_in', 'swa_q_norm_g', 'swa_k_norm_g', 'swa_sinks', 'mla_cq_norm_g', 'mla_ckv_norm_g', 'w_uq', 'w_ukv', 'mla_qn_norm_g', 'mla_qr_norm_g', 'mla_kn_norm_g', 'mla_kr_norm_g', 'mem_norm_g', 'w_mem_kv', 'mem_q_norm_g', 'mem_k_norm_g', 'w_out', 'ffn_norm_g', 'w_gate', 'w_up', 'w_down']
TWIN_DIFF_INPUT = 'x'
TWIN_INPUTS = ['x', 'mem', 'positions', 'attn_norm_g', 'w_in', 'swa_q_norm_g', 'swa_k_norm_g', 'swa_sinks', 'mla_cq_norm_g', 'mla_ckv_norm_g', 'w_uq', 'w_ukv', 'mla_qn_norm_g', 'mla_qr_norm_g', 'mla_kn_norm_g', 'mla_kr_norm_g', 'mem_norm_g', 'w_mem_kv', 'mem_q_norm_g', 'mem_k_norm_g', 'w_out', 'ffn_norm_g', 'w_gate', 'w_up', 'w_down', 'loss_target', 'm_attn_norm_g', 'm_w_in', 'm_swa_q_norm_g', 'm_swa_k_norm_g', 'm_swa_sinks', 'm_mla_cq_norm_g', 'm_mla_ckv_norm_g', 'm_w_uq', 'm_w_ukv', 'm_mla_qn_norm_g', 'm_mla_qr_norm_g', 'm_mla_kn_norm_g', 'm_mla_kr_norm_g', 'm_mem_norm_g', 'm_w_mem_kv', 'm_mem_q_norm_g', 'm_mem_k_norm_g', 'm_w_out', 'm_ffn_norm_g', 'm_w_gate', 'm_w_up', 'm_w_down', 'v_attn_norm_g', 'v_w_in', 'v_swa_q_norm_g', 'v_swa_k_norm_g', 'v_swa_sinks', 'v_mla_cq_norm_g', 'v_mla_ckv_norm_g', 'v_w_uq', 'v_w_ukv', 'v_mla_qn_norm_g', 'v_mla_qr_norm_g', 'v_mla_kn_norm_g', 'v_mla_kr_norm_g', 'v_mem_norm_g', 'v_w_mem_kv', 'v_mem_q_norm_g', 'v_mem_k_norm_g', 'v_w_out', 'v_ffn_norm_g', 'v_w_gate', 'v_w_up', 'v_w_down']
TWIN_OUTPUTS = ['loss', 'grad_x', 'grad_attn_norm_g', 'grad_w_in', 'grad_swa_q_norm_g', 'grad_swa_k_norm_g', 'grad_swa_sinks', 'grad_mla_cq_norm_g', 'grad_mla_ckv_norm_g', 'grad_w_uq', 'grad_w_ukv', 'grad_mla_qn_norm_g', 'grad_mla_qr_norm_g', 'grad_mla_kn_norm_g', 'grad_mla_kr_norm_g', 'grad_mem_norm_g', 'grad_w_mem_kv', 'grad_mem_q_norm_g', 'grad_mem_k_norm_g', 'grad_w_out', 'grad_ffn_norm_g', 'grad_w_gate', 'grad_w_up', 'grad_w_down', 'delta_attn_norm_g', 'delta_w_in', 'delta_swa_q_norm_g', 'delta_swa_k_norm_g', 'delta_swa_sinks', 'delta_mla_cq_norm_g', 'delta_mla_ckv_norm_g', 'delta_w_uq', 'delta_w_ukv', 'delta_mla_qn_norm_g', 'delta_mla_qr_norm_g', 'delta_mla_kn_norm_g', 'delta_mla_kr_norm_g', 'delta_mem_norm_g', 'delta_w_mem_kv', 'delta_mem_q_norm_g', 'delta_mem_k_norm_g', 'delta_w_out', 'delta_ffn_norm_g', 'delta_w_gate', 'delta_w_up', 'delta_w_down', 'new_m_attn_norm_g', 'new_m_w_in', 'new_m_swa_q_norm_g', 'new_m_swa_k_norm_g', 'new_m_swa_sinks', 'new_m_mla_cq_norm_g', 'new_m_mla_ckv_norm_g', 'new_m_w_uq', 'new_m_w_ukv', 'new_m_mla_qn_norm_g', 'new_m_mla_qr_norm_g', 'new_m_mla_kn_norm_g', 'new_m_mla_kr_norm_g', 'new_m_mem_norm_g', 'new_m_w_mem_kv', 'new_m_mem_q_norm_g', 'new_m_mem_k_norm_g', 'new_m_w_out', 'new_m_ffn_norm_g', 'new_m_w_gate', 'new_m_w_up', 'new_m_w_down', 'new_v_attn_norm_g', 'new_v_w_in', 'new_v_swa_q_norm_g', 'new_v_swa_k_norm_g', 'new_v_swa_sinks', 'new_v_mla_cq_norm_g', 'new_v_mla_ckv_norm_g', 'new_v_w_uq', 'new_v_w_ukv', 'new_v_mla_qn_norm_g', 'new_v_mla_qr_norm_g', 'new_v_mla_kn_norm_g', 'new_v_mla_kr_norm_g', 'new_v_mem_norm_g', 'new_v_w_mem_kv', 'new_v_mem_q_norm_g', 'new_v_mem_k_norm_g', 'new_v_w_out', 'new_v_ffn_norm_g', 'new_v_w_gate', 'new_v_w_up', 'new_v_w_down']
TWIN_LEAF_KINDS = {'loss': 'loss', 'grad_x': 'grad_x', 'grad_attn_norm_g': 'grad_w', 'grad_w_in': 'grad_w', 'grad_swa_q_norm_g': 'grad_w', 'grad_swa_k_norm_g': 'grad_w', 'grad_swa_sinks': 'grad_w', 'grad_mla_cq_norm_g': 'grad_w', 'grad_mla_ckv_norm_g': 'grad_w', 'grad_w_uq': 'grad_w', 'grad_w_ukv': 'grad_w', 'grad_mla_qn_norm_g': 'grad_w', 'grad_mla_qr_norm_g': 'grad_w', 'grad_mla_kn_norm_g': 'grad_w', 'grad_mla_kr_norm_g': 'grad_w', 'grad_mem_norm_g': 'grad_w', 'grad_w_mem_kv': 'grad_w', 'grad_mem_q_norm_g': 'grad_w', 'grad_mem_k_norm_g': 'grad_w', 'grad_w_out': 'grad_w', 'grad_ffn_norm_g': 'grad_w', 'grad_w_gate': 'grad_w', 'grad_w_up': 'grad_w', 'grad_w_down': 'grad_w', 'delta_attn_norm_g': 'delta_w', 'delta_w_in': 'delta_w', 'delta_swa_q_norm_g': 'delta_w', 'delta_swa_k_norm_g': 'delta_w', 'delta_swa_sinks': 'delta_w', 'delta_mla_cq_norm_g': 'delta_w', 'delta_mla_ckv_norm_g': 'delta_w', 'delta_w_uq': 'delta_w', 'delta_w_ukv': 'delta_w', 'delta_mla_qn_norm_g': 'delta_w', 'delta_mla_qr_norm_g': 'delta_w', 'delta_mla_kn_norm_g': 'delta_w', 'delta_mla_kr_norm_g': 'delta_w', 'delta_mem_norm_g': 'delta_w', 'delta_w_mem_kv': 'delta_w', 'delta_mem_q_norm_g': 'delta_w', 'delta_mem_k_norm_g': 'delta_w', 'delta_w_out': 'delta_w', 'delta_ffn_norm_g': 'delta_w', 'delta_w_gate': 'delta_w', 'delta_w_up': 'delta_w', 'delta_w_down': 'delta_w', 'new_m_attn_norm_g': 'new_m', 'new_m_w_in': 'new_m', 'new_m_swa_q_norm_g': 'new_m', 'new_m_swa_k_norm_g': 'new_m', 'new_m_swa_sinks': 'new_m', 'new_m_mla_cq_norm_g': 'new_m', 'new_m_mla_ckv_norm_g': 'new_m', 'new_m_w_uq': 'new_m', 'new_m_w_ukv': 'new_m', 'new_m_mla_qn_norm_g': 'new_m', 'new_m_mla_qr_norm_g': 'new_m', 'new_m_mla_kn_norm_g': 'new_m', 'new_m_mla_kr_norm_g': 'new_m', 'new_m_mem_norm_g': 'new_m', 'new_m_w_mem_kv': 'new_m', 'new_m_mem_q_norm_g': 'new_m', 'new_m_mem_k_norm_g': 'new_m', 'new_m_w_out': 'new_m', 'new_m_ffn_norm_g': 'new_m', 'new_m_w_gate': 'new_m', 'new_m_w_up': 'new_m', 'new_m_w_down': 'new_m', 'new_v_attn_norm_g': 'new_v', 'new_v_w_in': 'new_v', 'new_v_swa_q_norm_g': 'new_v', 'new_v_swa_k_norm_g': 'new_v', 'new_v_swa_sinks': 'new_v', 'new_v_mla_cq_norm_g': 'new_v', 'new_v_mla_ckv_norm_g': 'new_v', 'new_v_w_uq': 'new_v', 'new_v_w_ukv': 'new_v', 'new_v_mla_qn_norm_g': 'new_v', 'new_v_mla_qr_norm_g': 'new_v', 'new_v_mla_kn_norm_g': 'new_v', 'new_v_mla_kr_norm_g': 'new_v', 'new_v_mem_norm_g': 'new_v', 'new_v_w_mem_kv': 'new_v', 'new_v_mem_q_norm_g': 'new_v', 'new_v_mem_k_norm_g': 'new_v', 'new_v_w_out': 'new_v', 'new_v_ffn_norm_g': 'new_v', 'new_v_w_gate': 'new_v', 'new_v_w_up': 'new_v', 'new_v_w_down': 'new_v'}


def _forward(args):
    return _fwd_reference(*[args[k] for k in FWD_PARAMS])


def _output_shape():
    def fwd():
        inp = _fwd_setup_inputs(0)
        return _fwd_reference(*[inp[k] for k in FWD_PARAMS])
    out = _jax.eval_shape(fwd)
    return out.shape, out.dtype

N_MICROBATCH = 1
ADAM_LR = 0.001
ADAM_B1 = 0.9
ADAM_B2 = 0.999
ADAM_EPS = 1e-08
ADAM_WD = 0.01
ADAM_STEP = 10
PER_EXAMPLE_BATCH_AXIS = {'x': 0, 'mem': 0, 'positions': 0, 'loss_target': 0}
SHARED_INPUTS = []
_WEIGHT_DTYPES = {'attn_norm_g': _jnp.float32, 'w_in': _jnp.float32, 'swa_q_norm_g': _jnp.float32, 'swa_k_norm_g': _jnp.float32, 'swa_sinks': _jnp.float32, 'mla_cq_norm_g': _jnp.float32, 'mla_ckv_norm_g': _jnp.float32, 'w_uq': _jnp.float32, 'w_ukv': _jnp.float32, 'mla_qn_norm_g': _jnp.float32, 'mla_qr_norm_g': _jnp.float32, 'mla_kn_norm_g': _jnp.float32, 'mla_kr_norm_g': _jnp.float32, 'mem_norm_g': _jnp.float32, 'w_mem_kv': _jnp.float32, 'mem_q_norm_g': _jnp.float32, 'mem_k_norm_g': _jnp.float32, 'w_out': _jnp.float32, 'ffn_norm_g': _jnp.float32, 'w_gate': _jnp.float32, 'w_up': _jnp.float32, 'w_down': _jnp.float32}
MOMENT_SCALE = {'attn_norm_g': 6.621176e-01, 'w_in': 8.254546e-02, 'swa_q_norm_g': 9.955668e+00, 'swa_k_norm_g': 1.000492e+01, 'swa_sinks': 1.706202e+01, 'mla_cq_norm_g': 4.290601e-02, 'mla_ckv_norm_g': 2.006777e-01, 'w_uq': 3.334137e-02, 'w_ukv': 4.501373e-02, 'mla_qn_norm_g': 2.996436e-01, 'mla_qr_norm_g': 2.910148e-01, 'mla_kn_norm_g': 2.997836e-01, 'mla_kr_norm_g': 2.933487e-01, 'mem_norm_g': 4.858964e-02, 'w_mem_kv': 4.094424e-02, 'mem_q_norm_g': 6.109604e-01, 'mem_k_norm_g': 6.115195e-01, 'w_out': 6.406451e-02, 'ffn_norm_g': 1.236951e+01, 'w_gate': 7.296782e-02, 'w_up': 7.414071e-02, 'w_down': 1.119072e-01}


def _to_microbatches(a, axis):
    t = _jnp.moveaxis(a, axis, 0)
    t = t.reshape((N_MICROBATCH, t.shape[0] // N_MICROBATCH) + t.shape[1:])
    return _jnp.moveaxis(t, 1, axis + 1)


def setup_inputs(seed: int = 0) -> dict:
    inp = _fwd_setup_inputs(seed)
    key = _jax.random.fold_in(_jax.random.key(seed), 7919)
    shape, _ = _output_shape()
    out = dict(inp)
    out["loss_target"] = _jax.random.normal(_jax.random.fold_in(key, 0), shape, _jnp.float32)
    for i, name in enumerate(TWIN_WEIGHTS):
        w = inp[name].astype(_jnp.float32)
        if MOMENT_SCALE is None:
            s = _jnp.sqrt(_jnp.mean(_jnp.square(w)) + 1e-30)
        else:
            s = MOMENT_SCALE[name]
        km, kv = _jax.random.split(_jax.random.fold_in(key, i + 1))
        out[name] = w
        out["m_" + name] = s * _jax.random.normal(km, w.shape, _jnp.float32)
        out["v_" + name] = (s * s) * _jax.random.uniform(kv, w.shape, _jnp.float32, 0.5, 1.5)
    if N_MICROBATCH > 1:
        for name, axis in PER_EXAMPLE_BATCH_AXIS.items():
            out[name] = _to_microbatches(out[name], axis)
    return {'x': out['x'], 'mem': out['mem'], 'positions': out['positions'], 'attn_norm_g': out['attn_norm_g'], 'w_in': out['w_in'], 'swa_q_norm_g': out['swa_q_norm_g'], 'swa_k_norm_g': out['swa_k_norm_g'], 'swa_sinks': out['swa_sinks'], 'mla_cq_norm_g': out['mla_cq_norm_g'], 'mla_ckv_norm_g': out['mla_ckv_norm_g'], 'w_uq': out['w_uq'], 'w_ukv': out['w_ukv'], 'mla_qn_norm_g': out['mla_qn_norm_g'], 'mla_qr_norm_g': out['mla_qr_norm_g'], 'mla_kn_norm_g': out['mla_kn_norm_g'], 'mla_kr_norm_g': out['mla_kr_norm_g'], 'mem_norm_g': out['mem_norm_g'], 'w_mem_kv': out['w_mem_kv'], 'mem_q_norm_g': out['mem_q_norm_g'], 'mem_k_norm_g': out['mem_k_norm_g'], 'w_out': out['w_out'], 'ffn_norm_g': out['ffn_norm_g'], 'w_gate': out['w_gate'], 'w_up': out['w_up'], 'w_down': out['w_down'], 'loss_target': out['loss_target'], 'm_attn_norm_g': out['m_attn_norm_g'], 'm_w_in': out['m_w_in'], 'm_swa_q_norm_g': out['m_swa_q_norm_g'], 'm_swa_k_norm_g': out['m_swa_k_norm_g'], 'm_swa_sinks': out['m_swa_sinks'], 'm_mla_cq_norm_g': out['m_mla_cq_norm_g'], 'm_mla_ckv_norm_g': out['m_mla_ckv_norm_g'], 'm_w_uq': out['m_w_uq'], 'm_w_ukv': out['m_w_ukv'], 'm_mla_qn_norm_g': out['m_mla_qn_norm_g'], 'm_mla_qr_norm_g': out['m_mla_qr_norm_g'], 'm_mla_kn_norm_g': out['m_mla_kn_norm_g'], 'm_mla_kr_norm_g': out['m_mla_kr_norm_g'], 'm_mem_norm_g': out['m_mem_norm_g'], 'm_w_mem_kv': out['m_w_mem_kv'], 'm_mem_q_norm_g': out['m_mem_q_norm_g'], 'm_mem_k_norm_g': out['m_mem_k_norm_g'], 'm_w_out': out['m_w_out'], 'm_ffn_norm_g': out['m_ffn_norm_g'], 'm_w_gate': out['m_w_gate'], 'm_w_up': out['m_w_up'], 'm_w_down': out['m_w_down'], 'v_attn_norm_g': out['v_attn_norm_g'], 'v_w_in': out['v_w_in'], 'v_swa_q_norm_g': out['v_swa_q_norm_g'], 'v_swa_k_norm_g': out['v_swa_k_norm_g'], 'v_swa_sinks': out['v_swa_sinks'], 'v_mla_cq_norm_g': out['v_mla_cq_norm_g'], 'v_mla_ckv_norm_g': out['v_mla_ckv_norm_g'], 'v_w_uq': out['v_w_uq'], 'v_w_ukv': out['v_w_ukv'], 'v_mla_qn_norm_g': out['v_mla_qn_norm_g'], 'v_mla_qr_norm_g': out['v_mla_qr_norm_g'], 'v_mla_kn_norm_g': out['v_mla_kn_norm_g'], 'v_mla_kr_norm_g': out['v_mla_kr_norm_g'], 'v_mem_norm_g': out['v_mem_norm_g'], 'v_w_mem_kv': out['v_w_mem_kv'], 'v_mem_q_norm_g': out['v_mem_q_norm_g'], 'v_mem_k_norm_g': out['v_mem_k_norm_g'], 'v_w_out': out['v_w_out'], 'v_ffn_norm_g': out['v_ffn_norm_g'], 'v_w_gate': out['v_w_gate'], 'v_w_up': out['v_w_up'], 'v_w_down': out['v_w_down']}


def _loss(weights, diff, rest, loss_target):
    with _jax.named_scope("forward"):
        args = {**rest, TWIN_DIFF_INPUT: diff, **{k: w.astype(_WEIGHT_DTYPES[k]) for k, w in weights.items()}}
        y = _forward(args)
    with _jax.named_scope("loss_head"):
        err = _jnp.square(y.astype(_jnp.float32) - loss_target)
        return 0.5 * _jnp.sum(_jnp.mean(err, axis=-1)) if err.ndim else 0.5 * err


def _adamw(w, g, m, v):
    m = ADAM_B1 * m + (1.0 - ADAM_B1) * g
    v = ADAM_B2 * v + (1.0 - ADAM_B2) * _jnp.square(g)
    m_hat = m / (1.0 - ADAM_B1 ** ADAM_STEP)
    v_hat = v / (1.0 - ADAM_B2 ** ADAM_STEP)
    delta = -ADAM_LR * (m_hat / (_jnp.sqrt(v_hat) + ADAM_EPS) + ADAM_WD * w)
    return delta, m, v


def reference(x, mem, positions, attn_norm_g, w_in, swa_q_norm_g, swa_k_norm_g, swa_sinks, mla_cq_norm_g, mla_ckv_norm_g, w_uq, w_ukv, mla_qn_norm_g, mla_qr_norm_g, mla_kn_norm_g, mla_kr_norm_g, mem_norm_g, w_mem_kv, mem_q_norm_g, mem_k_norm_g, w_out, ffn_norm_g, w_gate, w_up, w_down, loss_target, m_attn_norm_g, m_w_in, m_swa_q_norm_g, m_swa_k_norm_g, m_swa_sinks, m_mla_cq_norm_g, m_mla_ckv_norm_g, m_w_uq, m_w_ukv, m_mla_qn_norm_g, m_mla_qr_norm_g, m_mla_kn_norm_g, m_mla_kr_norm_g, m_mem_norm_g, m_w_mem_kv, m_mem_q_norm_g, m_mem_k_norm_g, m_w_out, m_ffn_norm_g, m_w_gate, m_w_up, m_w_down, v_attn_norm_g, v_w_in, v_swa_q_norm_g, v_swa_k_norm_g, v_swa_sinks, v_mla_cq_norm_g, v_mla_ckv_norm_g, v_w_uq, v_w_ukv, v_mla_qn_norm_g, v_mla_qr_norm_g, v_mla_kn_norm_g, v_mla_kr_norm_g, v_mem_norm_g, v_w_mem_kv, v_mem_q_norm_g, v_mem_k_norm_g, v_w_out, v_ffn_norm_g, v_w_gate, v_w_up, v_w_down):
    given = dict(x=x, mem=mem, positions=positions, attn_norm_g=attn_norm_g, w_in=w_in, swa_q_norm_g=swa_q_norm_g, swa_k_norm_g=swa_k_norm_g, swa_sinks=swa_sinks, mla_cq_norm_g=mla_cq_norm_g, mla_ckv_norm_g=mla_ckv_norm_g, w_uq=w_uq, w_ukv=w_ukv, mla_qn_norm_g=mla_qn_norm_g, mla_qr_norm_g=mla_qr_norm_g, mla_kn_norm_g=mla_kn_norm_g, mla_kr_norm_g=mla_kr_norm_g, mem_norm_g=mem_norm_g, w_mem_kv=w_mem_kv, mem_q_norm_g=mem_q_norm_g, mem_k_norm_g=mem_k_norm_g, w_out=w_out, ffn_norm_g=ffn_norm_g, w_gate=w_gate, w_up=w_up, w_down=w_down, loss_target=loss_target, m_attn_norm_g=m_attn_norm_g, m_w_in=m_w_in, m_swa_q_norm_g=m_swa_q_norm_g, m_swa_k_norm_g=m_swa_k_norm_g, m_swa_sinks=m_swa_sinks, m_mla_cq_norm_g=m_mla_cq_norm_g, m_mla_ckv_norm_g=m_mla_ckv_norm_g, m_w_uq=m_w_uq, m_w_ukv=m_w_ukv, m_mla_qn_norm_g=m_mla_qn_norm_g, m_mla_qr_norm_g=m_mla_qr_norm_g, m_mla_kn_norm_g=m_mla_kn_norm_g, m_mla_kr_norm_g=m_mla_kr_norm_g, m_mem_norm_g=m_mem_norm_g, m_w_mem_kv=m_w_mem_kv, m_mem_q_norm_g=m_mem_q_norm_g, m_mem_k_norm_g=m_mem_k_norm_g, m_w_out=m_w_out, m_ffn_norm_g=m_ffn_norm_g, m_w_gate=m_w_gate, m_w_up=m_w_up, m_w_down=m_w_down, v_attn_norm_g=v_attn_norm_g, v_w_in=v_w_in, v_swa_q_norm_g=v_swa_q_norm_g, v_swa_k_norm_g=v_swa_k_norm_g, v_swa_sinks=v_swa_sinks, v_mla_cq_norm_g=v_mla_cq_norm_g, v_mla_ckv_norm_g=v_mla_ckv_norm_g, v_w_uq=v_w_uq, v_w_ukv=v_w_ukv, v_mla_qn_norm_g=v_mla_qn_norm_g, v_mla_qr_norm_g=v_mla_qr_norm_g, v_mla_kn_norm_g=v_mla_kn_norm_g, v_mla_kr_norm_g=v_mla_kr_norm_g, v_mem_norm_g=v_mem_norm_g, v_w_mem_kv=v_w_mem_kv, v_mem_q_norm_g=v_mem_q_norm_g, v_mem_k_norm_g=v_mem_k_norm_g, v_w_out=v_w_out, v_ffn_norm_g=v_ffn_norm_g, v_w_gate=v_w_gate, v_w_up=v_w_up, v_w_down=v_w_down)
    weights = {n: given[n] for n in TWIN_WEIGHTS}
    shared = {n: given[n] for n in SHARED_INPUTS}
    per_example = {n: given[n] for n in ['x', 'mem', 'positions']}
    grad_fn = _jax.value_and_grad(_loss, argnums=(0, 1))

    def one_microbatch(ex, loss_target):
        ex = dict(ex)
        diff = ex.pop(TWIN_DIFF_INPUT)
        return grad_fn(weights, diff, {**shared, **ex}, loss_target)

    if N_MICROBATCH == 1:
        loss, (grad_w, grad_x) = one_microbatch(per_example, given["loss_target"])
    else:
        def body(carry, xs):
            loss_sum, grad_sum = carry
            l_k, (gw_k, gx_k) = one_microbatch(xs[0], xs[1])
            with _jax.named_scope("update"):
                return (loss_sum + l_k, _jax.tree.map(_jnp.add, grad_sum, gw_k)), gx_k

        init = (_jnp.zeros((), _jnp.float32), _jax.tree.map(_jnp.zeros_like, weights))
        (loss, grad_w), grad_x = _jax.lax.scan(body, init, (per_example, given["loss_target"]))
    with _jax.named_scope("update"):
        delta_w, new_m, new_v = {}, {}, {}
        for n in TWIN_WEIGHTS:
            delta_w[n], new_m[n], new_v[n] = _adamw(weights[n], grad_w[n], given["m_" + n], given["v_" + n])
    return (loss, grad_x, *[grad_w[n] for n in TWIN_WEIGHTS], *[delta_w[n] for n in TWIN_WEIGHTS],
            *[new_m[n] for n in TWIN_WEIGHTS], *[new_v[n] for n in TWIN_WEIGHTS])
```

```python
import functools
import math

import jax
import jax.numpy as jnp
from jax import lax
from jax.experimental import pallas as pl
from jax.experimental.pallas import tpu as pltpu

F32 = jnp.float32
BF16 = jnp.bfloat16

D_MODEL = 2048
BLOCK = 128
EPS = 1e-6
NEG_INF = -1e30
SWA_Q_HEADS = 16
SWA_KV_HEADS = 2
SWA_HEAD_DIM = 64
MLA_HEADS = 4
MLA_RANK = 512
MLA_NOPE = 128
MLA_ROPE = 64
MLA_V = 128
ROPE_THETA = 10000.0
MEM_HEADS = 4
MEM_DIM = 128
D_FF = 5632
IN_WIDTH = 2880
IN_PAD = 2944
N_CHIPS = 4

ADAM_LR = 0.001
ADAM_B1 = 0.9
ADAM_B2 = 0.999
ADAM_EPS = 1e-08
ADAM_WD = 0.01
ADAM_STEP = 10

VMEM_LIMIT_BYTES = 56 * 1024 * 1024
LANES = 128

MESH = pl.DeviceIdType.MESH


def _params(sem=None, **kw):
    return pltpu.CompilerParams(dimension_semantics=sem, vmem_limit_bytes=VMEM_LIMIT_BYTES, **kw)


def _tile(n, want):
    if n <= want:
        return n
    t = want - want % LANES
    while t > 0:
        if n % t == 0:
            return t
        t -= LANES
    return n


def _matmul(a, b, *, name, ta=False, tb=False, add=None, out_dtype=F32, tm=1024, tn=1024, tk=2048,
            b_split=False, out_split=0):
    if ta:
        kdim, m = a.shape
    else:
        m, kdim = a.shape
    if b_split:
        assert tb
        nsp, n, kb = b.shape
        kb = kb * nsp
    elif tb:
        n, kb = b.shape
    else:
        kb, n = b.shape
    assert kb == kdim, (a.shape, b.shape, ta, tb)
    if b_split:
        tk = kdim // nsp
    if out_split:
        tn = _tile(n // out_split, tn)
    tm, tn, tk = _tile(m, tm), _tile(n, tn), _tile(kdim, tk)
    nk = kdim // tk
    dims = (((0 if ta else 1,), (1 if tb else 0,)), ((), ()))

    def body(*refs):
        if add is None:
            a_ref, b_ref, o_ref, acc_ref = refs
            add_ref = None
        else:
            a_ref, b_ref, add_ref, o_ref, acc_ref = refs
        k = pl.program_id(2)
        part = lax.dot_general(a_ref[...].astype(BF16), b_ref[...].astype(BF16), dims,
                               preferred_element_type=F32)

        @pl.when(k == 0)
        def _():
            acc_ref[...] = part

        @pl.when(k > 0)
        def _():
            acc_ref[...] += part

        @pl.when(k == nk - 1)
        def _():
            r = acc_ref[...]
            if add_ref is not None:
                r = r + add_ref[...].astype(F32)
            o_ref[...] = r.astype(o_ref.dtype)

    a_spec = pl.BlockSpec((tk, tm), lambda i, j, k: (k, i)) if ta else pl.BlockSpec((tm, tk), lambda i, j, k: (i, k))
    if b_split:
        b_spec = pl.BlockSpec((None, tn, tk), lambda i, j, k: (k, j, 0))
    elif tb:
        b_spec = pl.BlockSpec((tn, tk), lambda i, j, k: (j, k))
    else:
        b_spec = pl.BlockSpec((tk, tn), lambda i, j, k: (k, j))
    in_specs = [a_spec, b_spec]
    args = [a, b]
    if add is not None:
        in_specs.append(pl.BlockSpec((tm, tn), lambda i, j, k: (i, j)))
        args.append(add)
    if out_split:
        per = (n // out_split) // tn
        out_spec = pl.BlockSpec((None, tm, tn), lambda i, j, k: (j // per, i, j % per))
        out_shape = jax.ShapeDtypeStruct((out_split, m, n // out_split), out_dtype)
    else:
        out_spec = pl.BlockSpec((tm, tn), lambda i, j, k: (i, j))
        out_shape = jax.ShapeDtypeStruct((m, n), out_dtype)
    return pl.pallas_call(
        body,
        name=name,
        grid=(m // tm, n // tn, nk),
        in_specs=in_specs,
        out_specs=out_spec,
        out_shape=out_shape,
        scratch_shapes=[pltpu.VMEM((tm, tn), F32)],
        compiler_params=_params(("parallel", "parallel", "arbitrary")),
    )(*args)


def _rms_fwd(x, g, *, name, tm=512):
    s, d = x.shape
    tm = _tile(s, tm)

    def body(x_ref, g_ref, o_ref):
        xv = x_ref[...]
        r = lax.rsqrt(jnp.mean(xv * xv, axis=-1, keepdims=True) + EPS)
        o_ref[...] = (xv * r * g_ref[...]).astype(o_ref.dtype)

    return pl.pallas_call(
        body, name=name, grid=(s // tm,),
        in_specs=[pl.BlockSpec((tm, d), lambda i: (i, 0)), pl.BlockSpec((1, d), lambda i: (0, 0))],
        out_specs=pl.BlockSpec((tm, d), lambda i: (i, 0)),
        out_shape=jax.ShapeDtypeStruct((s, d), BF16),
        compiler_params=_params(("parallel",)),
    )(x, g)


def _rms_bwd(dy, x, g, res, *, name, tm=512):
    s, d = x.shape
    tm = _tile(s, tm)

    def body(dy_ref, x_ref, g_ref, res_ref, dx_ref, dg_ref):
        xv = x_ref[...]
        dyv = dy_ref[...]
        r = lax.rsqrt(jnp.mean(xv * xv, axis=-1, keepdims=True) + EPS)
        xhat = xv * r
        dyg = dyv * g_ref[...]
        mt = jnp.mean(dyg * xhat, axis=-1, keepdims=True)
        dx_ref[...] = res_ref[...] + r * (dyg - xhat * mt)
        part = jnp.sum(dyv * xhat, axis=0, keepdims=True)

        @pl.when(pl.program_id(0) == 0)
        def _():
            dg_ref[...] = part

        @pl.when(pl.program_id(0) > 0)
        def _():
            dg_ref[...] += part

    row = pl.BlockSpec((tm, d), lambda i: (i, 0))
    vec = pl.BlockSpec((1, d), lambda i: (0, 0))
    return pl.pallas_call(
        body, name=name, grid=(s // tm,),
        in_specs=[row, row, vec, row],
        out_specs=[row, vec],
        out_shape=[jax.ShapeDtypeStruct((s, d), F32), jax.ShapeDtypeStruct((1, d), F32)],
        compiler_params=_params(("arbitrary",)),
    )(dy, x, g, res)


def _lane(shape):
    return lax.broadcasted_iota(jnp.int32, shape, 1)


def _halfsum(t, lo):
    s_lo = jnp.sum(jnp.where(lo, t, 0.0), axis=-1, keepdims=True)
    s_hi = jnp.sum(jnp.where(lo, 0.0, t), axis=-1, keepdims=True)
    return jnp.where(lo, s_lo, s_hi)


def _norm_pair(x, g, lo):
    r = lax.rsqrt(_halfsum(x * x, lo) * (1.0 / 64.0) + EPS)
    xhat = x * r
    return xhat * g, xhat, r


def _norm_pair_bwd(dy, g, xhat, r, lo):
    dyg = dy * g
    mt = _halfsum(dyg * xhat, lo) * (1.0 / 64.0)
    return r * (dyg - xhat * mt), jnp.sum(dy * xhat, axis=0, keepdims=True)


def _norm_full(x, g):
    r = lax.rsqrt(jnp.mean(x * x, axis=-1, keepdims=True) + EPS)
    xhat = x * r
    return xhat * g, xhat, r


def _norm_full_bwd(dy, g, xhat, r):
    dyg = dy * g
    mt = jnp.mean(dyg * xhat, axis=-1, keepdims=True)
    return r * (dyg - xhat * mt), jnp.sum(dy * xhat, axis=0, keepdims=True)


def _rot(x, first32):
    return jnp.where(first32, pltpu.roll(x, 96, axis=1), pltpu.roll(x, 32, axis=1))


def _rope(x, cos_t, sin_t, first32):
    return x * cos_t + _rot(x, first32) * sin_t


def _rope_bwd(dy, cos_t, sin_t, first32):
    return dy * cos_t + _rot(dy * sin_t, first32)


G_SWA_Q, G_SWA_K, G_QN, G_QR, G_KN, G_KR, G_MQ = range(7)

C_QA, C_KA, C_VA, C_CQ, C_CKV, C_QM, C_KR = 0, 1024, 1152, 1280, 1792, 2304, 2816


def _prep_common(p_ref, g128_ref, gcq_ref, gckv_ref, wuq_ref, wukv_ref, cos_ref, sin_ref):
    tm = p_ref.shape[0]
    lane = _lane((tm, LANES))
    lo = lane < 64
    first32 = (lane % 64) < 32
    cos_t = cos_ref[...]
    sin_t = sin_ref[...]
    g = lambda row: g128_ref[row:row + 1, :]
    out = dict(lo=lo, first32=first32, cos_t=cos_t, sin_t=sin_t, lane=lane)
    cq_n, cq_hat, cq_r = _norm_full(p_ref[:, C_CQ:C_CQ + MLA_RANK], gcq_ref[...])
    ckv_n, ckv_hat, ckv_r = _norm_full(p_ref[:, C_CKV:C_CKV + MLA_RANK], gckv_ref[...])
    cq_b = cq_n.astype(BF16)
    ckv_b = ckv_n.astype(BF16)
    q_b = jnp.dot(cq_b, wuq_ref[...], preferred_element_type=F32)
    kv_b = jnp.dot(ckv_b, wukv_ref[...], preferred_element_type=F32)
    out.update(cq_b=cq_b, cq_hat=cq_hat, cq_r=cq_r, ckv_b=ckv_b, ckv_hat=ckv_hat, ckv_r=ckv_r, q_b=q_b, kv_b=kv_b, g=g)
    return out


def _attn_prep_fwd(proj, g128, gcq, gckv, wuq, wukv, cos_t, sin_t, *, tm=512):
    s = proj.shape[0]
    tm = _tile(s, tm)

    def body(p_ref, g128_ref, gcq_ref, gckv_ref, wuq_ref, wukv_ref, cos_ref, sin_ref,
             qa_ref, ka_ref, va_ref, qcat_ref, kcat_ref, vb_ref, qm_ref):
        c = _prep_common(p_ref, g128_ref, gcq_ref, gckv_ref, wuq_ref, wukv_ref, cos_ref, sin_ref)
        lo, first32, g = c["lo"], c["first32"], c["g"]
        for j in range(SWA_Q_HEADS // 2):
            y, _, _ = _norm_pair(p_ref[:, C_QA + 128 * j:C_QA + 128 * (j + 1)], g(G_SWA_Q), lo)
            qa_ref[:, 128 * j:128 * (j + 1)] = y.astype(BF16)
        y, _, _ = _norm_pair(p_ref[:, C_KA:C_KA + 128], g(G_SWA_K), lo)
        ka_ref[...] = y.astype(BF16)
        va_ref[...] = p_ref[:, C_VA:C_VA + 128].astype(BF16)
        kr, _, _ = _norm_pair(p_ref[:, C_KR:C_KR + 128], g(G_KR), lo)
        kr = jnp.where(lo, _rope(kr, c["cos_t"], c["sin_t"], first32), 0.0)
        krkr = (kr + pltpu.roll(kr, 64, axis=1)).astype(BF16)
        q_b, kv_b = c["q_b"], c["kv_b"]
        qr = []
        for j in range(MLA_HEADS // 2):
            y, _, _ = _norm_pair(q_b[:, 512 + 128 * j:512 + 128 * (j + 1)], g(G_QR), lo)
            qr.append(_rope(y, c["cos_t"], c["sin_t"], first32))
        for h in range(MLA_HEADS):
            qn, _, _ = _norm_full(q_b[:, 128 * h:128 * (h + 1)], g(G_QN))
            keep = lo if h % 2 == 0 else jnp.logical_not(lo)
            qcat_ref[h, :, 0:128] = qn.astype(BF16)
            qcat_ref[h, :, 128:256] = jnp.where(keep, qr[h // 2], 0.0).astype(BF16)
            kn, _, _ = _norm_full(kv_b[:, 128 * h:128 * (h + 1)], g(G_KN))
            kcat_ref[h, :, 0:128] = kn.astype(BF16)
            kcat_ref[h, :, 128:256] = krkr
        vb_ref[...] = kv_b[:, 512:1024].astype(BF16)
        for h in range(MEM_HEADS):
            y, _, _ = _norm_full(p_ref[:, C_QM + 128 * h:C_QM + 128 * (h + 1)], g(G_MQ))
            qm_ref[:, 128 * h:128 * (h + 1)] = y.astype(BF16)

    row = lambda w: pl.BlockSpec((tm, w), lambda i: (i, 0))
    full = lambda shape: pl.BlockSpec(shape, lambda i: tuple(0 for _ in shape))
    cat = pl.BlockSpec((MLA_HEADS, tm, 256), lambda i: (0, i, 0))
    return pl.pallas_call(
        body, name="attn_prep_fwd", grid=(s // tm,),
        in_specs=[row(IN_PAD), full((8, 128)), full((1, 512)), full((1, 512)), full((512, 768)), full((512, 1024)),
                  row(128), row(128)],
        out_specs=[row(1024), row(128), row(128), cat, cat, row(512), row(512)],
        out_shape=[jax.ShapeDtypeStruct((s, 1024), BF16), jax.ShapeDtypeStruct((s, 128), BF16),
                   jax.ShapeDtypeStruct((s, 128), BF16), jax.ShapeDtypeStruct((MLA_HEADS, s, 256), BF16),
                   jax.ShapeDtypeStruct((MLA_HEADS, s, 256), BF16), jax.ShapeDtypeStruct((s, 512), BF16),
                   jax.ShapeDtypeStruct((s, 512), BF16)],
        compiler_params=_params(("parallel",)),
    )(proj, g128, gcq, gckv, wuq, wukv, cos_t, sin_t)


def _attn_prep_bwd(proj, g128, gcq, gckv, wuq, wukv, cos_t, sin_t,
                   d_qa, d_ka, d_va, d_qcat, d_kcat, d_vb, d_qm, *, tm=256):
    s = proj.shape[0]
    tm = _tile(s, tm)

    def body(p_ref, g128_ref, gcq_ref, gckv_ref, wuq_ref, wukv_ref, cos_ref, sin_ref,
             dqa_ref, dka_ref, dva_ref, dqcat_ref, dkcat_ref, dvb_ref, dqm_ref,
             dp_ref, dwuq_ref, dwukv_ref, dg128_ref, dgcq_ref, dgckv_ref):
        c = _prep_common(p_ref, g128_ref, gcq_ref, gckv_ref, wuq_ref, wukv_ref, cos_ref, sin_ref)
        lo, first32, g = c["lo"], c["first32"], c["g"]
        cos_v, sin_v = c["cos_t"], c["sin_t"]
        q_b, kv_b = c["q_b"], c["kv_b"]
        zero_row = jnp.zeros((1, LANES), F32)
        dg = {k: zero_row for k in range(7)}

        for j in range(SWA_Q_HEADS // 2):
            sl = slice(C_QA + 128 * j, C_QA + 128 * (j + 1))
            _, xhat, r = _norm_pair(p_ref[:, sl], g(G_SWA_Q), lo)
            dx, dgj = _norm_pair_bwd(dqa_ref[:, 128 * j:128 * (j + 1)], g(G_SWA_Q), xhat, r, lo)
            dp_ref[:, sl] = dx
            dg[G_SWA_Q] = dg[G_SWA_Q] + dgj
        _, xhat, r = _norm_pair(p_ref[:, C_KA:C_KA + 128], g(G_SWA_K), lo)
        dx, dgj = _norm_pair_bwd(dka_ref[...], g(G_SWA_K), xhat, r, lo)
        dp_ref[:, C_KA:C_KA + 128] = dx
        dg[G_SWA_K] = dgj
        dp_ref[:, C_VA:C_VA + 128] = dva_ref[...]

        dqb_parts = [None] * 6
        for h in range(MLA_HEADS):
            _, xhat, r = _norm_full(q_b[:, 128 * h:128 * (h + 1)], g(G_QN))
            dx, dgj = _norm_full_bwd(dqcat_ref[h, :, 0:128], g(G_QN), xhat, r)
            dqb_parts[h] = dx
            dg[G_QN] = dg[G_QN] + dgj
        for j in range(MLA_HEADS // 2):
            _, xhat, r = _norm_pair(q_b[:, 512 + 128 * j:512 + 128 * (j + 1)], g(G_QR), lo)
            d_rot = jnp.where(lo, dqcat_ref[2 * j, :, 128:256], dqcat_ref[2 * j + 1, :, 128:256])
            d_y = _rope_bwd(d_rot, cos_v, sin_v, first32)
            dx, dgj = _norm_pair_bwd(d_y, g(G_QR), xhat, r, lo)
            dqb_parts[4 + j] = dx
            dg[G_QR] = dg[G_QR] + dgj
        d_qb = jnp.concatenate(dqb_parts, axis=1).astype(BF16)
        dwuq = lax.dot_general(c["cq_b"], d_qb, (((0,), (0,)), ((), ())), preferred_element_type=F32)
        d_cqn = lax.dot_general(d_qb, wuq_ref[...], (((1,), (1,)), ((), ())), preferred_element_type=F32)
        dx, dgcq = _norm_full_bwd(d_cqn, gcq_ref[...], c["cq_hat"], c["cq_r"])
        dp_ref[:, C_CQ:C_CQ + MLA_RANK] = dx

        dkv_parts = []
        d_krkr = jnp.zeros((p_ref.shape[0], LANES), F32)
        for h in range(MLA_HEADS):
            _, xhat, r = _norm_full(kv_b[:, 128 * h:128 * (h + 1)], g(G_KN))
            dx, dgj = _norm_full_bwd(dkcat_ref[h, :, 0:128], g(G_KN), xhat, r)
            dkv_parts.append(dx)
            dg[G_KN] = dg[G_KN] + dgj
            d_krkr = d_krkr + dkcat_ref[h, :, 128:256]
        d_kvb = jnp.concatenate(dkv_parts + [dvb_ref[...]], axis=1).astype(BF16)
        dwukv = lax.dot_general(c["ckv_b"], d_kvb, (((0,), (0,)), ((), ())), preferred_element_type=F32)
        d_ckvn = lax.dot_general(d_kvb, wukv_ref[...], (((1,), (1,)), ((), ())), preferred_element_type=F32)
        dx, dgckv = _norm_full_bwd(d_ckvn, gckv_ref[...], c["ckv_hat"], c["ckv_r"])
        dp_ref[:, C_CKV:C_CKV + MLA_RANK] = dx

        _, xhat, r = _norm_pair(p_ref[:, C_KR:C_KR + 128], g(G_KR), lo)
        d_kr = jnp.where(lo, d_krkr + pltpu.roll(d_krkr, 64, axis=1), 0.0)
        d_y = jnp.where(lo, _rope_bwd(d_kr, cos_v, sin_v, first32), 0.0)
        dx, dgj = _norm_pair_bwd(d_y, g(G_KR), xhat, r, lo)
        dp_ref[:, C_KR:C_KR + 128] = jnp.where(lo, dx, 0.0)
        dg[G_KR] = dgj

        for h in range(MEM_HEADS):
            sl = slice(C_QM + 128 * h, C_QM + 128 * (h + 1))
            _, xhat, r = _norm_full(p_ref[:, sl], g(G_MQ))
            dx, dgj = _norm_full_bwd(dqm_ref[:, 128 * h:128 * (h + 1)], g(G_MQ), xhat, r)
            dp_ref[:, sl] = dx
            dg[G_MQ] = dg[G_MQ] + dgj

        dg_tile = jnp.concatenate([dg[k] for k in range(7)] + [zero_row], axis=0)

        @pl.when(pl.program_id(0) == 0)
        def _():
            dwuq_ref[...] = dwuq
            dwukv_ref[...] = dwukv
            dg128_ref[...] = dg_tile
            dgcq_ref[...] = dgcq
            dgckv_ref[...] = dgckv

        @pl.when(pl.program_id(0) > 0)
        def _():
            dwuq_ref[...] += dwuq
            dwukv_ref[...] += dwukv
            dg128_ref[...] += dg_tile
            dgcq_ref[...] += dgcq
            dgckv_ref[...] += dgckv

    row = lambda w: pl.BlockSpec((tm, w), lambda i: (i, 0))
    full = lambda shape: pl.BlockSpec(shape, lambda i: tuple(0 for _ in shape))
    cat = pl.BlockSpec((MLA_HEADS, tm, 256), lambda i: (0, i, 0))
    return pl.pallas_call(
        body, name="attn_prep_bwd", grid=(s // tm,),
        in_specs=[row(IN_PAD), full((8, 128)), full((1, 512)), full((1, 512)), full((512, 768)), full((512, 1024)),
                  row(128), row(128),
                  row(1024), row(128), row(128), cat, cat, row(512), row(512)],
        out_specs=[row(IN_PAD), full((512, 768)), full((512, 1024)), full((8, 128)), full((1, 512)), full((1, 512))],
        out_shape=[jax.ShapeDtypeStruct((s, IN_PAD), F32), jax.ShapeDtypeStruct((512, 768), F32),
                   jax.ShapeDtypeStruct((512, 1024), F32), jax.ShapeDtypeStruct((8, 128), F32),
                   jax.ShapeDtypeStruct((1, 512), F32), jax.ShapeDtypeStruct((1, 512), F32)],
        compiler_params=_params(("arbitrary",)),
    )(proj, g128, gcq, gckv, wuq, wukv, cos_t, sin_t, d_qa, d_ka, d_va, d_qcat, d_kcat, d_vb, d_qm)


SWA_SLOPES = tuple(2.0 ** (-8.0 * h / SWA_Q_HEADS) for h in range(1, SWA_Q_HEADS + 1))
SWA_SCALE = SWA_HEAD_DIM ** -0.5
NT_DIMS = (((1,), (1,)), ((), ()))
TN_DIMS = (((0,), (0,)), ((), ()))


def _swa_span(n, kp_ref, kc_ref, vp_ref, vc_ref, pcol_ref, pprow_ref, pcrow_ref):
    k_span = jnp.concatenate([kp_ref[...], kc_ref[...]], axis=0).astype(F32)
    v_span = jnp.concatenate([vp_ref[...], vc_ref[...]], axis=0).astype(F32)
    lo = _lane((2 * BLOCK, LANES)) < 64
    k_sw = pltpu.roll(k_span, 64, axis=1)
    v_sw = pltpu.roll(v_span, 64, axis=1)
    kk = (jnp.where(lo, k_span, k_sw).astype(BF16), jnp.where(lo, k_sw, k_span).astype(BF16))
    vv_lo = (jnp.where(lo, v_span, 0.0).astype(BF16), jnp.where(lo, v_sw, 0.0).astype(BF16))
    vv_hi = (jnp.where(lo, 0.0, v_sw).astype(BF16), jnp.where(lo, 0.0, v_span).astype(BF16))
    pk = jnp.concatenate([pprow_ref[...], pcrow_ref[...]], axis=1)
    dist = jnp.abs(pcol_ref[...] - pk)
    qi = lax.broadcasted_iota(jnp.int32, (BLOCK, 2 * BLOCK), 0)
    ki = lax.broadcasted_iota(jnp.int32, (BLOCK, 2 * BLOCK), 1)
    first_key = jnp.where(n > 0, qi + 1, jnp.maximum(qi + 1, BLOCK))
    valid = jnp.logical_and(ki >= first_key, ki <= qi + BLOCK)
    mask_add = jnp.where(valid, 0.0, NEG_INF)
    return kk, vv_lo, vv_hi, dist, mask_add


def _swa_probs(q_half, kk, dist, mask_add, slope, sink):
    s = lax.dot_general(q_half, kk, NT_DIMS, preferred_element_type=F32) * SWA_SCALE - slope * dist + mask_add
    m = jnp.maximum(jnp.max(s, axis=-1, keepdims=True), sink)
    e = jnp.exp(s - m)
    e_sink = jnp.exp(sink - m)
    inv = 1.0 / (jnp.sum(e, axis=-1, keepdims=True) + e_sink)
    return e * inv, e_sink * inv


def _swa_specs():
    blk = lambda w: pl.BlockSpec((BLOCK, w), lambda n: (n, 0))
    prev = lambda w: pl.BlockSpec((BLOCK, w), lambda n: (jnp.maximum(n - 1, 0), 0))
    prow_c = pl.BlockSpec((1, BLOCK), lambda n: (0, n))
    prow_p = pl.BlockSpec((1, BLOCK), lambda n: (0, jnp.maximum(n - 1, 0)))
    smem = pl.BlockSpec(memory_space=pltpu.SMEM)
    return [blk(1024), prev(128), blk(128), prev(128), blk(128), blk(1), prow_p, prow_c, smem], blk


def _swa_fwd(qa, ka, va, pos_col, pos_row, sinks):
    s = qa.shape[0]
    in_specs, blk = _swa_specs()

    def body(q_ref, kp_ref, kc_ref, vp_ref, vc_ref, pcol_ref, pprow_ref, pcrow_ref, sink_ref, o_ref):
        n = pl.program_id(0)
        kk, vv_lo, vv_hi, dist, mask_add = _swa_span(n, kp_ref, kc_ref, vp_ref, vc_ref, pcol_ref, pprow_ref, pcrow_ref)
        lo = _lane((BLOCK, LANES)) < 64
        for j in range(SWA_Q_HEADS // 2):
            kv = (2 * j) // (SWA_Q_HEADS // SWA_KV_HEADS)
            q_pair = q_ref[:, 128 * j:128 * (j + 1)].astype(F32)
            q_e = jnp.where(lo, q_pair, 0.0).astype(BF16)
            q_o = jnp.where(lo, 0.0, q_pair).astype(BF16)
            p_e, _ = _swa_probs(q_e, kk[kv], dist, mask_add, SWA_SLOPES[2 * j], sink_ref[2 * j])
            p_o, _ = _swa_probs(q_o, kk[kv], dist, mask_add, SWA_SLOPES[2 * j + 1], sink_ref[2 * j + 1])
            o_ref[:, 128 * j:128 * (j + 1)] = (
                jnp.dot(p_e.astype(BF16), vv_lo[kv], preferred_element_type=F32)
                + jnp.dot(p_o.astype(BF16), vv_hi[kv], preferred_element_type=F32))

    return pl.pallas_call(
        body, name="swa_fwd", grid=(s // BLOCK,),
        in_specs=in_specs, out_specs=blk(1024),
        out_shape=jax.ShapeDtypeStruct((s, 1024), F32),
        compiler_params=_params(("parallel",)),
    )(qa, ka, ka, va, va, pos_col, pos_row, pos_row, sinks)


def _swa_bwd(qa, ka, va, pos_col, pos_row, sinks, y_a, d_y):
    s = qa.shape[0]
    in_specs, blk = _swa_specs()
    whole = pl.BlockSpec((s, 128), lambda n: (0, 0))

    def body(q_ref, kp_ref, kc_ref, vp_ref, vc_ref, pcol_ref, pprow_ref, pcrow_ref, sink_ref, y_ref, dy_ref,
             dq_ref, dk_ref, dv_ref, dsink_ref):
        n = pl.program_id(0)

        @pl.when(n == 0)
        def _():
            dk_ref[...] = jnp.zeros_like(dk_ref)
            dv_ref[...] = jnp.zeros_like(dv_ref)
            dsink_ref[...] = jnp.zeros_like(dsink_ref)

        kk, vv_lo, vv_hi, dist, mask_add = _swa_span(n, kp_ref, kc_ref, vp_ref, vc_ref, pcol_ref, pprow_ref, pcrow_ref)
        lo = _lane((BLOCK, LANES)) < 64
        lo2 = _lane((2 * BLOCK, LANES)) < 64
        lane1 = _lane((1, LANES))
        dsink = jnp.zeros((1, LANES), F32)
        dkk = [jnp.zeros((2 * BLOCK, LANES), F32) for _ in range(SWA_KV_HEADS)]
        dvv = [jnp.zeros((2 * BLOCK, LANES), F32) for _ in range(SWA_KV_HEADS)]
        for j in range(SWA_Q_HEADS // 2):
            kv = (2 * j) // (SWA_Q_HEADS // SWA_KV_HEADS)
            q_pair = q_ref[:, 128 * j:128 * (j + 1)].astype(F32)
            do_pair = dy_ref[:, 128 * j:128 * (j + 1)]
            do_b = do_pair.astype(BF16)
            doy = do_pair * y_ref[:, 128 * j:128 * (j + 1)]
            deltas = (jnp.sum(jnp.where(lo, doy, 0.0), axis=-1, keepdims=True),
                      jnp.sum(jnp.where(lo, 0.0, doy), axis=-1, keepdims=True))
            dq_halves = []
            for par, vv in ((0, vv_lo), (1, vv_hi)):
                h = 2 * j + par
                keep = lo if par == 0 else jnp.logical_not(lo)
                q_h = jnp.where(keep, q_pair, 0.0).astype(BF16)
                p, p_sink = _swa_probs(q_h, kk[kv], dist, mask_add, SWA_SLOPES[h], sink_ref[h])
                delta_h = deltas[par]
                dp = lax.dot_general(do_b, vv[kv], NT_DIMS, preferred_element_type=F32)
                ds = p * (dp - delta_h)
                dsink = dsink + jnp.where(lane1 == h, -jnp.sum(p_sink * delta_h, axis=0, keepdims=True), 0.0)
                ds_b = (ds * SWA_SCALE).astype(BF16)
                dq_halves.append(jnp.dot(ds_b, kk[kv], preferred_element_type=F32))
                dkk[kv] = dkk[kv] + lax.dot_general(ds_b, q_h, TN_DIMS, preferred_element_type=F32)
                pv = lax.dot_general(p.astype(BF16), do_b, TN_DIMS, preferred_element_type=F32)
                keep2 = lo2 if par == 0 else jnp.logical_not(lo2)
                dvv[kv] = dvv[kv] + jnp.where(keep2, pv, 0.0)
            dq_ref[:, 128 * j:128 * (j + 1)] = jnp.where(lo, dq_halves[0], dq_halves[1])
        fold = lambda t: t + pltpu.roll(t, 64, axis=1)
        dk_span = jnp.where(lo2, fold(dkk[0]), fold(dkk[1]))
        dv_span = jnp.where(lo2, fold(dvv[0]), fold(dvv[1]))
        prev0 = pl.multiple_of(jnp.maximum(n - 1, 0) * BLOCK, BLOCK)
        cur0 = pl.multiple_of(n * BLOCK, BLOCK)
        dk_ref[pl.ds(prev0, BLOCK), :] += dk_span[0:BLOCK]
        dk_ref[pl.ds(cur0, BLOCK), :] += dk_span[BLOCK:]
        dv_ref[pl.ds(prev0, BLOCK), :] += dv_span[0:BLOCK]
        dv_ref[pl.ds(cur0, BLOCK), :] += dv_span[BLOCK:]
        dsink_ref[...] += dsink

    return pl.pallas_call(
        body, name="swa_bwd", grid=(s // BLOCK,),
        in_specs=in_specs + [blk(1024), blk(1024)],
        out_specs=[blk(1024), whole, whole, pl.BlockSpec((1, LANES), lambda n: (0, 0))],
        out_shape=[jax.ShapeDtypeStruct((s, 1024), F32), jax.ShapeDtypeStruct((s, 128), F32),
                   jax.ShapeDtypeStruct((s, 128), F32), jax.ShapeDtypeStruct((1, LANES), F32)],
        compiler_params=_params(("arbitrary",)),
    )(qa, ka, ka, va, va, pos_col, pos_row, pos_row, sinks, y_a, d_y)


MLA_SCALE = (MLA_NOPE + MLA_ROPE) ** -0.5
MLA_TILE = 512


def _causal_mask(i, j, t):
    row = i * t + lax.broadcasted_iota(jnp.int32, (t, t), 0)
    col = j * t + lax.broadcasted_iota(jnp.int32, (t, t), 1)
    return col <= row


def _mla_fwd(q_cat, k_cat, v_b):
    nh, s, _ = q_cat.shape
    t = _tile(s, MLA_TILE)
    nt = s // t

    def body(q_ref, k_ref, v_ref, o_ref, lse_ref, m_sc, l_sc, acc_sc):
        i, j = pl.program_id(1), pl.program_id(2)

        @pl.when(j == 0)
        def _():
            m_sc[...] = jnp.full_like(m_sc, NEG_INF)
            l_sc[...] = jnp.zeros_like(l_sc)
            acc_sc[...] = jnp.zeros_like(acc_sc)

        @pl.when(j <= i)
        def _():
            sc = lax.dot_general(q_ref[0], k_ref[0], NT_DIMS, preferred_element_type=F32) * MLA_SCALE
            sc = jnp.where(_causal_mask(i, j, t), sc, NEG_INF)
            m_new = jnp.maximum(m_sc[...], jnp.max(sc, axis=-1, keepdims=True))
            alpha = jnp.exp(m_sc[...] - m_new)
            p = jnp.exp(sc - m_new)
            l_sc[...] = alpha * l_sc[...] + jnp.sum(p, axis=-1, keepdims=True)
            acc_sc[...] = alpha * acc_sc[...] + jnp.dot(p.astype(BF16), v_ref[...], preferred_element_type=F32)
            m_sc[...] = m_new

        @pl.when(j == i)
        def _():
            o_ref[...] = acc_sc[...] * (1.0 / l_sc[...])
            lse_ref[0] = m_sc[...] + jnp.log(l_sc[...])

    return pl.pallas_call(
        body, name="mla_fwd", grid=(nh, nt, nt),
        in_specs=[pl.BlockSpec((1, t, 256), lambda h, i, j: (h, i, 0)),
                  pl.BlockSpec((1, t, 256), lambda h, i, j: (h, jnp.minimum(j, i), 0)),
                  pl.BlockSpec((t, MLA_V), lambda h, i, j: (jnp.minimum(j, i), h))],
        out_specs=[pl.BlockSpec((t, MLA_V), lambda h, i, j: (i, h)),
                   pl.BlockSpec((1, t, 1), lambda h, i, j: (h, i, 0))],
        out_shape=[jax.ShapeDtypeStruct((s, nh * MLA_V), F32), jax.ShapeDtypeStruct((nh, s, 1), F32)],
        scratch_shapes=[pltpu.VMEM((t, 1), F32), pltpu.VMEM((t, 1), F32), pltpu.VMEM((t, MLA_V), F32)],
        compiler_params=_params(("parallel", "parallel", "arbitrary")),
    )(q_cat, k_cat, v_b)


def _mla_bwd(q_cat, k_cat, v_b, y_b, lse, d_y):
    nh, s, _ = q_cat.shape
    t = _tile(s, MLA_TILE)
    nt = s // t
    col0 = (SWA_Q_HEADS * SWA_HEAD_DIM) // MLA_V

    def body(q_ref, k_ref, v_ref, y_ref, lse_ref, dy_ref, dq_ref, dk_ref, dv_ref, dk_sc, dv_sc):
        j, i = pl.program_id(1), pl.program_id(2)

        @pl.when((j == 0) & (i == 0))
        def _():
            dq_ref[...] = jnp.zeros_like(dq_ref)

        @pl.when(i == 0)
        def _():
            dk_sc[...] = jnp.zeros_like(dk_sc)
            dv_sc[...] = jnp.zeros_like(dv_sc)

        @pl.when(i >= j)
        def _():
            q = q_ref[0]
            k = k_ref[0]
            do = dy_ref[...]
            do_b = do.astype(BF16)
            sc = lax.dot_general(q, k, NT_DIMS, preferred_element_type=F32) * MLA_SCALE
            p = jnp.where(_causal_mask(i, j, t), jnp.exp(sc - lse_ref[0]), 0.0)
            delta = jnp.sum(do * y_ref[...], axis=-1, keepdims=True)
            dv_sc[...] += lax.dot_general(p.astype(BF16), do_b, TN_DIMS, preferred_element_type=F32)
            dp = lax.dot_general(do_b, v_ref[...], NT_DIMS, preferred_element_type=F32)
            ds_b = (p * (dp - delta) * MLA_SCALE).astype(BF16)
            dk_sc[...] += lax.dot_general(ds_b, q, TN_DIMS, preferred_element_type=F32)
            rows = pl.ds(pl.multiple_of(i * t, t), t)
            dq_ref[0, rows, :] += jnp.dot(ds_b, k, preferred_element_type=F32)

        @pl.when(i == nt - 1)
        def _():
            dk_ref[0] = dk_sc[...]
            dv_ref[...] = dv_sc[...]

    qrow = lambda h, j, i: (h, jnp.maximum(i, j), 0)
    return pl.pallas_call(
        body, name="mla_bwd", grid=(nh, nt, nt),
        in_specs=[pl.BlockSpec((1, t, 256), qrow),
                  pl.BlockSpec((1, t, 256), lambda h, j, i: (h, j, 0)),
                  pl.BlockSpec((t, MLA_V), lambda h, j, i: (j, h)),
                  pl.BlockSpec((t, MLA_V), lambda h, j, i: (jnp.maximum(i, j), h)),
                  pl.BlockSpec((1, t, 1), qrow),
                  pl.BlockSpec((t, MLA_V), lambda h, j, i: (jnp.maximum(i, j), col0 + h))],
        out_specs=[pl.BlockSpec((1, s, 256), lambda h, j, i: (h, 0, 0)),
                   pl.BlockSpec((1, t, 256), lambda h, j, i: (h, j, 0)),
                   pl.BlockSpec((t, MLA_V), lambda h, j, i: (j, h))],
        out_shape=[jax.ShapeDtypeStruct((nh, s, 256), F32), jax.ShapeDtypeStruct((nh, s, 256), F32),
                   jax.ShapeDtypeStruct((s, nh * MLA_V), F32)],
        scratch_shapes=[pltpu.VMEM((t, 256), F32), pltpu.VMEM((t, MLA_V), F32)],
        compiler_params=_params(("arbitrary", "arbitrary", "arbitrary")),
    )(q_cat, k_cat, v_b, y_b, lse, d_y)


MEM_SCALE = MEM_DIM ** -0.5


def _mem_kv_fwd(mem, g_mem, w_memkv, g_mk):
    m_len = mem.shape[0]

    def body(mem_ref, g_ref, w_ref, gk_ref, mn_ref, kv_ref, kn_ref, v_ref):
        mn, _, _ = _norm_full(mem_ref[...], g_ref[...])
        mn_b = mn.astype(BF16)
        mn_ref[...] = mn_b
        kv = jnp.dot(mn_b, w_ref[...], preferred_element_type=F32)
        kv_ref[...] = kv
        for h in range(MEM_HEADS):
            kn, _, _ = _norm_full(kv[:, 128 * h:128 * (h + 1)], gk_ref[...])
            kn_ref[:, 128 * h:128 * (h + 1)] = kn.astype(BF16)
        v_ref[...] = kv[:, 512:1024].astype(BF16)

    return pl.pallas_call(
        body, name="mem_kv_fwd",
        out_shape=[jax.ShapeDtypeStruct((m_len, D_MODEL), BF16), jax.ShapeDtypeStruct((m_len, 1024), F32),
                   jax.ShapeDtypeStruct((m_len, 512), BF16), jax.ShapeDtypeStruct((m_len, 512), BF16)],
        compiler_params=_params(),
    )(mem, g_mem, w_memkv, g_mk)


def _mem_kv_bwd(mem, g_mem, w_memkv, g_mk, mn_b, kv, d_kn, d_v):
    m_len = mem.shape[0]

    def body(mem_ref, g_ref, w_ref, gk_ref, mn_ref, kv_ref, dkn_ref, dv_ref, dw_ref, dgmem_ref, dgk_ref):
        parts = []
        dgk = jnp.zeros((1, LANES), F32)
        for h in range(MEM_HEADS):
            _, xhat, r = _norm_full(kv_ref[:, 128 * h:128 * (h + 1)], gk_ref[...])
            dx, dgh = _norm_full_bwd(dkn_ref[:, 128 * h:128 * (h + 1)], gk_ref[...], xhat, r)
            parts.append(dx)
            dgk = dgk + dgh
        d_kv = jnp.concatenate(parts + [dv_ref[...]], axis=1).astype(BF16)
        dw_ref[...] = lax.dot_general(mn_ref[...], d_kv, TN_DIMS, preferred_element_type=F32)
        d_mn = lax.dot_general(d_kv, w_ref[...], NT_DIMS, preferred_element_type=F32)
        _, xhat, _ = _norm_full(mem_ref[...], g_ref[...])
        dgmem_ref[...] = jnp.sum(d_mn * xhat, axis=0, keepdims=True)
        dgk_ref[...] = dgk

    return pl.pallas_call(
        body, name="mem_kv_bwd",
        out_shape=[jax.ShapeDtypeStruct((D_MODEL, 1024), F32), jax.ShapeDtypeStruct((1, D_MODEL), F32),
                   jax.ShapeDtypeStruct((1, LANES), F32)],
        compiler_params=_params(),
    )(mem, g_mem, w_memkv, g_mk, mn_b, kv, d_kn, d_v)


def _mem_probs(q_h, k_h):
    sc = lax.dot_general(q_h, k_h, NT_DIMS, preferred_element_type=F32) * MEM_SCALE
    e = jnp.exp(sc - jnp.max(sc, axis=-1, keepdims=True))
    return e * (1.0 / jnp.sum(e, axis=-1, keepdims=True))


def _mem_attn_fwd(qm, km, vm, *, tm=512):
    s = qm.shape[0]
    tm = _tile(s, tm)
    m_len = km.shape[0]

    def body(q_ref, k_ref, v_ref, o_ref):
        for h in range(MEM_HEADS):
            sl = slice(128 * h, 128 * (h + 1))
            p = _mem_probs(q_ref[:, sl], k_ref[:, sl])
            o_ref[:, sl] = jnp.dot(p.astype(BF16), v_ref[:, sl], preferred_element_type=F32)

    kvspec = pl.BlockSpec((m_len, 512), lambda i: (0, 0))
    return pl.pallas_call(
        body, name="mem_attn_fwd", grid=(s // tm,),
        in_specs=[pl.BlockSpec((tm, 512), lambda i: (i, 0)), kvspec, kvspec],
        out_specs=pl.BlockSpec((tm, 512), lambda i: (i, 0)),
        out_shape=jax.ShapeDtypeStruct((s, 512), F32),
        compiler_params=_params(("parallel",)),
    )(qm, km, vm)


def _mem_attn_bwd(qm, km, vm, y_m, d_y, *, tm=512):
    s = qm.shape[0]
    tm = _tile(s, tm)
    m_len = km.shape[0]
    col0 = (SWA_Q_HEADS * SWA_HEAD_DIM + MLA_HEADS * MLA_V) // 512

    def body(q_ref, k_ref, v_ref, y_ref, dy_ref, dq_ref, dk_ref, dv_ref):
        @pl.when(pl.program_id(0) == 0)
        def _():
            dk_ref[...] = jnp.zeros_like(dk_ref)
            dv_ref[...] = jnp.zeros_like(dv_ref)

        for h in range(MEM_HEADS):
            sl = slice(128 * h, 128 * (h + 1))
            q_h, k_h = q_ref[:, sl], k_ref[:, sl]
            do = dy_ref[:, sl]
            do_b = do.astype(BF16)
            p = _mem_probs(q_h, k_h)
            delta = jnp.sum(do * y_ref[:, sl], axis=-1, keepdims=True)
            dv_ref[:, sl] += lax.dot_general(p.astype(BF16), do_b, TN_DIMS, preferred_element_type=F32)
            dp = lax.dot_general(do_b, v_ref[:, sl], NT_DIMS, preferred_element_type=F32)
            ds_b = (p * (dp - delta) * MEM_SCALE).astype(BF16)
            dq_ref[:, sl] = jnp.dot(ds_b, k_h, preferred_element_type=F32)
            dk_ref[:, sl] += lax.dot_general(ds_b, q_h, TN_DIMS, preferred_element_type=F32)

    kvspec = pl.BlockSpec((m_len, 512), lambda i: (0, 0))
    row = pl.BlockSpec((tm, 512), lambda i: (i, 0))
    return pl.pallas_call(
        body, name="mem_attn_bwd", grid=(s // tm,),
        in_specs=[row, kvspec, kvspec, row, pl.BlockSpec((tm, 512), lambda i: (i, col0))],
        out_specs=[row, kvspec, kvspec],
        out_shape=[jax.ShapeDtypeStruct((s, 512), F32), jax.ShapeDtypeStruct((m_len, 512), F32),
                   jax.ShapeDtypeStruct((m_len, 512), F32)],
        compiler_params=_params(("arbitrary",)),
    )(qm, km, vm, y_m, d_y)


def _ffn_gate_up(fn, w_gate, w_up, *, tm=512):
    s, d = fn.shape
    nsp, _, tf = w_gate.shape
    f = nsp * tf
    tm = _tile(s, tm)

    def body(x_ref, wg_ref, wu_ref, g_ref, u_ref, a_ref):
        x = x_ref[...]
        gate = jnp.dot(x, wg_ref[...], preferred_element_type=F32)
        up = jnp.dot(x, wu_ref[...], preferred_element_type=F32)
        g_ref[...] = gate.astype(BF16)
        u_ref[...] = up.astype(BF16)
        a_ref[...] = (gate * (1.0 / (1.0 + jnp.exp(-gate))) * up).astype(BF16)

    wspec = pl.BlockSpec((None, d, tf), lambda j, i: (j, 0, 0))
    ospec = pl.BlockSpec((tm, tf), lambda j, i: (i, j))
    osh = jax.ShapeDtypeStruct((s, f), BF16)
    return pl.pallas_call(
        body, name="ffn_gate_up", grid=(nsp, s // tm),
        in_specs=[pl.BlockSpec((tm, d), lambda j, i: (i, 0)), wspec, wspec],
        out_specs=[ospec, ospec, ospec], out_shape=[osh, osh, osh],
        compiler_params=_params(("parallel", "parallel")),
    )(fn, w_gate, w_up)


def _ffn_bwd_act(d_out, w_down, gate, up, *, tm=512, tf=1408):
    s, d = d_out.shape
    f = w_down.shape[0]
    tm, tf = _tile(s, tm), _tile(f, tf)

    def body(do_ref, wd_ref, g_ref, u_ref, dg_ref, du_ref):
        d_act = lax.dot_general(do_ref[...].astype(BF16), wd_ref[...], NT_DIMS, preferred_element_type=F32)
        gate = g_ref[...].astype(F32)
        sig = 1.0 / (1.0 + jnp.exp(-gate))
        du_ref[...] = (d_act * (gate * sig)).astype(BF16)
        dg_ref[...] = (d_act * u_ref[...].astype(F32) * (sig * (1.0 + gate * (1.0 - sig)))).astype(BF16)

    ospec = pl.BlockSpec((tm, tf), lambda j, i: (i, j))
    osh = jax.ShapeDtypeStruct((s, f), BF16)
    return pl.pallas_call(
        body, name="ffn_bwd_act", grid=(f // tf, s // tm),
        in_specs=[pl.BlockSpec((tm, d), lambda j, i: (i, 0)), pl.BlockSpec((tf, d), lambda j, i: (j, 0)), ospec, ospec],
        out_specs=[ospec, ospec], out_shape=[osh, osh],
        compiler_params=_params(("parallel", "parallel")),
    )(d_out, w_down, gate, up)


def _loss_head(out, target, *, tm=512):
    s, d = out.shape
    tm = _tile(s, tm)

    def body(o_ref, t_ref, d_ref, l_ref):
        err = o_ref[...] - t_ref[...]
        d_ref[...] = err * (1.0 / d)
        part = 0.5 * jnp.sum(jnp.mean(err * err, axis=-1, keepdims=True), axis=0, keepdims=True)
        part = jnp.broadcast_to(part, (1, LANES))

        @pl.when(pl.program_id(0) == 0)
        def _():
            l_ref[...] = part

        @pl.when(pl.program_id(0) > 0)
        def _():
            l_ref[...] += part

    row = pl.BlockSpec((tm, d), lambda i: (i, 0))
    return pl.pallas_call(
        body, name="loss_head", grid=(s // tm,),
        in_specs=[row, row], out_specs=[row, pl.BlockSpec((1, LANES), lambda i: (0, 0))],
        out_shape=[jax.ShapeDtypeStruct((s, d), F32), jax.ShapeDtypeStruct((1, LANES), F32)],
        compiler_params=_params(("arbitrary",)),
    )(out, target)


def _cols(g4):
    return jnp.concatenate([g4[k] for k in range(N_CHIPS)], axis=1)


def _full_w_in(g4):
    nat = _cols(g4)
    pad = jnp.zeros((nat.shape[0], IN_PAD - IN_WIDTH), nat.dtype)
    return jnp.concatenate([nat[:, :2304], nat[:, 2368:], nat[:, 2304:2368], pad], axis=1)


def _shards_w_in(dwp):
    nat = jnp.concatenate([dwp[:, :2304], dwp[:, C_KR:C_KR + 64], dwp[:, 2304:C_KR]], axis=1)
    per = IN_WIDTH // N_CHIPS
    return jnp.stack([nat[:, per * k:per * (k + 1)] for k in range(N_CHIPS)])


def _full_heads(g4, first):
    return jnp.concatenate([g4[k][:, :first] for k in range(N_CHIPS)] + [g4[k][:, first:] for k in range(N_CHIPS)], axis=1)


def _shards_heads(dwp, first, rest):
    base = N_CHIPS * first
    return jnp.stack([jnp.concatenate([dwp[:, first * k:first * (k + 1)], dwp[:, base + rest * k:base + rest * (k + 1)]], axis=1)
                      for k in range(N_CHIPS)])


def _rope_tables(pos):
    inv_freq = ROPE_THETA ** (-jnp.arange(0, MLA_ROPE, 2, dtype=F32) / MLA_ROPE)
    ang = pos.astype(F32)[:, None] * inv_freq
    cos, sin = jnp.cos(ang), jnp.sin(ang)
    return jnp.tile(cos, (1, 4)), jnp.concatenate([-sin, sin, -sin, sin], axis=1)


def _gain_table(sp):
    two = lambda v: jnp.tile(v, (1, 2))
    rows = [two(sp["swa_q_norm_g"]), two(sp["swa_k_norm_g"]), sp["mla_qn_norm_g"], two(sp["mla_qr_norm_g"]),
            sp["mla_kn_norm_g"], two(sp["mla_kr_norm_g"]), sp["mem_q_norm_g"], jnp.zeros((1, LANES), F32)]
    return jnp.concatenate(rows, axis=0)


def _device_step(x, mem, pos, target, sp, w):
    s = x.shape[0]
    cos_t, sin_t = _rope_tables(pos)
    pos_f = pos.astype(F32)
    pos_col, pos_row = pos_f.reshape(s, 1), pos_f.reshape(1, s)
    g128 = _gain_table(sp)
    sinks = sp["swa_sinks"].reshape(SWA_Q_HEADS)

    hn = _rms_fwd(x, sp["attn_norm_g"], name="attn_norm_fwd")
    proj = _matmul(hn, w["w_in"], name="in_proj")
    qa, ka, va, q_cat, k_cat, v_b, qm = _attn_prep_fwd(
        proj, g128, sp["mla_cq_norm_g"], sp["mla_ckv_norm_g"], w["w_uq"], w["w_ukv"], cos_t, sin_t)
    mn_b, kv_m, km, vm = _mem_kv_fwd(mem, sp["mem_norm_g"], w["w_mem_kv"], sp["mem_k_norm_g"])
    y_a = _swa_fwd(qa, ka, va, pos_col, pos_row, sinks)
    y_b, lse = _mla_fwd(q_cat, k_cat, v_b)
    y_m = _mem_attn_fwd(qm, km, vm)
    y = jnp.concatenate([y_a, y_b, y_m], axis=1)
    h1 = _matmul(y, w["w_out"], add=x, name="out_proj")
    fn = _rms_fwd(h1, sp["ffn_norm_g"], name="ffn_norm_fwd")
    gate, up, act = _ffn_gate_up(fn, w["w_gate"], w["w_up"])
    out = _matmul(act, w["w_down"], add=h1, name="down_proj", tk=1408)
    d_out, loss_tile = _loss_head(out, target)

    gw, gs = {}, {}
    gw["w_down"] = _matmul(act, d_out, ta=True, name="dw_down", tm=1408, tn=1024, tk=1024)
    d_gate, d_up = _ffn_bwd_act(d_out, w["w_down"], gate, up)
    gw["w_gate"] = _matmul(fn, d_gate, ta=True, name="dw_gate", tk=1024, out_split=N_CHIPS)
    gw["w_up"] = _matmul(fn, d_up, ta=True, name="dw_up", tk=1024, out_split=N_CHIPS)
    d_fn = _matmul(d_gate, w["w_gate"], tb=True, b_split=True, name="dfn_gate")
    d_fn = _matmul(d_up, w["w_up"], tb=True, b_split=True, add=d_fn, name="dfn_up")
    d_h1, gs["ffn_norm_g"] = _rms_bwd(d_fn, h1, sp["ffn_norm_g"], d_out, name="ffn_norm_bwd")
    gw["w_out"] = _matmul(y, d_h1, ta=True, name="dw_out", tk=1024)
    d_y = _matmul(d_h1, w["w_out"], tb=True, name="dy")
    d_qa, d_ka, d_va, d_sink = _swa_bwd(qa, ka, va, pos_col, pos_row, sinks, y_a, d_y)
    d_qcat, d_kcat, d_vb = _mla_bwd(q_cat, k_cat, v_b, y_b, lse, d_y)
    d_qm, d_km, d_vm = _mem_attn_bwd(qm, km, vm, y_m, d_y)
    d_proj, gw["w_uq"], gw["w_ukv"], dg128, gs["mla_cq_norm_g"], gs["mla_ckv_norm_g"] = _attn_prep_bwd(
        proj, g128, sp["mla_cq_norm_g"], sp["mla_ckv_norm_g"], w["w_uq"], w["w_ukv"], cos_t, sin_t,
        d_qa, d_ka, d_va, d_qcat, d_kcat, d_vb, d_qm)
    gw["w_mem_kv"], gs["mem_norm_g"], gs["mem_k_norm_g"] = _mem_kv_bwd(
        mem, sp["mem_norm_g"], w["w_mem_kv"], sp["mem_k_norm_g"], mn_b, kv_m, d_km, d_vm)
    gw["w_in"] = _matmul(hn, d_proj, ta=True, name="dw_in", tk=1024, tn=_tile(IN_PAD, 1536))
    d_hn = _matmul(d_proj, w["w_in"], tb=True, name="dhn", tk=_tile(IN_PAD, 1536))
    grad_x, gs["attn_norm_g"] = _rms_bwd(d_hn, x, sp["attn_norm_g"], d_h1, name="attn_norm_bwd")

    fold = lambda r: r[:, :64] + r[:, 64:]
    gs["swa_q_norm_g"] = fold(dg128[G_SWA_Q:G_SWA_Q + 1])
    gs["swa_k_norm_g"] = fold(dg128[G_SWA_K:G_SWA_K + 1])
    gs["mla_qn_norm_g"] = dg128[G_QN:G_QN + 1]
    gs["mla_qr_norm_g"] = fold(dg128[G_QR:G_QR + 1])
    gs["mla_kn_norm_g"] = dg128[G_KN:G_KN + 1]
    gs["mla_kr_norm_g"] = fold(dg128[G_KR:G_KR + 1])
    gs["mem_q_norm_g"] = dg128[G_MQ:G_MQ + 1]
    gs["swa_sinks"] = d_sink[:, :SWA_Q_HEADS]
    return loss_tile, grad_x, gw, gs


CHIP_DISTANCES = (1, 2, 3)
ANY = pl.BlockSpec(memory_space=pl.ANY)


def _place():
    x, y, c = lax.axis_index("x"), lax.axis_index("y"), lax.axis_index("c")
    return x, y, c, 2 * x + y


def _chip_at(x, y, d):
    px = 1 - x if d & 2 else x
    py = 1 - y if d & 1 else y
    return px, py, 2 * px + py


def _gather_weights(shards):
    n = len(shards)

    def body(*refs):
        ins, outs = refs[:n], refs[n:2 * n]
        lsem, ssem, rsem, fssem, frsem = refs[2 * n:]
        x, y, c, k_me = _place()
        halves = [r.shape[0] // 2 for r in ins]

        def slab(w, k, cc):
            return outs[w].at[k, pl.ds(cc * halves[w], halves[w])]

        def to_chip(w, d):
            px, py, _ = _chip_at(x, y, d)
            return pltpu.make_async_remote_copy(
                src_ref=ins[w].at[pl.ds(c * halves[w], halves[w])], dst_ref=slab(w, k_me, c),
                send_sem=ssem.at[3 * w + d - 1], recv_sem=rsem.at[3 * w + d - 1],
                device_id=(px, py, c), device_id_type=MESH)

        def from_chip(w, d):
            _, _, k_src = _chip_at(x, y, d)
            return pltpu.make_async_remote_copy(
                src_ref=slab(w, k_src, c), dst_ref=slab(w, k_src, c),
                send_sem=ssem.at[3 * w + d - 1], recv_sem=rsem.at[3 * w + d - 1],
                device_id=(x, y, c), device_id_type=MESH)

        def to_sibling(w, d):
            _, _, k_src = _chip_at(x, y, d)
            return pltpu.make_async_remote_copy(
                src_ref=slab(w, k_src, c), dst_ref=slab(w, k_src, c),
                send_sem=fssem.at[3 * w + d - 1], recv_sem=frsem.at[3 * w + d - 1],
                device_id=(x, y, 1 - c), device_id_type=MESH)

        def from_sibling(w, d):
            _, _, k_src = _chip_at(x, y, d)
            return pltpu.make_async_remote_copy(
                src_ref=slab(w, k_src, 1 - c), dst_ref=slab(w, k_src, 1 - c),
                send_sem=fssem.at[3 * w + d - 1], recv_sem=frsem.at[3 * w + d - 1],
                device_id=(x, y, c), device_id_type=MESH)

        own = [pltpu.make_async_copy(ins[w], outs[w].at[k_me], lsem.at[w]) for w in range(n)]
        pairs = [(w, d) for w in range(n) for d in CHIP_DISTANCES]
        for cp in own:
            cp.start()
        for w, d in pairs:
            to_chip(w, d).start()
        for w, d in pairs:
            from_chip(w, d).wait_recv()
            to_sibling(w, d).start()
        for w, d in pairs:
            from_sibling(w, d).wait_recv()
        for w, d in pairs:
            to_chip(w, d).wait_send()
            to_sibling(w, d).wait_send()
        for cp in own:
            cp.wait()

    sem = pltpu.SemaphoreType.DMA
    return pl.pallas_call(
        body, name="gather_weights",
        in_specs=[ANY] * n, out_specs=[ANY] * n,
        out_shape=[jax.ShapeDtypeStruct((N_CHIPS,) + s.shape, s.dtype) for s in shards],
        scratch_shapes=[sem((n,)), sem((3 * n,)), sem((3 * n,)), sem((3 * n,)), sem((3 * n,))],
    )(*shards)


def _exchange_halves(grads):
    n = len(grads)

    def body(*refs):
        ins, outs = refs[:n], refs[n:2 * n]
        ssem, rsem = refs[2 * n:]
        x, y, c, _ = _place()
        copies = []
        for w in range(n):
            half = ins[w].shape[1] // 2
            copies.append(pltpu.make_async_remote_copy(
                src_ref=ins[w].at[:, pl.ds((1 - c) * half, half)], dst_ref=outs[w],
                send_sem=ssem.at[w], recv_sem=rsem.at[w], device_id=(x, y, 1 - c), device_id_type=MESH))
        for cp in copies:
            cp.start()
        for cp in copies:
            cp.wait()

    sem = pltpu.SemaphoreType.DMA
    return pl.pallas_call(
        body, name="grad_exchange_halves",
        in_specs=[ANY] * n, out_specs=[ANY] * n,
        out_shape=[jax.ShapeDtypeStruct((N_CHIPS, g.shape[1] // 2, g.shape[2]), g.dtype) for g in grads],
        scratch_shapes=[sem((n,)), sem((n,))],
    )(*grads)


def _exchange_chips(parts):
    n = len(parts)

    def body(*refs):
        ins, outs = refs[:n], refs[n:2 * n]
        ssem, rsem = refs[2 * n:]
        x, y, c, _ = _place()
        copies = []
        for w in range(n):
            for d in CHIP_DISTANCES:
                px, py, _ = _chip_at(x, y, d)
                copies.append(pltpu.make_async_remote_copy(
                    src_ref=ins[w].at[d], dst_ref=outs[w].at[d - 1],
                    send_sem=ssem.at[3 * w + d - 1], recv_sem=rsem.at[3 * w + d - 1],
                    device_id=(px, py, c), device_id_type=MESH))
        for cp in copies:
            cp.start()
        for cp in copies:
            cp.wait()

    sem = pltpu.SemaphoreType.DMA
    return pl.pallas_call(
        body, name="grad_exchange_chips",
        in_specs=[ANY] * n, out_specs=[ANY] * n,
        out_shape=[jax.ShapeDtypeStruct((3,) + p.shape[1:], p.dtype) for p in parts],
        scratch_shapes=[sem((3 * n,)), sem((3 * n,))],
    )(*parts)


def _join_halves(totals):
    n = len(totals)

    def body(*refs):
        ins, outs = refs[:n], refs[n:2 * n]
        lsem, ssem, rsem = refs[2 * n:]
        x, y, c, _ = _place()
        local, remote = [], []
        for w in range(n):
            half = ins[w].shape[0]
            mine = outs[w].at[pl.ds(c * half, half)]
            local.append(pltpu.make_async_copy(ins[w], mine, lsem.at[w]))
            remote.append(pltpu.make_async_remote_copy(
                src_ref=ins[w], dst_ref=mine, send_sem=ssem.at[w], recv_sem=rsem.at[w],
                device_id=(x, y, 1 - c), device_id_type=MESH))
        for cp in local + remote:
            cp.start()
        for w in range(n):
            half = ins[w].shape[0]
            theirs = outs[w].at[pl.ds((1 - c) * half, half)]
            pltpu.make_async_remote_copy(
                src_ref=theirs, dst_ref=theirs, send_sem=ssem.at[w], recv_sem=rsem.at[w],
                device_id=(x, y, c), device_id_type=MESH).wait_recv()
        for cp in remote:
            cp.wait_send()
        for cp in local:
            cp.wait()

    sem = pltpu.SemaphoreType.DMA
    return pl.pallas_call(
        body, name="grad_join_halves",
        in_specs=[ANY] * n, out_specs=[ANY] * n,
        out_shape=[jax.ShapeDtypeStruct((2 * t.shape[0], t.shape[1]), t.dtype) for t in totals],
        scratch_shapes=[sem((n,)), sem((n,)), sem((n,))],
    )(*totals)


def _row_tile(rows, want=512):
    t = min(rows, want)
    t -= t % 8
    while rows % t:
        t -= 8
    return t


def _add_pair(meta, g4, recv, *, name):
    nsh, rows, cols = g4.shape
    half = rows // 2
    tr = _row_tile(half)
    nt = half // tr

    def body(meta_ref, g_ref, r_ref, o_ref):
        o_ref[...] = g_ref[...] + r_ref[...]

    blk = (None, tr, cols)
    grid_spec = pltpu.PrefetchScalarGridSpec(
        num_scalar_prefetch=1, grid=(nsh, nt),
        in_specs=[pl.BlockSpec(blk, lambda d, i, m: (jnp.bitwise_xor(d, m[0]), m[1] * nt + i, 0)),
                  pl.BlockSpec(blk, lambda d, i, m: (jnp.bitwise_xor(d, m[0]), i, 0))],
        out_specs=pl.BlockSpec(blk, lambda d, i, m: (d, i, 0)))
    return pl.pallas_call(
        body, name=name, grid_spec=grid_spec,
        out_shape=jax.ShapeDtypeStruct((nsh, half, cols), F32),
        compiler_params=_params(("parallel", "parallel")),
    )(meta, g4, recv)


def _add_chips(part, recv, *, name):
    _, half, cols = part.shape
    tr = _row_tile(half)

    def body(p_ref, a_ref, b_ref, c_ref, o_ref):
        o_ref[...] = ((p_ref[...] + a_ref[...]) + b_ref[...]) + c_ref[...]

    slot = lambda k: pl.BlockSpec((None, tr, cols), lambda i: (k, i, 0))
    return pl.pallas_call(
        body, name=name, grid=(half // tr,),
        in_specs=[slot(0), slot(0), slot(1), slot(2)],
        out_specs=pl.BlockSpec((tr, cols), lambda i: (i, 0)),
        out_shape=jax.ShapeDtypeStruct((half, cols), F32),
        compiler_params=_params(("parallel",)),
    )(part, recv, recv, recv)


def _reduce_scatter(grads):
    x, y, c = lax.axis_index("x"), lax.axis_index("y"), lax.axis_index("c")
    meta = jnp.stack([2 * x + y, c]).astype(jnp.int32)
    from_sibling = _exchange_halves(grads)
    parts = [_add_pair(meta, g, r, name=f"grad_add_pair_{i}") for i, (g, r) in enumerate(zip(grads, from_sibling))]
    from_chips = _exchange_chips(parts)
    totals = [_add_chips(p, r, name=f"grad_add_chips_{i}") for i, (p, r) in enumerate(zip(parts, from_chips))]
    return _join_halves(totals)


def _adamw_math(w, g, m, v):
    m = ADAM_B1 * m + (1.0 - ADAM_B1) * g
    v = ADAM_B2 * v + (1.0 - ADAM_B2) * (g * g)
    m_hat = m / (1.0 - ADAM_B1 ** ADAM_STEP)
    v_hat = v / (1.0 - ADAM_B2 ** ADAM_STEP)
    delta = -ADAM_LR * (m_hat / (jnp.sqrt(v_hat) + ADAM_EPS) + ADAM_WD * w)
    return delta, m, v


def _adamw(w, g, m, v, *, name):
    rows, cols = w.shape
    tr = _row_tile(rows, 256)

    def body(w_ref, g_ref, m_ref, v_ref, d_ref, mo_ref, vo_ref):
        d_ref[...], mo_ref[...], vo_ref[...] = _adamw_math(w_ref[...], g_ref[...], m_ref[...], v_ref[...])

    blk = pl.BlockSpec((tr, cols), lambda i: (i, 0))
    sh = jax.ShapeDtypeStruct((rows, cols), F32)
    return pl.pallas_call(
        body, name=name, grid=(rows // tr,),
        in_specs=[blk] * 4, out_specs=[blk] * 3, out_shape=[sh] * 3,
        compiler_params=_params(("parallel",)),
    )(w, g, m, v)


N_DEVICES = 8


def _small_step(g_pack, w_pack, m_pack, v_pack):
    rows = g_pack.shape[0]

    def body(g_ref, w_ref, m_ref, v_ref, sum_ref, d_ref, mo_ref, vo_ref, slots, ssem, rsem):
        x, y, c, _ = _place()
        me = 4 * x + 2 * y + c
        slots[me] = g_ref[...]
        copies = []
        for r in range(1, N_DEVICES):
            px = 1 - x if r & 4 else x
            py = 1 - y if r & 2 else y
            pc = 1 - c if r & 1 else c
            copies.append(pltpu.make_async_remote_copy(
                src_ref=g_ref, dst_ref=slots.at[me], send_sem=ssem.at[r - 1], recv_sem=rsem.at[r - 1],
                device_id=(px, py, pc), device_id_type=MESH))
        for cp in copies:
            cp.start()
        for r in range(1, N_DEVICES):
            src = jnp.bitwise_xor(me, r)
            pltpu.make_async_remote_copy(
                src_ref=g_ref, dst_ref=slots.at[src], send_sem=ssem.at[r - 1], recv_sem=rsem.at[r - 1],
                device_id=(x, y, c), device_id_type=MESH).wait_recv()
        for cp in copies:
            cp.wait_send()
        total = slots[0]
        for k in range(1, N_DEVICES):
            total = total + slots[k]
        sum_ref[...] = total
        d_ref[...], mo_ref[...], vo_ref[...] = _adamw_math(w_ref[...], total, m_ref[...], v_ref[...])

    sh = jax.ShapeDtypeStruct((rows, LANES), F32)
    vm = pl.BlockSpec(memory_space=pltpu.VMEM)
    return pl.pallas_call(
        body, name="small_allreduce_adamw",
        in_specs=[vm] * 4, out_specs=[vm] * 4, out_shape=[sh] * 4,
        scratch_shapes=[pltpu.VMEM((N_DEVICES, rows, LANES), F32),
                        pltpu.SemaphoreType.DMA((N_DEVICES - 1,)), pltpu.SemaphoreType.DMA((N_DEVICES - 1,))],
    )(g_pack, w_pack, m_pack, v_pack)


WEIGHTS = ("attn_norm_g", "w_in", "swa_q_norm_g", "swa_k_norm_g", "swa_sinks", "mla_cq_norm_g", "mla_ckv_norm_g",
           "w_uq", "w_ukv", "mla_qn_norm_g", "mla_qr_norm_g", "mla_kn_norm_g", "mla_kr_norm_g", "mem_norm_g",
           "w_mem_kv", "mem_q_norm_g", "mem_k_norm_g", "w_out", "ffn_norm_g", "w_gate", "w_up", "w_down")
BIG = ("w_in", "w_uq", "w_ukv", "w_mem_kv", "w_out", "w_gate", "w_up", "w_down")
SMALL = tuple(n for n in WEIGHTS if n not in BIG)
PACK_UNIT = 8 * LANES


def _pack(parts):
    out = []
    for p in parts:
        n = p.shape[1]
        padded = -(-n // PACK_UNIT) * PACK_UNIT
        out.append(jnp.pad(p, ((0, 0), (0, padded - n))).reshape(padded // LANES, LANES))
    return jnp.concatenate(out, axis=0)


def _unpack(buf, sizes):
    out, row = [], 0
    for n in sizes:
        rows = -(-n // PACK_UNIT) * 8
        out.append(buf[row:row + rows].reshape(1, rows * LANES)[:, :n])
        row += rows
    return out


def kernel(x, mem, positions, attn_norm_g, w_in, swa_q_norm_g, swa_k_norm_g, swa_sinks, mla_cq_norm_g, mla_ckv_norm_g, w_uq, w_ukv, mla_qn_norm_g, mla_qr_norm_g, mla_kn_norm_g, mla_kr_norm_g, mem_norm_g, w_mem_kv, mem_q_norm_g, mem_k_norm_g, w_out, ffn_norm_g, w_gate, w_up, w_down, loss_target, m_attn_norm_g, m_w_in, m_swa_q_norm_g, m_swa_k_norm_g, m_swa_sinks, m_mla_cq_norm_g, m_mla_ckv_norm_g, m_w_uq, m_w_ukv, m_mla_qn_norm_g, m_mla_qr_norm_g, m_mla_kn_norm_g, m_mla_kr_norm_g, m_mem_norm_g, m_w_mem_kv, m_mem_q_norm_g, m_mem_k_norm_g, m_w_out, m_ffn_norm_g, m_w_gate, m_w_up, m_w_down, v_attn_norm_g, v_w_in, v_swa_q_norm_g, v_swa_k_norm_g, v_swa_sinks, v_mla_cq_norm_g, v_mla_ckv_norm_g, v_w_uq, v_w_ukv, v_mla_qn_norm_g, v_mla_qr_norm_g, v_mla_kn_norm_g, v_mla_kr_norm_g, v_mem_norm_g, v_w_mem_kv, v_mem_q_norm_g, v_mem_k_norm_g, v_w_out, v_ffn_norm_g, v_w_gate, v_w_up, v_w_down):
    given = dict(locals())
    wts = {n: given[n] for n in WEIGHTS}
    mom_m = {n: given["m_" + n] for n in WEIGHTS}
    mom_v = {n: given["v_" + n] for n in WEIGHTS}

    gathered = dict(zip(BIG, _gather_weights([wts[n][0].astype(BF16) for n in BIG])))
    full = {
        "w_in": _full_w_in(gathered["w_in"]),
        "w_uq": _full_heads(gathered["w_uq"], MLA_NOPE),
        "w_ukv": _full_heads(gathered["w_ukv"], MLA_NOPE),
        "w_mem_kv": gathered["w_mem_kv"].reshape(D_MODEL, 2 * MEM_HEADS * MEM_DIM),
        "w_out": gathered["w_out"].reshape(D_MODEL, D_MODEL),
        "w_gate": gathered["w_gate"],
        "w_up": gathered["w_up"],
        "w_down": gathered["w_down"].reshape(D_FF, D_MODEL),
    }
    sp = {n: wts[n] for n in SMALL}
    loss_tile, grad_x, gw, gs = _device_step(x[0], mem[0], positions[0], loss_target[0], sp, full)

    chip_major = {
        "w_in": _shards_w_in(gw["w_in"]),
        "w_uq": _shards_heads(gw["w_uq"], MLA_NOPE, MLA_ROPE),
        "w_ukv": _shards_heads(gw["w_ukv"], MLA_NOPE, MLA_V),
        "w_mem_kv": gw["w_mem_kv"].reshape(N_CHIPS, D_MODEL // N_CHIPS, -1),
        "w_out": gw["w_out"].reshape(N_CHIPS, D_MODEL // N_CHIPS, -1),
        "w_gate": gw["w_gate"],
        "w_up": gw["w_up"],
        "w_down": gw["w_down"].reshape(N_CHIPS, D_FF // N_CHIPS, -1),
    }
    reduced = dict(zip(BIG, _reduce_scatter([chip_major[n] for n in BIG])))

    grad, delta, new_m, new_v = {}, {}, {}, {}
    for n in BIG:
        grad[n] = reduced[n][None]
        d, m2, v2 = _adamw(wts[n][0], reduced[n], mom_m[n][0], mom_v[n][0], name="adamw_" + n)
        delta[n], new_m[n], new_v[n] = d[None], m2[None], v2[None]

    sizes = [wts[n].shape[1] for n in SMALL]
    zero = jnp.zeros((1, LANES), F32)
    packs = _small_step(_pack([gs[n] for n in SMALL] + [loss_tile]), _pack([wts[n] for n in SMALL] + [zero]),
                        _pack([mom_m[n] for n in SMALL] + [zero]), _pack([mom_v[n] for n in SMALL] + [zero]))
    for store, buf in zip((grad, delta, new_m, new_v), packs):
        for n, val in zip(SMALL, _unpack(buf, sizes)):
            store[n] = val
    loss = _unpack(packs[0], sizes + [LANES])[-1][0, 0]

    return (loss, grad_x[None], *[grad[n] for n in WEIGHTS], *[delta[n] for n in WEIGHTS],
            *[new_m[n] for n in WEIGHTS], *[new_v[n] for n in WEIGHTS])
```

```python
import functools
import math

import jax
import jax.numpy as jnp
from jax import lax
from jax.experimental import pallas as pl
from jax.experimental.pallas import tpu as pltpu

F32 = jnp.float32
BF16 = jnp.bfloat16

D_MODEL = 2048
BLOCK = 128
EPS = 1e-6
NEG_INF = -1e30
SWA_Q_HEADS = 16
SWA_KV_HEADS = 2
SWA_HEAD_DIM = 64
MLA_HEADS = 4
MLA_RANK = 512
MLA_NOPE = 128
MLA_ROPE = 64
MLA_V = 128
ROPE_THETA = 10000.0
MEM_HEADS = 4
MEM_DIM = 128
D_FF = 5632
IN_WIDTH = 2880
IN_PAD = 3072
N_CHIPS = 4

ADAM_LR = 0.001
ADAM_B1 = 0.9
ADAM_B2 = 0.999
ADAM_EPS = 1e-08
ADAM_WD = 0.01
ADAM_STEP = 10

VMEM_LIMIT_BYTES = 56 * 1024 * 1024
LANES = 128

MESH = pl.DeviceIdType.MESH


def _params(sem=None, **kw):
    return pltpu.CompilerParams(dimension_semantics=sem, vmem_limit_bytes=VMEM_LIMIT_BYTES, **kw)


def _tile(n, want):
    if n <= want:
        return n
    t = want - want % LANES
    while t > 0:
        if n % t == 0:
            return t
        t -= LANES
    return n


def _matmul(a, b, *, name, ta=False, tb=False, add=None, out_dtype=F32, tm=1024, tn=1024, tk=2048,
            b_split=False, out_split=0):
    if ta:
        kdim, m = a.shape
    else:
        m, kdim = a.shape
    if b_split:
        assert tb
        nsp, n, kb = b.shape
        kb = kb * nsp
    elif tb:
        n, kb = b.shape
    else:
        kb, n = b.shape
    assert kb == kdim, (a.shape, b.shape, ta, tb)
    if b_split:
        tk = kdim // nsp
    if out_split:
        tn = _tile(n // out_split, tn)
    tm, tn, tk = _tile(m, tm), _tile(n, tn), _tile(kdim, tk)
    nk = kdim // tk
    dims = (((0 if ta else 1,), (1 if tb else 0,)), ((), ()))

    def body(*refs):
        if add is None:
            a_ref, b_ref, o_ref, acc_ref = refs
            add_ref = None
        else:
            a_ref, b_ref, add_ref, o_ref, acc_ref = refs
        k = pl.program_id(2)
        part = lax.dot_general(a_ref[...].astype(BF16), b_ref[...].astype(BF16), dims,
                               preferred_element_type=F32)

        @pl.when(k == 0)
        def _():
            acc_ref[...] = part

        @pl.when(k > 0)
        def _():
            acc_ref[...] += part

        @pl.when(k == nk - 1)
        def _():
            r = acc_ref[...]
            if add_ref is not None:
                r = r + add_ref[...].astype(F32)
            o_ref[...] = r.astype(o_ref.dtype)

    a_spec = pl.BlockSpec((tk, tm), lambda i, j, k: (k, i)) if ta else pl.BlockSpec((tm, tk), lambda i, j, k: (i, k))
    if b_split:
        b_spec = pl.BlockSpec((None, tn, tk), lambda i, j, k: (k, j, 0))
    elif tb:
        b_spec = pl.BlockSpec((tn, tk), lambda i, j, k: (j, k))
    else:
        b_spec = pl.BlockSpec((tk, tn), lambda i, j, k: (k, j))
    in_specs = [a_spec, b_spec]
    args = [a, b]
    if add is not None:
        in_specs.append(pl.BlockSpec((tm, tn), lambda i, j, k: (i, j)))
        args.append(add)
    if out_split:
        per = (n // out_split) // tn
        out_spec = pl.BlockSpec((None, tm, tn), lambda i, j, k: (j // per, i, j % per))
        out_shape = jax.ShapeDtypeStruct((out_split, m, n // out_split), out_dtype)
    else:
        out_spec = pl.BlockSpec((tm, tn), lambda i, j, k: (i, j))
        out_shape = jax.ShapeDtypeStruct((m, n), out_dtype)
    return pl.pallas_call(
        body,
        name=name,
        grid=(m // tm, n // tn, nk),
        in_specs=in_specs,
        out_specs=out_spec,
        out_shape=out_shape,
        scratch_shapes=[pltpu.VMEM((tm, tn), F32)],
        compiler_params=_params(("parallel", "parallel", "arbitrary")),
    )(*args)


def _rms_fwd(x, g, *, name, tm=512):
    s, d = x.shape
    tm = _tile(s, tm)

    def body(x_ref, g_ref, o_ref):
        xv = x_ref[...]
        r = lax.rsqrt(jnp.mean(xv * xv, axis=-1, keepdims=True) + EPS)
        o_ref[...] = (xv * r * g_ref[...]).astype(o_ref.dtype)

    return pl.pallas_call(
        body, name=name, grid=(s // tm,),
        in_specs=[pl.BlockSpec((tm, d), lambda i: (i, 0)), pl.BlockSpec((1, d), lambda i: (0, 0))],
        out_specs=pl.BlockSpec((tm, d), lambda i: (i, 0)),
        out_shape=jax.ShapeDtypeStruct((s, d), BF16),
        compiler_params=_params(("parallel",)),
    )(x, g)


def _rms_bwd(dy, x, g, res, *, name, tm=512):
    s, d = x.shape
    tm = _tile(s, tm)

    def body(dy_ref, x_ref, g_ref, res_ref, dx_ref, dg_ref):
        xv = x_ref[...]
        dyv = dy_ref[...]
        r = lax.rsqrt(jnp.mean(xv * xv, axis=-1, keepdims=True) + EPS)
        xhat = xv * r
        dyg = dyv * g_ref[...]
        mt = jnp.mean(dyg * xhat, axis=-1, keepdims=True)
        dx_ref[...] = res_ref[...] + r * (dyg - xhat * mt)
        part = jnp.sum(dyv * xhat, axis=0, keepdims=True)

        @pl.when(pl.program_id(0) == 0)
        def _():
            dg_ref[...] = part

        @pl.when(pl.program_id(0) > 0)
        def _():
            dg_ref[...] += part

    row = pl.BlockSpec((tm, d), lambda i: (i, 0))
    vec = pl.BlockSpec((1, d), lambda i: (0, 0))
    return pl.pallas_call(
        body, name=name, grid=(s // tm,),
        in_specs=[row, row, vec, row],
        out_specs=[row, vec],
        out_shape=[jax.ShapeDtypeStruct((s, d), F32), jax.ShapeDtypeStruct((1, d), F32)],
        compiler_params=_params(("arbitrary",)),
    )(dy, x, g, res)


def _lane(shape):
    return lax.broadcasted_iota(jnp.int32, shape, 1)


def _halfsum(t, lo):
    s_lo = jnp.sum(jnp.where(lo, t, 0.0), axis=-1, keepdims=True)
    s_hi = jnp.sum(jnp.where(lo, 0.0, t), axis=-1, keepdims=True)
    return jnp.where(lo, s_lo, s_hi)


def _norm_pair(x, g, lo):
    r = lax.rsqrt(_halfsum(x * x, lo) * (1.0 / 64.0) + EPS)
    xhat = x * r
    return xhat * g, xhat, r


def _norm_pair_bwd(dy, g, xhat, r, lo):
    dyg = dy * g
    mt = _halfsum(dyg * xhat, lo) * (1.0 / 64.0)
    return r * (dyg - xhat * mt), jnp.sum(dy * xhat, axis=0, keepdims=True)


def _norm_full(x, g):
    r = lax.rsqrt(jnp.mean(x * x, axis=-1, keepdims=True) + EPS)
    xhat = x * r
    return xhat * g, xhat, r


def _norm_full_bwd(dy, g, xhat, r):
    dyg = dy * g
    mt = jnp.mean(dyg * xhat, axis=-1, keepdims=True)
    return r * (dyg - xhat * mt), jnp.sum(dy * xhat, axis=0, keepdims=True)


def _rot(x, first32):
    return jnp.where(first32, pltpu.roll(x, 96, axis=1), pltpu.roll(x, 32, axis=1))


def _rope(x, cos_t, sin_t, first32):
    return x * cos_t + _rot(x, first32) * sin_t


def _rope_bwd(dy, cos_t, sin_t, first32):
    return dy * cos_t + _rot(dy * sin_t, first32)


G_SWA_Q, G_SWA_K, G_QN, G_QR, G_KN, G_KR, G_MQ = range(7)

C_QA, C_KA, C_VA, C_CQ, C_CKV, C_QM, C_KR = 0, 1024, 1152, 1280, 1792, 2304, 2816


def _prep_common(p_ref, g128_ref, gcq_ref, gckv_ref, wuq_ref, wukv_ref, cos_ref, sin_ref):
    tm = p_ref.shape[0]
    lane = _lane((tm, LANES))
    lo = lane < 64
    first32 = (lane % 64) < 32
    cos_t = cos_ref[...]
    sin_t = sin_ref[...]
    g = lambda row: g128_ref[row:row + 1, :]
    out = dict(lo=lo, first32=first32, cos_t=cos_t, sin_t=sin_t, lane=lane)
    cq_n, cq_hat, cq_r = _norm_full(p_ref[:, C_CQ:C_CQ + MLA_RANK], gcq_ref[...])
    ckv_n, ckv_hat, ckv_r = _norm_full(p_ref[:, C_CKV:C_CKV + MLA_RANK], gckv_ref[...])
    cq_b = cq_n.astype(BF16)
    ckv_b = ckv_n.astype(BF16)
    q_b = jnp.dot(cq_b, wuq_ref[...], preferred_element_type=F32)
    kv_b = jnp.dot(ckv_b, wukv_ref[...], preferred_element_type=F32)
    out.update(cq_b=cq_b, cq_hat=cq_hat, cq_r=cq_r, ckv_b=ckv_b, ckv_hat=ckv_hat, ckv_r=ckv_r, q_b=q_b, kv_b=kv_b, g=g)
    return out


def _attn_prep_fwd(proj, g128, gcq, gckv, wuq, wukv, cos_t, sin_t, *, tm=512):
    s = proj.shape[0]
    tm = _tile(s, tm)

    def body(p_ref, g128_ref, gcq_ref, gckv_ref, wuq_ref, wukv_ref, cos_ref, sin_ref,
             qa_ref, ka_ref, va_ref, qcat_ref, kcat_ref, vb_ref, qm_ref):
        c = _prep_common(p_ref, g128_ref, gcq_ref, gckv_ref, wuq_ref, wukv_ref, cos_ref, sin_ref)
        lo, first32, g = c["lo"], c["first32"], c["g"]
        for j in range(SWA_Q_HEADS // 2):
            y, _, _ = _norm_pair(p_ref[:, C_QA + 128 * j:C_QA + 128 * (j + 1)], g(G_SWA_Q), lo)
            qa_ref[:, 128 * j:128 * (j + 1)] = y.astype(BF16)
        y, _, _ = _norm_pair(p_ref[:, C_KA:C_KA + 128], g(G_SWA_K), lo)
        ka_ref[...] = y.astype(BF16)
        va_ref[...] = p_ref[:, C_VA:C_VA + 128].astype(BF16)
        kr, _, _ = _norm_pair(p_ref[:, C_KR:C_KR + 128], g(G_KR), lo)
        kr = jnp.where(lo, _rope(kr, c["cos_t"], c["sin_t"], first32), 0.0)
        krkr = (kr + pltpu.roll(kr, 64, axis=1)).astype(BF16)
        q_b, kv_b = c["q_b"], c["kv_b"]
        qr = []
        for j in range(MLA_HEADS // 2):
            y, _, _ = _norm_pair(q_b[:, 512 + 128 * j:512 + 128 * (j + 1)], g(G_QR), lo)
            qr.append(_rope(y, c["cos_t"], c["sin_t"], first32))
        for h in range(MLA_HEADS):
            qn, _, _ = _norm_full(q_b[:, 128 * h:128 * (h + 1)], g(G_QN))
            keep = lo if h % 2 == 0 else jnp.logical_not(lo)
            qcat_ref[h, :, 0:128] = qn.astype(BF16)
            qcat_ref[h, :, 128:256] = jnp.where(keep, qr[h // 2], 0.0).astype(BF16)
            kn, _, _ = _norm_full(kv_b[:, 128 * h:128 * (h + 1)], g(G_KN))
            kcat_ref[h, :, 0:128] = kn.astype(BF16)
            kcat_ref[h, :, 128:256] = krkr
        vb_ref[...] = kv_b[:, 512:1024].astype(BF16)
        for h in range(MEM_HEADS):
            y, _, _ = _norm_full(p_ref[:, C_QM + 128 * h:C_QM + 128 * (h + 1)], g(G_MQ))
            qm_ref[:, 128 * h:128 * (h + 1)] = y.astype(BF16)

    row = lambda w: pl.BlockSpec((tm, w), lambda i: (i, 0))
    full = lambda shape: pl.BlockSpec(shape, lambda i: tuple(0 for _ in shape))
    cat = pl.BlockSpec((MLA_HEADS, tm, 256), lambda i: (0, i, 0))
    return pl.pallas_call(
        body, name="attn_prep_fwd", grid=(s // tm,),
        in_specs=[row(IN_PAD), full((8, 128)), full((1, 512)), full((1, 512)), full((512, 768)), full((512, 1024)),
                  row(128), row(128)],
        out_specs=[row(1024), row(128), row(128), cat, cat, row(512), row(512)],
        out_shape=[jax.ShapeDtypeStruct((s, 1024), BF16), jax.ShapeDtypeStruct((s, 128), BF16),
                   jax.ShapeDtypeStruct((s, 128), BF16), jax.ShapeDtypeStruct((MLA_HEADS, s, 256), BF16),
                   jax.ShapeDtypeStruct((MLA_HEADS, s, 256), BF16), jax.ShapeDtypeStruct((s, 512), BF16),
                   jax.ShapeDtypeStruct((s, 512), BF16)],
        compiler_params=_params(("parallel",)),
    )(proj, g128, gcq, gckv, wuq, wukv, cos_t, sin_t)


def _attn_prep_bwd(proj, g128, gcq, gckv, wuq, wukv, cos_t, sin_t,
                   d_qa, d_ka, d_va, d_qcat, d_kcat, d_vb, d_qm, *, tm=256):
    s = proj.shape[0]
    tm = _tile(s, tm)

    def body(p_ref, g128_ref, gcq_ref, gckv_ref, wuq_ref, wukv_ref, cos_ref, sin_ref,
             dqa_ref, dka_ref, dva_ref, dqcat_ref, dkcat_ref, dvb_ref, dqm_ref,
             dp_ref, dwuq_ref, dwukv_ref, dg128_ref, dgcq_ref, dgckv_ref):
        c = _prep_common(p_ref, g128_ref, gcq_ref, gckv_ref, wuq_ref, wukv_ref, cos_ref, sin_ref)
        lo, first32, g = c["lo"], c["first32"], c["g"]
        cos_v, sin_v = c["cos_t"], c["sin_t"]
        q_b, kv_b = c["q_b"], c["kv_b"]
        zero_row = jnp.zeros((1, LANES), F32)
        dg = {k: zero_row for k in range(7)}

        for j in range(SWA_Q_HEADS // 2):
            sl = slice(C_QA + 128 * j, C_QA + 128 * (j + 1))
            _, xhat, r = _norm_pair(p_ref[:, sl], g(G_SWA_Q), lo)
            dx, dgj = _norm_pair_bwd(dqa_ref[:, 128 * j:128 * (j + 1)], g(G_SWA_Q), xhat, r, lo)
            dp_ref[:, sl] = dx
            dg[G_SWA_Q] = dg[G_SWA_Q] + dgj
        _, xhat, r = _norm_pair(p_ref[:, C_KA:C_KA + 128], g(G_SWA_K), lo)
        dx, dgj = _norm_pair_bwd(dka_ref[...], g(G_SWA_K), xhat, r, lo)
        dp_ref[:, C_KA:C_KA + 128] = dx
        dg[G_SWA_K] = dgj
        dp_ref[:, C_VA:C_VA + 128] = dva_ref[...]

        dqb_parts = [None] * 6
        for h in range(MLA_HEADS):
            _, xhat, r = _norm_full(q_b[:, 128 * h:128 * (h + 1)], g(G_QN))
            dx, dgj = _norm_full_bwd(dqcat_ref[h, :, 0:128], g(G_QN), xhat, r)
            dqb_parts[h] = dx
            dg[G_QN] = dg[G_QN] + dgj
        for j in range(MLA_HEADS // 2):
            _, xhat, r = _norm_pair(q_b[:, 512 + 128 * j:512 + 128 * (j + 1)], g(G_QR), lo)
            d_rot = jnp.where(lo, dqcat_ref[2 * j, :, 128:256], dqcat_ref[2 * j + 1, :, 128:256])
            d_y = _rope_bwd(d_rot, cos_v, sin_v, first32)
            dx, dgj = _norm_pair_bwd(d_y, g(G_QR), xhat, r, lo)
            dqb_parts[4 + j] = dx
            dg[G_QR] = dg[G_QR] + dgj
        d_qb = jnp.concatenate(dqb_parts, axis=1).astype(BF16)
        dwuq = lax.dot_general(c["cq_b"], d_qb, (((0,), (0,)), ((), ())), preferred_element_type=F32)
        d_cqn = lax.dot_general(d_qb, wuq_ref[...], (((1,), (1,)), ((), ())), preferred_element_type=F32)
        dx, dgcq = _norm_full_bwd(d_cqn, gcq_ref[...], c["cq_hat"], c["cq_r"])
        dp_ref[:, C_CQ:C_CQ + MLA_RANK] = dx

        dkv_parts = []
        d_krkr = jnp.zeros((p_ref.shape[0], LANES), F32)
        for h in range(MLA_HEADS):
            _, xhat, r = _norm_full(kv_b[:, 128 * h:128 * (h + 1)], g(G_KN))
            dx, dgj = _norm_full_bwd(dkcat_ref[h, :, 0:128], g(G_KN), xhat, r)
            dkv_parts.append(dx)
            dg[G_KN] = dg[G_KN] + dgj
            d_krkr = d_krkr + dkcat_ref[h, :, 128:256]
        d_kvb = jnp.concatenate(dkv_parts + [dvb_ref[...]], axis=1).astype(BF16)
        dwukv = lax.dot_general(c["ckv_b"], d_kvb, (((0,), (0,)), ((), ())), preferred_element_type=F32)
        d_ckvn = lax.dot_general(d_kvb, wukv_ref[...], (((1,), (1,)), ((), ())), preferred_element_type=F32)
        dx, dgckv = _norm_full_bwd(d_ckvn, gckv_ref[...], c["ckv_hat"], c["ckv_r"])
        dp_ref[:, C_CKV:C_CKV + MLA_RANK] = dx

        _, xhat, r = _norm_pair(p_ref[:, C_KR:C_KR + 128], g(G_KR), lo)
        d_kr = jnp.where(lo, d_krkr + pltpu.roll(d_krkr, 64, axis=1), 0.0)
        d_y = jnp.where(lo, _rope_bwd(d_kr, cos_v, sin_v, first32), 0.0)
        dx, dgj = _norm_pair_bwd(d_y, g(G_KR), xhat, r, lo)
        dp_ref[:, C_KR:C_KR + 128] = jnp.where(lo, dx, 0.0)
        dp_ref[:, C_KR + 128:] = jnp.zeros((p_ref.shape[0], IN_PAD - C_KR - 128), F32)
        dg[G_KR] = dgj

        for h in range(MEM_HEADS):
            sl = slice(C_QM + 128 * h, C_QM + 128 * (h + 1))
            _, xhat, r = _norm_full(p_ref[:, sl], g(G_MQ))
            dx, dgj = _norm_full_bwd(dqm_ref[:, 128 * h:128 * (h + 1)], g(G_MQ), xhat, r)
            dp_ref[:, sl] = dx
            dg[G_MQ] = dg[G_MQ] + dgj

        dg_tile = jnp.concatenate([dg[k] for k in range(7)] + [zero_row], axis=0)

        @pl.when(pl.program_id(0) == 0)
        def _():
            dwuq_ref[...] = dwuq
            dwukv_ref[...] = dwukv
            dg128_ref[...] = dg_tile
            dgcq_ref[...] = dgcq
            dgckv_ref[...] = dgckv

        @pl.when(pl.program_id(0) > 0)
        def _():
            dwuq_ref[...] += dwuq
            dwukv_ref[...] += dwukv
            dg128_ref[...] += dg_tile
            dgcq_ref[...] += dgcq
            dgckv_ref[...] += dgckv

    row = lambda w: pl.BlockSpec((tm, w), lambda i: (i, 0))
    full = lambda shape: pl.BlockSpec(shape, lambda i: tuple(0 for _ in shape))
    cat = pl.BlockSpec((MLA_HEADS, tm, 256), lambda i: (0, i, 0))
    return pl.pallas_call(
        body, name="attn_prep_bwd", grid=(s // tm,),
        in_specs=[row(IN_PAD), full((8, 128)), full((1, 512)), full((1, 512)), full((512, 768)), full((512, 1024)),
                  row(128), row(128),
                  row(1024), row(128), row(128), cat, cat, row(512), row(512)],
        out_specs=[row(IN_PAD), full((512, 768)), full((512, 1024)), full((8, 128)), full((1, 512)), full((1, 512))],
        out_shape=[jax.ShapeDtypeStruct((s, IN_PAD), F32), jax.ShapeDtypeStruct((512, 768), F32),
                   jax.ShapeDtypeStruct((512, 1024), F32), jax.ShapeDtypeStruct((8, 128), F32),
                   jax.ShapeDtypeStruct((1, 512), F32), jax.ShapeDtypeStruct((1, 512), F32)],
        compiler_params=_params(("arbitrary",)),
    )(proj, g128, gcq, gckv, wuq, wukv, cos_t, sin_t, d_qa, d_ka, d_va, d_qcat, d_kcat, d_vb, d_qm)


SWA_SLOPES = tuple(2.0 ** (-8.0 * h / SWA_Q_HEADS) for h in range(1, SWA_Q_HEADS + 1))
SWA_SCALE = SWA_HEAD_DIM ** -0.5
NT_DIMS = (((1,), (1,)), ((), ()))
TN_DIMS = (((0,), (0,)), ((), ()))


def _swa_span(n, kp_ref, kc_ref, vp_ref, vc_ref, pcol_ref, pprow_ref, pcrow_ref):
    k_span = jnp.concatenate([kp_ref[...], kc_ref[...]], axis=0).astype(F32)
    v_span = jnp.concatenate([vp_ref[...], vc_ref[...]], axis=0).astype(F32)
    lo = _lane((2 * BLOCK, LANES)) < 64
    k_sw = pltpu.roll(k_span, 64, axis=1)
    v_sw = pltpu.roll(v_span, 64, axis=1)
    kk = (jnp.where(lo, k_span, k_sw).astype(BF16), jnp.where(lo, k_sw, k_span).astype(BF16))
    vv_lo = (jnp.where(lo, v_span, 0.0).astype(BF16), jnp.where(lo, v_sw, 0.0).astype(BF16))
    vv_hi = (jnp.where(lo, 0.0, v_sw).astype(BF16), jnp.where(lo, 0.0, v_span).astype(BF16))
    pk = jnp.concatenate([pprow_ref[...], pcrow_ref[...]], axis=1)
    dist = jnp.abs(pcol_ref[...] - pk)
    qi = lax.broadcasted_iota(jnp.int32, (BLOCK, 2 * BLOCK), 0)
    ki = lax.broadcasted_iota(jnp.int32, (BLOCK, 2 * BLOCK), 1)
    first_key = jnp.where(n > 0, qi + 1, jnp.maximum(qi + 1, BLOCK))
    valid = jnp.logical_and(ki >= first_key, ki <= qi + BLOCK)
    mask_add = jnp.where(valid, 0.0, NEG_INF)
    return kk, vv_lo, vv_hi, dist, mask_add


def _swa_probs(q_half, kk, dist, mask_add, slope, sink):
    s = lax.dot_general(q_half, kk, NT_DIMS, preferred_element_type=F32) * SWA_SCALE - slope * dist + mask_add
    m = jnp.maximum(jnp.max(s, axis=-1, keepdims=True), sink)
    e = jnp.exp(s - m)
    e_sink = jnp.exp(sink - m)
    inv = 1.0 / (jnp.sum(e, axis=-1, keepdims=True) + e_sink)
    return e * inv, e_sink * inv


def _swa_specs():
    blk = lambda w: pl.BlockSpec((BLOCK, w), lambda n: (n, 0))
    prev = lambda w: pl.BlockSpec((BLOCK, w), lambda n: (jnp.maximum(n - 1, 0), 0))
    prow_c = pl.BlockSpec((1, BLOCK), lambda n: (0, n))
    prow_p = pl.BlockSpec((1, BLOCK), lambda n: (0, jnp.maximum(n - 1, 0)))
    smem = pl.BlockSpec(memory_space=pltpu.SMEM)
    return [blk(1024), prev(128), blk(128), prev(128), blk(128), blk(1), prow_p, prow_c, smem], blk


def _swa_fwd(qa, ka, va, pos_col, pos_row, sinks):
    s = qa.shape[0]
    in_specs, blk = _swa_specs()

    def body(q_ref, kp_ref, kc_ref, vp_ref, vc_ref, pcol_ref, pprow_ref, pcrow_ref, sink_ref, o_ref):
        n = pl.program_id(0)
        kk, vv_lo, vv_hi, dist, mask_add = _swa_span(n, kp_ref, kc_ref, vp_ref, vc_ref, pcol_ref, pprow_ref, pcrow_ref)
        lo = _lane((BLOCK, LANES)) < 64
        for j in range(SWA_Q_HEADS // 2):
            kv = (2 * j) // (SWA_Q_HEADS // SWA_KV_HEADS)
            q_pair = q_ref[:, 128 * j:128 * (j + 1)].astype(F32)
            q_e = jnp.where(lo, q_pair, 0.0).astype(BF16)
            q_o = jnp.where(lo, 0.0, q_pair).astype(BF16)
            p_e, _ = _swa_probs(q_e, kk[kv], dist, mask_add, SWA_SLOPES[2 * j], sink_ref[2 * j])
            p_o, _ = _swa_probs(q_o, kk[kv], dist, mask_add, SWA_SLOPES[2 * j + 1], sink_ref[2 * j + 1])
            o_ref[:, 128 * j:128 * (j + 1)] = (
                jnp.dot(p_e.astype(BF16), vv_lo[kv], preferred_element_type=F32)
                + jnp.dot(p_o.astype(BF16), vv_hi[kv], preferred_element_type=F32))

    return pl.pallas_call(
        body, name="swa_fwd", grid=(s // BLOCK,),
        in_specs=in_specs, out_specs=blk(1024),
        out_shape=jax.ShapeDtypeStruct((s, 1024), F32),
        compiler_params=_params(("parallel",)),
    )(qa, ka, ka, va, va, pos_col, pos_row, pos_row, sinks)


def _swa_bwd(qa, ka, va, pos_col, pos_row, sinks, y_a, d_y):
    s = qa.shape[0]
    in_specs, blk = _swa_specs()
    whole = pl.BlockSpec((s, 128), lambda n: (0, 0))

    def body(q_ref, kp_ref, kc_ref, vp_ref, vc_ref, pcol_ref, pprow_ref, pcrow_ref, sink_ref, y_ref, dy_ref,
             dq_ref, dk_ref, dv_ref, dsink_ref):
        n = pl.program_id(0)

        @pl.when(n == 0)
        def _():
            dk_ref[...] = jnp.zeros_like(dk_ref)
            dv_ref[...] = jnp.zeros_like(dv_ref)
            dsink_ref[...] = jnp.zeros_like(dsink_ref)

        kk, vv_lo, vv_hi, dist, mask_add = _swa_span(n, kp_ref, kc_ref, vp_ref, vc_ref, pcol_ref, pprow_ref, pcrow_ref)
        lo = _lane((BLOCK, LANES)) < 64
        lo2 = _lane((2 * BLOCK, LANES)) < 64
        lane1 = _lane((1, LANES))
        dsink = jnp.zeros((1, LANES), F32)
        dkk = [jnp.zeros((2 * BLOCK, LANES), F32) for _ in range(SWA_KV_HEADS)]
        dvv = [jnp.zeros((2 * BLOCK, LANES), F32) for _ in range(SWA_KV_HEADS)]
        for j in range(SWA_Q_HEADS // 2):
            kv = (2 * j) // (SWA_Q_HEADS // SWA_KV_HEADS)
            q_pair = q_ref[:, 128 * j:128 * (j + 1)].astype(F32)
            do_pair = dy_ref[:, 128 * j:128 * (j + 1)]
            do_b = do_pair.astype(BF16)
            doy = do_pair * y_ref[:, 128 * j:128 * (j + 1)]
            deltas = (jnp.sum(jnp.where(lo, doy, 0.0), axis=-1, keepdims=True),
                      jnp.sum(jnp.where(lo, 0.0, doy), axis=-1, keepdims=True))
            dq_halves = []
            for par, vv in ((0, vv_lo), (1, vv_hi)):
                h = 2 * j + par
                keep = lo if par == 0 else jnp.logical_not(lo)
                q_h = jnp.where(keep, q_pair, 0.0).astype(BF16)
                p, p_sink = _swa_probs(q_h, kk[kv], dist, mask_add, SWA_SLOPES[h], sink_ref[h])
                delta_h = deltas[par]
                dp = lax.dot_general(do_b, vv[kv], NT_DIMS, preferred_element_type=F32)
                ds = p * (dp - delta_h)
                dsink = dsink + jnp.where(lane1 == h, -jnp.sum(p_sink * delta_h, axis=0, keepdims=True), 0.0)
                ds_b = (ds * SWA_SCALE).astype(BF16)
                dq_halves.append(jnp.dot(ds_b, kk[kv], preferred_element_type=F32))
                dkk[kv] = dkk[kv] + lax.dot_general(ds_b, q_h, TN_DIMS, preferred_element_type=F32)
                pv = lax.dot_general(p.astype(BF16), do_b, TN_DIMS, preferred_element_type=F32)
                keep2 = lo2 if par == 0 else jnp.logical_not(lo2)
                dvv[kv] = dvv[kv] + jnp.where(keep2, pv, 0.0)
            dq_ref[:, 128 * j:128 * (j + 1)] = jnp.where(lo, dq_halves[0], dq_halves[1])
        fold = lambda t: t + pltpu.roll(t, 64, axis=1)
        dk_span = jnp.where(lo2, fold(dkk[0]), fold(dkk[1]))
        dv_span = jnp.where(lo2, fold(dvv[0]), fold(dvv[1]))
        prev0 = pl.multiple_of(jnp.maximum(n - 1, 0) * BLOCK, BLOCK)
        cur0 = pl.multiple_of(n * BLOCK, BLOCK)
        dk_ref[pl.ds(prev0, BLOCK), :] += dk_span[0:BLOCK]
        dk_ref[pl.ds(cur0, BLOCK), :] += dk_span[BLOCK:]
        dv_ref[pl.ds(prev0, BLOCK), :] += dv_span[0:BLOCK]
        dv_ref[pl.ds(cur0, BLOCK), :] += dv_span[BLOCK:]
        dsink_ref[...] += dsink

    return pl.pallas_call(
        body, name="swa_bwd", grid=(s // BLOCK,),
        in_specs=in_specs + [blk(1024), blk(1024)],
        out_specs=[blk(1024), whole, whole, pl.BlockSpec((1, LANES), lambda n: (0, 0))],
        out_shape=[jax.ShapeDtypeStruct((s, 1024), F32), jax.ShapeDtypeStruct((s, 128), F32),
                   jax.ShapeDtypeStruct((s, 128), F32), jax.ShapeDtypeStruct((1, LANES), F32)],
        compiler_params=_params(("arbitrary",)),
    )(qa, ka, ka, va, va, pos_col, pos_row, pos_row, sinks, y_a, d_y)


MLA_SCALE = (MLA_NOPE + MLA_ROPE) ** -0.5
MLA_TILE = 512


def _causal_mask(i, j, t):
    row = i * t + lax.broadcasted_iota(jnp.int32, (t, t), 0)
    col = j * t + lax.broadcasted_iota(jnp.int32, (t, t), 1)
    return col <= row


def _mla_fwd(q_cat, k_cat, v_b):
    nh, s, _ = q_cat.shape
    t = _tile(s, MLA_TILE)
    nt = s // t

    def body(q_ref, k_ref, v_ref, o_ref, lse_ref, m_sc, l_sc, acc_sc):
        i, j = pl.program_id(1), pl.program_id(2)

        @pl.when(j == 0)
        def _():
            m_sc[...] = jnp.full_like(m_sc, NEG_INF)
            l_sc[...] = jnp.zeros_like(l_sc)
            acc_sc[...] = jnp.zeros_like(acc_sc)

        @pl.when(j <= i)
        def _():
            sc = lax.dot_general(q_ref[0], k_ref[0], NT_DIMS, preferred_element_type=F32) * MLA_SCALE
            sc = jnp.where(_causal_mask(i, j, t), sc, NEG_INF)
            m_new = jnp.maximum(m_sc[...], jnp.max(sc, axis=-1, keepdims=True))
            alpha = jnp.exp(m_sc[...] - m_new)
            p = jnp.exp(sc - m_new)
            l_sc[...] = alpha * l_sc[...] + jnp.sum(p, axis=-1, keepdims=True)
            acc_sc[...] = alpha * acc_sc[...] + jnp.dot(p.astype(BF16), v_ref[...], preferred_element_type=F32)
            m_sc[...] = m_new

        @pl.when(j == i)
        def _():
            o_ref[...] = acc_sc[...] * (1.0 / l_sc[...])
            lse_ref[0] = m_sc[...] + jnp.log(l_sc[...])

    return pl.pallas_call(
        body, name="mla_fwd", grid=(nh, nt, nt),
        in_specs=[pl.BlockSpec((1, t, 256), lambda h, i, j: (h, i, 0)),
                  pl.BlockSpec((1, t, 256), lambda h, i, j: (h, jnp.minimum(j, i), 0)),
                  pl.BlockSpec((t, MLA_V), lambda h, i, j: (jnp.minimum(j, i), h))],
        out_specs=[pl.BlockSpec((t, MLA_V), lambda h, i, j: (i, h)),
                   pl.BlockSpec((1, t, 1), lambda h, i, j: (h, i, 0))],
        out_shape=[jax.ShapeDtypeStruct((s, nh * MLA_V), F32), jax.ShapeDtypeStruct((nh, s, 1), F32)],
        scratch_shapes=[pltpu.VMEM((t, 1), F32), pltpu.VMEM((t, 1), F32), pltpu.VMEM((t, MLA_V), F32)],
        compiler_params=_params(("parallel", "parallel", "arbitrary")),
    )(q_cat, k_cat, v_b)


def _mla_bwd(q_cat, k_cat, v_b, y_b, lse, d_y):
    nh, s, _ = q_cat.shape
    t = _tile(s, MLA_TILE)
    nt = s // t
    col0 = (SWA_Q_HEADS * SWA_HEAD_DIM) // MLA_V

    def body(q_ref, k_ref, v_ref, y_ref, lse_ref, dy_ref, dq_ref, dk_ref, dv_ref, dk_sc, dv_sc):
        j, i = pl.program_id(1), pl.program_id(2)

        @pl.when((j == 0) & (i == 0))
        def _():
            dq_ref[...] = jnp.zeros_like(dq_ref)

        @pl.when(i == 0)
        def _():
            dk_sc[...] = jnp.zeros_like(dk_sc)
            dv_sc[...] = jnp.zeros_like(dv_sc)

        @pl.when(i >= j)
        def _():
            q = q_ref[0]
            k = k_ref[0]
            do = dy_ref[...]
            do_b = do.astype(BF16)
            sc = lax.dot_general(q, k, NT_DIMS, preferred_element_type=F32) * MLA_SCALE
            p = jnp.where(_causal_mask(i, j, t), jnp.exp(sc - lse_ref[0]), 0.0)
            delta = jnp.sum(do * y_ref[...], axis=-1, keepdims=True)
            dv_sc[...] += lax.dot_general(p.astype(BF16), do_b, TN_DIMS, preferred_element_type=F32)
            dp = lax.dot_general(do_b, v_ref[...], NT_DIMS, preferred_element_type=F32)
            ds_b = (p * (dp - delta) * MLA_SCALE).astype(BF16)
            dk_sc[...] += lax.dot_general(ds_b, q, TN_DIMS, preferred_element_type=F32)
            rows = pl.ds(pl.multiple_of(i * t, t), t)
            dq_ref[0, rows, :] += jnp.dot(ds_b, k, preferred_element_type=F32)

        @pl.when(i == nt - 1)
        def _():
            dk_ref[0] = dk_sc[...]
            dv_ref[...] = dv_sc[...]

    qrow = lambda h, j, i: (h, jnp.maximum(i, j), 0)
    return pl.pallas_call(
        body, name="mla_bwd", grid=(nh, nt, nt),
        in_specs=[pl.BlockSpec((1, t, 256), qrow),
                  pl.BlockSpec((1, t, 256), lambda h, j, i: (h, j, 0)),
                  pl.BlockSpec((t, MLA_V), lambda h, j, i: (j, h)),
                  pl.BlockSpec((t, MLA_V), lambda h, j, i: (jnp.maximum(i, j), h)),
                  pl.BlockSpec((1, t, 1), qrow),
                  pl.BlockSpec((t, MLA_V), lambda h, j, i: (jnp.maximum(i, j), col0 + h))],
        out_specs=[pl.BlockSpec((1, s, 256), lambda h, j, i: (h, 0, 0)),
                   pl.BlockSpec((1, t, 256), lambda h, j, i: (h, j, 0)),
                   pl.BlockSpec((t, MLA_V), lambda h, j, i: (j, h))],
        out_shape=[jax.ShapeDtypeStruct((nh, s, 256), F32), jax.ShapeDtypeStruct((nh, s, 256), F32),
                   jax.ShapeDtypeStruct((s, nh * MLA_V), F32)],
        scratch_shapes=[pltpu.VMEM((t, 256), F32), pltpu.VMEM((t, MLA_V), F32)],
        compiler_params=_params(("arbitrary", "arbitrary", "arbitrary")),
    )(q_cat, k_cat, v_b, y_b, lse, d_y)


MEM_SCALE = MEM_DIM ** -0.5


def _mem_kv_fwd(mem, g_mem, w_memkv, g_mk):
    m_len = mem.shape[0]

    def body(mem_ref, g_ref, w_ref, gk_ref, mn_ref, kv_ref, kn_ref, v_ref):
        mn, _, _ = _norm_full(mem_ref[...], g_ref[...])
        mn_b = mn.astype(BF16)
        mn_ref[...] = mn_b
        kv = jnp.dot(mn_b, w_ref[...], preferred_element_type=F32)
        kv_ref[...] = kv
        for h in range(MEM_HEADS):
            kn, _, _ = _norm_full(kv[:, 128 * h:128 * (h + 1)], gk_ref[...])
            kn_ref[:, 128 * h:128 * (h + 1)] = kn.astype(BF16)
        v_ref[...] = kv[:, 512:1024].astype(BF16)

    return pl.pallas_call(
        body, name="mem_kv_fwd",
        out_shape=[jax.ShapeDtypeStruct((m_len, D_MODEL), BF16), jax.ShapeDtypeStruct((m_len, 1024), F32),
                   jax.ShapeDtypeStruct((m_len, 512), BF16), jax.ShapeDtypeStruct((m_len, 512), BF16)],
        compiler_params=_params(),
    )(mem, g_mem, w_memkv, g_mk)


def _mem_kv_bwd(mem, g_mem, w_memkv, g_mk, mn_b, kv, d_kn, d_v):
    m_len = mem.shape[0]

    def body(mem_ref, g_ref, w_ref, gk_ref, mn_ref, kv_ref, dkn_ref, dv_ref, dw_ref, dgmem_ref, dgk_ref):
        parts = []
        dgk = jnp.zeros((1, LANES), F32)
        for h in range(MEM_HEADS):
            _, xhat, r = _norm_full(kv_ref[:, 128 * h:128 * (h + 1)], gk_ref[...])
            dx, dgh = _norm_full_bwd(dkn_ref[:, 128 * h:128 * (h + 1)], gk_ref[...], xhat, r)
            parts.append(dx)
            dgk = dgk + dgh
        d_kv = jnp.concatenate(parts + [dv_ref[...]], axis=1).astype(BF16)
        dw_ref[...] = lax.dot_general(mn_ref[...], d_kv, TN_DIMS, preferred_element_type=F32)
        d_mn = lax.dot_general(d_kv, w_ref[...], NT_DIMS, preferred_element_type=F32)
        _, xhat, _ = _norm_full(mem_ref[...], g_ref[...])
        dgmem_ref[...] = jnp.sum(d_mn * xhat, axis=0, keepdims=True)
        dgk_ref[...] = dgk

    return pl.pallas_call(
        body, name="mem_kv_bwd",
        out_shape=[jax.ShapeDtypeStruct((D_MODEL, 1024), F32), jax.ShapeDtypeStruct((1, D_MODEL), F32),
                   jax.ShapeDtypeStruct((1, LANES), F32)],
        compiler_params=_params(),
    )(mem, g_mem, w_memkv, g_mk, mn_b, kv, d_kn, d_v)


def _mem_probs(q_h, k_h):
    sc = lax.dot_general(q_h, k_h, NT_DIMS, preferred_element_type=F32) * MEM_SCALE
    e = jnp.exp(sc - jnp.max(sc, axis=-1, keepdims=True))
    return e * (1.0 / jnp.sum(e, axis=-1, keepdims=True))


def _mem_attn_fwd(qm, km, vm, *, tm=512):
    s = qm.shape[0]
    tm = _tile(s, tm)
    m_len = km.shape[0]

    def body(q_ref, k_ref, v_ref, o_ref):
        for h in range(MEM_HEADS):
            sl = slice(128 * h, 128 * (h + 1))
            p = _mem_probs(q_ref[:, sl], k_ref[:, sl])
            o_ref[:, sl] = jnp.dot(p.astype(BF16), v_ref[:, sl], preferred_element_type=F32)

    kvspec = pl.BlockSpec((m_len, 512), lambda i: (0, 0))
    return pl.pallas_call(
        body, name="mem_attn_fwd", grid=(s // tm,),
        in_specs=[pl.BlockSpec((tm, 512), lambda i: (i, 0)), kvspec, kvspec],
        out_specs=pl.BlockSpec((tm, 512), lambda i: (i, 0)),
        out_shape=jax.ShapeDtypeStruct((s, 512), F32),
        compiler_params=_params(("parallel",)),
    )(qm, km, vm)


def _mem_attn_bwd(qm, km, vm, y_m, d_y, *, tm=512):
    s = qm.shape[0]
    tm = _tile(s, tm)
    m_len = km.shape[0]
    col0 = (SWA_Q_HEADS * SWA_HEAD_DIM + MLA_HEADS * MLA_V) // 512

    def body(q_ref, k_ref, v_ref, y_ref, dy_ref, dq_ref, dk_ref, dv_ref):
        @pl.when(pl.program_id(0) == 0)
        def _():
            dk_ref[...] = jnp.zeros_like(dk_ref)
            dv_ref[...] = jnp.zeros_like(dv_ref)

        for h in range(MEM_HEADS):
            sl = slice(128 * h, 128 * (h + 1))
            q_h, k_h = q_ref[:, sl], k_ref[:, sl]
            do = dy_ref[:, sl]
            do_b = do.astype(BF16)
            p = _mem_probs(q_h, k_h)
            delta = jnp.sum(do * y_ref[:, sl], axis=-1, keepdims=True)
            dv_ref[:, sl] += lax.dot_general(p.astype(BF16), do_b, TN_DIMS, preferred_element_type=F32)
            dp = lax.dot_general(do_b, v_ref[:, sl], NT_DIMS, preferred_element_type=F32)
            ds_b = (p * (dp - delta) * MEM_SCALE).astype(BF16)
            dq_ref[:, sl] = jnp.dot(ds_b, k_h, preferred_element_type=F32)
            dk_ref[:, sl] += lax.dot_general(ds_b, q_h, TN_DIMS, preferred_element_type=F32)

    kvspec = pl.BlockSpec((m_len, 512), lambda i: (0, 0))
    row = pl.BlockSpec((tm, 512), lambda i: (i, 0))
    return pl.pallas_call(
        body, name="mem_attn_bwd", grid=(s // tm,),
        in_specs=[row, kvspec, kvspec, row, pl.BlockSpec((tm, 512), lambda i: (i, col0))],
        out_specs=[row, kvspec, kvspec],
        out_shape=[jax.ShapeDtypeStruct((s, 512), F32), jax.ShapeDtypeStruct((m_len, 512), F32),
                   jax.ShapeDtypeStruct((m_len, 512), F32)],
        compiler_params=_params(("arbitrary",)),
    )(qm, km, vm, y_m, d_y)


def _ffn_gate_up(fn, w_gate, w_up, *, tm=512):
    s, d = fn.shape
    nsp, _, tf = w_gate.shape
    f = nsp * tf
    tm = _tile(s, tm)

    def body(x_ref, wg_ref, wu_ref, g_ref, u_ref, a_ref):
        x = x_ref[...]
        gate = jnp.dot(x, wg_ref[...], preferred_element_type=F32)
        up = jnp.dot(x, wu_ref[...], preferred_element_type=F32)
        g_ref[...] = gate.astype(BF16)
        u_ref[...] = up.astype(BF16)
        a_ref[...] = (gate * (1.0 / (1.0 + jnp.exp(-gate))) * up).astype(BF16)

    wspec = pl.BlockSpec((None, d, tf), lambda j, i: (j, 0, 0))
    ospec = pl.BlockSpec((tm, tf), lambda j, i: (i, j))
    osh = jax.ShapeDtypeStruct((s, f), BF16)
    return pl.pallas_call(
        body, name="ffn_gate_up", grid=(nsp, s // tm),
        in_specs=[pl.BlockSpec((tm, d), lambda j, i: (i, 0)), wspec, wspec],
        out_specs=[ospec, ospec, ospec], out_shape=[osh, osh, osh],
        compiler_params=_params(("parallel", "parallel")),
    )(fn, w_gate, w_up)


def _ffn_bwd_act(d_out, w_down, gate, up, *, tm=512, tf=1408):
    s, d = d_out.shape
    f = w_down.shape[0]
    tm, tf = _tile(s, tm), _tile(f, tf)

    def body(do_ref, wd_ref, g_ref, u_ref, dg_ref, du_ref):
        d_act = lax.dot_general(do_ref[...].astype(BF16), wd_ref[...], NT_DIMS, preferred_element_type=F32)
        gate = g_ref[...].astype(F32)
        sig = 1.0 / (1.0 + jnp.exp(-gate))
        du_ref[...] = (d_act * (gate * sig)).astype(BF16)
        dg_ref[...] = (d_act * u_ref[...].astype(F32) * (sig * (1.0 + gate * (1.0 - sig)))).astype(BF16)

    ospec = pl.BlockSpec((tm, tf), lambda j, i: (i, j))
    osh = jax.ShapeDtypeStruct((s, f), BF16)
    return pl.pallas_call(
        body, name="ffn_bwd_act", grid=(f // tf, s // tm),
        in_specs=[pl.BlockSpec((tm, d), lambda j, i: (i, 0)), pl.BlockSpec((tf, d), lambda j, i: (j, 0)), ospec, ospec],
        out_specs=[ospec, ospec], out_shape=[osh, osh],
        compiler_params=_params(("parallel", "parallel")),
    )(d_out, w_down, gate, up)


def _loss_head(out, target, *, tm=512):
    s, d = out.shape
    tm = _tile(s, tm)

    def body(o_ref, t_ref, d_ref, l_ref):
        err = o_ref[...] - t_ref[...]
        d_ref[...] = err * (1.0 / d)
        part = 0.5 * jnp.sum(jnp.mean(err * err, axis=-1, keepdims=True), axis=0, keepdims=True)
        part = jnp.broadcast_to(part, (1, LANES))

        @pl.when(pl.program_id(0) == 0)
        def _():
            l_ref[...] = part

        @pl.when(pl.program_id(0) > 0)
        def _():
            l_ref[...] += part

    row = pl.BlockSpec((tm, d), lambda i: (i, 0))
    return pl.pallas_call(
        body, name="loss_head", grid=(s // tm,),
        in_specs=[row, row], out_specs=[row, pl.BlockSpec((1, LANES), lambda i: (0, 0))],
        out_shape=[jax.ShapeDtypeStruct((s, d), F32), jax.ShapeDtypeStruct((1, LANES), F32)],
        compiler_params=_params(("arbitrary",)),
    )(out, target)


def _cols(g4):
    return jnp.concatenate([g4[k] for k in range(N_CHIPS)], axis=1)


def _full_w_in(g4):
    nat = _cols(g4)
    pad = jnp.zeros((nat.shape[0], IN_PAD - IN_WIDTH), nat.dtype)
    return jnp.concatenate([nat[:, :2304], nat[:, 2368:], nat[:, 2304:2368], pad], axis=1)


def _shards_w_in(dwp):
    nat = jnp.concatenate([dwp[:, :2304], dwp[:, C_KR:C_KR + 64], dwp[:, 2304:C_KR]], axis=1)
    per = IN_WIDTH // N_CHIPS
    return jnp.stack([nat[:, per * k:per * (k + 1)] for k in range(N_CHIPS)])


def _full_heads(g4, first):
    return jnp.concatenate([g4[k][:, :first] for k in range(N_CHIPS)] + [g4[k][:, first:] for k in range(N_CHIPS)], axis=1)


def _shards_heads(dwp, first, rest):
    base = N_CHIPS * first
    return jnp.stack([jnp.concatenate([dwp[:, first * k:first * (k + 1)], dwp[:, base + rest * k:base + rest * (k + 1)]], axis=1)
                      for k in range(N_CHIPS)])


def _rope_tables(pos):
    inv_freq = ROPE_THETA ** (-jnp.arange(0, MLA_ROPE, 2, dtype=F32) / MLA_ROPE)
    ang = pos.astype(F32)[:, None] * inv_freq
    cos, sin = jnp.cos(ang), jnp.sin(ang)
    return jnp.tile(cos, (1, 4)), jnp.concatenate([-sin, sin, -sin, sin], axis=1)


def _gain_table(sp):
    two = lambda v: jnp.tile(v, (1, 2))
    rows = [two(sp["swa_q_norm_g"]), two(sp["swa_k_norm_g"]), sp["mla_qn_norm_g"], two(sp["mla_qr_norm_g"]),
            sp["mla_kn_norm_g"], two(sp["mla_kr_norm_g"]), sp["mem_q_norm_g"], jnp.zeros((1, LANES), F32)]
    return jnp.concatenate(rows, axis=0)


def _device_step(x, mem, pos, target, sp, w):
    s = x.shape[0]
    cos_t, sin_t = _rope_tables(pos)
    pos_f = pos.astype(F32)
    pos_col, pos_row = pos_f.reshape(s, 1), pos_f.reshape(1, s)
    g128 = _gain_table(sp)
    sinks = sp["swa_sinks"].reshape(SWA_Q_HEADS)

    hn = _rms_fwd(x, sp["attn_norm_g"], name="attn_norm_fwd")
    proj = _matmul(hn, w["w_in"], name="in_proj")
    qa, ka, va, q_cat, k_cat, v_b, qm = _attn_prep_fwd(
        proj, g128, sp["mla_cq_norm_g"], sp["mla_ckv_norm_g"], w["w_uq"], w["w_ukv"], cos_t, sin_t)
    mn_b, kv_m, km, vm = _mem_kv_fwd(mem, sp["mem_norm_g"], w["w_mem_kv"], sp["mem_k_norm_g"])
    y_a = _swa_fwd(qa, ka, va, pos_col, pos_row, sinks)
    y_b, lse = _mla_fwd(q_cat, k_cat, v_b)
    y_m = _mem_attn_fwd(qm, km, vm)
    y = jnp.concatenate([y_a, y_b, y_m], axis=1)
    h1 = _matmul(y, w["w_out"], add=x, name="out_proj")
    fn = _rms_fwd(h1, sp["ffn_norm_g"], name="ffn_norm_fwd")
    gate, up, act = _ffn_gate_up(fn, w["w_gate"], w["w_up"])
    out = _matmul(act, w["w_down"], add=h1, name="down_proj", tk=1408)
    d_out, loss_tile = _loss_head(out, target)

    gw, gs = {}, {}
    gw["w_down"] = _matmul(act, d_out, ta=True, name="dw_down", tm=1408, tn=1024, tk=1024)
    d_gate, d_up = _ffn_bwd_act(d_out, w["w_down"], gate, up)
    gw["w_gate"] = _matmul(fn, d_gate, ta=True, name="dw_gate", tk=1024, tn=D_FF // N_CHIPS, out_split=N_CHIPS)
    gw["w_up"] = _matmul(fn, d_up, ta=True, name="dw_up", tk=1024, tn=D_FF // N_CHIPS, out_split=N_CHIPS)
    d_fn = _matmul(d_gate, w["w_gate"], tb=True, b_split=True, name="dfn_gate")
    d_fn = _matmul(d_up, w["w_up"], tb=True, b_split=True, add=d_fn, name="dfn_up")
    d_h1, gs["ffn_norm_g"] = _rms_bwd(d_fn, h1, sp["ffn_norm_g"], d_out, name="ffn_norm_bwd")
    gw["w_out"] = _matmul(y, d_h1, ta=True, name="dw_out", tk=1024)
    d_y = _matmul(d_h1, w["w_out"], tb=True, name="dy")
    d_qa, d_ka, d_va, d_sink = _swa_bwd(qa, ka, va, pos_col, pos_row, sinks, y_a, d_y)
    d_qcat, d_kcat, d_vb = _mla_bwd(q_cat, k_cat, v_b, y_b, lse, d_y)
    d_qm, d_km, d_vm = _mem_attn_bwd(qm, km, vm, y_m, d_y)
    d_proj, gw["w_uq"], gw["w_ukv"], dg128, gs["mla_cq_norm_g"], gs["mla_ckv_norm_g"] = _attn_prep_bwd(
        proj, g128, sp["mla_cq_norm_g"], sp["mla_ckv_norm_g"], w["w_uq"], w["w_ukv"], cos_t, sin_t,
        d_qa, d_ka, d_va, d_qcat, d_kcat, d_vb, d_qm)
    gw["w_mem_kv"], gs["mem_norm_g"], gs["mem_k_norm_g"] = _mem_kv_bwd(
        mem, sp["mem_norm_g"], w["w_mem_kv"], sp["mem_k_norm_g"], mn_b, kv_m, d_km, d_vm)
    gw["w_in"] = _matmul(hn, d_proj, ta=True, name="dw_in", tk=1024)
    d_hn = _matmul(d_proj, w["w_in"], tb=True, name="dhn", tk=1536)
    grad_x, gs["attn_norm_g"] = _rms_bwd(d_hn, x, sp["attn_norm_g"], d_h1, name="attn_norm_bwd")

    fold = lambda r: r[:, :64] + r[:, 64:]
    gs["swa_q_norm_g"] = fold(dg128[G_SWA_Q:G_SWA_Q + 1])
    gs["swa_k_norm_g"] = fold(dg128[G_SWA_K:G_SWA_K + 1])
    gs["mla_qn_norm_g"] = dg128[G_QN:G_QN + 1]
    gs["mla_qr_norm_g"] = fold(dg128[G_QR:G_QR + 1])
    gs["mla_kn_norm_g"] = dg128[G_KN:G_KN + 1]
    gs["mla_kr_norm_g"] = fold(dg128[G_KR:G_KR + 1])
    gs["mem_q_norm_g"] = dg128[G_MQ:G_MQ + 1]
    gs["swa_sinks"] = d_sink[:, :SWA_Q_HEADS]
    return loss_tile, grad_x, gw, gs


CHIP_DISTANCES = (1, 2, 3)
ANY = pl.BlockSpec(memory_space=pl.ANY)


def _place():
    x, y, c = lax.axis_index("x"), lax.axis_index("y"), lax.axis_index("c")
    return x, y, c, 2 * x + y


def _chip_at(x, y, d):
    px = 1 - x if d & 2 else x
    py = 1 - y if d & 1 else y
    return px, py, 2 * px + py


def _row_tile(rows, want=512, mult=8):
    t = min(rows, want)
    t -= t % mult
    while rows % t:
        t -= mult
    return t


def _cast_into_slot(w, meta, *, name):
    rows, cols = w.shape
    tr = _row_tile(rows, 512, 16)

    def body(meta_ref, w_ref, o_ref):
        o_ref[...] = w_ref[...].astype(BF16)

    grid_spec = pltpu.PrefetchScalarGridSpec(
        num_scalar_prefetch=1, grid=(rows // tr,),
        in_specs=[pl.BlockSpec((tr, cols), lambda i, m: (i, 0))],
        out_specs=pl.BlockSpec((None, tr, cols), lambda i, m: (m[0], i, 0)))
    return pl.pallas_call(
        body, name=name, grid_spec=grid_spec,
        out_shape=jax.ShapeDtypeStruct((N_CHIPS, rows, cols), BF16),
        compiler_params=_params(("parallel",)),
    )(meta, w)


def _gather_weights(slots):
    n = len(slots)

    def body(*refs):
        outs = refs[n:2 * n]
        ssem, rsem, fssem, frsem = refs[2 * n:]
        x, y, c, k_me = _place()
        halves = [r.shape[1] // 2 for r in outs]

        def slab(w, k, cc):
            return outs[w].at[k, pl.ds(cc * halves[w], halves[w])]

        def to_chip(w, d):
            px, py, _ = _chip_at(x, y, d)
            return pltpu.make_async_remote_copy(
                src_ref=slab(w, k_me, c), dst_ref=slab(w, k_me, c),
                send_sem=ssem.at[3 * w + d - 1], recv_sem=rsem.at[3 * w + d - 1],
                device_id=(px, py, c), device_id_type=MESH)

        def from_chip(w, d):
            _, _, k_src = _chip_at(x, y, d)
            return pltpu.make_async_remote_copy(
                src_ref=slab(w, k_src, c), dst_ref=slab(w, k_src, c),
                send_sem=ssem.at[3 * w + d - 1], recv_sem=rsem.at[3 * w + d - 1],
                device_id=(x, y, c), device_id_type=MESH)

        def to_sibling(w, d):
            _, _, k_src = _chip_at(x, y, d)
            return pltpu.make_async_remote_copy(
                src_ref=slab(w, k_src, c), dst_ref=slab(w, k_src, c),
                send_sem=fssem.at[3 * w + d - 1], recv_sem=frsem.at[3 * w + d - 1],
                device_id=(x, y, 1 - c), device_id_type=MESH)

        def from_sibling(w, d):
            _, _, k_src = _chip_at(x, y, d)
            return pltpu.make_async_remote_copy(
                src_ref=slab(w, k_src, 1 - c), dst_ref=slab(w, k_src, 1 - c),
                send_sem=fssem.at[3 * w + d - 1], recv_sem=frsem.at[3 * w + d - 1],
                device_id=(x, y, c), device_id_type=MESH)

        pairs = [(w, d) for w in range(n) for d in CHIP_DISTANCES]
        for w, d in pairs:
            to_chip(w, d).start()
        for w, d in pairs:
            from_chip(w, d).wait_recv()
            to_sibling(w, d).start()
        for w, d in pairs:
            from_sibling(w, d).wait_recv()
        for w, d in pairs:
            to_chip(w, d).wait_send()
            to_sibling(w, d).wait_send()

    sem = pltpu.SemaphoreType.DMA
    return pl.pallas_call(
        body, name="gather_weights",
        in_specs=[ANY] * n, out_specs=[ANY] * n,
        out_shape=[jax.ShapeDtypeStruct(s.shape, s.dtype) for s in slots],
        input_output_aliases={w: w for w in range(n)},
        scratch_shapes=[sem((3 * n,)), sem((3 * n,)), sem((3 * n,)), sem((3 * n,))],
    )(*slots)


def _exchange_halves(grads):
    n = len(grads)

    def body(*refs):
        ins, outs = refs[:n], refs[n:2 * n]
        ssem, rsem = refs[2 * n:]
        x, y, c, _ = _place()
        copies = []
        for w in range(n):
            half = ins[w].shape[1] // 2
            copies.append(pltpu.make_async_remote_copy(
                src_ref=ins[w].at[:, pl.ds((1 - c) * half, half)], dst_ref=outs[w],
                send_sem=ssem.at[w], recv_sem=rsem.at[w], device_id=(x, y, 1 - c), device_id_type=MESH))
        for cp in copies:
            cp.start()
        for cp in copies:
            cp.wait()

    sem = pltpu.SemaphoreType.DMA
    return pl.pallas_call(
        body, name="grad_exchange_halves",
        in_specs=[ANY] * n, out_specs=[ANY] * n,
        out_shape=[jax.ShapeDtypeStruct((N_CHIPS, g.shape[1] // 2, g.shape[2]), g.dtype) for g in grads],
        scratch_shapes=[sem((n,)), sem((n,))],
    )(*grads)


def _exchange_chips(parts):
    n = len(parts)

    def body(*refs):
        ins, outs = refs[:n], refs[n:2 * n]
        ssem, rsem = refs[2 * n:]
        x, y, c, _ = _place()
        copies = []
        for w in range(n):
            for d in CHIP_DISTANCES:
                px, py, _ = _chip_at(x, y, d)
                copies.append(pltpu.make_async_remote_copy(
                    src_ref=ins[w].at[d - 1], dst_ref=outs[w].at[d - 1],
                    send_sem=ssem.at[3 * w + d - 1], recv_sem=rsem.at[3 * w + d - 1],
                    device_id=(px, py, c), device_id_type=MESH))
        for cp in copies:
            cp.start()
        for cp in copies:
            cp.wait()

    sem = pltpu.SemaphoreType.DMA
    return pl.pallas_call(
        body, name="grad_exchange_chips",
        in_specs=[ANY] * n, out_specs=[ANY] * n,
        out_shape=[jax.ShapeDtypeStruct(p.shape, p.dtype) for p in parts],
        scratch_shapes=[sem((3 * n,)), sem((3 * n,))],
    )(*parts)


def _swap_halves(totals):
    n = len(totals)

    def body(*refs):
        ins, outs = refs[:n], refs[n:2 * n]
        ssem, rsem = refs[2 * n:]
        x, y, c, _ = _place()
        copies = [pltpu.make_async_remote_copy(
            src_ref=ins[w], dst_ref=outs[w], send_sem=ssem.at[w], recv_sem=rsem.at[w],
            device_id=(x, y, 1 - c), device_id_type=MESH) for w in range(n)]
        for cp in copies:
            cp.start()
        for cp in copies:
            cp.wait()

    sem = pltpu.SemaphoreType.DMA
    return pl.pallas_call(
        body, name="grad_swap_halves",
        in_specs=[ANY] * n, out_specs=[ANY] * n,
        out_shape=[jax.ShapeDtypeStruct(t.shape, t.dtype) for t in totals],
        scratch_shapes=[sem((n,)), sem((n,))],
    )(*totals)


def _add_pair(meta, g4, recv, *, name):
    nsh, rows, cols = g4.shape
    half = rows // 2
    tr = _row_tile(half, 128 if cols > 1024 else 256, 16)
    nt = half // tr

    def body(meta_ref, g0, g1, g2, g3, r0, r1, r2, r3, own_ref, oth_ref):
        own_ref[...] = g0[...] + r0[...]
        for d, (g, r) in enumerate(((g1, r1), (g2, r2), (g3, r3))):
            oth_ref[d] = (g[...] + r[...]).astype(BF16)

    blk = (None, tr, cols)
    gspec = lambda d: pl.BlockSpec(blk, lambda i, m: (jnp.bitwise_xor(m[0], d), m[1] * nt + i, 0))
    rspec = lambda d: pl.BlockSpec(blk, lambda i, m: (jnp.bitwise_xor(m[0], d), i, 0))
    grid_spec = pltpu.PrefetchScalarGridSpec(
        num_scalar_prefetch=1, grid=(nt,),
        in_specs=[gspec(d) for d in range(nsh)] + [rspec(d) for d in range(nsh)],
        out_specs=[pl.BlockSpec((tr, cols), lambda i, m: (i, 0)), pl.BlockSpec((3, tr, cols), lambda i, m: (0, i, 0))])
    return pl.pallas_call(
        body, name=name, grid_spec=grid_spec,
        out_shape=[jax.ShapeDtypeStruct((half, cols), F32), jax.ShapeDtypeStruct((3, half, cols), BF16)],
        compiler_params=_params(("parallel",)),
    )(meta, g4, g4, g4, g4, recv, recv, recv, recv)


def _add_chips(own, recv, *, name):
    half, cols = own.shape
    tr = _row_tile(half, 256, 16)

    def body(p_ref, r_ref, o_ref):
        o_ref[...] = ((p_ref[...] + r_ref[0].astype(F32)) + r_ref[1].astype(F32)) + r_ref[2].astype(F32)

    return pl.pallas_call(
        body, name=name, grid=(half // tr,),
        in_specs=[pl.BlockSpec((tr, cols), lambda i: (i, 0)), pl.BlockSpec((3, tr, cols), lambda i: (0, i, 0))],
        out_specs=pl.BlockSpec((tr, cols), lambda i: (i, 0)),
        out_shape=jax.ShapeDtypeStruct((half, cols), F32),
        compiler_params=_params(("parallel",)),
    )(own, recv)


def _reduce_scatter(meta, grads):
    from_sibling = _exchange_halves(grads)
    sums = [_add_pair(meta, g, r, name=f"grad_add_pair_{i}") for i, (g, r) in enumerate(zip(grads, from_sibling))]
    from_chips = _exchange_chips([others for _, others in sums])
    mine = [_add_chips(own, r, name=f"grad_add_chips_{i}") for i, ((own, _), r) in enumerate(zip(sums, from_chips))]
    return mine, _swap_halves(mine)


def _adamw_math(w, g, m, v):
    m = ADAM_B1 * m + (1.0 - ADAM_B1) * g
    v = ADAM_B2 * v + (1.0 - ADAM_B2) * (g * g)
    m_hat = m / (1.0 - ADAM_B1 ** ADAM_STEP)
    v_hat = v / (1.0 - ADAM_B2 ** ADAM_STEP)
    delta = -ADAM_LR * (m_hat / (jnp.sqrt(v_hat) + ADAM_EPS) + ADAM_WD * w)
    return delta, m, v


def _adamw(meta, w, g_mine, g_theirs, m, v, *, name):
    rows, cols = w.shape
    half = rows // 2
    tr = _row_tile(half, 256)
    nt = half // tr

    def body(meta_ref, w_ref, a_ref, b_ref, m_ref, v_ref, g_ref, d_ref, mo_ref, vo_ref):
        is_mine = (pl.program_id(0) // nt) == meta_ref[1]
        g = jnp.where(is_mine, a_ref[...], b_ref[...])
        g_ref[...] = g
        d_ref[...], mo_ref[...], vo_ref[...] = _adamw_math(w_ref[...], g, m_ref[...], v_ref[...])

    blk = pl.BlockSpec((tr, cols), lambda i, mt: (i, 0))
    mine = pl.BlockSpec((tr, cols), lambda i, mt: (jnp.where(i // nt == mt[1], i % nt, 0), 0))
    theirs = pl.BlockSpec((tr, cols), lambda i, mt: (jnp.where(i // nt == mt[1], 0, i % nt), 0))
    sh = jax.ShapeDtypeStruct((rows, cols), F32)
    grid_spec = pltpu.PrefetchScalarGridSpec(
        num_scalar_prefetch=1, grid=(rows // tr,),
        in_specs=[blk, mine, theirs, blk, blk], out_specs=[blk] * 4)
    return pl.pallas_call(
        body, name=name, grid_spec=grid_spec, out_shape=[sh] * 4,
        compiler_params=_params(("arbitrary",)),
    )(meta, w, g_mine, g_theirs, m, v)


N_DEVICES = 8


def _small_step(g_pack, w_pack, m_pack, v_pack):
    rows = g_pack.shape[0]

    def body(g_ref, w_ref, m_ref, v_ref, sum_ref, d_ref, mo_ref, vo_ref, slots, ssem, rsem):
        x, y, c, _ = _place()
        me = 4 * x + 2 * y + c
        slots[me] = g_ref[...]
        copies = []
        for r in range(1, N_DEVICES):
            px = 1 - x if r & 4 else x
            py = 1 - y if r & 2 else y
            pc = 1 - c if r & 1 else c
            copies.append(pltpu.make_async_remote_copy(
                src_ref=g_ref, dst_ref=slots.at[me], send_sem=ssem.at[r - 1], recv_sem=rsem.at[r - 1],
                device_id=(px, py, pc), device_id_type=MESH))
        for cp in copies:
            cp.start()
        for r in range(1, N_DEVICES):
            src = jnp.bitwise_xor(me, r)
            pltpu.make_async_remote_copy(
                src_ref=g_ref, dst_ref=slots.at[src], send_sem=ssem.at[r - 1], recv_sem=rsem.at[r - 1],
                device_id=(x, y, c), device_id_type=MESH).wait_recv()
        for cp in copies:
            cp.wait_send()
        total = slots[0]
        for k in range(1, N_DEVICES):
            total = total + slots[k]
        sum_ref[...] = total
        d_ref[...], mo_ref[...], vo_ref[...] = _adamw_math(w_ref[...], total, m_ref[...], v_ref[...])

    sh = jax.ShapeDtypeStruct((rows, LANES), F32)
    vm = pl.BlockSpec(memory_space=pltpu.VMEM)
    return pl.pallas_call(
        body, name="small_allreduce_adamw",
        in_specs=[vm] * 4, out_specs=[vm] * 4, out_shape=[sh] * 4,
        scratch_shapes=[pltpu.VMEM((N_DEVICES, rows, LANES), F32),
                        pltpu.SemaphoreType.DMA((N_DEVICES - 1,)), pltpu.SemaphoreType.DMA((N_DEVICES - 1,))],
    )(g_pack, w_pack, m_pack, v_pack)


WEIGHTS = ("attn_norm_g", "w_in", "swa_q_norm_g", "swa_k_norm_g", "swa_sinks", "mla_cq_norm_g", "mla_ckv_norm_g",
           "w_uq", "w_ukv", "mla_qn_norm_g", "mla_qr_norm_g", "mla_kn_norm_g", "mla_kr_norm_g", "mem_norm_g",
           "w_mem_kv", "mem_q_norm_g", "mem_k_norm_g", "w_out", "ffn_norm_g", "w_gate", "w_up", "w_down")
BIG = ("w_in", "w_uq", "w_ukv", "w_mem_kv", "w_out", "w_gate", "w_up", "w_down")
SMALL = tuple(n for n in WEIGHTS if n not in BIG)
PACK_UNIT = 8 * LANES


def _pack(parts):
    out = []
    for p in parts:
        n = p.shape[1]
        padded = -(-n // PACK_UNIT) * PACK_UNIT
        out.append(jnp.pad(p, ((0, 0), (0, padded - n))).reshape(padded // LANES, LANES))
    return jnp.concatenate(out, axis=0)


def _unpack(buf, sizes):
    out, row = [], 0
    for n in sizes:
        rows = -(-n // PACK_UNIT) * 8
        out.append(buf[row:row + rows].reshape(1, rows * LANES)[:, :n])
        row += rows
    return out


def kernel(x, mem, positions, attn_norm_g, w_in, swa_q_norm_g, swa_k_norm_g, swa_sinks, mla_cq_norm_g, mla_ckv_norm_g, w_uq, w_ukv, mla_qn_norm_g, mla_qr_norm_g, mla_kn_norm_g, mla_kr_norm_g, mem_norm_g, w_mem_kv, mem_q_norm_g, mem_k_norm_g, w_out, ffn_norm_g, w_gate, w_up, w_down, loss_target, m_attn_norm_g, m_w_in, m_swa_q_norm_g, m_swa_k_norm_g, m_swa_sinks, m_mla_cq_norm_g, m_mla_ckv_norm_g, m_w_uq, m_w_ukv, m_mla_qn_norm_g, m_mla_qr_norm_g, m_mla_kn_norm_g, m_mla_kr_norm_g, m_mem_norm_g, m_w_mem_kv, m_mem_q_norm_g, m_mem_k_norm_g, m_w_out, m_ffn_norm_g, m_w_gate, m_w_up, m_w_down, v_attn_norm_g, v_w_in, v_swa_q_norm_g, v_swa_k_norm_g, v_swa_sinks, v_mla_cq_norm_g, v_mla_ckv_norm_g, v_w_uq, v_w_ukv, v_mla_qn_norm_g, v_mla_qr_norm_g, v_mla_kn_norm_g, v_mla_kr_norm_g, v_mem_norm_g, v_w_mem_kv, v_mem_q_norm_g, v_mem_k_norm_g, v_w_out, v_ffn_norm_g, v_w_gate, v_w_up, v_w_down):
    given = dict(locals())
    wts = {n: given[n] for n in WEIGHTS}
    mom_m = {n: given["m_" + n] for n in WEIGHTS}
    mom_v = {n: given["v_" + n] for n in WEIGHTS}

    mx, my, mc = lax.axis_index("x"), lax.axis_index("y"), lax.axis_index("c")
    meta = jnp.stack([2 * mx + my, mc]).astype(jnp.int32)

    gathered = dict(zip(BIG, _gather_weights([_cast_into_slot(wts[n][0], meta, name="cast_" + n) for n in BIG])))
    full = {
        "w_in": _full_w_in(gathered["w_in"]),
        "w_uq": _full_heads(gathered["w_uq"], MLA_NOPE),
        "w_ukv": _full_heads(gathered["w_ukv"], MLA_NOPE),
        "w_mem_kv": gathered["w_mem_kv"].reshape(D_MODEL, 2 * MEM_HEADS * MEM_DIM),
        "w_out": gathered["w_out"].reshape(D_MODEL, D_MODEL),
        "w_gate": gathered["w_gate"],
        "w_up": gathered["w_up"],
        "w_down": gathered["w_down"].reshape(D_FF, D_MODEL),
    }
    sp = {n: wts[n] for n in SMALL}
    loss_tile, grad_x, gw, gs = _device_step(x[0], mem[0], positions[0], loss_target[0], sp, full)

    chip_major = {
        "w_in": _shards_w_in(gw["w_in"]),
        "w_uq": _shards_heads(gw["w_uq"], MLA_NOPE, MLA_ROPE),
        "w_ukv": _shards_heads(gw["w_ukv"], MLA_NOPE, MLA_V),
        "w_mem_kv": gw["w_mem_kv"].reshape(N_CHIPS, D_MODEL // N_CHIPS, -1),
        "w_out": gw["w_out"].reshape(N_CHIPS, D_MODEL // N_CHIPS, -1),
        "w_gate": gw["w_gate"],
        "w_up": gw["w_up"],
        "w_down": gw["w_down"].reshape(N_CHIPS, D_FF // N_CHIPS, -1),
    }
    mine, theirs = _reduce_scatter(meta, [chip_major[n] for n in BIG])

    grad, delta, new_m, new_v = {}, {}, {}, {}
    for n, ga, gb in zip(BIG, mine, theirs):
        g2, d, m2, v2 = _adamw(meta, wts[n][0], ga, gb, mom_m[n][0], mom_v[n][0], name="adamw_" + n)
        grad[n], delta[n], new_m[n], new_v[n] = g2[None], d[None], m2[None], v2[None]

    sizes = [wts[n].shape[1] for n in SMALL]
    zero = jnp.zeros((1, LANES), F32)
    packs = _small_step(_pack([gs[n] for n in SMALL] + [loss_tile]), _pack([wts[n] for n in SMALL] + [zero]),
                        _pack([mom_m[n] for n in SMALL] + [zero]), _pack([mom_v[n] for n in SMALL] + [zero]))
    for store, buf in zip((grad, delta, new_m, new_v), packs):
        for n, val in zip(SMALL, _unpack(buf, sizes)):
            store[n] = val
    loss = _unpack(packs[0], sizes + [LANES])[-1][0, 0]

    return (loss, grad_x[None], *[grad[n] for n in WEIGHTS], *[delta[n] for n in WEIGHTS],
            *[new_m[n] for n in WEIGHTS], *[new_v[n] for n in WEIGHTS])
```

```python
import functools
import math

import jax
import jax.numpy as jnp
from jax import lax
from jax.experimental import pallas as pl
from jax.experimental.pallas import tpu as pltpu

F32 = jnp.float32
BF16 = jnp.bfloat16

D_MODEL = 2048
BLOCK = 128
EPS = 1e-6
NEG_INF = -1e30
SWA_Q_HEADS = 16
SWA_KV_HEADS = 2
SWA_HEAD_DIM = 64
MLA_HEADS = 4
MLA_RANK = 512
MLA_NOPE = 128
MLA_ROPE = 64
MLA_V = 128
ROPE_THETA = 10000.0
MEM_HEADS = 4
MEM_DIM = 128
D_FF = 5632
IN_WIDTH = 2880
IN_PAD = 3072
N_CHIPS = 4

ADAM_LR = 0.001
ADAM_B1 = 0.9
ADAM_B2 = 0.999
ADAM_EPS = 1e-08
ADAM_WD = 0.01
ADAM_STEP = 10

VMEM_LIMIT_BYTES = 56 * 1024 * 1024
LANES = 128

MESH = pl.DeviceIdType.MESH


def _params(sem=None, **kw):
    return pltpu.CompilerParams(dimension_semantics=sem, vmem_limit_bytes=VMEM_LIMIT_BYTES, **kw)


def _tile(n, want):
    if n <= want:
        return n
    t = want - want % LANES
    while t > 0:
        if n % t == 0:
            return t
        t -= LANES
    return n


ANY = pl.BlockSpec(memory_space=pl.ANY)


class _Stage:
    def __init__(self, ins, out_shapes, aliases, n_sem, issue, wait):
        self.ins, self.out_shapes, self.aliases, self.n_sem = list(ins), list(out_shapes), dict(aliases), n_sem
        self.issue, self.wait = issue, wait


def _pcall(body, args, *, name, grid, in_specs, out_specs, out_shape, scratch_shapes=(), sem=None, comm=()):
    multi = isinstance(out_shape, (list, tuple))
    out_specs_l = list(out_specs) if multi else [out_specs]
    out_shape_l = list(out_shape) if multi else [out_shape]
    if not comm:
        return pl.pallas_call(body, name=name, grid=grid, in_specs=list(in_specs), out_specs=out_specs,
                              out_shape=out_shape, scratch_shapes=list(scratch_shapes),
                              compiler_params=_params(sem))(*args)
    n_in, n_out, n_scr = len(in_specs), len(out_specs_l), len(scratch_shapes)
    cins = [a for st in comm for a in st.ins]
    couts = [s for st in comm for s in st.out_shapes]
    aliases, ci, co = {}, 0, 0
    for st in comm:
        for a_i, o_i in st.aliases.items():
            aliases[n_in + ci + a_i] = n_out + co + o_i
        ci, co = ci + len(st.ins), co + len(st.out_shapes)

    def wrapped(*refs):
        p = 0
        ins = refs[p:p + n_in]; p += n_in
        cin_refs = refs[p:p + len(cins)]; p += len(cins)
        outs = refs[p:p + n_out]; p += n_out
        cout_refs = refs[p:p + len(couts)]; p += len(couts)
        scr = refs[p:p + n_scr]; p += n_scr
        sems = refs[p:]
        first = functools.reduce(jnp.logical_and, [pl.program_id(a) == 0 for a in range(len(grid))])
        last = functools.reduce(jnp.logical_and, [pl.program_id(a) == grid[a] - 1 for a in range(len(grid))])

        def each(what):
            i, o = 0, 0
            for k, st in enumerate(comm):
                fn = st.issue if what == "issue" else st.wait
                fn(cin_refs[i:i + len(st.ins)], cout_refs[o:o + len(st.out_shapes)], sems[2 * k], sems[2 * k + 1])
                i, o = i + len(st.ins), o + len(st.out_shapes)

        @pl.when(first)
        def _():
            each("issue")

        body(*ins, *outs, *scr)

        @pl.when(last)
        def _():
            each("wait")

    sem_scr = [pltpu.SemaphoreType.DMA((st.n_sem,)) for st in comm for _ in range(2)]
    res = pl.pallas_call(
        wrapped, name=name, grid=grid,
        in_specs=list(in_specs) + [ANY] * len(cins), out_specs=out_specs_l + [ANY] * len(couts),
        out_shape=out_shape_l + couts, scratch_shapes=list(scratch_shapes) + sem_scr,
        input_output_aliases=aliases,
        compiler_params=_params(("arbitrary",) * len(grid)),
    )(*args, *cins)
    normal = list(res[:n_out])
    stage_outs, o = [], n_out
    for st in comm:
        stage_outs.append(list(res[o:o + len(st.out_shapes)]))
        o += len(st.out_shapes)
    return (normal if multi else normal[0]), stage_outs


def _matmul(a, b, *, name, ta=False, tb=False, add=None, out_dtype=F32, tm=1024, tn=1024, tk=2048,
            b_split=False, out_split=0, comm=()):
    if ta:
        kdim, m = a.shape
    else:
        m, kdim = a.shape
    if b_split:
        assert tb
        nsp, n, kb = b.shape
        kb = kb * nsp
    elif tb:
        n, kb = b.shape
    else:
        kb, n = b.shape
    assert kb == kdim, (a.shape, b.shape, ta, tb)
    if b_split:
        tk = kdim // nsp
    if out_split:
        tn = _tile(n // out_split, tn)
    tm, tn, tk = _tile(m, tm), _tile(n, tn), _tile(kdim, tk)
    nk = kdim // tk
    dims = (((0 if ta else 1,), (1 if tb else 0,)), ((), ()))

    def body(*refs):
        if add is None:
            a_ref, b_ref, o_ref, acc_ref = refs
            add_ref = None
        else:
            a_ref, b_ref, add_ref, o_ref, acc_ref = refs
        k = pl.program_id(2)
        part = lax.dot_general(a_ref[...].astype(BF16), b_ref[...].astype(BF16), dims,
                               preferred_element_type=F32)

        @pl.when(k == 0)
        def _():
            acc_ref[...] = part

        @pl.when(k > 0)
        def _():
            acc_ref[...] += part

        @pl.when(k == nk - 1)
        def _():
            r = acc_ref[...]
            if add_ref is not None:
                r = r + add_ref[...].astype(F32)
            o_ref[...] = r.astype(o_ref.dtype)

    a_spec = pl.BlockSpec((tk, tm), lambda i, j, k: (k, i)) if ta else pl.BlockSpec((tm, tk), lambda i, j, k: (i, k))
    if b_split:
        b_spec = pl.BlockSpec((None, tn, tk), lambda i, j, k: (k, j, 0))
    elif tb:
        b_spec = pl.BlockSpec((tn, tk), lambda i, j, k: (j, k))
    else:
        b_spec = pl.BlockSpec((tk, tn), lambda i, j, k: (k, j))
    in_specs = [a_spec, b_spec]
    args = [a, b]
    if add is not None:
        in_specs.append(pl.BlockSpec((tm, tn), lambda i, j, k: (i, j)))
        args.append(add)
    if out_split:
        per = (n // out_split) // tn
        out_spec = pl.BlockSpec((None, tm, tn), lambda i, j, k: (j // per, i, j % per))
        out_shape = jax.ShapeDtypeStruct((out_split, m, n // out_split), out_dtype)
    else:
        out_spec = pl.BlockSpec((tm, tn), lambda i, j, k: (i, j))
        out_shape = jax.ShapeDtypeStruct((m, n), out_dtype)
    return _pcall(body, args, name=name, grid=(m // tm, n // tn, nk), in_specs=in_specs, out_specs=out_spec,
                  out_shape=out_shape, scratch_shapes=[pltpu.VMEM((tm, tn), F32)],
                  sem=("parallel", "parallel", "arbitrary"), comm=comm)


def _rms_fwd(x, g, *, name, tm=512, comm=()):
    s, d = x.shape
    tm = _tile(s, tm)

    def body(x_ref, g_ref, o_ref):
        xv = x_ref[...]
        r = lax.rsqrt(jnp.mean(xv * xv, axis=-1, keepdims=True) + EPS)
        o_ref[...] = (xv * r * g_ref[...]).astype(o_ref.dtype)

    return _pcall(body, (x, g), name=name, grid=(s // tm,),
                  in_specs=[pl.BlockSpec((tm, d), lambda i: (i, 0)), pl.BlockSpec((1, d), lambda i: (0, 0))],
                  out_specs=pl.BlockSpec((tm, d), lambda i: (i, 0)),
                  out_shape=jax.ShapeDtypeStruct((s, d), BF16), sem=("parallel",), comm=comm)


def _rms_bwd(dy, x, g, res, *, name, tm=512, comm=()):
    s, d = x.shape
    tm = _tile(s, tm)

    def body(dy_ref, x_ref, g_ref, res_ref, dx_ref, dg_ref):
        xv = x_ref[...]
        dyv = dy_ref[...]
        r = lax.rsqrt(jnp.mean(xv * xv, axis=-1, keepdims=True) + EPS)
        xhat = xv * r
        dyg = dyv * g_ref[...]
        mt = jnp.mean(dyg * xhat, axis=-1, keepdims=True)
        dx_ref[...] = res_ref[...] + r * (dyg - xhat * mt)
        part = jnp.sum(dyv * xhat, axis=0, keepdims=True)

        @pl.when(pl.program_id(0) == 0)
        def _():
            dg_ref[...] = part

        @pl.when(pl.program_id(0) > 0)
        def _():
            dg_ref[...] += part

    row = pl.BlockSpec((tm, d), lambda i: (i, 0))
    vec = pl.BlockSpec((1, d), lambda i: (0, 0))
    return _pcall(body, (dy, x, g, res), name=name, grid=(s // tm,), in_specs=[row, row, vec, row],
                  out_specs=[row, vec],
                  out_shape=[jax.ShapeDtypeStruct((s, d), F32), jax.ShapeDtypeStruct((1, d), F32)],
                  sem=("arbitrary",), comm=comm)


def _lane(shape):
    return lax.broadcasted_iota(jnp.int32, shape, 1)


def _halfsum(t, lo):
    s_lo = jnp.sum(jnp.where(lo, t, 0.0), axis=-1, keepdims=True)
    s_hi = jnp.sum(jnp.where(lo, 0.0, t), axis=-1, keepdims=True)
    return jnp.where(lo, s_lo, s_hi)


def _norm_pair(x, g, lo):
    r = lax.rsqrt(_halfsum(x * x, lo) * (1.0 / 64.0) + EPS)
    xhat = x * r
    return xhat * g, xhat, r


def _norm_pair_bwd(dy, g, xhat, r, lo):
    dyg = dy * g
    mt = _halfsum(dyg * xhat, lo) * (1.0 / 64.0)
    return r * (dyg - xhat * mt), jnp.sum(dy * xhat, axis=0, keepdims=True)


def _norm_full(x, g):
    r = lax.rsqrt(jnp.mean(x * x, axis=-1, keepdims=True) + EPS)
    xhat = x * r
    return xhat * g, xhat, r


def _norm_full_bwd(dy, g, xhat, r):
    dyg = dy * g
    mt = jnp.mean(dyg * xhat, axis=-1, keepdims=True)
    return r * (dyg - xhat * mt), jnp.sum(dy * xhat, axis=0, keepdims=True)


def _rot(x, first32):
    return jnp.where(first32, pltpu.roll(x, 96, axis=1), pltpu.roll(x, 32, axis=1))


def _rope(x, cos_t, sin_t, first32):
    return x * cos_t + _rot(x, first32) * sin_t


def _rope_bwd(dy, cos_t, sin_t, first32):
    return dy * cos_t + _rot(dy * sin_t, first32)


G_SWA_Q, G_SWA_K, G_QN, G_QR, G_KN, G_KR, G_MQ = range(7)

C_QA, C_KA, C_VA, C_CQ, C_CKV, C_QM, C_KR = 0, 1024, 1152, 1280, 1792, 2304, 2816


def _prep_common(p_ref, g128_ref, gcq_ref, gckv_ref, wuq_ref, wukv_ref, cos_ref, sin_ref):
    tm = p_ref.shape[0]
    lane = _lane((tm, LANES))
    lo = lane < 64
    first32 = (lane % 64) < 32
    cos_t = cos_ref[...]
    sin_t = sin_ref[...]
    g = lambda row: g128_ref[row:row + 1, :]
    out = dict(lo=lo, first32=first32, cos_t=cos_t, sin_t=sin_t, lane=lane)
    cq_n, cq_hat, cq_r = _norm_full(p_ref[:, C_CQ:C_CQ + MLA_RANK], gcq_ref[...])
    ckv_n, ckv_hat, ckv_r = _norm_full(p_ref[:, C_CKV:C_CKV + MLA_RANK], gckv_ref[...])
    cq_b = cq_n.astype(BF16)
    ckv_b = ckv_n.astype(BF16)
    q_b = jnp.dot(cq_b, wuq_ref[...], preferred_element_type=F32)
    kv_b = jnp.dot(ckv_b, wukv_ref[...], preferred_element_type=F32)
    out.update(cq_b=cq_b, cq_hat=cq_hat, cq_r=cq_r, ckv_b=ckv_b, ckv_hat=ckv_hat, ckv_r=ckv_r, q_b=q_b, kv_b=kv_b, g=g)
    return out


def _attn_prep_fwd(proj, g128, gcq, gckv, wuq, wukv, cos_t, sin_t, *, tm=512, comm=()):
    s = proj.shape[0]
    tm = _tile(s, tm)

    def body(p_ref, g128_ref, gcq_ref, gckv_ref, wuq_ref, wukv_ref, cos_ref, sin_ref,
             qa_ref, ka_ref, va_ref, qcat_ref, kcat_ref, vb_ref, qm_ref):
        c = _prep_common(p_ref, g128_ref, gcq_ref, gckv_ref, wuq_ref, wukv_ref, cos_ref, sin_ref)
        lo, first32, g = c["lo"], c["first32"], c["g"]
        for j in range(SWA_Q_HEADS // 2):
            y, _, _ = _norm_pair(p_ref[:, C_QA + 128 * j:C_QA + 128 * (j + 1)], g(G_SWA_Q), lo)
            qa_ref[:, 128 * j:128 * (j + 1)] = y.astype(BF16)
        y, _, _ = _norm_pair(p_ref[:, C_KA:C_KA + 128], g(G_SWA_K), lo)
        ka_ref[...] = y.astype(BF16)
        va_ref[...] = p_ref[:, C_VA:C_VA + 128].astype(BF16)
        kr, _, _ = _norm_pair(p_ref[:, C_KR:C_KR + 128], g(G_KR), lo)
        kr = jnp.where(lo, _rope(kr, c["cos_t"], c["sin_t"], first32), 0.0)
        krkr = (kr + pltpu.roll(kr, 64, axis=1)).astype(BF16)
        q_b, kv_b = c["q_b"], c["kv_b"]
        qr = []
        for j in range(MLA_HEADS // 2):
            y, _, _ = _norm_pair(q_b[:, 512 + 128 * j:512 + 128 * (j + 1)], g(G_QR), lo)
            qr.append(_rope(y, c["cos_t"], c["sin_t"], first32))
        for h in range(MLA_HEADS):
            qn, _, _ = _norm_full(q_b[:, 128 * h:128 * (h + 1)], g(G_QN))
            keep = lo if h % 2 == 0 else jnp.logical_not(lo)
            qcat_ref[h, :, 0:128] = qn.astype(BF16)
            qcat_ref[h, :, 128:256] = jnp.where(keep, qr[h // 2], 0.0).astype(BF16)
            kn, _, _ = _norm_full(kv_b[:, 128 * h:128 * (h + 1)], g(G_KN))
            kcat_ref[h, :, 0:128] = kn.astype(BF16)
            kcat_ref[h, :, 128:256] = krkr
        vb_ref[...] = kv_b[:, 512:1024].astype(BF16)
        for h in range(MEM_HEADS):
            y, _, _ = _norm_full(p_ref[:, C_QM + 128 * h:C_QM + 128 * (h + 1)], g(G_MQ))
            qm_ref[:, 128 * h:128 * (h + 1)] = y.astype(BF16)

    row = lambda w: pl.BlockSpec((tm, w), lambda i: (i, 0))
    full = lambda shape: pl.BlockSpec(shape, lambda i: tuple(0 for _ in shape))
    cat = pl.BlockSpec((MLA_HEADS, tm, 256), lambda i: (0, i, 0))
    return _pcall(
        body, (proj, g128, gcq, gckv, wuq, wukv, cos_t, sin_t), name="attn_prep_fwd", grid=(s // tm,),
        in_specs=[row(IN_PAD), full((8, 128)), full((1, 512)), full((1, 512)), full((512, 768)), full((512, 1024)),
                  row(128), row(128)],
        out_specs=[row(1024), row(128), row(128), cat, cat, row(512), row(512)],
        out_shape=[jax.ShapeDtypeStruct((s, 1024), BF16), jax.ShapeDtypeStruct((s, 128), BF16),
                   jax.ShapeDtypeStruct((s, 128), BF16), jax.ShapeDtypeStruct((MLA_HEADS, s, 256), BF16),
                   jax.ShapeDtypeStruct((MLA_HEADS, s, 256), BF16), jax.ShapeDtypeStruct((s, 512), BF16),
                   jax.ShapeDtypeStruct((s, 512), BF16)],
        sem=("parallel",), comm=comm)


def _attn_prep_bwd(proj, g128, gcq, gckv, wuq, wukv, cos_t, sin_t,
                   d_qa, d_ka, d_va, d_qcat, d_kcat, d_vb, d_qm, *, tm=256, comm=()):
    s = proj.shape[0]
    tm = _tile(s, tm)

    def body(p_ref, g128_ref, gcq_ref, gckv_ref, wuq_ref, wukv_ref, cos_ref, sin_ref,
             dqa_ref, dka_ref, dva_ref, dqcat_ref, dkcat_ref, dvb_ref, dqm_ref,
             dp_ref, dwuq_ref, dwukv_ref, dg128_ref, dgcq_ref, dgckv_ref):
        c = _prep_common(p_ref, g128_ref, gcq_ref, gckv_ref, wuq_ref, wukv_ref, cos_ref, sin_ref)
        lo, first32, g = c["lo"], c["first32"], c["g"]
        cos_v, sin_v = c["cos_t"], c["sin_t"]
        q_b, kv_b = c["q_b"], c["kv_b"]
        zero_row = jnp.zeros((1, LANES), F32)
        dg = {k: zero_row for k in range(7)}

        for j in range(SWA_Q_HEADS // 2):
            sl = slice(C_QA + 128 * j, C_QA + 128 * (j + 1))
            _, xhat, r = _norm_pair(p_ref[:, sl], g(G_SWA_Q), lo)
            dx, dgj = _norm_pair_bwd(dqa_ref[:, 128 * j:128 * (j + 1)], g(G_SWA_Q), xhat, r, lo)
            dp_ref[:, sl] = dx
            dg[G_SWA_Q] = dg[G_SWA_Q] + dgj
        _, xhat, r = _norm_pair(p_ref[:, C_KA:C_KA + 128], g(G_SWA_K), lo)
        dx, dgj = _norm_pair_bwd(dka_ref[...], g(G_SWA_K), xhat, r, lo)
        dp_ref[:, C_KA:C_KA + 128] = dx
        dg[G_SWA_K] = dgj
        dp_ref[:, C_VA:C_VA + 128] = dva_ref[...]

        dqb_parts = [None] * 6
        for h in range(MLA_HEADS):
            _, xhat, r = _norm_full(q_b[:, 128 * h:128 * (h + 1)], g(G_QN))
            dx, dgj = _norm_full_bwd(dqcat_ref[h, :, 0:128], g(G_QN), xhat, r)
            dqb_parts[h] = dx
            dg[G_QN] = dg[G_QN] + dgj
        for j in range(MLA_HEADS // 2):
            _, xhat, r = _norm_pair(q_b[:, 512 + 128 * j:512 + 128 * (j + 1)], g(G_QR), lo)
            d_rot = jnp.where(lo, dqcat_ref[2 * j, :, 128:256], dqcat_ref[2 * j + 1, :, 128:256])
            d_y = _rope_bwd(d_rot, cos_v, sin_v, first32)
            dx, dgj = _norm_pair_bwd(d_y, g(G_QR), xhat, r, lo)
            dqb_parts[4 + j] = dx
            dg[G_QR] = dg[G_QR] + dgj
        d_qb = jnp.concatenate(dqb_parts, axis=1).astype(BF16)
        dwuq = lax.dot_general(c["cq_b"], d_qb, (((0,), (0,)), ((), ())), preferred_element_type=F32)
        d_cqn = lax.dot_general(d_qb, wuq_ref[...], (((1,), (1,)), ((), ())), preferred_element_type=F32)
        dx, dgcq = _norm_full_bwd(d_cqn, gcq_ref[...], c["cq_hat"], c["cq_r"])
        dp_ref[:, C_CQ:C_CQ + MLA_RANK] = dx

        dkv_parts = []
        d_krkr = jnp.zeros((p_ref.shape[0], LANES), F32)
        for h in range(MLA_HEADS):
            _, xhat, r = _norm_full(kv_b[:, 128 * h:128 * (h + 1)], g(G_KN))
            dx, dgj = _norm_full_bwd(dkcat_ref[h, :, 0:128], g(G_KN), xhat, r)
            dkv_parts.append(dx)
            dg[G_KN] = dg[G_KN] + dgj
            d_krkr = d_krkr + dkcat_ref[h, :, 128:256]
        d_kvb = jnp.concatenate(dkv_parts + [dvb_ref[...]], axis=1).astype(BF16)
        dwukv = lax.dot_general(c["ckv_b"], d_kvb, (((0,), (0,)), ((), ())), preferred_element_type=F32)
        d_ckvn = lax.dot_general(d_kvb, wukv_ref[...], (((1,), (1,)), ((), ())), preferred_element_type=F32)
        dx, dgckv = _norm_full_bwd(d_ckvn, gckv_ref[...], c["ckv_hat"], c["ckv_r"])
        dp_ref[:, C_CKV:C_CKV + MLA_RANK] = dx

        _, xhat, r = _norm_pair(p_ref[:, C_KR:C_KR + 128], g(G_KR), lo)
        d_kr = jnp.where(lo, d_krkr + pltpu.roll(d_krkr, 64, axis=1), 0.0)
        d_y = jnp.where(lo, _rope_bwd(d_kr, cos_v, sin_v, first32), 0.0)
        dx, dgj = _norm_pair_bwd(d_y, g(G_KR), xhat, r, lo)
        dp_ref[:, C_KR:C_KR + 128] = jnp.where(lo, dx, 0.0)
        dp_ref[:, C_KR + 128:] = jnp.zeros((p_ref.shape[0], IN_PAD - C_KR - 128), F32)
        dg[G_KR] = dgj

        for h in range(MEM_HEADS):
            sl = slice(C_QM + 128 * h, C_QM + 128 * (h + 1))
            _, xhat, r = _norm_full(p_ref[:, sl], g(G_MQ))
            dx, dgj = _norm_full_bwd(dqm_ref[:, 128 * h:128 * (h + 1)], g(G_MQ), xhat, r)
            dp_ref[:, sl] = dx
            dg[G_MQ] = dg[G_MQ] + dgj

        dg_tile = jnp.concatenate([dg[k] for k in range(7)] + [zero_row], axis=0)

        @pl.when(pl.program_id(0) == 0)
        def _():
            dwuq_ref[...] = dwuq
            dwukv_ref[...] = dwukv
            dg128_ref[...] = dg_tile
            dgcq_ref[...] = dgcq
            dgckv_ref[...] = dgckv

        @pl.when(pl.program_id(0) > 0)
        def _():
            dwuq_ref[...] += dwuq
            dwukv_ref[...] += dwukv
            dg128_ref[...] += dg_tile
            dgcq_ref[...] += dgcq
            dgckv_ref[...] += dgckv

    row = lambda w: pl.BlockSpec((tm, w), lambda i: (i, 0))
    full = lambda shape: pl.BlockSpec(shape, lambda i: tuple(0 for _ in shape))
    cat = pl.BlockSpec((MLA_HEADS, tm, 256), lambda i: (0, i, 0))
    return _pcall(
        body, (proj, g128, gcq, gckv, wuq, wukv, cos_t, sin_t, d_qa, d_ka, d_va, d_qcat, d_kcat, d_vb, d_qm),
        name="attn_prep_bwd", grid=(s // tm,),
        in_specs=[row(IN_PAD), full((8, 128)), full((1, 512)), full((1, 512)), full((512, 768)), full((512, 1024)),
                  row(128), row(128),
                  row(1024), row(128), row(128), cat, cat, row(512), row(512)],
        out_specs=[row(IN_PAD), full((512, 768)), full((512, 1024)), full((8, 128)), full((1, 512)), full((1, 512))],
        out_shape=[jax.ShapeDtypeStruct((s, IN_PAD), F32), jax.ShapeDtypeStruct((512, 768), F32),
                   jax.ShapeDtypeStruct((512, 1024), F32), jax.ShapeDtypeStruct((8, 128), F32),
                   jax.ShapeDtypeStruct((1, 512), F32), jax.ShapeDtypeStruct((1, 512), F32)],
        sem=("arbitrary",), comm=comm)


SWA_SLOPES = tuple(2.0 ** (-8.0 * h / SWA_Q_HEADS) for h in range(1, SWA_Q_HEADS + 1))
SWA_SCALE = SWA_HEAD_DIM ** -0.5
NT_DIMS = (((1,), (1,)), ((), ()))
TN_DIMS = (((0,), (0,)), ((), ()))


def _swa_span(n, kp_ref, kc_ref, vp_ref, vc_ref, pcol_ref, pprow_ref, pcrow_ref):
    k_span = jnp.concatenate([kp_ref[...], kc_ref[...]], axis=0).astype(F32)
    v_span = jnp.concatenate([vp_ref[...], vc_ref[...]], axis=0).astype(F32)
    lo = _lane((2 * BLOCK, LANES)) < 64
    k_sw = pltpu.roll(k_span, 64, axis=1)
    v_sw = pltpu.roll(v_span, 64, axis=1)
    kk = (jnp.where(lo, k_span, k_sw).astype(BF16), jnp.where(lo, k_sw, k_span).astype(BF16))
    vv_lo = (jnp.where(lo, v_span, 0.0).astype(BF16), jnp.where(lo, v_sw, 0.0).astype(BF16))
    vv_hi = (jnp.where(lo, 0.0, v_sw).astype(BF16), jnp.where(lo, 0.0, v_span).astype(BF16))
    pk = jnp.concatenate([pprow_ref[...], pcrow_ref[...]], axis=1)
    dist = jnp.abs(pcol_ref[...] - pk)
    qi = lax.broadcasted_iota(jnp.int32, (BLOCK, 2 * BLOCK), 0)
    ki = lax.broadcasted_iota(jnp.int32, (BLOCK, 2 * BLOCK), 1)
    first_key = jnp.where(n > 0, qi + 1, jnp.maximum(qi + 1, BLOCK))
    valid = jnp.logical_and(ki >= first_key, ki <= qi + BLOCK)
    mask_add = jnp.where(valid, 0.0, NEG_INF)
    return kk, vv_lo, vv_hi, dist, mask_add


def _swa_probs(q_half, kk, dist, mask_add, slope, sink):
    s = lax.dot_general(q_half, kk, NT_DIMS, preferred_element_type=F32) * SWA_SCALE - slope * dist + mask_add
    m = jnp.maximum(jnp.max(s, axis=-1, keepdims=True), sink)
    e = jnp.exp(s - m)
    e_sink = jnp.exp(sink - m)
    inv = 1.0 / (jnp.sum(e, axis=-1, keepdims=True) + e_sink)
    return e * inv, e_sink * inv


def _swa_specs():
    blk = lambda w: pl.BlockSpec((BLOCK, w), lambda n: (n, 0))
    prev = lambda w: pl.BlockSpec((BLOCK, w), lambda n: (jnp.maximum(n - 1, 0), 0))
    prow_c = pl.BlockSpec((1, BLOCK), lambda n: (0, n))
    prow_p = pl.BlockSpec((1, BLOCK), lambda n: (0, jnp.maximum(n - 1, 0)))
    smem = pl.BlockSpec(memory_space=pltpu.SMEM)
    return [blk(1024), prev(128), blk(128), prev(128), blk(128), blk(1), prow_p, prow_c, smem], blk


def _swa_fwd(qa, ka, va, pos_col, pos_row, sinks, *, comm=()):
    s = qa.shape[0]
    in_specs, blk = _swa_specs()

    def body(q_ref, kp_ref, kc_ref, vp_ref, vc_ref, pcol_ref, pprow_ref, pcrow_ref, sink_ref, o_ref):
        n = pl.program_id(0)
        kk, vv_lo, vv_hi, dist, mask_add = _swa_span(n, kp_ref, kc_ref, vp_ref, vc_ref, pcol_ref, pprow_ref, pcrow_ref)
        lo = _lane((BLOCK, LANES)) < 64
        for j in range(SWA_Q_HEADS // 2):
            kv = (2 * j) // (SWA_Q_HEADS // SWA_KV_HEADS)
            q_pair = q_ref[:, 128 * j:128 * (j + 1)].astype(F32)
            q_e = jnp.where(lo, q_pair, 0.0).astype(BF16)
            q_o = jnp.where(lo, 0.0, q_pair).astype(BF16)
            p_e, _ = _swa_probs(q_e, kk[kv], dist, mask_add, SWA_SLOPES[2 * j], sink_ref[2 * j])
            p_o, _ = _swa_probs(q_o, kk[kv], dist, mask_add, SWA_SLOPES[2 * j + 1], sink_ref[2 * j + 1])
            o_ref[:, 128 * j:128 * (j + 1)] = (
                jnp.dot(p_e.astype(BF16), vv_lo[kv], preferred_element_type=F32)
                + jnp.dot(p_o.astype(BF16), vv_hi[kv], preferred_element_type=F32))

    return _pcall(body, (qa, ka, ka, va, va, pos_col, pos_row, pos_row, sinks), name="swa_fwd", grid=(s // BLOCK,),
                  in_specs=in_specs, out_specs=blk(1024), out_shape=jax.ShapeDtypeStruct((s, 1024), F32),
                  sem=("parallel",), comm=comm)


def _swa_bwd(qa, ka, va, pos_col, pos_row, sinks, y_a, d_y, *, comm=()):
    s = qa.shape[0]
    in_specs, blk = _swa_specs()
    whole = pl.BlockSpec((s, 128), lambda n: (0, 0))

    def body(q_ref, kp_ref, kc_ref, vp_ref, vc_ref, pcol_ref, pprow_ref, pcrow_ref, sink_ref, y_ref, dy_ref,
             dq_ref, dk_ref, dv_ref, dsink_ref):
        n = pl.program_id(0)

        @pl.when(n == 0)
        def _():
            dk_ref[...] = jnp.zeros_like(dk_ref)
            dv_ref[...] = jnp.zeros_like(dv_ref)
            dsink_ref[...] = jnp.zeros_like(dsink_ref)

        kk, vv_lo, vv_hi, dist, mask_add = _swa_span(n, kp_ref, kc_ref, vp_ref, vc_ref, pcol_ref, pprow_ref, pcrow_ref)
        lo = _lane((BLOCK, LANES)) < 64
        lo2 = _lane((2 * BLOCK, LANES)) < 64
        lane1 = _lane((1, LANES))
        dsink = jnp.zeros((1, LANES), F32)
        dkk = [jnp.zeros((2 * BLOCK, LANES), F32) for _ in range(SWA_KV_HEADS)]
        dvv = [jnp.zeros((2 * BLOCK, LANES), F32) for _ in range(SWA_KV_HEADS)]
        for j in range(SWA_Q_HEADS // 2):
            kv = (2 * j) // (SWA_Q_HEADS // SWA_KV_HEADS)
            q_pair = q_ref[:, 128 * j:128 * (j + 1)].astype(F32)
            do_pair = dy_ref[:, 128 * j:128 * (j + 1)]
            do_b = do_pair.astype(BF16)
            doy = do_pair * y_ref[:, 128 * j:128 * (j + 1)]
            deltas = (jnp.sum(jnp.where(lo, doy, 0.0), axis=-1, keepdims=True),
                      jnp.sum(jnp.where(lo, 0.0, doy), axis=-1, keepdims=True))
            dq_halves = []
            for par, vv in ((0, vv_lo), (1, vv_hi)):
                h = 2 * j + par
                keep = lo if par == 0 else jnp.logical_not(lo)
                q_h = jnp.where(keep, q_pair, 0.0).astype(BF16)
                p, p_sink = _swa_probs(q_h, kk[kv], dist, mask_add, SWA_SLOPES[h], sink_ref[h])
                delta_h = deltas[par]
                dp = lax.dot_general(do_b, vv[kv], NT_DIMS, preferred_element_type=F32)
                ds = p * (dp - delta_h)
                dsink = dsink + jnp.where(lane1 == h, -jnp.sum(p_sink * delta_h, axis=0, keepdims=True), 0.0)
                ds_b = (ds * SWA_SCALE).astype(BF16)
                dq_halves.append(jnp.dot(ds_b, kk[kv], preferred_element_type=F32))
                dkk[kv] = dkk[kv] + lax.dot_general(ds_b, q_h, TN_DIMS, preferred_element_type=F32)
                pv = lax.dot_general(p.astype(BF16), do_b, TN_DIMS, preferred_element_type=F32)
                keep2 = lo2 if par == 0 else jnp.logical_not(lo2)
                dvv[kv] = dvv[kv] + jnp.where(keep2, pv, 0.0)
            dq_ref[:, 128 * j:128 * (j + 1)] = jnp.where(lo, dq_halves[0], dq_halves[1])
        fold = lambda t: t + pltpu.roll(t, 64, axis=1)
        dk_span = jnp.where(lo2, fold(dkk[0]), fold(dkk[1]))
        dv_span = jnp.where(lo2, fold(dvv[0]), fold(dvv[1]))
        prev0 = pl.multiple_of(jnp.maximum(n - 1, 0) * BLOCK, BLOCK)
        cur0 = pl.multiple_of(n * BLOCK, BLOCK)
        dk_ref[pl.ds(prev0, BLOCK), :] += dk_span[0:BLOCK]
        dk_ref[pl.ds(cur0, BLOCK), :] += dk_span[BLOCK:]
        dv_ref[pl.ds(prev0, BLOCK), :] += dv_span[0:BLOCK]
        dv_ref[pl.ds(cur0, BLOCK), :] += dv_span[BLOCK:]
        dsink_ref[...] += dsink

    return _pcall(
        body, (qa, ka, ka, va, va, pos_col, pos_row, pos_row, sinks, y_a, d_y), name="swa_bwd", grid=(s // BLOCK,),
        in_specs=in_specs + [blk(1024), blk(1024)],
        out_specs=[blk(1024), whole, whole, pl.BlockSpec((1, LANES), lambda n: (0, 0))],
        out_shape=[jax.ShapeDtypeStruct((s, 1024), F32), jax.ShapeDtypeStruct((s, 128), F32),
                   jax.ShapeDtypeStruct((s, 128), F32), jax.ShapeDtypeStruct((1, LANES), F32)],
        sem=("arbitrary",), comm=comm)


MLA_SCALE = (MLA_NOPE + MLA_ROPE) ** -0.5
MLA_TILE = 512


def _causal_mask(i, j, t):
    row = i * t + lax.broadcasted_iota(jnp.int32, (t, t), 0)
    col = j * t + lax.broadcasted_iota(jnp.int32, (t, t), 1)
    return col <= row


def _mla_fwd(q_cat, k_cat, v_b, *, comm=()):
    nh, s, _ = q_cat.shape
    t = _tile(s, MLA_TILE)
    nt = s // t

    def body(q_ref, k_ref, v_ref, o_ref, lse_ref, m_sc, l_sc, acc_sc):
        i, j = pl.program_id(1), pl.program_id(2)

        @pl.when(j == 0)
        def _():
            m_sc[...] = jnp.full_like(m_sc, NEG_INF)
            l_sc[...] = jnp.zeros_like(l_sc)
            acc_sc[...] = jnp.zeros_like(acc_sc)

        @pl.when(j <= i)
        def _():
            sc = lax.dot_general(q_ref[0], k_ref[0], NT_DIMS, preferred_element_type=F32) * MLA_SCALE
            sc = jnp.where(_causal_mask(i, j, t), sc, NEG_INF)
            m_new = jnp.maximum(m_sc[...], jnp.max(sc, axis=-1, keepdims=True))
            alpha = jnp.exp(m_sc[...] - m_new)
            p = jnp.exp(sc - m_new)
            l_sc[...] = alpha * l_sc[...] + jnp.sum(p, axis=-1, keepdims=True)
            acc_sc[...] = alpha * acc_sc[...] + jnp.dot(p.astype(BF16), v_ref[...], preferred_element_type=F32)
            m_sc[...] = m_new

        @pl.when(j == i)
        def _():
            o_ref[...] = acc_sc[...] * (1.0 / l_sc[...])
            lse_ref[0] = m_sc[...] + jnp.log(l_sc[...])

    return _pcall(
        body, (q_cat, k_cat, v_b), name="mla_fwd", grid=(nh, nt, nt),
        in_specs=[pl.BlockSpec((1, t, 256), lambda h, i, j: (h, i, 0)),
                  pl.BlockSpec((1, t, 256), lambda h, i, j: (h, jnp.minimum(j, i), 0)),
                  pl.BlockSpec((t, MLA_V), lambda h, i, j: (jnp.minimum(j, i), h))],
        out_specs=[pl.BlockSpec((t, MLA_V), lambda h, i, j: (i, h)),
                   pl.BlockSpec((1, t, 1), lambda h, i, j: (h, i, 0))],
        out_shape=[jax.ShapeDtypeStruct((s, nh * MLA_V), F32), jax.ShapeDtypeStruct((nh, s, 1), F32)],
        scratch_shapes=[pltpu.VMEM((t, 1), F32), pltpu.VMEM((t, 1), F32), pltpu.VMEM((t, MLA_V), F32)],
        sem=("parallel", "parallel", "arbitrary"), comm=comm)


def _mla_bwd(q_cat, k_cat, v_b, y_b, lse, d_y, *, comm=()):
    nh, s, _ = q_cat.shape
    t = _tile(s, MLA_TILE)
    nt = s // t
    col0 = (SWA_Q_HEADS * SWA_HEAD_DIM) // MLA_V

    def body(q_ref, k_ref, v_ref, y_ref, lse_ref, dy_ref, dq_ref, dk_ref, dv_ref, dk_sc, dv_sc):
        j, i = pl.program_id(1), pl.program_id(2)

        @pl.when((j == 0) & (i == 0))
        def _():
            dq_ref[...] = jnp.zeros_like(dq_ref)

        @pl.when(i == 0)
        def _():
            dk_sc[...] = jnp.zeros_like(dk_sc)
            dv_sc[...] = jnp.zeros_like(dv_sc)

        @pl.when(i >= j)
        def _():
            q = q_ref[0]
            k = k_ref[0]
            do = dy_ref[...]
            do_b = do.astype(BF16)
            sc = lax.dot_general(q, k, NT_DIMS, preferred_element_type=F32) * MLA_SCALE
            p = jnp.where(_causal_mask(i, j, t), jnp.exp(sc - lse_ref[0]), 0.0)
            delta = jnp.sum(do * y_ref[...], axis=-1, keepdims=True)
            dv_sc[...] += lax.dot_general(p.astype(BF16), do_b, TN_DIMS, preferred_element_type=F32)
            dp = lax.dot_general(do_b, v_ref[...], NT_DIMS, preferred_element_type=F32)
            ds_b = (p * (dp - delta) * MLA_SCALE).astype(BF16)
            dk_sc[...] += lax.dot_general(ds_b, q, TN_DIMS, preferred_element_type=F32)
            rows = pl.ds(pl.multiple_of(i * t, t), t)
            dq_ref[0, rows, :] += jnp.dot(ds_b, k, preferred_element_type=F32)

        @pl.when(i == nt - 1)
        def _():
            dk_ref[0] = dk_sc[...]
            dv_ref[...] = dv_sc[...]

    qrow = lambda h, j, i: (h, jnp.maximum(i, j), 0)
    return _pcall(
        body, (q_cat, k_cat, v_b, y_b, lse, d_y), name="mla_bwd", grid=(nh, nt, nt),
        in_specs=[pl.BlockSpec((1, t, 256), qrow),
                  pl.BlockSpec((1, t, 256), lambda h, j, i: (h, j, 0)),
                  pl.BlockSpec((t, MLA_V), lambda h, j, i: (j, h)),
                  pl.BlockSpec((t, MLA_V), lambda h, j, i: (jnp.maximum(i, j), h)),
                  pl.BlockSpec((1, t, 1), qrow),
                  pl.BlockSpec((t, MLA_V), lambda h, j, i: (jnp.maximum(i, j), col0 + h))],
        out_specs=[pl.BlockSpec((1, s, 256), lambda h, j, i: (h, 0, 0)),
                   pl.BlockSpec((1, t, 256), lambda h, j, i: (h, j, 0)),
                   pl.BlockSpec((t, MLA_V), lambda h, j, i: (j, h))],
        out_shape=[jax.ShapeDtypeStruct((nh, s, 256), F32), jax.ShapeDtypeStruct((nh, s, 256), F32),
                   jax.ShapeDtypeStruct((s, nh * MLA_V), F32)],
        scratch_shapes=[pltpu.VMEM((t, 256), F32), pltpu.VMEM((t, MLA_V), F32)],
        sem=("arbitrary", "arbitrary", "arbitrary"), comm=comm)


MEM_SCALE = MEM_DIM ** -0.5


def _mem_kv_fwd(mem, g_mem, w_memkv, g_mk):
    m_len = mem.shape[0]

    def body(mem_ref, g_ref, w_ref, gk_ref, mn_ref, kv_ref, kn_ref, v_ref):
        mn, _, _ = _norm_full(mem_ref[...], g_ref[...])
        mn_b = mn.astype(BF16)
        mn_ref[...] = mn_b
        kv = jnp.dot(mn_b, w_ref[...], preferred_element_type=F32)
        kv_ref[...] = kv
        for h in range(MEM_HEADS):
            kn, _, _ = _norm_full(kv[:, 128 * h:128 * (h + 1)], gk_ref[...])
            kn_ref[:, 128 * h:128 * (h + 1)] = kn.astype(BF16)
        v_ref[...] = kv[:, 512:1024].astype(BF16)

    return pl.pallas_call(
        body, name="mem_kv_fwd",
        out_shape=[jax.ShapeDtypeStruct((m_len, D_MODEL), BF16), jax.ShapeDtypeStruct((m_len, 1024), F32),
                   jax.ShapeDtypeStruct((m_len, 512), BF16), jax.ShapeDtypeStruct((m_len, 512), BF16)],
        compiler_params=_params(),
    )(mem, g_mem, w_memkv, g_mk)


def _mem_kv_bwd(mem, g_mem, w_memkv, g_mk, mn_b, kv, d_kn, d_v):
    m_len = mem.shape[0]

    def body(mem_ref, g_ref, w_ref, gk_ref, mn_ref, kv_ref, dkn_ref, dv_ref, dw_ref, dgmem_ref, dgk_ref):
        parts = []
        dgk = jnp.zeros((1, LANES), F32)
        for h in range(MEM_HEADS):
            _, xhat, r = _norm_full(kv_ref[:, 128 * h:128 * (h + 1)], gk_ref[...])
            dx, dgh = _norm_full_bwd(dkn_ref[:, 128 * h:128 * (h + 1)], gk_ref[...], xhat, r)
            parts.append(dx)
            dgk = dgk + dgh
        d_kv = jnp.concatenate(parts + [dv_ref[...]], axis=1).astype(BF16)
        dw_ref[...] = lax.dot_general(mn_ref[...], d_kv, TN_DIMS, preferred_element_type=F32)
        d_mn = lax.dot_general(d_kv, w_ref[...], NT_DIMS, preferred_element_type=F32)
        _, xhat, _ = _norm_full(mem_ref[...], g_ref[...])
        dgmem_ref[...] = jnp.sum(d_mn * xhat, axis=0, keepdims=True)
        dgk_ref[...] = dgk

    return pl.pallas_call(
        body, name="mem_kv_bwd",
        out_shape=[jax.ShapeDtypeStruct((D_MODEL, 1024), F32), jax.ShapeDtypeStruct((1, D_MODEL), F32),
                   jax.ShapeDtypeStruct((1, LANES), F32)],
        compiler_params=_params(),
    )(mem, g_mem, w_memkv, g_mk, mn_b, kv, d_kn, d_v)


def _mem_probs(q_h, k_h):
    sc = lax.dot_general(q_h, k_h, NT_DIMS, preferred_element_type=F32) * MEM_SCALE
    e = jnp.exp(sc - jnp.max(sc, axis=-1, keepdims=True))
    return e * (1.0 / jnp.sum(e, axis=-1, keepdims=True))


def _mem_attn_fwd(qm, km, vm, *, tm=512):
    s = qm.shape[0]
    tm = _tile(s, tm)
    m_len = km.shape[0]

    def body(q_ref, k_ref, v_ref, o_ref):
        for h in range(MEM_HEADS):
            sl = slice(128 * h, 128 * (h + 1))
            p = _mem_probs(q_ref[:, sl], k_ref[:, sl])
            o_ref[:, sl] = jnp.dot(p.astype(BF16), v_ref[:, sl], preferred_element_type=F32)

    kvspec = pl.BlockSpec((m_len, 512), lambda i: (0, 0))
    return pl.pallas_call(
        body, name="mem_attn_fwd", grid=(s // tm,),
        in_specs=[pl.BlockSpec((tm, 512), lambda i: (i, 0)), kvspec, kvspec],
        out_specs=pl.BlockSpec((tm, 512), lambda i: (i, 0)),
        out_shape=jax.ShapeDtypeStruct((s, 512), F32),
        compiler_params=_params(("parallel",)),
    )(qm, km, vm)


def _mem_attn_bwd(qm, km, vm, y_m, d_y, *, tm=512):
    s = qm.shape[0]
    tm = _tile(s, tm)
    m_len = km.shape[0]
    col0 = (SWA_Q_HEADS * SWA_HEAD_DIM + MLA_HEADS * MLA_V) // 512

    def body(q_ref, k_ref, v_ref, y_ref, dy_ref, dq_ref, dk_ref, dv_ref):
        @pl.when(pl.program_id(0) == 0)
        def _():
            dk_ref[...] = jnp.zeros_like(dk_ref)
            dv_ref[...] = jnp.zeros_like(dv_ref)

        for h in range(MEM_HEADS):
            sl = slice(128 * h, 128 * (h + 1))
            q_h, k_h = q_ref[:, sl], k_ref[:, sl]
            do = dy_ref[:, sl]
            do_b = do.astype(BF16)
            p = _mem_probs(q_h, k_h)
            delta = jnp.sum(do * y_ref[:, sl], axis=-1, keepdims=True)
            dv_ref[:, sl] += lax.dot_general(p.astype(BF16), do_b, TN_DIMS, preferred_element_type=F32)
            dp = lax.dot_general(do_b, v_ref[:, sl], NT_DIMS, preferred_element_type=F32)
            ds_b = (p * (dp - delta) * MEM_SCALE).astype(BF16)
            dq_ref[:, sl] = jnp.dot(ds_b, k_h, preferred_element_type=F32)
            dk_ref[:, sl] += lax.dot_general(ds_b, q_h, TN_DIMS, preferred_element_type=F32)

    kvspec = pl.BlockSpec((m_len, 512), lambda i: (0, 0))
    row = pl.BlockSpec((tm, 512), lambda i: (i, 0))
    return pl.pallas_call(
        body, name="mem_attn_bwd", grid=(s // tm,),
        in_specs=[row, kvspec, kvspec, row, pl.BlockSpec((tm, 512), lambda i: (i, col0))],
        out_specs=[row, kvspec, kvspec],
        out_shape=[jax.ShapeDtypeStruct((s, 512), F32), jax.ShapeDtypeStruct((m_len, 512), F32),
                   jax.ShapeDtypeStruct((m_len, 512), F32)],
        compiler_params=_params(("arbitrary",)),
    )(qm, km, vm, y_m, d_y)


def _ffn_gate_up(fn, w_gate, w_up, *, tm=512):
    s, d = fn.shape
    nsp, _, tf = w_gate.shape
    f = nsp * tf
    tm = _tile(s, tm)

    def body(x_ref, wg_ref, wu_ref, g_ref, u_ref, a_ref):
        x = x_ref[...]
        gate = jnp.dot(x, wg_ref[...], preferred_element_type=F32)
        up = jnp.dot(x, wu_ref[...], preferred_element_type=F32)
        g_ref[...] = gate.astype(BF16)
        u_ref[...] = up.astype(BF16)
        a_ref[...] = (gate * (1.0 / (1.0 + jnp.exp(-gate))) * up).astype(BF16)

    wspec = pl.BlockSpec((None, d, tf), lambda j, i: (j, 0, 0))
    ospec = pl.BlockSpec((tm, tf), lambda j, i: (i, j))
    osh = jax.ShapeDtypeStruct((s, f), BF16)
    return pl.pallas_call(
        body, name="ffn_gate_up", grid=(nsp, s // tm),
        in_specs=[pl.BlockSpec((tm, d), lambda j, i: (i, 0)), wspec, wspec],
        out_specs=[ospec, ospec, ospec], out_shape=[osh, osh, osh],
        compiler_params=_params(("parallel", "parallel")),
    )(fn, w_gate, w_up)


def _ffn_bwd_act(d_out, w_down, gate, up, *, tm=512, tf=1408, comm=()):
    s, d = d_out.shape
    f = w_down.shape[0]
    tm, tf = _tile(s, tm), _tile(f, tf)

    def body(do_ref, wd_ref, g_ref, u_ref, dg_ref, du_ref):
        d_act = lax.dot_general(do_ref[...].astype(BF16), wd_ref[...], NT_DIMS, preferred_element_type=F32)
        gate = g_ref[...].astype(F32)
        sig = 1.0 / (1.0 + jnp.exp(-gate))
        du_ref[...] = (d_act * (gate * sig)).astype(BF16)
        dg_ref[...] = (d_act * u_ref[...].astype(F32) * (sig * (1.0 + gate * (1.0 - sig)))).astype(BF16)

    ospec = pl.BlockSpec((tm, tf), lambda j, i: (i, j))
    osh = jax.ShapeDtypeStruct((s, f), BF16)
    return _pcall(
        body, (d_out, w_down, gate, up), name="ffn_bwd_act", grid=(f // tf, s // tm),
        in_specs=[pl.BlockSpec((tm, d), lambda j, i: (i, 0)), pl.BlockSpec((tf, d), lambda j, i: (j, 0)), ospec, ospec],
        out_specs=[ospec, ospec], out_shape=[osh, osh], sem=("parallel", "parallel"), comm=comm)


def _loss_head(out, target, *, tm=512):
    s, d = out.shape
    tm = _tile(s, tm)

    def body(o_ref, t_ref, d_ref, l_ref):
        err = o_ref[...] - t_ref[...]
        d_ref[...] = err * (1.0 / d)
        part = 0.5 * jnp.sum(jnp.mean(err * err, axis=-1, keepdims=True), axis=0, keepdims=True)
        part = jnp.broadcast_to(part, (1, LANES))

        @pl.when(pl.program_id(0) == 0)
        def _():
            l_ref[...] = part

        @pl.when(pl.program_id(0) > 0)
        def _():
            l_ref[...] += part

    row = pl.BlockSpec((tm, d), lambda i: (i, 0))
    return pl.pallas_call(
        body, name="loss_head", grid=(s // tm,),
        in_specs=[row, row], out_specs=[row, pl.BlockSpec((1, LANES), lambda i: (0, 0))],
        out_shape=[jax.ShapeDtypeStruct((s, d), F32), jax.ShapeDtypeStruct((1, LANES), F32)],
        compiler_params=_params(("arbitrary",)),
    )(out, target)


def _cols(g4):
    return jnp.concatenate([g4[k] for k in range(N_CHIPS)], axis=1)


def _full_w_in(g4):
    nat = _cols(g4)
    pad = jnp.zeros((nat.shape[0], IN_PAD - IN_WIDTH), nat.dtype)
    return jnp.concatenate([nat[:, :2304], nat[:, 2368:], nat[:, 2304:2368], pad], axis=1)


def _shards_w_in(dwp):
    nat = jnp.concatenate([dwp[:, :2304], dwp[:, C_KR:C_KR + 64], dwp[:, 2304:C_KR]], axis=1)
    per = IN_WIDTH // N_CHIPS
    return jnp.stack([nat[:, per * k:per * (k + 1)] for k in range(N_CHIPS)])


def _full_heads(g4, first):
    return jnp.concatenate([g4[k][:, :first] for k in range(N_CHIPS)] + [g4[k][:, first:] for k in range(N_CHIPS)], axis=1)


def _shards_heads(dwp, first, rest):
    base = N_CHIPS * first
    return jnp.stack([jnp.concatenate([dwp[:, first * k:first * (k + 1)], dwp[:, base + rest * k:base + rest * (k + 1)]], axis=1)
                      for k in range(N_CHIPS)])


def _rope_tables(pos):
    inv_freq = ROPE_THETA ** (-jnp.arange(0, MLA_ROPE, 2, dtype=F32) / MLA_ROPE)
    ang = pos.astype(F32)[:, None] * inv_freq
    cos, sin = jnp.cos(ang), jnp.sin(ang)
    return jnp.tile(cos, (1, 4)), jnp.concatenate([-sin, sin, -sin, sin], axis=1)


def _gain_table(sp):
    two = lambda v: jnp.tile(v, (1, 2))
    rows = [two(sp["swa_q_norm_g"]), two(sp["swa_k_norm_g"]), sp["mla_qn_norm_g"], two(sp["mla_qr_norm_g"]),
            sp["mla_kn_norm_g"], two(sp["mla_kr_norm_g"]), sp["mem_q_norm_g"], jnp.zeros((1, LANES), F32)]
    return jnp.concatenate(rows, axis=0)


CHIP_DISTANCES = (1, 2, 3)


def _place():
    x, y, c = lax.axis_index("x"), lax.axis_index("y"), lax.axis_index("c")
    return x, y, c, 2 * x + y


def _chip_at(x, y, d):
    px = 1 - x if d & 2 else x
    py = 1 - y if d & 1 else y
    return px, py, 2 * px + py


def _row_tile(rows, want=512, mult=8):
    t = min(rows, want)
    t -= t % mult
    while rows % t:
        t -= mult
    return t


def _cast_into_slot(w, meta, *, name):
    rows, cols = w.shape
    tr = _row_tile(rows, 512, 16)

    def body(meta_ref, w_ref, o_ref):
        o_ref[...] = w_ref[...].astype(BF16)

    grid_spec = pltpu.PrefetchScalarGridSpec(
        num_scalar_prefetch=1, grid=(rows // tr,),
        in_specs=[pl.BlockSpec((tr, cols), lambda i, m: (i, 0))],
        out_specs=pl.BlockSpec((None, tr, cols), lambda i, m: (m[0], i, 0)))
    return pl.pallas_call(
        body, name=name, grid_spec=grid_spec,
        out_shape=jax.ShapeDtypeStruct((N_CHIPS, rows, cols), BF16),
        compiler_params=_params(("parallel",)),
    )(meta, w)


def _remote(src, dst, ssem, rsem, i, device):
    return pltpu.make_async_remote_copy(src_ref=src, dst_ref=dst, send_sem=ssem.at[i], recv_sem=rsem.at[i],
                                        device_id=device, device_id_type=MESH)


def _symmetric_stage(ins, out_shapes, aliases, n_sem, copies):
    def issue(i_refs, o_refs, ssem, rsem):
        for send, _ in copies(i_refs, o_refs, ssem, rsem):
            send.start()

    def wait(i_refs, o_refs, ssem, rsem):
        pairs = copies(i_refs, o_refs, ssem, rsem)
        for _, arrival in pairs:
            arrival.wait_recv()
        for send, _ in pairs:
            send.wait_send()

    return _Stage(ins, out_shapes, aliases, n_sem, issue, wait)


def _gather_stage(slots, leg):
    n = len(slots)

    def copies(_, outs, ssem, rsem):
        x, y, c, k_me = _place()
        pairs = []
        for w in range(n):
            half = outs[w].shape[1] // 2
            slab = lambda k, cc, w=w, half=half: outs[w].at[k, pl.ds(cc * half, half)]
            for d in CHIP_DISTANCES:
                px, py, k_src = _chip_at(x, y, d)
                i = 3 * w + d - 1
                if leg == "ici":
                    pairs.append((_remote(slab(k_me, c), slab(k_me, c), ssem, rsem, i, (px, py, c)),
                                  _remote(slab(k_src, c), slab(k_src, c), ssem, rsem, i, (x, y, c))))
                else:
                    pairs.append((_remote(slab(k_src, c), slab(k_src, c), ssem, rsem, i, (x, y, 1 - c)),
                                  _remote(slab(k_src, 1 - c), slab(k_src, 1 - c), ssem, rsem, i, (x, y, c))))
        return pairs

    shapes = [jax.ShapeDtypeStruct(s.shape, s.dtype) for s in slots]
    return _symmetric_stage(slots, shapes, {w: w for w in range(n)}, 3 * n, copies)


def _halves_stage(grads):
    n = len(grads)

    def copies(ins, outs, ssem, rsem):
        x, y, c, _ = _place()
        pairs = []
        for w in range(n):
            half = ins[w].shape[1] // 2
            pairs.append((_remote(ins[w].at[:, pl.ds((1 - c) * half, half)], outs[w], ssem, rsem, w, (x, y, 1 - c)),
                          _remote(outs[w], outs[w], ssem, rsem, w, (x, y, c))))
        return pairs

    shapes = [jax.ShapeDtypeStruct((N_CHIPS, g.shape[1] // 2, g.shape[2]), g.dtype) for g in grads]
    return _symmetric_stage(grads, shapes, {}, n, copies)


def _chips_stage(parts):
    n = len(parts)

    def copies(ins, outs, ssem, rsem):
        x, y, c, _ = _place()
        pairs = []
        for w in range(n):
            for d in CHIP_DISTANCES:
                px, py, _ = _chip_at(x, y, d)
                i = 3 * w + d - 1
                pairs.append((_remote(ins[w].at[d - 1], outs[w].at[d - 1], ssem, rsem, i, (px, py, c)),
                              _remote(outs[w].at[d - 1], outs[w].at[d - 1], ssem, rsem, i, (x, y, c))))
        return pairs

    shapes = [jax.ShapeDtypeStruct(p.shape, p.dtype) for p in parts]
    return _symmetric_stage(parts, shapes, {}, 3 * n, copies)


def _swap_stage(totals):
    n = len(totals)

    def copies(ins, outs, ssem, rsem):
        x, y, c, _ = _place()
        return [(_remote(ins[w], outs[w], ssem, rsem, w, (x, y, 1 - c)),
                 _remote(outs[w], outs[w], ssem, rsem, w, (x, y, c))) for w in range(n)]

    shapes = [jax.ShapeDtypeStruct(t.shape, t.dtype) for t in totals]
    return _symmetric_stage(totals, shapes, {}, n, copies)


def _run_stage(st, *, name):
    n_in, n_out = len(st.ins), len(st.out_shapes)

    def body(*refs):
        ins, outs, (ssem, rsem) = refs[:n_in], refs[n_in:n_in + n_out], refs[n_in + n_out:]
        st.issue(ins, outs, ssem, rsem)
        st.wait(ins, outs, ssem, rsem)

    sem = pltpu.SemaphoreType.DMA
    return list(pl.pallas_call(
        body, name=name, in_specs=[ANY] * n_in, out_specs=[ANY] * n_out, out_shape=st.out_shapes,
        input_output_aliases=st.aliases, scratch_shapes=[sem((st.n_sem,)), sem((st.n_sem,))],
    )(*st.ins))


def _add_pair(meta, g4, recv, *, name):
    nsh, rows, cols = g4.shape
    half = rows // 2
    tr = _row_tile(half, 128 if cols > 1024 else 256, 16)
    nt = half // tr

    def body(meta_ref, g0, g1, g2, g3, r0, r1, r2, r3, own_ref, oth_ref):
        own_ref[...] = g0[...] + r0[...]
        for d, (g, r) in enumerate(((g1, r1), (g2, r2), (g3, r3))):
            oth_ref[d] = (g[...] + r[...]).astype(BF16)

    blk = (None, tr, cols)
    gspec = lambda d: pl.BlockSpec(blk, lambda i, m: (jnp.bitwise_xor(m[0], d), m[1] * nt + i, 0))
    rspec = lambda d: pl.BlockSpec(blk, lambda i, m: (jnp.bitwise_xor(m[0], d), i, 0))
    grid_spec = pltpu.PrefetchScalarGridSpec(
        num_scalar_prefetch=1, grid=(nt,),
        in_specs=[gspec(d) for d in range(nsh)] + [rspec(d) for d in range(nsh)],
        out_specs=[pl.BlockSpec((tr, cols), lambda i, m: (i, 0)), pl.BlockSpec((3, tr, cols), lambda i, m: (0, i, 0))])
    return pl.pallas_call(
        body, name=name, grid_spec=grid_spec,
        out_shape=[jax.ShapeDtypeStruct((half, cols), F32), jax.ShapeDtypeStruct((3, half, cols), BF16)],
        compiler_params=_params(("parallel",)),
    )(meta, g4, g4, g4, g4, recv, recv, recv, recv)


def _add_chips(own, recv, *, name):
    half, cols = own.shape
    tr = _row_tile(half, 256, 16)

    def body(p_ref, r_ref, o_ref):
        o_ref[...] = ((p_ref[...] + r_ref[0].astype(F32)) + r_ref[1].astype(F32)) + r_ref[2].astype(F32)

    return pl.pallas_call(
        body, name=name, grid=(half // tr,),
        in_specs=[pl.BlockSpec((tr, cols), lambda i: (i, 0)), pl.BlockSpec((3, tr, cols), lambda i: (0, i, 0))],
        out_specs=pl.BlockSpec((tr, cols), lambda i: (i, 0)),
        out_shape=jax.ShapeDtypeStruct((half, cols), F32),
        compiler_params=_params(("parallel",)),
    )(own, recv)


def _adamw_math(w, g, m, v):
    m = ADAM_B1 * m + (1.0 - ADAM_B1) * g
    v = ADAM_B2 * v + (1.0 - ADAM_B2) * (g * g)
    m_hat = m / (1.0 - ADAM_B1 ** ADAM_STEP)
    v_hat = v / (1.0 - ADAM_B2 ** ADAM_STEP)
    delta = -ADAM_LR * (m_hat / (jnp.sqrt(v_hat) + ADAM_EPS) + ADAM_WD * w)
    return delta, m, v


def _adamw(meta, w, g_mine, g_theirs, m, v, *, name):
    rows, cols = w.shape
    half = rows // 2
    tr = _row_tile(half, 256)
    nt = half // tr

    def body(meta_ref, w_ref, a_ref, b_ref, m_ref, v_ref, g_ref, d_ref, mo_ref, vo_ref):
        is_mine = (pl.program_id(0) // nt) == meta_ref[1]
        g = jnp.where(is_mine, a_ref[...], b_ref[...])
        g_ref[...] = g
        d_ref[...], mo_ref[...], vo_ref[...] = _adamw_math(w_ref[...], g, m_ref[...], v_ref[...])

    blk = pl.BlockSpec((tr, cols), lambda i, mt: (i, 0))
    mine = pl.BlockSpec((tr, cols), lambda i, mt: (jnp.where(i // nt == mt[1], i % nt, 0), 0))
    theirs = pl.BlockSpec((tr, cols), lambda i, mt: (jnp.where(i // nt == mt[1], 0, i % nt), 0))
    sh = jax.ShapeDtypeStruct((rows, cols), F32)
    grid_spec = pltpu.PrefetchScalarGridSpec(
        num_scalar_prefetch=1, grid=(rows // tr,),
        in_specs=[blk, mine, theirs, blk, blk], out_specs=[blk] * 4)
    return pl.pallas_call(
        body, name=name, grid_spec=grid_spec, out_shape=[sh] * 4,
        compiler_params=_params(("arbitrary",)),
    )(meta, w, g_mine, g_theirs, m, v)


N_DEVICES = 8


def _small_step(g_pack, w_pack, m_pack, v_pack):
    rows = g_pack.shape[0]

    def body(g_ref, w_ref, m_ref, v_ref, sum_ref, d_ref, mo_ref, vo_ref, slots, ssem, rsem):
        x, y, c, _ = _place()
        me = 4 * x + 2 * y + c
        slots[me] = g_ref[...]
        copies = []
        for r in range(1, N_DEVICES):
            px = 1 - x if r & 4 else x
            py = 1 - y if r & 2 else y
            pc = 1 - c if r & 1 else c
            copies.append(pltpu.make_async_remote_copy(
                src_ref=g_ref, dst_ref=slots.at[me], send_sem=ssem.at[r - 1], recv_sem=rsem.at[r - 1],
                device_id=(px, py, pc), device_id_type=MESH))
        for cp in copies:
            cp.start()
        for r in range(1, N_DEVICES):
            src = jnp.bitwise_xor(me, r)
            pltpu.make_async_remote_copy(
                src_ref=g_ref, dst_ref=slots.at[src], send_sem=ssem.at[r - 1], recv_sem=rsem.at[r - 1],
                device_id=(x, y, c), device_id_type=MESH).wait_recv()
        for cp in copies:
            cp.wait_send()
        total = slots[0]
        for k in range(1, N_DEVICES):
            total = total + slots[k]
        sum_ref[...] = total
        d_ref[...], mo_ref[...], vo_ref[...] = _adamw_math(w_ref[...], total, m_ref[...], v_ref[...])

    sh = jax.ShapeDtypeStruct((rows, LANES), F32)
    vm = pl.BlockSpec(memory_space=pltpu.VMEM)
    return pl.pallas_call(
        body, name="small_allreduce_adamw",
        in_specs=[vm] * 4, out_specs=[vm] * 4, out_shape=[sh] * 4,
        scratch_shapes=[pltpu.VMEM((N_DEVICES, rows, LANES), F32),
                        pltpu.SemaphoreType.DMA((N_DEVICES - 1,)), pltpu.SemaphoreType.DMA((N_DEVICES - 1,))],
    )(g_pack, w_pack, m_pack, v_pack)


WEIGHTS = ("attn_norm_g", "w_in", "swa_q_norm_g", "swa_k_norm_g", "swa_sinks", "mla_cq_norm_g", "mla_ckv_norm_g",
           "w_uq", "w_ukv", "mla_qn_norm_g", "mla_qr_norm_g", "mla_kn_norm_g", "mla_kr_norm_g", "mem_norm_g",
           "w_mem_kv", "mem_q_norm_g", "mem_k_norm_g", "w_out", "ffn_norm_g", "w_gate", "w_up", "w_down")
BIG = ("w_in", "w_uq", "w_ukv", "w_mem_kv", "w_out", "w_gate", "w_up", "w_down")
SMALL = tuple(n for n in WEIGHTS if n not in BIG)
PACK_UNIT = 8 * LANES


def _pack(parts):
    out = []
    for p in parts:
        n = p.shape[1]
        padded = -(-n // PACK_UNIT) * PACK_UNIT
        out.append(jnp.pad(p, ((0, 0), (0, padded - n))).reshape(padded // LANES, LANES))
    return jnp.concatenate(out, axis=0)


def _unpack(buf, sizes):
    out, row = [], 0
    for n in sizes:
        rows = -(-n // PACK_UNIT) * 8
        out.append(buf[row:row + rows].reshape(1, rows * LANES)[:, :n])
        row += rows
    return out


def kernel(x, mem, positions, attn_norm_g, w_in, swa_q_norm_g, swa_k_norm_g, swa_sinks, mla_cq_norm_g, mla_ckv_norm_g, w_uq, w_ukv, mla_qn_norm_g, mla_qr_norm_g, mla_kn_norm_g, mla_kr_norm_g, mem_norm_g, w_mem_kv, mem_q_norm_g, mem_k_norm_g, w_out, ffn_norm_g, w_gate, w_up, w_down, loss_target, m_attn_norm_g, m_w_in, m_swa_q_norm_g, m_swa_k_norm_g, m_swa_sinks, m_mla_cq_norm_g, m_mla_ckv_norm_g, m_w_uq, m_w_ukv, m_mla_qn_norm_g, m_mla_qr_norm_g, m_mla_kn_norm_g, m_mla_kr_norm_g, m_mem_norm_g, m_w_mem_kv, m_mem_q_norm_g, m_mem_k_norm_g, m_w_out, m_ffn_norm_g, m_w_gate, m_w_up, m_w_down, v_attn_norm_g, v_w_in, v_swa_q_norm_g, v_swa_k_norm_g, v_swa_sinks, v_mla_cq_norm_g, v_mla_ckv_norm_g, v_w_uq, v_w_ukv, v_mla_qn_norm_g, v_mla_qr_norm_g, v_mla_kn_norm_g, v_mla_kr_norm_g, v_mem_norm_g, v_w_mem_kv, v_mem_q_norm_g, v_mem_k_norm_g, v_w_out, v_ffn_norm_g, v_w_gate, v_w_up, v_w_down):
    given = dict(locals())
    wts = {n: given[n] for n in WEIGHTS}
    mom_m = {n: given["m_" + n] for n in WEIGHTS}
    mom_v = {n: given["v_" + n] for n in WEIGHTS}

    mx, my, mc = lax.axis_index("x"), lax.axis_index("y"), lax.axis_index("c")
    meta = jnp.stack([2 * mx + my, mc]).astype(jnp.int32)
    x, mem, pos, target = x[0], mem[0], positions[0], loss_target[0]
    sp = {n: wts[n] for n in SMALL}
    s = x.shape[0]
    cos_t, sin_t = _rope_tables(pos)
    pos_f = pos.astype(F32)
    pos_col, pos_row = pos_f.reshape(s, 1), pos_f.reshape(1, s)
    g128 = _gain_table(sp)
    sinks = sp["swa_sinks"].reshape(SWA_Q_HEADS)
    gcq, gckv = sp["mla_cq_norm_g"], sp["mla_ckv_norm_g"]
    gs = {}

    slot = {n: _cast_into_slot(wts[n][0], meta, name="cast_" + n) for n in BIG}
    first = _run_stage(_gather_stage([slot["w_in"], slot["w_uq"], slot["w_ukv"]], "ici"), name="gather_first_ici")
    first = _run_stage(_gather_stage(first, "d2d"), name="gather_first_d2d")
    w_in_f, w_uq_f, w_ukv_f = _full_w_in(first[0]), _full_heads(first[1], MLA_NOPE), _full_heads(first[2], MLA_NOPE)

    hn = _rms_fwd(x, sp["attn_norm_g"], name="attn_norm_fwd")
    proj, [mid] = _matmul(hn, w_in_f, name="in_proj",
                          comm=[_gather_stage([slot["w_mem_kv"], slot["w_out"]], "ici")])
    (qa, ka, va, q_cat, k_cat, v_b, qm), [mid] = _attn_prep_fwd(
        proj, g128, gcq, gckv, w_uq_f, w_ukv_f, cos_t, sin_t, comm=[_gather_stage(mid, "d2d")])
    w_mem_kv_f = mid[0].reshape(D_MODEL, 2 * MEM_HEADS * MEM_DIM)
    w_out_f = mid[1].reshape(D_MODEL, D_MODEL)
    mn_b, kv_m, km, vm = _mem_kv_fwd(mem, sp["mem_norm_g"], w_mem_kv_f, sp["mem_k_norm_g"])
    y_a, [wg] = _swa_fwd(qa, ka, va, pos_col, pos_row, sinks, comm=[_gather_stage([slot["w_gate"]], "ici")])
    (y_b, lse), [wud, wg] = _mla_fwd(
        q_cat, k_cat, v_b, comm=[_gather_stage([slot["w_up"], slot["w_down"]], "ici"), _gather_stage(wg, "d2d")])
    y_m = _mem_attn_fwd(qm, km, vm)
    y = jnp.concatenate([y_a, y_b, y_m], axis=1)
    h1, [wud] = _matmul(y, w_out_f, add=x, name="out_proj", comm=[_gather_stage(wud, "d2d")])
    w_gate_f, w_up_f, w_down_f = wg[0], wud[0], wud[1].reshape(D_FF, D_MODEL)
    fn = _rms_fwd(h1, sp["ffn_norm_g"], name="ffn_norm_fwd")
    gate, up, act = _ffn_gate_up(fn, w_gate_f, w_up_f)
    out = _matmul(act, w_down_f, add=h1, name="down_proj", tk=1408)
    d_out, loss_tile = _loss_head(out, target)

    add_pair = lambda n, g4, r: _add_pair(meta, g4, r, name="grad_add_pair_" + n)
    add_chips = lambda n, own, r: _add_chips(own, r, name="grad_add_chips_" + n)
    mine, theirs = {}, {}

    dw_down = _matmul(act, d_out, ta=True, name="dw_down", tm=1408, tn=1024, tk=1024)
    dw_down = dw_down.reshape(N_CHIPS, D_FF // N_CHIPS, D_MODEL)
    (d_gate, d_up), [[r]] = _ffn_bwd_act(d_out, w_down_f, gate, up, comm=[_halves_stage([dw_down])])
    own_d, oth_d = add_pair("w_down", dw_down, r)
    dw_gate, [[r]] = _matmul(fn, d_gate, ta=True, name="dw_gate", tk=1024, tn=D_FF // N_CHIPS, out_split=N_CHIPS,
                             comm=[_chips_stage([oth_d])])
    mine["w_down"] = add_chips("w_down", own_d, r)
    dw_up, [[r]] = _matmul(fn, d_up, ta=True, name="dw_up", tk=1024, tn=D_FF // N_CHIPS, out_split=N_CHIPS,
                           comm=[_halves_stage([dw_gate])])
    own_g, oth_g = add_pair("w_gate", dw_gate, r)
    d_fn, [[r], [theirs["w_down"]]] = _matmul(
        d_gate, w_gate_f, tb=True, b_split=True, name="dfn_gate",
        comm=[_chips_stage([oth_g]), _swap_stage([mine["w_down"]])])
    mine["w_gate"] = add_chips("w_gate", own_g, r)
    d_fn, [[r]] = _matmul(d_up, w_up_f, tb=True, b_split=True, add=d_fn, name="dfn_up",
                          comm=[_halves_stage([dw_up])])
    own_u, oth_u = add_pair("w_up", dw_up, r)
    d_h1, gs["ffn_norm_g"] = _rms_bwd(d_fn, h1, sp["ffn_norm_g"], d_out, name="ffn_norm_bwd")
    dw_out, [[theirs["w_gate"]]] = _matmul(y, d_h1, ta=True, name="dw_out", tk=1024,
                                           comm=[_swap_stage([mine["w_gate"]])])
    dw_out = dw_out.reshape(N_CHIPS, D_MODEL // N_CHIPS, D_MODEL)
    d_y, [[r]] = _matmul(d_h1, w_out_f, tb=True, name="dy", comm=[_halves_stage([dw_out])])
    own_o, oth_o = add_pair("w_out", dw_out, r)
    (d_qa, d_ka, d_va, d_sink), [[r]] = _swa_bwd(qa, ka, va, pos_col, pos_row, sinks, y_a, d_y,
                                                 comm=[_chips_stage([oth_u])])
    mine["w_up"] = add_chips("w_up", own_u, r)
    (d_qcat, d_kcat, d_vb), [[r], [theirs["w_up"]]] = _mla_bwd(
        q_cat, k_cat, v_b, y_b, lse, d_y, comm=[_chips_stage([oth_o]), _swap_stage([mine["w_up"]])])
    mine["w_out"] = add_chips("w_out", own_o, r)
    d_qm, d_km, d_vm = _mem_attn_bwd(qm, km, vm, y_m, d_y)
    (d_proj, dw_uq, dw_ukv, dg128, gs["mla_cq_norm_g"], gs["mla_ckv_norm_g"]), [[theirs["w_out"]]] = _attn_prep_bwd(
        proj, g128, gcq, gckv, w_uq_f, w_ukv_f, cos_t, sin_t, d_qa, d_ka, d_va, d_qcat, d_kcat, d_vb, d_qm,
        comm=[_swap_stage([mine["w_out"]])])
    dw_mem_kv, gs["mem_norm_g"], gs["mem_k_norm_g"] = _mem_kv_bwd(
        mem, sp["mem_norm_g"], w_mem_kv_f, sp["mem_k_norm_g"], mn_b, kv_m, d_km, d_vm)
    late = ("w_uq", "w_ukv", "w_mem_kv")
    late_g = [_shards_heads(dw_uq, MLA_NOPE, MLA_ROPE), _shards_heads(dw_ukv, MLA_NOPE, MLA_V),
              dw_mem_kv.reshape(N_CHIPS, D_MODEL // N_CHIPS, -1)]
    dw_in, [rs] = _matmul(hn, d_proj, ta=True, name="dw_in", tk=1024, comm=[_halves_stage(late_g)])
    late_sums = [add_pair(n, g4, r) for n, g4, r in zip(late, late_g, rs)]
    dw_in = _shards_w_in(dw_in)
    d_hn, [rs, [r]] = _matmul(d_proj, w_in_f, tb=True, name="dhn", tk=1536,
                              comm=[_chips_stage([oth for _, oth in late_sums]), _halves_stage([dw_in])])
    for n, (own, _), r_n in zip(late, late_sums, rs):
        mine[n] = add_chips(n, own, r_n)
    own_i, oth_i = add_pair("w_in", dw_in, r)
    (grad_x, gs["attn_norm_g"]), [[r], late_theirs] = _rms_bwd(
        d_hn, x, sp["attn_norm_g"], d_h1, name="attn_norm_bwd",
        comm=[_chips_stage([oth_i]), _swap_stage([mine[n] for n in late])])
    theirs.update(zip(late, late_theirs))
    mine["w_in"] = add_chips("w_in", own_i, r)
    [theirs["w_in"]] = _run_stage(_swap_stage([mine["w_in"]]), name="grad_swap_w_in")

    fold = lambda r: r[:, :64] + r[:, 64:]
    gs["swa_q_norm_g"] = fold(dg128[G_SWA_Q:G_SWA_Q + 1])
    gs["swa_k_norm_g"] = fold(dg128[G_SWA_K:G_SWA_K + 1])
    gs["mla_qn_norm_g"] = dg128[G_QN:G_QN + 1]
    gs["mla_qr_norm_g"] = fold(dg128[G_QR:G_QR + 1])
    gs["mla_kn_norm_g"] = dg128[G_KN:G_KN + 1]
    gs["mla_kr_norm_g"] = fold(dg128[G_KR:G_KR + 1])
    gs["mem_q_norm_g"] = dg128[G_MQ:G_MQ + 1]
    gs["swa_sinks"] = d_sink[:, :SWA_Q_HEADS]

    grad, delta, new_m, new_v = {}, {}, {}, {}
    for n in BIG:
        g2, d, m2, v2 = _adamw(meta, wts[n][0], mine[n], theirs[n], mom_m[n][0], mom_v[n][0], name="adamw_" + n)
        grad[n], delta[n], new_m[n], new_v[n] = g2[None], d[None], m2[None], v2[None]

    sizes = [wts[n].shape[1] for n in SMALL]
    zero = jnp.zeros((1, LANES), F32)
    packs = _small_step(_pack([gs[n] for n in SMALL] + [loss_tile]), _pack([wts[n] for n in SMALL] + [zero]),
                        _pack([mom_m[n] for n in SMALL] + [zero]), _pack([mom_v[n] for n in SMALL] + [zero]))
    for store, buf in zip((grad, delta, new_m, new_v), packs):
        for n, val in zip(SMALL, _unpack(buf, sizes)):
            store[n] = val
    loss = _unpack(packs[0], sizes + [LANES])[-1][0, 0]

    return (loss, grad_x[None], *[grad[n] for n in WEIGHTS], *[delta[n] for n in WEIGHTS],
            *[new_m[n] for n in WEIGHTS], *[new_v[n] for n in WEIGHTS])
```

```python
import functools
import math

import jax
import jax.numpy as jnp
from jax import lax
from jax.experimental import pallas as pl
from jax.experimental.pallas import tpu as pltpu

F32 = jnp.float32
BF16 = jnp.bfloat16

D_MODEL = 2048
BLOCK = 128
EPS = 1e-6
NEG_INF = -1e30
SWA_Q_HEADS = 16
SWA_KV_HEADS = 2
SWA_HEAD_DIM = 64
MLA_HEADS = 4
MLA_RANK = 512
MLA_NOPE = 128
MLA_ROPE = 64
MLA_V = 128
ROPE_THETA = 10000.0
MEM_HEADS = 4
MEM_DIM = 128
D_FF = 5632
IN_WIDTH = 2880
IN_PAD = 3072
N_CHIPS = 4

ADAM_LR = 0.001
ADAM_B1 = 0.9
ADAM_B2 = 0.999
ADAM_EPS = 1e-08
ADAM_WD = 0.01
ADAM_STEP = 10

VMEM_LIMIT_BYTES = 56 * 1024 * 1024
LANES = 128

MESH = pl.DeviceIdType.MESH


def _params(sem=None, **kw):
    return pltpu.CompilerParams(dimension_semantics=sem, vmem_limit_bytes=VMEM_LIMIT_BYTES, **kw)


def _tile(n, want):
    if n <= want:
        return n
    t = want - want % LANES
    while t > 0:
        if n % t == 0:
            return t
        t -= LANES
    return n


ANY = pl.BlockSpec(memory_space=pl.ANY)


class _Stage:
    def __init__(self, ins, out_shapes, aliases, n_sem, issue, wait):
        self.ins, self.out_shapes, self.aliases, self.n_sem = list(ins), list(out_shapes), dict(aliases), n_sem
        self.issue, self.wait = issue, wait


def _pcall(body, args, *, name, grid, in_specs, out_specs, out_shape, scratch_shapes=(), sem=None, comm=(),
           prefetch=()):
    multi = isinstance(out_shape, (list, tuple))
    out_specs_l = list(out_specs) if multi else [out_specs]
    out_shape_l = list(out_shape) if multi else [out_shape]
    npf = len(prefetch)

    def call(fn, in_specs_, out_specs_, out_shape_, scratch_, operands, sem_, aliases=None):
        kw = dict(name=name, out_shape=out_shape_, compiler_params=_params(sem_))
        if aliases:
            kw["input_output_aliases"] = aliases
        if npf:
            spec = pltpu.PrefetchScalarGridSpec(num_scalar_prefetch=npf, grid=grid, in_specs=in_specs_,
                                                out_specs=out_specs_, scratch_shapes=scratch_)
            return pl.pallas_call(fn, grid_spec=spec, **kw)(*prefetch, *operands)
        return pl.pallas_call(fn, grid=grid, in_specs=in_specs_, out_specs=out_specs_, scratch_shapes=scratch_,
                              **kw)(*operands)

    if not comm:
        return call(body, list(in_specs), out_specs, out_shape, list(scratch_shapes), args, sem)
    n_in, n_out, n_scr = len(in_specs), len(out_specs_l), len(scratch_shapes)
    cins = [a for st in comm for a in st.ins]
    couts = [s for st in comm for s in st.out_shapes]
    aliases, ci, co = {}, 0, 0
    for st in comm:
        for a_i, o_i in st.aliases.items():
            aliases[npf + n_in + ci + a_i] = n_out + co + o_i
        ci, co = ci + len(st.ins), co + len(st.out_shapes)

    def wrapped(*refs):
        pre = refs[:npf]
        p = npf
        ins = refs[p:p + n_in]; p += n_in
        cin_refs = refs[p:p + len(cins)]; p += len(cins)
        outs = refs[p:p + n_out]; p += n_out
        cout_refs = refs[p:p + len(couts)]; p += len(couts)
        scr = refs[p:p + n_scr]; p += n_scr
        sems = refs[p:]
        first = functools.reduce(jnp.logical_and, [pl.program_id(a) == 0 for a in range(len(grid))])
        last = functools.reduce(jnp.logical_and, [pl.program_id(a) == grid[a] - 1 for a in range(len(grid))])

        def each(what):
            i, o = 0, 0
            for k, st in enumerate(comm):
                fn = st.issue if what == "issue" else st.wait
                fn(cin_refs[i:i + len(st.ins)], cout_refs[o:o + len(st.out_shapes)], sems[2 * k], sems[2 * k + 1])
                i, o = i + len(st.ins), o + len(st.out_shapes)

        @pl.when(first)
        def _():
            each("issue")

        body(*pre, *ins, *outs, *scr)

        @pl.when(last)
        def _():
            each("wait")

    sem_scr = [pltpu.SemaphoreType.DMA((st.n_sem,)) for st in comm for _ in range(2)]
    res = call(wrapped, list(in_specs) + [ANY] * len(cins), out_specs_l + [ANY] * len(couts), out_shape_l + couts,
               list(scratch_shapes) + sem_scr, (*args, *cins), ("arbitrary",) * len(grid), aliases)
    normal = list(res[:n_out])
    stage_outs, o = [], n_out
    for st in comm:
        stage_outs.append(list(res[o:o + len(st.out_shapes)]))
        o += len(st.out_shapes)
    return (normal if multi else normal[0]), stage_outs


def _matmul(a, b, *, name, ta=False, tb=False, add=None, out_dtype=F32, tm=1024, tn=1024, tk=2048,
            b_split=False, out_split=0, comm=()):
    if ta:
        kdim, m = a.shape
    else:
        m, kdim = a.shape
    if b_split:
        assert tb
        nsp, n, kb = b.shape
        kb = kb * nsp
    elif tb:
        n, kb = b.shape
    else:
        kb, n = b.shape
    assert kb == kdim, (a.shape, b.shape, ta, tb)
    if b_split:
        tk = kdim // nsp
    if out_split:
        tn = _tile(n // out_split, tn)
    tm, tn, tk = _tile(m, tm), _tile(n, tn), _tile(kdim, tk)
    nk = kdim // tk
    dims = (((0 if ta else 1,), (1 if tb else 0,)), ((), ()))

    def body(*refs):
        if add is None:
            a_ref, b_ref, o_ref, acc_ref = refs
            add_ref = None
        else:
            a_ref, b_ref, add_ref, o_ref, acc_ref = refs
        k = pl.program_id(2)
        part = lax.dot_general(a_ref[...].astype(BF16), b_ref[...].astype(BF16), dims,
                               preferred_element_type=F32)

        @pl.when(k == 0)
        def _():
            acc_ref[...] = part

        @pl.when(k > 0)
        def _():
            acc_ref[...] += part

        @pl.when(k == nk - 1)
        def _():
            r = acc_ref[...]
            if add_ref is not None:
                r = r + add_ref[...].astype(F32)
            o_ref[...] = r.astype(o_ref.dtype)

    a_spec = pl.BlockSpec((tk, tm), lambda i, j, k: (k, i)) if ta else pl.BlockSpec((tm, tk), lambda i, j, k: (i, k))
    if b_split:
        b_spec = pl.BlockSpec((None, tn, tk), lambda i, j, k: (k, j, 0))
    elif tb:
        b_spec = pl.BlockSpec((tn, tk), lambda i, j, k: (j, k))
    else:
        b_spec = pl.BlockSpec((tk, tn), lambda i, j, k: (k, j))
    in_specs = [a_spec, b_spec]
    args = [a, b]
    if add is not None:
        in_specs.append(pl.BlockSpec((tm, tn), lambda i, j, k: (i, j)))
        args.append(add)
    if out_split:
        per = (n // out_split) // tn
        out_spec = pl.BlockSpec((None, tm, tn), lambda i, j, k: (j // per, i, j % per))
        out_shape = jax.ShapeDtypeStruct((out_split, m, n // out_split), out_dtype)
    else:
        out_spec = pl.BlockSpec((tm, tn), lambda i, j, k: (i, j))
        out_shape = jax.ShapeDtypeStruct((m, n), out_dtype)
    return _pcall(body, args, name=name, grid=(m // tm, n // tn, nk), in_specs=in_specs, out_specs=out_spec,
                  out_shape=out_shape, scratch_shapes=[pltpu.VMEM((tm, tn), F32)],
                  sem=("parallel", "parallel", "arbitrary"), comm=comm)


def _rms_fwd(x, g, *, name, tm=512, comm=()):
    s, d = x.shape
    tm = _tile(s, tm)

    def body(x_ref, g_ref, o_ref):
        xv = x_ref[...]
        r = lax.rsqrt(jnp.mean(xv * xv, axis=-1, keepdims=True) + EPS)
        o_ref[...] = (xv * r * g_ref[...]).astype(o_ref.dtype)

    return _pcall(body, (x, g), name=name, grid=(s // tm,),
                  in_specs=[pl.BlockSpec((tm, d), lambda i: (i, 0)), pl.BlockSpec((1, d), lambda i: (0, 0))],
                  out_specs=pl.BlockSpec((tm, d), lambda i: (i, 0)),
                  out_shape=jax.ShapeDtypeStruct((s, d), BF16), sem=("parallel",), comm=comm)


def _rms_bwd(dy, x, g, res, *, name, tm=512, comm=()):
    s, d = x.shape
    tm = _tile(s, tm)

    def body(dy_ref, x_ref, g_ref, res_ref, dx_ref, dg_ref):
        xv = x_ref[...]
        dyv = dy_ref[...]
        r = lax.rsqrt(jnp.mean(xv * xv, axis=-1, keepdims=True) + EPS)
        xhat = xv * r
        dyg = dyv * g_ref[...]
        mt = jnp.mean(dyg * xhat, axis=-1, keepdims=True)
        dx_ref[...] = res_ref[...] + r * (dyg - xhat * mt)
        part = jnp.sum(dyv * xhat, axis=0, keepdims=True)

        @pl.when(pl.program_id(0) == 0)
        def _():
            dg_ref[...] = part

        @pl.when(pl.program_id(0) > 0)
        def _():
            dg_ref[...] += part

    row = pl.BlockSpec((tm, d), lambda i: (i, 0))
    vec = pl.BlockSpec((1, d), lambda i: (0, 0))
    return _pcall(body, (dy, x, g, res), name=name, grid=(s // tm,), in_specs=[row, row, vec, row],
                  out_specs=[row, vec],
                  out_shape=[jax.ShapeDtypeStruct((s, d), F32), jax.ShapeDtypeStruct((1, d), F32)],
                  sem=("arbitrary",), comm=comm)


def _lane(shape):
    return lax.broadcasted_iota(jnp.int32, shape, 1)


def _halfsum(t, lo):
    s_lo = jnp.sum(jnp.where(lo, t, 0.0), axis=-1, keepdims=True)
    s_hi = jnp.sum(jnp.where(lo, 0.0, t), axis=-1, keepdims=True)
    return jnp.where(lo, s_lo, s_hi)


def _norm_pair(x, g, lo):
    r = lax.rsqrt(_halfsum(x * x, lo) * (1.0 / 64.0) + EPS)
    xhat = x * r
    return xhat * g, xhat, r


def _norm_pair_bwd(dy, g, xhat, r, lo):
    dyg = dy * g
    mt = _halfsum(dyg * xhat, lo) * (1.0 / 64.0)
    return r * (dyg - xhat * mt), jnp.sum(dy * xhat, axis=0, keepdims=True)


def _norm_full(x, g):
    r = lax.rsqrt(jnp.mean(x * x, axis=-1, keepdims=True) + EPS)
    xhat = x * r
    return xhat * g, xhat, r


def _norm_full_bwd(dy, g, xhat, r):
    dyg = dy * g
    mt = jnp.mean(dyg * xhat, axis=-1, keepdims=True)
    return r * (dyg - xhat * mt), jnp.sum(dy * xhat, axis=0, keepdims=True)


def _rot(x, first32):
    return jnp.where(first32, pltpu.roll(x, 96, axis=1), pltpu.roll(x, 32, axis=1))


def _rope(x, cos_t, sin_t, first32):
    return x * cos_t + _rot(x, first32) * sin_t


def _rope_bwd(dy, cos_t, sin_t, first32):
    return dy * cos_t + _rot(dy * sin_t, first32)


G_SWA_Q, G_SWA_K, G_QN, G_QR, G_KN, G_KR, G_MQ = range(7)

C_QA, C_KA, C_VA, C_CQ, C_CKV, C_QM, C_KR = 0, 1024, 1152, 1280, 1792, 2304, 2816


def _prep_common(p_ref, g128_ref, gcq_ref, gckv_ref, wuq_ref, wukv_ref, cos_ref, sin_ref):
    tm = p_ref.shape[0]
    lane = _lane((tm, LANES))
    lo = lane < 64
    first32 = (lane % 64) < 32
    cos_t = cos_ref[...]
    sin_t = sin_ref[...]
    g = lambda row: g128_ref[row:row + 1, :]
    out = dict(lo=lo, first32=first32, cos_t=cos_t, sin_t=sin_t, lane=lane)
    cq_n, cq_hat, cq_r = _norm_full(p_ref[:, C_CQ:C_CQ + MLA_RANK], gcq_ref[...])
    ckv_n, ckv_hat, ckv_r = _norm_full(p_ref[:, C_CKV:C_CKV + MLA_RANK], gckv_ref[...])
    cq_b = cq_n.astype(BF16)
    ckv_b = ckv_n.astype(BF16)
    q_b = jnp.dot(cq_b, wuq_ref[...], preferred_element_type=F32)
    kv_b = jnp.dot(ckv_b, wukv_ref[...], preferred_element_type=F32)
    out.update(cq_b=cq_b, cq_hat=cq_hat, cq_r=cq_r, ckv_b=ckv_b, ckv_hat=ckv_hat, ckv_r=ckv_r, q_b=q_b, kv_b=kv_b, g=g)
    return out


def _attn_prep_fwd(proj, g128, gcq, gckv, wuq, wukv, cos_t, sin_t, *, tm=512, comm=()):
    s = proj.shape[0]
    tm = _tile(s, tm)

    def body(p_ref, g128_ref, gcq_ref, gckv_ref, wuq_ref, wukv_ref, cos_ref, sin_ref,
             qa_ref, ka_ref, va_ref, qcat_ref, kcat_ref, vb_ref, qm_ref):
        c = _prep_common(p_ref, g128_ref, gcq_ref, gckv_ref, wuq_ref, wukv_ref, cos_ref, sin_ref)
        lo, first32, g = c["lo"], c["first32"], c["g"]
        for j in range(SWA_Q_HEADS // 2):
            y, _, _ = _norm_pair(p_ref[:, C_QA + 128 * j:C_QA + 128 * (j + 1)], g(G_SWA_Q), lo)
            qa_ref[:, 128 * j:128 * (j + 1)] = y.astype(BF16)
        y, _, _ = _norm_pair(p_ref[:, C_KA:C_KA + 128], g(G_SWA_K), lo)
        ka_ref[...] = y.astype(BF16)
        va_ref[...] = p_ref[:, C_VA:C_VA + 128].astype(BF16)
        kr, _, _ = _norm_pair(p_ref[:, C_KR:C_KR + 128], g(G_KR), lo)
        kr = jnp.where(lo, _rope(kr, c["cos_t"], c["sin_t"], first32), 0.0)
        krkr = (kr + pltpu.roll(kr, 64, axis=1)).astype(BF16)
        q_b, kv_b = c["q_b"], c["kv_b"]
        qr = []
        for j in range(MLA_HEADS // 2):
            y, _, _ = _norm_pair(q_b[:, 512 + 128 * j:512 + 128 * (j + 1)], g(G_QR), lo)
            qr.append(_rope(y, c["cos_t"], c["sin_t"], first32))
        for h in range(MLA_HEADS):
            qn, _, _ = _norm_full(q_b[:, 128 * h:128 * (h + 1)], g(G_QN))
            keep = lo if h % 2 == 0 else jnp.logical_not(lo)
            qcat_ref[h, :, 0:128] = qn.astype(BF16)
            qcat_ref[h, :, 128:256] = jnp.where(keep, qr[h // 2], 0.0).astype(BF16)
            kn, _, _ = _norm_full(kv_b[:, 128 * h:128 * (h + 1)], g(G_KN))
            kcat_ref[h, :, 0:128] = kn.astype(BF16)
            kcat_ref[h, :, 128:256] = krkr
        vb_ref[...] = kv_b[:, 512:1024].astype(BF16)
        for h in range(MEM_HEADS):
            y, _, _ = _norm_full(p_ref[:, C_QM + 128 * h:C_QM + 128 * (h + 1)], g(G_MQ))
            qm_ref[:, 128 * h:128 * (h + 1)] = y.astype(BF16)

    row = lambda w: pl.BlockSpec((tm, w), lambda i: (i, 0))
    full = lambda shape: pl.BlockSpec(shape, lambda i: tuple(0 for _ in shape))
    cat = pl.BlockSpec((MLA_HEADS, tm, 256), lambda i: (0, i, 0))
    return _pcall(
        body, (proj, g128, gcq, gckv, wuq, wukv, cos_t, sin_t), name="attn_prep_fwd", grid=(s // tm,),
        in_specs=[row(IN_PAD), full((8, 128)), full((1, 512)), full((1, 512)), full((512, 768)), full((512, 1024)),
                  row(128), row(128)],
        out_specs=[row(1024), row(128), row(128), cat, cat, row(512), row(512)],
        out_shape=[jax.ShapeDtypeStruct((s, 1024), BF16), jax.ShapeDtypeStruct((s, 128), BF16),
                   jax.ShapeDtypeStruct((s, 128), BF16), jax.ShapeDtypeStruct((MLA_HEADS, s, 256), BF16),
                   jax.ShapeDtypeStruct((MLA_HEADS, s, 256), BF16), jax.ShapeDtypeStruct((s, 512), BF16),
                   jax.ShapeDtypeStruct((s, 512), BF16)],
        sem=("parallel",), comm=comm)


def _attn_prep_bwd(proj, g128, gcq, gckv, wuq, wukv, cos_t, sin_t,
                   d_qa, d_ka, d_va, d_qcat, d_kcat, d_vb, d_qm, *, tm=256, comm=()):
    s = proj.shape[0]
    tm = _tile(s, tm)

    def body(p_ref, g128_ref, gcq_ref, gckv_ref, wuq_ref, wukv_ref, cos_ref, sin_ref,
             dqa_ref, dka_ref, dva_ref, dqcat_ref, dkcat_ref, dvb_ref, dqm_ref,
             dp_ref, dwuq_ref, dwukv_ref, dg128_ref, dgcq_ref, dgckv_ref):
        c = _prep_common(p_ref, g128_ref, gcq_ref, gckv_ref, wuq_ref, wukv_ref, cos_ref, sin_ref)
        lo, first32, g = c["lo"], c["first32"], c["g"]
        cos_v, sin_v = c["cos_t"], c["sin_t"]
        q_b, kv_b = c["q_b"], c["kv_b"]
        zero_row = jnp.zeros((1, LANES), F32)
        dg = {k: zero_row for k in range(7)}

        for j in range(SWA_Q_HEADS // 2):
            sl = slice(C_QA + 128 * j, C_QA + 128 * (j + 1))
            _, xhat, r = _norm_pair(p_ref[:, sl], g(G_SWA_Q), lo)
            dx, dgj = _norm_pair_bwd(dqa_ref[:, 128 * j:128 * (j + 1)], g(G_SWA_Q), xhat, r, lo)
            dp_ref[:, sl] = dx
            dg[G_SWA_Q] = dg[G_SWA_Q] + dgj
        _, xhat, r = _norm_pair(p_ref[:, C_KA:C_KA + 128], g(G_SWA_K), lo)
        dx, dgj = _norm_pair_bwd(dka_ref[...], g(G_SWA_K), xhat, r, lo)
        dp_ref[:, C_KA:C_KA + 128] = dx
        dg[G_SWA_K] = dgj
        dp_ref[:, C_VA:C_VA + 128] = dva_ref[...]

        dqb_parts = [None] * 6
        for h in range(MLA_HEADS):
            _, xhat, r = _norm_full(q_b[:, 128 * h:128 * (h + 1)], g(G_QN))
            dx, dgj = _norm_full_bwd(dqcat_ref[h, :, 0:128], g(G_QN), xhat, r)
            dqb_parts[h] = dx
            dg[G_QN] = dg[G_QN] + dgj
        for j in range(MLA_HEADS // 2):
            _, xhat, r = _norm_pair(q_b[:, 512 + 128 * j:512 + 128 * (j + 1)], g(G_QR), lo)
            d_rot = jnp.where(lo, dqcat_ref[2 * j, :, 128:256], dqcat_ref[2 * j + 1, :, 128:256])
            d_y = _rope_bwd(d_rot, cos_v, sin_v, first32)
            dx, dgj = _norm_pair_bwd(d_y, g(G_QR), xhat, r, lo)
            dqb_parts[4 + j] = dx
            dg[G_QR] = dg[G_QR] + dgj
        d_qb = jnp.concatenate(dqb_parts, axis=1).astype(BF16)
        dwuq = lax.dot_general(c["cq_b"], d_qb, (((0,), (0,)), ((), ())), preferred_element_type=F32)
        d_cqn = lax.dot_general(d_qb, wuq_ref[...], (((1,), (1,)), ((), ())), preferred_element_type=F32)
        dx, dgcq = _norm_full_bwd(d_cqn, gcq_ref[...], c["cq_hat"], c["cq_r"])
        dp_ref[:, C_CQ:C_CQ + MLA_RANK] = dx

        dkv_parts = []
        d_krkr = jnp.zeros((p_ref.shape[0], LANES), F32)
        for h in range(MLA_HEADS):
            _, xhat, r = _norm_full(kv_b[:, 128 * h:128 * (h + 1)], g(G_KN))
            dx, dgj = _norm_full_bwd(dkcat_ref[h, :, 0:128], g(G_KN), xhat, r)
            dkv_parts.append(dx)
            dg[G_KN] = dg[G_KN] + dgj
            d_krkr = d_krkr + dkcat_ref[h, :, 128:256]
        d_kvb = jnp.concatenate(dkv_parts + [dvb_ref[...]], axis=1).astype(BF16)
        dwukv = lax.dot_general(c["ckv_b"], d_kvb, (((0,), (0,)), ((), ())), preferred_element_type=F32)
        d_ckvn = lax.dot_general(d_kvb, wukv_ref[...], (((1,), (1,)), ((), ())), preferred_element_type=F32)
        dx, dgckv = _norm_full_bwd(d_ckvn, gckv_ref[...], c["ckv_hat"], c["ckv_r"])
        dp_ref[:, C_CKV:C_CKV + MLA_RANK] = dx

        _, xhat, r = _norm_pair(p_ref[:, C_KR:C_KR + 128], g(G_KR), lo)
        d_kr = jnp.where(lo, d_krkr + pltpu.roll(d_krkr, 64, axis=1), 0.0)
        d_y = jnp.where(lo, _rope_bwd(d_kr, cos_v, sin_v, first32), 0.0)
        dx, dgj = _norm_pair_bwd(d_y, g(G_KR), xhat, r, lo)
        dp_ref[:, C_KR:C_KR + 128] = jnp.where(lo, dx, 0.0)
        dp_ref[:, C_KR + 128:] = jnp.zeros((p_ref.shape[0], IN_PAD - C_KR - 128), F32)
        dg[G_KR] = dgj

        for h in range(MEM_HEADS):
            sl = slice(C_QM + 128 * h, C_QM + 128 * (h + 1))
            _, xhat, r = _norm_full(p_ref[:, sl], g(G_MQ))
            dx, dgj = _norm_full_bwd(dqm_ref[:, 128 * h:128 * (h + 1)], g(G_MQ), xhat, r)
            dp_ref[:, sl] = dx
            dg[G_MQ] = dg[G_MQ] + dgj

        dg_tile = jnp.concatenate([dg[k] for k in range(7)] + [zero_row], axis=0)

        @pl.when(pl.program_id(0) == 0)
        def _():
            dwuq_ref[...] = dwuq
            dwukv_ref[...] = dwukv
            dg128_ref[...] = dg_tile
            dgcq_ref[...] = dgcq
            dgckv_ref[...] = dgckv

        @pl.when(pl.program_id(0) > 0)
        def _():
            dwuq_ref[...] += dwuq
            dwukv_ref[...] += dwukv
            dg128_ref[...] += dg_tile
            dgcq_ref[...] += dgcq
            dgckv_ref[...] += dgckv

    row = lambda w: pl.BlockSpec((tm, w), lambda i: (i, 0))
    full = lambda shape: pl.BlockSpec(shape, lambda i: tuple(0 for _ in shape))
    cat = pl.BlockSpec((MLA_HEADS, tm, 256), lambda i: (0, i, 0))
    return _pcall(
        body, (proj, g128, gcq, gckv, wuq, wukv, cos_t, sin_t, d_qa, d_ka, d_va, d_qcat, d_kcat, d_vb, d_qm),
        name="attn_prep_bwd", grid=(s // tm,),
        in_specs=[row(IN_PAD), full((8, 128)), full((1, 512)), full((1, 512)), full((512, 768)), full((512, 1024)),
                  row(128), row(128),
                  row(1024), row(128), row(128), cat, cat, row(512), row(512)],
        out_specs=[row(IN_PAD), full((512, 768)), full((512, 1024)), full((8, 128)), full((1, 512)), full((1, 512))],
        out_shape=[jax.ShapeDtypeStruct((s, IN_PAD), F32), jax.ShapeDtypeStruct((512, 768), F32),
                   jax.ShapeDtypeStruct((512, 1024), F32), jax.ShapeDtypeStruct((8, 128), F32),
                   jax.ShapeDtypeStruct((1, 512), F32), jax.ShapeDtypeStruct((1, 512), F32)],
        sem=("arbitrary",), comm=comm)


SWA_SLOPES = tuple(2.0 ** (-8.0 * h / SWA_Q_HEADS) for h in range(1, SWA_Q_HEADS + 1))
SWA_SCALE = SWA_HEAD_DIM ** -0.5
NT_DIMS = (((1,), (1,)), ((), ()))
TN_DIMS = (((0,), (0,)), ((), ()))


def _swa_span(n, kp_ref, kc_ref, vp_ref, vc_ref, pcol_ref, pprow_ref, pcrow_ref):
    k_span = jnp.concatenate([kp_ref[...], kc_ref[...]], axis=0).astype(F32)
    v_span = jnp.concatenate([vp_ref[...], vc_ref[...]], axis=0).astype(F32)
    lo = _lane((2 * BLOCK, LANES)) < 64
    k_sw = pltpu.roll(k_span, 64, axis=1)
    v_sw = pltpu.roll(v_span, 64, axis=1)
    kk = (jnp.where(lo, k_span, k_sw).astype(BF16), jnp.where(lo, k_sw, k_span).astype(BF16))
    vv_lo = (jnp.where(lo, v_span, 0.0).astype(BF16), jnp.where(lo, v_sw, 0.0).astype(BF16))
    vv_hi = (jnp.where(lo, 0.0, v_sw).astype(BF16), jnp.where(lo, 0.0, v_span).astype(BF16))
    pk = jnp.concatenate([pprow_ref[...], pcrow_ref[...]], axis=1)
    dist = jnp.abs(pcol_ref[...] - pk)
    qi = lax.broadcasted_iota(jnp.int32, (BLOCK, 2 * BLOCK), 0)
    ki = lax.broadcasted_iota(jnp.int32, (BLOCK, 2 * BLOCK), 1)
    first_key = jnp.where(n > 0, qi + 1, jnp.maximum(qi + 1, BLOCK))
    valid = jnp.logical_and(ki >= first_key, ki <= qi + BLOCK)
    mask_add = jnp.where(valid, 0.0, NEG_INF)
    return kk, vv_lo, vv_hi, dist, mask_add


def _swa_probs(q_half, kk, dist, mask_add, slope, sink):
    s = lax.dot_general(q_half, kk, NT_DIMS, preferred_element_type=F32) * SWA_SCALE - slope * dist + mask_add
    m = jnp.maximum(jnp.max(s, axis=-1, keepdims=True), sink)
    e = jnp.exp(s - m)
    e_sink = jnp.exp(sink - m)
    inv = 1.0 / (jnp.sum(e, axis=-1, keepdims=True) + e_sink)
    return e * inv, e_sink * inv


def _swa_specs():
    blk = lambda w: pl.BlockSpec((BLOCK, w), lambda n: (n, 0))
    prev = lambda w: pl.BlockSpec((BLOCK, w), lambda n: (jnp.maximum(n - 1, 0), 0))
    prow_c = pl.BlockSpec((1, BLOCK), lambda n: (0, n))
    prow_p = pl.BlockSpec((1, BLOCK), lambda n: (0, jnp.maximum(n - 1, 0)))
    smem = pl.BlockSpec(memory_space=pltpu.SMEM)
    return [blk(1024), prev(128), blk(128), prev(128), blk(128), blk(1), prow_p, prow_c, smem], blk


def _swa_fwd(qa, ka, va, pos_col, pos_row, sinks, *, comm=()):
    s = qa.shape[0]
    in_specs, blk = _swa_specs()

    def body(q_ref, kp_ref, kc_ref, vp_ref, vc_ref, pcol_ref, pprow_ref, pcrow_ref, sink_ref, o_ref):
        n = pl.program_id(0)
        kk, vv_lo, vv_hi, dist, mask_add = _swa_span(n, kp_ref, kc_ref, vp_ref, vc_ref, pcol_ref, pprow_ref, pcrow_ref)
        lo = _lane((BLOCK, LANES)) < 64
        for j in range(SWA_Q_HEADS // 2):
            kv = (2 * j) // (SWA_Q_HEADS // SWA_KV_HEADS)
            q_pair = q_ref[:, 128 * j:128 * (j + 1)].astype(F32)
            q_e = jnp.where(lo, q_pair, 0.0).astype(BF16)
            q_o = jnp.where(lo, 0.0, q_pair).astype(BF16)
            p_e, _ = _swa_probs(q_e, kk[kv], dist, mask_add, SWA_SLOPES[2 * j], sink_ref[2 * j])
            p_o, _ = _swa_probs(q_o, kk[kv], dist, mask_add, SWA_SLOPES[2 * j + 1], sink_ref[2 * j + 1])
            o_ref[:, 128 * j:128 * (j + 1)] = (
                jnp.dot(p_e.astype(BF16), vv_lo[kv], preferred_element_type=F32)
                + jnp.dot(p_o.astype(BF16), vv_hi[kv], preferred_element_type=F32))

    return _pcall(body, (qa, ka, ka, va, va, pos_col, pos_row, pos_row, sinks), name="swa_fwd", grid=(s // BLOCK,),
                  in_specs=in_specs, out_specs=blk(1024), out_shape=jax.ShapeDtypeStruct((s, 1024), F32),
                  sem=("parallel",), comm=comm)


def _swa_bwd(qa, ka, va, pos_col, pos_row, sinks, y_a, d_y, *, comm=()):
    s = qa.shape[0]
    in_specs, blk = _swa_specs()
    whole = pl.BlockSpec((s, 128), lambda n: (0, 0))

    def body(q_ref, kp_ref, kc_ref, vp_ref, vc_ref, pcol_ref, pprow_ref, pcrow_ref, sink_ref, y_ref, dy_ref,
             dq_ref, dk_ref, dv_ref, dsink_ref):
        n = pl.program_id(0)

        @pl.when(n == 0)
        def _():
            dk_ref[...] = jnp.zeros_like(dk_ref)
            dv_ref[...] = jnp.zeros_like(dv_ref)
            dsink_ref[...] = jnp.zeros_like(dsink_ref)

        kk, vv_lo, vv_hi, dist, mask_add = _swa_span(n, kp_ref, kc_ref, vp_ref, vc_ref, pcol_ref, pprow_ref, pcrow_ref)
        lo = _lane((BLOCK, LANES)) < 64
        lo2 = _lane((2 * BLOCK, LANES)) < 64
        lane1 = _lane((1, LANES))
        dsink = jnp.zeros((1, LANES), F32)
        dkk = [jnp.zeros((2 * BLOCK, LANES), F32) for _ in range(SWA_KV_HEADS)]
        dvv = [jnp.zeros((2 * BLOCK, LANES), F32) for _ in range(SWA_KV_HEADS)]
        for j in range(SWA_Q_HEADS // 2):
            kv = (2 * j) // (SWA_Q_HEADS // SWA_KV_HEADS)
            q_pair = q_ref[:, 128 * j:128 * (j + 1)].astype(F32)
            do_pair = dy_ref[:, 128 * j:128 * (j + 1)]
            do_b = do_pair.astype(BF16)
            doy = do_pair * y_ref[:, 128 * j:128 * (j + 1)]
            deltas = (jnp.sum(jnp.where(lo, doy, 0.0), axis=-1, keepdims=True),
                      jnp.sum(jnp.where(lo, 0.0, doy), axis=-1, keepdims=True))
            dq_halves = []
            for par, vv in ((0, vv_lo), (1, vv_hi)):
                h = 2 * j + par
                keep = lo if par == 0 else jnp.logical_not(lo)
                q_h = jnp.where(keep, q_pair, 0.0).astype(BF16)
                p, p_sink = _swa_probs(q_h, kk[kv], dist, mask_add, SWA_SLOPES[h], sink_ref[h])
                delta_h = deltas[par]
                dp = lax.dot_general(do_b, vv[kv], NT_DIMS, preferred_element_type=F32)
                ds = p * (dp - delta_h)
                dsink = dsink + jnp.where(lane1 == h, -jnp.sum(p_sink * delta_h, axis=0, keepdims=True), 0.0)
                ds_b = (ds * SWA_SCALE).astype(BF16)
                dq_halves.append(jnp.dot(ds_b, kk[kv], preferred_element_type=F32))
                dkk[kv] = dkk[kv] + lax.dot_general(ds_b, q_h, TN_DIMS, preferred_element_type=F32)
                pv = lax.dot_general(p.astype(BF16), do_b, TN_DIMS, preferred_element_type=F32)
                keep2 = lo2 if par == 0 else jnp.logical_not(lo2)
                dvv[kv] = dvv[kv] + jnp.where(keep2, pv, 0.0)
            dq_ref[:, 128 * j:128 * (j + 1)] = jnp.where(lo, dq_halves[0], dq_halves[1])
        fold = lambda t: t + pltpu.roll(t, 64, axis=1)
        dk_span = jnp.where(lo2, fold(dkk[0]), fold(dkk[1]))
        dv_span = jnp.where(lo2, fold(dvv[0]), fold(dvv[1]))
        prev0 = pl.multiple_of(jnp.maximum(n - 1, 0) * BLOCK, BLOCK)
        cur0 = pl.multiple_of(n * BLOCK, BLOCK)
        dk_ref[pl.ds(prev0, BLOCK), :] += dk_span[0:BLOCK]
        dk_ref[pl.ds(cur0, BLOCK), :] += dk_span[BLOCK:]
        dv_ref[pl.ds(prev0, BLOCK), :] += dv_span[0:BLOCK]
        dv_ref[pl.ds(cur0, BLOCK), :] += dv_span[BLOCK:]
        dsink_ref[...] += dsink

    return _pcall(
        body, (qa, ka, ka, va, va, pos_col, pos_row, pos_row, sinks, y_a, d_y), name="swa_bwd", grid=(s // BLOCK,),
        in_specs=in_specs + [blk(1024), blk(1024)],
        out_specs=[blk(1024), whole, whole, pl.BlockSpec((1, LANES), lambda n: (0, 0))],
        out_shape=[jax.ShapeDtypeStruct((s, 1024), F32), jax.ShapeDtypeStruct((s, 128), F32),
                   jax.ShapeDtypeStruct((s, 128), F32), jax.ShapeDtypeStruct((1, LANES), F32)],
        sem=("arbitrary",), comm=comm)


MLA_SCALE = (MLA_NOPE + MLA_ROPE) ** -0.5
MLA_TILE = 512


def _tile_pairs(nt, q_major):
    pairs = [(i, j) for i in range(nt) for j in range(i + 1)] if q_major else \
            [(i, j) for j in range(nt) for i in range(j, nt)]
    return jnp.asarray([p[0] for p in pairs], jnp.int32), jnp.asarray([p[1] for p in pairs], jnp.int32)


def _diag_mask(t):
    return lax.broadcasted_iota(jnp.int32, (t, t), 1) <= lax.broadcasted_iota(jnp.int32, (t, t), 0)


def _mla_fwd(q_cat, k_cat, v_b, *, comm=()):
    nh, s, _ = q_cat.shape
    t = _tile(s, MLA_TILE)
    qi, kj = _tile_pairs(s // t, True)

    def body(qi_ref, kj_ref, q_ref, k_ref, v_ref, o_ref, lse_ref, m_sc, l_sc, acc_sc):
        i, j = qi_ref[pl.program_id(0)], kj_ref[pl.program_id(0)]

        @pl.when(j == 0)
        def _():
            m_sc[...] = jnp.full_like(m_sc, NEG_INF)
            l_sc[...] = jnp.zeros_like(l_sc)
            acc_sc[...] = jnp.zeros_like(acc_sc)

        def update(diagonal):
            for h in range(nh):
                sc = lax.dot_general(q_ref[h], k_ref[h], NT_DIMS, preferred_element_type=F32) * MLA_SCALE
                if diagonal:
                    sc = jnp.where(_diag_mask(t), sc, NEG_INF)
                m_old = m_sc[h]
                m_new = jnp.maximum(m_old, jnp.max(sc, axis=-1, keepdims=True))
                alpha = jnp.exp(m_old - m_new)
                p = jnp.exp(sc - m_new)
                l_sc[h] = alpha * l_sc[h] + jnp.sum(p, axis=-1, keepdims=True)
                acc_sc[h] = alpha * acc_sc[h] + jnp.dot(p.astype(BF16), v_ref[:, MLA_V * h:MLA_V * (h + 1)],
                                                        preferred_element_type=F32)
                m_sc[h] = m_new

        @pl.when(j < i)
        def _():
            update(False)

        @pl.when(j == i)
        def _():
            update(True)
            for h in range(nh):
                o_ref[:, MLA_V * h:MLA_V * (h + 1)] = acc_sc[h] * (1.0 / l_sc[h])
                lse_ref[h] = m_sc[h] + jnp.log(l_sc[h])

    return _pcall(
        body, (q_cat, k_cat, v_b), name="mla_fwd", grid=(qi.shape[0],), prefetch=(qi, kj),
        in_specs=[pl.BlockSpec((nh, t, 256), lambda p, qi, kj: (0, qi[p], 0)),
                  pl.BlockSpec((nh, t, 256), lambda p, qi, kj: (0, kj[p], 0)),
                  pl.BlockSpec((t, nh * MLA_V), lambda p, qi, kj: (kj[p], 0))],
        out_specs=[pl.BlockSpec((t, nh * MLA_V), lambda p, qi, kj: (qi[p], 0)),
                   pl.BlockSpec((nh, t, 1), lambda p, qi, kj: (0, qi[p], 0))],
        out_shape=[jax.ShapeDtypeStruct((s, nh * MLA_V), F32), jax.ShapeDtypeStruct((nh, s, 1), F32)],
        scratch_shapes=[pltpu.VMEM((nh, t, 1), F32), pltpu.VMEM((nh, t, 1), F32), pltpu.VMEM((nh, t, MLA_V), F32)],
        sem=("arbitrary",), comm=comm)


def _mla_bwd(q_cat, k_cat, v_b, y_b, lse, d_y, *, comm=()):
    nh, s, _ = q_cat.shape
    t = _tile(s, MLA_TILE)
    nt = s // t
    hp = 2
    wv = hp * MLA_V
    col0 = (SWA_Q_HEADS * SWA_HEAD_DIM) // wv
    qi, kj = _tile_pairs(nt, False)

    def body(qi_ref, kj_ref, q_ref, k_ref, v_ref, y_ref, lse_ref, dy_ref, dq_ref, dk_ref, dv_ref, dk_sc, dv_sc):
        step = pl.program_id(1)
        i, j = qi_ref[step], kj_ref[step]

        @pl.when(step == 0)
        def _():
            dq_ref[...] = jnp.zeros_like(dq_ref)

        @pl.when(i == j)
        def _():
            dk_sc[...] = jnp.zeros_like(dk_sc)
            dv_sc[...] = jnp.zeros_like(dv_sc)

        def update(diagonal):
            rows = pl.ds(pl.multiple_of(i * t, t), t)
            for h in range(hp):
                q, k = q_ref[h], k_ref[h]
                cols = slice(MLA_V * h, MLA_V * (h + 1))
                do = dy_ref[:, cols]
                do_b = do.astype(BF16)
                sc = lax.dot_general(q, k, NT_DIMS, preferred_element_type=F32) * MLA_SCALE
                p = jnp.exp(sc - lse_ref[h])
                if diagonal:
                    p = jnp.where(_diag_mask(t), p, 0.0)
                delta = jnp.sum(do * y_ref[:, cols], axis=-1, keepdims=True)
                dv_sc[h] += lax.dot_general(p.astype(BF16), do_b, TN_DIMS, preferred_element_type=F32)
                dp = lax.dot_general(do_b, v_ref[:, cols], NT_DIMS, preferred_element_type=F32)
                ds_b = (p * (dp - delta) * MLA_SCALE).astype(BF16)
                dk_sc[h] += lax.dot_general(ds_b, q, TN_DIMS, preferred_element_type=F32)
                dq_ref[h, rows, :] += jnp.dot(ds_b, k, preferred_element_type=F32)

        @pl.when(i > j)
        def _():
            update(False)

        @pl.when(i == j)
        def _():
            update(True)

        @pl.when(i == nt - 1)
        def _():
            dk_ref[...] = dk_sc[...]
            for h in range(hp):
                dv_ref[:, MLA_V * h:MLA_V * (h + 1)] = dv_sc[h]

    return _pcall(
        body, (q_cat, k_cat, v_b, y_b, lse, d_y), name="mla_bwd", grid=(nh // hp, qi.shape[0]), prefetch=(qi, kj),
        in_specs=[pl.BlockSpec((hp, t, 256), lambda g, p, qi, kj: (g, qi[p], 0)),
                  pl.BlockSpec((hp, t, 256), lambda g, p, qi, kj: (g, kj[p], 0)),
                  pl.BlockSpec((t, wv), lambda g, p, qi, kj: (kj[p], g)),
                  pl.BlockSpec((t, wv), lambda g, p, qi, kj: (qi[p], g)),
                  pl.BlockSpec((hp, t, 1), lambda g, p, qi, kj: (g, qi[p], 0)),
                  pl.BlockSpec((t, wv), lambda g, p, qi, kj: (qi[p], col0 + g))],
        out_specs=[pl.BlockSpec((hp, s, 256), lambda g, p, qi, kj: (g, 0, 0)),
                   pl.BlockSpec((hp, t, 256), lambda g, p, qi, kj: (g, kj[p], 0)),
                   pl.BlockSpec((t, wv), lambda g, p, qi, kj: (kj[p], g))],
        out_shape=[jax.ShapeDtypeStruct((nh, s, 256), F32), jax.ShapeDtypeStruct((nh, s, 256), F32),
                   jax.ShapeDtypeStruct((s, nh * MLA_V), F32)],
        scratch_shapes=[pltpu.VMEM((hp, t, 256), F32), pltpu.VMEM((hp, t, MLA_V), F32)],
        sem=("arbitrary", "arbitrary"), comm=comm)


MEM_SCALE = MEM_DIM ** -0.5


def _mem_kv_fwd(mem, g_mem, w_memkv, g_mk):
    m_len = mem.shape[0]

    def body(mem_ref, g_ref, w_ref, gk_ref, mn_ref, kv_ref, kn_ref, v_ref):
        mn, _, _ = _norm_full(mem_ref[...], g_ref[...])
        mn_b = mn.astype(BF16)
        mn_ref[...] = mn_b
        kv = jnp.dot(mn_b, w_ref[...], preferred_element_type=F32)
        kv_ref[...] = kv
        for h in range(MEM_HEADS):
            kn, _, _ = _norm_full(kv[:, 128 * h:128 * (h + 1)], gk_ref[...])
            kn_ref[:, 128 * h:128 * (h + 1)] = kn.astype(BF16)
        v_ref[...] = kv[:, 512:1024].astype(BF16)

    return pl.pallas_call(
        body, name="mem_kv_fwd",
        out_shape=[jax.ShapeDtypeStruct((m_len, D_MODEL), BF16), jax.ShapeDtypeStruct((m_len, 1024), F32),
                   jax.ShapeDtypeStruct((m_len, 512), BF16), jax.ShapeDtypeStruct((m_len, 512), BF16)],
        compiler_params=_params(),
    )(mem, g_mem, w_memkv, g_mk)


def _mem_kv_bwd(mem, g_mem, w_memkv, g_mk, mn_b, kv, d_kn, d_v):
    m_len = mem.shape[0]

    def body(mem_ref, g_ref, w_ref, gk_ref, mn_ref, kv_ref, dkn_ref, dv_ref, dw_ref, dgmem_ref, dgk_ref):
        parts = []
        dgk = jnp.zeros((1, LANES), F32)
        for h in range(MEM_HEADS):
            _, xhat, r = _norm_full(kv_ref[:, 128 * h:128 * (h + 1)], gk_ref[...])
            dx, dgh = _norm_full_bwd(dkn_ref[:, 128 * h:128 * (h + 1)], gk_ref[...], xhat, r)
            parts.append(dx)
            dgk = dgk + dgh
        d_kv = jnp.concatenate(parts + [dv_ref[...]], axis=1).astype(BF16)
        dw_ref[...] = lax.dot_general(mn_ref[...], d_kv, TN_DIMS, preferred_element_type=F32)
        d_mn = lax.dot_general(d_kv, w_ref[...], NT_DIMS, preferred_element_type=F32)
        _, xhat, _ = _norm_full(mem_ref[...], g_ref[...])
        dgmem_ref[...] = jnp.sum(d_mn * xhat, axis=0, keepdims=True)
        dgk_ref[...] = dgk

    return pl.pallas_call(
        body, name="mem_kv_bwd",
        out_shape=[jax.ShapeDtypeStruct((D_MODEL, 1024), F32), jax.ShapeDtypeStruct((1, D_MODEL), F32),
                   jax.ShapeDtypeStruct((1, LANES), F32)],
        compiler_params=_params(),
    )(mem, g_mem, w_memkv, g_mk, mn_b, kv, d_kn, d_v)


def _mem_probs(q_h, k_h):
    sc = lax.dot_general(q_h, k_h, NT_DIMS, preferred_element_type=F32) * MEM_SCALE
    e = jnp.exp(sc - jnp.max(sc, axis=-1, keepdims=True))
    return e * (1.0 / jnp.sum(e, axis=-1, keepdims=True))


def _mem_attn_fwd(qm, km, vm, *, tm=512):
    s = qm.shape[0]
    tm = _tile(s, tm)
    m_len = km.shape[0]

    def body(q_ref, k_ref, v_ref, o_ref):
        for h in range(MEM_HEADS):
            sl = slice(128 * h, 128 * (h + 1))
            p = _mem_probs(q_ref[:, sl], k_ref[:, sl])
            o_ref[:, sl] = jnp.dot(p.astype(BF16), v_ref[:, sl], preferred_element_type=F32)

    kvspec = pl.BlockSpec((m_len, 512), lambda i: (0, 0))
    return pl.pallas_call(
        body, name="mem_attn_fwd", grid=(s // tm,),
        in_specs=[pl.BlockSpec((tm, 512), lambda i: (i, 0)), kvspec, kvspec],
        out_specs=pl.BlockSpec((tm, 512), lambda i: (i, 0)),
        out_shape=jax.ShapeDtypeStruct((s, 512), F32),
        compiler_params=_params(("parallel",)),
    )(qm, km, vm)


def _mem_attn_bwd(qm, km, vm, y_m, d_y, *, tm=512):
    s = qm.shape[0]
    tm = _tile(s, tm)
    m_len = km.shape[0]
    col0 = (SWA_Q_HEADS * SWA_HEAD_DIM + MLA_HEADS * MLA_V) // 512

    def body(q_ref, k_ref, v_ref, y_ref, dy_ref, dq_ref, dk_ref, dv_ref):
        @pl.when(pl.program_id(0) == 0)
        def _():
            dk_ref[...] = jnp.zeros_like(dk_ref)
            dv_ref[...] = jnp.zeros_like(dv_ref)

        for h in range(MEM_HEADS):
            sl = slice(128 * h, 128 * (h + 1))
            q_h, k_h = q_ref[:, sl], k_ref[:, sl]
            do = dy_ref[:, sl]
            do_b = do.astype(BF16)
            p = _mem_probs(q_h, k_h)
            delta = jnp.sum(do * y_ref[:, sl], axis=-1, keepdims=True)
            dv_ref[:, sl] += lax.dot_general(p.astype(BF16), do_b, TN_DIMS, preferred_element_type=F32)
            dp = lax.dot_general(do_b, v_ref[:, sl], NT_DIMS, preferred_element_type=F32)
            ds_b = (p * (dp - delta) * MEM_SCALE).astype(BF16)
            dq_ref[:, sl] = jnp.dot(ds_b, k_h, preferred_element_type=F32)
            dk_ref[:, sl] += lax.dot_general(ds_b, q_h, TN_DIMS, preferred_element_type=F32)

    kvspec = pl.BlockSpec((m_len, 512), lambda i: (0, 0))
    row = pl.BlockSpec((tm, 512), lambda i: (i, 0))
    return pl.pallas_call(
        body, name="mem_attn_bwd", grid=(s // tm,),
        in_specs=[row, kvspec, kvspec, row, pl.BlockSpec((tm, 512), lambda i: (i, col0))],
        out_specs=[row, kvspec, kvspec],
        out_shape=[jax.ShapeDtypeStruct((s, 512), F32), jax.ShapeDtypeStruct((m_len, 512), F32),
                   jax.ShapeDtypeStruct((m_len, 512), F32)],
        compiler_params=_params(("arbitrary",)),
    )(qm, km, vm, y_m, d_y)


def _ffn_gate_up(fn, w_gate, w_up, *, tm=512):
    s, d = fn.shape
    nsp, _, tf = w_gate.shape
    f = nsp * tf
    tm = _tile(s, tm)

    def body(x_ref, wg_ref, wu_ref, g_ref, u_ref, a_ref):
        x = x_ref[...]
        gate = jnp.dot(x, wg_ref[...], preferred_element_type=F32)
        up = jnp.dot(x, wu_ref[...], preferred_element_type=F32)
        g_ref[...] = gate.astype(BF16)
        u_ref[...] = up.astype(BF16)
        a_ref[...] = (gate * (1.0 / (1.0 + jnp.exp(-gate))) * up).astype(BF16)

    wspec = pl.BlockSpec((None, d, tf), lambda j, i: (j, 0, 0))
    ospec = pl.BlockSpec((tm, tf), lambda j, i: (i, j))
    osh = jax.ShapeDtypeStruct((s, f), BF16)
    return pl.pallas_call(
        body, name="ffn_gate_up", grid=(nsp, s // tm),
        in_specs=[pl.BlockSpec((tm, d), lambda j, i: (i, 0)), wspec, wspec],
        out_specs=[ospec, ospec, ospec], out_shape=[osh, osh, osh],
        compiler_params=_params(("parallel", "parallel")),
    )(fn, w_gate, w_up)


def _ffn_bwd_act(d_out, w_down, gate, up, *, tm=512, tf=1408, comm=()):
    s, d = d_out.shape
    f = w_down.shape[0]
    tm, tf = _tile(s, tm), _tile(f, tf)

    def body(do_ref, wd_ref, g_ref, u_ref, dg_ref, du_ref):
        d_act = lax.dot_general(do_ref[...].astype(BF16), wd_ref[...], NT_DIMS, preferred_element_type=F32)
        gate = g_ref[...].astype(F32)
        sig = 1.0 / (1.0 + jnp.exp(-gate))
        du_ref[...] = (d_act * (gate * sig)).astype(BF16)
        dg_ref[...] = (d_act * u_ref[...].astype(F32) * (sig * (1.0 + gate * (1.0 - sig)))).astype(BF16)

    ospec = pl.BlockSpec((tm, tf), lambda j, i: (i, j))
    osh = jax.ShapeDtypeStruct((s, f), BF16)
    return _pcall(
        body, (d_out, w_down, gate, up), name="ffn_bwd_act", grid=(f // tf, s // tm),
        in_specs=[pl.BlockSpec((tm, d), lambda j, i: (i, 0)), pl.BlockSpec((tf, d), lambda j, i: (j, 0)), ospec, ospec],
        out_specs=[ospec, ospec], out_shape=[osh, osh], sem=("parallel", "parallel"), comm=comm)


def _loss_head(out, target, *, tm=512):
    s, d = out.shape
    tm = _tile(s, tm)

    def body(o_ref, t_ref, d_ref, l_ref):
        err = o_ref[...] - t_ref[...]
        d_ref[...] = err * (1.0 / d)
        part = 0.5 * jnp.sum(jnp.mean(err * err, axis=-1, keepdims=True), axis=0, keepdims=True)
        part = jnp.broadcast_to(part, (1, LANES))

        @pl.when(pl.program_id(0) == 0)
        def _():
            l_ref[...] = part

        @pl.when(pl.program_id(0) > 0)
        def _():
            l_ref[...] += part

    row = pl.BlockSpec((tm, d), lambda i: (i, 0))
    return pl.pallas_call(
        body, name="loss_head", grid=(s // tm,),
        in_specs=[row, row], out_specs=[row, pl.BlockSpec((1, LANES), lambda i: (0, 0))],
        out_shape=[jax.ShapeDtypeStruct((s, d), F32), jax.ShapeDtypeStruct((1, LANES), F32)],
        compiler_params=_params(("arbitrary",)),
    )(out, target)


def _cols(g4):
    return jnp.concatenate([g4[k] for k in range(N_CHIPS)], axis=1)


def _full_w_in(g4):
    nat = _cols(g4)
    pad = jnp.zeros((nat.shape[0], IN_PAD - IN_WIDTH), nat.dtype)
    return jnp.concatenate([nat[:, :2304], nat[:, 2368:], nat[:, 2304:2368], pad], axis=1)


def _shards_w_in(dwp):
    nat = jnp.concatenate([dwp[:, :2304], dwp[:, C_KR:C_KR + 64], dwp[:, 2304:C_KR]], axis=1)
    per = IN_WIDTH // N_CHIPS
    return jnp.stack([nat[:, per * k:per * (k + 1)] for k in range(N_CHIPS)])


def _full_heads(g4, first):
    return jnp.concatenate([g4[k][:, :first] for k in range(N_CHIPS)] + [g4[k][:, first:] for k in range(N_CHIPS)], axis=1)


def _shards_heads(dwp, first, rest):
    base = N_CHIPS * first
    return jnp.stack([jnp.concatenate([dwp[:, first * k:first * (k + 1)], dwp[:, base + rest * k:base + rest * (k + 1)]], axis=1)
                      for k in range(N_CHIPS)])


def _rope_tables(pos):
    inv_freq = ROPE_THETA ** (-jnp.arange(0, MLA_ROPE, 2, dtype=F32) / MLA_ROPE)
    ang = pos.astype(F32)[:, None] * inv_freq
    cos, sin = jnp.cos(ang), jnp.sin(ang)
    return jnp.tile(cos, (1, 4)), jnp.concatenate([-sin, sin, -sin, sin], axis=1)


def _gain_table(sp):
    two = lambda v: jnp.tile(v, (1, 2))
    rows = [two(sp["swa_q_norm_g"]), two(sp["swa_k_norm_g"]), sp["mla_qn_norm_g"], two(sp["mla_qr_norm_g"]),
            sp["mla_kn_norm_g"], two(sp["mla_kr_norm_g"]), sp["mem_q_norm_g"], jnp.zeros((1, LANES), F32)]
    return jnp.concatenate(rows, axis=0)


CHIP_DISTANCES = (1, 2, 3)


def _place():
    x, y, c = lax.axis_index("x"), lax.axis_index("y"), lax.axis_index("c")
    return x, y, c, 2 * x + y


def _chip_at(x, y, d):
    px = 1 - x if d & 2 else x
    py = 1 - y if d & 1 else y
    return px, py, 2 * px + py


def _row_tile(rows, want=512, mult=8):
    t = min(rows, want)
    t -= t % mult
    while rows % t:
        t -= mult
    return t


def _cast_into_slot(w, meta, *, name):
    rows, cols = w.shape
    tr = _row_tile(rows, 512, 16)

    def body(meta_ref, w_ref, o_ref):
        o_ref[...] = w_ref[...].astype(BF16)

    grid_spec = pltpu.PrefetchScalarGridSpec(
        num_scalar_prefetch=1, grid=(rows // tr,),
        in_specs=[pl.BlockSpec((tr, cols), lambda i, m: (i, 0))],
        out_specs=pl.BlockSpec((None, tr, cols), lambda i, m: (m[0], i, 0)))
    return pl.pallas_call(
        body, name=name, grid_spec=grid_spec,
        out_shape=jax.ShapeDtypeStruct((N_CHIPS, rows, cols), BF16),
        compiler_params=_params(("parallel",)),
    )(meta, w)


def _remote(src, dst, ssem, rsem, i, device):
    return pltpu.make_async_remote_copy(src_ref=src, dst_ref=dst, send_sem=ssem.at[i], recv_sem=rsem.at[i],
                                        device_id=device, device_id_type=MESH)


def _symmetric_stage(ins, out_shapes, aliases, n_sem, copies):
    def issue(i_refs, o_refs, ssem, rsem):
        for send, _ in copies(i_refs, o_refs, ssem, rsem):
            send.start()

    def wait(i_refs, o_refs, ssem, rsem):
        pairs = copies(i_refs, o_refs, ssem, rsem)
        for _, arrival in pairs:
            arrival.wait_recv()
        for send, _ in pairs:
            send.wait_send()

    return _Stage(ins, out_shapes, aliases, n_sem, issue, wait)


def _gather_stage(slots, leg):
    n = len(slots)

    def copies(_, outs, ssem, rsem):
        x, y, c, k_me = _place()
        pairs = []
        for w in range(n):
            half = outs[w].shape[1] // 2
            slab = lambda k, cc, w=w, half=half: outs[w].at[k, pl.ds(cc * half, half)]
            for d in CHIP_DISTANCES:
                px, py, k_src = _chip_at(x, y, d)
                i = 3 * w + d - 1
                if leg == "ici":
                    pairs.append((_remote(slab(k_me, c), slab(k_me, c), ssem, rsem, i, (px, py, c)),
                                  _remote(slab(k_src, c), slab(k_src, c), ssem, rsem, i, (x, y, c))))
                else:
                    pairs.append((_remote(slab(k_src, c), slab(k_src, c), ssem, rsem, i, (x, y, 1 - c)),
                                  _remote(slab(k_src, 1 - c), slab(k_src, 1 - c), ssem, rsem, i, (x, y, c))))
        return pairs

    shapes = [jax.ShapeDtypeStruct(s.shape, s.dtype) for s in slots]
    return _symmetric_stage(slots, shapes, {w: w for w in range(n)}, 3 * n, copies)


def _halves_stage(grads):
    n = len(grads)

    def copies(ins, outs, ssem, rsem):
        x, y, c, _ = _place()
        pairs = []
        for w in range(n):
            half = ins[w].shape[1] // 2
            pairs.append((_remote(ins[w].at[:, pl.ds((1 - c) * half, half)], outs[w], ssem, rsem, w, (x, y, 1 - c)),
                          _remote(outs[w], outs[w], ssem, rsem, w, (x, y, c))))
        return pairs

    shapes = [jax.ShapeDtypeStruct((N_CHIPS, g.shape[1] // 2, g.shape[2]), g.dtype) for g in grads]
    return _symmetric_stage(grads, shapes, {}, n, copies)


def _chips_stage(parts):
    n = len(parts)

    def copies(ins, outs, ssem, rsem):
        x, y, c, _ = _place()
        pairs = []
        for w in range(n):
            for d in CHIP_DISTANCES:
                px, py, _ = _chip_at(x, y, d)
                i = 3 * w + d - 1
                pairs.append((_remote(ins[w].at[d - 1], outs[w].at[d - 1], ssem, rsem, i, (px, py, c)),
                              _remote(outs[w].at[d - 1], outs[w].at[d - 1], ssem, rsem, i, (x, y, c))))
        return pairs

    shapes = [jax.ShapeDtypeStruct(p.shape, p.dtype) for p in parts]
    return _symmetric_stage(parts, shapes, {}, 3 * n, copies)


def _swap_stage(totals):
    n = len(totals)

    def copies(ins, outs, ssem, rsem):
        x, y, c, _ = _place()
        return [(_remote(ins[w], outs[w], ssem, rsem, w, (x, y, 1 - c)),
                 _remote(outs[w], outs[w], ssem, rsem, w, (x, y, c))) for w in range(n)]

    shapes = [jax.ShapeDtypeStruct(t.shape, t.dtype) for t in totals]
    return _symmetric_stage(totals, shapes, {}, n, copies)


def _run_stage(st, *, name):
    n_in, n_out = len(st.ins), len(st.out_shapes)

    def body(*refs):
        ins, outs, (ssem, rsem) = refs[:n_in], refs[n_in:n_in + n_out], refs[n_in + n_out:]
        st.issue(ins, outs, ssem, rsem)
        st.wait(ins, outs, ssem, rsem)

    sem = pltpu.SemaphoreType.DMA
    return list(pl.pallas_call(
        body, name=name, in_specs=[ANY] * n_in, out_specs=[ANY] * n_out, out_shape=st.out_shapes,
        input_output_aliases=st.aliases, scratch_shapes=[sem((st.n_sem,)), sem((st.n_sem,))],
    )(*st.ins))


def _add_pair(meta, g4, recv, *, name):
    nsh, rows, cols = g4.shape
    half = rows // 2
    tr = _row_tile(half, 128 if cols > 1024 else 256, 16)
    nt = half // tr

    def body(meta_ref, g0, g1, g2, g3, r0, r1, r2, r3, own_ref, oth_ref):
        own_ref[...] = g0[...] + r0[...]
        for d, (g, r) in enumerate(((g1, r1), (g2, r2), (g3, r3))):
            oth_ref[d] = (g[...] + r[...]).astype(BF16)

    blk = (None, tr, cols)
    gspec = lambda d: pl.BlockSpec(blk, lambda i, m: (jnp.bitwise_xor(m[0], d), m[1] * nt + i, 0))
    rspec = lambda d: pl.BlockSpec(blk, lambda i, m: (jnp.bitwise_xor(m[0], d), i, 0))
    grid_spec = pltpu.PrefetchScalarGridSpec(
        num_scalar_prefetch=1, grid=(nt,),
        in_specs=[gspec(d) for d in range(nsh)] + [rspec(d) for d in range(nsh)],
        out_specs=[pl.BlockSpec((tr, cols), lambda i, m: (i, 0)), pl.BlockSpec((3, tr, cols), lambda i, m: (0, i, 0))])
    return pl.pallas_call(
        body, name=name, grid_spec=grid_spec,
        out_shape=[jax.ShapeDtypeStruct((half, cols), F32), jax.ShapeDtypeStruct((3, half, cols), BF16)],
        compiler_params=_params(("parallel",)),
    )(meta, g4, g4, g4, g4, recv, recv, recv, recv)


def _add_chips(own, recv, *, name):
    half, cols = own.shape
    tr = _row_tile(half, 256, 16)

    def body(p_ref, r_ref, o_ref):
        o_ref[...] = ((p_ref[...] + r_ref[0].astype(F32)) + r_ref[1].astype(F32)) + r_ref[2].astype(F32)

    return pl.pallas_call(
        body, name=name, grid=(half // tr,),
        in_specs=[pl.BlockSpec((tr, cols), lambda i: (i, 0)), pl.BlockSpec((3, tr, cols), lambda i: (0, i, 0))],
        out_specs=pl.BlockSpec((tr, cols), lambda i: (i, 0)),
        out_shape=jax.ShapeDtypeStruct((half, cols), F32),
        compiler_params=_params(("parallel",)),
    )(own, recv)


def _adamw_math(w, g, m, v):
    m = ADAM_B1 * m + (1.0 - ADAM_B1) * g
    v = ADAM_B2 * v + (1.0 - ADAM_B2) * (g * g)
    m_hat = m / (1.0 - ADAM_B1 ** ADAM_STEP)
    v_hat = v / (1.0 - ADAM_B2 ** ADAM_STEP)
    delta = -ADAM_LR * (m_hat / (jnp.sqrt(v_hat) + ADAM_EPS) + ADAM_WD * w)
    return delta, m, v


def _adamw(meta, w, g_mine, g_theirs, m, v, *, name):
    rows, cols = w.shape
    half = rows // 2
    tr = _row_tile(half, 256)
    nt = half // tr

    def body(meta_ref, w_ref, a_ref, b_ref, m_ref, v_ref, g_ref, d_ref, mo_ref, vo_ref):
        is_mine = (pl.program_id(0) // nt) == meta_ref[1]
        g = jnp.where(is_mine, a_ref[...], b_ref[...])
        g_ref[...] = g
        d_ref[...], mo_ref[...], vo_ref[...] = _adamw_math(w_ref[...], g, m_ref[...], v_ref[...])

    blk = pl.BlockSpec((tr, cols), lambda i, mt: (i, 0))
    mine = pl.BlockSpec((tr, cols), lambda i, mt: (jnp.where(i // nt == mt[1], i % nt, 0), 0))
    theirs = pl.BlockSpec((tr, cols), lambda i, mt: (jnp.where(i // nt == mt[1], 0, i % nt), 0))
    sh = jax.ShapeDtypeStruct((rows, cols), F32)
    grid_spec = pltpu.PrefetchScalarGridSpec(
        num_scalar_prefetch=1, grid=(rows // tr,),
        in_specs=[blk, mine, theirs, blk, blk], out_specs=[blk] * 4)
    return pl.pallas_call(
        body, name=name, grid_spec=grid_spec, out_shape=[sh] * 4,
        compiler_params=_params(("arbitrary",)),
    )(meta, w, g_mine, g_theirs, m, v)


N_DEVICES = 8


def _small_step(g_pack, w_pack, m_pack, v_pack):
    rows = g_pack.shape[0]

    def body(g_ref, w_ref, m_ref, v_ref, sum_ref, d_ref, mo_ref, vo_ref, slots, ssem, rsem):
        x, y, c, _ = _place()
        me = 4 * x + 2 * y + c
        slots[me] = g_ref[...]
        copies = []
        for r in range(1, N_DEVICES):
            px = 1 - x if r & 4 else x
            py = 1 - y if r & 2 else y
            pc = 1 - c if r & 1 else c
            copies.append(pltpu.make_async_remote_copy(
                src_ref=g_ref, dst_ref=slots.at[me], send_sem=ssem.at[r - 1], recv_sem=rsem.at[r - 1],
                device_id=(px, py, pc), device_id_type=MESH))
        for cp in copies:
            cp.start()
        for r in range(1, N_DEVICES):
            src = jnp.bitwise_xor(me, r)
            pltpu.make_async_remote_copy(
                src_ref=g_ref, dst_ref=slots.at[src], send_sem=ssem.at[r - 1], recv_sem=rsem.at[r - 1],
                device_id=(x, y, c), device_id_type=MESH).wait_recv()
        for cp in copies:
            cp.wait_send()
        total = slots[0]
        for k in range(1, N_DEVICES):
            total = total + slots[k]
        sum_ref[...] = total
        d_ref[...], mo_ref[...], vo_ref[...] = _adamw_math(w_ref[...], total, m_ref[...], v_ref[...])

    sh = jax.ShapeDtypeStruct((rows, LANES), F32)
    vm = pl.BlockSpec(memory_space=pltpu.VMEM)
    return pl.pallas_call(
        body, name="small_allreduce_adamw",
        in_specs=[vm] * 4, out_specs=[vm] * 4, out_shape=[sh] * 4,
        scratch_shapes=[pltpu.VMEM((N_DEVICES, rows, LANES), F32),
                        pltpu.SemaphoreType.DMA((N_DEVICES - 1,)), pltpu.SemaphoreType.DMA((N_DEVICES - 1,))],
    )(g_pack, w_pack, m_pack, v_pack)


WEIGHTS = ("attn_norm_g", "w_in", "swa_q_norm_g", "swa_k_norm_g", "swa_sinks", "mla_cq_norm_g", "mla_ckv_norm_g",
           "w_uq", "w_ukv", "mla_qn_norm_g", "mla_qr_norm_g", "mla_kn_norm_g", "mla_kr_norm_g", "mem_norm_g",
           "w_mem_kv", "mem_q_norm_g", "mem_k_norm_g", "w_out", "ffn_norm_g", "w_gate", "w_up", "w_down")
BIG = ("w_in", "w_uq", "w_ukv", "w_mem_kv", "w_out", "w_gate", "w_up", "w_down")
SMALL = tuple(n for n in WEIGHTS if n not in BIG)
PACK_UNIT = 8 * LANES


def _pack(parts):
    out = []
    for p in parts:
        n = p.shape[1]
        padded = -(-n // PACK_UNIT) * PACK_UNIT
        out.append(jnp.pad(p, ((0, 0), (0, padded - n))).reshape(padded // LANES, LANES))
    return jnp.concatenate(out, axis=0)


def _unpack(buf, sizes):
    out, row = [], 0
    for n in sizes:
        rows = -(-n // PACK_UNIT) * 8
        out.append(buf[row:row + rows].reshape(1, rows * LANES)[:, :n])
        row += rows
    return out


def kernel(x, mem, positions, attn_norm_g, w_in, swa_q_norm_g, swa_k_norm_g, swa_sinks, mla_cq_norm_g, mla_ckv_norm_g, w_uq, w_ukv, mla_qn_norm_g, mla_qr_norm_g, mla_kn_norm_g, mla_kr_norm_g, mem_norm_g, w_mem_kv, mem_q_norm_g, mem_k_norm_g, w_out, ffn_norm_g, w_gate, w_up, w_down, loss_target, m_attn_norm_g, m_w_in, m_swa_q_norm_g, m_swa_k_norm_g, m_swa_sinks, m_mla_cq_norm_g, m_mla_ckv_norm_g, m_w_uq, m_w_ukv, m_mla_qn_norm_g, m_mla_qr_norm_g, m_mla_kn_norm_g, m_mla_kr_norm_g, m_mem_norm_g, m_w_mem_kv, m_mem_q_norm_g, m_mem_k_norm_g, m_w_out, m_ffn_norm_g, m_w_gate, m_w_up, m_w_down, v_attn_norm_g, v_w_in, v_swa_q_norm_g, v_swa_k_norm_g, v_swa_sinks, v_mla_cq_norm_g, v_mla_ckv_norm_g, v_w_uq, v_w_ukv, v_mla_qn_norm_g, v_mla_qr_norm_g, v_mla_kn_norm_g, v_mla_kr_norm_g, v_mem_norm_g, v_w_mem_kv, v_mem_q_norm_g, v_mem_k_norm_g, v_w_out, v_ffn_norm_g, v_w_gate, v_w_up, v_w_down):
    given = dict(locals())
    wts = {n: given[n] for n in WEIGHTS}
    mom_m = {n: given["m_" + n] for n in WEIGHTS}
    mom_v = {n: given["v_" + n] for n in WEIGHTS}

    mx, my, mc = lax.axis_index("x"), lax.axis_index("y"), lax.axis_index("c")
    meta = jnp.stack([2 * mx + my, mc]).astype(jnp.int32)
    x, mem, pos, target = x[0], mem[0], positions[0], loss_target[0]
    sp = {n: wts[n] for n in SMALL}
    s = x.shape[0]
    cos_t, sin_t = _rope_tables(pos)
    pos_f = pos.astype(F32)
    pos_col, pos_row = pos_f.reshape(s, 1), pos_f.reshape(1, s)
    g128 = _gain_table(sp)
    sinks = sp["swa_sinks"].reshape(SWA_Q_HEADS)
    gcq, gckv = sp["mla_cq_norm_g"], sp["mla_ckv_norm_g"]
    gs = {}

    slot = {n: _cast_into_slot(wts[n][0], meta, name="cast_" + n) for n in BIG}
    first = _run_stage(_gather_stage([slot["w_in"], slot["w_uq"], slot["w_ukv"]], "ici"), name="gather_first_ici")
    first = _run_stage(_gather_stage(first, "d2d"), name="gather_first_d2d")
    w_in_f, w_uq_f, w_ukv_f = _full_w_in(first[0]), _full_heads(first[1], MLA_NOPE), _full_heads(first[2], MLA_NOPE)

    hn = _rms_fwd(x, sp["attn_norm_g"], name="attn_norm_fwd")
    proj, [mid] = _matmul(hn, w_in_f, name="in_proj",
                          comm=[_gather_stage([slot["w_mem_kv"], slot["w_out"]], "ici")])
    (qa, ka, va, q_cat, k_cat, v_b, qm), [mid] = _attn_prep_fwd(
        proj, g128, gcq, gckv, w_uq_f, w_ukv_f, cos_t, sin_t, comm=[_gather_stage(mid, "d2d")])
    w_mem_kv_f = mid[0].reshape(D_MODEL, 2 * MEM_HEADS * MEM_DIM)
    w_out_f = mid[1].reshape(D_MODEL, D_MODEL)
    mn_b, kv_m, km, vm = _mem_kv_fwd(mem, sp["mem_norm_g"], w_mem_kv_f, sp["mem_k_norm_g"])
    y_a, [wg] = _swa_fwd(qa, ka, va, pos_col, pos_row, sinks, comm=[_gather_stage([slot["w_gate"]], "ici")])
    (y_b, lse), [wud, wg] = _mla_fwd(
        q_cat, k_cat, v_b, comm=[_gather_stage([slot["w_up"], slot["w_down"]], "ici"), _gather_stage(wg, "d2d")])
    y_m = _mem_attn_fwd(qm, km, vm)
    y = jnp.concatenate([y_a, y_b, y_m], axis=1)
    h1, [wud] = _matmul(y, w_out_f, add=x, name="out_proj", comm=[_gather_stage(wud, "d2d")])
    w_gate_f, w_up_f, w_down_f = wg[0], wud[0], wud[1].reshape(D_FF, D_MODEL)
    fn = _rms_fwd(h1, sp["ffn_norm_g"], name="ffn_norm_fwd")
    gate, up, act = _ffn_gate_up(fn, w_gate_f, w_up_f)
    out = _matmul(act, w_down_f, add=h1, name="down_proj", tk=1408)
    d_out, loss_tile = _loss_head(out, target)

    add_pair = lambda n, g4, r: _add_pair(meta, g4, r, name="grad_add_pair_" + n)
    add_chips = lambda n, own, r: _add_chips(own, r, name="grad_add_chips_" + n)
    mine, theirs = {}, {}

    dw_down = _matmul(act, d_out, ta=True, name="dw_down", tm=1408, tn=1024, tk=1024)
    dw_down = dw_down.reshape(N_CHIPS, D_FF // N_CHIPS, D_MODEL)
    (d_gate, d_up), [[r]] = _ffn_bwd_act(d_out, w_down_f, gate, up, comm=[_halves_stage([dw_down])])
    own_d, oth_d = add_pair("w_down", dw_down, r)
    dw_gate, [[r]] = _matmul(fn, d_gate, ta=True, name="dw_gate", tk=1024, tn=D_FF // N_CHIPS, out_split=N_CHIPS,
                             comm=[_chips_stage([oth_d])])
    mine["w_down"] = add_chips("w_down", own_d, r)
    dw_up, [[r]] = _matmul(fn, d_up, ta=True, name="dw_up", tk=1024, tn=D_FF // N_CHIPS, out_split=N_CHIPS,
                           comm=[_halves_stage([dw_gate])])
    own_g, oth_g = add_pair("w_gate", dw_gate, r)
    d_fn, [[r], [theirs["w_down"]]] = _matmul(
        d_gate, w_gate_f, tb=True, b_split=True, name="dfn_gate",
        comm=[_chips_stage([oth_g]), _swap_stage([mine["w_down"]])])
    mine["w_gate"] = add_chips("w_gate", own_g, r)
    d_fn, [[r]] = _matmul(d_up, w_up_f, tb=True, b_split=True, add=d_fn, name="dfn_up",
                          comm=[_halves_stage([dw_up])])
    own_u, oth_u = add_pair("w_up", dw_up, r)
    d_h1, gs["ffn_norm_g"] = _rms_bwd(d_fn, h1, sp["ffn_norm_g"], d_out, name="ffn_norm_bwd")
    dw_out, [[theirs["w_gate"]]] = _matmul(y, d_h1, ta=True, name="dw_out", tk=1024,
                                           comm=[_swap_stage([mine["w_gate"]])])
    dw_out = dw_out.reshape(N_CHIPS, D_MODEL // N_CHIPS, D_MODEL)
    d_y, [[r]] = _matmul(d_h1, w_out_f, tb=True, name="dy", comm=[_halves_stage([dw_out])])
    own_o, oth_o = add_pair("w_out", dw_out, r)
    (d_qa, d_ka, d_va, d_sink), [[r]] = _swa_bwd(qa, ka, va, pos_col, pos_row, sinks, y_a, d_y,
                                                 comm=[_chips_stage([oth_u])])
    mine["w_up"] = add_chips("w_up", own_u, r)
    (d_qcat, d_kcat, d_vb), [[r], [theirs["w_up"]]] = _mla_bwd(
        q_cat, k_cat, v_b, y_b, lse, d_y, comm=[_chips_stage([oth_o]), _swap_stage([mine["w_up"]])])
    mine["w_out"] = add_chips("w_out", own_o, r)
    d_qm, d_km, d_vm = _mem_attn_bwd(qm, km, vm, y_m, d_y)
    (d_proj, dw_uq, dw_ukv, dg128, gs["mla_cq_norm_g"], gs["mla_ckv_norm_g"]), [[theirs["w_out"]]] = _attn_prep_bwd(
        proj, g128, gcq, gckv, w_uq_f, w_ukv_f, cos_t, sin_t, d_qa, d_ka, d_va, d_qcat, d_kcat, d_vb, d_qm,
        comm=[_swap_stage([mine["w_out"]])])
    dw_mem_kv, gs["mem_norm_g"], gs["mem_k_norm_g"] = _mem_kv_bwd(
        mem, sp["mem_norm_g"], w_mem_kv_f, sp["mem_k_norm_g"], mn_b, kv_m, d_km, d_vm)
    late = ("w_uq", "w_ukv", "w_mem_kv")
    late_g = [_shards_heads(dw_uq, MLA_NOPE, MLA_ROPE), _shards_heads(dw_ukv, MLA_NOPE, MLA_V),
              dw_mem_kv.reshape(N_CHIPS, D_MODEL // N_CHIPS, -1)]
    dw_in, [rs] = _matmul(hn, d_proj, ta=True, name="dw_in", tk=1024, comm=[_halves_stage(late_g)])
    late_sums = [add_pair(n, g4, r) for n, g4, r in zip(late, late_g, rs)]
    dw_in = _shards_w_in(dw_in)
    d_hn, [rs, [r]] = _matmul(d_proj, w_in_f, tb=True, name="dhn", tk=1536,
                              comm=[_chips_stage([oth for _, oth in late_sums]), _halves_stage([dw_in])])
    for n, (own, _), r_n in zip(late, late_sums, rs):
        mine[n] = add_chips(n, own, r_n)
    own_i, oth_i = add_pair("w_in", dw_in, r)
    (grad_x, gs["attn_norm_g"]), [[r], late_theirs] = _rms_bwd(
        d_hn, x, sp["attn_norm_g"], d_h1, name="attn_norm_bwd",
        comm=[_chips_stage([oth_i]), _swap_stage([mine[n] for n in late])])
    theirs.update(zip(late, late_theirs))
    mine["w_in"] = add_chips("w_in", own_i, r)
    [theirs["w_in"]] = _run_stage(_swap_stage([mine["w_in"]]), name="grad_swap_w_in")

    fold = lambda r: r[:, :64] + r[:, 64:]
    gs["swa_q_norm_g"] = fold(dg128[G_SWA_Q:G_SWA_Q + 1])
    gs["swa_k_norm_g"] = fold(dg128[G_SWA_K:G_SWA_K + 1])
    gs["mla_qn_norm_g"] = dg128[G_QN:G_QN + 1]
    gs["mla_qr_norm_g"] = fold(dg128[G_QR:G_QR + 1])
    gs["mla_kn_norm_g"] = dg128[G_KN:G_KN + 1]
    gs["mla_kr_norm_g"] = fold(dg128[G_KR:G_KR + 1])
    gs["mem_q_norm_g"] = dg128[G_MQ:G_MQ + 1]
    gs["swa_sinks"] = d_sink[:, :SWA_Q_HEADS]

    grad, delta, new_m, new_v = {}, {}, {}, {}
    for n in BIG:
        g2, d, m2, v2 = _adamw(meta, wts[n][0], mine[n], theirs[n], mom_m[n][0], mom_v[n][0], name="adamw_" + n)
        grad[n], delta[n], new_m[n], new_v[n] = g2[None], d[None], m2[None], v2[None]

    sizes = [wts[n].shape[1] for n in SMALL]
    zero = jnp.zeros((1, LANES), F32)
    packs = _small_step(_pack([gs[n] for n in SMALL] + [loss_tile]), _pack([wts[n] for n in SMALL] + [zero]),
                        _pack([mom_m[n] for n in SMALL] + [zero]), _pack([mom_v[n] for n in SMALL] + [zero]))
    for store, buf in zip((grad, delta, new_m, new_v), packs):
        for n, val in zip(SMALL, _unpack(buf, sizes)):
            store[n] = val
    loss = _unpack(packs[0], sizes + [LANES])[-1][0, 0]

    return (loss, grad_x[None], *[grad[n] for n in WEIGHTS], *[delta[n] for n in WEIGHTS],
            *[new_m[n] for n in WEIGHTS], *[new_v[n] for n in WEIGHTS])
```

```python
import functools
import math

import jax
import jax.numpy as jnp
from jax import lax
from jax.experimental import pallas as pl
from jax.experimental.pallas import tpu as pltpu

F32 = jnp.float32
BF16 = jnp.bfloat16

D_MODEL = 2048
BLOCK = 128
EPS = 1e-6
NEG_INF = -1e30
SWA_Q_HEADS = 16
SWA_KV_HEADS = 2
SWA_HEAD_DIM = 64
MLA_HEADS = 4
MLA_RANK = 512
MLA_NOPE = 128
MLA_ROPE = 64
MLA_V = 128
ROPE_THETA = 10000.0
MEM_HEADS = 4
MEM_DIM = 128
D_FF = 5632
IN_WIDTH = 2880
IN_PAD = 3072
N_CHIPS = 4

ADAM_LR = 0.001
ADAM_B1 = 0.9
ADAM_B2 = 0.999
ADAM_EPS = 1e-08
ADAM_WD = 0.01
ADAM_STEP = 10

VMEM_LIMIT_BYTES = 56 * 1024 * 1024
LANES = 128

MESH = pl.DeviceIdType.MESH


def _params(sem=None, **kw):
    return pltpu.CompilerParams(dimension_semantics=sem, vmem_limit_bytes=VMEM_LIMIT_BYTES, **kw)


def _tile(n, want):
    if n <= want:
        return n
    t = want - want % LANES
    while t > 0:
        if n % t == 0:
            return t
        t -= LANES
    return n


ANY = pl.BlockSpec(memory_space=pl.ANY)


class _Stage:
    def __init__(self, ins, out_shapes, aliases, n_sem, issue, wait, mid=None):
        self.ins, self.out_shapes, self.aliases, self.n_sem = list(ins), list(out_shapes), dict(aliases), n_sem
        self.issue, self.wait, self.mid = issue, wait, mid


def _pcall(body, args, *, name, grid, in_specs, out_specs, out_shape, scratch_shapes=(), sem=None, comm=(),
           prefetch=()):
    multi = isinstance(out_shape, (list, tuple))
    out_specs_l = list(out_specs) if multi else [out_specs]
    out_shape_l = list(out_shape) if multi else [out_shape]
    npf = len(prefetch)

    def call(fn, in_specs_, out_specs_, out_shape_, scratch_, operands, sem_, aliases=None):
        kw = dict(name=name, out_shape=out_shape_, compiler_params=_params(sem_))
        if aliases:
            kw["input_output_aliases"] = aliases
        if npf:
            spec = pltpu.PrefetchScalarGridSpec(num_scalar_prefetch=npf, grid=grid, in_specs=in_specs_,
                                                out_specs=out_specs_, scratch_shapes=scratch_)
            return pl.pallas_call(fn, grid_spec=spec, **kw)(*prefetch, *operands)
        return pl.pallas_call(fn, grid=grid, in_specs=in_specs_, out_specs=out_specs_, scratch_shapes=scratch_,
                              **kw)(*operands)

    if not comm:
        return call(body, list(in_specs), out_specs, out_shape, list(scratch_shapes), args, sem)
    n_in, n_out, n_scr = len(in_specs), len(out_specs_l), len(scratch_shapes)
    cins = [a for st in comm for a in st.ins]
    couts = [s for st in comm for s in st.out_shapes]
    aliases, ci, co = {}, 0, 0
    for st in comm:
        for a_i, o_i in st.aliases.items():
            aliases[npf + n_in + ci + a_i] = n_out + co + o_i
        ci, co = ci + len(st.ins), co + len(st.out_shapes)

    def wrapped(*refs):
        pre = refs[:npf]
        p = npf
        ins = refs[p:p + n_in]; p += n_in
        cin_refs = refs[p:p + len(cins)]; p += len(cins)
        outs = refs[p:p + n_out]; p += n_out
        cout_refs = refs[p:p + len(couts)]; p += len(couts)
        scr = refs[p:p + n_scr]; p += n_scr
        sems = refs[p:]
        first = functools.reduce(jnp.logical_and, [pl.program_id(a) == 0 for a in range(len(grid))])
        last = functools.reduce(jnp.logical_and, [pl.program_id(a) == grid[a] - 1 for a in range(len(grid))])

        def each(what):
            i, o = 0, 0
            for k, st in enumerate(comm):
                fn = getattr(st, what)
                if fn is not None:
                    fn(cin_refs[i:i + len(st.ins)], cout_refs[o:o + len(st.out_shapes)], sems[2 * k], sems[2 * k + 1])
                i, o = i + len(st.ins), o + len(st.out_shapes)

        @pl.when(first)
        def _():
            each("issue")

        if any(st.mid is not None for st in comm):
            n_steps = math.prod(grid)
            assert n_steps >= 4, "a two-leg stage needs a carrier with several grid steps"
            lin = functools.reduce(lambda acc, a: acc * grid[a] + pl.program_id(a), range(len(grid)), 0)

            @pl.when(lin == (3 * n_steps) // 4)
            def _():
                each("mid")

        body(*pre, *ins, *outs, *scr)

        @pl.when(last)
        def _():
            each("wait")

    sem_scr = [pltpu.SemaphoreType.DMA((st.n_sem,)) for st in comm for _ in range(2)]
    res = call(wrapped, list(in_specs) + [ANY] * len(cins), out_specs_l + [ANY] * len(couts), out_shape_l + couts,
               list(scratch_shapes) + sem_scr, (*args, *cins), ("arbitrary",) * len(grid), aliases)
    normal = list(res[:n_out])
    stage_outs, o = [], n_out
    for st in comm:
        stage_outs.append(list(res[o:o + len(st.out_shapes)]))
        o += len(st.out_shapes)
    return (normal if multi else normal[0]), stage_outs


def _matmul(a, b, *, name, ta=False, tb=False, add=None, out_dtype=F32, tm=1024, tn=1024, tk=2048,
            b_split=False, out_split=0, comm=()):
    if ta:
        kdim, m = a.shape
    else:
        m, kdim = a.shape
    if b_split:
        assert tb
        nsp, n, kb = b.shape
        kb = kb * nsp
    elif tb:
        n, kb = b.shape
    else:
        kb, n = b.shape
    assert kb == kdim, (a.shape, b.shape, ta, tb)
    if b_split:
        tk = kdim // nsp
    if out_split:
        tn = _tile(n // out_split, tn)
    tm, tn, tk = _tile(m, tm), _tile(n, tn), _tile(kdim, tk)
    nk = kdim // tk
    dims = (((0 if ta else 1,), (1 if tb else 0,)), ((), ()))

    def body(*refs):
        if add is None:
            a_ref, b_ref, o_ref, acc_ref = refs
            add_ref = None
        else:
            a_ref, b_ref, add_ref, o_ref, acc_ref = refs
        k = pl.program_id(2)
        part = lax.dot_general(a_ref[...].astype(BF16), b_ref[...].astype(BF16), dims,
                               preferred_element_type=F32)

        @pl.when(k == 0)
        def _():
            acc_ref[...] = part

        @pl.when(k > 0)
        def _():
            acc_ref[...] += part

        @pl.when(k == nk - 1)
        def _():
            r = acc_ref[...]
            if add_ref is not None:
                r = r + add_ref[...].astype(F32)
            o_ref[...] = r.astype(o_ref.dtype)

    a_spec = pl.BlockSpec((tk, tm), lambda i, j, k: (k, i)) if ta else pl.BlockSpec((tm, tk), lambda i, j, k: (i, k))
    if b_split:
        b_spec = pl.BlockSpec((None, tn, tk), lambda i, j, k: (k, j, 0))
    elif tb:
        b_spec = pl.BlockSpec((tn, tk), lambda i, j, k: (j, k))
    else:
        b_spec = pl.BlockSpec((tk, tn), lambda i, j, k: (k, j))
    in_specs = [a_spec, b_spec]
    args = [a, b]
    if add is not None:
        in_specs.append(pl.BlockSpec((tm, tn), lambda i, j, k: (i, j)))
        args.append(add)
    if out_split:
        per = (n // out_split) // tn
        out_spec = pl.BlockSpec((None, tm, tn), lambda i, j, k: (j // per, i, j % per))
        out_shape = jax.ShapeDtypeStruct((out_split, m, n // out_split), out_dtype)
    else:
        out_spec = pl.BlockSpec((tm, tn), lambda i, j, k: (i, j))
        out_shape = jax.ShapeDtypeStruct((m, n), out_dtype)
    return _pcall(body, args, name=name, grid=(m // tm, n // tn, nk), in_specs=in_specs, out_specs=out_spec,
                  out_shape=out_shape, scratch_shapes=[pltpu.VMEM((tm, tn), F32)],
                  sem=("parallel", "parallel", "arbitrary"), comm=comm)


def _rms_fwd(x, g, *, name, tm=512, comm=()):
    s, d = x.shape
    tm = _tile(s, tm)

    def body(x_ref, g_ref, o_ref):
        xv = x_ref[...]
        r = lax.rsqrt(jnp.mean(xv * xv, axis=-1, keepdims=True) + EPS)
        o_ref[...] = (xv * r * g_ref[...]).astype(o_ref.dtype)

    return _pcall(body, (x, g), name=name, grid=(s // tm,),
                  in_specs=[pl.BlockSpec((tm, d), lambda i: (i, 0)), pl.BlockSpec((1, d), lambda i: (0, 0))],
                  out_specs=pl.BlockSpec((tm, d), lambda i: (i, 0)),
                  out_shape=jax.ShapeDtypeStruct((s, d), BF16), sem=("parallel",), comm=comm)


def _rms_bwd(dy, x, g, res, *, name, tm=512, comm=()):
    s, d = x.shape
    tm = _tile(s, tm)

    def body(dy_ref, x_ref, g_ref, res_ref, dx_ref, dxb_ref, dg_ref):
        xv = x_ref[...]
        dyv = dy_ref[...]
        r = lax.rsqrt(jnp.mean(xv * xv, axis=-1, keepdims=True) + EPS)
        xhat = xv * r
        dyg = dyv * g_ref[...]
        mt = jnp.mean(dyg * xhat, axis=-1, keepdims=True)
        dx = res_ref[...] + r * (dyg - xhat * mt)
        dx_ref[...] = dx
        dxb_ref[...] = dx.astype(BF16)
        part = jnp.sum(dyv * xhat, axis=0, keepdims=True)

        @pl.when(pl.program_id(0) == 0)
        def _():
            dg_ref[...] = part

        @pl.when(pl.program_id(0) > 0)
        def _():
            dg_ref[...] += part

    row = pl.BlockSpec((tm, d), lambda i: (i, 0))
    vec = pl.BlockSpec((1, d), lambda i: (0, 0))
    return _pcall(body, (dy, x, g, res), name=name, grid=(s // tm,), in_specs=[row, row, vec, row],
                  out_specs=[row, row, vec],
                  out_shape=[jax.ShapeDtypeStruct((s, d), F32), jax.ShapeDtypeStruct((s, d), BF16),
                             jax.ShapeDtypeStruct((1, d), F32)],
                  sem=("arbitrary",), comm=comm)


def _lane(shape):
    return lax.broadcasted_iota(jnp.int32, shape, 1)


def _halfsum(t, lo):
    s_lo = jnp.sum(jnp.where(lo, t, 0.0), axis=-1, keepdims=True)
    s_hi = jnp.sum(jnp.where(lo, 0.0, t), axis=-1, keepdims=True)
    return jnp.where(lo, s_lo, s_hi)


def _norm_pair(x, g, lo):
    r = lax.rsqrt(_halfsum(x * x, lo) * (1.0 / 64.0) + EPS)
    xhat = x * r
    return xhat * g, xhat, r


def _norm_pair_bwd(dy, g, xhat, r, lo):
    dyg = dy * g
    mt = _halfsum(dyg * xhat, lo) * (1.0 / 64.0)
    return r * (dyg - xhat * mt), jnp.sum(dy * xhat, axis=0, keepdims=True)


def _norm_full(x, g):
    r = lax.rsqrt(jnp.mean(x * x, axis=-1, keepdims=True) + EPS)
    xhat = x * r
    return xhat * g, xhat, r


def _norm_full_bwd(dy, g, xhat, r):
    dyg = dy * g
    mt = jnp.mean(dyg * xhat, axis=-1, keepdims=True)
    return r * (dyg - xhat * mt), jnp.sum(dy * xhat, axis=0, keepdims=True)


def _rot(x, first32):
    return jnp.where(first32, pltpu.roll(x, 96, axis=1), pltpu.roll(x, 32, axis=1))


def _rope(x, cos_t, sin_t, first32):
    return x * cos_t + _rot(x, first32) * sin_t


def _rope_bwd(dy, cos_t, sin_t, first32):
    return dy * cos_t + _rot(dy * sin_t, first32)


G_SWA_Q, G_SWA_K, G_QN, G_QR, G_KN, G_KR, G_MQ = range(7)

C_QA, C_KA, C_VA, C_CQ, C_CKV, C_QM, C_KR = 0, 1024, 1152, 1280, 1792, 2304, 2816


def _prep_common(p_ref, g128_ref, gcq_ref, gckv_ref, wuq_ref, wukv_ref, cos_ref, sin_ref):
    tm = p_ref.shape[0]
    lane = _lane((tm, LANES))
    lo = lane < 64
    first32 = (lane % 64) < 32
    cos_t = cos_ref[...]
    sin_t = sin_ref[...]
    g = lambda row: g128_ref[row:row + 1, :]
    out = dict(lo=lo, first32=first32, cos_t=cos_t, sin_t=sin_t, lane=lane)
    cq_n, cq_hat, cq_r = _norm_full(p_ref[:, C_CQ:C_CQ + MLA_RANK], gcq_ref[...])
    ckv_n, ckv_hat, ckv_r = _norm_full(p_ref[:, C_CKV:C_CKV + MLA_RANK], gckv_ref[...])
    cq_b = cq_n.astype(BF16)
    ckv_b = ckv_n.astype(BF16)
    q_b = jnp.dot(cq_b, wuq_ref[...], preferred_element_type=F32)
    kv_b = jnp.dot(ckv_b, wukv_ref[...], preferred_element_type=F32)
    out.update(cq_b=cq_b, cq_hat=cq_hat, cq_r=cq_r, ckv_b=ckv_b, ckv_hat=ckv_hat, ckv_r=ckv_r, q_b=q_b, kv_b=kv_b, g=g)
    return out


def _attn_prep_fwd(proj, g128, gcq, gckv, wuq, wukv, cos_t, sin_t, *, tm=512, comm=()):
    s = proj.shape[0]
    tm = _tile(s, tm)

    def body(p_ref, g128_ref, gcq_ref, gckv_ref, wuq_ref, wukv_ref, cos_ref, sin_ref,
             qa_ref, ka_ref, va_ref, qcat_ref, kcat_ref, vb_ref, qm_ref):
        c = _prep_common(p_ref, g128_ref, gcq_ref, gckv_ref, wuq_ref, wukv_ref, cos_ref, sin_ref)
        lo, first32, g = c["lo"], c["first32"], c["g"]
        for j in range(SWA_Q_HEADS // 2):
            y, _, _ = _norm_pair(p_ref[:, C_QA + 128 * j:C_QA + 128 * (j + 1)], g(G_SWA_Q), lo)
            qa_ref[:, 128 * j:128 * (j + 1)] = y.astype(BF16)
        y, _, _ = _norm_pair(p_ref[:, C_KA:C_KA + 128], g(G_SWA_K), lo)
        ka_ref[...] = y.astype(BF16)
        va_ref[...] = p_ref[:, C_VA:C_VA + 128].astype(BF16)
        kr, _, _ = _norm_pair(p_ref[:, C_KR:C_KR + 128], g(G_KR), lo)
        kr = jnp.where(lo, _rope(kr, c["cos_t"], c["sin_t"], first32), 0.0)
        krkr = (kr + pltpu.roll(kr, 64, axis=1)).astype(BF16)
        q_b, kv_b = c["q_b"], c["kv_b"]
        qr = []
        for j in range(MLA_HEADS // 2):
            y, _, _ = _norm_pair(q_b[:, 512 + 128 * j:512 + 128 * (j + 1)], g(G_QR), lo)
            qr.append(_rope(y, c["cos_t"], c["sin_t"], first32))
        for h in range(MLA_HEADS):
            qn, _, _ = _norm_full(q_b[:, 128 * h:128 * (h + 1)], g(G_QN))
            keep = lo if h % 2 == 0 else jnp.logical_not(lo)
            qcat_ref[h, :, 0:128] = qn.astype(BF16)
            qcat_ref[h, :, 128:256] = jnp.where(keep, qr[h // 2], 0.0).astype(BF16)
            kn, _, _ = _norm_full(kv_b[:, 128 * h:128 * (h + 1)], g(G_KN))
            kcat_ref[h, :, 0:128] = kn.astype(BF16)
            kcat_ref[h, :, 128:256] = krkr
        vb_ref[...] = kv_b[:, 512:1024].astype(BF16)
        for h in range(MEM_HEADS):
            y, _, _ = _norm_full(p_ref[:, C_QM + 128 * h:C_QM + 128 * (h + 1)], g(G_MQ))
            qm_ref[:, 128 * h:128 * (h + 1)] = y.astype(BF16)

    row = lambda w: pl.BlockSpec((tm, w), lambda i: (i, 0))
    full = lambda shape: pl.BlockSpec(shape, lambda i: tuple(0 for _ in shape))
    cat = pl.BlockSpec((MLA_HEADS, tm, 256), lambda i: (0, i, 0))
    return _pcall(
        body, (proj, g128, gcq, gckv, wuq, wukv, cos_t, sin_t), name="attn_prep_fwd", grid=(s // tm,),
        in_specs=[row(IN_PAD), full((8, 128)), full((1, 512)), full((1, 512)), full((512, 768)), full((512, 1024)),
                  row(128), row(128)],
        out_specs=[row(1024), row(128), row(128), cat, cat, row(512), row(512)],
        out_shape=[jax.ShapeDtypeStruct((s, 1024), BF16), jax.ShapeDtypeStruct((s, 128), BF16),
                   jax.ShapeDtypeStruct((s, 128), BF16), jax.ShapeDtypeStruct((MLA_HEADS, s, 256), BF16),
                   jax.ShapeDtypeStruct((MLA_HEADS, s, 256), BF16), jax.ShapeDtypeStruct((s, 512), BF16),
                   jax.ShapeDtypeStruct((s, 512), BF16)],
        sem=("parallel",), comm=comm)


def _attn_prep_bwd(proj, g128, gcq, gckv, wuq, wukv, cos_t, sin_t,
                   d_qa, d_ka, d_va, d_qcat, d_kcat, d_vb, d_qm, *, tm=256, comm=()):
    s = proj.shape[0]
    tm = _tile(s, tm)

    def body(p_ref, g128_ref, gcq_ref, gckv_ref, wuq_ref, wukv_ref, cos_ref, sin_ref,
             dqa_ref, dka_ref, dva_ref, dqcat_ref, dkcat_ref, dvb_ref, dqm_ref,
             dp_ref, dwuq_ref, dwukv_ref, dg128_ref, dgcq_ref, dgckv_ref):
        c = _prep_common(p_ref, g128_ref, gcq_ref, gckv_ref, wuq_ref, wukv_ref, cos_ref, sin_ref)
        lo, first32, g = c["lo"], c["first32"], c["g"]
        cos_v, sin_v = c["cos_t"], c["sin_t"]
        q_b, kv_b = c["q_b"], c["kv_b"]
        zero_row = jnp.zeros((1, LANES), F32)
        dg = {k: zero_row for k in range(7)}

        for j in range(SWA_Q_HEADS // 2):
            sl = slice(C_QA + 128 * j, C_QA + 128 * (j + 1))
            _, xhat, r = _norm_pair(p_ref[:, sl], g(G_SWA_Q), lo)
            dx, dgj = _norm_pair_bwd(dqa_ref[:, 128 * j:128 * (j + 1)], g(G_SWA_Q), xhat, r, lo)
            dp_ref[:, sl] = dx.astype(BF16)
            dg[G_SWA_Q] = dg[G_SWA_Q] + dgj
        _, xhat, r = _norm_pair(p_ref[:, C_KA:C_KA + 128], g(G_SWA_K), lo)
        dx, dgj = _norm_pair_bwd(dka_ref[...], g(G_SWA_K), xhat, r, lo)
        dp_ref[:, C_KA:C_KA + 128] = dx.astype(BF16)
        dg[G_SWA_K] = dgj
        dp_ref[:, C_VA:C_VA + 128] = dva_ref[...].astype(BF16)

        dqb_parts = [None] * 6
        for h in range(MLA_HEADS):
            _, xhat, r = _norm_full(q_b[:, 128 * h:128 * (h + 1)], g(G_QN))
            dx, dgj = _norm_full_bwd(dqcat_ref[h, :, 0:128], g(G_QN), xhat, r)
            dqb_parts[h] = dx
            dg[G_QN] = dg[G_QN] + dgj
        for j in range(MLA_HEADS // 2):
            _, xhat, r = _norm_pair(q_b[:, 512 + 128 * j:512 + 128 * (j + 1)], g(G_QR), lo)
            d_rot = jnp.where(lo, dqcat_ref[2 * j, :, 128:256], dqcat_ref[2 * j + 1, :, 128:256])
            d_y = _rope_bwd(d_rot, cos_v, sin_v, first32)
            dx, dgj = _norm_pair_bwd(d_y, g(G_QR), xhat, r, lo)
            dqb_parts[4 + j] = dx
            dg[G_QR] = dg[G_QR] + dgj
        d_qb = jnp.concatenate(dqb_parts, axis=1).astype(BF16)
        dwuq = lax.dot_general(c["cq_b"], d_qb, (((0,), (0,)), ((), ())), preferred_element_type=F32)
        d_cqn = lax.dot_general(d_qb, wuq_ref[...], (((1,), (1,)), ((), ())), preferred_element_type=F32)
        dx, dgcq = _norm_full_bwd(d_cqn, gcq_ref[...], c["cq_hat"], c["cq_r"])
        dp_ref[:, C_CQ:C_CQ + MLA_RANK] = dx.astype(BF16)

        dkv_parts = []
        d_krkr = jnp.zeros((p_ref.shape[0], LANES), F32)
        for h in range(MLA_HEADS):
            _, xhat, r = _norm_full(kv_b[:, 128 * h:128 * (h + 1)], g(G_KN))
            dx, dgj = _norm_full_bwd(dkcat_ref[h, :, 0:128], g(G_KN), xhat, r)
            dkv_parts.append(dx)
            dg[G_KN] = dg[G_KN] + dgj
            d_krkr = d_krkr + dkcat_ref[h, :, 128:256]
        d_kvb = jnp.concatenate(dkv_parts + [dvb_ref[...]], axis=1).astype(BF16)
        dwukv = lax.dot_general(c["ckv_b"], d_kvb, (((0,), (0,)), ((), ())), preferred_element_type=F32)
        d_ckvn = lax.dot_general(d_kvb, wukv_ref[...], (((1,), (1,)), ((), ())), preferred_element_type=F32)
        dx, dgckv = _norm_full_bwd(d_ckvn, gckv_ref[...], c["ckv_hat"], c["ckv_r"])
        dp_ref[:, C_CKV:C_CKV + MLA_RANK] = dx.astype(BF16)

        _, xhat, r = _norm_pair(p_ref[:, C_KR:C_KR + 128], g(G_KR), lo)
        d_kr = jnp.where(lo, d_krkr + pltpu.roll(d_krkr, 64, axis=1), 0.0)
        d_y = jnp.where(lo, _rope_bwd(d_kr, cos_v, sin_v, first32), 0.0)
        dx, dgj = _norm_pair_bwd(d_y, g(G_KR), xhat, r, lo)
        dp_ref[:, C_KR:C_KR + 128] = jnp.where(lo, dx, 0.0).astype(BF16)
        dp_ref[:, C_KR + 128:] = jnp.zeros((p_ref.shape[0], IN_PAD - C_KR - 128), BF16)
        dg[G_KR] = dgj

        for h in range(MEM_HEADS):
            sl = slice(C_QM + 128 * h, C_QM + 128 * (h + 1))
            _, xhat, r = _norm_full(p_ref[:, sl], g(G_MQ))
            dx, dgj = _norm_full_bwd(dqm_ref[:, 128 * h:128 * (h + 1)], g(G_MQ), xhat, r)
            dp_ref[:, sl] = dx.astype(BF16)
            dg[G_MQ] = dg[G_MQ] + dgj

        dg_tile = jnp.concatenate([dg[k] for k in range(7)] + [zero_row], axis=0)

        @pl.when(pl.program_id(0) == 0)
        def _():
            dwuq_ref[...] = dwuq
            dwukv_ref[...] = dwukv
            dg128_ref[...] = dg_tile
            dgcq_ref[...] = dgcq
            dgckv_ref[...] = dgckv

        @pl.when(pl.program_id(0) > 0)
        def _():
            dwuq_ref[...] += dwuq
            dwukv_ref[...] += dwukv
            dg128_ref[...] += dg_tile
            dgcq_ref[...] += dgcq
            dgckv_ref[...] += dgckv

    row = lambda w: pl.BlockSpec((tm, w), lambda i: (i, 0))
    full = lambda shape: pl.BlockSpec(shape, lambda i: tuple(0 for _ in shape))
    cat = pl.BlockSpec((MLA_HEADS, tm, 256), lambda i: (0, i, 0))
    return _pcall(
        body, (proj, g128, gcq, gckv, wuq, wukv, cos_t, sin_t, d_qa, d_ka, d_va, d_qcat, d_kcat, d_vb, d_qm),
        name="attn_prep_bwd", grid=(s // tm,),
        in_specs=[row(IN_PAD), full((8, 128)), full((1, 512)), full((1, 512)), full((512, 768)), full((512, 1024)),
                  row(128), row(128),
                  row(1024), row(128), row(128), cat, cat, row(512), row(512)],
        out_specs=[row(IN_PAD), full((512, 768)), full((512, 1024)), full((8, 128)), full((1, 512)), full((1, 512))],
        out_shape=[jax.ShapeDtypeStruct((s, IN_PAD), BF16), jax.ShapeDtypeStruct((512, 768), F32),
                   jax.ShapeDtypeStruct((512, 1024), F32), jax.ShapeDtypeStruct((8, 128), F32),
                   jax.ShapeDtypeStruct((1, 512), F32), jax.ShapeDtypeStruct((1, 512), F32)],
        sem=("arbitrary",), comm=comm)


SWA_SLOPES = tuple(2.0 ** (-8.0 * h / SWA_Q_HEADS) for h in range(1, SWA_Q_HEADS + 1))
SWA_SCALE = SWA_HEAD_DIM ** -0.5
NT_DIMS = (((1,), (1,)), ((), ()))
TN_DIMS = (((0,), (0,)), ((), ()))


def _swa_span(n, kp_ref, kc_ref, vp_ref, vc_ref, pcol_ref, pprow_ref, pcrow_ref):
    k_span = jnp.concatenate([kp_ref[...], kc_ref[...]], axis=0).astype(F32)
    v_span = jnp.concatenate([vp_ref[...], vc_ref[...]], axis=0).astype(F32)
    lo = _lane((2 * BLOCK, LANES)) < 64
    k_sw = pltpu.roll(k_span, 64, axis=1)
    v_sw = pltpu.roll(v_span, 64, axis=1)
    kk = (jnp.where(lo, k_span, k_sw).astype(BF16), jnp.where(lo, k_sw, k_span).astype(BF16))
    vv_lo = (jnp.where(lo, v_span, 0.0).astype(BF16), jnp.where(lo, v_sw, 0.0).astype(BF16))
    vv_hi = (jnp.where(lo, 0.0, v_sw).astype(BF16), jnp.where(lo, 0.0, v_span).astype(BF16))
    pk = jnp.concatenate([pprow_ref[...], pcrow_ref[...]], axis=1)
    dist = jnp.abs(pcol_ref[...] - pk)
    qi = lax.broadcasted_iota(jnp.int32, (BLOCK, 2 * BLOCK), 0)
    ki = lax.broadcasted_iota(jnp.int32, (BLOCK, 2 * BLOCK), 1)
    first_key = jnp.where(n > 0, qi + 1, jnp.maximum(qi + 1, BLOCK))
    valid = jnp.logical_and(ki >= first_key, ki <= qi + BLOCK)
    mask_add = jnp.where(valid, 0.0, NEG_INF)
    return kk, vv_lo, vv_hi, dist, mask_add


def _swa_heads(q_ref, lo):
    heads = []
    for j in range(SWA_Q_HEADS // 2):
        q_pair = q_ref[:, 128 * j:128 * (j + 1)].astype(F32)
        for par in (0, 1):
            q_h = jnp.where(lo if par == 0 else jnp.logical_not(lo), q_pair, 0.0).astype(BF16)
            heads.append((2 * j + par, (2 * j) // (SWA_Q_HEADS // SWA_KV_HEADS), par, q_h))
    return heads


def _swa_probs(raw, dist, mask_add, slope, sink):
    s = raw * SWA_SCALE - slope * dist + mask_add
    m = jnp.maximum(jnp.max(s, axis=-1, keepdims=True), sink)
    e = jnp.exp(s - m)
    e_sink = jnp.exp(sink - m)
    inv = 1.0 / (jnp.sum(e, axis=-1, keepdims=True) + e_sink)
    return e * inv, e_sink * inv


def _swa_specs():
    blk = lambda w: pl.BlockSpec((BLOCK, w), lambda n: (n, 0))
    prev = lambda w: pl.BlockSpec((BLOCK, w), lambda n: (jnp.maximum(n - 1, 0), 0))
    prow_c = pl.BlockSpec((1, BLOCK), lambda n: (0, n))
    prow_p = pl.BlockSpec((1, BLOCK), lambda n: (0, jnp.maximum(n - 1, 0)))
    smem = pl.BlockSpec(memory_space=pltpu.SMEM)
    return [blk(1024), prev(128), blk(128), prev(128), blk(128), blk(1), prow_p, prow_c, smem], blk


def _swa_fwd(qa, ka, va, pos_col, pos_row, sinks, *, comm=()):
    s = qa.shape[0]
    in_specs, blk = _swa_specs()

    def body(q_ref, kp_ref, kc_ref, vp_ref, vc_ref, pcol_ref, pprow_ref, pcrow_ref, sink_ref, o_ref):
        n = pl.program_id(0)
        kk, vv_lo, vv_hi, dist, mask_add = _swa_span(n, kp_ref, kc_ref, vp_ref, vc_ref, pcol_ref, pprow_ref, pcrow_ref)
        lo = _lane((BLOCK, LANES)) < 64
        heads = _swa_heads(q_ref, lo)
        raws = [lax.dot_general(q_h, kk[kv], NT_DIMS, preferred_element_type=F32) for _, kv, _, q_h in heads]
        probs = [_swa_probs(raw, dist, mask_add, SWA_SLOPES[h], sink_ref[h])[0].astype(BF16)
                 for raw, (h, _, _, _) in zip(raws, heads)]
        for j in range(SWA_Q_HEADS // 2):
            kv = heads[2 * j][1]
            o_ref[:, 128 * j:128 * (j + 1)] = (
                jnp.dot(probs[2 * j], vv_lo[kv], preferred_element_type=F32)
                + jnp.dot(probs[2 * j + 1], vv_hi[kv], preferred_element_type=F32))

    return _pcall(body, (qa, ka, ka, va, va, pos_col, pos_row, pos_row, sinks), name="swa_fwd", grid=(s // BLOCK,),
                  in_specs=in_specs, out_specs=blk(1024), out_shape=jax.ShapeDtypeStruct((s, 1024), F32),
                  sem=("parallel",), comm=comm)


def _swa_bwd(qa, ka, va, pos_col, pos_row, sinks, y_a, d_y, *, comm=()):
    s = qa.shape[0]
    in_specs, blk = _swa_specs()
    whole = pl.BlockSpec((s, 128), lambda n: (0, 0))

    def body(q_ref, kp_ref, kc_ref, vp_ref, vc_ref, pcol_ref, pprow_ref, pcrow_ref, sink_ref, y_ref, dy_ref,
             dq_ref, dk_ref, dv_ref, dsink_ref):
        n = pl.program_id(0)

        @pl.when(n == 0)
        def _():
            dk_ref[...] = jnp.zeros_like(dk_ref)
            dv_ref[...] = jnp.zeros_like(dv_ref)
            dsink_ref[...] = jnp.zeros_like(dsink_ref)

        kk, vv_lo, vv_hi, dist, mask_add = _swa_span(n, kp_ref, kc_ref, vp_ref, vc_ref, pcol_ref, pprow_ref, pcrow_ref)
        lo = _lane((BLOCK, LANES)) < 64
        lo2 = _lane((2 * BLOCK, LANES)) < 64
        lane1 = _lane((1, LANES))
        dsink = jnp.zeros((1, LANES), F32)
        dkk = [jnp.zeros((2 * BLOCK, LANES), F32) for _ in range(SWA_KV_HEADS)]
        dvv = [jnp.zeros((2 * BLOCK, LANES), F32) for _ in range(SWA_KV_HEADS)]
        heads = _swa_heads(q_ref, lo)
        do_b, deltas = [], []
        for j in range(SWA_Q_HEADS // 2):
            do_pair = dy_ref[:, 128 * j:128 * (j + 1)]
            doy = do_pair * y_ref[:, 128 * j:128 * (j + 1)]
            do_b.append(do_pair.astype(BF16))
            deltas.append(jnp.sum(jnp.where(lo, doy, 0.0), axis=-1, keepdims=True))
            deltas.append(jnp.sum(jnp.where(lo, 0.0, doy), axis=-1, keepdims=True))
        raws = [lax.dot_general(q_h, kk[kv], NT_DIMS, preferred_element_type=F32) for _, kv, _, q_h in heads]
        dps = [lax.dot_general(do_b[h // 2], (vv_lo, vv_hi)[par][kv], NT_DIMS, preferred_element_type=F32)
               for h, kv, par, _ in heads]
        p_b, ds_b = [], []
        for h, kv, par, _ in heads:
            p, p_sink = _swa_probs(raws[h], dist, mask_add, SWA_SLOPES[h], sink_ref[h])
            ds = p * (dps[h] - deltas[h])
            dsink = dsink + jnp.where(lane1 == h, -jnp.sum(p_sink * deltas[h], axis=0, keepdims=True), 0.0)
            p_b.append(p.astype(BF16))
            ds_b.append((ds * SWA_SCALE).astype(BF16))
        dq_halves = []
        for h, kv, par, q_h in heads:
            dq_halves.append(jnp.dot(ds_b[h], kk[kv], preferred_element_type=F32))
            dkk[kv] = dkk[kv] + lax.dot_general(ds_b[h], q_h, TN_DIMS, preferred_element_type=F32)
            pv = lax.dot_general(p_b[h], do_b[h // 2], TN_DIMS, preferred_element_type=F32)
            dvv[kv] = dvv[kv] + jnp.where(lo2 if par == 0 else jnp.logical_not(lo2), pv, 0.0)
        for j in range(SWA_Q_HEADS // 2):
            dq_ref[:, 128 * j:128 * (j + 1)] = jnp.where(lo, dq_halves[2 * j], dq_halves[2 * j + 1])
        fold = lambda t: t + pltpu.roll(t, 64, axis=1)
        dk_span = jnp.where(lo2, fold(dkk[0]), fold(dkk[1]))
        dv_span = jnp.where(lo2, fold(dvv[0]), fold(dvv[1]))
        prev0 = pl.multiple_of(jnp.maximum(n - 1, 0) * BLOCK, BLOCK)
        cur0 = pl.multiple_of(n * BLOCK, BLOCK)
        dk_ref[pl.ds(prev0, BLOCK), :] += dk_span[0:BLOCK]
        dk_ref[pl.ds(cur0, BLOCK), :] += dk_span[BLOCK:]
        dv_ref[pl.ds(prev0, BLOCK), :] += dv_span[0:BLOCK]
        dv_ref[pl.ds(cur0, BLOCK), :] += dv_span[BLOCK:]
        dsink_ref[...] += dsink

    return _pcall(
        body, (qa, ka, ka, va, va, pos_col, pos_row, pos_row, sinks, y_a, d_y), name="swa_bwd", grid=(s // BLOCK,),
        in_specs=in_specs + [blk(1024), blk(1024)],
        out_specs=[blk(1024), whole, whole, pl.BlockSpec((1, LANES), lambda n: (0, 0))],
        out_shape=[jax.ShapeDtypeStruct((s, 1024), F32), jax.ShapeDtypeStruct((s, 128), F32),
                   jax.ShapeDtypeStruct((s, 128), F32), jax.ShapeDtypeStruct((1, LANES), F32)],
        sem=("arbitrary",), comm=comm)


MLA_SCALE = (MLA_NOPE + MLA_ROPE) ** -0.5
MLA_TILE = 512


def _tile_pairs(nt, q_major):
    pairs = [(i, j) for i in range(nt) for j in range(i + 1)] if q_major else \
            [(i, j) for j in range(nt) for i in range(j, nt)]
    return jnp.asarray([p[0] for p in pairs], jnp.int32), jnp.asarray([p[1] for p in pairs], jnp.int32)


def _diag_mask(t):
    return lax.broadcasted_iota(jnp.int32, (t, t), 1) <= lax.broadcasted_iota(jnp.int32, (t, t), 0)


def _mla_fwd(q_cat, k_cat, v_b, *, comm=()):
    nh, s, _ = q_cat.shape
    t = _tile(s, MLA_TILE)
    qi, kj = _tile_pairs(s // t, True)

    def body(qi_ref, kj_ref, q_ref, k_ref, v_ref, o_ref, lse_ref, m_sc, l_sc, acc_sc):
        i, j = qi_ref[pl.program_id(0)], kj_ref[pl.program_id(0)]

        @pl.when(j == 0)
        def _():
            m_sc[...] = jnp.full_like(m_sc, NEG_INF)
            l_sc[...] = jnp.zeros_like(l_sc)
            acc_sc[...] = jnp.zeros_like(acc_sc)

        def update(diagonal):
            scores = [lax.dot_general(q_ref[h], k_ref[h], NT_DIMS, preferred_element_type=F32) for h in range(nh)]
            probs, alphas = [], []
            for h in range(nh):
                sc = scores[h] * MLA_SCALE
                if diagonal:
                    sc = jnp.where(_diag_mask(t), sc, NEG_INF)
                m_old = m_sc[h]
                m_new = jnp.maximum(m_old, jnp.max(sc, axis=-1, keepdims=True))
                alpha = jnp.exp(m_old - m_new)
                p = jnp.exp(sc - m_new)
                l_sc[h] = alpha * l_sc[h] + jnp.sum(p, axis=-1, keepdims=True)
                m_sc[h] = m_new
                probs.append(p.astype(BF16))
                alphas.append(alpha)
            for h in range(nh):
                acc_sc[h] = alphas[h] * acc_sc[h] + jnp.dot(probs[h], v_ref[:, MLA_V * h:MLA_V * (h + 1)],
                                                            preferred_element_type=F32)

        @pl.when(j < i)
        def _():
            update(False)

        @pl.when(j == i)
        def _():
            update(True)
            for h in range(nh):
                o_ref[:, MLA_V * h:MLA_V * (h + 1)] = acc_sc[h] * (1.0 / l_sc[h])
                lse_ref[h] = m_sc[h] + jnp.log(l_sc[h])

    return _pcall(
        body, (q_cat, k_cat, v_b), name="mla_fwd", grid=(qi.shape[0],), prefetch=(qi, kj),
        in_specs=[pl.BlockSpec((nh, t, 256), lambda p, qi, kj: (0, qi[p], 0)),
                  pl.BlockSpec((nh, t, 256), lambda p, qi, kj: (0, kj[p], 0)),
                  pl.BlockSpec((t, nh * MLA_V), lambda p, qi, kj: (kj[p], 0))],
        out_specs=[pl.BlockSpec((t, nh * MLA_V), lambda p, qi, kj: (qi[p], 0)),
                   pl.BlockSpec((nh, t, 1), lambda p, qi, kj: (0, qi[p], 0))],
        out_shape=[jax.ShapeDtypeStruct((s, nh * MLA_V), F32), jax.ShapeDtypeStruct((nh, s, 1), F32)],
        scratch_shapes=[pltpu.VMEM((nh, t, 1), F32), pltpu.VMEM((nh, t, 1), F32), pltpu.VMEM((nh, t, MLA_V), F32)],
        sem=("arbitrary",), comm=comm)


def _mla_bwd(q_cat, k_cat, v_b, y_b, lse, d_y, *, comm=()):
    nh, s, _ = q_cat.shape
    t = _tile(s, MLA_TILE)
    nt = s // t
    hp = 2
    wv = hp * MLA_V
    col0 = (SWA_Q_HEADS * SWA_HEAD_DIM) // wv
    qi, kj = _tile_pairs(nt, False)

    def body(qi_ref, kj_ref, q_ref, k_ref, v_ref, y_ref, lse_ref, dy_ref, dq_ref, dk_ref, dv_ref, dk_sc, dv_sc):
        step = pl.program_id(1)
        i, j = qi_ref[step], kj_ref[step]

        @pl.when(step == 0)
        def _():
            dq_ref[...] = jnp.zeros_like(dq_ref)

        @pl.when(i == j)
        def _():
            dk_sc[...] = jnp.zeros_like(dk_sc)
            dv_sc[...] = jnp.zeros_like(dv_sc)

        def update(diagonal):
            rows = pl.ds(pl.multiple_of(i * t, t), t)
            cols = [slice(MLA_V * h, MLA_V * (h + 1)) for h in range(hp)]
            do_b = [dy_ref[:, cols[h]].astype(BF16) for h in range(hp)]
            scores = [lax.dot_general(q_ref[h], k_ref[h], NT_DIMS, preferred_element_type=F32) for h in range(hp)]
            dps = [lax.dot_general(do_b[h], v_ref[:, cols[h]], NT_DIMS, preferred_element_type=F32) for h in range(hp)]
            p_b, ds_b = [], []
            for h in range(hp):
                p = jnp.exp(scores[h] * MLA_SCALE - lse_ref[h])
                if diagonal:
                    p = jnp.where(_diag_mask(t), p, 0.0)
                delta = jnp.sum(dy_ref[:, cols[h]] * y_ref[:, cols[h]], axis=-1, keepdims=True)
                p_b.append(p.astype(BF16))
                ds_b.append((p * (dps[h] - delta) * MLA_SCALE).astype(BF16))
            for h in range(hp):
                dv_sc[h] += lax.dot_general(p_b[h], do_b[h], TN_DIMS, preferred_element_type=F32)
                dk_sc[h] += lax.dot_general(ds_b[h], q_ref[h], TN_DIMS, preferred_element_type=F32)
                dq_ref[h, rows, :] += jnp.dot(ds_b[h], k_ref[h], preferred_element_type=F32)

        @pl.when(i > j)
        def _():
            update(False)

        @pl.when(i == j)
        def _():
            update(True)

        @pl.when(i == nt - 1)
        def _():
            dk_ref[...] = dk_sc[...]
            for h in range(hp):
                dv_ref[:, MLA_V * h:MLA_V * (h + 1)] = dv_sc[h]

    return _pcall(
        body, (q_cat, k_cat, v_b, y_b, lse, d_y), name="mla_bwd", grid=(nh // hp, qi.shape[0]), prefetch=(qi, kj),
        in_specs=[pl.BlockSpec((hp, t, 256), lambda g, p, qi, kj: (g, qi[p], 0)),
                  pl.BlockSpec((hp, t, 256), lambda g, p, qi, kj: (g, kj[p], 0)),
                  pl.BlockSpec((t, wv), lambda g, p, qi, kj: (kj[p], g)),
                  pl.BlockSpec((t, wv), lambda g, p, qi, kj: (qi[p], g)),
                  pl.BlockSpec((hp, t, 1), lambda g, p, qi, kj: (g, qi[p], 0)),
                  pl.BlockSpec((t, wv), lambda g, p, qi, kj: (qi[p], col0 + g))],
        out_specs=[pl.BlockSpec((hp, s, 256), lambda g, p, qi, kj: (g, 0, 0)),
                   pl.BlockSpec((hp, t, 256), lambda g, p, qi, kj: (g, kj[p], 0)),
                   pl.BlockSpec((t, wv), lambda g, p, qi, kj: (kj[p], g))],
        out_shape=[jax.ShapeDtypeStruct((nh, s, 256), F32), jax.ShapeDtypeStruct((nh, s, 256), F32),
                   jax.ShapeDtypeStruct((s, nh * MLA_V), F32)],
        scratch_shapes=[pltpu.VMEM((hp, t, 256), F32), pltpu.VMEM((hp, t, MLA_V), F32)],
        sem=("arbitrary", "arbitrary"), comm=comm)


MEM_SCALE = MEM_DIM ** -0.5


def _mem_kv_fwd(mem, g_mem, w_memkv, g_mk):
    m_len = mem.shape[0]

    def body(mem_ref, g_ref, w_ref, gk_ref, mn_ref, kv_ref, kn_ref, v_ref):
        mn, _, _ = _norm_full(mem_ref[...], g_ref[...])
        mn_b = mn.astype(BF16)
        mn_ref[...] = mn_b
        kv = jnp.dot(mn_b, w_ref[...], preferred_element_type=F32)
        kv_ref[...] = kv
        for h in range(MEM_HEADS):
            kn, _, _ = _norm_full(kv[:, 128 * h:128 * (h + 1)], gk_ref[...])
            kn_ref[:, 128 * h:128 * (h + 1)] = kn.astype(BF16)
        v_ref[...] = kv[:, 512:1024].astype(BF16)

    return pl.pallas_call(
        body, name="mem_kv_fwd",
        out_shape=[jax.ShapeDtypeStruct((m_len, D_MODEL), BF16), jax.ShapeDtypeStruct((m_len, 1024), F32),
                   jax.ShapeDtypeStruct((m_len, 512), BF16), jax.ShapeDtypeStruct((m_len, 512), BF16)],
        compiler_params=_params(),
    )(mem, g_mem, w_memkv, g_mk)


def _mem_kv_bwd(mem, g_mem, w_memkv, g_mk, mn_b, kv, d_kn, d_v):
    m_len = mem.shape[0]

    def body(mem_ref, g_ref, w_ref, gk_ref, mn_ref, kv_ref, dkn_ref, dv_ref, dw_ref, dgmem_ref, dgk_ref):
        parts = []
        dgk = jnp.zeros((1, LANES), F32)
        for h in range(MEM_HEADS):
            _, xhat, r = _norm_full(kv_ref[:, 128 * h:128 * (h + 1)], gk_ref[...])
            dx, dgh = _norm_full_bwd(dkn_ref[:, 128 * h:128 * (h + 1)], gk_ref[...], xhat, r)
            parts.append(dx)
            dgk = dgk + dgh
        d_kv = jnp.concatenate(parts + [dv_ref[...]], axis=1).astype(BF16)
        dw_ref[...] = lax.dot_general(mn_ref[...], d_kv, TN_DIMS, preferred_element_type=F32)
        d_mn = lax.dot_general(d_kv, w_ref[...], NT_DIMS, preferred_element_type=F32)
        _, xhat, _ = _norm_full(mem_ref[...], g_ref[...])
        dgmem_ref[...] = jnp.sum(d_mn * xhat, axis=0, keepdims=True)
        dgk_ref[...] = dgk

    return pl.pallas_call(
        body, name="mem_kv_bwd",
        out_shape=[jax.ShapeDtypeStruct((D_MODEL, 1024), F32), jax.ShapeDtypeStruct((1, D_MODEL), F32),
                   jax.ShapeDtypeStruct((1, LANES), F32)],
        compiler_params=_params(),
    )(mem, g_mem, w_memkv, g_mk, mn_b, kv, d_kn, d_v)


def _mem_probs(q_h, k_h):
    sc = lax.dot_general(q_h, k_h, NT_DIMS, preferred_element_type=F32) * MEM_SCALE
    e = jnp.exp(sc - jnp.max(sc, axis=-1, keepdims=True))
    return e * (1.0 / jnp.sum(e, axis=-1, keepdims=True))


def _mem_attn_fwd(qm, km, vm, *, tm=512):
    s = qm.shape[0]
    tm = _tile(s, tm)
    m_len = km.shape[0]

    def body(q_ref, k_ref, v_ref, o_ref):
        for h in range(MEM_HEADS):
            sl = slice(128 * h, 128 * (h + 1))
            p = _mem_probs(q_ref[:, sl], k_ref[:, sl])
            o_ref[:, sl] = jnp.dot(p.astype(BF16), v_ref[:, sl], preferred_element_type=F32)

    kvspec = pl.BlockSpec((m_len, 512), lambda i: (0, 0))
    return pl.pallas_call(
        body, name="mem_attn_fwd", grid=(s // tm,),
        in_specs=[pl.BlockSpec((tm, 512), lambda i: (i, 0)), kvspec, kvspec],
        out_specs=pl.BlockSpec((tm, 512), lambda i: (i, 0)),
        out_shape=jax.ShapeDtypeStruct((s, 512), F32),
        compiler_params=_params(("parallel",)),
    )(qm, km, vm)


def _mem_attn_bwd(qm, km, vm, y_m, d_y, *, tm=512):
    s = qm.shape[0]
    tm = _tile(s, tm)
    m_len = km.shape[0]
    col0 = (SWA_Q_HEADS * SWA_HEAD_DIM + MLA_HEADS * MLA_V) // 512

    def body(q_ref, k_ref, v_ref, y_ref, dy_ref, dq_ref, dk_ref, dv_ref):
        @pl.when(pl.program_id(0) == 0)
        def _():
            dk_ref[...] = jnp.zeros_like(dk_ref)
            dv_ref[...] = jnp.zeros_like(dv_ref)

        for h in range(MEM_HEADS):
            sl = slice(128 * h, 128 * (h + 1))
            q_h, k_h = q_ref[:, sl], k_ref[:, sl]
            do = dy_ref[:, sl]
            do_b = do.astype(BF16)
            p = _mem_probs(q_h, k_h)
            delta = jnp.sum(do * y_ref[:, sl], axis=-1, keepdims=True)
            dv_ref[:, sl] += lax.dot_general(p.astype(BF16), do_b, TN_DIMS, preferred_element_type=F32)
            dp = lax.dot_general(do_b, v_ref[:, sl], NT_DIMS, preferred_element_type=F32)
            ds_b = (p * (dp - delta) * MEM_SCALE).astype(BF16)
            dq_ref[:, sl] = jnp.dot(ds_b, k_h, preferred_element_type=F32)
            dk_ref[:, sl] += lax.dot_general(ds_b, q_h, TN_DIMS, preferred_element_type=F32)

    kvspec = pl.BlockSpec((m_len, 512), lambda i: (0, 0))
    row = pl.BlockSpec((tm, 512), lambda i: (i, 0))
    return pl.pallas_call(
        body, name="mem_attn_bwd", grid=(s // tm,),
        in_specs=[row, kvspec, kvspec, row, pl.BlockSpec((tm, 512), lambda i: (i, col0))],
        out_specs=[row, kvspec, kvspec],
        out_shape=[jax.ShapeDtypeStruct((s, 512), F32), jax.ShapeDtypeStruct((m_len, 512), F32),
                   jax.ShapeDtypeStruct((m_len, 512), F32)],
        compiler_params=_params(("arbitrary",)),
    )(qm, km, vm, y_m, d_y)


def _ffn_gate_up(fn, w_gate, w_up, *, tm=512, comm=()):
    s, d = fn.shape
    nsp, _, tf = w_gate.shape
    f = nsp * tf
    tm = _tile(s, tm)

    def body(x_ref, wg_ref, wu_ref, g_ref, u_ref, a_ref):
        x = x_ref[...]
        gate = jnp.dot(x, wg_ref[...], preferred_element_type=F32)
        up = jnp.dot(x, wu_ref[...], preferred_element_type=F32)
        g_ref[...] = gate.astype(BF16)
        u_ref[...] = up.astype(BF16)
        a_ref[...] = (gate * (1.0 / (1.0 + jnp.exp(-gate))) * up).astype(BF16)

    wspec = pl.BlockSpec((None, d, tf), lambda j, i: (j, 0, 0))
    ospec = pl.BlockSpec((tm, tf), lambda j, i: (i, j))
    osh = jax.ShapeDtypeStruct((s, f), BF16)
    return _pcall(body, (fn, w_gate, w_up), name="ffn_gate_up", grid=(nsp, s // tm),
                  in_specs=[pl.BlockSpec((tm, d), lambda j, i: (i, 0)), wspec, wspec],
                  out_specs=[ospec, ospec, ospec], out_shape=[osh, osh, osh], sem=("parallel", "parallel"), comm=comm)


def _ffn_bwd_act(d_out, w_down, gate, up, *, tm=512, tf=1408, comm=()):
    s, d = d_out.shape
    f = w_down.shape[0]
    tm, tf = _tile(s, tm), _tile(f, tf)

    def body(do_ref, wd_ref, g_ref, u_ref, dg_ref, du_ref):
        d_act = lax.dot_general(do_ref[...].astype(BF16), wd_ref[...], NT_DIMS, preferred_element_type=F32)
        gate = g_ref[...].astype(F32)
        sig = 1.0 / (1.0 + jnp.exp(-gate))
        du_ref[...] = (d_act * (gate * sig)).astype(BF16)
        dg_ref[...] = (d_act * u_ref[...].astype(F32) * (sig * (1.0 + gate * (1.0 - sig)))).astype(BF16)

    ospec = pl.BlockSpec((tm, tf), lambda j, i: (i, j))
    osh = jax.ShapeDtypeStruct((s, f), BF16)
    return _pcall(
        body, (d_out, w_down, gate, up), name="ffn_bwd_act", grid=(f // tf, s // tm),
        in_specs=[pl.BlockSpec((tm, d), lambda j, i: (i, 0)), pl.BlockSpec((tf, d), lambda j, i: (j, 0)), ospec, ospec],
        out_specs=[ospec, ospec], out_shape=[osh, osh], sem=("parallel", "parallel"), comm=comm)


def _loss_head(out, target, *, tm=512):
    s, d = out.shape
    tm = _tile(s, tm)

    def body(o_ref, t_ref, d_ref, db_ref, l_ref):
        err = o_ref[...] - t_ref[...]
        d_out = err * (1.0 / d)
        d_ref[...] = d_out
        db_ref[...] = d_out.astype(BF16)
        part = 0.5 * jnp.sum(jnp.mean(err * err, axis=-1, keepdims=True), axis=0, keepdims=True)
        part = jnp.broadcast_to(part, (1, LANES))

        @pl.when(pl.program_id(0) == 0)
        def _():
            l_ref[...] = part

        @pl.when(pl.program_id(0) > 0)
        def _():
            l_ref[...] += part

    row = pl.BlockSpec((tm, d), lambda i: (i, 0))
    return pl.pallas_call(
        body, name="loss_head", grid=(s // tm,),
        in_specs=[row, row], out_specs=[row, row, pl.BlockSpec((1, LANES), lambda i: (0, 0))],
        out_shape=[jax.ShapeDtypeStruct((s, d), F32), jax.ShapeDtypeStruct((s, d), BF16),
                   jax.ShapeDtypeStruct((1, LANES), F32)],
        compiler_params=_params(("arbitrary",)),
    )(out, target)


def _cols(g4):
    return jnp.concatenate([g4[k] for k in range(N_CHIPS)], axis=1)


def _full_w_in(g4):
    nat = _cols(g4)
    pad = jnp.zeros((nat.shape[0], IN_PAD - IN_WIDTH), nat.dtype)
    return jnp.concatenate([nat[:, :2304], nat[:, 2368:], nat[:, 2304:2368], pad], axis=1)


def _shards_w_in(dwp):
    nat = jnp.concatenate([dwp[:, :2304], dwp[:, C_KR:C_KR + 64], dwp[:, 2304:C_KR]], axis=1)
    per = IN_WIDTH // N_CHIPS
    return jnp.stack([nat[:, per * k:per * (k + 1)] for k in range(N_CHIPS)])


def _full_heads(g4, first):
    return jnp.concatenate([g4[k][:, :first] for k in range(N_CHIPS)] + [g4[k][:, first:] for k in range(N_CHIPS)], axis=1)


def _shards_heads(dwp, first, rest):
    base = N_CHIPS * first
    return jnp.stack([jnp.concatenate([dwp[:, first * k:first * (k + 1)], dwp[:, base + rest * k:base + rest * (k + 1)]], axis=1)
                      for k in range(N_CHIPS)])


def _rope_tables(pos):
    inv_freq = ROPE_THETA ** (-jnp.arange(0, MLA_ROPE, 2, dtype=F32) / MLA_ROPE)
    ang = pos.astype(F32)[:, None] * inv_freq
    cos, sin = jnp.cos(ang), jnp.sin(ang)
    return jnp.tile(cos, (1, 4)), jnp.concatenate([-sin, sin, -sin, sin], axis=1)


def _gain_table(sp):
    two = lambda v: jnp.tile(v, (1, 2))
    rows = [two(sp["swa_q_norm_g"]), two(sp["swa_k_norm_g"]), sp["mla_qn_norm_g"], two(sp["mla_qr_norm_g"]),
            sp["mla_kn_norm_g"], two(sp["mla_kr_norm_g"]), sp["mem_q_norm_g"], jnp.zeros((1, LANES), F32)]
    return jnp.concatenate(rows, axis=0)


CHIP_DISTANCES = (1, 2, 3)


def _place():
    x, y, c = lax.axis_index("x"), lax.axis_index("y"), lax.axis_index("c")
    return x, y, c, 2 * x + y


def _chip_at(x, y, d):
    px = 1 - x if d & 2 else x
    py = 1 - y if d & 1 else y
    return px, py, 2 * px + py


def _row_tile(rows, want=512, mult=8):
    t = min(rows, want)
    t -= t % mult
    while rows % t:
        t -= mult
    return t


def _cast_into_slot(w, meta, *, name):
    rows, cols = w.shape
    tr = _row_tile(rows, 512, 16)

    def body(meta_ref, w_ref, o_ref):
        o_ref[...] = w_ref[...].astype(BF16)

    grid_spec = pltpu.PrefetchScalarGridSpec(
        num_scalar_prefetch=1, grid=(rows // tr,),
        in_specs=[pl.BlockSpec((tr, cols), lambda i, m: (i, 0))],
        out_specs=pl.BlockSpec((None, tr, cols), lambda i, m: (m[0], i, 0)))
    return pl.pallas_call(
        body, name=name, grid_spec=grid_spec,
        out_shape=jax.ShapeDtypeStruct((N_CHIPS, rows, cols), BF16),
        compiler_params=_params(("parallel",)),
    )(meta, w)


def _remote(src, dst, ssem, rsem, i, device):
    return pltpu.make_async_remote_copy(src_ref=src, dst_ref=dst, send_sem=ssem.at[i], recv_sem=rsem.at[i],
                                        device_id=device, device_id_type=MESH)


def _symmetric_stage(ins, out_shapes, aliases, n_sem, copies):
    def issue(i_refs, o_refs, ssem, rsem):
        for send, _ in copies(i_refs, o_refs, ssem, rsem):
            send.start()

    def wait(i_refs, o_refs, ssem, rsem):
        pairs = copies(i_refs, o_refs, ssem, rsem)
        for _, arrival in pairs:
            arrival.wait_recv()
        for send, _ in pairs:
            send.wait_send()

    return _Stage(ins, out_shapes, aliases, n_sem, issue, wait)


def _gather_stage(slots, leg):
    n = len(slots)

    def leg_copies(which, base):
        def copies(_, outs, ssem, rsem):
            x, y, c, k_me = _place()
            pairs = []
            for w in range(n):
                half = outs[w].shape[1] // 2
                slab = lambda k, cc, w=w, half=half: outs[w].at[k, pl.ds(cc * half, half)]
                for d in CHIP_DISTANCES:
                    px, py, k_src = _chip_at(x, y, d)
                    i = base + 3 * w + d - 1
                    if which == "ici":
                        pairs.append((_remote(slab(k_me, c), slab(k_me, c), ssem, rsem, i, (px, py, c)),
                                      _remote(slab(k_src, c), slab(k_src, c), ssem, rsem, i, (x, y, c))))
                    else:
                        pairs.append((_remote(slab(k_src, c), slab(k_src, c), ssem, rsem, i, (x, y, 1 - c)),
                                      _remote(slab(k_src, 1 - c), slab(k_src, 1 - c), ssem, rsem, i, (x, y, c))))
            return pairs
        return copies

    shapes = [jax.ShapeDtypeStruct(s.shape, s.dtype) for s in slots]
    in_place = {w: w for w in range(n)}
    if leg != "both":
        return _symmetric_stage(slots, shapes, in_place, 3 * n, leg_copies(leg, 0))
    ici = _symmetric_stage(slots, shapes, in_place, 6 * n, leg_copies("ici", 0))
    d2d = _symmetric_stage(slots, shapes, in_place, 6 * n, leg_copies("d2d", 3 * n))

    def mid(*refs):
        ici.wait(*refs)
        d2d.issue(*refs)

    return _Stage(slots, shapes, in_place, 6 * n, ici.issue, d2d.wait, mid)


def _halves_stage(grads):
    n = len(grads)

    def copies(ins, outs, ssem, rsem):
        x, y, c, _ = _place()
        pairs = []
        for w in range(n):
            half = ins[w].shape[1] // 2
            pairs.append((_remote(ins[w].at[:, pl.ds((1 - c) * half, half)], outs[w], ssem, rsem, w, (x, y, 1 - c)),
                          _remote(outs[w], outs[w], ssem, rsem, w, (x, y, c))))
        return pairs

    shapes = [jax.ShapeDtypeStruct((N_CHIPS, g.shape[1] // 2, g.shape[2]), g.dtype) for g in grads]
    return _symmetric_stage(grads, shapes, {}, n, copies)


def _chips_stage(parts):
    n = len(parts)

    def copies(ins, outs, ssem, rsem):
        x, y, c, _ = _place()
        pairs = []
        for w in range(n):
            for d in CHIP_DISTANCES:
                px, py, _ = _chip_at(x, y, d)
                i = 3 * w + d - 1
                pairs.append((_remote(ins[w].at[d - 1], outs[w].at[d - 1], ssem, rsem, i, (px, py, c)),
                              _remote(outs[w].at[d - 1], outs[w].at[d - 1], ssem, rsem, i, (x, y, c))))
        return pairs

    shapes = [jax.ShapeDtypeStruct(p.shape, p.dtype) for p in parts]
    return _symmetric_stage(parts, shapes, {}, 3 * n, copies)


def _swap_stage(totals):
    n = len(totals)

    def copies(ins, outs, ssem, rsem):
        x, y, c, _ = _place()
        return [(_remote(ins[w], outs[w], ssem, rsem, w, (x, y, 1 - c)),
                 _remote(outs[w], outs[w], ssem, rsem, w, (x, y, c))) for w in range(n)]

    shapes = [jax.ShapeDtypeStruct(t.shape, t.dtype) for t in totals]
    return _symmetric_stage(totals, shapes, {}, n, copies)


def _run_stage(st, *, name):
    n_in, n_out = len(st.ins), len(st.out_shapes)

    def body(*refs):
        ins, outs, (ssem, rsem) = refs[:n_in], refs[n_in:n_in + n_out], refs[n_in + n_out:]
        st.issue(ins, outs, ssem, rsem)
        st.wait(ins, outs, ssem, rsem)

    sem = pltpu.SemaphoreType.DMA
    return list(pl.pallas_call(
        body, name=name, in_specs=[ANY] * n_in, out_specs=[ANY] * n_out, out_shape=st.out_shapes,
        input_output_aliases=st.aliases, scratch_shapes=[sem((st.n_sem,)), sem((st.n_sem,))],
    )(*st.ins))


def _add_pair(meta, g4, recv, *, name):
    nsh, rows, cols = g4.shape
    half = rows // 2
    tr = _row_tile(half, 128 if cols > 1024 else 256, 16)
    nt = half // tr

    def body(meta_ref, g0, g1, g2, g3, r0, r1, r2, r3, own_ref, oth_ref):
        own_ref[...] = g0[...] + r0[...]
        for d, (g, r) in enumerate(((g1, r1), (g2, r2), (g3, r3))):
            oth_ref[d] = (g[...] + r[...]).astype(BF16)

    blk = (None, tr, cols)
    gspec = lambda d: pl.BlockSpec(blk, lambda i, m: (jnp.bitwise_xor(m[0], d), m[1] * nt + i, 0))
    rspec = lambda d: pl.BlockSpec(blk, lambda i, m: (jnp.bitwise_xor(m[0], d), i, 0))
    grid_spec = pltpu.PrefetchScalarGridSpec(
        num_scalar_prefetch=1, grid=(nt,),
        in_specs=[gspec(d) for d in range(nsh)] + [rspec(d) for d in range(nsh)],
        out_specs=[pl.BlockSpec((tr, cols), lambda i, m: (i, 0)), pl.BlockSpec((3, tr, cols), lambda i, m: (0, i, 0))])
    return pl.pallas_call(
        body, name=name, grid_spec=grid_spec,
        out_shape=[jax.ShapeDtypeStruct((half, cols), F32), jax.ShapeDtypeStruct((3, half, cols), BF16)],
        compiler_params=_params(("parallel",)),
    )(meta, g4, g4, g4, g4, recv, recv, recv, recv)


def _add_chips(own, recv, *, name):
    half, cols = own.shape
    tr = _row_tile(half, 256, 16)

    def body(p_ref, r_ref, o_ref):
        o_ref[...] = ((p_ref[...] + r_ref[0].astype(F32)) + r_ref[1].astype(F32)) + r_ref[2].astype(F32)

    return pl.pallas_call(
        body, name=name, grid=(half // tr,),
        in_specs=[pl.BlockSpec((tr, cols), lambda i: (i, 0)), pl.BlockSpec((3, tr, cols), lambda i: (0, i, 0))],
        out_specs=pl.BlockSpec((tr, cols), lambda i: (i, 0)),
        out_shape=jax.ShapeDtypeStruct((half, cols), F32),
        compiler_params=_params(("parallel",)),
    )(own, recv)


def _adamw_math(w, g, m, v):
    m = ADAM_B1 * m + (1.0 - ADAM_B1) * g
    v = ADAM_B2 * v + (1.0 - ADAM_B2) * (g * g)
    m_hat = m / (1.0 - ADAM_B1 ** ADAM_STEP)
    v_hat = v / (1.0 - ADAM_B2 ** ADAM_STEP)
    delta = -ADAM_LR * (m_hat / (jnp.sqrt(v_hat) + ADAM_EPS) + ADAM_WD * w)
    return delta, m, v


def _adamw(meta, w, g_mine, g_theirs, m, v, *, name):
    rows, cols = w.shape
    half = rows // 2
    tr = _row_tile(half, 256)
    nt = half // tr

    def body(meta_ref, w_ref, a_ref, b_ref, m_ref, v_ref, g_ref, d_ref, mo_ref, vo_ref):
        is_mine = (pl.program_id(0) // nt) == meta_ref[1]
        g = jnp.where(is_mine, a_ref[...], b_ref[...])
        g_ref[...] = g
        d_ref[...], mo_ref[...], vo_ref[...] = _adamw_math(w_ref[...], g, m_ref[...], v_ref[...])

    blk = pl.BlockSpec((tr, cols), lambda i, mt: (i, 0))
    mine = pl.BlockSpec((tr, cols), lambda i, mt: (jnp.where(i // nt == mt[1], i % nt, 0), 0))
    theirs = pl.BlockSpec((tr, cols), lambda i, mt: (jnp.where(i // nt == mt[1], 0, i % nt), 0))
    sh = jax.ShapeDtypeStruct((rows, cols), F32)
    grid_spec = pltpu.PrefetchScalarGridSpec(
        num_scalar_prefetch=1, grid=(rows // tr,),
        in_specs=[blk, mine, theirs, blk, blk], out_specs=[blk] * 4)
    return pl.pallas_call(
        body, name=name, grid_spec=grid_spec, out_shape=[sh] * 4,
        compiler_params=_params(("arbitrary",)),
    )(meta, w, g_mine, g_theirs, m, v)


N_DEVICES = 8


def _small_step(g_pack, w_pack, m_pack, v_pack):
    rows = g_pack.shape[0]

    def body(g_ref, w_ref, m_ref, v_ref, sum_ref, d_ref, mo_ref, vo_ref, slots, ssem, rsem):
        x, y, c, _ = _place()
        me = 4 * x + 2 * y + c
        slots[me] = g_ref[...]
        copies = []
        for r in range(1, N_DEVICES):
            px = 1 - x if r & 4 else x
            py = 1 - y if r & 2 else y
            pc = 1 - c if r & 1 else c
            copies.append(pltpu.make_async_remote_copy(
                src_ref=g_ref, dst_ref=slots.at[me], send_sem=ssem.at[r - 1], recv_sem=rsem.at[r - 1],
                device_id=(px, py, pc), device_id_type=MESH))
        for cp in copies:
            cp.start()
        for r in range(1, N_DEVICES):
            src = jnp.bitwise_xor(me, r)
            pltpu.make_async_remote_copy(
                src_ref=g_ref, dst_ref=slots.at[src], send_sem=ssem.at[r - 1], recv_sem=rsem.at[r - 1],
                device_id=(x, y, c), device_id_type=MESH).wait_recv()
        for cp in copies:
            cp.wait_send()
        total = slots[0]
        for k in range(1, N_DEVICES):
            total = total + slots[k]
        sum_ref[...] = total
        d_ref[...], mo_ref[...], vo_ref[...] = _adamw_math(w_ref[...], total, m_ref[...], v_ref[...])

    sh = jax.ShapeDtypeStruct((rows, LANES), F32)
    vm = pl.BlockSpec(memory_space=pltpu.VMEM)
    return pl.pallas_call(
        body, name="small_allreduce_adamw",
        in_specs=[vm] * 4, out_specs=[vm] * 4, out_shape=[sh] * 4,
        scratch_shapes=[pltpu.VMEM((N_DEVICES, rows, LANES), F32),
                        pltpu.SemaphoreType.DMA((N_DEVICES - 1,)), pltpu.SemaphoreType.DMA((N_DEVICES - 1,))],
    )(g_pack, w_pack, m_pack, v_pack)


WEIGHTS = ("attn_norm_g", "w_in", "swa_q_norm_g", "swa_k_norm_g", "swa_sinks", "mla_cq_norm_g", "mla_ckv_norm_g",
           "w_uq", "w_ukv", "mla_qn_norm_g", "mla_qr_norm_g", "mla_kn_norm_g", "mla_kr_norm_g", "mem_norm_g",
           "w_mem_kv", "mem_q_norm_g", "mem_k_norm_g", "w_out", "ffn_norm_g", "w_gate", "w_up", "w_down")
BIG = ("w_in", "w_uq", "w_ukv", "w_mem_kv", "w_out", "w_gate", "w_up", "w_down")
SMALL = tuple(n for n in WEIGHTS if n not in BIG)
PACK_UNIT = 8 * LANES


def _pack(parts):
    out = []
    for p in parts:
        n = p.shape[1]
        padded = -(-n // PACK_UNIT) * PACK_UNIT
        out.append(jnp.pad(p, ((0, 0), (0, padded - n))).reshape(padded // LANES, LANES))
    return jnp.concatenate(out, axis=0)


def _unpack(buf, sizes):
    out, row = [], 0
    for n in sizes:
        rows = -(-n // PACK_UNIT) * 8
        out.append(buf[row:row + rows].reshape(1, rows * LANES)[:, :n])
        row += rows
    return out


def kernel(x, mem, positions, attn_norm_g, w_in, swa_q_norm_g, swa_k_norm_g, swa_sinks, mla_cq_norm_g, mla_ckv_norm_g, w_uq, w_ukv, mla_qn_norm_g, mla_qr_norm_g, mla_kn_norm_g, mla_kr_norm_g, mem_norm_g, w_mem_kv, mem_q_norm_g, mem_k_norm_g, w_out, ffn_norm_g, w_gate, w_up, w_down, loss_target, m_attn_norm_g, m_w_in, m_swa_q_norm_g, m_swa_k_norm_g, m_swa_sinks, m_mla_cq_norm_g, m_mla_ckv_norm_g, m_w_uq, m_w_ukv, m_mla_qn_norm_g, m_mla_qr_norm_g, m_mla_kn_norm_g, m_mla_kr_norm_g, m_mem_norm_g, m_w_mem_kv, m_mem_q_norm_g, m_mem_k_norm_g, m_w_out, m_ffn_norm_g, m_w_gate, m_w_up, m_w_down, v_attn_norm_g, v_w_in, v_swa_q_norm_g, v_swa_k_norm_g, v_swa_sinks, v_mla_cq_norm_g, v_mla_ckv_norm_g, v_w_uq, v_w_ukv, v_mla_qn_norm_g, v_mla_qr_norm_g, v_mla_kn_norm_g, v_mla_kr_norm_g, v_mem_norm_g, v_w_mem_kv, v_mem_q_norm_g, v_mem_k_norm_g, v_w_out, v_ffn_norm_g, v_w_gate, v_w_up, v_w_down):
    given = dict(locals())
    wts = {n: given[n] for n in WEIGHTS}
    mom_m = {n: given["m_" + n] for n in WEIGHTS}
    mom_v = {n: given["v_" + n] for n in WEIGHTS}

    mx, my, mc = lax.axis_index("x"), lax.axis_index("y"), lax.axis_index("c")
    meta = jnp.stack([2 * mx + my, mc]).astype(jnp.int32)
    x, mem, pos, target = x[0], mem[0], positions[0], loss_target[0]
    sp = {n: wts[n] for n in SMALL}
    s = x.shape[0]
    cos_t, sin_t = _rope_tables(pos)
    pos_f = pos.astype(F32)
    pos_col, pos_row = pos_f.reshape(s, 1), pos_f.reshape(1, s)
    g128 = _gain_table(sp)
    sinks = sp["swa_sinks"].reshape(SWA_Q_HEADS)
    gcq, gckv = sp["mla_cq_norm_g"], sp["mla_ckv_norm_g"]
    gs = {}

    slot = {n: _cast_into_slot(wts[n][0], meta, name="cast_" + n) for n in BIG}
    first = _run_stage(_gather_stage([slot["w_in"], slot["w_uq"], slot["w_ukv"]], "ici"), name="gather_first_ici")
    first = _run_stage(_gather_stage(first, "d2d"), name="gather_first_d2d")
    w_in_f, w_uq_f, w_ukv_f = _full_w_in(first[0]), _full_heads(first[1], MLA_NOPE), _full_heads(first[2], MLA_NOPE)

    hn = _rms_fwd(x, sp["attn_norm_g"], name="attn_norm_fwd")
    proj, [mid] = _matmul(hn, w_in_f, name="in_proj",
                          comm=[_gather_stage([slot["w_mem_kv"], slot["w_out"]], "ici")])
    (qa, ka, va, q_cat, k_cat, v_b, qm), [mid] = _attn_prep_fwd(
        proj, g128, gcq, gckv, w_uq_f, w_ukv_f, cos_t, sin_t, comm=[_gather_stage(mid, "d2d")])
    w_mem_kv_f = mid[0].reshape(D_MODEL, 2 * MEM_HEADS * MEM_DIM)
    w_out_f = mid[1].reshape(D_MODEL, D_MODEL)
    mn_b, kv_m, km, vm = _mem_kv_fwd(mem, sp["mem_norm_g"], w_mem_kv_f, sp["mem_k_norm_g"])
    y_a, [wg] = _swa_fwd(qa, ka, va, pos_col, pos_row, sinks, comm=[_gather_stage([slot["w_gate"]], "ici")])
    (y_b, lse), [wu, wg] = _mla_fwd(
        q_cat, k_cat, v_b, comm=[_gather_stage([slot["w_up"]], "ici"), _gather_stage(wg, "d2d")])
    y_m = _mem_attn_fwd(qm, km, vm)
    y = jnp.concatenate([y_a, y_b, y_m], axis=1).astype(BF16)
    h1, [wu] = _matmul(y, w_out_f, add=x, name="out_proj", comm=[_gather_stage(wu, "d2d")])
    w_gate_f, w_up_f = wg[0], wu[0]
    fn = _rms_fwd(h1, sp["ffn_norm_g"], name="ffn_norm_fwd")
    (gate, up, act), [wd] = _ffn_gate_up(fn, w_gate_f, w_up_f, comm=[_gather_stage([slot["w_down"]], "both")])
    w_down_f = wd[0].reshape(D_FF, D_MODEL)
    out = _matmul(act, w_down_f, add=h1, name="down_proj", tk=1408)
    d_out, d_out_b, loss_tile = _loss_head(out, target)

    add_pair = lambda n, g4, r: _add_pair(meta, g4, r, name="grad_add_pair_" + n)
    add_chips = lambda n, own, r: _add_chips(own, r, name="grad_add_chips_" + n)
    mine, theirs = {}, {}

    dw_down = _matmul(act, d_out_b, ta=True, name="dw_down", tm=1408, tn=1024, tk=2048)
    dw_down = dw_down.reshape(N_CHIPS, D_FF // N_CHIPS, D_MODEL)
    (d_gate, d_up), [[r]] = _ffn_bwd_act(d_out_b, w_down_f, gate, up, comm=[_halves_stage([dw_down])])
    own_d, oth_d = add_pair("w_down", dw_down, r)
    dw_gate, [[r]] = _matmul(fn, d_gate, ta=True, name="dw_gate", tk=2048, tn=D_FF // N_CHIPS, out_split=N_CHIPS,
                             comm=[_chips_stage([oth_d])])
    mine["w_down"] = add_chips("w_down", own_d, r)
    dw_up, [[r]] = _matmul(fn, d_up, ta=True, name="dw_up", tk=2048, tn=D_FF // N_CHIPS, out_split=N_CHIPS,
                           comm=[_halves_stage([dw_gate])])
    own_g, oth_g = add_pair("w_gate", dw_gate, r)
    d_fn, [[r], [theirs["w_down"]]] = _matmul(
        d_gate, w_gate_f, tb=True, b_split=True, name="dfn_gate",
        comm=[_chips_stage([oth_g]), _swap_stage([mine["w_down"]])])
    mine["w_gate"] = add_chips("w_gate", own_g, r)
    d_fn, [[r]] = _matmul(d_up, w_up_f, tb=True, b_split=True, add=d_fn, name="dfn_up",
                          comm=[_halves_stage([dw_up])])
    own_u, oth_u = add_pair("w_up", dw_up, r)
    d_h1, d_h1_b, gs["ffn_norm_g"] = _rms_bwd(d_fn, h1, sp["ffn_norm_g"], d_out, name="ffn_norm_bwd")
    dw_out, [[theirs["w_gate"]]] = _matmul(y, d_h1_b, ta=True, name="dw_out", tk=2048,
                                           comm=[_swap_stage([mine["w_gate"]])])
    dw_out = dw_out.reshape(N_CHIPS, D_MODEL // N_CHIPS, D_MODEL)
    d_y, [[r]] = _matmul(d_h1_b, w_out_f, tb=True, name="dy", comm=[_halves_stage([dw_out])])
    own_o, oth_o = add_pair("w_out", dw_out, r)
    (d_qa, d_ka, d_va, d_sink), [[r]] = _swa_bwd(qa, ka, va, pos_col, pos_row, sinks, y_a, d_y,
                                                 comm=[_chips_stage([oth_u])])
    mine["w_up"] = add_chips("w_up", own_u, r)
    (d_qcat, d_kcat, d_vb), [[r], [theirs["w_up"]]] = _mla_bwd(
        q_cat, k_cat, v_b, y_b, lse, d_y, comm=[_chips_stage([oth_o]), _swap_stage([mine["w_up"]])])
    mine["w_out"] = add_chips("w_out", own_o, r)
    d_qm, d_km, d_vm = _mem_attn_bwd(qm, km, vm, y_m, d_y)
    (d_proj, dw_uq, dw_ukv, dg128, gs["mla_cq_norm_g"], gs["mla_ckv_norm_g"]), [[theirs["w_out"]]] = _attn_prep_bwd(
        proj, g128, gcq, gckv, w_uq_f, w_ukv_f, cos_t, sin_t, d_qa, d_ka, d_va, d_qcat, d_kcat, d_vb, d_qm,
        comm=[_swap_stage([mine["w_out"]])])
    dw_mem_kv, gs["mem_norm_g"], gs["mem_k_norm_g"] = _mem_kv_bwd(
        mem, sp["mem_norm_g"], w_mem_kv_f, sp["mem_k_norm_g"], mn_b, kv_m, d_km, d_vm)
    late = ("w_uq", "w_ukv", "w_mem_kv")
    late_g = [_shards_heads(dw_uq, MLA_NOPE, MLA_ROPE), _shards_heads(dw_ukv, MLA_NOPE, MLA_V),
              dw_mem_kv.reshape(N_CHIPS, D_MODEL // N_CHIPS, -1)]
    dw_in, [rs] = _matmul(hn, d_proj, ta=True, name="dw_in", tk=2048, comm=[_halves_stage(late_g)])
    late_sums = [add_pair(n, g4, r) for n, g4, r in zip(late, late_g, rs)]
    dw_in = _shards_w_in(dw_in)
    d_hn, [rs, [r]] = _matmul(d_proj, w_in_f, tb=True, name="dhn", tk=1536,
                              comm=[_chips_stage([oth for _, oth in late_sums]), _halves_stage([dw_in])])
    for n, (own, _), r_n in zip(late, late_sums, rs):
        mine[n] = add_chips(n, own, r_n)
    own_i, oth_i = add_pair("w_in", dw_in, r)
    (grad_x, _, gs["attn_norm_g"]), [[r], late_theirs] = _rms_bwd(
        d_hn, x, sp["attn_norm_g"], d_h1, name="attn_norm_bwd",
        comm=[_chips_stage([oth_i]), _swap_stage([mine[n] for n in late])])
    theirs.update(zip(late, late_theirs))
    mine["w_in"] = add_chips("w_in", own_i, r)
    [theirs["w_in"]] = _run_stage(_swap_stage([mine["w_in"]]), name="grad_swap_w_in")

    fold = lambda r: r[:, :64] + r[:, 64:]
    gs["swa_q_norm_g"] = fold(dg128[G_SWA_Q:G_SWA_Q + 1])
    gs["swa_k_norm_g"] = fold(dg128[G_SWA_K:G_SWA_K + 1])
    gs["mla_qn_norm_g"] = dg128[G_QN:G_QN + 1]
    gs["mla_qr_norm_g"] = fold(dg128[G_QR:G_QR + 1])
    gs["mla_kn_norm_g"] = dg128[G_KN:G_KN + 1]
    gs["mla_kr_norm_g"] = fold(dg128[G_KR:G_KR + 1])
    gs["mem_q_norm_g"] = dg128[G_MQ:G_MQ + 1]
    gs["swa_sinks"] = d_sink[:, :SWA_Q_HEADS]

    grad, delta, new_m, new_v = {}, {}, {}, {}
    for n in BIG:
        g2, d, m2, v2 = _adamw(meta, wts[n][0], mine[n], theirs[n], mom_m[n][0], mom_v[n][0], name="adamw_" + n)
        grad[n], delta[n], new_m[n], new_v[n] = g2[None], d[None], m2[None], v2[None]

    sizes = [wts[n].shape[1] for n in SMALL]
    zero = jnp.zeros((1, LANES), F32)
    packs = _small_step(_pack([gs[n] for n in SMALL] + [loss_tile]), _pack([wts[n] for n in SMALL] + [zero]),
                        _pack([mom_m[n] for n in SMALL] + [zero]), _pack([mom_v[n] for n in SMALL] + [zero]))
    for store, buf in zip((grad, delta, new_m, new_v), packs):
        for n, val in zip(SMALL, _unpack(buf, sizes)):
            store[n] = val
    loss = _unpack(packs[0], sizes + [LANES])[-1][0, 0]

    return (loss, grad_x[None], *[grad[n] for n in WEIGHTS], *[delta[n] for n in WEIGHTS],
            *[new_m[n] for n in WEIGHTS], *[new_v[n] for n in WEIGHTS])
```

```python
import functools
import math

import jax
import jax.numpy as jnp
from jax import lax
from jax.experimental import pallas as pl
from jax.experimental.pallas import tpu as pltpu

F32 = jnp.float32
BF16 = jnp.bfloat16

D_MODEL = 2048
BLOCK = 128
EPS = 1e-6
NEG_INF = -1e30
SWA_Q_HEADS = 16
SWA_KV_HEADS = 2
SWA_HEAD_DIM = 64
MLA_HEADS = 4
MLA_RANK = 512
MLA_NOPE = 128
MLA_ROPE = 64
MLA_V = 128
ROPE_THETA = 10000.0
MEM_HEADS = 4
MEM_DIM = 128
D_FF = 5632
IN_WIDTH = 2880
IN_PAD = 3072
N_CHIPS = 4

ADAM_LR = 0.001
ADAM_B1 = 0.9
ADAM_B2 = 0.999
ADAM_EPS = 1e-08
ADAM_WD = 0.01
ADAM_STEP = 10

VMEM_LIMIT_BYTES = 56 * 1024 * 1024
LANES = 128

MESH = pl.DeviceIdType.MESH


def _params(sem=None, **kw):
    return pltpu.CompilerParams(dimension_semantics=sem, vmem_limit_bytes=VMEM_LIMIT_BYTES, **kw)


def _tile(n, want):
    if n <= want:
        return n
    t = want - want % LANES
    while t > 0:
        if n % t == 0:
            return t
        t -= LANES
    return n


ANY = pl.BlockSpec(memory_space=pl.ANY)


class _Stage:
    def __init__(self, ins, out_shapes, aliases, n_sem, issue, wait, mid=None):
        self.ins, self.out_shapes, self.aliases, self.n_sem = list(ins), list(out_shapes), dict(aliases), n_sem
        self.issue, self.wait, self.mid = issue, wait, mid


def _pcall(body, args, *, name, grid, in_specs, out_specs, out_shape, scratch_shapes=(), sem=None, comm=(),
           prefetch=(), io_alias=None):
    multi = isinstance(out_shape, (list, tuple))
    out_specs_l = list(out_specs) if multi else [out_specs]
    out_shape_l = list(out_shape) if multi else [out_shape]
    npf = len(prefetch)
    own_aliases = {npf + a: o for a, o in (io_alias or {}).items()}

    def call(fn, in_specs_, out_specs_, out_shape_, scratch_, operands, sem_, aliases=None):
        kw = dict(name=name, out_shape=out_shape_, compiler_params=_params(sem_))
        if aliases:
            kw["input_output_aliases"] = aliases
        if npf:
            spec = pltpu.PrefetchScalarGridSpec(num_scalar_prefetch=npf, grid=grid, in_specs=in_specs_,
                                                out_specs=out_specs_, scratch_shapes=scratch_)
            return pl.pallas_call(fn, grid_spec=spec, **kw)(*prefetch, *operands)
        return pl.pallas_call(fn, grid=grid, in_specs=in_specs_, out_specs=out_specs_, scratch_shapes=scratch_,
                              **kw)(*operands)

    if not comm:
        return call(body, list(in_specs), out_specs, out_shape, list(scratch_shapes), args, sem, own_aliases)
    n_in, n_out, n_scr = len(in_specs), len(out_specs_l), len(scratch_shapes)
    cins = [a for st in comm for a in st.ins]
    couts = [s for st in comm for s in st.out_shapes]
    aliases, ci, co = dict(own_aliases), 0, 0
    for st in comm:
        for a_i, o_i in st.aliases.items():
            aliases[npf + n_in + ci + a_i] = n_out + co + o_i
        ci, co = ci + len(st.ins), co + len(st.out_shapes)

    def wrapped(*refs):
        pre = refs[:npf]
        p = npf
        ins = refs[p:p + n_in]; p += n_in
        cin_refs = refs[p:p + len(cins)]; p += len(cins)
        outs = refs[p:p + n_out]; p += n_out
        cout_refs = refs[p:p + len(couts)]; p += len(couts)
        scr = refs[p:p + n_scr]; p += n_scr
        sems = refs[p:]
        first = functools.reduce(jnp.logical_and, [pl.program_id(a) == 0 for a in range(len(grid))])
        last = functools.reduce(jnp.logical_and, [pl.program_id(a) == grid[a] - 1 for a in range(len(grid))])

        def each(what):
            i, o = 0, 0
            for k, st in enumerate(comm):
                fn = getattr(st, what)
                if fn is not None:
                    fn(cin_refs[i:i + len(st.ins)], cout_refs[o:o + len(st.out_shapes)], sems[2 * k], sems[2 * k + 1])
                i, o = i + len(st.ins), o + len(st.out_shapes)

        @pl.when(first)
        def _():
            each("issue")

        if any(st.mid is not None for st in comm):
            n_steps = math.prod(grid)
            assert n_steps >= 4, "a two-leg stage needs a carrier with several grid steps"
            lin = functools.reduce(lambda acc, a: acc * grid[a] + pl.program_id(a), range(len(grid)), 0)

            @pl.when(lin == (3 * n_steps) // 4)
            def _():
                each("mid")

        body(*pre, *ins, *outs, *scr)

        @pl.when(last)
        def _():
            each("wait")

    sem_scr = [pltpu.SemaphoreType.DMA((st.n_sem,)) for st in comm for _ in range(2)]
    res = call(wrapped, list(in_specs) + [ANY] * len(cins), out_specs_l + [ANY] * len(couts), out_shape_l + couts,
               list(scratch_shapes) + sem_scr, (*args, *cins), ("arbitrary",) * len(grid), aliases)
    normal = list(res[:n_out])
    stage_outs, o = [], n_out
    for st in comm:
        stage_outs.append(list(res[o:o + len(st.out_shapes)]))
        o += len(st.out_shapes)
    return (normal if multi else normal[0]), stage_outs


def _matmul(a, b, *, name, ta=False, tb=False, add=None, out_dtype=F32, tm=1024, tn=1024, tk=2048,
            b_split=False, out_split=0, comm=()):
    if ta:
        kdim, m = a.shape
    else:
        m, kdim = a.shape
    if b_split:
        assert tb
        nsp, n, kb = b.shape
        kb = kb * nsp
    elif tb:
        n, kb = b.shape
    else:
        kb, n = b.shape
    assert kb == kdim, (a.shape, b.shape, ta, tb)
    if b_split:
        tk = kdim
    if out_split:
        tn = _tile(n // out_split, tn)
    tm, tn, tk = _tile(m, tm), _tile(n, tn), _tile(kdim, tk)
    nk = kdim // tk
    dims = (((0 if ta else 1,), (1 if tb else 0,)), ((), ()))

    def product(a_ref, b_ref):
        if not b_split:
            return lax.dot_general(a_ref[...].astype(BF16), b_ref[...].astype(BF16), dims, preferred_element_type=F32)
        per = kdim // nsp
        return sum(lax.dot_general(a_ref[:, per * c:per * (c + 1)].astype(BF16), b_ref[c].astype(BF16), dims,
                                   preferred_element_type=F32) for c in range(nsp))

    def body(*refs):
        a_ref, b_ref = refs[:2]
        add_ref = refs[2] if add is not None else None
        o_ref = refs[3 if add is not None else 2]

        def finish(r):
            if add_ref is not None:
                r = r + add_ref[...].astype(F32)
            o_ref[...] = r.astype(o_ref.dtype)

        if nk == 1:
            finish(product(a_ref, b_ref))
            return
        acc_ref = refs[-1]
        k = pl.program_id(2)
        part = product(a_ref, b_ref)

        @pl.when(k == 0)
        def _():
            acc_ref[...] = part

        @pl.when(k > 0)
        def _():
            acc_ref[...] += part

        @pl.when(k == nk - 1)
        def _():
            finish(acc_ref[...])

    a_spec = pl.BlockSpec((tk, tm), lambda i, j, k: (k, i)) if ta else pl.BlockSpec((tm, tk), lambda i, j, k: (i, k))
    if b_split:
        b_spec = pl.BlockSpec((nsp, tn, kdim // nsp), lambda i, j, k: (0, j, 0))
    elif tb:
        b_spec = pl.BlockSpec((tn, tk), lambda i, j, k: (j, k))
    else:
        b_spec = pl.BlockSpec((tk, tn), lambda i, j, k: (k, j))
    in_specs = [a_spec, b_spec]
    args = [a, b]
    if add is not None:
        in_specs.append(pl.BlockSpec((tm, tn), lambda i, j, k: (i, j)))
        args.append(add)
    if out_split:
        per = (n // out_split) // tn
        out_spec = pl.BlockSpec((None, tm, tn), lambda i, j, k: (j // per, i, j % per))
        out_shape = jax.ShapeDtypeStruct((out_split, m, n // out_split), out_dtype)
    else:
        out_spec = pl.BlockSpec((tm, tn), lambda i, j, k: (i, j))
        out_shape = jax.ShapeDtypeStruct((m, n), out_dtype)
    return _pcall(body, args, name=name, grid=(m // tm, n // tn, nk), in_specs=in_specs, out_specs=out_spec,
                  out_shape=out_shape, scratch_shapes=[pltpu.VMEM((tm, tn), F32)] if nk > 1 else [],
                  sem=("parallel", "parallel", "arbitrary"), comm=comm)


def _rms_fwd(x, g, *, name, tm=512, comm=()):
    s, d = x.shape
    tm = _tile(s, tm)

    def body(x_ref, g_ref, o_ref):
        xv = x_ref[...]
        r = lax.rsqrt(jnp.mean(xv * xv, axis=-1, keepdims=True) + EPS)
        o_ref[...] = (xv * r * g_ref[...]).astype(o_ref.dtype)

    return _pcall(body, (x, g), name=name, grid=(s // tm,),
                  in_specs=[pl.BlockSpec((tm, d), lambda i: (i, 0)), pl.BlockSpec((1, d), lambda i: (0, 0))],
                  out_specs=pl.BlockSpec((tm, d), lambda i: (i, 0)),
                  out_shape=jax.ShapeDtypeStruct((s, d), BF16), sem=("parallel",), comm=comm)


def _rms_bwd(dy, x, g, res, *, name, tm=512, comm=()):
    s, d = x.shape
    tm = _tile(s, tm)

    def body(dy_ref, x_ref, g_ref, res_ref, dx_ref, dxb_ref, dg_ref):
        xv = x_ref[...]
        dyv = dy_ref[...]
        r = lax.rsqrt(jnp.mean(xv * xv, axis=-1, keepdims=True) + EPS)
        xhat = xv * r
        dyg = dyv * g_ref[...]
        mt = jnp.mean(dyg * xhat, axis=-1, keepdims=True)
        dx = res_ref[...] + r * (dyg - xhat * mt)
        dx_ref[...] = dx
        dxb_ref[...] = dx.astype(BF16)
        part = jnp.sum(dyv * xhat, axis=0, keepdims=True)

        @pl.when(pl.program_id(0) == 0)
        def _():
            dg_ref[...] = part

        @pl.when(pl.program_id(0) > 0)
        def _():
            dg_ref[...] += part

    row = pl.BlockSpec((tm, d), lambda i: (i, 0))
    vec = pl.BlockSpec((1, d), lambda i: (0, 0))
    return _pcall(body, (dy, x, g, res), name=name, grid=(s // tm,), in_specs=[row, row, vec, row],
                  out_specs=[row, row, vec],
                  out_shape=[jax.ShapeDtypeStruct((s, d), F32), jax.ShapeDtypeStruct((s, d), BF16),
                             jax.ShapeDtypeStruct((1, d), F32)],
                  sem=("arbitrary",), comm=comm)


def _lane(shape):
    return lax.broadcasted_iota(jnp.int32, shape, 1)


def _halfsum(t, lo):
    s_lo = jnp.sum(jnp.where(lo, t, 0.0), axis=-1, keepdims=True)
    s_hi = jnp.sum(jnp.where(lo, 0.0, t), axis=-1, keepdims=True)
    return jnp.where(lo, s_lo, s_hi)


def _norm_pair(x, g, lo):
    r = lax.rsqrt(_halfsum(x * x, lo) * (1.0 / 64.0) + EPS)
    xhat = x * r
    return xhat * g, xhat, r


def _norm_pair_bwd(dy, g, xhat, r, lo):
    dyg = dy * g
    mt = _halfsum(dyg * xhat, lo) * (1.0 / 64.0)
    return r * (dyg - xhat * mt), jnp.sum(dy * xhat, axis=0, keepdims=True)


def _norm_full(x, g):
    r = lax.rsqrt(jnp.mean(x * x, axis=-1, keepdims=True) + EPS)
    xhat = x * r
    return xhat * g, xhat, r


def _norm_full_bwd(dy, g, xhat, r):
    dyg = dy * g
    mt = jnp.mean(dyg * xhat, axis=-1, keepdims=True)
    return r * (dyg - xhat * mt), jnp.sum(dy * xhat, axis=0, keepdims=True)


def _rot(x, first32):
    return jnp.where(first32, pltpu.roll(x, 96, axis=1), pltpu.roll(x, 32, axis=1))


def _rope(x, cos_t, sin_t, first32):
    return x * cos_t + _rot(x, first32) * sin_t


def _rope_bwd(dy, cos_t, sin_t, first32):
    return dy * cos_t + _rot(dy * sin_t, first32)


G_SWA_Q, G_SWA_K, G_QN, G_QR, G_KN, G_KR, G_MQ = range(7)

C_QA, C_KA, C_VA, C_CQ, C_CKV, C_QM, C_KR = 0, 1024, 1152, 1280, 1792, 2304, 2816


def _prep_common(p_ref, g128_ref, gcq_ref, gckv_ref, wuq_ref, wukv_ref, cos_ref, sin_ref):
    tm = p_ref.shape[0]
    lane = _lane((tm, LANES))
    lo = lane < 64
    first32 = (lane % 64) < 32
    cos_t = cos_ref[...]
    sin_t = sin_ref[...]
    g = lambda row: g128_ref[row:row + 1, :]
    out = dict(lo=lo, first32=first32, cos_t=cos_t, sin_t=sin_t, lane=lane)
    cq_n, cq_hat, cq_r = _norm_full(p_ref[:, C_CQ:C_CQ + MLA_RANK], gcq_ref[...])
    ckv_n, ckv_hat, ckv_r = _norm_full(p_ref[:, C_CKV:C_CKV + MLA_RANK], gckv_ref[...])
    cq_b = cq_n.astype(BF16)
    ckv_b = ckv_n.astype(BF16)
    q_b = jnp.dot(cq_b, wuq_ref[...], preferred_element_type=F32)
    kv_b = jnp.dot(ckv_b, wukv_ref[...], preferred_element_type=F32)
    out.update(cq_b=cq_b, cq_hat=cq_hat, cq_r=cq_r, ckv_b=ckv_b, ckv_hat=ckv_hat, ckv_r=ckv_r, q_b=q_b, kv_b=kv_b, g=g)
    return out


def _attn_prep_fwd(proj, g128, gcq, gckv, wuq, wukv, cos_t, sin_t, *, tm=512, comm=()):
    s = proj.shape[0]
    tm = _tile(s, tm)

    def body(p_ref, g128_ref, gcq_ref, gckv_ref, wuq_ref, wukv_ref, cos_ref, sin_ref,
             qa_ref, ka_ref, va_ref, qcat_ref, kcat_ref, vb_ref, qm_ref):
        c = _prep_common(p_ref, g128_ref, gcq_ref, gckv_ref, wuq_ref, wukv_ref, cos_ref, sin_ref)
        lo, first32, g = c["lo"], c["first32"], c["g"]
        for j in range(SWA_Q_HEADS // 2):
            y, _, _ = _norm_pair(p_ref[:, C_QA + 128 * j:C_QA + 128 * (j + 1)], g(G_SWA_Q), lo)
            qa_ref[:, 128 * j:128 * (j + 1)] = y.astype(BF16)
        y, _, _ = _norm_pair(p_ref[:, C_KA:C_KA + 128], g(G_SWA_K), lo)
        ka_ref[...] = y.astype(BF16)
        va_ref[...] = p_ref[:, C_VA:C_VA + 128].astype(BF16)
        kr, _, _ = _norm_pair(p_ref[:, C_KR:C_KR + 128], g(G_KR), lo)
        kr = jnp.where(lo, _rope(kr, c["cos_t"], c["sin_t"], first32), 0.0)
        krkr = (kr + pltpu.roll(kr, 64, axis=1)).astype(BF16)
        q_b, kv_b = c["q_b"], c["kv_b"]
        qr = []
        for j in range(MLA_HEADS // 2):
            y, _, _ = _norm_pair(q_b[:, 512 + 128 * j:512 + 128 * (j + 1)], g(G_QR), lo)
            qr.append(_rope(y, c["cos_t"], c["sin_t"], first32))
        for h in range(MLA_HEADS):
            qn, _, _ = _norm_full(q_b[:, 128 * h:128 * (h + 1)], g(G_QN))
            keep = lo if h % 2 == 0 else jnp.logical_not(lo)
            qcat_ref[h, :, 0:128] = qn.astype(BF16)
            qcat_ref[h, :, 128:256] = jnp.where(keep, qr[h // 2], 0.0).astype(BF16)
            kn, _, _ = _norm_full(kv_b[:, 128 * h:128 * (h + 1)], g(G_KN))
            kcat_ref[h, :, 0:128] = kn.astype(BF16)
            kcat_ref[h, :, 128:256] = krkr
        vb_ref[...] = kv_b[:, 512:1024].astype(BF16)
        for h in range(MEM_HEADS):
            y, _, _ = _norm_full(p_ref[:, C_QM + 128 * h:C_QM + 128 * (h + 1)], g(G_MQ))
            qm_ref[:, 128 * h:128 * (h + 1)] = y.astype(BF16)

    row = lambda w: pl.BlockSpec((tm, w), lambda i: (i, 0))
    full = lambda shape: pl.BlockSpec(shape, lambda i: tuple(0 for _ in shape))
    cat = pl.BlockSpec((MLA_HEADS, tm, 256), lambda i: (0, i, 0))
    return _pcall(
        body, (proj, g128, gcq, gckv, wuq, wukv, cos_t, sin_t), name="attn_prep_fwd", grid=(s // tm,),
        in_specs=[row(IN_PAD), full((8, 128)), full((1, 512)), full((1, 512)), full((512, 768)), full((512, 1024)),
                  row(128), row(128)],
        out_specs=[row(1024), row(128), row(128), cat, cat, row(512), row(512)],
        out_shape=[jax.ShapeDtypeStruct((s, 1024), BF16), jax.ShapeDtypeStruct((s, 128), BF16),
                   jax.ShapeDtypeStruct((s, 128), BF16), jax.ShapeDtypeStruct((MLA_HEADS, s, 256), BF16),
                   jax.ShapeDtypeStruct((MLA_HEADS, s, 256), BF16), jax.ShapeDtypeStruct((s, 512), BF16),
                   jax.ShapeDtypeStruct((s, 512), BF16)],
        sem=("parallel",), comm=comm)


def _attn_prep_bwd(proj, g128, gcq, gckv, wuq, wukv, cos_t, sin_t,
                   d_qa, d_ka, d_va, d_qcat, d_kcat, d_vb, d_qm, *, tm=256, comm=()):
    s = proj.shape[0]
    tm = _tile(s, tm)

    def body(p_ref, g128_ref, gcq_ref, gckv_ref, wuq_ref, wukv_ref, cos_ref, sin_ref,
             dqa_ref, dka_ref, dva_ref, dqcat_ref, dkcat_ref, dvb_ref, dqm_ref,
             dp_ref, dwuq_ref, dwukv_ref, dg128_ref, dgcq_ref, dgckv_ref):
        c = _prep_common(p_ref, g128_ref, gcq_ref, gckv_ref, wuq_ref, wukv_ref, cos_ref, sin_ref)
        lo, first32, g = c["lo"], c["first32"], c["g"]
        cos_v, sin_v = c["cos_t"], c["sin_t"]
        q_b, kv_b = c["q_b"], c["kv_b"]
        zero_row = jnp.zeros((1, LANES), F32)
        dg = {k: zero_row for k in range(7)}

        for j in range(SWA_Q_HEADS // 2):
            sl = slice(C_QA + 128 * j, C_QA + 128 * (j + 1))
            _, xhat, r = _norm_pair(p_ref[:, sl], g(G_SWA_Q), lo)
            dx, dgj = _norm_pair_bwd(dqa_ref[:, 128 * j:128 * (j + 1)], g(G_SWA_Q), xhat, r, lo)
            dp_ref[:, sl] = dx.astype(BF16)
            dg[G_SWA_Q] = dg[G_SWA_Q] + dgj
        _, xhat, r = _norm_pair(p_ref[:, C_KA:C_KA + 128], g(G_SWA_K), lo)
        dx, dgj = _norm_pair_bwd(dka_ref[...], g(G_SWA_K), xhat, r, lo)
        dp_ref[:, C_KA:C_KA + 128] = dx.astype(BF16)
        dg[G_SWA_K] = dgj
        dp_ref[:, C_VA:C_VA + 128] = dva_ref[...].astype(BF16)

        dqb_parts = [None] * 6
        for h in range(MLA_HEADS):
            _, xhat, r = _norm_full(q_b[:, 128 * h:128 * (h + 1)], g(G_QN))
            dx, dgj = _norm_full_bwd(dqcat_ref[h, :, 0:128], g(G_QN), xhat, r)
            dqb_parts[h] = dx
            dg[G_QN] = dg[G_QN] + dgj
        for j in range(MLA_HEADS // 2):
            _, xhat, r = _norm_pair(q_b[:, 512 + 128 * j:512 + 128 * (j + 1)], g(G_QR), lo)
            d_rot = jnp.where(lo, dqcat_ref[2 * j, :, 128:256], dqcat_ref[2 * j + 1, :, 128:256])
            d_y = _rope_bwd(d_rot, cos_v, sin_v, first32)
            dx, dgj = _norm_pair_bwd(d_y, g(G_QR), xhat, r, lo)
            dqb_parts[4 + j] = dx
            dg[G_QR] = dg[G_QR] + dgj
        d_qb = jnp.concatenate(dqb_parts, axis=1).astype(BF16)
        dwuq = lax.dot_general(c["cq_b"], d_qb, (((0,), (0,)), ((), ())), preferred_element_type=F32)
        d_cqn = lax.dot_general(d_qb, wuq_ref[...], (((1,), (1,)), ((), ())), preferred_element_type=F32)
        dx, dgcq = _norm_full_bwd(d_cqn, gcq_ref[...], c["cq_hat"], c["cq_r"])
        dp_ref[:, C_CQ:C_CQ + MLA_RANK] = dx.astype(BF16)

        dkv_parts = []
        d_krkr = jnp.zeros((p_ref.shape[0], LANES), F32)
        for h in range(MLA_HEADS):
            _, xhat, r = _norm_full(kv_b[:, 128 * h:128 * (h + 1)], g(G_KN))
            dx, dgj = _norm_full_bwd(dkcat_ref[h, :, 0:128], g(G_KN), xhat, r)
            dkv_parts.append(dx)
            dg[G_KN] = dg[G_KN] + dgj
            d_krkr = d_krkr + dkcat_ref[h, :, 128:256]
        d_kvb = jnp.concatenate(dkv_parts + [dvb_ref[...]], axis=1).astype(BF16)
        dwukv = lax.dot_general(c["ckv_b"], d_kvb, (((0,), (0,)), ((), ())), preferred_element_type=F32)
        d_ckvn = lax.dot_general(d_kvb, wukv_ref[...], (((1,), (1,)), ((), ())), preferred_element_type=F32)
        dx, dgckv = _norm_full_bwd(d_ckvn, gckv_ref[...], c["ckv_hat"], c["ckv_r"])
        dp_ref[:, C_CKV:C_CKV + MLA_RANK] = dx.astype(BF16)

        _, xhat, r = _norm_pair(p_ref[:, C_KR:C_KR + 128], g(G_KR), lo)
        d_kr = jnp.where(lo, d_krkr + pltpu.roll(d_krkr, 64, axis=1), 0.0)
        d_y = jnp.where(lo, _rope_bwd(d_kr, cos_v, sin_v, first32), 0.0)
        dx, dgj = _norm_pair_bwd(d_y, g(G_KR), xhat, r, lo)
        dp_ref[:, C_KR:C_KR + 128] = jnp.where(lo, dx, 0.0).astype(BF16)
        dp_ref[:, C_KR + 128:] = jnp.zeros((p_ref.shape[0], IN_PAD - C_KR - 128), BF16)
        dg[G_KR] = dgj

        for h in range(MEM_HEADS):
            sl = slice(C_QM + 128 * h, C_QM + 128 * (h + 1))
            _, xhat, r = _norm_full(p_ref[:, sl], g(G_MQ))
            dx, dgj = _norm_full_bwd(dqm_ref[:, 128 * h:128 * (h + 1)], g(G_MQ), xhat, r)
            dp_ref[:, sl] = dx.astype(BF16)
            dg[G_MQ] = dg[G_MQ] + dgj

        dg_tile = jnp.concatenate([dg[k] for k in range(7)] + [zero_row], axis=0)

        @pl.when(pl.program_id(0) == 0)
        def _():
            dwuq_ref[...] = dwuq
            dwukv_ref[...] = dwukv
            dg128_ref[...] = dg_tile
            dgcq_ref[...] = dgcq
            dgckv_ref[...] = dgckv

        @pl.when(pl.program_id(0) > 0)
        def _():
            dwuq_ref[...] += dwuq
            dwukv_ref[...] += dwukv
            dg128_ref[...] += dg_tile
            dgcq_ref[...] += dgcq
            dgckv_ref[...] += dgckv

    row = lambda w: pl.BlockSpec((tm, w), lambda i: (i, 0))
    full = lambda shape: pl.BlockSpec(shape, lambda i: tuple(0 for _ in shape))
    cat = pl.BlockSpec((MLA_HEADS, tm, 256), lambda i: (0, i, 0))
    return _pcall(
        body, (proj, g128, gcq, gckv, wuq, wukv, cos_t, sin_t, d_qa, d_ka, d_va, d_qcat, d_kcat, d_vb, d_qm),
        name="attn_prep_bwd", grid=(s // tm,),
        in_specs=[row(IN_PAD), full((8, 128)), full((1, 512)), full((1, 512)), full((512, 768)), full((512, 1024)),
                  row(128), row(128),
                  row(1024), row(128), row(128), cat, cat, row(512), row(512)],
        out_specs=[row(IN_PAD), full((512, 768)), full((512, 1024)), full((8, 128)), full((1, 512)), full((1, 512))],
        out_shape=[jax.ShapeDtypeStruct((s, IN_PAD), BF16), jax.ShapeDtypeStruct((512, 768), F32),
                   jax.ShapeDtypeStruct((512, 1024), F32), jax.ShapeDtypeStruct((8, 128), F32),
                   jax.ShapeDtypeStruct((1, 512), F32), jax.ShapeDtypeStruct((1, 512), F32)],
        sem=("arbitrary",), comm=comm)


SWA_SLOPES = tuple(2.0 ** (-8.0 * h / SWA_Q_HEADS) for h in range(1, SWA_Q_HEADS + 1))
SWA_SCALE = SWA_HEAD_DIM ** -0.5
NT_DIMS = (((1,), (1,)), ((), ()))
TN_DIMS = (((0,), (0,)), ((), ()))


def _swa_span(n, kp_ref, kc_ref, vp_ref, vc_ref, pcol_ref, pprow_ref, pcrow_ref):
    k_span = jnp.concatenate([kp_ref[...], kc_ref[...]], axis=0).astype(F32)
    v_span = jnp.concatenate([vp_ref[...], vc_ref[...]], axis=0).astype(F32)
    lo = _lane((2 * BLOCK, LANES)) < 64
    k_sw = pltpu.roll(k_span, 64, axis=1)
    v_sw = pltpu.roll(v_span, 64, axis=1)
    kk = (jnp.where(lo, k_span, k_sw).astype(BF16), jnp.where(lo, k_sw, k_span).astype(BF16))
    vv_lo = (jnp.where(lo, v_span, 0.0).astype(BF16), jnp.where(lo, v_sw, 0.0).astype(BF16))
    vv_hi = (jnp.where(lo, 0.0, v_sw).astype(BF16), jnp.where(lo, 0.0, v_span).astype(BF16))
    pk = jnp.concatenate([pprow_ref[...], pcrow_ref[...]], axis=1)
    dist = jnp.abs(pcol_ref[...] - pk)
    qi = lax.broadcasted_iota(jnp.int32, (BLOCK, 2 * BLOCK), 0)
    ki = lax.broadcasted_iota(jnp.int32, (BLOCK, 2 * BLOCK), 1)
    first_key = jnp.where(n > 0, qi + 1, jnp.maximum(qi + 1, BLOCK))
    valid = jnp.logical_and(ki >= first_key, ki <= qi + BLOCK)
    mask_add = jnp.where(valid, 0.0, NEG_INF)
    return kk, vv_lo, vv_hi, dist, mask_add


def _swa_heads(q_ref, lo):
    heads = []
    for j in range(SWA_Q_HEADS // 2):
        q_pair = q_ref[:, 128 * j:128 * (j + 1)].astype(F32)
        for par in (0, 1):
            q_h = jnp.where(lo if par == 0 else jnp.logical_not(lo), q_pair, 0.0).astype(BF16)
            heads.append((2 * j + par, (2 * j) // (SWA_Q_HEADS // SWA_KV_HEADS), par, q_h))
    return heads


def _swa_probs(raw, dist, mask_add, slope, sink):
    s = raw * SWA_SCALE - slope * dist + mask_add
    m = jnp.maximum(jnp.max(s, axis=-1, keepdims=True), sink)
    e = jnp.exp(s - m)
    e_sink = jnp.exp(sink - m)
    inv = 1.0 / (jnp.sum(e, axis=-1, keepdims=True) + e_sink)
    return e * inv, e_sink * inv


def _swa_specs():
    blk = lambda w: pl.BlockSpec((BLOCK, w), lambda n: (n, 0))
    prev = lambda w: pl.BlockSpec((BLOCK, w), lambda n: (jnp.maximum(n - 1, 0), 0))
    prow_c = pl.BlockSpec((1, BLOCK), lambda n: (0, n))
    prow_p = pl.BlockSpec((1, BLOCK), lambda n: (0, jnp.maximum(n - 1, 0)))
    smem = pl.BlockSpec(memory_space=pltpu.SMEM)
    return [blk(1024), prev(128), blk(128), prev(128), blk(128), blk(1), prow_p, prow_c, smem], blk


def _swa_fwd(qa, ka, va, pos_col, pos_row, sinks, *, comm=()):
    s = qa.shape[0]
    in_specs, blk = _swa_specs()

    def body(q_ref, kp_ref, kc_ref, vp_ref, vc_ref, pcol_ref, pprow_ref, pcrow_ref, sink_ref, o_ref, yb_ref):
        n = pl.program_id(0)
        kk, vv_lo, vv_hi, dist, mask_add = _swa_span(n, kp_ref, kc_ref, vp_ref, vc_ref, pcol_ref, pprow_ref, pcrow_ref)
        lo = _lane((BLOCK, LANES)) < 64
        heads = _swa_heads(q_ref, lo)
        raws = [lax.dot_general(q_h, kk[kv], NT_DIMS, preferred_element_type=F32) for _, kv, _, q_h in heads]
        probs = [_swa_probs(raw, dist, mask_add, SWA_SLOPES[h], sink_ref[h])[0].astype(BF16)
                 for raw, (h, _, _, _) in zip(raws, heads)]
        for j in range(SWA_Q_HEADS // 2):
            kv = heads[2 * j][1]
            out = (jnp.dot(probs[2 * j], vv_lo[kv], preferred_element_type=F32)
                   + jnp.dot(probs[2 * j + 1], vv_hi[kv], preferred_element_type=F32))
            o_ref[:, 128 * j:128 * (j + 1)] = out
            yb_ref[:, 128 * j:128 * (j + 1)] = out.astype(BF16)

    return _pcall(body, (qa, ka, ka, va, va, pos_col, pos_row, pos_row, sinks), name="swa_fwd", grid=(s // BLOCK,),
                  in_specs=in_specs, out_specs=[blk(1024), blk(1024)],
                  out_shape=[jax.ShapeDtypeStruct((s, 1024), F32), jax.ShapeDtypeStruct((s, D_MODEL), BF16)],
                  sem=("parallel",), comm=comm)


def _swa_bwd(qa, ka, va, pos_col, pos_row, sinks, y_a, d_y, *, comm=()):
    s = qa.shape[0]
    in_specs, blk = _swa_specs()
    whole = pl.BlockSpec((s, 128), lambda n: (0, 0))

    def body(q_ref, kp_ref, kc_ref, vp_ref, vc_ref, pcol_ref, pprow_ref, pcrow_ref, sink_ref, y_ref, dy_ref,
             dq_ref, dk_ref, dv_ref, dsink_ref):
        n = pl.program_id(0)

        @pl.when(n == 0)
        def _():
            dk_ref[...] = jnp.zeros_like(dk_ref)
            dv_ref[...] = jnp.zeros_like(dv_ref)
            dsink_ref[...] = jnp.zeros_like(dsink_ref)

        kk, vv_lo, vv_hi, dist, mask_add = _swa_span(n, kp_ref, kc_ref, vp_ref, vc_ref, pcol_ref, pprow_ref, pcrow_ref)
        lo = _lane((BLOCK, LANES)) < 64
        lo2 = _lane((2 * BLOCK, LANES)) < 64
        lane1 = _lane((1, LANES))
        dsink = jnp.zeros((1, LANES), F32)
        dkk = [jnp.zeros((2 * BLOCK, LANES), F32) for _ in range(SWA_KV_HEADS)]
        dvv = [jnp.zeros((2 * BLOCK, LANES), F32) for _ in range(SWA_KV_HEADS)]
        heads = _swa_heads(q_ref, lo)
        do_b, deltas = [], []
        for j in range(SWA_Q_HEADS // 2):
            do_pair = dy_ref[:, 128 * j:128 * (j + 1)]
            doy = do_pair * y_ref[:, 128 * j:128 * (j + 1)]
            do_b.append(do_pair.astype(BF16))
            deltas.append(jnp.sum(jnp.where(lo, doy, 0.0), axis=-1, keepdims=True))
            deltas.append(jnp.sum(jnp.where(lo, 0.0, doy), axis=-1, keepdims=True))
        raws = [lax.dot_general(q_h, kk[kv], NT_DIMS, preferred_element_type=F32) for _, kv, _, q_h in heads]
        dps = [lax.dot_general(do_b[h // 2], (vv_lo, vv_hi)[par][kv], NT_DIMS, preferred_element_type=F32)
               for h, kv, par, _ in heads]
        p_b, ds_b = [], []
        for h, kv, par, _ in heads:
            p, p_sink = _swa_probs(raws[h], dist, mask_add, SWA_SLOPES[h], sink_ref[h])
            ds = p * (dps[h] - deltas[h])
            dsink = dsink + jnp.where(lane1 == h, -jnp.sum(p_sink * deltas[h], axis=0, keepdims=True), 0.0)
            p_b.append(p.astype(BF16))
            ds_b.append((ds * SWA_SCALE).astype(BF16))
        dq_halves = []
        for h, kv, par, q_h in heads:
            dq_halves.append(jnp.dot(ds_b[h], kk[kv], preferred_element_type=F32))
            dkk[kv] = dkk[kv] + lax.dot_general(ds_b[h], q_h, TN_DIMS, preferred_element_type=F32)
            pv = lax.dot_general(p_b[h], do_b[h // 2], TN_DIMS, preferred_element_type=F32)
            dvv[kv] = dvv[kv] + jnp.where(lo2 if par == 0 else jnp.logical_not(lo2), pv, 0.0)
        for j in range(SWA_Q_HEADS // 2):
            dq_ref[:, 128 * j:128 * (j + 1)] = jnp.where(lo, dq_halves[2 * j], dq_halves[2 * j + 1])
        fold = lambda t: t + pltpu.roll(t, 64, axis=1)
        dk_span = jnp.where(lo2, fold(dkk[0]), fold(dkk[1]))
        dv_span = jnp.where(lo2, fold(dvv[0]), fold(dvv[1]))
        prev0 = pl.multiple_of(jnp.maximum(n - 1, 0) * BLOCK, BLOCK)
        cur0 = pl.multiple_of(n * BLOCK, BLOCK)
        dk_ref[pl.ds(prev0, BLOCK), :] += dk_span[0:BLOCK]
        dk_ref[pl.ds(cur0, BLOCK), :] += dk_span[BLOCK:]
        dv_ref[pl.ds(prev0, BLOCK), :] += dv_span[0:BLOCK]
        dv_ref[pl.ds(cur0, BLOCK), :] += dv_span[BLOCK:]
        dsink_ref[...] += dsink

    return _pcall(
        body, (qa, ka, ka, va, va, pos_col, pos_row, pos_row, sinks, y_a, d_y), name="swa_bwd", grid=(s // BLOCK,),
        in_specs=in_specs + [blk(1024), blk(1024)],
        out_specs=[blk(1024), whole, whole, pl.BlockSpec((1, LANES), lambda n: (0, 0))],
        out_shape=[jax.ShapeDtypeStruct((s, 1024), F32), jax.ShapeDtypeStruct((s, 128), F32),
                   jax.ShapeDtypeStruct((s, 128), F32), jax.ShapeDtypeStruct((1, LANES), F32)],
        sem=("arbitrary",), comm=comm)


MLA_SCALE = (MLA_NOPE + MLA_ROPE) ** -0.5
MLA_TILE = 512


def _tile_pairs(nt, q_major):
    pairs = [(i, j) for i in range(nt) for j in range(i + 1)] if q_major else \
            [(i, j) for j in range(nt) for i in range(j, nt)]
    return jnp.asarray([p[0] for p in pairs], jnp.int32), jnp.asarray([p[1] for p in pairs], jnp.int32)


def _diag_mask(t):
    return lax.broadcasted_iota(jnp.int32, (t, t), 1) <= lax.broadcasted_iota(jnp.int32, (t, t), 0)


def _mla_fwd(q_cat, k_cat, v_b, y_all, *, comm=()):
    nh, s, _ = q_cat.shape
    t = _tile(s, MLA_TILE)
    qi, kj = _tile_pairs(s // t, True)
    ycol = (SWA_Q_HEADS * SWA_HEAD_DIM) // (nh * MLA_V)

    def body(qi_ref, kj_ref, q_ref, k_ref, v_ref, _, o_ref, lse_ref, yb_ref, m_sc, l_sc, acc_sc):
        i, j = qi_ref[pl.program_id(0)], kj_ref[pl.program_id(0)]

        @pl.when(j == 0)
        def _():
            m_sc[...] = jnp.full_like(m_sc, NEG_INF)
            l_sc[...] = jnp.zeros_like(l_sc)
            acc_sc[...] = jnp.zeros_like(acc_sc)

        def update(diagonal):
            scores = [lax.dot_general(q_ref[h], k_ref[h], NT_DIMS, preferred_element_type=F32) for h in range(nh)]
            probs, alphas = [], []
            for h in range(nh):
                sc = scores[h] * MLA_SCALE
                if diagonal:
                    sc = jnp.where(_diag_mask(t), sc, NEG_INF)
                m_old = m_sc[h]
                m_new = jnp.maximum(m_old, jnp.max(sc, axis=-1, keepdims=True))
                alpha = jnp.exp(m_old - m_new)
                p = jnp.exp(sc - m_new)
                l_sc[h] = alpha * l_sc[h] + jnp.sum(p, axis=-1, keepdims=True)
                m_sc[h] = m_new
                probs.append(p.astype(BF16))
                alphas.append(alpha)
            for h in range(nh):
                acc_sc[h] = alphas[h] * acc_sc[h] + jnp.dot(probs[h], v_ref[:, MLA_V * h:MLA_V * (h + 1)],
                                                            preferred_element_type=F32)

        @pl.when(j < i)
        def _():
            update(False)

        @pl.when(j == i)
        def _():
            update(True)
            for h in range(nh):
                out = acc_sc[h] * (1.0 / l_sc[h])
                o_ref[:, MLA_V * h:MLA_V * (h + 1)] = out
                yb_ref[:, MLA_V * h:MLA_V * (h + 1)] = out.astype(BF16)
                lse_ref[h] = m_sc[h] + jnp.log(l_sc[h])

    return _pcall(
        body, (q_cat, k_cat, v_b, y_all), name="mla_fwd", grid=(qi.shape[0],), prefetch=(qi, kj),
        in_specs=[pl.BlockSpec((nh, t, 256), lambda p, qi, kj: (0, qi[p], 0)),
                  pl.BlockSpec((nh, t, 256), lambda p, qi, kj: (0, kj[p], 0)),
                  pl.BlockSpec((t, nh * MLA_V), lambda p, qi, kj: (kj[p], 0)), ANY],
        out_specs=[pl.BlockSpec((t, nh * MLA_V), lambda p, qi, kj: (qi[p], 0)),
                   pl.BlockSpec((nh, t, 1), lambda p, qi, kj: (0, qi[p], 0)),
                   pl.BlockSpec((t, nh * MLA_V), lambda p, qi, kj: (qi[p], ycol))],
        out_shape=[jax.ShapeDtypeStruct((s, nh * MLA_V), F32), jax.ShapeDtypeStruct((nh, s, 1), F32),
                   jax.ShapeDtypeStruct(y_all.shape, y_all.dtype)],
        scratch_shapes=[pltpu.VMEM((nh, t, 1), F32), pltpu.VMEM((nh, t, 1), F32), pltpu.VMEM((nh, t, MLA_V), F32)],
        sem=("arbitrary",), comm=comm, io_alias={3: 2})


def _mla_bwd(q_cat, k_cat, v_b, y_b, lse, d_y, *, comm=()):
    nh, s, _ = q_cat.shape
    t = _tile(s, MLA_TILE)
    nt = s // t
    hp = 2
    wv = hp * MLA_V
    col0 = (SWA_Q_HEADS * SWA_HEAD_DIM) // wv
    qi, kj = _tile_pairs(nt, False)

    def body(qi_ref, kj_ref, q_ref, k_ref, v_ref, y_ref, lse_ref, dy_ref, dq_ref, dk_ref, dv_ref, dk_sc, dv_sc):
        step = pl.program_id(1)
        i, j = qi_ref[step], kj_ref[step]

        @pl.when(step == 0)
        def _():
            dq_ref[...] = jnp.zeros_like(dq_ref)

        @pl.when(i == j)
        def _():
            dk_sc[...] = jnp.zeros_like(dk_sc)
            dv_sc[...] = jnp.zeros_like(dv_sc)

        def update(diagonal):
            rows = pl.ds(pl.multiple_of(i * t, t), t)
            cols = [slice(MLA_V * h, MLA_V * (h + 1)) for h in range(hp)]
            do_b = [dy_ref[:, cols[h]].astype(BF16) for h in range(hp)]
            scores = [lax.dot_general(q_ref[h], k_ref[h], NT_DIMS, preferred_element_type=F32) for h in range(hp)]
            dps = [lax.dot_general(do_b[h], v_ref[:, cols[h]], NT_DIMS, preferred_element_type=F32) for h in range(hp)]
            p_b, ds_b = [], []
            for h in range(hp):
                p = jnp.exp(scores[h] * MLA_SCALE - lse_ref[h])
                if diagonal:
                    p = jnp.where(_diag_mask(t), p, 0.0)
                delta = jnp.sum(dy_ref[:, cols[h]] * y_ref[:, cols[h]], axis=-1, keepdims=True)
                p_b.append(p.astype(BF16))
                ds_b.append((p * (dps[h] - delta) * MLA_SCALE).astype(BF16))
            for h in range(hp):
                dv_sc[h] += lax.dot_general(p_b[h], do_b[h], TN_DIMS, preferred_element_type=F32)
                dk_sc[h] += lax.dot_general(ds_b[h], q_ref[h], TN_DIMS, preferred_element_type=F32)
                dq_ref[h, rows, :] += jnp.dot(ds_b[h], k_ref[h], preferred_element_type=F32)

        @pl.when(i > j)
        def _():
            update(False)

        @pl.when(i == j)
        def _():
            update(True)

        @pl.when(i == nt - 1)
        def _():
            dk_ref[...] = dk_sc[...]
            for h in range(hp):
                dv_ref[:, MLA_V * h:MLA_V * (h + 1)] = dv_sc[h]

    return _pcall(
        body, (q_cat, k_cat, v_b, y_b, lse, d_y), name="mla_bwd", grid=(nh // hp, qi.shape[0]), prefetch=(qi, kj),
        in_specs=[pl.BlockSpec((hp, t, 256), lambda g, p, qi, kj: (g, qi[p], 0)),
                  pl.BlockSpec((hp, t, 256), lambda g, p, qi, kj: (g, kj[p], 0)),
                  pl.BlockSpec((t, wv), lambda g, p, qi, kj: (kj[p], g)),
                  pl.BlockSpec((t, wv), lambda g, p, qi, kj: (qi[p], g)),
                  pl.BlockSpec((hp, t, 1), lambda g, p, qi, kj: (g, qi[p], 0)),
                  pl.BlockSpec((t, wv), lambda g, p, qi, kj: (qi[p], col0 + g))],
        out_specs=[pl.BlockSpec((hp, s, 256), lambda g, p, qi, kj: (g, 0, 0)),
                   pl.BlockSpec((hp, t, 256), lambda g, p, qi, kj: (g, kj[p], 0)),
                   pl.BlockSpec((t, wv), lambda g, p, qi, kj: (kj[p], g))],
        out_shape=[jax.ShapeDtypeStruct((nh, s, 256), F32), jax.ShapeDtypeStruct((nh, s, 256), F32),
                   jax.ShapeDtypeStruct((s, nh * MLA_V), F32)],
        scratch_shapes=[pltpu.VMEM((hp, t, 256), F32), pltpu.VMEM((hp, t, MLA_V), F32)],
        sem=("arbitrary", "arbitrary"), comm=comm)


MEM_SCALE = MEM_DIM ** -0.5


def _mem_kv_fwd(mem, g_mem, w_memkv, g_mk):
    m_len = mem.shape[0]

    def body(mem_ref, g_ref, w_ref, gk_ref, mn_ref, kv_ref, kn_ref, v_ref):
        mn, _, _ = _norm_full(mem_ref[...], g_ref[...])
        mn_b = mn.astype(BF16)
        mn_ref[...] = mn_b
        kv = jnp.dot(mn_b, w_ref[...], preferred_element_type=F32)
        kv_ref[...] = kv
        for h in range(MEM_HEADS):
            kn, _, _ = _norm_full(kv[:, 128 * h:128 * (h + 1)], gk_ref[...])
            kn_ref[:, 128 * h:128 * (h + 1)] = kn.astype(BF16)
        v_ref[...] = kv[:, 512:1024].astype(BF16)

    return pl.pallas_call(
        body, name="mem_kv_fwd",
        out_shape=[jax.ShapeDtypeStruct((m_len, D_MODEL), BF16), jax.ShapeDtypeStruct((m_len, 1024), F32),
                   jax.ShapeDtypeStruct((m_len, 512), BF16), jax.ShapeDtypeStruct((m_len, 512), BF16)],
        compiler_params=_params(),
    )(mem, g_mem, w_memkv, g_mk)


def _mem_kv_bwd(mem, g_mem, w_memkv, g_mk, mn_b, kv, d_kn, d_v):
    m_len = mem.shape[0]

    def body(mem_ref, g_ref, w_ref, gk_ref, mn_ref, kv_ref, dkn_ref, dv_ref, dw_ref, dgmem_ref, dgk_ref):
        parts = []
        dgk = jnp.zeros((1, LANES), F32)
        for h in range(MEM_HEADS):
            _, xhat, r = _norm_full(kv_ref[:, 128 * h:128 * (h + 1)], gk_ref[...])
            dx, dgh = _norm_full_bwd(dkn_ref[:, 128 * h:128 * (h + 1)], gk_ref[...], xhat, r)
            parts.append(dx)
            dgk = dgk + dgh
        d_kv = jnp.concatenate(parts + [dv_ref[...]], axis=1).astype(BF16)
        dw_ref[...] = lax.dot_general(mn_ref[...], d_kv, TN_DIMS, preferred_element_type=F32)
        d_mn = lax.dot_general(d_kv, w_ref[...], NT_DIMS, preferred_element_type=F32)
        _, xhat, _ = _norm_full(mem_ref[...], g_ref[...])
        dgmem_ref[...] = jnp.sum(d_mn * xhat, axis=0, keepdims=True)
        dgk_ref[...] = dgk

    return pl.pallas_call(
        body, name="mem_kv_bwd",
        out_shape=[jax.ShapeDtypeStruct((D_MODEL, 1024), F32), jax.ShapeDtypeStruct((1, D_MODEL), F32),
                   jax.ShapeDtypeStruct((1, LANES), F32)],
        compiler_params=_params(),
    )(mem, g_mem, w_memkv, g_mk, mn_b, kv, d_kn, d_v)


def _mem_probs(q_h, k_h):
    sc = lax.dot_general(q_h, k_h, NT_DIMS, preferred_element_type=F32) * MEM_SCALE
    e = jnp.exp(sc - jnp.max(sc, axis=-1, keepdims=True))
    return e * (1.0 / jnp.sum(e, axis=-1, keepdims=True))


def _mem_attn_fwd(qm, km, vm, y_all, *, tm=512):
    s = qm.shape[0]
    tm = _tile(s, tm)
    m_len = km.shape[0]
    ycol = (SWA_Q_HEADS * SWA_HEAD_DIM + MLA_HEADS * MLA_V) // 512

    def body(q_ref, k_ref, v_ref, _, o_ref, yb_ref):
        for h in range(MEM_HEADS):
            sl = slice(128 * h, 128 * (h + 1))
            p = _mem_probs(q_ref[:, sl], k_ref[:, sl])
            out = jnp.dot(p.astype(BF16), v_ref[:, sl], preferred_element_type=F32)
            o_ref[:, sl] = out
            yb_ref[:, sl] = out.astype(BF16)

    kvspec = pl.BlockSpec((m_len, 512), lambda i: (0, 0))
    return _pcall(
        body, (qm, km, vm, y_all), name="mem_attn_fwd", grid=(s // tm,),
        in_specs=[pl.BlockSpec((tm, 512), lambda i: (i, 0)), kvspec, kvspec, ANY],
        out_specs=[pl.BlockSpec((tm, 512), lambda i: (i, 0)), pl.BlockSpec((tm, 512), lambda i: (i, ycol))],
        out_shape=[jax.ShapeDtypeStruct((s, 512), F32), jax.ShapeDtypeStruct(y_all.shape, y_all.dtype)],
        sem=("parallel",), io_alias={3: 1})


def _mem_attn_bwd(qm, km, vm, y_m, d_y, *, tm=512):
    s = qm.shape[0]
    tm = _tile(s, tm)
    m_len = km.shape[0]
    col0 = (SWA_Q_HEADS * SWA_HEAD_DIM + MLA_HEADS * MLA_V) // 512

    def body(q_ref, k_ref, v_ref, y_ref, dy_ref, dq_ref, dk_ref, dv_ref):
        @pl.when(pl.program_id(0) == 0)
        def _():
            dk_ref[...] = jnp.zeros_like(dk_ref)
            dv_ref[...] = jnp.zeros_like(dv_ref)

        for h in range(MEM_HEADS):
            sl = slice(128 * h, 128 * (h + 1))
            q_h, k_h = q_ref[:, sl], k_ref[:, sl]
            do = dy_ref[:, sl]
            do_b = do.astype(BF16)
            p = _mem_probs(q_h, k_h)
            delta = jnp.sum(do * y_ref[:, sl], axis=-1, keepdims=True)
            dv_ref[:, sl] += lax.dot_general(p.astype(BF16), do_b, TN_DIMS, preferred_element_type=F32)
            dp = lax.dot_general(do_b, v_ref[:, sl], NT_DIMS, preferred_element_type=F32)
            ds_b = (p * (dp - delta) * MEM_SCALE).astype(BF16)
            dq_ref[:, sl] = jnp.dot(ds_b, k_h, preferred_element_type=F32)
            dk_ref[:, sl] += lax.dot_general(ds_b, q_h, TN_DIMS, preferred_element_type=F32)

    kvspec = pl.BlockSpec((m_len, 512), lambda i: (0, 0))
    row = pl.BlockSpec((tm, 512), lambda i: (i, 0))
    return pl.pallas_call(
        body, name="mem_attn_bwd", grid=(s // tm,),
        in_specs=[row, kvspec, kvspec, row, pl.BlockSpec((tm, 512), lambda i: (i, col0))],
        out_specs=[row, kvspec, kvspec],
        out_shape=[jax.ShapeDtypeStruct((s, 512), F32), jax.ShapeDtypeStruct((m_len, 512), F32),
                   jax.ShapeDtypeStruct((m_len, 512), F32)],
        compiler_params=_params(("arbitrary",)),
    )(qm, km, vm, y_m, d_y)


def _ffn_gate_up(fn, w_gate, w_up, *, tm=512, comm=()):
    s, d = fn.shape
    nsp, _, tf = w_gate.shape
    f = nsp * tf
    tm = _tile(s, tm)

    def body(x_ref, wg_ref, wu_ref, g_ref, u_ref, a_ref):
        x = x_ref[...]
        gate = jnp.dot(x, wg_ref[...], preferred_element_type=F32)
        up = jnp.dot(x, wu_ref[...], preferred_element_type=F32)
        g_ref[...] = gate.astype(BF16)
        u_ref[...] = up.astype(BF16)
        a_ref[...] = (gate * (1.0 / (1.0 + jnp.exp(-gate))) * up).astype(BF16)

    wspec = pl.BlockSpec((None, d, tf), lambda j, i: (j, 0, 0))
    ospec = pl.BlockSpec((tm, tf), lambda j, i: (i, j))
    osh = jax.ShapeDtypeStruct((s, f), BF16)
    return _pcall(body, (fn, w_gate, w_up), name="ffn_gate_up", grid=(nsp, s // tm),
                  in_specs=[pl.BlockSpec((tm, d), lambda j, i: (i, 0)), wspec, wspec],
                  out_specs=[ospec, ospec, ospec], out_shape=[osh, osh, osh], sem=("parallel", "parallel"), comm=comm)


def _ffn_bwd_act(d_out, w_down, gate, up, *, tm=512, tf=1408, comm=()):
    s, d = d_out.shape
    f = w_down.shape[0]
    tm, tf = _tile(s, tm), _tile(f, tf)

    chunks = [(c0, min(256, tf - c0)) for c0 in range(0, tf, 256)]

    def body(do_ref, wd_ref, g_ref, u_ref, dg_ref, du_ref):
        do = do_ref[...].astype(BF16)
        parts = [lax.dot_general(do, wd_ref[c0:c0 + cw, :], NT_DIMS, preferred_element_type=F32) for c0, cw in chunks]
        for (c0, cw), d_act in zip(chunks, parts):
            cols = slice(c0, c0 + cw)
            gate = g_ref[:, cols].astype(F32)
            sig = 1.0 / (1.0 + jnp.exp(-gate))
            du_ref[:, cols] = (d_act * (gate * sig)).astype(BF16)
            dg_ref[:, cols] = (d_act * u_ref[:, cols].astype(F32) * (sig * (1.0 + gate * (1.0 - sig)))).astype(BF16)

    ospec = pl.BlockSpec((tm, tf), lambda j, i: (i, j))
    osh = jax.ShapeDtypeStruct((s, f), BF16)
    return _pcall(
        body, (d_out, w_down, gate, up), name="ffn_bwd_act", grid=(f // tf, s // tm),
        in_specs=[pl.BlockSpec((tm, d), lambda j, i: (i, 0)), pl.BlockSpec((tf, d), lambda j, i: (j, 0)), ospec, ospec],
        out_specs=[ospec, ospec], out_shape=[osh, osh], sem=("parallel", "parallel"), comm=comm)


def _loss_head(out, target, *, tm=512):
    s, d = out.shape
    tm = _tile(s, tm)

    def body(o_ref, t_ref, d_ref, db_ref, l_ref):
        err = o_ref[...] - t_ref[...]
        d_out = err * (1.0 / d)
        d_ref[...] = d_out
        db_ref[...] = d_out.astype(BF16)
        part = 0.5 * jnp.sum(jnp.mean(err * err, axis=-1, keepdims=True), axis=0, keepdims=True)
        part = jnp.broadcast_to(part, (1, LANES))

        @pl.when(pl.program_id(0) == 0)
        def _():
            l_ref[...] = part

        @pl.when(pl.program_id(0) > 0)
        def _():
            l_ref[...] += part

    row = pl.BlockSpec((tm, d), lambda i: (i, 0))
    return pl.pallas_call(
        body, name="loss_head", grid=(s // tm,),
        in_specs=[row, row], out_specs=[row, row, pl.BlockSpec((1, LANES), lambda i: (0, 0))],
        out_shape=[jax.ShapeDtypeStruct((s, d), F32), jax.ShapeDtypeStruct((s, d), BF16),
                   jax.ShapeDtypeStruct((1, LANES), F32)],
        compiler_params=_params(("arbitrary",)),
    )(out, target)


def _cols(g4):
    return jnp.concatenate([g4[k] for k in range(N_CHIPS)], axis=1)


def _full_w_in(g4):
    per = IN_WIDTH // N_CHIPS
    kr0 = 2304 - (N_CHIPS - 1) * per
    last = g4[N_CHIPS - 1]
    pad = jnp.zeros((last.shape[0], IN_PAD - IN_WIDTH), last.dtype)
    return jnp.concatenate([g4[0], g4[1], g4[2], last[:, :kr0], last[:, kr0 + 64:], last[:, kr0:kr0 + 64], pad], axis=1)


def _shards_w_in(dwp):
    per = IN_WIDTH // N_CHIPS
    kr0 = 2304 - (N_CHIPS - 1) * per
    last = jnp.concatenate([dwp[:, (N_CHIPS - 1) * per:2304], dwp[:, C_KR:C_KR + 64], dwp[:, 2304:C_KR]], axis=1)
    assert last.shape[1] == per and kr0 == 144
    return jnp.stack([dwp[:, per * k:per * (k + 1)] for k in range(N_CHIPS - 1)] + [last])


def _full_heads(g4, first):
    return jnp.concatenate([g4[k][:, :first] for k in range(N_CHIPS)] + [g4[k][:, first:] for k in range(N_CHIPS)], axis=1)


def _shards_heads(dwp, first, rest):
    base = N_CHIPS * first
    return jnp.stack([jnp.concatenate([dwp[:, first * k:first * (k + 1)], dwp[:, base + rest * k:base + rest * (k + 1)]], axis=1)
                      for k in range(N_CHIPS)])


def _rope_tables(pos):
    inv_freq = ROPE_THETA ** (-jnp.arange(0, MLA_ROPE, 2, dtype=F32) / MLA_ROPE)
    ang = pos.astype(F32)[:, None] * inv_freq
    cos, sin = jnp.cos(ang), jnp.sin(ang)
    return jnp.tile(cos, (1, 4)), jnp.concatenate([-sin, sin, -sin, sin], axis=1)


def _gain_table(sp):
    two = lambda v: jnp.tile(v, (1, 2))
    rows = [two(sp["swa_q_norm_g"]), two(sp["swa_k_norm_g"]), sp["mla_qn_norm_g"], two(sp["mla_qr_norm_g"]),
            sp["mla_kn_norm_g"], two(sp["mla_kr_norm_g"]), sp["mem_q_norm_g"], jnp.zeros((1, LANES), F32)]
    return jnp.concatenate(rows, axis=0)


CHIP_DISTANCES = (1, 2, 3)


def _place():
    x, y, c = lax.axis_index("x"), lax.axis_index("y"), lax.axis_index("c")
    return x, y, c, 2 * x + y


def _chip_at(x, y, d):
    px = 1 - x if d & 2 else x
    py = 1 - y if d & 1 else y
    return px, py, 2 * px + py


def _row_tile(rows, want=512, mult=8):
    t = min(rows, want)
    t -= t % mult
    while rows % t:
        t -= mult
    return t


def _cast_into_slot(w, meta, *, name):
    rows, cols = w.shape
    tr = _row_tile(rows, 512, 16)

    def body(meta_ref, w_ref, o_ref):
        o_ref[...] = w_ref[...].astype(BF16)

    grid_spec = pltpu.PrefetchScalarGridSpec(
        num_scalar_prefetch=1, grid=(rows // tr,),
        in_specs=[pl.BlockSpec((tr, cols), lambda i, m: (i, 0))],
        out_specs=pl.BlockSpec((None, tr, cols), lambda i, m: (m[0], i, 0)))
    return pl.pallas_call(
        body, name=name, grid_spec=grid_spec,
        out_shape=jax.ShapeDtypeStruct((N_CHIPS, rows, cols), BF16),
        compiler_params=_params(("parallel",)),
    )(meta, w)


def _remote(src, dst, ssem, rsem, i, device):
    return pltpu.make_async_remote_copy(src_ref=src, dst_ref=dst, send_sem=ssem.at[i], recv_sem=rsem.at[i],
                                        device_id=device, device_id_type=MESH)


def _symmetric_stage(ins, out_shapes, aliases, n_sem, copies):
    def issue(i_refs, o_refs, ssem, rsem):
        for send, _ in copies(i_refs, o_refs, ssem, rsem):
            send.start()

    def wait(i_refs, o_refs, ssem, rsem):
        pairs = copies(i_refs, o_refs, ssem, rsem)
        for _, arrival in pairs:
            arrival.wait_recv()
        for send, _ in pairs:
            send.wait_send()

    return _Stage(ins, out_shapes, aliases, n_sem, issue, wait)


def _gather_stage(slots, leg):
    n = len(slots)

    def leg_copies(which, base):
        def copies(_, outs, ssem, rsem):
            x, y, c, k_me = _place()
            pairs = []
            for w in range(n):
                half = outs[w].shape[1] // 2
                slab = lambda k, cc, w=w, half=half: outs[w].at[k, pl.ds(cc * half, half)]
                for d in CHIP_DISTANCES:
                    px, py, k_src = _chip_at(x, y, d)
                    i = base + 3 * w + d - 1
                    if which == "ici":
                        pairs.append((_remote(slab(k_me, c), slab(k_me, c), ssem, rsem, i, (px, py, c)),
                                      _remote(slab(k_src, c), slab(k_src, c), ssem, rsem, i, (x, y, c))))
                    else:
                        pairs.append((_remote(slab(k_src, c), slab(k_src, c), ssem, rsem, i, (x, y, 1 - c)),
                                      _remote(slab(k_src, 1 - c), slab(k_src, 1 - c), ssem, rsem, i, (x, y, c))))
            return pairs
        return copies

    shapes = [jax.ShapeDtypeStruct(s.shape, s.dtype) for s in slots]
    in_place = {w: w for w in range(n)}
    if leg != "both":
        return _symmetric_stage(slots, shapes, in_place, 3 * n, leg_copies(leg, 0))
    ici = _symmetric_stage(slots, shapes, in_place, 6 * n, leg_copies("ici", 0))
    d2d = _symmetric_stage(slots, shapes, in_place, 6 * n, leg_copies("d2d", 3 * n))

    def mid(*refs):
        ici.wait(*refs)
        d2d.issue(*refs)

    return _Stage(slots, shapes, in_place, 6 * n, ici.issue, d2d.wait, mid)


def _halves_stage(grads):
    n = len(grads)

    def copies(ins, outs, ssem, rsem):
        x, y, c, _ = _place()
        pairs = []
        for w in range(n):
            half = ins[w].shape[1] // 2
            pairs.append((_remote(ins[w].at[:, pl.ds((1 - c) * half, half)], outs[w], ssem, rsem, w, (x, y, 1 - c)),
                          _remote(outs[w], outs[w], ssem, rsem, w, (x, y, c))))
        return pairs

    shapes = [jax.ShapeDtypeStruct((N_CHIPS, g.shape[1] // 2, g.shape[2]), g.dtype) for g in grads]
    return _symmetric_stage(grads, shapes, {}, n, copies)


def _chips_stage(parts):
    n = len(parts)

    def copies(ins, outs, ssem, rsem):
        x, y, c, _ = _place()
        pairs = []
        for w in range(n):
            for d in CHIP_DISTANCES:
                px, py, _ = _chip_at(x, y, d)
                i = 3 * w + d - 1
                pairs.append((_remote(ins[w].at[d - 1], outs[w].at[d - 1], ssem, rsem, i, (px, py, c)),
                              _remote(outs[w].at[d - 1], outs[w].at[d - 1], ssem, rsem, i, (x, y, c))))
        return pairs

    shapes = [jax.ShapeDtypeStruct(p.shape, p.dtype) for p in parts]
    return _symmetric_stage(parts, shapes, {}, 3 * n, copies)


def _swap_stage(totals):
    n = len(totals)

    def copies(ins, outs, ssem, rsem):
        x, y, c, _ = _place()
        return [(_remote(ins[w], outs[w], ssem, rsem, w, (x, y, 1 - c)),
                 _remote(outs[w], outs[w], ssem, rsem, w, (x, y, c))) for w in range(n)]

    shapes = [jax.ShapeDtypeStruct(t.shape, t.dtype) for t in totals]
    return _symmetric_stage(totals, shapes, {}, n, copies)


def _run_stages(stages, *, name):
    n_ins = [len(st.ins) for st in stages]
    n_outs = [len(st.out_shapes) for st in stages]
    tot_in, tot_out = sum(n_ins), sum(n_outs)
    aliases, i0, o0 = {}, 0, 0
    for st, ni, no in zip(stages, n_ins, n_outs):
        aliases.update({i0 + a: o0 + b for a, b in st.aliases.items()})
        i0, o0 = i0 + ni, o0 + no

    def body(*refs):
        sems = refs[tot_in + tot_out:]
        for what in ("issue", "wait"):
            i0, o0 = 0, tot_in
            for k, (st, ni, no) in enumerate(zip(stages, n_ins, n_outs)):
                getattr(st, what)(refs[i0:i0 + ni], refs[o0:o0 + no], sems[2 * k], sems[2 * k + 1])
                i0, o0 = i0 + ni, o0 + no

    sem = pltpu.SemaphoreType.DMA
    res = pl.pallas_call(
        body, name=name, in_specs=[ANY] * tot_in, out_specs=[ANY] * tot_out,
        out_shape=[s for st in stages for s in st.out_shapes], input_output_aliases=aliases,
        scratch_shapes=[sem((st.n_sem,)) for st in stages for _ in range(2)],
    )(*[a for st in stages for a in st.ins])
    outs, o0 = [], 0
    for no in n_outs:
        outs.append(list(res[o0:o0 + no]))
        o0 += no
    return outs


def _add_pair(meta, g4, recv, *, name):
    nsh, rows, cols = g4.shape
    half = rows // 2
    tr = _row_tile(half, 128 if cols > 1024 else 256, 16)
    nt = half // tr

    def body(meta_ref, g0, g1, g2, g3, r0, r1, r2, r3, own_ref, oth_ref):
        own_ref[...] = g0[...] + r0[...]
        for d, (g, r) in enumerate(((g1, r1), (g2, r2), (g3, r3))):
            oth_ref[d] = (g[...] + r[...]).astype(BF16)

    blk = (None, tr, cols)
    gspec = lambda d: pl.BlockSpec(blk, lambda i, m: (jnp.bitwise_xor(m[0], d), m[1] * nt + i, 0))
    rspec = lambda d: pl.BlockSpec(blk, lambda i, m: (jnp.bitwise_xor(m[0], d), i, 0))
    grid_spec = pltpu.PrefetchScalarGridSpec(
        num_scalar_prefetch=1, grid=(nt,),
        in_specs=[gspec(d) for d in range(nsh)] + [rspec(d) for d in range(nsh)],
        out_specs=[pl.BlockSpec((tr, cols), lambda i, m: (i, 0)), pl.BlockSpec((3, tr, cols), lambda i, m: (0, i, 0))])
    return pl.pallas_call(
        body, name=name, grid_spec=grid_spec,
        out_shape=[jax.ShapeDtypeStruct((half, cols), F32), jax.ShapeDtypeStruct((3, half, cols), BF16)],
        compiler_params=_params(("parallel",)),
    )(meta, g4, g4, g4, g4, recv, recv, recv, recv)


def _add_chips(own, recv, *, name):
    half, cols = own.shape
    tr = _row_tile(half, 256, 16)

    def body(p_ref, r_ref, o_ref):
        o_ref[...] = ((p_ref[...] + r_ref[0].astype(F32)) + r_ref[1].astype(F32)) + r_ref[2].astype(F32)

    return pl.pallas_call(
        body, name=name, grid=(half // tr,),
        in_specs=[pl.BlockSpec((tr, cols), lambda i: (i, 0)), pl.BlockSpec((3, tr, cols), lambda i: (0, i, 0))],
        out_specs=pl.BlockSpec((tr, cols), lambda i: (i, 0)),
        out_shape=jax.ShapeDtypeStruct((half, cols), F32),
        compiler_params=_params(("parallel",)),
    )(own, recv)


def _adamw_math(w, g, m, v):
    m = ADAM_B1 * m + (1.0 - ADAM_B1) * g
    v = ADAM_B2 * v + (1.0 - ADAM_B2) * (g * g)
    m_hat = m / (1.0 - ADAM_B1 ** ADAM_STEP)
    v_hat = v / (1.0 - ADAM_B2 ** ADAM_STEP)
    delta = -ADAM_LR * (m_hat / (jnp.sqrt(v_hat) + ADAM_EPS) + ADAM_WD * w)
    return delta, m, v


def _adamw(meta, w, g_mine, g_theirs, m, v, *, name):
    rows, cols = w.shape
    half = rows // 2
    tr = _row_tile(half, 256)
    nt = half // tr

    def body(meta_ref, w_ref, a_ref, b_ref, m_ref, v_ref, g_ref, d_ref, mo_ref, vo_ref):
        is_mine = (pl.program_id(0) // nt) == meta_ref[1]
        g = jnp.where(is_mine, a_ref[...], b_ref[...])
        g_ref[...] = g
        d_ref[...], mo_ref[...], vo_ref[...] = _adamw_math(w_ref[...], g, m_ref[...], v_ref[...])

    blk = pl.BlockSpec((tr, cols), lambda i, mt: (i, 0))
    mine = pl.BlockSpec((tr, cols), lambda i, mt: (jnp.where(i // nt == mt[1], i % nt, 0), 0))
    theirs = pl.BlockSpec((tr, cols), lambda i, mt: (jnp.where(i // nt == mt[1], 0, i % nt), 0))
    sh = jax.ShapeDtypeStruct((rows, cols), F32)
    grid_spec = pltpu.PrefetchScalarGridSpec(
        num_scalar_prefetch=1, grid=(rows // tr,),
        in_specs=[blk, mine, theirs, blk, blk], out_specs=[blk] * 4)
    return pl.pallas_call(
        body, name=name, grid_spec=grid_spec, out_shape=[sh] * 4,
        compiler_params=_params(("arbitrary",)),
    )(meta, w, g_mine, g_theirs, m, v)


N_DEVICES = 8


def _small_step(g_pack, w_pack, m_pack, v_pack):
    rows = g_pack.shape[0]

    def body(g_ref, w_ref, m_ref, v_ref, sum_ref, d_ref, mo_ref, vo_ref, slots, ssem, rsem):
        x, y, c, _ = _place()
        me = 4 * x + 2 * y + c
        slots[me] = g_ref[...]
        copies = []
        for r in range(1, N_DEVICES):
            px = 1 - x if r & 4 else x
            py = 1 - y if r & 2 else y
            pc = 1 - c if r & 1 else c
            copies.append(pltpu.make_async_remote_copy(
                src_ref=g_ref, dst_ref=slots.at[me], send_sem=ssem.at[r - 1], recv_sem=rsem.at[r - 1],
                device_id=(px, py, pc), device_id_type=MESH))
        for cp in copies:
            cp.start()
        for r in range(1, N_DEVICES):
            src = jnp.bitwise_xor(me, r)
            pltpu.make_async_remote_copy(
                src_ref=g_ref, dst_ref=slots.at[src], send_sem=ssem.at[r - 1], recv_sem=rsem.at[r - 1],
                device_id=(x, y, c), device_id_type=MESH).wait_recv()
        for cp in copies:
            cp.wait_send()
        total = slots[0]
        for k in range(1, N_DEVICES):
            total = total + slots[k]
        sum_ref[...] = total
        d_ref[...], mo_ref[...], vo_ref[...] = _adamw_math(w_ref[...], total, m_ref[...], v_ref[...])

    sh = jax.ShapeDtypeStruct((rows, LANES), F32)
    vm = pl.BlockSpec(memory_space=pltpu.VMEM)
    return pl.pallas_call(
        body, name="small_allreduce_adamw",
        in_specs=[vm] * 4, out_specs=[vm] * 4, out_shape=[sh] * 4,
        scratch_shapes=[pltpu.VMEM((N_DEVICES, rows, LANES), F32),
                        pltpu.SemaphoreType.DMA((N_DEVICES - 1,)), pltpu.SemaphoreType.DMA((N_DEVICES - 1,))],
    )(g_pack, w_pack, m_pack, v_pack)


WEIGHTS = ("attn_norm_g", "w_in", "swa_q_norm_g", "swa_k_norm_g", "swa_sinks", "mla_cq_norm_g", "mla_ckv_norm_g",
           "w_uq", "w_ukv", "mla_qn_norm_g", "mla_qr_norm_g", "mla_kn_norm_g", "mla_kr_norm_g", "mem_norm_g",
           "w_mem_kv", "mem_q_norm_g", "mem_k_norm_g", "w_out", "ffn_norm_g", "w_gate", "w_up", "w_down")
BIG = ("w_in", "w_uq", "w_ukv", "w_mem_kv", "w_out", "w_gate", "w_up", "w_down")
SMALL = tuple(n for n in WEIGHTS if n not in BIG)
PACK_UNIT = 8 * LANES


def _pack(parts):
    out = []
    for p in parts:
        n = p.shape[1]
        padded = -(-n // PACK_UNIT) * PACK_UNIT
        out.append(jnp.pad(p, ((0, 0), (0, padded - n))).reshape(padded // LANES, LANES))
    return jnp.concatenate(out, axis=0)


def _unpack(buf, sizes):
    out, row = [], 0
    for n in sizes:
        rows = -(-n // PACK_UNIT) * 8
        out.append(buf[row:row + rows].reshape(1, rows * LANES)[:, :n])
        row += rows
    return out


def kernel(x, mem, positions, attn_norm_g, w_in, swa_q_norm_g, swa_k_norm_g, swa_sinks, mla_cq_norm_g, mla_ckv_norm_g, w_uq, w_ukv, mla_qn_norm_g, mla_qr_norm_g, mla_kn_norm_g, mla_kr_norm_g, mem_norm_g, w_mem_kv, mem_q_norm_g, mem_k_norm_g, w_out, ffn_norm_g, w_gate, w_up, w_down, loss_target, m_attn_norm_g, m_w_in, m_swa_q_norm_g, m_swa_k_norm_g, m_swa_sinks, m_mla_cq_norm_g, m_mla_ckv_norm_g, m_w_uq, m_w_ukv, m_mla_qn_norm_g, m_mla_qr_norm_g, m_mla_kn_norm_g, m_mla_kr_norm_g, m_mem_norm_g, m_w_mem_kv, m_mem_q_norm_g, m_mem_k_norm_g, m_w_out, m_ffn_norm_g, m_w_gate, m_w_up, m_w_down, v_attn_norm_g, v_w_in, v_swa_q_norm_g, v_swa_k_norm_g, v_swa_sinks, v_mla_cq_norm_g, v_mla_ckv_norm_g, v_w_uq, v_w_ukv, v_mla_qn_norm_g, v_mla_qr_norm_g, v_mla_kn_norm_g, v_mla_kr_norm_g, v_mem_norm_g, v_w_mem_kv, v_mem_q_norm_g, v_mem_k_norm_g, v_w_out, v_ffn_norm_g, v_w_gate, v_w_up, v_w_down):
    given = dict(locals())
    wts = {n: given[n] for n in WEIGHTS}
    mom_m = {n: given["m_" + n] for n in WEIGHTS}
    mom_v = {n: given["v_" + n] for n in WEIGHTS}

    mx, my, mc = lax.axis_index("x"), lax.axis_index("y"), lax.axis_index("c")
    meta = jnp.stack([2 * mx + my, mc]).astype(jnp.int32)
    x, mem, pos, target = x[0], mem[0], positions[0], loss_target[0]
    sp = {n: wts[n] for n in SMALL}
    s = x.shape[0]
    cos_t, sin_t = _rope_tables(pos)
    pos_f = pos.astype(F32)
    pos_col, pos_row = pos_f.reshape(s, 1), pos_f.reshape(1, s)
    g128 = _gain_table(sp)
    sinks = sp["swa_sinks"].reshape(SWA_Q_HEADS)
    gcq, gckv = sp["mla_cq_norm_g"], sp["mla_ckv_norm_g"]
    gs = {}

    slot = {n: _cast_into_slot(wts[n][0], meta, name="cast_" + n) for n in BIG}
    [first] = _run_stages([_gather_stage([slot["w_in"], slot["w_uq"], slot["w_ukv"]], "ici")], name="gather_first_ici")
    [first] = _run_stages([_gather_stage(first, "d2d")], name="gather_first_d2d")
    w_in_f, w_uq_f, w_ukv_f = _full_w_in(first[0]), _full_heads(first[1], MLA_NOPE), _full_heads(first[2], MLA_NOPE)

    hn = _rms_fwd(x, sp["attn_norm_g"], name="attn_norm_fwd")
    proj, [mid] = _matmul(hn, w_in_f, name="in_proj",
                          comm=[_gather_stage([slot["w_mem_kv"], slot["w_out"]], "ici")])
    (qa, ka, va, q_cat, k_cat, v_b, qm), [mid] = _attn_prep_fwd(
        proj, g128, gcq, gckv, w_uq_f, w_ukv_f, cos_t, sin_t, comm=[_gather_stage(mid, "d2d")])
    w_mem_kv_f = mid[0].reshape(D_MODEL, 2 * MEM_HEADS * MEM_DIM)
    w_out_f = mid[1].reshape(D_MODEL, D_MODEL)
    mn_b, kv_m, km, vm = _mem_kv_fwd(mem, sp["mem_norm_g"], w_mem_kv_f, sp["mem_k_norm_g"])
    (y_a, y), [wg] = _swa_fwd(qa, ka, va, pos_col, pos_row, sinks, comm=[_gather_stage([slot["w_gate"]], "ici")])
    (y_b, lse, y), [wu, wg] = _mla_fwd(
        q_cat, k_cat, v_b, y, comm=[_gather_stage([slot["w_up"]], "ici"), _gather_stage(wg, "d2d")])
    y_m, y = _mem_attn_fwd(qm, km, vm, y)
    h1, [wu] = _matmul(y, w_out_f, add=x, name="out_proj", comm=[_gather_stage(wu, "d2d")])
    w_gate_f, w_up_f = wg[0], wu[0]
    fn = _rms_fwd(h1, sp["ffn_norm_g"], name="ffn_norm_fwd")
    (gate, up, act), [wd] = _ffn_gate_up(fn, w_gate_f, w_up_f, comm=[_gather_stage([slot["w_down"]], "both")])
    w_down_f = wd[0].reshape(D_FF, D_MODEL)
    out = _matmul(act, w_down_f, add=h1, name="down_proj", tm=512, tk=D_FF)
    d_out, d_out_b, loss_tile = _loss_head(out, target)

    add_pair = lambda n, g4, r: _add_pair(meta, g4, r, name="grad_add_pair_" + n)
    add_chips = lambda n, own, r: _add_chips(own, r, name="grad_add_chips_" + n)
    mine, theirs = {}, {}

    dw_down = _matmul(act, d_out_b, ta=True, name="dw_down", tm=1408, tn=1024, tk=2048)
    dw_down = dw_down.reshape(N_CHIPS, D_FF // N_CHIPS, D_MODEL)
    (d_gate, d_up), [[r]] = _ffn_bwd_act(d_out_b, w_down_f, gate, up, comm=[_halves_stage([dw_down])])
    own_d, oth_d = add_pair("w_down", dw_down, r)
    dw_gate, [[r]] = _matmul(fn, d_gate, ta=True, name="dw_gate", tk=2048, tn=D_FF // N_CHIPS, out_split=N_CHIPS,
                             comm=[_chips_stage([oth_d])])
    mine["w_down"] = add_chips("w_down", own_d, r)
    dw_up, [[r]] = _matmul(fn, d_up, ta=True, name="dw_up", tk=2048, tn=D_FF // N_CHIPS, out_split=N_CHIPS,
                           comm=[_halves_stage([dw_gate])])
    own_g, oth_g = add_pair("w_gate", dw_gate, r)
    d_fn, [[r], [theirs["w_down"]]] = _matmul(
        d_gate, w_gate_f, tb=True, b_split=True, name="dfn_gate", tm=512,
        comm=[_chips_stage([oth_g]), _swap_stage([mine["w_down"]])])
    mine["w_gate"] = add_chips("w_gate", own_g, r)
    d_fn, [[r]] = _matmul(d_up, w_up_f, tb=True, b_split=True, add=d_fn, name="dfn_up", tm=512,
                          comm=[_halves_stage([dw_up])])
    own_u, oth_u = add_pair("w_up", dw_up, r)
    d_h1, d_h1_b, gs["ffn_norm_g"] = _rms_bwd(d_fn, h1, sp["ffn_norm_g"], d_out, name="ffn_norm_bwd")
    dw_out, [[theirs["w_gate"]]] = _matmul(y, d_h1_b, ta=True, name="dw_out", tk=2048,
                                           comm=[_swap_stage([mine["w_gate"]])])
    dw_out = dw_out.reshape(N_CHIPS, D_MODEL // N_CHIPS, D_MODEL)
    d_y, [[r]] = _matmul(d_h1_b, w_out_f, tb=True, name="dy", comm=[_halves_stage([dw_out])])
    own_o, oth_o = add_pair("w_out", dw_out, r)
    (d_qa, d_ka, d_va, d_sink), [[r]] = _swa_bwd(qa, ka, va, pos_col, pos_row, sinks, y_a, d_y,
                                                 comm=[_chips_stage([oth_u])])
    mine["w_up"] = add_chips("w_up", own_u, r)
    (d_qcat, d_kcat, d_vb), [[r], [theirs["w_up"]]] = _mla_bwd(
        q_cat, k_cat, v_b, y_b, lse, d_y, comm=[_chips_stage([oth_o]), _swap_stage([mine["w_up"]])])
    mine["w_out"] = add_chips("w_out", own_o, r)
    d_qm, d_km, d_vm = _mem_attn_bwd(qm, km, vm, y_m, d_y)
    (d_proj, dw_uq, dw_ukv, dg128, gs["mla_cq_norm_g"], gs["mla_ckv_norm_g"]), [[theirs["w_out"]]] = _attn_prep_bwd(
        proj, g128, gcq, gckv, w_uq_f, w_ukv_f, cos_t, sin_t, d_qa, d_ka, d_va, d_qcat, d_kcat, d_vb, d_qm,
        comm=[_swap_stage([mine["w_out"]])])
    dw_mem_kv, gs["mem_norm_g"], gs["mem_k_norm_g"] = _mem_kv_bwd(
        mem, sp["mem_norm_g"], w_mem_kv_f, sp["mem_k_norm_g"], mn_b, kv_m, d_km, d_vm)
    late = ("w_uq", "w_ukv", "w_mem_kv")
    late_g = [_shards_heads(dw_uq, MLA_NOPE, MLA_ROPE), _shards_heads(dw_ukv, MLA_NOPE, MLA_V),
              dw_mem_kv.reshape(N_CHIPS, D_MODEL // N_CHIPS, -1)]
    dw_in, [rs] = _matmul(hn, d_proj, ta=True, name="dw_in", tk=2048, comm=[_halves_stage(late_g)])
    late_sums = [add_pair(n, g4, r) for n, g4, r in zip(late, late_g, rs)]
    dw_in = _shards_w_in(dw_in)
    d_hn, [rs, [r]] = _matmul(d_proj, w_in_f, tb=True, name="dhn", tk=1536,
                              comm=[_chips_stage([oth for _, oth in late_sums]), _halves_stage([dw_in])])
    for n, (own, _), r_n in zip(late, late_sums, rs):
        mine[n] = add_chips(n, own, r_n)
    own_i, oth_i = add_pair("w_in", dw_in, r)
    grad_x, _, gs["attn_norm_g"] = _rms_bwd(d_hn, x, sp["attn_norm_g"], d_h1, name="attn_norm_bwd")
    [r], late_theirs = _run_stages([_chips_stage([oth_i]), _swap_stage([mine[n] for n in late])],
                                   name="grad_last_chips")
    theirs.update(zip(late, late_theirs))
    mine["w_in"] = add_chips("w_in", own_i, r)
    [[theirs["w_in"]]] = _run_stages([_swap_stage([mine["w_in"]])], name="grad_swap_w_in")

    fold = lambda r: r[:, :64] + r[:, 64:]
    gs["swa_q_norm_g"] = fold(dg128[G_SWA_Q:G_SWA_Q + 1])
    gs["swa_k_norm_g"] = fold(dg128[G_SWA_K:G_SWA_K + 1])
    gs["mla_qn_norm_g"] = dg128[G_QN:G_QN + 1]
    gs["mla_qr_norm_g"] = fold(dg128[G_QR:G_QR + 1])
    gs["mla_kn_norm_g"] = dg128[G_KN:G_KN + 1]
    gs["mla_kr_norm_g"] = fold(dg128[G_KR:G_KR + 1])
    gs["mem_q_norm_g"] = dg128[G_MQ:G_MQ + 1]
    gs["swa_sinks"] = d_sink[:, :SWA_Q_HEADS]

    grad, delta, new_m, new_v = {}, {}, {}, {}
    for n in BIG:
        g2, d, m2, v2 = _adamw(meta, wts[n][0], mine[n], theirs[n], mom_m[n][0], mom_v[n][0], name="adamw_" + n)
        grad[n], delta[n], new_m[n], new_v[n] = g2[None], d[None], m2[None], v2[None]

    sizes = [wts[n].shape[1] for n in SMALL]
    zero = jnp.zeros((1, LANES), F32)
    packs = _small_step(_pack([gs[n] for n in SMALL] + [loss_tile]), _pack([wts[n] for n in SMALL] + [zero]),
                        _pack([mom_m[n] for n in SMALL] + [zero]), _pack([mom_v[n] for n in SMALL] + [zero]))
    for store, buf in zip((grad, delta, new_m, new_v), packs):
        for n, val in zip(SMALL, _unpack(buf, sizes)):
            store[n] = val
    loss = _unpack(packs[0], sizes + [LANES])[-1][0, 0]

    return (loss, grad_x[None], *[grad[n] for n in WEIGHTS], *[delta[n] for n in WEIGHTS],
            *[new_m[n] for n in WEIGHTS], *[new_v[n] for n in WEIGHTS])
```

```python
import functools
import math

import jax
import jax.numpy as jnp
from jax import lax
from jax.experimental import pallas as pl
from jax.experimental.pallas import tpu as pltpu

F32 = jnp.float32
BF16 = jnp.bfloat16

D_MODEL = 2048
BLOCK = 128
EPS = 1e-6
NEG_INF = -1e30
SWA_Q_HEADS = 16
SWA_KV_HEADS = 2
SWA_HEAD_DIM = 64
MLA_HEADS = 4
MLA_RANK = 512
MLA_NOPE = 128
MLA_ROPE = 64
MLA_V = 128
ROPE_THETA = 10000.0
MEM_HEADS = 4
MEM_DIM = 128
D_FF = 5632
IN_WIDTH = 2880
IN_PAD = 3072
N_CHIPS = 4

ADAM_LR = 0.001
ADAM_B1 = 0.9
ADAM_B2 = 0.999
ADAM_EPS = 1e-08
ADAM_WD = 0.01
ADAM_STEP = 10

VMEM_LIMIT_BYTES = 56 * 1024 * 1024
LANES = 128

MESH = pl.DeviceIdType.MESH


def _params(sem=None, **kw):
    return pltpu.CompilerParams(dimension_semantics=sem, vmem_limit_bytes=VMEM_LIMIT_BYTES, **kw)


def _tile(n, want):
    if n <= want:
        return n
    t = want - want % LANES
    while t > 0:
        if n % t == 0:
            return t
        t -= LANES
    return n


ANY = pl.BlockSpec(memory_space=pl.ANY)


class _Stage:
    def __init__(self, ins, out_shapes, aliases, n_sem, issue, wait, mid=None):
        self.ins, self.out_shapes, self.aliases, self.n_sem = list(ins), list(out_shapes), dict(aliases), n_sem
        self.issue, self.wait, self.mid = issue, wait, mid


def _pcall(body, args, *, name, grid, in_specs, out_specs, out_shape, scratch_shapes=(), sem=None, comm=(),
           prefetch=(), io_alias=None):
    multi = isinstance(out_shape, (list, tuple))
    out_specs_l = list(out_specs) if multi else [out_specs]
    out_shape_l = list(out_shape) if multi else [out_shape]
    npf = len(prefetch)
    own_aliases = {npf + a: o for a, o in (io_alias or {}).items()}

    def call(fn, in_specs_, out_specs_, out_shape_, scratch_, operands, sem_, aliases=None):
        kw = dict(name=name, out_shape=out_shape_, compiler_params=_params(sem_))
        if aliases:
            kw["input_output_aliases"] = aliases
        if npf:
            spec = pltpu.PrefetchScalarGridSpec(num_scalar_prefetch=npf, grid=grid, in_specs=in_specs_,
                                                out_specs=out_specs_, scratch_shapes=scratch_)
            return pl.pallas_call(fn, grid_spec=spec, **kw)(*prefetch, *operands)
        return pl.pallas_call(fn, grid=grid, in_specs=in_specs_, out_specs=out_specs_, scratch_shapes=scratch_,
                              **kw)(*operands)

    if not comm:
        return call(body, list(in_specs), out_specs, out_shape, list(scratch_shapes), args, sem, own_aliases)
    n_in, n_out, n_scr = len(in_specs), len(out_specs_l), len(scratch_shapes)
    cins = [a for st in comm for a in st.ins]
    couts = [s for st in comm for s in st.out_shapes]
    aliases, ci, co = dict(own_aliases), 0, 0
    for st in comm:
        for a_i, o_i in st.aliases.items():
            aliases[npf + n_in + ci + a_i] = n_out + co + o_i
        ci, co = ci + len(st.ins), co + len(st.out_shapes)

    def wrapped(*refs):
        pre = refs[:npf]
        p = npf
        ins = refs[p:p + n_in]; p += n_in
        cin_refs = refs[p:p + len(cins)]; p += len(cins)
        outs = refs[p:p + n_out]; p += n_out
        cout_refs = refs[p:p + len(couts)]; p += len(couts)
        scr = refs[p:p + n_scr]; p += n_scr
        sems = refs[p:]
        first = functools.reduce(jnp.logical_and, [pl.program_id(a) == 0 for a in range(len(grid))])
        last = functools.reduce(jnp.logical_and, [pl.program_id(a) == grid[a] - 1 for a in range(len(grid))])

        def each(what):
            i, o = 0, 0
            for k, st in enumerate(comm):
                fn = getattr(st, what)
                if fn is not None:
                    fn(cin_refs[i:i + len(st.ins)], cout_refs[o:o + len(st.out_shapes)], sems[2 * k], sems[2 * k + 1])
                i, o = i + len(st.ins), o + len(st.out_shapes)

        @pl.when(first)
        def _():
            each("issue")

        if any(st.mid is not None for st in comm):
            n_steps = math.prod(grid)
            assert n_steps >= 4, "a two-leg stage needs a carrier with several grid steps"
            lin = functools.reduce(lambda acc, a: acc * grid[a] + pl.program_id(a), range(len(grid)), 0)

            @pl.when(lin == (3 * n_steps) // 4)
            def _():
                each("mid")

        body(*pre, *ins, *outs, *scr)

        @pl.when(last)
        def _():
            each("wait")

    sem_scr = [pltpu.SemaphoreType.DMA((st.n_sem,)) for st in comm for _ in range(2)]
    res = call(wrapped, list(in_specs) + [ANY] * len(cins), out_specs_l + [ANY] * len(couts), out_shape_l + couts,
               list(scratch_shapes) + sem_scr, (*args, *cins), ("arbitrary",) * len(grid), aliases)
    normal = list(res[:n_out])
    stage_outs, o = [], n_out
    for st in comm:
        stage_outs.append(list(res[o:o + len(st.out_shapes)]))
        o += len(st.out_shapes)
    return (normal if multi else normal[0]), stage_outs


def _matmul(a, b, *, name, ta=False, tb=False, add=None, out_dtype=F32, tm=1024, tn=1024, tk=2048,
            b_split=False, out_split=0, comm=()):
    if ta:
        kdim, m = a.shape
    else:
        m, kdim = a.shape
    if b_split:
        assert tb
        nsp, n, kb = b.shape
        kb = kb * nsp
    elif tb:
        n, kb = b.shape
    else:
        kb, n = b.shape
    assert kb == kdim, (a.shape, b.shape, ta, tb)
    if b_split:
        tk = kdim
    if out_split:
        tn = _tile(n // out_split, tn)
    tm, tn, tk = _tile(m, tm), _tile(n, tn), _tile(kdim, tk)
    nk = kdim // tk
    dims = (((0 if ta else 1,), (1 if tb else 0,)), ((), ()))

    def product(a_ref, b_ref):
        if not b_split:
            return lax.dot_general(a_ref[...].astype(BF16), b_ref[...].astype(BF16), dims, preferred_element_type=F32)
        per = kdim // nsp
        return sum(lax.dot_general(a_ref[:, per * c:per * (c + 1)].astype(BF16), b_ref[c].astype(BF16), dims,
                                   preferred_element_type=F32) for c in range(nsp))

    def body(*refs):
        a_ref, b_ref = refs[:2]
        add_ref = refs[2] if add is not None else None
        o_ref = refs[3 if add is not None else 2]

        def finish(r):
            if add_ref is not None:
                r = r + add_ref[...].astype(F32)
            o_ref[...] = r.astype(o_ref.dtype)

        if nk == 1:
            finish(product(a_ref, b_ref))
            return
        acc_ref = refs[-1]
        k = pl.program_id(2)
        part = product(a_ref, b_ref)

        @pl.when(k == 0)
        def _():
            acc_ref[...] = part

        @pl.when(k > 0)
        def _():
            acc_ref[...] += part

        @pl.when(k == nk - 1)
        def _():
            finish(acc_ref[...])

    a_spec = pl.BlockSpec((tk, tm), lambda i, j, k: (k, i)) if ta else pl.BlockSpec((tm, tk), lambda i, j, k: (i, k))
    if b_split:
        b_spec = pl.BlockSpec((nsp, tn, kdim // nsp), lambda i, j, k: (0, j, 0))
    elif tb:
        b_spec = pl.BlockSpec((tn, tk), lambda i, j, k: (j, k))
    else:
        b_spec = pl.BlockSpec((tk, tn), lambda i, j, k: (k, j))
    in_specs = [a_spec, b_spec]
    args = [a, b]
    if add is not None:
        in_specs.append(pl.BlockSpec((tm, tn), lambda i, j, k: (i, j)))
        args.append(add)
    if out_split:
        per = (n // out_split) // tn
        out_spec = pl.BlockSpec((None, tm, tn), lambda i, j, k: (j // per, i, j % per))
        out_shape = jax.ShapeDtypeStruct((out_split, m, n // out_split), out_dtype)
    else:
        out_spec = pl.BlockSpec((tm, tn), lambda i, j, k: (i, j))
        out_shape = jax.ShapeDtypeStruct((m, n), out_dtype)
    return _pcall(body, args, name=name, grid=(m // tm, n // tn, nk), in_specs=in_specs, out_specs=out_spec,
                  out_shape=out_shape, scratch_shapes=[pltpu.VMEM((tm, tn), F32)] if nk > 1 else [],
                  sem=("parallel", "parallel", "arbitrary"), comm=comm)


def _rms_fwd(x, g, *, name, tm=512, comm=()):
    s, d = x.shape
    tm = _tile(s, tm)

    def body(x_ref, g_ref, o_ref):
        xv = x_ref[...]
        r = lax.rsqrt(jnp.mean(xv * xv, axis=-1, keepdims=True) + EPS)
        o_ref[...] = (xv * r * g_ref[...]).astype(o_ref.dtype)

    return _pcall(body, (x, g), name=name, grid=(s // tm,),
                  in_specs=[pl.BlockSpec((tm, d), lambda i: (i, 0)), pl.BlockSpec((1, d), lambda i: (0, 0))],
                  out_specs=pl.BlockSpec((tm, d), lambda i: (i, 0)),
                  out_shape=jax.ShapeDtypeStruct((s, d), BF16), sem=("parallel",), comm=comm)


def _rms_bwd(dy, x, g, res, *, name, tm=512, comm=()):
    s, d = x.shape
    tm = _tile(s, tm)

    def body(dy_ref, x_ref, g_ref, res_ref, dx_ref, dxb_ref, dg_ref):
        xv = x_ref[...]
        dyv = dy_ref[...]
        r = lax.rsqrt(jnp.mean(xv * xv, axis=-1, keepdims=True) + EPS)
        xhat = xv * r
        dyg = dyv * g_ref[...]
        mt = jnp.mean(dyg * xhat, axis=-1, keepdims=True)
        dx = res_ref[...] + r * (dyg - xhat * mt)
        dx_ref[...] = dx
        dxb_ref[...] = dx.astype(BF16)
        part = jnp.sum(dyv * xhat, axis=0, keepdims=True)

        @pl.when(pl.program_id(0) == 0)
        def _():
            dg_ref[...] = part

        @pl.when(pl.program_id(0) > 0)
        def _():
            dg_ref[...] += part

    row = pl.BlockSpec((tm, d), lambda i: (i, 0))
    vec = pl.BlockSpec((1, d), lambda i: (0, 0))
    return _pcall(body, (dy, x, g, res), name=name, grid=(s // tm,), in_specs=[row, row, vec, row],
                  out_specs=[row, row, vec],
                  out_shape=[jax.ShapeDtypeStruct((s, d), F32), jax.ShapeDtypeStruct((s, d), BF16),
                             jax.ShapeDtypeStruct((1, d), F32)],
                  sem=("arbitrary",), comm=comm)


def _lane(shape):
    return lax.broadcasted_iota(jnp.int32, shape, 1)


def _halfsum(t, lo):
    s_lo = jnp.sum(jnp.where(lo, t, 0.0), axis=-1, keepdims=True)
    s_hi = jnp.sum(jnp.where(lo, 0.0, t), axis=-1, keepdims=True)
    return jnp.where(lo, s_lo, s_hi)


def _norm_pair(x, g, lo):
    r = lax.rsqrt(_halfsum(x * x, lo) * (1.0 / 64.0) + EPS)
    xhat = x * r
    return xhat * g, xhat, r


def _norm_pair_bwd(dy, g, xhat, r, lo):
    dyg = dy * g
    mt = _halfsum(dyg * xhat, lo) * (1.0 / 64.0)
    return r * (dyg - xhat * mt), jnp.sum(dy * xhat, axis=0, keepdims=True)


def _norm_full(x, g):
    r = lax.rsqrt(jnp.mean(x * x, axis=-1, keepdims=True) + EPS)
    xhat = x * r
    return xhat * g, xhat, r


def _norm_full_bwd(dy, g, xhat, r):
    dyg = dy * g
    mt = jnp.mean(dyg * xhat, axis=-1, keepdims=True)
    return r * (dyg - xhat * mt), jnp.sum(dy * xhat, axis=0, keepdims=True)


def _rot(x, first32):
    return jnp.where(first32, pltpu.roll(x, 96, axis=1), pltpu.roll(x, 32, axis=1))


def _rope(x, cos_t, sin_t, first32):
    return x * cos_t + _rot(x, first32) * sin_t


def _rope_bwd(dy, cos_t, sin_t, first32):
    return dy * cos_t + _rot(dy * sin_t, first32)


G_SWA_Q, G_SWA_K, G_QN, G_QR, G_KN, G_KR, G_MQ = range(7)

C_QA, C_KA, C_VA, C_CQ, C_CKV, C_QM, C_KR = 0, 1024, 1152, 1280, 1792, 2304, 2816


def _prep_common(p_ref, g128_ref, gcq_ref, gckv_ref, wuq_ref, wukv_ref, cos_ref, sin_ref):
    tm = p_ref.shape[0]
    lane = _lane((tm, LANES))
    lo = lane < 64
    first32 = (lane % 64) < 32
    cos_t = cos_ref[...]
    sin_t = sin_ref[...]
    g = lambda row: g128_ref[row:row + 1, :]
    out = dict(lo=lo, first32=first32, cos_t=cos_t, sin_t=sin_t, lane=lane)
    cq_n, cq_hat, cq_r = _norm_full(p_ref[:, C_CQ:C_CQ + MLA_RANK], gcq_ref[...])
    ckv_n, ckv_hat, ckv_r = _norm_full(p_ref[:, C_CKV:C_CKV + MLA_RANK], gckv_ref[...])
    cq_b = cq_n.astype(BF16)
    ckv_b = ckv_n.astype(BF16)
    q_b = jnp.dot(cq_b, wuq_ref[...], preferred_element_type=F32)
    kv_b = jnp.dot(ckv_b, wukv_ref[...], preferred_element_type=F32)
    out.update(cq_b=cq_b, cq_hat=cq_hat, cq_r=cq_r, ckv_b=ckv_b, ckv_hat=ckv_hat, ckv_r=ckv_r, q_b=q_b, kv_b=kv_b, g=g)
    return out


def _attn_prep_fwd(proj, g128, gcq, gckv, wuq, wukv, cos_t, sin_t, *, tm=512, comm=()):
    s = proj.shape[0]
    tm = _tile(s, tm)

    def body(p_ref, g128_ref, gcq_ref, gckv_ref, wuq_ref, wukv_ref, cos_ref, sin_ref,
             qa_ref, ka_ref, va_ref, qcat_ref, kcat_ref, vb_ref, qm_ref):
        c = _prep_common(p_ref, g128_ref, gcq_ref, gckv_ref, wuq_ref, wukv_ref, cos_ref, sin_ref)
        lo, first32, g = c["lo"], c["first32"], c["g"]
        for j in range(SWA_Q_HEADS // 2):
            y, _, _ = _norm_pair(p_ref[:, C_QA + 128 * j:C_QA + 128 * (j + 1)], g(G_SWA_Q), lo)
            qa_ref[:, 128 * j:128 * (j + 1)] = y.astype(BF16)
        y, _, _ = _norm_pair(p_ref[:, C_KA:C_KA + 128], g(G_SWA_K), lo)
        ka_ref[...] = y.astype(BF16)
        va_ref[...] = p_ref[:, C_VA:C_VA + 128].astype(BF16)
        kr, _, _ = _norm_pair(p_ref[:, C_KR:C_KR + 128], g(G_KR), lo)
        kr = jnp.where(lo, _rope(kr, c["cos_t"], c["sin_t"], first32), 0.0)
        krkr = (kr + pltpu.roll(kr, 64, axis=1)).astype(BF16)
        q_b, kv_b = c["q_b"], c["kv_b"]
        qr = []
        for j in range(MLA_HEADS // 2):
            y, _, _ = _norm_pair(q_b[:, 512 + 128 * j:512 + 128 * (j + 1)], g(G_QR), lo)
            qr.append(_rope(y, c["cos_t"], c["sin_t"], first32))
        for h in range(MLA_HEADS):
            qn, _, _ = _norm_full(q_b[:, 128 * h:128 * (h + 1)], g(G_QN))
            keep = lo if h % 2 == 0 else jnp.logical_not(lo)
            qcat_ref[h, :, 0:128] = qn.astype(BF16)
            qcat_ref[h, :, 128:256] = jnp.where(keep, qr[h // 2], 0.0).astype(BF16)
            kn, _, _ = _norm_full(kv_b[:, 128 * h:128 * (h + 1)], g(G_KN))
            kcat_ref[h, :, 0:128] = kn.astype(BF16)
            kcat_ref[h, :, 128:256] = krkr
        vb_ref[...] = kv_b[:, 512:1024].astype(BF16)
        for h in range(MEM_HEADS):
            y, _, _ = _norm_full(p_ref[:, C_QM + 128 * h:C_QM + 128 * (h + 1)], g(G_MQ))
            qm_ref[:, 128 * h:128 * (h + 1)] = y.astype(BF16)

    row = lambda w: pl.BlockSpec((tm, w), lambda i: (i, 0))
    full = lambda shape: pl.BlockSpec(shape, lambda i: tuple(0 for _ in shape))
    cat = pl.BlockSpec((MLA_HEADS, tm, 256), lambda i: (0, i, 0))
    return _pcall(
        body, (proj, g128, gcq, gckv, wuq, wukv, cos_t, sin_t), name="attn_prep_fwd", grid=(s // tm,),
        in_specs=[row(IN_PAD), full((8, 128)), full((1, 512)), full((1, 512)), full((512, 768)), full((512, 1024)),
                  row(128), row(128)],
        out_specs=[row(1024), row(128), row(128), cat, cat, row(512), row(512)],
        out_shape=[jax.ShapeDtypeStruct((s, 1024), BF16), jax.ShapeDtypeStruct((s, 128), BF16),
                   jax.ShapeDtypeStruct((s, 128), BF16), jax.ShapeDtypeStruct((MLA_HEADS, s, 256), BF16),
                   jax.ShapeDtypeStruct((MLA_HEADS, s, 256), BF16), jax.ShapeDtypeStruct((s, 512), BF16),
                   jax.ShapeDtypeStruct((s, 512), BF16)],
        sem=("parallel",), comm=comm)


def _attn_prep_bwd(proj, g128, gcq, gckv, wuq, wukv, cos_t, sin_t,
                   d_qa, d_ka, d_va, d_qcat, d_kcat, d_vb, d_qm, *, tm=256, comm=()):
    s = proj.shape[0]
    tm = _tile(s, tm)

    def body(p_ref, g128_ref, gcq_ref, gckv_ref, wuq_ref, wukv_ref, cos_ref, sin_ref,
             dqa_ref, dka_ref, dva_ref, dqcat_ref, dkcat_ref, dvb_ref, dqm_ref,
             dp_ref, dwuq_ref, dwukv_ref, dg128_ref, dgcq_ref, dgckv_ref):
        c = _prep_common(p_ref, g128_ref, gcq_ref, gckv_ref, wuq_ref, wukv_ref, cos_ref, sin_ref)
        lo, first32, g = c["lo"], c["first32"], c["g"]
        cos_v, sin_v = c["cos_t"], c["sin_t"]
        q_b, kv_b = c["q_b"], c["kv_b"]
        zero_row = jnp.zeros((1, LANES), F32)
        dg = {k: zero_row for k in range(7)}

        for j in range(SWA_Q_HEADS // 2):
            sl = slice(C_QA + 128 * j, C_QA + 128 * (j + 1))
            _, xhat, r = _norm_pair(p_ref[:, sl], g(G_SWA_Q), lo)
            dx, dgj = _norm_pair_bwd(dqa_ref[:, 128 * j:128 * (j + 1)], g(G_SWA_Q), xhat, r, lo)
            dp_ref[:, sl] = dx.astype(BF16)
            dg[G_SWA_Q] = dg[G_SWA_Q] + dgj
        _, xhat, r = _norm_pair(p_ref[:, C_KA:C_KA + 128], g(G_SWA_K), lo)
        dx, dgj = _norm_pair_bwd(dka_ref[...], g(G_SWA_K), xhat, r, lo)
        dp_ref[:, C_KA:C_KA + 128] = dx.astype(BF16)
        dg[G_SWA_K] = dgj
        dp_ref[:, C_VA:C_VA + 128] = dva_ref[...].astype(BF16)

        dqb_parts = [None] * 6
        for h in range(MLA_HEADS):
            _, xhat, r = _norm_full(q_b[:, 128 * h:128 * (h + 1)], g(G_QN))
            dx, dgj = _norm_full_bwd(dqcat_ref[h, :, 0:128], g(G_QN), xhat, r)
            dqb_parts[h] = dx
            dg[G_QN] = dg[G_QN] + dgj
        for j in range(MLA_HEADS // 2):
            _, xhat, r = _norm_pair(q_b[:, 512 + 128 * j:512 + 128 * (j + 1)], g(G_QR), lo)
            d_rot = jnp.where(lo, dqcat_ref[2 * j, :, 128:256], dqcat_ref[2 * j + 1, :, 128:256])
            d_y = _rope_bwd(d_rot, cos_v, sin_v, first32)
            dx, dgj = _norm_pair_bwd(d_y, g(G_QR), xhat, r, lo)
            dqb_parts[4 + j] = dx
            dg[G_QR] = dg[G_QR] + dgj
        d_qb = jnp.concatenate(dqb_parts, axis=1).astype(BF16)
        dwuq = lax.dot_general(c["cq_b"], d_qb, (((0,), (0,)), ((), ())), preferred_element_type=F32)
        d_cqn = lax.dot_general(d_qb, wuq_ref[...], (((1,), (1,)), ((), ())), preferred_element_type=F32)
        dx, dgcq = _norm_full_bwd(d_cqn, gcq_ref[...], c["cq_hat"], c["cq_r"])
        dp_ref[:, C_CQ:C_CQ + MLA_RANK] = dx.astype(BF16)

        dkv_parts = []
        d_krkr = jnp.zeros((p_ref.shape[0], LANES), F32)
        for h in range(MLA_HEADS):
            _, xhat, r = _norm_full(kv_b[:, 128 * h:128 * (h + 1)], g(G_KN))
            dx, dgj = _norm_full_bwd(dkcat_ref[h, :, 0:128], g(G_KN), xhat, r)
            dkv_parts.append(dx)
            dg[G_KN] = dg[G_KN] + dgj
            d_krkr = d_krkr + dkcat_ref[h, :, 128:256]
        d_kvb = jnp.concatenate(dkv_parts + [dvb_ref[...]], axis=1).astype(BF16)
        dwukv = lax.dot_general(c["ckv_b"], d_kvb, (((0,), (0,)), ((), ())), preferred_element_type=F32)
        d_ckvn = lax.dot_general(d_kvb, wukv_ref[...], (((1,), (1,)), ((), ())), preferred_element_type=F32)
        dx, dgckv = _norm_full_bwd(d_ckvn, gckv_ref[...], c["ckv_hat"], c["ckv_r"])
        dp_ref[:, C_CKV:C_CKV + MLA_RANK] = dx.astype(BF16)

        _, xhat, r = _norm_pair(p_ref[:, C_KR:C_KR + 128], g(G_KR), lo)
        d_kr = jnp.where(lo, d_krkr + pltpu.roll(d_krkr, 64, axis=1), 0.0)
        d_y = jnp.where(lo, _rope_bwd(d_kr, cos_v, sin_v, first32), 0.0)
        dx, dgj = _norm_pair_bwd(d_y, g(G_KR), xhat, r, lo)
        dp_ref[:, C_KR:C_KR + 128] = jnp.where(lo, dx, 0.0).astype(BF16)
        dp_ref[:, C_KR + 128:] = jnp.zeros((p_ref.shape[0], IN_PAD - C_KR - 128), BF16)
        dg[G_KR] = dgj

        for h in range(MEM_HEADS):
            sl = slice(C_QM + 128 * h, C_QM + 128 * (h + 1))
            _, xhat, r = _norm_full(p_ref[:, sl], g(G_MQ))
            dx, dgj = _norm_full_bwd(dqm_ref[:, 128 * h:128 * (h + 1)], g(G_MQ), xhat, r)
            dp_ref[:, sl] = dx.astype(BF16)
            dg[G_MQ] = dg[G_MQ] + dgj

        dg_tile = jnp.concatenate([dg[k] for k in range(7)] + [zero_row], axis=0)

        @pl.when(pl.program_id(0) == 0)
        def _():
            dwuq_ref[...] = dwuq
            dwukv_ref[...] = dwukv
            dg128_ref[...] = dg_tile
            dgcq_ref[...] = dgcq
            dgckv_ref[...] = dgckv

        @pl.when(pl.program_id(0) > 0)
        def _():
            dwuq_ref[...] += dwuq
            dwukv_ref[...] += dwukv
            dg128_ref[...] += dg_tile
            dgcq_ref[...] += dgcq
            dgckv_ref[...] += dgckv

    row = lambda w: pl.BlockSpec((tm, w), lambda i: (i, 0))
    full = lambda shape: pl.BlockSpec(shape, lambda i: tuple(0 for _ in shape))
    cat = pl.BlockSpec((MLA_HEADS, tm, 256), lambda i: (0, i, 0))
    return _pcall(
        body, (proj, g128, gcq, gckv, wuq, wukv, cos_t, sin_t, d_qa, d_ka, d_va, d_qcat, d_kcat, d_vb, d_qm),
        name="attn_prep_bwd", grid=(s // tm,),
        in_specs=[row(IN_PAD), full((8, 128)), full((1, 512)), full((1, 512)), full((512, 768)), full((512, 1024)),
                  row(128), row(128),
                  row(1024), row(128), row(128), cat, cat, row(512), row(512)],
        out_specs=[row(IN_PAD), full((512, 768)), full((512, 1024)), full((8, 128)), full((1, 512)), full((1, 512))],
        out_shape=[jax.ShapeDtypeStruct((s, IN_PAD), BF16), jax.ShapeDtypeStruct((512, 768), F32),
                   jax.ShapeDtypeStruct((512, 1024), F32), jax.ShapeDtypeStruct((8, 128), F32),
                   jax.ShapeDtypeStruct((1, 512), F32), jax.ShapeDtypeStruct((1, 512), F32)],
        sem=("arbitrary",), comm=comm)


SWA_SLOPES = tuple(2.0 ** (-8.0 * h / SWA_Q_HEADS) for h in range(1, SWA_Q_HEADS + 1))
SWA_SCALE = SWA_HEAD_DIM ** -0.5
NT_DIMS = (((1,), (1,)), ((), ()))
TN_DIMS = (((0,), (0,)), ((), ()))


def _swa_span(n, kp_ref, kc_ref, vp_ref, vc_ref, pcol_ref, pprow_ref, pcrow_ref):
    k_span = jnp.concatenate([kp_ref[...], kc_ref[...]], axis=0).astype(F32)
    v_span = jnp.concatenate([vp_ref[...], vc_ref[...]], axis=0).astype(F32)
    lo = _lane((2 * BLOCK, LANES)) < 64
    k_sw = pltpu.roll(k_span, 64, axis=1)
    v_sw = pltpu.roll(v_span, 64, axis=1)
    kk = (jnp.where(lo, k_span, k_sw).astype(BF16), jnp.where(lo, k_sw, k_span).astype(BF16))
    vv_lo = (jnp.where(lo, v_span, 0.0).astype(BF16), jnp.where(lo, v_sw, 0.0).astype(BF16))
    vv_hi = (jnp.where(lo, 0.0, v_sw).astype(BF16), jnp.where(lo, 0.0, v_span).astype(BF16))
    pk = jnp.concatenate([pprow_ref[...], pcrow_ref[...]], axis=1)
    dist = jnp.abs(pcol_ref[...] - pk)
    qi = lax.broadcasted_iota(jnp.int32, (BLOCK, 2 * BLOCK), 0)
    ki = lax.broadcasted_iota(jnp.int32, (BLOCK, 2 * BLOCK), 1)
    first_key = jnp.where(n > 0, qi + 1, jnp.maximum(qi + 1, BLOCK))
    valid = jnp.logical_and(ki >= first_key, ki <= qi + BLOCK)
    mask_add = jnp.where(valid, 0.0, NEG_INF)
    return kk, vv_lo, vv_hi, dist, mask_add


def _swa_heads(q_ref, lo):
    heads = []
    for j in range(SWA_Q_HEADS // 2):
        q_pair = q_ref[:, 128 * j:128 * (j + 1)].astype(F32)
        for par in (0, 1):
            q_h = jnp.where(lo if par == 0 else jnp.logical_not(lo), q_pair, 0.0).astype(BF16)
            heads.append((2 * j + par, (2 * j) // (SWA_Q_HEADS // SWA_KV_HEADS), par, q_h))
    return heads


def _swa_probs(raw, dist, mask_add, slope, sink):
    s = raw * SWA_SCALE - slope * dist + mask_add
    m = jnp.maximum(jnp.max(s, axis=-1, keepdims=True), sink)
    e = jnp.exp(s - m)
    e_sink = jnp.exp(sink - m)
    inv = 1.0 / (jnp.sum(e, axis=-1, keepdims=True) + e_sink)
    return e * inv, e_sink * inv


def _swa_specs():
    blk = lambda w: pl.BlockSpec((BLOCK, w), lambda n: (n, 0))
    prev = lambda w: pl.BlockSpec((BLOCK, w), lambda n: (jnp.maximum(n - 1, 0), 0))
    prow_c = pl.BlockSpec((1, BLOCK), lambda n: (0, n))
    prow_p = pl.BlockSpec((1, BLOCK), lambda n: (0, jnp.maximum(n - 1, 0)))
    smem = pl.BlockSpec(memory_space=pltpu.SMEM)
    return [blk(1024), prev(128), blk(128), prev(128), blk(128), blk(1), prow_p, prow_c, smem], blk


def _swa_fwd(qa, ka, va, pos_col, pos_row, sinks, *, comm=()):
    s = qa.shape[0]
    in_specs, blk = _swa_specs()

    def body(q_ref, kp_ref, kc_ref, vp_ref, vc_ref, pcol_ref, pprow_ref, pcrow_ref, sink_ref, o_ref, yb_ref):
        n = pl.program_id(0)
        kk, vv_lo, vv_hi, dist, mask_add = _swa_span(n, kp_ref, kc_ref, vp_ref, vc_ref, pcol_ref, pprow_ref, pcrow_ref)
        lo = _lane((BLOCK, LANES)) < 64
        heads = _swa_heads(q_ref, lo)
        raws = [lax.dot_general(q_h, kk[kv], NT_DIMS, preferred_element_type=F32) for _, kv, _, q_h in heads]
        probs = [_swa_probs(raw, dist, mask_add, SWA_SLOPES[h], sink_ref[h])[0].astype(BF16)
                 for raw, (h, _, _, _) in zip(raws, heads)]
        for j in range(SWA_Q_HEADS // 2):
            kv = heads[2 * j][1]
            out = (jnp.dot(probs[2 * j], vv_lo[kv], preferred_element_type=F32)
                   + jnp.dot(probs[2 * j + 1], vv_hi[kv], preferred_element_type=F32))
            o_ref[:, 128 * j:128 * (j + 1)] = out
            yb_ref[:, 128 * j:128 * (j + 1)] = out.astype(BF16)

    return _pcall(body, (qa, ka, ka, va, va, pos_col, pos_row, pos_row, sinks), name="swa_fwd", grid=(s // BLOCK,),
                  in_specs=in_specs, out_specs=[blk(1024), blk(1024)],
                  out_shape=[jax.ShapeDtypeStruct((s, 1024), F32), jax.ShapeDtypeStruct((s, D_MODEL), BF16)],
                  sem=("parallel",), comm=comm)


def _swa_bwd(qa, ka, va, pos_col, pos_row, sinks, y_a, d_y, *, comm=()):
    s = qa.shape[0]
    in_specs, blk = _swa_specs()
    whole = pl.BlockSpec((s, 128), lambda n: (0, 0))

    def body(q_ref, kp_ref, kc_ref, vp_ref, vc_ref, pcol_ref, pprow_ref, pcrow_ref, sink_ref, y_ref, dy_ref,
             dq_ref, dk_ref, dv_ref, dsink_ref):
        n = pl.program_id(0)

        @pl.when(n == 0)
        def _():
            dk_ref[...] = jnp.zeros_like(dk_ref)
            dv_ref[...] = jnp.zeros_like(dv_ref)
            dsink_ref[...] = jnp.zeros_like(dsink_ref)

        kk, vv_lo, vv_hi, dist, mask_add = _swa_span(n, kp_ref, kc_ref, vp_ref, vc_ref, pcol_ref, pprow_ref, pcrow_ref)
        lo = _lane((BLOCK, LANES)) < 64
        lo2 = _lane((2 * BLOCK, LANES)) < 64
        lane1 = _lane((1, LANES))
        dsink = jnp.zeros((1, LANES), F32)
        dkk = [jnp.zeros((2 * BLOCK, LANES), F32) for _ in range(SWA_KV_HEADS)]
        dvv = [jnp.zeros((2 * BLOCK, LANES), F32) for _ in range(SWA_KV_HEADS)]
        heads = _swa_heads(q_ref, lo)
        do_b, deltas = [], []
        for j in range(SWA_Q_HEADS // 2):
            do_pair = dy_ref[:, 128 * j:128 * (j + 1)]
            doy = do_pair * y_ref[:, 128 * j:128 * (j + 1)]
            do_b.append(do_pair.astype(BF16))
            deltas.append(jnp.sum(jnp.where(lo, doy, 0.0), axis=-1, keepdims=True))
            deltas.append(jnp.sum(jnp.where(lo, 0.0, doy), axis=-1, keepdims=True))
        raws = [lax.dot_general(q_h, kk[kv], NT_DIMS, preferred_element_type=F32) for _, kv, _, q_h in heads]
        dps = [lax.dot_general(do_b[h // 2], (vv_lo, vv_hi)[par][kv], NT_DIMS, preferred_element_type=F32)
               for h, kv, par, _ in heads]
        p_b, ds_b = [], []
        for h, kv, par, _ in heads:
            p, p_sink = _swa_probs(raws[h], dist, mask_add, SWA_SLOPES[h], sink_ref[h])
            ds = p * (dps[h] - deltas[h])
            dsink = dsink + jnp.where(lane1 == h, -jnp.sum(p_sink * deltas[h], axis=0, keepdims=True), 0.0)
            p_b.append(p.astype(BF16))
            ds_b.append((ds * SWA_SCALE).astype(BF16))
        dq_halves = []
        for h, kv, par, q_h in heads:
            dq_halves.append(jnp.dot(ds_b[h], kk[kv], preferred_element_type=F32))
            dkk[kv] = dkk[kv] + lax.dot_general(ds_b[h], q_h, TN_DIMS, preferred_element_type=F32)
            pv = lax.dot_general(p_b[h], do_b[h // 2], TN_DIMS, preferred_element_type=F32)
            dvv[kv] = dvv[kv] + jnp.where(lo2 if par == 0 else jnp.logical_not(lo2), pv, 0.0)
        for j in range(SWA_Q_HEADS // 2):
            dq_ref[:, 128 * j:128 * (j + 1)] = jnp.where(lo, dq_halves[2 * j], dq_halves[2 * j + 1])
        fold = lambda t: t + pltpu.roll(t, 64, axis=1)
        dk_span = jnp.where(lo2, fold(dkk[0]), fold(dkk[1]))
        dv_span = jnp.where(lo2, fold(dvv[0]), fold(dvv[1]))
        prev0 = pl.multiple_of(jnp.maximum(n - 1, 0) * BLOCK, BLOCK)
        cur0 = pl.multiple_of(n * BLOCK, BLOCK)
        dk_ref[pl.ds(prev0, BLOCK), :] += dk_span[0:BLOCK]
        dk_ref[pl.ds(cur0, BLOCK), :] += dk_span[BLOCK:]
        dv_ref[pl.ds(prev0, BLOCK), :] += dv_span[0:BLOCK]
        dv_ref[pl.ds(cur0, BLOCK), :] += dv_span[BLOCK:]
        dsink_ref[...] += dsink

    return _pcall(
        body, (qa, ka, ka, va, va, pos_col, pos_row, pos_row, sinks, y_a, d_y), name="swa_bwd", grid=(s // BLOCK,),
        in_specs=in_specs + [blk(1024), blk(1024)],
        out_specs=[blk(1024), whole, whole, pl.BlockSpec((1, LANES), lambda n: (0, 0))],
        out_shape=[jax.ShapeDtypeStruct((s, 1024), F32), jax.ShapeDtypeStruct((s, 128), F32),
                   jax.ShapeDtypeStruct((s, 128), F32), jax.ShapeDtypeStruct((1, LANES), F32)],
        sem=("arbitrary",), comm=comm)


MLA_SCALE = (MLA_NOPE + MLA_ROPE) ** -0.5
MLA_TILE = 512


def _tile_pairs(nt, q_major):
    pairs = [(i, j) for i in range(nt) for j in range(i + 1)] if q_major else \
            [(i, j) for j in range(nt) for i in range(j, nt)]
    return jnp.asarray([p[0] for p in pairs], jnp.int32), jnp.asarray([p[1] for p in pairs], jnp.int32)


def _diag_mask(t):
    return lax.broadcasted_iota(jnp.int32, (t, t), 1) <= lax.broadcasted_iota(jnp.int32, (t, t), 0)


def _mla_fwd(q_cat, k_cat, v_b, y_all, *, comm=()):
    nh, s, _ = q_cat.shape
    t = _tile(s, MLA_TILE)
    qi, kj = _tile_pairs(s // t, True)
    ycol = (SWA_Q_HEADS * SWA_HEAD_DIM) // (nh * MLA_V)

    def body(qi_ref, kj_ref, q_ref, k_ref, v_ref, _, o_ref, lse_ref, yb_ref, m_sc, l_sc, acc_sc):
        i, j = qi_ref[pl.program_id(0)], kj_ref[pl.program_id(0)]

        @pl.when(j == 0)
        def _():
            m_sc[...] = jnp.full_like(m_sc, NEG_INF)
            l_sc[...] = jnp.zeros_like(l_sc)
            acc_sc[...] = jnp.zeros_like(acc_sc)

        def update(diagonal):
            scores = [lax.dot_general(q_ref[h], k_ref[h], NT_DIMS, preferred_element_type=F32) for h in range(nh)]
            probs, alphas = [], []
            for h in range(nh):
                sc = scores[h] * MLA_SCALE
                if diagonal:
                    sc = jnp.where(_diag_mask(t), sc, NEG_INF)
                m_old = m_sc[h]
                m_new = jnp.maximum(m_old, jnp.max(sc, axis=-1, keepdims=True))
                alpha = jnp.exp(m_old - m_new)
                p = jnp.exp(sc - m_new)
                l_sc[h] = alpha * l_sc[h] + jnp.sum(p, axis=-1, keepdims=True)
                m_sc[h] = m_new
                probs.append(p.astype(BF16))
                alphas.append(alpha)
            for h in range(nh):
                acc_sc[h] = alphas[h] * acc_sc[h] + jnp.dot(probs[h], v_ref[:, MLA_V * h:MLA_V * (h + 1)],
                                                            preferred_element_type=F32)

        @pl.when(j < i)
        def _():
            update(False)

        @pl.when(j == i)
        def _():
            update(True)
            for h in range(nh):
                out = acc_sc[h] * (1.0 / l_sc[h])
                o_ref[:, MLA_V * h:MLA_V * (h + 1)] = out
                yb_ref[:, MLA_V * h:MLA_V * (h + 1)] = out.astype(BF16)
                lse_ref[h] = m_sc[h] + jnp.log(l_sc[h])

    return _pcall(
        body, (q_cat, k_cat, v_b, y_all), name="mla_fwd", grid=(qi.shape[0],), prefetch=(qi, kj),
        in_specs=[pl.BlockSpec((nh, t, 256), lambda p, qi, kj: (0, qi[p], 0)),
                  pl.BlockSpec((nh, t, 256), lambda p, qi, kj: (0, kj[p], 0)),
                  pl.BlockSpec((t, nh * MLA_V), lambda p, qi, kj: (kj[p], 0)), ANY],
        out_specs=[pl.BlockSpec((t, nh * MLA_V), lambda p, qi, kj: (qi[p], 0)),
                   pl.BlockSpec((nh, t, 1), lambda p, qi, kj: (0, qi[p], 0)),
                   pl.BlockSpec((t, nh * MLA_V), lambda p, qi, kj: (qi[p], ycol))],
        out_shape=[jax.ShapeDtypeStruct((s, nh * MLA_V), F32), jax.ShapeDtypeStruct((nh, s, 1), F32),
                   jax.ShapeDtypeStruct(y_all.shape, y_all.dtype)],
        scratch_shapes=[pltpu.VMEM((nh, t, 1), F32), pltpu.VMEM((nh, t, 1), F32), pltpu.VMEM((nh, t, MLA_V), F32)],
        sem=("arbitrary",), comm=comm, io_alias={3: 2})


def _mla_bwd(q_cat, k_cat, v_b, y_b, lse, d_y, *, comm=()):
    nh, s, _ = q_cat.shape
    t = _tile(s, MLA_TILE)
    nt = s // t
    hp = 2
    wv = hp * MLA_V
    col0 = (SWA_Q_HEADS * SWA_HEAD_DIM) // wv
    qi, kj = _tile_pairs(nt, False)

    def body(qi_ref, kj_ref, q_ref, k_ref, v_ref, y_ref, lse_ref, dy_ref, dq_ref, dk_ref, dv_ref, dk_sc, dv_sc):
        step = pl.program_id(1)
        i, j = qi_ref[step], kj_ref[step]

        @pl.when(step == 0)
        def _():
            dq_ref[...] = jnp.zeros_like(dq_ref)

        @pl.when(i == j)
        def _():
            dk_sc[...] = jnp.zeros_like(dk_sc)
            dv_sc[...] = jnp.zeros_like(dv_sc)

        def update(diagonal):
            rows = pl.ds(pl.multiple_of(i * t, t), t)
            cols = [slice(MLA_V * h, MLA_V * (h + 1)) for h in range(hp)]
            do_b = [dy_ref[:, cols[h]].astype(BF16) for h in range(hp)]
            scores = [lax.dot_general(q_ref[h], k_ref[h], NT_DIMS, preferred_element_type=F32) for h in range(hp)]
            dps = [lax.dot_general(do_b[h], v_ref[:, cols[h]], NT_DIMS, preferred_element_type=F32) for h in range(hp)]
            p_b, ds_b = [], []
            for h in range(hp):
                p = jnp.exp(scores[h] * MLA_SCALE - lse_ref[h])
                if diagonal:
                    p = jnp.where(_diag_mask(t), p, 0.0)
                delta = jnp.sum(dy_ref[:, cols[h]] * y_ref[:, cols[h]], axis=-1, keepdims=True)
                p_b.append(p.astype(BF16))
                ds_b.append((p * (dps[h] - delta) * MLA_SCALE).astype(BF16))
            for h in range(hp):
                dv_sc[h] += lax.dot_general(p_b[h], do_b[h], TN_DIMS, preferred_element_type=F32)
                dk_sc[h] += lax.dot_general(ds_b[h], q_ref[h], TN_DIMS, preferred_element_type=F32)
                dq_ref[h, rows, :] += jnp.dot(ds_b[h], k_ref[h], preferred_element_type=F32)

        @pl.when(i > j)
        def _():
            update(False)

        @pl.when(i == j)
        def _():
            update(True)

        @pl.when(i == nt - 1)
        def _():
            dk_ref[...] = dk_sc[...]
            for h in range(hp):
                dv_ref[:, MLA_V * h:MLA_V * (h + 1)] = dv_sc[h]

    return _pcall(
        body, (q_cat, k_cat, v_b, y_b, lse, d_y), name="mla_bwd", grid=(nh // hp, qi.shape[0]), prefetch=(qi, kj),
        in_specs=[pl.BlockSpec((hp, t, 256), lambda g, p, qi, kj: (g, qi[p], 0)),
                  pl.BlockSpec((hp, t, 256), lambda g, p, qi, kj: (g, kj[p], 0)),
                  pl.BlockSpec((t, wv), lambda g, p, qi, kj: (kj[p], g)),
                  pl.BlockSpec((t, wv), lambda g, p, qi, kj: (qi[p], g)),
                  pl.BlockSpec((hp, t, 1), lambda g, p, qi, kj: (g, qi[p], 0)),
                  pl.BlockSpec((t, wv), lambda g, p, qi, kj: (qi[p], col0 + g))],
        out_specs=[pl.BlockSpec((hp, s, 256), lambda g, p, qi, kj: (g, 0, 0)),
                   pl.BlockSpec((hp, t, 256), lambda g, p, qi, kj: (g, kj[p], 0)),
                   pl.BlockSpec((t, wv), lambda g, p, qi, kj: (kj[p], g))],
        out_shape=[jax.ShapeDtypeStruct((nh, s, 256), F32), jax.ShapeDtypeStruct((nh, s, 256), F32),
                   jax.ShapeDtypeStruct((s, nh * MLA_V), F32)],
        scratch_shapes=[pltpu.VMEM((hp, t, 256), F32), pltpu.VMEM((hp, t, MLA_V), F32)],
        sem=("arbitrary", "arbitrary"), comm=comm)


MEM_SCALE = MEM_DIM ** -0.5


def _mem_kv_fwd(mem, g_mem, w_memkv, g_mk):
    m_len = mem.shape[0]

    def body(mem_ref, g_ref, w_ref, gk_ref, mn_ref, kv_ref, kn_ref, v_ref):
        mn, _, _ = _norm_full(mem_ref[...], g_ref[...])
        mn_b = mn.astype(BF16)
        mn_ref[...] = mn_b
        kv = jnp.dot(mn_b, w_ref[...], preferred_element_type=F32)
        kv_ref[...] = kv
        for h in range(MEM_HEADS):
            kn, _, _ = _norm_full(kv[:, 128 * h:128 * (h + 1)], gk_ref[...])
            kn_ref[:, 128 * h:128 * (h + 1)] = kn.astype(BF16)
        v_ref[...] = kv[:, 512:1024].astype(BF16)

    return pl.pallas_call(
        body, name="mem_kv_fwd",
        out_shape=[jax.ShapeDtypeStruct((m_len, D_MODEL), BF16), jax.ShapeDtypeStruct((m_len, 1024), F32),
                   jax.ShapeDtypeStruct((m_len, 512), BF16), jax.ShapeDtypeStruct((m_len, 512), BF16)],
        compiler_params=_params(),
    )(mem, g_mem, w_memkv, g_mk)


def _mem_kv_bwd(mem, g_mem, w_memkv, g_mk, mn_b, kv, d_kn, d_v):
    m_len = mem.shape[0]

    def body(mem_ref, g_ref, w_ref, gk_ref, mn_ref, kv_ref, dkn_ref, dv_ref, dw_ref, dgmem_ref, dgk_ref):
        parts = []
        dgk = jnp.zeros((1, LANES), F32)
        for h in range(MEM_HEADS):
            _, xhat, r = _norm_full(kv_ref[:, 128 * h:128 * (h + 1)], gk_ref[...])
            dx, dgh = _norm_full_bwd(dkn_ref[:, 128 * h:128 * (h + 1)], gk_ref[...], xhat, r)
            parts.append(dx)
            dgk = dgk + dgh
        d_kv = jnp.concatenate(parts + [dv_ref[...]], axis=1).astype(BF16)
        dw_ref[...] = lax.dot_general(mn_ref[...], d_kv, TN_DIMS, preferred_element_type=F32)
        d_mn = lax.dot_general(d_kv, w_ref[...], NT_DIMS, preferred_element_type=F32)
        _, xhat, _ = _norm_full(mem_ref[...], g_ref[...])
        dgmem_ref[...] = jnp.sum(d_mn * xhat, axis=0, keepdims=True)
        dgk_ref[...] = dgk

    return pl.pallas_call(
        body, name="mem_kv_bwd",
        out_shape=[jax.ShapeDtypeStruct((D_MODEL, 1024), F32), jax.ShapeDtypeStruct((1, D_MODEL), F32),
                   jax.ShapeDtypeStruct((1, LANES), F32)],
        compiler_params=_params(),
    )(mem, g_mem, w_memkv, g_mk, mn_b, kv, d_kn, d_v)


def _mem_probs(q_h, k_h):
    sc = lax.dot_general(q_h, k_h, NT_DIMS, preferred_element_type=F32) * MEM_SCALE
    e = jnp.exp(sc - jnp.max(sc, axis=-1, keepdims=True))
    return e * (1.0 / jnp.sum(e, axis=-1, keepdims=True))


def _mem_attn_fwd(qm, km, vm, y_all, *, tm=512):
    s = qm.shape[0]
    tm = _tile(s, tm)
    m_len = km.shape[0]
    ycol = (SWA_Q_HEADS * SWA_HEAD_DIM + MLA_HEADS * MLA_V) // 512

    def body(q_ref, k_ref, v_ref, _, o_ref, yb_ref):
        for h in range(MEM_HEADS):
            sl = slice(128 * h, 128 * (h + 1))
            p = _mem_probs(q_ref[:, sl], k_ref[:, sl])
            out = jnp.dot(p.astype(BF16), v_ref[:, sl], preferred_element_type=F32)
            o_ref[:, sl] = out
            yb_ref[:, sl] = out.astype(BF16)

    kvspec = pl.BlockSpec((m_len, 512), lambda i: (0, 0))
    return _pcall(
        body, (qm, km, vm, y_all), name="mem_attn_fwd", grid=(s // tm,),
        in_specs=[pl.BlockSpec((tm, 512), lambda i: (i, 0)), kvspec, kvspec, ANY],
        out_specs=[pl.BlockSpec((tm, 512), lambda i: (i, 0)), pl.BlockSpec((tm, 512), lambda i: (i, ycol))],
        out_shape=[jax.ShapeDtypeStruct((s, 512), F32), jax.ShapeDtypeStruct(y_all.shape, y_all.dtype)],
        sem=("parallel",), io_alias={3: 1})


def _mem_attn_bwd(qm, km, vm, y_m, d_y, *, tm=512):
    s = qm.shape[0]
    tm = _tile(s, tm)
    m_len = km.shape[0]
    col0 = (SWA_Q_HEADS * SWA_HEAD_DIM + MLA_HEADS * MLA_V) // 512

    def body(q_ref, k_ref, v_ref, y_ref, dy_ref, dq_ref, dk_ref, dv_ref):
        @pl.when(pl.program_id(0) == 0)
        def _():
            dk_ref[...] = jnp.zeros_like(dk_ref)
            dv_ref[...] = jnp.zeros_like(dv_ref)

        for h in range(MEM_HEADS):
            sl = slice(128 * h, 128 * (h + 1))
            q_h, k_h = q_ref[:, sl], k_ref[:, sl]
            do = dy_ref[:, sl]
            do_b = do.astype(BF16)
            p = _mem_probs(q_h, k_h)
            delta = jnp.sum(do * y_ref[:, sl], axis=-1, keepdims=True)
            dv_ref[:, sl] += lax.dot_general(p.astype(BF16), do_b, TN_DIMS, preferred_element_type=F32)
            dp = lax.dot_general(do_b, v_ref[:, sl], NT_DIMS, preferred_element_type=F32)
            ds_b = (p * (dp - delta) * MEM_SCALE).astype(BF16)
            dq_ref[:, sl] = jnp.dot(ds_b, k_h, preferred_element_type=F32)
            dk_ref[:, sl] += lax.dot_general(ds_b, q_h, TN_DIMS, preferred_element_type=F32)

    kvspec = pl.BlockSpec((m_len, 512), lambda i: (0, 0))
    row = pl.BlockSpec((tm, 512), lambda i: (i, 0))
    return pl.pallas_call(
        body, name="mem_attn_bwd", grid=(s // tm,),
        in_specs=[row, kvspec, kvspec, row, pl.BlockSpec((tm, 512), lambda i: (i, col0))],
        out_specs=[row, kvspec, kvspec],
        out_shape=[jax.ShapeDtypeStruct((s, 512), F32), jax.ShapeDtypeStruct((m_len, 512), F32),
                   jax.ShapeDtypeStruct((m_len, 512), F32)],
        compiler_params=_params(("arbitrary",)),
    )(qm, km, vm, y_m, d_y)


def _ffn_gate_up(fn, w_gate, w_up, *, tm=512, comm=()):
    s, d = fn.shape
    nsp, _, tf = w_gate.shape
    f = nsp * tf
    tm = _tile(s, tm)

    def body(x_ref, wg_ref, wu_ref, g_ref, u_ref, a_ref):
        x = x_ref[...]
        gate = jnp.dot(x, wg_ref[...], preferred_element_type=F32)
        up = jnp.dot(x, wu_ref[...], preferred_element_type=F32)
        g_ref[...] = gate.astype(BF16)
        u_ref[...] = up.astype(BF16)
        a_ref[...] = (gate * (1.0 / (1.0 + jnp.exp(-gate))) * up).astype(BF16)

    wspec = pl.BlockSpec((None, d, tf), lambda j, i: (j, 0, 0))
    ospec = pl.BlockSpec((tm, tf), lambda j, i: (i, j))
    osh = jax.ShapeDtypeStruct((s, f), BF16)
    return _pcall(body, (fn, w_gate, w_up), name="ffn_gate_up", grid=(nsp, s // tm),
                  in_specs=[pl.BlockSpec((tm, d), lambda j, i: (i, 0)), wspec, wspec],
                  out_specs=[ospec, ospec, ospec], out_shape=[osh, osh, osh], sem=("parallel", "parallel"), comm=comm)


def _ffn_bwd_act(d_out, w_down, gate, up, *, tm=512, tf=1408, comm=()):
    s, d = d_out.shape
    f = w_down.shape[0]
    tm, tf = _tile(s, tm), _tile(f, tf)

    chunks = [(c0, min(256, tf - c0)) for c0 in range(0, tf, 256)]

    def body(do_ref, wd_ref, g_ref, u_ref, dg_ref, du_ref):
        do = do_ref[...].astype(BF16)
        parts = [lax.dot_general(do, wd_ref[c0:c0 + cw, :], NT_DIMS, preferred_element_type=F32) for c0, cw in chunks]
        for (c0, cw), d_act in zip(chunks, parts):
            cols = slice(c0, c0 + cw)
            gate = g_ref[:, cols].astype(F32)
            sig = 1.0 / (1.0 + jnp.exp(-gate))
            du_ref[:, cols] = (d_act * (gate * sig)).astype(BF16)
            dg_ref[:, cols] = (d_act * u_ref[:, cols].astype(F32) * (sig * (1.0 + gate * (1.0 - sig)))).astype(BF16)

    ospec = pl.BlockSpec((tm, tf), lambda j, i: (i, j))
    osh = jax.ShapeDtypeStruct((s, f), BF16)
    return _pcall(
        body, (d_out, w_down, gate, up), name="ffn_bwd_act", grid=(f // tf, s // tm),
        in_specs=[pl.BlockSpec((tm, d), lambda j, i: (i, 0)), pl.BlockSpec((tf, d), lambda j, i: (j, 0)), ospec, ospec],
        out_specs=[ospec, ospec], out_shape=[osh, osh], sem=("parallel", "parallel"), comm=comm)


def _loss_head(out, target, *, tm=512):
    s, d = out.shape
    tm = _tile(s, tm)

    def body(o_ref, t_ref, d_ref, db_ref, l_ref):
        err = o_ref[...] - t_ref[...]
        d_out = err * (1.0 / d)
        d_ref[...] = d_out
        db_ref[...] = d_out.astype(BF16)
        part = 0.5 * jnp.sum(jnp.mean(err * err, axis=-1, keepdims=True), axis=0, keepdims=True)
        part = jnp.broadcast_to(part, (1, LANES))

        @pl.when(pl.program_id(0) == 0)
        def _():
            l_ref[...] = part

        @pl.when(pl.program_id(0) > 0)
        def _():
            l_ref[...] += part

    row = pl.BlockSpec((tm, d), lambda i: (i, 0))
    return pl.pallas_call(
        body, name="loss_head", grid=(s // tm,),
        in_specs=[row, row], out_specs=[row, row, pl.BlockSpec((1, LANES), lambda i: (0, 0))],
        out_shape=[jax.ShapeDtypeStruct((s, d), F32), jax.ShapeDtypeStruct((s, d), BF16),
                   jax.ShapeDtypeStruct((1, LANES), F32)],
        compiler_params=_params(("arbitrary",)),
    )(out, target)


def _cols(g4):
    return jnp.concatenate([g4[k] for k in range(N_CHIPS)], axis=1)


def _full_w_in(g4):
    per = IN_WIDTH // N_CHIPS
    kr0 = 2304 - (N_CHIPS - 1) * per
    last = g4[N_CHIPS - 1]
    pad = jnp.zeros((last.shape[0], IN_PAD - IN_WIDTH), last.dtype)
    return jnp.concatenate([g4[0], g4[1], g4[2], last[:, :kr0], last[:, kr0 + 64:], last[:, kr0:kr0 + 64], pad], axis=1)


def _shards_w_in(dwp):
    per = IN_WIDTH // N_CHIPS
    kr0 = 2304 - (N_CHIPS - 1) * per
    last = jnp.concatenate([dwp[:, (N_CHIPS - 1) * per:2304], dwp[:, C_KR:C_KR + 64], dwp[:, 2304:C_KR]], axis=1)
    assert last.shape[1] == per and kr0 == 144
    return jnp.stack([dwp[:, per * k:per * (k + 1)] for k in range(N_CHIPS - 1)] + [last])


def _full_heads(g4, first):
    return jnp.concatenate([g4[k][:, :first] for k in range(N_CHIPS)] + [g4[k][:, first:] for k in range(N_CHIPS)], axis=1)


def _shards_heads(dwp, first, rest):
    base = N_CHIPS * first
    return jnp.stack([jnp.concatenate([dwp[:, first * k:first * (k + 1)], dwp[:, base + rest * k:base + rest * (k + 1)]], axis=1)
                      for k in range(N_CHIPS)])


def _rope_tables(pos):
    inv_freq = ROPE_THETA ** (-jnp.arange(0, MLA_ROPE, 2, dtype=F32) / MLA_ROPE)
    ang = pos.astype(F32)[:, None] * inv_freq
    cos, sin = jnp.cos(ang), jnp.sin(ang)
    return jnp.tile(cos, (1, 4)), jnp.concatenate([-sin, sin, -sin, sin], axis=1)


def _gain_table(sp):
    two = lambda v: jnp.tile(v, (1, 2))
    rows = [two(sp["swa_q_norm_g"]), two(sp["swa_k_norm_g"]), sp["mla_qn_norm_g"], two(sp["mla_qr_norm_g"]),
            sp["mla_kn_norm_g"], two(sp["mla_kr_norm_g"]), sp["mem_q_norm_g"], jnp.zeros((1, LANES), F32)]
    return jnp.concatenate(rows, axis=0)


CHIP_DISTANCES = (1, 2, 3)


def _place():
    x, y, c = lax.axis_index("x"), lax.axis_index("y"), lax.axis_index("c")
    return x, y, c, 2 * x + y


def _chip_at(x, y, d):
    px = 1 - x if d & 2 else x
    py = 1 - y if d & 1 else y
    return px, py, 2 * px + py


def _row_tile(rows, want=512, mult=8):
    t = min(rows, want)
    t -= t % mult
    while rows % t:
        t -= mult
    return t


def _cast_into_slot(w, meta, *, name, comm=()):
    rows, cols = w.shape
    tr = _row_tile(rows, 512, 16)

    def body(meta_ref, w_ref, o_ref):
        o_ref[...] = w_ref[...].astype(BF16)

    return _pcall(body, (w,), name=name, grid=(rows // tr,), prefetch=(meta,),
                  in_specs=[pl.BlockSpec((tr, cols), lambda i, m: (i, 0))],
                  out_specs=pl.BlockSpec((None, tr, cols), lambda i, m: (m[0], i, 0)),
                  out_shape=jax.ShapeDtypeStruct((N_CHIPS, rows, cols), BF16), sem=("parallel",), comm=comm)


def _remote(src, dst, ssem, rsem, i, device):
    return pltpu.make_async_remote_copy(src_ref=src, dst_ref=dst, send_sem=ssem.at[i], recv_sem=rsem.at[i],
                                        device_id=device, device_id_type=MESH)


def _symmetric_stage(ins, out_shapes, aliases, n_sem, copies):
    def issue(i_refs, o_refs, ssem, rsem):
        for send, _ in copies(i_refs, o_refs, ssem, rsem):
            send.start()

    def wait(i_refs, o_refs, ssem, rsem):
        pairs = copies(i_refs, o_refs, ssem, rsem)
        for _, arrival in pairs:
            arrival.wait_recv()
        for send, _ in pairs:
            send.wait_send()

    return _Stage(ins, out_shapes, aliases, n_sem, issue, wait)


def _gather_stage(slots, leg, part=(0, 1)):
    n = len(slots)
    shapes = [jax.ShapeDtypeStruct(s.shape, s.dtype) for s in slots]
    in_place = {w: w for w in range(n)}
    if not isinstance(leg, str):
        legs = list(leg)

        def copies(i_refs, o_refs, ssem, rsem):
            return [pr for k, (which, prt) in enumerate(legs)
                    for pr in _gather_stage(slots, which, prt).leg_copies(which, 3 * n * k)(i_refs, o_refs, ssem, rsem)]

        return _symmetric_stage(slots, shapes, in_place, 3 * n * len(legs), copies)

    def leg_copies(which, base):
        def copies(_, outs, ssem, rsem):
            x, y, c, k_me = _place()
            pairs = []
            for w in range(n):
                half = outs[w].shape[1] // 2
                r0, size = _window(half, part)
                slab = lambda k, cc, w=w, half=half, r0=r0, size=size: outs[w].at[k, pl.ds(cc * half + r0, size)]
                for d in CHIP_DISTANCES:
                    px, py, k_src = _chip_at(x, y, d)
                    i = base + 3 * w + d - 1
                    if which == "ici":
                        pairs.append((_remote(slab(k_me, c), slab(k_me, c), ssem, rsem, i, (px, py, c)),
                                      _remote(slab(k_src, c), slab(k_src, c), ssem, rsem, i, (x, y, c))))
                    else:
                        pairs.append((_remote(slab(k_src, c), slab(k_src, c), ssem, rsem, i, (x, y, 1 - c)),
                                      _remote(slab(k_src, 1 - c), slab(k_src, 1 - c), ssem, rsem, i, (x, y, c))))
            return pairs
        return copies

    if leg != "both":
        st = _symmetric_stage(slots, shapes, in_place, 3 * n, leg_copies(leg, 0))
        st.leg_copies = leg_copies
        return st
    ici = _symmetric_stage(slots, shapes, in_place, 6 * n, leg_copies("ici", 0))
    d2d = _symmetric_stage(slots, shapes, in_place, 6 * n, leg_copies("d2d", 3 * n))

    def mid(*refs):
        ici.wait(*refs)
        d2d.issue(*refs)

    return _Stage(slots, shapes, in_place, 6 * n, ici.issue, d2d.wait, mid)


def _halves_stage(grads):
    n = len(grads)

    def copies(ins, outs, ssem, rsem):
        x, y, c, _ = _place()
        pairs = []
        for w in range(n):
            half = ins[w].shape[1] // 2
            pairs.append((_remote(ins[w].at[:, pl.ds((1 - c) * half, half)], outs[w], ssem, rsem, w, (x, y, 1 - c)),
                          _remote(outs[w], outs[w], ssem, rsem, w, (x, y, c))))
        return pairs

    shapes = [jax.ShapeDtypeStruct((N_CHIPS, g.shape[1] // 2, g.shape[2]), g.dtype) for g in grads]
    return _symmetric_stage(grads, shapes, {}, n, copies)


def _window(rows, part):
    idx, count = part
    size = rows // count
    assert size * count == rows and size % 16 == 0, (rows, part)
    return idx * size, size


def _chips_stage(parts, part=(0, 1), into=None):
    n = len(parts)

    def copies(ins, outs, ssem, rsem):
        x, y, c, _ = _place()
        pairs = []
        for w in range(n):
            r0, size = _window(ins[w].shape[1], part)
            for d in CHIP_DISTANCES:
                px, py, _ = _chip_at(x, y, d)
                i = 3 * w + d - 1
                land = outs[w].at[d - 1, pl.ds(r0, size)]
                pairs.append((_remote(ins[w].at[d - 1, pl.ds(r0, size)], land, ssem, rsem, i, (px, py, c)),
                              _remote(land, land, ssem, rsem, i, (x, y, c))))
        return pairs

    shapes = [jax.ShapeDtypeStruct(p.shape, p.dtype) for p in parts]
    if into is None:
        return _symmetric_stage(parts, shapes, {}, 3 * n, copies)
    return _symmetric_stage(list(parts) + list(into), shapes, {n + w: w for w in range(n)}, 3 * n, copies)


def _swap_stage(totals):
    n = len(totals)

    def copies(ins, outs, ssem, rsem):
        x, y, c, _ = _place()
        return [(_remote(ins[w], outs[w], ssem, rsem, w, (x, y, 1 - c)),
                 _remote(outs[w], outs[w], ssem, rsem, w, (x, y, c))) for w in range(n)]

    shapes = [jax.ShapeDtypeStruct(t.shape, t.dtype) for t in totals]
    return _symmetric_stage(totals, shapes, {}, n, copies)


def _run_stages(stages, *, name):
    n_ins = [len(st.ins) for st in stages]
    n_outs = [len(st.out_shapes) for st in stages]
    tot_in, tot_out = sum(n_ins), sum(n_outs)
    aliases, i0, o0 = {}, 0, 0
    for st, ni, no in zip(stages, n_ins, n_outs):
        aliases.update({i0 + a: o0 + b for a, b in st.aliases.items()})
        i0, o0 = i0 + ni, o0 + no

    def body(*refs):
        sems = refs[tot_in + tot_out:]
        for what in ("issue", "wait"):
            i0, o0 = 0, tot_in
            for k, (st, ni, no) in enumerate(zip(stages, n_ins, n_outs)):
                getattr(st, what)(refs[i0:i0 + ni], refs[o0:o0 + no], sems[2 * k], sems[2 * k + 1])
                i0, o0 = i0 + ni, o0 + no

    sem = pltpu.SemaphoreType.DMA
    res = pl.pallas_call(
        body, name=name, in_specs=[ANY] * tot_in, out_specs=[ANY] * tot_out,
        out_shape=[s for st in stages for s in st.out_shapes], input_output_aliases=aliases,
        scratch_shapes=[sem((st.n_sem,)) for st in stages for _ in range(2)],
    )(*[a for st in stages for a in st.ins])
    outs, o0 = [], 0
    for no in n_outs:
        outs.append(list(res[o0:o0 + no]))
        o0 += no
    return outs


def _add_pair(meta, g4, recv, *, name):
    nsh, rows, cols = g4.shape
    half = rows // 2
    tr = _row_tile(half, 128 if cols > 1024 else 256, 16)
    nt = half // tr

    def body(meta_ref, g0, g1, g2, g3, r0, r1, r2, r3, own_ref, oth_ref):
        own_ref[...] = g0[...] + r0[...]
        for d, (g, r) in enumerate(((g1, r1), (g2, r2), (g3, r3))):
            oth_ref[d] = (g[...] + r[...]).astype(BF16)

    blk = (None, tr, cols)
    gspec = lambda d: pl.BlockSpec(blk, lambda i, m: (jnp.bitwise_xor(m[0], d), m[1] * nt + i, 0))
    rspec = lambda d: pl.BlockSpec(blk, lambda i, m: (jnp.bitwise_xor(m[0], d), i, 0))
    grid_spec = pltpu.PrefetchScalarGridSpec(
        num_scalar_prefetch=1, grid=(nt,),
        in_specs=[gspec(d) for d in range(nsh)] + [rspec(d) for d in range(nsh)],
        out_specs=[pl.BlockSpec((tr, cols), lambda i, m: (i, 0)), pl.BlockSpec((3, tr, cols), lambda i, m: (0, i, 0))])
    return pl.pallas_call(
        body, name=name, grid_spec=grid_spec,
        out_shape=[jax.ShapeDtypeStruct((half, cols), F32), jax.ShapeDtypeStruct((3, half, cols), BF16)],
        compiler_params=_params(("parallel",)),
    )(meta, g4, g4, g4, g4, recv, recv, recv, recv)


def _add_chips(own, recv, *, name):
    half, cols = own.shape
    tr = _row_tile(half, 256, 16)

    def body(p_ref, r_ref, o_ref):
        o_ref[...] = ((p_ref[...] + r_ref[0].astype(F32)) + r_ref[1].astype(F32)) + r_ref[2].astype(F32)

    return pl.pallas_call(
        body, name=name, grid=(half // tr,),
        in_specs=[pl.BlockSpec((tr, cols), lambda i: (i, 0)), pl.BlockSpec((3, tr, cols), lambda i: (0, i, 0))],
        out_specs=pl.BlockSpec((tr, cols), lambda i: (i, 0)),
        out_shape=jax.ShapeDtypeStruct((half, cols), F32),
        compiler_params=_params(("parallel",)),
    )(own, recv)


def _adamw_math(w, g, m, v):
    m = ADAM_B1 * m + (1.0 - ADAM_B1) * g
    v = ADAM_B2 * v + (1.0 - ADAM_B2) * (g * g)
    m_hat = m / (1.0 - ADAM_B1 ** ADAM_STEP)
    v_hat = v / (1.0 - ADAM_B2 ** ADAM_STEP)
    delta = -ADAM_LR * (m_hat / (jnp.sqrt(v_hat) + ADAM_EPS) + ADAM_WD * w)
    return delta, m, v


def _adamw(meta, w, g_mine, g_theirs, m, v, *, name):
    rows, cols = w.shape
    half = rows // 2
    tr = _row_tile(half, 256)
    nt = half // tr

    def body(meta_ref, w_ref, a_ref, b_ref, m_ref, v_ref, g_ref, d_ref, mo_ref, vo_ref):
        is_mine = (pl.program_id(0) // nt) == meta_ref[1]
        g = jnp.where(is_mine, a_ref[...], b_ref[...])
        g_ref[...] = g
        d_ref[...], mo_ref[...], vo_ref[...] = _adamw_math(w_ref[...], g, m_ref[...], v_ref[...])

    blk = pl.BlockSpec((tr, cols), lambda i, mt: (i, 0))
    mine = pl.BlockSpec((tr, cols), lambda i, mt: (jnp.where(i // nt == mt[1], i % nt, 0), 0))
    theirs = pl.BlockSpec((tr, cols), lambda i, mt: (jnp.where(i // nt == mt[1], 0, i % nt), 0))
    sh = jax.ShapeDtypeStruct((rows, cols), F32)
    grid_spec = pltpu.PrefetchScalarGridSpec(
        num_scalar_prefetch=1, grid=(rows // tr,),
        in_specs=[blk, mine, theirs, blk, blk], out_specs=[blk] * 4)
    return pl.pallas_call(
        body, name=name, grid_spec=grid_spec, out_shape=[sh] * 4,
        compiler_params=_params(("arbitrary",)),
    )(meta, w, g_mine, g_theirs, m, v)


N_DEVICES = 8


def _small_step(g_pack, w_pack, m_pack, v_pack):
    rows = g_pack.shape[0]

    def body(g_ref, w_ref, m_ref, v_ref, sum_ref, d_ref, mo_ref, vo_ref, slots, ssem, rsem):
        x, y, c, _ = _place()
        me = 4 * x + 2 * y + c
        slots[me] = g_ref[...]
        copies = []
        for r in range(1, N_DEVICES):
            px = 1 - x if r & 4 else x
            py = 1 - y if r & 2 else y
            pc = 1 - c if r & 1 else c
            copies.append(pltpu.make_async_remote_copy(
                src_ref=g_ref, dst_ref=slots.at[me], send_sem=ssem.at[r - 1], recv_sem=rsem.at[r - 1],
                device_id=(px, py, pc), device_id_type=MESH))
        for cp in copies:
            cp.start()
        for r in range(1, N_DEVICES):
            src = jnp.bitwise_xor(me, r)
            pltpu.make_async_remote_copy(
                src_ref=g_ref, dst_ref=slots.at[src], send_sem=ssem.at[r - 1], recv_sem=rsem.at[r - 1],
                device_id=(x, y, c), device_id_type=MESH).wait_recv()
        for cp in copies:
            cp.wait_send()
        total = slots[0]
        for k in range(1, N_DEVICES):
            total = total + slots[k]
        sum_ref[...] = total
        d_ref[...], mo_ref[...], vo_ref[...] = _adamw_math(w_ref[...], total, m_ref[...], v_ref[...])

    sh = jax.ShapeDtypeStruct((rows, LANES), F32)
    vm = pl.BlockSpec(memory_space=pltpu.VMEM)
    return pl.pallas_call(
        body, name="small_allreduce_adamw",
        in_specs=[vm] * 4, out_specs=[vm] * 4, out_shape=[sh] * 4,
        scratch_shapes=[pltpu.VMEM((N_DEVICES, rows, LANES), F32),
                        pltpu.SemaphoreType.DMA((N_DEVICES - 1,)), pltpu.SemaphoreType.DMA((N_DEVICES - 1,))],
    )(g_pack, w_pack, m_pack, v_pack)


WEIGHTS = ("attn_norm_g", "w_in", "swa_q_norm_g", "swa_k_norm_g", "swa_sinks", "mla_cq_norm_g", "mla_ckv_norm_g",
           "w_uq", "w_ukv", "mla_qn_norm_g", "mla_qr_norm_g", "mla_kn_norm_g", "mla_kr_norm_g", "mem_norm_g",
           "w_mem_kv", "mem_q_norm_g", "mem_k_norm_g", "w_out", "ffn_norm_g", "w_gate", "w_up", "w_down")
BIG = ("w_in", "w_uq", "w_ukv", "w_mem_kv", "w_out", "w_gate", "w_up", "w_down")
SMALL = tuple(n for n in WEIGHTS if n not in BIG)
PACK_UNIT = 8 * LANES


def _pack(parts):
    out = []
    for p in parts:
        n = p.shape[1]
        padded = -(-n // PACK_UNIT) * PACK_UNIT
        out.append(jnp.pad(p, ((0, 0), (0, padded - n))).reshape(padded // LANES, LANES))
    return jnp.concatenate(out, axis=0)


def _unpack(buf, sizes):
    out, row = [], 0
    for n in sizes:
        rows = -(-n // PACK_UNIT) * 8
        out.append(buf[row:row + rows].reshape(1, rows * LANES)[:, :n])
        row += rows
    return out


def kernel(x, mem, positions, attn_norm_g, w_in, swa_q_norm_g, swa_k_norm_g, swa_sinks, mla_cq_norm_g, mla_ckv_norm_g, w_uq, w_ukv, mla_qn_norm_g, mla_qr_norm_g, mla_kn_norm_g, mla_kr_norm_g, mem_norm_g, w_mem_kv, mem_q_norm_g, mem_k_norm_g, w_out, ffn_norm_g, w_gate, w_up, w_down, loss_target, m_attn_norm_g, m_w_in, m_swa_q_norm_g, m_swa_k_norm_g, m_swa_sinks, m_mla_cq_norm_g, m_mla_ckv_norm_g, m_w_uq, m_w_ukv, m_mla_qn_norm_g, m_mla_qr_norm_g, m_mla_kn_norm_g, m_mla_kr_norm_g, m_mem_norm_g, m_w_mem_kv, m_mem_q_norm_g, m_mem_k_norm_g, m_w_out, m_ffn_norm_g, m_w_gate, m_w_up, m_w_down, v_attn_norm_g, v_w_in, v_swa_q_norm_g, v_swa_k_norm_g, v_swa_sinks, v_mla_cq_norm_g, v_mla_ckv_norm_g, v_w_uq, v_w_ukv, v_mla_qn_norm_g, v_mla_qr_norm_g, v_mla_kn_norm_g, v_mla_kr_norm_g, v_mem_norm_g, v_w_mem_kv, v_mem_q_norm_g, v_mem_k_norm_g, v_w_out, v_ffn_norm_g, v_w_gate, v_w_up, v_w_down):
    given = dict(locals())
    wts = {n: given[n] for n in WEIGHTS}
    mom_m = {n: given["m_" + n] for n in WEIGHTS}
    mom_v = {n: given["v_" + n] for n in WEIGHTS}

    mx, my, mc = lax.axis_index("x"), lax.axis_index("y"), lax.axis_index("c")
    meta = jnp.stack([2 * mx + my, mc]).astype(jnp.int32)
    x, mem, pos, target = x[0], mem[0], positions[0], loss_target[0]
    sp = {n: wts[n] for n in SMALL}
    s = x.shape[0]
    cos_t, sin_t = _rope_tables(pos)
    pos_f = pos.astype(F32)
    pos_col, pos_row = pos_f.reshape(s, 1), pos_f.reshape(1, s)
    g128 = _gain_table(sp)
    sinks = sp["swa_sinks"].reshape(SWA_Q_HEADS)
    gcq, gckv = sp["mla_cq_norm_g"], sp["mla_ckv_norm_g"]
    gs = {}

    slot = {n: _cast_into_slot(wts[n][0], meta, name="cast_" + n) for n in BIG if n not in ("w_gate", "w_up", "w_down")}
    first = [slot["w_in"], slot["w_uq"], slot["w_ukv"]]
    slot["w_gate"], [first] = _cast_into_slot(wts["w_gate"][0], meta, name="cast_w_gate",
                                              comm=[_gather_stage(first, "ici", (0, 4))])
    slot["w_up"], [first] = _cast_into_slot(wts["w_up"][0], meta, name="cast_w_up",
                                            comm=[_gather_stage(first, [("ici", (1, 4)), ("d2d", (0, 4))])])
    slot["w_down"], [first] = _cast_into_slot(wts["w_down"][0], meta, name="cast_w_down",
                                              comm=[_gather_stage(first, [("ici", (2, 4)), ("d2d", (1, 4))])])
    hn, [first] = _rms_fwd(x, sp["attn_norm_g"], name="attn_norm_fwd",
                           comm=[_gather_stage(first, [("ici", (3, 4)), ("d2d", (2, 4))])])
    [first] = _run_stages([_gather_stage(first, "d2d", (3, 4))], name="gather_first_last_d2d")
    w_in_f, w_uq_f, w_ukv_f = _full_w_in(first[0]), _full_heads(first[1], MLA_NOPE), _full_heads(first[2], MLA_NOPE)

    proj, [mid] = _matmul(hn, w_in_f, name="in_proj",
                          comm=[_gather_stage([slot["w_mem_kv"], slot["w_out"]], "ici")])
    (qa, ka, va, q_cat, k_cat, v_b, qm), [mid, wg] = _attn_prep_fwd(
        proj, g128, gcq, gckv, w_uq_f, w_ukv_f, cos_t, sin_t,
        comm=[_gather_stage(mid, "d2d"), _gather_stage([slot["w_gate"]], "ici", (0, 4))])
    w_mem_kv_f = mid[0].reshape(D_MODEL, 2 * MEM_HEADS * MEM_DIM)
    w_out_f = mid[1].reshape(D_MODEL, D_MODEL)
    mn_b, kv_m, km, vm = _mem_kv_fwd(mem, sp["mem_norm_g"], w_mem_kv_f, sp["mem_k_norm_g"])
    (y_a, y), [wg] = _swa_fwd(qa, ka, va, pos_col, pos_row, sinks,
                              comm=[_gather_stage(wg, [("ici", (k, 4)) for k in (1, 2, 3)])])
    (y_b, lse, y), [wu, wg] = _mla_fwd(
        q_cat, k_cat, v_b, y, comm=[_gather_stage([slot["w_up"]], "ici"), _gather_stage(wg, "d2d")])
    y_m, y = _mem_attn_fwd(qm, km, vm, y)
    h1, [wu] = _matmul(y, w_out_f, add=x, name="out_proj", comm=[_gather_stage(wu, "d2d")])
    w_gate_f, w_up_f = wg[0], wu[0]
    fn = _rms_fwd(h1, sp["ffn_norm_g"], name="ffn_norm_fwd")
    (gate, up, act), [wd] = _ffn_gate_up(fn, w_gate_f, w_up_f, comm=[_gather_stage([slot["w_down"]], "both")])
    w_down_f = wd[0].reshape(D_FF, D_MODEL)
    out = _matmul(act, w_down_f, add=h1, name="down_proj", tm=512, tk=D_FF)
    d_out, d_out_b, loss_tile = _loss_head(out, target)

    add_pair = lambda n, g4, r: _add_pair(meta, g4, r, name="grad_add_pair_" + n)
    add_chips = lambda n, own, r: _add_chips(own, r, name="grad_add_chips_" + n)
    mine, theirs = {}, {}

    dw_down = _matmul(act, d_out_b, ta=True, name="dw_down", tm=1408, tn=1024, tk=2048)
    dw_down = dw_down.reshape(N_CHIPS, D_FF // N_CHIPS, D_MODEL)
    (d_gate, d_up), [[r]] = _ffn_bwd_act(d_out_b, w_down_f, gate, up, comm=[_halves_stage([dw_down])])
    own_d, oth_d = add_pair("w_down", dw_down, r)
    dw_gate, [rd] = _matmul(fn, d_gate, ta=True, name="dw_gate", tk=2048, tn=D_FF // N_CHIPS, out_split=N_CHIPS,
                            comm=[_chips_stage([oth_d], (0, 2))])
    dw_up, [[r], rd] = _matmul(fn, d_up, ta=True, name="dw_up", tk=2048, tn=D_FF // N_CHIPS, out_split=N_CHIPS,
                               comm=[_halves_stage([dw_gate]), _chips_stage([oth_d], (1, 2), into=rd)])
    mine["w_down"] = add_chips("w_down", own_d, rd[0])
    own_g, oth_g = add_pair("w_gate", dw_gate, r)
    d_fn, [rg, [theirs["w_down"]]] = _matmul(
        d_gate, w_gate_f, tb=True, b_split=True, name="dfn_gate", tm=512,
        comm=[_chips_stage([oth_g], (0, 2)), _swap_stage([mine["w_down"]])])
    d_fn, [[r], rg] = _matmul(d_up, w_up_f, tb=True, b_split=True, add=d_fn, name="dfn_up", tm=512,
                              comm=[_halves_stage([dw_up]), _chips_stage([oth_g], (1, 2), into=rg)])
    mine["w_gate"] = add_chips("w_gate", own_g, rg[0])
    own_u, oth_u = add_pair("w_up", dw_up, r)
    d_h1, d_h1_b, gs["ffn_norm_g"] = _rms_bwd(d_fn, h1, sp["ffn_norm_g"], d_out, name="ffn_norm_bwd")
    dw_out, [[theirs["w_gate"]]] = _matmul(y, d_h1_b, ta=True, name="dw_out", tk=2048,
                                           comm=[_swap_stage([mine["w_gate"]])])
    dw_out = dw_out.reshape(N_CHIPS, D_MODEL // N_CHIPS, D_MODEL)
    d_y, [[r]] = _matmul(d_h1_b, w_out_f, tb=True, name="dy", comm=[_halves_stage([dw_out])])
    own_o, oth_o = add_pair("w_out", dw_out, r)
    (d_qa, d_ka, d_va, d_sink), [ru] = _swa_bwd(qa, ka, va, pos_col, pos_row, sinks, y_a, d_y,
                                                comm=[_chips_stage([oth_u], (0, 2))])
    (d_qcat, d_kcat, d_vb), [ru, [r]] = _mla_bwd(
        q_cat, k_cat, v_b, y_b, lse, d_y, comm=[_chips_stage([oth_u], (1, 2), into=ru), _chips_stage([oth_o])])
    mine["w_up"] = add_chips("w_up", own_u, ru[0])
    mine["w_out"] = add_chips("w_out", own_o, r)
    d_qm, d_km, d_vm = _mem_attn_bwd(qm, km, vm, y_m, d_y)
    (d_proj, dw_uq, dw_ukv, dg128, gs["mla_cq_norm_g"], gs["mla_ckv_norm_g"]), [[theirs["w_up"], theirs["w_out"]]] = \
        _attn_prep_bwd(proj, g128, gcq, gckv, w_uq_f, w_ukv_f, cos_t, sin_t, d_qa, d_ka, d_va, d_qcat, d_kcat, d_vb,
                       d_qm, comm=[_swap_stage([mine["w_up"], mine["w_out"]])])
    dw_mem_kv, gs["mem_norm_g"], gs["mem_k_norm_g"] = _mem_kv_bwd(
        mem, sp["mem_norm_g"], w_mem_kv_f, sp["mem_k_norm_g"], mn_b, kv_m, d_km, d_vm)
    late = ("w_uq", "w_ukv", "w_mem_kv")
    late_g = [_shards_heads(dw_uq, MLA_NOPE, MLA_ROPE), _shards_heads(dw_ukv, MLA_NOPE, MLA_V),
              dw_mem_kv.reshape(N_CHIPS, D_MODEL // N_CHIPS, -1)]
    dw_in, [rs] = _matmul(hn, d_proj, ta=True, name="dw_in", tk=2048, comm=[_halves_stage(late_g)])
    late_sums = [add_pair(n, g4, r) for n, g4, r in zip(late, late_g, rs)]
    dw_in = _shards_w_in(dw_in)
    d_hn, [rs, [r]] = _matmul(d_proj, w_in_f, tb=True, name="dhn", tk=1536,
                              comm=[_chips_stage([oth for _, oth in late_sums]), _halves_stage([dw_in])])
    for n, (own, _), r_n in zip(late, late_sums, rs):
        mine[n] = add_chips(n, own, r_n)
    own_i, oth_i = add_pair("w_in", dw_in, r)
    (grad_x, _, gs["attn_norm_g"]), [[r], late_theirs] = _rms_bwd(
        d_hn, x, sp["attn_norm_g"], d_h1, name="attn_norm_bwd",
        comm=[_chips_stage([oth_i]), _swap_stage([mine[n] for n in late])])
    theirs.update(zip(late, late_theirs))
    mine["w_in"] = add_chips("w_in", own_i, r)
    [[theirs["w_in"]]] = _run_stages([_swap_stage([mine["w_in"]])], name="grad_swap_w_in")

    fold = lambda r: r[:, :64] + r[:, 64:]
    gs["swa_q_norm_g"] = fold(dg128[G_SWA_Q:G_SWA_Q + 1])
    gs["swa_k_norm_g"] = fold(dg128[G_SWA_K:G_SWA_K + 1])
    gs["mla_qn_norm_g"] = dg128[G_QN:G_QN + 1]
    gs["mla_qr_norm_g"] = fold(dg128[G_QR:G_QR + 1])
    gs["mla_kn_norm_g"] = dg128[G_KN:G_KN + 1]
    gs["mla_kr_norm_g"] = fold(dg128[G_KR:G_KR + 1])
    gs["mem_q_norm_g"] = dg128[G_MQ:G_MQ + 1]
    gs["swa_sinks"] = d_sink[:, :SWA_Q_HEADS]

    grad, delta, new_m, new_v = {}, {}, {}, {}
    for n in BIG:
        g2, d, m2, v2 = _adamw(meta, wts[n][0], mine[n], theirs[n], mom_m[n][0], mom_v[n][0], name="adamw_" + n)
        grad[n], delta[n], new_m[n], new_v[n] = g2[None], d[None], m2[None], v2[None]

    sizes = [wts[n].shape[1] for n in SMALL]
    zero = jnp.zeros((1, LANES), F32)
    packs = _small_step(_pack([gs[n] for n in SMALL] + [loss_tile]), _pack([wts[n] for n in SMALL] + [zero]),
                        _pack([mom_m[n] for n in SMALL] + [zero]), _pack([mom_v[n] for n in SMALL] + [zero]))
    for store, buf in zip((grad, delta, new_m, new_v), packs):
        for n, val in zip(SMALL, _unpack(buf, sizes)):
            store[n] = val
    loss = _unpack(packs[0], sizes + [LANES])[-1][0, 0]

    return (loss, grad_x[None], *[grad[n] for n in WEIGHTS], *[delta[n] for n in WEIGHTS],
            *[new_m[n] for n in WEIGHTS], *[new_v[n] for n in WEIGHTS])
```

```python
import functools
import math

import jax
import jax.numpy as jnp
from jax import lax
from jax.experimental import pallas as pl
from jax.experimental.pallas import tpu as pltpu

F32 = jnp.float32
BF16 = jnp.bfloat16

D_MODEL = 2048
BLOCK = 128
EPS = 1e-6
NEG_INF = -1e30
SWA_Q_HEADS = 16
SWA_KV_HEADS = 2
SWA_HEAD_DIM = 64
MLA_HEADS = 4
MLA_RANK = 512
MLA_NOPE = 128
MLA_ROPE = 64
MLA_V = 128
ROPE_THETA = 10000.0
MEM_HEADS = 4
MEM_DIM = 128
D_FF = 5632
IN_WIDTH = 2880
IN_PAD = 3072
N_CHIPS = 4

ADAM_LR = 0.001
ADAM_B1 = 0.9
ADAM_B2 = 0.999
ADAM_EPS = 1e-08
ADAM_WD = 0.01
ADAM_STEP = 10

VMEM_LIMIT_BYTES = 56 * 1024 * 1024
LANES = 128

MESH = pl.DeviceIdType.MESH


def _params(sem=None, **kw):
    return pltpu.CompilerParams(dimension_semantics=sem, vmem_limit_bytes=VMEM_LIMIT_BYTES, **kw)


def _tile(n, want):
    if n <= want:
        return n
    t = want - want % LANES
    while t > 0:
        if n % t == 0:
            return t
        t -= LANES
    return n


ANY = pl.BlockSpec(memory_space=pl.ANY)


class _Stage:
    def __init__(self, ins, out_shapes, aliases, n_sem, issue, wait, mid=None):
        self.ins, self.out_shapes, self.aliases, self.n_sem = list(ins), list(out_shapes), dict(aliases), n_sem
        self.issue, self.wait, self.mid = issue, wait, mid


def _pcall(body, args, *, name, grid, in_specs, out_specs, out_shape, scratch_shapes=(), sem=None, comm=(),
           prefetch=(), io_alias=None):
    multi = isinstance(out_shape, (list, tuple))
    out_specs_l = list(out_specs) if multi else [out_specs]
    out_shape_l = list(out_shape) if multi else [out_shape]
    npf = len(prefetch)
    own_aliases = {npf + a: o for a, o in (io_alias or {}).items()}

    def call(fn, in_specs_, out_specs_, out_shape_, scratch_, operands, sem_, aliases=None):
        kw = dict(name=name, out_shape=out_shape_, compiler_params=_params(sem_))
        if aliases:
            kw["input_output_aliases"] = aliases
        if npf:
            spec = pltpu.PrefetchScalarGridSpec(num_scalar_prefetch=npf, grid=grid, in_specs=in_specs_,
                                                out_specs=out_specs_, scratch_shapes=scratch_)
            return pl.pallas_call(fn, grid_spec=spec, **kw)(*prefetch, *operands)
        return pl.pallas_call(fn, grid=grid, in_specs=in_specs_, out_specs=out_specs_, scratch_shapes=scratch_,
                              **kw)(*operands)

    if not comm:
        return call(body, list(in_specs), out_specs, out_shape, list(scratch_shapes), args, sem, own_aliases)
    n_in, n_out, n_scr = len(in_specs), len(out_specs_l), len(scratch_shapes)
    cins = [a for st in comm for a in st.ins]
    couts = [s for st in comm for s in st.out_shapes]
    aliases, ci, co = dict(own_aliases), 0, 0
    for st in comm:
        for a_i, o_i in st.aliases.items():
            aliases[npf + n_in + ci + a_i] = n_out + co + o_i
        ci, co = ci + len(st.ins), co + len(st.out_shapes)

    def wrapped(*refs):
        pre = refs[:npf]
        p = npf
        ins = refs[p:p + n_in]; p += n_in
        cin_refs = refs[p:p + len(cins)]; p += len(cins)
        outs = refs[p:p + n_out]; p += n_out
        cout_refs = refs[p:p + len(couts)]; p += len(couts)
        scr = refs[p:p + n_scr]; p += n_scr
        sems = refs[p:]
        first = functools.reduce(jnp.logical_and, [pl.program_id(a) == 0 for a in range(len(grid))])
        last = functools.reduce(jnp.logical_and, [pl.program_id(a) == grid[a] - 1 for a in range(len(grid))])

        def each(what):
            i, o = 0, 0
            for k, st in enumerate(comm):
                fn = getattr(st, what)
                if fn is not None:
                    fn(cin_refs[i:i + len(st.ins)], cout_refs[o:o + len(st.out_shapes)], sems[2 * k], sems[2 * k + 1])
                i, o = i + len(st.ins), o + len(st.out_shapes)

        @pl.when(first)
        def _():
            each("issue")

        if any(st.mid is not None for st in comm):
            n_steps = math.prod(grid)
            assert n_steps >= 4, "a two-leg stage needs a carrier with several grid steps"
            lin = functools.reduce(lambda acc, a: acc * grid[a] + pl.program_id(a), range(len(grid)), 0)

            @pl.when(lin == (3 * n_steps) // 4)
            def _():
                each("mid")

        body(*pre, *ins, *outs, *scr)

        @pl.when(last)
        def _():
            each("wait")

    sem_scr = [pltpu.SemaphoreType.DMA((st.n_sem,)) for st in comm for _ in range(2)]
    res = call(wrapped, list(in_specs) + [ANY] * len(cins), out_specs_l + [ANY] * len(couts), out_shape_l + couts,
               list(scratch_shapes) + sem_scr, (*args, *cins), ("arbitrary",) * len(grid), aliases)
    normal = list(res[:n_out])
    stage_outs, o = [], n_out
    for st in comm:
        stage_outs.append(list(res[o:o + len(st.out_shapes)]))
        o += len(st.out_shapes)
    return (normal if multi else normal[0]), stage_outs


def _matmul(a, b, *, name, ta=False, tb=False, add=None, out_dtype=F32, tm=1024, tn=1024, tk=2048,
            b_split=False, out_split=0, comm=(), loss_target=None):
    if ta:
        kdim, m = a.shape
    else:
        m, kdim = a.shape
    if b_split:
        assert tb
        nsp, n, kb = b.shape
        kb = kb * nsp
    elif tb:
        n, kb = b.shape
    else:
        kb, n = b.shape
    assert kb == kdim, (a.shape, b.shape, ta, tb)
    if b_split:
        tk = kdim
    if out_split:
        tn = _tile(n // out_split, tn)
    tm, tn, tk = _tile(m, tm), _tile(n, tn), _tile(kdim, tk)
    nk = kdim // tk
    dims = (((0 if ta else 1,), (1 if tb else 0,)), ((), ()))

    def product(a_ref, b_ref):
        if not b_split:
            return lax.dot_general(a_ref[...].astype(BF16), b_ref[...].astype(BF16), dims, preferred_element_type=F32)
        per = kdim // nsp
        return sum(lax.dot_general(a_ref[:, per * c:per * (c + 1)].astype(BF16), b_ref[c].astype(BF16), dims,
                                   preferred_element_type=F32) for c in range(nsp))

    def body(*refs):
        a_ref, b_ref = refs[:2]
        add_ref = refs[2] if add is not None else None
        n_in = 2 + (add is not None) + (loss_target is not None)
        o_ref = refs[n_in]

        def finish(r):
            if add_ref is not None:
                r = r + add_ref[...].astype(F32)
            if loss_target is None:
                o_ref[...] = r.astype(o_ref.dtype)
                return
            db_ref, l_ref = refs[n_in + 1], refs[n_in + 2]
            err = r - refs[n_in - 1][...]
            d_out = err * (1.0 / n)
            o_ref[...] = d_out
            db_ref[...] = d_out.astype(BF16)
            part = jnp.broadcast_to((0.5 / n) * jnp.sum(jnp.sum(err * err, axis=-1, keepdims=True), axis=0, keepdims=True),
                                    (1, LANES))
            first = jnp.logical_and(pl.program_id(0) == 0, pl.program_id(1) == 0)

            @pl.when(first)
            def _():
                l_ref[...] = part

            @pl.when(jnp.logical_not(first))
            def _():
                l_ref[...] += part

        if nk == 1:
            finish(product(a_ref, b_ref))
            return
        acc_ref = refs[-1]
        k = pl.program_id(2)
        part = product(a_ref, b_ref)

        @pl.when(k == 0)
        def _():
            acc_ref[...] = part

        @pl.when(k > 0)
        def _():
            acc_ref[...] += part

        @pl.when(k == nk - 1)
        def _():
            finish(acc_ref[...])

    a_spec = pl.BlockSpec((tk, tm), lambda i, j, k: (k, i)) if ta else pl.BlockSpec((tm, tk), lambda i, j, k: (i, k))
    if b_split:
        b_spec = pl.BlockSpec((nsp, tn, kdim // nsp), lambda i, j, k: (0, j, 0))
    elif tb:
        b_spec = pl.BlockSpec((tn, tk), lambda i, j, k: (j, k))
    else:
        b_spec = pl.BlockSpec((tk, tn), lambda i, j, k: (k, j))
    in_specs = [a_spec, b_spec]
    args = [a, b]
    if add is not None:
        in_specs.append(pl.BlockSpec((tm, tn), lambda i, j, k: (i, j)))
        args.append(add)
    tile = pl.BlockSpec((tm, tn), lambda i, j, k: (i, j))
    sem = ("parallel", "parallel", "arbitrary")
    if out_split:
        per = (n // out_split) // tn
        out_spec = pl.BlockSpec((None, tm, tn), lambda i, j, k: (j // per, i, j % per))
        out_shape = jax.ShapeDtypeStruct((out_split, m, n // out_split), out_dtype)
    elif loss_target is not None:
        in_specs.append(tile)
        args.append(loss_target)
        out_spec = [tile, tile, pl.BlockSpec((1, LANES), lambda i, j, k: (0, 0))]
        out_shape = [jax.ShapeDtypeStruct((m, n), F32), jax.ShapeDtypeStruct((m, n), BF16),
                     jax.ShapeDtypeStruct((1, LANES), F32)]
        sem = ("arbitrary",) * 3
    else:
        out_spec = tile
        out_shape = jax.ShapeDtypeStruct((m, n), out_dtype)
    return _pcall(body, args, name=name, grid=(m // tm, n // tn, nk), in_specs=in_specs, out_specs=out_spec,
                  out_shape=out_shape, scratch_shapes=[pltpu.VMEM((tm, tn), F32)] if nk > 1 else [],
                  sem=sem, comm=comm)


def _rms_fwd(x, g, *, name, tm=512, comm=()):
    s, d = x.shape
    tm = _tile(s, tm)

    def body(x_ref, g_ref, o_ref):
        xv = x_ref[...]
        r = lax.rsqrt(jnp.mean(xv * xv, axis=-1, keepdims=True) + EPS)
        o_ref[...] = (xv * r * g_ref[...]).astype(o_ref.dtype)

    return _pcall(body, (x, g), name=name, grid=(s // tm,),
                  in_specs=[pl.BlockSpec((tm, d), lambda i: (i, 0)), pl.BlockSpec((1, d), lambda i: (0, 0))],
                  out_specs=pl.BlockSpec((tm, d), lambda i: (i, 0)),
                  out_shape=jax.ShapeDtypeStruct((s, d), BF16), sem=("parallel",), comm=comm)


def _rms_bwd(dy, x, g, res, *, name, tm=512, comm=()):
    s, d = x.shape
    tm = _tile(s, tm)

    def body(dy_ref, x_ref, g_ref, res_ref, dx_ref, dxb_ref, dg_ref):
        xv = x_ref[...]
        dyv = dy_ref[...]
        r = lax.rsqrt(jnp.mean(xv * xv, axis=-1, keepdims=True) + EPS)
        xhat = xv * r
        dyg = dyv * g_ref[...]
        mt = jnp.mean(dyg * xhat, axis=-1, keepdims=True)
        dx = res_ref[...] + r * (dyg - xhat * mt)
        dx_ref[...] = dx
        dxb_ref[...] = dx.astype(BF16)
        part = jnp.sum(dyv * xhat, axis=0, keepdims=True)

        @pl.when(pl.program_id(0) == 0)
        def _():
            dg_ref[...] = part

        @pl.when(pl.program_id(0) > 0)
        def _():
            dg_ref[...] += part

    row = pl.BlockSpec((tm, d), lambda i: (i, 0))
    vec = pl.BlockSpec((1, d), lambda i: (0, 0))
    return _pcall(body, (dy, x, g, res), name=name, grid=(s // tm,), in_specs=[row, row, vec, row],
                  out_specs=[row, row, vec],
                  out_shape=[jax.ShapeDtypeStruct((s, d), F32), jax.ShapeDtypeStruct((s, d), BF16),
                             jax.ShapeDtypeStruct((1, d), F32)],
                  sem=("arbitrary",), comm=comm)


def _lane(shape):
    return lax.broadcasted_iota(jnp.int32, shape, 1)


def _halfsum(t, lo):
    s_lo = jnp.sum(jnp.where(lo, t, 0.0), axis=-1, keepdims=True)
    s_hi = jnp.sum(jnp.where(lo, 0.0, t), axis=-1, keepdims=True)
    return jnp.where(lo, s_lo, s_hi)


def _norm_pair(x, g, lo):
    r = lax.rsqrt(_halfsum(x * x, lo) * (1.0 / 64.0) + EPS)
    xhat = x * r
    return xhat * g, xhat, r


def _norm_pair_bwd(dy, g, xhat, r, lo):
    dyg = dy * g
    mt = _halfsum(dyg * xhat, lo) * (1.0 / 64.0)
    return r * (dyg - xhat * mt), jnp.sum(dy * xhat, axis=0, keepdims=True)


def _norm_full(x, g):
    r = lax.rsqrt(jnp.mean(x * x, axis=-1, keepdims=True) + EPS)
    xhat = x * r
    return xhat * g, xhat, r


def _norm_full_bwd(dy, g, xhat, r):
    dyg = dy * g
    mt = jnp.mean(dyg * xhat, axis=-1, keepdims=True)
    return r * (dyg - xhat * mt), jnp.sum(dy * xhat, axis=0, keepdims=True)


def _rot(x, first32):
    return jnp.where(first32, pltpu.roll(x, 96, axis=1), pltpu.roll(x, 32, axis=1))


def _rope(x, cos_t, sin_t, first32):
    return x * cos_t + _rot(x, first32) * sin_t


def _rope_bwd(dy, cos_t, sin_t, first32):
    return dy * cos_t + _rot(dy * sin_t, first32)


G_SWA_Q, G_SWA_K, G_QN, G_QR, G_KN, G_KR, G_MQ = range(7)

C_QA, C_KA, C_VA, C_CQ, C_CKV, C_QM, C_KR = 0, 1024, 1152, 1280, 1792, 2304, 2816


def _prep_common(p_ref, g128_ref, gcq_ref, gckv_ref, wuq_ref, wukv_ref, cos_ref, sin_ref):
    tm = p_ref.shape[0]
    lane = _lane((tm, LANES))
    lo = lane < 64
    first32 = (lane % 64) < 32
    cos_t = cos_ref[...]
    sin_t = sin_ref[...]
    g = lambda row: g128_ref[row:row + 1, :]
    out = dict(lo=lo, first32=first32, cos_t=cos_t, sin_t=sin_t, lane=lane)
    cq_n, cq_hat, cq_r = _norm_full(p_ref[:, C_CQ:C_CQ + MLA_RANK], gcq_ref[...])
    ckv_n, ckv_hat, ckv_r = _norm_full(p_ref[:, C_CKV:C_CKV + MLA_RANK], gckv_ref[...])
    cq_b = cq_n.astype(BF16)
    ckv_b = ckv_n.astype(BF16)
    q_b = jnp.dot(cq_b, wuq_ref[...], preferred_element_type=F32)
    kv_b = jnp.dot(ckv_b, wukv_ref[...], preferred_element_type=F32)
    out.update(cq_b=cq_b, cq_hat=cq_hat, cq_r=cq_r, ckv_b=ckv_b, ckv_hat=ckv_hat, ckv_r=ckv_r, q_b=q_b, kv_b=kv_b, g=g)
    return out


def _attn_prep_fwd(proj, g128, gcq, gckv, wuq, wukv, cos_t, sin_t, *, tm=512, comm=()):
    s = proj.shape[0]
    tm = _tile(s, tm)

    def body(p_ref, g128_ref, gcq_ref, gckv_ref, wuq_ref, wukv_ref, cos_ref, sin_ref,
             qa_ref, ka_ref, va_ref, qcat_ref, kcat_ref, vb_ref, qm_ref):
        c = _prep_common(p_ref, g128_ref, gcq_ref, gckv_ref, wuq_ref, wukv_ref, cos_ref, sin_ref)
        lo, first32, g = c["lo"], c["first32"], c["g"]
        for j in range(SWA_Q_HEADS // 2):
            y, _, _ = _norm_pair(p_ref[:, C_QA + 128 * j:C_QA + 128 * (j + 1)], g(G_SWA_Q), lo)
            qa_ref[:, 128 * j:128 * (j + 1)] = y.astype(BF16)
        y, _, _ = _norm_pair(p_ref[:, C_KA:C_KA + 128], g(G_SWA_K), lo)
        ka_ref[...] = y.astype(BF16)
        va_ref[...] = p_ref[:, C_VA:C_VA + 128].astype(BF16)
        kr, _, _ = _norm_pair(p_ref[:, C_KR:C_KR + 128], g(G_KR), lo)
        kr = jnp.where(lo, _rope(kr, c["cos_t"], c["sin_t"], first32), 0.0)
        krkr = (kr + pltpu.roll(kr, 64, axis=1)).astype(BF16)
        q_b, kv_b = c["q_b"], c["kv_b"]
        qr = []
        for j in range(MLA_HEADS // 2):
            y, _, _ = _norm_pair(q_b[:, 512 + 128 * j:512 + 128 * (j + 1)], g(G_QR), lo)
            qr.append(_rope(y, c["cos_t"], c["sin_t"], first32))
        for h in range(MLA_HEADS):
            qn, _, _ = _norm_full(q_b[:, 128 * h:128 * (h + 1)], g(G_QN))
            keep = lo if h % 2 == 0 else jnp.logical_not(lo)
            qcat_ref[h, :, 0:128] = qn.astype(BF16)
            qcat_ref[h, :, 128:256] = jnp.where(keep, qr[h // 2], 0.0).astype(BF16)
            kn, _, _ = _norm_full(kv_b[:, 128 * h:128 * (h + 1)], g(G_KN))
            kcat_ref[h, :, 0:128] = kn.astype(BF16)
            kcat_ref[h, :, 128:256] = krkr
        vb_ref[...] = kv_b[:, 512:1024].astype(BF16)
        for h in range(MEM_HEADS):
            y, _, _ = _norm_full(p_ref[:, C_QM + 128 * h:C_QM + 128 * (h + 1)], g(G_MQ))
            qm_ref[:, 128 * h:128 * (h + 1)] = y.astype(BF16)

    row = lambda w: pl.BlockSpec((tm, w), lambda i: (i, 0))
    full = lambda shape: pl.BlockSpec(shape, lambda i: tuple(0 for _ in shape))
    cat = pl.BlockSpec((MLA_HEADS, tm, 256), lambda i: (0, i, 0))
    return _pcall(
        body, (proj, g128, gcq, gckv, wuq, wukv, cos_t, sin_t), name="attn_prep_fwd", grid=(s // tm,),
        in_specs=[row(IN_PAD), full((8, 128)), full((1, 512)), full((1, 512)), full((512, 768)), full((512, 1024)),
                  row(128), row(128)],
        out_specs=[row(1024), row(128), row(128), cat, cat, row(512), row(512)],
        out_shape=[jax.ShapeDtypeStruct((s, 1024), BF16), jax.ShapeDtypeStruct((s, 128), BF16),
                   jax.ShapeDtypeStruct((s, 128), BF16), jax.ShapeDtypeStruct((MLA_HEADS, s, 256), BF16),
                   jax.ShapeDtypeStruct((MLA_HEADS, s, 256), BF16), jax.ShapeDtypeStruct((s, 512), BF16),
                   jax.ShapeDtypeStruct((s, 512), BF16)],
        sem=("parallel",), comm=comm)


def _attn_prep_bwd(proj, g128, gcq, gckv, wuq, wukv, cos_t, sin_t,
                   d_qa, d_ka, d_va, d_qcat, d_kcat, d_vb, d_qm, *, tm=256, comm=()):
    s = proj.shape[0]
    tm = _tile(s, tm)

    def body(p_ref, g128_ref, gcq_ref, gckv_ref, wuq_ref, wukv_ref, cos_ref, sin_ref,
             dqa_ref, dka_ref, dva_ref, dqcat_ref, dkcat_ref, dvb_ref, dqm_ref,
             dp_ref, dwuq_ref, dwukv_ref, dg128_ref, dgcq_ref, dgckv_ref):
        c = _prep_common(p_ref, g128_ref, gcq_ref, gckv_ref, wuq_ref, wukv_ref, cos_ref, sin_ref)
        lo, first32, g = c["lo"], c["first32"], c["g"]
        cos_v, sin_v = c["cos_t"], c["sin_t"]
        q_b, kv_b = c["q_b"], c["kv_b"]
        zero_row = jnp.zeros((1, LANES), F32)
        dg = {k: zero_row for k in range(7)}

        for j in range(SWA_Q_HEADS // 2):
            sl = slice(C_QA + 128 * j, C_QA + 128 * (j + 1))
            _, xhat, r = _norm_pair(p_ref[:, sl], g(G_SWA_Q), lo)
            dx, dgj = _norm_pair_bwd(dqa_ref[:, 128 * j:128 * (j + 1)], g(G_SWA_Q), xhat, r, lo)
            dp_ref[:, sl] = dx.astype(BF16)
            dg[G_SWA_Q] = dg[G_SWA_Q] + dgj
        _, xhat, r = _norm_pair(p_ref[:, C_KA:C_KA + 128], g(G_SWA_K), lo)
        dx, dgj = _norm_pair_bwd(dka_ref[...], g(G_SWA_K), xhat, r, lo)
        dp_ref[:, C_KA:C_KA + 128] = dx.astype(BF16)
        dg[G_SWA_K] = dgj
        dp_ref[:, C_VA:C_VA + 128] = dva_ref[...].astype(BF16)

        dqb_parts = [None] * 6
        for h in range(MLA_HEADS):
            _, xhat, r = _norm_full(q_b[:, 128 * h:128 * (h + 1)], g(G_QN))
            dx, dgj = _norm_full_bwd(dqcat_ref[h, :, 0:128], g(G_QN), xhat, r)
            dqb_parts[h] = dx
            dg[G_QN] = dg[G_QN] + dgj
        for j in range(MLA_HEADS // 2):
            _, xhat, r = _norm_pair(q_b[:, 512 + 128 * j:512 + 128 * (j + 1)], g(G_QR), lo)
            d_rot = jnp.where(lo, dqcat_ref[2 * j, :, 128:256], dqcat_ref[2 * j + 1, :, 128:256])
            d_y = _rope_bwd(d_rot, cos_v, sin_v, first32)
            dx, dgj = _norm_pair_bwd(d_y, g(G_QR), xhat, r, lo)
            dqb_parts[4 + j] = dx
            dg[G_QR] = dg[G_QR] + dgj
        d_qb = jnp.concatenate(dqb_parts, axis=1).astype(BF16)
        dwuq = lax.dot_general(c["cq_b"], d_qb, (((0,), (0,)), ((), ())), preferred_element_type=F32)
        d_cqn = lax.dot_general(d_qb, wuq_ref[...], (((1,), (1,)), ((), ())), preferred_element_type=F32)
        dx, dgcq = _norm_full_bwd(d_cqn, gcq_ref[...], c["cq_hat"], c["cq_r"])
        dp_ref[:, C_CQ:C_CQ + MLA_RANK] = dx.astype(BF16)

        dkv_parts = []
        d_krkr = jnp.zeros((p_ref.shape[0], LANES), F32)
        for h in range(MLA_HEADS):
            _, xhat, r = _norm_full(kv_b[:, 128 * h:128 * (h + 1)], g(G_KN))
            dx, dgj = _norm_full_bwd(dkcat_ref[h, :, 0:128], g(G_KN), xhat, r)
            dkv_parts.append(dx)
            dg[G_KN] = dg[G_KN] + dgj
            d_krkr = d_krkr + dkcat_ref[h, :, 128:256]
        d_kvb = jnp.concatenate(dkv_parts + [dvb_ref[...]], axis=1).astype(BF16)
        dwukv = lax.dot_general(c["ckv_b"], d_kvb, (((0,), (0,)), ((), ())), preferred_element_type=F32)
        d_ckvn = lax.dot_general(d_kvb, wukv_ref[...], (((1,), (1,)), ((), ())), preferred_element_type=F32)
        dx, dgckv = _norm_full_bwd(d_ckvn, gckv_ref[...], c["ckv_hat"], c["ckv_r"])
        dp_ref[:, C_CKV:C_CKV + MLA_RANK] = dx.astype(BF16)

        _, xhat, r = _norm_pair(p_ref[:, C_KR:C_KR + 128], g(G_KR), lo)
        d_kr = jnp.where(lo, d_krkr + pltpu.roll(d_krkr, 64, axis=1), 0.0)
        d_y = jnp.where(lo, _rope_bwd(d_kr, cos_v, sin_v, first32), 0.0)
        dx, dgj = _norm_pair_bwd(d_y, g(G_KR), xhat, r, lo)
        dp_ref[:, C_KR:C_KR + 128] = jnp.where(lo, dx, 0.0).astype(BF16)
        dp_ref[:, C_KR + 128:] = jnp.zeros((p_ref.shape[0], IN_PAD - C_KR - 128), BF16)
        dg[G_KR] = dgj

        for h in range(MEM_HEADS):
            sl = slice(C_QM + 128 * h, C_QM + 128 * (h + 1))
            _, xhat, r = _norm_full(p_ref[:, sl], g(G_MQ))
            dx, dgj = _norm_full_bwd(dqm_ref[:, 128 * h:128 * (h + 1)], g(G_MQ), xhat, r)
            dp_ref[:, sl] = dx.astype(BF16)
            dg[G_MQ] = dg[G_MQ] + dgj

        dg_tile = jnp.concatenate([dg[k] for k in range(7)] + [zero_row], axis=0)

        @pl.when(pl.program_id(0) == 0)
        def _():
            dwuq_ref[...] = dwuq
            dwukv_ref[...] = dwukv
            dg128_ref[...] = dg_tile
            dgcq_ref[...] = dgcq
            dgckv_ref[...] = dgckv

        @pl.when(pl.program_id(0) > 0)
        def _():
            dwuq_ref[...] += dwuq
            dwukv_ref[...] += dwukv
            dg128_ref[...] += dg_tile
            dgcq_ref[...] += dgcq
            dgckv_ref[...] += dgckv

    row = lambda w: pl.BlockSpec((tm, w), lambda i: (i, 0))
    full = lambda shape: pl.BlockSpec(shape, lambda i: tuple(0 for _ in shape))
    cat = pl.BlockSpec((MLA_HEADS, tm, 256), lambda i: (0, i, 0))
    return _pcall(
        body, (proj, g128, gcq, gckv, wuq, wukv, cos_t, sin_t, d_qa, d_ka, d_va, d_qcat, d_kcat, d_vb, d_qm),
        name="attn_prep_bwd", grid=(s // tm,),
        in_specs=[row(IN_PAD), full((8, 128)), full((1, 512)), full((1, 512)), full((512, 768)), full((512, 1024)),
                  row(128), row(128),
                  row(1024), row(128), row(128), cat, cat, row(512), row(512)],
        out_specs=[row(IN_PAD), full((512, 768)), full((512, 1024)), full((8, 128)), full((1, 512)), full((1, 512))],
        out_shape=[jax.ShapeDtypeStruct((s, IN_PAD), BF16), jax.ShapeDtypeStruct((512, 768), F32),
                   jax.ShapeDtypeStruct((512, 1024), F32), jax.ShapeDtypeStruct((8, 128), F32),
                   jax.ShapeDtypeStruct((1, 512), F32), jax.ShapeDtypeStruct((1, 512), F32)],
        sem=("arbitrary",), comm=comm)


SWA_SLOPES = tuple(2.0 ** (-8.0 * h / SWA_Q_HEADS) for h in range(1, SWA_Q_HEADS + 1))
SWA_SCALE = SWA_HEAD_DIM ** -0.5
NT_DIMS = (((1,), (1,)), ((), ()))
TN_DIMS = (((0,), (0,)), ((), ()))


def _swa_span(n, kp_ref, kc_ref, vp_ref, vc_ref, pcol_ref, pprow_ref, pcrow_ref):
    k_span = jnp.concatenate([kp_ref[...], kc_ref[...]], axis=0).astype(F32)
    v_span = jnp.concatenate([vp_ref[...], vc_ref[...]], axis=0).astype(F32)
    lo = _lane((2 * BLOCK, LANES)) < 64
    k_sw = pltpu.roll(k_span, 64, axis=1)
    v_sw = pltpu.roll(v_span, 64, axis=1)
    kk = (jnp.where(lo, k_span, k_sw).astype(BF16), jnp.where(lo, k_sw, k_span).astype(BF16))
    vv_lo = (jnp.where(lo, v_span, 0.0).astype(BF16), jnp.where(lo, v_sw, 0.0).astype(BF16))
    vv_hi = (jnp.where(lo, 0.0, v_sw).astype(BF16), jnp.where(lo, 0.0, v_span).astype(BF16))
    pk = jnp.concatenate([pprow_ref[...], pcrow_ref[...]], axis=1)
    dist = jnp.abs(pcol_ref[...] - pk)
    qi = lax.broadcasted_iota(jnp.int32, (BLOCK, 2 * BLOCK), 0)
    ki = lax.broadcasted_iota(jnp.int32, (BLOCK, 2 * BLOCK), 1)
    first_key = jnp.where(n > 0, qi + 1, jnp.maximum(qi + 1, BLOCK))
    valid = jnp.logical_and(ki >= first_key, ki <= qi + BLOCK)
    mask_add = jnp.where(valid, 0.0, NEG_INF)
    return kk, vv_lo, vv_hi, dist, mask_add


def _swa_heads(q_ref, lo):
    heads = []
    for j in range(SWA_Q_HEADS // 2):
        q_pair = q_ref[:, 128 * j:128 * (j + 1)].astype(F32)
        for par in (0, 1):
            q_h = jnp.where(lo if par == 0 else jnp.logical_not(lo), q_pair, 0.0).astype(BF16)
            heads.append((2 * j + par, (2 * j) // (SWA_Q_HEADS // SWA_KV_HEADS), par, q_h))
    return heads


def _swa_probs(raw, dist, mask_add, slope, sink):
    s = raw * SWA_SCALE - slope * dist + mask_add
    m = jnp.maximum(jnp.max(s, axis=-1, keepdims=True), sink)
    e = jnp.exp(s - m)
    e_sink = jnp.exp(sink - m)
    inv = 1.0 / (jnp.sum(e, axis=-1, keepdims=True) + e_sink)
    return e * inv, e_sink * inv


def _swa_specs():
    blk = lambda w: pl.BlockSpec((BLOCK, w), lambda n: (n, 0))
    prev = lambda w: pl.BlockSpec((BLOCK, w), lambda n: (jnp.maximum(n - 1, 0), 0))
    prow_c = pl.BlockSpec((1, BLOCK), lambda n: (0, n))
    prow_p = pl.BlockSpec((1, BLOCK), lambda n: (0, jnp.maximum(n - 1, 0)))
    smem = pl.BlockSpec(memory_space=pltpu.SMEM)
    return [blk(1024), prev(128), blk(128), prev(128), blk(128), blk(1), prow_p, prow_c, smem], blk


def _swa_fwd(qa, ka, va, pos_col, pos_row, sinks, *, comm=()):
    s = qa.shape[0]
    in_specs, blk = _swa_specs()

    def body(q_ref, kp_ref, kc_ref, vp_ref, vc_ref, pcol_ref, pprow_ref, pcrow_ref, sink_ref, o_ref, yb_ref):
        n = pl.program_id(0)
        kk, vv_lo, vv_hi, dist, mask_add = _swa_span(n, kp_ref, kc_ref, vp_ref, vc_ref, pcol_ref, pprow_ref, pcrow_ref)
        lo = _lane((BLOCK, LANES)) < 64
        heads = _swa_heads(q_ref, lo)
        raws = [lax.dot_general(q_h, kk[kv], NT_DIMS, preferred_element_type=F32) for _, kv, _, q_h in heads]
        probs = [_swa_probs(raw, dist, mask_add, SWA_SLOPES[h], sink_ref[h])[0].astype(BF16)
                 for raw, (h, _, _, _) in zip(raws, heads)]
        for j in range(SWA_Q_HEADS // 2):
            kv = heads[2 * j][1]
            out = (jnp.dot(probs[2 * j], vv_lo[kv], preferred_element_type=F32)
                   + jnp.dot(probs[2 * j + 1], vv_hi[kv], preferred_element_type=F32))
            o_ref[:, 128 * j:128 * (j + 1)] = out
            yb_ref[:, 128 * j:128 * (j + 1)] = out.astype(BF16)

    return _pcall(body, (qa, ka, ka, va, va, pos_col, pos_row, pos_row, sinks), name="swa_fwd", grid=(s // BLOCK,),
                  in_specs=in_specs, out_specs=[blk(1024), blk(1024)],
                  out_shape=[jax.ShapeDtypeStruct((s, 1024), F32), jax.ShapeDtypeStruct((s, D_MODEL), BF16)],
                  sem=("parallel",), comm=comm)


def _swa_bwd(qa, ka, va, pos_col, pos_row, sinks, y_a, d_y, *, comm=()):
    s = qa.shape[0]
    in_specs, blk = _swa_specs()
    whole = pl.BlockSpec((s, 128), lambda n: (0, 0))

    def body(q_ref, kp_ref, kc_ref, vp_ref, vc_ref, pcol_ref, pprow_ref, pcrow_ref, sink_ref, y_ref, dy_ref,
             dq_ref, dk_ref, dv_ref, dsink_ref):
        n = pl.program_id(0)

        @pl.when(n == 0)
        def _():
            dk_ref[...] = jnp.zeros_like(dk_ref)
            dv_ref[...] = jnp.zeros_like(dv_ref)
            dsink_ref[...] = jnp.zeros_like(dsink_ref)

        kk, vv_lo, vv_hi, dist, mask_add = _swa_span(n, kp_ref, kc_ref, vp_ref, vc_ref, pcol_ref, pprow_ref, pcrow_ref)
        lo = _lane((BLOCK, LANES)) < 64
        lo2 = _lane((2 * BLOCK, LANES)) < 64
        lane1 = _lane((1, LANES))
        dsink = jnp.zeros((1, LANES), F32)
        dkk = [jnp.zeros((2 * BLOCK, LANES), F32) for _ in range(SWA_KV_HEADS)]
        dvv = [jnp.zeros((2 * BLOCK, LANES), F32) for _ in range(SWA_KV_HEADS)]
        heads = _swa_heads(q_ref, lo)
        do_b, deltas = [], []
        for j in range(SWA_Q_HEADS // 2):
            do_pair = dy_ref[:, 128 * j:128 * (j + 1)]
            doy = do_pair * y_ref[:, 128 * j:128 * (j + 1)]
            do_b.append(do_pair.astype(BF16))
            deltas.append(jnp.sum(jnp.where(lo, doy, 0.0), axis=-1, keepdims=True))
            deltas.append(jnp.sum(jnp.where(lo, 0.0, doy), axis=-1, keepdims=True))
        raws = [lax.dot_general(q_h, kk[kv], NT_DIMS, preferred_element_type=F32) for _, kv, _, q_h in heads]
        dps = [lax.dot_general(do_b[h // 2], (vv_lo, vv_hi)[par][kv], NT_DIMS, preferred_element_type=F32)
               for h, kv, par, _ in heads]
        p_b, ds_b = [], []
        for h, kv, par, _ in heads:
            p, p_sink = _swa_probs(raws[h], dist, mask_add, SWA_SLOPES[h], sink_ref[h])
            ds = p * (dps[h] - deltas[h])
            dsink = dsink + jnp.where(lane1 == h, -jnp.sum(p_sink * deltas[h], axis=0, keepdims=True), 0.0)
            p_b.append(p.astype(BF16))
            ds_b.append((ds * SWA_SCALE).astype(BF16))
        dq_halves = []
        for h, kv, par, q_h in heads:
            dq_halves.append(jnp.dot(ds_b[h], kk[kv], preferred_element_type=F32))
            dkk[kv] = dkk[kv] + lax.dot_general(ds_b[h], q_h, TN_DIMS, preferred_element_type=F32)
            pv = lax.dot_general(p_b[h], do_b[h // 2], TN_DIMS, preferred_element_type=F32)
            dvv[kv] = dvv[kv] + jnp.where(lo2 if par == 0 else jnp.logical_not(lo2), pv, 0.0)
        for j in range(SWA_Q_HEADS // 2):
            dq_ref[:, 128 * j:128 * (j + 1)] = jnp.where(lo, dq_halves[2 * j], dq_halves[2 * j + 1])
        fold = lambda t: t + pltpu.roll(t, 64, axis=1)
        dk_span = jnp.where(lo2, fold(dkk[0]), fold(dkk[1]))
        dv_span = jnp.where(lo2, fold(dvv[0]), fold(dvv[1]))
        prev0 = pl.multiple_of(jnp.maximum(n - 1, 0) * BLOCK, BLOCK)
        cur0 = pl.multiple_of(n * BLOCK, BLOCK)
        dk_ref[pl.ds(prev0, BLOCK), :] += dk_span[0:BLOCK]
        dk_ref[pl.ds(cur0, BLOCK), :] += dk_span[BLOCK:]
        dv_ref[pl.ds(prev0, BLOCK), :] += dv_span[0:BLOCK]
        dv_ref[pl.ds(cur0, BLOCK), :] += dv_span[BLOCK:]
        dsink_ref[...] += dsink

    return _pcall(
        body, (qa, ka, ka, va, va, pos_col, pos_row, pos_row, sinks, y_a, d_y), name="swa_bwd", grid=(s // BLOCK,),
        in_specs=in_specs + [blk(1024), blk(1024)],
        out_specs=[blk(1024), whole, whole, pl.BlockSpec((1, LANES), lambda n: (0, 0))],
        out_shape=[jax.ShapeDtypeStruct((s, 1024), F32), jax.ShapeDtypeStruct((s, 128), F32),
                   jax.ShapeDtypeStruct((s, 128), F32), jax.ShapeDtypeStruct((1, LANES), F32)],
        sem=("arbitrary",), comm=comm)


MLA_SCALE = (MLA_NOPE + MLA_ROPE) ** -0.5
MLA_TILE = 512


def _tile_pairs(nt, q_major):
    pairs = [(i, j) for i in range(nt) for j in range(i + 1)] if q_major else \
            [(i, j) for j in range(nt) for i in range(j, nt)]
    return jnp.asarray([p[0] for p in pairs], jnp.int32), jnp.asarray([p[1] for p in pairs], jnp.int32)


def _diag_mask(t):
    return lax.broadcasted_iota(jnp.int32, (t, t), 1) <= lax.broadcasted_iota(jnp.int32, (t, t), 0)


def _mla_fwd(q_cat, k_cat, v_b, y_all, *, comm=()):
    nh, s, _ = q_cat.shape
    t = _tile(s, MLA_TILE)
    qi, kj = _tile_pairs(s // t, True)
    ycol = (SWA_Q_HEADS * SWA_HEAD_DIM) // (nh * MLA_V)

    def body(qi_ref, kj_ref, q_ref, k_ref, v_ref, _, o_ref, lse_ref, yb_ref, m_sc, l_sc, acc_sc):
        i, j = qi_ref[pl.program_id(0)], kj_ref[pl.program_id(0)]

        @pl.when(j == 0)
        def _():
            m_sc[...] = jnp.full_like(m_sc, NEG_INF)
            l_sc[...] = jnp.zeros_like(l_sc)
            acc_sc[...] = jnp.zeros_like(acc_sc)

        def update(diagonal):
            scores = [lax.dot_general(q_ref[h], k_ref[h], NT_DIMS, preferred_element_type=F32) for h in range(nh)]
            probs, alphas = [], []
            for h in range(nh):
                sc = scores[h] * MLA_SCALE
                if diagonal:
                    sc = jnp.where(_diag_mask(t), sc, NEG_INF)
                m_old = m_sc[h]
                m_new = jnp.maximum(m_old, jnp.max(sc, axis=-1, keepdims=True))
                alpha = jnp.exp(m_old - m_new)
                p = jnp.exp(sc - m_new)
                l_sc[h] = alpha * l_sc[h] + jnp.sum(p, axis=-1, keepdims=True)
                m_sc[h] = m_new
                probs.append(p.astype(BF16))
                alphas.append(alpha)
            for h in range(nh):
                acc_sc[h] = alphas[h] * acc_sc[h] + jnp.dot(probs[h], v_ref[:, MLA_V * h:MLA_V * (h + 1)],
                                                            preferred_element_type=F32)

        @pl.when(j < i)
        def _():
            update(False)

        @pl.when(j == i)
        def _():
            update(True)
            for h in range(nh):
                out = acc_sc[h] * (1.0 / l_sc[h])
                o_ref[:, MLA_V * h:MLA_V * (h + 1)] = out
                yb_ref[:, MLA_V * h:MLA_V * (h + 1)] = out.astype(BF16)
                lse_ref[h] = m_sc[h] + jnp.log(l_sc[h])

    return _pcall(
        body, (q_cat, k_cat, v_b, y_all), name="mla_fwd", grid=(qi.shape[0],), prefetch=(qi, kj),
        in_specs=[pl.BlockSpec((nh, t, 256), lambda p, qi, kj: (0, qi[p], 0)),
                  pl.BlockSpec((nh, t, 256), lambda p, qi, kj: (0, kj[p], 0)),
                  pl.BlockSpec((t, nh * MLA_V), lambda p, qi, kj: (kj[p], 0)), ANY],
        out_specs=[pl.BlockSpec((t, nh * MLA_V), lambda p, qi, kj: (qi[p], 0)),
                   pl.BlockSpec((nh, t, 1), lambda p, qi, kj: (0, qi[p], 0)),
                   pl.BlockSpec((t, nh * MLA_V), lambda p, qi, kj: (qi[p], ycol))],
        out_shape=[jax.ShapeDtypeStruct((s, nh * MLA_V), F32), jax.ShapeDtypeStruct((nh, s, 1), F32),
                   jax.ShapeDtypeStruct(y_all.shape, y_all.dtype)],
        scratch_shapes=[pltpu.VMEM((nh, t, 1), F32), pltpu.VMEM((nh, t, 1), F32), pltpu.VMEM((nh, t, MLA_V), F32)],
        sem=("arbitrary",), comm=comm, io_alias={3: 2})


def _mla_bwd(q_cat, k_cat, v_b, y_b, lse, d_y, *, comm=()):
    nh, s, _ = q_cat.shape
    t = _tile(s, MLA_TILE)
    nt = s // t
    hp = 2
    wv = hp * MLA_V
    col0 = (SWA_Q_HEADS * SWA_HEAD_DIM) // wv
    qi, kj = _tile_pairs(nt, False)

    def body(qi_ref, kj_ref, q_ref, k_ref, v_ref, y_ref, lse_ref, dy_ref, dq_ref, dk_ref, dv_ref, dk_sc, dv_sc):
        step = pl.program_id(1)
        i, j = qi_ref[step], kj_ref[step]

        @pl.when(step == 0)
        def _():
            dq_ref[...] = jnp.zeros_like(dq_ref)

        @pl.when(i == j)
        def _():
            dk_sc[...] = jnp.zeros_like(dk_sc)
            dv_sc[...] = jnp.zeros_like(dv_sc)

        def update(diagonal):
            rows = pl.ds(pl.multiple_of(i * t, t), t)
            cols = [slice(MLA_V * h, MLA_V * (h + 1)) for h in range(hp)]
            do_b = [dy_ref[:, cols[h]].astype(BF16) for h in range(hp)]
            scores = [lax.dot_general(q_ref[h], k_ref[h], NT_DIMS, preferred_element_type=F32) for h in range(hp)]
            dps = [lax.dot_general(do_b[h], v_ref[:, cols[h]], NT_DIMS, preferred_element_type=F32) for h in range(hp)]
            p_b, ds_b = [], []
            for h in range(hp):
                p = jnp.exp(scores[h] * MLA_SCALE - lse_ref[h])
                if diagonal:
                    p = jnp.where(_diag_mask(t), p, 0.0)
                delta = jnp.sum(dy_ref[:, cols[h]] * y_ref[:, cols[h]], axis=-1, keepdims=True)
                p_b.append(p.astype(BF16))
                ds_b.append((p * (dps[h] - delta) * MLA_SCALE).astype(BF16))
            for h in range(hp):
                dv_sc[h] += lax.dot_general(p_b[h], do_b[h], TN_DIMS, preferred_element_type=F32)
                dk_sc[h] += lax.dot_general(ds_b[h], q_ref[h], TN_DIMS, preferred_element_type=F32)
                dq_ref[h, rows, :] += jnp.dot(ds_b[h], k_ref[h], preferred_element_type=F32)

        @pl.when(i > j)
        def _():
            update(False)

        @pl.when(i == j)
        def _():
            update(True)

        @pl.when(i == nt - 1)
        def _():
            dk_ref[...] = dk_sc[...]
            for h in range(hp):
                dv_ref[:, MLA_V * h:MLA_V * (h + 1)] = dv_sc[h]

    return _pcall(
        body, (q_cat, k_cat, v_b, y_b, lse, d_y), name="mla_bwd", grid=(nh // hp, qi.shape[0]), prefetch=(qi, kj),
        in_specs=[pl.BlockSpec((hp, t, 256), lambda g, p, qi, kj: (g, qi[p], 0)),
                  pl.BlockSpec((hp, t, 256), lambda g, p, qi, kj: (g, kj[p], 0)),
                  pl.BlockSpec((t, wv), lambda g, p, qi, kj: (kj[p], g)),
                  pl.BlockSpec((t, wv), lambda g, p, qi, kj: (qi[p], g)),
                  pl.BlockSpec((hp, t, 1), lambda g, p, qi, kj: (g, qi[p], 0)),
                  pl.BlockSpec((t, wv), lambda g, p, qi, kj: (qi[p], col0 + g))],
        out_specs=[pl.BlockSpec((hp, s, 256), lambda g, p, qi, kj: (g, 0, 0)),
                   pl.BlockSpec((hp, t, 256), lambda g, p, qi, kj: (g, kj[p], 0)),
                   pl.BlockSpec((t, wv), lambda g, p, qi, kj: (kj[p], g))],
        out_shape=[jax.ShapeDtypeStruct((nh, s, 256), F32), jax.ShapeDtypeStruct((nh, s, 256), F32),
                   jax.ShapeDtypeStruct((s, nh * MLA_V), F32)],
        scratch_shapes=[pltpu.VMEM((hp, t, 256), F32), pltpu.VMEM((hp, t, MLA_V), F32)],
        sem=("arbitrary", "arbitrary"), comm=comm)


MEM_SCALE = MEM_DIM ** -0.5


def _mem_kv_fwd(mem, g_mem, w_memkv, g_mk):
    m_len = mem.shape[0]

    def body(mem_ref, g_ref, w_ref, gk_ref, mn_ref, kv_ref, kn_ref, v_ref):
        mn, _, _ = _norm_full(mem_ref[...], g_ref[...])
        mn_b = mn.astype(BF16)
        mn_ref[...] = mn_b
        kv = jnp.dot(mn_b, w_ref[...], preferred_element_type=F32)
        kv_ref[...] = kv
        for h in range(MEM_HEADS):
            kn, _, _ = _norm_full(kv[:, 128 * h:128 * (h + 1)], gk_ref[...])
            kn_ref[:, 128 * h:128 * (h + 1)] = kn.astype(BF16)
        v_ref[...] = kv[:, 512:1024].astype(BF16)

    return pl.pallas_call(
        body, name="mem_kv_fwd",
        out_shape=[jax.ShapeDtypeStruct((m_len, D_MODEL), BF16), jax.ShapeDtypeStruct((m_len, 1024), F32),
                   jax.ShapeDtypeStruct((m_len, 512), BF16), jax.ShapeDtypeStruct((m_len, 512), BF16)],
        compiler_params=_params(),
    )(mem, g_mem, w_memkv, g_mk)


def _mem_kv_bwd(mem, g_mem, w_memkv, g_mk, mn_b, kv, d_kn, d_v):
    m_len = mem.shape[0]

    def body(mem_ref, g_ref, w_ref, gk_ref, mn_ref, kv_ref, dkn_ref, dv_ref, dw_ref, dgmem_ref, dgk_ref):
        parts = []
        dgk = jnp.zeros((1, LANES), F32)
        for h in range(MEM_HEADS):
            _, xhat, r = _norm_full(kv_ref[:, 128 * h:128 * (h + 1)], gk_ref[...])
            dx, dgh = _norm_full_bwd(dkn_ref[:, 128 * h:128 * (h + 1)], gk_ref[...], xhat, r)
            parts.append(dx)
            dgk = dgk + dgh
        d_kv = jnp.concatenate(parts + [dv_ref[...]], axis=1).astype(BF16)
        dw_ref[...] = lax.dot_general(mn_ref[...], d_kv, TN_DIMS, preferred_element_type=F32)
        d_mn = lax.dot_general(d_kv, w_ref[...], NT_DIMS, preferred_element_type=F32)
        _, xhat, _ = _norm_full(mem_ref[...], g_ref[...])
        dgmem_ref[...] = jnp.sum(d_mn * xhat, axis=0, keepdims=True)
        dgk_ref[...] = dgk

    return pl.pallas_call(
        body, name="mem_kv_bwd",
        out_shape=[jax.ShapeDtypeStruct((D_MODEL, 1024), F32), jax.ShapeDtypeStruct((1, D_MODEL), F32),
                   jax.ShapeDtypeStruct((1, LANES), F32)],
        compiler_params=_params(),
    )(mem, g_mem, w_memkv, g_mk, mn_b, kv, d_kn, d_v)


def _mem_probs(q_h, k_h):
    sc = lax.dot_general(q_h, k_h, NT_DIMS, preferred_element_type=F32) * MEM_SCALE
    e = jnp.exp(sc - jnp.max(sc, axis=-1, keepdims=True))
    return e * (1.0 / jnp.sum(e, axis=-1, keepdims=True))


def _mem_attn_fwd(qm, km, vm, y_all, *, tm=512):
    s = qm.shape[0]
    tm = _tile(s, tm)
    m_len = km.shape[0]
    ycol = (SWA_Q_HEADS * SWA_HEAD_DIM + MLA_HEADS * MLA_V) // 512

    def body(q_ref, k_ref, v_ref, _, o_ref, yb_ref):
        for h in range(MEM_HEADS):
            sl = slice(128 * h, 128 * (h + 1))
            p = _mem_probs(q_ref[:, sl], k_ref[:, sl])
            out = jnp.dot(p.astype(BF16), v_ref[:, sl], preferred_element_type=F32)
            o_ref[:, sl] = out
            yb_ref[:, sl] = out.astype(BF16)

    kvspec = pl.BlockSpec((m_len, 512), lambda i: (0, 0))
    return _pcall(
        body, (qm, km, vm, y_all), name="mem_attn_fwd", grid=(s // tm,),
        in_specs=[pl.BlockSpec((tm, 512), lambda i: (i, 0)), kvspec, kvspec, ANY],
        out_specs=[pl.BlockSpec((tm, 512), lambda i: (i, 0)), pl.BlockSpec((tm, 512), lambda i: (i, ycol))],
        out_shape=[jax.ShapeDtypeStruct((s, 512), F32), jax.ShapeDtypeStruct(y_all.shape, y_all.dtype)],
        sem=("parallel",), io_alias={3: 1})


def _mem_attn_bwd(qm, km, vm, y_m, d_y, *, tm=512):
    s = qm.shape[0]
    tm = _tile(s, tm)
    m_len = km.shape[0]
    col0 = (SWA_Q_HEADS * SWA_HEAD_DIM + MLA_HEADS * MLA_V) // 512

    def body(q_ref, k_ref, v_ref, y_ref, dy_ref, dq_ref, dk_ref, dv_ref):
        @pl.when(pl.program_id(0) == 0)
        def _():
            dk_ref[...] = jnp.zeros_like(dk_ref)
            dv_ref[...] = jnp.zeros_like(dv_ref)

        for h in range(MEM_HEADS):
            sl = slice(128 * h, 128 * (h + 1))
            q_h, k_h = q_ref[:, sl], k_ref[:, sl]
            do = dy_ref[:, sl]
            do_b = do.astype(BF16)
            p = _mem_probs(q_h, k_h)
            delta = jnp.sum(do * y_ref[:, sl], axis=-1, keepdims=True)
            dv_ref[:, sl] += lax.dot_general(p.astype(BF16), do_b, TN_DIMS, preferred_element_type=F32)
            dp = lax.dot_general(do_b, v_ref[:, sl], NT_DIMS, preferred_element_type=F32)
            ds_b = (p * (dp - delta) * MEM_SCALE).astype(BF16)
            dq_ref[:, sl] = jnp.dot(ds_b, k_h, preferred_element_type=F32)
            dk_ref[:, sl] += lax.dot_general(ds_b, q_h, TN_DIMS, preferred_element_type=F32)

    kvspec = pl.BlockSpec((m_len, 512), lambda i: (0, 0))
    row = pl.BlockSpec((tm, 512), lambda i: (i, 0))
    return pl.pallas_call(
        body, name="mem_attn_bwd", grid=(s // tm,),
        in_specs=[row, kvspec, kvspec, row, pl.BlockSpec((tm, 512), lambda i: (i, col0))],
        out_specs=[row, kvspec, kvspec],
        out_shape=[jax.ShapeDtypeStruct((s, 512), F32), jax.ShapeDtypeStruct((m_len, 512), F32),
                   jax.ShapeDtypeStruct((m_len, 512), F32)],
        compiler_params=_params(("arbitrary",)),
    )(qm, km, vm, y_m, d_y)


def _ffn_gate_up(fn, w_gate, w_up, *, tm=512, comm=()):
    s, d = fn.shape
    nsp, _, tf = w_gate.shape
    f = nsp * tf
    tm = _tile(s, tm)

    def body(x_ref, wg_ref, wu_ref, g_ref, u_ref, a_ref):
        x = x_ref[...]
        gate = jnp.dot(x, wg_ref[...], preferred_element_type=F32)
        up = jnp.dot(x, wu_ref[...], preferred_element_type=F32)
        g_ref[...] = gate.astype(BF16)
        u_ref[...] = up.astype(BF16)
        a_ref[...] = (gate * (1.0 / (1.0 + jnp.exp(-gate))) * up).astype(BF16)

    wspec = pl.BlockSpec((None, d, tf), lambda j, i: (j, 0, 0))
    ospec = pl.BlockSpec((tm, tf), lambda j, i: (i, j))
    osh = jax.ShapeDtypeStruct((s, f), BF16)
    return _pcall(body, (fn, w_gate, w_up), name="ffn_gate_up", grid=(nsp, s // tm),
                  in_specs=[pl.BlockSpec((tm, d), lambda j, i: (i, 0)), wspec, wspec],
                  out_specs=[ospec, ospec, ospec], out_shape=[osh, osh, osh], sem=("parallel", "parallel"), comm=comm)


def _ffn_bwd_act(d_out, w_down, gate, up, *, tm=1024, tf=1408, comm=()):
    s, d = d_out.shape
    f = w_down.shape[0]
    tm, tf = _tile(s, tm), _tile(f, tf)

    sub = tm // 4 if tm % 1024 == 0 else tm

    def body(do_ref, wd_ref, g_ref, u_ref, dg_ref, du_ref):
        groups = [slice(r, r + sub) for r in range(0, tm, sub)]
        parts = [lax.dot_general(do_ref[rows, :].astype(BF16), wd_ref[...], NT_DIMS, preferred_element_type=F32)
                 for rows in groups]
        for rows, d_act in zip(groups, parts):
            gate = g_ref[rows, :].astype(F32)
            sig = 1.0 / (1.0 + jnp.exp(-gate))
            du_ref[rows, :] = (d_act * (gate * sig)).astype(BF16)
            dg_ref[rows, :] = (d_act * u_ref[rows, :].astype(F32) * (sig * (1.0 + gate * (1.0 - sig)))).astype(BF16)

    ospec = pl.BlockSpec((tm, tf), lambda j, i: (i, j))
    osh = jax.ShapeDtypeStruct((s, f), BF16)
    return _pcall(
        body, (d_out, w_down, gate, up), name="ffn_bwd_act", grid=(f // tf, s // tm),
        in_specs=[pl.BlockSpec((tm, d), lambda j, i: (i, 0)), pl.BlockSpec((tf, d), lambda j, i: (j, 0)), ospec, ospec],
        out_specs=[ospec, ospec], out_shape=[osh, osh], sem=("parallel", "parallel"), comm=comm)


def _cols(g4):
    return jnp.concatenate([g4[k] for k in range(N_CHIPS)], axis=1)


def _full_w_in(g4):
    per = IN_WIDTH // N_CHIPS
    kr0 = 2304 - (N_CHIPS - 1) * per
    last = g4[N_CHIPS - 1]
    pad = jnp.zeros((last.shape[0], IN_PAD - IN_WIDTH), last.dtype)
    return jnp.concatenate([g4[0], g4[1], g4[2], last[:, :kr0], last[:, kr0 + 64:], last[:, kr0:kr0 + 64], pad], axis=1)


def _shards_w_in(dwp):
    per = IN_WIDTH // N_CHIPS
    kr0 = 2304 - (N_CHIPS - 1) * per
    last = jnp.concatenate([dwp[:, (N_CHIPS - 1) * per:2304], dwp[:, C_KR:C_KR + 64], dwp[:, 2304:C_KR]], axis=1)
    assert last.shape[1] == per and kr0 == 144
    return jnp.stack([dwp[:, per * k:per * (k + 1)] for k in range(N_CHIPS - 1)] + [last])


def _full_heads(g4, first):
    return jnp.concatenate([g4[k][:, :first] for k in range(N_CHIPS)] + [g4[k][:, first:] for k in range(N_CHIPS)], axis=1)


def _shards_heads(dwp, first, rest):
    base = N_CHIPS * first
    return jnp.stack([jnp.concatenate([dwp[:, first * k:first * (k + 1)], dwp[:, base + rest * k:base + rest * (k + 1)]], axis=1)
                      for k in range(N_CHIPS)])


def _rope_tables(pos):
    inv_freq = ROPE_THETA ** (-jnp.arange(0, MLA_ROPE, 2, dtype=F32) / MLA_ROPE)
    ang = pos.astype(F32)[:, None] * inv_freq
    cos, sin = jnp.cos(ang), jnp.sin(ang)
    return jnp.tile(cos, (1, 4)), jnp.concatenate([-sin, sin, -sin, sin], axis=1)


def _gain_table(sp):
    two = lambda v: jnp.tile(v, (1, 2))
    rows = [two(sp["swa_q_norm_g"]), two(sp["swa_k_norm_g"]), sp["mla_qn_norm_g"], two(sp["mla_qr_norm_g"]),
            sp["mla_kn_norm_g"], two(sp["mla_kr_norm_g"]), sp["mem_q_norm_g"], jnp.zeros((1, LANES), F32)]
    return jnp.concatenate(rows, axis=0)


CHIP_DISTANCES = (1, 2, 3)


def _place():
    x, y, c = lax.axis_index("x"), lax.axis_index("y"), lax.axis_index("c")
    return x, y, c, 2 * x + y


def _chip_at(x, y, d):
    px = 1 - x if d & 2 else x
    py = 1 - y if d & 1 else y
    return px, py, 2 * px + py


def _row_tile(rows, want=512, mult=8):
    t = min(rows, want)
    t -= t % mult
    while rows % t:
        t -= mult
    return t


def _cast_into_slot(w, meta, *, name, comm=()):
    rows, cols = w.shape
    tr = _row_tile(rows, 512, 16)

    def body(meta_ref, w_ref, o_ref):
        o_ref[...] = w_ref[...].astype(BF16)

    return _pcall(body, (w,), name=name, grid=(rows // tr,), prefetch=(meta,),
                  in_specs=[pl.BlockSpec((tr, cols), lambda i, m: (i, 0))],
                  out_specs=pl.BlockSpec((None, tr, cols), lambda i, m: (m[0], i, 0)),
                  out_shape=jax.ShapeDtypeStruct((N_CHIPS, rows, cols), BF16), sem=("parallel",), comm=comm)


def _remote(src, dst, ssem, rsem, i, device):
    return pltpu.make_async_remote_copy(src_ref=src, dst_ref=dst, send_sem=ssem.at[i], recv_sem=rsem.at[i],
                                        device_id=device, device_id_type=MESH)


def _symmetric_stage(ins, out_shapes, aliases, n_sem, copies):
    def issue(i_refs, o_refs, ssem, rsem):
        for send, _ in copies(i_refs, o_refs, ssem, rsem):
            send.start()

    def wait(i_refs, o_refs, ssem, rsem):
        pairs = copies(i_refs, o_refs, ssem, rsem)
        for _, arrival in pairs:
            arrival.wait_recv()
        for send, _ in pairs:
            send.wait_send()

    return _Stage(ins, out_shapes, aliases, n_sem, issue, wait)


def _gather_stage(slots, leg, part=(0, 1)):
    n = len(slots)
    shapes = [jax.ShapeDtypeStruct(s.shape, s.dtype) for s in slots]
    in_place = {w: w for w in range(n)}
    if not isinstance(leg, str):
        legs = list(leg)

        def copies(i_refs, o_refs, ssem, rsem):
            return [pr for k, (which, prt) in enumerate(legs)
                    for pr in _gather_stage(slots, which, prt).leg_copies(which, 3 * n * k)(i_refs, o_refs, ssem, rsem)]

        return _symmetric_stage(slots, shapes, in_place, 3 * n * len(legs), copies)

    def leg_copies(which, base):
        def copies(_, outs, ssem, rsem):
            x, y, c, k_me = _place()
            pairs = []
            for w in range(n):
                half = outs[w].shape[1] // 2
                r0, size = _window(half, part)
                slab = lambda k, cc, w=w, half=half, r0=r0, size=size: outs[w].at[k, pl.ds(cc * half + r0, size)]
                for d in CHIP_DISTANCES:
                    px, py, k_src = _chip_at(x, y, d)
                    i = base + 3 * w + d - 1
                    if which == "ici":
                        pairs.append((_remote(slab(k_me, c), slab(k_me, c), ssem, rsem, i, (px, py, c)),
                                      _remote(slab(k_src, c), slab(k_src, c), ssem, rsem, i, (x, y, c))))
                    else:
                        pairs.append((_remote(slab(k_src, c), slab(k_src, c), ssem, rsem, i, (x, y, 1 - c)),
                                      _remote(slab(k_src, 1 - c), slab(k_src, 1 - c), ssem, rsem, i, (x, y, c))))
            return pairs
        return copies

    if leg != "both":
        st = _symmetric_stage(slots, shapes, in_place, 3 * n, leg_copies(leg, 0))
        st.leg_copies = leg_copies
        return st
    ici = _symmetric_stage(slots, shapes, in_place, 6 * n, leg_copies("ici", 0))
    d2d = _symmetric_stage(slots, shapes, in_place, 6 * n, leg_copies("d2d", 3 * n))

    def mid(*refs):
        ici.wait(*refs)
        d2d.issue(*refs)

    return _Stage(slots, shapes, in_place, 6 * n, ici.issue, d2d.wait, mid)


def _halves_stage(grads):
    n = len(grads)

    def copies(ins, outs, ssem, rsem):
        x, y, c, _ = _place()
        pairs = []
        for w in range(n):
            half = ins[w].shape[1] // 2
            pairs.append((_remote(ins[w].at[:, pl.ds((1 - c) * half, half)], outs[w], ssem, rsem, w, (x, y, 1 - c)),
                          _remote(outs[w], outs[w], ssem, rsem, w, (x, y, c))))
        return pairs

    shapes = [jax.ShapeDtypeStruct((N_CHIPS, g.shape[1] // 2, g.shape[2]), g.dtype) for g in grads]
    return _symmetric_stage(grads, shapes, {}, n, copies)


def _window(rows, part):
    idx, count = part
    size = rows // count
    assert size * count == rows and size % 16 == 0, (rows, part)
    return idx * size, size


def _chips_stage(parts, part=(0, 1), into=None):
    n = len(parts)

    def copies(ins, outs, ssem, rsem):
        x, y, c, _ = _place()
        pairs = []
        for w in range(n):
            r0, size = _window(ins[w].shape[1], part)
            for d in CHIP_DISTANCES:
                px, py, _ = _chip_at(x, y, d)
                i = 3 * w + d - 1
                land = outs[w].at[d - 1, pl.ds(r0, size)]
                pairs.append((_remote(ins[w].at[d - 1, pl.ds(r0, size)], land, ssem, rsem, i, (px, py, c)),
                              _remote(land, land, ssem, rsem, i, (x, y, c))))
        return pairs

    shapes = [jax.ShapeDtypeStruct(p.shape, p.dtype) for p in parts]
    if into is None:
        return _symmetric_stage(parts, shapes, {}, 3 * n, copies)
    return _symmetric_stage(list(parts) + list(into), shapes, {n + w: w for w in range(n)}, 3 * n, copies)


def _swap_stage(totals):
    n = len(totals)

    def copies(ins, outs, ssem, rsem):
        x, y, c, _ = _place()
        return [(_remote(ins[w], outs[w], ssem, rsem, w, (x, y, 1 - c)),
                 _remote(outs[w], outs[w], ssem, rsem, w, (x, y, c))) for w in range(n)]

    shapes = [jax.ShapeDtypeStruct(t.shape, t.dtype) for t in totals]
    return _symmetric_stage(totals, shapes, {}, n, copies)


def _run_stages(stages, *, name):
    n_ins = [len(st.ins) for st in stages]
    n_outs = [len(st.out_shapes) for st in stages]
    tot_in, tot_out = sum(n_ins), sum(n_outs)
    aliases, i0, o0 = {}, 0, 0
    for st, ni, no in zip(stages, n_ins, n_outs):
        aliases.update({i0 + a: o0 + b for a, b in st.aliases.items()})
        i0, o0 = i0 + ni, o0 + no

    def body(*refs):
        sems = refs[tot_in + tot_out:]
        for what in ("issue", "wait"):
            i0, o0 = 0, tot_in
            for k, (st, ni, no) in enumerate(zip(stages, n_ins, n_outs)):
                getattr(st, what)(refs[i0:i0 + ni], refs[o0:o0 + no], sems[2 * k], sems[2 * k + 1])
                i0, o0 = i0 + ni, o0 + no

    sem = pltpu.SemaphoreType.DMA
    res = pl.pallas_call(
        body, name=name, in_specs=[ANY] * tot_in, out_specs=[ANY] * tot_out,
        out_shape=[s for st in stages for s in st.out_shapes], input_output_aliases=aliases,
        scratch_shapes=[sem((st.n_sem,)) for st in stages for _ in range(2)],
    )(*[a for st in stages for a in st.ins])
    outs, o0 = [], 0
    for no in n_outs:
        outs.append(list(res[o0:o0 + no]))
        o0 += no
    return outs


def _add_pair(meta, g4, recv, *, name):
    nsh, rows, cols = g4.shape
    half = rows // 2
    tr = _row_tile(half, 128 if cols > 1024 else 256, 16)
    nt = half // tr

    def body(meta_ref, g0, g1, g2, g3, r0, r1, r2, r3, own_ref, oth_ref):
        own_ref[...] = g0[...] + r0[...]
        for d, (g, r) in enumerate(((g1, r1), (g2, r2), (g3, r3))):
            oth_ref[d] = (g[...] + r[...]).astype(BF16)

    blk = (None, tr, cols)
    gspec = lambda d: pl.BlockSpec(blk, lambda i, m: (jnp.bitwise_xor(m[0], d), m[1] * nt + i, 0))
    rspec = lambda d: pl.BlockSpec(blk, lambda i, m: (jnp.bitwise_xor(m[0], d), i, 0))
    grid_spec = pltpu.PrefetchScalarGridSpec(
        num_scalar_prefetch=1, grid=(nt,),
        in_specs=[gspec(d) for d in range(nsh)] + [rspec(d) for d in range(nsh)],
        out_specs=[pl.BlockSpec((tr, cols), lambda i, m: (i, 0)), pl.BlockSpec((3, tr, cols), lambda i, m: (0, i, 0))])
    return pl.pallas_call(
        body, name=name, grid_spec=grid_spec,
        out_shape=[jax.ShapeDtypeStruct((half, cols), F32), jax.ShapeDtypeStruct((3, half, cols), BF16)],
        compiler_params=_params(("parallel",)),
    )(meta, g4, g4, g4, g4, recv, recv, recv, recv)


def _add_chips(own, recv, *, name):
    half, cols = own.shape
    tr = _row_tile(half, 256, 16)

    def body(p_ref, r_ref, o_ref):
        o_ref[...] = ((p_ref[...] + r_ref[0].astype(F32)) + r_ref[1].astype(F32)) + r_ref[2].astype(F32)

    return pl.pallas_call(
        body, name=name, grid=(half // tr,),
        in_specs=[pl.BlockSpec((tr, cols), lambda i: (i, 0)), pl.BlockSpec((3, tr, cols), lambda i: (0, i, 0))],
        out_specs=pl.BlockSpec((tr, cols), lambda i: (i, 0)),
        out_shape=jax.ShapeDtypeStruct((half, cols), F32),
        compiler_params=_params(("parallel",)),
    )(own, recv)


def _adamw_math(w, g, m, v):
    m = ADAM_B1 * m + (1.0 - ADAM_B1) * g
    v = ADAM_B2 * v + (1.0 - ADAM_B2) * (g * g)
    m_hat = m / (1.0 - ADAM_B1 ** ADAM_STEP)
    v_hat = v / (1.0 - ADAM_B2 ** ADAM_STEP)
    delta = -ADAM_LR * (m_hat / (jnp.sqrt(v_hat) + ADAM_EPS) + ADAM_WD * w)
    return delta, m, v


def _adamw(meta, w, g_mine, g_theirs, m, v, *, name):
    rows, cols = w.shape
    half = rows // 2
    tr = _row_tile(half, 256)
    nt = half // tr

    def body(meta_ref, w_ref, a_ref, b_ref, m_ref, v_ref, g_ref, d_ref, mo_ref, vo_ref):
        is_mine = (pl.program_id(0) // nt) == meta_ref[1]
        g = jnp.where(is_mine, a_ref[...], b_ref[...])
        g_ref[...] = g
        d_ref[...], mo_ref[...], vo_ref[...] = _adamw_math(w_ref[...], g, m_ref[...], v_ref[...])

    blk = pl.BlockSpec((tr, cols), lambda i, mt: (i, 0))
    mine = pl.BlockSpec((tr, cols), lambda i, mt: (jnp.where(i // nt == mt[1], i % nt, 0), 0))
    theirs = pl.BlockSpec((tr, cols), lambda i, mt: (jnp.where(i // nt == mt[1], 0, i % nt), 0))
    sh = jax.ShapeDtypeStruct((rows, cols), F32)
    grid_spec = pltpu.PrefetchScalarGridSpec(
        num_scalar_prefetch=1, grid=(rows // tr,),
        in_specs=[blk, mine, theirs, blk, blk], out_specs=[blk] * 4)
    return pl.pallas_call(
        body, name=name, grid_spec=grid_spec, out_shape=[sh] * 4,
        compiler_params=_params(("arbitrary",)),
    )(meta, w, g_mine, g_theirs, m, v)


N_DEVICES = 8


def _small_step(g_pack, w_pack, m_pack, v_pack):
    rows = g_pack.shape[0]

    def body(g_ref, w_ref, m_ref, v_ref, sum_ref, d_ref, mo_ref, vo_ref, slots, ssem, rsem):
        x, y, c, _ = _place()
        me = 4 * x + 2 * y + c
        slots[me] = g_ref[...]
        copies = []
        for r in range(1, N_DEVICES):
            px = 1 - x if r & 4 else x
            py = 1 - y if r & 2 else y
            pc = 1 - c if r & 1 else c
            copies.append(pltpu.make_async_remote_copy(
                src_ref=g_ref, dst_ref=slots.at[me], send_sem=ssem.at[r - 1], recv_sem=rsem.at[r - 1],
                device_id=(px, py, pc), device_id_type=MESH))
        for cp in copies:
            cp.start()
        for r in range(1, N_DEVICES):
            src = jnp.bitwise_xor(me, r)
            pltpu.make_async_remote_copy(
                src_ref=g_ref, dst_ref=slots.at[src], send_sem=ssem.at[r - 1], recv_sem=rsem.at[r - 1],
                device_id=(x, y, c), device_id_type=MESH).wait_recv()
        for cp in copies:
            cp.wait_send()
        total = slots[0]
        for k in range(1, N_DEVICES):
            total = total + slots[k]
        sum_ref[...] = total
        d_ref[...], mo_ref[...], vo_ref[...] = _adamw_math(w_ref[...], total, m_ref[...], v_ref[...])

    sh = jax.ShapeDtypeStruct((rows, LANES), F32)
    vm = pl.BlockSpec(memory_space=pltpu.VMEM)
    return pl.pallas_call(
        body, name="small_allreduce_adamw",
        in_specs=[vm] * 4, out_specs=[vm] * 4, out_shape=[sh] * 4,
        scratch_shapes=[pltpu.VMEM((N_DEVICES, rows, LANES), F32),
                        pltpu.SemaphoreType.DMA((N_DEVICES - 1,)), pltpu.SemaphoreType.DMA((N_DEVICES - 1,))],
    )(g_pack, w_pack, m_pack, v_pack)


WEIGHTS = ("attn_norm_g", "w_in", "swa_q_norm_g", "swa_k_norm_g", "swa_sinks", "mla_cq_norm_g", "mla_ckv_norm_g",
           "w_uq", "w_ukv", "mla_qn_norm_g", "mla_qr_norm_g", "mla_kn_norm_g", "mla_kr_norm_g", "mem_norm_g",
           "w_mem_kv", "mem_q_norm_g", "mem_k_norm_g", "w_out", "ffn_norm_g", "w_gate", "w_up", "w_down")
BIG = ("w_in", "w_uq", "w_ukv", "w_mem_kv", "w_out", "w_gate", "w_up", "w_down")
SMALL = tuple(n for n in WEIGHTS if n not in BIG)
PACK_UNIT = 8 * LANES


def _pack(parts):
    out = []
    for p in parts:
        n = p.shape[1]
        padded = -(-n // PACK_UNIT) * PACK_UNIT
        out.append(jnp.pad(p, ((0, 0), (0, padded - n))).reshape(padded // LANES, LANES))
    return jnp.concatenate(out, axis=0)


def _unpack(buf, sizes):
    out, row = [], 0
    for n in sizes:
        rows = -(-n // PACK_UNIT) * 8
        out.append(buf[row:row + rows].reshape(1, rows * LANES)[:, :n])
        row += rows
    return out


def kernel(x, mem, positions, attn_norm_g, w_in, swa_q_norm_g, swa_k_norm_g, swa_sinks, mla_cq_norm_g, mla_ckv_norm_g, w_uq, w_ukv, mla_qn_norm_g, mla_qr_norm_g, mla_kn_norm_g, mla_kr_norm_g, mem_norm_g, w_mem_kv, mem_q_norm_g, mem_k_norm_g, w_out, ffn_norm_g, w_gate, w_up, w_down, loss_target, m_attn_norm_g, m_w_in, m_swa_q_norm_g, m_swa_k_norm_g, m_swa_sinks, m_mla_cq_norm_g, m_mla_ckv_norm_g, m_w_uq, m_w_ukv, m_mla_qn_norm_g, m_mla_qr_norm_g, m_mla_kn_norm_g, m_mla_kr_norm_g, m_mem_norm_g, m_w_mem_kv, m_mem_q_norm_g, m_mem_k_norm_g, m_w_out, m_ffn_norm_g, m_w_gate, m_w_up, m_w_down, v_attn_norm_g, v_w_in, v_swa_q_norm_g, v_swa_k_norm_g, v_swa_sinks, v_mla_cq_norm_g, v_mla_ckv_norm_g, v_w_uq, v_w_ukv, v_mla_qn_norm_g, v_mla_qr_norm_g, v_mla_kn_norm_g, v_mla_kr_norm_g, v_mem_norm_g, v_w_mem_kv, v_mem_q_norm_g, v_mem_k_norm_g, v_w_out, v_ffn_norm_g, v_w_gate, v_w_up, v_w_down):
    given = dict(locals())
    wts = {n: given[n] for n in WEIGHTS}
    mom_m = {n: given["m_" + n] for n in WEIGHTS}
    mom_v = {n: given["v_" + n] for n in WEIGHTS}

    mx, my, mc = lax.axis_index("x"), lax.axis_index("y"), lax.axis_index("c")
    meta = jnp.stack([2 * mx + my, mc]).astype(jnp.int32)
    x, mem, pos, target = x[0], mem[0], positions[0], loss_target[0]
    sp = {n: wts[n] for n in SMALL}
    s = x.shape[0]
    cos_t, sin_t = _rope_tables(pos)
    pos_f = pos.astype(F32)
    pos_col, pos_row = pos_f.reshape(s, 1), pos_f.reshape(1, s)
    g128 = _gain_table(sp)
    sinks = sp["swa_sinks"].reshape(SWA_Q_HEADS)
    gcq, gckv = sp["mla_cq_norm_g"], sp["mla_ckv_norm_g"]
    gs = {}

    slot = {n: _cast_into_slot(wts[n][0], meta, name="cast_" + n) for n in BIG if n not in ("w_gate", "w_up", "w_down")}
    first = [slot["w_in"], slot["w_uq"], slot["w_ukv"]]
    slot["w_gate"], [first] = _cast_into_slot(wts["w_gate"][0], meta, name="cast_w_gate",
                                              comm=[_gather_stage(first, "ici", (0, 4))])
    slot["w_up"], [first] = _cast_into_slot(wts["w_up"][0], meta, name="cast_w_up",
                                            comm=[_gather_stage(first, [("ici", (1, 4)), ("d2d", (0, 4))])])
    slot["w_down"], [first] = _cast_into_slot(wts["w_down"][0], meta, name="cast_w_down",
                                              comm=[_gather_stage(first, [("ici", (2, 4)), ("d2d", (1, 4))])])
    hn, [first] = _rms_fwd(x, sp["attn_norm_g"], name="attn_norm_fwd",
                           comm=[_gather_stage(first, [("ici", (3, 4)), ("d2d", (2, 4))])])
    [first] = _run_stages([_gather_stage(first, "d2d", (3, 4))], name="gather_first_last_d2d")
    w_in_f, w_uq_f, w_ukv_f = _full_w_in(first[0]), _full_heads(first[1], MLA_NOPE), _full_heads(first[2], MLA_NOPE)

    proj, [mid] = _matmul(hn, w_in_f, name="in_proj",
                          comm=[_gather_stage([slot["w_mem_kv"], slot["w_out"]], "ici")])
    (qa, ka, va, q_cat, k_cat, v_b, qm), [mid, wg] = _attn_prep_fwd(
        proj, g128, gcq, gckv, w_uq_f, w_ukv_f, cos_t, sin_t,
        comm=[_gather_stage(mid, "d2d"), _gather_stage([slot["w_gate"]], "ici", (0, 4))])
    w_mem_kv_f = mid[0].reshape(D_MODEL, 2 * MEM_HEADS * MEM_DIM)
    w_out_f = mid[1].reshape(D_MODEL, D_MODEL)
    mn_b, kv_m, km, vm = _mem_kv_fwd(mem, sp["mem_norm_g"], w_mem_kv_f, sp["mem_k_norm_g"])
    (y_a, y), [wg] = _swa_fwd(qa, ka, va, pos_col, pos_row, sinks,
                              comm=[_gather_stage(wg, [("ici", (k, 4)) for k in (1, 2, 3)])])
    (y_b, lse, y), [wu, wg] = _mla_fwd(
        q_cat, k_cat, v_b, y, comm=[_gather_stage([slot["w_up"]], "ici"), _gather_stage(wg, "d2d")])
    y_m, y = _mem_attn_fwd(qm, km, vm, y)
    h1, [wu] = _matmul(y, w_out_f, add=x, name="out_proj", comm=[_gather_stage(wu, "d2d")])
    w_gate_f, w_up_f = wg[0], wu[0]
    fn = _rms_fwd(h1, sp["ffn_norm_g"], name="ffn_norm_fwd")
    (gate, up, act), [wd] = _ffn_gate_up(fn, w_gate_f, w_up_f, comm=[_gather_stage([slot["w_down"]], "both")])
    w_down_f = wd[0].reshape(D_FF, D_MODEL)
    d_out, d_out_b, loss_tile = _matmul(act, w_down_f, add=h1, name="down_proj", tm=512, tk=D_FF, loss_target=target)

    add_pair = lambda n, g4, r: _add_pair(meta, g4, r, name="grad_add_pair_" + n)
    add_chips = lambda n, own, r: _add_chips(own, r, name="grad_add_chips_" + n)
    mine, theirs = {}, {}

    dw_down = _matmul(act, d_out_b, ta=True, name="dw_down", tm=1408, tn=1024, tk=2048)
    dw_down = dw_down.reshape(N_CHIPS, D_FF // N_CHIPS, D_MODEL)
    (d_gate, d_up), [[r]] = _ffn_bwd_act(d_out_b, w_down_f, gate, up, comm=[_halves_stage([dw_down])])
    own_d, oth_d = add_pair("w_down", dw_down, r)
    dw_gate, [rd] = _matmul(fn, d_gate, ta=True, name="dw_gate", tk=2048, tn=D_FF // N_CHIPS, out_split=N_CHIPS,
                            comm=[_chips_stage([oth_d], (0, 2))])
    dw_up, [[r], rd] = _matmul(fn, d_up, ta=True, name="dw_up", tk=2048, tn=D_FF // N_CHIPS, out_split=N_CHIPS,
                               comm=[_halves_stage([dw_gate]), _chips_stage([oth_d], (1, 2), into=rd)])
    mine["w_down"] = add_chips("w_down", own_d, rd[0])
    own_g, oth_g = add_pair("w_gate", dw_gate, r)
    d_fn, [rg, [theirs["w_down"]]] = _matmul(
        d_gate, w_gate_f, tb=True, b_split=True, name="dfn_gate", tm=512,
        comm=[_chips_stage([oth_g], (0, 2)), _swap_stage([mine["w_down"]])])
    d_fn, [[r], rg] = _matmul(d_up, w_up_f, tb=True, b_split=True, add=d_fn, name="dfn_up", tm=512,
                              comm=[_halves_stage([dw_up]), _chips_stage([oth_g], (1, 2), into=rg)])
    mine["w_gate"] = add_chips("w_gate", own_g, rg[0])
    own_u, oth_u = add_pair("w_up", dw_up, r)
    d_h1, d_h1_b, gs["ffn_norm_g"] = _rms_bwd(d_fn, h1, sp["ffn_norm_g"], d_out, name="ffn_norm_bwd")
    dw_out, [[theirs["w_gate"]]] = _matmul(y, d_h1_b, ta=True, name="dw_out", tk=2048,
                                           comm=[_swap_stage([mine["w_gate"]])])
    dw_out = dw_out.reshape(N_CHIPS, D_MODEL // N_CHIPS, D_MODEL)
    d_y, [[r]] = _matmul(d_h1_b, w_out_f, tb=True, name="dy", comm=[_halves_stage([dw_out])])
    own_o, oth_o = add_pair("w_out", dw_out, r)
    (d_qa, d_ka, d_va, d_sink), [ru] = _swa_bwd(qa, ka, va, pos_col, pos_row, sinks, y_a, d_y,
                                                comm=[_chips_stage([oth_u], (0, 2))])
    (d_qcat, d_kcat, d_vb), [ru, [r]] = _mla_bwd(
        q_cat, k_cat, v_b, y_b, lse, d_y, comm=[_chips_stage([oth_u], (1, 2), into=ru), _chips_stage([oth_o])])
    mine["w_up"] = add_chips("w_up", own_u, ru[0])
    mine["w_out"] = add_chips("w_out", own_o, r)
    d_qm, d_km, d_vm = _mem_attn_bwd(qm, km, vm, y_m, d_y)
    (d_proj, dw_uq, dw_ukv, dg128, gs["mla_cq_norm_g"], gs["mla_ckv_norm_g"]), [[theirs["w_up"], theirs["w_out"]]] = \
        _attn_prep_bwd(proj, g128, gcq, gckv, w_uq_f, w_ukv_f, cos_t, sin_t, d_qa, d_ka, d_va, d_qcat, d_kcat, d_vb,
                       d_qm, comm=[_swap_stage([mine["w_up"], mine["w_out"]])])
    dw_mem_kv, gs["mem_norm_g"], gs["mem_k_norm_g"] = _mem_kv_bwd(
        mem, sp["mem_norm_g"], w_mem_kv_f, sp["mem_k_norm_g"], mn_b, kv_m, d_km, d_vm)
    late = ("w_uq", "w_ukv", "w_mem_kv")
    late_g = [_shards_heads(dw_uq, MLA_NOPE, MLA_ROPE), _shards_heads(dw_ukv, MLA_NOPE, MLA_V),
              dw_mem_kv.reshape(N_CHIPS, D_MODEL // N_CHIPS, -1)]
    dw_in, [rs] = _matmul(hn, d_proj, ta=True, name="dw_in", tk=2048, comm=[_halves_stage(late_g)])
    late_sums = [add_pair(n, g4, r) for n, g4, r in zip(late, late_g, rs)]
    dw_in = _shards_w_in(dw_in)
    d_hn, [rs, [r]] = _matmul(d_proj, w_in_f, tb=True, name="dhn", tk=1536,
                              comm=[_chips_stage([oth for _, oth in late_sums]), _halves_stage([dw_in])])
    for n, (own, _), r_n in zip(late, late_sums, rs):
        mine[n] = add_chips(n, own, r_n)
    own_i, oth_i = add_pair("w_in", dw_in, r)
    (grad_x, _, gs["attn_norm_g"]), [[r], late_theirs] = _rms_bwd(
        d_hn, x, sp["attn_norm_g"], d_h1, name="attn_norm_bwd",
        comm=[_chips_stage([oth_i]), _swap_stage([mine[n] for n in late])])
    theirs.update(zip(late, late_theirs))
    mine["w_in"] = add_chips("w_in", own_i, r)
    [[theirs["w_in"]]] = _run_stages([_swap_stage([mine["w_in"]])], name="grad_swap_w_in")

    fold = lambda r: r[:, :64] + r[:, 64:]
    gs["swa_q_norm_g"] = fold(dg128[G_SWA_Q:G_SWA_Q + 1])
    gs["swa_k_norm_g"] = fold(dg128[G_SWA_K:G_SWA_K + 1])
    gs["mla_qn_norm_g"] = dg128[G_QN:G_QN + 1]
    gs["mla_qr_norm_g"] = fold(dg128[G_QR:G_QR + 1])
    gs["mla_kn_norm_g"] = dg128[G_KN:G_KN + 1]
    gs["mla_kr_norm_g"] = fold(dg128[G_KR:G_KR + 1])
    gs["mem_q_norm_g"] = dg128[G_MQ:G_MQ + 1]
    gs["swa_sinks"] = d_sink[:, :SWA_Q_HEADS]

    grad, delta, new_m, new_v = {}, {}, {}, {}
    for n in BIG:
        g2, d, m2, v2 = _adamw(meta, wts[n][0], mine[n], theirs[n], mom_m[n][0], mom_v[n][0], name="adamw_" + n)
        grad[n], delta[n], new_m[n], new_v[n] = g2[None], d[None], m2[None], v2[None]

    sizes = [wts[n].shape[1] for n in SMALL]
    zero = jnp.zeros((1, LANES), F32)
    packs = _small_step(_pack([gs[n] for n in SMALL] + [loss_tile]), _pack([wts[n] for n in SMALL] + [zero]),
                        _pack([mom_m[n] for n in SMALL] + [zero]), _pack([mom_v[n] for n in SMALL] + [zero]))
    for store, buf in zip((grad, delta, new_m, new_v), packs):
        for n, val in zip(SMALL, _unpack(buf, sizes)):
            store[n] = val
    loss = _unpack(packs[0], sizes + [LANES])[-1][0, 0]

    return (loss, grad_x[None], *[grad[n] for n in WEIGHTS], *[delta[n] for n in WEIGHTS],
            *[new_m[n] for n in WEIGHTS], *[new_v[n] for n in WEIGHTS])
```

```python
import functools
import math

import jax
import jax.numpy as jnp
from jax import lax
from jax.experimental import pallas as pl
from jax.experimental.pallas import tpu as pltpu

F32 = jnp.float32
BF16 = jnp.bfloat16

D_MODEL = 2048
BLOCK = 128
EPS = 1e-6
NEG_INF = -1e30
SWA_Q_HEADS = 16
SWA_KV_HEADS = 2
SWA_HEAD_DIM = 64
MLA_HEADS = 4
MLA_RANK = 512
MLA_NOPE = 128
MLA_ROPE = 64
MLA_V = 128
ROPE_THETA = 10000.0
MEM_HEADS = 4
MEM_DIM = 128
D_FF = 5632
IN_WIDTH = 2880
IN_PAD = 3072
N_CHIPS = 4

ADAM_LR = 0.001
ADAM_B1 = 0.9
ADAM_B2 = 0.999
ADAM_EPS = 1e-08
ADAM_WD = 0.01
ADAM_STEP = 10

VMEM_LIMIT_BYTES = 56 * 1024 * 1024
LANES = 128

MESH = pl.DeviceIdType.MESH


def _params(sem=None, **kw):
    return pltpu.CompilerParams(dimension_semantics=sem, vmem_limit_bytes=VMEM_LIMIT_BYTES, **kw)


def _tile(n, want):
    if n <= want:
        return n
    t = want - want % LANES
    while t > 0:
        if n % t == 0:
            return t
        t -= LANES
    return n


ANY = pl.BlockSpec(memory_space=pl.ANY)


class _Stage:
    def __init__(self, ins, out_shapes, aliases, n_sem, issue, wait, mid=None):
        self.ins, self.out_shapes, self.aliases, self.n_sem = list(ins), list(out_shapes), dict(aliases), n_sem
        self.issue, self.wait, self.mid = issue, wait, mid


def _pcall(body, args, *, name, grid, in_specs, out_specs, out_shape, scratch_shapes=(), sem=None, comm=(),
           prefetch=(), io_alias=None):
    multi = isinstance(out_shape, (list, tuple))
    out_specs_l = list(out_specs) if multi else [out_specs]
    out_shape_l = list(out_shape) if multi else [out_shape]
    npf = len(prefetch)
    own_aliases = {npf + a: o for a, o in (io_alias or {}).items()}

    def call(fn, in_specs_, out_specs_, out_shape_, scratch_, operands, sem_, aliases=None):
        kw = dict(name=name, out_shape=out_shape_, compiler_params=_params(sem_))
        if aliases:
            kw["input_output_aliases"] = aliases
        if npf:
            spec = pltpu.PrefetchScalarGridSpec(num_scalar_prefetch=npf, grid=grid, in_specs=in_specs_,
                                                out_specs=out_specs_, scratch_shapes=scratch_)
            return pl.pallas_call(fn, grid_spec=spec, **kw)(*prefetch, *operands)
        return pl.pallas_call(fn, grid=grid, in_specs=in_specs_, out_specs=out_specs_, scratch_shapes=scratch_,
                              **kw)(*operands)

    if not comm:
        return call(body, list(in_specs), out_specs, out_shape, list(scratch_shapes), args, sem, own_aliases)
    n_in, n_out, n_scr = len(in_specs), len(out_specs_l), len(scratch_shapes)
    cins = [a for st in comm for a in st.ins]
    couts = [s for st in comm for s in st.out_shapes]
    aliases, ci, co = dict(own_aliases), 0, 0
    for st in comm:
        for a_i, o_i in st.aliases.items():
            aliases[npf + n_in + ci + a_i] = n_out + co + o_i
        ci, co = ci + len(st.ins), co + len(st.out_shapes)

    def wrapped(*refs):
        pre = refs[:npf]
        p = npf
        ins = refs[p:p + n_in]; p += n_in
        cin_refs = refs[p:p + len(cins)]; p += len(cins)
        outs = refs[p:p + n_out]; p += n_out
        cout_refs = refs[p:p + len(couts)]; p += len(couts)
        scr = refs[p:p + n_scr]; p += n_scr
        sems = refs[p:]
        first = functools.reduce(jnp.logical_and, [pl.program_id(a) == 0 for a in range(len(grid))])
        last = functools.reduce(jnp.logical_and, [pl.program_id(a) == grid[a] - 1 for a in range(len(grid))])

        def each(what):
            i, o = 0, 0
            for k, st in enumerate(comm):
                fn = getattr(st, what)
                if fn is not None:
                    fn(cin_refs[i:i + len(st.ins)], cout_refs[o:o + len(st.out_shapes)], sems[2 * k], sems[2 * k + 1])
                i, o = i + len(st.ins), o + len(st.out_shapes)

        @pl.when(first)
        def _():
            each("issue")

        if any(st.mid is not None for st in comm):
            n_steps = math.prod(grid)
            assert n_steps >= 4, "a two-leg stage needs a carrier with several grid steps"
            lin = functools.reduce(lambda acc, a: acc * grid[a] + pl.program_id(a), range(len(grid)), 0)

            @pl.when(lin == (3 * n_steps) // 4)
            def _():
                each("mid")

        body(*pre, *ins, *outs, *scr)

        @pl.when(last)
        def _():
            each("wait")

    sem_scr = [pltpu.SemaphoreType.DMA((st.n_sem,)) for st in comm for _ in range(2)]
    res = call(wrapped, list(in_specs) + [ANY] * len(cins), out_specs_l + [ANY] * len(couts), out_shape_l + couts,
               list(scratch_shapes) + sem_scr, (*args, *cins), ("arbitrary",) * len(grid), aliases)
    normal = list(res[:n_out])
    stage_outs, o = [], n_out
    for st in comm:
        stage_outs.append(list(res[o:o + len(st.out_shapes)]))
        o += len(st.out_shapes)
    return (normal if multi else normal[0]), stage_outs


def _matmul(a, b, *, name, ta=False, tb=False, add=None, out_dtype=F32, tm=1024, tn=1024, tk=2048,
            b_split=False, out_split=0, comm=(), loss_target=None):
    if ta:
        kdim, m = a.shape
    else:
        m, kdim = a.shape
    if b_split:
        assert tb
        nsp, n, kb = b.shape
        kb = kb * nsp
    elif tb:
        n, kb = b.shape
    else:
        kb, n = b.shape
    assert kb == kdim, (a.shape, b.shape, ta, tb)
    if b_split:
        tk = kdim
    if out_split:
        tn = _tile(n // out_split, tn)
    tm, tn, tk = _tile(m, tm), _tile(n, tn), _tile(kdim, tk)
    nk = kdim // tk
    dims = (((0 if ta else 1,), (1 if tb else 0,)), ((), ()))

    def product(a_ref, b_ref):
        if not b_split:
            return lax.dot_general(a_ref[...].astype(BF16), b_ref[...].astype(BF16), dims, preferred_element_type=F32)
        per = kdim // nsp
        return sum(lax.dot_general(a_ref[:, per * c:per * (c + 1)].astype(BF16), b_ref[c].astype(BF16), dims,
                                   preferred_element_type=F32) for c in range(nsp))

    def body(*refs):
        a_ref, b_ref = refs[:2]
        add_ref = refs[2] if add is not None else None
        n_in = 2 + (add is not None) + (loss_target is not None)
        o_ref = refs[n_in]

        def finish(r):
            if add_ref is not None:
                r = r + add_ref[...].astype(F32)
            if loss_target is None:
                o_ref[...] = r.astype(o_ref.dtype)
                return
            db_ref, l_ref = refs[n_in + 1], refs[n_in + 2]
            err = r - refs[n_in - 1][...]
            d_out = err * (1.0 / n)
            o_ref[...] = d_out
            db_ref[...] = d_out.astype(BF16)
            part = jnp.broadcast_to((0.5 / n) * jnp.sum(jnp.sum(err * err, axis=-1, keepdims=True), axis=0, keepdims=True),
                                    (1, LANES))
            first = jnp.logical_and(pl.program_id(0) == 0, pl.program_id(1) == 0)

            @pl.when(first)
            def _():
                l_ref[...] = part

            @pl.when(jnp.logical_not(first))
            def _():
                l_ref[...] += part

        if nk == 1:
            finish(product(a_ref, b_ref))
            return
        acc_ref = refs[-1]
        k = pl.program_id(2)
        part = product(a_ref, b_ref)

        @pl.when(k == 0)
        def _():
            acc_ref[...] = part

        @pl.when(k > 0)
        def _():
            acc_ref[...] += part

        @pl.when(k == nk - 1)
        def _():
            finish(acc_ref[...])

    a_spec = pl.BlockSpec((tk, tm), lambda i, j, k: (k, i)) if ta else pl.BlockSpec((tm, tk), lambda i, j, k: (i, k))
    if b_split:
        b_spec = pl.BlockSpec((nsp, tn, kdim // nsp), lambda i, j, k: (0, j, 0))
    elif tb:
        b_spec = pl.BlockSpec((tn, tk), lambda i, j, k: (j, k))
    else:
        b_spec = pl.BlockSpec((tk, tn), lambda i, j, k: (k, j))
    in_specs = [a_spec, b_spec]
    args = [a, b]
    if add is not None:
        in_specs.append(pl.BlockSpec((tm, tn), lambda i, j, k: (i, j)))
        args.append(add)
    tile = pl.BlockSpec((tm, tn), lambda i, j, k: (i, j))
    sem = ("parallel", "parallel", "arbitrary")
    if out_split:
        per = (n // out_split) // tn
        out_spec = pl.BlockSpec((None, tm, tn), lambda i, j, k: (j // per, i, j % per))
        out_shape = jax.ShapeDtypeStruct((out_split, m, n // out_split), out_dtype)
    elif loss_target is not None:
        in_specs.append(tile)
        args.append(loss_target)
        out_spec = [tile, tile, pl.BlockSpec((1, LANES), lambda i, j, k: (0, 0))]
        out_shape = [jax.ShapeDtypeStruct((m, n), F32), jax.ShapeDtypeStruct((m, n), BF16),
                     jax.ShapeDtypeStruct((1, LANES), F32)]
        sem = ("arbitrary",) * 3
    else:
        out_spec = tile
        out_shape = jax.ShapeDtypeStruct((m, n), out_dtype)
    return _pcall(body, args, name=name, grid=(m // tm, n // tn, nk), in_specs=in_specs, out_specs=out_spec,
                  out_shape=out_shape, scratch_shapes=[pltpu.VMEM((tm, tn), F32)] if nk > 1 else [],
                  sem=sem, comm=comm)


def _rms_fwd(x, g, *, name, tm=512, comm=()):
    s, d = x.shape
    tm = _tile(s, tm)

    def body(x_ref, g_ref, o_ref):
        xv = x_ref[...]
        r = lax.rsqrt(jnp.mean(xv * xv, axis=-1, keepdims=True) + EPS)
        o_ref[...] = (xv * r * g_ref[...]).astype(o_ref.dtype)

    return _pcall(body, (x, g), name=name, grid=(s // tm,),
                  in_specs=[pl.BlockSpec((tm, d), lambda i: (i, 0)), pl.BlockSpec((1, d), lambda i: (0, 0))],
                  out_specs=pl.BlockSpec((tm, d), lambda i: (i, 0)),
                  out_shape=jax.ShapeDtypeStruct((s, d), BF16), sem=("parallel",), comm=comm)


def _rms_bwd(dy, x, g, res, *, name, tm=512, comm=()):
    s, d = x.shape
    tm = _tile(s, tm)

    def body(dy_ref, x_ref, g_ref, res_ref, dx_ref, dxb_ref, dg_ref):
        xv = x_ref[...]
        dyv = dy_ref[...]
        r = lax.rsqrt(jnp.mean(xv * xv, axis=-1, keepdims=True) + EPS)
        xhat = xv * r
        dyg = dyv * g_ref[...]
        mt = jnp.mean(dyg * xhat, axis=-1, keepdims=True)
        dx = res_ref[...] + r * (dyg - xhat * mt)
        dx_ref[...] = dx
        dxb_ref[...] = dx.astype(BF16)
        part = jnp.sum(dyv * xhat, axis=0, keepdims=True)

        @pl.when(pl.program_id(0) == 0)
        def _():
            dg_ref[...] = part

        @pl.when(pl.program_id(0) > 0)
        def _():
            dg_ref[...] += part

    row = pl.BlockSpec((tm, d), lambda i: (i, 0))
    vec = pl.BlockSpec((1, d), lambda i: (0, 0))
    return _pcall(body, (dy, x, g, res), name=name, grid=(s // tm,), in_specs=[row, row, vec, row],
                  out_specs=[row, row, vec],
                  out_shape=[jax.ShapeDtypeStruct((s, d), F32), jax.ShapeDtypeStruct((s, d), BF16),
                             jax.ShapeDtypeStruct((1, d), F32)],
                  sem=("arbitrary",), comm=comm)


def _lane(shape):
    return lax.broadcasted_iota(jnp.int32, shape, 1)


def _halfsum(t, lo):
    s_lo = jnp.sum(jnp.where(lo, t, 0.0), axis=-1, keepdims=True)
    s_hi = jnp.sum(jnp.where(lo, 0.0, t), axis=-1, keepdims=True)
    return jnp.where(lo, s_lo, s_hi)


def _norm_pair(x, g, lo):
    r = lax.rsqrt(_halfsum(x * x, lo) * (1.0 / 64.0) + EPS)
    xhat = x * r
    return xhat * g, xhat, r


def _norm_pair_bwd(dy, g, xhat, r, lo):
    dyg = dy * g
    mt = _halfsum(dyg * xhat, lo) * (1.0 / 64.0)
    return r * (dyg - xhat * mt), jnp.sum(dy * xhat, axis=0, keepdims=True)


def _norm_full(x, g):
    r = lax.rsqrt(jnp.mean(x * x, axis=-1, keepdims=True) + EPS)
    xhat = x * r
    return xhat * g, xhat, r


def _norm_full_bwd(dy, g, xhat, r):
    dyg = dy * g
    mt = jnp.mean(dyg * xhat, axis=-1, keepdims=True)
    return r * (dyg - xhat * mt), jnp.sum(dy * xhat, axis=0, keepdims=True)


def _rot(x, first32):
    return jnp.where(first32, pltpu.roll(x, 96, axis=1), pltpu.roll(x, 32, axis=1))


def _rope(x, cos_t, sin_t, first32):
    return x * cos_t + _rot(x, first32) * sin_t


def _rope_bwd(dy, cos_t, sin_t, first32):
    return dy * cos_t + _rot(dy * sin_t, first32)


G_SWA_Q, G_SWA_K, G_QN, G_QR, G_KN, G_KR, G_MQ = range(7)

C_QA, C_KA, C_VA, C_CQ, C_CKV, C_QM, C_KR = 0, 1024, 1152, 1280, 1792, 2304, 2816


def _prep_common(p_ref, g128_ref, gcq_ref, gckv_ref, wuq_ref, wukv_ref, cos_ref, sin_ref):
    tm = p_ref.shape[0]
    lane = _lane((tm, LANES))
    lo = lane < 64
    first32 = (lane % 64) < 32
    cos_t = cos_ref[...]
    sin_t = sin_ref[...]
    g = lambda row: g128_ref[row:row + 1, :]
    out = dict(lo=lo, first32=first32, cos_t=cos_t, sin_t=sin_t, lane=lane)
    cq_n, cq_hat, cq_r = _norm_full(p_ref[:, C_CQ:C_CQ + MLA_RANK], gcq_ref[...])
    ckv_n, ckv_hat, ckv_r = _norm_full(p_ref[:, C_CKV:C_CKV + MLA_RANK], gckv_ref[...])
    cq_b = cq_n.astype(BF16)
    ckv_b = ckv_n.astype(BF16)
    q_b = jnp.dot(cq_b, wuq_ref[...], preferred_element_type=F32)
    kv_b = jnp.dot(ckv_b, wukv_ref[...], preferred_element_type=F32)
    out.update(cq_b=cq_b, cq_hat=cq_hat, cq_r=cq_r, ckv_b=ckv_b, ckv_hat=ckv_hat, ckv_r=ckv_r, q_b=q_b, kv_b=kv_b, g=g)
    return out


def _attn_prep_fwd(proj, g128, gcq, gckv, wuq, wukv, cos_t, sin_t, *, tm=512, comm=()):
    s = proj.shape[0]
    tm = _tile(s, tm)

    def body(p_ref, g128_ref, gcq_ref, gckv_ref, wuq_ref, wukv_ref, cos_ref, sin_ref,
             qa_ref, ka_ref, va_ref, qcat_ref, kcat_ref, vb_ref, qm_ref):
        c = _prep_common(p_ref, g128_ref, gcq_ref, gckv_ref, wuq_ref, wukv_ref, cos_ref, sin_ref)
        lo, first32, g = c["lo"], c["first32"], c["g"]
        for j in range(SWA_Q_HEADS // 2):
            y, _, _ = _norm_pair(p_ref[:, C_QA + 128 * j:C_QA + 128 * (j + 1)], g(G_SWA_Q), lo)
            qa_ref[:, 128 * j:128 * (j + 1)] = y.astype(BF16)
        y, _, _ = _norm_pair(p_ref[:, C_KA:C_KA + 128], g(G_SWA_K), lo)
        ka_ref[...] = y.astype(BF16)
        va_ref[...] = p_ref[:, C_VA:C_VA + 128].astype(BF16)
        kr, _, _ = _norm_pair(p_ref[:, C_KR:C_KR + 128], g(G_KR), lo)
        kr = jnp.where(lo, _rope(kr, c["cos_t"], c["sin_t"], first32), 0.0)
        krkr = (kr + pltpu.roll(kr, 64, axis=1)).astype(BF16)
        q_b, kv_b = c["q_b"], c["kv_b"]
        qr = []
        for j in range(MLA_HEADS // 2):
            y, _, _ = _norm_pair(q_b[:, 512 + 128 * j:512 + 128 * (j + 1)], g(G_QR), lo)
            qr.append(_rope(y, c["cos_t"], c["sin_t"], first32))
        for h in range(MLA_HEADS):
            qn, _, _ = _norm_full(q_b[:, 128 * h:128 * (h + 1)], g(G_QN))
            keep = lo if h % 2 == 0 else jnp.logical_not(lo)
            qcat_ref[h, :, 0:128] = qn.astype(BF16)
            qcat_ref[h, :, 128:256] = jnp.where(keep, qr[h // 2], 0.0).astype(BF16)
            kn, _, _ = _norm_full(kv_b[:, 128 * h:128 * (h + 1)], g(G_KN))
            kcat_ref[h, :, 0:128] = kn.astype(BF16)
            kcat_ref[h, :, 128:256] = krkr
        vb_ref[...] = kv_b[:, 512:1024].astype(BF16)
        for h in range(MEM_HEADS):
            y, _, _ = _norm_full(p_ref[:, C_QM + 128 * h:C_QM + 128 * (h + 1)], g(G_MQ))
            qm_ref[:, 128 * h:128 * (h + 1)] = y.astype(BF16)

    row = lambda w: pl.BlockSpec((tm, w), lambda i: (i, 0))
    full = lambda shape: pl.BlockSpec(shape, lambda i: tuple(0 for _ in shape))
    cat = pl.BlockSpec((MLA_HEADS, tm, 256), lambda i: (0, i, 0))
    return _pcall(
        body, (proj, g128, gcq, gckv, wuq, wukv, cos_t, sin_t), name="attn_prep_fwd", grid=(s // tm,),
        in_specs=[row(IN_PAD), full((8, 128)), full((1, 512)), full((1, 512)), full((512, 768)), full((512, 1024)),
                  row(128), row(128)],
        out_specs=[row(1024), row(128), row(128), cat, cat, row(512), row(512)],
        out_shape=[jax.ShapeDtypeStruct((s, 1024), BF16), jax.ShapeDtypeStruct((s, 128), BF16),
                   jax.ShapeDtypeStruct((s, 128), BF16), jax.ShapeDtypeStruct((MLA_HEADS, s, 256), BF16),
                   jax.ShapeDtypeStruct((MLA_HEADS, s, 256), BF16), jax.ShapeDtypeStruct((s, 512), BF16),
                   jax.ShapeDtypeStruct((s, 512), BF16)],
        sem=("parallel",), comm=comm)


def _attn_prep_bwd(proj, g128, gcq, gckv, wuq, wukv, cos_t, sin_t,
                   d_qa, d_ka, d_va, d_qcat, d_kcat, d_vb, d_qm, *, tm=256, comm=()):
    s = proj.shape[0]
    tm = _tile(s, tm)

    def body(p_ref, g128_ref, gcq_ref, gckv_ref, wuq_ref, wukv_ref, cos_ref, sin_ref,
             dqa_ref, dka_ref, dva_ref, dqcat_ref, dkcat_ref, dvb_ref, dqm_ref,
             dp_ref, dwuq_ref, dwukv_ref, dg128_ref, dgcq_ref, dgckv_ref):
        c = _prep_common(p_ref, g128_ref, gcq_ref, gckv_ref, wuq_ref, wukv_ref, cos_ref, sin_ref)
        lo, first32, g = c["lo"], c["first32"], c["g"]
        cos_v, sin_v = c["cos_t"], c["sin_t"]
        q_b, kv_b = c["q_b"], c["kv_b"]
        zero_row = jnp.zeros((1, LANES), F32)
        dg = {k: zero_row for k in range(7)}

        for j in range(SWA_Q_HEADS // 2):
            sl = slice(C_QA + 128 * j, C_QA + 128 * (j + 1))
            _, xhat, r = _norm_pair(p_ref[:, sl], g(G_SWA_Q), lo)
            dx, dgj = _norm_pair_bwd(dqa_ref[:, 128 * j:128 * (j + 1)], g(G_SWA_Q), xhat, r, lo)
            dp_ref[:, sl] = dx.astype(BF16)
            dg[G_SWA_Q] = dg[G_SWA_Q] + dgj
        _, xhat, r = _norm_pair(p_ref[:, C_KA:C_KA + 128], g(G_SWA_K), lo)
        dx, dgj = _norm_pair_bwd(dka_ref[...], g(G_SWA_K), xhat, r, lo)
        dp_ref[:, C_KA:C_KA + 128] = dx.astype(BF16)
        dg[G_SWA_K] = dgj
        dp_ref[:, C_VA:C_VA + 128] = dva_ref[...].astype(BF16)

        dqb_parts = [None] * 6
        for h in range(MLA_HEADS):
            _, xhat, r = _norm_full(q_b[:, 128 * h:128 * (h + 1)], g(G_QN))
            dx, dgj = _norm_full_bwd(dqcat_ref[h, :, 0:128], g(G_QN), xhat, r)
            dqb_parts[h] = dx
            dg[G_QN] = dg[G_QN] + dgj
        for j in range(MLA_HEADS // 2):
            _, xhat, r = _norm_pair(q_b[:, 512 + 128 * j:512 + 128 * (j + 1)], g(G_QR), lo)
            d_rot = jnp.where(lo, dqcat_ref[2 * j, :, 128:256], dqcat_ref[2 * j + 1, :, 128:256])
            d_y = _rope_bwd(d_rot, cos_v, sin_v, first32)
            dx, dgj = _norm_pair_bwd(d_y, g(G_QR), xhat, r, lo)
            dqb_parts[4 + j] = dx
            dg[G_QR] = dg[G_QR] + dgj
        d_qb = jnp.concatenate(dqb_parts, axis=1).astype(BF16)
        dwuq = lax.dot_general(c["cq_b"], d_qb, (((0,), (0,)), ((), ())), preferred_element_type=F32)
        d_cqn = lax.dot_general(d_qb, wuq_ref[...], (((1,), (1,)), ((), ())), preferred_element_type=F32)
        dx, dgcq = _norm_full_bwd(d_cqn, gcq_ref[...], c["cq_hat"], c["cq_r"])
        dp_ref[:, C_CQ:C_CQ + MLA_RANK] = dx.astype(BF16)

        dkv_parts = []
        d_krkr = jnp.zeros((p_ref.shape[0], LANES), F32)
        for h in range(MLA_HEADS):
            _, xhat, r = _norm_full(kv_b[:, 128 * h:128 * (h + 1)], g(G_KN))
            dx, dgj = _norm_full_bwd(dkcat_ref[h, :, 0:128], g(G_KN), xhat, r)
            dkv_parts.append(dx)
            dg[G_KN] = dg[G_KN] + dgj
            d_krkr = d_krkr + dkcat_ref[h, :, 128:256]
        d_kvb = jnp.concatenate(dkv_parts + [dvb_ref[...]], axis=1).astype(BF16)
        dwukv = lax.dot_general(c["ckv_b"], d_kvb, (((0,), (0,)), ((), ())), preferred_element_type=F32)
        d_ckvn = lax.dot_general(d_kvb, wukv_ref[...], (((1,), (1,)), ((), ())), preferred_element_type=F32)
        dx, dgckv = _norm_full_bwd(d_ckvn, gckv_ref[...], c["ckv_hat"], c["ckv_r"])
        dp_ref[:, C_CKV:C_CKV + MLA_RANK] = dx.astype(BF16)

        _, xhat, r = _norm_pair(p_ref[:, C_KR:C_KR + 128], g(G_KR), lo)
        d_kr = jnp.where(lo, d_krkr + pltpu.roll(d_krkr, 64, axis=1), 0.0)
        d_y = jnp.where(lo, _rope_bwd(d_kr, cos_v, sin_v, first32), 0.0)
        dx, dgj = _norm_pair_bwd(d_y, g(G_KR), xhat, r, lo)
        dp_ref[:, C_KR:C_KR + 128] = jnp.where(lo, dx, 0.0).astype(BF16)
        dp_ref[:, C_KR + 128:] = jnp.zeros((p_ref.shape[0], IN_PAD - C_KR - 128), BF16)
        dg[G_KR] = dgj

        for h in range(MEM_HEADS):
            sl = slice(C_QM + 128 * h, C_QM + 128 * (h + 1))
            _, xhat, r = _norm_full(p_ref[:, sl], g(G_MQ))
            dx, dgj = _norm_full_bwd(dqm_ref[:, 128 * h:128 * (h + 1)], g(G_MQ), xhat, r)
            dp_ref[:, sl] = dx.astype(BF16)
            dg[G_MQ] = dg[G_MQ] + dgj

        dg_tile = jnp.concatenate([dg[k] for k in range(7)] + [zero_row], axis=0)

        @pl.when(pl.program_id(0) == 0)
        def _():
            dwuq_ref[...] = dwuq
            dwukv_ref[...] = dwukv
            dg128_ref[...] = dg_tile
            dgcq_ref[...] = dgcq
            dgckv_ref[...] = dgckv

        @pl.when(pl.program_id(0) > 0)
        def _():
            dwuq_ref[...] += dwuq
            dwukv_ref[...] += dwukv
            dg128_ref[...] += dg_tile
            dgcq_ref[...] += dgcq
            dgckv_ref[...] += dgckv

    row = lambda w: pl.BlockSpec((tm, w), lambda i: (i, 0))
    full = lambda shape: pl.BlockSpec(shape, lambda i: tuple(0 for _ in shape))
    cat = pl.BlockSpec((MLA_HEADS, tm, 256), lambda i: (0, i, 0))
    return _pcall(
        body, (proj, g128, gcq, gckv, wuq, wukv, cos_t, sin_t, d_qa, d_ka, d_va, d_qcat, d_kcat, d_vb, d_qm),
        name="attn_prep_bwd", grid=(s // tm,),
        in_specs=[row(IN_PAD), full((8, 128)), full((1, 512)), full((1, 512)), full((512, 768)), full((512, 1024)),
                  row(128), row(128),
                  row(1024), row(128), row(128), cat, cat, row(512), row(512)],
        out_specs=[row(IN_PAD), full((512, 768)), full((512, 1024)), full((8, 128)), full((1, 512)), full((1, 512))],
        out_shape=[jax.ShapeDtypeStruct((s, IN_PAD), BF16), jax.ShapeDtypeStruct((512, 768), F32),
                   jax.ShapeDtypeStruct((512, 1024), F32), jax.ShapeDtypeStruct((8, 128), F32),
                   jax.ShapeDtypeStruct((1, 512), F32), jax.ShapeDtypeStruct((1, 512), F32)],
        sem=("arbitrary",), comm=comm)


SWA_SLOPES = tuple(2.0 ** (-8.0 * h / SWA_Q_HEADS) for h in range(1, SWA_Q_HEADS + 1))
SWA_SCALE = SWA_HEAD_DIM ** -0.5
NT_DIMS = (((1,), (1,)), ((), ()))
TN_DIMS = (((0,), (0,)), ((), ()))


def _swa_span(n, kp_ref, kc_ref, vp_ref, vc_ref, pcol_ref, pprow_ref, pcrow_ref):
    k_span = jnp.concatenate([kp_ref[...], kc_ref[...]], axis=0).astype(F32)
    v_span = jnp.concatenate([vp_ref[...], vc_ref[...]], axis=0).astype(F32)
    lo = _lane((2 * BLOCK, LANES)) < 64
    k_sw = pltpu.roll(k_span, 64, axis=1)
    v_sw = pltpu.roll(v_span, 64, axis=1)
    kk = (jnp.where(lo, k_span, k_sw).astype(BF16), jnp.where(lo, k_sw, k_span).astype(BF16))
    vv_lo = (jnp.where(lo, v_span, 0.0).astype(BF16), jnp.where(lo, v_sw, 0.0).astype(BF16))
    vv_hi = (jnp.where(lo, 0.0, v_sw).astype(BF16), jnp.where(lo, 0.0, v_span).astype(BF16))
    pk = jnp.concatenate([pprow_ref[...], pcrow_ref[...]], axis=1)
    dist = jnp.abs(pcol_ref[...] - pk)
    qi = lax.broadcasted_iota(jnp.int32, (BLOCK, 2 * BLOCK), 0)
    ki = lax.broadcasted_iota(jnp.int32, (BLOCK, 2 * BLOCK), 1)
    first_key = jnp.where(n > 0, qi + 1, jnp.maximum(qi + 1, BLOCK))
    valid = jnp.logical_and(ki >= first_key, ki <= qi + BLOCK)
    mask_add = jnp.where(valid, 0.0, NEG_INF)
    return kk, vv_lo, vv_hi, dist, mask_add


def _swa_heads(q_ref, lo):
    heads = []
    for j in range(SWA_Q_HEADS // 2):
        q_pair = q_ref[:, 128 * j:128 * (j + 1)].astype(F32)
        for par in (0, 1):
            q_h = jnp.where(lo if par == 0 else jnp.logical_not(lo), q_pair, 0.0).astype(BF16)
            heads.append((2 * j + par, (2 * j) // (SWA_Q_HEADS // SWA_KV_HEADS), par, q_h))
    return heads


def _swa_probs(raw, dist, mask_add, slope, sink):
    s = raw * SWA_SCALE - slope * dist + mask_add
    m = jnp.maximum(jnp.max(s, axis=-1, keepdims=True), sink)
    e = jnp.exp(s - m)
    e_sink = jnp.exp(sink - m)
    inv = 1.0 / (jnp.sum(e, axis=-1, keepdims=True) + e_sink)
    return e * inv, e_sink * inv


def _swa_specs():
    blk = lambda w: pl.BlockSpec((BLOCK, w), lambda n: (n, 0))
    prev = lambda w: pl.BlockSpec((BLOCK, w), lambda n: (jnp.maximum(n - 1, 0), 0))
    prow_c = pl.BlockSpec((1, BLOCK), lambda n: (0, n))
    prow_p = pl.BlockSpec((1, BLOCK), lambda n: (0, jnp.maximum(n - 1, 0)))
    smem = pl.BlockSpec(memory_space=pltpu.SMEM)
    return [blk(1024), prev(128), blk(128), prev(128), blk(128), blk(1), prow_p, prow_c, smem], blk


def _swa_fwd(qa, ka, va, pos_col, pos_row, sinks, *, comm=()):
    s = qa.shape[0]
    in_specs, blk = _swa_specs()

    def body(q_ref, kp_ref, kc_ref, vp_ref, vc_ref, pcol_ref, pprow_ref, pcrow_ref, sink_ref, o_ref, yb_ref):
        n = pl.program_id(0)
        kk, vv_lo, vv_hi, dist, mask_add = _swa_span(n, kp_ref, kc_ref, vp_ref, vc_ref, pcol_ref, pprow_ref, pcrow_ref)
        lo = _lane((BLOCK, LANES)) < 64
        heads = _swa_heads(q_ref, lo)
        raws = [lax.dot_general(q_h, kk[kv], NT_DIMS, preferred_element_type=F32) for _, kv, _, q_h in heads]
        probs = [_swa_probs(raw, dist, mask_add, SWA_SLOPES[h], sink_ref[h])[0].astype(BF16)
                 for raw, (h, _, _, _) in zip(raws, heads)]
        for j in range(SWA_Q_HEADS // 2):
            kv = heads[2 * j][1]
            out = (jnp.dot(probs[2 * j], vv_lo[kv], preferred_element_type=F32)
                   + jnp.dot(probs[2 * j + 1], vv_hi[kv], preferred_element_type=F32))
            o_ref[:, 128 * j:128 * (j + 1)] = out
            yb_ref[:, 128 * j:128 * (j + 1)] = out.astype(BF16)

    return _pcall(body, (qa, ka, ka, va, va, pos_col, pos_row, pos_row, sinks), name="swa_fwd", grid=(s // BLOCK,),
                  in_specs=in_specs, out_specs=[blk(1024), blk(1024)],
                  out_shape=[jax.ShapeDtypeStruct((s, 1024), F32), jax.ShapeDtypeStruct((s, D_MODEL), BF16)],
                  sem=("parallel",), comm=comm)


def _swa_bwd(qa, ka, va, pos_col, pos_row, sinks, y_a, d_y, *, comm=()):
    s = qa.shape[0]
    in_specs, blk = _swa_specs()
    whole = pl.BlockSpec((s, 128), lambda n: (0, 0))

    def body(q_ref, kp_ref, kc_ref, vp_ref, vc_ref, pcol_ref, pprow_ref, pcrow_ref, sink_ref, y_ref, dy_ref,
             dq_ref, dk_ref, dv_ref, dsink_ref):
        n = pl.program_id(0)

        @pl.when(n == 0)
        def _():
            dk_ref[...] = jnp.zeros_like(dk_ref)
            dv_ref[...] = jnp.zeros_like(dv_ref)
            dsink_ref[...] = jnp.zeros_like(dsink_ref)

        kk, vv_lo, vv_hi, dist, mask_add = _swa_span(n, kp_ref, kc_ref, vp_ref, vc_ref, pcol_ref, pprow_ref, pcrow_ref)
        lo = _lane((BLOCK, LANES)) < 64
        lo2 = _lane((2 * BLOCK, LANES)) < 64
        lane1 = _lane((1, LANES))
        dsink = jnp.zeros((1, LANES), F32)
        dkk = [jnp.zeros((2 * BLOCK, LANES), F32) for _ in range(SWA_KV_HEADS)]
        dvv = [jnp.zeros((2 * BLOCK, LANES), F32) for _ in range(SWA_KV_HEADS)]
        heads = _swa_heads(q_ref, lo)
        do_b, deltas = [], []
        for j in range(SWA_Q_HEADS // 2):
            do_pair = dy_ref[:, 128 * j:128 * (j + 1)]
            doy = do_pair * y_ref[:, 128 * j:128 * (j + 1)]
            do_b.append(do_pair.astype(BF16))
            deltas.append(jnp.sum(jnp.where(lo, doy, 0.0), axis=-1, keepdims=True))
            deltas.append(jnp.sum(jnp.where(lo, 0.0, doy), axis=-1, keepdims=True))
        raws = [lax.dot_general(q_h, kk[kv], NT_DIMS, preferred_element_type=F32) for _, kv, _, q_h in heads]
        dps = [lax.dot_general(do_b[h // 2], (vv_lo, vv_hi)[par][kv], NT_DIMS, preferred_element_type=F32)
               for h, kv, par, _ in heads]
        p_b, ds_b = [], []
        for h, kv, par, _ in heads:
            p, p_sink = _swa_probs(raws[h], dist, mask_add, SWA_SLOPES[h], sink_ref[h])
            ds = p * (dps[h] - deltas[h])
            dsink = dsink + jnp.where(lane1 == h, -jnp.sum(p_sink * deltas[h], axis=0, keepdims=True), 0.0)
            p_b.append(p.astype(BF16))
            ds_b.append((ds * SWA_SCALE).astype(BF16))
        dq_halves = []
        for h, kv, par, q_h in heads:
            dq_halves.append(jnp.dot(ds_b[h], kk[kv], preferred_element_type=F32))
            dkk[kv] = dkk[kv] + lax.dot_general(ds_b[h], q_h, TN_DIMS, preferred_element_type=F32)
            pv = lax.dot_general(p_b[h], do_b[h // 2], TN_DIMS, preferred_element_type=F32)
            dvv[kv] = dvv[kv] + jnp.where(lo2 if par == 0 else jnp.logical_not(lo2), pv, 0.0)
        for j in range(SWA_Q_HEADS // 2):
            dq_ref[:, 128 * j:128 * (j + 1)] = jnp.where(lo, dq_halves[2 * j], dq_halves[2 * j + 1])
        fold = lambda t: t + pltpu.roll(t, 64, axis=1)
        dk_span = jnp.where(lo2, fold(dkk[0]), fold(dkk[1]))
        dv_span = jnp.where(lo2, fold(dvv[0]), fold(dvv[1]))
        prev0 = pl.multiple_of(jnp.maximum(n - 1, 0) * BLOCK, BLOCK)
        cur0 = pl.multiple_of(n * BLOCK, BLOCK)
        dk_ref[pl.ds(prev0, BLOCK), :] += dk_span[0:BLOCK]
        dk_ref[pl.ds(cur0, BLOCK), :] += dk_span[BLOCK:]
        dv_ref[pl.ds(prev0, BLOCK), :] += dv_span[0:BLOCK]
        dv_ref[pl.ds(cur0, BLOCK), :] += dv_span[BLOCK:]
        dsink_ref[...] += dsink

    return _pcall(
        body, (qa, ka, ka, va, va, pos_col, pos_row, pos_row, sinks, y_a, d_y), name="swa_bwd", grid=(s // BLOCK,),
        in_specs=in_specs + [blk(1024), blk(1024)],
        out_specs=[blk(1024), whole, whole, pl.BlockSpec((1, LANES), lambda n: (0, 0))],
        out_shape=[jax.ShapeDtypeStruct((s, 1024), F32), jax.ShapeDtypeStruct((s, 128), F32),
                   jax.ShapeDtypeStruct((s, 128), F32), jax.ShapeDtypeStruct((1, LANES), F32)],
        sem=("arbitrary",), comm=comm)


MLA_SCALE = (MLA_NOPE + MLA_ROPE) ** -0.5
MLA_TILE = 1024


def _tile_pairs(nt, q_major):
    pairs = [(i, j) for i in range(nt) for j in range(i + 1)] if q_major else \
            [(i, j) for j in range(nt) for i in range(j, nt)]
    return jnp.asarray([p[0] for p in pairs], jnp.int32), jnp.asarray([p[1] for p in pairs], jnp.int32)


def _diag_mask(t):
    return lax.broadcasted_iota(jnp.int32, (t, t), 1) <= lax.broadcasted_iota(jnp.int32, (t, t), 0)


def _mla_fwd(q_cat, k_cat, v_b, y_all, *, comm=()):
    nh, s, _ = q_cat.shape
    t = _tile(s, MLA_TILE)
    qi, kj = _tile_pairs(s // t, True)
    ycol = (SWA_Q_HEADS * SWA_HEAD_DIM) // (nh * MLA_V)

    def body(qi_ref, kj_ref, q_ref, k_ref, v_ref, _, o_ref, lse_ref, yb_ref, m_sc, l_sc, acc_sc):
        i, j = qi_ref[pl.program_id(0)], kj_ref[pl.program_id(0)]

        @pl.when(j == 0)
        def _():
            m_sc[...] = jnp.full_like(m_sc, NEG_INF)
            l_sc[...] = jnp.zeros_like(l_sc)
            acc_sc[...] = jnp.zeros_like(acc_sc)

        def update(diagonal):
            scores = [lax.dot_general(q_ref[h], k_ref[h], NT_DIMS, preferred_element_type=F32) for h in range(nh)]
            probs, alphas = [], []
            for h in range(nh):
                sc = scores[h] * MLA_SCALE
                if diagonal:
                    sc = jnp.where(_diag_mask(t), sc, NEG_INF)
                m_old = m_sc[h]
                m_new = jnp.maximum(m_old, jnp.max(sc, axis=-1, keepdims=True))
                alpha = jnp.exp(m_old - m_new)
                p = jnp.exp(sc - m_new)
                l_sc[h] = alpha * l_sc[h] + jnp.sum(p, axis=-1, keepdims=True)
                m_sc[h] = m_new
                probs.append(p.astype(BF16))
                alphas.append(alpha)
            for h in range(nh):
                acc_sc[h] = alphas[h] * acc_sc[h] + jnp.dot(probs[h], v_ref[:, MLA_V * h:MLA_V * (h + 1)],
                                                            preferred_element_type=F32)

        @pl.when(j < i)
        def _():
            update(False)

        @pl.when(j == i)
        def _():
            update(True)
            for h in range(nh):
                out = acc_sc[h] * (1.0 / l_sc[h])
                o_ref[:, MLA_V * h:MLA_V * (h + 1)] = out
                yb_ref[:, MLA_V * h:MLA_V * (h + 1)] = out.astype(BF16)
                lse_ref[h] = m_sc[h] + jnp.log(l_sc[h])

    return _pcall(
        body, (q_cat, k_cat, v_b, y_all), name="mla_fwd", grid=(qi.shape[0],), prefetch=(qi, kj),
        in_specs=[pl.BlockSpec((nh, t, 256), lambda p, qi, kj: (0, qi[p], 0)),
                  pl.BlockSpec((nh, t, 256), lambda p, qi, kj: (0, kj[p], 0)),
                  pl.BlockSpec((t, nh * MLA_V), lambda p, qi, kj: (kj[p], 0)), ANY],
        out_specs=[pl.BlockSpec((t, nh * MLA_V), lambda p, qi, kj: (qi[p], 0)),
                   pl.BlockSpec((nh, t, 1), lambda p, qi, kj: (0, qi[p], 0)),
                   pl.BlockSpec((t, nh * MLA_V), lambda p, qi, kj: (qi[p], ycol))],
        out_shape=[jax.ShapeDtypeStruct((s, nh * MLA_V), F32), jax.ShapeDtypeStruct((nh, s, 1), F32),
                   jax.ShapeDtypeStruct(y_all.shape, y_all.dtype)],
        scratch_shapes=[pltpu.VMEM((nh, t, 1), F32), pltpu.VMEM((nh, t, 1), F32), pltpu.VMEM((nh, t, MLA_V), F32)],
        sem=("arbitrary",), comm=comm, io_alias={3: 2})


def _mla_bwd(q_cat, k_cat, v_b, y_b, lse, d_y, *, comm=()):
    nh, s, _ = q_cat.shape
    t = _tile(s, MLA_TILE)
    nt = s // t
    hp = 2
    wv = hp * MLA_V
    col0 = (SWA_Q_HEADS * SWA_HEAD_DIM) // wv
    qi, kj = _tile_pairs(nt, False)

    def body(qi_ref, kj_ref, q_ref, k_ref, v_ref, y_ref, lse_ref, dy_ref, dq_ref, dk_ref, dv_ref, dk_sc, dv_sc):
        step = pl.program_id(1)
        i, j = qi_ref[step], kj_ref[step]

        @pl.when(step == 0)
        def _():
            dq_ref[...] = jnp.zeros_like(dq_ref)

        @pl.when(i == j)
        def _():
            dk_sc[...] = jnp.zeros_like(dk_sc)
            dv_sc[...] = jnp.zeros_like(dv_sc)

        def update(diagonal):
            rows = pl.ds(pl.multiple_of(i * t, t), t)
            cols = [slice(MLA_V * h, MLA_V * (h + 1)) for h in range(hp)]
            do_b = [dy_ref[:, cols[h]].astype(BF16) for h in range(hp)]
            scores = [lax.dot_general(q_ref[h], k_ref[h], NT_DIMS, preferred_element_type=F32) for h in range(hp)]
            dps = [lax.dot_general(do_b[h], v_ref[:, cols[h]], NT_DIMS, preferred_element_type=F32) for h in range(hp)]
            p_b, ds_b = [], []
            for h in range(hp):
                p = jnp.exp(scores[h] * MLA_SCALE - lse_ref[h])
                if diagonal:
                    p = jnp.where(_diag_mask(t), p, 0.0)
                delta = jnp.sum(dy_ref[:, cols[h]] * y_ref[:, cols[h]], axis=-1, keepdims=True)
                p_b.append(p.astype(BF16))
                ds_b.append((p * (dps[h] - delta) * MLA_SCALE).astype(BF16))
            for h in range(hp):
                dv_sc[h] += lax.dot_general(p_b[h], do_b[h], TN_DIMS, preferred_element_type=F32)
                dk_sc[h] += lax.dot_general(ds_b[h], q_ref[h], TN_DIMS, preferred_element_type=F32)
                dq_ref[h, rows, :] += jnp.dot(ds_b[h], k_ref[h], preferred_element_type=F32)

        @pl.when(i > j)
        def _():
            update(False)

        @pl.when(i == j)
        def _():
            update(True)

        @pl.when(i == nt - 1)
        def _():
            dk_ref[...] = dk_sc[...]
            for h in range(hp):
                dv_ref[:, MLA_V * h:MLA_V * (h + 1)] = dv_sc[h]

    return _pcall(
        body, (q_cat, k_cat, v_b, y_b, lse, d_y), name="mla_bwd", grid=(nh // hp, qi.shape[0]), prefetch=(qi, kj),
        in_specs=[pl.BlockSpec((hp, t, 256), lambda g, p, qi, kj: (g, qi[p], 0)),
                  pl.BlockSpec((hp, t, 256), lambda g, p, qi, kj: (g, kj[p], 0)),
                  pl.BlockSpec((t, wv), lambda g, p, qi, kj: (kj[p], g)),
                  pl.BlockSpec((t, wv), lambda g, p, qi, kj: (qi[p], g)),
                  pl.BlockSpec((hp, t, 1), lambda g, p, qi, kj: (g, qi[p], 0)),
                  pl.BlockSpec((t, wv), lambda g, p, qi, kj: (qi[p], col0 + g))],
        out_specs=[pl.BlockSpec((hp, s, 256), lambda g, p, qi, kj: (g, 0, 0)),
                   pl.BlockSpec((hp, t, 256), lambda g, p, qi, kj: (g, kj[p], 0)),
                   pl.BlockSpec((t, wv), lambda g, p, qi, kj: (kj[p], g))],
        out_shape=[jax.ShapeDtypeStruct((nh, s, 256), F32), jax.ShapeDtypeStruct((nh, s, 256), F32),
                   jax.ShapeDtypeStruct((s, nh * MLA_V), F32)],
        scratch_shapes=[pltpu.VMEM((hp, t, 256), F32), pltpu.VMEM((hp, t, MLA_V), F32)],
        sem=("arbitrary", "arbitrary"), comm=comm)


MEM_SCALE = MEM_DIM ** -0.5


def _mem_kv_fwd(mem, g_mem, w_memkv, g_mk):
    m_len = mem.shape[0]

    def body(mem_ref, g_ref, w_ref, gk_ref, mn_ref, kv_ref, kn_ref, v_ref):
        mn, _, _ = _norm_full(mem_ref[...], g_ref[...])
        mn_b = mn.astype(BF16)
        mn_ref[...] = mn_b
        kv = jnp.dot(mn_b, w_ref[...], preferred_element_type=F32)
        kv_ref[...] = kv
        for h in range(MEM_HEADS):
            kn, _, _ = _norm_full(kv[:, 128 * h:128 * (h + 1)], gk_ref[...])
            kn_ref[:, 128 * h:128 * (h + 1)] = kn.astype(BF16)
        v_ref[...] = kv[:, 512:1024].astype(BF16)

    return pl.pallas_call(
        body, name="mem_kv_fwd",
        out_shape=[jax.ShapeDtypeStruct((m_len, D_MODEL), BF16), jax.ShapeDtypeStruct((m_len, 1024), F32),
                   jax.ShapeDtypeStruct((m_len, 512), BF16), jax.ShapeDtypeStruct((m_len, 512), BF16)],
        compiler_params=_params(),
    )(mem, g_mem, w_memkv, g_mk)


def _mem_kv_bwd(mem, g_mem, w_memkv, g_mk, mn_b, kv, d_kn, d_v):
    m_len = mem.shape[0]

    def body(mem_ref, g_ref, w_ref, gk_ref, mn_ref, kv_ref, dkn_ref, dv_ref, dw_ref, dgmem_ref, dgk_ref):
        parts = []
        dgk = jnp.zeros((1, LANES), F32)
        for h in range(MEM_HEADS):
            _, xhat, r = _norm_full(kv_ref[:, 128 * h:128 * (h + 1)], gk_ref[...])
            dx, dgh = _norm_full_bwd(dkn_ref[:, 128 * h:128 * (h + 1)], gk_ref[...], xhat, r)
            parts.append(dx)
            dgk = dgk + dgh
        d_kv = jnp.concatenate(parts + [dv_ref[...]], axis=1).astype(BF16)
        dw_ref[...] = lax.dot_general(mn_ref[...], d_kv, TN_DIMS, preferred_element_type=F32)
        d_mn = lax.dot_general(d_kv, w_ref[...], NT_DIMS, preferred_element_type=F32)
        _, xhat, _ = _norm_full(mem_ref[...], g_ref[...])
        dgmem_ref[...] = jnp.sum(d_mn * xhat, axis=0, keepdims=True)
        dgk_ref[...] = dgk

    return pl.pallas_call(
        body, name="mem_kv_bwd",
        out_shape=[jax.ShapeDtypeStruct((D_MODEL, 1024), F32), jax.ShapeDtypeStruct((1, D_MODEL), F32),
                   jax.ShapeDtypeStruct((1, LANES), F32)],
        compiler_params=_params(),
    )(mem, g_mem, w_memkv, g_mk, mn_b, kv, d_kn, d_v)


def _mem_probs(q_h, k_h):
    sc = lax.dot_general(q_h, k_h, NT_DIMS, preferred_element_type=F32) * MEM_SCALE
    e = jnp.exp(sc - jnp.max(sc, axis=-1, keepdims=True))
    return e * (1.0 / jnp.sum(e, axis=-1, keepdims=True))


def _mem_attn_fwd(qm, km, vm, y_all, *, tm=512):
    s = qm.shape[0]
    tm = _tile(s, tm)
    m_len = km.shape[0]
    ycol = (SWA_Q_HEADS * SWA_HEAD_DIM + MLA_HEADS * MLA_V) // 512

    def body(q_ref, k_ref, v_ref, _, o_ref, yb_ref):
        for h in range(MEM_HEADS):
            sl = slice(128 * h, 128 * (h + 1))
            p = _mem_probs(q_ref[:, sl], k_ref[:, sl])
            out = jnp.dot(p.astype(BF16), v_ref[:, sl], preferred_element_type=F32)
            o_ref[:, sl] = out
            yb_ref[:, sl] = out.astype(BF16)

    kvspec = pl.BlockSpec((m_len, 512), lambda i: (0, 0))
    return _pcall(
        body, (qm, km, vm, y_all), name="mem_attn_fwd", grid=(s // tm,),
        in_specs=[pl.BlockSpec((tm, 512), lambda i: (i, 0)), kvspec, kvspec, ANY],
        out_specs=[pl.BlockSpec((tm, 512), lambda i: (i, 0)), pl.BlockSpec((tm, 512), lambda i: (i, ycol))],
        out_shape=[jax.ShapeDtypeStruct((s, 512), F32), jax.ShapeDtypeStruct(y_all.shape, y_all.dtype)],
        sem=("parallel",), io_alias={3: 1})


def _mem_attn_bwd(qm, km, vm, y_m, d_y, *, tm=512):
    s = qm.shape[0]
    tm = _tile(s, tm)
    m_len = km.shape[0]
    col0 = (SWA_Q_HEADS * SWA_HEAD_DIM + MLA_HEADS * MLA_V) // 512

    def body(q_ref, k_ref, v_ref, y_ref, dy_ref, dq_ref, dk_ref, dv_ref):
        @pl.when(pl.program_id(0) == 0)
        def _():
            dk_ref[...] = jnp.zeros_like(dk_ref)
            dv_ref[...] = jnp.zeros_like(dv_ref)

        for h in range(MEM_HEADS):
            sl = slice(128 * h, 128 * (h + 1))
            q_h, k_h = q_ref[:, sl], k_ref[:, sl]
            do = dy_ref[:, sl]
            do_b = do.astype(BF16)
            p = _mem_probs(q_h, k_h)
            delta = jnp.sum(do * y_ref[:, sl], axis=-1, keepdims=True)
            dv_ref[:, sl] += lax.dot_general(p.astype(BF16), do_b, TN_DIMS, preferred_element_type=F32)
            dp = lax.dot_general(do_b, v_ref[:, sl], NT_DIMS, preferred_element_type=F32)
            ds_b = (p * (dp - delta) * MEM_SCALE).astype(BF16)
            dq_ref[:, sl] = jnp.dot(ds_b, k_h, preferred_element_type=F32)
            dk_ref[:, sl] += lax.dot_general(ds_b, q_h, TN_DIMS, preferred_element_type=F32)

    kvspec = pl.BlockSpec((m_len, 512), lambda i: (0, 0))
    row = pl.BlockSpec((tm, 512), lambda i: (i, 0))
    return pl.pallas_call(
        body, name="mem_attn_bwd", grid=(s // tm,),
        in_specs=[row, kvspec, kvspec, row, pl.BlockSpec((tm, 512), lambda i: (i, col0))],
        out_specs=[row, kvspec, kvspec],
        out_shape=[jax.ShapeDtypeStruct((s, 512), F32), jax.ShapeDtypeStruct((m_len, 512), F32),
                   jax.ShapeDtypeStruct((m_len, 512), F32)],
        compiler_params=_params(("arbitrary",)),
    )(qm, km, vm, y_m, d_y)


def _ffn_gate_up(fn, w_gate, w_up, *, tm=512, comm=()):
    s, d = fn.shape
    nsp, _, tf = w_gate.shape
    f = nsp * tf
    tm = _tile(s, tm)

    def body(x_ref, wg_ref, wu_ref, g_ref, u_ref, a_ref):
        x = x_ref[...]
        gate = jnp.dot(x, wg_ref[...], preferred_element_type=F32)
        up = jnp.dot(x, wu_ref[...], preferred_element_type=F32)
        g_ref[...] = gate.astype(BF16)
        u_ref[...] = up.astype(BF16)
        a_ref[...] = (gate * (1.0 / (1.0 + jnp.exp(-gate))) * up).astype(BF16)

    wspec = pl.BlockSpec((None, d, tf), lambda j, i: (j, 0, 0))
    ospec = pl.BlockSpec((tm, tf), lambda j, i: (i, j))
    osh = jax.ShapeDtypeStruct((s, f), BF16)
    return _pcall(body, (fn, w_gate, w_up), name="ffn_gate_up", grid=(nsp, s // tm),
                  in_specs=[pl.BlockSpec((tm, d), lambda j, i: (i, 0)), wspec, wspec],
                  out_specs=[ospec, ospec, ospec], out_shape=[osh, osh, osh], sem=("parallel", "parallel"), comm=comm)


def _ffn_bwd_act(d_out, w_down, gate, up, *, tm=1024, tf=1408, comm=()):
    s, d = d_out.shape
    f = w_down.shape[0]
    tm, tf = _tile(s, tm), _tile(f, tf)

    sub = tm // 4 if tm % 1024 == 0 else tm

    def body(do_ref, wd_ref, g_ref, u_ref, dg_ref, du_ref):
        groups = [slice(r, r + sub) for r in range(0, tm, sub)]
        parts = [lax.dot_general(do_ref[rows, :].astype(BF16), wd_ref[...], NT_DIMS, preferred_element_type=F32)
                 for rows in groups]
        for rows, d_act in zip(groups, parts):
            gate = g_ref[rows, :].astype(F32)
            sig = 1.0 / (1.0 + jnp.exp(-gate))
            du_ref[rows, :] = (d_act * (gate * sig)).astype(BF16)
            dg_ref[rows, :] = (d_act * u_ref[rows, :].astype(F32) * (sig * (1.0 + gate * (1.0 - sig)))).astype(BF16)

    ospec = pl.BlockSpec((tm, tf), lambda j, i: (i, j))
    osh = jax.ShapeDtypeStruct((s, f), BF16)
    return _pcall(
        body, (d_out, w_down, gate, up), name="ffn_bwd_act", grid=(f // tf, s // tm),
        in_specs=[pl.BlockSpec((tm, d), lambda j, i: (i, 0)), pl.BlockSpec((tf, d), lambda j, i: (j, 0)), ospec, ospec],
        out_specs=[ospec, ospec], out_shape=[osh, osh], sem=("parallel", "parallel"), comm=comm)


def _cols(g4):
    return jnp.concatenate([g4[k] for k in range(N_CHIPS)], axis=1)


def _full_w_in(g4):
    per = IN_WIDTH // N_CHIPS
    kr0 = 2304 - (N_CHIPS - 1) * per
    last = g4[N_CHIPS - 1]
    pad = jnp.zeros((last.shape[0], IN_PAD - IN_WIDTH), last.dtype)
    return jnp.concatenate([g4[0], g4[1], g4[2], last[:, :kr0], last[:, kr0 + 64:], last[:, kr0:kr0 + 64], pad], axis=1)


def _shards_w_in(dwp):
    per = IN_WIDTH // N_CHIPS
    kr0 = 2304 - (N_CHIPS - 1) * per
    last = jnp.concatenate([dwp[:, (N_CHIPS - 1) * per:2304], dwp[:, C_KR:C_KR + 64], dwp[:, 2304:C_KR]], axis=1)
    assert last.shape[1] == per and kr0 == 144
    return jnp.stack([dwp[:, per * k:per * (k + 1)] for k in range(N_CHIPS - 1)] + [last])


def _full_heads(g4, first):
    return jnp.concatenate([g4[k][:, :first] for k in range(N_CHIPS)] + [g4[k][:, first:] for k in range(N_CHIPS)], axis=1)


def _shards_heads(dwp, first, rest):
    base = N_CHIPS * first
    return jnp.stack([jnp.concatenate([dwp[:, first * k:first * (k + 1)], dwp[:, base + rest * k:base + rest * (k + 1)]], axis=1)
                      for k in range(N_CHIPS)])


def _rope_tables(pos):
    inv_freq = ROPE_THETA ** (-jnp.arange(0, MLA_ROPE, 2, dtype=F32) / MLA_ROPE)
    ang = pos.astype(F32)[:, None] * inv_freq
    cos, sin = jnp.cos(ang), jnp.sin(ang)
    return jnp.tile(cos, (1, 4)), jnp.concatenate([-sin, sin, -sin, sin], axis=1)


def _gain_table(sp):
    two = lambda v: jnp.tile(v, (1, 2))
    rows = [two(sp["swa_q_norm_g"]), two(sp["swa_k_norm_g"]), sp["mla_qn_norm_g"], two(sp["mla_qr_norm_g"]),
            sp["mla_kn_norm_g"], two(sp["mla_kr_norm_g"]), sp["mem_q_norm_g"], jnp.zeros((1, LANES), F32)]
    return jnp.concatenate(rows, axis=0)


CHIP_DISTANCES = (1, 2, 3)


def _place():
    x, y, c = lax.axis_index("x"), lax.axis_index("y"), lax.axis_index("c")
    return x, y, c, 2 * x + y


def _chip_at(x, y, d):
    px = 1 - x if d & 2 else x
    py = 1 - y if d & 1 else y
    return px, py, 2 * px + py


def _row_tile(rows, want=512, mult=8):
    t = min(rows, want)
    t -= t % mult
    while rows % t:
        t -= mult
    return t


def _cast_into_slot(w, meta, *, name, comm=()):
    rows, cols = w.shape
    tr = _row_tile(rows, 512, 16)

    def body(meta_ref, w_ref, o_ref):
        o_ref[...] = w_ref[...].astype(BF16)

    return _pcall(body, (w,), name=name, grid=(rows // tr,), prefetch=(meta,),
                  in_specs=[pl.BlockSpec((tr, cols), lambda i, m: (i, 0))],
                  out_specs=pl.BlockSpec((None, tr, cols), lambda i, m: (m[0], i, 0)),
                  out_shape=jax.ShapeDtypeStruct((N_CHIPS, rows, cols), BF16), sem=("parallel",), comm=comm)


def _remote(src, dst, ssem, rsem, i, device):
    return pltpu.make_async_remote_copy(src_ref=src, dst_ref=dst, send_sem=ssem.at[i], recv_sem=rsem.at[i],
                                        device_id=device, device_id_type=MESH)


def _symmetric_stage(ins, out_shapes, aliases, n_sem, copies):
    def issue(i_refs, o_refs, ssem, rsem):
        for send, _ in copies(i_refs, o_refs, ssem, rsem):
            send.start()

    def wait(i_refs, o_refs, ssem, rsem):
        pairs = copies(i_refs, o_refs, ssem, rsem)
        for _, arrival in pairs:
            arrival.wait_recv()
        for send, _ in pairs:
            send.wait_send()

    return _Stage(ins, out_shapes, aliases, n_sem, issue, wait)


def _gather_stage(slots, leg, part=(0, 1)):
    n = len(slots)
    shapes = [jax.ShapeDtypeStruct(s.shape, s.dtype) for s in slots]
    in_place = {w: w for w in range(n)}
    if not isinstance(leg, str):
        legs = list(leg)

        def copies(i_refs, o_refs, ssem, rsem):
            return [pr for k, (which, prt) in enumerate(legs)
                    for pr in _gather_stage(slots, which, prt).leg_copies(which, 3 * n * k)(i_refs, o_refs, ssem, rsem)]

        return _symmetric_stage(slots, shapes, in_place, 3 * n * len(legs), copies)

    def leg_copies(which, base):
        def copies(_, outs, ssem, rsem):
            x, y, c, k_me = _place()
            pairs = []
            for w in range(n):
                half = outs[w].shape[1] // 2
                r0, size = _window(half, part)
                slab = lambda k, cc, w=w, half=half, r0=r0, size=size: outs[w].at[k, pl.ds(cc * half + r0, size)]
                for d in CHIP_DISTANCES:
                    px, py, k_src = _chip_at(x, y, d)
                    i = base + 3 * w + d - 1
                    if which == "ici":
                        pairs.append((_remote(slab(k_me, c), slab(k_me, c), ssem, rsem, i, (px, py, c)),
                                      _remote(slab(k_src, c), slab(k_src, c), ssem, rsem, i, (x, y, c))))
                    else:
                        pairs.append((_remote(slab(k_src, c), slab(k_src, c), ssem, rsem, i, (x, y, 1 - c)),
                                      _remote(slab(k_src, 1 - c), slab(k_src, 1 - c), ssem, rsem, i, (x, y, c))))
            return pairs
        return copies

    if leg != "both":
        st = _symmetric_stage(slots, shapes, in_place, 3 * n, leg_copies(leg, 0))
        st.leg_copies = leg_copies
        return st
    ici = _symmetric_stage(slots, shapes, in_place, 6 * n, leg_copies("ici", 0))
    d2d = _symmetric_stage(slots, shapes, in_place, 6 * n, leg_copies("d2d", 3 * n))

    def mid(*refs):
        ici.wait(*refs)
        d2d.issue(*refs)

    return _Stage(slots, shapes, in_place, 6 * n, ici.issue, d2d.wait, mid)


def _halves_stage(grads):
    n = len(grads)

    def copies(ins, outs, ssem, rsem):
        x, y, c, _ = _place()
        pairs = []
        for w in range(n):
            half = ins[w].shape[1] // 2
            pairs.append((_remote(ins[w].at[:, pl.ds((1 - c) * half, half)], outs[w], ssem, rsem, w, (x, y, 1 - c)),
                          _remote(outs[w], outs[w], ssem, rsem, w, (x, y, c))))
        return pairs

    shapes = [jax.ShapeDtypeStruct((N_CHIPS, g.shape[1] // 2, g.shape[2]), g.dtype) for g in grads]
    return _symmetric_stage(grads, shapes, {}, n, copies)


def _window(rows, part):
    idx, count = part
    size = rows // count
    assert size * count == rows and size % 16 == 0, (rows, part)
    return idx * size, size


def _chips_stage(parts, part=(0, 1), into=None):
    n = len(parts)

    def copies(ins, outs, ssem, rsem):
        x, y, c, _ = _place()
        pairs = []
        for w in range(n):
            r0, size = _window(ins[w].shape[1], part)
            for d in CHIP_DISTANCES:
                px, py, _ = _chip_at(x, y, d)
                i = 3 * w + d - 1
                land = outs[w].at[d - 1, pl.ds(r0, size)]
                pairs.append((_remote(ins[w].at[d - 1, pl.ds(r0, size)], land, ssem, rsem, i, (px, py, c)),
                              _remote(land, land, ssem, rsem, i, (x, y, c))))
        return pairs

    shapes = [jax.ShapeDtypeStruct(p.shape, p.dtype) for p in parts]
    if into is None:
        return _symmetric_stage(parts, shapes, {}, 3 * n, copies)
    return _symmetric_stage(list(parts) + list(into), shapes, {n + w: w for w in range(n)}, 3 * n, copies)


def _swap_stage(totals):
    n = len(totals)

    def copies(ins, outs, ssem, rsem):
        x, y, c, _ = _place()
        return [(_remote(ins[w], outs[w], ssem, rsem, w, (x, y, 1 - c)),
                 _remote(outs[w], outs[w], ssem, rsem, w, (x, y, c))) for w in range(n)]

    shapes = [jax.ShapeDtypeStruct(t.shape, t.dtype) for t in totals]
    return _symmetric_stage(totals, shapes, {}, n, copies)


def _run_stages(stages, *, name):
    n_ins = [len(st.ins) for st in stages]
    n_outs = [len(st.out_shapes) for st in stages]
    tot_in, tot_out = sum(n_ins), sum(n_outs)
    aliases, i0, o0 = {}, 0, 0
    for st, ni, no in zip(stages, n_ins, n_outs):
        aliases.update({i0 + a: o0 + b for a, b in st.aliases.items()})
        i0, o0 = i0 + ni, o0 + no

    def body(*refs):
        sems = refs[tot_in + tot_out:]
        for what in ("issue", "wait"):
            i0, o0 = 0, tot_in
            for k, (st, ni, no) in enumerate(zip(stages, n_ins, n_outs)):
                getattr(st, what)(refs[i0:i0 + ni], refs[o0:o0 + no], sems[2 * k], sems[2 * k + 1])
                i0, o0 = i0 + ni, o0 + no

    sem = pltpu.SemaphoreType.DMA
    res = pl.pallas_call(
        body, name=name, in_specs=[ANY] * tot_in, out_specs=[ANY] * tot_out,
        out_shape=[s for st in stages for s in st.out_shapes], input_output_aliases=aliases,
        scratch_shapes=[sem((st.n_sem,)) for st in stages for _ in range(2)],
    )(*[a for st in stages for a in st.ins])
    outs, o0 = [], 0
    for no in n_outs:
        outs.append(list(res[o0:o0 + no]))
        o0 += no
    return outs


def _add_pair(meta, g4, recv, *, name):
    nsh, rows, cols = g4.shape
    half = rows // 2
    tr = _row_tile(half, 128 if cols > 1024 else 256, 16)
    nt = half // tr

    def body(meta_ref, g0, g1, g2, g3, r0, r1, r2, r3, own_ref, oth_ref):
        own_ref[...] = g0[...] + r0[...]
        for d, (g, r) in enumerate(((g1, r1), (g2, r2), (g3, r3))):
            oth_ref[d] = (g[...] + r[...]).astype(BF16)

    blk = (None, tr, cols)
    gspec = lambda d: pl.BlockSpec(blk, lambda i, m: (jnp.bitwise_xor(m[0], d), m[1] * nt + i, 0))
    rspec = lambda d: pl.BlockSpec(blk, lambda i, m: (jnp.bitwise_xor(m[0], d), i, 0))
    grid_spec = pltpu.PrefetchScalarGridSpec(
        num_scalar_prefetch=1, grid=(nt,),
        in_specs=[gspec(d) for d in range(nsh)] + [rspec(d) for d in range(nsh)],
        out_specs=[pl.BlockSpec((tr, cols), lambda i, m: (i, 0)), pl.BlockSpec((3, tr, cols), lambda i, m: (0, i, 0))])
    return pl.pallas_call(
        body, name=name, grid_spec=grid_spec,
        out_shape=[jax.ShapeDtypeStruct((half, cols), F32), jax.ShapeDtypeStruct((3, half, cols), BF16)],
        compiler_params=_params(("parallel",)),
    )(meta, g4, g4, g4, g4, recv, recv, recv, recv)


def _add_chips(own, recv, *, name):
    half, cols = own.shape
    tr = _row_tile(half, 256, 16)

    def body(p_ref, r_ref, o_ref):
        o_ref[...] = ((p_ref[...] + r_ref[0].astype(F32)) + r_ref[1].astype(F32)) + r_ref[2].astype(F32)

    return pl.pallas_call(
        body, name=name, grid=(half // tr,),
        in_specs=[pl.BlockSpec((tr, cols), lambda i: (i, 0)), pl.BlockSpec((3, tr, cols), lambda i: (0, i, 0))],
        out_specs=pl.BlockSpec((tr, cols), lambda i: (i, 0)),
        out_shape=jax.ShapeDtypeStruct((half, cols), F32),
        compiler_params=_params(("parallel",)),
    )(own, recv)


def _adamw_math(w, g, m, v):
    m = ADAM_B1 * m + (1.0 - ADAM_B1) * g
    v = ADAM_B2 * v + (1.0 - ADAM_B2) * (g * g)
    m_hat = m / (1.0 - ADAM_B1 ** ADAM_STEP)
    v_hat = v / (1.0 - ADAM_B2 ** ADAM_STEP)
    delta = -ADAM_LR * (m_hat / (jnp.sqrt(v_hat) + ADAM_EPS) + ADAM_WD * w)
    return delta, m, v


def _adamw(meta, w, g_mine, g_theirs, m, v, *, name):
    rows, cols = w.shape
    half = rows // 2
    tr = _row_tile(half, 256)
    nt = half // tr

    def body(meta_ref, w_ref, a_ref, b_ref, m_ref, v_ref, g_ref, d_ref, mo_ref, vo_ref):
        is_mine = (pl.program_id(0) // nt) == meta_ref[1]
        g = jnp.where(is_mine, a_ref[...], b_ref[...])
        g_ref[...] = g
        d_ref[...], mo_ref[...], vo_ref[...] = _adamw_math(w_ref[...], g, m_ref[...], v_ref[...])

    blk = pl.BlockSpec((tr, cols), lambda i, mt: (i, 0))
    mine = pl.BlockSpec((tr, cols), lambda i, mt: (jnp.where(i // nt == mt[1], i % nt, 0), 0))
    theirs = pl.BlockSpec((tr, cols), lambda i, mt: (jnp.where(i // nt == mt[1], 0, i % nt), 0))
    sh = jax.ShapeDtypeStruct((rows, cols), F32)
    grid_spec = pltpu.PrefetchScalarGridSpec(
        num_scalar_prefetch=1, grid=(rows // tr,),
        in_specs=[blk, mine, theirs, blk, blk], out_specs=[blk] * 4)
    return pl.pallas_call(
        body, name=name, grid_spec=grid_spec, out_shape=[sh] * 4,
        compiler_params=_params(("arbitrary",)),
    )(meta, w, g_mine, g_theirs, m, v)


N_DEVICES = 8


def _small_step(g_pack, w_pack, m_pack, v_pack):
    rows = g_pack.shape[0]

    def body(g_ref, w_ref, m_ref, v_ref, sum_ref, d_ref, mo_ref, vo_ref, slots, ssem, rsem):
        x, y, c, _ = _place()
        me = 4 * x + 2 * y + c
        slots[me] = g_ref[...]
        copies = []
        for r in range(1, N_DEVICES):
            px = 1 - x if r & 4 else x
            py = 1 - y if r & 2 else y
            pc = 1 - c if r & 1 else c
            copies.append(pltpu.make_async_remote_copy(
                src_ref=g_ref, dst_ref=slots.at[me], send_sem=ssem.at[r - 1], recv_sem=rsem.at[r - 1],
                device_id=(px, py, pc), device_id_type=MESH))
        for cp in copies:
            cp.start()
        for r in range(1, N_DEVICES):
            src = jnp.bitwise_xor(me, r)
            pltpu.make_async_remote_copy(
                src_ref=g_ref, dst_ref=slots.at[src], send_sem=ssem.at[r - 1], recv_sem=rsem.at[r - 1],
                device_id=(x, y, c), device_id_type=MESH).wait_recv()
        for cp in copies:
            cp.wait_send()
        total = slots[0]
        for k in range(1, N_DEVICES):
            total = total + slots[k]
        sum_ref[...] = total
        d_ref[...], mo_ref[...], vo_ref[...] = _adamw_math(w_ref[...], total, m_ref[...], v_ref[...])

    sh = jax.ShapeDtypeStruct((rows, LANES), F32)
    vm = pl.BlockSpec(memory_space=pltpu.VMEM)
    return pl.pallas_call(
        body, name="small_allreduce_adamw",
        in_specs=[vm] * 4, out_specs=[vm] * 4, out_shape=[sh] * 4,
        scratch_shapes=[pltpu.VMEM((N_DEVICES, rows, LANES), F32),
                        pltpu.SemaphoreType.DMA((N_DEVICES - 1,)), pltpu.SemaphoreType.DMA((N_DEVICES - 1,))],
    )(g_pack, w_pack, m_pack, v_pack)


WEIGHTS = ("attn_norm_g", "w_in", "swa_q_norm_g", "swa_k_norm_g", "swa_sinks", "mla_cq_norm_g", "mla_ckv_norm_g",
           "w_uq", "w_ukv", "mla_qn_norm_g", "mla_qr_norm_g", "mla_kn_norm_g", "mla_kr_norm_g", "mem_norm_g",
           "w_mem_kv", "mem_q_norm_g", "mem_k_norm_g", "w_out", "ffn_norm_g", "w_gate", "w_up", "w_down")
BIG = ("w_in", "w_uq", "w_ukv", "w_mem_kv", "w_out", "w_gate", "w_up", "w_down")
SMALL = tuple(n for n in WEIGHTS if n not in BIG)
PACK_UNIT = 8 * LANES


def _pack(parts):
    out = []
    for p in parts:
        n = p.shape[1]
        padded = -(-n // PACK_UNIT) * PACK_UNIT
        out.append(jnp.pad(p, ((0, 0), (0, padded - n))).reshape(padded // LANES, LANES))
    return jnp.concatenate(out, axis=0)


def _unpack(buf, sizes):
    out, row = [], 0
    for n in sizes:
        rows = -(-n // PACK_UNIT) * 8
        out.append(buf[row:row + rows].reshape(1, rows * LANES)[:, :n])
        row += rows
    return out


def kernel(x, mem, positions, attn_norm_g, w_in, swa_q_norm_g, swa_k_norm_g, swa_sinks, mla_cq_norm_g, mla_ckv_norm_g, w_uq, w_ukv, mla_qn_norm_g, mla_qr_norm_g, mla_kn_norm_g, mla_kr_norm_g, mem_norm_g, w_mem_kv, mem_q_norm_g, mem_k_norm_g, w_out, ffn_norm_g, w_gate, w_up, w_down, loss_target, m_attn_norm_g, m_w_in, m_swa_q_norm_g, m_swa_k_norm_g, m_swa_sinks, m_mla_cq_norm_g, m_mla_ckv_norm_g, m_w_uq, m_w_ukv, m_mla_qn_norm_g, m_mla_qr_norm_g, m_mla_kn_norm_g, m_mla_kr_norm_g, m_mem_norm_g, m_w_mem_kv, m_mem_q_norm_g, m_mem_k_norm_g, m_w_out, m_ffn_norm_g, m_w_gate, m_w_up, m_w_down, v_attn_norm_g, v_w_in, v_swa_q_norm_g, v_swa_k_norm_g, v_swa_sinks, v_mla_cq_norm_g, v_mla_ckv_norm_g, v_w_uq, v_w_ukv, v_mla_qn_norm_g, v_mla_qr_norm_g, v_mla_kn_norm_g, v_mla_kr_norm_g, v_mem_norm_g, v_w_mem_kv, v_mem_q_norm_g, v_mem_k_norm_g, v_w_out, v_ffn_norm_g, v_w_gate, v_w_up, v_w_down):
    given = dict(locals())
    wts = {n: given[n] for n in WEIGHTS}
    mom_m = {n: given["m_" + n] for n in WEIGHTS}
    mom_v = {n: given["v_" + n] for n in WEIGHTS}

    mx, my, mc = lax.axis_index("x"), lax.axis_index("y"), lax.axis_index("c")
    meta = jnp.stack([2 * mx + my, mc]).astype(jnp.int32)
    x, mem, pos, target = x[0], mem[0], positions[0], loss_target[0]
    sp = {n: wts[n] for n in SMALL}
    s = x.shape[0]
    cos_t, sin_t = _rope_tables(pos)
    pos_f = pos.astype(F32)
    pos_col, pos_row = pos_f.reshape(s, 1), pos_f.reshape(1, s)
    g128 = _gain_table(sp)
    sinks = sp["swa_sinks"].reshape(SWA_Q_HEADS)
    gcq, gckv = sp["mla_cq_norm_g"], sp["mla_ckv_norm_g"]
    gs = {}

    slot = {n: _cast_into_slot(wts[n][0], meta, name="cast_" + n) for n in BIG if n not in ("w_gate", "w_up", "w_down")}
    first = [slot["w_in"], slot["w_uq"], slot["w_ukv"]]
    slot["w_gate"], [first] = _cast_into_slot(wts["w_gate"][0], meta, name="cast_w_gate",
                                              comm=[_gather_stage(first, "ici", (0, 4))])
    slot["w_up"], [first] = _cast_into_slot(wts["w_up"][0], meta, name="cast_w_up",
                                            comm=[_gather_stage(first, [("ici", (1, 4)), ("d2d", (0, 4))])])
    slot["w_down"], [first] = _cast_into_slot(wts["w_down"][0], meta, name="cast_w_down",
                                              comm=[_gather_stage(first, [("ici", (2, 4)), ("d2d", (1, 4))])])
    hn, [first] = _rms_fwd(x, sp["attn_norm_g"], name="attn_norm_fwd",
                           comm=[_gather_stage(first, [("ici", (3, 4)), ("d2d", (2, 4))])])
    [first] = _run_stages([_gather_stage(first, "d2d", (3, 4))], name="gather_first_last_d2d")
    w_in_f, w_uq_f, w_ukv_f = _full_w_in(first[0]), _full_heads(first[1], MLA_NOPE), _full_heads(first[2], MLA_NOPE)

    proj, [mid] = _matmul(hn, w_in_f, name="in_proj",
                          comm=[_gather_stage([slot["w_mem_kv"], slot["w_out"]], "ici")])
    (qa, ka, va, q_cat, k_cat, v_b, qm), [mid, wg] = _attn_prep_fwd(
        proj, g128, gcq, gckv, w_uq_f, w_ukv_f, cos_t, sin_t,
        comm=[_gather_stage(mid, "d2d"), _gather_stage([slot["w_gate"]], "ici", (0, 4))])
    w_mem_kv_f = mid[0].reshape(D_MODEL, 2 * MEM_HEADS * MEM_DIM)
    w_out_f = mid[1].reshape(D_MODEL, D_MODEL)
    mn_b, kv_m, km, vm = _mem_kv_fwd(mem, sp["mem_norm_g"], w_mem_kv_f, sp["mem_k_norm_g"])
    (y_a, y), [wg] = _swa_fwd(qa, ka, va, pos_col, pos_row, sinks,
                              comm=[_gather_stage(wg, [("ici", (k, 4)) for k in (1, 2, 3)])])
    (y_b, lse, y), [wu, wg] = _mla_fwd(
        q_cat, k_cat, v_b, y, comm=[_gather_stage([slot["w_up"]], "ici"), _gather_stage(wg, "d2d")])
    y_m, y = _mem_attn_fwd(qm, km, vm, y)
    h1, [wu] = _matmul(y, w_out_f, add=x, name="out_proj", comm=[_gather_stage(wu, "d2d")])
    w_gate_f, w_up_f = wg[0], wu[0]
    fn = _rms_fwd(h1, sp["ffn_norm_g"], name="ffn_norm_fwd")
    (gate, up, act), [wd] = _ffn_gate_up(fn, w_gate_f, w_up_f, comm=[_gather_stage([slot["w_down"]], "both")])
    w_down_f = wd[0].reshape(D_FF, D_MODEL)
    d_out, d_out_b, loss_tile = _matmul(act, w_down_f, add=h1, name="down_proj", tm=512, tk=D_FF, loss_target=target)

    add_pair = lambda n, g4, r: _add_pair(meta, g4, r, name="grad_add_pair_" + n)
    add_chips = lambda n, own, r: _add_chips(own, r, name="grad_add_chips_" + n)
    mine, theirs = {}, {}

    dw_down = _matmul(act, d_out_b, ta=True, name="dw_down", tm=1408, tn=1024, tk=2048)
    dw_down = dw_down.reshape(N_CHIPS, D_FF // N_CHIPS, D_MODEL)
    (d_gate, d_up), [[r]] = _ffn_bwd_act(d_out_b, w_down_f, gate, up, comm=[_halves_stage([dw_down])])
    own_d, oth_d = add_pair("w_down", dw_down, r)
    dw_gate, [rd] = _matmul(fn, d_gate, ta=True, name="dw_gate", tk=2048, tn=D_FF // N_CHIPS, out_split=N_CHIPS,
                            comm=[_chips_stage([oth_d], (0, 2))])
    dw_up, [[r], rd] = _matmul(fn, d_up, ta=True, name="dw_up", tk=2048, tn=D_FF // N_CHIPS, out_split=N_CHIPS,
                               comm=[_halves_stage([dw_gate]), _chips_stage([oth_d], (1, 2), into=rd)])
    mine["w_down"] = add_chips("w_down", own_d, rd[0])
    own_g, oth_g = add_pair("w_gate", dw_gate, r)
    d_fn, [rg, [theirs["w_down"]]] = _matmul(
        d_gate, w_gate_f, tb=True, b_split=True, name="dfn_gate", tm=512,
        comm=[_chips_stage([oth_g], (0, 2)), _swap_stage([mine["w_down"]])])
    d_fn, [[r], rg] = _matmul(d_up, w_up_f, tb=True, b_split=True, add=d_fn, name="dfn_up", tm=512,
                              comm=[_halves_stage([dw_up]), _chips_stage([oth_g], (1, 2), into=rg)])
    mine["w_gate"] = add_chips("w_gate", own_g, rg[0])
    own_u, oth_u = add_pair("w_up", dw_up, r)
    d_h1, d_h1_b, gs["ffn_norm_g"] = _rms_bwd(d_fn, h1, sp["ffn_norm_g"], d_out, name="ffn_norm_bwd")
    dw_out, [[theirs["w_gate"]]] = _matmul(y, d_h1_b, ta=True, name="dw_out", tk=2048,
                                           comm=[_swap_stage([mine["w_gate"]])])
    dw_out = dw_out.reshape(N_CHIPS, D_MODEL // N_CHIPS, D_MODEL)
    d_y, [[r]] = _matmul(d_h1_b, w_out_f, tb=True, name="dy", comm=[_halves_stage([dw_out])])
    own_o, oth_o = add_pair("w_out", dw_out, r)
    (d_qa, d_ka, d_va, d_sink), [ru] = _swa_bwd(qa, ka, va, pos_col, pos_row, sinks, y_a, d_y,
                                                comm=[_chips_stage([oth_u], (0, 2))])
    (d_qcat, d_kcat, d_vb), [ru, [r]] = _mla_bwd(
        q_cat, k_cat, v_b, y_b, lse, d_y, comm=[_chips_stage([oth_u], (1, 2), into=ru), _chips_stage([oth_o])])
    mine["w_up"] = add_chips("w_up", own_u, ru[0])
    mine["w_out"] = add_chips("w_out", own_o, r)
    d_qm, d_km, d_vm = _mem_attn_bwd(qm, km, vm, y_m, d_y)
    (d_proj, dw_uq, dw_ukv, dg128, gs["mla_cq_norm_g"], gs["mla_ckv_norm_g"]), [[theirs["w_up"], theirs["w_out"]]] = \
        _attn_prep_bwd(proj, g128, gcq, gckv, w_uq_f, w_ukv_f, cos_t, sin_t, d_qa, d_ka, d_va, d_qcat, d_kcat, d_vb,
                       d_qm, comm=[_swap_stage([mine["w_up"], mine["w_out"]])])
    dw_mem_kv, gs["mem_norm_g"], gs["mem_k_norm_g"] = _mem_kv_bwd(
        mem, sp["mem_norm_g"], w_mem_kv_f, sp["mem_k_norm_g"], mn_b, kv_m, d_km, d_vm)
    late = ("w_uq", "w_ukv", "w_mem_kv")
    late_g = [_shards_heads(dw_uq, MLA_NOPE, MLA_ROPE), _shards_heads(dw_ukv, MLA_NOPE, MLA_V),
              dw_mem_kv.reshape(N_CHIPS, D_MODEL // N_CHIPS, -1)]
    dw_in, [rs] = _matmul(hn, d_proj, ta=True, name="dw_in", tk=2048, comm=[_halves_stage(late_g)])
    late_sums = [add_pair(n, g4, r) for n, g4, r in zip(late, late_g, rs)]
    dw_in = _shards_w_in(dw_in)
    d_hn, [rs, [r]] = _matmul(d_proj, w_in_f, tb=True, name="dhn", tk=1536,
                              comm=[_chips_stage([oth for _, oth in late_sums]), _halves_stage([dw_in])])
    for n, (own, _), r_n in zip(late, late_sums, rs):
        mine[n] = add_chips(n, own, r_n)
    own_i, oth_i = add_pair("w_in", dw_in, r)
    (grad_x, _, gs["attn_norm_g"]), [[r], late_theirs] = _rms_bwd(
        d_hn, x, sp["attn_norm_g"], d_h1, name="attn_norm_bwd",
        comm=[_chips_stage([oth_i]), _swap_stage([mine[n] for n in late])])
    theirs.update(zip(late, late_theirs))
    mine["w_in"] = add_chips("w_in", own_i, r)
    [[theirs["w_in"]]] = _run_stages([_swap_stage([mine["w_in"]])], name="grad_swap_w_in")

    fold = lambda r: r[:, :64] + r[:, 64:]
    gs["swa_q_norm_g"] = fold(dg128[G_SWA_Q:G_SWA_Q + 1])
    gs["swa_k_norm_g"] = fold(dg128[G_SWA_K:G_SWA_K + 1])
    gs["mla_qn_norm_g"] = dg128[G_QN:G_QN + 1]
    gs["mla_qr_norm_g"] = fold(dg128[G_QR:G_QR + 1])
    gs["mla_kn_norm_g"] = dg128[G_KN:G_KN + 1]
    gs["mla_kr_norm_g"] = fold(dg128[G_KR:G_KR + 1])
    gs["mem_q_norm_g"] = dg128[G_MQ:G_MQ + 1]
    gs["swa_sinks"] = d_sink[:, :SWA_Q_HEADS]

    grad, delta, new_m, new_v = {}, {}, {}, {}
    for n in BIG:
        g2, d, m2, v2 = _adamw(meta, wts[n][0], mine[n], theirs[n], mom_m[n][0], mom_v[n][0], name="adamw_" + n)
        grad[n], delta[n], new_m[n], new_v[n] = g2[None], d[None], m2[None], v2[None]

    sizes = [wts[n].shape[1] for n in SMALL]
    zero = jnp.zeros((1, LANES), F32)
    packs = _small_step(_pack([gs[n] for n in SMALL] + [loss_tile]), _pack([wts[n] for n in SMALL] + [zero]),
                        _pack([mom_m[n] for n in SMALL] + [zero]), _pack([mom_v[n] for n in SMALL] + [zero]))
    for store, buf in zip((grad, delta, new_m, new_v), packs):
        for n, val in zip(SMALL, _unpack(buf, sizes)):
            store[n] = val
    loss = _unpack(packs[0], sizes + [LANES])[-1][0, 0]

    return (loss, grad_x[None], *[grad[n] for n in WEIGHTS], *[delta[n] for n in WEIGHTS],
            *[new_m[n] for n in WEIGHTS], *[new_v[n] for n in WEIGHTS])
```

```python
import functools
import math

import jax
import jax.numpy as jnp
from jax import lax
from jax.experimental import pallas as pl
from jax.experimental.pallas import tpu as pltpu

F32 = jnp.float32
BF16 = jnp.bfloat16

D_MODEL = 2048
BLOCK = 128
EPS = 1e-6
NEG_INF = -1e30
SWA_Q_HEADS = 16
SWA_KV_HEADS = 2
SWA_HEAD_DIM = 64
MLA_HEADS = 4
MLA_RANK = 512
MLA_NOPE = 128
MLA_ROPE = 64
MLA_V = 128
ROPE_THETA = 10000.0
MEM_HEADS = 4
MEM_DIM = 128
D_FF = 5632
IN_WIDTH = 2880
IN_PAD = 3072
N_CHIPS = 4

ADAM_LR = 0.001
ADAM_B1 = 0.9
ADAM_B2 = 0.999
ADAM_EPS = 1e-08
ADAM_WD = 0.01
ADAM_STEP = 10

VMEM_LIMIT_BYTES = 56 * 1024 * 1024
LANES = 128

MESH = pl.DeviceIdType.MESH


def _params(sem=None, **kw):
    return pltpu.CompilerParams(dimension_semantics=sem, vmem_limit_bytes=VMEM_LIMIT_BYTES, **kw)


def _tile(n, want):
    if n <= want:
        return n
    t = want - want % LANES
    while t > 0:
        if n % t == 0:
            return t
        t -= LANES
    return n


ANY = pl.BlockSpec(memory_space=pl.ANY)


class _Stage:
    def __init__(self, ins, out_shapes, aliases, n_sem, issue, wait, mid=None):
        self.ins, self.out_shapes, self.aliases, self.n_sem = list(ins), list(out_shapes), dict(aliases), n_sem
        self.issue, self.wait, self.mid = issue, wait, mid


def _pcall(body, args, *, name, grid, in_specs, out_specs, out_shape, scratch_shapes=(), sem=None, comm=(),
           prefetch=(), io_alias=None):
    multi = isinstance(out_shape, (list, tuple))
    out_specs_l = list(out_specs) if multi else [out_specs]
    out_shape_l = list(out_shape) if multi else [out_shape]
    npf = len(prefetch)
    own_aliases = {npf + a: o for a, o in (io_alias or {}).items()}

    def call(fn, in_specs_, out_specs_, out_shape_, scratch_, operands, sem_, aliases=None):
        kw = dict(name=name, out_shape=out_shape_, compiler_params=_params(sem_))
        if aliases:
            kw["input_output_aliases"] = aliases
        if npf:
            spec = pltpu.PrefetchScalarGridSpec(num_scalar_prefetch=npf, grid=grid, in_specs=in_specs_,
                                                out_specs=out_specs_, scratch_shapes=scratch_)
            return pl.pallas_call(fn, grid_spec=spec, **kw)(*prefetch, *operands)
        return pl.pallas_call(fn, grid=grid, in_specs=in_specs_, out_specs=out_specs_, scratch_shapes=scratch_,
                              **kw)(*operands)

    if not comm:
        return call(body, list(in_specs), out_specs, out_shape, list(scratch_shapes), args, sem, own_aliases)
    n_in, n_out, n_scr = len(in_specs), len(out_specs_l), len(scratch_shapes)
    cins = [a for st in comm for a in st.ins]
    couts = [s for st in comm for s in st.out_shapes]
    aliases, ci, co = dict(own_aliases), 0, 0
    for st in comm:
        for a_i, o_i in st.aliases.items():
            aliases[npf + n_in + ci + a_i] = n_out + co + o_i
        ci, co = ci + len(st.ins), co + len(st.out_shapes)

    def wrapped(*refs):
        pre = refs[:npf]
        p = npf
        ins = refs[p:p + n_in]; p += n_in
        cin_refs = refs[p:p + len(cins)]; p += len(cins)
        outs = refs[p:p + n_out]; p += n_out
        cout_refs = refs[p:p + len(couts)]; p += len(couts)
        scr = refs[p:p + n_scr]; p += n_scr
        sems = refs[p:]
        first = functools.reduce(jnp.logical_and, [pl.program_id(a) == 0 for a in range(len(grid))])
        last = functools.reduce(jnp.logical_and, [pl.program_id(a) == grid[a] - 1 for a in range(len(grid))])

        def each(what):
            i, o = 0, 0
            for k, st in enumerate(comm):
                fn = getattr(st, what)
                if fn is not None:
                    fn(cin_refs[i:i + len(st.ins)], cout_refs[o:o + len(st.out_shapes)], sems[2 * k], sems[2 * k + 1])
                i, o = i + len(st.ins), o + len(st.out_shapes)

        @pl.when(first)
        def _():
            each("issue")

        if any(st.mid is not None for st in comm):
            n_steps = math.prod(grid)
            assert n_steps >= 4, "a two-leg stage needs a carrier with several grid steps"
            lin = functools.reduce(lambda acc, a: acc * grid[a] + pl.program_id(a), range(len(grid)), 0)

            @pl.when(lin == (3 * n_steps) // 4)
            def _():
                each("mid")

        body(*pre, *ins, *outs, *scr)

        @pl.when(last)
        def _():
            each("wait")

    sem_scr = [pltpu.SemaphoreType.DMA((st.n_sem,)) for st in comm for _ in range(2)]
    res = call(wrapped, list(in_specs) + [ANY] * len(cins), out_specs_l + [ANY] * len(couts), out_shape_l + couts,
               list(scratch_shapes) + sem_scr, (*args, *cins), ("arbitrary",) * len(grid), aliases)
    normal = list(res[:n_out])
    stage_outs, o = [], n_out
    for st in comm:
        stage_outs.append(list(res[o:o + len(st.out_shapes)]))
        o += len(st.out_shapes)
    return (normal if multi else normal[0]), stage_outs


def _matmul(a, b, *, name, ta=False, tb=False, add=None, out_dtype=F32, tm=1024, tn=1024, tk=2048,
            b_split=False, out_split=0, comm=(), loss_target=None):
    if ta:
        kdim, m = a.shape
    else:
        m, kdim = a.shape
    if b_split:
        assert tb
        nsp, n, kb = b.shape
        kb = kb * nsp
    elif tb:
        n, kb = b.shape
    else:
        kb, n = b.shape
    assert kb == kdim, (a.shape, b.shape, ta, tb)
    if b_split:
        tk = kdim
    if out_split:
        tn = _tile(n // out_split, tn)
    tm, tn, tk = _tile(m, tm), _tile(n, tn), _tile(kdim, tk)
    nk = kdim // tk
    dims = (((0 if ta else 1,), (1 if tb else 0,)), ((), ()))

    def product(a_ref, b_ref):
        if not b_split:
            return lax.dot_general(a_ref[...].astype(BF16), b_ref[...].astype(BF16), dims, preferred_element_type=F32)
        per = kdim // nsp
        return sum(lax.dot_general(a_ref[:, per * c:per * (c + 1)].astype(BF16), b_ref[c].astype(BF16), dims,
                                   preferred_element_type=F32) for c in range(nsp))

    def body(*refs):
        a_ref, b_ref = refs[:2]
        add_ref = refs[2] if add is not None else None
        n_in = 2 + (add is not None) + (loss_target is not None)
        o_ref = refs[n_in]

        def finish(r):
            if add_ref is not None:
                r = r + add_ref[...].astype(F32)
            if loss_target is None:
                o_ref[...] = r.astype(o_ref.dtype)
                return
            db_ref, l_ref = refs[n_in + 1], refs[n_in + 2]
            err = r - refs[n_in - 1][...]
            d_out = err * (1.0 / n)
            o_ref[...] = d_out
            db_ref[...] = d_out.astype(BF16)
            part = jnp.broadcast_to((0.5 / n) * jnp.sum(jnp.sum(err * err, axis=-1, keepdims=True), axis=0, keepdims=True),
                                    (1, LANES))
            first = jnp.logical_and(pl.program_id(0) == 0, pl.program_id(1) == 0)

            @pl.when(first)
            def _():
                l_ref[...] = part

            @pl.when(jnp.logical_not(first))
            def _():
                l_ref[...] += part

        if nk == 1:
            finish(product(a_ref, b_ref))
            return
        acc_ref = refs[-1]
        k = pl.program_id(2)
        part = product(a_ref, b_ref)

        @pl.when(k == 0)
        def _():
            acc_ref[...] = part

        @pl.when(k > 0)
        def _():
            acc_ref[...] += part

        @pl.when(k == nk - 1)
        def _():
            finish(acc_ref[...])

    a_spec = pl.BlockSpec((tk, tm), lambda i, j, k: (k, i)) if ta else pl.BlockSpec((tm, tk), lambda i, j, k: (i, k))
    if b_split:
        b_spec = pl.BlockSpec((nsp, tn, kdim // nsp), lambda i, j, k: (0, j, 0))
    elif tb:
        b_spec = pl.BlockSpec((tn, tk), lambda i, j, k: (j, k))
    else:
        b_spec = pl.BlockSpec((tk, tn), lambda i, j, k: (k, j))
    in_specs = [a_spec, b_spec]
    args = [a, b]
    if add is not None:
        in_specs.append(pl.BlockSpec((tm, tn), lambda i, j, k: (i, j)))
        args.append(add)
    tile = pl.BlockSpec((tm, tn), lambda i, j, k: (i, j))
    sem = ("parallel", "parallel", "arbitrary")
    if out_split:
        per = (n // out_split) // tn
        out_spec = pl.BlockSpec((None, tm, tn), lambda i, j, k: (j // per, i, j % per))
        out_shape = jax.ShapeDtypeStruct((out_split, m, n // out_split), out_dtype)
    elif loss_target is not None:
        in_specs.append(tile)
        args.append(loss_target)
        out_spec = [tile, tile, pl.BlockSpec((1, LANES), lambda i, j, k: (0, 0))]
        out_shape = [jax.ShapeDtypeStruct((m, n), F32), jax.ShapeDtypeStruct((m, n), BF16),
                     jax.ShapeDtypeStruct((1, LANES), F32)]
        sem = ("arbitrary",) * 3
    else:
        out_spec = tile
        out_shape = jax.ShapeDtypeStruct((m, n), out_dtype)
    return _pcall(body, args, name=name, grid=(m // tm, n // tn, nk), in_specs=in_specs, out_specs=out_spec,
                  out_shape=out_shape, scratch_shapes=[pltpu.VMEM((tm, tn), F32)] if nk > 1 else [],
                  sem=sem, comm=comm)


def _rms_fwd(x, g, *, name, tm=512, comm=()):
    s, d = x.shape
    tm = _tile(s, tm)

    def body(x_ref, g_ref, o_ref):
        xv = x_ref[...]
        r = lax.rsqrt(jnp.mean(xv * xv, axis=-1, keepdims=True) + EPS)
        o_ref[...] = (xv * r * g_ref[...]).astype(o_ref.dtype)

    return _pcall(body, (x, g), name=name, grid=(s // tm,),
                  in_specs=[pl.BlockSpec((tm, d), lambda i: (i, 0)), pl.BlockSpec((1, d), lambda i: (0, 0))],
                  out_specs=pl.BlockSpec((tm, d), lambda i: (i, 0)),
                  out_shape=jax.ShapeDtypeStruct((s, d), BF16), sem=("parallel",), comm=comm)


def _rms_bwd(dy, x, g, res, *, name, tm=512, comm=()):
    s, d = x.shape
    tm = _tile(s, tm)

    def body(dy_ref, x_ref, g_ref, res_ref, dx_ref, dxb_ref, dg_ref):
        xv = x_ref[...]
        dyv = dy_ref[...]
        r = lax.rsqrt(jnp.mean(xv * xv, axis=-1, keepdims=True) + EPS)
        xhat = xv * r
        dyg = dyv * g_ref[...]
        mt = jnp.mean(dyg * xhat, axis=-1, keepdims=True)
        dx = res_ref[...] + r * (dyg - xhat * mt)
        dx_ref[...] = dx
        dxb_ref[...] = dx.astype(BF16)
        part = jnp.sum(dyv * xhat, axis=0, keepdims=True)

        @pl.when(pl.program_id(0) == 0)
        def _():
            dg_ref[...] = part

        @pl.when(pl.program_id(0) > 0)
        def _():
            dg_ref[...] += part

    row = pl.BlockSpec((tm, d), lambda i: (i, 0))
    vec = pl.BlockSpec((1, d), lambda i: (0, 0))
    return _pcall(body, (dy, x, g, res), name=name, grid=(s // tm,), in_specs=[row, row, vec, row],
                  out_specs=[row, row, vec],
                  out_shape=[jax.ShapeDtypeStruct((s, d), F32), jax.ShapeDtypeStruct((s, d), BF16),
                             jax.ShapeDtypeStruct((1, d), F32)],
                  sem=("arbitrary",), comm=comm)


def _lane(shape):
    return lax.broadcasted_iota(jnp.int32, shape, 1)


def _halfsum(t, lo):
    s_lo = jnp.sum(jnp.where(lo, t, 0.0), axis=-1, keepdims=True)
    s_hi = jnp.sum(jnp.where(lo, 0.0, t), axis=-1, keepdims=True)
    return jnp.where(lo, s_lo, s_hi)


def _norm_pair(x, g, lo):
    r = lax.rsqrt(_halfsum(x * x, lo) * (1.0 / 64.0) + EPS)
    xhat = x * r
    return xhat * g, xhat, r


def _norm_pair_bwd(dy, g, xhat, r, lo):
    dyg = dy * g
    mt = _halfsum(dyg * xhat, lo) * (1.0 / 64.0)
    return r * (dyg - xhat * mt), jnp.sum(dy * xhat, axis=0, keepdims=True)


def _norm_full(x, g):
    r = lax.rsqrt(jnp.mean(x * x, axis=-1, keepdims=True) + EPS)
    xhat = x * r
    return xhat * g, xhat, r


def _norm_full_bwd(dy, g, xhat, r):
    dyg = dy * g
    mt = jnp.mean(dyg * xhat, axis=-1, keepdims=True)
    return r * (dyg - xhat * mt), jnp.sum(dy * xhat, axis=0, keepdims=True)


def _rot(x, first32):
    return jnp.where(first32, pltpu.roll(x, 96, axis=1), pltpu.roll(x, 32, axis=1))


def _rope(x, cos_t, sin_t, first32):
    return x * cos_t + _rot(x, first32) * sin_t


def _rope_bwd(dy, cos_t, sin_t, first32):
    return dy * cos_t + _rot(dy * sin_t, first32)


G_SWA_Q, G_SWA_K, G_QN, G_QR, G_KN, G_KR, G_MQ = range(7)

C_QA, C_KA, C_VA, C_CQ, C_CKV, C_QM, C_KR = 0, 1024, 1152, 1280, 1792, 2304, 2816


def _prep_common(p_ref, g128_ref, gcq_ref, gckv_ref, wuq_ref, wukv_ref, cos_ref, sin_ref):
    tm = p_ref.shape[0]
    lane = _lane((tm, LANES))
    lo = lane < 64
    first32 = (lane % 64) < 32
    cos_t = cos_ref[...]
    sin_t = sin_ref[...]
    g = lambda row: g128_ref[row:row + 1, :]
    out = dict(lo=lo, first32=first32, cos_t=cos_t, sin_t=sin_t, lane=lane)
    cq_n, cq_hat, cq_r = _norm_full(p_ref[:, C_CQ:C_CQ + MLA_RANK], gcq_ref[...])
    ckv_n, ckv_hat, ckv_r = _norm_full(p_ref[:, C_CKV:C_CKV + MLA_RANK], gckv_ref[...])
    cq_b = cq_n.astype(BF16)
    ckv_b = ckv_n.astype(BF16)
    q_b = jnp.dot(cq_b, wuq_ref[...], preferred_element_type=F32)
    kv_b = jnp.dot(ckv_b, wukv_ref[...], preferred_element_type=F32)
    out.update(cq_b=cq_b, cq_hat=cq_hat, cq_r=cq_r, ckv_b=ckv_b, ckv_hat=ckv_hat, ckv_r=ckv_r, q_b=q_b, kv_b=kv_b, g=g)
    return out


def _attn_prep_fwd(proj, g128, gcq, gckv, wuq, wukv, cos_t, sin_t, *, tm=512, comm=()):
    s = proj.shape[0]
    tm = _tile(s, tm)

    def body(p_ref, g128_ref, gcq_ref, gckv_ref, wuq_ref, wukv_ref, cos_ref, sin_ref,
             qa_ref, ka_ref, va_ref, qcat_ref, kcat_ref, vb_ref, qm_ref):
        c = _prep_common(p_ref, g128_ref, gcq_ref, gckv_ref, wuq_ref, wukv_ref, cos_ref, sin_ref)
        lo, first32, g = c["lo"], c["first32"], c["g"]
        for j in range(SWA_Q_HEADS // 2):
            y, _, _ = _norm_pair(p_ref[:, C_QA + 128 * j:C_QA + 128 * (j + 1)], g(G_SWA_Q), lo)
            qa_ref[:, 128 * j:128 * (j + 1)] = y.astype(BF16)
        y, _, _ = _norm_pair(p_ref[:, C_KA:C_KA + 128], g(G_SWA_K), lo)
        ka_ref[...] = y.astype(BF16)
        va_ref[...] = p_ref[:, C_VA:C_VA + 128].astype(BF16)
        kr, _, _ = _norm_pair(p_ref[:, C_KR:C_KR + 128], g(G_KR), lo)
        kr = jnp.where(lo, _rope(kr, c["cos_t"], c["sin_t"], first32), 0.0)
        krkr = (kr + pltpu.roll(kr, 64, axis=1)).astype(BF16)
        q_b, kv_b = c["q_b"], c["kv_b"]
        qr = []
        for j in range(MLA_HEADS // 2):
            y, _, _ = _norm_pair(q_b[:, 512 + 128 * j:512 + 128 * (j + 1)], g(G_QR), lo)
            qr.append(_rope(y, c["cos_t"], c["sin_t"], first32))
        for h in range(MLA_HEADS):
            qn, _, _ = _norm_full(q_b[:, 128 * h:128 * (h + 1)], g(G_QN))
            keep = lo if h % 2 == 0 else jnp.logical_not(lo)
            qcat_ref[h, :, 0:128] = qn.astype(BF16)
            qcat_ref[h, :, 128:256] = jnp.where(keep, qr[h // 2], 0.0).astype(BF16)
            kn, _, _ = _norm_full(kv_b[:, 128 * h:128 * (h + 1)], g(G_KN))
            kcat_ref[h, :, 0:128] = kn.astype(BF16)
            kcat_ref[h, :, 128:256] = krkr
        vb_ref[...] = kv_b[:, 512:1024].astype(BF16)
        for h in range(MEM_HEADS):
            y, _, _ = _norm_full(p_ref[:, C_QM + 128 * h:C_QM + 128 * (h + 1)], g(G_MQ))
            qm_ref[:, 128 * h:128 * (h + 1)] = y.astype(BF16)

    row = lambda w: pl.BlockSpec((tm, w), lambda i: (i, 0))
    full = lambda shape: pl.BlockSpec(shape, lambda i: tuple(0 for _ in shape))
    cat = pl.BlockSpec((MLA_HEADS, tm, 256), lambda i: (0, i, 0))
    return _pcall(
        body, (proj, g128, gcq, gckv, wuq, wukv, cos_t, sin_t), name="attn_prep_fwd", grid=(s // tm,),
        in_specs=[row(IN_PAD), full((8, 128)), full((1, 512)), full((1, 512)), full((512, 768)), full((512, 1024)),
                  row(128), row(128)],
        out_specs=[row(1024), row(128), row(128), cat, cat, row(512), row(512)],
        out_shape=[jax.ShapeDtypeStruct((s, 1024), BF16), jax.ShapeDtypeStruct((s, 128), BF16),
                   jax.ShapeDtypeStruct((s, 128), BF16), jax.ShapeDtypeStruct((MLA_HEADS, s, 256), BF16),
                   jax.ShapeDtypeStruct((MLA_HEADS, s, 256), BF16), jax.ShapeDtypeStruct((s, 512), BF16),
                   jax.ShapeDtypeStruct((s, 512), BF16)],
        sem=("parallel",), comm=comm)


def _attn_prep_bwd(proj, g128, gcq, gckv, wuq, wukv, cos_t, sin_t,
                   d_qa, d_ka, d_va, d_qcat, d_kcat, d_vb, d_qm, *, tm=256, comm=()):
    s = proj.shape[0]
    tm = _tile(s, tm)

    def body(p_ref, g128_ref, gcq_ref, gckv_ref, wuq_ref, wukv_ref, cos_ref, sin_ref,
             dqa_ref, dka_ref, dva_ref, dqcat_ref, dkcat_ref, dvb_ref, dqm_ref,
             dp_ref, dwuq_ref, dwukv_ref, dg128_ref, dgcq_ref, dgckv_ref):
        c = _prep_common(p_ref, g128_ref, gcq_ref, gckv_ref, wuq_ref, wukv_ref, cos_ref, sin_ref)
        lo, first32, g = c["lo"], c["first32"], c["g"]
        cos_v, sin_v = c["cos_t"], c["sin_t"]
        q_b, kv_b = c["q_b"], c["kv_b"]
        zero_row = jnp.zeros((1, LANES), F32)
        dg = {k: zero_row for k in range(7)}

        for j in range(SWA_Q_HEADS // 2):
            sl = slice(C_QA + 128 * j, C_QA + 128 * (j + 1))
            _, xhat, r = _norm_pair(p_ref[:, sl], g(G_SWA_Q), lo)
            dx, dgj = _norm_pair_bwd(dqa_ref[:, 128 * j:128 * (j + 1)], g(G_SWA_Q), xhat, r, lo)
            dp_ref[:, sl] = dx.astype(BF16)
            dg[G_SWA_Q] = dg[G_SWA_Q] + dgj
        _, xhat, r = _norm_pair(p_ref[:, C_KA:C_KA + 128], g(G_SWA_K), lo)
        dx, dgj = _norm_pair_bwd(dka_ref[...], g(G_SWA_K), xhat, r, lo)
        dp_ref[:, C_KA:C_KA + 128] = dx.astype(BF16)
        dg[G_SWA_K] = dgj
        dp_ref[:, C_VA:C_VA + 128] = dva_ref[...].astype(BF16)

        dqb_parts = [None] * 6
        for h in range(MLA_HEADS):
            _, xhat, r = _norm_full(q_b[:, 128 * h:128 * (h + 1)], g(G_QN))
            dx, dgj = _norm_full_bwd(dqcat_ref[h, :, 0:128], g(G_QN), xhat, r)
            dqb_parts[h] = dx
            dg[G_QN] = dg[G_QN] + dgj
        for j in range(MLA_HEADS // 2):
            _, xhat, r = _norm_pair(q_b[:, 512 + 128 * j:512 + 128 * (j + 1)], g(G_QR), lo)
            d_rot = jnp.where(lo, dqcat_ref[2 * j, :, 128:256], dqcat_ref[2 * j + 1, :, 128:256])
            d_y = _rope_bwd(d_rot, cos_v, sin_v, first32)
            dx, dgj = _norm_pair_bwd(d_y, g(G_QR), xhat, r, lo)
            dqb_parts[4 + j] = dx
            dg[G_QR] = dg[G_QR] + dgj
        d_qb = jnp.concatenate(dqb_parts, axis=1).astype(BF16)
        dwuq = lax.dot_general(c["cq_b"], d_qb, (((0,), (0,)), ((), ())), preferred_element_type=F32)
        d_cqn = lax.dot_general(d_qb, wuq_ref[...], (((1,), (1,)), ((), ())), preferred_element_type=F32)
        dx, dgcq = _norm_full_bwd(d_cqn, gcq_ref[...], c["cq_hat"], c["cq_r"])
        dp_ref[:, C_CQ:C_CQ + MLA_RANK] = dx.astype(BF16)

        dkv_parts = []
        d_krkr = jnp.zeros((p_ref.shape[0], LANES), F32)
        for h in range(MLA_HEADS):
            _, xhat, r = _norm_full(kv_b[:, 128 * h:128 * (h + 1)], g(G_KN))
            dx, dgj = _norm_full_bwd(dkcat_ref[h, :, 0:128], g(G_KN), xhat, r)
            dkv_parts.append(dx)
            dg[G_KN] = dg[G_KN] + dgj
            d_krkr = d_krkr + dkcat_ref[h, :, 128:256]
        d_kvb = jnp.concatenate(dkv_parts + [dvb_ref[...]], axis=1).astype(BF16)
        dwukv = lax.dot_general(c["ckv_b"], d_kvb, (((0,), (0,)), ((), ())), preferred_element_type=F32)
        d_ckvn = lax.dot_general(d_kvb, wukv_ref[...], (((1,), (1,)), ((), ())), preferred_element_type=F32)
        dx, dgckv = _norm_full_bwd(d_ckvn, gckv_ref[...], c["ckv_hat"], c["ckv_r"])
        dp_ref[:, C_CKV:C_CKV + MLA_RANK] = dx.astype(BF16)

        _, xhat, r = _norm_pair(p_ref[:, C_KR:C_KR + 128], g(G_KR), lo)
        d_kr = jnp.where(lo, d_krkr + pltpu.roll(d_krkr, 64, axis=1), 0.0)
        d_y = jnp.where(lo, _rope_bwd(d_kr, cos_v, sin_v, first32), 0.0)
        dx, dgj = _norm_pair_bwd(d_y, g(G_KR), xhat, r, lo)
        dp_ref[:, C_KR:C_KR + 128] = jnp.where(lo, dx, 0.0).astype(BF16)
        dp_ref[:, C_KR + 128:] = jnp.zeros((p_ref.shape[0], IN_PAD - C_KR - 128), BF16)
        dg[G_KR] = dgj

        for h in range(MEM_HEADS):
            sl = slice(C_QM + 128 * h, C_QM + 128 * (h + 1))
            _, xhat, r = _norm_full(p_ref[:, sl], g(G_MQ))
            dx, dgj = _norm_full_bwd(dqm_ref[:, 128 * h:128 * (h + 1)], g(G_MQ), xhat, r)
            dp_ref[:, sl] = dx.astype(BF16)
            dg[G_MQ] = dg[G_MQ] + dgj

        dg_tile = jnp.concatenate([dg[k] for k in range(7)] + [zero_row], axis=0)

        @pl.when(pl.program_id(0) == 0)
        def _():
            dwuq_ref[...] = dwuq
            dwukv_ref[...] = dwukv
            dg128_ref[...] = dg_tile
            dgcq_ref[...] = dgcq
            dgckv_ref[...] = dgckv

        @pl.when(pl.program_id(0) > 0)
        def _():
            dwuq_ref[...] += dwuq
            dwukv_ref[...] += dwukv
            dg128_ref[...] += dg_tile
            dgcq_ref[...] += dgcq
            dgckv_ref[...] += dgckv

    row = lambda w: pl.BlockSpec((tm, w), lambda i: (i, 0))
    full = lambda shape: pl.BlockSpec(shape, lambda i: tuple(0 for _ in shape))
    cat = pl.BlockSpec((MLA_HEADS, tm, 256), lambda i: (0, i, 0))
    return _pcall(
        body, (proj, g128, gcq, gckv, wuq, wukv, cos_t, sin_t, d_qa, d_ka, d_va, d_qcat, d_kcat, d_vb, d_qm),
        name="attn_prep_bwd", grid=(s // tm,),
        in_specs=[row(IN_PAD), full((8, 128)), full((1, 512)), full((1, 512)), full((512, 768)), full((512, 1024)),
                  row(128), row(128),
                  row(1024), row(128), row(128), cat, cat, row(512), row(512)],
        out_specs=[row(IN_PAD), full((512, 768)), full((512, 1024)), full((8, 128)), full((1, 512)), full((1, 512))],
        out_shape=[jax.ShapeDtypeStruct((s, IN_PAD), BF16), jax.ShapeDtypeStruct((512, 768), F32),
                   jax.ShapeDtypeStruct((512, 1024), F32), jax.ShapeDtypeStruct((8, 128), F32),
                   jax.ShapeDtypeStruct((1, 512), F32), jax.ShapeDtypeStruct((1, 512), F32)],
        sem=("arbitrary",), comm=comm)


SWA_SLOPES = tuple(2.0 ** (-8.0 * h / SWA_Q_HEADS) for h in range(1, SWA_Q_HEADS + 1))
SWA_SCALE = SWA_HEAD_DIM ** -0.5
NT_DIMS = (((1,), (1,)), ((), ()))
TN_DIMS = (((0,), (0,)), ((), ()))


def _swa_span(n, kp_ref, kc_ref, vp_ref, vc_ref, pcol_ref, pprow_ref, pcrow_ref):
    k_span = jnp.concatenate([kp_ref[...], kc_ref[...]], axis=0).astype(F32)
    v_span = jnp.concatenate([vp_ref[...], vc_ref[...]], axis=0).astype(F32)
    lo = _lane((2 * BLOCK, LANES)) < 64
    k_sw = pltpu.roll(k_span, 64, axis=1)
    v_sw = pltpu.roll(v_span, 64, axis=1)
    kk = (jnp.where(lo, k_span, k_sw).astype(BF16), jnp.where(lo, k_sw, k_span).astype(BF16))
    vv_lo = (jnp.where(lo, v_span, 0.0).astype(BF16), jnp.where(lo, v_sw, 0.0).astype(BF16))
    vv_hi = (jnp.where(lo, 0.0, v_sw).astype(BF16), jnp.where(lo, 0.0, v_span).astype(BF16))
    pk = jnp.concatenate([pprow_ref[...], pcrow_ref[...]], axis=1)
    dist = jnp.abs(pcol_ref[...] - pk)
    qi = lax.broadcasted_iota(jnp.int32, (BLOCK, 2 * BLOCK), 0)
    ki = lax.broadcasted_iota(jnp.int32, (BLOCK, 2 * BLOCK), 1)
    first_key = jnp.where(n > 0, qi + 1, jnp.maximum(qi + 1, BLOCK))
    valid = jnp.logical_and(ki >= first_key, ki <= qi + BLOCK)
    mask_add = jnp.where(valid, 0.0, NEG_INF)
    return kk, vv_lo, vv_hi, dist, mask_add


def _swa_heads(q_ref, lo):
    heads = []
    for j in range(SWA_Q_HEADS // 2):
        q_pair = q_ref[:, 128 * j:128 * (j + 1)].astype(F32)
        for par in (0, 1):
            q_h = jnp.where(lo if par == 0 else jnp.logical_not(lo), q_pair, 0.0).astype(BF16)
            heads.append((2 * j + par, (2 * j) // (SWA_Q_HEADS // SWA_KV_HEADS), par, q_h))
    return heads


def _swa_probs(raw, dist, mask_add, slope, sink):
    s = raw * SWA_SCALE - slope * dist + mask_add
    m = jnp.maximum(jnp.max(s, axis=-1, keepdims=True), sink)
    e = jnp.exp(s - m)
    e_sink = jnp.exp(sink - m)
    inv = 1.0 / (jnp.sum(e, axis=-1, keepdims=True) + e_sink)
    return e * inv, e_sink * inv


def _swa_specs():
    blk = lambda w: pl.BlockSpec((BLOCK, w), lambda n: (n, 0))
    prev = lambda w: pl.BlockSpec((BLOCK, w), lambda n: (jnp.maximum(n - 1, 0), 0))
    prow_c = pl.BlockSpec((1, BLOCK), lambda n: (0, n))
    prow_p = pl.BlockSpec((1, BLOCK), lambda n: (0, jnp.maximum(n - 1, 0)))
    smem = pl.BlockSpec(memory_space=pltpu.SMEM)
    return [blk(1024), prev(128), blk(128), prev(128), blk(128), blk(1), prow_p, prow_c, smem], blk


def _swa_fwd(qa, ka, va, pos_col, pos_row, sinks, *, comm=()):
    s = qa.shape[0]
    in_specs, blk = _swa_specs()

    def body(q_ref, kp_ref, kc_ref, vp_ref, vc_ref, pcol_ref, pprow_ref, pcrow_ref, sink_ref, o_ref, yb_ref):
        n = pl.program_id(0)
        kk, vv_lo, vv_hi, dist, mask_add = _swa_span(n, kp_ref, kc_ref, vp_ref, vc_ref, pcol_ref, pprow_ref, pcrow_ref)
        lo = _lane((BLOCK, LANES)) < 64
        heads = _swa_heads(q_ref, lo)
        raws = [lax.dot_general(q_h, kk[kv], NT_DIMS, preferred_element_type=F32) for _, kv, _, q_h in heads]
        probs = [_swa_probs(raw, dist, mask_add, SWA_SLOPES[h], sink_ref[h])[0].astype(BF16)
                 for raw, (h, _, _, _) in zip(raws, heads)]
        for j in range(SWA_Q_HEADS // 2):
            kv = heads[2 * j][1]
            out = (jnp.dot(probs[2 * j], vv_lo[kv], preferred_element_type=F32)
                   + jnp.dot(probs[2 * j + 1], vv_hi[kv], preferred_element_type=F32))
            o_ref[:, 128 * j:128 * (j + 1)] = out
            yb_ref[:, 128 * j:128 * (j + 1)] = out.astype(BF16)

    return _pcall(body, (qa, ka, ka, va, va, pos_col, pos_row, pos_row, sinks), name="swa_fwd", grid=(s // BLOCK,),
                  in_specs=in_specs, out_specs=[blk(1024), blk(1024)],
                  out_shape=[jax.ShapeDtypeStruct((s, 1024), F32), jax.ShapeDtypeStruct((s, D_MODEL), BF16)],
                  sem=("parallel",), comm=comm)


def _swa_bwd(qa, ka, va, pos_col, pos_row, sinks, y_a, d_y, *, comm=()):
    s = qa.shape[0]
    in_specs, blk = _swa_specs()
    whole = pl.BlockSpec((s, 128), lambda n: (0, 0))

    def body(q_ref, kp_ref, kc_ref, vp_ref, vc_ref, pcol_ref, pprow_ref, pcrow_ref, sink_ref, y_ref, dy_ref,
             dq_ref, dk_ref, dv_ref, dsink_ref):
        n = pl.program_id(0)

        @pl.when(n == 0)
        def _():
            dk_ref[...] = jnp.zeros_like(dk_ref)
            dv_ref[...] = jnp.zeros_like(dv_ref)
            dsink_ref[...] = jnp.zeros_like(dsink_ref)

        kk, vv_lo, vv_hi, dist, mask_add = _swa_span(n, kp_ref, kc_ref, vp_ref, vc_ref, pcol_ref, pprow_ref, pcrow_ref)
        lo = _lane((BLOCK, LANES)) < 64
        lo2 = _lane((2 * BLOCK, LANES)) < 64
        lane1 = _lane((1, LANES))
        dsink = jnp.zeros((1, LANES), F32)
        dkk = [jnp.zeros((2 * BLOCK, LANES), F32) for _ in range(SWA_KV_HEADS)]
        dvv = [jnp.zeros((2 * BLOCK, LANES), F32) for _ in range(SWA_KV_HEADS)]
        heads = _swa_heads(q_ref, lo)
        do_b, deltas = [], []
        for j in range(SWA_Q_HEADS // 2):
            do_pair = dy_ref[:, 128 * j:128 * (j + 1)]
            doy = do_pair * y_ref[:, 128 * j:128 * (j + 1)]
            do_b.append(do_pair.astype(BF16))
            deltas.append(jnp.sum(jnp.where(lo, doy, 0.0), axis=-1, keepdims=True))
            deltas.append(jnp.sum(jnp.where(lo, 0.0, doy), axis=-1, keepdims=True))
        raws = [lax.dot_general(q_h, kk[kv], NT_DIMS, preferred_element_type=F32) for _, kv, _, q_h in heads]
        dps = [lax.dot_general(do_b[h // 2], (vv_lo, vv_hi)[par][kv], NT_DIMS, preferred_element_type=F32)
               for h, kv, par, _ in heads]
        p_b, ds_b = [], []
        for h, kv, par, _ in heads:
            p, p_sink = _swa_probs(raws[h], dist, mask_add, SWA_SLOPES[h], sink_ref[h])
            ds = p * (dps[h] - deltas[h])
            dsink = dsink + jnp.where(lane1 == h, -jnp.sum(p_sink * deltas[h], axis=0, keepdims=True), 0.0)
            p_b.append(p.astype(BF16))
            ds_b.append((ds * SWA_SCALE).astype(BF16))
        dq_halves = []
        for h, kv, par, q_h in heads:
            dq_halves.append(jnp.dot(ds_b[h], kk[kv], preferred_element_type=F32))
            dkk[kv] = dkk[kv] + lax.dot_general(ds_b[h], q_h, TN_DIMS, preferred_element_type=F32)
            pv = lax.dot_general(p_b[h], do_b[h // 2], TN_DIMS, preferred_element_type=F32)
            dvv[kv] = dvv[kv] + jnp.where(lo2 if par == 0 else jnp.logical_not(lo2), pv, 0.0)
        for j in range(SWA_Q_HEADS // 2):
            dq_ref[:, 128 * j:128 * (j + 1)] = jnp.where(lo, dq_halves[2 * j], dq_halves[2 * j + 1])
        fold = lambda t: t + pltpu.roll(t, 64, axis=1)
        dk_span = jnp.where(lo2, fold(dkk[0]), fold(dkk[1]))
        dv_span = jnp.where(lo2, fold(dvv[0]), fold(dvv[1]))
        prev0 = pl.multiple_of(jnp.maximum(n - 1, 0) * BLOCK, BLOCK)
        cur0 = pl.multiple_of(n * BLOCK, BLOCK)
        dk_ref[pl.ds(prev0, BLOCK), :] += dk_span[0:BLOCK]
        dk_ref[pl.ds(cur0, BLOCK), :] += dk_span[BLOCK:]
        dv_ref[pl.ds(prev0, BLOCK), :] += dv_span[0:BLOCK]
        dv_ref[pl.ds(cur0, BLOCK), :] += dv_span[BLOCK:]
        dsink_ref[...] += dsink

    return _pcall(
        body, (qa, ka, ka, va, va, pos_col, pos_row, pos_row, sinks, y_a, d_y), name="swa_bwd", grid=(s // BLOCK,),
        in_specs=in_specs + [blk(1024), blk(1024)],
        out_specs=[blk(1024), whole, whole, pl.BlockSpec((1, LANES), lambda n: (0, 0))],
        out_shape=[jax.ShapeDtypeStruct((s, 1024), F32), jax.ShapeDtypeStruct((s, 128), F32),
                   jax.ShapeDtypeStruct((s, 128), F32), jax.ShapeDtypeStruct((1, LANES), F32)],
        sem=("arbitrary",), comm=comm)


MLA_SCALE = (MLA_NOPE + MLA_ROPE) ** -0.5
MLA_TILE = 1024


def _tile_pairs(nt, q_major):
    pairs = [(i, j) for i in range(nt) for j in range(i + 1)] if q_major else \
            [(i, j) for j in range(nt) for i in range(j, nt)]
    return jnp.asarray([p[0] for p in pairs], jnp.int32), jnp.asarray([p[1] for p in pairs], jnp.int32)


def _diag_mask(t):
    return lax.broadcasted_iota(jnp.int32, (t, t), 1) <= lax.broadcasted_iota(jnp.int32, (t, t), 0)


def _mla_fwd(q_cat, k_cat, v_b, y_all, *, comm=()):
    nh, s, _ = q_cat.shape
    t = _tile(s, MLA_TILE)
    qi, kj = _tile_pairs(s // t, True)
    ycol = (SWA_Q_HEADS * SWA_HEAD_DIM) // (nh * MLA_V)

    def body(qi_ref, kj_ref, q_ref, k_ref, v_ref, _, o_ref, lse_ref, yb_ref, m_sc, l_sc, acc_sc):
        i, j = qi_ref[pl.program_id(0)], kj_ref[pl.program_id(0)]

        @pl.when(j == 0)
        def _():
            m_sc[...] = jnp.full_like(m_sc, NEG_INF)
            l_sc[...] = jnp.zeros_like(l_sc)
            acc_sc[...] = jnp.zeros_like(acc_sc)

        def update(diagonal):
            scores = [lax.dot_general(q_ref[h], k_ref[h], NT_DIMS, preferred_element_type=F32) for h in range(nh)]
            probs, alphas = [], []
            for h in range(nh):
                sc = scores[h] * MLA_SCALE
                if diagonal:
                    sc = jnp.where(_diag_mask(t), sc, NEG_INF)
                m_old = m_sc[h]
                m_new = jnp.maximum(m_old, jnp.max(sc, axis=-1, keepdims=True))
                alpha = jnp.exp(m_old - m_new)
                p = jnp.exp(sc - m_new)
                l_sc[h] = alpha * l_sc[h] + jnp.sum(p, axis=-1, keepdims=True)
                m_sc[h] = m_new
                probs.append(p.astype(BF16))
                alphas.append(alpha)
            for h in range(nh):
                acc_sc[h] = alphas[h] * acc_sc[h] + jnp.dot(probs[h], v_ref[:, MLA_V * h:MLA_V * (h + 1)],
                                                            preferred_element_type=F32)

        @pl.when(j < i)
        def _():
            update(False)

        @pl.when(j == i)
        def _():
            update(True)
            for h in range(nh):
                out = acc_sc[h] * (1.0 / l_sc[h])
                o_ref[:, MLA_V * h:MLA_V * (h + 1)] = out
                yb_ref[:, MLA_V * h:MLA_V * (h + 1)] = out.astype(BF16)
                lse_ref[h] = m_sc[h] + jnp.log(l_sc[h])

    return _pcall(
        body, (q_cat, k_cat, v_b, y_all), name="mla_fwd", grid=(qi.shape[0],), prefetch=(qi, kj),
        in_specs=[pl.BlockSpec((nh, t, 256), lambda p, qi, kj: (0, qi[p], 0)),
                  pl.BlockSpec((nh, t, 256), lambda p, qi, kj: (0, kj[p], 0)),
                  pl.BlockSpec((t, nh * MLA_V), lambda p, qi, kj: (kj[p], 0)), ANY],
        out_specs=[pl.BlockSpec((t, nh * MLA_V), lambda p, qi, kj: (qi[p], 0)),
                   pl.BlockSpec((nh, t, 1), lambda p, qi, kj: (0, qi[p], 0)),
                   pl.BlockSpec((t, nh * MLA_V), lambda p, qi, kj: (qi[p], ycol))],
        out_shape=[jax.ShapeDtypeStruct((s, nh * MLA_V), F32), jax.ShapeDtypeStruct((nh, s, 1), F32),
                   jax.ShapeDtypeStruct(y_all.shape, y_all.dtype)],
        scratch_shapes=[pltpu.VMEM((nh, t, 1), F32), pltpu.VMEM((nh, t, 1), F32), pltpu.VMEM((nh, t, MLA_V), F32)],
        sem=("arbitrary",), comm=comm, io_alias={3: 2})


def _mla_bwd(q_cat, k_cat, v_b, y_b, lse, d_y, *, comm=()):
    nh, s, _ = q_cat.shape
    t = _tile(s, MLA_TILE)
    nt = s // t
    hp = 2
    wv = hp * MLA_V
    col0 = (SWA_Q_HEADS * SWA_HEAD_DIM) // wv
    qi, kj = _tile_pairs(nt, False)

    def body(qi_ref, kj_ref, q_ref, k_ref, v_ref, y_ref, lse_ref, dy_ref, dq_ref, dk_ref, dv_ref, dk_sc, dv_sc):
        step = pl.program_id(1)
        i, j = qi_ref[step], kj_ref[step]

        @pl.when(step == 0)
        def _():
            dq_ref[...] = jnp.zeros_like(dq_ref)

        @pl.when(i == j)
        def _():
            dk_sc[...] = jnp.zeros_like(dk_sc)
            dv_sc[...] = jnp.zeros_like(dv_sc)

        def update(diagonal):
            rows = pl.ds(pl.multiple_of(i * t, t), t)
            cols = [slice(MLA_V * h, MLA_V * (h + 1)) for h in range(hp)]
            do_b = [dy_ref[:, cols[h]].astype(BF16) for h in range(hp)]
            scores = [lax.dot_general(q_ref[h], k_ref[h], NT_DIMS, preferred_element_type=F32) for h in range(hp)]
            dps = [lax.dot_general(do_b[h], v_ref[:, cols[h]], NT_DIMS, preferred_element_type=F32) for h in range(hp)]
            p_b, ds_b = [], []
            for h in range(hp):
                p = jnp.exp(scores[h] * MLA_SCALE - lse_ref[h])
                if diagonal:
                    p = jnp.where(_diag_mask(t), p, 0.0)
                delta = jnp.sum(dy_ref[:, cols[h]] * y_ref[:, cols[h]], axis=-1, keepdims=True)
                p_b.append(p.astype(BF16))
                ds_b.append((p * (dps[h] - delta) * MLA_SCALE).astype(BF16))
            for h in range(hp):
                dv_sc[h] += lax.dot_general(p_b[h], do_b[h], TN_DIMS, preferred_element_type=F32)
                dk_sc[h] += lax.dot_general(ds_b[h], q_ref[h], TN_DIMS, preferred_element_type=F32)
                dq_ref[h, rows, :] += jnp.dot(ds_b[h], k_ref[h], preferred_element_type=F32)

        @pl.when(i > j)
        def _():
            update(False)

        @pl.when(i == j)
        def _():
            update(True)

        @pl.when(i == nt - 1)
        def _():
            dk_ref[...] = dk_sc[...]
            for h in range(hp):
                dv_ref[:, MLA_V * h:MLA_V * (h + 1)] = dv_sc[h]

    return _pcall(
        body, (q_cat, k_cat, v_b, y_b, lse, d_y), name="mla_bwd", grid=(nh // hp, qi.shape[0]), prefetch=(qi, kj),
        in_specs=[pl.BlockSpec((hp, t, 256), lambda g, p, qi, kj: (g, qi[p], 0)),
                  pl.BlockSpec((hp, t, 256), lambda g, p, qi, kj: (g, kj[p], 0)),
                  pl.BlockSpec((t, wv), lambda g, p, qi, kj: (kj[p], g)),
                  pl.BlockSpec((t, wv), lambda g, p, qi, kj: (qi[p], g)),
                  pl.BlockSpec((hp, t, 1), lambda g, p, qi, kj: (g, qi[p], 0)),
                  pl.BlockSpec((t, wv), lambda g, p, qi, kj: (qi[p], col0 + g))],
        out_specs=[pl.BlockSpec((hp, s, 256), lambda g, p, qi, kj: (g, 0, 0)),
                   pl.BlockSpec((hp, t, 256), lambda g, p, qi, kj: (g, kj[p], 0)),
                   pl.BlockSpec((t, wv), lambda g, p, qi, kj: (kj[p], g))],
        out_shape=[jax.ShapeDtypeStruct((nh, s, 256), F32), jax.ShapeDtypeStruct((nh, s, 256), F32),
                   jax.ShapeDtypeStruct((s, nh * MLA_V), F32)],
        scratch_shapes=[pltpu.VMEM((hp, t, 256), F32), pltpu.VMEM((hp, t, MLA_V), F32)],
        sem=("arbitrary", "arbitrary"), comm=comm)


MEM_SCALE = MEM_DIM ** -0.5


def _mem_kv_fwd(mem, g_mem, w_memkv, g_mk):
    m_len = mem.shape[0]

    def body(mem_ref, g_ref, w_ref, gk_ref, mn_ref, kv_ref, kn_ref, v_ref):
        mn, _, _ = _norm_full(mem_ref[...], g_ref[...])
        mn_b = mn.astype(BF16)
        mn_ref[...] = mn_b
        kv = jnp.dot(mn_b, w_ref[...], preferred_element_type=F32)
        kv_ref[...] = kv
        for h in range(MEM_HEADS):
            kn, _, _ = _norm_full(kv[:, 128 * h:128 * (h + 1)], gk_ref[...])
            kn_ref[:, 128 * h:128 * (h + 1)] = kn.astype(BF16)
        v_ref[...] = kv[:, 512:1024].astype(BF16)

    return pl.pallas_call(
        body, name="mem_kv_fwd",
        out_shape=[jax.ShapeDtypeStruct((m_len, D_MODEL), BF16), jax.ShapeDtypeStruct((m_len, 1024), F32),
                   jax.ShapeDtypeStruct((m_len, 512), BF16), jax.ShapeDtypeStruct((m_len, 512), BF16)],
        compiler_params=_params(),
    )(mem, g_mem, w_memkv, g_mk)


def _mem_kv_bwd(mem, g_mem, w_memkv, g_mk, mn_b, kv, d_kn, d_v):
    m_len = mem.shape[0]

    def body(mem_ref, g_ref, w_ref, gk_ref, mn_ref, kv_ref, dkn_ref, dv_ref, dw_ref, dgmem_ref, dgk_ref):
        parts = []
        dgk = jnp.zeros((1, LANES), F32)
        for h in range(MEM_HEADS):
            _, xhat, r = _norm_full(kv_ref[:, 128 * h:128 * (h + 1)], gk_ref[...])
            dx, dgh = _norm_full_bwd(dkn_ref[:, 128 * h:128 * (h + 1)], gk_ref[...], xhat, r)
            parts.append(dx)
            dgk = dgk + dgh
        d_kv = jnp.concatenate(parts + [dv_ref[...]], axis=1).astype(BF16)
        dw_ref[...] = lax.dot_general(mn_ref[...], d_kv, TN_DIMS, preferred_element_type=F32)
        d_mn = lax.dot_general(d_kv, w_ref[...], NT_DIMS, preferred_element_type=F32)
        _, xhat, _ = _norm_full(mem_ref[...], g_ref[...])
        dgmem_ref[...] = jnp.sum(d_mn * xhat, axis=0, keepdims=True)
        dgk_ref[...] = dgk

    return pl.pallas_call(
        body, name="mem_kv_bwd",
        out_shape=[jax.ShapeDtypeStruct((D_MODEL, 1024), F32), jax.ShapeDtypeStruct((1, D_MODEL), F32),
                   jax.ShapeDtypeStruct((1, LANES), F32)],
        compiler_params=_params(),
    )(mem, g_mem, w_memkv, g_mk, mn_b, kv, d_kn, d_v)


def _mem_probs(q_h, k_h):
    sc = lax.dot_general(q_h, k_h, NT_DIMS, preferred_element_type=F32) * MEM_SCALE
    e = jnp.exp(sc - jnp.max(sc, axis=-1, keepdims=True))
    return e * (1.0 / jnp.sum(e, axis=-1, keepdims=True))


def _mem_attn_fwd(qm, km, vm, y_all, *, tm=512):
    s = qm.shape[0]
    tm = _tile(s, tm)
    m_len = km.shape[0]
    ycol = (SWA_Q_HEADS * SWA_HEAD_DIM + MLA_HEADS * MLA_V) // 512

    def body(q_ref, k_ref, v_ref, _, o_ref, yb_ref):
        for h in range(MEM_HEADS):
            sl = slice(128 * h, 128 * (h + 1))
            p = _mem_probs(q_ref[:, sl], k_ref[:, sl])
            out = jnp.dot(p.astype(BF16), v_ref[:, sl], preferred_element_type=F32)
            o_ref[:, sl] = out
            yb_ref[:, sl] = out.astype(BF16)

    kvspec = pl.BlockSpec((m_len, 512), lambda i: (0, 0))
    return _pcall(
        body, (qm, km, vm, y_all), name="mem_attn_fwd", grid=(s // tm,),
        in_specs=[pl.BlockSpec((tm, 512), lambda i: (i, 0)), kvspec, kvspec, ANY],
        out_specs=[pl.BlockSpec((tm, 512), lambda i: (i, 0)), pl.BlockSpec((tm, 512), lambda i: (i, ycol))],
        out_shape=[jax.ShapeDtypeStruct((s, 512), F32), jax.ShapeDtypeStruct(y_all.shape, y_all.dtype)],
        sem=("parallel",), io_alias={3: 1})


def _mem_attn_bwd(qm, km, vm, y_m, d_y, *, tm=512):
    s = qm.shape[0]
    tm = _tile(s, tm)
    m_len = km.shape[0]
    col0 = (SWA_Q_HEADS * SWA_HEAD_DIM + MLA_HEADS * MLA_V) // 512

    def body(q_ref, k_ref, v_ref, y_ref, dy_ref, dq_ref, dk_ref, dv_ref):
        @pl.when(pl.program_id(0) == 0)
        def _():
            dk_ref[...] = jnp.zeros_like(dk_ref)
            dv_ref[...] = jnp.zeros_like(dv_ref)

        for h in range(MEM_HEADS):
            sl = slice(128 * h, 128 * (h + 1))
            q_h, k_h = q_ref[:, sl], k_ref[:, sl]
            do = dy_ref[:, sl]
            do_b = do.astype(BF16)
            p = _mem_probs(q_h, k_h)
            delta = jnp.sum(do * y_ref[:, sl], axis=-1, keepdims=True)
            dv_ref[:, sl] += lax.dot_general(p.astype(BF16), do_b, TN_DIMS, preferred_element_type=F32)
            dp = lax.dot_general(do_b, v_ref[:, sl], NT_DIMS, preferred_element_type=F32)
            ds_b = (p * (dp - delta) * MEM_SCALE).astype(BF16)
            dq_ref[:, sl] = jnp.dot(ds_b, k_h, preferred_element_type=F32)
            dk_ref[:, sl] += lax.dot_general(ds_b, q_h, TN_DIMS, preferred_element_type=F32)

    kvspec = pl.BlockSpec((m_len, 512), lambda i: (0, 0))
    row = pl.BlockSpec((tm, 512), lambda i: (i, 0))
    return pl.pallas_call(
        body, name="mem_attn_bwd", grid=(s // tm,),
        in_specs=[row, kvspec, kvspec, row, pl.BlockSpec((tm, 512), lambda i: (i, col0))],
        out_specs=[row, kvspec, kvspec],
        out_shape=[jax.ShapeDtypeStruct((s, 512), F32), jax.ShapeDtypeStruct((m_len, 512), F32),
                   jax.ShapeDtypeStruct((m_len, 512), F32)],
        compiler_params=_params(("arbitrary",)),
    )(qm, km, vm, y_m, d_y)


def _ffn_gate_up(fn, w_gate, w_up, *, tm=512, comm=()):
    s, d = fn.shape
    nsp, _, tf = w_gate.shape
    f = nsp * tf
    tm = _tile(s, tm)

    def body(x_ref, wg_ref, wu_ref, g_ref, u_ref, a_ref):
        x = x_ref[...]
        gate = jnp.dot(x, wg_ref[...], preferred_element_type=F32)
        up = jnp.dot(x, wu_ref[...], preferred_element_type=F32)
        g_ref[...] = gate.astype(BF16)
        u_ref[...] = up.astype(BF16)
        a_ref[...] = (gate * (1.0 / (1.0 + jnp.exp(-gate))) * up).astype(BF16)

    wspec = pl.BlockSpec((None, d, tf), lambda j, i: (j, 0, 0))
    ospec = pl.BlockSpec((tm, tf), lambda j, i: (i, j))
    osh = jax.ShapeDtypeStruct((s, f), BF16)
    return _pcall(body, (fn, w_gate, w_up), name="ffn_gate_up", grid=(nsp, s // tm),
                  in_specs=[pl.BlockSpec((tm, d), lambda j, i: (i, 0)), wspec, wspec],
                  out_specs=[ospec, ospec, ospec], out_shape=[osh, osh, osh], sem=("parallel", "parallel"), comm=comm)


def _ffn_bwd_act(d_out, w_down, gate, up, *, tm=1024, tf=1408, comm=()):
    s, d = d_out.shape
    f = w_down.shape[0]
    tm, tf = _tile(s, tm), _tile(f, tf)

    sub = tm // 4 if tm % 1024 == 0 else tm

    def body(do_ref, wd_ref, g_ref, u_ref, dg_ref, du_ref):
        groups = [slice(r, r + sub) for r in range(0, tm, sub)]
        parts = [lax.dot_general(do_ref[rows, :].astype(BF16), wd_ref[...], NT_DIMS, preferred_element_type=F32)
                 for rows in groups]
        for rows, d_act in zip(groups, parts):
            gate = g_ref[rows, :].astype(F32)
            sig = 1.0 / (1.0 + jnp.exp(-gate))
            du_ref[rows, :] = (d_act * (gate * sig)).astype(BF16)
            dg_ref[rows, :] = (d_act * u_ref[rows, :].astype(F32) * (sig * (1.0 + gate * (1.0 - sig)))).astype(BF16)

    ospec = pl.BlockSpec((tm, tf), lambda j, i: (i, j))
    osh = jax.ShapeDtypeStruct((s, f), BF16)
    return _pcall(
        body, (d_out, w_down, gate, up), name="ffn_bwd_act", grid=(f // tf, s // tm),
        in_specs=[pl.BlockSpec((tm, d), lambda j, i: (i, 0)), pl.BlockSpec((tf, d), lambda j, i: (j, 0)), ospec, ospec],
        out_specs=[ospec, ospec], out_shape=[osh, osh], sem=("parallel", "parallel"), comm=comm)


def _cols(g4):
    return jnp.concatenate([g4[k] for k in range(N_CHIPS)], axis=1)


def _full_w_in(g4):
    per = IN_WIDTH // N_CHIPS
    kr0 = 2304 - (N_CHIPS - 1) * per
    last = g4[N_CHIPS - 1]
    pad = jnp.zeros((last.shape[0], IN_PAD - IN_WIDTH), last.dtype)
    return jnp.concatenate([g4[0], g4[1], g4[2], last[:, :kr0], last[:, kr0 + 64:], last[:, kr0:kr0 + 64], pad], axis=1)


def _shards_w_in(dwp):
    per = IN_WIDTH // N_CHIPS
    kr0 = 2304 - (N_CHIPS - 1) * per
    last = jnp.concatenate([dwp[:, (N_CHIPS - 1) * per:2304], dwp[:, C_KR:C_KR + 64], dwp[:, 2304:C_KR]], axis=1)
    assert last.shape[1] == per and kr0 == 144
    return jnp.stack([dwp[:, per * k:per * (k + 1)] for k in range(N_CHIPS - 1)] + [last])


def _full_heads(g4, first):
    return jnp.concatenate([g4[k][:, :first] for k in range(N_CHIPS)] + [g4[k][:, first:] for k in range(N_CHIPS)], axis=1)


def _shards_heads(dwp, first, rest):
    base = N_CHIPS * first
    return jnp.stack([jnp.concatenate([dwp[:, first * k:first * (k + 1)], dwp[:, base + rest * k:base + rest * (k + 1)]], axis=1)
                      for k in range(N_CHIPS)])


def _rope_tables(pos):
    inv_freq = ROPE_THETA ** (-jnp.arange(0, MLA_ROPE, 2, dtype=F32) / MLA_ROPE)
    ang = pos.astype(F32)[:, None] * inv_freq
    cos, sin = jnp.cos(ang), jnp.sin(ang)
    return jnp.tile(cos, (1, 4)), jnp.concatenate([-sin, sin, -sin, sin], axis=1)


def _gain_table(sp):
    two = lambda v: jnp.tile(v, (1, 2))
    rows = [two(sp["swa_q_norm_g"]), two(sp["swa_k_norm_g"]), sp["mla_qn_norm_g"], two(sp["mla_qr_norm_g"]),
            sp["mla_kn_norm_g"], two(sp["mla_kr_norm_g"]), sp["mem_q_norm_g"], jnp.zeros((1, LANES), F32)]
    return jnp.concatenate(rows, axis=0)


CHIP_DISTANCES = (1, 2, 3)


def _place():
    x, y, c = lax.axis_index("x"), lax.axis_index("y"), lax.axis_index("c")
    return x, y, c, 2 * x + y


def _chip_at(x, y, d):
    px = 1 - x if d & 2 else x
    py = 1 - y if d & 1 else y
    return px, py, 2 * px + py


def _row_tile(rows, want=512, mult=8):
    t = min(rows, want)
    t -= t % mult
    while rows % t:
        t -= mult
    return t


def _cast_into_slot(w, meta, *, name, comm=()):
    rows, cols = w.shape
    tr = _row_tile(rows, 512, 16)

    def body(meta_ref, w_ref, o_ref):
        o_ref[...] = w_ref[...].astype(BF16)

    return _pcall(body, (w,), name=name, grid=(rows // tr,), prefetch=(meta,),
                  in_specs=[pl.BlockSpec((tr, cols), lambda i, m: (i, 0))],
                  out_specs=pl.BlockSpec((None, tr, cols), lambda i, m: (m[0], i, 0)),
                  out_shape=jax.ShapeDtypeStruct((N_CHIPS, rows, cols), BF16), sem=("parallel",), comm=comm)


def _remote(src, dst, ssem, rsem, i, device):
    return pltpu.make_async_remote_copy(src_ref=src, dst_ref=dst, send_sem=ssem.at[i], recv_sem=rsem.at[i],
                                        device_id=device, device_id_type=MESH)


def _symmetric_stage(ins, out_shapes, aliases, n_sem, copies):
    def issue(i_refs, o_refs, ssem, rsem):
        for send, _ in copies(i_refs, o_refs, ssem, rsem):
            send.start()

    def wait(i_refs, o_refs, ssem, rsem):
        pairs = copies(i_refs, o_refs, ssem, rsem)
        for _, arrival in pairs:
            arrival.wait_recv()
        for send, _ in pairs:
            send.wait_send()

    return _Stage(ins, out_shapes, aliases, n_sem, issue, wait)


def _gather_stage(slots, leg, part=(0, 1)):
    n = len(slots)
    shapes = [jax.ShapeDtypeStruct(s.shape, s.dtype) for s in slots]
    in_place = {w: w for w in range(n)}
    if not isinstance(leg, str):
        legs = list(leg)

        def copies(i_refs, o_refs, ssem, rsem):
            return [pr for k, (which, prt) in enumerate(legs)
                    for pr in _gather_stage(slots, which, prt).leg_copies(which, 3 * n * k)(i_refs, o_refs, ssem, rsem)]

        return _symmetric_stage(slots, shapes, in_place, 3 * n * len(legs), copies)

    def leg_copies(which, base):
        def copies(_, outs, ssem, rsem):
            x, y, c, k_me = _place()
            pairs = []
            for w in range(n):
                half = outs[w].shape[1] // 2
                r0, size = _window(half, part)
                slab = lambda k, cc, w=w, half=half, r0=r0, size=size: outs[w].at[k, pl.ds(cc * half + r0, size)]
                for d in CHIP_DISTANCES:
                    px, py, k_src = _chip_at(x, y, d)
                    i = base + 3 * w + d - 1
                    if which == "ici":
                        pairs.append((_remote(slab(k_me, c), slab(k_me, c), ssem, rsem, i, (px, py, c)),
                                      _remote(slab(k_src, c), slab(k_src, c), ssem, rsem, i, (x, y, c))))
                    else:
                        pairs.append((_remote(slab(k_src, c), slab(k_src, c), ssem, rsem, i, (x, y, 1 - c)),
                                      _remote(slab(k_src, 1 - c), slab(k_src, 1 - c), ssem, rsem, i, (x, y, c))))
            return pairs
        return copies

    if leg != "both":
        st = _symmetric_stage(slots, shapes, in_place, 3 * n, leg_copies(leg, 0))
        st.leg_copies = leg_copies
        return st
    ici = _symmetric_stage(slots, shapes, in_place, 6 * n, leg_copies("ici", 0))
    d2d = _symmetric_stage(slots, shapes, in_place, 6 * n, leg_copies("d2d", 3 * n))

    def mid(*refs):
        ici.wait(*refs)
        d2d.issue(*refs)

    return _Stage(slots, shapes, in_place, 6 * n, ici.issue, d2d.wait, mid)


def _halves_stage(grads):
    n = len(grads)

    def copies(ins, outs, ssem, rsem):
        x, y, c, _ = _place()
        pairs = []
        for w in range(n):
            half = ins[w].shape[1] // 2
            pairs.append((_remote(ins[w].at[:, pl.ds((1 - c) * half, half)], outs[w], ssem, rsem, w, (x, y, 1 - c)),
                          _remote(outs[w], outs[w], ssem, rsem, w, (x, y, c))))
        return pairs

    shapes = [jax.ShapeDtypeStruct((N_CHIPS, g.shape[1] // 2, g.shape[2]), g.dtype) for g in grads]
    return _symmetric_stage(grads, shapes, {}, n, copies)


def _window(rows, part):
    idx, count = part
    size = rows // count
    assert size * count == rows and size % 16 == 0, (rows, part)
    return idx * size, size


def _chips_stage(parts, part=(0, 1), into=None):
    n = len(parts)

    def copies(ins, outs, ssem, rsem):
        x, y, c, _ = _place()
        pairs = []
        for w in range(n):
            r0, size = _window(ins[w].shape[1], part)
            for d in CHIP_DISTANCES:
                px, py, _ = _chip_at(x, y, d)
                i = 3 * w + d - 1
                land = outs[w].at[d - 1, pl.ds(r0, size)]
                pairs.append((_remote(ins[w].at[d - 1, pl.ds(r0, size)], land, ssem, rsem, i, (px, py, c)),
                              _remote(land, land, ssem, rsem, i, (x, y, c))))
        return pairs

    shapes = [jax.ShapeDtypeStruct(p.shape, p.dtype) for p in parts]
    if into is None:
        return _symmetric_stage(parts, shapes, {}, 3 * n, copies)
    return _symmetric_stage(list(parts) + list(into), shapes, {n + w: w for w in range(n)}, 3 * n, copies)


def _swap_stage(totals):
    n = len(totals)

    def copies(ins, outs, ssem, rsem):
        x, y, c, _ = _place()
        return [(_remote(ins[w], outs[w], ssem, rsem, w, (x, y, 1 - c)),
                 _remote(outs[w], outs[w], ssem, rsem, w, (x, y, c))) for w in range(n)]

    shapes = [jax.ShapeDtypeStruct(t.shape, t.dtype) for t in totals]
    return _symmetric_stage(totals, shapes, {}, n, copies)


def _run_stages(stages, *, name):
    n_ins = [len(st.ins) for st in stages]
    n_outs = [len(st.out_shapes) for st in stages]
    tot_in, tot_out = sum(n_ins), sum(n_outs)
    aliases, i0, o0 = {}, 0, 0
    for st, ni, no in zip(stages, n_ins, n_outs):
        aliases.update({i0 + a: o0 + b for a, b in st.aliases.items()})
        i0, o0 = i0 + ni, o0 + no

    def body(*refs):
        sems = refs[tot_in + tot_out:]
        for what in ("issue", "wait"):
            i0, o0 = 0, tot_in
            for k, (st, ni, no) in enumerate(zip(stages, n_ins, n_outs)):
                getattr(st, what)(refs[i0:i0 + ni], refs[o0:o0 + no], sems[2 * k], sems[2 * k + 1])
                i0, o0 = i0 + ni, o0 + no

    sem = pltpu.SemaphoreType.DMA
    res = pl.pallas_call(
        body, name=name, in_specs=[ANY] * tot_in, out_specs=[ANY] * tot_out,
        out_shape=[s for st in stages for s in st.out_shapes], input_output_aliases=aliases,
        scratch_shapes=[sem((st.n_sem,)) for st in stages for _ in range(2)],
    )(*[a for st in stages for a in st.ins])
    outs, o0 = [], 0
    for no in n_outs:
        outs.append(list(res[o0:o0 + no]))
        o0 += no
    return outs


def _add_pair(meta, g4, recv, *, name):
    nsh, rows, cols = g4.shape
    half = rows // 2
    tr = _row_tile(half, 128 if cols > 1024 else 256, 16)
    nt = half // tr

    def body(meta_ref, g0, g1, g2, g3, r0, r1, r2, r3, own_ref, oth_ref):
        own_ref[...] = g0[...] + r0[...]
        for d, (g, r) in enumerate(((g1, r1), (g2, r2), (g3, r3))):
            oth_ref[d] = (g[...] + r[...]).astype(BF16)

    blk = (None, tr, cols)
    gspec = lambda d: pl.BlockSpec(blk, lambda i, m: (jnp.bitwise_xor(m[0], d), m[1] * nt + i, 0))
    rspec = lambda d: pl.BlockSpec(blk, lambda i, m: (jnp.bitwise_xor(m[0], d), i, 0))
    grid_spec = pltpu.PrefetchScalarGridSpec(
        num_scalar_prefetch=1, grid=(nt,),
        in_specs=[gspec(d) for d in range(nsh)] + [rspec(d) for d in range(nsh)],
        out_specs=[pl.BlockSpec((tr, cols), lambda i, m: (i, 0)), pl.BlockSpec((3, tr, cols), lambda i, m: (0, i, 0))])
    return pl.pallas_call(
        body, name=name, grid_spec=grid_spec,
        out_shape=[jax.ShapeDtypeStruct((half, cols), F32), jax.ShapeDtypeStruct((3, half, cols), BF16)],
        compiler_params=_params(("parallel",)),
    )(meta, g4, g4, g4, g4, recv, recv, recv, recv)


def _add_chips(own, recv, *, name):
    half, cols = own.shape
    tr = _row_tile(half, 256, 16)

    def body(p_ref, r_ref, o_ref):
        o_ref[...] = ((p_ref[...] + r_ref[0].astype(F32)) + r_ref[1].astype(F32)) + r_ref[2].astype(F32)

    return pl.pallas_call(
        body, name=name, grid=(half // tr,),
        in_specs=[pl.BlockSpec((tr, cols), lambda i: (i, 0)), pl.BlockSpec((3, tr, cols), lambda i: (0, i, 0))],
        out_specs=pl.BlockSpec((tr, cols), lambda i: (i, 0)),
        out_shape=jax.ShapeDtypeStruct((half, cols), F32),
        compiler_params=_params(("parallel",)),
    )(own, recv)


def _adamw_math(w, g, m, v):
    m = ADAM_B1 * m + (1.0 - ADAM_B1) * g
    v = ADAM_B2 * v + (1.0 - ADAM_B2) * (g * g)
    m_hat = m / (1.0 - ADAM_B1 ** ADAM_STEP)
    v_hat = v / (1.0 - ADAM_B2 ** ADAM_STEP)
    delta = -ADAM_LR * (m_hat / (jnp.sqrt(v_hat) + ADAM_EPS) + ADAM_WD * w)
    return delta, m, v


def _adamw(meta, w, g_mine, g_theirs, m, v, *, name):
    rows, cols = w.shape
    half = rows // 2
    tr = _row_tile(half, 256)
    nt = half // tr

    def body(meta_ref, w_ref, a_ref, b_ref, m_ref, v_ref, g_ref, d_ref, mo_ref, vo_ref):
        is_mine = (pl.program_id(0) // nt) == meta_ref[1]
        g = jnp.where(is_mine, a_ref[...], b_ref[...])
        g_ref[...] = g
        d_ref[...], mo_ref[...], vo_ref[...] = _adamw_math(w_ref[...], g, m_ref[...], v_ref[...])

    blk = pl.BlockSpec((tr, cols), lambda i, mt: (i, 0))
    mine = pl.BlockSpec((tr, cols), lambda i, mt: (jnp.where(i // nt == mt[1], i % nt, 0), 0))
    theirs = pl.BlockSpec((tr, cols), lambda i, mt: (jnp.where(i // nt == mt[1], 0, i % nt), 0))
    sh = jax.ShapeDtypeStruct((rows, cols), F32)
    grid_spec = pltpu.PrefetchScalarGridSpec(
        num_scalar_prefetch=1, grid=(rows // tr,),
        in_specs=[blk, mine, theirs, blk, blk], out_specs=[blk] * 4)
    return pl.pallas_call(
        body, name=name, grid_spec=grid_spec, out_shape=[sh] * 4,
        compiler_params=_params(("arbitrary",)),
    )(meta, w, g_mine, g_theirs, m, v)


N_DEVICES = 8


def _small_step(g_pack, w_pack, m_pack, v_pack):
    rows = g_pack.shape[0]

    def body(g_ref, w_ref, m_ref, v_ref, sum_ref, d_ref, mo_ref, vo_ref, slots, ssem, rsem):
        x, y, c, _ = _place()
        me = 4 * x + 2 * y + c
        slots[me] = g_ref[...]
        copies = []
        for r in range(1, N_DEVICES):
            px = 1 - x if r & 4 else x
            py = 1 - y if r & 2 else y
            pc = 1 - c if r & 1 else c
            copies.append(pltpu.make_async_remote_copy(
                src_ref=g_ref, dst_ref=slots.at[me], send_sem=ssem.at[r - 1], recv_sem=rsem.at[r - 1],
                device_id=(px, py, pc), device_id_type=MESH))
        for cp in copies:
            cp.start()
        for r in range(1, N_DEVICES):
            src = jnp.bitwise_xor(me, r)
            pltpu.make_async_remote_copy(
                src_ref=g_ref, dst_ref=slots.at[src], send_sem=ssem.at[r - 1], recv_sem=rsem.at[r - 1],
                device_id=(x, y, c), device_id_type=MESH).wait_recv()
        for cp in copies:
            cp.wait_send()
        total = slots[0]
        for k in range(1, N_DEVICES):
            total = total + slots[k]
        sum_ref[...] = total
        d_ref[...], mo_ref[...], vo_ref[...] = _adamw_math(w_ref[...], total, m_ref[...], v_ref[...])

    sh = jax.ShapeDtypeStruct((rows, LANES), F32)
    vm = pl.BlockSpec(memory_space=pltpu.VMEM)
    return pl.pallas_call(
        body, name="small_allreduce_adamw",
        in_specs=[vm] * 4, out_specs=[vm] * 4, out_shape=[sh] * 4,
        scratch_shapes=[pltpu.VMEM((N_DEVICES, rows, LANES), F32),
                        pltpu.SemaphoreType.DMA((N_DEVICES - 1,)), pltpu.SemaphoreType.DMA((N_DEVICES - 1,))],
    )(g_pack, w_pack, m_pack, v_pack)


WEIGHTS = ("attn_norm_g", "w_in", "swa_q_norm_g", "swa_k_norm_g", "swa_sinks", "mla_cq_norm_g", "mla_ckv_norm_g",
           "w_uq", "w_ukv", "mla_qn_norm_g", "mla_qr_norm_g", "mla_kn_norm_g", "mla_kr_norm_g", "mem_norm_g",
           "w_mem_kv", "mem_q_norm_g", "mem_k_norm_g", "w_out", "ffn_norm_g", "w_gate", "w_up", "w_down")
BIG = ("w_in", "w_uq", "w_ukv", "w_mem_kv", "w_out", "w_gate", "w_up", "w_down")
SMALL = tuple(n for n in WEIGHTS if n not in BIG)
PACK_UNIT = 8 * LANES


def _pack(parts):
    out = []
    for p in parts:
        n = p.shape[1]
        padded = -(-n // PACK_UNIT) * PACK_UNIT
        out.append(jnp.pad(p, ((0, 0), (0, padded - n))).reshape(padded // LANES, LANES))
    return jnp.concatenate(out, axis=0)


def _unpack(buf, sizes):
    out, row = [], 0
    for n in sizes:
        rows = -(-n // PACK_UNIT) * 8
        out.append(buf[row:row + rows].reshape(1, rows * LANES)[:, :n])
        row += rows
    return out


def kernel(x, mem, positions, attn_norm_g, w_in, swa_q_norm_g, swa_k_norm_g, swa_sinks, mla_cq_norm_g, mla_ckv_norm_g, w_uq, w_ukv, mla_qn_norm_g, mla_qr_norm_g, mla_kn_norm_g, mla_kr_norm_g, mem_norm_g, w_mem_kv, mem_q_norm_g, mem_k_norm_g, w_out, ffn_norm_g, w_gate, w_up, w_down, loss_target, m_attn_norm_g, m_w_in, m_swa_q_norm_g, m_swa_k_norm_g, m_swa_sinks, m_mla_cq_norm_g, m_mla_ckv_norm_g, m_w_uq, m_w_ukv, m_mla_qn_norm_g, m_mla_qr_norm_g, m_mla_kn_norm_g, m_mla_kr_norm_g, m_mem_norm_g, m_w_mem_kv, m_mem_q_norm_g, m_mem_k_norm_g, m_w_out, m_ffn_norm_g, m_w_gate, m_w_up, m_w_down, v_attn_norm_g, v_w_in, v_swa_q_norm_g, v_swa_k_norm_g, v_swa_sinks, v_mla_cq_norm_g, v_mla_ckv_norm_g, v_w_uq, v_w_ukv, v_mla_qn_norm_g, v_mla_qr_norm_g, v_mla_kn_norm_g, v_mla_kr_norm_g, v_mem_norm_g, v_w_mem_kv, v_mem_q_norm_g, v_mem_k_norm_g, v_w_out, v_ffn_norm_g, v_w_gate, v_w_up, v_w_down):
    given = dict(locals())
    wts = {n: given[n] for n in WEIGHTS}
    mom_m = {n: given["m_" + n] for n in WEIGHTS}
    mom_v = {n: given["v_" + n] for n in WEIGHTS}

    mx, my, mc = lax.axis_index("x"), lax.axis_index("y"), lax.axis_index("c")
    meta = jnp.stack([2 * mx + my, mc]).astype(jnp.int32)
    x, mem, pos, target = x[0], mem[0], positions[0], loss_target[0]
    sp = {n: wts[n] for n in SMALL}
    s = x.shape[0]
    cos_t, sin_t = _rope_tables(pos)
    pos_f = pos.astype(F32)
    pos_col, pos_row = pos_f.reshape(s, 1), pos_f.reshape(1, s)
    g128 = _gain_table(sp)
    sinks = sp["swa_sinks"].reshape(SWA_Q_HEADS)
    gcq, gckv = sp["mla_cq_norm_g"], sp["mla_ckv_norm_g"]
    gs = {}

    slot = {n: _cast_into_slot(wts[n][0], meta, name="cast_" + n) for n in BIG if n not in ("w_gate", "w_up", "w_down")}
    first = [slot["w_in"], slot["w_uq"], slot["w_ukv"]]
    slot["w_gate"], [first] = _cast_into_slot(wts["w_gate"][0], meta, name="cast_w_gate",
                                              comm=[_gather_stage(first, "ici", (0, 4))])
    slot["w_up"], [first] = _cast_into_slot(wts["w_up"][0], meta, name="cast_w_up",
                                            comm=[_gather_stage(first, [("ici", (1, 4)), ("d2d", (0, 4))])])
    slot["w_down"], [first] = _cast_into_slot(wts["w_down"][0], meta, name="cast_w_down",
                                              comm=[_gather_stage(first, [("ici", (2, 4)), ("d2d", (1, 4))])])
    hn, [first] = _rms_fwd(x, sp["attn_norm_g"], name="attn_norm_fwd",
                           comm=[_gather_stage(first, [("ici", (3, 4)), ("d2d", (2, 4))])])
    [first] = _run_stages([_gather_stage(first, "d2d", (3, 4))], name="gather_first_last_d2d")
    w_in_f, w_uq_f, w_ukv_f = _full_w_in(first[0]), _full_heads(first[1], MLA_NOPE), _full_heads(first[2], MLA_NOPE)

    proj, [mid] = _matmul(hn, w_in_f, name="in_proj",
                          comm=[_gather_stage([slot["w_mem_kv"], slot["w_out"]], "ici")])
    (qa, ka, va, q_cat, k_cat, v_b, qm), [mid, wg] = _attn_prep_fwd(
        proj, g128, gcq, gckv, w_uq_f, w_ukv_f, cos_t, sin_t,
        comm=[_gather_stage(mid, "d2d"), _gather_stage([slot["w_gate"]], "ici", (0, 4))])
    w_mem_kv_f = mid[0].reshape(D_MODEL, 2 * MEM_HEADS * MEM_DIM)
    w_out_f = mid[1].reshape(D_MODEL, D_MODEL)
    mn_b, kv_m, km, vm = _mem_kv_fwd(mem, sp["mem_norm_g"], w_mem_kv_f, sp["mem_k_norm_g"])
    (y_a, y), [wg] = _swa_fwd(qa, ka, va, pos_col, pos_row, sinks,
                              comm=[_gather_stage(wg, [("ici", (k, 4)) for k in (1, 2, 3)])])
    (y_b, lse, y), [wu, wg] = _mla_fwd(
        q_cat, k_cat, v_b, y, comm=[_gather_stage([slot["w_up"]], [("ici", (k, 4)) for k in (0, 1, 2)]),
                                    _gather_stage(wg, "d2d")])
    y_m, y = _mem_attn_fwd(qm, km, vm, y)
    h1, [wu] = _matmul(y, w_out_f, add=x, name="out_proj",
                       comm=[_gather_stage(wu, [("ici", (3, 4))] + [("d2d", (k, 4)) for k in (0, 1, 2)])])
    fn, [wu] = _rms_fwd(h1, sp["ffn_norm_g"], name="ffn_norm_fwd", comm=[_gather_stage(wu, "d2d", (3, 4))])
    w_gate_f, w_up_f = wg[0], wu[0]
    (gate, up, act), [wd] = _ffn_gate_up(fn, w_gate_f, w_up_f, comm=[_gather_stage([slot["w_down"]], "both")])
    w_down_f = wd[0].reshape(D_FF, D_MODEL)
    d_out, d_out_b, loss_tile = _matmul(act, w_down_f, add=h1, name="down_proj", tm=512, tk=D_FF, loss_target=target)

    add_pair = lambda n, g4, r: _add_pair(meta, g4, r, name="grad_add_pair_" + n)
    add_chips = lambda n, own, r: _add_chips(own, r, name="grad_add_chips_" + n)
    mine, theirs = {}, {}

    dw_down = _matmul(act, d_out_b, ta=True, name="dw_down", tm=512, tn=1024, tk=s)
    dw_down = dw_down.reshape(N_CHIPS, D_FF // N_CHIPS, D_MODEL)
    (d_gate, d_up), [[r]] = _ffn_bwd_act(d_out_b, w_down_f, gate, up, comm=[_halves_stage([dw_down])])
    own_d, oth_d = add_pair("w_down", dw_down, r)
    dw_gate, [rd] = _matmul(fn, d_gate, ta=True, name="dw_gate", tm=512, tk=s, tn=D_FF // N_CHIPS, out_split=N_CHIPS,
                            comm=[_chips_stage([oth_d], (0, 2))])
    dw_up, [[r], rd] = _matmul(fn, d_up, ta=True, name="dw_up", tm=512, tk=s, tn=D_FF // N_CHIPS, out_split=N_CHIPS,
                               comm=[_halves_stage([dw_gate]), _chips_stage([oth_d], (1, 2), into=rd)])
    mine["w_down"] = add_chips("w_down", own_d, rd[0])
    own_g, oth_g = add_pair("w_gate", dw_gate, r)
    d_fn, [rg, [theirs["w_down"]]] = _matmul(
        d_gate, w_gate_f, tb=True, b_split=True, name="dfn_gate", tm=512,
        comm=[_chips_stage([oth_g], (0, 2)), _swap_stage([mine["w_down"]])])
    d_fn, [[r], rg] = _matmul(d_up, w_up_f, tb=True, b_split=True, add=d_fn, name="dfn_up", tm=512,
                              comm=[_halves_stage([dw_up]), _chips_stage([oth_g], (1, 2), into=rg)])
    mine["w_gate"] = add_chips("w_gate", own_g, rg[0])
    own_u, oth_u = add_pair("w_up", dw_up, r)
    d_h1, d_h1_b, gs["ffn_norm_g"] = _rms_bwd(d_fn, h1, sp["ffn_norm_g"], d_out, name="ffn_norm_bwd")
    dw_out, [[theirs["w_gate"]]] = _matmul(y, d_h1_b, ta=True, name="dw_out", tm=512, tk=s,
                                           comm=[_swap_stage([mine["w_gate"]])])
    dw_out = dw_out.reshape(N_CHIPS, D_MODEL // N_CHIPS, D_MODEL)
    d_y, [[r]] = _matmul(d_h1_b, w_out_f, tb=True, name="dy", comm=[_halves_stage([dw_out])])
    own_o, oth_o = add_pair("w_out", dw_out, r)
    (d_qa, d_ka, d_va, d_sink), [ru] = _swa_bwd(qa, ka, va, pos_col, pos_row, sinks, y_a, d_y,
                                                comm=[_chips_stage([oth_u], (0, 2))])
    (d_qcat, d_kcat, d_vb), [ru, [r]] = _mla_bwd(
        q_cat, k_cat, v_b, y_b, lse, d_y, comm=[_chips_stage([oth_u], (1, 2), into=ru), _chips_stage([oth_o])])
    mine["w_up"] = add_chips("w_up", own_u, ru[0])
    mine["w_out"] = add_chips("w_out", own_o, r)
    d_qm, d_km, d_vm = _mem_attn_bwd(qm, km, vm, y_m, d_y)
    (d_proj, dw_uq, dw_ukv, dg128, gs["mla_cq_norm_g"], gs["mla_ckv_norm_g"]), [[theirs["w_up"], theirs["w_out"]]] = \
        _attn_prep_bwd(proj, g128, gcq, gckv, w_uq_f, w_ukv_f, cos_t, sin_t, d_qa, d_ka, d_va, d_qcat, d_kcat, d_vb,
                       d_qm, comm=[_swap_stage([mine["w_up"], mine["w_out"]])])
    dw_mem_kv, gs["mem_norm_g"], gs["mem_k_norm_g"] = _mem_kv_bwd(
        mem, sp["mem_norm_g"], w_mem_kv_f, sp["mem_k_norm_g"], mn_b, kv_m, d_km, d_vm)
    late = ("w_uq", "w_ukv", "w_mem_kv")
    late_g = [_shards_heads(dw_uq, MLA_NOPE, MLA_ROPE), _shards_heads(dw_ukv, MLA_NOPE, MLA_V),
              dw_mem_kv.reshape(N_CHIPS, D_MODEL // N_CHIPS, -1)]
    dw_in, [rs] = _matmul(hn, d_proj, ta=True, name="dw_in", tm=512, tk=s, comm=[_halves_stage(late_g)])
    late_sums = [add_pair(n, g4, r) for n, g4, r in zip(late, late_g, rs)]
    dw_in = _shards_w_in(dw_in)
    d_hn, [rs, [r]] = _matmul(d_proj, w_in_f, tb=True, name="dhn", tk=1536,
                              comm=[_chips_stage([oth for _, oth in late_sums]), _halves_stage([dw_in])])
    for n, (own, _), r_n in zip(late, late_sums, rs):
        mine[n] = add_chips(n, own, r_n)
    own_i, oth_i = add_pair("w_in", dw_in, r)
    (grad_x, _, gs["attn_norm_g"]), [[r], late_theirs] = _rms_bwd(
        d_hn, x, sp["attn_norm_g"], d_h1, name="attn_norm_bwd",
        comm=[_chips_stage([oth_i]), _swap_stage([mine[n] for n in late])])
    theirs.update(zip(late, late_theirs))
    mine["w_in"] = add_chips("w_in", own_i, r)
    [[theirs["w_in"]]] = _run_stages([_swap_stage([mine["w_in"]])], name="grad_swap_w_in")

    fold = lambda r: r[:, :64] + r[:, 64:]
    gs["swa_q_norm_g"] = fold(dg128[G_SWA_Q:G_SWA_Q + 1])
    gs["swa_k_norm_g"] = fold(dg128[G_SWA_K:G_SWA_K + 1])
    gs["mla_qn_norm_g"] = dg128[G_QN:G_QN + 1]
    gs["mla_qr_norm_g"] = fold(dg128[G_QR:G_QR + 1])
    gs["mla_kn_norm_g"] = dg128[G_KN:G_KN + 1]
    gs["mla_kr_norm_g"] = fold(dg128[G_KR:G_KR + 1])
    gs["mem_q_norm_g"] = dg128[G_MQ:G_MQ + 1]
    gs["swa_sinks"] = d_sink[:, :SWA_Q_HEADS]

    grad, delta, new_m, new_v = {}, {}, {}, {}
    for n in BIG:
        g2, d, m2, v2 = _adamw(meta, wts[n][0], mine[n], theirs[n], mom_m[n][0], mom_v[n][0], name="adamw_" + n)
        grad[n], delta[n], new_m[n], new_v[n] = g2[None], d[None], m2[None], v2[None]

    sizes = [wts[n].shape[1] for n in SMALL]
    zero = jnp.zeros((1, LANES), F32)
    packs = _small_step(_pack([gs[n] for n in SMALL] + [loss_tile]), _pack([wts[n] for n in SMALL] + [zero]),
                        _pack([mom_m[n] for n in SMALL] + [zero]), _pack([mom_v[n] for n in SMALL] + [zero]))
    for store, buf in zip((grad, delta, new_m, new_v), packs):
        for n, val in zip(SMALL, _unpack(buf, sizes)):
            store[n] = val
    loss = _unpack(packs[0], sizes + [LANES])[-1][0, 0]

    return (loss, grad_x[None], *[grad[n] for n in WEIGHTS], *[delta[n] for n in WEIGHTS],
            *[new_m[n] for n in WEIGHTS], *[new_v[n] for n in WEIGHTS])
```

```python
import functools
import math

import jax
import jax.numpy as jnp
from jax import lax
from jax.experimental import pallas as pl
from jax.experimental.pallas import tpu as pltpu

F32 = jnp.float32
BF16 = jnp.bfloat16

D_MODEL = 2048
BLOCK = 128
EPS = 1e-6
NEG_INF = -1e30
SWA_Q_HEADS = 16
SWA_KV_HEADS = 2
SWA_HEAD_DIM = 64
MLA_HEADS = 4
MLA_RANK = 512
MLA_NOPE = 128
MLA_ROPE = 64
MLA_V = 128
ROPE_THETA = 10000.0
MEM_HEADS = 4
MEM_DIM = 128
D_FF = 5632
IN_WIDTH = 2880
IN_PAD = 3072
N_CHIPS = 4

ADAM_LR = 0.001
ADAM_B1 = 0.9
ADAM_B2 = 0.999
ADAM_EPS = 1e-08
ADAM_WD = 0.01
ADAM_STEP = 10

VMEM_LIMIT_BYTES = 56 * 1024 * 1024
LANES = 128

MESH = pl.DeviceIdType.MESH


def _params(sem=None, **kw):
    return pltpu.CompilerParams(dimension_semantics=sem, vmem_limit_bytes=VMEM_LIMIT_BYTES, **kw)


def _tile(n, want):
    if n <= want:
        return n
    t = want - want % LANES
    while t > 0:
        if n % t == 0:
            return t
        t -= LANES
    return n


ANY = pl.BlockSpec(memory_space=pl.ANY)


class _Stage:
    def __init__(self, ins, out_shapes, aliases, n_sem, issue, wait, mid=None):
        self.ins, self.out_shapes, self.aliases, self.n_sem = list(ins), list(out_shapes), dict(aliases), n_sem
        self.issue, self.wait, self.mid = issue, wait, mid


def _pcall(body, args, *, name, grid, in_specs, out_specs, out_shape, scratch_shapes=(), sem=None, comm=(),
           prefetch=(), io_alias=None):
    multi = isinstance(out_shape, (list, tuple))
    out_specs_l = list(out_specs) if multi else [out_specs]
    out_shape_l = list(out_shape) if multi else [out_shape]
    npf = len(prefetch)
    own_aliases = {npf + a: o for a, o in (io_alias or {}).items()}

    def call(fn, in_specs_, out_specs_, out_shape_, scratch_, operands, sem_, aliases=None):
        kw = dict(name=name, out_shape=out_shape_, compiler_params=_params(sem_))
        if aliases:
            kw["input_output_aliases"] = aliases
        if npf:
            spec = pltpu.PrefetchScalarGridSpec(num_scalar_prefetch=npf, grid=grid, in_specs=in_specs_,
                                                out_specs=out_specs_, scratch_shapes=scratch_)
            return pl.pallas_call(fn, grid_spec=spec, **kw)(*prefetch, *operands)
        return pl.pallas_call(fn, grid=grid, in_specs=in_specs_, out_specs=out_specs_, scratch_shapes=scratch_,
                              **kw)(*operands)

    if not comm:
        return call(body, list(in_specs), out_specs, out_shape, list(scratch_shapes), args, sem, own_aliases)
    n_in, n_out, n_scr = len(in_specs), len(out_specs_l), len(scratch_shapes)
    cins = [a for st in comm for a in st.ins]
    couts = [s for st in comm for s in st.out_shapes]
    aliases, ci, co = dict(own_aliases), 0, 0
    for st in comm:
        for a_i, o_i in st.aliases.items():
            aliases[npf + n_in + ci + a_i] = n_out + co + o_i
        ci, co = ci + len(st.ins), co + len(st.out_shapes)

    def wrapped(*refs):
        pre = refs[:npf]
        p = npf
        ins = refs[p:p + n_in]; p += n_in
        cin_refs = refs[p:p + len(cins)]; p += len(cins)
        outs = refs[p:p + n_out]; p += n_out
        cout_refs = refs[p:p + len(couts)]; p += len(couts)
        scr = refs[p:p + n_scr]; p += n_scr
        sems = refs[p:]
        first = functools.reduce(jnp.logical_and, [pl.program_id(a) == 0 for a in range(len(grid))])
        last = functools.reduce(jnp.logical_and, [pl.program_id(a) == grid[a] - 1 for a in range(len(grid))])

        def each(what):
            i, o = 0, 0
            for k, st in enumerate(comm):
                fn = getattr(st, what)
                if fn is not None:
                    fn(cin_refs[i:i + len(st.ins)], cout_refs[o:o + len(st.out_shapes)], sems[2 * k], sems[2 * k + 1])
                i, o = i + len(st.ins), o + len(st.out_shapes)

        @pl.when(first)
        def _():
            each("issue")

        if any(st.mid is not None for st in comm):
            n_steps = math.prod(grid)
            assert n_steps >= 4, "a two-leg stage needs a carrier with several grid steps"
            lin = functools.reduce(lambda acc, a: acc * grid[a] + pl.program_id(a), range(len(grid)), 0)

            @pl.when(lin == (3 * n_steps) // 4)
            def _():
                each("mid")

        body(*pre, *ins, *outs, *scr)

        @pl.when(last)
        def _():
            each("wait")

    sem_scr = [pltpu.SemaphoreType.DMA((st.n_sem,)) for st in comm for _ in range(2)]
    res = call(wrapped, list(in_specs) + [ANY] * len(cins), out_specs_l + [ANY] * len(couts), out_shape_l + couts,
               list(scratch_shapes) + sem_scr, (*args, *cins), ("arbitrary",) * len(grid), aliases)
    normal = list(res[:n_out])
    stage_outs, o = [], n_out
    for st in comm:
        stage_outs.append(list(res[o:o + len(st.out_shapes)]))
        o += len(st.out_shapes)
    return (normal if multi else normal[0]), stage_outs


def _matmul(a, b, *, name, ta=False, tb=False, add=None, out_dtype=F32, tm=1024, tn=1024, tk=2048,
            b_split=False, out_split=0, comm=(), loss_target=None):
    if ta:
        kdim, m = a.shape
    else:
        m, kdim = a.shape
    if b_split:
        assert tb
        nsp, n, kb = b.shape
        kb = kb * nsp
    elif tb:
        n, kb = b.shape
    else:
        kb, n = b.shape
    assert kb == kdim, (a.shape, b.shape, ta, tb)
    if b_split:
        tk = kdim
    if out_split:
        tn = _tile(n // out_split, tn)
    tm, tn, tk = _tile(m, tm), _tile(n, tn), _tile(kdim, tk)
    nk = kdim // tk
    dims = (((0 if ta else 1,), (1 if tb else 0,)), ((), ()))

    def product(a_ref, b_ref):
        if not b_split:
            return lax.dot_general(a_ref[...].astype(BF16), b_ref[...].astype(BF16), dims, preferred_element_type=F32)
        per = kdim // nsp
        return sum(lax.dot_general(a_ref[:, per * c:per * (c + 1)].astype(BF16), b_ref[c].astype(BF16), dims,
                                   preferred_element_type=F32) for c in range(nsp))

    def body(*refs):
        a_ref, b_ref = refs[:2]
        add_ref = refs[2] if add is not None else None
        n_in = 2 + (add is not None) + (loss_target is not None)
        o_ref = refs[n_in]

        def finish(r):
            if add_ref is not None:
                r = r + add_ref[...].astype(F32)
            if loss_target is None:
                o_ref[...] = r.astype(o_ref.dtype)
                return
            db_ref, l_ref = refs[n_in + 1], refs[n_in + 2]
            err = r - refs[n_in - 1][...]
            d_out = err * (1.0 / n)
            o_ref[...] = d_out
            db_ref[...] = d_out.astype(BF16)
            part = jnp.broadcast_to((0.5 / n) * jnp.sum(jnp.sum(err * err, axis=-1, keepdims=True), axis=0, keepdims=True),
                                    (1, LANES))
            first = jnp.logical_and(pl.program_id(0) == 0, pl.program_id(1) == 0)

            @pl.when(first)
            def _():
                l_ref[...] = part

            @pl.when(jnp.logical_not(first))
            def _():
                l_ref[...] += part

        if nk == 1:
            finish(product(a_ref, b_ref))
            return
        acc_ref = refs[-1]
        k = pl.program_id(2)
        part = product(a_ref, b_ref)

        @pl.when(k == 0)
        def _():
            acc_ref[...] = part

        @pl.when(k > 0)
        def _():
            acc_ref[...] += part

        @pl.when(k == nk - 1)
        def _():
            finish(acc_ref[...])

    a_spec = pl.BlockSpec((tk, tm), lambda i, j, k: (k, i)) if ta else pl.BlockSpec((tm, tk), lambda i, j, k: (i, k))
    if b_split:
        b_spec = pl.BlockSpec((nsp, tn, kdim // nsp), lambda i, j, k: (0, j, 0))
    elif tb:
        b_spec = pl.BlockSpec((tn, tk), lambda i, j, k: (j, k))
    else:
        b_spec = pl.BlockSpec((tk, tn), lambda i, j, k: (k, j))
    in_specs = [a_spec, b_spec]
    args = [a, b]
    if add is not None:
        in_specs.append(pl.BlockSpec((tm, tn), lambda i, j, k: (i, j)))
        args.append(add)
    tile = pl.BlockSpec((tm, tn), lambda i, j, k: (i, j))
    sem = ("parallel", "parallel", "arbitrary")
    if out_split:
        per = (n // out_split) // tn
        out_spec = pl.BlockSpec((None, tm, tn), lambda i, j, k: (j // per, i, j % per))
        out_shape = jax.ShapeDtypeStruct((out_split, m, n // out_split), out_dtype)
    elif loss_target is not None:
        in_specs.append(tile)
        args.append(loss_target)
        out_spec = [tile, tile, pl.BlockSpec((1, LANES), lambda i, j, k: (0, 0))]
        out_shape = [jax.ShapeDtypeStruct((m, n), F32), jax.ShapeDtypeStruct((m, n), BF16),
                     jax.ShapeDtypeStruct((1, LANES), F32)]
        sem = ("arbitrary",) * 3
    else:
        out_spec = tile
        out_shape = jax.ShapeDtypeStruct((m, n), out_dtype)
    return _pcall(body, args, name=name, grid=(m // tm, n // tn, nk), in_specs=in_specs, out_specs=out_spec,
                  out_shape=out_shape, scratch_shapes=[pltpu.VMEM((tm, tn), F32)] if nk > 1 else [],
                  sem=sem, comm=comm)


def _rms_fwd(x, g, *, name, tm=512, comm=()):
    s, d = x.shape
    tm = _tile(s, tm)

    def body(x_ref, g_ref, o_ref):
        xv = x_ref[...]
        r = lax.rsqrt(jnp.mean(xv * xv, axis=-1, keepdims=True) + EPS)
        o_ref[...] = (xv * r * g_ref[...]).astype(o_ref.dtype)

    return _pcall(body, (x, g), name=name, grid=(s // tm,),
                  in_specs=[pl.BlockSpec((tm, d), lambda i: (i, 0)), pl.BlockSpec((1, d), lambda i: (0, 0))],
                  out_specs=pl.BlockSpec((tm, d), lambda i: (i, 0)),
                  out_shape=jax.ShapeDtypeStruct((s, d), BF16), sem=("parallel",), comm=comm)


def _rms_bwd(dy, x, g, res, *, name, tm=512, comm=()):
    s, d = x.shape
    tm = _tile(s, tm)

    def body(dy_ref, x_ref, g_ref, res_ref, dx_ref, dxb_ref, dg_ref):
        xv = x_ref[...]
        dyv = dy_ref[...]
        r = lax.rsqrt(jnp.mean(xv * xv, axis=-1, keepdims=True) + EPS)
        xhat = xv * r
        dyg = dyv * g_ref[...]
        mt = jnp.mean(dyg * xhat, axis=-1, keepdims=True)
        dx = res_ref[...] + r * (dyg - xhat * mt)
        dx_ref[...] = dx
        dxb_ref[...] = dx.astype(BF16)
        part = jnp.sum(dyv * xhat, axis=0, keepdims=True)

        @pl.when(pl.program_id(0) == 0)
        def _():
            dg_ref[...] = part

        @pl.when(pl.program_id(0) > 0)
        def _():
            dg_ref[...] += part

    row = pl.BlockSpec((tm, d), lambda i: (i, 0))
    vec = pl.BlockSpec((1, d), lambda i: (0, 0))
    return _pcall(body, (dy, x, g, res), name=name, grid=(s // tm,), in_specs=[row, row, vec, row],
                  out_specs=[row, row, vec],
                  out_shape=[jax.ShapeDtypeStruct((s, d), F32), jax.ShapeDtypeStruct((s, d), BF16),
                             jax.ShapeDtypeStruct((1, d), F32)],
                  sem=("arbitrary",), comm=comm)


def _lane(shape):
    return lax.broadcasted_iota(jnp.int32, shape, 1)


def _halfsum(t, lo):
    s_lo = jnp.sum(jnp.where(lo, t, 0.0), axis=-1, keepdims=True)
    s_hi = jnp.sum(jnp.where(lo, 0.0, t), axis=-1, keepdims=True)
    return jnp.where(lo, s_lo, s_hi)


def _norm_pair(x, g, lo):
    r = lax.rsqrt(_halfsum(x * x, lo) * (1.0 / 64.0) + EPS)
    xhat = x * r
    return xhat * g, xhat, r


def _norm_pair_bwd(dy, g, xhat, r, lo):
    dyg = dy * g
    mt = _halfsum(dyg * xhat, lo) * (1.0 / 64.0)
    return r * (dyg - xhat * mt), jnp.sum(dy * xhat, axis=0, keepdims=True)


def _norm_full(x, g):
    r = lax.rsqrt(jnp.mean(x * x, axis=-1, keepdims=True) + EPS)
    xhat = x * r
    return xhat * g, xhat, r


def _norm_full_bwd(dy, g, xhat, r):
    dyg = dy * g
    mt = jnp.mean(dyg * xhat, axis=-1, keepdims=True)
    return r * (dyg - xhat * mt), jnp.sum(dy * xhat, axis=0, keepdims=True)


def _rot(x, first32):
    return jnp.where(first32, pltpu.roll(x, 96, axis=1), pltpu.roll(x, 32, axis=1))


def _rope(x, cos_t, sin_t, first32):
    return x * cos_t + _rot(x, first32) * sin_t


def _rope_bwd(dy, cos_t, sin_t, first32):
    return dy * cos_t + _rot(dy * sin_t, first32)


G_SWA_Q, G_SWA_K, G_QN, G_QR, G_KN, G_KR, G_MQ = range(7)

C_QA, C_KA, C_VA, C_CQ, C_CKV, C_QM, C_KR = 0, 1024, 1152, 1280, 1792, 2304, 2816


def _prep_common(p_ref, g128_ref, gcq_ref, gckv_ref, wuq_ref, wukv_ref, cos_ref, sin_ref):
    tm = p_ref.shape[0]
    lane = _lane((tm, LANES))
    lo = lane < 64
    first32 = (lane % 64) < 32
    cos_t = cos_ref[...]
    sin_t = sin_ref[...]
    g = lambda row: g128_ref[row:row + 1, :]
    out = dict(lo=lo, first32=first32, cos_t=cos_t, sin_t=sin_t, lane=lane)
    cq_n, cq_hat, cq_r = _norm_full(p_ref[:, C_CQ:C_CQ + MLA_RANK], gcq_ref[...])
    ckv_n, ckv_hat, ckv_r = _norm_full(p_ref[:, C_CKV:C_CKV + MLA_RANK], gckv_ref[...])
    cq_b = cq_n.astype(BF16)
    ckv_b = ckv_n.astype(BF16)
    q_b = jnp.dot(cq_b, wuq_ref[...], preferred_element_type=F32)
    kv_b = jnp.dot(ckv_b, wukv_ref[...], preferred_element_type=F32)
    out.update(cq_b=cq_b, cq_hat=cq_hat, cq_r=cq_r, ckv_b=ckv_b, ckv_hat=ckv_hat, ckv_r=ckv_r, q_b=q_b, kv_b=kv_b, g=g)
    return out


def _attn_prep_fwd(proj, g128, gcq, gckv, wuq, wukv, cos_t, sin_t, *, tm=512, comm=()):
    s = proj.shape[0]
    tm = _tile(s, tm)

    def body(p_ref, g128_ref, gcq_ref, gckv_ref, wuq_ref, wukv_ref, cos_ref, sin_ref,
             qa_ref, ka_ref, va_ref, qcat_ref, kcat_ref, vb_ref, qm_ref):
        c = _prep_common(p_ref, g128_ref, gcq_ref, gckv_ref, wuq_ref, wukv_ref, cos_ref, sin_ref)
        lo, first32, g = c["lo"], c["first32"], c["g"]
        for j in range(SWA_Q_HEADS // 2):
            y, _, _ = _norm_pair(p_ref[:, C_QA + 128 * j:C_QA + 128 * (j + 1)], g(G_SWA_Q), lo)
            qa_ref[:, 128 * j:128 * (j + 1)] = y.astype(BF16)
        y, _, _ = _norm_pair(p_ref[:, C_KA:C_KA + 128], g(G_SWA_K), lo)
        ka_ref[...] = y.astype(BF16)
        va_ref[...] = p_ref[:, C_VA:C_VA + 128].astype(BF16)
        kr, _, _ = _norm_pair(p_ref[:, C_KR:C_KR + 128], g(G_KR), lo)
        kr = jnp.where(lo, _rope(kr, c["cos_t"], c["sin_t"], first32), 0.0)
        krkr = (kr + pltpu.roll(kr, 64, axis=1)).astype(BF16)
        q_b, kv_b = c["q_b"], c["kv_b"]
        qr = []
        for j in range(MLA_HEADS // 2):
            y, _, _ = _norm_pair(q_b[:, 512 + 128 * j:512 + 128 * (j + 1)], g(G_QR), lo)
            qr.append(_rope(y, c["cos_t"], c["sin_t"], first32))
        for h in range(MLA_HEADS):
            qn, _, _ = _norm_full(q_b[:, 128 * h:128 * (h + 1)], g(G_QN))
            keep = lo if h % 2 == 0 else jnp.logical_not(lo)
            qcat_ref[h, :, 0:128] = qn.astype(BF16)
            qcat_ref[h, :, 128:256] = jnp.where(keep, qr[h // 2], 0.0).astype(BF16)
            kn, _, _ = _norm_full(kv_b[:, 128 * h:128 * (h + 1)], g(G_KN))
            kcat_ref[h, :, 0:128] = kn.astype(BF16)
            kcat_ref[h, :, 128:256] = krkr
        vb_ref[...] = kv_b[:, 512:1024].astype(BF16)
        for h in range(MEM_HEADS):
            y, _, _ = _norm_full(p_ref[:, C_QM + 128 * h:C_QM + 128 * (h + 1)], g(G_MQ))
            qm_ref[:, 128 * h:128 * (h + 1)] = y.astype(BF16)

    row = lambda w: pl.BlockSpec((tm, w), lambda i: (i, 0))
    full = lambda shape: pl.BlockSpec(shape, lambda i: tuple(0 for _ in shape))
    cat = pl.BlockSpec((MLA_HEADS, tm, 256), lambda i: (0, i, 0))
    return _pcall(
        body, (proj, g128, gcq, gckv, wuq, wukv, cos_t, sin_t), name="attn_prep_fwd", grid=(s // tm,),
        in_specs=[row(IN_PAD), full((8, 128)), full((1, 512)), full((1, 512)), full((512, 768)), full((512, 1024)),
                  row(128), row(128)],
        out_specs=[row(1024), row(128), row(128), cat, cat, row(512), row(512)],
        out_shape=[jax.ShapeDtypeStruct((s, 1024), BF16), jax.ShapeDtypeStruct((s, 128), BF16),
                   jax.ShapeDtypeStruct((s, 128), BF16), jax.ShapeDtypeStruct((MLA_HEADS, s, 256), BF16),
                   jax.ShapeDtypeStruct((MLA_HEADS, s, 256), BF16), jax.ShapeDtypeStruct((s, 512), BF16),
                   jax.ShapeDtypeStruct((s, 512), BF16)],
        sem=("parallel",), comm=comm)


def _attn_prep_bwd(proj, g128, gcq, gckv, wuq, wukv, cos_t, sin_t,
                   d_qa, d_ka, d_va, d_qcat, d_kcat, d_vb, d_qm, *, tm=256, comm=()):
    s = proj.shape[0]
    tm = _tile(s, tm)

    def body(p_ref, g128_ref, gcq_ref, gckv_ref, wuq_ref, wukv_ref, cos_ref, sin_ref,
             dqa_ref, dka_ref, dva_ref, dqcat_ref, dkcat_ref, dvb_ref, dqm_ref,
             dp_ref, dwuq_ref, dwukv_ref, dg128_ref, dgcq_ref, dgckv_ref):
        c = _prep_common(p_ref, g128_ref, gcq_ref, gckv_ref, wuq_ref, wukv_ref, cos_ref, sin_ref)
        lo, first32, g = c["lo"], c["first32"], c["g"]
        cos_v, sin_v = c["cos_t"], c["sin_t"]
        q_b, kv_b = c["q_b"], c["kv_b"]
        zero_row = jnp.zeros((1, LANES), F32)
        dg = {k: zero_row for k in range(7)}

        for j in range(SWA_Q_HEADS // 2):
            sl = slice(C_QA + 128 * j, C_QA + 128 * (j + 1))
            _, xhat, r = _norm_pair(p_ref[:, sl], g(G_SWA_Q), lo)
            dx, dgj = _norm_pair_bwd(dqa_ref[:, 128 * j:128 * (j + 1)], g(G_SWA_Q), xhat, r, lo)
            dp_ref[:, sl] = dx.astype(BF16)
            dg[G_SWA_Q] = dg[G_SWA_Q] + dgj
        _, xhat, r = _norm_pair(p_ref[:, C_KA:C_KA + 128], g(G_SWA_K), lo)
        dx, dgj = _norm_pair_bwd(dka_ref[...], g(G_SWA_K), xhat, r, lo)
        dp_ref[:, C_KA:C_KA + 128] = dx.astype(BF16)
        dg[G_SWA_K] = dgj
        dp_ref[:, C_VA:C_VA + 128] = dva_ref[...].astype(BF16)

        dqb_parts = [None] * 6
        for h in range(MLA_HEADS):
            _, xhat, r = _norm_full(q_b[:, 128 * h:128 * (h + 1)], g(G_QN))
            dx, dgj = _norm_full_bwd(dqcat_ref[h, :, 0:128], g(G_QN), xhat, r)
            dqb_parts[h] = dx
            dg[G_QN] = dg[G_QN] + dgj
        for j in range(MLA_HEADS // 2):
            _, xhat, r = _norm_pair(q_b[:, 512 + 128 * j:512 + 128 * (j + 1)], g(G_QR), lo)
            d_rot = jnp.where(lo, dqcat_ref[2 * j, :, 128:256], dqcat_ref[2 * j + 1, :, 128:256])
            d_y = _rope_bwd(d_rot, cos_v, sin_v, first32)
            dx, dgj = _norm_pair_bwd(d_y, g(G_QR), xhat, r, lo)
            dqb_parts[4 + j] = dx
            dg[G_QR] = dg[G_QR] + dgj
        d_qb = jnp.concatenate(dqb_parts, axis=1).astype(BF16)
        dwuq = lax.dot_general(c["cq_b"], d_qb, (((0,), (0,)), ((), ())), preferred_element_type=F32)
        d_cqn = lax.dot_general(d_qb, wuq_ref[...], (((1,), (1,)), ((), ())), preferred_element_type=F32)
        dx, dgcq = _norm_full_bwd(d_cqn, gcq_ref[...], c["cq_hat"], c["cq_r"])
        dp_ref[:, C_CQ:C_CQ + MLA_RANK] = dx.astype(BF16)

        dkv_parts = []
        d_krkr = jnp.zeros((p_ref.shape[0], LANES), F32)
        for h in range(MLA_HEADS):
            _, xhat, r = _norm_full(kv_b[:, 128 * h:128 * (h + 1)], g(G_KN))
            dx, dgj = _norm_full_bwd(dkcat_ref[h, :, 0:128], g(G_KN), xhat, r)
            dkv_parts.append(dx)
            dg[G_KN] = dg[G_KN] + dgj
            d_krkr = d_krkr + dkcat_ref[h, :, 128:256]
        d_kvb = jnp.concatenate(dkv_parts + [dvb_ref[...]], axis=1).astype(BF16)
        dwukv = lax.dot_general(c["ckv_b"], d_kvb, (((0,), (0,)), ((), ())), preferred_element_type=F32)
        d_ckvn = lax.dot_general(d_kvb, wukv_ref[...], (((1,), (1,)), ((), ())), preferred_element_type=F32)
        dx, dgckv = _norm_full_bwd(d_ckvn, gckv_ref[...], c["ckv_hat"], c["ckv_r"])
        dp_ref[:, C_CKV:C_CKV + MLA_RANK] = dx.astype(BF16)

        _, xhat, r = _norm_pair(p_ref[:, C_KR:C_KR + 128], g(G_KR), lo)
        d_kr = jnp.where(lo, d_krkr + pltpu.roll(d_krkr, 64, axis=1), 0.0)
        d_y = jnp.where(lo, _rope_bwd(d_kr, cos_v, sin_v, first32), 0.0)
        dx, dgj = _norm_pair_bwd(d_y, g(G_KR), xhat, r, lo)
        dp_ref[:, C_KR:C_KR + 128] = jnp.where(lo, dx, 0.0).astype(BF16)
        dp_ref[:, C_KR + 128:] = jnp.zeros((p_ref.shape[0], IN_PAD - C_KR - 128), BF16)
        dg[G_KR] = dgj

        for h in range(MEM_HEADS):
            sl = slice(C_QM + 128 * h, C_QM + 128 * (h + 1))
            _, xhat, r = _norm_full(p_ref[:, sl], g(G_MQ))
            dx, dgj = _norm_full_bwd(dqm_ref[:, 128 * h:128 * (h + 1)], g(G_MQ), xhat, r)
            dp_ref[:, sl] = dx.astype(BF16)
            dg[G_MQ] = dg[G_MQ] + dgj

        dg_tile = jnp.concatenate([dg[k] for k in range(7)] + [zero_row], axis=0)

        @pl.when(pl.program_id(0) == 0)
        def _():
            dwuq_ref[...] = dwuq
            dwukv_ref[...] = dwukv
            dg128_ref[...] = dg_tile
            dgcq_ref[...] = dgcq
            dgckv_ref[...] = dgckv

        @pl.when(pl.program_id(0) > 0)
        def _():
            dwuq_ref[...] += dwuq
            dwukv_ref[...] += dwukv
            dg128_ref[...] += dg_tile
            dgcq_ref[...] += dgcq
            dgckv_ref[...] += dgckv

    row = lambda w: pl.BlockSpec((tm, w), lambda i: (i, 0))
    full = lambda shape: pl.BlockSpec(shape, lambda i: tuple(0 for _ in shape))
    cat = pl.BlockSpec((MLA_HEADS, tm, 256), lambda i: (0, i, 0))
    return _pcall(
        body, (proj, g128, gcq, gckv, wuq, wukv, cos_t, sin_t, d_qa, d_ka, d_va, d_qcat, d_kcat, d_vb, d_qm),
        name="attn_prep_bwd", grid=(s // tm,),
        in_specs=[row(IN_PAD), full((8, 128)), full((1, 512)), full((1, 512)), full((512, 768)), full((512, 1024)),
                  row(128), row(128),
                  row(1024), row(128), row(128), cat, cat, row(512), row(512)],
        out_specs=[row(IN_PAD), full((512, 768)), full((512, 1024)), full((8, 128)), full((1, 512)), full((1, 512))],
        out_shape=[jax.ShapeDtypeStruct((s, IN_PAD), BF16), jax.ShapeDtypeStruct((512, 768), F32),
                   jax.ShapeDtypeStruct((512, 1024), F32), jax.ShapeDtypeStruct((8, 128), F32),
                   jax.ShapeDtypeStruct((1, 512), F32), jax.ShapeDtypeStruct((1, 512), F32)],
        sem=("arbitrary",), comm=comm)


SWA_SLOPES = tuple(2.0 ** (-8.0 * h / SWA_Q_HEADS) for h in range(1, SWA_Q_HEADS + 1))
SWA_SCALE = SWA_HEAD_DIM ** -0.5
NT_DIMS = (((1,), (1,)), ((), ()))
TN_DIMS = (((0,), (0,)), ((), ()))


def _swa_span(n, kp_ref, kc_ref, vp_ref, vc_ref, pcol_ref, pprow_ref, pcrow_ref):
    k_span = jnp.concatenate([kp_ref[...], kc_ref[...]], axis=0).astype(F32)
    v_span = jnp.concatenate([vp_ref[...], vc_ref[...]], axis=0).astype(F32)
    lo = _lane((2 * BLOCK, LANES)) < 64
    k_sw = pltpu.roll(k_span, 64, axis=1)
    v_sw = pltpu.roll(v_span, 64, axis=1)
    kk = (jnp.where(lo, k_span, k_sw).astype(BF16), jnp.where(lo, k_sw, k_span).astype(BF16))
    vv_lo = (jnp.where(lo, v_span, 0.0).astype(BF16), jnp.where(lo, v_sw, 0.0).astype(BF16))
    vv_hi = (jnp.where(lo, 0.0, v_sw).astype(BF16), jnp.where(lo, 0.0, v_span).astype(BF16))
    pk = jnp.concatenate([pprow_ref[...], pcrow_ref[...]], axis=1)
    dist = jnp.abs(pcol_ref[...] - pk)
    qi = lax.broadcasted_iota(jnp.int32, (BLOCK, 2 * BLOCK), 0)
    ki = lax.broadcasted_iota(jnp.int32, (BLOCK, 2 * BLOCK), 1)
    first_key = jnp.where(n > 0, qi + 1, jnp.maximum(qi + 1, BLOCK))
    valid = jnp.logical_and(ki >= first_key, ki <= qi + BLOCK)
    mask_add = jnp.where(valid, 0.0, NEG_INF)
    return kk, vv_lo, vv_hi, dist, mask_add


def _swa_heads(q_ref, lo):
    heads = []
    for j in range(SWA_Q_HEADS // 2):
        q_pair = q_ref[:, 128 * j:128 * (j + 1)].astype(F32)
        for par in (0, 1):
            q_h = jnp.where(lo if par == 0 else jnp.logical_not(lo), q_pair, 0.0).astype(BF16)
            heads.append((2 * j + par, (2 * j) // (SWA_Q_HEADS // SWA_KV_HEADS), par, q_h))
    return heads


def _swa_probs(raw, dist, mask_add, slope, sink):
    s = raw * SWA_SCALE - slope * dist + mask_add
    m = jnp.maximum(jnp.max(s, axis=-1, keepdims=True), sink)
    e = jnp.exp(s - m)
    e_sink = jnp.exp(sink - m)
    inv = 1.0 / (jnp.sum(e, axis=-1, keepdims=True) + e_sink)
    return e * inv, e_sink * inv


def _swa_specs():
    blk = lambda w: pl.BlockSpec((BLOCK, w), lambda n: (n, 0))
    prev = lambda w: pl.BlockSpec((BLOCK, w), lambda n: (jnp.maximum(n - 1, 0), 0))
    prow_c = pl.BlockSpec((1, BLOCK), lambda n: (0, n))
    prow_p = pl.BlockSpec((1, BLOCK), lambda n: (0, jnp.maximum(n - 1, 0)))
    smem = pl.BlockSpec(memory_space=pltpu.SMEM)
    return [blk(1024), prev(128), blk(128), prev(128), blk(128), blk(1), prow_p, prow_c, smem], blk


def _swa_fwd(qa, ka, va, pos_col, pos_row, sinks, *, comm=()):
    s = qa.shape[0]
    in_specs, blk = _swa_specs()

    def body(q_ref, kp_ref, kc_ref, vp_ref, vc_ref, pcol_ref, pprow_ref, pcrow_ref, sink_ref, o_ref, yb_ref):
        n = pl.program_id(0)
        kk, vv_lo, vv_hi, dist, mask_add = _swa_span(n, kp_ref, kc_ref, vp_ref, vc_ref, pcol_ref, pprow_ref, pcrow_ref)
        lo = _lane((BLOCK, LANES)) < 64
        heads = _swa_heads(q_ref, lo)
        raws = [lax.dot_general(q_h, kk[kv], NT_DIMS, preferred_element_type=F32) for _, kv, _, q_h in heads]
        probs = [_swa_probs(raw, dist, mask_add, SWA_SLOPES[h], sink_ref[h])[0].astype(BF16)
                 for raw, (h, _, _, _) in zip(raws, heads)]
        for j in range(SWA_Q_HEADS // 2):
            kv = heads[2 * j][1]
            out = (jnp.dot(probs[2 * j], vv_lo[kv], preferred_element_type=F32)
                   + jnp.dot(probs[2 * j + 1], vv_hi[kv], preferred_element_type=F32))
            o_ref[:, 128 * j:128 * (j + 1)] = out
            yb_ref[:, 128 * j:128 * (j + 1)] = out.astype(BF16)

    return _pcall(body, (qa, ka, ka, va, va, pos_col, pos_row, pos_row, sinks), name="swa_fwd", grid=(s // BLOCK,),
                  in_specs=in_specs, out_specs=[blk(1024), blk(1024)],
                  out_shape=[jax.ShapeDtypeStruct((s, 1024), F32), jax.ShapeDtypeStruct((s, D_MODEL), BF16)],
                  sem=("parallel",), comm=comm)


def _swa_bwd(qa, ka, va, pos_col, pos_row, sinks, y_a, d_y, *, comm=()):
    s = qa.shape[0]
    in_specs, blk = _swa_specs()
    whole = pl.BlockSpec((s, 128), lambda n: (0, 0))

    def body(q_ref, kp_ref, kc_ref, vp_ref, vc_ref, pcol_ref, pprow_ref, pcrow_ref, sink_ref, y_ref, dy_ref,
             dq_ref, dk_ref, dv_ref, dsink_ref):
        n = pl.program_id(0)

        @pl.when(n == 0)
        def _():
            dk_ref[...] = jnp.zeros_like(dk_ref)
            dv_ref[...] = jnp.zeros_like(dv_ref)
            dsink_ref[...] = jnp.zeros_like(dsink_ref)

        kk, vv_lo, vv_hi, dist, mask_add = _swa_span(n, kp_ref, kc_ref, vp_ref, vc_ref, pcol_ref, pprow_ref, pcrow_ref)
        lo = _lane((BLOCK, LANES)) < 64
        lo2 = _lane((2 * BLOCK, LANES)) < 64
        lane1 = _lane((1, LANES))
        dsink = jnp.zeros((1, LANES), F32)
        dkk = [jnp.zeros((2 * BLOCK, LANES), F32) for _ in range(SWA_KV_HEADS)]
        dvv = [jnp.zeros((2 * BLOCK, LANES), F32) for _ in range(SWA_KV_HEADS)]
        heads = _swa_heads(q_ref, lo)
        do_b, deltas = [], []
        for j in range(SWA_Q_HEADS // 2):
            do_pair = dy_ref[:, 128 * j:128 * (j + 1)]
            doy = do_pair * y_ref[:, 128 * j:128 * (j + 1)]
            do_b.append(do_pair.astype(BF16))
            deltas.append(jnp.sum(jnp.where(lo, doy, 0.0), axis=-1, keepdims=True))
            deltas.append(jnp.sum(jnp.where(lo, 0.0, doy), axis=-1, keepdims=True))
        raws = [lax.dot_general(q_h, kk[kv], NT_DIMS, preferred_element_type=F32) for _, kv, _, q_h in heads]
        dps = [lax.dot_general(do_b[h // 2], (vv_lo, vv_hi)[par][kv], NT_DIMS, preferred_element_type=F32)
               for h, kv, par, _ in heads]
        p_b, ds_b = [], []
        for h, kv, par, _ in heads:
            p, p_sink = _swa_probs(raws[h], dist, mask_add, SWA_SLOPES[h], sink_ref[h])
            ds = p * (dps[h] - deltas[h])
            dsink = dsink + jnp.where(lane1 == h, -jnp.sum(p_sink * deltas[h], axis=0, keepdims=True), 0.0)
            p_b.append(p.astype(BF16))
            ds_b.append((ds * SWA_SCALE).astype(BF16))
        dq_halves = []
        for h, kv, par, q_h in heads:
            dq_halves.append(jnp.dot(ds_b[h], kk[kv], preferred_element_type=F32))
            dkk[kv] = dkk[kv] + lax.dot_general(ds_b[h], q_h, TN_DIMS, preferred_element_type=F32)
            pv = lax.dot_general(p_b[h], do_b[h // 2], TN_DIMS, preferred_element_type=F32)
            dvv[kv] = dvv[kv] + jnp.where(lo2 if par == 0 else jnp.logical_not(lo2), pv, 0.0)
        for j in range(SWA_Q_HEADS // 2):
            dq_ref[:, 128 * j:128 * (j + 1)] = jnp.where(lo, dq_halves[2 * j], dq_halves[2 * j + 1])
        fold = lambda t: t + pltpu.roll(t, 64, axis=1)
        dk_span = jnp.where(lo2, fold(dkk[0]), fold(dkk[1]))
        dv_span = jnp.where(lo2, fold(dvv[0]), fold(dvv[1]))
        prev0 = pl.multiple_of(jnp.maximum(n - 1, 0) * BLOCK, BLOCK)
        cur0 = pl.multiple_of(n * BLOCK, BLOCK)
        dk_ref[pl.ds(prev0, BLOCK), :] += dk_span[0:BLOCK]
        dk_ref[pl.ds(cur0, BLOCK), :] += dk_span[BLOCK:]
        dv_ref[pl.ds(prev0, BLOCK), :] += dv_span[0:BLOCK]
        dv_ref[pl.ds(cur0, BLOCK), :] += dv_span[BLOCK:]
        dsink_ref[...] += dsink

    return _pcall(
        body, (qa, ka, ka, va, va, pos_col, pos_row, pos_row, sinks, y_a, d_y), name="swa_bwd", grid=(s // BLOCK,),
        in_specs=in_specs + [blk(1024), blk(1024)],
        out_specs=[blk(1024), whole, whole, pl.BlockSpec((1, LANES), lambda n: (0, 0))],
        out_shape=[jax.ShapeDtypeStruct((s, 1024), F32), jax.ShapeDtypeStruct((s, 128), F32),
                   jax.ShapeDtypeStruct((s, 128), F32), jax.ShapeDtypeStruct((1, LANES), F32)],
        sem=("arbitrary",), comm=comm)


MLA_SCALE = (MLA_NOPE + MLA_ROPE) ** -0.5
MLA_TILE = 1024


def _tile_pairs(nt, q_major):
    pairs = [(i, j) for i in range(nt) for j in range(i + 1)] if q_major else \
            [(i, j) for j in range(nt) for i in range(j, nt)]
    return jnp.asarray([p[0] for p in pairs], jnp.int32), jnp.asarray([p[1] for p in pairs], jnp.int32)


def _diag_mask(t):
    return lax.broadcasted_iota(jnp.int32, (t, t), 1) <= lax.broadcasted_iota(jnp.int32, (t, t), 0)


def _mla_fwd(q_cat, k_cat, v_b, y_all, *, comm=()):
    nh, s, _ = q_cat.shape
    t = _tile(s, MLA_TILE)
    qi, kj = _tile_pairs(s // t, True)
    ycol = (SWA_Q_HEADS * SWA_HEAD_DIM) // (nh * MLA_V)

    def body(qi_ref, kj_ref, q_ref, k_ref, v_ref, _, o_ref, lse_ref, yb_ref, m_sc, l_sc, acc_sc):
        i, j = qi_ref[pl.program_id(0)], kj_ref[pl.program_id(0)]

        @pl.when(j == 0)
        def _():
            m_sc[...] = jnp.full_like(m_sc, NEG_INF)
            l_sc[...] = jnp.zeros_like(l_sc)
            acc_sc[...] = jnp.zeros_like(acc_sc)

        def update(diagonal):
            scores = [lax.dot_general(q_ref[h], k_ref[h], NT_DIMS, preferred_element_type=F32) for h in range(nh)]
            probs, alphas = [], []
            for h in range(nh):
                sc = scores[h] * MLA_SCALE
                if diagonal:
                    sc = jnp.where(_diag_mask(t), sc, NEG_INF)
                m_old = m_sc[h]
                m_new = jnp.maximum(m_old, jnp.max(sc, axis=-1, keepdims=True))
                alpha = jnp.exp(m_old - m_new)
                p = jnp.exp(sc - m_new)
                l_sc[h] = alpha * l_sc[h] + jnp.sum(p, axis=-1, keepdims=True)
                m_sc[h] = m_new
                probs.append(p.astype(BF16))
                alphas.append(alpha)
            for h in range(nh):
                acc_sc[h] = alphas[h] * acc_sc[h] + jnp.dot(probs[h], v_ref[:, MLA_V * h:MLA_V * (h + 1)],
                                                            preferred_element_type=F32)

        @pl.when(j < i)
        def _():
            update(False)

        @pl.when(j == i)
        def _():
            update(True)
            for h in range(nh):
                out = acc_sc[h] * (1.0 / l_sc[h])
                o_ref[:, MLA_V * h:MLA_V * (h + 1)] = out
                yb_ref[:, MLA_V * h:MLA_V * (h + 1)] = out.astype(BF16)
                lse_ref[h] = m_sc[h] + jnp.log(l_sc[h])

    return _pcall(
        body, (q_cat, k_cat, v_b, y_all), name="mla_fwd", grid=(qi.shape[0],), prefetch=(qi, kj),
        in_specs=[pl.BlockSpec((nh, t, 256), lambda p, qi, kj: (0, qi[p], 0)),
                  pl.BlockSpec((nh, t, 256), lambda p, qi, kj: (0, kj[p], 0)),
                  pl.BlockSpec((t, nh * MLA_V), lambda p, qi, kj: (kj[p], 0)), ANY],
        out_specs=[pl.BlockSpec((t, nh * MLA_V), lambda p, qi, kj: (qi[p], 0)),
                   pl.BlockSpec((nh, t, 1), lambda p, qi, kj: (0, qi[p], 0)),
                   pl.BlockSpec((t, nh * MLA_V), lambda p, qi, kj: (qi[p], ycol))],
        out_shape=[jax.ShapeDtypeStruct((s, nh * MLA_V), F32), jax.ShapeDtypeStruct((nh, s, 1), F32),
                   jax.ShapeDtypeStruct(y_all.shape, y_all.dtype)],
        scratch_shapes=[pltpu.VMEM((nh, t, 1), F32), pltpu.VMEM((nh, t, 1), F32), pltpu.VMEM((nh, t, MLA_V), F32)],
        sem=("arbitrary",), comm=comm, io_alias={3: 2})


def _mla_bwd(q_cat, k_cat, v_b, y_b, lse, d_y, *, comm=()):
    nh, s, _ = q_cat.shape
    t = _tile(s, MLA_TILE)
    nt = s // t
    hp = 2
    wv = hp * MLA_V
    col0 = (SWA_Q_HEADS * SWA_HEAD_DIM) // wv
    qi, kj = _tile_pairs(nt, False)

    def body(qi_ref, kj_ref, q_ref, k_ref, v_ref, y_ref, lse_ref, dy_ref, dq_ref, dk_ref, dv_ref, dk_sc, dv_sc):
        step = pl.program_id(1)
        i, j = qi_ref[step], kj_ref[step]

        @pl.when(step == 0)
        def _():
            dq_ref[...] = jnp.zeros_like(dq_ref)

        @pl.when(i == j)
        def _():
            dk_sc[...] = jnp.zeros_like(dk_sc)
            dv_sc[...] = jnp.zeros_like(dv_sc)

        def update(diagonal):
            rows = pl.ds(pl.multiple_of(i * t, t), t)
            cols = [slice(MLA_V * h, MLA_V * (h + 1)) for h in range(hp)]
            do_b = [dy_ref[:, cols[h]].astype(BF16) for h in range(hp)]
            scores = [lax.dot_general(q_ref[h], k_ref[h], NT_DIMS, preferred_element_type=F32) for h in range(hp)]
            dps = [lax.dot_general(do_b[h], v_ref[:, cols[h]], NT_DIMS, preferred_element_type=F32) for h in range(hp)]
            p_b, ds_b = [], []
            for h in range(hp):
                p = jnp.exp(scores[h] * MLA_SCALE - lse_ref[h])
                if diagonal:
                    p = jnp.where(_diag_mask(t), p, 0.0)
                delta = jnp.sum(dy_ref[:, cols[h]] * y_ref[:, cols[h]], axis=-1, keepdims=True)
                p_b.append(p.astype(BF16))
                ds_b.append((p * (dps[h] - delta) * MLA_SCALE).astype(BF16))
            for h in range(hp):
                dv_sc[h] += lax.dot_general(p_b[h], do_b[h], TN_DIMS, preferred_element_type=F32)
                dk_sc[h] += lax.dot_general(ds_b[h], q_ref[h], TN_DIMS, preferred_element_type=F32)
                dq_ref[h, rows, :] += jnp.dot(ds_b[h], k_ref[h], preferred_element_type=F32)

        @pl.when(i > j)
        def _():
            update(False)

        @pl.when(i == j)
        def _():
            update(True)

        @pl.when(i == nt - 1)
        def _():
            dk_ref[...] = dk_sc[...]
            for h in range(hp):
                dv_ref[:, MLA_V * h:MLA_V * (h + 1)] = dv_sc[h]

    return _pcall(
        body, (q_cat, k_cat, v_b, y_b, lse, d_y), name="mla_bwd", grid=(nh // hp, qi.shape[0]), prefetch=(qi, kj),
        in_specs=[pl.BlockSpec((hp, t, 256), lambda g, p, qi, kj: (g, qi[p], 0)),
                  pl.BlockSpec((hp, t, 256), lambda g, p, qi, kj: (g, kj[p], 0)),
                  pl.BlockSpec((t, wv), lambda g, p, qi, kj: (kj[p], g)),
                  pl.BlockSpec((t, wv), lambda g, p, qi, kj: (qi[p], g)),
                  pl.BlockSpec((hp, t, 1), lambda g, p, qi, kj: (g, qi[p], 0)),
                  pl.BlockSpec((t, wv), lambda g, p, qi, kj: (qi[p], col0 + g))],
        out_specs=[pl.BlockSpec((hp, s, 256), lambda g, p, qi, kj: (g, 0, 0)),
                   pl.BlockSpec((hp, t, 256), lambda g, p, qi, kj: (g, kj[p], 0)),
                   pl.BlockSpec((t, wv), lambda g, p, qi, kj: (kj[p], g))],
        out_shape=[jax.ShapeDtypeStruct((nh, s, 256), F32), jax.ShapeDtypeStruct((nh, s, 256), F32),
                   jax.ShapeDtypeStruct((s, nh * MLA_V), F32)],
        scratch_shapes=[pltpu.VMEM((hp, t, 256), F32), pltpu.VMEM((hp, t, MLA_V), F32)],
        sem=("arbitrary", "arbitrary"), comm=comm)


MEM_SCALE = MEM_DIM ** -0.5


def _mem_kv_fwd(mem, g_mem, w_memkv, g_mk):
    m_len = mem.shape[0]

    def body(mem_ref, g_ref, w_ref, gk_ref, mn_ref, kv_ref, kn_ref, v_ref):
        mn, _, _ = _norm_full(mem_ref[...], g_ref[...])
        mn_b = mn.astype(BF16)
        mn_ref[...] = mn_b
        kv = jnp.dot(mn_b, w_ref[...], preferred_element_type=F32)
        kv_ref[...] = kv
        for h in range(MEM_HEADS):
            kn, _, _ = _norm_full(kv[:, 128 * h:128 * (h + 1)], gk_ref[...])
            kn_ref[:, 128 * h:128 * (h + 1)] = kn.astype(BF16)
        v_ref[...] = kv[:, 512:1024].astype(BF16)

    return pl.pallas_call(
        body, name="mem_kv_fwd",
        out_shape=[jax.ShapeDtypeStruct((m_len, D_MODEL), BF16), jax.ShapeDtypeStruct((m_len, 1024), F32),
                   jax.ShapeDtypeStruct((m_len, 512), BF16), jax.ShapeDtypeStruct((m_len, 512), BF16)],
        compiler_params=_params(),
    )(mem, g_mem, w_memkv, g_mk)


def _mem_kv_bwd(mem, g_mem, w_memkv, g_mk, mn_b, kv, d_kn, d_v):
    m_len = mem.shape[0]

    def body(mem_ref, g_ref, w_ref, gk_ref, mn_ref, kv_ref, dkn_ref, dv_ref, dw_ref, dgmem_ref, dgk_ref):
        parts = []
        dgk = jnp.zeros((1, LANES), F32)
        for h in range(MEM_HEADS):
            _, xhat, r = _norm_full(kv_ref[:, 128 * h:128 * (h + 1)], gk_ref[...])
            dx, dgh = _norm_full_bwd(dkn_ref[:, 128 * h:128 * (h + 1)], gk_ref[...], xhat, r)
            parts.append(dx)
            dgk = dgk + dgh
        d_kv = jnp.concatenate(parts + [dv_ref[...]], axis=1).astype(BF16)
        dw_ref[...] = lax.dot_general(mn_ref[...], d_kv, TN_DIMS, preferred_element_type=F32)
        d_mn = lax.dot_general(d_kv, w_ref[...], NT_DIMS, preferred_element_type=F32)
        _, xhat, _ = _norm_full(mem_ref[...], g_ref[...])
        dgmem_ref[...] = jnp.sum(d_mn * xhat, axis=0, keepdims=True)
        dgk_ref[...] = dgk

    return pl.pallas_call(
        body, name="mem_kv_bwd",
        out_shape=[jax.ShapeDtypeStruct((D_MODEL, 1024), F32), jax.ShapeDtypeStruct((1, D_MODEL), F32),
                   jax.ShapeDtypeStruct((1, LANES), F32)],
        compiler_params=_params(),
    )(mem, g_mem, w_memkv, g_mk, mn_b, kv, d_kn, d_v)


def _mem_probs(q_h, k_h):
    sc = lax.dot_general(q_h, k_h, NT_DIMS, preferred_element_type=F32) * MEM_SCALE
    e = jnp.exp(sc - jnp.max(sc, axis=-1, keepdims=True))
    return e * (1.0 / jnp.sum(e, axis=-1, keepdims=True))


def _mem_attn_fwd(qm, km, vm, y_all, *, tm=512):
    s = qm.shape[0]
    tm = _tile(s, tm)
    m_len = km.shape[0]
    ycol = (SWA_Q_HEADS * SWA_HEAD_DIM + MLA_HEADS * MLA_V) // 512

    def body(q_ref, k_ref, v_ref, _, o_ref, yb_ref):
        for h in range(MEM_HEADS):
            sl = slice(128 * h, 128 * (h + 1))
            p = _mem_probs(q_ref[:, sl], k_ref[:, sl])
            out = jnp.dot(p.astype(BF16), v_ref[:, sl], preferred_element_type=F32)
            o_ref[:, sl] = out
            yb_ref[:, sl] = out.astype(BF16)

    kvspec = pl.BlockSpec((m_len, 512), lambda i: (0, 0))
    return _pcall(
        body, (qm, km, vm, y_all), name="mem_attn_fwd", grid=(s // tm,),
        in_specs=[pl.BlockSpec((tm, 512), lambda i: (i, 0)), kvspec, kvspec, ANY],
        out_specs=[pl.BlockSpec((tm, 512), lambda i: (i, 0)), pl.BlockSpec((tm, 512), lambda i: (i, ycol))],
        out_shape=[jax.ShapeDtypeStruct((s, 512), F32), jax.ShapeDtypeStruct(y_all.shape, y_all.dtype)],
        sem=("parallel",), io_alias={3: 1})


def _mem_attn_bwd(qm, km, vm, y_m, d_y, *, tm=512):
    s = qm.shape[0]
    tm = _tile(s, tm)
    m_len = km.shape[0]
    col0 = (SWA_Q_HEADS * SWA_HEAD_DIM + MLA_HEADS * MLA_V) // 512

    def body(q_ref, k_ref, v_ref, y_ref, dy_ref, dq_ref, dk_ref, dv_ref):
        @pl.when(pl.program_id(0) == 0)
        def _():
            dk_ref[...] = jnp.zeros_like(dk_ref)
            dv_ref[...] = jnp.zeros_like(dv_ref)

        for h in range(MEM_HEADS):
            sl = slice(128 * h, 128 * (h + 1))
            q_h, k_h = q_ref[:, sl], k_ref[:, sl]
            do = dy_ref[:, sl]
            do_b = do.astype(BF16)
            p = _mem_probs(q_h, k_h)
            delta = jnp.sum(do * y_ref[:, sl], axis=-1, keepdims=True)
            dv_ref[:, sl] += lax.dot_general(p.astype(BF16), do_b, TN_DIMS, preferred_element_type=F32)
            dp = lax.dot_general(do_b, v_ref[:, sl], NT_DIMS, preferred_element_type=F32)
            ds_b = (p * (dp - delta) * MEM_SCALE).astype(BF16)
            dq_ref[:, sl] = jnp.dot(ds_b, k_h, preferred_element_type=F32)
            dk_ref[:, sl] += lax.dot_general(ds_b, q_h, TN_DIMS, preferred_element_type=F32)

    kvspec = pl.BlockSpec((m_len, 512), lambda i: (0, 0))
    row = pl.BlockSpec((tm, 512), lambda i: (i, 0))
    return pl.pallas_call(
        body, name="mem_attn_bwd", grid=(s // tm,),
        in_specs=[row, kvspec, kvspec, row, pl.BlockSpec((tm, 512), lambda i: (i, col0))],
        out_specs=[row, kvspec, kvspec],
        out_shape=[jax.ShapeDtypeStruct((s, 512), F32), jax.ShapeDtypeStruct((m_len, 512), F32),
                   jax.ShapeDtypeStruct((m_len, 512), F32)],
        compiler_params=_params(("arbitrary",)),
    )(qm, km, vm, y_m, d_y)


def _ffn_gate_up(fn, w_gate, w_up, *, tm=512, comm=()):
    s, d = fn.shape
    nsp, _, tf = w_gate.shape
    f = nsp * tf
    tm = _tile(s, tm)

    def body(x_ref, wg_ref, wu_ref, g_ref, u_ref, a_ref):
        x = x_ref[...]
        gate = jnp.dot(x, wg_ref[...], preferred_element_type=F32)
        up = jnp.dot(x, wu_ref[...], preferred_element_type=F32)
        g_ref[...] = gate.astype(BF16)
        u_ref[...] = up.astype(BF16)
        a_ref[...] = (gate * (1.0 / (1.0 + jnp.exp(-gate))) * up).astype(BF16)

    wspec = pl.BlockSpec((None, d, tf), lambda j, i: (j, 0, 0))
    ospec = pl.BlockSpec((tm, tf), lambda j, i: (i, j))
    osh = jax.ShapeDtypeStruct((s, f), BF16)
    return _pcall(body, (fn, w_gate, w_up), name="ffn_gate_up", grid=(nsp, s // tm),
                  in_specs=[pl.BlockSpec((tm, d), lambda j, i: (i, 0)), wspec, wspec],
                  out_specs=[ospec, ospec, ospec], out_shape=[osh, osh, osh], sem=("parallel", "parallel"), comm=comm)


def _ffn_bwd_act(d_out, w_down, gate, up, *, tm=1024, tf=1408, comm=()):
    s, d = d_out.shape
    f = w_down.shape[0]
    tm, tf = _tile(s, tm), _tile(f, tf)

    sub = tm // 4 if tm % 1024 == 0 else tm

    def body(do_ref, wd_ref, g_ref, u_ref, dg_ref, du_ref):
        groups = [slice(r, r + sub) for r in range(0, tm, sub)]
        parts = [lax.dot_general(do_ref[rows, :].astype(BF16), wd_ref[...], NT_DIMS, preferred_element_type=F32)
                 for rows in groups]
        for rows, d_act in zip(groups, parts):
            gate = g_ref[rows, :].astype(F32)
            sig = 1.0 / (1.0 + jnp.exp(-gate))
            du_ref[rows, :] = (d_act * (gate * sig)).astype(BF16)
            dg_ref[rows, :] = (d_act * u_ref[rows, :].astype(F32) * (sig * (1.0 + gate * (1.0 - sig)))).astype(BF16)

    ospec = pl.BlockSpec((tm, tf), lambda j, i: (i, j))
    osh = jax.ShapeDtypeStruct((s, f), BF16)
    return _pcall(
        body, (d_out, w_down, gate, up), name="ffn_bwd_act", grid=(f // tf, s // tm),
        in_specs=[pl.BlockSpec((tm, d), lambda j, i: (i, 0)), pl.BlockSpec((tf, d), lambda j, i: (j, 0)), ospec, ospec],
        out_specs=[ospec, ospec], out_shape=[osh, osh], sem=("parallel", "parallel"), comm=comm)


def _cols(g4):
    return jnp.concatenate([g4[k] for k in range(N_CHIPS)], axis=1)


def _full_w_in(g4):
    per = IN_WIDTH // N_CHIPS
    kr0 = 2304 - (N_CHIPS - 1) * per
    last = g4[N_CHIPS - 1]
    pad = jnp.zeros((last.shape[0], IN_PAD - IN_WIDTH), last.dtype)
    return jnp.concatenate([g4[0], g4[1], g4[2], last[:, :kr0], last[:, kr0 + 64:], last[:, kr0:kr0 + 64], pad], axis=1)


def _shards_w_in(dwp):
    per = IN_WIDTH // N_CHIPS
    kr0 = 2304 - (N_CHIPS - 1) * per
    last = jnp.concatenate([dwp[:, (N_CHIPS - 1) * per:2304], dwp[:, C_KR:C_KR + 64], dwp[:, 2304:C_KR]], axis=1)
    assert last.shape[1] == per and kr0 == 144
    return jnp.stack([dwp[:, per * k:per * (k + 1)] for k in range(N_CHIPS - 1)] + [last])


def _full_heads(g4, first):
    return jnp.concatenate([g4[k][:, :first] for k in range(N_CHIPS)] + [g4[k][:, first:] for k in range(N_CHIPS)], axis=1)


def _shards_heads(dwp, first, rest):
    base = N_CHIPS * first
    return jnp.stack([jnp.concatenate([dwp[:, first * k:first * (k + 1)], dwp[:, base + rest * k:base + rest * (k + 1)]], axis=1)
                      for k in range(N_CHIPS)])


def _rope_tables(pos):
    inv_freq = ROPE_THETA ** (-jnp.arange(0, MLA_ROPE, 2, dtype=F32) / MLA_ROPE)
    ang = pos.astype(F32)[:, None] * inv_freq
    cos, sin = jnp.cos(ang), jnp.sin(ang)
    return jnp.tile(cos, (1, 4)), jnp.concatenate([-sin, sin, -sin, sin], axis=1)


def _gain_table(sp):
    two = lambda v: jnp.tile(v, (1, 2))
    rows = [two(sp["swa_q_norm_g"]), two(sp["swa_k_norm_g"]), sp["mla_qn_norm_g"], two(sp["mla_qr_norm_g"]),
            sp["mla_kn_norm_g"], two(sp["mla_kr_norm_g"]), sp["mem_q_norm_g"], jnp.zeros((1, LANES), F32)]
    return jnp.concatenate(rows, axis=0)


CHIP_DISTANCES = (1, 2, 3)


def _place():
    x, y, c = lax.axis_index("x"), lax.axis_index("y"), lax.axis_index("c")
    return x, y, c, 2 * x + y


def _chip_at(x, y, d):
    px = 1 - x if d & 2 else x
    py = 1 - y if d & 1 else y
    return px, py, 2 * px + py


def _row_tile(rows, want=512, mult=8):
    t = min(rows, want)
    t -= t % mult
    while rows % t:
        t -= mult
    return t


def _cast_into_slot(w, meta, *, name, comm=()):
    rows, cols = w.shape
    tr = _row_tile(rows, 512, 16)

    def body(meta_ref, w_ref, o_ref):
        o_ref[...] = w_ref[...].astype(BF16)

    return _pcall(body, (w,), name=name, grid=(rows // tr,), prefetch=(meta,),
                  in_specs=[pl.BlockSpec((tr, cols), lambda i, m: (i, 0))],
                  out_specs=pl.BlockSpec((None, tr, cols), lambda i, m: (m[0], i, 0)),
                  out_shape=jax.ShapeDtypeStruct((N_CHIPS, rows, cols), BF16), sem=("parallel",), comm=comm)


def _remote(src, dst, ssem, rsem, i, device):
    return pltpu.make_async_remote_copy(src_ref=src, dst_ref=dst, send_sem=ssem.at[i], recv_sem=rsem.at[i],
                                        device_id=device, device_id_type=MESH)


def _symmetric_stage(ins, out_shapes, aliases, n_sem, copies):
    def issue(i_refs, o_refs, ssem, rsem):
        for send, _ in copies(i_refs, o_refs, ssem, rsem):
            send.start()

    def wait(i_refs, o_refs, ssem, rsem):
        pairs = copies(i_refs, o_refs, ssem, rsem)
        for _, arrival in pairs:
            arrival.wait_recv()
        for send, _ in pairs:
            send.wait_send()

    return _Stage(ins, out_shapes, aliases, n_sem, issue, wait)


def _gather_stage(slots, leg, part=(0, 1)):
    n = len(slots)
    shapes = [jax.ShapeDtypeStruct(s.shape, s.dtype) for s in slots]
    in_place = {w: w for w in range(n)}
    if not isinstance(leg, str):
        legs = list(leg)

        def copies(i_refs, o_refs, ssem, rsem):
            return [pr for k, (which, prt) in enumerate(legs)
                    for pr in _gather_stage(slots, which, prt).leg_copies(which, 3 * n * k)(i_refs, o_refs, ssem, rsem)]

        return _symmetric_stage(slots, shapes, in_place, 3 * n * len(legs), copies)

    def leg_copies(which, base):
        def copies(_, outs, ssem, rsem):
            x, y, c, k_me = _place()
            pairs = []
            for w in range(n):
                half = outs[w].shape[1] // 2
                r0, size = _window(half, part)
                slab = lambda k, cc, w=w, half=half, r0=r0, size=size: outs[w].at[k, pl.ds(cc * half + r0, size)]
                for d in CHIP_DISTANCES:
                    px, py, k_src = _chip_at(x, y, d)
                    i = base + 3 * w + d - 1
                    if which == "ici":
                        pairs.append((_remote(slab(k_me, c), slab(k_me, c), ssem, rsem, i, (px, py, c)),
                                      _remote(slab(k_src, c), slab(k_src, c), ssem, rsem, i, (x, y, c))))
                    else:
                        pairs.append((_remote(slab(k_src, c), slab(k_src, c), ssem, rsem, i, (x, y, 1 - c)),
                                      _remote(slab(k_src, 1 - c), slab(k_src, 1 - c), ssem, rsem, i, (x, y, c))))
            return pairs
        return copies

    if leg != "both":
        st = _symmetric_stage(slots, shapes, in_place, 3 * n, leg_copies(leg, 0))
        st.leg_copies = leg_copies
        return st
    ici = _symmetric_stage(slots, shapes, in_place, 6 * n, leg_copies("ici", 0))
    d2d = _symmetric_stage(slots, shapes, in_place, 6 * n, leg_copies("d2d", 3 * n))

    def mid(*refs):
        ici.wait(*refs)
        d2d.issue(*refs)

    return _Stage(slots, shapes, in_place, 6 * n, ici.issue, d2d.wait, mid)


def _halves_stage(grads):
    n = len(grads)

    def copies(ins, outs, ssem, rsem):
        x, y, c, _ = _place()
        pairs = []
        for w in range(n):
            half = ins[w].shape[1] // 2
            pairs.append((_remote(ins[w].at[:, pl.ds((1 - c) * half, half)], outs[w], ssem, rsem, w, (x, y, 1 - c)),
                          _remote(outs[w], outs[w], ssem, rsem, w, (x, y, c))))
        return pairs

    shapes = [jax.ShapeDtypeStruct((N_CHIPS, g.shape[1] // 2, g.shape[2]), g.dtype) for g in grads]
    return _symmetric_stage(grads, shapes, {}, n, copies)


def _window(rows, part):
    idx, count = part
    size = rows // count
    assert size * count == rows and size % 16 == 0, (rows, part)
    return idx * size, size


def _chips_stage(parts, part=(0, 1), into=None):
    n = len(parts)

    def copies(ins, outs, ssem, rsem):
        x, y, c, _ = _place()
        pairs = []
        for w in range(n):
            r0, size = _window(ins[w].shape[1], part)
            for d in CHIP_DISTANCES:
                px, py, _ = _chip_at(x, y, d)
                i = 3 * w + d - 1
                land = outs[w].at[d - 1, pl.ds(r0, size)]
                pairs.append((_remote(ins[w].at[d - 1, pl.ds(r0, size)], land, ssem, rsem, i, (px, py, c)),
                              _remote(land, land, ssem, rsem, i, (x, y, c))))
        return pairs

    shapes = [jax.ShapeDtypeStruct(p.shape, p.dtype) for p in parts]
    if into is None:
        return _symmetric_stage(parts, shapes, {}, 3 * n, copies)
    return _symmetric_stage(list(parts) + list(into), shapes, {n + w: w for w in range(n)}, 3 * n, copies)


def _swap_stage(totals):
    n = len(totals)

    def copies(ins, outs, ssem, rsem):
        x, y, c, _ = _place()
        return [(_remote(ins[w], outs[w], ssem, rsem, w, (x, y, 1 - c)),
                 _remote(outs[w], outs[w], ssem, rsem, w, (x, y, c))) for w in range(n)]

    shapes = [jax.ShapeDtypeStruct(t.shape, t.dtype) for t in totals]
    return _symmetric_stage(totals, shapes, {}, n, copies)


def _run_stages(stages, *, name):
    n_ins = [len(st.ins) for st in stages]
    n_outs = [len(st.out_shapes) for st in stages]
    tot_in, tot_out = sum(n_ins), sum(n_outs)
    aliases, i0, o0 = {}, 0, 0
    for st, ni, no in zip(stages, n_ins, n_outs):
        aliases.update({i0 + a: o0 + b for a, b in st.aliases.items()})
        i0, o0 = i0 + ni, o0 + no

    def body(*refs):
        sems = refs[tot_in + tot_out:]
        for what in ("issue", "wait"):
            i0, o0 = 0, tot_in
            for k, (st, ni, no) in enumerate(zip(stages, n_ins, n_outs)):
                getattr(st, what)(refs[i0:i0 + ni], refs[o0:o0 + no], sems[2 * k], sems[2 * k + 1])
                i0, o0 = i0 + ni, o0 + no

    sem = pltpu.SemaphoreType.DMA
    res = pl.pallas_call(
        body, name=name, in_specs=[ANY] * tot_in, out_specs=[ANY] * tot_out,
        out_shape=[s for st in stages for s in st.out_shapes], input_output_aliases=aliases,
        scratch_shapes=[sem((st.n_sem,)) for st in stages for _ in range(2)],
    )(*[a for st in stages for a in st.ins])
    outs, o0 = [], 0
    for no in n_outs:
        outs.append(list(res[o0:o0 + no]))
        o0 += no
    return outs


def _add_pair(meta, g4, recv, *, name):
    nsh, rows, cols = g4.shape
    half = rows // 2
    tr = _row_tile(half, 128 if cols > 1024 else 256, 16)
    nt = half // tr

    def body(meta_ref, g0, g1, g2, g3, r0, r1, r2, r3, own_ref, oth_ref):
        own_ref[...] = g0[...] + r0[...]
        for d, (g, r) in enumerate(((g1, r1), (g2, r2), (g3, r3))):
            oth_ref[d] = (g[...] + r[...]).astype(BF16)

    blk = (None, tr, cols)
    gspec = lambda d: pl.BlockSpec(blk, lambda i, m: (jnp.bitwise_xor(m[0], d), m[1] * nt + i, 0))
    rspec = lambda d: pl.BlockSpec(blk, lambda i, m: (jnp.bitwise_xor(m[0], d), i, 0))
    grid_spec = pltpu.PrefetchScalarGridSpec(
        num_scalar_prefetch=1, grid=(nt,),
        in_specs=[gspec(d) for d in range(nsh)] + [rspec(d) for d in range(nsh)],
        out_specs=[pl.BlockSpec((tr, cols), lambda i, m: (i, 0)), pl.BlockSpec((3, tr, cols), lambda i, m: (0, i, 0))])
    return pl.pallas_call(
        body, name=name, grid_spec=grid_spec,
        out_shape=[jax.ShapeDtypeStruct((half, cols), F32), jax.ShapeDtypeStruct((3, half, cols), BF16)],
        compiler_params=_params(("parallel",)),
    )(meta, g4, g4, g4, g4, recv, recv, recv, recv)


def _add_chips(own, recv, *, name):
    half, cols = own.shape
    tr = _row_tile(half, 256, 16)

    def body(p_ref, r_ref, o_ref):
        o_ref[...] = ((p_ref[...] + r_ref[0].astype(F32)) + r_ref[1].astype(F32)) + r_ref[2].astype(F32)

    return pl.pallas_call(
        body, name=name, grid=(half // tr,),
        in_specs=[pl.BlockSpec((tr, cols), lambda i: (i, 0)), pl.BlockSpec((3, tr, cols), lambda i: (0, i, 0))],
        out_specs=pl.BlockSpec((tr, cols), lambda i: (i, 0)),
        out_shape=jax.ShapeDtypeStruct((half, cols), F32),
        compiler_params=_params(("parallel",)),
    )(own, recv)


def _adamw_math(w, g, m, v):
    m = ADAM_B1 * m + (1.0 - ADAM_B1) * g
    v = ADAM_B2 * v + (1.0 - ADAM_B2) * (g * g)
    m_hat = m / (1.0 - ADAM_B1 ** ADAM_STEP)
    v_hat = v / (1.0 - ADAM_B2 ** ADAM_STEP)
    delta = -ADAM_LR * (m_hat / (jnp.sqrt(v_hat) + ADAM_EPS) + ADAM_WD * w)
    return delta, m, v


def _adamw(meta, w, g_mine, g_theirs, m, v, *, name):
    rows, cols = w.shape
    half = rows // 2
    tr = _row_tile(half, 256)
    nt = half // tr

    def body(meta_ref, w_ref, a_ref, b_ref, m_ref, v_ref, g_ref, d_ref, mo_ref, vo_ref):
        is_mine = (pl.program_id(0) // nt) == meta_ref[1]
        g = jnp.where(is_mine, a_ref[...], b_ref[...])
        g_ref[...] = g
        d_ref[...], mo_ref[...], vo_ref[...] = _adamw_math(w_ref[...], g, m_ref[...], v_ref[...])

    blk = pl.BlockSpec((tr, cols), lambda i, mt: (i, 0))
    mine = pl.BlockSpec((tr, cols), lambda i, mt: (jnp.where(i // nt == mt[1], i % nt, 0), 0))
    theirs = pl.BlockSpec((tr, cols), lambda i, mt: (jnp.where(i // nt == mt[1], 0, i % nt), 0))
    sh = jax.ShapeDtypeStruct((rows, cols), F32)
    grid_spec = pltpu.PrefetchScalarGridSpec(
        num_scalar_prefetch=1, grid=(rows // tr,),
        in_specs=[blk, mine, theirs, blk, blk], out_specs=[blk] * 4)
    return pl.pallas_call(
        body, name=name, grid_spec=grid_spec, out_shape=[sh] * 4,
        compiler_params=_params(("arbitrary",)),
    )(meta, w, g_mine, g_theirs, m, v)


N_DEVICES = 8


def _small_step(g_pack, w_pack, m_pack, v_pack):
    rows = g_pack.shape[0]

    def body(g_ref, w_ref, m_ref, v_ref, sum_ref, d_ref, mo_ref, vo_ref, slots, ssem, rsem):
        x, y, c, _ = _place()
        me = 4 * x + 2 * y + c
        slots[me] = g_ref[...]
        copies = []
        for r in range(1, N_DEVICES):
            px = 1 - x if r & 4 else x
            py = 1 - y if r & 2 else y
            pc = 1 - c if r & 1 else c
            copies.append(pltpu.make_async_remote_copy(
                src_ref=g_ref, dst_ref=slots.at[me], send_sem=ssem.at[r - 1], recv_sem=rsem.at[r - 1],
                device_id=(px, py, pc), device_id_type=MESH))
        for cp in copies:
            cp.start()
        for r in range(1, N_DEVICES):
            src = jnp.bitwise_xor(me, r)
            pltpu.make_async_remote_copy(
                src_ref=g_ref, dst_ref=slots.at[src], send_sem=ssem.at[r - 1], recv_sem=rsem.at[r - 1],
                device_id=(x, y, c), device_id_type=MESH).wait_recv()
        for cp in copies:
            cp.wait_send()
        total = slots[0]
        for k in range(1, N_DEVICES):
            total = total + slots[k]
        sum_ref[...] = total
        d_ref[...], mo_ref[...], vo_ref[...] = _adamw_math(w_ref[...], total, m_ref[...], v_ref[...])

    sh = jax.ShapeDtypeStruct((rows, LANES), F32)
    vm = pl.BlockSpec(memory_space=pltpu.VMEM)
    return pl.pallas_call(
        body, name="small_allreduce_adamw",
        in_specs=[vm] * 4, out_specs=[vm] * 4, out_shape=[sh] * 4,
        scratch_shapes=[pltpu.VMEM((N_DEVICES, rows, LANES), F32),
                        pltpu.SemaphoreType.DMA((N_DEVICES - 1,)), pltpu.SemaphoreType.DMA((N_DEVICES - 1,))],
    )(g_pack, w_pack, m_pack, v_pack)


WEIGHTS = ("attn_norm_g", "w_in", "swa_q_norm_g", "swa_k_norm_g", "swa_sinks", "mla_cq_norm_g", "mla_ckv_norm_g",
           "w_uq", "w_ukv", "mla_qn_norm_g", "mla_qr_norm_g", "mla_kn_norm_g", "mla_kr_norm_g", "mem_norm_g",
           "w_mem_kv", "mem_q_norm_g", "mem_k_norm_g", "w_out", "ffn_norm_g", "w_gate", "w_up", "w_down")
BIG = ("w_in", "w_uq", "w_ukv", "w_mem_kv", "w_out", "w_gate", "w_up", "w_down")
SMALL = tuple(n for n in WEIGHTS if n not in BIG)
PACK_UNIT = 8 * LANES


def _pack(parts):
    flat = jnp.concatenate(parts, axis=1)
    total = flat.shape[1]
    padded = -(-total // PACK_UNIT) * PACK_UNIT
    return jnp.pad(flat, ((0, 0), (0, padded - total))).reshape(padded // LANES, LANES)


def _unpack(buf, sizes):
    flat = buf.reshape(1, buf.shape[0] * LANES)
    out, at = [], 0
    for n in sizes:
        out.append(flat[:, at:at + n])
        at += n
    return out


def kernel(x, mem, positions, attn_norm_g, w_in, swa_q_norm_g, swa_k_norm_g, swa_sinks, mla_cq_norm_g, mla_ckv_norm_g, w_uq, w_ukv, mla_qn_norm_g, mla_qr_norm_g, mla_kn_norm_g, mla_kr_norm_g, mem_norm_g, w_mem_kv, mem_q_norm_g, mem_k_norm_g, w_out, ffn_norm_g, w_gate, w_up, w_down, loss_target, m_attn_norm_g, m_w_in, m_swa_q_norm_g, m_swa_k_norm_g, m_swa_sinks, m_mla_cq_norm_g, m_mla_ckv_norm_g, m_w_uq, m_w_ukv, m_mla_qn_norm_g, m_mla_qr_norm_g, m_mla_kn_norm_g, m_mla_kr_norm_g, m_mem_norm_g, m_w_mem_kv, m_mem_q_norm_g, m_mem_k_norm_g, m_w_out, m_ffn_norm_g, m_w_gate, m_w_up, m_w_down, v_attn_norm_g, v_w_in, v_swa_q_norm_g, v_swa_k_norm_g, v_swa_sinks, v_mla_cq_norm_g, v_mla_ckv_norm_g, v_w_uq, v_w_ukv, v_mla_qn_norm_g, v_mla_qr_norm_g, v_mla_kn_norm_g, v_mla_kr_norm_g, v_mem_norm_g, v_w_mem_kv, v_mem_q_norm_g, v_mem_k_norm_g, v_w_out, v_ffn_norm_g, v_w_gate, v_w_up, v_w_down):
    given = dict(locals())
    wts = {n: given[n] for n in WEIGHTS}
    mom_m = {n: given["m_" + n] for n in WEIGHTS}
    mom_v = {n: given["v_" + n] for n in WEIGHTS}

    mx, my, mc = lax.axis_index("x"), lax.axis_index("y"), lax.axis_index("c")
    meta = jnp.stack([2 * mx + my, mc]).astype(jnp.int32)
    x, mem, pos, target = x[0], mem[0], positions[0], loss_target[0]
    sp = {n: wts[n] for n in SMALL}
    s = x.shape[0]
    cos_t, sin_t = _rope_tables(pos)
    pos_f = pos.astype(F32)
    pos_col, pos_row = pos_f.reshape(s, 1), pos_f.reshape(1, s)
    g128 = _gain_table(sp)
    sinks = sp["swa_sinks"].reshape(SWA_Q_HEADS)
    gcq, gckv = sp["mla_cq_norm_g"], sp["mla_ckv_norm_g"]
    gs = {}

    slot = {n: _cast_into_slot(wts[n][0], meta, name="cast_" + n) for n in BIG if n not in ("w_gate", "w_up", "w_down")}
    first = [slot["w_in"], slot["w_uq"], slot["w_ukv"]]
    slot["w_gate"], [first] = _cast_into_slot(wts["w_gate"][0], meta, name="cast_w_gate",
                                              comm=[_gather_stage(first, "ici", (0, 4))])
    slot["w_up"], [first] = _cast_into_slot(wts["w_up"][0], meta, name="cast_w_up",
                                            comm=[_gather_stage(first, [("ici", (1, 4)), ("d2d", (0, 4))])])
    slot["w_down"], [first] = _cast_into_slot(wts["w_down"][0], meta, name="cast_w_down",
                                              comm=[_gather_stage(first, [("ici", (2, 4)), ("d2d", (1, 4))])])
    hn, [first] = _rms_fwd(x, sp["attn_norm_g"], name="attn_norm_fwd",
                           comm=[_gather_stage(first, [("ici", (3, 4)), ("d2d", (2, 4))])])
    [first] = _run_stages([_gather_stage(first, "d2d", (3, 4))], name="gather_first_last_d2d")
    w_in_f, w_uq_f, w_ukv_f = _full_w_in(first[0]), _full_heads(first[1], MLA_NOPE), _full_heads(first[2], MLA_NOPE)

    proj, [mid] = _matmul(hn, w_in_f, name="in_proj",
                          comm=[_gather_stage([slot["w_mem_kv"], slot["w_out"]], "ici")])
    (qa, ka, va, q_cat, k_cat, v_b, qm), [mid, wg] = _attn_prep_fwd(
        proj, g128, gcq, gckv, w_uq_f, w_ukv_f, cos_t, sin_t,
        comm=[_gather_stage(mid, "d2d"), _gather_stage([slot["w_gate"]], "ici", (0, 4))])
    w_mem_kv_f = mid[0].reshape(D_MODEL, 2 * MEM_HEADS * MEM_DIM)
    w_out_f = mid[1].reshape(D_MODEL, D_MODEL)
    mn_b, kv_m, km, vm = _mem_kv_fwd(mem, sp["mem_norm_g"], w_mem_kv_f, sp["mem_k_norm_g"])
    (y_a, y), [wg] = _swa_fwd(qa, ka, va, pos_col, pos_row, sinks,
                              comm=[_gather_stage(wg, [("ici", (k, 4)) for k in (1, 2, 3)])])
    (y_b, lse, y), [wu, wg] = _mla_fwd(
        q_cat, k_cat, v_b, y, comm=[_gather_stage([slot["w_up"]], [("ici", (k, 4)) for k in (0, 1, 2)]),
                                    _gather_stage(wg, "d2d")])
    y_m, y = _mem_attn_fwd(qm, km, vm, y)
    h1, [wu] = _matmul(y, w_out_f, add=x, name="out_proj",
                       comm=[_gather_stage(wu, [("ici", (3, 4))] + [("d2d", (k, 4)) for k in (0, 1, 2)])])
    fn, [wu] = _rms_fwd(h1, sp["ffn_norm_g"], name="ffn_norm_fwd", comm=[_gather_stage(wu, "d2d", (3, 4))])
    w_gate_f, w_up_f = wg[0], wu[0]
    (gate, up, act), [wd] = _ffn_gate_up(fn, w_gate_f, w_up_f, comm=[_gather_stage([slot["w_down"]], "both")])
    w_down_f = wd[0].reshape(D_FF, D_MODEL)
    d_out, d_out_b, loss_tile = _matmul(act, w_down_f, add=h1, name="down_proj", tm=512, tk=D_FF, loss_target=target)

    add_pair = lambda n, g4, r: _add_pair(meta, g4, r, name="grad_add_pair_" + n)
    add_chips = lambda n, own, r: _add_chips(own, r, name="grad_add_chips_" + n)
    mine, theirs = {}, {}

    dw_down = _matmul(act, d_out_b, ta=True, name="dw_down", tm=512, tn=1024, tk=s)
    dw_down = dw_down.reshape(N_CHIPS, D_FF // N_CHIPS, D_MODEL)
    (d_gate, d_up), [[r]] = _ffn_bwd_act(d_out_b, w_down_f, gate, up, comm=[_halves_stage([dw_down])])
    own_d, oth_d = add_pair("w_down", dw_down, r)
    dw_gate, [rd] = _matmul(fn, d_gate, ta=True, name="dw_gate", tm=512, tk=s, tn=D_FF // N_CHIPS, out_split=N_CHIPS,
                            comm=[_chips_stage([oth_d], (0, 2))])
    dw_up, [[r], rd] = _matmul(fn, d_up, ta=True, name="dw_up", tm=512, tk=s, tn=D_FF // N_CHIPS, out_split=N_CHIPS,
                               comm=[_halves_stage([dw_gate]), _chips_stage([oth_d], (1, 2), into=rd)])
    mine["w_down"] = add_chips("w_down", own_d, rd[0])
    own_g, oth_g = add_pair("w_gate", dw_gate, r)
    d_fn, [rg, [theirs["w_down"]]] = _matmul(
        d_gate, w_gate_f, tb=True, b_split=True, name="dfn_gate", tm=512,
        comm=[_chips_stage([oth_g], (0, 2)), _swap_stage([mine["w_down"]])])
    d_fn, [[r], rg] = _matmul(d_up, w_up_f, tb=True, b_split=True, add=d_fn, name="dfn_up", tm=512,
                              comm=[_halves_stage([dw_up]), _chips_stage([oth_g], (1, 2), into=rg)])
    mine["w_gate"] = add_chips("w_gate", own_g, rg[0])
    own_u, oth_u = add_pair("w_up", dw_up, r)
    d_h1, d_h1_b, gs["ffn_norm_g"] = _rms_bwd(d_fn, h1, sp["ffn_norm_g"], d_out, name="ffn_norm_bwd")
    dw_out, [[theirs["w_gate"]]] = _matmul(y, d_h1_b, ta=True, name="dw_out", tm=512, tk=s,
                                           comm=[_swap_stage([mine["w_gate"]])])
    dw_out = dw_out.reshape(N_CHIPS, D_MODEL // N_CHIPS, D_MODEL)
    d_y, [[r]] = _matmul(d_h1_b, w_out_f, tb=True, name="dy", comm=[_halves_stage([dw_out])])
    own_o, oth_o = add_pair("w_out", dw_out, r)
    (d_qa, d_ka, d_va, d_sink), [ru] = _swa_bwd(qa, ka, va, pos_col, pos_row, sinks, y_a, d_y,
                                                comm=[_chips_stage([oth_u], (0, 2))])
    (d_qcat, d_kcat, d_vb), [ru, [r]] = _mla_bwd(
        q_cat, k_cat, v_b, y_b, lse, d_y, comm=[_chips_stage([oth_u], (1, 2), into=ru), _chips_stage([oth_o])])
    mine["w_up"] = add_chips("w_up", own_u, ru[0])
    mine["w_out"] = add_chips("w_out", own_o, r)
    d_qm, d_km, d_vm = _mem_attn_bwd(qm, km, vm, y_m, d_y)
    (d_proj, dw_uq, dw_ukv, dg128, gs["mla_cq_norm_g"], gs["mla_ckv_norm_g"]), [[theirs["w_up"], theirs["w_out"]]] = \
        _attn_prep_bwd(proj, g128, gcq, gckv, w_uq_f, w_ukv_f, cos_t, sin_t, d_qa, d_ka, d_va, d_qcat, d_kcat, d_vb,
                       d_qm, comm=[_swap_stage([mine["w_up"], mine["w_out"]])])
    dw_mem_kv, gs["mem_norm_g"], gs["mem_k_norm_g"] = _mem_kv_bwd(
        mem, sp["mem_norm_g"], w_mem_kv_f, sp["mem_k_norm_g"], mn_b, kv_m, d_km, d_vm)
    late = ("w_uq", "w_ukv", "w_mem_kv")
    late_g = [_shards_heads(dw_uq, MLA_NOPE, MLA_ROPE), _shards_heads(dw_ukv, MLA_NOPE, MLA_V),
              dw_mem_kv.reshape(N_CHIPS, D_MODEL // N_CHIPS, -1)]
    dw_in, [rs] = _matmul(hn, d_proj, ta=True, name="dw_in", tm=512, tk=s, comm=[_halves_stage(late_g)])
    late_sums = [add_pair(n, g4, r) for n, g4, r in zip(late, late_g, rs)]
    dw_in = _shards_w_in(dw_in)
    d_hn, [rs, [r]] = _matmul(d_proj, w_in_f, tb=True, name="dhn", tk=1536,
                              comm=[_chips_stage([oth for _, oth in late_sums]), _halves_stage([dw_in])])
    for n, (own, _), r_n in zip(late, late_sums, rs):
        mine[n] = add_chips(n, own, r_n)
    own_i, oth_i = add_pair("w_in", dw_in, r)
    (grad_x, _, gs["attn_norm_g"]), [[r], late_theirs] = _rms_bwd(
        d_hn, x, sp["attn_norm_g"], d_h1, name="attn_norm_bwd",
        comm=[_chips_stage([oth_i]), _swap_stage([mine[n] for n in late])])
    theirs.update(zip(late, late_theirs))
    mine["w_in"] = add_chips("w_in", own_i, r)
    [[theirs["w_in"]]] = _run_stages([_swap_stage([mine["w_in"]])], name="grad_swap_w_in")

    fold = lambda r: r[:, :64] + r[:, 64:]
    gs["swa_q_norm_g"] = fold(dg128[G_SWA_Q:G_SWA_Q + 1])
    gs["swa_k_norm_g"] = fold(dg128[G_SWA_K:G_SWA_K + 1])
    gs["mla_qn_norm_g"] = dg128[G_QN:G_QN + 1]
    gs["mla_qr_norm_g"] = fold(dg128[G_QR:G_QR + 1])
    gs["mla_kn_norm_g"] = dg128[G_KN:G_KN + 1]
    gs["mla_kr_norm_g"] = fold(dg128[G_KR:G_KR + 1])
    gs["mem_q_norm_g"] = dg128[G_MQ:G_MQ + 1]
    gs["swa_sinks"] = d_sink[:, :SWA_Q_HEADS]

    grad, delta, new_m, new_v = {}, {}, {}, {}
    for n in BIG:
        g2, d, m2, v2 = _adamw(meta, wts[n][0], mine[n], theirs[n], mom_m[n][0], mom_v[n][0], name="adamw_" + n)
        grad[n], delta[n], new_m[n], new_v[n] = g2[None], d[None], m2[None], v2[None]

    sizes = [wts[n].shape[1] for n in SMALL]
    zero = jnp.zeros((1, LANES), F32)
    packs = _small_step(_pack([gs[n] for n in SMALL] + [loss_tile]), _pack([wts[n] for n in SMALL] + [zero]),
                        _pack([mom_m[n] for n in SMALL] + [zero]), _pack([mom_v[n] for n in SMALL] + [zero]))
    for store, buf in zip((grad, delta, new_m, new_v), packs):
        for n, val in zip(SMALL, _unpack(buf, sizes)):
            store[n] = val
    loss = _unpack(packs[0], sizes + [LANES])[-1][0, 0]

    return (loss, grad_x[None], *[grad[n] for n in WEIGHTS], *[delta[n] for n in WEIGHTS],
            *[new_m[n] for n in WEIGHTS], *[new_v[n] for n in WEIGHTS])
```

```python
import functools
import math

import jax
import jax.numpy as jnp
from jax import lax
from jax.experimental import pallas as pl
from jax.experimental.pallas import tpu as pltpu

F32 = jnp.float32
BF16 = jnp.bfloat16

D_MODEL = 2048
BLOCK = 128
EPS = 1e-6
NEG_INF = -1e30
SWA_Q_HEADS = 16
SWA_KV_HEADS = 2
SWA_HEAD_DIM = 64
MLA_HEADS = 4
MLA_RANK = 512
MLA_NOPE = 128
MLA_ROPE = 64
MLA_V = 128
ROPE_THETA = 10000.0
MEM_HEADS = 4
MEM_DIM = 128
D_FF = 5632
IN_WIDTH = 2880
IN_PAD = 3072
N_CHIPS = 4

ADAM_LR = 0.001
ADAM_B1 = 0.9
ADAM_B2 = 0.999
ADAM_EPS = 1e-08
ADAM_WD = 0.01
ADAM_STEP = 10

VMEM_LIMIT_BYTES = 56 * 1024 * 1024
LANES = 128

MESH = pl.DeviceIdType.MESH


def _params(sem=None, **kw):
    return pltpu.CompilerParams(dimension_semantics=sem, vmem_limit_bytes=VMEM_LIMIT_BYTES, **kw)


def _tile(n, want):
    if n <= want:
        return n
    t = want - want % LANES
    while t > 0:
        if n % t == 0:
            return t
        t -= LANES
    return n


ANY = pl.BlockSpec(memory_space=pl.ANY)


class _Stage:
    def __init__(self, ins, out_shapes, aliases, n_sem, issue, wait, mid=None):
        self.ins, self.out_shapes, self.aliases, self.n_sem = list(ins), list(out_shapes), dict(aliases), n_sem
        self.issue, self.wait, self.mid = issue, wait, mid


def _pcall(body, args, *, name, grid, in_specs, out_specs, out_shape, scratch_shapes=(), sem=None, comm=(),
           prefetch=(), io_alias=None):
    multi = isinstance(out_shape, (list, tuple))
    out_specs_l = list(out_specs) if multi else [out_specs]
    out_shape_l = list(out_shape) if multi else [out_shape]
    npf = len(prefetch)
    own_aliases = {npf + a: o for a, o in (io_alias or {}).items()}

    def call(fn, in_specs_, out_specs_, out_shape_, scratch_, operands, sem_, aliases=None):
        kw = dict(name=name, out_shape=out_shape_, compiler_params=_params(sem_))
        if aliases:
            kw["input_output_aliases"] = aliases
        if npf:
            spec = pltpu.PrefetchScalarGridSpec(num_scalar_prefetch=npf, grid=grid, in_specs=in_specs_,
                                                out_specs=out_specs_, scratch_shapes=scratch_)
            return pl.pallas_call(fn, grid_spec=spec, **kw)(*prefetch, *operands)
        return pl.pallas_call(fn, grid=grid, in_specs=in_specs_, out_specs=out_specs_, scratch_shapes=scratch_,
                              **kw)(*operands)

    if not comm:
        return call(body, list(in_specs), out_specs, out_shape, list(scratch_shapes), args, sem, own_aliases)
    n_in, n_out, n_scr = len(in_specs), len(out_specs_l), len(scratch_shapes)
    cins = [a for st in comm for a in st.ins]
    couts = [s for st in comm for s in st.out_shapes]
    aliases, ci, co = dict(own_aliases), 0, 0
    for st in comm:
        for a_i, o_i in st.aliases.items():
            aliases[npf + n_in + ci + a_i] = n_out + co + o_i
        ci, co = ci + len(st.ins), co + len(st.out_shapes)

    def wrapped(*refs):
        pre = refs[:npf]
        p = npf
        ins = refs[p:p + n_in]; p += n_in
        cin_refs = refs[p:p + len(cins)]; p += len(cins)
        outs = refs[p:p + n_out]; p += n_out
        cout_refs = refs[p:p + len(couts)]; p += len(couts)
        scr = refs[p:p + n_scr]; p += n_scr
        sems = refs[p:]
        first = functools.reduce(jnp.logical_and, [pl.program_id(a) == 0 for a in range(len(grid))])
        last = functools.reduce(jnp.logical_and, [pl.program_id(a) == grid[a] - 1 for a in range(len(grid))])

        def each(what):
            i, o = 0, 0
            for k, st in enumerate(comm):
                fn = getattr(st, what)
                if fn is not None:
                    fn(cin_refs[i:i + len(st.ins)], cout_refs[o:o + len(st.out_shapes)], sems[2 * k], sems[2 * k + 1])
                i, o = i + len(st.ins), o + len(st.out_shapes)

        @pl.when(first)
        def _():
            each("issue")

        if any(st.mid is not None for st in comm):
            n_steps = math.prod(grid)
            assert n_steps >= 4, "a two-leg stage needs a carrier with several grid steps"
            lin = functools.reduce(lambda acc, a: acc * grid[a] + pl.program_id(a), range(len(grid)), 0)

            @pl.when(lin == (3 * n_steps) // 4)
            def _():
                each("mid")

        body(*pre, *ins, *outs, *scr)

        @pl.when(last)
        def _():
            each("wait")

    sem_scr = [pltpu.SemaphoreType.DMA((st.n_sem,)) for st in comm for _ in range(2)]
    res = call(wrapped, list(in_specs) + [ANY] * len(cins), out_specs_l + [ANY] * len(couts), out_shape_l + couts,
               list(scratch_shapes) + sem_scr, (*args, *cins), ("arbitrary",) * len(grid), aliases)
    normal = list(res[:n_out])
    stage_outs, o = [], n_out
    for st in comm:
        stage_outs.append(list(res[o:o + len(st.out_shapes)]))
        o += len(st.out_shapes)
    return (normal if multi else normal[0]), stage_outs


def _matmul(a, b, *, name, ta=False, tb=False, add=None, out_dtype=F32, tm=1024, tn=1024, tk=2048,
            b_split=False, out_split=0, comm=(), loss_target=None):
    if ta:
        kdim, m = a.shape
    else:
        m, kdim = a.shape
    if b_split:
        assert tb
        nsp, n, kb = b.shape
        kb = kb * nsp
    elif tb:
        n, kb = b.shape
    else:
        kb, n = b.shape
    assert kb == kdim, (a.shape, b.shape, ta, tb)
    if b_split:
        tk = kdim
    if out_split:
        tn = _tile(n // out_split, tn)
    tm, tn, tk = _tile(m, tm), _tile(n, tn), _tile(kdim, tk)
    nk = kdim // tk
    dims = (((0 if ta else 1,), (1 if tb else 0,)), ((), ()))

    def product(a_ref, b_ref):
        if not b_split:
            return lax.dot_general(a_ref[...].astype(BF16), b_ref[...].astype(BF16), dims, preferred_element_type=F32)
        per = kdim // nsp
        return sum(lax.dot_general(a_ref[:, per * c:per * (c + 1)].astype(BF16), b_ref[c].astype(BF16), dims,
                                   preferred_element_type=F32) for c in range(nsp))

    def body(*refs):
        a_ref, b_ref = refs[:2]
        add_ref = refs[2] if add is not None else None
        n_in = 2 + (add is not None) + (loss_target is not None)
        o_ref = refs[n_in]

        def finish(r):
            if add_ref is not None:
                r = r + add_ref[...].astype(F32)
            if loss_target is None:
                o_ref[...] = r.astype(o_ref.dtype)
                return
            db_ref, l_ref = refs[n_in + 1], refs[n_in + 2]
            err = r - refs[n_in - 1][...]
            d_out = err * (1.0 / n)
            o_ref[...] = d_out
            db_ref[...] = d_out.astype(BF16)
            part = jnp.broadcast_to((0.5 / n) * jnp.sum(jnp.sum(err * err, axis=-1, keepdims=True), axis=0, keepdims=True),
                                    (1, LANES))
            first = jnp.logical_and(pl.program_id(0) == 0, pl.program_id(1) == 0)

            @pl.when(first)
            def _():
                l_ref[...] = part

            @pl.when(jnp.logical_not(first))
            def _():
                l_ref[...] += part

        if nk == 1:
            finish(product(a_ref, b_ref))
            return
        acc_ref = refs[-1]
        k = pl.program_id(2)
        part = product(a_ref, b_ref)

        @pl.when(k == 0)
        def _():
            acc_ref[...] = part

        @pl.when(k > 0)
        def _():
            acc_ref[...] += part

        @pl.when(k == nk - 1)
        def _():
            finish(acc_ref[...])

    a_spec = pl.BlockSpec((tk, tm), lambda i, j, k: (k, i)) if ta else pl.BlockSpec((tm, tk), lambda i, j, k: (i, k))
    if b_split:
        b_spec = pl.BlockSpec((nsp, tn, kdim // nsp), lambda i, j, k: (0, j, 0))
    elif tb:
        b_spec = pl.BlockSpec((tn, tk), lambda i, j, k: (j, k))
    else:
        b_spec = pl.BlockSpec((tk, tn), lambda i, j, k: (k, j))
    in_specs = [a_spec, b_spec]
    args = [a, b]
    if add is not None:
        in_specs.append(pl.BlockSpec((tm, tn), lambda i, j, k: (i, j)))
        args.append(add)
    tile = pl.BlockSpec((tm, tn), lambda i, j, k: (i, j))
    sem = ("parallel", "parallel", "arbitrary")
    if out_split:
        per = (n // out_split) // tn
        out_spec = pl.BlockSpec((None, tm, tn), lambda i, j, k: (j // per, i, j % per))
        out_shape = jax.ShapeDtypeStruct((out_split, m, n // out_split), out_dtype)
    elif loss_target is not None:
        in_specs.append(tile)
        args.append(loss_target)
        out_spec = [tile, tile, pl.BlockSpec((1, LANES), lambda i, j, k: (0, 0))]
        out_shape = [jax.ShapeDtypeStruct((m, n), F32), jax.ShapeDtypeStruct((m, n), BF16),
                     jax.ShapeDtypeStruct((1, LANES), F32)]
        sem = ("arbitrary",) * 3
    else:
        out_spec = tile
        out_shape = jax.ShapeDtypeStruct((m, n), out_dtype)
    return _pcall(body, args, name=name, grid=(m // tm, n // tn, nk), in_specs=in_specs, out_specs=out_spec,
                  out_shape=out_shape, scratch_shapes=[pltpu.VMEM((tm, tn), F32)] if nk > 1 else [],
                  sem=sem, comm=comm)


def _rms_fwd(x, g, *, name, tm=512, comm=()):
    s, d = x.shape
    tm = _tile(s, tm)

    def body(x_ref, g_ref, o_ref):
        xv = x_ref[...]
        r = lax.rsqrt(jnp.mean(xv * xv, axis=-1, keepdims=True) + EPS)
        o_ref[...] = (xv * r * g_ref[...]).astype(o_ref.dtype)

    return _pcall(body, (x, g), name=name, grid=(s // tm,),
                  in_specs=[pl.BlockSpec((tm, d), lambda i: (i, 0)), pl.BlockSpec((1, d), lambda i: (0, 0))],
                  out_specs=pl.BlockSpec((tm, d), lambda i: (i, 0)),
                  out_shape=jax.ShapeDtypeStruct((s, d), BF16), sem=("parallel",), comm=comm)


def _rms_bwd(dy, x, g, res, *, name, tm=512, comm=()):
    s, d = x.shape
    tm = _tile(s, tm)

    def body(dy_ref, x_ref, g_ref, res_ref, dx_ref, dxb_ref, dg_ref):
        xv = x_ref[...]
        dyv = dy_ref[...]
        r = lax.rsqrt(jnp.mean(xv * xv, axis=-1, keepdims=True) + EPS)
        xhat = xv * r
        dyg = dyv * g_ref[...]
        mt = jnp.mean(dyg * xhat, axis=-1, keepdims=True)
        dx = res_ref[...] + r * (dyg - xhat * mt)
        dx_ref[...] = dx
        dxb_ref[...] = dx.astype(BF16)
        part = jnp.sum(dyv * xhat, axis=0, keepdims=True)

        @pl.when(pl.program_id(0) == 0)
        def _():
            dg_ref[...] = part

        @pl.when(pl.program_id(0) > 0)
        def _():
            dg_ref[...] += part

    row = pl.BlockSpec((tm, d), lambda i: (i, 0))
    vec = pl.BlockSpec((1, d), lambda i: (0, 0))
    return _pcall(body, (dy, x, g, res), name=name, grid=(s // tm,), in_specs=[row, row, vec, row],
                  out_specs=[row, row, vec],
                  out_shape=[jax.ShapeDtypeStruct((s, d), F32), jax.ShapeDtypeStruct((s, d), BF16),
                             jax.ShapeDtypeStruct((1, d), F32)],
                  sem=("arbitrary",), comm=comm)


def _lane(shape):
    return lax.broadcasted_iota(jnp.int32, shape, 1)


def _halfsum(t, lo):
    s_lo = jnp.sum(jnp.where(lo, t, 0.0), axis=-1, keepdims=True)
    s_hi = jnp.sum(jnp.where(lo, 0.0, t), axis=-1, keepdims=True)
    return jnp.where(lo, s_lo, s_hi)


def _norm_pair(x, g, lo):
    r = lax.rsqrt(_halfsum(x * x, lo) * (1.0 / 64.0) + EPS)
    xhat = x * r
    return xhat * g, xhat, r


def _norm_pair_bwd(dy, g, xhat, r, lo):
    dyg = dy * g
    mt = _halfsum(dyg * xhat, lo) * (1.0 / 64.0)
    return r * (dyg - xhat * mt), jnp.sum(dy * xhat, axis=0, keepdims=True)


def _norm_full(x, g):
    r = lax.rsqrt(jnp.mean(x * x, axis=-1, keepdims=True) + EPS)
    xhat = x * r
    return xhat * g, xhat, r


def _norm_full_bwd(dy, g, xhat, r):
    dyg = dy * g
    mt = jnp.mean(dyg * xhat, axis=-1, keepdims=True)
    return r * (dyg - xhat * mt), jnp.sum(dy * xhat, axis=0, keepdims=True)


def _rot(x, first32):
    return jnp.where(first32, pltpu.roll(x, 96, axis=1), pltpu.roll(x, 32, axis=1))


def _rope(x, cos_t, sin_t, first32):
    return x * cos_t + _rot(x, first32) * sin_t


def _rope_bwd(dy, cos_t, sin_t, first32):
    return dy * cos_t + _rot(dy * sin_t, first32)


G_SWA_Q, G_SWA_K, G_QN, G_QR, G_KN, G_KR, G_MQ = range(7)

C_QA, C_KA, C_VA, C_CQ, C_CKV, C_QM, C_KR = 0, 1024, 1152, 1280, 1792, 2304, 2816


def _prep_common(p_ref, g128_ref, gcq_ref, gckv_ref, wuq_ref, wukv_ref, cos_ref, sin_ref):
    tm = p_ref.shape[0]
    lane = _lane((tm, LANES))
    lo = lane < 64
    first32 = (lane % 64) < 32
    cos_t = cos_ref[...]
    sin_t = sin_ref[...]
    g = lambda row: g128_ref[row:row + 1, :]
    out = dict(lo=lo, first32=first32, cos_t=cos_t, sin_t=sin_t, lane=lane)
    cq_n, cq_hat, cq_r = _norm_full(p_ref[:, C_CQ:C_CQ + MLA_RANK], gcq_ref[...])
    ckv_n, ckv_hat, ckv_r = _norm_full(p_ref[:, C_CKV:C_CKV + MLA_RANK], gckv_ref[...])
    cq_b = cq_n.astype(BF16)
    ckv_b = ckv_n.astype(BF16)
    q_b = jnp.dot(cq_b, wuq_ref[...], preferred_element_type=F32)
    kv_b = jnp.dot(ckv_b, wukv_ref[...], preferred_element_type=F32)
    out.update(cq_b=cq_b, cq_hat=cq_hat, cq_r=cq_r, ckv_b=ckv_b, ckv_hat=ckv_hat, ckv_r=ckv_r, q_b=q_b, kv_b=kv_b, g=g)
    return out


def _attn_prep_fwd(proj, g128, gcq, gckv, wuq, wukv, cos_t, sin_t, *, tm=512, comm=()):
    s = proj.shape[0]
    tm = _tile(s, tm)

    def body(p_ref, g128_ref, gcq_ref, gckv_ref, wuq_ref, wukv_ref, cos_ref, sin_ref,
             qa_ref, ka_ref, va_ref, qcat_ref, kcat_ref, vb_ref, qm_ref):
        c = _prep_common(p_ref, g128_ref, gcq_ref, gckv_ref, wuq_ref, wukv_ref, cos_ref, sin_ref)
        lo, first32, g = c["lo"], c["first32"], c["g"]
        for j in range(SWA_Q_HEADS // 2):
            y, _, _ = _norm_pair(p_ref[:, C_QA + 128 * j:C_QA + 128 * (j + 1)], g(G_SWA_Q), lo)
            qa_ref[:, 128 * j:128 * (j + 1)] = y.astype(BF16)
        y, _, _ = _norm_pair(p_ref[:, C_KA:C_KA + 128], g(G_SWA_K), lo)
        ka_ref[...] = y.astype(BF16)
        va_ref[...] = p_ref[:, C_VA:C_VA + 128].astype(BF16)
        kr, _, _ = _norm_pair(p_ref[:, C_KR:C_KR + 128], g(G_KR), lo)
        kr = jnp.where(lo, _rope(kr, c["cos_t"], c["sin_t"], first32), 0.0)
        krkr = (kr + pltpu.roll(kr, 64, axis=1)).astype(BF16)
        q_b, kv_b = c["q_b"], c["kv_b"]
        qr = []
        for j in range(MLA_HEADS // 2):
            y, _, _ = _norm_pair(q_b[:, 512 + 128 * j:512 + 128 * (j + 1)], g(G_QR), lo)
            qr.append(_rope(y, c["cos_t"], c["sin_t"], first32))
        for h in range(MLA_HEADS):
            qn, _, _ = _norm_full(q_b[:, 128 * h:128 * (h + 1)], g(G_QN))
            keep = lo if h % 2 == 0 else jnp.logical_not(lo)
            qcat_ref[h, :, 0:128] = qn.astype(BF16)
            qcat_ref[h, :, 128:256] = jnp.where(keep, qr[h // 2], 0.0).astype(BF16)
            kn, _, _ = _norm_full(kv_b[:, 128 * h:128 * (h + 1)], g(G_KN))
            kcat_ref[h, :, 0:128] = kn.astype(BF16)
            kcat_ref[h, :, 128:256] = krkr
        vb_ref[...] = kv_b[:, 512:1024].astype(BF16)
        for h in range(MEM_HEADS):
            y, _, _ = _norm_full(p_ref[:, C_QM + 128 * h:C_QM + 128 * (h + 1)], g(G_MQ))
            qm_ref[:, 128 * h:128 * (h + 1)] = y.astype(BF16)

    row = lambda w: pl.BlockSpec((tm, w), lambda i: (i, 0))
    full = lambda shape: pl.BlockSpec(shape, lambda i: tuple(0 for _ in shape))
    cat = pl.BlockSpec((MLA_HEADS, tm, 256), lambda i: (0, i, 0))
    return _pcall(
        body, (proj, g128, gcq, gckv, wuq, wukv, cos_t, sin_t), name="attn_prep_fwd", grid=(s // tm,),
        in_specs=[row(IN_PAD), full((8, 128)), full((1, 512)), full((1, 512)), full((512, 768)), full((512, 1024)),
                  row(128), row(128)],
        out_specs=[row(1024), row(128), row(128), cat, cat, row(512), row(512)],
        out_shape=[jax.ShapeDtypeStruct((s, 1024), BF16), jax.ShapeDtypeStruct((s, 128), BF16),
                   jax.ShapeDtypeStruct((s, 128), BF16), jax.ShapeDtypeStruct((MLA_HEADS, s, 256), BF16),
                   jax.ShapeDtypeStruct((MLA_HEADS, s, 256), BF16), jax.ShapeDtypeStruct((s, 512), BF16),
                   jax.ShapeDtypeStruct((s, 512), BF16)],
        sem=("parallel",), comm=comm)


def _attn_prep_bwd(proj, g128, gcq, gckv, wuq, wukv, cos_t, sin_t,
                   d_qa, d_ka, d_va, d_qcat, d_kcat, d_vb, d_qm, *, tm=256, comm=()):
    s = proj.shape[0]
    tm = _tile(s, tm)

    def body(p_ref, g128_ref, gcq_ref, gckv_ref, wuq_ref, wukv_ref, cos_ref, sin_ref,
             dqa_ref, dka_ref, dva_ref, dqcat_ref, dkcat_ref, dvb_ref, dqm_ref,
             dp_ref, dwuq_ref, dwukv_ref, dg128_ref, dgcq_ref, dgckv_ref):
        c = _prep_common(p_ref, g128_ref, gcq_ref, gckv_ref, wuq_ref, wukv_ref, cos_ref, sin_ref)
        lo, first32, g = c["lo"], c["first32"], c["g"]
        cos_v, sin_v = c["cos_t"], c["sin_t"]
        q_b, kv_b = c["q_b"], c["kv_b"]
        zero_row = jnp.zeros((1, LANES), F32)
        dg = {k: zero_row for k in range(7)}

        for j in range(SWA_Q_HEADS // 2):
            sl = slice(C_QA + 128 * j, C_QA + 128 * (j + 1))
            _, xhat, r = _norm_pair(p_ref[:, sl], g(G_SWA_Q), lo)
            dx, dgj = _norm_pair_bwd(dqa_ref[:, 128 * j:128 * (j + 1)], g(G_SWA_Q), xhat, r, lo)
            dp_ref[:, sl] = dx.astype(BF16)
            dg[G_SWA_Q] = dg[G_SWA_Q] + dgj
        _, xhat, r = _norm_pair(p_ref[:, C_KA:C_KA + 128], g(G_SWA_K), lo)
        dx, dgj = _norm_pair_bwd(dka_ref[...], g(G_SWA_K), xhat, r, lo)
        dp_ref[:, C_KA:C_KA + 128] = dx.astype(BF16)
        dg[G_SWA_K] = dgj
        dp_ref[:, C_VA:C_VA + 128] = dva_ref[...].astype(BF16)

        dqb_parts = [None] * 6
        for h in range(MLA_HEADS):
            _, xhat, r = _norm_full(q_b[:, 128 * h:128 * (h + 1)], g(G_QN))
            dx, dgj = _norm_full_bwd(dqcat_ref[h, :, 0:128], g(G_QN), xhat, r)
            dqb_parts[h] = dx
            dg[G_QN] = dg[G_QN] + dgj
        for j in range(MLA_HEADS // 2):
            _, xhat, r = _norm_pair(q_b[:, 512 + 128 * j:512 + 128 * (j + 1)], g(G_QR), lo)
            d_rot = jnp.where(lo, dqcat_ref[2 * j, :, 128:256], dqcat_ref[2 * j + 1, :, 128:256])
            d_y = _rope_bwd(d_rot, cos_v, sin_v, first32)
            dx, dgj = _norm_pair_bwd(d_y, g(G_QR), xhat, r, lo)
            dqb_parts[4 + j] = dx
            dg[G_QR] = dg[G_QR] + dgj
        d_qb = jnp.concatenate(dqb_parts, axis=1).astype(BF16)
        dwuq = lax.dot_general(c["cq_b"], d_qb, (((0,), (0,)), ((), ())), preferred_element_type=F32)
        d_cqn = lax.dot_general(d_qb, wuq_ref[...], (((1,), (1,)), ((), ())), preferred_element_type=F32)
        dx, dgcq = _norm_full_bwd(d_cqn, gcq_ref[...], c["cq_hat"], c["cq_r"])
        dp_ref[:, C_CQ:C_CQ + MLA_RANK] = dx.astype(BF16)

        dkv_parts = []
        d_krkr = jnp.zeros((p_ref.shape[0], LANES), F32)
        for h in range(MLA_HEADS):
            _, xhat, r = _norm_full(kv_b[:, 128 * h:128 * (h + 1)], g(G_KN))
            dx, dgj = _norm_full_bwd(dkcat_ref[h, :, 0:128], g(G_KN), xhat, r)
            dkv_parts.append(dx)
            dg[G_KN] = dg[G_KN] + dgj
            d_krkr = d_krkr + dkcat_ref[h, :, 128:256]
        d_kvb = jnp.concatenate(dkv_parts + [dvb_ref[...]], axis=1).astype(BF16)
        dwukv = lax.dot_general(c["ckv_b"], d_kvb, (((0,), (0,)), ((), ())), preferred_element_type=F32)
        d_ckvn = lax.dot_general(d_kvb, wukv_ref[...], (((1,), (1,)), ((), ())), preferred_element_type=F32)
        dx, dgckv = _norm_full_bwd(d_ckvn, gckv_ref[...], c["ckv_hat"], c["ckv_r"])
        dp_ref[:, C_CKV:C_CKV + MLA_RANK] = dx.astype(BF16)

        _, xhat, r = _norm_pair(p_ref[:, C_KR:C_KR + 128], g(G_KR), lo)
        d_kr = jnp.where(lo, d_krkr + pltpu.roll(d_krkr, 64, axis=1), 0.0)
        d_y = jnp.where(lo, _rope_bwd(d_kr, cos_v, sin_v, first32), 0.0)
        dx, dgj = _norm_pair_bwd(d_y, g(G_KR), xhat, r, lo)
        dp_ref[:, C_KR:C_KR + 128] = jnp.where(lo, dx, 0.0).astype(BF16)
        dp_ref[:, C_KR + 128:] = jnp.zeros((p_ref.shape[0], IN_PAD - C_KR - 128), BF16)
        dg[G_KR] = dgj

        for h in range(MEM_HEADS):
            sl = slice(C_QM + 128 * h, C_QM + 128 * (h + 1))
            _, xhat, r = _norm_full(p_ref[:, sl], g(G_MQ))
            dx, dgj = _norm_full_bwd(dqm_ref[:, 128 * h:128 * (h + 1)], g(G_MQ), xhat, r)
            dp_ref[:, sl] = dx.astype(BF16)
            dg[G_MQ] = dg[G_MQ] + dgj

        dg_tile = jnp.concatenate([dg[k] for k in range(7)] + [zero_row], axis=0)

        @pl.when(pl.program_id(0) == 0)
        def _():
            dwuq_ref[...] = dwuq
            dwukv_ref[...] = dwukv
            dg128_ref[...] = dg_tile
            dgcq_ref[...] = dgcq
            dgckv_ref[...] = dgckv

        @pl.when(pl.program_id(0) > 0)
        def _():
            dwuq_ref[...] += dwuq
            dwukv_ref[...] += dwukv
            dg128_ref[...] += dg_tile
            dgcq_ref[...] += dgcq
            dgckv_ref[...] += dgckv

    row = lambda w: pl.BlockSpec((tm, w), lambda i: (i, 0))
    full = lambda shape: pl.BlockSpec(shape, lambda i: tuple(0 for _ in shape))
    cat = pl.BlockSpec((MLA_HEADS, tm, 256), lambda i: (0, i, 0))
    return _pcall(
        body, (proj, g128, gcq, gckv, wuq, wukv, cos_t, sin_t, d_qa, d_ka, d_va, d_qcat, d_kcat, d_vb, d_qm),
        name="attn_prep_bwd", grid=(s // tm,),
        in_specs=[row(IN_PAD), full((8, 128)), full((1, 512)), full((1, 512)), full((512, 768)), full((512, 1024)),
                  row(128), row(128),
                  row(1024), row(128), row(128), cat, cat, row(512), row(512)],
        out_specs=[row(IN_PAD), full((512, 768)), full((512, 1024)), full((8, 128)), full((1, 512)), full((1, 512))],
        out_shape=[jax.ShapeDtypeStruct((s, IN_PAD), BF16), jax.ShapeDtypeStruct((512, 768), F32),
                   jax.ShapeDtypeStruct((512, 1024), F32), jax.ShapeDtypeStruct((8, 128), F32),
                   jax.ShapeDtypeStruct((1, 512), F32), jax.ShapeDtypeStruct((1, 512), F32)],
        sem=("arbitrary",), comm=comm)


SWA_SLOPES = tuple(2.0 ** (-8.0 * h / SWA_Q_HEADS) for h in range(1, SWA_Q_HEADS + 1))
SWA_SCALE = SWA_HEAD_DIM ** -0.5
NT_DIMS = (((1,), (1,)), ((), ()))
TN_DIMS = (((0,), (0,)), ((), ()))


def _swa_span(n, kp_ref, kc_ref, vp_ref, vc_ref, pcol_ref, pprow_ref, pcrow_ref):
    k_span = jnp.concatenate([kp_ref[...], kc_ref[...]], axis=0).astype(F32)
    v_span = jnp.concatenate([vp_ref[...], vc_ref[...]], axis=0).astype(F32)
    lo = _lane((2 * BLOCK, LANES)) < 64
    k_sw = pltpu.roll(k_span, 64, axis=1)
    v_sw = pltpu.roll(v_span, 64, axis=1)
    kk = (jnp.where(lo, k_span, k_sw).astype(BF16), jnp.where(lo, k_sw, k_span).astype(BF16))
    vv_lo = (jnp.where(lo, v_span, 0.0).astype(BF16), jnp.where(lo, v_sw, 0.0).astype(BF16))
    vv_hi = (jnp.where(lo, 0.0, v_sw).astype(BF16), jnp.where(lo, 0.0, v_span).astype(BF16))
    pk = jnp.concatenate([pprow_ref[...], pcrow_ref[...]], axis=1)
    dist = jnp.abs(pcol_ref[...] - pk)
    qi = lax.broadcasted_iota(jnp.int32, (BLOCK, 2 * BLOCK), 0)
    ki = lax.broadcasted_iota(jnp.int32, (BLOCK, 2 * BLOCK), 1)
    first_key = jnp.where(n > 0, qi + 1, jnp.maximum(qi + 1, BLOCK))
    valid = jnp.logical_and(ki >= first_key, ki <= qi + BLOCK)
    mask_add = jnp.where(valid, 0.0, NEG_INF)
    return kk, vv_lo, vv_hi, dist, mask_add


def _swa_heads(q_ref, lo):
    heads = []
    for j in range(SWA_Q_HEADS // 2):
        q_pair = q_ref[:, 128 * j:128 * (j + 1)].astype(F32)
        for par in (0, 1):
            q_h = jnp.where(lo if par == 0 else jnp.logical_not(lo), q_pair, 0.0).astype(BF16)
            heads.append((2 * j + par, (2 * j) // (SWA_Q_HEADS // SWA_KV_HEADS), par, q_h))
    return heads


def _swa_probs(raw, dist, mask_add, slope, sink):
    s = raw * SWA_SCALE - slope * dist + mask_add
    m = jnp.maximum(jnp.max(s, axis=-1, keepdims=True), sink)
    e = jnp.exp(s - m)
    e_sink = jnp.exp(sink - m)
    inv = 1.0 / (jnp.sum(e, axis=-1, keepdims=True) + e_sink)
    return e * inv, e_sink * inv


def _swa_specs():
    blk = lambda w: pl.BlockSpec((BLOCK, w), lambda n: (n, 0))
    prev = lambda w: pl.BlockSpec((BLOCK, w), lambda n: (jnp.maximum(n - 1, 0), 0))
    prow_c = pl.BlockSpec((1, BLOCK), lambda n: (0, n))
    prow_p = pl.BlockSpec((1, BLOCK), lambda n: (0, jnp.maximum(n - 1, 0)))
    smem = pl.BlockSpec(memory_space=pltpu.SMEM)
    return [blk(1024), prev(128), blk(128), prev(128), blk(128), blk(1), prow_p, prow_c, smem], blk


def _swa_fwd(qa, ka, va, pos_col, pos_row, sinks, *, comm=()):
    s = qa.shape[0]
    in_specs, blk = _swa_specs()

    def body(q_ref, kp_ref, kc_ref, vp_ref, vc_ref, pcol_ref, pprow_ref, pcrow_ref, sink_ref, o_ref, yb_ref):
        n = pl.program_id(0)
        kk, vv_lo, vv_hi, dist, mask_add = _swa_span(n, kp_ref, kc_ref, vp_ref, vc_ref, pcol_ref, pprow_ref, pcrow_ref)
        lo = _lane((BLOCK, LANES)) < 64
        heads = _swa_heads(q_ref, lo)
        raws = [lax.dot_general(q_h, kk[kv], NT_DIMS, preferred_element_type=F32) for _, kv, _, q_h in heads]
        probs = [_swa_probs(raw, dist, mask_add, SWA_SLOPES[h], sink_ref[h])[0].astype(BF16)
                 for raw, (h, _, _, _) in zip(raws, heads)]
        for j in range(SWA_Q_HEADS // 2):
            kv = heads[2 * j][1]
            out = (jnp.dot(probs[2 * j], vv_lo[kv], preferred_element_type=F32)
                   + jnp.dot(probs[2 * j + 1], vv_hi[kv], preferred_element_type=F32))
            o_ref[:, 128 * j:128 * (j + 1)] = out
            yb_ref[:, 128 * j:128 * (j + 1)] = out.astype(BF16)

    return _pcall(body, (qa, ka, ka, va, va, pos_col, pos_row, pos_row, sinks), name="swa_fwd", grid=(s // BLOCK,),
                  in_specs=in_specs, out_specs=[blk(1024), blk(1024)],
                  out_shape=[jax.ShapeDtypeStruct((s, 1024), F32), jax.ShapeDtypeStruct((s, D_MODEL), BF16)],
                  sem=("parallel",), comm=comm)


def _swa_bwd(qa, ka, va, pos_col, pos_row, sinks, y_a, d_y, *, comm=()):
    s = qa.shape[0]
    in_specs, blk = _swa_specs()
    whole = pl.BlockSpec((s, 128), lambda n: (0, 0))

    def body(q_ref, kp_ref, kc_ref, vp_ref, vc_ref, pcol_ref, pprow_ref, pcrow_ref, sink_ref, y_ref, dy_ref,
             dq_ref, dk_ref, dv_ref, dsink_ref):
        n = pl.program_id(0)

        @pl.when(n == 0)
        def _():
            dk_ref[...] = jnp.zeros_like(dk_ref)
            dv_ref[...] = jnp.zeros_like(dv_ref)
            dsink_ref[...] = jnp.zeros_like(dsink_ref)

        kk, vv_lo, vv_hi, dist, mask_add = _swa_span(n, kp_ref, kc_ref, vp_ref, vc_ref, pcol_ref, pprow_ref, pcrow_ref)
        lo = _lane((BLOCK, LANES)) < 64
        lo2 = _lane((2 * BLOCK, LANES)) < 64
        lane1 = _lane((1, LANES))
        dsink = jnp.zeros((1, LANES), F32)
        dkk = [jnp.zeros((2 * BLOCK, LANES), F32) for _ in range(SWA_KV_HEADS)]
        dvv = [jnp.zeros((2 * BLOCK, LANES), F32) for _ in range(SWA_KV_HEADS)]
        heads = _swa_heads(q_ref, lo)
        do_b, deltas = [], []
        for j in range(SWA_Q_HEADS // 2):
            do_pair = dy_ref[:, 128 * j:128 * (j + 1)]
            doy = do_pair * y_ref[:, 128 * j:128 * (j + 1)]
            do_b.append(do_pair.astype(BF16))
            deltas.append(jnp.sum(jnp.where(lo, doy, 0.0), axis=-1, keepdims=True))
            deltas.append(jnp.sum(jnp.where(lo, 0.0, doy), axis=-1, keepdims=True))
        raws = [lax.dot_general(q_h, kk[kv], NT_DIMS, preferred_element_type=F32) for _, kv, _, q_h in heads]
        dps = [lax.dot_general(do_b[h // 2], (vv_lo, vv_hi)[par][kv], NT_DIMS, preferred_element_type=F32)
               for h, kv, par, _ in heads]
        p_b, ds_b = [], []
        for h, kv, par, _ in heads:
            p, p_sink = _swa_probs(raws[h], dist, mask_add, SWA_SLOPES[h], sink_ref[h])
            ds = p * (dps[h] - deltas[h])
            dsink = dsink + jnp.where(lane1 == h, -jnp.sum(p_sink * deltas[h], axis=0, keepdims=True), 0.0)
            p_b.append(p.astype(BF16))
            ds_b.append((ds * SWA_SCALE).astype(BF16))
        dq_halves = []
        for h, kv, par, q_h in heads:
            dq_halves.append(jnp.dot(ds_b[h], kk[kv], preferred_element_type=F32))
            dkk[kv] = dkk[kv] + lax.dot_general(ds_b[h], q_h, TN_DIMS, preferred_element_type=F32)
            pv = lax.dot_general(p_b[h], do_b[h // 2], TN_DIMS, preferred_element_type=F32)
            dvv[kv] = dvv[kv] + jnp.where(lo2 if par == 0 else jnp.logical_not(lo2), pv, 0.0)
        for j in range(SWA_Q_HEADS // 2):
            dq_ref[:, 128 * j:128 * (j + 1)] = jnp.where(lo, dq_halves[2 * j], dq_halves[2 * j + 1])
        fold = lambda t: t + pltpu.roll(t, 64, axis=1)
        dk_span = jnp.where(lo2, fold(dkk[0]), fold(dkk[1]))
        dv_span = jnp.where(lo2, fold(dvv[0]), fold(dvv[1]))
        prev0 = pl.multiple_of(jnp.maximum(n - 1, 0) * BLOCK, BLOCK)
        cur0 = pl.multiple_of(n * BLOCK, BLOCK)
        dk_ref[pl.ds(prev0, BLOCK), :] += dk_span[0:BLOCK]
        dk_ref[pl.ds(cur0, BLOCK), :] += dk_span[BLOCK:]
        dv_ref[pl.ds(prev0, BLOCK), :] += dv_span[0:BLOCK]
        dv_ref[pl.ds(cur0, BLOCK), :] += dv_span[BLOCK:]
        dsink_ref[...] += dsink

    return _pcall(
        body, (qa, ka, ka, va, va, pos_col, pos_row, pos_row, sinks, y_a, d_y), name="swa_bwd", grid=(s // BLOCK,),
        in_specs=in_specs + [blk(1024), blk(1024)],
        out_specs=[blk(1024), whole, whole, pl.BlockSpec((1, LANES), lambda n: (0, 0))],
        out_shape=[jax.ShapeDtypeStruct((s, 1024), F32), jax.ShapeDtypeStruct((s, 128), F32),
                   jax.ShapeDtypeStruct((s, 128), F32), jax.ShapeDtypeStruct((1, LANES), F32)],
        sem=("arbitrary",), comm=comm)


MLA_SCALE = (MLA_NOPE + MLA_ROPE) ** -0.5
LOG2_E = math.log2(math.e)
MLA_TILE = 1024


def _tile_pairs(nt, q_major):
    pairs = [(i, j) for i in range(nt) for j in range(i + 1)] if q_major else \
            [(i, j) for j in range(nt) for i in range(j, nt)]
    return jnp.asarray([p[0] for p in pairs], jnp.int32), jnp.asarray([p[1] for p in pairs], jnp.int32)


def _diag_mask(t):
    return lax.broadcasted_iota(jnp.int32, (t, t), 1) <= lax.broadcasted_iota(jnp.int32, (t, t), 0)


def _mla_fwd(q_cat, k_cat, v_b, y_all, *, comm=()):
    nh, s, _ = q_cat.shape
    t = _tile(s, MLA_TILE)
    qi, kj = _tile_pairs(s // t, True)
    ycol = (SWA_Q_HEADS * SWA_HEAD_DIM) // (nh * MLA_V)

    def body(qi_ref, kj_ref, q_ref, k_ref, v_ref, _, o_ref, lse_ref, yb_ref, m_sc, l_sc, acc_sc):
        i, j = qi_ref[pl.program_id(0)], kj_ref[pl.program_id(0)]

        @pl.when(j == 0)
        def _():
            m_sc[...] = jnp.full_like(m_sc, NEG_INF)
            l_sc[...] = jnp.zeros_like(l_sc)
            acc_sc[...] = jnp.zeros_like(acc_sc)

        def update(diagonal):
            scores = [lax.dot_general(q_ref[h], k_ref[h], NT_DIMS, preferred_element_type=F32) for h in range(nh)]
            probs, alphas = [], []
            for h in range(nh):
                raw = scores[h]
                if diagonal:
                    raw = jnp.where(_diag_mask(t), raw, NEG_INF)
                m_old = m_sc[h]
                m_new = jnp.maximum(m_old, jnp.max(raw, axis=-1, keepdims=True))
                alpha = jnp.exp2((m_old - m_new) * (MLA_SCALE * LOG2_E))
                p = jnp.exp2((raw - m_new) * (MLA_SCALE * LOG2_E))
                l_sc[h] = alpha * l_sc[h] + jnp.sum(p, axis=-1, keepdims=True)
                m_sc[h] = m_new
                probs.append(p.astype(BF16))
                alphas.append(alpha)
            for h in range(nh):
                acc_sc[h] = alphas[h] * acc_sc[h] + jnp.dot(probs[h], v_ref[:, MLA_V * h:MLA_V * (h + 1)],
                                                            preferred_element_type=F32)

        @pl.when(j < i)
        def _():
            update(False)

        @pl.when(j == i)
        def _():
            update(True)
            for h in range(nh):
                out = acc_sc[h] * (1.0 / l_sc[h])
                o_ref[:, MLA_V * h:MLA_V * (h + 1)] = out
                yb_ref[:, MLA_V * h:MLA_V * (h + 1)] = out.astype(BF16)
                lse_ref[h] = m_sc[h] * MLA_SCALE + jnp.log(l_sc[h])

    return _pcall(
        body, (q_cat, k_cat, v_b, y_all), name="mla_fwd", grid=(qi.shape[0],), prefetch=(qi, kj),
        in_specs=[pl.BlockSpec((nh, t, 256), lambda p, qi, kj: (0, qi[p], 0)),
                  pl.BlockSpec((nh, t, 256), lambda p, qi, kj: (0, kj[p], 0)),
                  pl.BlockSpec((t, nh * MLA_V), lambda p, qi, kj: (kj[p], 0)), ANY],
        out_specs=[pl.BlockSpec((t, nh * MLA_V), lambda p, qi, kj: (qi[p], 0)),
                   pl.BlockSpec((nh, t, 1), lambda p, qi, kj: (0, qi[p], 0)),
                   pl.BlockSpec((t, nh * MLA_V), lambda p, qi, kj: (qi[p], ycol))],
        out_shape=[jax.ShapeDtypeStruct((s, nh * MLA_V), F32), jax.ShapeDtypeStruct((nh, s, 1), F32),
                   jax.ShapeDtypeStruct(y_all.shape, y_all.dtype)],
        scratch_shapes=[pltpu.VMEM((nh, t, 1), F32), pltpu.VMEM((nh, t, 1), F32), pltpu.VMEM((nh, t, MLA_V), F32)],
        sem=("arbitrary",), comm=comm, io_alias={3: 2})


def _mla_bwd(q_cat, k_cat, v_b, y_b, lse, d_y, *, comm=()):
    nh, s, _ = q_cat.shape
    t = _tile(s, MLA_TILE)
    nt = s // t
    hp = 2
    wv = hp * MLA_V
    col0 = (SWA_Q_HEADS * SWA_HEAD_DIM) // wv
    qi, kj = _tile_pairs(nt, False)

    def body(qi_ref, kj_ref, q_ref, k_ref, v_ref, y_ref, lse_ref, dy_ref, dq_ref, dk_ref, dv_ref, dk_sc, dv_sc):
        step = pl.program_id(1)
        i, j = qi_ref[step], kj_ref[step]

        @pl.when(step == 0)
        def _():
            dq_ref[...] = jnp.zeros_like(dq_ref)

        @pl.when(i == j)
        def _():
            dk_sc[...] = jnp.zeros_like(dk_sc)
            dv_sc[...] = jnp.zeros_like(dv_sc)

        def update(diagonal):
            rows = pl.ds(pl.multiple_of(i * t, t), t)
            cols = [slice(MLA_V * h, MLA_V * (h + 1)) for h in range(hp)]
            do_b = [dy_ref[:, cols[h]].astype(BF16) for h in range(hp)]
            scores = [lax.dot_general(q_ref[h], k_ref[h], NT_DIMS, preferred_element_type=F32) for h in range(hp)]
            dps = [lax.dot_general(do_b[h], v_ref[:, cols[h]], NT_DIMS, preferred_element_type=F32) for h in range(hp)]
            p_b, ds_b = [], []
            for h in range(hp):
                p = jnp.exp(scores[h] * MLA_SCALE - lse_ref[h])
                if diagonal:
                    p = jnp.where(_diag_mask(t), p, 0.0)
                delta = jnp.sum(dy_ref[:, cols[h]] * y_ref[:, cols[h]], axis=-1, keepdims=True)
                p_b.append(p.astype(BF16))
                ds_b.append((p * (dps[h] - delta) * MLA_SCALE).astype(BF16))
            for h in range(hp):
                dv_sc[h] += lax.dot_general(p_b[h], do_b[h], TN_DIMS, preferred_element_type=F32)
                dk_sc[h] += lax.dot_general(ds_b[h], q_ref[h], TN_DIMS, preferred_element_type=F32)
                dq_ref[h, rows, :] += jnp.dot(ds_b[h], k_ref[h], preferred_element_type=F32)

        @pl.when(i > j)
        def _():
            update(False)

        @pl.when(i == j)
        def _():
            update(True)

        @pl.when(i == nt - 1)
        def _():
            dk_ref[...] = dk_sc[...]
            for h in range(hp):
                dv_ref[:, MLA_V * h:MLA_V * (h + 1)] = dv_sc[h]

    return _pcall(
        body, (q_cat, k_cat, v_b, y_b, lse, d_y), name="mla_bwd", grid=(nh // hp, qi.shape[0]), prefetch=(qi, kj),
        in_specs=[pl.BlockSpec((hp, t, 256), lambda g, p, qi, kj: (g, qi[p], 0)),
                  pl.BlockSpec((hp, t, 256), lambda g, p, qi, kj: (g, kj[p], 0)),
                  pl.BlockSpec((t, wv), lambda g, p, qi, kj: (kj[p], g)),
                  pl.BlockSpec((t, wv), lambda g, p, qi, kj: (qi[p], g)),
                  pl.BlockSpec((hp, t, 1), lambda g, p, qi, kj: (g, qi[p], 0)),
                  pl.BlockSpec((t, wv), lambda g, p, qi, kj: (qi[p], col0 + g))],
        out_specs=[pl.BlockSpec((hp, s, 256), lambda g, p, qi, kj: (g, 0, 0)),
                   pl.BlockSpec((hp, t, 256), lambda g, p, qi, kj: (g, kj[p], 0)),
                   pl.BlockSpec((t, wv), lambda g, p, qi, kj: (kj[p], g))],
        out_shape=[jax.ShapeDtypeStruct((nh, s, 256), F32), jax.ShapeDtypeStruct((nh, s, 256), F32),
                   jax.ShapeDtypeStruct((s, nh * MLA_V), F32)],
        scratch_shapes=[pltpu.VMEM((hp, t, 256), F32), pltpu.VMEM((hp, t, MLA_V), F32)],
        sem=("arbitrary", "arbitrary"), comm=comm)


MEM_SCALE = MEM_DIM ** -0.5


def _mem_kv_fwd(mem, g_mem, w_memkv, g_mk):
    m_len = mem.shape[0]

    def body(mem_ref, g_ref, w_ref, gk_ref, mn_ref, kv_ref, kn_ref, v_ref):
        mn, _, _ = _norm_full(mem_ref[...], g_ref[...])
        mn_b = mn.astype(BF16)
        mn_ref[...] = mn_b
        kv = jnp.dot(mn_b, w_ref[...], preferred_element_type=F32)
        kv_ref[...] = kv
        for h in range(MEM_HEADS):
            kn, _, _ = _norm_full(kv[:, 128 * h:128 * (h + 1)], gk_ref[...])
            kn_ref[:, 128 * h:128 * (h + 1)] = kn.astype(BF16)
        v_ref[...] = kv[:, 512:1024].astype(BF16)

    return pl.pallas_call(
        body, name="mem_kv_fwd",
        out_shape=[jax.ShapeDtypeStruct((m_len, D_MODEL), BF16), jax.ShapeDtypeStruct((m_len, 1024), F32),
                   jax.ShapeDtypeStruct((m_len, 512), BF16), jax.ShapeDtypeStruct((m_len, 512), BF16)],
        compiler_params=_params(),
    )(mem, g_mem, w_memkv, g_mk)


def _mem_kv_bwd(mem, g_mem, w_memkv, g_mk, mn_b, kv, d_kn, d_v):
    m_len = mem.shape[0]

    def body(mem_ref, g_ref, w_ref, gk_ref, mn_ref, kv_ref, dkn_ref, dv_ref, dw_ref, dgmem_ref, dgk_ref):
        parts = []
        dgk = jnp.zeros((1, LANES), F32)
        for h in range(MEM_HEADS):
            _, xhat, r = _norm_full(kv_ref[:, 128 * h:128 * (h + 1)], gk_ref[...])
            dx, dgh = _norm_full_bwd(dkn_ref[:, 128 * h:128 * (h + 1)], gk_ref[...], xhat, r)
            parts.append(dx)
            dgk = dgk + dgh
        d_kv = jnp.concatenate(parts + [dv_ref[...]], axis=1).astype(BF16)
        dw_ref[...] = lax.dot_general(mn_ref[...], d_kv, TN_DIMS, preferred_element_type=F32)
        d_mn = lax.dot_general(d_kv, w_ref[...], NT_DIMS, preferred_element_type=F32)
        _, xhat, _ = _norm_full(mem_ref[...], g_ref[...])
        dgmem_ref[...] = jnp.sum(d_mn * xhat, axis=0, keepdims=True)
        dgk_ref[...] = dgk

    return pl.pallas_call(
        body, name="mem_kv_bwd",
        out_shape=[jax.ShapeDtypeStruct((D_MODEL, 1024), F32), jax.ShapeDtypeStruct((1, D_MODEL), F32),
                   jax.ShapeDtypeStruct((1, LANES), F32)],
        compiler_params=_params(),
    )(mem, g_mem, w_memkv, g_mk, mn_b, kv, d_kn, d_v)


def _mem_probs(q_h, k_h):
    sc = lax.dot_general(q_h, k_h, NT_DIMS, preferred_element_type=F32) * MEM_SCALE
    e = jnp.exp(sc - jnp.max(sc, axis=-1, keepdims=True))
    return e * (1.0 / jnp.sum(e, axis=-1, keepdims=True))


def _mem_attn_fwd(qm, km, vm, y_all, *, tm=512):
    s = qm.shape[0]
    tm = _tile(s, tm)
    m_len = km.shape[0]
    ycol = (SWA_Q_HEADS * SWA_HEAD_DIM + MLA_HEADS * MLA_V) // 512

    def body(q_ref, k_ref, v_ref, _, o_ref, yb_ref):
        for h in range(MEM_HEADS):
            sl = slice(128 * h, 128 * (h + 1))
            p = _mem_probs(q_ref[:, sl], k_ref[:, sl])
            out = jnp.dot(p.astype(BF16), v_ref[:, sl], preferred_element_type=F32)
            o_ref[:, sl] = out
            yb_ref[:, sl] = out.astype(BF16)

    kvspec = pl.BlockSpec((m_len, 512), lambda i: (0, 0))
    return _pcall(
        body, (qm, km, vm, y_all), name="mem_attn_fwd", grid=(s // tm,),
        in_specs=[pl.BlockSpec((tm, 512), lambda i: (i, 0)), kvspec, kvspec, ANY],
        out_specs=[pl.BlockSpec((tm, 512), lambda i: (i, 0)), pl.BlockSpec((tm, 512), lambda i: (i, ycol))],
        out_shape=[jax.ShapeDtypeStruct((s, 512), F32), jax.ShapeDtypeStruct(y_all.shape, y_all.dtype)],
        sem=("parallel",), io_alias={3: 1})


def _mem_attn_bwd(qm, km, vm, y_m, d_y, *, tm=512):
    s = qm.shape[0]
    tm = _tile(s, tm)
    m_len = km.shape[0]
    col0 = (SWA_Q_HEADS * SWA_HEAD_DIM + MLA_HEADS * MLA_V) // 512

    def body(q_ref, k_ref, v_ref, y_ref, dy_ref, dq_ref, dk_ref, dv_ref):
        @pl.when(pl.program_id(0) == 0)
        def _():
            dk_ref[...] = jnp.zeros_like(dk_ref)
            dv_ref[...] = jnp.zeros_like(dv_ref)

        for h in range(MEM_HEADS):
            sl = slice(128 * h, 128 * (h + 1))
            q_h, k_h = q_ref[:, sl], k_ref[:, sl]
            do = dy_ref[:, sl]
            do_b = do.astype(BF16)
            p = _mem_probs(q_h, k_h)
            delta = jnp.sum(do * y_ref[:, sl], axis=-1, keepdims=True)
            dv_ref[:, sl] += lax.dot_general(p.astype(BF16), do_b, TN_DIMS, preferred_element_type=F32)
            dp = lax.dot_general(do_b, v_ref[:, sl], NT_DIMS, preferred_element_type=F32)
            ds_b = (p * (dp - delta) * MEM_SCALE).astype(BF16)
            dq_ref[:, sl] = jnp.dot(ds_b, k_h, preferred_element_type=F32)
            dk_ref[:, sl] += lax.dot_general(ds_b, q_h, TN_DIMS, preferred_element_type=F32)

    kvspec = pl.BlockSpec((m_len, 512), lambda i: (0, 0))
    row = pl.BlockSpec((tm, 512), lambda i: (i, 0))
    return pl.pallas_call(
        body, name="mem_attn_bwd", grid=(s // tm,),
        in_specs=[row, kvspec, kvspec, row, pl.BlockSpec((tm, 512), lambda i: (i, col0))],
        out_specs=[row, kvspec, kvspec],
        out_shape=[jax.ShapeDtypeStruct((s, 512), F32), jax.ShapeDtypeStruct((m_len, 512), F32),
                   jax.ShapeDtypeStruct((m_len, 512), F32)],
        compiler_params=_params(("arbitrary",)),
    )(qm, km, vm, y_m, d_y)


def _ffn_gate_up(fn, w_gate, w_up, *, tm=512, comm=()):
    s, d = fn.shape
    nsp, _, tf = w_gate.shape
    f = nsp * tf
    tm = _tile(s, tm)

    def body(x_ref, wg_ref, wu_ref, g_ref, u_ref, a_ref):
        x = x_ref[...]
        gate = jnp.dot(x, wg_ref[...], preferred_element_type=F32)
        up = jnp.dot(x, wu_ref[...], preferred_element_type=F32)
        g_ref[...] = gate.astype(BF16)
        u_ref[...] = up.astype(BF16)
        a_ref[...] = (gate * (1.0 / (1.0 + jnp.exp(-gate))) * up).astype(BF16)

    wspec = pl.BlockSpec((None, d, tf), lambda j, i: (j, 0, 0))
    ospec = pl.BlockSpec((tm, tf), lambda j, i: (i, j))
    osh = jax.ShapeDtypeStruct((s, f), BF16)
    return _pcall(body, (fn, w_gate, w_up), name="ffn_gate_up", grid=(nsp, s // tm),
                  in_specs=[pl.BlockSpec((tm, d), lambda j, i: (i, 0)), wspec, wspec],
                  out_specs=[ospec, ospec, ospec], out_shape=[osh, osh, osh], sem=("parallel", "parallel"), comm=comm)


def _ffn_bwd_act(d_out, w_down, gate, up, *, tm=1024, tf=1408, comm=()):
    s, d = d_out.shape
    f = w_down.shape[0]
    tm, tf = _tile(s, tm), _tile(f, tf)

    sub = tm // 4 if tm % 1024 == 0 else tm

    def body(do_ref, wd_ref, g_ref, u_ref, dg_ref, du_ref):
        groups = [slice(r, r + sub) for r in range(0, tm, sub)]
        parts = [lax.dot_general(do_ref[rows, :].astype(BF16), wd_ref[...], NT_DIMS, preferred_element_type=F32)
                 for rows in groups]
        for rows, d_act in zip(groups, parts):
            gate = g_ref[rows, :].astype(F32)
            sig = 1.0 / (1.0 + jnp.exp(-gate))
            du_ref[rows, :] = (d_act * (gate * sig)).astype(BF16)
            dg_ref[rows, :] = (d_act * u_ref[rows, :].astype(F32) * (sig * (1.0 + gate * (1.0 - sig)))).astype(BF16)

    ospec = pl.BlockSpec((tm, tf), lambda j, i: (i, j))
    osh = jax.ShapeDtypeStruct((s, f), BF16)
    return _pcall(
        body, (d_out, w_down, gate, up), name="ffn_bwd_act", grid=(f // tf, s // tm),
        in_specs=[pl.BlockSpec((tm, d), lambda j, i: (i, 0)), pl.BlockSpec((tf, d), lambda j, i: (j, 0)), ospec, ospec],
        out_specs=[ospec, ospec], out_shape=[osh, osh], sem=("parallel", "parallel"), comm=comm)


def _cols(g4):
    return jnp.concatenate([g4[k] for k in range(N_CHIPS)], axis=1)


def _full_w_in(g4):
    per = IN_WIDTH // N_CHIPS
    kr0 = 2304 - (N_CHIPS - 1) * per
    last = g4[N_CHIPS - 1]
    pad = jnp.zeros((last.shape[0], IN_PAD - IN_WIDTH), last.dtype)
    return jnp.concatenate([g4[0], g4[1], g4[2], last[:, :kr0], last[:, kr0 + 64:], last[:, kr0:kr0 + 64], pad], axis=1)


def _shards_w_in(dwp):
    per = IN_WIDTH // N_CHIPS
    kr0 = 2304 - (N_CHIPS - 1) * per
    last = jnp.concatenate([dwp[:, (N_CHIPS - 1) * per:2304], dwp[:, C_KR:C_KR + 64], dwp[:, 2304:C_KR]], axis=1)
    assert last.shape[1] == per and kr0 == 144
    return jnp.stack([dwp[:, per * k:per * (k + 1)] for k in range(N_CHIPS - 1)] + [last])


def _full_heads(g4, first):
    return jnp.concatenate([g4[k][:, :first] for k in range(N_CHIPS)] + [g4[k][:, first:] for k in range(N_CHIPS)], axis=1)


def _shards_heads(dwp, first, rest):
    base = N_CHIPS * first
    return jnp.stack([jnp.concatenate([dwp[:, first * k:first * (k + 1)], dwp[:, base + rest * k:base + rest * (k + 1)]], axis=1)
                      for k in range(N_CHIPS)])


def _rope_tables(pos):
    inv_freq = ROPE_THETA ** (-jnp.arange(0, MLA_ROPE, 2, dtype=F32) / MLA_ROPE)
    ang = pos.astype(F32)[:, None] * inv_freq
    cos, sin = jnp.cos(ang), jnp.sin(ang)
    return jnp.tile(cos, (1, 4)), jnp.concatenate([-sin, sin, -sin, sin], axis=1)


def _gain_table(sp):
    two = lambda v: jnp.tile(v, (1, 2))
    rows = [two(sp["swa_q_norm_g"]), two(sp["swa_k_norm_g"]), sp["mla_qn_norm_g"], two(sp["mla_qr_norm_g"]),
            sp["mla_kn_norm_g"], two(sp["mla_kr_norm_g"]), sp["mem_q_norm_g"], jnp.zeros((1, LANES), F32)]
    return jnp.concatenate(rows, axis=0)


CHIP_DISTANCES = (1, 2, 3)


def _place():
    x, y, c = lax.axis_index("x"), lax.axis_index("y"), lax.axis_index("c")
    return x, y, c, 2 * x + y


def _chip_at(x, y, d):
    px = 1 - x if d & 2 else x
    py = 1 - y if d & 1 else y
    return px, py, 2 * px + py


def _row_tile(rows, want=512, mult=8):
    t = min(rows, want)
    t -= t % mult
    while rows % t:
        t -= mult
    return t


def _cast_into_slot(w, meta, *, name, comm=()):
    rows, cols = w.shape
    tr = _row_tile(rows, 512, 16)

    def body(meta_ref, w_ref, o_ref):
        o_ref[...] = w_ref[...].astype(BF16)

    return _pcall(body, (w,), name=name, grid=(rows // tr,), prefetch=(meta,),
                  in_specs=[pl.BlockSpec((tr, cols), lambda i, m: (i, 0))],
                  out_specs=pl.BlockSpec((None, tr, cols), lambda i, m: (m[0], i, 0)),
                  out_shape=jax.ShapeDtypeStruct((N_CHIPS, rows, cols), BF16), sem=("parallel",), comm=comm)


def _remote(src, dst, ssem, rsem, i, device):
    return pltpu.make_async_remote_copy(src_ref=src, dst_ref=dst, send_sem=ssem.at[i], recv_sem=rsem.at[i],
                                        device_id=device, device_id_type=MESH)


def _symmetric_stage(ins, out_shapes, aliases, n_sem, copies):
    def issue(i_refs, o_refs, ssem, rsem):
        for send, _ in copies(i_refs, o_refs, ssem, rsem):
            send.start()

    def wait(i_refs, o_refs, ssem, rsem):
        pairs = copies(i_refs, o_refs, ssem, rsem)
        for _, arrival in pairs:
            arrival.wait_recv()
        for send, _ in pairs:
            send.wait_send()

    return _Stage(ins, out_shapes, aliases, n_sem, issue, wait)


def _gather_stage(slots, leg, part=(0, 1)):
    n = len(slots)
    shapes = [jax.ShapeDtypeStruct(s.shape, s.dtype) for s in slots]
    in_place = {w: w for w in range(n)}
    if not isinstance(leg, str):
        legs = list(leg)

        def copies(i_refs, o_refs, ssem, rsem):
            return [pr for k, (which, prt) in enumerate(legs)
                    for pr in _gather_stage(slots, which, prt).leg_copies(which, 3 * n * k)(i_refs, o_refs, ssem, rsem)]

        return _symmetric_stage(slots, shapes, in_place, 3 * n * len(legs), copies)

    def leg_copies(which, base):
        def copies(_, outs, ssem, rsem):
            x, y, c, k_me = _place()
            pairs = []
            for w in range(n):
                half = outs[w].shape[1] // 2
                r0, size = _window(half, part)
                slab = lambda k, cc, w=w, half=half, r0=r0, size=size: outs[w].at[k, pl.ds(cc * half + r0, size)]
                for d in CHIP_DISTANCES:
                    px, py, k_src = _chip_at(x, y, d)
                    i = base + 3 * w + d - 1
                    if which == "ici":
                        pairs.append((_remote(slab(k_me, c), slab(k_me, c), ssem, rsem, i, (px, py, c)),
                                      _remote(slab(k_src, c), slab(k_src, c), ssem, rsem, i, (x, y, c))))
                    else:
                        pairs.append((_remote(slab(k_src, c), slab(k_src, c), ssem, rsem, i, (x, y, 1 - c)),
                                      _remote(slab(k_src, 1 - c), slab(k_src, 1 - c), ssem, rsem, i, (x, y, c))))
            return pairs
        return copies

    if leg != "both":
        st = _symmetric_stage(slots, shapes, in_place, 3 * n, leg_copies(leg, 0))
        st.leg_copies = leg_copies
        return st
    ici = _symmetric_stage(slots, shapes, in_place, 6 * n, leg_copies("ici", 0))
    d2d = _symmetric_stage(slots, shapes, in_place, 6 * n, leg_copies("d2d", 3 * n))

    def mid(*refs):
        ici.wait(*refs)
        d2d.issue(*refs)

    return _Stage(slots, shapes, in_place, 6 * n, ici.issue, d2d.wait, mid)


def _halves_stage(grads):
    n = len(grads)

    def copies(ins, outs, ssem, rsem):
        x, y, c, _ = _place()
        pairs = []
        for w in range(n):
            half = ins[w].shape[1] // 2
            pairs.append((_remote(ins[w].at[:, pl.ds((1 - c) * half, half)], outs[w], ssem, rsem, w, (x, y, 1 - c)),
                          _remote(outs[w], outs[w], ssem, rsem, w, (x, y, c))))
        return pairs

    shapes = [jax.ShapeDtypeStruct((N_CHIPS, g.shape[1] // 2, g.shape[2]), g.dtype) for g in grads]
    return _symmetric_stage(grads, shapes, {}, n, copies)


def _window(rows, part):
    idx, count = part
    size = rows // count
    assert size * count == rows and size % 16 == 0, (rows, part)
    return idx * size, size


def _chips_stage(parts, part=(0, 1), into=None):
    n = len(parts)

    def copies(ins, outs, ssem, rsem):
        x, y, c, _ = _place()
        pairs = []
        for w in range(n):
            r0, size = _window(ins[w].shape[1], part)
            for d in CHIP_DISTANCES:
                px, py, _ = _chip_at(x, y, d)
                i = 3 * w + d - 1
                land = outs[w].at[d - 1, pl.ds(r0, size)]
                pairs.append((_remote(ins[w].at[d - 1, pl.ds(r0, size)], land, ssem, rsem, i, (px, py, c)),
                              _remote(land, land, ssem, rsem, i, (x, y, c))))
        return pairs

    shapes = [jax.ShapeDtypeStruct(p.shape, p.dtype) for p in parts]
    if into is None:
        return _symmetric_stage(parts, shapes, {}, 3 * n, copies)
    return _symmetric_stage(list(parts) + list(into), shapes, {n + w: w for w in range(n)}, 3 * n, copies)


def _swap_stage(totals):
    n = len(totals)

    def copies(ins, outs, ssem, rsem):
        x, y, c, _ = _place()
        return [(_remote(ins[w], outs[w], ssem, rsem, w, (x, y, 1 - c)),
                 _remote(outs[w], outs[w], ssem, rsem, w, (x, y, c))) for w in range(n)]

    shapes = [jax.ShapeDtypeStruct(t.shape, t.dtype) for t in totals]
    return _symmetric_stage(totals, shapes, {}, n, copies)


def _run_stages(stages, *, name):
    n_ins = [len(st.ins) for st in stages]
    n_outs = [len(st.out_shapes) for st in stages]
    tot_in, tot_out = sum(n_ins), sum(n_outs)
    aliases, i0, o0 = {}, 0, 0
    for st, ni, no in zip(stages, n_ins, n_outs):
        aliases.update({i0 + a: o0 + b for a, b in st.aliases.items()})
        i0, o0 = i0 + ni, o0 + no

    def body(*refs):
        sems = refs[tot_in + tot_out:]
        for what in ("issue", "wait"):
            i0, o0 = 0, tot_in
            for k, (st, ni, no) in enumerate(zip(stages, n_ins, n_outs)):
                getattr(st, what)(refs[i0:i0 + ni], refs[o0:o0 + no], sems[2 * k], sems[2 * k + 1])
                i0, o0 = i0 + ni, o0 + no

    sem = pltpu.SemaphoreType.DMA
    res = pl.pallas_call(
        body, name=name, in_specs=[ANY] * tot_in, out_specs=[ANY] * tot_out,
        out_shape=[s for st in stages for s in st.out_shapes], input_output_aliases=aliases,
        scratch_shapes=[sem((st.n_sem,)) for st in stages for _ in range(2)],
    )(*[a for st in stages for a in st.ins])
    outs, o0 = [], 0
    for no in n_outs:
        outs.append(list(res[o0:o0 + no]))
        o0 += no
    return outs


def _add_pair(meta, g4, recv, *, name):
    nsh, rows, cols = g4.shape
    half = rows // 2
    tr = _row_tile(half, 128 if cols > 1024 else 256, 16)
    nt = half // tr

    def body(meta_ref, g0, g1, g2, g3, r0, r1, r2, r3, own_ref, oth_ref):
        own_ref[...] = g0[...] + r0[...]
        for d, (g, r) in enumerate(((g1, r1), (g2, r2), (g3, r3))):
            oth_ref[d] = (g[...] + r[...]).astype(BF16)

    blk = (None, tr, cols)
    gspec = lambda d: pl.BlockSpec(blk, lambda i, m: (jnp.bitwise_xor(m[0], d), m[1] * nt + i, 0))
    rspec = lambda d: pl.BlockSpec(blk, lambda i, m: (jnp.bitwise_xor(m[0], d), i, 0))
    grid_spec = pltpu.PrefetchScalarGridSpec(
        num_scalar_prefetch=1, grid=(nt,),
        in_specs=[gspec(d) for d in range(nsh)] + [rspec(d) for d in range(nsh)],
        out_specs=[pl.BlockSpec((tr, cols), lambda i, m: (i, 0)), pl.BlockSpec((3, tr, cols), lambda i, m: (0, i, 0))])
    return pl.pallas_call(
        body, name=name, grid_spec=grid_spec,
        out_shape=[jax.ShapeDtypeStruct((half, cols), F32), jax.ShapeDtypeStruct((3, half, cols), BF16)],
        compiler_params=_params(("parallel",)),
    )(meta, g4, g4, g4, g4, recv, recv, recv, recv)


def _add_chips(own, recv, *, name):
    half, cols = own.shape
    tr = _row_tile(half, 256, 16)

    def body(p_ref, r_ref, o_ref):
        o_ref[...] = ((p_ref[...] + r_ref[0].astype(F32)) + r_ref[1].astype(F32)) + r_ref[2].astype(F32)

    return pl.pallas_call(
        body, name=name, grid=(half // tr,),
        in_specs=[pl.BlockSpec((tr, cols), lambda i: (i, 0)), pl.BlockSpec((3, tr, cols), lambda i: (0, i, 0))],
        out_specs=pl.BlockSpec((tr, cols), lambda i: (i, 0)),
        out_shape=jax.ShapeDtypeStruct((half, cols), F32),
        compiler_params=_params(("parallel",)),
    )(own, recv)


def _adamw_math(w, g, m, v):
    m = ADAM_B1 * m + (1.0 - ADAM_B1) * g
    v = ADAM_B2 * v + (1.0 - ADAM_B2) * (g * g)
    m_hat = m / (1.0 - ADAM_B1 ** ADAM_STEP)
    v_hat = v / (1.0 - ADAM_B2 ** ADAM_STEP)
    delta = -ADAM_LR * (m_hat / (jnp.sqrt(v_hat) + ADAM_EPS) + ADAM_WD * w)
    return delta, m, v


def _adamw(meta, w, g_mine, g_theirs, m, v, *, name):
    rows, cols = w.shape
    half = rows // 2
    tr = _row_tile(half, 256)
    nt = half // tr

    def body(meta_ref, w_ref, a_ref, b_ref, m_ref, v_ref, g_ref, d_ref, mo_ref, vo_ref):
        is_mine = (pl.program_id(0) // nt) == meta_ref[1]
        g = jnp.where(is_mine, a_ref[...], b_ref[...])
        g_ref[...] = g
        d_ref[...], mo_ref[...], vo_ref[...] = _adamw_math(w_ref[...], g, m_ref[...], v_ref[...])

    blk = pl.BlockSpec((tr, cols), lambda i, mt: (i, 0))
    mine = pl.BlockSpec((tr, cols), lambda i, mt: (jnp.where(i // nt == mt[1], i % nt, 0), 0))
    theirs = pl.BlockSpec((tr, cols), lambda i, mt: (jnp.where(i // nt == mt[1], 0, i % nt), 0))
    sh = jax.ShapeDtypeStruct((rows, cols), F32)
    grid_spec = pltpu.PrefetchScalarGridSpec(
        num_scalar_prefetch=1, grid=(rows // tr,),
        in_specs=[blk, mine, theirs, blk, blk], out_specs=[blk] * 4)
    return pl.pallas_call(
        body, name=name, grid_spec=grid_spec, out_shape=[sh] * 4,
        compiler_params=_params(("arbitrary",)),
    )(meta, w, g_mine, g_theirs, m, v)


N_DEVICES = 8


def _small_step(g_pack, w_pack, m_pack, v_pack):
    rows = g_pack.shape[0]

    def body(g_ref, w_ref, m_ref, v_ref, sum_ref, d_ref, mo_ref, vo_ref, slots, ssem, rsem):
        x, y, c, _ = _place()
        me = 4 * x + 2 * y + c
        slots[me] = g_ref[...]
        copies = []
        for r in range(1, N_DEVICES):
            px = 1 - x if r & 4 else x
            py = 1 - y if r & 2 else y
            pc = 1 - c if r & 1 else c
            copies.append(pltpu.make_async_remote_copy(
                src_ref=g_ref, dst_ref=slots.at[me], send_sem=ssem.at[r - 1], recv_sem=rsem.at[r - 1],
                device_id=(px, py, pc), device_id_type=MESH))
        for cp in copies:
            cp.start()
        for r in range(1, N_DEVICES):
            src = jnp.bitwise_xor(me, r)
            pltpu.make_async_remote_copy(
                src_ref=g_ref, dst_ref=slots.at[src], send_sem=ssem.at[r - 1], recv_sem=rsem.at[r - 1],
                device_id=(x, y, c), device_id_type=MESH).wait_recv()
        for cp in copies:
            cp.wait_send()
        total = slots[0]
        for k in range(1, N_DEVICES):
            total = total + slots[k]
        sum_ref[...] = total
        d_ref[...], mo_ref[...], vo_ref[...] = _adamw_math(w_ref[...], total, m_ref[...], v_ref[...])

    sh = jax.ShapeDtypeStruct((rows, LANES), F32)
    vm = pl.BlockSpec(memory_space=pltpu.VMEM)
    return pl.pallas_call(
        body, name="small_allreduce_adamw",
        in_specs=[vm] * 4, out_specs=[vm] * 4, out_shape=[sh] * 4,
        scratch_shapes=[pltpu.VMEM((N_DEVICES, rows, LANES), F32),
                        pltpu.SemaphoreType.DMA((N_DEVICES - 1,)), pltpu.SemaphoreType.DMA((N_DEVICES - 1,))],
    )(g_pack, w_pack, m_pack, v_pack)


WEIGHTS = ("attn_norm_g", "w_in", "swa_q_norm_g", "swa_k_norm_g", "swa_sinks", "mla_cq_norm_g", "mla_ckv_norm_g",
           "w_uq", "w_ukv", "mla_qn_norm_g", "mla_qr_norm_g", "mla_kn_norm_g", "mla_kr_norm_g", "mem_norm_g",
           "w_mem_kv", "mem_q_norm_g", "mem_k_norm_g", "w_out", "ffn_norm_g", "w_gate", "w_up", "w_down")
BIG = ("w_in", "w_uq", "w_ukv", "w_mem_kv", "w_out", "w_gate", "w_up", "w_down")
SMALL = tuple(n for n in WEIGHTS if n not in BIG)
PACK_UNIT = 8 * LANES


def _pack(parts):
    flat = jnp.concatenate(parts, axis=1)
    total = flat.shape[1]
    padded = -(-total // PACK_UNIT) * PACK_UNIT
    return jnp.pad(flat, ((0, 0), (0, padded - total))).reshape(padded // LANES, LANES)


def _unpack(buf, sizes):
    flat = buf.reshape(1, buf.shape[0] * LANES)
    out, at = [], 0
    for n in sizes:
        out.append(flat[:, at:at + n])
        at += n
    return out


def kernel(x, mem, positions, attn_norm_g, w_in, swa_q_norm_g, swa_k_norm_g, swa_sinks, mla_cq_norm_g, mla_ckv_norm_g, w_uq, w_ukv, mla_qn_norm_g, mla_qr_norm_g, mla_kn_norm_g, mla_kr_norm_g, mem_norm_g, w_mem_kv, mem_q_norm_g, mem_k_norm_g, w_out, ffn_norm_g, w_gate, w_up, w_down, loss_target, m_attn_norm_g, m_w_in, m_swa_q_norm_g, m_swa_k_norm_g, m_swa_sinks, m_mla_cq_norm_g, m_mla_ckv_norm_g, m_w_uq, m_w_ukv, m_mla_qn_norm_g, m_mla_qr_norm_g, m_mla_kn_norm_g, m_mla_kr_norm_g, m_mem_norm_g, m_w_mem_kv, m_mem_q_norm_g, m_mem_k_norm_g, m_w_out, m_ffn_norm_g, m_w_gate, m_w_up, m_w_down, v_attn_norm_g, v_w_in, v_swa_q_norm_g, v_swa_k_norm_g, v_swa_sinks, v_mla_cq_norm_g, v_mla_ckv_norm_g, v_w_uq, v_w_ukv, v_mla_qn_norm_g, v_mla_qr_norm_g, v_mla_kn_norm_g, v_mla_kr_norm_g, v_mem_norm_g, v_w_mem_kv, v_mem_q_norm_g, v_mem_k_norm_g, v_w_out, v_ffn_norm_g, v_w_gate, v_w_up, v_w_down):
    given = dict(locals())
    wts = {n: given[n] for n in WEIGHTS}
    mom_m = {n: given["m_" + n] for n in WEIGHTS}
    mom_v = {n: given["v_" + n] for n in WEIGHTS}

    mx, my, mc = lax.axis_index("x"), lax.axis_index("y"), lax.axis_index("c")
    meta = jnp.stack([2 * mx + my, mc]).astype(jnp.int32)
    x, mem, pos, target = x[0], mem[0], positions[0], loss_target[0]
    sp = {n: wts[n] for n in SMALL}
    s = x.shape[0]
    cos_t, sin_t = _rope_tables(pos)
    pos_f = pos.astype(F32)
    pos_col, pos_row = pos_f.reshape(s, 1), pos_f.reshape(1, s)
    g128 = _gain_table(sp)
    sinks = sp["swa_sinks"].reshape(SWA_Q_HEADS)
    gcq, gckv = sp["mla_cq_norm_g"], sp["mla_ckv_norm_g"]
    gs = {}

    slot = {n: _cast_into_slot(wts[n][0], meta, name="cast_" + n) for n in BIG if n not in ("w_gate", "w_up", "w_down")}
    first = [slot["w_in"], slot["w_uq"], slot["w_ukv"]]
    slot["w_gate"], [first] = _cast_into_slot(wts["w_gate"][0], meta, name="cast_w_gate",
                                              comm=[_gather_stage(first, "ici", (0, 4))])
    slot["w_up"], [first] = _cast_into_slot(wts["w_up"][0], meta, name="cast_w_up",
                                            comm=[_gather_stage(first, [("ici", (1, 4)), ("d2d", (0, 4))])])
    slot["w_down"], [first] = _cast_into_slot(wts["w_down"][0], meta, name="cast_w_down",
                                              comm=[_gather_stage(first, [("ici", (2, 4)), ("d2d", (1, 4))])])
    hn, [first] = _rms_fwd(x, sp["attn_norm_g"], name="attn_norm_fwd",
                           comm=[_gather_stage(first, [("ici", (3, 4)), ("d2d", (2, 4))])])
    [first] = _run_stages([_gather_stage(first, "d2d", (3, 4))], name="gather_first_last_d2d")
    w_in_f, w_uq_f, w_ukv_f = _full_w_in(first[0]), _full_heads(first[1], MLA_NOPE), _full_heads(first[2], MLA_NOPE)

    proj, [mid] = _matmul(hn, w_in_f, name="in_proj",
                          comm=[_gather_stage([slot["w_mem_kv"], slot["w_out"]], "ici")])
    (qa, ka, va, q_cat, k_cat, v_b, qm), [mid, wg] = _attn_prep_fwd(
        proj, g128, gcq, gckv, w_uq_f, w_ukv_f, cos_t, sin_t,
        comm=[_gather_stage(mid, "d2d"), _gather_stage([slot["w_gate"]], "ici", (0, 4))])
    w_mem_kv_f = mid[0].reshape(D_MODEL, 2 * MEM_HEADS * MEM_DIM)
    w_out_f = mid[1].reshape(D_MODEL, D_MODEL)
    mn_b, kv_m, km, vm = _mem_kv_fwd(mem, sp["mem_norm_g"], w_mem_kv_f, sp["mem_k_norm_g"])
    eighths = lambda leg, ks: [(leg, (k, 8)) for k in ks]
    (y_a, y), [wg] = _swa_fwd(qa, ka, va, pos_col, pos_row, sinks, comm=[_gather_stage(wg, eighths("ici", (2, 3, 4, 5)))])
    (y_b, lse, y), [wg, wu] = _mla_fwd(
        q_cat, k_cat, v_b, y, comm=[_gather_stage(wg, eighths("ici", (6, 7))),
                                    _gather_stage([slot["w_up"]], eighths("ici", (0, 1, 2, 3)))])
    y_m, y = _mem_attn_fwd(qm, km, vm, y)
    h1, [wu, wg] = _matmul(y, w_out_f, add=x, name="out_proj",
                           comm=[_gather_stage(wu, eighths("ici", (4, 5, 6, 7)) + eighths("d2d", (0, 1, 2, 3))),
                                 _gather_stage(wg, "d2d")])
    fn, [wu] = _rms_fwd(h1, sp["ffn_norm_g"], name="ffn_norm_fwd",
                        comm=[_gather_stage(wu, eighths("d2d", (4, 5, 6, 7)))])
    w_gate_f, w_up_f = wg[0], wu[0]
    (gate, up, act), [wd] = _ffn_gate_up(fn, w_gate_f, w_up_f, comm=[_gather_stage([slot["w_down"]], "both")])
    w_down_f = wd[0].reshape(D_FF, D_MODEL)
    d_out, d_out_b, loss_tile = _matmul(act, w_down_f, add=h1, name="down_proj", tm=512, tk=D_FF, loss_target=target)

    add_pair = lambda n, g4, r: _add_pair(meta, g4, r, name="grad_add_pair_" + n)
    add_chips = lambda n, own, r: _add_chips(own, r, name="grad_add_chips_" + n)
    mine, theirs = {}, {}

    dw_down = _matmul(act, d_out_b, ta=True, name="dw_down", tm=512, tn=1024, tk=s)
    dw_down = dw_down.reshape(N_CHIPS, D_FF // N_CHIPS, D_MODEL)
    (d_gate, d_up), [[r]] = _ffn_bwd_act(d_out_b, w_down_f, gate, up, comm=[_halves_stage([dw_down])])
    own_d, oth_d = add_pair("w_down", dw_down, r)
    dw_gate, [rd] = _matmul(fn, d_gate, ta=True, name="dw_gate", tm=512, tk=s, tn=D_FF // N_CHIPS, out_split=N_CHIPS,
                            comm=[_chips_stage([oth_d], (0, 2))])
    dw_up, [[r], rd] = _matmul(fn, d_up, ta=True, name="dw_up", tm=512, tk=s, tn=D_FF // N_CHIPS, out_split=N_CHIPS,
                               comm=[_halves_stage([dw_gate]), _chips_stage([oth_d], (1, 2), into=rd)])
    mine["w_down"] = add_chips("w_down", own_d, rd[0])
    own_g, oth_g = add_pair("w_gate", dw_gate, r)
    d_fn, [rg, [theirs["w_down"]]] = _matmul(
        d_gate, w_gate_f, tb=True, b_split=True, name="dfn_gate", tm=512,
        comm=[_chips_stage([oth_g], (0, 2)), _swap_stage([mine["w_down"]])])
    d_fn, [[r], rg] = _matmul(d_up, w_up_f, tb=True, b_split=True, add=d_fn, name="dfn_up", tm=512,
                              comm=[_halves_stage([dw_up]), _chips_stage([oth_g], (1, 2), into=rg)])
    mine["w_gate"] = add_chips("w_gate", own_g, rg[0])
    own_u, oth_u = add_pair("w_up", dw_up, r)
    d_h1, d_h1_b, gs["ffn_norm_g"] = _rms_bwd(d_fn, h1, sp["ffn_norm_g"], d_out, name="ffn_norm_bwd")
    dw_out, [[theirs["w_gate"]]] = _matmul(y, d_h1_b, ta=True, name="dw_out", tm=512, tk=s,
                                           comm=[_swap_stage([mine["w_gate"]])])
    dw_out = dw_out.reshape(N_CHIPS, D_MODEL // N_CHIPS, D_MODEL)
    d_y, [[r]] = _matmul(d_h1_b, w_out_f, tb=True, name="dy", comm=[_halves_stage([dw_out])])
    own_o, oth_o = add_pair("w_out", dw_out, r)
    (d_qa, d_ka, d_va, d_sink), [ru] = _swa_bwd(qa, ka, va, pos_col, pos_row, sinks, y_a, d_y,
                                                comm=[_chips_stage([oth_u], (0, 2))])
    (d_qcat, d_kcat, d_vb), [ru, [r]] = _mla_bwd(
        q_cat, k_cat, v_b, y_b, lse, d_y, comm=[_chips_stage([oth_u], (1, 2), into=ru), _chips_stage([oth_o])])
    mine["w_up"] = add_chips("w_up", own_u, ru[0])
    mine["w_out"] = add_chips("w_out", own_o, r)
    d_qm, d_km, d_vm = _mem_attn_bwd(qm, km, vm, y_m, d_y)
    (d_proj, dw_uq, dw_ukv, dg128, gs["mla_cq_norm_g"], gs["mla_ckv_norm_g"]), [[theirs["w_up"], theirs["w_out"]]] = \
        _attn_prep_bwd(proj, g128, gcq, gckv, w_uq_f, w_ukv_f, cos_t, sin_t, d_qa, d_ka, d_va, d_qcat, d_kcat, d_vb,
                       d_qm, comm=[_swap_stage([mine["w_up"], mine["w_out"]])])
    dw_mem_kv, gs["mem_norm_g"], gs["mem_k_norm_g"] = _mem_kv_bwd(
        mem, sp["mem_norm_g"], w_mem_kv_f, sp["mem_k_norm_g"], mn_b, kv_m, d_km, d_vm)
    late = ("w_uq", "w_ukv", "w_mem_kv")
    late_g = [_shards_heads(dw_uq, MLA_NOPE, MLA_ROPE), _shards_heads(dw_ukv, MLA_NOPE, MLA_V),
              dw_mem_kv.reshape(N_CHIPS, D_MODEL // N_CHIPS, -1)]
    dw_in, [rs] = _matmul(hn, d_proj, ta=True, name="dw_in", tm=512, tk=s, comm=[_halves_stage(late_g)])
    late_sums = [add_pair(n, g4, r) for n, g4, r in zip(late, late_g, rs)]
    dw_in = _shards_w_in(dw_in)
    d_hn, [rs, [r]] = _matmul(d_proj, w_in_f, tb=True, name="dhn", tk=1536,
                              comm=[_chips_stage([oth for _, oth in late_sums]), _halves_stage([dw_in])])
    for n, (own, _), r_n in zip(late, late_sums, rs):
        mine[n] = add_chips(n, own, r_n)
    own_i, oth_i = add_pair("w_in", dw_in, r)
    (grad_x, _, gs["attn_norm_g"]), [[r], late_theirs] = _rms_bwd(
        d_hn, x, sp["attn_norm_g"], d_h1, name="attn_norm_bwd",
        comm=[_chips_stage([oth_i]), _swap_stage([mine[n] for n in late])])
    theirs.update(zip(late, late_theirs))
    mine["w_in"] = add_chips("w_in", own_i, r)
    [[theirs["w_in"]]] = _run_stages([_swap_stage([mine["w_in"]])], name="grad_swap_w_in")

    fold = lambda r: r[:, :64] + r[:, 64:]
    gs["swa_q_norm_g"] = fold(dg128[G_SWA_Q:G_SWA_Q + 1])
    gs["swa_k_norm_g"] = fold(dg128[G_SWA_K:G_SWA_K + 1])
    gs["mla_qn_norm_g"] = dg128[G_QN:G_QN + 1]
    gs["mla_qr_norm_g"] = fold(dg128[G_QR:G_QR + 1])
    gs["mla_kn_norm_g"] = dg128[G_KN:G_KN + 1]
    gs["mla_kr_norm_g"] = fold(dg128[G_KR:G_KR + 1])
    gs["mem_q_norm_g"] = dg128[G_MQ:G_MQ + 1]
    gs["swa_sinks"] = d_sink[:, :SWA_Q_HEADS]

    grad, delta, new_m, new_v = {}, {}, {}, {}
    for n in BIG:
        g2, d, m2, v2 = _adamw(meta, wts[n][0], mine[n], theirs[n], mom_m[n][0], mom_v[n][0], name="adamw_" + n)
        grad[n], delta[n], new_m[n], new_v[n] = g2[None], d[None], m2[None], v2[None]

    sizes = [wts[n].shape[1] for n in SMALL]
    zero = jnp.zeros((1, LANES), F32)
    packs = _small_step(_pack([gs[n] for n in SMALL] + [loss_tile]), _pack([wts[n] for n in SMALL] + [zero]),
                        _pack([mom_m[n] for n in SMALL] + [zero]), _pack([mom_v[n] for n in SMALL] + [zero]))
    for store, buf in zip((grad, delta, new_m, new_v), packs):
        for n, val in zip(SMALL, _unpack(buf, sizes)):
            store[n] = val
    loss = _unpack(packs[0], sizes + [LANES])[-1][0, 0]

    return (loss, grad_x[None], *[grad[n] for n in WEIGHTS], *[delta[n] for n in WEIGHTS],
            *[new_m[n] for n in WEIGHTS], *[new_v[n] for n in WEIGHTS])
```

```python
import functools
import math

import jax
import jax.numpy as jnp
from jax import lax
from jax.experimental import pallas as pl
from jax.experimental.pallas import tpu as pltpu

F32 = jnp.float32
BF16 = jnp.bfloat16

D_MODEL = 2048
BLOCK = 128
EPS = 1e-6
NEG_INF = -1e30
SWA_Q_HEADS = 16
SWA_KV_HEADS = 2
SWA_HEAD_DIM = 64
MLA_HEADS = 4
MLA_RANK = 512
MLA_NOPE = 128
MLA_ROPE = 64
MLA_V = 128
ROPE_THETA = 10000.0
MEM_HEADS = 4
MEM_DIM = 128
D_FF = 5632
IN_WIDTH = 2880
IN_PAD = 3072
N_CHIPS = 4

ADAM_LR = 0.001
ADAM_B1 = 0.9
ADAM_B2 = 0.999
ADAM_EPS = 1e-08
ADAM_WD = 0.01
ADAM_STEP = 10

VMEM_LIMIT_BYTES = 56 * 1024 * 1024
LANES = 128

MESH = pl.DeviceIdType.MESH


def _params(sem=None, **kw):
    return pltpu.CompilerParams(dimension_semantics=sem, vmem_limit_bytes=VMEM_LIMIT_BYTES, **kw)


def _tile(n, want):
    if n <= want:
        return n
    t = want - want % LANES
    while t > 0:
        if n % t == 0:
            return t
        t -= LANES
    return n


ANY = pl.BlockSpec(memory_space=pl.ANY)


class _Stage:
    def __init__(self, ins, out_shapes, aliases, n_sem, issue, wait, mid=None):
        self.ins, self.out_shapes, self.aliases, self.n_sem = list(ins), list(out_shapes), dict(aliases), n_sem
        self.issue, self.wait, self.mid = issue, wait, mid


def _pcall(body, args, *, name, grid, in_specs, out_specs, out_shape, scratch_shapes=(), sem=None, comm=(),
           prefetch=(), io_alias=None):
    multi = isinstance(out_shape, (list, tuple))
    out_specs_l = list(out_specs) if multi else [out_specs]
    out_shape_l = list(out_shape) if multi else [out_shape]
    npf = len(prefetch)
    own_aliases = {npf + a: o for a, o in (io_alias or {}).items()}

    def call(fn, in_specs_, out_specs_, out_shape_, scratch_, operands, sem_, aliases=None):
        kw = dict(name=name, out_shape=out_shape_, compiler_params=_params(sem_))
        if aliases:
            kw["input_output_aliases"] = aliases
        if npf:
            spec = pltpu.PrefetchScalarGridSpec(num_scalar_prefetch=npf, grid=grid, in_specs=in_specs_,
                                                out_specs=out_specs_, scratch_shapes=scratch_)
            return pl.pallas_call(fn, grid_spec=spec, **kw)(*prefetch, *operands)
        return pl.pallas_call(fn, grid=grid, in_specs=in_specs_, out_specs=out_specs_, scratch_shapes=scratch_,
                              **kw)(*operands)

    if not comm:
        return call(body, list(in_specs), out_specs, out_shape, list(scratch_shapes), args, sem, own_aliases)
    n_in, n_out, n_scr = len(in_specs), len(out_specs_l), len(scratch_shapes)
    cins = [a for st in comm for a in st.ins]
    couts = [s for st in comm for s in st.out_shapes]
    aliases, ci, co = dict(own_aliases), 0, 0
    for st in comm:
        for a_i, o_i in st.aliases.items():
            aliases[npf + n_in + ci + a_i] = n_out + co + o_i
        ci, co = ci + len(st.ins), co + len(st.out_shapes)

    def wrapped(*refs):
        pre = refs[:npf]
        p = npf
        ins = refs[p:p + n_in]; p += n_in
        cin_refs = refs[p:p + len(cins)]; p += len(cins)
        outs = refs[p:p + n_out]; p += n_out
        cout_refs = refs[p:p + len(couts)]; p += len(couts)
        scr = refs[p:p + n_scr]; p += n_scr
        sems = refs[p:]
        first = functools.reduce(jnp.logical_and, [pl.program_id(a) == 0 for a in range(len(grid))])
        last = functools.reduce(jnp.logical_and, [pl.program_id(a) == grid[a] - 1 for a in range(len(grid))])

        def each(what):
            i, o = 0, 0
            for k, st in enumerate(comm):
                fn = getattr(st, what)
                if fn is not None:
                    fn(cin_refs[i:i + len(st.ins)], cout_refs[o:o + len(st.out_shapes)], sems[2 * k], sems[2 * k + 1])
                i, o = i + len(st.ins), o + len(st.out_shapes)

        @pl.when(first)
        def _():
            each("issue")

        if any(st.mid is not None for st in comm):
            n_steps = math.prod(grid)
            assert n_steps >= 4, "a two-leg stage needs a carrier with several grid steps"
            lin = functools.reduce(lambda acc, a: acc * grid[a] + pl.program_id(a), range(len(grid)), 0)

            @pl.when(lin == (3 * n_steps) // 4)
            def _():
                each("mid")

        body(*pre, *ins, *outs, *scr)

        @pl.when(last)
        def _():
            each("wait")

    sem_scr = [pltpu.SemaphoreType.DMA((st.n_sem,)) for st in comm for _ in range(2)]
    res = call(wrapped, list(in_specs) + [ANY] * len(cins), out_specs_l + [ANY] * len(couts), out_shape_l + couts,
               list(scratch_shapes) + sem_scr, (*args, *cins), ("arbitrary",) * len(grid), aliases)
    normal = list(res[:n_out])
    stage_outs, o = [], n_out
    for st in comm:
        stage_outs.append(list(res[o:o + len(st.out_shapes)]))
        o += len(st.out_shapes)
    return (normal if multi else normal[0]), stage_outs


def _matmul(a, b, *, name, ta=False, tb=False, add=None, out_dtype=F32, tm=1024, tn=1024, tk=2048,
            b_split=False, out_split=0, comm=(), loss_target=None):
    if ta:
        kdim, m = a.shape
    else:
        m, kdim = a.shape
    if b_split:
        assert tb
        nsp, n, kb = b.shape
        kb = kb * nsp
    elif tb:
        n, kb = b.shape
    else:
        kb, n = b.shape
    assert kb == kdim, (a.shape, b.shape, ta, tb)
    if b_split:
        tk = kdim
    if out_split:
        tn = _tile(n // out_split, tn)
    tm, tn, tk = _tile(m, tm), _tile(n, tn), _tile(kdim, tk)
    nk = kdim // tk
    dims = (((0 if ta else 1,), (1 if tb else 0,)), ((), ()))

    def product(a_ref, b_ref):
        if not b_split:
            return lax.dot_general(a_ref[...].astype(BF16), b_ref[...].astype(BF16), dims, preferred_element_type=F32)
        per = kdim // nsp
        return sum(lax.dot_general(a_ref[:, per * c:per * (c + 1)].astype(BF16), b_ref[c].astype(BF16), dims,
                                   preferred_element_type=F32) for c in range(nsp))

    def body(*refs):
        a_ref, b_ref = refs[:2]
        add_ref = refs[2] if add is not None else None
        n_in = 2 + (add is not None) + (loss_target is not None)
        o_ref = refs[n_in]

        def finish(r):
            if add_ref is not None:
                r = r + add_ref[...].astype(F32)
            if loss_target is None:
                o_ref[...] = r.astype(o_ref.dtype)
                return
            db_ref, l_ref = refs[n_in + 1], refs[n_in + 2]
            err = r - refs[n_in - 1][...]
            d_out = err * (1.0 / n)
            o_ref[...] = d_out
            db_ref[...] = d_out.astype(BF16)
            part = jnp.broadcast_to((0.5 / n) * jnp.sum(jnp.sum(err * err, axis=-1, keepdims=True), axis=0, keepdims=True),
                                    (1, LANES))
            first = jnp.logical_and(pl.program_id(0) == 0, pl.program_id(1) == 0)

            @pl.when(first)
            def _():
                l_ref[...] = part

            @pl.when(jnp.logical_not(first))
            def _():
                l_ref[...] += part

        if nk == 1:
            finish(product(a_ref, b_ref))
            return
        acc_ref = refs[-1]
        k = pl.program_id(2)
        part = product(a_ref, b_ref)

        @pl.when(k == 0)
        def _():
            acc_ref[...] = part

        @pl.when(k > 0)
        def _():
            acc_ref[...] += part

        @pl.when(k == nk - 1)
        def _():
            finish(acc_ref[...])

    a_spec = pl.BlockSpec((tk, tm), lambda i, j, k: (k, i)) if ta else pl.BlockSpec((tm, tk), lambda i, j, k: (i, k))
    if b_split:
        b_spec = pl.BlockSpec((nsp, tn, kdim // nsp), lambda i, j, k: (0, j, 0))
    elif tb:
        b_spec = pl.BlockSpec((tn, tk), lambda i, j, k: (j, k))
    else:
        b_spec = pl.BlockSpec((tk, tn), lambda i, j, k: (k, j))
    in_specs = [a_spec, b_spec]
    args = [a, b]
    if add is not None:
        in_specs.append(pl.BlockSpec((tm, tn), lambda i, j, k: (i, j)))
        args.append(add)
    tile = pl.BlockSpec((tm, tn), lambda i, j, k: (i, j))
    sem = ("parallel", "parallel", "arbitrary")
    if out_split:
        per = (n // out_split) // tn
        out_spec = pl.BlockSpec((None, tm, tn), lambda i, j, k: (j // per, i, j % per))
        out_shape = jax.ShapeDtypeStruct((out_split, m, n // out_split), out_dtype)
    elif loss_target is not None:
        in_specs.append(tile)
        args.append(loss_target)
        out_spec = [tile, tile, pl.BlockSpec((1, LANES), lambda i, j, k: (0, 0))]
        out_shape = [jax.ShapeDtypeStruct((m, n), F32), jax.ShapeDtypeStruct((m, n), BF16),
                     jax.ShapeDtypeStruct((1, LANES), F32)]
        sem = ("arbitrary",) * 3
    else:
        out_spec = tile
        out_shape = jax.ShapeDtypeStruct((m, n), out_dtype)
    return _pcall(body, args, name=name, grid=(m // tm, n // tn, nk), in_specs=in_specs, out_specs=out_spec,
                  out_shape=out_shape, scratch_shapes=[pltpu.VMEM((tm, tn), F32)] if nk > 1 else [],
                  sem=sem, comm=comm)


def _rms_fwd(x, g, *, name, tm=512, comm=()):
    s, d = x.shape
    tm = _tile(s, tm)

    def body(x_ref, g_ref, o_ref):
        xv = x_ref[...]
        r = lax.rsqrt(jnp.mean(xv * xv, axis=-1, keepdims=True) + EPS)
        o_ref[...] = (xv * r * g_ref[...]).astype(o_ref.dtype)

    return _pcall(body, (x, g), name=name, grid=(s // tm,),
                  in_specs=[pl.BlockSpec((tm, d), lambda i: (i, 0)), pl.BlockSpec((1, d), lambda i: (0, 0))],
                  out_specs=pl.BlockSpec((tm, d), lambda i: (i, 0)),
                  out_shape=jax.ShapeDtypeStruct((s, d), BF16), sem=("parallel",), comm=comm)


def _rms_bwd(dy, x, g, res, *, name, tm=512, comm=()):
    s, d = x.shape
    tm = _tile(s, tm)

    def body(dy_ref, x_ref, g_ref, res_ref, dx_ref, dxb_ref, dg_ref):
        xv = x_ref[...]
        dyv = dy_ref[...]
        r = lax.rsqrt(jnp.mean(xv * xv, axis=-1, keepdims=True) + EPS)
        xhat = xv * r
        dyg = dyv * g_ref[...]
        mt = jnp.mean(dyg * xhat, axis=-1, keepdims=True)
        dx = res_ref[...] + r * (dyg - xhat * mt)
        dx_ref[...] = dx
        dxb_ref[...] = dx.astype(BF16)
        part = jnp.sum(dyv * xhat, axis=0, keepdims=True)

        @pl.when(pl.program_id(0) == 0)
        def _():
            dg_ref[...] = part

        @pl.when(pl.program_id(0) > 0)
        def _():
            dg_ref[...] += part

    row = pl.BlockSpec((tm, d), lambda i: (i, 0))
    vec = pl.BlockSpec((1, d), lambda i: (0, 0))
    return _pcall(body, (dy, x, g, res), name=name, grid=(s // tm,), in_specs=[row, row, vec, row],
                  out_specs=[row, row, vec],
                  out_shape=[jax.ShapeDtypeStruct((s, d), F32), jax.ShapeDtypeStruct((s, d), BF16),
                             jax.ShapeDtypeStruct((1, d), F32)],
                  sem=("arbitrary",), comm=comm)


def _lane(shape):
    return lax.broadcasted_iota(jnp.int32, shape, 1)


def _halfsum(t, lo):
    s_lo = jnp.sum(jnp.where(lo, t, 0.0), axis=-1, keepdims=True)
    s_hi = jnp.sum(jnp.where(lo, 0.0, t), axis=-1, keepdims=True)
    return jnp.where(lo, s_lo, s_hi)


def _norm_pair(x, g, lo):
    r = lax.rsqrt(_halfsum(x * x, lo) * (1.0 / 64.0) + EPS)
    xhat = x * r
    return xhat * g, xhat, r


def _norm_pair_bwd(dy, g, xhat, r, lo):
    dyg = dy * g
    mt = _halfsum(dyg * xhat, lo) * (1.0 / 64.0)
    return r * (dyg - xhat * mt), jnp.sum(dy * xhat, axis=0, keepdims=True)


def _norm_full(x, g):
    r = lax.rsqrt(jnp.mean(x * x, axis=-1, keepdims=True) + EPS)
    xhat = x * r
    return xhat * g, xhat, r


def _norm_full_bwd(dy, g, xhat, r):
    dyg = dy * g
    mt = jnp.mean(dyg * xhat, axis=-1, keepdims=True)
    return r * (dyg - xhat * mt), jnp.sum(dy * xhat, axis=0, keepdims=True)


def _rot(x, first32):
    return jnp.where(first32, pltpu.roll(x, 96, axis=1), pltpu.roll(x, 32, axis=1))


def _rope(x, cos_t, sin_t, first32):
    return x * cos_t + _rot(x, first32) * sin_t


def _rope_bwd(dy, cos_t, sin_t, first32):
    return dy * cos_t + _rot(dy * sin_t, first32)


G_SWA_Q, G_SWA_K, G_QN, G_QR, G_KN, G_KR, G_MQ = range(7)

C_QA, C_KA, C_VA, C_CQ, C_CKV, C_QM, C_KR = 0, 1024, 1152, 1280, 1792, 2304, 2816


def _prep_common(p_ref, g128_ref, gcq_ref, gckv_ref, wuq_ref, wukv_ref, cos_ref, sin_ref):
    tm = p_ref.shape[0]
    lane = _lane((tm, LANES))
    lo = lane < 64
    first32 = (lane % 64) < 32
    cos_t = cos_ref[...]
    sin_t = sin_ref[...]
    g = lambda row: g128_ref[row:row + 1, :]
    out = dict(lo=lo, first32=first32, cos_t=cos_t, sin_t=sin_t, lane=lane)
    cq_n, cq_hat, cq_r = _norm_full(p_ref[:, C_CQ:C_CQ + MLA_RANK], gcq_ref[...])
    ckv_n, ckv_hat, ckv_r = _norm_full(p_ref[:, C_CKV:C_CKV + MLA_RANK], gckv_ref[...])
    cq_b = cq_n.astype(BF16)
    ckv_b = ckv_n.astype(BF16)
    q_b = jnp.dot(cq_b, wuq_ref[...], preferred_element_type=F32)
    kv_b = jnp.dot(ckv_b, wukv_ref[...], preferred_element_type=F32)
    out.update(cq_b=cq_b, cq_hat=cq_hat, cq_r=cq_r, ckv_b=ckv_b, ckv_hat=ckv_hat, ckv_r=ckv_r, q_b=q_b, kv_b=kv_b, g=g)
    return out


def _attn_prep_fwd(proj, g128, gcq, gckv, wuq, wukv, cos_t, sin_t, *, tm=512, comm=()):
    s = proj.shape[0]
    tm = _tile(s, tm)

    def body(p_ref, g128_ref, gcq_ref, gckv_ref, wuq_ref, wukv_ref, cos_ref, sin_ref,
             qa_ref, ka_ref, va_ref, qcat_ref, kcat_ref, vb_ref, qm_ref):
        c = _prep_common(p_ref, g128_ref, gcq_ref, gckv_ref, wuq_ref, wukv_ref, cos_ref, sin_ref)
        lo, first32, g = c["lo"], c["first32"], c["g"]
        for j in range(SWA_Q_HEADS // 2):
            y, _, _ = _norm_pair(p_ref[:, C_QA + 128 * j:C_QA + 128 * (j + 1)], g(G_SWA_Q), lo)
            qa_ref[:, 128 * j:128 * (j + 1)] = y.astype(BF16)
        y, _, _ = _norm_pair(p_ref[:, C_KA:C_KA + 128], g(G_SWA_K), lo)
        ka_ref[...] = y.astype(BF16)
        va_ref[...] = p_ref[:, C_VA:C_VA + 128].astype(BF16)
        kr, _, _ = _norm_pair(p_ref[:, C_KR:C_KR + 128], g(G_KR), lo)
        kr = jnp.where(lo, _rope(kr, c["cos_t"], c["sin_t"], first32), 0.0)
        krkr = (kr + pltpu.roll(kr, 64, axis=1)).astype(BF16)
        q_b, kv_b = c["q_b"], c["kv_b"]
        qr = []
        for j in range(MLA_HEADS // 2):
            y, _, _ = _norm_pair(q_b[:, 512 + 128 * j:512 + 128 * (j + 1)], g(G_QR), lo)
            qr.append(_rope(y, c["cos_t"], c["sin_t"], first32))
        for h in range(MLA_HEADS):
            qn, _, _ = _norm_full(q_b[:, 128 * h:128 * (h + 1)], g(G_QN))
            keep = lo if h % 2 == 0 else jnp.logical_not(lo)
            qcat_ref[h, :, 0:128] = qn.astype(BF16)
            qcat_ref[h, :, 128:256] = jnp.where(keep, qr[h // 2], 0.0).astype(BF16)
            kn, _, _ = _norm_full(kv_b[:, 128 * h:128 * (h + 1)], g(G_KN))
            kcat_ref[h, :, 0:128] = kn.astype(BF16)
            kcat_ref[h, :, 128:256] = krkr
        vb_ref[...] = kv_b[:, 512:1024].astype(BF16)
        for h in range(MEM_HEADS):
            y, _, _ = _norm_full(p_ref[:, C_QM + 128 * h:C_QM + 128 * (h + 1)], g(G_MQ))
            qm_ref[:, 128 * h:128 * (h + 1)] = y.astype(BF16)

    row = lambda w: pl.BlockSpec((tm, w), lambda i: (i, 0))
    full = lambda shape: pl.BlockSpec(shape, lambda i: tuple(0 for _ in shape))
    cat = pl.BlockSpec((MLA_HEADS, tm, 256), lambda i: (0, i, 0))
    return _pcall(
        body, (proj, g128, gcq, gckv, wuq, wukv, cos_t, sin_t), name="attn_prep_fwd", grid=(s // tm,),
        in_specs=[row(IN_PAD), full((8, 128)), full((1, 512)), full((1, 512)), full((512, 768)), full((512, 1024)),
                  row(128), row(128)],
        out_specs=[row(1024), row(128), row(128), cat, cat, row(512), row(512)],
        out_shape=[jax.ShapeDtypeStruct((s, 1024), BF16), jax.ShapeDtypeStruct((s, 128), BF16),
                   jax.ShapeDtypeStruct((s, 128), BF16), jax.ShapeDtypeStruct((MLA_HEADS, s, 256), BF16),
                   jax.ShapeDtypeStruct((MLA_HEADS, s, 256), BF16), jax.ShapeDtypeStruct((s, 512), BF16),
                   jax.ShapeDtypeStruct((s, 512), BF16)],
        sem=("parallel",), comm=comm)


def _attn_prep_bwd(proj, g128, gcq, gckv, wuq, wukv, cos_t, sin_t,
                   d_qa, d_ka, d_va, d_qcat, d_kcat, d_vb, d_qm, *, tm=512, comm=()):
    s = proj.shape[0]
    tm = _tile(s, tm)

    def body(p_ref, g128_ref, gcq_ref, gckv_ref, wuq_ref, wukv_ref, cos_ref, sin_ref,
             dqa_ref, dka_ref, dva_ref, dqcat_ref, dkcat_ref, dvb_ref, dqm_ref,
             dp_ref, dwuq_ref, dwukv_ref, dg128_ref, dgcq_ref, dgckv_ref):
        c = _prep_common(p_ref, g128_ref, gcq_ref, gckv_ref, wuq_ref, wukv_ref, cos_ref, sin_ref)
        lo, first32, g = c["lo"], c["first32"], c["g"]
        cos_v, sin_v = c["cos_t"], c["sin_t"]
        q_b, kv_b = c["q_b"], c["kv_b"]
        zero_row = jnp.zeros((1, LANES), F32)
        dg = {k: zero_row for k in range(7)}

        for j in range(SWA_Q_HEADS // 2):
            sl = slice(C_QA + 128 * j, C_QA + 128 * (j + 1))
            _, xhat, r = _norm_pair(p_ref[:, sl], g(G_SWA_Q), lo)
            dx, dgj = _norm_pair_bwd(dqa_ref[:, 128 * j:128 * (j + 1)], g(G_SWA_Q), xhat, r, lo)
            dp_ref[:, sl] = dx.astype(BF16)
            dg[G_SWA_Q] = dg[G_SWA_Q] + dgj
        _, xhat, r = _norm_pair(p_ref[:, C_KA:C_KA + 128], g(G_SWA_K), lo)
        dx, dgj = _norm_pair_bwd(dka_ref[...], g(G_SWA_K), xhat, r, lo)
        dp_ref[:, C_KA:C_KA + 128] = dx.astype(BF16)
        dg[G_SWA_K] = dgj
        dp_ref[:, C_VA:C_VA + 128] = dva_ref[...].astype(BF16)

        dqb_parts = [None] * 6
        for h in range(MLA_HEADS):
            _, xhat, r = _norm_full(q_b[:, 128 * h:128 * (h + 1)], g(G_QN))
            dx, dgj = _norm_full_bwd(dqcat_ref[h, :, 0:128], g(G_QN), xhat, r)
            dqb_parts[h] = dx
            dg[G_QN] = dg[G_QN] + dgj
        for j in range(MLA_HEADS // 2):
            _, xhat, r = _norm_pair(q_b[:, 512 + 128 * j:512 + 128 * (j + 1)], g(G_QR), lo)
            d_rot = jnp.where(lo, dqcat_ref[2 * j, :, 128:256], dqcat_ref[2 * j + 1, :, 128:256])
            d_y = _rope_bwd(d_rot, cos_v, sin_v, first32)
            dx, dgj = _norm_pair_bwd(d_y, g(G_QR), xhat, r, lo)
            dqb_parts[4 + j] = dx
            dg[G_QR] = dg[G_QR] + dgj
        d_qb = jnp.concatenate(dqb_parts, axis=1).astype(BF16)
        dwuq = lax.dot_general(c["cq_b"], d_qb, (((0,), (0,)), ((), ())), preferred_element_type=F32)
        d_cqn = lax.dot_general(d_qb, wuq_ref[...], (((1,), (1,)), ((), ())), preferred_element_type=F32)
        dx, dgcq = _norm_full_bwd(d_cqn, gcq_ref[...], c["cq_hat"], c["cq_r"])
        dp_ref[:, C_CQ:C_CQ + MLA_RANK] = dx.astype(BF16)

        dkv_parts = []
        d_krkr = jnp.zeros((p_ref.shape[0], LANES), F32)
        for h in range(MLA_HEADS):
            _, xhat, r = _norm_full(kv_b[:, 128 * h:128 * (h + 1)], g(G_KN))
            dx, dgj = _norm_full_bwd(dkcat_ref[h, :, 0:128], g(G_KN), xhat, r)
            dkv_parts.append(dx)
            dg[G_KN] = dg[G_KN] + dgj
            d_krkr = d_krkr + dkcat_ref[h, :, 128:256]
        d_kvb = jnp.concatenate(dkv_parts + [dvb_ref[...]], axis=1).astype(BF16)
        dwukv = lax.dot_general(c["ckv_b"], d_kvb, (((0,), (0,)), ((), ())), preferred_element_type=F32)
        d_ckvn = lax.dot_general(d_kvb, wukv_ref[...], (((1,), (1,)), ((), ())), preferred_element_type=F32)
        dx, dgckv = _norm_full_bwd(d_ckvn, gckv_ref[...], c["ckv_hat"], c["ckv_r"])
        dp_ref[:, C_CKV:C_CKV + MLA_RANK] = dx.astype(BF16)

        _, xhat, r = _norm_pair(p_ref[:, C_KR:C_KR + 128], g(G_KR), lo)
        d_kr = jnp.where(lo, d_krkr + pltpu.roll(d_krkr, 64, axis=1), 0.0)
        d_y = jnp.where(lo, _rope_bwd(d_kr, cos_v, sin_v, first32), 0.0)
        dx, dgj = _norm_pair_bwd(d_y, g(G_KR), xhat, r, lo)
        dp_ref[:, C_KR:C_KR + 128] = jnp.where(lo, dx, 0.0).astype(BF16)
        dp_ref[:, C_KR + 128:] = jnp.zeros((p_ref.shape[0], IN_PAD - C_KR - 128), BF16)
        dg[G_KR] = dgj

        for h in range(MEM_HEADS):
            sl = slice(C_QM + 128 * h, C_QM + 128 * (h + 1))
            _, xhat, r = _norm_full(p_ref[:, sl], g(G_MQ))
            dx, dgj = _norm_full_bwd(dqm_ref[:, 128 * h:128 * (h + 1)], g(G_MQ), xhat, r)
            dp_ref[:, sl] = dx.astype(BF16)
            dg[G_MQ] = dg[G_MQ] + dgj

        dg_tile = jnp.concatenate([dg[k] for k in range(7)] + [zero_row], axis=0)

        @pl.when(pl.program_id(0) == 0)
        def _():
            dwuq_ref[...] = dwuq
            dwukv_ref[...] = dwukv
            dg128_ref[...] = dg_tile
            dgcq_ref[...] = dgcq
            dgckv_ref[...] = dgckv

        @pl.when(pl.program_id(0) > 0)
        def _():
            dwuq_ref[...] += dwuq
            dwukv_ref[...] += dwukv
            dg128_ref[...] += dg_tile
            dgcq_ref[...] += dgcq
            dgckv_ref[...] += dgckv

    row = lambda w: pl.BlockSpec((tm, w), lambda i: (i, 0))
    full = lambda shape: pl.BlockSpec(shape, lambda i: tuple(0 for _ in shape))
    cat = pl.BlockSpec((MLA_HEADS, tm, 256), lambda i: (0, i, 0))
    return _pcall(
        body, (proj, g128, gcq, gckv, wuq, wukv, cos_t, sin_t, d_qa, d_ka, d_va, d_qcat, d_kcat, d_vb, d_qm),
        name="attn_prep_bwd", grid=(s // tm,),
        in_specs=[row(IN_PAD), full((8, 128)), full((1, 512)), full((1, 512)), full((512, 768)), full((512, 1024)),
                  row(128), row(128),
                  row(1024), row(128), row(128), cat, cat, row(512), row(512)],
        out_specs=[row(IN_PAD), full((512, 768)), full((512, 1024)), full((8, 128)), full((1, 512)), full((1, 512))],
        out_shape=[jax.ShapeDtypeStruct((s, IN_PAD), BF16), jax.ShapeDtypeStruct((512, 768), F32),
                   jax.ShapeDtypeStruct((512, 1024), F32), jax.ShapeDtypeStruct((8, 128), F32),
                   jax.ShapeDtypeStruct((1, 512), F32), jax.ShapeDtypeStruct((1, 512), F32)],
        sem=("arbitrary",), comm=comm)


SWA_SLOPES = tuple(2.0 ** (-8.0 * h / SWA_Q_HEADS) for h in range(1, SWA_Q_HEADS + 1))
SWA_SCALE = SWA_HEAD_DIM ** -0.5
NT_DIMS = (((1,), (1,)), ((), ()))
TN_DIMS = (((0,), (0,)), ((), ()))


def _swa_span(n, kp_ref, kc_ref, vp_ref, vc_ref, pcol_ref, pprow_ref, pcrow_ref):
    k_span = jnp.concatenate([kp_ref[...], kc_ref[...]], axis=0).astype(F32)
    v_span = jnp.concatenate([vp_ref[...], vc_ref[...]], axis=0).astype(F32)
    lo = _lane((2 * BLOCK, LANES)) < 64
    k_sw = pltpu.roll(k_span, 64, axis=1)
    v_sw = pltpu.roll(v_span, 64, axis=1)
    kk = (jnp.where(lo, k_span, k_sw).astype(BF16), jnp.where(lo, k_sw, k_span).astype(BF16))
    vv_lo = (jnp.where(lo, v_span, 0.0).astype(BF16), jnp.where(lo, v_sw, 0.0).astype(BF16))
    vv_hi = (jnp.where(lo, 0.0, v_sw).astype(BF16), jnp.where(lo, 0.0, v_span).astype(BF16))
    pk = jnp.concatenate([pprow_ref[...], pcrow_ref[...]], axis=1)
    dist = jnp.abs(pcol_ref[...] - pk)
    qi = lax.broadcasted_iota(jnp.int32, (BLOCK, 2 * BLOCK), 0)
    ki = lax.broadcasted_iota(jnp.int32, (BLOCK, 2 * BLOCK), 1)
    first_key = jnp.where(n > 0, qi + 1, jnp.maximum(qi + 1, BLOCK))
    valid = jnp.logical_and(ki >= first_key, ki <= qi + BLOCK)
    mask_add = jnp.where(valid, 0.0, NEG_INF)
    return kk, vv_lo, vv_hi, dist, mask_add


def _swa_heads(q_ref, lo):
    heads = []
    for j in range(SWA_Q_HEADS // 2):
        q_pair = q_ref[:, 128 * j:128 * (j + 1)].astype(F32)
        for par in (0, 1):
            q_h = jnp.where(lo if par == 0 else jnp.logical_not(lo), q_pair, 0.0).astype(BF16)
            heads.append((2 * j + par, (2 * j) // (SWA_Q_HEADS // SWA_KV_HEADS), par, q_h))
    return heads


def _swa_probs(raw, dist, mask_add, slope, sink):
    s = raw * SWA_SCALE - slope * dist + mask_add
    m = jnp.maximum(jnp.max(s, axis=-1, keepdims=True), sink)
    e = jnp.exp(s - m)
    e_sink = jnp.exp(sink - m)
    inv = 1.0 / (jnp.sum(e, axis=-1, keepdims=True) + e_sink)
    return e * inv, e_sink * inv


def _swa_specs():
    blk = lambda w: pl.BlockSpec((BLOCK, w), lambda n: (n, 0))
    prev = lambda w: pl.BlockSpec((BLOCK, w), lambda n: (jnp.maximum(n - 1, 0), 0))
    prow_c = pl.BlockSpec((1, BLOCK), lambda n: (0, n))
    prow_p = pl.BlockSpec((1, BLOCK), lambda n: (0, jnp.maximum(n - 1, 0)))
    smem = pl.BlockSpec(memory_space=pltpu.SMEM)
    return [blk(1024), prev(128), blk(128), prev(128), blk(128), blk(1), prow_p, prow_c, smem], blk


def _swa_fwd(qa, ka, va, pos_col, pos_row, sinks, *, comm=()):
    s = qa.shape[0]
    in_specs, blk = _swa_specs()

    def body(q_ref, kp_ref, kc_ref, vp_ref, vc_ref, pcol_ref, pprow_ref, pcrow_ref, sink_ref, o_ref, yb_ref):
        n = pl.program_id(0)
        kk, vv_lo, vv_hi, dist, mask_add = _swa_span(n, kp_ref, kc_ref, vp_ref, vc_ref, pcol_ref, pprow_ref, pcrow_ref)
        lo = _lane((BLOCK, LANES)) < 64
        heads = _swa_heads(q_ref, lo)
        raws = [lax.dot_general(q_h, kk[kv], NT_DIMS, preferred_element_type=F32) for _, kv, _, q_h in heads]
        probs = [_swa_probs(raw, dist, mask_add, SWA_SLOPES[h], sink_ref[h])[0].astype(BF16)
                 for raw, (h, _, _, _) in zip(raws, heads)]
        for j in range(SWA_Q_HEADS // 2):
            kv = heads[2 * j][1]
            out = (jnp.dot(probs[2 * j], vv_lo[kv], preferred_element_type=F32)
                   + jnp.dot(probs[2 * j + 1], vv_hi[kv], preferred_element_type=F32))
            o_ref[:, 128 * j:128 * (j + 1)] = out
            yb_ref[:, 128 * j:128 * (j + 1)] = out.astype(BF16)

    return _pcall(body, (qa, ka, ka, va, va, pos_col, pos_row, pos_row, sinks), name="swa_fwd", grid=(s // BLOCK,),
                  in_specs=in_specs, out_specs=[blk(1024), blk(1024)],
                  out_shape=[jax.ShapeDtypeStruct((s, 1024), F32), jax.ShapeDtypeStruct((s, D_MODEL), BF16)],
                  sem=("parallel",), comm=comm)


def _swa_bwd(qa, ka, va, pos_col, pos_row, sinks, y_a, d_y, *, comm=()):
    s = qa.shape[0]
    in_specs, blk = _swa_specs()
    whole = pl.BlockSpec((s, 128), lambda n: (0, 0))

    def body(q_ref, kp_ref, kc_ref, vp_ref, vc_ref, pcol_ref, pprow_ref, pcrow_ref, sink_ref, y_ref, dy_ref,
             dq_ref, dk_ref, dv_ref, dsink_ref):
        n = pl.program_id(0)

        @pl.when(n == 0)
        def _():
            dk_ref[...] = jnp.zeros_like(dk_ref)
            dv_ref[...] = jnp.zeros_like(dv_ref)
            dsink_ref[...] = jnp.zeros_like(dsink_ref)

        kk, vv_lo, vv_hi, dist, mask_add = _swa_span(n, kp_ref, kc_ref, vp_ref, vc_ref, pcol_ref, pprow_ref, pcrow_ref)
        lo = _lane((BLOCK, LANES)) < 64
        lo2 = _lane((2 * BLOCK, LANES)) < 64
        lane1 = _lane((1, LANES))
        dsink = jnp.zeros((1, LANES), F32)
        dkk = [jnp.zeros((2 * BLOCK, LANES), F32) for _ in range(SWA_KV_HEADS)]
        dvv = [jnp.zeros((2 * BLOCK, LANES), F32) for _ in range(SWA_KV_HEADS)]
        heads = _swa_heads(q_ref, lo)
        do_b, deltas = [], []
        for j in range(SWA_Q_HEADS // 2):
            do_pair = dy_ref[:, 128 * j:128 * (j + 1)]
            doy = do_pair * y_ref[:, 128 * j:128 * (j + 1)]
            do_b.append(do_pair.astype(BF16))
            deltas.append(jnp.sum(jnp.where(lo, doy, 0.0), axis=-1, keepdims=True))
            deltas.append(jnp.sum(jnp.where(lo, 0.0, doy), axis=-1, keepdims=True))
        raws = [lax.dot_general(q_h, kk[kv], NT_DIMS, preferred_element_type=F32) for _, kv, _, q_h in heads]
        dps = [lax.dot_general(do_b[h // 2], (vv_lo, vv_hi)[par][kv], NT_DIMS, preferred_element_type=F32)
               for h, kv, par, _ in heads]
        p_b, ds_b = [], []
        for h, kv, par, _ in heads:
            p, p_sink = _swa_probs(raws[h], dist, mask_add, SWA_SLOPES[h], sink_ref[h])
            ds = p * (dps[h] - deltas[h])
            dsink = dsink + jnp.where(lane1 == h, -jnp.sum(p_sink * deltas[h], axis=0, keepdims=True), 0.0)
            p_b.append(p.astype(BF16))
            ds_b.append((ds * SWA_SCALE).astype(BF16))
        dq_halves = []
        for h, kv, par, q_h in heads:
            dq_halves.append(jnp.dot(ds_b[h], kk[kv], preferred_element_type=F32))
            dkk[kv] = dkk[kv] + lax.dot_general(ds_b[h], q_h, TN_DIMS, preferred_element_type=F32)
            pv = lax.dot_general(p_b[h], do_b[h // 2], TN_DIMS, preferred_element_type=F32)
            dvv[kv] = dvv[kv] + jnp.where(lo2 if par == 0 else jnp.logical_not(lo2), pv, 0.0)
        for j in range(SWA_Q_HEADS // 2):
            dq_ref[:, 128 * j:128 * (j + 1)] = jnp.where(lo, dq_halves[2 * j], dq_halves[2 * j + 1])
        fold = lambda t: t + pltpu.roll(t, 64, axis=1)
        dk_span = jnp.where(lo2, fold(dkk[0]), fold(dkk[1]))
        dv_span = jnp.where(lo2, fold(dvv[0]), fold(dvv[1]))
        prev0 = pl.multiple_of(jnp.maximum(n - 1, 0) * BLOCK, BLOCK)
        cur0 = pl.multiple_of(n * BLOCK, BLOCK)
        dk_ref[pl.ds(prev0, BLOCK), :] += dk_span[0:BLOCK]
        dk_ref[pl.ds(cur0, BLOCK), :] += dk_span[BLOCK:]
        dv_ref[pl.ds(prev0, BLOCK), :] += dv_span[0:BLOCK]
        dv_ref[pl.ds(cur0, BLOCK), :] += dv_span[BLOCK:]
        dsink_ref[...] += dsink

    return _pcall(
        body, (qa, ka, ka, va, va, pos_col, pos_row, pos_row, sinks, y_a, d_y), name="swa_bwd", grid=(s // BLOCK,),
        in_specs=in_specs + [blk(1024), blk(1024)],
        out_specs=[blk(1024), whole, whole, pl.BlockSpec((1, LANES), lambda n: (0, 0))],
        out_shape=[jax.ShapeDtypeStruct((s, 1024), F32), jax.ShapeDtypeStruct((s, 128), F32),
                   jax.ShapeDtypeStruct((s, 128), F32), jax.ShapeDtypeStruct((1, LANES), F32)],
        sem=("arbitrary",), comm=comm)


MLA_SCALE = (MLA_NOPE + MLA_ROPE) ** -0.5
LOG2_E = math.log2(math.e)
MLA_TILE = 1024


def _tile_pairs(nt, q_major):
    pairs = [(i, j) for i in range(nt) for j in range(i + 1)] if q_major else \
            [(i, j) for j in range(nt) for i in range(j, nt)]
    return jnp.asarray([p[0] for p in pairs], jnp.int32), jnp.asarray([p[1] for p in pairs], jnp.int32)


def _diag_mask(t):
    return lax.broadcasted_iota(jnp.int32, (t, t), 1) <= lax.broadcasted_iota(jnp.int32, (t, t), 0)


def _mla_fwd(q_cat, k_cat, v_b, y_all, *, comm=()):
    nh, s, _ = q_cat.shape
    t = _tile(s, MLA_TILE)
    qi, kj = _tile_pairs(s // t, True)
    ycol = (SWA_Q_HEADS * SWA_HEAD_DIM) // (nh * MLA_V)

    def body(qi_ref, kj_ref, q_ref, k_ref, v_ref, _, o_ref, lse_ref, yb_ref, m_sc, l_sc, acc_sc):
        i, j = qi_ref[pl.program_id(0)], kj_ref[pl.program_id(0)]

        @pl.when(j == 0)
        def _():
            m_sc[...] = jnp.full_like(m_sc, NEG_INF)
            l_sc[...] = jnp.zeros_like(l_sc)
            acc_sc[...] = jnp.zeros_like(acc_sc)

        def update(diagonal):
            scores = [lax.dot_general(q_ref[h], k_ref[h], NT_DIMS, preferred_element_type=F32) for h in range(nh)]
            probs, alphas = [], []
            for h in range(nh):
                raw = scores[h]
                if diagonal:
                    raw = jnp.where(_diag_mask(t), raw, NEG_INF)
                m_old = m_sc[h]
                m_new = jnp.maximum(m_old, jnp.max(raw, axis=-1, keepdims=True))
                alpha = jnp.exp2((m_old - m_new) * (MLA_SCALE * LOG2_E))
                p = jnp.exp2((raw - m_new) * (MLA_SCALE * LOG2_E))
                l_sc[h] = alpha * l_sc[h] + jnp.sum(p, axis=-1, keepdims=True)
                m_sc[h] = m_new
                probs.append(p.astype(BF16))
                alphas.append(alpha)
            for h in range(nh):
                acc_sc[h] = alphas[h] * acc_sc[h] + jnp.dot(probs[h], v_ref[:, MLA_V * h:MLA_V * (h + 1)],
                                                            preferred_element_type=F32)

        @pl.when(j < i)
        def _():
            update(False)

        @pl.when(j == i)
        def _():
            update(True)
            for h in range(nh):
                out = acc_sc[h] * (1.0 / l_sc[h])
                o_ref[:, MLA_V * h:MLA_V * (h + 1)] = out
                yb_ref[:, MLA_V * h:MLA_V * (h + 1)] = out.astype(BF16)
                lse_ref[h] = m_sc[h] * MLA_SCALE + jnp.log(l_sc[h])

    return _pcall(
        body, (q_cat, k_cat, v_b, y_all), name="mla_fwd", grid=(qi.shape[0],), prefetch=(qi, kj),
        in_specs=[pl.BlockSpec((nh, t, 256), lambda p, qi, kj: (0, qi[p], 0)),
                  pl.BlockSpec((nh, t, 256), lambda p, qi, kj: (0, kj[p], 0)),
                  pl.BlockSpec((t, nh * MLA_V), lambda p, qi, kj: (kj[p], 0)), ANY],
        out_specs=[pl.BlockSpec((t, nh * MLA_V), lambda p, qi, kj: (qi[p], 0)),
                   pl.BlockSpec((nh, t, 1), lambda p, qi, kj: (0, qi[p], 0)),
                   pl.BlockSpec((t, nh * MLA_V), lambda p, qi, kj: (qi[p], ycol))],
        out_shape=[jax.ShapeDtypeStruct((s, nh * MLA_V), F32), jax.ShapeDtypeStruct((nh, s, 1), F32),
                   jax.ShapeDtypeStruct(y_all.shape, y_all.dtype)],
        scratch_shapes=[pltpu.VMEM((nh, t, 1), F32), pltpu.VMEM((nh, t, 1), F32), pltpu.VMEM((nh, t, MLA_V), F32)],
        sem=("arbitrary",), comm=comm, io_alias={3: 2})


def _mla_bwd(q_cat, k_cat, v_b, y_b, lse, d_y, *, comm=()):
    nh, s, _ = q_cat.shape
    t = _tile(s, MLA_TILE)
    nt = s // t
    hp = 2
    wv = hp * MLA_V
    col0 = (SWA_Q_HEADS * SWA_HEAD_DIM) // wv
    qi, kj = _tile_pairs(nt, False)

    def body(qi_ref, kj_ref, q_ref, k_ref, v_ref, y_ref, lse_ref, dy_ref, dq_ref, dk_ref, dv_ref, dk_sc, dv_sc):
        step = pl.program_id(1)
        i, j = qi_ref[step], kj_ref[step]

        @pl.when(step == 0)
        def _():
            dq_ref[...] = jnp.zeros_like(dq_ref)

        @pl.when(i == j)
        def _():
            dk_sc[...] = jnp.zeros_like(dk_sc)
            dv_sc[...] = jnp.zeros_like(dv_sc)

        def update(diagonal):
            rows = pl.ds(pl.multiple_of(i * t, t), t)
            cols = [slice(MLA_V * h, MLA_V * (h + 1)) for h in range(hp)]
            do_b = [dy_ref[:, cols[h]].astype(BF16) for h in range(hp)]
            scores = [lax.dot_general(q_ref[h], k_ref[h], NT_DIMS, preferred_element_type=F32) for h in range(hp)]
            dps = [lax.dot_general(do_b[h], v_ref[:, cols[h]], NT_DIMS, preferred_element_type=F32) for h in range(hp)]
            p_b, ds_b = [], []
            for h in range(hp):
                p = jnp.exp(scores[h] * MLA_SCALE - lse_ref[h])
                if diagonal:
                    p = jnp.where(_diag_mask(t), p, 0.0)
                delta = jnp.sum(dy_ref[:, cols[h]] * y_ref[:, cols[h]], axis=-1, keepdims=True)
                p_b.append(p.astype(BF16))
                ds_b.append((p * (dps[h] - delta) * MLA_SCALE).astype(BF16))
            for h in range(hp):
                dv_sc[h] += lax.dot_general(p_b[h], do_b[h], TN_DIMS, preferred_element_type=F32)
                dk_sc[h] += lax.dot_general(ds_b[h], q_ref[h], TN_DIMS, preferred_element_type=F32)
                dq_ref[h, rows, :] += jnp.dot(ds_b[h], k_ref[h], preferred_element_type=F32)

        @pl.when(i > j)
        def _():
            update(False)

        @pl.when(i == j)
        def _():
            update(True)

        @pl.when(i == nt - 1)
        def _():
            dk_ref[...] = dk_sc[...]
            for h in range(hp):
                dv_ref[:, MLA_V * h:MLA_V * (h + 1)] = dv_sc[h]

    return _pcall(
        body, (q_cat, k_cat, v_b, y_b, lse, d_y), name="mla_bwd", grid=(nh // hp, qi.shape[0]), prefetch=(qi, kj),
        in_specs=[pl.BlockSpec((hp, t, 256), lambda g, p, qi, kj: (g, qi[p], 0)),
                  pl.BlockSpec((hp, t, 256), lambda g, p, qi, kj: (g, kj[p], 0)),
                  pl.BlockSpec((t, wv), lambda g, p, qi, kj: (kj[p], g)),
                  pl.BlockSpec((t, wv), lambda g, p, qi, kj: (qi[p], g)),
                  pl.BlockSpec((hp, t, 1), lambda g, p, qi, kj: (g, qi[p], 0)),
                  pl.BlockSpec((t, wv), lambda g, p, qi, kj: (qi[p], col0 + g))],
        out_specs=[pl.BlockSpec((hp, s, 256), lambda g, p, qi, kj: (g, 0, 0)),
                   pl.BlockSpec((hp, t, 256), lambda g, p, qi, kj: (g, kj[p], 0)),
                   pl.BlockSpec((t, wv), lambda g, p, qi, kj: (kj[p], g))],
        out_shape=[jax.ShapeDtypeStruct((nh, s, 256), F32), jax.ShapeDtypeStruct((nh, s, 256), F32),
                   jax.ShapeDtypeStruct((s, nh * MLA_V), F32)],
        scratch_shapes=[pltpu.VMEM((hp, t, 256), F32), pltpu.VMEM((hp, t, MLA_V), F32)],
        sem=("arbitrary", "arbitrary"), comm=comm)


MEM_SCALE = MEM_DIM ** -0.5


def _mem_kv_fwd(mem, g_mem, w_memkv, g_mk):
    m_len = mem.shape[0]

    def body(mem_ref, g_ref, w_ref, gk_ref, mn_ref, kv_ref, kn_ref, v_ref):
        mn, _, _ = _norm_full(mem_ref[...], g_ref[...])
        mn_b = mn.astype(BF16)
        mn_ref[...] = mn_b
        kv = jnp.dot(mn_b, w_ref[...], preferred_element_type=F32)
        kv_ref[...] = kv
        for h in range(MEM_HEADS):
            kn, _, _ = _norm_full(kv[:, 128 * h:128 * (h + 1)], gk_ref[...])
            kn_ref[:, 128 * h:128 * (h + 1)] = kn.astype(BF16)
        v_ref[...] = kv[:, 512:1024].astype(BF16)

    return pl.pallas_call(
        body, name="mem_kv_fwd",
        out_shape=[jax.ShapeDtypeStruct((m_len, D_MODEL), BF16), jax.ShapeDtypeStruct((m_len, 1024), F32),
                   jax.ShapeDtypeStruct((m_len, 512), BF16), jax.ShapeDtypeStruct((m_len, 512), BF16)],
        compiler_params=_params(),
    )(mem, g_mem, w_memkv, g_mk)


def _mem_kv_bwd(mem, g_mem, w_memkv, g_mk, mn_b, kv, d_kn, d_v):
    m_len = mem.shape[0]

    def body(mem_ref, g_ref, w_ref, gk_ref, mn_ref, kv_ref, dkn_ref, dv_ref, dw_ref, dgmem_ref, dgk_ref):
        parts = []
        dgk = jnp.zeros((1, LANES), F32)
        for h in range(MEM_HEADS):
            _, xhat, r = _norm_full(kv_ref[:, 128 * h:128 * (h + 1)], gk_ref[...])
            dx, dgh = _norm_full_bwd(dkn_ref[:, 128 * h:128 * (h + 1)], gk_ref[...], xhat, r)
            parts.append(dx)
            dgk = dgk + dgh
        d_kv = jnp.concatenate(parts + [dv_ref[...]], axis=1).astype(BF16)
        dw_ref[...] = lax.dot_general(mn_ref[...], d_kv, TN_DIMS, preferred_element_type=F32)
        d_mn = lax.dot_general(d_kv, w_ref[...], NT_DIMS, preferred_element_type=F32)
        _, xhat, _ = _norm_full(mem_ref[...], g_ref[...])
        dgmem_ref[...] = jnp.sum(d_mn * xhat, axis=0, keepdims=True)
        dgk_ref[...] = dgk

    return pl.pallas_call(
        body, name="mem_kv_bwd",
        out_shape=[jax.ShapeDtypeStruct((D_MODEL, 1024), F32), jax.ShapeDtypeStruct((1, D_MODEL), F32),
                   jax.ShapeDtypeStruct((1, LANES), F32)],
        compiler_params=_params(),
    )(mem, g_mem, w_memkv, g_mk, mn_b, kv, d_kn, d_v)


def _mem_softmax(raw):
    sc = raw * MEM_SCALE
    e = jnp.exp(sc - jnp.max(sc, axis=-1, keepdims=True))
    return e * (1.0 / jnp.sum(e, axis=-1, keepdims=True))


def _mem_attn_fwd(qm, km, vm, y_all, *, tm=512):
    s = qm.shape[0]
    tm = _tile(s, tm)
    m_len = km.shape[0]
    ycol = (SWA_Q_HEADS * SWA_HEAD_DIM + MLA_HEADS * MLA_V) // 512

    def body(q_ref, k_ref, v_ref, _, o_ref, yb_ref):
        cols = [slice(128 * h, 128 * (h + 1)) for h in range(MEM_HEADS)]
        raws = [lax.dot_general(q_ref[:, sl], k_ref[:, sl], NT_DIMS, preferred_element_type=F32) for sl in cols]
        probs = [_mem_softmax(raw).astype(BF16) for raw in raws]
        for p, sl in zip(probs, cols):
            out = jnp.dot(p, v_ref[:, sl], preferred_element_type=F32)
            o_ref[:, sl] = out
            yb_ref[:, sl] = out.astype(BF16)

    kvspec = pl.BlockSpec((m_len, 512), lambda i: (0, 0))
    return _pcall(
        body, (qm, km, vm, y_all), name="mem_attn_fwd", grid=(s // tm,),
        in_specs=[pl.BlockSpec((tm, 512), lambda i: (i, 0)), kvspec, kvspec, ANY],
        out_specs=[pl.BlockSpec((tm, 512), lambda i: (i, 0)), pl.BlockSpec((tm, 512), lambda i: (i, ycol))],
        out_shape=[jax.ShapeDtypeStruct((s, 512), F32), jax.ShapeDtypeStruct(y_all.shape, y_all.dtype)],
        sem=("parallel",), io_alias={3: 1})


def _mem_attn_bwd(qm, km, vm, y_m, d_y, *, tm=1024):
    s = qm.shape[0]
    tm = _tile(s, tm)
    m_len = km.shape[0]
    col0 = (SWA_Q_HEADS * SWA_HEAD_DIM + MLA_HEADS * MLA_V) // 512

    def body(q_ref, k_ref, v_ref, y_ref, dy_ref, dq_ref, dk_ref, dv_ref):
        @pl.when(pl.program_id(0) == 0)
        def _():
            dk_ref[...] = jnp.zeros_like(dk_ref)
            dv_ref[...] = jnp.zeros_like(dv_ref)

        cols = [slice(128 * h, 128 * (h + 1)) for h in range(MEM_HEADS)]
        do_b = [dy_ref[:, sl].astype(BF16) for sl in cols]
        raws = [lax.dot_general(q_ref[:, sl], k_ref[:, sl], NT_DIMS, preferred_element_type=F32) for sl in cols]
        dps = [lax.dot_general(do_b[h], v_ref[:, sl], NT_DIMS, preferred_element_type=F32) for h, sl in enumerate(cols)]
        p_b, ds_b = [], []
        for h, sl in enumerate(cols):
            p = _mem_softmax(raws[h])
            delta = jnp.sum(dy_ref[:, sl] * y_ref[:, sl], axis=-1, keepdims=True)
            p_b.append(p.astype(BF16))
            ds_b.append((p * (dps[h] - delta) * MEM_SCALE).astype(BF16))
        for h, sl in enumerate(cols):
            dv_ref[:, sl] += lax.dot_general(p_b[h], do_b[h], TN_DIMS, preferred_element_type=F32)
            dq_ref[:, sl] = jnp.dot(ds_b[h], k_ref[:, sl], preferred_element_type=F32)
            dk_ref[:, sl] += lax.dot_general(ds_b[h], q_ref[:, sl], TN_DIMS, preferred_element_type=F32)

    kvspec = pl.BlockSpec((m_len, 512), lambda i: (0, 0))
    row = pl.BlockSpec((tm, 512), lambda i: (i, 0))
    return pl.pallas_call(
        body, name="mem_attn_bwd", grid=(s // tm,),
        in_specs=[row, kvspec, kvspec, row, pl.BlockSpec((tm, 512), lambda i: (i, col0))],
        out_specs=[row, kvspec, kvspec],
        out_shape=[jax.ShapeDtypeStruct((s, 512), F32), jax.ShapeDtypeStruct((m_len, 512), F32),
                   jax.ShapeDtypeStruct((m_len, 512), F32)],
        compiler_params=_params(("arbitrary",)),
    )(qm, km, vm, y_m, d_y)


def _ffn_gate_up(fn, w_gate, w_up, *, tm=512, comm=()):
    s, d = fn.shape
    nsp, _, tf = w_gate.shape
    f = nsp * tf
    tm = _tile(s, tm)

    def body(x_ref, wg_ref, wu_ref, g_ref, u_ref, a_ref):
        x = x_ref[...]
        gate = jnp.dot(x, wg_ref[...], preferred_element_type=F32)
        up = jnp.dot(x, wu_ref[...], preferred_element_type=F32)
        g_ref[...] = gate.astype(BF16)
        u_ref[...] = up.astype(BF16)
        a_ref[...] = (gate * (1.0 / (1.0 + jnp.exp(-gate))) * up).astype(BF16)

    wspec = pl.BlockSpec((None, d, tf), lambda j, i: (j, 0, 0))
    ospec = pl.BlockSpec((tm, tf), lambda j, i: (i, j))
    osh = jax.ShapeDtypeStruct((s, f), BF16)
    return _pcall(body, (fn, w_gate, w_up), name="ffn_gate_up", grid=(nsp, s // tm),
                  in_specs=[pl.BlockSpec((tm, d), lambda j, i: (i, 0)), wspec, wspec],
                  out_specs=[ospec, ospec, ospec], out_shape=[osh, osh, osh], sem=("parallel", "parallel"), comm=comm)


def _ffn_bwd_act(d_out, w_down, gate, up, *, tm=1024, tf=1408, comm=()):
    s, d = d_out.shape
    f = w_down.shape[0]
    tm, tf = _tile(s, tm), _tile(f, tf)

    sub = tm // 4 if tm % 1024 == 0 else tm

    def body(do_ref, wd_ref, g_ref, u_ref, dg_ref, du_ref):
        groups = [slice(r, r + sub) for r in range(0, tm, sub)]
        parts = [lax.dot_general(do_ref[rows, :].astype(BF16), wd_ref[...], NT_DIMS, preferred_element_type=F32)
                 for rows in groups]
        for rows, d_act in zip(groups, parts):
            gate = g_ref[rows, :].astype(F32)
            sig = 1.0 / (1.0 + jnp.exp(-gate))
            du_ref[rows, :] = (d_act * (gate * sig)).astype(BF16)
            dg_ref[rows, :] = (d_act * u_ref[rows, :].astype(F32) * (sig * (1.0 + gate * (1.0 - sig)))).astype(BF16)

    ospec = pl.BlockSpec((tm, tf), lambda j, i: (i, j))
    osh = jax.ShapeDtypeStruct((s, f), BF16)
    return _pcall(
        body, (d_out, w_down, gate, up), name="ffn_bwd_act", grid=(f // tf, s // tm),
        in_specs=[pl.BlockSpec((tm, d), lambda j, i: (i, 0)), pl.BlockSpec((tf, d), lambda j, i: (j, 0)), ospec, ospec],
        out_specs=[ospec, ospec], out_shape=[osh, osh], sem=("parallel", "parallel"), comm=comm)


def _cols(g4):
    return jnp.concatenate([g4[k] for k in range(N_CHIPS)], axis=1)


def _full_w_in(g4):
    per = IN_WIDTH // N_CHIPS
    kr0 = 2304 - (N_CHIPS - 1) * per
    last = g4[N_CHIPS - 1]
    pad = jnp.zeros((last.shape[0], IN_PAD - IN_WIDTH), last.dtype)
    return jnp.concatenate([g4[0], g4[1], g4[2], last[:, :kr0], last[:, kr0 + 64:], last[:, kr0:kr0 + 64], pad], axis=1)


def _shards_w_in(dwp):
    per = IN_WIDTH // N_CHIPS
    kr0 = 2304 - (N_CHIPS - 1) * per
    last = jnp.concatenate([dwp[:, (N_CHIPS - 1) * per:2304], dwp[:, C_KR:C_KR + 64], dwp[:, 2304:C_KR]], axis=1)
    assert last.shape[1] == per and kr0 == 144
    return jnp.stack([dwp[:, per * k:per * (k + 1)] for k in range(N_CHIPS - 1)] + [last])


def _full_heads(g4, first):
    return jnp.concatenate([g4[k][:, :first] for k in range(N_CHIPS)] + [g4[k][:, first:] for k in range(N_CHIPS)], axis=1)


def _shards_heads(dwp, first, rest):
    base = N_CHIPS * first
    return jnp.stack([jnp.concatenate([dwp[:, first * k:first * (k + 1)], dwp[:, base + rest * k:base + rest * (k + 1)]], axis=1)
                      for k in range(N_CHIPS)])


def _rope_tables(pos):
    inv_freq = ROPE_THETA ** (-jnp.arange(0, MLA_ROPE, 2, dtype=F32) / MLA_ROPE)
    ang = pos.astype(F32)[:, None] * inv_freq
    cos, sin = jnp.cos(ang), jnp.sin(ang)
    return jnp.tile(cos, (1, 4)), jnp.concatenate([-sin, sin, -sin, sin], axis=1)


def _gain_table(sp):
    two = lambda v: jnp.tile(v, (1, 2))
    rows = [two(sp["swa_q_norm_g"]), two(sp["swa_k_norm_g"]), sp["mla_qn_norm_g"], two(sp["mla_qr_norm_g"]),
            sp["mla_kn_norm_g"], two(sp["mla_kr_norm_g"]), sp["mem_q_norm_g"], jnp.zeros((1, LANES), F32)]
    return jnp.concatenate(rows, axis=0)


CHIP_DISTANCES = (1, 2, 3)


def _place():
    x, y, c = lax.axis_index("x"), lax.axis_index("y"), lax.axis_index("c")
    return x, y, c, 2 * x + y


def _chip_at(x, y, d):
    px = 1 - x if d & 2 else x
    py = 1 - y if d & 1 else y
    return px, py, 2 * px + py


def _row_tile(rows, want=512, mult=8):
    t = min(rows, want)
    t -= t % mult
    while rows % t:
        t -= mult
    return t


def _cast_into_slot(w, meta, *, name, comm=()):
    rows, cols = w.shape
    tr = _row_tile(rows, 512, 16)

    def body(meta_ref, w_ref, o_ref):
        o_ref[...] = w_ref[...].astype(BF16)

    return _pcall(body, (w,), name=name, grid=(rows // tr,), prefetch=(meta,),
                  in_specs=[pl.BlockSpec((tr, cols), lambda i, m: (i, 0))],
                  out_specs=pl.BlockSpec((None, tr, cols), lambda i, m: (m[0], i, 0)),
                  out_shape=jax.ShapeDtypeStruct((N_CHIPS, rows, cols), BF16), sem=("parallel",), comm=comm)


def _remote(src, dst, ssem, rsem, i, device):
    return pltpu.make_async_remote_copy(src_ref=src, dst_ref=dst, send_sem=ssem.at[i], recv_sem=rsem.at[i],
                                        device_id=device, device_id_type=MESH)


def _symmetric_stage(ins, out_shapes, aliases, n_sem, copies):
    def issue(i_refs, o_refs, ssem, rsem):
        for send, _ in copies(i_refs, o_refs, ssem, rsem):
            send.start()

    def wait(i_refs, o_refs, ssem, rsem):
        pairs = copies(i_refs, o_refs, ssem, rsem)
        for _, arrival in pairs:
            arrival.wait_recv()
        for send, _ in pairs:
            send.wait_send()

    return _Stage(ins, out_shapes, aliases, n_sem, issue, wait)


def _gather_stage(slots, leg, part=(0, 1)):
    n = len(slots)
    shapes = [jax.ShapeDtypeStruct(s.shape, s.dtype) for s in slots]
    in_place = {w: w for w in range(n)}
    if not isinstance(leg, str):
        legs = list(leg)

        def copies(i_refs, o_refs, ssem, rsem):
            return [pr for k, (which, prt) in enumerate(legs)
                    for pr in _gather_stage(slots, which, prt).leg_copies(which, 3 * n * k)(i_refs, o_refs, ssem, rsem)]

        return _symmetric_stage(slots, shapes, in_place, 3 * n * len(legs), copies)

    def leg_copies(which, base):
        def copies(_, outs, ssem, rsem):
            x, y, c, k_me = _place()
            pairs = []
            for w in range(n):
                half = outs[w].shape[1] // 2
                r0, size = _window(half, part)
                slab = lambda k, cc, w=w, half=half, r0=r0, size=size: outs[w].at[k, pl.ds(cc * half + r0, size)]
                for d in CHIP_DISTANCES:
                    px, py, k_src = _chip_at(x, y, d)
                    i = base + 3 * w + d - 1
                    if which == "ici":
                        pairs.append((_remote(slab(k_me, c), slab(k_me, c), ssem, rsem, i, (px, py, c)),
                                      _remote(slab(k_src, c), slab(k_src, c), ssem, rsem, i, (x, y, c))))
                    else:
                        pairs.append((_remote(slab(k_src, c), slab(k_src, c), ssem, rsem, i, (x, y, 1 - c)),
                                      _remote(slab(k_src, 1 - c), slab(k_src, 1 - c), ssem, rsem, i, (x, y, c))))
            return pairs
        return copies

    if leg != "both":
        st = _symmetric_stage(slots, shapes, in_place, 3 * n, leg_copies(leg, 0))
        st.leg_copies = leg_copies
        return st
    ici = _symmetric_stage(slots, shapes, in_place, 6 * n, leg_copies("ici", 0))
    d2d = _symmetric_stage(slots, shapes, in_place, 6 * n, leg_copies("d2d", 3 * n))

    def mid(*refs):
        ici.wait(*refs)
        d2d.issue(*refs)

    return _Stage(slots, shapes, in_place, 6 * n, ici.issue, d2d.wait, mid)


def _halves_stage(grads):
    n = len(grads)

    def copies(ins, outs, ssem, rsem):
        x, y, c, _ = _place()
        pairs = []
        for w in range(n):
            half = ins[w].shape[1] // 2
            pairs.append((_remote(ins[w].at[:, pl.ds((1 - c) * half, half)], outs[w], ssem, rsem, w, (x, y, 1 - c)),
                          _remote(outs[w], outs[w], ssem, rsem, w, (x, y, c))))
        return pairs

    shapes = [jax.ShapeDtypeStruct((N_CHIPS, g.shape[1] // 2, g.shape[2]), g.dtype) for g in grads]
    return _symmetric_stage(grads, shapes, {}, n, copies)


def _window(rows, part):
    idx, count = part
    size = rows // count
    assert size * count == rows and size % 16 == 0, (rows, part)
    return idx * size, size


def _chips_stage(parts, part=(0, 1), into=None):
    n = len(parts)

    def copies(ins, outs, ssem, rsem):
        x, y, c, _ = _place()
        pairs = []
        for w in range(n):
            r0, size = _window(ins[w].shape[1], part)
            for d in CHIP_DISTANCES:
                px, py, _ = _chip_at(x, y, d)
                i = 3 * w + d - 1
                land = outs[w].at[d - 1, pl.ds(r0, size)]
                pairs.append((_remote(ins[w].at[d - 1, pl.ds(r0, size)], land, ssem, rsem, i, (px, py, c)),
                              _remote(land, land, ssem, rsem, i, (x, y, c))))
        return pairs

    shapes = [jax.ShapeDtypeStruct(p.shape, p.dtype) for p in parts]
    if into is None:
        return _symmetric_stage(parts, shapes, {}, 3 * n, copies)
    return _symmetric_stage(list(parts) + list(into), shapes, {n + w: w for w in range(n)}, 3 * n, copies)


def _swap_stage(totals):
    n = len(totals)

    def copies(ins, outs, ssem, rsem):
        x, y, c, _ = _place()
        return [(_remote(ins[w], outs[w], ssem, rsem, w, (x, y, 1 - c)),
                 _remote(outs[w], outs[w], ssem, rsem, w, (x, y, c))) for w in range(n)]

    shapes = [jax.ShapeDtypeStruct(t.shape, t.dtype) for t in totals]
    return _symmetric_stage(totals, shapes, {}, n, copies)


def _run_stages(stages, *, name):
    n_ins = [len(st.ins) for st in stages]
    n_outs = [len(st.out_shapes) for st in stages]
    tot_in, tot_out = sum(n_ins), sum(n_outs)
    aliases, i0, o0 = {}, 0, 0
    for st, ni, no in zip(stages, n_ins, n_outs):
        aliases.update({i0 + a: o0 + b for a, b in st.aliases.items()})
        i0, o0 = i0 + ni, o0 + no

    def body(*refs):
        sems = refs[tot_in + tot_out:]
        for what in ("issue", "wait"):
            i0, o0 = 0, tot_in
            for k, (st, ni, no) in enumerate(zip(stages, n_ins, n_outs)):
                getattr(st, what)(refs[i0:i0 + ni], refs[o0:o0 + no], sems[2 * k], sems[2 * k + 1])
                i0, o0 = i0 + ni, o0 + no

    sem = pltpu.SemaphoreType.DMA
    res = pl.pallas_call(
        body, name=name, in_specs=[ANY] * tot_in, out_specs=[ANY] * tot_out,
        out_shape=[s for st in stages for s in st.out_shapes], input_output_aliases=aliases,
        scratch_shapes=[sem((st.n_sem,)) for st in stages for _ in range(2)],
    )(*[a for st in stages for a in st.ins])
    outs, o0 = [], 0
    for no in n_outs:
        outs.append(list(res[o0:o0 + no]))
        o0 += no
    return outs


def _add_pair(meta, g4, recv, *, name):
    nsh, rows, cols = g4.shape
    half = rows // 2
    tr = _row_tile(half, 128 if cols > 1024 else 256, 16)
    nt = half // tr

    def body(meta_ref, g0, g1, g2, g3, r0, r1, r2, r3, own_ref, oth_ref):
        own_ref[...] = g0[...] + r0[...]
        for d, (g, r) in enumerate(((g1, r1), (g2, r2), (g3, r3))):
            oth_ref[d] = (g[...] + r[...]).astype(BF16)

    blk = (None, tr, cols)
    gspec = lambda d: pl.BlockSpec(blk, lambda i, m: (jnp.bitwise_xor(m[0], d), m[1] * nt + i, 0))
    rspec = lambda d: pl.BlockSpec(blk, lambda i, m: (jnp.bitwise_xor(m[0], d), i, 0))
    grid_spec = pltpu.PrefetchScalarGridSpec(
        num_scalar_prefetch=1, grid=(nt,),
        in_specs=[gspec(d) for d in range(nsh)] + [rspec(d) for d in range(nsh)],
        out_specs=[pl.BlockSpec((tr, cols), lambda i, m: (i, 0)), pl.BlockSpec((3, tr, cols), lambda i, m: (0, i, 0))])
    return pl.pallas_call(
        body, name=name, grid_spec=grid_spec,
        out_shape=[jax.ShapeDtypeStruct((half, cols), F32), jax.ShapeDtypeStruct((3, half, cols), BF16)],
        compiler_params=_params(("parallel",)),
    )(meta, g4, g4, g4, g4, recv, recv, recv, recv)


def _add_chips(own, recv, *, name):
    half, cols = own.shape
    tr = _row_tile(half, 256, 16)

    def body(p_ref, r_ref, o_ref):
        o_ref[...] = ((p_ref[...] + r_ref[0].astype(F32)) + r_ref[1].astype(F32)) + r_ref[2].astype(F32)

    return pl.pallas_call(
        body, name=name, grid=(half // tr,),
        in_specs=[pl.BlockSpec((tr, cols), lambda i: (i, 0)), pl.BlockSpec((3, tr, cols), lambda i: (0, i, 0))],
        out_specs=pl.BlockSpec((tr, cols), lambda i: (i, 0)),
        out_shape=jax.ShapeDtypeStruct((half, cols), F32),
        compiler_params=_params(("parallel",)),
    )(own, recv)


def _adamw_math(w, g, m, v):
    m = ADAM_B1 * m + (1.0 - ADAM_B1) * g
    v = ADAM_B2 * v + (1.0 - ADAM_B2) * (g * g)
    m_hat = m / (1.0 - ADAM_B1 ** ADAM_STEP)
    v_hat = v / (1.0 - ADAM_B2 ** ADAM_STEP)
    delta = -ADAM_LR * (m_hat / (jnp.sqrt(v_hat) + ADAM_EPS) + ADAM_WD * w)
    return delta, m, v


def _adamw(meta, w, g_mine, g_theirs, m, v, *, name):
    rows, cols = w.shape
    half = rows // 2
    tr = _row_tile(half, 256)
    nt = half // tr

    def body(meta_ref, w_ref, a_ref, b_ref, m_ref, v_ref, g_ref, d_ref, mo_ref, vo_ref):
        is_mine = (pl.program_id(0) // nt) == meta_ref[1]
        g = jnp.where(is_mine, a_ref[...], b_ref[...])
        g_ref[...] = g
        d_ref[...], mo_ref[...], vo_ref[...] = _adamw_math(w_ref[...], g, m_ref[...], v_ref[...])

    blk = pl.BlockSpec((tr, cols), lambda i, mt: (i, 0))
    mine = pl.BlockSpec((tr, cols), lambda i, mt: (jnp.where(i // nt == mt[1], i % nt, 0), 0))
    theirs = pl.BlockSpec((tr, cols), lambda i, mt: (jnp.where(i // nt == mt[1], 0, i % nt), 0))
    sh = jax.ShapeDtypeStruct((rows, cols), F32)
    grid_spec = pltpu.PrefetchScalarGridSpec(
        num_scalar_prefetch=1, grid=(rows // tr,),
        in_specs=[blk, mine, theirs, blk, blk], out_specs=[blk] * 4)
    return pl.pallas_call(
        body, name=name, grid_spec=grid_spec, out_shape=[sh] * 4,
        compiler_params=_params(("arbitrary",)),
    )(meta, w, g_mine, g_theirs, m, v)


N_DEVICES = 8


def _small_step(g_pack, w_pack, m_pack, v_pack):
    rows = g_pack.shape[0]

    def body(g_ref, w_ref, m_ref, v_ref, sum_ref, d_ref, mo_ref, vo_ref, slots, ssem, rsem):
        x, y, c, _ = _place()
        me = 4 * x + 2 * y + c
        slots[me] = g_ref[...]
        copies = []
        for r in range(1, N_DEVICES):
            px = 1 - x if r & 4 else x
            py = 1 - y if r & 2 else y
            pc = 1 - c if r & 1 else c
            copies.append(pltpu.make_async_remote_copy(
                src_ref=g_ref, dst_ref=slots.at[me], send_sem=ssem.at[r - 1], recv_sem=rsem.at[r - 1],
                device_id=(px, py, pc), device_id_type=MESH))
        for cp in copies:
            cp.start()
        for r in range(1, N_DEVICES):
            src = jnp.bitwise_xor(me, r)
            pltpu.make_async_remote_copy(
                src_ref=g_ref, dst_ref=slots.at[src], send_sem=ssem.at[r - 1], recv_sem=rsem.at[r - 1],
                device_id=(x, y, c), device_id_type=MESH).wait_recv()
        for cp in copies:
            cp.wait_send()
        total = slots[0]
        for k in range(1, N_DEVICES):
            total = total + slots[k]
        sum_ref[...] = total
        d_ref[...], mo_ref[...], vo_ref[...] = _adamw_math(w_ref[...], total, m_ref[...], v_ref[...])

    sh = jax.ShapeDtypeStruct((rows, LANES), F32)
    vm = pl.BlockSpec(memory_space=pltpu.VMEM)
    return pl.pallas_call(
        body, name="small_allreduce_adamw",
        in_specs=[vm] * 4, out_specs=[vm] * 4, out_shape=[sh] * 4,
        scratch_shapes=[pltpu.VMEM((N_DEVICES, rows, LANES), F32),
                        pltpu.SemaphoreType.DMA((N_DEVICES - 1,)), pltpu.SemaphoreType.DMA((N_DEVICES - 1,))],
    )(g_pack, w_pack, m_pack, v_pack)


WEIGHTS = ("attn_norm_g", "w_in", "swa_q_norm_g", "swa_k_norm_g", "swa_sinks", "mla_cq_norm_g", "mla_ckv_norm_g",
           "w_uq", "w_ukv", "mla_qn_norm_g", "mla_qr_norm_g", "mla_kn_norm_g", "mla_kr_norm_g", "mem_norm_g",
           "w_mem_kv", "mem_q_norm_g", "mem_k_norm_g", "w_out", "ffn_norm_g", "w_gate", "w_up", "w_down")
BIG = ("w_in", "w_uq", "w_ukv", "w_mem_kv", "w_out", "w_gate", "w_up", "w_down")
SMALL = tuple(n for n in WEIGHTS if n not in BIG)
PACK_UNIT = 8 * LANES


def _pack(parts):
    flat = jnp.concatenate(parts, axis=1)
    total = flat.shape[1]
    padded = -(-total // PACK_UNIT) * PACK_UNIT
    return jnp.pad(flat, ((0, 0), (0, padded - total))).reshape(padded // LANES, LANES)


def _unpack(buf, sizes):
    flat = buf.reshape(1, buf.shape[0] * LANES)
    out, at = [], 0
    for n in sizes:
        out.append(flat[:, at:at + n])
        at += n
    return out


def kernel(x, mem, positions, attn_norm_g, w_in, swa_q_norm_g, swa_k_norm_g, swa_sinks, mla_cq_norm_g, mla_ckv_norm_g, w_uq, w_ukv, mla_qn_norm_g, mla_qr_norm_g, mla_kn_norm_g, mla_kr_norm_g, mem_norm_g, w_mem_kv, mem_q_norm_g, mem_k_norm_g, w_out, ffn_norm_g, w_gate, w_up, w_down, loss_target, m_attn_norm_g, m_w_in, m_swa_q_norm_g, m_swa_k_norm_g, m_swa_sinks, m_mla_cq_norm_g, m_mla_ckv_norm_g, m_w_uq, m_w_ukv, m_mla_qn_norm_g, m_mla_qr_norm_g, m_mla_kn_norm_g, m_mla_kr_norm_g, m_mem_norm_g, m_w_mem_kv, m_mem_q_norm_g, m_mem_k_norm_g, m_w_out, m_ffn_norm_g, m_w_gate, m_w_up, m_w_down, v_attn_norm_g, v_w_in, v_swa_q_norm_g, v_swa_k_norm_g, v_swa_sinks, v_mla_cq_norm_g, v_mla_ckv_norm_g, v_w_uq, v_w_ukv, v_mla_qn_norm_g, v_mla_qr_norm_g, v_mla_kn_norm_g, v_mla_kr_norm_g, v_mem_norm_g, v_w_mem_kv, v_mem_q_norm_g, v_mem_k_norm_g, v_w_out, v_ffn_norm_g, v_w_gate, v_w_up, v_w_down):
    given = dict(locals())
    wts = {n: given[n] for n in WEIGHTS}
    mom_m = {n: given["m_" + n] for n in WEIGHTS}
    mom_v = {n: given["v_" + n] for n in WEIGHTS}

    mx, my, mc = lax.axis_index("x"), lax.axis_index("y"), lax.axis_index("c")
    meta = jnp.stack([2 * mx + my, mc]).astype(jnp.int32)
    x, mem, pos, target = x[0], mem[0], positions[0], loss_target[0]
    sp = {n: wts[n] for n in SMALL}
    s = x.shape[0]
    cos_t, sin_t = _rope_tables(pos)
    pos_f = pos.astype(F32)
    pos_col, pos_row = pos_f.reshape(s, 1), pos_f.reshape(1, s)
    g128 = _gain_table(sp)
    sinks = sp["swa_sinks"].reshape(SWA_Q_HEADS)
    gcq, gckv = sp["mla_cq_norm_g"], sp["mla_ckv_norm_g"]
    gs = {}

    slot = {n: _cast_into_slot(wts[n][0], meta, name="cast_" + n) for n in BIG if n not in ("w_gate", "w_up", "w_down")}
    first = [slot["w_in"], slot["w_uq"], slot["w_ukv"]]
    slot["w_gate"], [first] = _cast_into_slot(wts["w_gate"][0], meta, name="cast_w_gate",
                                              comm=[_gather_stage(first, "ici", (0, 4))])
    slot["w_up"], [first] = _cast_into_slot(wts["w_up"][0], meta, name="cast_w_up",
                                            comm=[_gather_stage(first, [("ici", (1, 4)), ("d2d", (0, 4))])])
    slot["w_down"], [first] = _cast_into_slot(wts["w_down"][0], meta, name="cast_w_down",
                                              comm=[_gather_stage(first, [("ici", (2, 4)), ("d2d", (1, 4))])])
    hn, [first] = _rms_fwd(x, sp["attn_norm_g"], name="attn_norm_fwd",
                           comm=[_gather_stage(first, [("ici", (3, 4)), ("d2d", (2, 4))])])
    [first] = _run_stages([_gather_stage(first, "d2d", (3, 4))], name="gather_first_last_d2d")
    w_in_f, w_uq_f, w_ukv_f = _full_w_in(first[0]), _full_heads(first[1], MLA_NOPE), _full_heads(first[2], MLA_NOPE)

    proj, [mid] = _matmul(hn, w_in_f, name="in_proj",
                          comm=[_gather_stage([slot["w_mem_kv"], slot["w_out"]], "ici")])
    (qa, ka, va, q_cat, k_cat, v_b, qm), [mid, wg] = _attn_prep_fwd(
        proj, g128, gcq, gckv, w_uq_f, w_ukv_f, cos_t, sin_t,
        comm=[_gather_stage(mid, "d2d"), _gather_stage([slot["w_gate"]], "ici", (0, 4))])
    w_mem_kv_f = mid[0].reshape(D_MODEL, 2 * MEM_HEADS * MEM_DIM)
    w_out_f = mid[1].reshape(D_MODEL, D_MODEL)
    mn_b, kv_m, km, vm = _mem_kv_fwd(mem, sp["mem_norm_g"], w_mem_kv_f, sp["mem_k_norm_g"])
    eighths = lambda leg, ks: [(leg, (k, 8)) for k in ks]
    (y_a, y), [wg] = _swa_fwd(qa, ka, va, pos_col, pos_row, sinks, comm=[_gather_stage(wg, eighths("ici", (2, 3, 4, 5)))])
    (y_b, lse, y), [wg, wu] = _mla_fwd(
        q_cat, k_cat, v_b, y, comm=[_gather_stage(wg, eighths("ici", (6, 7))),
                                    _gather_stage([slot["w_up"]], eighths("ici", (0, 1, 2, 3)))])
    y_m, y = _mem_attn_fwd(qm, km, vm, y)
    h1, [wu, wg] = _matmul(y, w_out_f, add=x, name="out_proj",
                           comm=[_gather_stage(wu, eighths("ici", (4, 5, 6, 7)) + eighths("d2d", (0, 1, 2, 3))),
                                 _gather_stage(wg, "d2d")])
    fn, [wu] = _rms_fwd(h1, sp["ffn_norm_g"], name="ffn_norm_fwd",
                        comm=[_gather_stage(wu, eighths("d2d", (4, 5, 6, 7)))])
    w_gate_f, w_up_f = wg[0], wu[0]
    (gate, up, act), [wd] = _ffn_gate_up(fn, w_gate_f, w_up_f, comm=[_gather_stage([slot["w_down"]], "both")])
    w_down_f = wd[0].reshape(D_FF, D_MODEL)
    d_out, d_out_b, loss_tile = _matmul(act, w_down_f, add=h1, name="down_proj", tm=512, tk=D_FF, loss_target=target)

    add_pair = lambda n, g4, r: _add_pair(meta, g4, r, name="grad_add_pair_" + n)
    add_chips = lambda n, own, r: _add_chips(own, r, name="grad_add_chips_" + n)
    mine, theirs = {}, {}

    dw_down = _matmul(act, d_out_b, ta=True, name="dw_down", tm=512, tn=1024, tk=s)
    dw_down = dw_down.reshape(N_CHIPS, D_FF // N_CHIPS, D_MODEL)
    (d_gate, d_up), [[r]] = _ffn_bwd_act(d_out_b, w_down_f, gate, up, comm=[_halves_stage([dw_down])])
    own_d, oth_d = add_pair("w_down", dw_down, r)
    dw_gate, [rd] = _matmul(fn, d_gate, ta=True, name="dw_gate", tm=512, tk=s, tn=D_FF // N_CHIPS, out_split=N_CHIPS,
                            comm=[_chips_stage([oth_d], (0, 2))])
    dw_up, [[r], rd] = _matmul(fn, d_up, ta=True, name="dw_up", tm=512, tk=s, tn=D_FF // N_CHIPS, out_split=N_CHIPS,
                               comm=[_halves_stage([dw_gate]), _chips_stage([oth_d], (1, 2), into=rd)])
    mine["w_down"] = add_chips("w_down", own_d, rd[0])
    own_g, oth_g = add_pair("w_gate", dw_gate, r)
    d_fn, [rg, [theirs["w_down"]]] = _matmul(
        d_gate, w_gate_f, tb=True, b_split=True, name="dfn_gate", tm=512,
        comm=[_chips_stage([oth_g], (0, 2)), _swap_stage([mine["w_down"]])])
    d_fn, [[r], rg] = _matmul(d_up, w_up_f, tb=True, b_split=True, add=d_fn, name="dfn_up", tm=512,
                              comm=[_halves_stage([dw_up]), _chips_stage([oth_g], (1, 2), into=rg)])
    mine["w_gate"] = add_chips("w_gate", own_g, rg[0])
    own_u, oth_u = add_pair("w_up", dw_up, r)
    d_h1, d_h1_b, gs["ffn_norm_g"] = _rms_bwd(d_fn, h1, sp["ffn_norm_g"], d_out, name="ffn_norm_bwd")
    dw_out, [[theirs["w_gate"]]] = _matmul(y, d_h1_b, ta=True, name="dw_out", tm=512, tk=s,
                                           comm=[_swap_stage([mine["w_gate"]])])
    dw_out = dw_out.reshape(N_CHIPS, D_MODEL // N_CHIPS, D_MODEL)
    d_y, [[r]] = _matmul(d_h1_b, w_out_f, tb=True, name="dy", comm=[_halves_stage([dw_out])])
    own_o, oth_o = add_pair("w_out", dw_out, r)
    (d_qa, d_ka, d_va, d_sink), [ru] = _swa_bwd(qa, ka, va, pos_col, pos_row, sinks, y_a, d_y,
                                                comm=[_chips_stage([oth_u], (0, 2))])
    (d_qcat, d_kcat, d_vb), [ru, [r]] = _mla_bwd(
        q_cat, k_cat, v_b, y_b, lse, d_y, comm=[_chips_stage([oth_u], (1, 2), into=ru), _chips_stage([oth_o])])
    mine["w_up"] = add_chips("w_up", own_u, ru[0])
    mine["w_out"] = add_chips("w_out", own_o, r)
    d_qm, d_km, d_vm = _mem_attn_bwd(qm, km, vm, y_m, d_y)
    (d_proj, dw_uq, dw_ukv, dg128, gs["mla_cq_norm_g"], gs["mla_ckv_norm_g"]), [[theirs["w_up"], theirs["w_out"]]] = \
        _attn_prep_bwd(proj, g128, gcq, gckv, w_uq_f, w_ukv_f, cos_t, sin_t, d_qa, d_ka, d_va, d_qcat, d_kcat, d_vb,
                       d_qm, comm=[_swap_stage([mine["w_up"], mine["w_out"]])])
    dw_mem_kv, gs["mem_norm_g"], gs["mem_k_norm_g"] = _mem_kv_bwd(
        mem, sp["mem_norm_g"], w_mem_kv_f, sp["mem_k_norm_g"], mn_b, kv_m, d_km, d_vm)
    late = ("w_uq", "w_ukv", "w_mem_kv")
    late_g = [_shards_heads(dw_uq, MLA_NOPE, MLA_ROPE), _shards_heads(dw_ukv, MLA_NOPE, MLA_V),
              dw_mem_kv.reshape(N_CHIPS, D_MODEL // N_CHIPS, -1)]
    dw_in, [rs] = _matmul(hn, d_proj, ta=True, name="dw_in", tm=512, tk=s, comm=[_halves_stage(late_g)])
    late_sums = [add_pair(n, g4, r) for n, g4, r in zip(late, late_g, rs)]
    dw_in = _shards_w_in(dw_in)
    d_hn, [rs, [r]] = _matmul(d_proj, w_in_f, tb=True, name="dhn", tk=1536,
                              comm=[_chips_stage([oth for _, oth in late_sums]), _halves_stage([dw_in])])
    for n, (own, _), r_n in zip(late, late_sums, rs):
        mine[n] = add_chips(n, own, r_n)
    own_i, oth_i = add_pair("w_in", dw_in, r)
    (grad_x, _, gs["attn_norm_g"]), [[r], late_theirs] = _rms_bwd(
        d_hn, x, sp["attn_norm_g"], d_h1, name="attn_norm_bwd",
        comm=[_chips_stage([oth_i]), _swap_stage([mine[n] for n in late])])
    theirs.update(zip(late, late_theirs))
    mine["w_in"] = add_chips("w_in", own_i, r)
    [[theirs["w_in"]]] = _run_stages([_swap_stage([mine["w_in"]])], name="grad_swap_w_in")

    fold = lambda r: r[:, :64] + r[:, 64:]
    gs["swa_q_norm_g"] = fold(dg128[G_SWA_Q:G_SWA_Q + 1])
    gs["swa_k_norm_g"] = fold(dg128[G_SWA_K:G_SWA_K + 1])
    gs["mla_qn_norm_g"] = dg128[G_QN:G_QN + 1]
    gs["mla_qr_norm_g"] = fold(dg128[G_QR:G_QR + 1])
    gs["mla_kn_norm_g"] = dg128[G_KN:G_KN + 1]
    gs["mla_kr_norm_g"] = fold(dg128[G_KR:G_KR + 1])
    gs["mem_q_norm_g"] = dg128[G_MQ:G_MQ + 1]
    gs["swa_sinks"] = d_sink[:, :SWA_Q_HEADS]

    grad, delta, new_m, new_v = {}, {}, {}, {}
    for n in BIG:
        g2, d, m2, v2 = _adamw(meta, wts[n][0], mine[n], theirs[n], mom_m[n][0], mom_v[n][0], name="adamw_" + n)
        grad[n], delta[n], new_m[n], new_v[n] = g2[None], d[None], m2[None], v2[None]

    sizes = [wts[n].shape[1] for n in SMALL]
    zero = jnp.zeros((1, LANES), F32)
    packs = _small_step(_pack([gs[n] for n in SMALL] + [loss_tile]), _pack([wts[n] for n in SMALL] + [zero]),
                        _pack([mom_m[n] for n in SMALL] + [zero]), _pack([mom_v[n] for n in SMALL] + [zero]))
    for store, buf in zip((grad, delta, new_m, new_v), packs):
        for n, val in zip(SMALL, _unpack(buf, sizes)):
            store[n] = val
    loss = _unpack(packs[0], sizes + [LANES])[-1][0, 0]

    return (loss, grad_x[None], *[grad[n] for n in WEIGHTS], *[delta[n] for n in WEIGHTS],
            *[new_m[n] for n in WEIGHTS], *[new_v[n] for n in WEIGHTS])
```

```python
import functools
import math

import jax
import jax.numpy as jnp
from jax import lax
from jax.experimental import pallas as pl
from jax.experimental.pallas import tpu as pltpu

F32 = jnp.float32
BF16 = jnp.bfloat16

D_MODEL = 2048
BLOCK = 128
EPS = 1e-6
NEG_INF = -1e30
SWA_Q_HEADS = 16
SWA_KV_HEADS = 2
SWA_HEAD_DIM = 64
MLA_HEADS = 4
MLA_RANK = 512
MLA_NOPE = 128
MLA_ROPE = 64
MLA_V = 128
ROPE_THETA = 10000.0
MEM_HEADS = 4
MEM_DIM = 128
D_FF = 5632
IN_WIDTH = 2880
IN_PAD = 3072
N_CHIPS = 4

ADAM_LR = 0.001
ADAM_B1 = 0.9
ADAM_B2 = 0.999
ADAM_EPS = 1e-08
ADAM_WD = 0.01
ADAM_STEP = 10

VMEM_LIMIT_BYTES = 56 * 1024 * 1024
LANES = 128

MESH = pl.DeviceIdType.MESH


def _params(sem=None, **kw):
    return pltpu.CompilerParams(dimension_semantics=sem, vmem_limit_bytes=VMEM_LIMIT_BYTES, **kw)


def _tile(n, want):
    if n <= want:
        return n
    t = want - want % LANES
    while t > 0:
        if n % t == 0:
            return t
        t -= LANES
    return n


ANY = pl.BlockSpec(memory_space=pl.ANY)


class _Stage:
    def __init__(self, ins, out_shapes, aliases, n_sem, issue, wait, mid=None):
        self.ins, self.out_shapes, self.aliases, self.n_sem = list(ins), list(out_shapes), dict(aliases), n_sem
        self.issue, self.wait, self.mid = issue, wait, mid


def _pcall(body, args, *, name, grid, in_specs, out_specs, out_shape, scratch_shapes=(), sem=None, comm=(),
           prefetch=(), io_alias=None):
    multi = isinstance(out_shape, (list, tuple))
    out_specs_l = list(out_specs) if multi else [out_specs]
    out_shape_l = list(out_shape) if multi else [out_shape]
    npf = len(prefetch)
    own_aliases = {npf + a: o for a, o in (io_alias or {}).items()}

    def call(fn, in_specs_, out_specs_, out_shape_, scratch_, operands, sem_, aliases=None):
        kw = dict(name=name, out_shape=out_shape_, compiler_params=_params(sem_))
        if aliases:
            kw["input_output_aliases"] = aliases
        if npf:
            spec = pltpu.PrefetchScalarGridSpec(num_scalar_prefetch=npf, grid=grid, in_specs=in_specs_,
                                                out_specs=out_specs_, scratch_shapes=scratch_)
            return pl.pallas_call(fn, grid_spec=spec, **kw)(*prefetch, *operands)
        return pl.pallas_call(fn, grid=grid, in_specs=in_specs_, out_specs=out_specs_, scratch_shapes=scratch_,
                              **kw)(*operands)

    if not comm:
        return call(body, list(in_specs), out_specs, out_shape, list(scratch_shapes), args, sem, own_aliases)
    n_in, n_out, n_scr = len(in_specs), len(out_specs_l), len(scratch_shapes)
    cins = [a for st in comm for a in st.ins]
    couts = [s for st in comm for s in st.out_shapes]
    aliases, ci, co = dict(own_aliases), 0, 0
    for st in comm:
        for a_i, o_i in st.aliases.items():
            aliases[npf + n_in + ci + a_i] = n_out + co + o_i
        ci, co = ci + len(st.ins), co + len(st.out_shapes)

    def wrapped(*refs):
        pre = refs[:npf]
        p = npf
        ins = refs[p:p + n_in]; p += n_in
        cin_refs = refs[p:p + len(cins)]; p += len(cins)
        outs = refs[p:p + n_out]; p += n_out
        cout_refs = refs[p:p + len(couts)]; p += len(couts)
        scr = refs[p:p + n_scr]; p += n_scr
        sems = refs[p:]
        first = functools.reduce(jnp.logical_and, [pl.program_id(a) == 0 for a in range(len(grid))])
        last = functools.reduce(jnp.logical_and, [pl.program_id(a) == grid[a] - 1 for a in range(len(grid))])

        def each(what):
            i, o = 0, 0
            for k, st in enumerate(comm):
                fn = getattr(st, what)
                if fn is not None:
                    fn(cin_refs[i:i + len(st.ins)], cout_refs[o:o + len(st.out_shapes)], sems[2 * k], sems[2 * k + 1])
                i, o = i + len(st.ins), o + len(st.out_shapes)

        @pl.when(first)
        def _():
            each("issue")

        if any(st.mid is not None for st in comm):
            n_steps = math.prod(grid)
            assert n_steps >= 4, "a two-leg stage needs a carrier with several grid steps"
            lin = functools.reduce(lambda acc, a: acc * grid[a] + pl.program_id(a), range(len(grid)), 0)

            @pl.when(lin == (3 * n_steps) // 4)
            def _():
                each("mid")

        body(*pre, *ins, *outs, *scr)

        @pl.when(last)
        def _():
            each("wait")

    sem_scr = [pltpu.SemaphoreType.DMA((st.n_sem,)) for st in comm for _ in range(2)]
    res = call(wrapped, list(in_specs) + [ANY] * len(cins), out_specs_l + [ANY] * len(couts), out_shape_l + couts,
               list(scratch_shapes) + sem_scr, (*args, *cins), ("arbitrary",) * len(grid), aliases)
    normal = list(res[:n_out])
    stage_outs, o = [], n_out
    for st in comm:
        stage_outs.append(list(res[o:o + len(st.out_shapes)]))
        o += len(st.out_shapes)
    return (normal if multi else normal[0]), stage_outs


def _matmul(a, b, *, name, ta=False, tb=False, add=None, out_dtype=F32, tm=1024, tn=1024, tk=2048,
            b_split=False, out_split=0, comm=(), loss_target=None, pair2=None):
    if ta:
        kdim, m = a.shape
    else:
        m, kdim = a.shape
    if b_split:
        assert tb
        nsp, n, kb = b.shape
        kb = kb * nsp
    elif tb:
        n, kb = b.shape
    else:
        kb, n = b.shape
    assert kb == kdim, (a.shape, b.shape, ta, tb)
    if b_split:
        tk = kdim
    if out_split:
        tn = _tile(n // out_split, tn)
    tm, tn, tk = _tile(m, tm), _tile(n, tn), _tile(kdim, tk)
    nk = kdim // tk
    dims = (((0 if ta else 1,), (1 if tb else 0,)), ((), ()))

    def product(a_ref, b_ref):
        if not b_split:
            return lax.dot_general(a_ref[...].astype(BF16), b_ref[...].astype(BF16), dims, preferred_element_type=F32)
        per = kdim // nsp
        return sum(lax.dot_general(a_ref[:, per * c:per * (c + 1)].astype(BF16), b_ref[c].astype(BF16), dims,
                                   preferred_element_type=F32) for c in range(nsp))

    def body(*refs):
        a_ref, b_ref = refs[:2]
        n_ab = 4 if pair2 is not None else 2
        add_ref = refs[n_ab] if add is not None else None
        n_in = n_ab + (add is not None) + (loss_target is not None)

        def products():
            r = product(a_ref, b_ref)
            return r if pair2 is None else r + product(refs[2], refs[3])
        o_ref = refs[n_in]

        def finish(r):
            if add_ref is not None:
                r = r + add_ref[...].astype(F32)
            if loss_target is None:
                o_ref[...] = r.astype(o_ref.dtype)
                return
            db_ref, l_ref = refs[n_in + 1], refs[n_in + 2]
            err = r - refs[n_in - 1][...]
            d_out = err * (1.0 / n)
            o_ref[...] = d_out
            db_ref[...] = d_out.astype(BF16)
            part = jnp.broadcast_to((0.5 / n) * jnp.sum(jnp.sum(err * err, axis=-1, keepdims=True), axis=0, keepdims=True),
                                    (1, LANES))
            first = jnp.logical_and(pl.program_id(0) == 0, pl.program_id(1) == 0)

            @pl.when(first)
            def _():
                l_ref[...] = part

            @pl.when(jnp.logical_not(first))
            def _():
                l_ref[...] += part

        if nk == 1:
            finish(products())
            return
        acc_ref = refs[-1]
        k = pl.program_id(2)
        part = products()

        @pl.when(k == 0)
        def _():
            acc_ref[...] = part

        @pl.when(k > 0)
        def _():
            acc_ref[...] += part

        @pl.when(k == nk - 1)
        def _():
            finish(acc_ref[...])

    a_spec = pl.BlockSpec((tk, tm), lambda i, j, k: (k, i)) if ta else pl.BlockSpec((tm, tk), lambda i, j, k: (i, k))
    if b_split:
        b_spec = pl.BlockSpec((nsp, tn, kdim // nsp), lambda i, j, k: (0, j, 0))
    elif tb:
        b_spec = pl.BlockSpec((tn, tk), lambda i, j, k: (j, k))
    else:
        b_spec = pl.BlockSpec((tk, tn), lambda i, j, k: (k, j))
    in_specs = [a_spec, b_spec]
    args = [a, b]
    if pair2 is not None:
        assert pair2[0].shape == a.shape and pair2[1].shape == b.shape
        in_specs += [a_spec, b_spec]
        args += list(pair2)
    if add is not None:
        in_specs.append(pl.BlockSpec((tm, tn), lambda i, j, k: (i, j)))
        args.append(add)
    tile = pl.BlockSpec((tm, tn), lambda i, j, k: (i, j))
    sem = ("parallel", "parallel", "arbitrary")
    if out_split:
        per = (n // out_split) // tn
        out_spec = pl.BlockSpec((None, tm, tn), lambda i, j, k: (j // per, i, j % per))
        out_shape = jax.ShapeDtypeStruct((out_split, m, n // out_split), out_dtype)
    elif loss_target is not None:
        in_specs.append(tile)
        args.append(loss_target)
        out_spec = [tile, tile, pl.BlockSpec((1, LANES), lambda i, j, k: (0, 0))]
        out_shape = [jax.ShapeDtypeStruct((m, n), F32), jax.ShapeDtypeStruct((m, n), BF16),
                     jax.ShapeDtypeStruct((1, LANES), F32)]
        sem = ("arbitrary",) * 3
    else:
        out_spec = tile
        out_shape = jax.ShapeDtypeStruct((m, n), out_dtype)
    return _pcall(body, args, name=name, grid=(m // tm, n // tn, nk), in_specs=in_specs, out_specs=out_spec,
                  out_shape=out_shape, scratch_shapes=[pltpu.VMEM((tm, tn), F32)] if nk > 1 else [],
                  sem=sem, comm=comm)


def _rms_fwd(x, g, *, name, tm=512, comm=()):
    s, d = x.shape
    tm = _tile(s, tm)

    def body(x_ref, g_ref, o_ref):
        xv = x_ref[...]
        r = lax.rsqrt(jnp.mean(xv * xv, axis=-1, keepdims=True) + EPS)
        o_ref[...] = (xv * r * g_ref[...]).astype(o_ref.dtype)

    return _pcall(body, (x, g), name=name, grid=(s // tm,),
                  in_specs=[pl.BlockSpec((tm, d), lambda i: (i, 0)), pl.BlockSpec((1, d), lambda i: (0, 0))],
                  out_specs=pl.BlockSpec((tm, d), lambda i: (i, 0)),
                  out_shape=jax.ShapeDtypeStruct((s, d), BF16), sem=("parallel",), comm=comm)


def _rms_bwd(dy, x, g, res, *, name, tm=512, comm=()):
    s, d = x.shape
    tm = _tile(s, tm)

    def body(dy_ref, x_ref, g_ref, res_ref, dx_ref, dxb_ref, dg_ref):
        xv = x_ref[...]
        dyv = dy_ref[...]
        r = lax.rsqrt(jnp.mean(xv * xv, axis=-1, keepdims=True) + EPS)
        xhat = xv * r
        dyg = dyv * g_ref[...]
        mt = jnp.mean(dyg * xhat, axis=-1, keepdims=True)
        dx = res_ref[...] + r * (dyg - xhat * mt)
        dx_ref[...] = dx
        dxb_ref[...] = dx.astype(BF16)
        part = jnp.sum(dyv * xhat, axis=0, keepdims=True)

        @pl.when(pl.program_id(0) == 0)
        def _():
            dg_ref[...] = part

        @pl.when(pl.program_id(0) > 0)
        def _():
            dg_ref[...] += part

    row = pl.BlockSpec((tm, d), lambda i: (i, 0))
    vec = pl.BlockSpec((1, d), lambda i: (0, 0))
    return _pcall(body, (dy, x, g, res), name=name, grid=(s // tm,), in_specs=[row, row, vec, row],
                  out_specs=[row, row, vec],
                  out_shape=[jax.ShapeDtypeStruct((s, d), F32), jax.ShapeDtypeStruct((s, d), BF16),
                             jax.ShapeDtypeStruct((1, d), F32)],
                  sem=("arbitrary",), comm=comm)


def _lane(shape):
    return lax.broadcasted_iota(jnp.int32, shape, 1)


def _halfsum(t, lo):
    s_lo = jnp.sum(jnp.where(lo, t, 0.0), axis=-1, keepdims=True)
    s_hi = jnp.sum(jnp.where(lo, 0.0, t), axis=-1, keepdims=True)
    return jnp.where(lo, s_lo, s_hi)


def _norm_pair(x, g, lo):
    r = lax.rsqrt(_halfsum(x * x, lo) * (1.0 / 64.0) + EPS)
    xhat = x * r
    return xhat * g, xhat, r


def _norm_pair_bwd(dy, g, xhat, r, lo):
    dyg = dy * g
    mt = _halfsum(dyg * xhat, lo) * (1.0 / 64.0)
    return r * (dyg - xhat * mt), jnp.sum(dy * xhat, axis=0, keepdims=True)


def _norm_full(x, g):
    r = lax.rsqrt(jnp.mean(x * x, axis=-1, keepdims=True) + EPS)
    xhat = x * r
    return xhat * g, xhat, r


def _norm_full_bwd(dy, g, xhat, r):
    dyg = dy * g
    mt = jnp.mean(dyg * xhat, axis=-1, keepdims=True)
    return r * (dyg - xhat * mt), jnp.sum(dy * xhat, axis=0, keepdims=True)


def _rot(x, first32):
    return jnp.where(first32, pltpu.roll(x, 96, axis=1), pltpu.roll(x, 32, axis=1))


def _rope(x, cos_t, sin_t, first32):
    return x * cos_t + _rot(x, first32) * sin_t


def _rope_bwd(dy, cos_t, sin_t, first32):
    return dy * cos_t + _rot(dy * sin_t, first32)


G_SWA_Q, G_SWA_K, G_QN, G_QR, G_KN, G_KR, G_MQ = range(7)

C_QA, C_KA, C_VA, C_CQ, C_CKV, C_QM, C_KR = 0, 1024, 1152, 1280, 1792, 2304, 2816


def _prep_common(p_ref, g128_ref, gcq_ref, gckv_ref, wuq_ref, wukv_ref, cos_ref, sin_ref):
    tm = p_ref.shape[0]
    lane = _lane((tm, LANES))
    lo = lane < 64
    first32 = (lane % 64) < 32
    cos_t = cos_ref[...]
    sin_t = sin_ref[...]
    g = lambda row: g128_ref[row:row + 1, :]
    out = dict(lo=lo, first32=first32, cos_t=cos_t, sin_t=sin_t, lane=lane)
    cq_n, cq_hat, cq_r = _norm_full(p_ref[:, C_CQ:C_CQ + MLA_RANK], gcq_ref[...])
    ckv_n, ckv_hat, ckv_r = _norm_full(p_ref[:, C_CKV:C_CKV + MLA_RANK], gckv_ref[...])
    cq_b = cq_n.astype(BF16)
    ckv_b = ckv_n.astype(BF16)
    q_b = jnp.dot(cq_b, wuq_ref[...], preferred_element_type=F32)
    kv_b = jnp.dot(ckv_b, wukv_ref[...], preferred_element_type=F32)
    out.update(cq_b=cq_b, cq_hat=cq_hat, cq_r=cq_r, ckv_b=ckv_b, ckv_hat=ckv_hat, ckv_r=ckv_r, q_b=q_b, kv_b=kv_b, g=g)
    return out


def _attn_prep_fwd(proj, g128, gcq, gckv, wuq, wukv, cos_t, sin_t, *, tm=512, comm=()):
    s = proj.shape[0]
    tm = _tile(s, tm)

    def body(p_ref, g128_ref, gcq_ref, gckv_ref, wuq_ref, wukv_ref, cos_ref, sin_ref,
             qa_ref, ka_ref, va_ref, qcat_ref, kcat_ref, vb_ref, qm_ref):
        c = _prep_common(p_ref, g128_ref, gcq_ref, gckv_ref, wuq_ref, wukv_ref, cos_ref, sin_ref)
        lo, first32, g = c["lo"], c["first32"], c["g"]
        for j in range(SWA_Q_HEADS // 2):
            y, _, _ = _norm_pair(p_ref[:, C_QA + 128 * j:C_QA + 128 * (j + 1)], g(G_SWA_Q), lo)
            qa_ref[:, 128 * j:128 * (j + 1)] = y.astype(BF16)
        y, _, _ = _norm_pair(p_ref[:, C_KA:C_KA + 128], g(G_SWA_K), lo)
        ka_ref[...] = y.astype(BF16)
        va_ref[...] = p_ref[:, C_VA:C_VA + 128].astype(BF16)
        kr, _, _ = _norm_pair(p_ref[:, C_KR:C_KR + 128], g(G_KR), lo)
        kr = jnp.where(lo, _rope(kr, c["cos_t"], c["sin_t"], first32), 0.0)
        krkr = (kr + pltpu.roll(kr, 64, axis=1)).astype(BF16)
        q_b, kv_b = c["q_b"], c["kv_b"]
        qr = []
        for j in range(MLA_HEADS // 2):
            y, _, _ = _norm_pair(q_b[:, 512 + 128 * j:512 + 128 * (j + 1)], g(G_QR), lo)
            qr.append(_rope(y, c["cos_t"], c["sin_t"], first32))
        for h in range(MLA_HEADS):
            qn, _, _ = _norm_full(q_b[:, 128 * h:128 * (h + 1)], g(G_QN))
            keep = lo if h % 2 == 0 else jnp.logical_not(lo)
            qcat_ref[h, :, 0:128] = qn.astype(BF16)
            qcat_ref[h, :, 128:256] = jnp.where(keep, qr[h // 2], 0.0).astype(BF16)
            kn, _, _ = _norm_full(kv_b[:, 128 * h:128 * (h + 1)], g(G_KN))
            kcat_ref[h, :, 0:128] = kn.astype(BF16)
            kcat_ref[h, :, 128:256] = krkr
        vb_ref[...] = kv_b[:, 512:1024].astype(BF16)
        for h in range(MEM_HEADS):
            y, _, _ = _norm_full(p_ref[:, C_QM + 128 * h:C_QM + 128 * (h + 1)], g(G_MQ))
            qm_ref[:, 128 * h:128 * (h + 1)] = y.astype(BF16)

    row = lambda w: pl.BlockSpec((tm, w), lambda i: (i, 0))
    full = lambda shape: pl.BlockSpec(shape, lambda i: tuple(0 for _ in shape))
    cat = pl.BlockSpec((MLA_HEADS, tm, 256), lambda i: (0, i, 0))
    return _pcall(
        body, (proj, g128, gcq, gckv, wuq, wukv, cos_t, sin_t), name="attn_prep_fwd", grid=(s // tm,),
        in_specs=[row(IN_PAD), full((8, 128)), full((1, 512)), full((1, 512)), full((512, 768)), full((512, 1024)),
                  row(128), row(128)],
        out_specs=[row(1024), row(128), row(128), cat, cat, row(512), row(512)],
        out_shape=[jax.ShapeDtypeStruct((s, 1024), BF16), jax.ShapeDtypeStruct((s, 128), BF16),
                   jax.ShapeDtypeStruct((s, 128), BF16), jax.ShapeDtypeStruct((MLA_HEADS, s, 256), BF16),
                   jax.ShapeDtypeStruct((MLA_HEADS, s, 256), BF16), jax.ShapeDtypeStruct((s, 512), BF16),
                   jax.ShapeDtypeStruct((s, 512), BF16)],
        sem=("parallel",), comm=comm)


def _attn_prep_bwd(proj, g128, gcq, gckv, wuq, wukv, cos_t, sin_t,
                   d_qa, d_ka, d_va, d_qcat, d_kcat, d_vb, d_qm, *, tm=512, comm=()):
    s = proj.shape[0]
    tm = _tile(s, tm)

    def body(p_ref, g128_ref, gcq_ref, gckv_ref, wuq_ref, wukv_ref, cos_ref, sin_ref,
             dqa_ref, dka_ref, dva_ref, dqcat_ref, dkcat_ref, dvb_ref, dqm_ref,
             dp_ref, dwuq_ref, dwukv_ref, dg128_ref, dgcq_ref, dgckv_ref):
        c = _prep_common(p_ref, g128_ref, gcq_ref, gckv_ref, wuq_ref, wukv_ref, cos_ref, sin_ref)
        lo, first32, g = c["lo"], c["first32"], c["g"]
        cos_v, sin_v = c["cos_t"], c["sin_t"]
        q_b, kv_b = c["q_b"], c["kv_b"]
        zero_row = jnp.zeros((1, LANES), F32)
        dg = {k: zero_row for k in range(7)}

        for j in range(SWA_Q_HEADS // 2):
            sl = slice(C_QA + 128 * j, C_QA + 128 * (j + 1))
            _, xhat, r = _norm_pair(p_ref[:, sl], g(G_SWA_Q), lo)
            dx, dgj = _norm_pair_bwd(dqa_ref[:, 128 * j:128 * (j + 1)], g(G_SWA_Q), xhat, r, lo)
            dp_ref[:, sl] = dx.astype(BF16)
            dg[G_SWA_Q] = dg[G_SWA_Q] + dgj
        _, xhat, r = _norm_pair(p_ref[:, C_KA:C_KA + 128], g(G_SWA_K), lo)
        dx, dgj = _norm_pair_bwd(dka_ref[...], g(G_SWA_K), xhat, r, lo)
        dp_ref[:, C_KA:C_KA + 128] = dx.astype(BF16)
        dg[G_SWA_K] = dgj
        dp_ref[:, C_VA:C_VA + 128] = dva_ref[...].astype(BF16)

        dqb_parts = [None] * 6
        for h in range(MLA_HEADS):
            _, xhat, r = _norm_full(q_b[:, 128 * h:128 * (h + 1)], g(G_QN))
            dx, dgj = _norm_full_bwd(dqcat_ref[h, :, 0:128], g(G_QN), xhat, r)
            dqb_parts[h] = dx
            dg[G_QN] = dg[G_QN] + dgj
        for j in range(MLA_HEADS // 2):
            _, xhat, r = _norm_pair(q_b[:, 512 + 128 * j:512 + 128 * (j + 1)], g(G_QR), lo)
            d_rot = jnp.where(lo, dqcat_ref[2 * j, :, 128:256], dqcat_ref[2 * j + 1, :, 128:256])
            d_y = _rope_bwd(d_rot, cos_v, sin_v, first32)
            dx, dgj = _norm_pair_bwd(d_y, g(G_QR), xhat, r, lo)
            dqb_parts[4 + j] = dx
            dg[G_QR] = dg[G_QR] + dgj
        d_qb = jnp.concatenate(dqb_parts, axis=1).astype(BF16)
        dwuq = lax.dot_general(c["cq_b"], d_qb, (((0,), (0,)), ((), ())), preferred_element_type=F32)
        d_cqn = lax.dot_general(d_qb, wuq_ref[...], (((1,), (1,)), ((), ())), preferred_element_type=F32)
        dx, dgcq = _norm_full_bwd(d_cqn, gcq_ref[...], c["cq_hat"], c["cq_r"])
        dp_ref[:, C_CQ:C_CQ + MLA_RANK] = dx.astype(BF16)

        dkv_parts = []
        d_krkr = jnp.zeros((p_ref.shape[0], LANES), F32)
        for h in range(MLA_HEADS):
            _, xhat, r = _norm_full(kv_b[:, 128 * h:128 * (h + 1)], g(G_KN))
            dx, dgj = _norm_full_bwd(dkcat_ref[h, :, 0:128], g(G_KN), xhat, r)
            dkv_parts.append(dx)
            dg[G_KN] = dg[G_KN] + dgj
            d_krkr = d_krkr + dkcat_ref[h, :, 128:256]
        d_kvb = jnp.concatenate(dkv_parts + [dvb_ref[...]], axis=1).astype(BF16)
        dwukv = lax.dot_general(c["ckv_b"], d_kvb, (((0,), (0,)), ((), ())), preferred_element_type=F32)
        d_ckvn = lax.dot_general(d_kvb, wukv_ref[...], (((1,), (1,)), ((), ())), preferred_element_type=F32)
        dx, dgckv = _norm_full_bwd(d_ckvn, gckv_ref[...], c["ckv_hat"], c["ckv_r"])
        dp_ref[:, C_CKV:C_CKV + MLA_RANK] = dx.astype(BF16)

        _, xhat, r = _norm_pair(p_ref[:, C_KR:C_KR + 128], g(G_KR), lo)
        d_kr = jnp.where(lo, d_krkr + pltpu.roll(d_krkr, 64, axis=1), 0.0)
        d_y = jnp.where(lo, _rope_bwd(d_kr, cos_v, sin_v, first32), 0.0)
        dx, dgj = _norm_pair_bwd(d_y, g(G_KR), xhat, r, lo)
        dp_ref[:, C_KR:C_KR + 128] = jnp.where(lo, dx, 0.0).astype(BF16)
        dp_ref[:, C_KR + 128:] = jnp.zeros((p_ref.shape[0], IN_PAD - C_KR - 128), BF16)
        dg[G_KR] = dgj

        for h in range(MEM_HEADS):
            sl = slice(C_QM + 128 * h, C_QM + 128 * (h + 1))
            _, xhat, r = _norm_full(p_ref[:, sl], g(G_MQ))
            dx, dgj = _norm_full_bwd(dqm_ref[:, 128 * h:128 * (h + 1)], g(G_MQ), xhat, r)
            dp_ref[:, sl] = dx.astype(BF16)
            dg[G_MQ] = dg[G_MQ] + dgj

        dg_tile = jnp.concatenate([dg[k] for k in range(7)] + [zero_row], axis=0)

        @pl.when(pl.program_id(0) == 0)
        def _():
            dwuq_ref[...] = dwuq
            dwukv_ref[...] = dwukv
            dg128_ref[...] = dg_tile
            dgcq_ref[...] = dgcq
            dgckv_ref[...] = dgckv

        @pl.when(pl.program_id(0) > 0)
        def _():
            dwuq_ref[...] += dwuq
            dwukv_ref[...] += dwukv
            dg128_ref[...] += dg_tile
            dgcq_ref[...] += dgcq
            dgckv_ref[...] += dgckv

    row = lambda w: pl.BlockSpec((tm, w), lambda i: (i, 0))
    full = lambda shape: pl.BlockSpec(shape, lambda i: tuple(0 for _ in shape))
    cat = pl.BlockSpec((MLA_HEADS, tm, 256), lambda i: (0, i, 0))
    return _pcall(
        body, (proj, g128, gcq, gckv, wuq, wukv, cos_t, sin_t, d_qa, d_ka, d_va, d_qcat, d_kcat, d_vb, d_qm),
        name="attn_prep_bwd", grid=(s // tm,),
        in_specs=[row(IN_PAD), full((8, 128)), full((1, 512)), full((1, 512)), full((512, 768)), full((512, 1024)),
                  row(128), row(128),
                  row(1024), row(128), row(128), cat, cat, row(512), row(512)],
        out_specs=[row(IN_PAD), full((512, 768)), full((512, 1024)), full((8, 128)), full((1, 512)), full((1, 512))],
        out_shape=[jax.ShapeDtypeStruct((s, IN_PAD), BF16), jax.ShapeDtypeStruct((512, 768), F32),
                   jax.ShapeDtypeStruct((512, 1024), F32), jax.ShapeDtypeStruct((8, 128), F32),
                   jax.ShapeDtypeStruct((1, 512), F32), jax.ShapeDtypeStruct((1, 512), F32)],
        sem=("arbitrary",), comm=comm)


SWA_SLOPES = tuple(2.0 ** (-8.0 * h / SWA_Q_HEADS) for h in range(1, SWA_Q_HEADS + 1))
SWA_SCALE = SWA_HEAD_DIM ** -0.5
NT_DIMS = (((1,), (1,)), ((), ()))
TN_DIMS = (((0,), (0,)), ((), ()))


def _swa_span(n, kp_ref, kc_ref, vp_ref, vc_ref, pcol_ref, pprow_ref, pcrow_ref):
    k_span = jnp.concatenate([kp_ref[...], kc_ref[...]], axis=0).astype(F32)
    v_span = jnp.concatenate([vp_ref[...], vc_ref[...]], axis=0).astype(F32)
    lo = _lane((2 * BLOCK, LANES)) < 64
    k_sw = pltpu.roll(k_span, 64, axis=1)
    v_sw = pltpu.roll(v_span, 64, axis=1)
    kk = (jnp.where(lo, k_span, k_sw).astype(BF16), jnp.where(lo, k_sw, k_span).astype(BF16))
    vv_lo = (jnp.where(lo, v_span, 0.0).astype(BF16), jnp.where(lo, v_sw, 0.0).astype(BF16))
    vv_hi = (jnp.where(lo, 0.0, v_sw).astype(BF16), jnp.where(lo, 0.0, v_span).astype(BF16))
    pk = jnp.concatenate([pprow_ref[...], pcrow_ref[...]], axis=1)
    dist = jnp.abs(pcol_ref[...] - pk)
    qi = lax.broadcasted_iota(jnp.int32, (BLOCK, 2 * BLOCK), 0)
    ki = lax.broadcasted_iota(jnp.int32, (BLOCK, 2 * BLOCK), 1)
    first_key = jnp.where(n > 0, qi + 1, jnp.maximum(qi + 1, BLOCK))
    valid = jnp.logical_and(ki >= first_key, ki <= qi + BLOCK)
    mask_add = jnp.where(valid, 0.0, NEG_INF)
    return kk, vv_lo, vv_hi, dist, mask_add


def _swa_heads(q_ref, lo):
    heads = []
    for j in range(SWA_Q_HEADS // 2):
        q_pair = q_ref[:, 128 * j:128 * (j + 1)].astype(F32)
        for par in (0, 1):
            q_h = jnp.where(lo if par == 0 else jnp.logical_not(lo), q_pair, 0.0).astype(BF16)
            heads.append((2 * j + par, (2 * j) // (SWA_Q_HEADS // SWA_KV_HEADS), par, q_h))
    return heads


def _swa_probs(raw, dist, mask_add, slope, sink):
    s = raw * SWA_SCALE - slope * dist + mask_add
    m = jnp.maximum(jnp.max(s, axis=-1, keepdims=True), sink)
    e = jnp.exp(s - m)
    e_sink = jnp.exp(sink - m)
    inv = 1.0 / (jnp.sum(e, axis=-1, keepdims=True) + e_sink)
    return e * inv, e_sink * inv


def _swa_specs():
    blk = lambda w: pl.BlockSpec((BLOCK, w), lambda n: (n, 0))
    prev = lambda w: pl.BlockSpec((BLOCK, w), lambda n: (jnp.maximum(n - 1, 0), 0))
    prow_c = pl.BlockSpec((1, BLOCK), lambda n: (0, n))
    prow_p = pl.BlockSpec((1, BLOCK), lambda n: (0, jnp.maximum(n - 1, 0)))
    smem = pl.BlockSpec(memory_space=pltpu.SMEM)
    return [blk(1024), prev(128), blk(128), prev(128), blk(128), blk(1), prow_p, prow_c, smem], blk


def _swa_fwd(qa, ka, va, pos_col, pos_row, sinks, *, comm=()):
    s = qa.shape[0]
    in_specs, blk = _swa_specs()

    def body(q_ref, kp_ref, kc_ref, vp_ref, vc_ref, pcol_ref, pprow_ref, pcrow_ref, sink_ref, o_ref, yb_ref):
        n = pl.program_id(0)
        kk, vv_lo, vv_hi, dist, mask_add = _swa_span(n, kp_ref, kc_ref, vp_ref, vc_ref, pcol_ref, pprow_ref, pcrow_ref)
        lo = _lane((BLOCK, LANES)) < 64
        heads = _swa_heads(q_ref, lo)
        raws = [lax.dot_general(q_h, kk[kv], NT_DIMS, preferred_element_type=F32) for _, kv, _, q_h in heads]
        probs = [_swa_probs(raw, dist, mask_add, SWA_SLOPES[h], sink_ref[h])[0].astype(BF16)
                 for raw, (h, _, _, _) in zip(raws, heads)]
        for j in range(SWA_Q_HEADS // 2):
            kv = heads[2 * j][1]
            out = (jnp.dot(probs[2 * j], vv_lo[kv], preferred_element_type=F32)
                   + jnp.dot(probs[2 * j + 1], vv_hi[kv], preferred_element_type=F32))
            o_ref[:, 128 * j:128 * (j + 1)] = out
            yb_ref[:, 128 * j:128 * (j + 1)] = out.astype(BF16)

    return _pcall(body, (qa, ka, ka, va, va, pos_col, pos_row, pos_row, sinks), name="swa_fwd", grid=(s // BLOCK,),
                  in_specs=in_specs, out_specs=[blk(1024), blk(1024)],
                  out_shape=[jax.ShapeDtypeStruct((s, 1024), F32), jax.ShapeDtypeStruct((s, D_MODEL), BF16)],
                  sem=("parallel",), comm=comm)


def _swa_bwd(qa, ka, va, pos_col, pos_row, sinks, y_a, d_y, *, comm=()):
    s = qa.shape[0]
    in_specs, blk = _swa_specs()
    whole = pl.BlockSpec((s, 128), lambda n: (0, 0))

    def body(q_ref, kp_ref, kc_ref, vp_ref, vc_ref, pcol_ref, pprow_ref, pcrow_ref, sink_ref, y_ref, dy_ref,
             dq_ref, dk_ref, dv_ref, dsink_ref):
        n = pl.program_id(0)

        @pl.when(n == 0)
        def _():
            dk_ref[...] = jnp.zeros_like(dk_ref)
            dv_ref[...] = jnp.zeros_like(dv_ref)
            dsink_ref[...] = jnp.zeros_like(dsink_ref)

        kk, vv_lo, vv_hi, dist, mask_add = _swa_span(n, kp_ref, kc_ref, vp_ref, vc_ref, pcol_ref, pprow_ref, pcrow_ref)
        lo = _lane((BLOCK, LANES)) < 64
        lo2 = _lane((2 * BLOCK, LANES)) < 64
        lane1 = _lane((1, LANES))
        dsink = jnp.zeros((1, LANES), F32)
        dkk = [jnp.zeros((2 * BLOCK, LANES), F32) for _ in range(SWA_KV_HEADS)]
        dvv = [jnp.zeros((2 * BLOCK, LANES), F32) for _ in range(SWA_KV_HEADS)]
        heads = _swa_heads(q_ref, lo)
        do_b, deltas = [], []
        for j in range(SWA_Q_HEADS // 2):
            do_pair = dy_ref[:, 128 * j:128 * (j + 1)]
            doy = do_pair * y_ref[:, 128 * j:128 * (j + 1)]
            do_b.append(do_pair.astype(BF16))
            deltas.append(jnp.sum(jnp.where(lo, doy, 0.0), axis=-1, keepdims=True))
            deltas.append(jnp.sum(jnp.where(lo, 0.0, doy), axis=-1, keepdims=True))
        raws = [lax.dot_general(q_h, kk[kv], NT_DIMS, preferred_element_type=F32) for _, kv, _, q_h in heads]
        dps = [lax.dot_general(do_b[h // 2], (vv_lo, vv_hi)[par][kv], NT_DIMS, preferred_element_type=F32)
               for h, kv, par, _ in heads]
        p_b, ds_b = [], []
        for h, kv, par, _ in heads:
            p, p_sink = _swa_probs(raws[h], dist, mask_add, SWA_SLOPES[h], sink_ref[h])
            ds = p * (dps[h] - deltas[h])
            dsink = dsink + jnp.where(lane1 == h, -jnp.sum(p_sink * deltas[h], axis=0, keepdims=True), 0.0)
            p_b.append(p.astype(BF16))
            ds_b.append((ds * SWA_SCALE).astype(BF16))
        dq_halves = []
        for h, kv, par, q_h in heads:
            dq_halves.append(jnp.dot(ds_b[h], kk[kv], preferred_element_type=F32))
            dkk[kv] = dkk[kv] + lax.dot_general(ds_b[h], q_h, TN_DIMS, preferred_element_type=F32)
            pv = lax.dot_general(p_b[h], do_b[h // 2], TN_DIMS, preferred_element_type=F32)
            dvv[kv] = dvv[kv] + jnp.where(lo2 if par == 0 else jnp.logical_not(lo2), pv, 0.0)
        for j in range(SWA_Q_HEADS // 2):
            dq_ref[:, 128 * j:128 * (j + 1)] = jnp.where(lo, dq_halves[2 * j], dq_halves[2 * j + 1])
        fold = lambda t: t + pltpu.roll(t, 64, axis=1)
        dk_span = jnp.where(lo2, fold(dkk[0]), fold(dkk[1]))
        dv_span = jnp.where(lo2, fold(dvv[0]), fold(dvv[1]))
        prev0 = pl.multiple_of(jnp.maximum(n - 1, 0) * BLOCK, BLOCK)
        cur0 = pl.multiple_of(n * BLOCK, BLOCK)
        dk_ref[pl.ds(prev0, BLOCK), :] += dk_span[0:BLOCK]
        dk_ref[pl.ds(cur0, BLOCK), :] += dk_span[BLOCK:]
        dv_ref[pl.ds(prev0, BLOCK), :] += dv_span[0:BLOCK]
        dv_ref[pl.ds(cur0, BLOCK), :] += dv_span[BLOCK:]
        dsink_ref[...] += dsink

    return _pcall(
        body, (qa, ka, ka, va, va, pos_col, pos_row, pos_row, sinks, y_a, d_y), name="swa_bwd", grid=(s // BLOCK,),
        in_specs=in_specs + [blk(1024), blk(1024)],
        out_specs=[blk(1024), whole, whole, pl.BlockSpec((1, LANES), lambda n: (0, 0))],
        out_shape=[jax.ShapeDtypeStruct((s, 1024), F32), jax.ShapeDtypeStruct((s, 128), F32),
                   jax.ShapeDtypeStruct((s, 128), F32), jax.ShapeDtypeStruct((1, LANES), F32)],
        sem=("arbitrary",), comm=comm)


MLA_SCALE = (MLA_NOPE + MLA_ROPE) ** -0.5
LOG2_E = math.log2(math.e)
MLA_TILE = 1024


def _tile_pairs(nt, q_major):
    pairs = [(i, j) for i in range(nt) for j in range(i + 1)] if q_major else \
            [(i, j) for j in range(nt) for i in range(j, nt)]
    return jnp.asarray([p[0] for p in pairs], jnp.int32), jnp.asarray([p[1] for p in pairs], jnp.int32)


def _diag_mask(t):
    return lax.broadcasted_iota(jnp.int32, (t, t), 1) <= lax.broadcasted_iota(jnp.int32, (t, t), 0)


def _mla_fwd(q_cat, k_cat, v_b, y_all, *, comm=()):
    nh, s, _ = q_cat.shape
    t = _tile(s, MLA_TILE)
    qi, kj = _tile_pairs(s // t, True)
    ycol = (SWA_Q_HEADS * SWA_HEAD_DIM) // (nh * MLA_V)

    def body(qi_ref, kj_ref, q_ref, k_ref, v_ref, _, o_ref, lse_ref, yb_ref, m_sc, l_sc, acc_sc):
        i, j = qi_ref[pl.program_id(0)], kj_ref[pl.program_id(0)]

        @pl.when(j == 0)
        def _():
            m_sc[...] = jnp.full_like(m_sc, NEG_INF)
            l_sc[...] = jnp.zeros_like(l_sc)
            acc_sc[...] = jnp.zeros_like(acc_sc)

        def update(diagonal):
            scores = [lax.dot_general(q_ref[h], k_ref[h], NT_DIMS, preferred_element_type=F32) for h in range(nh)]
            probs, alphas = [], []
            for h in range(nh):
                raw = scores[h]
                if diagonal:
                    raw = jnp.where(_diag_mask(t), raw, NEG_INF)
                m_old = m_sc[h]
                m_new = jnp.maximum(m_old, jnp.max(raw, axis=-1, keepdims=True))
                alpha = jnp.exp2((m_old - m_new) * (MLA_SCALE * LOG2_E))
                p = jnp.exp2((raw - m_new) * (MLA_SCALE * LOG2_E))
                l_sc[h] = alpha * l_sc[h] + jnp.sum(p, axis=-1, keepdims=True)
                m_sc[h] = m_new
                probs.append(p.astype(BF16))
                alphas.append(alpha)
            for h in range(nh):
                acc_sc[h] = alphas[h] * acc_sc[h] + jnp.dot(probs[h], v_ref[:, MLA_V * h:MLA_V * (h + 1)],
                                                            preferred_element_type=F32)

        @pl.when(j < i)
        def _():
            update(False)

        @pl.when(j == i)
        def _():
            update(True)
            for h in range(nh):
                out = acc_sc[h] * (1.0 / l_sc[h])
                o_ref[:, MLA_V * h:MLA_V * (h + 1)] = out
                yb_ref[:, MLA_V * h:MLA_V * (h + 1)] = out.astype(BF16)
                lse_ref[h] = m_sc[h] * MLA_SCALE + jnp.log(l_sc[h])

    return _pcall(
        body, (q_cat, k_cat, v_b, y_all), name="mla_fwd", grid=(qi.shape[0],), prefetch=(qi, kj),
        in_specs=[pl.BlockSpec((nh, t, 256), lambda p, qi, kj: (0, qi[p], 0)),
                  pl.BlockSpec((nh, t, 256), lambda p, qi, kj: (0, kj[p], 0)),
                  pl.BlockSpec((t, nh * MLA_V), lambda p, qi, kj: (kj[p], 0)), ANY],
        out_specs=[pl.BlockSpec((t, nh * MLA_V), lambda p, qi, kj: (qi[p], 0)),
                   pl.BlockSpec((nh, t, 1), lambda p, qi, kj: (0, qi[p], 0)),
                   pl.BlockSpec((t, nh * MLA_V), lambda p, qi, kj: (qi[p], ycol))],
        out_shape=[jax.ShapeDtypeStruct((s, nh * MLA_V), F32), jax.ShapeDtypeStruct((nh, s, 1), F32),
                   jax.ShapeDtypeStruct(y_all.shape, y_all.dtype)],
        scratch_shapes=[pltpu.VMEM((nh, t, 1), F32), pltpu.VMEM((nh, t, 1), F32), pltpu.VMEM((nh, t, MLA_V), F32)],
        sem=("arbitrary",), comm=comm, io_alias={3: 2})


def _mla_bwd(q_cat, k_cat, v_b, y_b, lse, d_y, *, comm=()):
    nh, s, _ = q_cat.shape
    t = _tile(s, MLA_TILE)
    nt = s // t
    hp = 2
    wv = hp * MLA_V
    col0 = (SWA_Q_HEADS * SWA_HEAD_DIM) // wv
    qi, kj = _tile_pairs(nt, False)

    def body(qi_ref, kj_ref, q_ref, k_ref, v_ref, y_ref, lse_ref, dy_ref, dq_ref, dk_ref, dv_ref, dk_sc, dv_sc):
        step = pl.program_id(1)
        i, j = qi_ref[step], kj_ref[step]

        @pl.when(step == 0)
        def _():
            dq_ref[...] = jnp.zeros_like(dq_ref)

        @pl.when(i == j)
        def _():
            dk_sc[...] = jnp.zeros_like(dk_sc)
            dv_sc[...] = jnp.zeros_like(dv_sc)

        def update(diagonal):
            rows = pl.ds(pl.multiple_of(i * t, t), t)
            cols = [slice(MLA_V * h, MLA_V * (h + 1)) for h in range(hp)]
            do_b = [dy_ref[:, cols[h]].astype(BF16) for h in range(hp)]
            scores = [lax.dot_general(q_ref[h], k_ref[h], NT_DIMS, preferred_element_type=F32) for h in range(hp)]
            dps = [lax.dot_general(do_b[h], v_ref[:, cols[h]], NT_DIMS, preferred_element_type=F32) for h in range(hp)]
            p_b, ds_b = [], []
            for h in range(hp):
                p = jnp.exp(scores[h] * MLA_SCALE - lse_ref[h])
                if diagonal:
                    p = jnp.where(_diag_mask(t), p, 0.0)
                delta = jnp.sum(dy_ref[:, cols[h]] * y_ref[:, cols[h]], axis=-1, keepdims=True)
                p_b.append(p.astype(BF16))
                ds_b.append((p * (dps[h] - delta) * MLA_SCALE).astype(BF16))
            for h in range(hp):
                dv_sc[h] += lax.dot_general(p_b[h], do_b[h], TN_DIMS, preferred_element_type=F32)
                dk_sc[h] += lax.dot_general(ds_b[h], q_ref[h], TN_DIMS, preferred_element_type=F32)
                dq_ref[h, rows, :] += jnp.dot(ds_b[h], k_ref[h], preferred_element_type=F32)

        @pl.when(i > j)
        def _():
            update(False)

        @pl.when(i == j)
        def _():
            update(True)

        @pl.when(i == nt - 1)
        def _():
            dk_ref[...] = dk_sc[...]
            for h in range(hp):
                dv_ref[:, MLA_V * h:MLA_V * (h + 1)] = dv_sc[h]

    return _pcall(
        body, (q_cat, k_cat, v_b, y_b, lse, d_y), name="mla_bwd", grid=(nh // hp, qi.shape[0]), prefetch=(qi, kj),
        in_specs=[pl.BlockSpec((hp, t, 256), lambda g, p, qi, kj: (g, qi[p], 0)),
                  pl.BlockSpec((hp, t, 256), lambda g, p, qi, kj: (g, kj[p], 0)),
                  pl.BlockSpec((t, wv), lambda g, p, qi, kj: (kj[p], g)),
                  pl.BlockSpec((t, wv), lambda g, p, qi, kj: (qi[p], g)),
                  pl.BlockSpec((hp, t, 1), lambda g, p, qi, kj: (g, qi[p], 0)),
                  pl.BlockSpec((t, wv), lambda g, p, qi, kj: (qi[p], col0 + g))],
        out_specs=[pl.BlockSpec((hp, s, 256), lambda g, p, qi, kj: (g, 0, 0)),
                   pl.BlockSpec((hp, t, 256), lambda g, p, qi, kj: (g, kj[p], 0)),
                   pl.BlockSpec((t, wv), lambda g, p, qi, kj: (kj[p], g))],
        out_shape=[jax.ShapeDtypeStruct((nh, s, 256), F32), jax.ShapeDtypeStruct((nh, s, 256), F32),
                   jax.ShapeDtypeStruct((s, nh * MLA_V), F32)],
        scratch_shapes=[pltpu.VMEM((hp, t, 256), F32), pltpu.VMEM((hp, t, MLA_V), F32)],
        sem=("arbitrary", "arbitrary"), comm=comm)


MEM_SCALE = MEM_DIM ** -0.5


def _mem_kv_fwd(mem, g_mem, w_memkv, g_mk):
    m_len = mem.shape[0]

    def body(mem_ref, g_ref, w_ref, gk_ref, mn_ref, kv_ref, kn_ref, v_ref):
        mn, _, _ = _norm_full(mem_ref[...], g_ref[...])
        mn_b = mn.astype(BF16)
        mn_ref[...] = mn_b
        kv = jnp.dot(mn_b, w_ref[...], preferred_element_type=F32)
        kv_ref[...] = kv
        for h in range(MEM_HEADS):
            kn, _, _ = _norm_full(kv[:, 128 * h:128 * (h + 1)], gk_ref[...])
            kn_ref[:, 128 * h:128 * (h + 1)] = kn.astype(BF16)
        v_ref[...] = kv[:, 512:1024].astype(BF16)

    return pl.pallas_call(
        body, name="mem_kv_fwd",
        out_shape=[jax.ShapeDtypeStruct((m_len, D_MODEL), BF16), jax.ShapeDtypeStruct((m_len, 1024), F32),
                   jax.ShapeDtypeStruct((m_len, 512), BF16), jax.ShapeDtypeStruct((m_len, 512), BF16)],
        compiler_params=_params(),
    )(mem, g_mem, w_memkv, g_mk)


def _mem_kv_bwd(mem, g_mem, w_memkv, g_mk, mn_b, kv, d_kn, d_v):
    m_len = mem.shape[0]

    def body(mem_ref, g_ref, w_ref, gk_ref, mn_ref, kv_ref, dkn_ref, dv_ref, dw_ref, dgmem_ref, dgk_ref):
        parts = []
        dgk = jnp.zeros((1, LANES), F32)
        for h in range(MEM_HEADS):
            _, xhat, r = _norm_full(kv_ref[:, 128 * h:128 * (h + 1)], gk_ref[...])
            dx, dgh = _norm_full_bwd(dkn_ref[:, 128 * h:128 * (h + 1)], gk_ref[...], xhat, r)
            parts.append(dx)
            dgk = dgk + dgh
        d_kv = jnp.concatenate(parts + [dv_ref[...]], axis=1).astype(BF16)
        dw_ref[...] = lax.dot_general(mn_ref[...], d_kv, TN_DIMS, preferred_element_type=F32)
        d_mn = lax.dot_general(d_kv, w_ref[...], NT_DIMS, preferred_element_type=F32)
        _, xhat, _ = _norm_full(mem_ref[...], g_ref[...])
        dgmem_ref[...] = jnp.sum(d_mn * xhat, axis=0, keepdims=True)
        dgk_ref[...] = dgk

    return pl.pallas_call(
        body, name="mem_kv_bwd",
        out_shape=[jax.ShapeDtypeStruct((D_MODEL, 1024), F32), jax.ShapeDtypeStruct((1, D_MODEL), F32),
                   jax.ShapeDtypeStruct((1, LANES), F32)],
        compiler_params=_params(),
    )(mem, g_mem, w_memkv, g_mk, mn_b, kv, d_kn, d_v)


def _mem_softmax(raw):
    sc = raw * MEM_SCALE
    e = jnp.exp(sc - jnp.max(sc, axis=-1, keepdims=True))
    return e * (1.0 / jnp.sum(e, axis=-1, keepdims=True))


def _mem_attn_fwd(qm, km, vm, y_all, *, tm=512):
    s = qm.shape[0]
    tm = _tile(s, tm)
    m_len = km.shape[0]
    ycol = (SWA_Q_HEADS * SWA_HEAD_DIM + MLA_HEADS * MLA_V) // 512

    def body(q_ref, k_ref, v_ref, _, o_ref, yb_ref):
        cols = [slice(128 * h, 128 * (h + 1)) for h in range(MEM_HEADS)]
        raws = [lax.dot_general(q_ref[:, sl], k_ref[:, sl], NT_DIMS, preferred_element_type=F32) for sl in cols]
        probs = [_mem_softmax(raw).astype(BF16) for raw in raws]
        for p, sl in zip(probs, cols):
            out = jnp.dot(p, v_ref[:, sl], preferred_element_type=F32)
            o_ref[:, sl] = out
            yb_ref[:, sl] = out.astype(BF16)

    kvspec = pl.BlockSpec((m_len, 512), lambda i: (0, 0))
    return _pcall(
        body, (qm, km, vm, y_all), name="mem_attn_fwd", grid=(s // tm,),
        in_specs=[pl.BlockSpec((tm, 512), lambda i: (i, 0)), kvspec, kvspec, ANY],
        out_specs=[pl.BlockSpec((tm, 512), lambda i: (i, 0)), pl.BlockSpec((tm, 512), lambda i: (i, ycol))],
        out_shape=[jax.ShapeDtypeStruct((s, 512), F32), jax.ShapeDtypeStruct(y_all.shape, y_all.dtype)],
        sem=("parallel",), io_alias={3: 1})


def _mem_attn_bwd(qm, km, vm, y_m, d_y, *, tm=1024):
    s = qm.shape[0]
    tm = _tile(s, tm)
    m_len = km.shape[0]
    col0 = (SWA_Q_HEADS * SWA_HEAD_DIM + MLA_HEADS * MLA_V) // 512

    def body(q_ref, k_ref, v_ref, y_ref, dy_ref, dq_ref, dk_ref, dv_ref):
        @pl.when(pl.program_id(0) == 0)
        def _():
            dk_ref[...] = jnp.zeros_like(dk_ref)
            dv_ref[...] = jnp.zeros_like(dv_ref)

        cols = [slice(128 * h, 128 * (h + 1)) for h in range(MEM_HEADS)]
        do_b = [dy_ref[:, sl].astype(BF16) for sl in cols]
        raws = [lax.dot_general(q_ref[:, sl], k_ref[:, sl], NT_DIMS, preferred_element_type=F32) for sl in cols]
        dps = [lax.dot_general(do_b[h], v_ref[:, sl], NT_DIMS, preferred_element_type=F32) for h, sl in enumerate(cols)]
        p_b, ds_b = [], []
        for h, sl in enumerate(cols):
            p = _mem_softmax(raws[h])
            delta = jnp.sum(dy_ref[:, sl] * y_ref[:, sl], axis=-1, keepdims=True)
            p_b.append(p.astype(BF16))
            ds_b.append((p * (dps[h] - delta) * MEM_SCALE).astype(BF16))
        for h, sl in enumerate(cols):
            dv_ref[:, sl] += lax.dot_general(p_b[h], do_b[h], TN_DIMS, preferred_element_type=F32)
            dq_ref[:, sl] = jnp.dot(ds_b[h], k_ref[:, sl], preferred_element_type=F32)
            dk_ref[:, sl] += lax.dot_general(ds_b[h], q_ref[:, sl], TN_DIMS, preferred_element_type=F32)

    kvspec = pl.BlockSpec((m_len, 512), lambda i: (0, 0))
    row = pl.BlockSpec((tm, 512), lambda i: (i, 0))
    return pl.pallas_call(
        body, name="mem_attn_bwd", grid=(s // tm,),
        in_specs=[row, kvspec, kvspec, row, pl.BlockSpec((tm, 512), lambda i: (i, col0))],
        out_specs=[row, kvspec, kvspec],
        out_shape=[jax.ShapeDtypeStruct((s, 512), F32), jax.ShapeDtypeStruct((m_len, 512), F32),
                   jax.ShapeDtypeStruct((m_len, 512), F32)],
        compiler_params=_params(("arbitrary",)),
    )(qm, km, vm, y_m, d_y)


def _ffn_gate_up(fn, w_gate, w_up, *, tm=512, comm=()):
    s, d = fn.shape
    nsp, _, tf = w_gate.shape
    f = nsp * tf
    tm = _tile(s, tm)

    def body(x_ref, wg_ref, wu_ref, g_ref, u_ref, a_ref):
        x = x_ref[...]
        gate = jnp.dot(x, wg_ref[...], preferred_element_type=F32)
        up = jnp.dot(x, wu_ref[...], preferred_element_type=F32)
        g_ref[...] = gate.astype(BF16)
        u_ref[...] = up.astype(BF16)
        a_ref[...] = (gate * (1.0 / (1.0 + jnp.exp(-gate))) * up).astype(BF16)

    wspec = pl.BlockSpec((None, d, tf), lambda j, i: (j, 0, 0))
    ospec = pl.BlockSpec((tm, tf), lambda j, i: (i, j))
    osh = jax.ShapeDtypeStruct((s, f), BF16)
    return _pcall(body, (fn, w_gate, w_up), name="ffn_gate_up", grid=(nsp, s // tm),
                  in_specs=[pl.BlockSpec((tm, d), lambda j, i: (i, 0)), wspec, wspec],
                  out_specs=[ospec, ospec, ospec], out_shape=[osh, osh, osh], sem=("parallel", "parallel"), comm=comm)


def _ffn_bwd_act(d_out, w_down, gate, up, *, tm=1024, tf=1408, comm=()):
    s, d = d_out.shape
    f = w_down.shape[0]
    tm, tf = _tile(s, tm), _tile(f, tf)

    sub = tm // 4 if tm % 1024 == 0 else tm

    def body(do_ref, wd_ref, g_ref, u_ref, dg_ref, du_ref):
        groups = [slice(r, r + sub) for r in range(0, tm, sub)]
        parts = [lax.dot_general(do_ref[rows, :].astype(BF16), wd_ref[...], NT_DIMS, preferred_element_type=F32)
                 for rows in groups]
        for rows, d_act in zip(groups, parts):
            gate = g_ref[rows, :].astype(F32)
            sig = 1.0 / (1.0 + jnp.exp(-gate))
            du_ref[rows, :] = (d_act * (gate * sig)).astype(BF16)
            dg_ref[rows, :] = (d_act * u_ref[rows, :].astype(F32) * (sig * (1.0 + gate * (1.0 - sig)))).astype(BF16)

    ospec = pl.BlockSpec((tm, tf), lambda j, i: (i, j))
    osh = jax.ShapeDtypeStruct((s, f), BF16)
    return _pcall(
        body, (d_out, w_down, gate, up), name="ffn_bwd_act", grid=(f // tf, s // tm),
        in_specs=[pl.BlockSpec((tm, d), lambda j, i: (i, 0)), pl.BlockSpec((tf, d), lambda j, i: (j, 0)), ospec, ospec],
        out_specs=[ospec, ospec], out_shape=[osh, osh], sem=("parallel", "parallel"), comm=comm)


def _cols(g4):
    return jnp.concatenate([g4[k] for k in range(N_CHIPS)], axis=1)


def _full_w_in(g4):
    per = IN_WIDTH // N_CHIPS
    kr0 = 2304 - (N_CHIPS - 1) * per
    last = g4[N_CHIPS - 1]
    pad = jnp.zeros((last.shape[0], IN_PAD - IN_WIDTH), last.dtype)
    return jnp.concatenate([g4[0], g4[1], g4[2], last[:, :kr0], last[:, kr0 + 64:], last[:, kr0:kr0 + 64], pad], axis=1)


def _shards_w_in(dwp):
    per = IN_WIDTH // N_CHIPS
    kr0 = 2304 - (N_CHIPS - 1) * per
    last = jnp.concatenate([dwp[:, (N_CHIPS - 1) * per:2304], dwp[:, C_KR:C_KR + 64], dwp[:, 2304:C_KR]], axis=1)
    assert last.shape[1] == per and kr0 == 144
    return jnp.stack([dwp[:, per * k:per * (k + 1)] for k in range(N_CHIPS - 1)] + [last])


def _full_heads(g4, first):
    return jnp.concatenate([g4[k][:, :first] for k in range(N_CHIPS)] + [g4[k][:, first:] for k in range(N_CHIPS)], axis=1)


def _shards_heads(dwp, first, rest):
    base = N_CHIPS * first
    return jnp.stack([jnp.concatenate([dwp[:, first * k:first * (k + 1)], dwp[:, base + rest * k:base + rest * (k + 1)]], axis=1)
                      for k in range(N_CHIPS)])


def _rope_tables(pos):
    inv_freq = ROPE_THETA ** (-jnp.arange(0, MLA_ROPE, 2, dtype=F32) / MLA_ROPE)
    ang = pos.astype(F32)[:, None] * inv_freq
    cos, sin = jnp.cos(ang), jnp.sin(ang)
    return jnp.tile(cos, (1, 4)), jnp.concatenate([-sin, sin, -sin, sin], axis=1)


def _gain_table(sp):
    two = lambda v: jnp.tile(v, (1, 2))
    rows = [two(sp["swa_q_norm_g"]), two(sp["swa_k_norm_g"]), sp["mla_qn_norm_g"], two(sp["mla_qr_norm_g"]),
            sp["mla_kn_norm_g"], two(sp["mla_kr_norm_g"]), sp["mem_q_norm_g"], jnp.zeros((1, LANES), F32)]
    return jnp.concatenate(rows, axis=0)


CHIP_DISTANCES = (1, 2, 3)


def _place():
    x, y, c = lax.axis_index("x"), lax.axis_index("y"), lax.axis_index("c")
    return x, y, c, 2 * x + y


def _chip_at(x, y, d):
    px = 1 - x if d & 2 else x
    py = 1 - y if d & 1 else y
    return px, py, 2 * px + py


def _row_tile(rows, want=512, mult=8):
    t = min(rows, want)
    t -= t % mult
    while rows % t:
        t -= mult
    return t


def _cast_into_slot(w, meta, *, name, comm=()):
    rows, cols = w.shape
    tr = _row_tile(rows, 512, 16)

    def body(meta_ref, w_ref, o_ref):
        o_ref[...] = w_ref[...].astype(BF16)

    return _pcall(body, (w,), name=name, grid=(rows // tr,), prefetch=(meta,),
                  in_specs=[pl.BlockSpec((tr, cols), lambda i, m: (i, 0))],
                  out_specs=pl.BlockSpec((None, tr, cols), lambda i, m: (m[0], i, 0)),
                  out_shape=jax.ShapeDtypeStruct((N_CHIPS, rows, cols), BF16), sem=("parallel",), comm=comm)


def _remote(src, dst, ssem, rsem, i, device):
    return pltpu.make_async_remote_copy(src_ref=src, dst_ref=dst, send_sem=ssem.at[i], recv_sem=rsem.at[i],
                                        device_id=device, device_id_type=MESH)


def _symmetric_stage(ins, out_shapes, aliases, n_sem, copies):
    def issue(i_refs, o_refs, ssem, rsem):
        for send, _ in copies(i_refs, o_refs, ssem, rsem):
            send.start()

    def wait(i_refs, o_refs, ssem, rsem):
        pairs = copies(i_refs, o_refs, ssem, rsem)
        for _, arrival in pairs:
            arrival.wait_recv()
        for send, _ in pairs:
            send.wait_send()

    return _Stage(ins, out_shapes, aliases, n_sem, issue, wait)


def _gather_stage(slots, leg, part=(0, 1)):
    n = len(slots)
    shapes = [jax.ShapeDtypeStruct(s.shape, s.dtype) for s in slots]
    in_place = {w: w for w in range(n)}
    if not isinstance(leg, str):
        legs = list(leg)

        def copies(i_refs, o_refs, ssem, rsem):
            return [pr for k, (which, prt) in enumerate(legs)
                    for pr in _gather_stage(slots, which, prt).leg_copies(which, 3 * n * k)(i_refs, o_refs, ssem, rsem)]

        return _symmetric_stage(slots, shapes, in_place, 3 * n * len(legs), copies)

    def leg_copies(which, base):
        def copies(_, outs, ssem, rsem):
            x, y, c, k_me = _place()
            pairs = []
            for w in range(n):
                half = outs[w].shape[1] // 2
                r0, size = _window(half, part)
                slab = lambda k, cc, w=w, half=half, r0=r0, size=size: outs[w].at[k, pl.ds(cc * half + r0, size)]
                for d in CHIP_DISTANCES:
                    px, py, k_src = _chip_at(x, y, d)
                    i = base + 3 * w + d - 1
                    if which == "ici":
                        pairs.append((_remote(slab(k_me, c), slab(k_me, c), ssem, rsem, i, (px, py, c)),
                                      _remote(slab(k_src, c), slab(k_src, c), ssem, rsem, i, (x, y, c))))
                    else:
                        pairs.append((_remote(slab(k_src, c), slab(k_src, c), ssem, rsem, i, (x, y, 1 - c)),
                                      _remote(slab(k_src, 1 - c), slab(k_src, 1 - c), ssem, rsem, i, (x, y, c))))
            return pairs
        return copies

    if leg != "both":
        st = _symmetric_stage(slots, shapes, in_place, 3 * n, leg_copies(leg, 0))
        st.leg_copies = leg_copies
        return st
    ici = _symmetric_stage(slots, shapes, in_place, 6 * n, leg_copies("ici", 0))
    d2d = _symmetric_stage(slots, shapes, in_place, 6 * n, leg_copies("d2d", 3 * n))

    def mid(*refs):
        ici.wait(*refs)
        d2d.issue(*refs)

    return _Stage(slots, shapes, in_place, 6 * n, ici.issue, d2d.wait, mid)


def _halves_stage(grads):
    n = len(grads)

    def copies(ins, outs, ssem, rsem):
        x, y, c, _ = _place()
        pairs = []
        for w in range(n):
            half = ins[w].shape[1] // 2
            pairs.append((_remote(ins[w].at[:, pl.ds((1 - c) * half, half)], outs[w], ssem, rsem, w, (x, y, 1 - c)),
                          _remote(outs[w], outs[w], ssem, rsem, w, (x, y, c))))
        return pairs

    shapes = [jax.ShapeDtypeStruct((N_CHIPS, g.shape[1] // 2, g.shape[2]), g.dtype) for g in grads]
    return _symmetric_stage(grads, shapes, {}, n, copies)


def _window(rows, part):
    idx, count = part
    size = rows // count
    assert size * count == rows and size % 16 == 0, (rows, part)
    return idx * size, size


def _chips_stage(parts, part=(0, 1), into=None):
    n = len(parts)

    def copies(ins, outs, ssem, rsem):
        x, y, c, _ = _place()
        pairs = []
        for w in range(n):
            r0, size = _window(ins[w].shape[1], part)
            for d in CHIP_DISTANCES:
                px, py, _ = _chip_at(x, y, d)
                i = 3 * w + d - 1
                land = outs[w].at[d - 1, pl.ds(r0, size)]
                pairs.append((_remote(ins[w].at[d - 1, pl.ds(r0, size)], land, ssem, rsem, i, (px, py, c)),
                              _remote(land, land, ssem, rsem, i, (x, y, c))))
        return pairs

    shapes = [jax.ShapeDtypeStruct(p.shape, p.dtype) for p in parts]
    if into is None:
        return _symmetric_stage(parts, shapes, {}, 3 * n, copies)
    return _symmetric_stage(list(parts) + list(into), shapes, {n + w: w for w in range(n)}, 3 * n, copies)


def _swap_stage(totals):
    n = len(totals)

    def copies(ins, outs, ssem, rsem):
        x, y, c, _ = _place()
        return [(_remote(ins[w], outs[w], ssem, rsem, w, (x, y, 1 - c)),
                 _remote(outs[w], outs[w], ssem, rsem, w, (x, y, c))) for w in range(n)]

    shapes = [jax.ShapeDtypeStruct(t.shape, t.dtype) for t in totals]
    return _symmetric_stage(totals, shapes, {}, n, copies)


def _run_stages(stages, *, name):
    n_ins = [len(st.ins) for st in stages]
    n_outs = [len(st.out_shapes) for st in stages]
    tot_in, tot_out = sum(n_ins), sum(n_outs)
    aliases, i0, o0 = {}, 0, 0
    for st, ni, no in zip(stages, n_ins, n_outs):
        aliases.update({i0 + a: o0 + b for a, b in st.aliases.items()})
        i0, o0 = i0 + ni, o0 + no

    def body(*refs):
        sems = refs[tot_in + tot_out:]
        for what in ("issue", "wait"):
            i0, o0 = 0, tot_in
            for k, (st, ni, no) in enumerate(zip(stages, n_ins, n_outs)):
                getattr(st, what)(refs[i0:i0 + ni], refs[o0:o0 + no], sems[2 * k], sems[2 * k + 1])
                i0, o0 = i0 + ni, o0 + no

    sem = pltpu.SemaphoreType.DMA
    res = pl.pallas_call(
        body, name=name, in_specs=[ANY] * tot_in, out_specs=[ANY] * tot_out,
        out_shape=[s for st in stages for s in st.out_shapes], input_output_aliases=aliases,
        scratch_shapes=[sem((st.n_sem,)) for st in stages for _ in range(2)],
    )(*[a for st in stages for a in st.ins])
    outs, o0 = [], 0
    for no in n_outs:
        outs.append(list(res[o0:o0 + no]))
        o0 += no
    return outs


def _add_pair(meta, g4, recv, *, name):
    nsh, rows, cols = g4.shape
    half = rows // 2
    tr = _row_tile(half, 128 if cols > 1024 else 256, 16)
    nt = half // tr

    def body(meta_ref, g0, g1, g2, g3, r0, r1, r2, r3, own_ref, oth_ref):
        own_ref[...] = g0[...] + r0[...]
        for d, (g, r) in enumerate(((g1, r1), (g2, r2), (g3, r3))):
            oth_ref[d] = (g[...] + r[...]).astype(BF16)

    blk = (None, tr, cols)
    gspec = lambda d: pl.BlockSpec(blk, lambda i, m: (jnp.bitwise_xor(m[0], d), m[1] * nt + i, 0))
    rspec = lambda d: pl.BlockSpec(blk, lambda i, m: (jnp.bitwise_xor(m[0], d), i, 0))
    grid_spec = pltpu.PrefetchScalarGridSpec(
        num_scalar_prefetch=1, grid=(nt,),
        in_specs=[gspec(d) for d in range(nsh)] + [rspec(d) for d in range(nsh)],
        out_specs=[pl.BlockSpec((tr, cols), lambda i, m: (i, 0)), pl.BlockSpec((3, tr, cols), lambda i, m: (0, i, 0))])
    return pl.pallas_call(
        body, name=name, grid_spec=grid_spec,
        out_shape=[jax.ShapeDtypeStruct((half, cols), F32), jax.ShapeDtypeStruct((3, half, cols), BF16)],
        compiler_params=_params(("parallel",)),
    )(meta, g4, g4, g4, g4, recv, recv, recv, recv)


def _add_chips(own, recv, *, name):
    half, cols = own.shape
    tr = _row_tile(half, 256, 16)

    def body(p_ref, r_ref, o_ref):
        o_ref[...] = ((p_ref[...] + r_ref[0].astype(F32)) + r_ref[1].astype(F32)) + r_ref[2].astype(F32)

    return pl.pallas_call(
        body, name=name, grid=(half // tr,),
        in_specs=[pl.BlockSpec((tr, cols), lambda i: (i, 0)), pl.BlockSpec((3, tr, cols), lambda i: (0, i, 0))],
        out_specs=pl.BlockSpec((tr, cols), lambda i: (i, 0)),
        out_shape=jax.ShapeDtypeStruct((half, cols), F32),
        compiler_params=_params(("parallel",)),
    )(own, recv)


def _adamw_math(w, g, m, v):
    m = ADAM_B1 * m + (1.0 - ADAM_B1) * g
    v = ADAM_B2 * v + (1.0 - ADAM_B2) * (g * g)
    m_hat = m / (1.0 - ADAM_B1 ** ADAM_STEP)
    v_hat = v / (1.0 - ADAM_B2 ** ADAM_STEP)
    delta = -ADAM_LR * (m_hat / (jnp.sqrt(v_hat) + ADAM_EPS) + ADAM_WD * w)
    return delta, m, v


def _adamw(meta, w, g_mine, g_theirs, m, v, *, name):
    rows, cols = w.shape
    half = rows // 2
    tr = _row_tile(half, 256)
    nt = half // tr

    def body(meta_ref, w_ref, a_ref, b_ref, m_ref, v_ref, g_ref, d_ref, mo_ref, vo_ref):
        is_mine = (pl.program_id(0) // nt) == meta_ref[1]
        g = jnp.where(is_mine, a_ref[...], b_ref[...])
        g_ref[...] = g
        d_ref[...], mo_ref[...], vo_ref[...] = _adamw_math(w_ref[...], g, m_ref[...], v_ref[...])

    blk = pl.BlockSpec((tr, cols), lambda i, mt: (i, 0))
    mine = pl.BlockSpec((tr, cols), lambda i, mt: (jnp.where(i // nt == mt[1], i % nt, 0), 0))
    theirs = pl.BlockSpec((tr, cols), lambda i, mt: (jnp.where(i // nt == mt[1], 0, i % nt), 0))
    sh = jax.ShapeDtypeStruct((rows, cols), F32)
    grid_spec = pltpu.PrefetchScalarGridSpec(
        num_scalar_prefetch=1, grid=(rows // tr,),
        in_specs=[blk, mine, theirs, blk, blk], out_specs=[blk] * 4)
    return pl.pallas_call(
        body, name=name, grid_spec=grid_spec, out_shape=[sh] * 4,
        compiler_params=_params(("arbitrary",)),
    )(meta, w, g_mine, g_theirs, m, v)


N_DEVICES = 8


def _small_step(g_pack, w_pack, m_pack, v_pack):
    rows = g_pack.shape[0]

    def body(g_ref, w_ref, m_ref, v_ref, sum_ref, d_ref, mo_ref, vo_ref, slots, ssem, rsem):
        x, y, c, _ = _place()
        me = 4 * x + 2 * y + c
        slots[me] = g_ref[...]
        copies = []
        for r in range(1, N_DEVICES):
            px = 1 - x if r & 4 else x
            py = 1 - y if r & 2 else y
            pc = 1 - c if r & 1 else c
            copies.append(pltpu.make_async_remote_copy(
                src_ref=g_ref, dst_ref=slots.at[me], send_sem=ssem.at[r - 1], recv_sem=rsem.at[r - 1],
                device_id=(px, py, pc), device_id_type=MESH))
        for cp in copies:
            cp.start()
        for r in range(1, N_DEVICES):
            src = jnp.bitwise_xor(me, r)
            pltpu.make_async_remote_copy(
                src_ref=g_ref, dst_ref=slots.at[src], send_sem=ssem.at[r - 1], recv_sem=rsem.at[r - 1],
                device_id=(x, y, c), device_id_type=MESH).wait_recv()
        for cp in copies:
            cp.wait_send()
        total = slots[0]
        for k in range(1, N_DEVICES):
            total = total + slots[k]
        sum_ref[...] = total
        d_ref[...], mo_ref[...], vo_ref[...] = _adamw_math(w_ref[...], total, m_ref[...], v_ref[...])

    sh = jax.ShapeDtypeStruct((rows, LANES), F32)
    vm = pl.BlockSpec(memory_space=pltpu.VMEM)
    return pl.pallas_call(
        body, name="small_allreduce_adamw",
        in_specs=[vm] * 4, out_specs=[vm] * 4, out_shape=[sh] * 4,
        scratch_shapes=[pltpu.VMEM((N_DEVICES, rows, LANES), F32),
                        pltpu.SemaphoreType.DMA((N_DEVICES - 1,)), pltpu.SemaphoreType.DMA((N_DEVICES - 1,))],
    )(g_pack, w_pack, m_pack, v_pack)


WEIGHTS = ("attn_norm_g", "w_in", "swa_q_norm_g", "swa_k_norm_g", "swa_sinks", "mla_cq_norm_g", "mla_ckv_norm_g",
           "w_uq", "w_ukv", "mla_qn_norm_g", "mla_qr_norm_g", "mla_kn_norm_g", "mla_kr_norm_g", "mem_norm_g",
           "w_mem_kv", "mem_q_norm_g", "mem_k_norm_g", "w_out", "ffn_norm_g", "w_gate", "w_up", "w_down")
BIG = ("w_in", "w_uq", "w_ukv", "w_mem_kv", "w_out", "w_gate", "w_up", "w_down")
SMALL = tuple(n for n in WEIGHTS if n not in BIG)
PACK_UNIT = 8 * LANES


def _pack(parts):
    flat = jnp.concatenate(parts, axis=1)
    total = flat.shape[1]
    padded = -(-total // PACK_UNIT) * PACK_UNIT
    return jnp.pad(flat, ((0, 0), (0, padded - total))).reshape(padded // LANES, LANES)


def _unpack(buf, sizes):
    flat = buf.reshape(1, buf.shape[0] * LANES)
    out, at = [], 0
    for n in sizes:
        out.append(flat[:, at:at + n])
        at += n
    return out


def kernel(x, mem, positions, attn_norm_g, w_in, swa_q_norm_g, swa_k_norm_g, swa_sinks, mla_cq_norm_g, mla_ckv_norm_g, w_uq, w_ukv, mla_qn_norm_g, mla_qr_norm_g, mla_kn_norm_g, mla_kr_norm_g, mem_norm_g, w_mem_kv, mem_q_norm_g, mem_k_norm_g, w_out, ffn_norm_g, w_gate, w_up, w_down, loss_target, m_attn_norm_g, m_w_in, m_swa_q_norm_g, m_swa_k_norm_g, m_swa_sinks, m_mla_cq_norm_g, m_mla_ckv_norm_g, m_w_uq, m_w_ukv, m_mla_qn_norm_g, m_mla_qr_norm_g, m_mla_kn_norm_g, m_mla_kr_norm_g, m_mem_norm_g, m_w_mem_kv, m_mem_q_norm_g, m_mem_k_norm_g, m_w_out, m_ffn_norm_g, m_w_gate, m_w_up, m_w_down, v_attn_norm_g, v_w_in, v_swa_q_norm_g, v_swa_k_norm_g, v_swa_sinks, v_mla_cq_norm_g, v_mla_ckv_norm_g, v_w_uq, v_w_ukv, v_mla_qn_norm_g, v_mla_qr_norm_g, v_mla_kn_norm_g, v_mla_kr_norm_g, v_mem_norm_g, v_w_mem_kv, v_mem_q_norm_g, v_mem_k_norm_g, v_w_out, v_ffn_norm_g, v_w_gate, v_w_up, v_w_down):
    given = dict(locals())
    wts = {n: given[n] for n in WEIGHTS}
    mom_m = {n: given["m_" + n] for n in WEIGHTS}
    mom_v = {n: given["v_" + n] for n in WEIGHTS}

    mx, my, mc = lax.axis_index("x"), lax.axis_index("y"), lax.axis_index("c")
    meta = jnp.stack([2 * mx + my, mc]).astype(jnp.int32)
    x, mem, pos, target = x[0], mem[0], positions[0], loss_target[0]
    sp = {n: wts[n] for n in SMALL}
    s = x.shape[0]
    cos_t, sin_t = _rope_tables(pos)
    pos_f = pos.astype(F32)
    pos_col, pos_row = pos_f.reshape(s, 1), pos_f.reshape(1, s)
    g128 = _gain_table(sp)
    sinks = sp["swa_sinks"].reshape(SWA_Q_HEADS)
    gcq, gckv = sp["mla_cq_norm_g"], sp["mla_ckv_norm_g"]
    gs = {}

    slot = {n: _cast_into_slot(wts[n][0], meta, name="cast_" + n) for n in BIG if n not in ("w_gate", "w_up", "w_down")}
    first = [slot["w_in"], slot["w_uq"], slot["w_ukv"]]
    slot["w_gate"], [first] = _cast_into_slot(wts["w_gate"][0], meta, name="cast_w_gate",
                                              comm=[_gather_stage(first, "ici", (0, 4))])
    slot["w_up"], [first] = _cast_into_slot(wts["w_up"][0], meta, name="cast_w_up",
                                            comm=[_gather_stage(first, [("ici", (1, 4)), ("d2d", (0, 4))])])
    slot["w_down"], [first] = _cast_into_slot(wts["w_down"][0], meta, name="cast_w_down",
                                              comm=[_gather_stage(first, [("ici", (2, 4)), ("d2d", (1, 4))])])
    hn, [first] = _rms_fwd(x, sp["attn_norm_g"], name="attn_norm_fwd",
                           comm=[_gather_stage(first, [("ici", (3, 4)), ("d2d", (2, 4))])])
    [first] = _run_stages([_gather_stage(first, "d2d", (3, 4))], name="gather_first_last_d2d")
    w_in_f, w_uq_f, w_ukv_f = _full_w_in(first[0]), _full_heads(first[1], MLA_NOPE), _full_heads(first[2], MLA_NOPE)

    proj, [mid] = _matmul(hn, w_in_f, name="in_proj",
                          comm=[_gather_stage([slot["w_mem_kv"], slot["w_out"]], "ici")])
    (qa, ka, va, q_cat, k_cat, v_b, qm), [mid, wg] = _attn_prep_fwd(
        proj, g128, gcq, gckv, w_uq_f, w_ukv_f, cos_t, sin_t,
        comm=[_gather_stage(mid, "d2d"), _gather_stage([slot["w_gate"]], "ici", (0, 4))])
    w_mem_kv_f = mid[0].reshape(D_MODEL, 2 * MEM_HEADS * MEM_DIM)
    w_out_f = mid[1].reshape(D_MODEL, D_MODEL)
    mn_b, kv_m, km, vm = _mem_kv_fwd(mem, sp["mem_norm_g"], w_mem_kv_f, sp["mem_k_norm_g"])
    eighths = lambda leg, ks: [(leg, (k, 8)) for k in ks]
    (y_a, y), [wg] = _swa_fwd(qa, ka, va, pos_col, pos_row, sinks, comm=[_gather_stage(wg, eighths("ici", (2, 3, 4, 5)))])
    (y_b, lse, y), [wg, wu] = _mla_fwd(
        q_cat, k_cat, v_b, y, comm=[_gather_stage(wg, eighths("ici", (6, 7))),
                                    _gather_stage([slot["w_up"]], eighths("ici", (0, 1, 2, 3)))])
    y_m, y = _mem_attn_fwd(qm, km, vm, y)
    h1, [wu, wg] = _matmul(y, w_out_f, add=x, name="out_proj",
                           comm=[_gather_stage(wu, eighths("ici", (4, 5, 6, 7)) + eighths("d2d", (0, 1, 2, 3))),
                                 _gather_stage(wg, "d2d")])
    fn, [wu] = _rms_fwd(h1, sp["ffn_norm_g"], name="ffn_norm_fwd",
                        comm=[_gather_stage(wu, eighths("d2d", (4, 5, 6, 7)))])
    w_gate_f, w_up_f = wg[0], wu[0]
    (gate, up, act), [wd] = _ffn_gate_up(fn, w_gate_f, w_up_f, comm=[_gather_stage([slot["w_down"]], "both")])
    w_down_f = wd[0].reshape(D_FF, D_MODEL)
    d_out, d_out_b, loss_tile = _matmul(act, w_down_f, add=h1, name="down_proj", tm=512, tk=D_FF, loss_target=target)

    add_pair = lambda n, g4, r: _add_pair(meta, g4, r, name="grad_add_pair_" + n)
    add_chips = lambda n, own, r: _add_chips(own, r, name="grad_add_chips_" + n)
    mine, theirs = {}, {}

    dw_down = _matmul(act, d_out_b, ta=True, name="dw_down", tm=512, tn=1024, tk=s)
    dw_down = dw_down.reshape(N_CHIPS, D_FF // N_CHIPS, D_MODEL)
    (d_gate, d_up), [[r]] = _ffn_bwd_act(d_out_b, w_down_f, gate, up, comm=[_halves_stage([dw_down])])
    own_d, oth_d = add_pair("w_down", dw_down, r)
    dw_gate, [rd] = _matmul(fn, d_gate, ta=True, name="dw_gate", tm=512, tk=s, tn=D_FF // N_CHIPS, out_split=N_CHIPS,
                            comm=[_chips_stage([oth_d], (0, 2))])
    dw_up, [[r], rd] = _matmul(fn, d_up, ta=True, name="dw_up", tm=512, tk=s, tn=D_FF // N_CHIPS, out_split=N_CHIPS,
                               comm=[_halves_stage([dw_gate]), _chips_stage([oth_d], (1, 2), into=rd)])
    mine["w_down"] = add_chips("w_down", own_d, rd[0])
    own_g, oth_g = add_pair("w_gate", dw_gate, r)
    d_fn, [[r], rg, [theirs["w_down"]]] = _matmul(
        d_gate, w_gate_f, tb=True, b_split=True, pair2=(d_up, w_up_f), name="dfn", tm=512, tn=512,
        comm=[_halves_stage([dw_up]), _chips_stage([oth_g]), _swap_stage([mine["w_down"]])])
    mine["w_gate"] = add_chips("w_gate", own_g, rg[0])
    own_u, oth_u = add_pair("w_up", dw_up, r)
    d_h1, d_h1_b, gs["ffn_norm_g"] = _rms_bwd(d_fn, h1, sp["ffn_norm_g"], d_out, name="ffn_norm_bwd")
    dw_out, [[theirs["w_gate"]]] = _matmul(y, d_h1_b, ta=True, name="dw_out", tm=512, tk=s,
                                           comm=[_swap_stage([mine["w_gate"]])])
    dw_out = dw_out.reshape(N_CHIPS, D_MODEL // N_CHIPS, D_MODEL)
    d_y, [[r]] = _matmul(d_h1_b, w_out_f, tb=True, name="dy", comm=[_halves_stage([dw_out])])
    own_o, oth_o = add_pair("w_out", dw_out, r)
    (d_qa, d_ka, d_va, d_sink), [ru] = _swa_bwd(qa, ka, va, pos_col, pos_row, sinks, y_a, d_y,
                                                comm=[_chips_stage([oth_u], (0, 2))])
    (d_qcat, d_kcat, d_vb), [ru, [r]] = _mla_bwd(
        q_cat, k_cat, v_b, y_b, lse, d_y, comm=[_chips_stage([oth_u], (1, 2), into=ru), _chips_stage([oth_o])])
    mine["w_up"] = add_chips("w_up", own_u, ru[0])
    mine["w_out"] = add_chips("w_out", own_o, r)
    d_qm, d_km, d_vm = _mem_attn_bwd(qm, km, vm, y_m, d_y)
    (d_proj, dw_uq, dw_ukv, dg128, gs["mla_cq_norm_g"], gs["mla_ckv_norm_g"]), [[theirs["w_up"], theirs["w_out"]]] = \
        _attn_prep_bwd(proj, g128, gcq, gckv, w_uq_f, w_ukv_f, cos_t, sin_t, d_qa, d_ka, d_va, d_qcat, d_kcat, d_vb,
                       d_qm, comm=[_swap_stage([mine["w_up"], mine["w_out"]])])
    dw_mem_kv, gs["mem_norm_g"], gs["mem_k_norm_g"] = _mem_kv_bwd(
        mem, sp["mem_norm_g"], w_mem_kv_f, sp["mem_k_norm_g"], mn_b, kv_m, d_km, d_vm)
    late = ("w_uq", "w_ukv", "w_mem_kv")
    late_g = [_shards_heads(dw_uq, MLA_NOPE, MLA_ROPE), _shards_heads(dw_ukv, MLA_NOPE, MLA_V),
              dw_mem_kv.reshape(N_CHIPS, D_MODEL // N_CHIPS, -1)]
    dw_in, [rs] = _matmul(hn, d_proj, ta=True, name="dw_in", tm=512, tk=s, comm=[_halves_stage(late_g)])
    late_sums = [add_pair(n, g4, r) for n, g4, r in zip(late, late_g, rs)]
    dw_in = _shards_w_in(dw_in)
    d_hn, [rs, [r]] = _matmul(d_proj, w_in_f, tb=True, name="dhn", tk=1536,
                              comm=[_chips_stage([oth for _, oth in late_sums]), _halves_stage([dw_in])])
    for n, (own, _), r_n in zip(late, late_sums, rs):
        mine[n] = add_chips(n, own, r_n)
    own_i, oth_i = add_pair("w_in", dw_in, r)
    (grad_x, _, gs["attn_norm_g"]), [[r], late_theirs] = _rms_bwd(
        d_hn, x, sp["attn_norm_g"], d_h1, name="attn_norm_bwd",
        comm=[_chips_stage([oth_i]), _swap_stage([mine[n] for n in late])])
    theirs.update(zip(late, late_theirs))
    mine["w_in"] = add_chips("w_in", own_i, r)
    [[theirs["w_in"]]] = _run_stages([_swap_stage([mine["w_in"]])], name="grad_swap_w_in")

    fold = lambda r: r[:, :64] + r[:, 64:]
    gs["swa_q_norm_g"] = fold(dg128[G_SWA_Q:G_SWA_Q + 1])
    gs["swa_k_norm_g"] = fold(dg128[G_SWA_K:G_SWA_K + 1])
    gs["mla_qn_norm_g"] = dg128[G_QN:G_QN + 1]
    gs["mla_qr_norm_g"] = fold(dg128[G_QR:G_QR + 1])
    gs["mla_kn_norm_g"] = dg128[G_KN:G_KN + 1]
    gs["mla_kr_norm_g"] = fold(dg128[G_KR:G_KR + 1])
    gs["mem_q_norm_g"] = dg128[G_MQ:G_MQ + 1]
    gs["swa_sinks"] = d_sink[:, :SWA_Q_HEADS]

    grad, delta, new_m, new_v = {}, {}, {}, {}
    for n in BIG:
        g2, d, m2, v2 = _adamw(meta, wts[n][0], mine[n], theirs[n], mom_m[n][0], mom_v[n][0], name="adamw_" + n)
        grad[n], delta[n], new_m[n], new_v[n] = g2[None], d[None], m2[None], v2[None]

    sizes = [wts[n].shape[1] for n in SMALL]
    zero = jnp.zeros((1, LANES), F32)
    packs = _small_step(_pack([gs[n] for n in SMALL] + [loss_tile]), _pack([wts[n] for n in SMALL] + [zero]),
                        _pack([mom_m[n] for n in SMALL] + [zero]), _pack([mom_v[n] for n in SMALL] + [zero]))
    for store, buf in zip((grad, delta, new_m, new_v), packs):
        for n, val in zip(SMALL, _unpack(buf, sizes)):
            store[n] = val
    loss = _unpack(packs[0], sizes + [LANES])[-1][0, 0]

    return (loss, grad_x[None], *[grad[n] for n in WEIGHTS], *[delta[n] for n in WEIGHTS],
            *[new_m[n] for n in WEIGHTS], *[new_v[n] for n in WEIGHTS])
```

```python
import functools
import math

import jax
import jax.numpy as jnp
from jax import lax
from jax.experimental import pallas as pl
from jax.experimental.pallas import tpu as pltpu

F32 = jnp.float32
BF16 = jnp.bfloat16

D_MODEL = 2048
BLOCK = 128
EPS = 1e-6
NEG_INF = -1e30
SWA_Q_HEADS = 16
SWA_KV_HEADS = 2
SWA_HEAD_DIM = 64
MLA_HEADS = 4
MLA_RANK = 512
MLA_NOPE = 128
MLA_ROPE = 64
MLA_V = 128
ROPE_THETA = 10000.0
MEM_HEADS = 4
MEM_DIM = 128
D_FF = 5632
IN_WIDTH = 2880
IN_PAD = 3072
N_CHIPS = 4

ADAM_LR = 0.001
ADAM_B1 = 0.9
ADAM_B2 = 0.999
ADAM_EPS = 1e-08
ADAM_WD = 0.01
ADAM_STEP = 10

VMEM_LIMIT_BYTES = 56 * 1024 * 1024
LANES = 128

MESH = pl.DeviceIdType.MESH


def _params(sem=None, **kw):
    return pltpu.CompilerParams(dimension_semantics=sem, vmem_limit_bytes=VMEM_LIMIT_BYTES, **kw)


def _tile(n, want):
    if n <= want:
        return n
    t = want - want % LANES
    while t > 0:
        if n % t == 0:
            return t
        t -= LANES
    return n


ANY = pl.BlockSpec(memory_space=pl.ANY)


class _Stage:
    def __init__(self, ins, out_shapes, aliases, n_sem, issue, wait, mid=None):
        self.ins, self.out_shapes, self.aliases, self.n_sem = list(ins), list(out_shapes), dict(aliases), n_sem
        self.issue, self.wait, self.mid = issue, wait, mid


def _pcall(body, args, *, name, grid, in_specs, out_specs, out_shape, scratch_shapes=(), sem=None, comm=(),
           prefetch=(), io_alias=None):
    multi = isinstance(out_shape, (list, tuple))
    out_specs_l = list(out_specs) if multi else [out_specs]
    out_shape_l = list(out_shape) if multi else [out_shape]
    npf = len(prefetch)
    own_aliases = {npf + a: o for a, o in (io_alias or {}).items()}

    def call(fn, in_specs_, out_specs_, out_shape_, scratch_, operands, sem_, aliases=None):
        kw = dict(name=name, out_shape=out_shape_, compiler_params=_params(sem_))
        if aliases:
            kw["input_output_aliases"] = aliases
        if npf:
            spec = pltpu.PrefetchScalarGridSpec(num_scalar_prefetch=npf, grid=grid, in_specs=in_specs_,
                                                out_specs=out_specs_, scratch_shapes=scratch_)
            return pl.pallas_call(fn, grid_spec=spec, **kw)(*prefetch, *operands)
        return pl.pallas_call(fn, grid=grid, in_specs=in_specs_, out_specs=out_specs_, scratch_shapes=scratch_,
                              **kw)(*operands)

    if not comm:
        return call(body, list(in_specs), out_specs, out_shape, list(scratch_shapes), args, sem, own_aliases)
    n_in, n_out, n_scr = len(in_specs), len(out_specs_l), len(scratch_shapes)
    cins = [a for st in comm for a in st.ins]
    couts = [s for st in comm for s in st.out_shapes]
    aliases, ci, co = dict(own_aliases), 0, 0
    for st in comm:
        for a_i, o_i in st.aliases.items():
            aliases[npf + n_in + ci + a_i] = n_out + co + o_i
        ci, co = ci + len(st.ins), co + len(st.out_shapes)

    def wrapped(*refs):
        pre = refs[:npf]
        p = npf
        ins = refs[p:p + n_in]; p += n_in
        cin_refs = refs[p:p + len(cins)]; p += len(cins)
        outs = refs[p:p + n_out]; p += n_out
        cout_refs = refs[p:p + len(couts)]; p += len(couts)
        scr = refs[p:p + n_scr]; p += n_scr
        sems = refs[p:]
        first = functools.reduce(jnp.logical_and, [pl.program_id(a) == 0 for a in range(len(grid))])
        last = functools.reduce(jnp.logical_and, [pl.program_id(a) == grid[a] - 1 for a in range(len(grid))])

        def each(what):
            i, o = 0, 0
            for k, st in enumerate(comm):
                fn = getattr(st, what)
                if fn is not None:
                    fn(cin_refs[i:i + len(st.ins)], cout_refs[o:o + len(st.out_shapes)], sems[2 * k], sems[2 * k + 1])
                i, o = i + len(st.ins), o + len(st.out_shapes)

        @pl.when(first)
        def _():
            each("issue")

        if any(st.mid is not None for st in comm):
            n_steps = math.prod(grid)
            assert n_steps >= 4, "a two-leg stage needs a carrier with several grid steps"
            lin = functools.reduce(lambda acc, a: acc * grid[a] + pl.program_id(a), range(len(grid)), 0)

            @pl.when(lin == (3 * n_steps) // 4)
            def _():
                each("mid")

        body(*pre, *ins, *outs, *scr)

        @pl.when(last)
        def _():
            each("wait")

    sem_scr = [pltpu.SemaphoreType.DMA((st.n_sem,)) for st in comm for _ in range(2)]
    res = call(wrapped, list(in_specs) + [ANY] * len(cins), out_specs_l + [ANY] * len(couts), out_shape_l + couts,
               list(scratch_shapes) + sem_scr, (*args, *cins), ("arbitrary",) * len(grid), aliases)
    normal = list(res[:n_out])
    stage_outs, o = [], n_out
    for st in comm:
        stage_outs.append(list(res[o:o + len(st.out_shapes)]))
        o += len(st.out_shapes)
    return (normal if multi else normal[0]), stage_outs


def _matmul(a, b, *, name, ta=False, tb=False, add=None, out_dtype=F32, tm=1024, tn=1024, tk=2048,
            b_split=False, out_split=0, comm=(), loss_target=None, pair2=None):
    if ta:
        kdim, m = a.shape
    else:
        m, kdim = a.shape
    if b_split:
        assert tb
        nsp, n, kb = b.shape
        kb = kb * nsp
    elif tb:
        n, kb = b.shape
    else:
        kb, n = b.shape
    assert kb == kdim, (a.shape, b.shape, ta, tb)
    if b_split:
        tk = kdim
    if out_split:
        tn = _tile(n // out_split, tn)
    tm, tn, tk = _tile(m, tm), _tile(n, tn), _tile(kdim, tk)
    nk = kdim // tk
    dims = (((0 if ta else 1,), (1 if tb else 0,)), ((), ()))

    def product(a_ref, b_ref):
        if not b_split:
            return lax.dot_general(a_ref[...].astype(BF16), b_ref[...].astype(BF16), dims, preferred_element_type=F32)
        per = kdim // nsp
        return sum(lax.dot_general(a_ref[:, per * c:per * (c + 1)].astype(BF16), b_ref[c].astype(BF16), dims,
                                   preferred_element_type=F32) for c in range(nsp))

    def body(*refs):
        a_ref, b_ref = refs[:2]
        n_ab = 4 if pair2 is not None else 2
        add_ref = refs[n_ab] if add is not None else None
        n_in = n_ab + (add is not None) + (loss_target is not None)

        def products():
            r = product(a_ref, b_ref)
            return r if pair2 is None else r + product(refs[2], refs[3])
        o_ref = refs[n_in]

        def finish(r):
            if add_ref is not None:
                r = r + add_ref[...].astype(F32)
            if loss_target is None:
                o_ref[...] = r.astype(o_ref.dtype)
                return
            db_ref, l_ref = refs[n_in + 1], refs[n_in + 2]
            err = r - refs[n_in - 1][...]
            d_out = err * (1.0 / n)
            o_ref[...] = d_out
            db_ref[...] = d_out.astype(BF16)
            part = jnp.broadcast_to((0.5 / n) * jnp.sum(jnp.sum(err * err, axis=-1, keepdims=True), axis=0, keepdims=True),
                                    (1, LANES))
            first = jnp.logical_and(pl.program_id(0) == 0, pl.program_id(1) == 0)

            @pl.when(first)
            def _():
                l_ref[...] = part

            @pl.when(jnp.logical_not(first))
            def _():
                l_ref[...] += part

        if nk == 1:
            finish(products())
            return
        acc_ref = refs[-1]
        k = pl.program_id(2)
        part = products()

        @pl.when(k == 0)
        def _():
            acc_ref[...] = part

        @pl.when(k > 0)
        def _():
            acc_ref[...] += part

        @pl.when(k == nk - 1)
        def _():
            finish(acc_ref[...])

    a_spec = pl.BlockSpec((tk, tm), lambda i, j, k: (k, i)) if ta else pl.BlockSpec((tm, tk), lambda i, j, k: (i, k))
    if b_split:
        b_spec = pl.BlockSpec((nsp, tn, kdim // nsp), lambda i, j, k: (0, j, 0))
    elif tb:
        b_spec = pl.BlockSpec((tn, tk), lambda i, j, k: (j, k))
    else:
        b_spec = pl.BlockSpec((tk, tn), lambda i, j, k: (k, j))
    in_specs = [a_spec, b_spec]
    args = [a, b]
    if pair2 is not None:
        assert pair2[0].shape == a.shape and pair2[1].shape == b.shape
        in_specs += [a_spec, b_spec]
        args += list(pair2)
    if add is not None:
        in_specs.append(pl.BlockSpec((tm, tn), lambda i, j, k: (i, j)))
        args.append(add)
    tile = pl.BlockSpec((tm, tn), lambda i, j, k: (i, j))
    sem = ("parallel", "parallel", "arbitrary")
    if out_split:
        per = (n // out_split) // tn
        out_spec = pl.BlockSpec((None, tm, tn), lambda i, j, k: (j // per, i, j % per))
        out_shape = jax.ShapeDtypeStruct((out_split, m, n // out_split), out_dtype)
    elif loss_target is not None:
        in_specs.append(tile)
        args.append(loss_target)
        out_spec = [tile, tile, pl.BlockSpec((1, LANES), lambda i, j, k: (0, 0))]
        out_shape = [jax.ShapeDtypeStruct((m, n), F32), jax.ShapeDtypeStruct((m, n), BF16),
                     jax.ShapeDtypeStruct((1, LANES), F32)]
        sem = ("arbitrary",) * 3
    else:
        out_spec = tile
        out_shape = jax.ShapeDtypeStruct((m, n), out_dtype)
    return _pcall(body, args, name=name, grid=(m // tm, n // tn, nk), in_specs=in_specs, out_specs=out_spec,
                  out_shape=out_shape, scratch_shapes=[pltpu.VMEM((tm, tn), F32)] if nk > 1 else [],
                  sem=sem, comm=comm)


def _rms_fwd(x, g, *, name, tm=512, comm=()):
    s, d = x.shape
    tm = _tile(s, tm)

    def body(x_ref, g_ref, o_ref):
        xv = x_ref[...]
        r = lax.rsqrt(jnp.mean(xv * xv, axis=-1, keepdims=True) + EPS)
        o_ref[...] = (xv * r * g_ref[...]).astype(o_ref.dtype)

    return _pcall(body, (x, g), name=name, grid=(s // tm,),
                  in_specs=[pl.BlockSpec((tm, d), lambda i: (i, 0)), pl.BlockSpec((1, d), lambda i: (0, 0))],
                  out_specs=pl.BlockSpec((tm, d), lambda i: (i, 0)),
                  out_shape=jax.ShapeDtypeStruct((s, d), BF16), sem=("parallel",), comm=comm)


def _rms_bwd(dy, x, g, res, *, name, tm=512, comm=()):
    s, d = x.shape
    tm = _tile(s, tm)

    def body(dy_ref, x_ref, g_ref, res_ref, dx_ref, dxb_ref, dg_ref):
        xv = x_ref[...]
        dyv = dy_ref[...]
        r = lax.rsqrt(jnp.mean(xv * xv, axis=-1, keepdims=True) + EPS)
        xhat = xv * r
        dyg = dyv * g_ref[...]
        mt = jnp.mean(dyg * xhat, axis=-1, keepdims=True)
        dx = res_ref[...] + r * (dyg - xhat * mt)
        dx_ref[...] = dx
        dxb_ref[...] = dx.astype(BF16)
        part = jnp.sum(dyv * xhat, axis=0, keepdims=True)

        @pl.when(pl.program_id(0) == 0)
        def _():
            dg_ref[...] = part

        @pl.when(pl.program_id(0) > 0)
        def _():
            dg_ref[...] += part

    row = pl.BlockSpec((tm, d), lambda i: (i, 0))
    vec = pl.BlockSpec((1, d), lambda i: (0, 0))
    return _pcall(body, (dy, x, g, res), name=name, grid=(s // tm,), in_specs=[row, row, vec, row],
                  out_specs=[row, row, vec],
                  out_shape=[jax.ShapeDtypeStruct((s, d), F32), jax.ShapeDtypeStruct((s, d), BF16),
                             jax.ShapeDtypeStruct((1, d), F32)],
                  sem=("arbitrary",), comm=comm)


def _lane(shape):
    return lax.broadcasted_iota(jnp.int32, shape, 1)


def _halfsum(t, lo):
    s_lo = jnp.sum(jnp.where(lo, t, 0.0), axis=-1, keepdims=True)
    s_hi = jnp.sum(jnp.where(lo, 0.0, t), axis=-1, keepdims=True)
    return jnp.where(lo, s_lo, s_hi)


def _norm_pair(x, g, lo):
    r = lax.rsqrt(_halfsum(x * x, lo) * (1.0 / 64.0) + EPS)
    xhat = x * r
    return xhat * g, xhat, r


def _norm_pair_bwd(dy, g, xhat, r, lo):
    dyg = dy * g
    mt = _halfsum(dyg * xhat, lo) * (1.0 / 64.0)
    return r * (dyg - xhat * mt), jnp.sum(dy * xhat, axis=0, keepdims=True)


def _norm_full(x, g):
    r = lax.rsqrt(jnp.mean(x * x, axis=-1, keepdims=True) + EPS)
    xhat = x * r
    return xhat * g, xhat, r


def _norm_full_bwd(dy, g, xhat, r):
    dyg = dy * g
    mt = jnp.mean(dyg * xhat, axis=-1, keepdims=True)
    return r * (dyg - xhat * mt), jnp.sum(dy * xhat, axis=0, keepdims=True)


def _rot(x, first32):
    return jnp.where(first32, pltpu.roll(x, 96, axis=1), pltpu.roll(x, 32, axis=1))


def _rope(x, cos_t, sin_t, first32):
    return x * cos_t + _rot(x, first32) * sin_t


def _rope_bwd(dy, cos_t, sin_t, first32):
    return dy * cos_t + _rot(dy * sin_t, first32)


G_SWA_Q, G_SWA_K, G_QN, G_QR, G_KN, G_KR, G_MQ = range(7)

C_QA, C_KA, C_VA, C_CQ, C_CKV, C_QM, C_KR = 0, 1024, 1152, 1280, 1792, 2304, 2816


def _prep_common(p_ref, g128_ref, gcq_ref, gckv_ref, wuq_ref, wukv_ref, cos_ref, sin_ref):
    tm = p_ref.shape[0]
    lane = _lane((tm, LANES))
    lo = lane < 64
    first32 = (lane % 64) < 32
    cos_t = cos_ref[...]
    sin_t = sin_ref[...]
    g = lambda row: g128_ref[row:row + 1, :]
    out = dict(lo=lo, first32=first32, cos_t=cos_t, sin_t=sin_t, lane=lane)
    cq_n, cq_hat, cq_r = _norm_full(p_ref[:, C_CQ:C_CQ + MLA_RANK], gcq_ref[...])
    ckv_n, ckv_hat, ckv_r = _norm_full(p_ref[:, C_CKV:C_CKV + MLA_RANK], gckv_ref[...])
    cq_b = cq_n.astype(BF16)
    ckv_b = ckv_n.astype(BF16)
    q_b = jnp.dot(cq_b, wuq_ref[...], preferred_element_type=F32)
    kv_b = jnp.dot(ckv_b, wukv_ref[...], preferred_element_type=F32)
    out.update(cq_b=cq_b, cq_hat=cq_hat, cq_r=cq_r, ckv_b=ckv_b, ckv_hat=ckv_hat, ckv_r=ckv_r, q_b=q_b, kv_b=kv_b, g=g)
    return out


def _attn_prep_fwd(proj, g128, gcq, gckv, wuq, wukv, cos_t, sin_t, *, tm=512, comm=()):
    s = proj.shape[0]
    tm = _tile(s, tm)

    def body(p_ref, g128_ref, gcq_ref, gckv_ref, wuq_ref, wukv_ref, cos_ref, sin_ref,
             qa_ref, ka_ref, va_ref, qcat_ref, kcat_ref, vb_ref, qm_ref):
        c = _prep_common(p_ref, g128_ref, gcq_ref, gckv_ref, wuq_ref, wukv_ref, cos_ref, sin_ref)
        lo, first32, g = c["lo"], c["first32"], c["g"]
        for j in range(SWA_Q_HEADS // 2):
            y, _, _ = _norm_pair(p_ref[:, C_QA + 128 * j:C_QA + 128 * (j + 1)], g(G_SWA_Q), lo)
            qa_ref[:, 128 * j:128 * (j + 1)] = y.astype(BF16)
        y, _, _ = _norm_pair(p_ref[:, C_KA:C_KA + 128], g(G_SWA_K), lo)
        ka_ref[...] = y.astype(BF16)
        va_ref[...] = p_ref[:, C_VA:C_VA + 128].astype(BF16)
        kr, _, _ = _norm_pair(p_ref[:, C_KR:C_KR + 128], g(G_KR), lo)
        kr = jnp.where(lo, _rope(kr, c["cos_t"], c["sin_t"], first32), 0.0)
        krkr = (kr + pltpu.roll(kr, 64, axis=1)).astype(BF16)
        q_b, kv_b = c["q_b"], c["kv_b"]
        qr = []
        for j in range(MLA_HEADS // 2):
            y, _, _ = _norm_pair(q_b[:, 512 + 128 * j:512 + 128 * (j + 1)], g(G_QR), lo)
            qr.append(_rope(y, c["cos_t"], c["sin_t"], first32))
        for h in range(MLA_HEADS):
            qn, _, _ = _norm_full(q_b[:, 128 * h:128 * (h + 1)], g(G_QN))
            keep = lo if h % 2 == 0 else jnp.logical_not(lo)
            qcat_ref[h, :, 0:128] = qn.astype(BF16)
            qcat_ref[h, :, 128:256] = jnp.where(keep, qr[h // 2], 0.0).astype(BF16)
            kn, _, _ = _norm_full(kv_b[:, 128 * h:128 * (h + 1)], g(G_KN))
            kcat_ref[h, :, 0:128] = kn.astype(BF16)
            kcat_ref[h, :, 128:256] = krkr
        vb_ref[...] = kv_b[:, 512:1024].astype(BF16)
        for h in range(MEM_HEADS):
            y, _, _ = _norm_full(p_ref[:, C_QM + 128 * h:C_QM + 128 * (h + 1)], g(G_MQ))
            qm_ref[:, 128 * h:128 * (h + 1)] = y.astype(BF16)

    row = lambda w: pl.BlockSpec((tm, w), lambda i: (i, 0))
    full = lambda shape: pl.BlockSpec(shape, lambda i: tuple(0 for _ in shape))
    cat = pl.BlockSpec((MLA_HEADS, tm, 256), lambda i: (0, i, 0))
    return _pcall(
        body, (proj, g128, gcq, gckv, wuq, wukv, cos_t, sin_t), name="attn_prep_fwd", grid=(s // tm,),
        in_specs=[row(IN_PAD), full((8, 128)), full((1, 512)), full((1, 512)), full((512, 768)), full((512, 1024)),
                  row(128), row(128)],
        out_specs=[row(1024), row(128), row(128), cat, cat, row(512), row(512)],
        out_shape=[jax.ShapeDtypeStruct((s, 1024), BF16), jax.ShapeDtypeStruct((s, 128), BF16),
                   jax.ShapeDtypeStruct((s, 128), BF16), jax.ShapeDtypeStruct((MLA_HEADS, s, 256), BF16),
                   jax.ShapeDtypeStruct((MLA_HEADS, s, 256), BF16), jax.ShapeDtypeStruct((s, 512), BF16),
                   jax.ShapeDtypeStruct((s, 512), BF16)],
        sem=("parallel",), comm=comm)


def _attn_prep_bwd(proj, g128, gcq, gckv, wuq, wukv, cos_t, sin_t,
                   d_qa, d_ka, d_va, d_qcat, d_kcat, d_vb, d_qm, *, tm=512, comm=()):
    s = proj.shape[0]
    tm = _tile(s, tm)

    def body(p_ref, g128_ref, gcq_ref, gckv_ref, wuq_ref, wukv_ref, cos_ref, sin_ref,
             dqa_ref, dka_ref, dva_ref, dqcat_ref, dkcat_ref, dvb_ref, dqm_ref,
             dp_ref, dwuq_ref, dwukv_ref, dg128_ref, dgcq_ref, dgckv_ref):
        c = _prep_common(p_ref, g128_ref, gcq_ref, gckv_ref, wuq_ref, wukv_ref, cos_ref, sin_ref)
        lo, first32, g = c["lo"], c["first32"], c["g"]
        cos_v, sin_v = c["cos_t"], c["sin_t"]
        q_b, kv_b = c["q_b"], c["kv_b"]
        zero_row = jnp.zeros((1, LANES), F32)
        dg = {k: zero_row for k in range(7)}

        for j in range(SWA_Q_HEADS // 2):
            sl = slice(C_QA + 128 * j, C_QA + 128 * (j + 1))
            _, xhat, r = _norm_pair(p_ref[:, sl], g(G_SWA_Q), lo)
            dx, dgj = _norm_pair_bwd(dqa_ref[:, 128 * j:128 * (j + 1)], g(G_SWA_Q), xhat, r, lo)
            dp_ref[:, sl] = dx.astype(BF16)
            dg[G_SWA_Q] = dg[G_SWA_Q] + dgj
        _, xhat, r = _norm_pair(p_ref[:, C_KA:C_KA + 128], g(G_SWA_K), lo)
        dx, dgj = _norm_pair_bwd(dka_ref[...], g(G_SWA_K), xhat, r, lo)
        dp_ref[:, C_KA:C_KA + 128] = dx.astype(BF16)
        dg[G_SWA_K] = dgj
        dp_ref[:, C_VA:C_VA + 128] = dva_ref[...].astype(BF16)

        dqb_parts = [None] * 6
        for h in range(MLA_HEADS):
            _, xhat, r = _norm_full(q_b[:, 128 * h:128 * (h + 1)], g(G_QN))
            dx, dgj = _norm_full_bwd(dqcat_ref[h, :, 0:128], g(G_QN), xhat, r)
            dqb_parts[h] = dx
            dg[G_QN] = dg[G_QN] + dgj
        for j in range(MLA_HEADS // 2):
            _, xhat, r = _norm_pair(q_b[:, 512 + 128 * j:512 + 128 * (j + 1)], g(G_QR), lo)
            d_rot = jnp.where(lo, dqcat_ref[2 * j, :, 128:256], dqcat_ref[2 * j + 1, :, 128:256])
            d_y = _rope_bwd(d_rot, cos_v, sin_v, first32)
            dx, dgj = _norm_pair_bwd(d_y, g(G_QR), xhat, r, lo)
            dqb_parts[4 + j] = dx
            dg[G_QR] = dg[G_QR] + dgj
        d_qb = jnp.concatenate(dqb_parts, axis=1).astype(BF16)
        dwuq = lax.dot_general(c["cq_b"], d_qb, (((0,), (0,)), ((), ())), preferred_element_type=F32)
        d_cqn = lax.dot_general(d_qb, wuq_ref[...], (((1,), (1,)), ((), ())), preferred_element_type=F32)
        dx, dgcq = _norm_full_bwd(d_cqn, gcq_ref[...], c["cq_hat"], c["cq_r"])
        dp_ref[:, C_CQ:C_CQ + MLA_RANK] = dx.astype(BF16)

        dkv_parts = []
        d_krkr = jnp.zeros((p_ref.shape[0], LANES), F32)
        for h in range(MLA_HEADS):
            _, xhat, r = _norm_full(kv_b[:, 128 * h:128 * (h + 1)], g(G_KN))
            dx, dgj = _norm_full_bwd(dkcat_ref[h, :, 0:128], g(G_KN), xhat, r)
            dkv_parts.append(dx)
            dg[G_KN] = dg[G_KN] + dgj
            d_krkr = d_krkr + dkcat_ref[h, :, 128:256]
        d_kvb = jnp.concatenate(dkv_parts + [dvb_ref[...]], axis=1).astype(BF16)
        dwukv = lax.dot_general(c["ckv_b"], d_kvb, (((0,), (0,)), ((), ())), preferred_element_type=F32)
        d_ckvn = lax.dot_general(d_kvb, wukv_ref[...], (((1,), (1,)), ((), ())), preferred_element_type=F32)
        dx, dgckv = _norm_full_bwd(d_ckvn, gckv_ref[...], c["ckv_hat"], c["ckv_r"])
        dp_ref[:, C_CKV:C_CKV + MLA_RANK] = dx.astype(BF16)

        _, xhat, r = _norm_pair(p_ref[:, C_KR:C_KR + 128], g(G_KR), lo)
        d_kr = jnp.where(lo, d_krkr + pltpu.roll(d_krkr, 64, axis=1), 0.0)
        d_y = jnp.where(lo, _rope_bwd(d_kr, cos_v, sin_v, first32), 0.0)
        dx, dgj = _norm_pair_bwd(d_y, g(G_KR), xhat, r, lo)
        dp_ref[:, C_KR:C_KR + 128] = jnp.where(lo, dx, 0.0).astype(BF16)
        dp_ref[:, C_KR + 128:] = jnp.zeros((p_ref.shape[0], IN_PAD - C_KR - 128), BF16)
        dg[G_KR] = dgj

        for h in range(MEM_HEADS):
            sl = slice(C_QM + 128 * h, C_QM + 128 * (h + 1))
            _, xhat, r = _norm_full(p_ref[:, sl], g(G_MQ))
            dx, dgj = _norm_full_bwd(dqm_ref[:, 128 * h:128 * (h + 1)], g(G_MQ), xhat, r)
            dp_ref[:, sl] = dx.astype(BF16)
            dg[G_MQ] = dg[G_MQ] + dgj

        dg_tile = jnp.concatenate([dg[k] for k in range(7)] + [zero_row], axis=0)

        @pl.when(pl.program_id(0) == 0)
        def _():
            dwuq_ref[...] = dwuq
            dwukv_ref[...] = dwukv
            dg128_ref[...] = dg_tile
            dgcq_ref[...] = dgcq
            dgckv_ref[...] = dgckv

        @pl.when(pl.program_id(0) > 0)
        def _():
            dwuq_ref[...] += dwuq
            dwukv_ref[...] += dwukv
            dg128_ref[...] += dg_tile
            dgcq_ref[...] += dgcq
            dgckv_ref[...] += dgckv

    row = lambda w: pl.BlockSpec((tm, w), lambda i: (i, 0))
    full = lambda shape: pl.BlockSpec(shape, lambda i: tuple(0 for _ in shape))
    cat = pl.BlockSpec((MLA_HEADS, tm, 256), lambda i: (0, i, 0))
    return _pcall(
        body, (proj, g128, gcq, gckv, wuq, wukv, cos_t, sin_t, d_qa, d_ka, d_va, d_qcat, d_kcat, d_vb, d_qm),
        name="attn_prep_bwd", grid=(s // tm,),
        in_specs=[row(IN_PAD), full((8, 128)), full((1, 512)), full((1, 512)), full((512, 768)), full((512, 1024)),
                  row(128), row(128),
                  row(1024), row(128), row(128), cat, cat, row(512), row(512)],
        out_specs=[row(IN_PAD), full((512, 768)), full((512, 1024)), full((8, 128)), full((1, 512)), full((1, 512))],
        out_shape=[jax.ShapeDtypeStruct((s, IN_PAD), BF16), jax.ShapeDtypeStruct((512, 768), F32),
                   jax.ShapeDtypeStruct((512, 1024), F32), jax.ShapeDtypeStruct((8, 128), F32),
                   jax.ShapeDtypeStruct((1, 512), F32), jax.ShapeDtypeStruct((1, 512), F32)],
        sem=("arbitrary",), comm=comm)


SWA_SLOPES = tuple(2.0 ** (-8.0 * h / SWA_Q_HEADS) for h in range(1, SWA_Q_HEADS + 1))
SWA_SCALE = SWA_HEAD_DIM ** -0.5
NT_DIMS = (((1,), (1,)), ((), ()))
TN_DIMS = (((0,), (0,)), ((), ()))


def _swa_span(n, kp_ref, kc_ref, vp_ref, vc_ref, pcol_ref, pprow_ref, pcrow_ref):
    k_span = jnp.concatenate([kp_ref[...], kc_ref[...]], axis=0).astype(F32)
    v_span = jnp.concatenate([vp_ref[...], vc_ref[...]], axis=0).astype(F32)
    lo = _lane((2 * BLOCK, LANES)) < 64
    k_sw = pltpu.roll(k_span, 64, axis=1)
    v_sw = pltpu.roll(v_span, 64, axis=1)
    kk = (jnp.where(lo, k_span, k_sw).astype(BF16), jnp.where(lo, k_sw, k_span).astype(BF16))
    vv_lo = (jnp.where(lo, v_span, 0.0).astype(BF16), jnp.where(lo, v_sw, 0.0).astype(BF16))
    vv_hi = (jnp.where(lo, 0.0, v_sw).astype(BF16), jnp.where(lo, 0.0, v_span).astype(BF16))
    pk = jnp.concatenate([pprow_ref[...], pcrow_ref[...]], axis=1)
    dist = jnp.abs(pcol_ref[...] - pk)
    qi = lax.broadcasted_iota(jnp.int32, (BLOCK, 2 * BLOCK), 0)
    ki = lax.broadcasted_iota(jnp.int32, (BLOCK, 2 * BLOCK), 1)
    first_key = jnp.where(n > 0, qi + 1, jnp.maximum(qi + 1, BLOCK))
    valid = jnp.logical_and(ki >= first_key, ki <= qi + BLOCK)
    mask_add = jnp.where(valid, 0.0, NEG_INF)
    return kk, vv_lo, vv_hi, dist, mask_add


def _swa_heads(q_ref, lo):
    heads = []
    for j in range(SWA_Q_HEADS // 2):
        q_pair = q_ref[:, 128 * j:128 * (j + 1)].astype(F32)
        for par in (0, 1):
            q_h = jnp.where(lo if par == 0 else jnp.logical_not(lo), q_pair, 0.0).astype(BF16)
            heads.append((2 * j + par, (2 * j) // (SWA_Q_HEADS // SWA_KV_HEADS), par, q_h))
    return heads


def _swa_probs(raw, dist, mask_add, slope, sink):
    s = raw * SWA_SCALE - slope * dist + mask_add
    m = jnp.maximum(jnp.max(s, axis=-1, keepdims=True), sink)
    e = jnp.exp(s - m)
    e_sink = jnp.exp(sink - m)
    inv = 1.0 / (jnp.sum(e, axis=-1, keepdims=True) + e_sink)
    return e * inv, e_sink * inv


def _swa_specs():
    blk = lambda w: pl.BlockSpec((BLOCK, w), lambda n: (n, 0))
    prev = lambda w: pl.BlockSpec((BLOCK, w), lambda n: (jnp.maximum(n - 1, 0), 0))
    prow_c = pl.BlockSpec((1, BLOCK), lambda n: (0, n))
    prow_p = pl.BlockSpec((1, BLOCK), lambda n: (0, jnp.maximum(n - 1, 0)))
    smem = pl.BlockSpec(memory_space=pltpu.SMEM)
    return [blk(1024), prev(128), blk(128), prev(128), blk(128), blk(1), prow_p, prow_c, smem], blk


def _swa_fwd(qa, ka, va, pos_col, pos_row, sinks, *, comm=()):
    s = qa.shape[0]
    in_specs, blk = _swa_specs()

    def body(q_ref, kp_ref, kc_ref, vp_ref, vc_ref, pcol_ref, pprow_ref, pcrow_ref, sink_ref, o_ref, yb_ref):
        n = pl.program_id(0)
        kk, vv_lo, vv_hi, dist, mask_add = _swa_span(n, kp_ref, kc_ref, vp_ref, vc_ref, pcol_ref, pprow_ref, pcrow_ref)
        lo = _lane((BLOCK, LANES)) < 64
        heads = _swa_heads(q_ref, lo)
        raws = [lax.dot_general(q_h, kk[kv], NT_DIMS, preferred_element_type=F32) for _, kv, _, q_h in heads]
        probs = [_swa_probs(raw, dist, mask_add, SWA_SLOPES[h], sink_ref[h])[0].astype(BF16)
                 for raw, (h, _, _, _) in zip(raws, heads)]
        for j in range(SWA_Q_HEADS // 2):
            kv = heads[2 * j][1]
            out = (jnp.dot(probs[2 * j], vv_lo[kv], preferred_element_type=F32)
                   + jnp.dot(probs[2 * j + 1], vv_hi[kv], preferred_element_type=F32))
            o_ref[:, 128 * j:128 * (j + 1)] = out
            yb_ref[:, 128 * j:128 * (j + 1)] = out.astype(BF16)

    return _pcall(body, (qa, ka, ka, va, va, pos_col, pos_row, pos_row, sinks), name="swa_fwd", grid=(s // BLOCK,),
                  in_specs=in_specs, out_specs=[blk(1024), blk(1024)],
                  out_shape=[jax.ShapeDtypeStruct((s, 1024), F32), jax.ShapeDtypeStruct((s, D_MODEL), BF16)],
                  sem=("parallel",), comm=comm)


def _swa_bwd(qa, ka, va, pos_col, pos_row, sinks, y_a, d_y, *, comm=()):
    s = qa.shape[0]
    in_specs, blk = _swa_specs()
    whole = pl.BlockSpec((s, 128), lambda n: (0, 0))

    def body(q_ref, kp_ref, kc_ref, vp_ref, vc_ref, pcol_ref, pprow_ref, pcrow_ref, sink_ref, y_ref, dy_ref,
             dq_ref, dk_ref, dv_ref, dsink_ref):
        n = pl.program_id(0)

        @pl.when(n == 0)
        def _():
            dk_ref[...] = jnp.zeros_like(dk_ref)
            dv_ref[...] = jnp.zeros_like(dv_ref)
            dsink_ref[...] = jnp.zeros_like(dsink_ref)

        kk, vv_lo, vv_hi, dist, mask_add = _swa_span(n, kp_ref, kc_ref, vp_ref, vc_ref, pcol_ref, pprow_ref, pcrow_ref)
        lo = _lane((BLOCK, LANES)) < 64
        lo2 = _lane((2 * BLOCK, LANES)) < 64
        lane1 = _lane((1, LANES))
        dsink = jnp.zeros((1, LANES), F32)
        dkk = [jnp.zeros((2 * BLOCK, LANES), F32) for _ in range(SWA_KV_HEADS)]
        dvv = [jnp.zeros((2 * BLOCK, LANES), F32) for _ in range(SWA_KV_HEADS)]
        heads = _swa_heads(q_ref, lo)
        do_b, deltas = [], []
        for j in range(SWA_Q_HEADS // 2):
            do_pair = dy_ref[:, 128 * j:128 * (j + 1)]
            doy = do_pair * y_ref[:, 128 * j:128 * (j + 1)]
            do_b.append(do_pair.astype(BF16))
            deltas.append(jnp.sum(jnp.where(lo, doy, 0.0), axis=-1, keepdims=True))
            deltas.append(jnp.sum(jnp.where(lo, 0.0, doy), axis=-1, keepdims=True))
        raws = [lax.dot_general(q_h, kk[kv], NT_DIMS, preferred_element_type=F32) for _, kv, _, q_h in heads]
        dps = [lax.dot_general(do_b[h // 2], (vv_lo, vv_hi)[par][kv], NT_DIMS, preferred_element_type=F32)
               for h, kv, par, _ in heads]
        p_b, ds_b = [], []
        for h, kv, par, _ in heads:
            p, p_sink = _swa_probs(raws[h], dist, mask_add, SWA_SLOPES[h], sink_ref[h])
            ds = p * (dps[h] - deltas[h])
            dsink = dsink + jnp.where(lane1 == h, -jnp.sum(p_sink * deltas[h], axis=0, keepdims=True), 0.0)
            p_b.append(p.astype(BF16))
            ds_b.append((ds * SWA_SCALE).astype(BF16))
        dq_halves = []
        for h, kv, par, q_h in heads:
            dq_halves.append(jnp.dot(ds_b[h], kk[kv], preferred_element_type=F32))
            dkk[kv] = dkk[kv] + lax.dot_general(ds_b[h], q_h, TN_DIMS, preferred_element_type=F32)
            pv = lax.dot_general(p_b[h], do_b[h // 2], TN_DIMS, preferred_element_type=F32)
            dvv[kv] = dvv[kv] + jnp.where(lo2 if par == 0 else jnp.logical_not(lo2), pv, 0.0)
        for j in range(SWA_Q_HEADS // 2):
            dq_ref[:, 128 * j:128 * (j + 1)] = jnp.where(lo, dq_halves[2 * j], dq_halves[2 * j + 1])
        fold = lambda t: t + pltpu.roll(t, 64, axis=1)
        dk_span = jnp.where(lo2, fold(dkk[0]), fold(dkk[1]))
        dv_span = jnp.where(lo2, fold(dvv[0]), fold(dvv[1]))
        prev0 = pl.multiple_of(jnp.maximum(n - 1, 0) * BLOCK, BLOCK)
        cur0 = pl.multiple_of(n * BLOCK, BLOCK)
        dk_ref[pl.ds(prev0, BLOCK), :] += dk_span[0:BLOCK]
        dk_ref[pl.ds(cur0, BLOCK), :] += dk_span[BLOCK:]
        dv_ref[pl.ds(prev0, BLOCK), :] += dv_span[0:BLOCK]
        dv_ref[pl.ds(cur0, BLOCK), :] += dv_span[BLOCK:]
        dsink_ref[...] += dsink

    return _pcall(
        body, (qa, ka, ka, va, va, pos_col, pos_row, pos_row, sinks, y_a, d_y), name="swa_bwd", grid=(s // BLOCK,),
        in_specs=in_specs + [blk(1024), blk(1024)],
        out_specs=[blk(1024), whole, whole, pl.BlockSpec((1, LANES), lambda n: (0, 0))],
        out_shape=[jax.ShapeDtypeStruct((s, 1024), F32), jax.ShapeDtypeStruct((s, 128), F32),
                   jax.ShapeDtypeStruct((s, 128), F32), jax.ShapeDtypeStruct((1, LANES), F32)],
        sem=("arbitrary",), comm=comm)


MLA_SCALE = (MLA_NOPE + MLA_ROPE) ** -0.5
LOG2_E = math.log2(math.e)
MLA_TILE = 1024
MLA_ROW_GROUP = 256


def _tile_pairs(nt, q_major):
    pairs = [(i, j) for i in range(nt) for j in range(i + 1)] if q_major else \
            [(i, j) for j in range(nt) for i in range(j, nt)]
    return jnp.asarray([p[0] for p in pairs], jnp.int32), jnp.asarray([p[1] for p in pairs], jnp.int32)


def _diag_mask(t):
    return lax.broadcasted_iota(jnp.int32, (t, t), 1) <= lax.broadcasted_iota(jnp.int32, (t, t), 0)


def _mla_fwd(q_cat, k_cat, v_b, y_all, *, comm=()):
    nh, s, _ = q_cat.shape
    t = _tile(s, MLA_TILE)
    qi, kj = _tile_pairs(s // t, True)
    ycol = (SWA_Q_HEADS * SWA_HEAD_DIM) // (nh * MLA_V)

    def body(qi_ref, kj_ref, q_ref, k_ref, v_ref, _, o_ref, lse_ref, yb_ref, m_sc, l_sc, acc_sc):
        i, j = qi_ref[pl.program_id(0)], kj_ref[pl.program_id(0)]

        @pl.when(j == 0)
        def _():
            m_sc[...] = jnp.full_like(m_sc, NEG_INF)
            l_sc[...] = jnp.zeros_like(l_sc)
            acc_sc[...] = jnp.zeros_like(acc_sc)

        def update(diagonal):
            scores = [lax.dot_general(q_ref[h], k_ref[h], NT_DIMS, preferred_element_type=F32) for h in range(nh)]
            probs, alphas = [], []
            for h in range(nh):
                raw = scores[h]
                if diagonal:
                    raw = jnp.where(_diag_mask(t), raw, NEG_INF)
                m_old = m_sc[h]
                m_new = jnp.maximum(m_old, jnp.max(raw, axis=-1, keepdims=True))
                alpha = jnp.exp2((m_old - m_new) * (MLA_SCALE * LOG2_E))
                p = jnp.exp2((raw - m_new) * (MLA_SCALE * LOG2_E))
                l_sc[h] = alpha * l_sc[h] + jnp.sum(p, axis=-1, keepdims=True)
                m_sc[h] = m_new
                probs.append(p.astype(BF16))
                alphas.append(alpha)
            for h in range(nh):
                acc_sc[h] = alphas[h] * acc_sc[h] + jnp.dot(probs[h], v_ref[:, MLA_V * h:MLA_V * (h + 1)],
                                                            preferred_element_type=F32)

        @pl.when(j < i)
        def _():
            update(False)

        @pl.when(j == i)
        def _():
            update(True)
            for h in range(nh):
                out = acc_sc[h] * (1.0 / l_sc[h])
                o_ref[:, MLA_V * h:MLA_V * (h + 1)] = out
                yb_ref[:, MLA_V * h:MLA_V * (h + 1)] = out.astype(BF16)
                lse_ref[h] = m_sc[h] * MLA_SCALE + jnp.log(l_sc[h])

    return _pcall(
        body, (q_cat, k_cat, v_b, y_all), name="mla_fwd", grid=(qi.shape[0],), prefetch=(qi, kj),
        in_specs=[pl.BlockSpec((nh, t, 256), lambda p, qi, kj: (0, qi[p], 0)),
                  pl.BlockSpec((nh, t, 256), lambda p, qi, kj: (0, kj[p], 0)),
                  pl.BlockSpec((t, nh * MLA_V), lambda p, qi, kj: (kj[p], 0)), ANY],
        out_specs=[pl.BlockSpec((t, nh * MLA_V), lambda p, qi, kj: (qi[p], 0)),
                   pl.BlockSpec((nh, t, 1), lambda p, qi, kj: (0, qi[p], 0)),
                   pl.BlockSpec((t, nh * MLA_V), lambda p, qi, kj: (qi[p], ycol))],
        out_shape=[jax.ShapeDtypeStruct((s, nh * MLA_V), F32), jax.ShapeDtypeStruct((nh, s, 1), F32),
                   jax.ShapeDtypeStruct(y_all.shape, y_all.dtype)],
        scratch_shapes=[pltpu.VMEM((nh, t, 1), F32), pltpu.VMEM((nh, t, 1), F32), pltpu.VMEM((nh, t, MLA_V), F32)],
        sem=("arbitrary",), comm=comm, io_alias={3: 2})


def _mla_bwd(q_cat, k_cat, v_b, y_b, lse, d_y, *, comm=()):
    nh, s, _ = q_cat.shape
    t = _tile(s, MLA_TILE)
    nt = s // t
    hp = 2
    wv = hp * MLA_V
    col0 = (SWA_Q_HEADS * SWA_HEAD_DIM) // wv
    qi, kj = _tile_pairs(nt, False)

    def body(qi_ref, kj_ref, q_ref, k_ref, v_ref, y_ref, lse_ref, dy_ref, dq_ref, dk_ref, dv_ref, dk_sc, dv_sc):
        step = pl.program_id(1)
        i, j = qi_ref[step], kj_ref[step]

        @pl.when(step == 0)
        def _():
            dq_ref[...] = jnp.zeros_like(dq_ref)

        @pl.when(i == j)
        def _():
            dk_sc[...] = jnp.zeros_like(dk_sc)
            dv_sc[...] = jnp.zeros_like(dv_sc)

        def update(diagonal):
            rg = min(MLA_ROW_GROUP, t) if diagonal else t
            units = [(h, r0, (r0 + rg) if diagonal else t) for h in range(hp) for r0 in range(0, t, rg)]
            cols = [slice(MLA_V * h, MLA_V * (h + 1)) for h in range(hp)]
            do_b = [dy_ref[r0:r0 + rg, cols[h]].astype(BF16) for h, r0, _ in units]
            scores = [lax.dot_general(q_ref[h, r0:r0 + rg, :], k_ref[h, 0:nk, :], NT_DIMS, preferred_element_type=F32)
                      for h, r0, nk in units]
            dps = [lax.dot_general(do_b[u], v_ref[0:nk, cols[h]], NT_DIMS, preferred_element_type=F32)
                   for u, (h, r0, nk) in enumerate(units)]
            p_b, ds_b = [], []
            for u, (h, r0, nk) in enumerate(units):
                p = jnp.exp(scores[u] * MLA_SCALE - lse_ref[h, r0:r0 + rg])
                if diagonal:
                    row = r0 + lax.broadcasted_iota(jnp.int32, (rg, nk), 0)
                    p = jnp.where(lax.broadcasted_iota(jnp.int32, (rg, nk), 1) <= row, p, 0.0)
                delta = jnp.sum(dy_ref[r0:r0 + rg, cols[h]] * y_ref[r0:r0 + rg, cols[h]], axis=-1, keepdims=True)
                p_b.append(p.astype(BF16))
                ds_b.append((p * (dps[u] - delta) * MLA_SCALE).astype(BF16))
            for u, (h, r0, nk) in enumerate(units):
                dv_sc[h, 0:nk] += lax.dot_general(p_b[u], do_b[u], TN_DIMS, preferred_element_type=F32)
                dk_sc[h, 0:nk] += lax.dot_general(ds_b[u], q_ref[h, r0:r0 + rg, :], TN_DIMS, preferred_element_type=F32)
                rows = pl.ds(pl.multiple_of(i * t + r0, rg), rg)
                dq_ref[h, rows, :] += jnp.dot(ds_b[u], k_ref[h, 0:nk, :], preferred_element_type=F32)

        @pl.when(i > j)
        def _():
            update(False)

        @pl.when(i == j)
        def _():
            update(True)

        @pl.when(i == nt - 1)
        def _():
            dk_ref[...] = dk_sc[...]
            for h in range(hp):
                dv_ref[:, MLA_V * h:MLA_V * (h + 1)] = dv_sc[h]

    return _pcall(
        body, (q_cat, k_cat, v_b, y_b, lse, d_y), name="mla_bwd", grid=(nh // hp, qi.shape[0]), prefetch=(qi, kj),
        in_specs=[pl.BlockSpec((hp, t, 256), lambda g, p, qi, kj: (g, qi[p], 0)),
                  pl.BlockSpec((hp, t, 256), lambda g, p, qi, kj: (g, kj[p], 0)),
                  pl.BlockSpec((t, wv), lambda g, p, qi, kj: (kj[p], g)),
                  pl.BlockSpec((t, wv), lambda g, p, qi, kj: (qi[p], g)),
                  pl.BlockSpec((hp, t, 1), lambda g, p, qi, kj: (g, qi[p], 0)),
                  pl.BlockSpec((t, wv), lambda g, p, qi, kj: (qi[p], col0 + g))],
        out_specs=[pl.BlockSpec((hp, s, 256), lambda g, p, qi, kj: (g, 0, 0)),
                   pl.BlockSpec((hp, t, 256), lambda g, p, qi, kj: (g, kj[p], 0)),
                   pl.BlockSpec((t, wv), lambda g, p, qi, kj: (kj[p], g))],
        out_shape=[jax.ShapeDtypeStruct((nh, s, 256), F32), jax.ShapeDtypeStruct((nh, s, 256), F32),
                   jax.ShapeDtypeStruct((s, nh * MLA_V), F32)],
        scratch_shapes=[pltpu.VMEM((hp, t, 256), F32), pltpu.VMEM((hp, t, MLA_V), F32)],
        sem=("arbitrary", "arbitrary"), comm=comm)


MEM_SCALE = MEM_DIM ** -0.5


def _mem_kv_fwd(mem, g_mem, w_memkv, g_mk):
    m_len = mem.shape[0]

    def body(mem_ref, g_ref, w_ref, gk_ref, mn_ref, kv_ref, kn_ref, v_ref):
        mn, _, _ = _norm_full(mem_ref[...], g_ref[...])
        mn_b = mn.astype(BF16)
        mn_ref[...] = mn_b
        kv = jnp.dot(mn_b, w_ref[...], preferred_element_type=F32)
        kv_ref[...] = kv
        for h in range(MEM_HEADS):
            kn, _, _ = _norm_full(kv[:, 128 * h:128 * (h + 1)], gk_ref[...])
            kn_ref[:, 128 * h:128 * (h + 1)] = kn.astype(BF16)
        v_ref[...] = kv[:, 512:1024].astype(BF16)

    return pl.pallas_call(
        body, name="mem_kv_fwd",
        out_shape=[jax.ShapeDtypeStruct((m_len, D_MODEL), BF16), jax.ShapeDtypeStruct((m_len, 1024), F32),
                   jax.ShapeDtypeStruct((m_len, 512), BF16), jax.ShapeDtypeStruct((m_len, 512), BF16)],
        compiler_params=_params(),
    )(mem, g_mem, w_memkv, g_mk)


def _mem_kv_bwd(mem, g_mem, w_memkv, g_mk, mn_b, kv, d_kn, d_v):
    m_len = mem.shape[0]

    def body(mem_ref, g_ref, w_ref, gk_ref, mn_ref, kv_ref, dkn_ref, dv_ref, dw_ref, dgmem_ref, dgk_ref):
        parts = []
        dgk = jnp.zeros((1, LANES), F32)
        for h in range(MEM_HEADS):
            _, xhat, r = _norm_full(kv_ref[:, 128 * h:128 * (h + 1)], gk_ref[...])
            dx, dgh = _norm_full_bwd(dkn_ref[:, 128 * h:128 * (h + 1)], gk_ref[...], xhat, r)
            parts.append(dx)
            dgk = dgk + dgh
        d_kv = jnp.concatenate(parts + [dv_ref[...]], axis=1).astype(BF16)
        dw_ref[...] = lax.dot_general(mn_ref[...], d_kv, TN_DIMS, preferred_element_type=F32)
        d_mn = lax.dot_general(d_kv, w_ref[...], NT_DIMS, preferred_element_type=F32)
        _, xhat, _ = _norm_full(mem_ref[...], g_ref[...])
        dgmem_ref[...] = jnp.sum(d_mn * xhat, axis=0, keepdims=True)
        dgk_ref[...] = dgk

    return pl.pallas_call(
        body, name="mem_kv_bwd",
        out_shape=[jax.ShapeDtypeStruct((D_MODEL, 1024), F32), jax.ShapeDtypeStruct((1, D_MODEL), F32),
                   jax.ShapeDtypeStruct((1, LANES), F32)],
        compiler_params=_params(),
    )(mem, g_mem, w_memkv, g_mk, mn_b, kv, d_kn, d_v)


def _mem_softmax(raw):
    sc = raw * MEM_SCALE
    e = jnp.exp(sc - jnp.max(sc, axis=-1, keepdims=True))
    return e * (1.0 / jnp.sum(e, axis=-1, keepdims=True))


def _mem_attn_fwd(qm, km, vm, y_all, *, tm=512):
    s = qm.shape[0]
    tm = _tile(s, tm)
    m_len = km.shape[0]
    ycol = (SWA_Q_HEADS * SWA_HEAD_DIM + MLA_HEADS * MLA_V) // 512

    def body(q_ref, k_ref, v_ref, _, o_ref, yb_ref):
        cols = [slice(128 * h, 128 * (h + 1)) for h in range(MEM_HEADS)]
        raws = [lax.dot_general(q_ref[:, sl], k_ref[:, sl], NT_DIMS, preferred_element_type=F32) for sl in cols]
        probs = [_mem_softmax(raw).astype(BF16) for raw in raws]
        for p, sl in zip(probs, cols):
            out = jnp.dot(p, v_ref[:, sl], preferred_element_type=F32)
            o_ref[:, sl] = out
            yb_ref[:, sl] = out.astype(BF16)

    kvspec = pl.BlockSpec((m_len, 512), lambda i: (0, 0))
    return _pcall(
        body, (qm, km, vm, y_all), name="mem_attn_fwd", grid=(s // tm,),
        in_specs=[pl.BlockSpec((tm, 512), lambda i: (i, 0)), kvspec, kvspec, ANY],
        out_specs=[pl.BlockSpec((tm, 512), lambda i: (i, 0)), pl.BlockSpec((tm, 512), lambda i: (i, ycol))],
        out_shape=[jax.ShapeDtypeStruct((s, 512), F32), jax.ShapeDtypeStruct(y_all.shape, y_all.dtype)],
        sem=("parallel",), io_alias={3: 1})


def _mem_attn_bwd(qm, km, vm, y_m, d_y, *, tm=1024):
    s = qm.shape[0]
    tm = _tile(s, tm)
    m_len = km.shape[0]
    col0 = (SWA_Q_HEADS * SWA_HEAD_DIM + MLA_HEADS * MLA_V) // 512

    def body(q_ref, k_ref, v_ref, y_ref, dy_ref, dq_ref, dk_ref, dv_ref):
        @pl.when(pl.program_id(0) == 0)
        def _():
            dk_ref[...] = jnp.zeros_like(dk_ref)
            dv_ref[...] = jnp.zeros_like(dv_ref)

        cols = [slice(128 * h, 128 * (h + 1)) for h in range(MEM_HEADS)]
        do_b = [dy_ref[:, sl].astype(BF16) for sl in cols]
        raws = [lax.dot_general(q_ref[:, sl], k_ref[:, sl], NT_DIMS, preferred_element_type=F32) for sl in cols]
        dps = [lax.dot_general(do_b[h], v_ref[:, sl], NT_DIMS, preferred_element_type=F32) for h, sl in enumerate(cols)]
        p_b, ds_b = [], []
        for h, sl in enumerate(cols):
            p = _mem_softmax(raws[h])
            delta = jnp.sum(dy_ref[:, sl] * y_ref[:, sl], axis=-1, keepdims=True)
            p_b.append(p.astype(BF16))
            ds_b.append((p * (dps[h] - delta) * MEM_SCALE).astype(BF16))
        for h, sl in enumerate(cols):
            dv_ref[:, sl] += lax.dot_general(p_b[h], do_b[h], TN_DIMS, preferred_element_type=F32)
            dq_ref[:, sl] = jnp.dot(ds_b[h], k_ref[:, sl], preferred_element_type=F32)
            dk_ref[:, sl] += lax.dot_general(ds_b[h], q_ref[:, sl], TN_DIMS, preferred_element_type=F32)

    kvspec = pl.BlockSpec((m_len, 512), lambda i: (0, 0))
    row = pl.BlockSpec((tm, 512), lambda i: (i, 0))
    return pl.pallas_call(
        body, name="mem_attn_bwd", grid=(s // tm,),
        in_specs=[row, kvspec, kvspec, row, pl.BlockSpec((tm, 512), lambda i: (i, col0))],
        out_specs=[row, kvspec, kvspec],
        out_shape=[jax.ShapeDtypeStruct((s, 512), F32), jax.ShapeDtypeStruct((m_len, 512), F32),
                   jax.ShapeDtypeStruct((m_len, 512), F32)],
        compiler_params=_params(("arbitrary",)),
    )(qm, km, vm, y_m, d_y)


def _ffn_gate_up(fn, w_gate, w_up, *, tm=512, comm=()):
    s, d = fn.shape
    nsp, _, tf = w_gate.shape
    f = nsp * tf
    tm = _tile(s, tm)

    def body(x_ref, wg_ref, wu_ref, g_ref, u_ref, a_ref):
        x = x_ref[...]
        gate = jnp.dot(x, wg_ref[...], preferred_element_type=F32)
        up = jnp.dot(x, wu_ref[...], preferred_element_type=F32)
        g_ref[...] = gate.astype(BF16)
        u_ref[...] = up.astype(BF16)
        a_ref[...] = (gate * (1.0 / (1.0 + jnp.exp(-gate))) * up).astype(BF16)

    wspec = pl.BlockSpec((None, d, tf), lambda j, i: (j, 0, 0))
    ospec = pl.BlockSpec((tm, tf), lambda j, i: (i, j))
    osh = jax.ShapeDtypeStruct((s, f), BF16)
    return _pcall(body, (fn, w_gate, w_up), name="ffn_gate_up", grid=(nsp, s // tm),
                  in_specs=[pl.BlockSpec((tm, d), lambda j, i: (i, 0)), wspec, wspec],
                  out_specs=[ospec, ospec, ospec], out_shape=[osh, osh, osh], sem=("parallel", "parallel"), comm=comm)


def _ffn_bwd_act(d_out, w_down, gate, up, *, tm=1024, tf=1408, comm=()):
    s, d = d_out.shape
    f = w_down.shape[0]
    tm, tf = _tile(s, tm), _tile(f, tf)

    sub = tm // 4 if tm % 1024 == 0 else tm

    def body(do_ref, wd_ref, g_ref, u_ref, dg_ref, du_ref):
        groups = [slice(r, r + sub) for r in range(0, tm, sub)]
        parts = [lax.dot_general(do_ref[rows, :].astype(BF16), wd_ref[...], NT_DIMS, preferred_element_type=F32)
                 for rows in groups]
        for rows, d_act in zip(groups, parts):
            gate = g_ref[rows, :].astype(F32)
            sig = 1.0 / (1.0 + jnp.exp(-gate))
            du_ref[rows, :] = (d_act * (gate * sig)).astype(BF16)
            dg_ref[rows, :] = (d_act * u_ref[rows, :].astype(F32) * (sig * (1.0 + gate * (1.0 - sig)))).astype(BF16)

    ospec = pl.BlockSpec((tm, tf), lambda j, i: (i, j))
    osh = jax.ShapeDtypeStruct((s, f), BF16)
    return _pcall(
        body, (d_out, w_down, gate, up), name="ffn_bwd_act", grid=(f // tf, s // tm),
        in_specs=[pl.BlockSpec((tm, d), lambda j, i: (i, 0)), pl.BlockSpec((tf, d), lambda j, i: (j, 0)), ospec, ospec],
        out_specs=[ospec, ospec], out_shape=[osh, osh], sem=("parallel", "parallel"), comm=comm)


def _cols(g4):
    return jnp.concatenate([g4[k] for k in range(N_CHIPS)], axis=1)


def _full_w_in(g4):
    per = IN_WIDTH // N_CHIPS
    kr0 = 2304 - (N_CHIPS - 1) * per
    last = g4[N_CHIPS - 1]
    pad = jnp.zeros((last.shape[0], IN_PAD - IN_WIDTH), last.dtype)
    return jnp.concatenate([g4[0], g4[1], g4[2], last[:, :kr0], last[:, kr0 + 64:], last[:, kr0:kr0 + 64], pad], axis=1)


def _shards_w_in(dwp):
    per = IN_WIDTH // N_CHIPS
    kr0 = 2304 - (N_CHIPS - 1) * per
    last = jnp.concatenate([dwp[:, (N_CHIPS - 1) * per:2304], dwp[:, C_KR:C_KR + 64], dwp[:, 2304:C_KR]], axis=1)
    assert last.shape[1] == per and kr0 == 144
    return jnp.stack([dwp[:, per * k:per * (k + 1)] for k in range(N_CHIPS - 1)] + [last])


def _full_heads(g4, first):
    return jnp.concatenate([g4[k][:, :first] for k in range(N_CHIPS)] + [g4[k][:, first:] for k in range(N_CHIPS)], axis=1)


def _shards_heads(dwp, first, rest):
    base = N_CHIPS * first
    return jnp.stack([jnp.concatenate([dwp[:, first * k:first * (k + 1)], dwp[:, base + rest * k:base + rest * (k + 1)]], axis=1)
                      for k in range(N_CHIPS)])


def _rope_tables(pos):
    inv_freq = ROPE_THETA ** (-jnp.arange(0, MLA_ROPE, 2, dtype=F32) / MLA_ROPE)
    ang = pos.astype(F32)[:, None] * inv_freq
    cos, sin = jnp.cos(ang), jnp.sin(ang)
    return jnp.tile(cos, (1, 4)), jnp.concatenate([-sin, sin, -sin, sin], axis=1)


def _gain_table(sp):
    two = lambda v: jnp.tile(v, (1, 2))
    rows = [two(sp["swa_q_norm_g"]), two(sp["swa_k_norm_g"]), sp["mla_qn_norm_g"], two(sp["mla_qr_norm_g"]),
            sp["mla_kn_norm_g"], two(sp["mla_kr_norm_g"]), sp["mem_q_norm_g"], jnp.zeros((1, LANES), F32)]
    return jnp.concatenate(rows, axis=0)


CHIP_DISTANCES = (1, 2, 3)


def _place():
    x, y, c = lax.axis_index("x"), lax.axis_index("y"), lax.axis_index("c")
    return x, y, c, 2 * x + y


def _chip_at(x, y, d):
    px = 1 - x if d & 2 else x
    py = 1 - y if d & 1 else y
    return px, py, 2 * px + py


def _row_tile(rows, want=512, mult=8):
    t = min(rows, want)
    t -= t % mult
    while rows % t:
        t -= mult
    return t


def _cast_into_slot(w, meta, *, name, comm=()):
    rows, cols = w.shape
    tr = _row_tile(rows, 512, 16)

    def body(meta_ref, w_ref, o_ref):
        o_ref[...] = w_ref[...].astype(BF16)

    return _pcall(body, (w,), name=name, grid=(rows // tr,), prefetch=(meta,),
                  in_specs=[pl.BlockSpec((tr, cols), lambda i, m: (i, 0))],
                  out_specs=pl.BlockSpec((None, tr, cols), lambda i, m: (m[0], i, 0)),
                  out_shape=jax.ShapeDtypeStruct((N_CHIPS, rows, cols), BF16), sem=("parallel",), comm=comm)


def _remote(src, dst, ssem, rsem, i, device):
    return pltpu.make_async_remote_copy(src_ref=src, dst_ref=dst, send_sem=ssem.at[i], recv_sem=rsem.at[i],
                                        device_id=device, device_id_type=MESH)


def _symmetric_stage(ins, out_shapes, aliases, n_sem, copies):
    def issue(i_refs, o_refs, ssem, rsem):
        for send, _ in copies(i_refs, o_refs, ssem, rsem):
            send.start()

    def wait(i_refs, o_refs, ssem, rsem):
        pairs = copies(i_refs, o_refs, ssem, rsem)
        for _, arrival in pairs:
            arrival.wait_recv()
        for send, _ in pairs:
            send.wait_send()

    return _Stage(ins, out_shapes, aliases, n_sem, issue, wait)


def _gather_stage(slots, leg, part=(0, 1)):
    n = len(slots)
    shapes = [jax.ShapeDtypeStruct(s.shape, s.dtype) for s in slots]
    in_place = {w: w for w in range(n)}
    if not isinstance(leg, str):
        legs = list(leg)

        def copies(i_refs, o_refs, ssem, rsem):
            return [pr for k, (which, prt) in enumerate(legs)
                    for pr in _gather_stage(slots, which, prt).leg_copies(which, 3 * n * k)(i_refs, o_refs, ssem, rsem)]

        return _symmetric_stage(slots, shapes, in_place, 3 * n * len(legs), copies)

    def leg_copies(which, base):
        def copies(_, outs, ssem, rsem):
            x, y, c, k_me = _place()
            pairs = []
            for w in range(n):
                half = outs[w].shape[1] // 2
                r0, size = _window(half, part)
                slab = lambda k, cc, w=w, half=half, r0=r0, size=size: outs[w].at[k, pl.ds(cc * half + r0, size)]
                for d in CHIP_DISTANCES:
                    px, py, k_src = _chip_at(x, y, d)
                    i = base + 3 * w + d - 1
                    if which == "ici":
                        pairs.append((_remote(slab(k_me, c), slab(k_me, c), ssem, rsem, i, (px, py, c)),
                                      _remote(slab(k_src, c), slab(k_src, c), ssem, rsem, i, (x, y, c))))
                    else:
                        pairs.append((_remote(slab(k_src, c), slab(k_src, c), ssem, rsem, i, (x, y, 1 - c)),
                                      _remote(slab(k_src, 1 - c), slab(k_src, 1 - c), ssem, rsem, i, (x, y, c))))
            return pairs
        return copies

    if leg != "both":
        st = _symmetric_stage(slots, shapes, in_place, 3 * n, leg_copies(leg, 0))
        st.leg_copies = leg_copies
        return st
    ici = _symmetric_stage(slots, shapes, in_place, 6 * n, leg_copies("ici", 0))
    d2d = _symmetric_stage(slots, shapes, in_place, 6 * n, leg_copies("d2d", 3 * n))

    def mid(*refs):
        ici.wait(*refs)
        d2d.issue(*refs)

    return _Stage(slots, shapes, in_place, 6 * n, ici.issue, d2d.wait, mid)


def _halves_stage(grads):
    n = len(grads)

    def copies(ins, outs, ssem, rsem):
        x, y, c, _ = _place()
        pairs = []
        for w in range(n):
            half = ins[w].shape[1] // 2
            pairs.append((_remote(ins[w].at[:, pl.ds((1 - c) * half, half)], outs[w], ssem, rsem, w, (x, y, 1 - c)),
                          _remote(outs[w], outs[w], ssem, rsem, w, (x, y, c))))
        return pairs

    shapes = [jax.ShapeDtypeStruct((N_CHIPS, g.shape[1] // 2, g.shape[2]), g.dtype) for g in grads]
    return _symmetric_stage(grads, shapes, {}, n, copies)


def _window(rows, part):
    idx, count = part
    size = rows // count
    assert size * count == rows and size % 16 == 0, (rows, part)
    return idx * size, size


def _chips_stage(parts, part=(0, 1), into=None):
    n = len(parts)

    def copies(ins, outs, ssem, rsem):
        x, y, c, _ = _place()
        pairs = []
        for w in range(n):
            r0, size = _window(ins[w].shape[1], part)
            for d in CHIP_DISTANCES:
                px, py, _ = _chip_at(x, y, d)
                i = 3 * w + d - 1
                land = outs[w].at[d - 1, pl.ds(r0, size)]
                pairs.append((_remote(ins[w].at[d - 1, pl.ds(r0, size)], land, ssem, rsem, i, (px, py, c)),
                              _remote(land, land, ssem, rsem, i, (x, y, c))))
        return pairs

    shapes = [jax.ShapeDtypeStruct(p.shape, p.dtype) for p in parts]
    if into is None:
        return _symmetric_stage(parts, shapes, {}, 3 * n, copies)
    return _symmetric_stage(list(parts) + list(into), shapes, {n + w: w for w in range(n)}, 3 * n, copies)


def _swap_stage(totals):
    n = len(totals)

    def copies(ins, outs, ssem, rsem):
        x, y, c, _ = _place()
        return [(_remote(ins[w], outs[w], ssem, rsem, w, (x, y, 1 - c)),
                 _remote(outs[w], outs[w], ssem, rsem, w, (x, y, c))) for w in range(n)]

    shapes = [jax.ShapeDtypeStruct(t.shape, t.dtype) for t in totals]
    return _symmetric_stage(totals, shapes, {}, n, copies)


def _run_stages(stages, *, name):
    n_ins = [len(st.ins) for st in stages]
    n_outs = [len(st.out_shapes) for st in stages]
    tot_in, tot_out = sum(n_ins), sum(n_outs)
    aliases, i0, o0 = {}, 0, 0
    for st, ni, no in zip(stages, n_ins, n_outs):
        aliases.update({i0 + a: o0 + b for a, b in st.aliases.items()})
        i0, o0 = i0 + ni, o0 + no

    def body(*refs):
        sems = refs[tot_in + tot_out:]
        for what in ("issue", "wait"):
            i0, o0 = 0, tot_in
            for k, (st, ni, no) in enumerate(zip(stages, n_ins, n_outs)):
                getattr(st, what)(refs[i0:i0 + ni], refs[o0:o0 + no], sems[2 * k], sems[2 * k + 1])
                i0, o0 = i0 + ni, o0 + no

    sem = pltpu.SemaphoreType.DMA
    res = pl.pallas_call(
        body, name=name, in_specs=[ANY] * tot_in, out_specs=[ANY] * tot_out,
        out_shape=[s for st in stages for s in st.out_shapes], input_output_aliases=aliases,
        scratch_shapes=[sem((st.n_sem,)) for st in stages for _ in range(2)],
    )(*[a for st in stages for a in st.ins])
    outs, o0 = [], 0
    for no in n_outs:
        outs.append(list(res[o0:o0 + no]))
        o0 += no
    return outs


def _add_pair(meta, g4, recv, *, name):
    nsh, rows, cols = g4.shape
    half = rows // 2
    tr = _row_tile(half, 128 if cols > 1024 else 256, 16)
    nt = half // tr

    def body(meta_ref, g0, g1, g2, g3, r0, r1, r2, r3, own_ref, oth_ref):
        own_ref[...] = g0[...] + r0[...]
        for d, (g, r) in enumerate(((g1, r1), (g2, r2), (g3, r3))):
            oth_ref[d] = (g[...] + r[...]).astype(BF16)

    blk = (None, tr, cols)
    gspec = lambda d: pl.BlockSpec(blk, lambda i, m: (jnp.bitwise_xor(m[0], d), m[1] * nt + i, 0))
    rspec = lambda d: pl.BlockSpec(blk, lambda i, m: (jnp.bitwise_xor(m[0], d), i, 0))
    grid_spec = pltpu.PrefetchScalarGridSpec(
        num_scalar_prefetch=1, grid=(nt,),
        in_specs=[gspec(d) for d in range(nsh)] + [rspec(d) for d in range(nsh)],
        out_specs=[pl.BlockSpec((tr, cols), lambda i, m: (i, 0)), pl.BlockSpec((3, tr, cols), lambda i, m: (0, i, 0))])
    return pl.pallas_call(
        body, name=name, grid_spec=grid_spec,
        out_shape=[jax.ShapeDtypeStruct((half, cols), F32), jax.ShapeDtypeStruct((3, half, cols), BF16)],
        compiler_params=_params(("parallel",)),
    )(meta, g4, g4, g4, g4, recv, recv, recv, recv)


def _add_chips(own, recv, *, name):
    half, cols = own.shape
    tr = _row_tile(half, 256, 16)

    def body(p_ref, r_ref, o_ref):
        o_ref[...] = ((p_ref[...] + r_ref[0].astype(F32)) + r_ref[1].astype(F32)) + r_ref[2].astype(F32)

    return pl.pallas_call(
        body, name=name, grid=(half // tr,),
        in_specs=[pl.BlockSpec((tr, cols), lambda i: (i, 0)), pl.BlockSpec((3, tr, cols), lambda i: (0, i, 0))],
        out_specs=pl.BlockSpec((tr, cols), lambda i: (i, 0)),
        out_shape=jax.ShapeDtypeStruct((half, cols), F32),
        compiler_params=_params(("parallel",)),
    )(own, recv)


def _adamw_math(w, g, m, v):
    m = ADAM_B1 * m + (1.0 - ADAM_B1) * g
    v = ADAM_B2 * v + (1.0 - ADAM_B2) * (g * g)
    m_hat = m / (1.0 - ADAM_B1 ** ADAM_STEP)
    v_hat = v / (1.0 - ADAM_B2 ** ADAM_STEP)
    delta = -ADAM_LR * (m_hat / (jnp.sqrt(v_hat) + ADAM_EPS) + ADAM_WD * w)
    return delta, m, v


def _adamw(meta, w, g_mine, g_theirs, m, v, *, name):
    rows, cols = w.shape
    half = rows // 2
    tr = _row_tile(half, 256)
    nt = half // tr

    def body(meta_ref, w_ref, a_ref, b_ref, m_ref, v_ref, g_ref, d_ref, mo_ref, vo_ref):
        is_mine = (pl.program_id(0) // nt) == meta_ref[1]
        g = jnp.where(is_mine, a_ref[...], b_ref[...])
        g_ref[...] = g
        d_ref[...], mo_ref[...], vo_ref[...] = _adamw_math(w_ref[...], g, m_ref[...], v_ref[...])

    blk = pl.BlockSpec((tr, cols), lambda i, mt: (i, 0))
    mine = pl.BlockSpec((tr, cols), lambda i, mt: (jnp.where(i // nt == mt[1], i % nt, 0), 0))
    theirs = pl.BlockSpec((tr, cols), lambda i, mt: (jnp.where(i // nt == mt[1], 0, i % nt), 0))
    sh = jax.ShapeDtypeStruct((rows, cols), F32)
    grid_spec = pltpu.PrefetchScalarGridSpec(
        num_scalar_prefetch=1, grid=(rows // tr,),
        in_specs=[blk, mine, theirs, blk, blk], out_specs=[blk] * 4)
    return pl.pallas_call(
        body, name=name, grid_spec=grid_spec, out_shape=[sh] * 4,
        compiler_params=_params(("arbitrary",)),
    )(meta, w, g_mine, g_theirs, m, v)


N_DEVICES = 8


def _small_step(g_pack, w_pack, m_pack, v_pack):
    rows = g_pack.shape[0]

    def body(g_ref, w_ref, m_ref, v_ref, sum_ref, d_ref, mo_ref, vo_ref, slots, ssem, rsem):
        x, y, c, _ = _place()
        me = 4 * x + 2 * y + c
        slots[me] = g_ref[...]
        copies = []
        for r in range(1, N_DEVICES):
            px = 1 - x if r & 4 else x
            py = 1 - y if r & 2 else y
            pc = 1 - c if r & 1 else c
            copies.append(pltpu.make_async_remote_copy(
                src_ref=g_ref, dst_ref=slots.at[me], send_sem=ssem.at[r - 1], recv_sem=rsem.at[r - 1],
                device_id=(px, py, pc), device_id_type=MESH))
        for cp in copies:
            cp.start()
        for r in range(1, N_DEVICES):
            src = jnp.bitwise_xor(me, r)
            pltpu.make_async_remote_copy(
                src_ref=g_ref, dst_ref=slots.at[src], send_sem=ssem.at[r - 1], recv_sem=rsem.at[r - 1],
                device_id=(x, y, c), device_id_type=MESH).wait_recv()
        for cp in copies:
            cp.wait_send()
        total = slots[0]
        for k in range(1, N_DEVICES):
            total = total + slots[k]
        sum_ref[...] = total
        d_ref[...], mo_ref[...], vo_ref[...] = _adamw_math(w_ref[...], total, m_ref[...], v_ref[...])

    sh = jax.ShapeDtypeStruct((rows, LANES), F32)
    vm = pl.BlockSpec(memory_space=pltpu.VMEM)
    return pl.pallas_call(
        body, name="small_allreduce_adamw",
        in_specs=[vm] * 4, out_specs=[vm] * 4, out_shape=[sh] * 4,
        scratch_shapes=[pltpu.VMEM((N_DEVICES, rows, LANES), F32),
                        pltpu.SemaphoreType.DMA((N_DEVICES - 1,)), pltpu.SemaphoreType.DMA((N_DEVICES - 1,))],
    )(g_pack, w_pack, m_pack, v_pack)


WEIGHTS = ("attn_norm_g", "w_in", "swa_q_norm_g", "swa_k_norm_g", "swa_sinks", "mla_cq_norm_g", "mla_ckv_norm_g",
           "w_uq", "w_ukv", "mla_qn_norm_g", "mla_qr_norm_g", "mla_kn_norm_g", "mla_kr_norm_g", "mem_norm_g",
           "w_mem_kv", "mem_q_norm_g", "mem_k_norm_g", "w_out", "ffn_norm_g", "w_gate", "w_up", "w_down")
BIG = ("w_in", "w_uq", "w_ukv", "w_mem_kv", "w_out", "w_gate", "w_up", "w_down")
SMALL = tuple(n for n in WEIGHTS if n not in BIG)
PACK_UNIT = 8 * LANES


def _pack(parts):
    flat = jnp.concatenate(parts, axis=1)
    total = flat.shape[1]
    padded = -(-total // PACK_UNIT) * PACK_UNIT
    return jnp.pad(flat, ((0, 0), (0, padded - total))).reshape(padded // LANES, LANES)


def _unpack(buf, sizes):
    flat = buf.reshape(1, buf.shape[0] * LANES)
    out, at = [], 0
    for n in sizes:
        out.append(flat[:, at:at + n])
        at += n
    return out


def kernel(x, mem, positions, attn_norm_g, w_in, swa_q_norm_g, swa_k_norm_g, swa_sinks, mla_cq_norm_g, mla_ckv_norm_g, w_uq, w_ukv, mla_qn_norm_g, mla_qr_norm_g, mla_kn_norm_g, mla_kr_norm_g, mem_norm_g, w_mem_kv, mem_q_norm_g, mem_k_norm_g, w_out, ffn_norm_g, w_gate, w_up, w_down, loss_target, m_attn_norm_g, m_w_in, m_swa_q_norm_g, m_swa_k_norm_g, m_swa_sinks, m_mla_cq_norm_g, m_mla_ckv_norm_g, m_w_uq, m_w_ukv, m_mla_qn_norm_g, m_mla_qr_norm_g, m_mla_kn_norm_g, m_mla_kr_norm_g, m_mem_norm_g, m_w_mem_kv, m_mem_q_norm_g, m_mem_k_norm_g, m_w_out, m_ffn_norm_g, m_w_gate, m_w_up, m_w_down, v_attn_norm_g, v_w_in, v_swa_q_norm_g, v_swa_k_norm_g, v_swa_sinks, v_mla_cq_norm_g, v_mla_ckv_norm_g, v_w_uq, v_w_ukv, v_mla_qn_norm_g, v_mla_qr_norm_g, v_mla_kn_norm_g, v_mla_kr_norm_g, v_mem_norm_g, v_w_mem_kv, v_mem_q_norm_g, v_mem_k_norm_g, v_w_out, v_ffn_norm_g, v_w_gate, v_w_up, v_w_down):
    given = dict(locals())
    wts = {n: given[n] for n in WEIGHTS}
    mom_m = {n: given["m_" + n] for n in WEIGHTS}
    mom_v = {n: given["v_" + n] for n in WEIGHTS}

    mx, my, mc = lax.axis_index("x"), lax.axis_index("y"), lax.axis_index("c")
    meta = jnp.stack([2 * mx + my, mc]).astype(jnp.int32)
    x, mem, pos, target = x[0], mem[0], positions[0], loss_target[0]
    sp = {n: wts[n] for n in SMALL}
    s = x.shape[0]
    cos_t, sin_t = _rope_tables(pos)
    pos_f = pos.astype(F32)
    pos_col, pos_row = pos_f.reshape(s, 1), pos_f.reshape(1, s)
    g128 = _gain_table(sp)
    sinks = sp["swa_sinks"].reshape(SWA_Q_HEADS)
    gcq, gckv = sp["mla_cq_norm_g"], sp["mla_ckv_norm_g"]
    gs = {}

    slot = {n: _cast_into_slot(wts[n][0], meta, name="cast_" + n) for n in BIG if n not in ("w_gate", "w_up", "w_down")}
    first = [slot["w_in"], slot["w_uq"], slot["w_ukv"]]
    slot["w_gate"], [first] = _cast_into_slot(wts["w_gate"][0], meta, name="cast_w_gate",
                                              comm=[_gather_stage(first, "ici", (0, 4))])
    slot["w_up"], [first] = _cast_into_slot(wts["w_up"][0], meta, name="cast_w_up",
                                            comm=[_gather_stage(first, [("ici", (1, 4)), ("d2d", (0, 4))])])
    slot["w_down"], [first] = _cast_into_slot(wts["w_down"][0], meta, name="cast_w_down",
                                              comm=[_gather_stage(first, [("ici", (2, 4)), ("d2d", (1, 4))])])
    hn, [first] = _rms_fwd(x, sp["attn_norm_g"], name="attn_norm_fwd",
                           comm=[_gather_stage(first, [("ici", (3, 4)), ("d2d", (2, 4))])])
    [first] = _run_stages([_gather_stage(first, "d2d", (3, 4))], name="gather_first_last_d2d")
    w_in_f, w_uq_f, w_ukv_f = _full_w_in(first[0]), _full_heads(first[1], MLA_NOPE), _full_heads(first[2], MLA_NOPE)

    proj, [mid] = _matmul(hn, w_in_f, name="in_proj",
                          comm=[_gather_stage([slot["w_mem_kv"], slot["w_out"]], "ici")])
    (qa, ka, va, q_cat, k_cat, v_b, qm), [mid, wg] = _attn_prep_fwd(
        proj, g128, gcq, gckv, w_uq_f, w_ukv_f, cos_t, sin_t,
        comm=[_gather_stage(mid, "d2d"), _gather_stage([slot["w_gate"]], "ici", (0, 4))])
    w_mem_kv_f = mid[0].reshape(D_MODEL, 2 * MEM_HEADS * MEM_DIM)
    w_out_f = mid[1].reshape(D_MODEL, D_MODEL)
    mn_b, kv_m, km, vm = _mem_kv_fwd(mem, sp["mem_norm_g"], w_mem_kv_f, sp["mem_k_norm_g"])
    eighths = lambda leg, ks: [(leg, (k, 8)) for k in ks]
    (y_a, y), [wg] = _swa_fwd(qa, ka, va, pos_col, pos_row, sinks, comm=[_gather_stage(wg, eighths("ici", (2, 3, 4, 5)))])
    (y_b, lse, y), [wg, wu] = _mla_fwd(
        q_cat, k_cat, v_b, y, comm=[_gather_stage(wg, eighths("ici", (6, 7))),
                                    _gather_stage([slot["w_up"]], eighths("ici", (0, 1, 2, 3)))])
    y_m, y = _mem_attn_fwd(qm, km, vm, y)
    h1, [wu, wg] = _matmul(y, w_out_f, add=x, name="out_proj",
                           comm=[_gather_stage(wu, eighths("ici", (4, 5, 6, 7)) + eighths("d2d", (0, 1, 2, 3))),
                                 _gather_stage(wg, "d2d")])
    fn, [wu] = _rms_fwd(h1, sp["ffn_norm_g"], name="ffn_norm_fwd",
                        comm=[_gather_stage(wu, eighths("d2d", (4, 5, 6, 7)))])
    w_gate_f, w_up_f = wg[0], wu[0]
    (gate, up, act), [wd] = _ffn_gate_up(fn, w_gate_f, w_up_f, comm=[_gather_stage([slot["w_down"]], "both")])
    w_down_f = wd[0].reshape(D_FF, D_MODEL)
    d_out, d_out_b, loss_tile = _matmul(act, w_down_f, add=h1, name="down_proj", tm=512, tk=D_FF, loss_target=target)

    add_pair = lambda n, g4, r: _add_pair(meta, g4, r, name="grad_add_pair_" + n)
    add_chips = lambda n, own, r: _add_chips(own, r, name="grad_add_chips_" + n)
    mine, theirs = {}, {}

    dw_down = _matmul(act, d_out_b, ta=True, name="dw_down", tm=512, tn=1024, tk=s)
    dw_down = dw_down.reshape(N_CHIPS, D_FF // N_CHIPS, D_MODEL)
    (d_gate, d_up), [[r]] = _ffn_bwd_act(d_out_b, w_down_f, gate, up, comm=[_halves_stage([dw_down])])
    own_d, oth_d = add_pair("w_down", dw_down, r)
    dw_gate, [rd] = _matmul(fn, d_gate, ta=True, name="dw_gate", tm=512, tk=s, tn=D_FF // N_CHIPS, out_split=N_CHIPS,
                            comm=[_chips_stage([oth_d], (0, 2))])
    dw_up, [[r], rd] = _matmul(fn, d_up, ta=True, name="dw_up", tm=512, tk=s, tn=D_FF // N_CHIPS, out_split=N_CHIPS,
                               comm=[_halves_stage([dw_gate]), _chips_stage([oth_d], (1, 2), into=rd)])
    mine["w_down"] = add_chips("w_down", own_d, rd[0])
    own_g, oth_g = add_pair("w_gate", dw_gate, r)
    d_fn, [[r], rg, [theirs["w_down"]]] = _matmul(
        d_gate, w_gate_f, tb=True, b_split=True, pair2=(d_up, w_up_f), name="dfn", tm=512, tn=512,
        comm=[_halves_stage([dw_up]), _chips_stage([oth_g]), _swap_stage([mine["w_down"]])])
    mine["w_gate"] = add_chips("w_gate", own_g, rg[0])
    own_u, oth_u = add_pair("w_up", dw_up, r)
    d_h1, d_h1_b, gs["ffn_norm_g"] = _rms_bwd(d_fn, h1, sp["ffn_norm_g"], d_out, name="ffn_norm_bwd")
    dw_out, [[theirs["w_gate"]]] = _matmul(y, d_h1_b, ta=True, name="dw_out", tm=512, tk=s,
                                           comm=[_swap_stage([mine["w_gate"]])])
    dw_out = dw_out.reshape(N_CHIPS, D_MODEL // N_CHIPS, D_MODEL)
    d_y, [[r]] = _matmul(d_h1_b, w_out_f, tb=True, name="dy", comm=[_halves_stage([dw_out])])
    own_o, oth_o = add_pair("w_out", dw_out, r)
    (d_qa, d_ka, d_va, d_sink), [ru] = _swa_bwd(qa, ka, va, pos_col, pos_row, sinks, y_a, d_y,
                                                comm=[_chips_stage([oth_u], (0, 2))])
    (d_qcat, d_kcat, d_vb), [ru, [r]] = _mla_bwd(
        q_cat, k_cat, v_b, y_b, lse, d_y, comm=[_chips_stage([oth_u], (1, 2), into=ru), _chips_stage([oth_o])])
    mine["w_up"] = add_chips("w_up", own_u, ru[0])
    mine["w_out"] = add_chips("w_out", own_o, r)
    d_qm, d_km, d_vm = _mem_attn_bwd(qm, km, vm, y_m, d_y)
    (d_proj, dw_uq, dw_ukv, dg128, gs["mla_cq_norm_g"], gs["mla_ckv_norm_g"]), [[theirs["w_up"], theirs["w_out"]]] = \
        _attn_prep_bwd(proj, g128, gcq, gckv, w_uq_f, w_ukv_f, cos_t, sin_t, d_qa, d_ka, d_va, d_qcat, d_kcat, d_vb,
                       d_qm, comm=[_swap_stage([mine["w_up"], mine["w_out"]])])
    dw_mem_kv, gs["mem_norm_g"], gs["mem_k_norm_g"] = _mem_kv_bwd(
        mem, sp["mem_norm_g"], w_mem_kv_f, sp["mem_k_norm_g"], mn_b, kv_m, d_km, d_vm)
    late = ("w_uq", "w_ukv", "w_mem_kv")
    late_g = [_shards_heads(dw_uq, MLA_NOPE, MLA_ROPE), _shards_heads(dw_ukv, MLA_NOPE, MLA_V),
              dw_mem_kv.reshape(N_CHIPS, D_MODEL // N_CHIPS, -1)]
    dw_in, [rs] = _matmul(hn, d_proj, ta=True, name="dw_in", tm=512, tk=s, comm=[_halves_stage(late_g)])
    late_sums = [add_pair(n, g4, r) for n, g4, r in zip(late, late_g, rs)]
    dw_in = _shards_w_in(dw_in)
    d_hn, [rs, [r]] = _matmul(d_proj, w_in_f, tb=True, name="dhn", tk=1536,
                              comm=[_chips_stage([oth for _, oth in late_sums]), _halves_stage([dw_in])])
    for n, (own, _), r_n in zip(late, late_sums, rs):
        mine[n] = add_chips(n, own, r_n)
    own_i, oth_i = add_pair("w_in", dw_in, r)
    (grad_x, _, gs["attn_norm_g"]), [[r], late_theirs] = _rms_bwd(
        d_hn, x, sp["attn_norm_g"], d_h1, name="attn_norm_bwd",
        comm=[_chips_stage([oth_i]), _swap_stage([mine[n] for n in late])])
    theirs.update(zip(late, late_theirs))
    mine["w_in"] = add_chips("w_in", own_i, r)
    [[theirs["w_in"]]] = _run_stages([_swap_stage([mine["w_in"]])], name="grad_swap_w_in")

    fold = lambda r: r[:, :64] + r[:, 64:]
    gs["swa_q_norm_g"] = fold(dg128[G_SWA_Q:G_SWA_Q + 1])
    gs["swa_k_norm_g"] = fold(dg128[G_SWA_K:G_SWA_K + 1])
    gs["mla_qn_norm_g"] = dg128[G_QN:G_QN + 1]
    gs["mla_qr_norm_g"] = fold(dg128[G_QR:G_QR + 1])
    gs["mla_kn_norm_g"] = dg128[G_KN:G_KN + 1]
    gs["mla_kr_norm_g"] = fold(dg128[G_KR:G_KR + 1])
    gs["mem_q_norm_g"] = dg128[G_MQ:G_MQ + 1]
    gs["swa_sinks"] = d_sink[:, :SWA_Q_HEADS]

    grad, delta, new_m, new_v = {}, {}, {}, {}
    for n in BIG:
        g2, d, m2, v2 = _adamw(meta, wts[n][0], mine[n], theirs[n], mom_m[n][0], mom_v[n][0], name="adamw_" + n)
        grad[n], delta[n], new_m[n], new_v[n] = g2[None], d[None], m2[None], v2[None]

    sizes = [wts[n].shape[1] for n in SMALL]
    zero = jnp.zeros((1, LANES), F32)
    packs = _small_step(_pack([gs[n] for n in SMALL] + [loss_tile]), _pack([wts[n] for n in SMALL] + [zero]),
                        _pack([mom_m[n] for n in SMALL] + [zero]), _pack([mom_v[n] for n in SMALL] + [zero]))
    for store, buf in zip((grad, delta, new_m, new_v), packs):
        for n, val in zip(SMALL, _unpack(buf, sizes)):
            store[n] = val
    loss = _unpack(packs[0], sizes + [LANES])[-1][0, 0]

    return (loss, grad_x[None], *[grad[n] for n in WEIGHTS], *[delta[n] for n in WEIGHTS],
            *[new_m[n] for n in WEIGHTS], *[new_v[n] for n in WEIGHTS])
```

```python
import functools
import math

import jax
import jax.numpy as jnp
from jax import lax
from jax.experimental import pallas as pl
from jax.experimental.pallas import tpu as pltpu

F32 = jnp.float32
BF16 = jnp.bfloat16

D_MODEL = 2048
BLOCK = 128
EPS = 1e-6
NEG_INF = -1e30
SWA_Q_HEADS = 16
SWA_KV_HEADS = 2
SWA_HEAD_DIM = 64
MLA_HEADS = 4
MLA_RANK = 512
MLA_NOPE = 128
MLA_ROPE = 64
MLA_V = 128
ROPE_THETA = 10000.0
MEM_HEADS = 4
MEM_DIM = 128
D_FF = 5632
IN_WIDTH = 2880
IN_PAD = 3072
N_CHIPS = 4

ADAM_LR = 0.001
ADAM_B1 = 0.9
ADAM_B2 = 0.999
ADAM_EPS = 1e-08
ADAM_WD = 0.01
ADAM_STEP = 10

VMEM_LIMIT_BYTES = 56 * 1024 * 1024
LANES = 128

MESH = pl.DeviceIdType.MESH


def _params(sem=None, **kw):
    return pltpu.CompilerParams(dimension_semantics=sem, vmem_limit_bytes=VMEM_LIMIT_BYTES, **kw)


def _tile(n, want):
    if n <= want:
        return n
    t = want - want % LANES
    while t > 0:
        if n % t == 0:
            return t
        t -= LANES
    return n


ANY = pl.BlockSpec(memory_space=pl.ANY)


class _Stage:
    def __init__(self, ins, out_shapes, aliases, n_sem, issue, wait, mid=None):
        self.ins, self.out_shapes, self.aliases, self.n_sem = list(ins), list(out_shapes), dict(aliases), n_sem
        self.issue, self.wait, self.mid = issue, wait, mid


def _pcall(body, args, *, name, grid, in_specs, out_specs, out_shape, scratch_shapes=(), sem=None, comm=(),
           prefetch=(), io_alias=None):
    multi = isinstance(out_shape, (list, tuple))
    out_specs_l = list(out_specs) if multi else [out_specs]
    out_shape_l = list(out_shape) if multi else [out_shape]
    npf = len(prefetch)
    own_aliases = {npf + a: o for a, o in (io_alias or {}).items()}

    def call(fn, in_specs_, out_specs_, out_shape_, scratch_, operands, sem_, aliases=None):
        kw = dict(name=name, out_shape=out_shape_, compiler_params=_params(sem_))
        if aliases:
            kw["input_output_aliases"] = aliases
        if npf:
            spec = pltpu.PrefetchScalarGridSpec(num_scalar_prefetch=npf, grid=grid, in_specs=in_specs_,
                                                out_specs=out_specs_, scratch_shapes=scratch_)
            return pl.pallas_call(fn, grid_spec=spec, **kw)(*prefetch, *operands)
        return pl.pallas_call(fn, grid=grid, in_specs=in_specs_, out_specs=out_specs_, scratch_shapes=scratch_,
                              **kw)(*operands)

    if not comm:
        return call(body, list(in_specs), out_specs, out_shape, list(scratch_shapes), args, sem, own_aliases)
    n_in, n_out, n_scr = len(in_specs), len(out_specs_l), len(scratch_shapes)
    cins = [a for st in comm for a in st.ins]
    couts = [s for st in comm for s in st.out_shapes]
    aliases, ci, co = dict(own_aliases), 0, 0
    for st in comm:
        for a_i, o_i in st.aliases.items():
            aliases[npf + n_in + ci + a_i] = n_out + co + o_i
        ci, co = ci + len(st.ins), co + len(st.out_shapes)

    def wrapped(*refs):
        pre = refs[:npf]
        p = npf
        ins = refs[p:p + n_in]; p += n_in
        cin_refs = refs[p:p + len(cins)]; p += len(cins)
        outs = refs[p:p + n_out]; p += n_out
        cout_refs = refs[p:p + len(couts)]; p += len(couts)
        scr = refs[p:p + n_scr]; p += n_scr
        sems = refs[p:]
        first = functools.reduce(jnp.logical_and, [pl.program_id(a) == 0 for a in range(len(grid))])
        last = functools.reduce(jnp.logical_and, [pl.program_id(a) == grid[a] - 1 for a in range(len(grid))])

        def each(what):
            i, o = 0, 0
            for k, st in enumerate(comm):
                fn = getattr(st, what)
                if fn is not None:
                    fn(cin_refs[i:i + len(st.ins)], cout_refs[o:o + len(st.out_shapes)], sems[2 * k], sems[2 * k + 1])
                i, o = i + len(st.ins), o + len(st.out_shapes)

        @pl.when(first)
        def _():
            each("issue")

        if any(st.mid is not None for st in comm):
            n_steps = math.prod(grid)
            assert n_steps >= 4, "a two-leg stage needs a carrier with several grid steps"
            lin = functools.reduce(lambda acc, a: acc * grid[a] + pl.program_id(a), range(len(grid)), 0)

            @pl.when(lin == (3 * n_steps) // 4)
            def _():
                each("mid")

        body(*pre, *ins, *outs, *scr)

        @pl.when(last)
        def _():
            each("wait")

    sem_scr = [pltpu.SemaphoreType.DMA((st.n_sem,)) for st in comm for _ in range(2)]
    res = call(wrapped, list(in_specs) + [ANY] * len(cins), out_specs_l + [ANY] * len(couts), out_shape_l + couts,
               list(scratch_shapes) + sem_scr, (*args, *cins), ("arbitrary",) * len(grid), aliases)
    normal = list(res[:n_out])
    stage_outs, o = [], n_out
    for st in comm:
        stage_outs.append(list(res[o:o + len(st.out_shapes)]))
        o += len(st.out_shapes)
    return (normal if multi else normal[0]), stage_outs


def _matmul(a, b, *, name, ta=False, tb=False, add=None, out_dtype=F32, tm=1024, tn=1024, tk=2048,
            b_split=False, out_split=0, comm=(), loss_target=None, pair2=None):
    if ta:
        kdim, m = a.shape
    else:
        m, kdim = a.shape
    if b_split:
        assert tb
        nsp, n, kb = b.shape
        kb = kb * nsp
    elif tb:
        n, kb = b.shape
    else:
        kb, n = b.shape
    assert kb == kdim, (a.shape, b.shape, ta, tb)
    if b_split:
        tk = kdim
    if out_split:
        tn = _tile(n // out_split, tn)
    tm, tn, tk = _tile(m, tm), _tile(n, tn), _tile(kdim, tk)
    nk = kdim // tk
    dims = (((0 if ta else 1,), (1 if tb else 0,)), ((), ()))

    def product(a_ref, b_ref):
        if not b_split:
            return lax.dot_general(a_ref[...].astype(BF16), b_ref[...].astype(BF16), dims, preferred_element_type=F32)
        per = kdim // nsp
        return sum(lax.dot_general(a_ref[:, per * c:per * (c + 1)].astype(BF16), b_ref[c].astype(BF16), dims,
                                   preferred_element_type=F32) for c in range(nsp))

    def body(*refs):
        a_ref, b_ref = refs[:2]
        n_ab = 4 if pair2 is not None else 2
        add_ref = refs[n_ab] if add is not None else None
        n_in = n_ab + (add is not None) + (loss_target is not None)

        def products():
            r = product(a_ref, b_ref)
            return r if pair2 is None else r + product(refs[2], refs[3])
        o_ref = refs[n_in]

        def finish(r):
            if add_ref is not None:
                r = r + add_ref[...].astype(F32)
            if loss_target is None:
                o_ref[...] = r.astype(o_ref.dtype)
                return
            db_ref, l_ref = refs[n_in + 1], refs[n_in + 2]
            err = r - refs[n_in - 1][...]
            d_out = err * (1.0 / n)
            o_ref[...] = d_out
            db_ref[...] = d_out.astype(BF16)
            part = jnp.broadcast_to((0.5 / n) * jnp.sum(jnp.sum(err * err, axis=-1, keepdims=True), axis=0, keepdims=True),
                                    (1, LANES))
            first = jnp.logical_and(pl.program_id(0) == 0, pl.program_id(1) == 0)

            @pl.when(first)
            def _():
                l_ref[...] = part

            @pl.when(jnp.logical_not(first))
            def _():
                l_ref[...] += part

        if nk == 1:
            finish(products())
            return
        acc_ref = refs[-1]
        k = pl.program_id(2)
        part = products()

        @pl.when(k == 0)
        def _():
            acc_ref[...] = part

        @pl.when(k > 0)
        def _():
            acc_ref[...] += part

        @pl.when(k == nk - 1)
        def _():
            finish(acc_ref[...])

    a_spec = pl.BlockSpec((tk, tm), lambda i, j, k: (k, i)) if ta else pl.BlockSpec((tm, tk), lambda i, j, k: (i, k))
    if b_split:
        b_spec = pl.BlockSpec((nsp, tn, kdim // nsp), lambda i, j, k: (0, j, 0))
    elif tb:
        b_spec = pl.BlockSpec((tn, tk), lambda i, j, k: (j, k))
    else:
        b_spec = pl.BlockSpec((tk, tn), lambda i, j, k: (k, j))
    in_specs = [a_spec, b_spec]
    args = [a, b]
    if pair2 is not None:
        assert pair2[0].shape == a.shape and pair2[1].shape == b.shape
        in_specs += [a_spec, b_spec]
        args += list(pair2)
    if add is not None:
        in_specs.append(pl.BlockSpec((tm, tn), lambda i, j, k: (i, j)))
        args.append(add)
    tile = pl.BlockSpec((tm, tn), lambda i, j, k: (i, j))
    sem = ("parallel", "parallel", "arbitrary")
    if out_split:
        per = (n // out_split) // tn
        out_spec = pl.BlockSpec((None, tm, tn), lambda i, j, k: (j // per, i, j % per))
        out_shape = jax.ShapeDtypeStruct((out_split, m, n // out_split), out_dtype)
    elif loss_target is not None:
        in_specs.append(tile)
        args.append(loss_target)
        out_spec = [tile, tile, pl.BlockSpec((1, LANES), lambda i, j, k: (0, 0))]
        out_shape = [jax.ShapeDtypeStruct((m, n), F32), jax.ShapeDtypeStruct((m, n), BF16),
                     jax.ShapeDtypeStruct((1, LANES), F32)]
        sem = ("arbitrary",) * 3
    else:
        out_spec = tile
        out_shape = jax.ShapeDtypeStruct((m, n), out_dtype)
    return _pcall(body, args, name=name, grid=(m // tm, n // tn, nk), in_specs=in_specs, out_specs=out_spec,
                  out_shape=out_shape, scratch_shapes=[pltpu.VMEM((tm, tn), F32)] if nk > 1 else [],
                  sem=sem, comm=comm)


def _rms_fwd(x, g, *, name, tm=512, comm=()):
    s, d = x.shape
    tm = _tile(s, tm)

    def body(x_ref, g_ref, o_ref):
        xv = x_ref[...]
        r = lax.rsqrt(jnp.mean(xv * xv, axis=-1, keepdims=True) + EPS)
        o_ref[...] = (xv * r * g_ref[...]).astype(o_ref.dtype)

    return _pcall(body, (x, g), name=name, grid=(s // tm,),
                  in_specs=[pl.BlockSpec((tm, d), lambda i: (i, 0)), pl.BlockSpec((1, d), lambda i: (0, 0))],
                  out_specs=pl.BlockSpec((tm, d), lambda i: (i, 0)),
                  out_shape=jax.ShapeDtypeStruct((s, d), BF16), sem=("parallel",), comm=comm)


def _rms_bwd(dy, x, g, res, *, name, tm=512, comm=()):
    s, d = x.shape
    tm = _tile(s, tm)

    def body(dy_ref, x_ref, g_ref, res_ref, dx_ref, dxb_ref, dg_ref):
        xv = x_ref[...]
        dyv = dy_ref[...]
        r = lax.rsqrt(jnp.mean(xv * xv, axis=-1, keepdims=True) + EPS)
        xhat = xv * r
        dyg = dyv * g_ref[...]
        mt = jnp.mean(dyg * xhat, axis=-1, keepdims=True)
        dx = res_ref[...] + r * (dyg - xhat * mt)
        dx_ref[...] = dx
        dxb_ref[...] = dx.astype(BF16)
        part = jnp.sum(dyv * xhat, axis=0, keepdims=True)

        @pl.when(pl.program_id(0) == 0)
        def _():
            dg_ref[...] = part

        @pl.when(pl.program_id(0) > 0)
        def _():
            dg_ref[...] += part

    row = pl.BlockSpec((tm, d), lambda i: (i, 0))
    vec = pl.BlockSpec((1, d), lambda i: (0, 0))
    return _pcall(body, (dy, x, g, res), name=name, grid=(s // tm,), in_specs=[row, row, vec, row],
                  out_specs=[row, row, vec],
                  out_shape=[jax.ShapeDtypeStruct((s, d), F32), jax.ShapeDtypeStruct((s, d), BF16),
                             jax.ShapeDtypeStruct((1, d), F32)],
                  sem=("arbitrary",), comm=comm)


def _lane(shape):
    return lax.broadcasted_iota(jnp.int32, shape, 1)


def _halfsum(t, lo):
    s_lo = jnp.sum(jnp.where(lo, t, 0.0), axis=-1, keepdims=True)
    s_hi = jnp.sum(jnp.where(lo, 0.0, t), axis=-1, keepdims=True)
    return jnp.where(lo, s_lo, s_hi)


def _norm_pair(x, g, lo):
    r = lax.rsqrt(_halfsum(x * x, lo) * (1.0 / 64.0) + EPS)
    xhat = x * r
    return xhat * g, xhat, r


def _norm_pair_bwd(dy, g, xhat, r, lo):
    dyg = dy * g
    mt = _halfsum(dyg * xhat, lo) * (1.0 / 64.0)
    return r * (dyg - xhat * mt), jnp.sum(dy * xhat, axis=0, keepdims=True)


def _norm_full(x, g):
    r = lax.rsqrt(jnp.mean(x * x, axis=-1, keepdims=True) + EPS)
    xhat = x * r
    return xhat * g, xhat, r


def _norm_full_bwd(dy, g, xhat, r):
    dyg = dy * g
    mt = jnp.mean(dyg * xhat, axis=-1, keepdims=True)
    return r * (dyg - xhat * mt), jnp.sum(dy * xhat, axis=0, keepdims=True)


def _rot(x, first32):
    return jnp.where(first32, pltpu.roll(x, 96, axis=1), pltpu.roll(x, 32, axis=1))


def _rope(x, cos_t, sin_t, first32):
    return x * cos_t + _rot(x, first32) * sin_t


def _rope_bwd(dy, cos_t, sin_t, first32):
    return dy * cos_t + _rot(dy * sin_t, first32)


G_SWA_Q, G_SWA_K, G_QN, G_QR, G_KN, G_KR, G_MQ = range(7)

C_QA, C_KA, C_VA, C_CQ, C_CKV, C_QM, C_KR = 0, 1024, 1152, 1280, 1792, 2304, 2816


def _prep_common(p_ref, g128_ref, gcq_ref, gckv_ref, wuq_ref, wukv_ref, cos_ref, sin_ref):
    tm = p_ref.shape[0]
    lane = _lane((tm, LANES))
    lo = lane < 64
    first32 = (lane % 64) < 32
    cos_t = cos_ref[...]
    sin_t = sin_ref[...]
    g = lambda row: g128_ref[row:row + 1, :]
    out = dict(lo=lo, first32=first32, cos_t=cos_t, sin_t=sin_t, lane=lane)
    cq_n, cq_hat, cq_r = _norm_full(p_ref[:, C_CQ:C_CQ + MLA_RANK], gcq_ref[...])
    ckv_n, ckv_hat, ckv_r = _norm_full(p_ref[:, C_CKV:C_CKV + MLA_RANK], gckv_ref[...])
    cq_b = cq_n.astype(BF16)
    ckv_b = ckv_n.astype(BF16)
    q_b = jnp.dot(cq_b, wuq_ref[...], preferred_element_type=F32)
    kv_b = jnp.dot(ckv_b, wukv_ref[...], preferred_element_type=F32)
    out.update(cq_b=cq_b, cq_hat=cq_hat, cq_r=cq_r, ckv_b=ckv_b, ckv_hat=ckv_hat, ckv_r=ckv_r, q_b=q_b, kv_b=kv_b, g=g)
    return out


def _attn_prep_fwd(proj, g128, gcq, gckv, wuq, wukv, cos_t, sin_t, *, tm=512, comm=()):
    s = proj.shape[0]
    tm = _tile(s, tm)

    def body(p_ref, g128_ref, gcq_ref, gckv_ref, wuq_ref, wukv_ref, cos_ref, sin_ref,
             qa_ref, ka_ref, va_ref, qcat_ref, kcat_ref, vb_ref, qm_ref):
        c = _prep_common(p_ref, g128_ref, gcq_ref, gckv_ref, wuq_ref, wukv_ref, cos_ref, sin_ref)
        lo, first32, g = c["lo"], c["first32"], c["g"]
        for j in range(SWA_Q_HEADS // 2):
            y, _, _ = _norm_pair(p_ref[:, C_QA + 128 * j:C_QA + 128 * (j + 1)], g(G_SWA_Q), lo)
            qa_ref[:, 128 * j:128 * (j + 1)] = y.astype(BF16)
        y, _, _ = _norm_pair(p_ref[:, C_KA:C_KA + 128], g(G_SWA_K), lo)
        ka_ref[...] = y.astype(BF16)
        va_ref[...] = p_ref[:, C_VA:C_VA + 128].astype(BF16)
        kr, _, _ = _norm_pair(p_ref[:, C_KR:C_KR + 128], g(G_KR), lo)
        kr = jnp.where(lo, _rope(kr, c["cos_t"], c["sin_t"], first32), 0.0)
        krkr = (kr + pltpu.roll(kr, 64, axis=1)).astype(BF16)
        q_b, kv_b = c["q_b"], c["kv_b"]
        qr = []
        for j in range(MLA_HEADS // 2):
            y, _, _ = _norm_pair(q_b[:, 512 + 128 * j:512 + 128 * (j + 1)], g(G_QR), lo)
            qr.append(_rope(y, c["cos_t"], c["sin_t"], first32))
        for h in range(MLA_HEADS):
            qn, _, _ = _norm_full(q_b[:, 128 * h:128 * (h + 1)], g(G_QN))
            keep = lo if h % 2 == 0 else jnp.logical_not(lo)
            qcat_ref[h, :, 0:128] = qn.astype(BF16)
            qcat_ref[h, :, 128:256] = jnp.where(keep, qr[h // 2], 0.0).astype(BF16)
            kn, _, _ = _norm_full(kv_b[:, 128 * h:128 * (h + 1)], g(G_KN))
            kcat_ref[h, :, 0:128] = kn.astype(BF16)
            kcat_ref[h, :, 128:256] = krkr
        vb_ref[...] = kv_b[:, 512:1024].astype(BF16)
        for h in range(MEM_HEADS):
            y, _, _ = _norm_full(p_ref[:, C_QM + 128 * h:C_QM + 128 * (h + 1)], g(G_MQ))
            qm_ref[:, 128 * h:128 * (h + 1)] = y.astype(BF16)

    row = lambda w: pl.BlockSpec((tm, w), lambda i: (i, 0))
    full = lambda shape: pl.BlockSpec(shape, lambda i: tuple(0 for _ in shape))
    cat = pl.BlockSpec((MLA_HEADS, tm, 256), lambda i: (0, i, 0))
    return _pcall(
        body, (proj, g128, gcq, gckv, wuq, wukv, cos_t, sin_t), name="attn_prep_fwd", grid=(s // tm,),
        in_specs=[row(IN_PAD), full((8, 128)), full((1, 512)), full((1, 512)), full((512, 768)), full((512, 1024)),
                  row(128), row(128)],
        out_specs=[row(1024), row(128), row(128), cat, cat, row(512), row(512)],
        out_shape=[jax.ShapeDtypeStruct((s, 1024), BF16), jax.ShapeDtypeStruct((s, 128), BF16),
                   jax.ShapeDtypeStruct((s, 128), BF16), jax.ShapeDtypeStruct((MLA_HEADS, s, 256), BF16),
                   jax.ShapeDtypeStruct((MLA_HEADS, s, 256), BF16), jax.ShapeDtypeStruct((s, 512), BF16),
                   jax.ShapeDtypeStruct((s, 512), BF16)],
        sem=("parallel",), comm=comm)


def _attn_prep_bwd(proj, g128, gcq, gckv, wuq, wukv, cos_t, sin_t,
                   d_qa, d_ka, d_va, d_qcat, d_kcat, d_vb, d_qm, *, tm=512, comm=()):
    s = proj.shape[0]
    tm = _tile(s, tm)

    def body(p_ref, g128_ref, gcq_ref, gckv_ref, wuq_ref, wukv_ref, cos_ref, sin_ref,
             dqa_ref, dka_ref, dva_ref, dqcat_ref, dkcat_ref, dvb_ref, dqm_ref,
             dp_ref, dwuq_ref, dwukv_ref, dg128_ref, dgcq_ref, dgckv_ref):
        c = _prep_common(p_ref, g128_ref, gcq_ref, gckv_ref, wuq_ref, wukv_ref, cos_ref, sin_ref)
        lo, first32, g = c["lo"], c["first32"], c["g"]
        cos_v, sin_v = c["cos_t"], c["sin_t"]
        q_b, kv_b = c["q_b"], c["kv_b"]
        zero_row = jnp.zeros((1, LANES), F32)
        dg = {k: zero_row for k in range(7)}

        for j in range(SWA_Q_HEADS // 2):
            sl = slice(C_QA + 128 * j, C_QA + 128 * (j + 1))
            _, xhat, r = _norm_pair(p_ref[:, sl], g(G_SWA_Q), lo)
            dx, dgj = _norm_pair_bwd(dqa_ref[:, 128 * j:128 * (j + 1)], g(G_SWA_Q), xhat, r, lo)
            dp_ref[:, sl] = dx.astype(BF16)
            dg[G_SWA_Q] = dg[G_SWA_Q] + dgj
        _, xhat, r = _norm_pair(p_ref[:, C_KA:C_KA + 128], g(G_SWA_K), lo)
        dx, dgj = _norm_pair_bwd(dka_ref[...], g(G_SWA_K), xhat, r, lo)
        dp_ref[:, C_KA:C_KA + 128] = dx.astype(BF16)
        dg[G_SWA_K] = dgj
        dp_ref[:, C_VA:C_VA + 128] = dva_ref[...].astype(BF16)

        dqb_parts = [None] * 6
        for h in range(MLA_HEADS):
            _, xhat, r = _norm_full(q_b[:, 128 * h:128 * (h + 1)], g(G_QN))
            dx, dgj = _norm_full_bwd(dqcat_ref[h, :, 0:128], g(G_QN), xhat, r)
            dqb_parts[h] = dx
            dg[G_QN] = dg[G_QN] + dgj
        for j in range(MLA_HEADS // 2):
            _, xhat, r = _norm_pair(q_b[:, 512 + 128 * j:512 + 128 * (j + 1)], g(G_QR), lo)
            d_rot = jnp.where(lo, dqcat_ref[2 * j, :, 128:256], dqcat_ref[2 * j + 1, :, 128:256])
            d_y = _rope_bwd(d_rot, cos_v, sin_v, first32)
            dx, dgj = _norm_pair_bwd(d_y, g(G_QR), xhat, r, lo)
            dqb_parts[4 + j] = dx
            dg[G_QR] = dg[G_QR] + dgj
        d_qb = jnp.concatenate(dqb_parts, axis=1).astype(BF16)
        dwuq = lax.dot_general(c["cq_b"], d_qb, (((0,), (0,)), ((), ())), preferred_element_type=F32)
        d_cqn = lax.dot_general(d_qb, wuq_ref[...], (((1,), (1,)), ((), ())), preferred_element_type=F32)
        dx, dgcq = _norm_full_bwd(d_cqn, gcq_ref[...], c["cq_hat"], c["cq_r"])
        dp_ref[:, C_CQ:C_CQ + MLA_RANK] = dx.astype(BF16)

        dkv_parts = []
        d_krkr = jnp.zeros((p_ref.shape[0], LANES), F32)
        for h in range(MLA_HEADS):
            _, xhat, r = _norm_full(kv_b[:, 128 * h:128 * (h + 1)], g(G_KN))
            dx, dgj = _norm_full_bwd(dkcat_ref[h, :, 0:128], g(G_KN), xhat, r)
            dkv_parts.append(dx)
            dg[G_KN] = dg[G_KN] + dgj
            d_krkr = d_krkr + dkcat_ref[h, :, 128:256]
        d_kvb = jnp.concatenate(dkv_parts + [dvb_ref[...]], axis=1).astype(BF16)
        dwukv = lax.dot_general(c["ckv_b"], d_kvb, (((0,), (0,)), ((), ())), preferred_element_type=F32)
        d_ckvn = lax.dot_general(d_kvb, wukv_ref[...], (((1,), (1,)), ((), ())), preferred_element_type=F32)
        dx, dgckv = _norm_full_bwd(d_ckvn, gckv_ref[...], c["ckv_hat"], c["ckv_r"])
        dp_ref[:, C_CKV:C_CKV + MLA_RANK] = dx.astype(BF16)

        _, xhat, r = _norm_pair(p_ref[:, C_KR:C_KR + 128], g(G_KR), lo)
        d_kr = jnp.where(lo, d_krkr + pltpu.roll(d_krkr, 64, axis=1), 0.0)
        d_y = jnp.where(lo, _rope_bwd(d_kr, cos_v, sin_v, first32), 0.0)
        dx, dgj = _norm_pair_bwd(d_y, g(G_KR), xhat, r, lo)
        dp_ref[:, C_KR:C_KR + 128] = jnp.where(lo, dx, 0.0).astype(BF16)
        dp_ref[:, C_KR + 128:] = jnp.zeros((p_ref.shape[0], IN_PAD - C_KR - 128), BF16)
        dg[G_KR] = dgj

        for h in range(MEM_HEADS):
            sl = slice(C_QM + 128 * h, C_QM + 128 * (h + 1))
            _, xhat, r = _norm_full(p_ref[:, sl], g(G_MQ))
            dx, dgj = _norm_full_bwd(dqm_ref[:, 128 * h:128 * (h + 1)], g(G_MQ), xhat, r)
            dp_ref[:, sl] = dx.astype(BF16)
            dg[G_MQ] = dg[G_MQ] + dgj

        dg_tile = jnp.concatenate([dg[k] for k in range(7)] + [zero_row], axis=0)

        @pl.when(pl.program_id(0) == 0)
        def _():
            dwuq_ref[...] = dwuq
            dwukv_ref[...] = dwukv
            dg128_ref[...] = dg_tile
            dgcq_ref[...] = dgcq
            dgckv_ref[...] = dgckv

        @pl.when(pl.program_id(0) > 0)
        def _():
            dwuq_ref[...] += dwuq
            dwukv_ref[...] += dwukv
            dg128_ref[...] += dg_tile
            dgcq_ref[...] += dgcq
            dgckv_ref[...] += dgckv

    row = lambda w: pl.BlockSpec((tm, w), lambda i: (i, 0))
    full = lambda shape: pl.BlockSpec(shape, lambda i: tuple(0 for _ in shape))
    cat = pl.BlockSpec((MLA_HEADS, tm, 256), lambda i: (0, i, 0))
    return _pcall(
        body, (proj, g128, gcq, gckv, wuq, wukv, cos_t, sin_t, d_qa, d_ka, d_va, d_qcat, d_kcat, d_vb, d_qm),
        name="attn_prep_bwd", grid=(s // tm,),
        in_specs=[row(IN_PAD), full((8, 128)), full((1, 512)), full((1, 512)), full((512, 768)), full((512, 1024)),
                  row(128), row(128),
                  row(1024), row(128), row(128), cat, cat, row(512), row(512)],
        out_specs=[row(IN_PAD), full((512, 768)), full((512, 1024)), full((8, 128)), full((1, 512)), full((1, 512))],
        out_shape=[jax.ShapeDtypeStruct((s, IN_PAD), BF16), jax.ShapeDtypeStruct((512, 768), F32),
                   jax.ShapeDtypeStruct((512, 1024), F32), jax.ShapeDtypeStruct((8, 128), F32),
                   jax.ShapeDtypeStruct((1, 512), F32), jax.ShapeDtypeStruct((1, 512), F32)],
        sem=("arbitrary",), comm=comm)


SWA_SLOPES = tuple(2.0 ** (-8.0 * h / SWA_Q_HEADS) for h in range(1, SWA_Q_HEADS + 1))
SWA_SCALE = SWA_HEAD_DIM ** -0.5
NT_DIMS = (((1,), (1,)), ((), ()))
TN_DIMS = (((0,), (0,)), ((), ()))


def _swa_span(n, kp_ref, kc_ref, vp_ref, vc_ref, pcol_ref, pprow_ref, pcrow_ref):
    k_span = jnp.concatenate([kp_ref[...], kc_ref[...]], axis=0).astype(F32)
    v_span = jnp.concatenate([vp_ref[...], vc_ref[...]], axis=0).astype(F32)
    lo = _lane((2 * BLOCK, LANES)) < 64
    k_sw = pltpu.roll(k_span, 64, axis=1)
    v_sw = pltpu.roll(v_span, 64, axis=1)
    kk = (jnp.where(lo, k_span, k_sw).astype(BF16), jnp.where(lo, k_sw, k_span).astype(BF16))
    vv_lo = (jnp.where(lo, v_span, 0.0).astype(BF16), jnp.where(lo, v_sw, 0.0).astype(BF16))
    vv_hi = (jnp.where(lo, 0.0, v_sw).astype(BF16), jnp.where(lo, 0.0, v_span).astype(BF16))
    pk = jnp.concatenate([pprow_ref[...], pcrow_ref[...]], axis=1)
    dist = jnp.abs(pcol_ref[...] - pk)
    qi = lax.broadcasted_iota(jnp.int32, (BLOCK, 2 * BLOCK), 0)
    ki = lax.broadcasted_iota(jnp.int32, (BLOCK, 2 * BLOCK), 1)
    first_key = jnp.where(n > 0, qi + 1, jnp.maximum(qi + 1, BLOCK))
    valid = jnp.logical_and(ki >= first_key, ki <= qi + BLOCK)
    mask_add = jnp.where(valid, 0.0, NEG_INF)
    return kk, vv_lo, vv_hi, dist, mask_add


def _swa_heads(q_ref, lo):
    heads = []
    for j in range(SWA_Q_HEADS // 2):
        q_pair = q_ref[:, 128 * j:128 * (j + 1)].astype(F32)
        for par in (0, 1):
            q_h = jnp.where(lo if par == 0 else jnp.logical_not(lo), q_pair, 0.0).astype(BF16)
            heads.append((2 * j + par, (2 * j) // (SWA_Q_HEADS // SWA_KV_HEADS), par, q_h))
    return heads


def _swa_probs(raw, dist, mask_add, slope, sink):
    s = raw * SWA_SCALE - slope * dist + mask_add
    m = jnp.maximum(jnp.max(s, axis=-1, keepdims=True), sink)
    e = jnp.exp(s - m)
    e_sink = jnp.exp(sink - m)
    inv = 1.0 / (jnp.sum(e, axis=-1, keepdims=True) + e_sink)
    return e * inv, e_sink * inv


def _swa_specs():
    blk = lambda w: pl.BlockSpec((BLOCK, w), lambda n: (n, 0))
    prev = lambda w: pl.BlockSpec((BLOCK, w), lambda n: (jnp.maximum(n - 1, 0), 0))
    prow_c = pl.BlockSpec((1, BLOCK), lambda n: (0, n))
    prow_p = pl.BlockSpec((1, BLOCK), lambda n: (0, jnp.maximum(n - 1, 0)))
    smem = pl.BlockSpec(memory_space=pltpu.SMEM)
    return [blk(1024), prev(128), blk(128), prev(128), blk(128), blk(1), prow_p, prow_c, smem], blk


def _swa_fwd(qa, ka, va, pos_col, pos_row, sinks, *, comm=()):
    s = qa.shape[0]
    in_specs, blk = _swa_specs()

    def body(q_ref, kp_ref, kc_ref, vp_ref, vc_ref, pcol_ref, pprow_ref, pcrow_ref, sink_ref, o_ref, yb_ref):
        n = pl.program_id(0)
        kk, vv_lo, vv_hi, dist, mask_add = _swa_span(n, kp_ref, kc_ref, vp_ref, vc_ref, pcol_ref, pprow_ref, pcrow_ref)
        lo = _lane((BLOCK, LANES)) < 64
        heads = _swa_heads(q_ref, lo)
        raws = [lax.dot_general(q_h, kk[kv], NT_DIMS, preferred_element_type=F32) for _, kv, _, q_h in heads]
        probs = [_swa_probs(raw, dist, mask_add, SWA_SLOPES[h], sink_ref[h])[0].astype(BF16)
                 for raw, (h, _, _, _) in zip(raws, heads)]
        for j in range(SWA_Q_HEADS // 2):
            kv = heads[2 * j][1]
            out = (jnp.dot(probs[2 * j], vv_lo[kv], preferred_element_type=F32)
                   + jnp.dot(probs[2 * j + 1], vv_hi[kv], preferred_element_type=F32))
            o_ref[:, 128 * j:128 * (j + 1)] = out
            yb_ref[:, 128 * j:128 * (j + 1)] = out.astype(BF16)

    return _pcall(body, (qa, ka, ka, va, va, pos_col, pos_row, pos_row, sinks), name="swa_fwd", grid=(s // BLOCK,),
                  in_specs=in_specs, out_specs=[blk(1024), blk(1024)],
                  out_shape=[jax.ShapeDtypeStruct((s, 1024), F32), jax.ShapeDtypeStruct((s, D_MODEL), BF16)],
                  sem=("parallel",), comm=comm)


def _swa_bwd(qa, ka, va, pos_col, pos_row, sinks, y_a, d_y, *, comm=()):
    s = qa.shape[0]
    in_specs, blk = _swa_specs()
    whole = pl.BlockSpec((s, 128), lambda n: (0, 0))

    def body(q_ref, kp_ref, kc_ref, vp_ref, vc_ref, pcol_ref, pprow_ref, pcrow_ref, sink_ref, y_ref, dy_ref,
             dq_ref, dk_ref, dv_ref, dsink_ref):
        n = pl.program_id(0)

        @pl.when(n == 0)
        def _():
            dk_ref[...] = jnp.zeros_like(dk_ref)
            dv_ref[...] = jnp.zeros_like(dv_ref)
            dsink_ref[...] = jnp.zeros_like(dsink_ref)

        kk, vv_lo, vv_hi, dist, mask_add = _swa_span(n, kp_ref, kc_ref, vp_ref, vc_ref, pcol_ref, pprow_ref, pcrow_ref)
        lo = _lane((BLOCK, LANES)) < 64
        lo2 = _lane((2 * BLOCK, LANES)) < 64
        lane1 = _lane((1, LANES))
        dsink = jnp.zeros((1, LANES), F32)
        dkk = [jnp.zeros((2 * BLOCK, LANES), F32) for _ in range(SWA_KV_HEADS)]
        dvv = [jnp.zeros((2 * BLOCK, LANES), F32) for _ in range(SWA_KV_HEADS)]
        heads = _swa_heads(q_ref, lo)
        do_b, deltas = [], []
        for j in range(SWA_Q_HEADS // 2):
            do_pair = dy_ref[:, 128 * j:128 * (j + 1)]
            doy = do_pair * y_ref[:, 128 * j:128 * (j + 1)]
            do_b.append(do_pair.astype(BF16))
            deltas.append(jnp.sum(jnp.where(lo, doy, 0.0), axis=-1, keepdims=True))
            deltas.append(jnp.sum(jnp.where(lo, 0.0, doy), axis=-1, keepdims=True))
        raws = [lax.dot_general(q_h, kk[kv], NT_DIMS, preferred_element_type=F32) for _, kv, _, q_h in heads]
        dps = [lax.dot_general(do_b[h // 2], (vv_lo, vv_hi)[par][kv], NT_DIMS, preferred_element_type=F32)
               for h, kv, par, _ in heads]
        p_b, ds_b = [], []
        for h, kv, par, _ in heads:
            p, p_sink = _swa_probs(raws[h], dist, mask_add, SWA_SLOPES[h], sink_ref[h])
            ds = p * (dps[h] - deltas[h])
            dsink = dsink + jnp.where(lane1 == h, -jnp.sum(p_sink * deltas[h], axis=0, keepdims=True), 0.0)
            p_b.append(p.astype(BF16))
            ds_b.append((ds * SWA_SCALE).astype(BF16))
        dq_halves = []
        for h, kv, par, q_h in heads:
            dq_halves.append(jnp.dot(ds_b[h], kk[kv], preferred_element_type=F32))
            dkk[kv] = dkk[kv] + lax.dot_general(ds_b[h], q_h, TN_DIMS, preferred_element_type=F32)
            pv = lax.dot_general(p_b[h], do_b[h // 2], TN_DIMS, preferred_element_type=F32)
            dvv[kv] = dvv[kv] + jnp.where(lo2 if par == 0 else jnp.logical_not(lo2), pv, 0.0)
        for j in range(SWA_Q_HEADS // 2):
            dq_ref[:, 128 * j:128 * (j + 1)] = jnp.where(lo, dq_halves[2 * j], dq_halves[2 * j + 1])
        fold = lambda t: t + pltpu.roll(t, 64, axis=1)
        dk_span = jnp.where(lo2, fold(dkk[0]), fold(dkk[1]))
        dv_span = jnp.where(lo2, fold(dvv[0]), fold(dvv[1]))
        prev0 = pl.multiple_of(jnp.maximum(n - 1, 0) * BLOCK, BLOCK)
        cur0 = pl.multiple_of(n * BLOCK, BLOCK)
        dk_ref[pl.ds(prev0, BLOCK), :] += dk_span[0:BLOCK]
        dk_ref[pl.ds(cur0, BLOCK), :] += dk_span[BLOCK:]
        dv_ref[pl.ds(prev0, BLOCK), :] += dv_span[0:BLOCK]
        dv_ref[pl.ds(cur0, BLOCK), :] += dv_span[BLOCK:]
        dsink_ref[...] += dsink

    return _pcall(
        body, (qa, ka, ka, va, va, pos_col, pos_row, pos_row, sinks, y_a, d_y), name="swa_bwd", grid=(s // BLOCK,),
        in_specs=in_specs + [blk(1024), blk(1024)],
        out_specs=[blk(1024), whole, whole, pl.BlockSpec((1, LANES), lambda n: (0, 0))],
        out_shape=[jax.ShapeDtypeStruct((s, 1024), F32), jax.ShapeDtypeStruct((s, 128), F32),
                   jax.ShapeDtypeStruct((s, 128), F32), jax.ShapeDtypeStruct((1, LANES), F32)],
        sem=("arbitrary",), comm=comm)


MLA_SCALE = (MLA_NOPE + MLA_ROPE) ** -0.5
LOG2_E = math.log2(math.e)
MLA_TILE = 1024
MLA_ROW_GROUP = 256


def _tile_pairs(nt, q_major):
    pairs = [(i, j) for i in range(nt) for j in range(i + 1)] if q_major else \
            [(i, j) for j in range(nt) for i in range(j, nt)]
    return jnp.asarray([p[0] for p in pairs], jnp.int32), jnp.asarray([p[1] for p in pairs], jnp.int32)


def _diag_mask(t):
    return lax.broadcasted_iota(jnp.int32, (t, t), 1) <= lax.broadcasted_iota(jnp.int32, (t, t), 0)


def _mla_fwd(q_cat, k_cat, v_b, y_all, *, comm=()):
    nh, s, _ = q_cat.shape
    t = _tile(s, MLA_TILE)
    qi, kj = _tile_pairs(s // t, True)
    ycol = (SWA_Q_HEADS * SWA_HEAD_DIM) // (nh * MLA_V)

    def body(qi_ref, kj_ref, q_ref, k_ref, v_ref, _, o_ref, lse_ref, yb_ref, m_sc, l_sc, acc_sc):
        i, j = qi_ref[pl.program_id(0)], kj_ref[pl.program_id(0)]

        @pl.when(j == 0)
        def _():
            m_sc[...] = jnp.full_like(m_sc, NEG_INF)
            l_sc[...] = jnp.zeros_like(l_sc)
            acc_sc[...] = jnp.zeros_like(acc_sc)

        def update(diagonal):
            rg = min(MLA_ROW_GROUP, t)
            units = [(h, slice(r0, r0 + rg), r0, (r0 + rg) if diagonal else t) for h in range(nh) for r0 in range(0, t, rg)]
            scores = [lax.dot_general(q_ref[h, rows, :], k_ref[h, 0:nk, :], NT_DIMS, preferred_element_type=F32)
                      for h, rows, _, nk in units]
            probs, alphas = [], []
            for (h, rows, r0, nk), raw in zip(units, scores):
                if diagonal:
                    row = r0 + lax.broadcasted_iota(jnp.int32, (rg, nk), 0)
                    raw = jnp.where(lax.broadcasted_iota(jnp.int32, (rg, nk), 1) <= row, raw, NEG_INF)
                m_old = m_sc[h, rows]
                m_new = jnp.maximum(m_old, jnp.max(raw, axis=-1, keepdims=True))
                alpha = jnp.exp2((m_old - m_new) * (MLA_SCALE * LOG2_E))
                p = jnp.exp2((raw - m_new) * (MLA_SCALE * LOG2_E))
                l_sc[h, rows] = alpha * l_sc[h, rows] + jnp.sum(p, axis=-1, keepdims=True)
                m_sc[h, rows] = m_new
                probs.append(p.astype(BF16))
                alphas.append(alpha)
            for (h, rows, _, nk), p, alpha in zip(units, probs, alphas):
                acc_sc[h, rows] = alpha * acc_sc[h, rows] + jnp.dot(p, v_ref[0:nk, MLA_V * h:MLA_V * (h + 1)],
                                                                    preferred_element_type=F32)

        @pl.when(j < i)
        def _():
            update(False)

        @pl.when(j == i)
        def _():
            update(True)
            for h in range(nh):
                out = acc_sc[h] * (1.0 / l_sc[h])
                o_ref[:, MLA_V * h:MLA_V * (h + 1)] = out
                yb_ref[:, MLA_V * h:MLA_V * (h + 1)] = out.astype(BF16)
                lse_ref[h] = m_sc[h] * MLA_SCALE + jnp.log(l_sc[h])

    return _pcall(
        body, (q_cat, k_cat, v_b, y_all), name="mla_fwd", grid=(qi.shape[0],), prefetch=(qi, kj),
        in_specs=[pl.BlockSpec((nh, t, 256), lambda p, qi, kj: (0, qi[p], 0)),
                  pl.BlockSpec((nh, t, 256), lambda p, qi, kj: (0, kj[p], 0)),
                  pl.BlockSpec((t, nh * MLA_V), lambda p, qi, kj: (kj[p], 0)), ANY],
        out_specs=[pl.BlockSpec((t, nh * MLA_V), lambda p, qi, kj: (qi[p], 0)),
                   pl.BlockSpec((nh, t, 1), lambda p, qi, kj: (0, qi[p], 0)),
                   pl.BlockSpec((t, nh * MLA_V), lambda p, qi, kj: (qi[p], ycol))],
        out_shape=[jax.ShapeDtypeStruct((s, nh * MLA_V), F32), jax.ShapeDtypeStruct((nh, s, 1), F32),
                   jax.ShapeDtypeStruct(y_all.shape, y_all.dtype)],
        scratch_shapes=[pltpu.VMEM((nh, t, 1), F32), pltpu.VMEM((nh, t, 1), F32), pltpu.VMEM((nh, t, MLA_V), F32)],
        sem=("arbitrary",), comm=comm, io_alias={3: 2})


def _mla_bwd(q_cat, k_cat, v_b, y_b, lse, d_y, *, comm=()):
    nh, s, _ = q_cat.shape
    t = _tile(s, MLA_TILE)
    nt = s // t
    hp = 2
    wv = hp * MLA_V
    col0 = (SWA_Q_HEADS * SWA_HEAD_DIM) // wv
    qi, kj = _tile_pairs(nt, False)

    def body(qi_ref, kj_ref, q_ref, k_ref, v_ref, y_ref, lse_ref, dy_ref, dq_ref, dk_ref, dv_ref, dk_sc, dv_sc):
        step = pl.program_id(1)
        i, j = qi_ref[step], kj_ref[step]

        @pl.when(step == 0)
        def _():
            dq_ref[...] = jnp.zeros_like(dq_ref)

        @pl.when(i == j)
        def _():
            dk_sc[...] = jnp.zeros_like(dk_sc)
            dv_sc[...] = jnp.zeros_like(dv_sc)

        def update(diagonal):
            rg = min(MLA_ROW_GROUP, t) if diagonal else t
            units = [(h, r0, (r0 + rg) if diagonal else t) for h in range(hp) for r0 in range(0, t, rg)]
            cols = [slice(MLA_V * h, MLA_V * (h + 1)) for h in range(hp)]
            do_b = [dy_ref[r0:r0 + rg, cols[h]].astype(BF16) for h, r0, _ in units]
            scores = [lax.dot_general(q_ref[h, r0:r0 + rg, :], k_ref[h, 0:nk, :], NT_DIMS, preferred_element_type=F32)
                      for h, r0, nk in units]
            dps = [lax.dot_general(do_b[u], v_ref[0:nk, cols[h]], NT_DIMS, preferred_element_type=F32)
                   for u, (h, r0, nk) in enumerate(units)]
            p_b, ds_b = [], []
            for u, (h, r0, nk) in enumerate(units):
                p = jnp.exp(scores[u] * MLA_SCALE - lse_ref[h, r0:r0 + rg])
                if diagonal:
                    row = r0 + lax.broadcasted_iota(jnp.int32, (rg, nk), 0)
                    p = jnp.where(lax.broadcasted_iota(jnp.int32, (rg, nk), 1) <= row, p, 0.0)
                delta = jnp.sum(dy_ref[r0:r0 + rg, cols[h]] * y_ref[r0:r0 + rg, cols[h]], axis=-1, keepdims=True)
                p_b.append(p.astype(BF16))
                ds_b.append((p * (dps[u] - delta) * MLA_SCALE).astype(BF16))
            for u, (h, r0, nk) in enumerate(units):
                dv_sc[h, 0:nk] += lax.dot_general(p_b[u], do_b[u], TN_DIMS, preferred_element_type=F32)
                dk_sc[h, 0:nk] += lax.dot_general(ds_b[u], q_ref[h, r0:r0 + rg, :], TN_DIMS, preferred_element_type=F32)
                rows = pl.ds(pl.multiple_of(i * t + r0, rg), rg)
                dq_ref[h, rows, :] += jnp.dot(ds_b[u], k_ref[h, 0:nk, :], preferred_element_type=F32)

        @pl.when(i > j)
        def _():
            update(False)

        @pl.when(i == j)
        def _():
            update(True)

        @pl.when(i == nt - 1)
        def _():
            dk_ref[...] = dk_sc[...]
            for h in range(hp):
                dv_ref[:, MLA_V * h:MLA_V * (h + 1)] = dv_sc[h]

    return _pcall(
        body, (q_cat, k_cat, v_b, y_b, lse, d_y), name="mla_bwd", grid=(nh // hp, qi.shape[0]), prefetch=(qi, kj),
        in_specs=[pl.BlockSpec((hp, t, 256), lambda g, p, qi, kj: (g, qi[p], 0)),
                  pl.BlockSpec((hp, t, 256), lambda g, p, qi, kj: (g, kj[p], 0)),
                  pl.BlockSpec((t, wv), lambda g, p, qi, kj: (kj[p], g)),
                  pl.BlockSpec((t, wv), lambda g, p, qi, kj: (qi[p], g)),
                  pl.BlockSpec((hp, t, 1), lambda g, p, qi, kj: (g, qi[p], 0)),
                  pl.BlockSpec((t, wv), lambda g, p, qi, kj: (qi[p], col0 + g))],
        out_specs=[pl.BlockSpec((hp, s, 256), lambda g, p, qi, kj: (g, 0, 0)),
                   pl.BlockSpec((hp, t, 256), lambda g, p, qi, kj: (g, kj[p], 0)),
                   pl.BlockSpec((t, wv), lambda g, p, qi, kj: (kj[p], g))],
        out_shape=[jax.ShapeDtypeStruct((nh, s, 256), F32), jax.ShapeDtypeStruct((nh, s, 256), F32),
                   jax.ShapeDtypeStruct((s, nh * MLA_V), F32)],
        scratch_shapes=[pltpu.VMEM((hp, t, 256), F32), pltpu.VMEM((hp, t, MLA_V), F32)],
        sem=("arbitrary", "arbitrary"), comm=comm)


MEM_SCALE = MEM_DIM ** -0.5


def _mem_kv_fwd(mem, g_mem, w_memkv, g_mk):
    m_len = mem.shape[0]

    def body(mem_ref, g_ref, w_ref, gk_ref, mn_ref, kv_ref, kn_ref, v_ref):
        mn, _, _ = _norm_full(mem_ref[...], g_ref[...])
        mn_b = mn.astype(BF16)
        mn_ref[...] = mn_b
        kv = jnp.dot(mn_b, w_ref[...], preferred_element_type=F32)
        kv_ref[...] = kv
        for h in range(MEM_HEADS):
            kn, _, _ = _norm_full(kv[:, 128 * h:128 * (h + 1)], gk_ref[...])
            kn_ref[:, 128 * h:128 * (h + 1)] = kn.astype(BF16)
        v_ref[...] = kv[:, 512:1024].astype(BF16)

    return pl.pallas_call(
        body, name="mem_kv_fwd",
        out_shape=[jax.ShapeDtypeStruct((m_len, D_MODEL), BF16), jax.ShapeDtypeStruct((m_len, 1024), F32),
                   jax.ShapeDtypeStruct((m_len, 512), BF16), jax.ShapeDtypeStruct((m_len, 512), BF16)],
        compiler_params=_params(),
    )(mem, g_mem, w_memkv, g_mk)


def _mem_kv_bwd(mem, g_mem, w_memkv, g_mk, mn_b, kv, d_kn, d_v):
    m_len = mem.shape[0]

    def body(mem_ref, g_ref, w_ref, gk_ref, mn_ref, kv_ref, dkn_ref, dv_ref, dw_ref, dgmem_ref, dgk_ref):
        parts = []
        dgk = jnp.zeros((1, LANES), F32)
        for h in range(MEM_HEADS):
            _, xhat, r = _norm_full(kv_ref[:, 128 * h:128 * (h + 1)], gk_ref[...])
            dx, dgh = _norm_full_bwd(dkn_ref[:, 128 * h:128 * (h + 1)], gk_ref[...], xhat, r)
            parts.append(dx)
            dgk = dgk + dgh
        d_kv = jnp.concatenate(parts + [dv_ref[...]], axis=1).astype(BF16)
        dw_ref[...] = lax.dot_general(mn_ref[...], d_kv, TN_DIMS, preferred_element_type=F32)
        d_mn = lax.dot_general(d_kv, w_ref[...], NT_DIMS, preferred_element_type=F32)
        _, xhat, _ = _norm_full(mem_ref[...], g_ref[...])
        dgmem_ref[...] = jnp.sum(d_mn * xhat, axis=0, keepdims=True)
        dgk_ref[...] = dgk

    return pl.pallas_call(
        body, name="mem_kv_bwd",
        out_shape=[jax.ShapeDtypeStruct((D_MODEL, 1024), F32), jax.ShapeDtypeStruct((1, D_MODEL), F32),
                   jax.ShapeDtypeStruct((1, LANES), F32)],
        compiler_params=_params(),
    )(mem, g_mem, w_memkv, g_mk, mn_b, kv, d_kn, d_v)


def _mem_softmax(raw):
    sc = raw * MEM_SCALE
    e = jnp.exp(sc - jnp.max(sc, axis=-1, keepdims=True))
    return e * (1.0 / jnp.sum(e, axis=-1, keepdims=True))


def _mem_attn_fwd(qm, km, vm, y_all, *, tm=512):
    s = qm.shape[0]
    tm = _tile(s, tm)
    m_len = km.shape[0]
    ycol = (SWA_Q_HEADS * SWA_HEAD_DIM + MLA_HEADS * MLA_V) // 512

    def body(q_ref, k_ref, v_ref, _, o_ref, yb_ref):
        cols = [slice(128 * h, 128 * (h + 1)) for h in range(MEM_HEADS)]
        raws = [lax.dot_general(q_ref[:, sl], k_ref[:, sl], NT_DIMS, preferred_element_type=F32) for sl in cols]
        probs = [_mem_softmax(raw).astype(BF16) for raw in raws]
        for p, sl in zip(probs, cols):
            out = jnp.dot(p, v_ref[:, sl], preferred_element_type=F32)
            o_ref[:, sl] = out
            yb_ref[:, sl] = out.astype(BF16)

    kvspec = pl.BlockSpec((m_len, 512), lambda i: (0, 0))
    return _pcall(
        body, (qm, km, vm, y_all), name="mem_attn_fwd", grid=(s // tm,),
        in_specs=[pl.BlockSpec((tm, 512), lambda i: (i, 0)), kvspec, kvspec, ANY],
        out_specs=[pl.BlockSpec((tm, 512), lambda i: (i, 0)), pl.BlockSpec((tm, 512), lambda i: (i, ycol))],
        out_shape=[jax.ShapeDtypeStruct((s, 512), F32), jax.ShapeDtypeStruct(y_all.shape, y_all.dtype)],
        sem=("parallel",), io_alias={3: 1})


def _mem_attn_bwd(qm, km, vm, y_m, d_y, *, tm=1024):
    s = qm.shape[0]
    tm = _tile(s, tm)
    m_len = km.shape[0]
    col0 = (SWA_Q_HEADS * SWA_HEAD_DIM + MLA_HEADS * MLA_V) // 512

    def body(q_ref, k_ref, v_ref, y_ref, dy_ref, dq_ref, dk_ref, dv_ref):
        @pl.when(pl.program_id(0) == 0)
        def _():
            dk_ref[...] = jnp.zeros_like(dk_ref)
            dv_ref[...] = jnp.zeros_like(dv_ref)

        cols = [slice(128 * h, 128 * (h + 1)) for h in range(MEM_HEADS)]
        do_b = [dy_ref[:, sl].astype(BF16) for sl in cols]
        raws = [lax.dot_general(q_ref[:, sl], k_ref[:, sl], NT_DIMS, preferred_element_type=F32) for sl in cols]
        dps = [lax.dot_general(do_b[h], v_ref[:, sl], NT_DIMS, preferred_element_type=F32) for h, sl in enumerate(cols)]
        p_b, ds_b = [], []
        for h, sl in enumerate(cols):
            p = _mem_softmax(raws[h])
            delta = jnp.sum(dy_ref[:, sl] * y_ref[:, sl], axis=-1, keepdims=True)
            p_b.append(p.astype(BF16))
            ds_b.append((p * (dps[h] - delta) * MEM_SCALE).astype(BF16))
        for h, sl in enumerate(cols):
            dv_ref[:, sl] += lax.dot_general(p_b[h], do_b[h], TN_DIMS, preferred_element_type=F32)
            dq_ref[:, sl] = jnp.dot(ds_b[h], k_ref[:, sl], preferred_element_type=F32)
            dk_ref[:, sl] += lax.dot_general(ds_b[h], q_ref[:, sl], TN_DIMS, preferred_element_type=F32)

    kvspec = pl.BlockSpec((m_len, 512), lambda i: (0, 0))
    row = pl.BlockSpec((tm, 512), lambda i: (i, 0))
    return pl.pallas_call(
        body, name="mem_attn_bwd", grid=(s // tm,),
        in_specs=[row, kvspec, kvspec, row, pl.BlockSpec((tm, 512), lambda i: (i, col0))],
        out_specs=[row, kvspec, kvspec],
        out_shape=[jax.ShapeDtypeStruct((s, 512), F32), jax.ShapeDtypeStruct((m_len, 512), F32),
                   jax.ShapeDtypeStruct((m_len, 512), F32)],
        compiler_params=_params(("arbitrary",)),
    )(qm, km, vm, y_m, d_y)


def _ffn_gate_up(fn, w_gate, w_up, *, tm=512, comm=()):
    s, d = fn.shape
    nsp, _, tf = w_gate.shape
    f = nsp * tf
    tm = _tile(s, tm)

    def body(x_ref, wg_ref, wu_ref, g_ref, u_ref, a_ref):
        x = x_ref[...]
        gate = jnp.dot(x, wg_ref[...], preferred_element_type=F32)
        up = jnp.dot(x, wu_ref[...], preferred_element_type=F32)
        g_ref[...] = gate.astype(BF16)
        u_ref[...] = up.astype(BF16)
        a_ref[...] = (gate * (1.0 / (1.0 + jnp.exp(-gate))) * up).astype(BF16)

    wspec = pl.BlockSpec((None, d, tf), lambda j, i: (j, 0, 0))
    ospec = pl.BlockSpec((tm, tf), lambda j, i: (i, j))
    osh = jax.ShapeDtypeStruct((s, f), BF16)
    return _pcall(body, (fn, w_gate, w_up), name="ffn_gate_up", grid=(nsp, s // tm),
                  in_specs=[pl.BlockSpec((tm, d), lambda j, i: (i, 0)), wspec, wspec],
                  out_specs=[ospec, ospec, ospec], out_shape=[osh, osh, osh], sem=("parallel", "parallel"), comm=comm)


def _ffn_bwd_act(d_out, w_down, gate, up, *, tm=1024, tf=1408, comm=()):
    s, d = d_out.shape
    f = w_down.shape[0]
    tm, tf = _tile(s, tm), _tile(f, tf)

    sub = tm // 4 if tm % 1024 == 0 else tm

    def body(do_ref, wd_ref, g_ref, u_ref, dg_ref, du_ref):
        groups = [slice(r, r + sub) for r in range(0, tm, sub)]
        parts = [lax.dot_general(do_ref[rows, :].astype(BF16), wd_ref[...], NT_DIMS, preferred_element_type=F32)
                 for rows in groups]
        for rows, d_act in zip(groups, parts):
            gate = g_ref[rows, :].astype(F32)
            sig = 1.0 / (1.0 + jnp.exp(-gate))
            du_ref[rows, :] = (d_act * (gate * sig)).astype(BF16)
            dg_ref[rows, :] = (d_act * u_ref[rows, :].astype(F32) * (sig * (1.0 + gate * (1.0 - sig)))).astype(BF16)

    ospec = pl.BlockSpec((tm, tf), lambda j, i: (i, j))
    osh = jax.ShapeDtypeStruct((s, f), BF16)
    return _pcall(
        body, (d_out, w_down, gate, up), name="ffn_bwd_act", grid=(f // tf, s // tm),
        in_specs=[pl.BlockSpec((tm, d), lambda j, i: (i, 0)), pl.BlockSpec((tf, d), lambda j, i: (j, 0)), ospec, ospec],
        out_specs=[ospec, ospec], out_shape=[osh, osh], sem=("parallel", "parallel"), comm=comm)


def _cols(g4):
    return jnp.concatenate([g4[k] for k in range(N_CHIPS)], axis=1)


def _full_w_in(g4):
    per = IN_WIDTH // N_CHIPS
    kr0 = 2304 - (N_CHIPS - 1) * per
    last = g4[N_CHIPS - 1]
    pad = jnp.zeros((last.shape[0], IN_PAD - IN_WIDTH), last.dtype)
    return jnp.concatenate([g4[0], g4[1], g4[2], last[:, :kr0], last[:, kr0 + 64:], last[:, kr0:kr0 + 64], pad], axis=1)


def _shards_w_in(dwp):
    per = IN_WIDTH // N_CHIPS
    kr0 = 2304 - (N_CHIPS - 1) * per
    last = jnp.concatenate([dwp[:, (N_CHIPS - 1) * per:2304], dwp[:, C_KR:C_KR + 64], dwp[:, 2304:C_KR]], axis=1)
    assert last.shape[1] == per and kr0 == 144
    return jnp.stack([dwp[:, per * k:per * (k + 1)] for k in range(N_CHIPS - 1)] + [last])


def _full_heads(g4, first):
    return jnp.concatenate([g4[k][:, :first] for k in range(N_CHIPS)] + [g4[k][:, first:] for k in range(N_CHIPS)], axis=1)


def _shards_heads(dwp, first, rest):
    base = N_CHIPS * first
    return jnp.stack([jnp.concatenate([dwp[:, first * k:first * (k + 1)], dwp[:, base + rest * k:base + rest * (k + 1)]], axis=1)
                      for k in range(N_CHIPS)])


def _rope_tables(pos):
    inv_freq = ROPE_THETA ** (-jnp.arange(0, MLA_ROPE, 2, dtype=F32) / MLA_ROPE)
    ang = pos.astype(F32)[:, None] * inv_freq
    cos, sin = jnp.cos(ang), jnp.sin(ang)
    return jnp.tile(cos, (1, 4)), jnp.concatenate([-sin, sin, -sin, sin], axis=1)


def _gain_table(sp):
    two = lambda v: jnp.tile(v, (1, 2))
    rows = [two(sp["swa_q_norm_g"]), two(sp["swa_k_norm_g"]), sp["mla_qn_norm_g"], two(sp["mla_qr_norm_g"]),
            sp["mla_kn_norm_g"], two(sp["mla_kr_norm_g"]), sp["mem_q_norm_g"], jnp.zeros((1, LANES), F32)]
    return jnp.concatenate(rows, axis=0)


CHIP_DISTANCES = (1, 2, 3)


def _place():
    x, y, c = lax.axis_index("x"), lax.axis_index("y"), lax.axis_index("c")
    return x, y, c, 2 * x + y


def _chip_at(x, y, d):
    px = 1 - x if d & 2 else x
    py = 1 - y if d & 1 else y
    return px, py, 2 * px + py


def _row_tile(rows, want=512, mult=8):
    t = min(rows, want)
    t -= t % mult
    while rows % t:
        t -= mult
    return t


def _cast_into_slot(w, meta, *, name, comm=()):
    rows, cols = w.shape
    tr = _row_tile(rows, 512, 16)

    def body(meta_ref, w_ref, o_ref):
        o_ref[...] = w_ref[...].astype(BF16)

    return _pcall(body, (w,), name=name, grid=(rows // tr,), prefetch=(meta,),
                  in_specs=[pl.BlockSpec((tr, cols), lambda i, m: (i, 0))],
                  out_specs=pl.BlockSpec((None, tr, cols), lambda i, m: (m[0], i, 0)),
                  out_shape=jax.ShapeDtypeStruct((N_CHIPS, rows, cols), BF16), sem=("parallel",), comm=comm)


def _remote(src, dst, ssem, rsem, i, device):
    return pltpu.make_async_remote_copy(src_ref=src, dst_ref=dst, send_sem=ssem.at[i], recv_sem=rsem.at[i],
                                        device_id=device, device_id_type=MESH)


def _symmetric_stage(ins, out_shapes, aliases, n_sem, copies):
    def issue(i_refs, o_refs, ssem, rsem):
        for send, _ in copies(i_refs, o_refs, ssem, rsem):
            send.start()

    def wait(i_refs, o_refs, ssem, rsem):
        pairs = copies(i_refs, o_refs, ssem, rsem)
        for _, arrival in pairs:
            arrival.wait_recv()
        for send, _ in pairs:
            send.wait_send()

    return _Stage(ins, out_shapes, aliases, n_sem, issue, wait)


def _gather_stage(slots, leg, part=(0, 1)):
    n = len(slots)
    shapes = [jax.ShapeDtypeStruct(s.shape, s.dtype) for s in slots]
    in_place = {w: w for w in range(n)}
    if not isinstance(leg, str):
        legs = list(leg)

        def copies(i_refs, o_refs, ssem, rsem):
            return [pr for k, (which, prt) in enumerate(legs)
                    for pr in _gather_stage(slots, which, prt).leg_copies(which, 3 * n * k)(i_refs, o_refs, ssem, rsem)]

        return _symmetric_stage(slots, shapes, in_place, 3 * n * len(legs), copies)

    def leg_copies(which, base):
        def copies(_, outs, ssem, rsem):
            x, y, c, k_me = _place()
            pairs = []
            for w in range(n):
                half = outs[w].shape[1] // 2
                r0, size = _window(half, part)
                slab = lambda k, cc, w=w, half=half, r0=r0, size=size: outs[w].at[k, pl.ds(cc * half + r0, size)]
                for d in CHIP_DISTANCES:
                    px, py, k_src = _chip_at(x, y, d)
                    i = base + 3 * w + d - 1
                    if which == "ici":
                        pairs.append((_remote(slab(k_me, c), slab(k_me, c), ssem, rsem, i, (px, py, c)),
                                      _remote(slab(k_src, c), slab(k_src, c), ssem, rsem, i, (x, y, c))))
                    else:
                        pairs.append((_remote(slab(k_src, c), slab(k_src, c), ssem, rsem, i, (x, y, 1 - c)),
                                      _remote(slab(k_src, 1 - c), slab(k_src, 1 - c), ssem, rsem, i, (x, y, c))))
            return pairs
        return copies

    if leg != "both":
        st = _symmetric_stage(slots, shapes, in_place, 3 * n, leg_copies(leg, 0))
        st.leg_copies = leg_copies
        return st
    ici = _symmetric_stage(slots, shapes, in_place, 6 * n, leg_copies("ici", 0))
    d2d = _symmetric_stage(slots, shapes, in_place, 6 * n, leg_copies("d2d", 3 * n))

    def mid(*refs):
        ici.wait(*refs)
        d2d.issue(*refs)

    return _Stage(slots, shapes, in_place, 6 * n, ici.issue, d2d.wait, mid)


def _halves_stage(grads):
    n = len(grads)

    def copies(ins, outs, ssem, rsem):
        x, y, c, _ = _place()
        pairs = []
        for w in range(n):
            half = ins[w].shape[1] // 2
            pairs.append((_remote(ins[w].at[:, pl.ds((1 - c) * half, half)], outs[w], ssem, rsem, w, (x, y, 1 - c)),
                          _remote(outs[w], outs[w], ssem, rsem, w, (x, y, c))))
        return pairs

    shapes = [jax.ShapeDtypeStruct((N_CHIPS, g.shape[1] // 2, g.shape[2]), g.dtype) for g in grads]
    return _symmetric_stage(grads, shapes, {}, n, copies)


def _window(rows, part):
    idx, count = part
    size = rows // count
    assert size * count == rows and size % 16 == 0, (rows, part)
    return idx * size, size


def _chips_stage(parts, part=(0, 1), into=None):
    n = len(parts)

    def copies(ins, outs, ssem, rsem):
        x, y, c, _ = _place()
        pairs = []
        for w in range(n):
            r0, size = _window(ins[w].shape[1], part)
            for d in CHIP_DISTANCES:
                px, py, _ = _chip_at(x, y, d)
                i = 3 * w + d - 1
                land = outs[w].at[d - 1, pl.ds(r0, size)]
                pairs.append((_remote(ins[w].at[d - 1, pl.ds(r0, size)], land, ssem, rsem, i, (px, py, c)),
                              _remote(land, land, ssem, rsem, i, (x, y, c))))
        return pairs

    shapes = [jax.ShapeDtypeStruct(p.shape, p.dtype) for p in parts]
    if into is None:
        return _symmetric_stage(parts, shapes, {}, 3 * n, copies)
    return _symmetric_stage(list(parts) + list(into), shapes, {n + w: w for w in range(n)}, 3 * n, copies)


def _swap_stage(totals):
    n = len(totals)

    def copies(ins, outs, ssem, rsem):
        x, y, c, _ = _place()
        return [(_remote(ins[w], outs[w], ssem, rsem, w, (x, y, 1 - c)),
                 _remote(outs[w], outs[w], ssem, rsem, w, (x, y, c))) for w in range(n)]

    shapes = [jax.ShapeDtypeStruct(t.shape, t.dtype) for t in totals]
    return _symmetric_stage(totals, shapes, {}, n, copies)


def _run_stages(stages, *, name):
    n_ins = [len(st.ins) for st in stages]
    n_outs = [len(st.out_shapes) for st in stages]
    tot_in, tot_out = sum(n_ins), sum(n_outs)
    aliases, i0, o0 = {}, 0, 0
    for st, ni, no in zip(stages, n_ins, n_outs):
        aliases.update({i0 + a: o0 + b for a, b in st.aliases.items()})
        i0, o0 = i0 + ni, o0 + no

    def body(*refs):
        sems = refs[tot_in + tot_out:]
        for what in ("issue", "wait"):
            i0, o0 = 0, tot_in
            for k, (st, ni, no) in enumerate(zip(stages, n_ins, n_outs)):
                getattr(st, what)(refs[i0:i0 + ni], refs[o0:o0 + no], sems[2 * k], sems[2 * k + 1])
                i0, o0 = i0 + ni, o0 + no

    sem = pltpu.SemaphoreType.DMA
    res = pl.pallas_call(
        body, name=name, in_specs=[ANY] * tot_in, out_specs=[ANY] * tot_out,
        out_shape=[s for st in stages for s in st.out_shapes], input_output_aliases=aliases,
        scratch_shapes=[sem((st.n_sem,)) for st in stages for _ in range(2)],
    )(*[a for st in stages for a in st.ins])
    outs, o0 = [], 0
    for no in n_outs:
        outs.append(list(res[o0:o0 + no]))
        o0 += no
    return outs


def _add_pair(meta, g4, recv, *, name):
    nsh, rows, cols = g4.shape
    half = rows // 2
    tr = _row_tile(half, 128 if cols > 1024 else 256, 16)
    nt = half // tr

    def body(meta_ref, g0, g1, g2, g3, r0, r1, r2, r3, own_ref, oth_ref):
        own_ref[...] = g0[...] + r0[...]
        for d, (g, r) in enumerate(((g1, r1), (g2, r2), (g3, r3))):
            oth_ref[d] = (g[...] + r[...]).astype(BF16)

    blk = (None, tr, cols)
    gspec = lambda d: pl.BlockSpec(blk, lambda i, m: (jnp.bitwise_xor(m[0], d), m[1] * nt + i, 0))
    rspec = lambda d: pl.BlockSpec(blk, lambda i, m: (jnp.bitwise_xor(m[0], d), i, 0))
    grid_spec = pltpu.PrefetchScalarGridSpec(
        num_scalar_prefetch=1, grid=(nt,),
        in_specs=[gspec(d) for d in range(nsh)] + [rspec(d) for d in range(nsh)],
        out_specs=[pl.BlockSpec((tr, cols), lambda i, m: (i, 0)), pl.BlockSpec((3, tr, cols), lambda i, m: (0, i, 0))])
    return pl.pallas_call(
        body, name=name, grid_spec=grid_spec,
        out_shape=[jax.ShapeDtypeStruct((half, cols), F32), jax.ShapeDtypeStruct((3, half, cols), BF16)],
        compiler_params=_params(("parallel",)),
    )(meta, g4, g4, g4, g4, recv, recv, recv, recv)


def _add_chips(own, recv, *, name):
    half, cols = own.shape
    tr = _row_tile(half, 256, 16)

    def body(p_ref, r_ref, o_ref):
        o_ref[...] = ((p_ref[...] + r_ref[0].astype(F32)) + r_ref[1].astype(F32)) + r_ref[2].astype(F32)

    return pl.pallas_call(
        body, name=name, grid=(half // tr,),
        in_specs=[pl.BlockSpec((tr, cols), lambda i: (i, 0)), pl.BlockSpec((3, tr, cols), lambda i: (0, i, 0))],
        out_specs=pl.BlockSpec((tr, cols), lambda i: (i, 0)),
        out_shape=jax.ShapeDtypeStruct((half, cols), F32),
        compiler_params=_params(("parallel",)),
    )(own, recv)


def _adamw_math(w, g, m, v):
    m = ADAM_B1 * m + (1.0 - ADAM_B1) * g
    v = ADAM_B2 * v + (1.0 - ADAM_B2) * (g * g)
    m_hat = m / (1.0 - ADAM_B1 ** ADAM_STEP)
    v_hat = v / (1.0 - ADAM_B2 ** ADAM_STEP)
    delta = -ADAM_LR * (m_hat / (jnp.sqrt(v_hat) + ADAM_EPS) + ADAM_WD * w)
    return delta, m, v


def _adamw(meta, w, g_mine, g_theirs, m, v, *, name):
    rows, cols = w.shape
    half = rows // 2
    tr = _row_tile(half, 256)
    nt = half // tr

    def body(meta_ref, w_ref, a_ref, b_ref, m_ref, v_ref, g_ref, d_ref, mo_ref, vo_ref):
        is_mine = (pl.program_id(0) // nt) == meta_ref[1]
        g = jnp.where(is_mine, a_ref[...], b_ref[...])
        g_ref[...] = g
        d_ref[...], mo_ref[...], vo_ref[...] = _adamw_math(w_ref[...], g, m_ref[...], v_ref[...])

    blk = pl.BlockSpec((tr, cols), lambda i, mt: (i, 0))
    mine = pl.BlockSpec((tr, cols), lambda i, mt: (jnp.where(i // nt == mt[1], i % nt, 0), 0))
    theirs = pl.BlockSpec((tr, cols), lambda i, mt: (jnp.where(i // nt == mt[1], 0, i % nt), 0))
    sh = jax.ShapeDtypeStruct((rows, cols), F32)
    grid_spec = pltpu.PrefetchScalarGridSpec(
        num_scalar_prefetch=1, grid=(rows // tr,),
        in_specs=[blk, mine, theirs, blk, blk], out_specs=[blk] * 4)
    return pl.pallas_call(
        body, name=name, grid_spec=grid_spec, out_shape=[sh] * 4,
        compiler_params=_params(("arbitrary",)),
    )(meta, w, g_mine, g_theirs, m, v)


N_DEVICES = 8


def _small_step(g_pack, w_pack, m_pack, v_pack):
    rows = g_pack.shape[0]

    def body(g_ref, w_ref, m_ref, v_ref, sum_ref, d_ref, mo_ref, vo_ref, slots, ssem, rsem):
        x, y, c, _ = _place()
        me = 4 * x + 2 * y + c
        slots[me] = g_ref[...]
        copies = []
        for r in range(1, N_DEVICES):
            px = 1 - x if r & 4 else x
            py = 1 - y if r & 2 else y
            pc = 1 - c if r & 1 else c
            copies.append(pltpu.make_async_remote_copy(
                src_ref=g_ref, dst_ref=slots.at[me], send_sem=ssem.at[r - 1], recv_sem=rsem.at[r - 1],
                device_id=(px, py, pc), device_id_type=MESH))
        for cp in copies:
            cp.start()
        for r in range(1, N_DEVICES):
            src = jnp.bitwise_xor(me, r)
            pltpu.make_async_remote_copy(
                src_ref=g_ref, dst_ref=slots.at[src], send_sem=ssem.at[r - 1], recv_sem=rsem.at[r - 1],
                device_id=(x, y, c), device_id_type=MESH).wait_recv()
        for cp in copies:
            cp.wait_send()
        total = slots[0]
        for k in range(1, N_DEVICES):
            total = total + slots[k]
        sum_ref[...] = total
        d_ref[...], mo_ref[...], vo_ref[...] = _adamw_math(w_ref[...], total, m_ref[...], v_ref[...])

    sh = jax.ShapeDtypeStruct((rows, LANES), F32)
    vm = pl.BlockSpec(memory_space=pltpu.VMEM)
    return pl.pallas_call(
        body, name="small_allreduce_adamw",
        in_specs=[vm] * 4, out_specs=[vm] * 4, out_shape=[sh] * 4,
        scratch_shapes=[pltpu.VMEM((N_DEVICES, rows, LANES), F32),
                        pltpu.SemaphoreType.DMA((N_DEVICES - 1,)), pltpu.SemaphoreType.DMA((N_DEVICES - 1,))],
    )(g_pack, w_pack, m_pack, v_pack)


WEIGHTS = ("attn_norm_g", "w_in", "swa_q_norm_g", "swa_k_norm_g", "swa_sinks", "mla_cq_norm_g", "mla_ckv_norm_g",
           "w_uq", "w_ukv", "mla_qn_norm_g", "mla_qr_norm_g", "mla_kn_norm_g", "mla_kr_norm_g", "mem_norm_g",
           "w_mem_kv", "mem_q_norm_g", "mem_k_norm_g", "w_out", "ffn_norm_g", "w_gate", "w_up", "w_down")
BIG = ("w_in", "w_uq", "w_ukv", "w_mem_kv", "w_out", "w_gate", "w_up", "w_down")
SMALL = tuple(n for n in WEIGHTS if n not in BIG)
PACK_UNIT = 8 * LANES


def _pack(parts):
    flat = jnp.concatenate(parts, axis=1)
    total = flat.shape[1]
    padded = -(-total // PACK_UNIT) * PACK_UNIT
    return jnp.pad(flat, ((0, 0), (0, padded - total))).reshape(padded // LANES, LANES)


def _unpack(buf, sizes):
    flat = buf.reshape(1, buf.shape[0] * LANES)
    out, at = [], 0
    for n in sizes:
        out.append(flat[:, at:at + n])
        at += n
    return out


def kernel(x, mem, positions, attn_norm_g, w_in, swa_q_norm_g, swa_k_norm_g, swa_sinks, mla_cq_norm_g, mla_ckv_norm_g, w_uq, w_ukv, mla_qn_norm_g, mla_qr_norm_g, mla_kn_norm_g, mla_kr_norm_g, mem_norm_g, w_mem_kv, mem_q_norm_g, mem_k_norm_g, w_out, ffn_norm_g, w_gate, w_up, w_down, loss_target, m_attn_norm_g, m_w_in, m_swa_q_norm_g, m_swa_k_norm_g, m_swa_sinks, m_mla_cq_norm_g, m_mla_ckv_norm_g, m_w_uq, m_w_ukv, m_mla_qn_norm_g, m_mla_qr_norm_g, m_mla_kn_norm_g, m_mla_kr_norm_g, m_mem_norm_g, m_w_mem_kv, m_mem_q_norm_g, m_mem_k_norm_g, m_w_out, m_ffn_norm_g, m_w_gate, m_w_up, m_w_down, v_attn_norm_g, v_w_in, v_swa_q_norm_g, v_swa_k_norm_g, v_swa_sinks, v_mla_cq_norm_g, v_mla_ckv_norm_g, v_w_uq, v_w_ukv, v_mla_qn_norm_g, v_mla_qr_norm_g, v_mla_kn_norm_g, v_mla_kr_norm_g, v_mem_norm_g, v_w_mem_kv, v_mem_q_norm_g, v_mem_k_norm_g, v_w_out, v_ffn_norm_g, v_w_gate, v_w_up, v_w_down):
    given = dict(locals())
    wts = {n: given[n] for n in WEIGHTS}
    mom_m = {n: given["m_" + n] for n in WEIGHTS}
    mom_v = {n: given["v_" + n] for n in WEIGHTS}

    mx, my, mc = lax.axis_index("x"), lax.axis_index("y"), lax.axis_index("c")
    meta = jnp.stack([2 * mx + my, mc]).astype(jnp.int32)
    x, mem, pos, target = x[0], mem[0], positions[0], loss_target[0]
    sp = {n: wts[n] for n in SMALL}
    s = x.shape[0]
    cos_t, sin_t = _rope_tables(pos)
    pos_f = pos.astype(F32)
    pos_col, pos_row = pos_f.reshape(s, 1), pos_f.reshape(1, s)
    g128 = _gain_table(sp)
    sinks = sp["swa_sinks"].reshape(SWA_Q_HEADS)
    gcq, gckv = sp["mla_cq_norm_g"], sp["mla_ckv_norm_g"]
    gs = {}

    slot = {n: _cast_into_slot(wts[n][0], meta, name="cast_" + n) for n in BIG if n not in ("w_gate", "w_up", "w_down")}
    first = [slot["w_in"], slot["w_uq"], slot["w_ukv"]]
    slot["w_gate"], [first] = _cast_into_slot(wts["w_gate"][0], meta, name="cast_w_gate",
                                              comm=[_gather_stage(first, "ici", (0, 4))])
    slot["w_up"], [first] = _cast_into_slot(wts["w_up"][0], meta, name="cast_w_up",
                                            comm=[_gather_stage(first, [("ici", (1, 4)), ("d2d", (0, 4))])])
    slot["w_down"], [first] = _cast_into_slot(wts["w_down"][0], meta, name="cast_w_down",
                                              comm=[_gather_stage(first, [("ici", (2, 4)), ("d2d", (1, 4))])])
    hn, [first] = _rms_fwd(x, sp["attn_norm_g"], name="attn_norm_fwd",
                           comm=[_gather_stage(first, [("ici", (3, 4)), ("d2d", (2, 4))])])
    [first] = _run_stages([_gather_stage(first, "d2d", (3, 4))], name="gather_first_last_d2d")
    w_in_f, w_uq_f, w_ukv_f = _full_w_in(first[0]), _full_heads(first[1], MLA_NOPE), _full_heads(first[2], MLA_NOPE)

    proj, [mid] = _matmul(hn, w_in_f, name="in_proj",
                          comm=[_gather_stage([slot["w_mem_kv"], slot["w_out"]], "ici")])
    (qa, ka, va, q_cat, k_cat, v_b, qm), [mid, wg] = _attn_prep_fwd(
        proj, g128, gcq, gckv, w_uq_f, w_ukv_f, cos_t, sin_t,
        comm=[_gather_stage(mid, "d2d"), _gather_stage([slot["w_gate"]], "ici", (0, 4))])
    w_mem_kv_f = mid[0].reshape(D_MODEL, 2 * MEM_HEADS * MEM_DIM)
    w_out_f = mid[1].reshape(D_MODEL, D_MODEL)
    mn_b, kv_m, km, vm = _mem_kv_fwd(mem, sp["mem_norm_g"], w_mem_kv_f, sp["mem_k_norm_g"])
    eighths = lambda leg, ks: [(leg, (k, 8)) for k in ks]
    (y_a, y), [wg] = _swa_fwd(qa, ka, va, pos_col, pos_row, sinks, comm=[_gather_stage(wg, eighths("ici", (2, 3, 4, 5)))])
    (y_b, lse, y), [wg, wu] = _mla_fwd(
        q_cat, k_cat, v_b, y, comm=[_gather_stage(wg, eighths("ici", (6, 7))),
                                    _gather_stage([slot["w_up"]], eighths("ici", (0, 1, 2, 3)))])
    y_m, y = _mem_attn_fwd(qm, km, vm, y)
    h1, [wu, wg] = _matmul(y, w_out_f, add=x, name="out_proj",
                           comm=[_gather_stage(wu, eighths("ici", (4, 5, 6, 7)) + eighths("d2d", (0, 1, 2, 3))),
                                 _gather_stage(wg, "d2d")])
    fn, [wu] = _rms_fwd(h1, sp["ffn_norm_g"], name="ffn_norm_fwd",
                        comm=[_gather_stage(wu, eighths("d2d", (4, 5, 6, 7)))])
    w_gate_f, w_up_f = wg[0], wu[0]
    (gate, up, act), [wd] = _ffn_gate_up(fn, w_gate_f, w_up_f, comm=[_gather_stage([slot["w_down"]], "both")])
    w_down_f = wd[0].reshape(D_FF, D_MODEL)
    d_out, d_out_b, loss_tile = _matmul(act, w_down_f, add=h1, name="down_proj", tm=512, tk=D_FF, loss_target=target)

    add_pair = lambda n, g4, r: _add_pair(meta, g4, r, name="grad_add_pair_" + n)
    add_chips = lambda n, own, r: _add_chips(own, r, name="grad_add_chips_" + n)
    mine, theirs = {}, {}

    dw_down = _matmul(act, d_out_b, ta=True, name="dw_down", tm=512, tn=1024, tk=s)
    dw_down = dw_down.reshape(N_CHIPS, D_FF // N_CHIPS, D_MODEL)
    (d_gate, d_up), [[r]] = _ffn_bwd_act(d_out_b, w_down_f, gate, up, comm=[_halves_stage([dw_down])])
    own_d, oth_d = add_pair("w_down", dw_down, r)
    dw_gate, [rd] = _matmul(fn, d_gate, ta=True, name="dw_gate", tm=512, tk=s, tn=D_FF // N_CHIPS, out_split=N_CHIPS,
                            comm=[_chips_stage([oth_d], (0, 2))])
    dw_up, [[r], rd] = _matmul(fn, d_up, ta=True, name="dw_up", tm=512, tk=s, tn=D_FF // N_CHIPS, out_split=N_CHIPS,
                               comm=[_halves_stage([dw_gate]), _chips_stage([oth_d], (1, 2), into=rd)])
    mine["w_down"] = add_chips("w_down", own_d, rd[0])
    own_g, oth_g = add_pair("w_gate", dw_gate, r)
    d_fn, [[r], rg, [theirs["w_down"]]] = _matmul(
        d_gate, w_gate_f, tb=True, b_split=True, pair2=(d_up, w_up_f), name="dfn", tm=512, tn=512,
        comm=[_halves_stage([dw_up]), _chips_stage([oth_g]), _swap_stage([mine["w_down"]])])
    mine["w_gate"] = add_chips("w_gate", own_g, rg[0])
    own_u, oth_u = add_pair("w_up", dw_up, r)
    d_h1, d_h1_b, gs["ffn_norm_g"] = _rms_bwd(d_fn, h1, sp["ffn_norm_g"], d_out, name="ffn_norm_bwd")
    dw_out, [[theirs["w_gate"]]] = _matmul(y, d_h1_b, ta=True, name="dw_out", tm=512, tk=s,
                                           comm=[_swap_stage([mine["w_gate"]])])
    dw_out = dw_out.reshape(N_CHIPS, D_MODEL // N_CHIPS, D_MODEL)
    d_y, [[r]] = _matmul(d_h1_b, w_out_f, tb=True, name="dy", comm=[_halves_stage([dw_out])])
    own_o, oth_o = add_pair("w_out", dw_out, r)
    (d_qa, d_ka, d_va, d_sink), [ru] = _swa_bwd(qa, ka, va, pos_col, pos_row, sinks, y_a, d_y,
                                                comm=[_chips_stage([oth_u], (0, 2))])
    (d_qcat, d_kcat, d_vb), [ru, [r]] = _mla_bwd(
        q_cat, k_cat, v_b, y_b, lse, d_y, comm=[_chips_stage([oth_u], (1, 2), into=ru), _chips_stage([oth_o])])
    mine["w_up"] = add_chips("w_up", own_u, ru[0])
    mine["w_out"] = add_chips("w_out", own_o, r)
    d_qm, d_km, d_vm = _mem_attn_bwd(qm, km, vm, y_m, d_y)
    (d_proj, dw_uq, dw_ukv, dg128, gs["mla_cq_norm_g"], gs["mla_ckv_norm_g"]), [[theirs["w_up"], theirs["w_out"]]] = \
        _attn_prep_bwd(proj, g128, gcq, gckv, w_uq_f, w_ukv_f, cos_t, sin_t, d_qa, d_ka, d_va, d_qcat, d_kcat, d_vb,
                       d_qm, comm=[_swap_stage([mine["w_up"], mine["w_out"]])])
    dw_mem_kv, gs["mem_norm_g"], gs["mem_k_norm_g"] = _mem_kv_bwd(
        mem, sp["mem_norm_g"], w_mem_kv_f, sp["mem_k_norm_g"], mn_b, kv_m, d_km, d_vm)
    late = ("w_uq", "w_ukv", "w_mem_kv")
    late_g = [_shards_heads(dw_uq, MLA_NOPE, MLA_ROPE), _shards_heads(dw_ukv, MLA_NOPE, MLA_V),
              dw_mem_kv.reshape(N_CHIPS, D_MODEL // N_CHIPS, -1)]
    dw_in, [rs] = _matmul(hn, d_proj, ta=True, name="dw_in", tm=512, tk=s, comm=[_halves_stage(late_g)])
    late_sums = [add_pair(n, g4, r) for n, g4, r in zip(late, late_g, rs)]
    dw_in = _shards_w_in(dw_in)
    d_hn, [rs, [r]] = _matmul(d_proj, w_in_f, tb=True, name="dhn", tk=1536,
                              comm=[_chips_stage([oth for _, oth in late_sums]), _halves_stage([dw_in])])
    for n, (own, _), r_n in zip(late, late_sums, rs):
        mine[n] = add_chips(n, own, r_n)
    own_i, oth_i = add_pair("w_in", dw_in, r)
    (grad_x, _, gs["attn_norm_g"]), [[r], late_theirs] = _rms_bwd(
        d_hn, x, sp["attn_norm_g"], d_h1, name="attn_norm_bwd",
        comm=[_chips_stage([oth_i]), _swap_stage([mine[n] for n in late])])
    theirs.update(zip(late, late_theirs))
    mine["w_in"] = add_chips("w_in", own_i, r)
    [[theirs["w_in"]]] = _run_stages([_swap_stage([mine["w_in"]])], name="grad_swap_w_in")

    fold = lambda r: r[:, :64] + r[:, 64:]
    gs["swa_q_norm_g"] = fold(dg128[G_SWA_Q:G_SWA_Q + 1])
    gs["swa_k_norm_g"] = fold(dg128[G_SWA_K:G_SWA_K + 1])
    gs["mla_qn_norm_g"] = dg128[G_QN:G_QN + 1]
    gs["mla_qr_norm_g"] = fold(dg128[G_QR:G_QR + 1])
    gs["mla_kn_norm_g"] = dg128[G_KN:G_KN + 1]
    gs["mla_kr_norm_g"] = fold(dg128[G_KR:G_KR + 1])
    gs["mem_q_norm_g"] = dg128[G_MQ:G_MQ + 1]
    gs["swa_sinks"] = d_sink[:, :SWA_Q_HEADS]

    grad, delta, new_m, new_v = {}, {}, {}, {}
    for n in BIG:
        g2, d, m2, v2 = _adamw(meta, wts[n][0], mine[n], theirs[n], mom_m[n][0], mom_v[n][0], name="adamw_" + n)
        grad[n], delta[n], new_m[n], new_v[n] = g2[None], d[None], m2[None], v2[None]

    sizes = [wts[n].shape[1] for n in SMALL]
    zero = jnp.zeros((1, LANES), F32)
    packs = _small_step(_pack([gs[n] for n in SMALL] + [loss_tile]), _pack([wts[n] for n in SMALL] + [zero]),
                        _pack([mom_m[n] for n in SMALL] + [zero]), _pack([mom_v[n] for n in SMALL] + [zero]))
    for store, buf in zip((grad, delta, new_m, new_v), packs):
        for n, val in zip(SMALL, _unpack(buf, sizes)):
            store[n] = val
    loss = _unpack(packs[0], sizes + [LANES])[-1][0, 0]

    return (loss, grad_x[None], *[grad[n] for n in WEIGHTS], *[delta[n] for n in WEIGHTS],
            *[new_m[n] for n in WEIGHTS], *[new_v[n] for n in WEIGHTS])
```

```python
import functools
import math

import jax
import jax.numpy as jnp
from jax import lax
from jax.experimental import pallas as pl
from jax.experimental.pallas import tpu as pltpu

F32 = jnp.float32
BF16 = jnp.bfloat16

D_MODEL = 2048
BLOCK = 128
EPS = 1e-6
NEG_INF = -1e30
SWA_Q_HEADS = 16
SWA_KV_HEADS = 2
SWA_HEAD_DIM = 64
MLA_HEADS = 4
MLA_RANK = 512
MLA_NOPE = 128
MLA_ROPE = 64
MLA_V = 128
ROPE_THETA = 10000.0
MEM_HEADS = 4
MEM_DIM = 128
D_FF = 5632
IN_WIDTH = 2880
IN_PAD = 3072
N_CHIPS = 4

ADAM_LR = 0.001
ADAM_B1 = 0.9
ADAM_B2 = 0.999
ADAM_EPS = 1e-08
ADAM_WD = 0.01
ADAM_STEP = 10

VMEM_LIMIT_BYTES = 56 * 1024 * 1024
LANES = 128

MESH = pl.DeviceIdType.MESH


def _params(sem=None, **kw):
    return pltpu.CompilerParams(dimension_semantics=sem, vmem_limit_bytes=VMEM_LIMIT_BYTES, **kw)


def _tile(n, want):
    if n <= want:
        return n
    t = want - want % LANES
    while t > 0:
        if n % t == 0:
            return t
        t -= LANES
    return n


ANY = pl.BlockSpec(memory_space=pl.ANY)


class _Stage:
    def __init__(self, ins, out_shapes, aliases, n_sem, issue, wait, mid=None):
        self.ins, self.out_shapes, self.aliases, self.n_sem = list(ins), list(out_shapes), dict(aliases), n_sem
        self.issue, self.wait, self.mid = issue, wait, mid


def _pcall(body, args, *, name, grid, in_specs, out_specs, out_shape, scratch_shapes=(), sem=None, comm=(),
           prefetch=(), io_alias=None):
    multi = isinstance(out_shape, (list, tuple))
    out_specs_l = list(out_specs) if multi else [out_specs]
    out_shape_l = list(out_shape) if multi else [out_shape]
    npf = len(prefetch)
    own_aliases = {npf + a: o for a, o in (io_alias or {}).items()}

    def call(fn, in_specs_, out_specs_, out_shape_, scratch_, operands, sem_, aliases=None):
        kw = dict(name=name, out_shape=out_shape_, compiler_params=_params(sem_))
        if aliases:
            kw["input_output_aliases"] = aliases
        if npf:
            spec = pltpu.PrefetchScalarGridSpec(num_scalar_prefetch=npf, grid=grid, in_specs=in_specs_,
                                                out_specs=out_specs_, scratch_shapes=scratch_)
            return pl.pallas_call(fn, grid_spec=spec, **kw)(*prefetch, *operands)
        return pl.pallas_call(fn, grid=grid, in_specs=in_specs_, out_specs=out_specs_, scratch_shapes=scratch_,
                              **kw)(*operands)

    if not comm:
        return call(body, list(in_specs), out_specs, out_shape, list(scratch_shapes), args, sem, own_aliases)
    n_in, n_out, n_scr = len(in_specs), len(out_specs_l), len(scratch_shapes)
    cins = [a for st in comm for a in st.ins]
    couts = [s for st in comm for s in st.out_shapes]
    aliases, ci, co = dict(own_aliases), 0, 0
    for st in comm:
        for a_i, o_i in st.aliases.items():
            aliases[npf + n_in + ci + a_i] = n_out + co + o_i
        ci, co = ci + len(st.ins), co + len(st.out_shapes)

    def wrapped(*refs):
        pre = refs[:npf]
        p = npf
        ins = refs[p:p + n_in]; p += n_in
        cin_refs = refs[p:p + len(cins)]; p += len(cins)
        outs = refs[p:p + n_out]; p += n_out
        cout_refs = refs[p:p + len(couts)]; p += len(couts)
        scr = refs[p:p + n_scr]; p += n_scr
        sems = refs[p:]
        first = functools.reduce(jnp.logical_and, [pl.program_id(a) == 0 for a in range(len(grid))])
        last = functools.reduce(jnp.logical_and, [pl.program_id(a) == grid[a] - 1 for a in range(len(grid))])

        def each(what):
            i, o = 0, 0
            for k, st in enumerate(comm):
                fn = getattr(st, what)
                if fn is not None:
                    fn(cin_refs[i:i + len(st.ins)], cout_refs[o:o + len(st.out_shapes)], sems[2 * k], sems[2 * k + 1])
                i, o = i + len(st.ins), o + len(st.out_shapes)

        @pl.when(first)
        def _():
            each("issue")

        if any(st.mid is not None for st in comm):
            n_steps = math.prod(grid)
            assert n_steps >= 4, "a two-leg stage needs a carrier with several grid steps"
            lin = functools.reduce(lambda acc, a: acc * grid[a] + pl.program_id(a), range(len(grid)), 0)

            @pl.when(lin == (3 * n_steps) // 4)
            def _():
                each("mid")

        body(*pre, *ins, *outs, *scr)

        @pl.when(last)
        def _():
            each("wait")

    sem_scr = [pltpu.SemaphoreType.DMA((st.n_sem,)) for st in comm for _ in range(2)]
    res = call(wrapped, list(in_specs) + [ANY] * len(cins), out_specs_l + [ANY] * len(couts), out_shape_l + couts,
               list(scratch_shapes) + sem_scr, (*args, *cins), ("arbitrary",) * len(grid), aliases)
    normal = list(res[:n_out])
    stage_outs, o = [], n_out
    for st in comm:
        stage_outs.append(list(res[o:o + len(st.out_shapes)]))
        o += len(st.out_shapes)
    return (normal if multi else normal[0]), stage_outs


def _matmul(a, b, *, name, ta=False, tb=False, add=None, out_dtype=F32, tm=1024, tn=1024, tk=2048,
            b_split=False, out_split=0, comm=(), loss_target=None, pair2=None):
    if ta:
        kdim, m = a.shape
    else:
        m, kdim = a.shape
    if b_split:
        assert tb
        nsp, n, kb = b.shape
        kb = kb * nsp
    elif tb:
        n, kb = b.shape
    else:
        kb, n = b.shape
    assert kb == kdim, (a.shape, b.shape, ta, tb)
    if b_split:
        tk = kdim
    if out_split:
        tn = _tile(n // out_split, tn)
    tm, tn, tk = _tile(m, tm), _tile(n, tn), _tile(kdim, tk)
    nk = kdim // tk
    dims = (((0 if ta else 1,), (1 if tb else 0,)), ((), ()))

    def product(a_ref, b_ref):
        if not b_split:
            return lax.dot_general(a_ref[...].astype(BF16), b_ref[...].astype(BF16), dims, preferred_element_type=F32)
        per = kdim // nsp
        return sum(lax.dot_general(a_ref[:, per * c:per * (c + 1)].astype(BF16), b_ref[c].astype(BF16), dims,
                                   preferred_element_type=F32) for c in range(nsp))

    def body(*refs):
        a_ref, b_ref = refs[:2]
        n_ab = 4 if pair2 is not None else 2
        add_ref = refs[n_ab] if add is not None else None
        n_in = n_ab + (add is not None) + (loss_target is not None)

        def products():
            r = product(a_ref, b_ref)
            return r if pair2 is None else r + product(refs[2], refs[3])
        o_ref = refs[n_in]

        def finish(r):
            if add_ref is not None:
                r = r + add_ref[...].astype(F32)
            if loss_target is None:
                o_ref[...] = r.astype(o_ref.dtype)
                return
            db_ref, l_ref = refs[n_in + 1], refs[n_in + 2]
            err = r - refs[n_in - 1][...]
            d_out = err * (1.0 / n)
            o_ref[...] = d_out
            db_ref[...] = d_out.astype(BF16)
            part = jnp.broadcast_to((0.5 / n) * jnp.sum(jnp.sum(err * err, axis=-1, keepdims=True), axis=0, keepdims=True),
                                    (1, LANES))
            first = jnp.logical_and(pl.program_id(0) == 0, pl.program_id(1) == 0)

            @pl.when(first)
            def _():
                l_ref[...] = part

            @pl.when(jnp.logical_not(first))
            def _():
                l_ref[...] += part

        if nk == 1:
            finish(products())
            return
        acc_ref = refs[-1]
        k = pl.program_id(2)
        part = products()

        @pl.when(k == 0)
        def _():
            acc_ref[...] = part

        @pl.when(k > 0)
        def _():
            acc_ref[...] += part

        @pl.when(k == nk - 1)
        def _():
            finish(acc_ref[...])

    a_spec = pl.BlockSpec((tk, tm), lambda i, j, k: (k, i)) if ta else pl.BlockSpec((tm, tk), lambda i, j, k: (i, k))
    if b_split:
        b_spec = pl.BlockSpec((nsp, tn, kdim // nsp), lambda i, j, k: (0, j, 0))
    elif tb:
        b_spec = pl.BlockSpec((tn, tk), lambda i, j, k: (j, k))
    else:
        b_spec = pl.BlockSpec((tk, tn), lambda i, j, k: (k, j))
    in_specs = [a_spec, b_spec]
    args = [a, b]
    if pair2 is not None:
        assert pair2[0].shape == a.shape and pair2[1].shape == b.shape
        in_specs += [a_spec, b_spec]
        args += list(pair2)
    if add is not None:
        in_specs.append(pl.BlockSpec((tm, tn), lambda i, j, k: (i, j)))
        args.append(add)
    tile = pl.BlockSpec((tm, tn), lambda i, j, k: (i, j))
    sem = ("parallel", "parallel", "arbitrary")
    if out_split:
        per = (n // out_split) // tn
        out_spec = pl.BlockSpec((None, tm, tn), lambda i, j, k: (j // per, i, j % per))
        out_shape = jax.ShapeDtypeStruct((out_split, m, n // out_split), out_dtype)
    elif loss_target is not None:
        in_specs.append(tile)
        args.append(loss_target)
        out_spec = [tile, tile, pl.BlockSpec((1, LANES), lambda i, j, k: (0, 0))]
        out_shape = [jax.ShapeDtypeStruct((m, n), F32), jax.ShapeDtypeStruct((m, n), BF16),
                     jax.ShapeDtypeStruct((1, LANES), F32)]
        sem = ("arbitrary",) * 3
    else:
        out_spec = tile
        out_shape = jax.ShapeDtypeStruct((m, n), out_dtype)
    return _pcall(body, args, name=name, grid=(m // tm, n // tn, nk), in_specs=in_specs, out_specs=out_spec,
                  out_shape=out_shape, scratch_shapes=[pltpu.VMEM((tm, tn), F32)] if nk > 1 else [],
                  sem=sem, comm=comm)


def _rms_fwd(x, g, *, name, tm=512, comm=()):
    s, d = x.shape
    tm = _tile(s, tm)

    def body(x_ref, g_ref, o_ref):
        xv = x_ref[...]
        r = lax.rsqrt(jnp.mean(xv * xv, axis=-1, keepdims=True) + EPS)
        o_ref[...] = (xv * r * g_ref[...]).astype(o_ref.dtype)

    return _pcall(body, (x, g), name=name, grid=(s // tm,),
                  in_specs=[pl.BlockSpec((tm, d), lambda i: (i, 0)), pl.BlockSpec((1, d), lambda i: (0, 0))],
                  out_specs=pl.BlockSpec((tm, d), lambda i: (i, 0)),
                  out_shape=jax.ShapeDtypeStruct((s, d), BF16), sem=("parallel",), comm=comm)


def _rms_bwd(dy, x, g, res, *, name, tm=512, comm=()):
    s, d = x.shape
    tm = _tile(s, tm)

    def body(dy_ref, x_ref, g_ref, res_ref, dx_ref, dxb_ref, dg_ref):
        xv = x_ref[...]
        dyv = dy_ref[...]
        r = lax.rsqrt(jnp.mean(xv * xv, axis=-1, keepdims=True) + EPS)
        xhat = xv * r
        dyg = dyv * g_ref[...]
        mt = jnp.mean(dyg * xhat, axis=-1, keepdims=True)
        dx = res_ref[...] + r * (dyg - xhat * mt)
        dx_ref[...] = dx
        dxb_ref[...] = dx.astype(BF16)
        part = jnp.sum(dyv * xhat, axis=0, keepdims=True)

        @pl.when(pl.program_id(0) == 0)
        def _():
            dg_ref[...] = part

        @pl.when(pl.program_id(0) > 0)
        def _():
            dg_ref[...] += part

    row = pl.BlockSpec((tm, d), lambda i: (i, 0))
    vec = pl.BlockSpec((1, d), lambda i: (0, 0))
    return _pcall(body, (dy, x, g, res), name=name, grid=(s // tm,), in_specs=[row, row, vec, row],
                  out_specs=[row, row, vec],
                  out_shape=[jax.ShapeDtypeStruct((s, d), F32), jax.ShapeDtypeStruct((s, d), BF16),
                             jax.ShapeDtypeStruct((1, d), F32)],
                  sem=("arbitrary",), comm=comm)


def _lane(shape):
    return lax.broadcasted_iota(jnp.int32, shape, 1)


def _halfsum(t, lo):
    s_lo = jnp.sum(jnp.where(lo, t, 0.0), axis=-1, keepdims=True)
    s_hi = jnp.sum(jnp.where(lo, 0.0, t), axis=-1, keepdims=True)
    return jnp.where(lo, s_lo, s_hi)


def _norm_pair(x, g, lo):
    r = lax.rsqrt(_halfsum(x * x, lo) * (1.0 / 64.0) + EPS)
    xhat = x * r
    return xhat * g, xhat, r


def _norm_pair_bwd(dy, g, xhat, r, lo):
    dyg = dy * g
    mt = _halfsum(dyg * xhat, lo) * (1.0 / 64.0)
    return r * (dyg - xhat * mt), jnp.sum(dy * xhat, axis=0, keepdims=True)


def _norm_full(x, g):
    r = lax.rsqrt(jnp.mean(x * x, axis=-1, keepdims=True) + EPS)
    xhat = x * r
    return xhat * g, xhat, r


def _norm_full_bwd(dy, g, xhat, r):
    dyg = dy * g
    mt = jnp.mean(dyg * xhat, axis=-1, keepdims=True)
    return r * (dyg - xhat * mt), jnp.sum(dy * xhat, axis=0, keepdims=True)


def _rot(x, first32):
    return jnp.where(first32, pltpu.roll(x, 96, axis=1), pltpu.roll(x, 32, axis=1))


def _rope(x, cos_t, sin_t, first32):
    return x * cos_t + _rot(x, first32) * sin_t


def _rope_bwd(dy, cos_t, sin_t, first32):
    return dy * cos_t + _rot(dy * sin_t, first32)


G_SWA_Q, G_SWA_K, G_QN, G_QR, G_KN, G_KR, G_MQ = range(7)

C_QA, C_KA, C_VA, C_CQ, C_CKV, C_QM, C_KR = 0, 1024, 1152, 1280, 1792, 2304, 2816


def _prep_common(p_ref, g128_ref, gcq_ref, gckv_ref, wuq_ref, wukv_ref, cos_ref, sin_ref):
    tm = p_ref.shape[0]
    lane = _lane((tm, LANES))
    lo = lane < 64
    first32 = (lane % 64) < 32
    cos_t = cos_ref[...]
    sin_t = sin_ref[...]
    g = lambda row: g128_ref[row:row + 1, :]
    out = dict(lo=lo, first32=first32, cos_t=cos_t, sin_t=sin_t, lane=lane)
    cq_n, cq_hat, cq_r = _norm_full(p_ref[:, C_CQ:C_CQ + MLA_RANK], gcq_ref[...])
    ckv_n, ckv_hat, ckv_r = _norm_full(p_ref[:, C_CKV:C_CKV + MLA_RANK], gckv_ref[...])
    cq_b = cq_n.astype(BF16)
    ckv_b = ckv_n.astype(BF16)
    q_b = jnp.dot(cq_b, wuq_ref[...], preferred_element_type=F32)
    kv_b = jnp.dot(ckv_b, wukv_ref[...], preferred_element_type=F32)
    out.update(cq_b=cq_b, cq_hat=cq_hat, cq_r=cq_r, ckv_b=ckv_b, ckv_hat=ckv_hat, ckv_r=ckv_r, q_b=q_b, kv_b=kv_b, g=g)
    return out


def _attn_prep_fwd(proj, g128, gcq, gckv, wuq, wukv, cos_t, sin_t, *, tm=512, comm=()):
    s = proj.shape[0]
    tm = _tile(s, tm)

    def body(p_ref, g128_ref, gcq_ref, gckv_ref, wuq_ref, wukv_ref, cos_ref, sin_ref,
             qa_ref, ka_ref, va_ref, qcat_ref, kcat_ref, vb_ref, qm_ref):
        c = _prep_common(p_ref, g128_ref, gcq_ref, gckv_ref, wuq_ref, wukv_ref, cos_ref, sin_ref)
        lo, first32, g = c["lo"], c["first32"], c["g"]
        for j in range(SWA_Q_HEADS // 2):
            y, _, _ = _norm_pair(p_ref[:, C_QA + 128 * j:C_QA + 128 * (j + 1)], g(G_SWA_Q), lo)
            qa_ref[:, 128 * j:128 * (j + 1)] = y.astype(BF16)
        y, _, _ = _norm_pair(p_ref[:, C_KA:C_KA + 128], g(G_SWA_K), lo)
        ka_ref[...] = y.astype(BF16)
        va_ref[...] = p_ref[:, C_VA:C_VA + 128].astype(BF16)
        kr, _, _ = _norm_pair(p_ref[:, C_KR:C_KR + 128], g(G_KR), lo)
        kr = jnp.where(lo, _rope(kr, c["cos_t"], c["sin_t"], first32), 0.0)
        krkr = (kr + pltpu.roll(kr, 64, axis=1)).astype(BF16)
        q_b, kv_b = c["q_b"], c["kv_b"]
        qr = []
        for j in range(MLA_HEADS // 2):
            y, _, _ = _norm_pair(q_b[:, 512 + 128 * j:512 + 128 * (j + 1)], g(G_QR), lo)
            qr.append(_rope(y, c["cos_t"], c["sin_t"], first32))
        for h in range(MLA_HEADS):
            qn, _, _ = _norm_full(q_b[:, 128 * h:128 * (h + 1)], g(G_QN))
            keep = lo if h % 2 == 0 else jnp.logical_not(lo)
            qcat_ref[h, :, 0:128] = qn.astype(BF16)
            qcat_ref[h, :, 128:256] = jnp.where(keep, qr[h // 2], 0.0).astype(BF16)
            kn, _, _ = _norm_full(kv_b[:, 128 * h:128 * (h + 1)], g(G_KN))
            kcat_ref[h, :, 0:128] = kn.astype(BF16)
            kcat_ref[h, :, 128:256] = krkr
        vb_ref[...] = kv_b[:, 512:1024].astype(BF16)
        for h in range(MEM_HEADS):
            y, _, _ = _norm_full(p_ref[:, C_QM + 128 * h:C_QM + 128 * (h + 1)], g(G_MQ))
            qm_ref[:, 128 * h:128 * (h + 1)] = y.astype(BF16)

    row = lambda w: pl.BlockSpec((tm, w), lambda i: (i, 0))
    full = lambda shape: pl.BlockSpec(shape, lambda i: tuple(0 for _ in shape))
    cat = pl.BlockSpec((MLA_HEADS, tm, 256), lambda i: (0, i, 0))
    return _pcall(
        body, (proj, g128, gcq, gckv, wuq, wukv, cos_t, sin_t), name="attn_prep_fwd", grid=(s // tm,),
        in_specs=[row(IN_PAD), full((8, 128)), full((1, 512)), full((1, 512)), full((512, 768)), full((512, 1024)),
                  row(128), row(128)],
        out_specs=[row(1024), row(128), row(128), cat, cat, row(512), row(512)],
        out_shape=[jax.ShapeDtypeStruct((s, 1024), BF16), jax.ShapeDtypeStruct((s, 128), BF16),
                   jax.ShapeDtypeStruct((s, 128), BF16), jax.ShapeDtypeStruct((MLA_HEADS, s, 256), BF16),
                   jax.ShapeDtypeStruct((MLA_HEADS, s, 256), BF16), jax.ShapeDtypeStruct((s, 512), BF16),
                   jax.ShapeDtypeStruct((s, 512), BF16)],
        sem=("parallel",), comm=comm)


def _attn_prep_bwd(proj, g128, gcq, gckv, wuq, wukv, cos_t, sin_t,
                   d_qa, d_ka, d_va, d_qcat, d_kcat, d_vb, d_qm, *, tm=512, comm=()):
    s = proj.shape[0]
    tm = _tile(s, tm)

    def body(p_ref, g128_ref, gcq_ref, gckv_ref, wuq_ref, wukv_ref, cos_ref, sin_ref,
             dqa_ref, dka_ref, dva_ref, dqcat_ref, dkcat_ref, dvb_ref, dqm_ref,
             dp_ref, dwuq_ref, dwukv_ref, dg128_ref, dgcq_ref, dgckv_ref):
        c = _prep_common(p_ref, g128_ref, gcq_ref, gckv_ref, wuq_ref, wukv_ref, cos_ref, sin_ref)
        lo, first32, g = c["lo"], c["first32"], c["g"]
        cos_v, sin_v = c["cos_t"], c["sin_t"]
        q_b, kv_b = c["q_b"], c["kv_b"]
        zero_row = jnp.zeros((1, LANES), F32)
        dg = {k: zero_row for k in range(7)}

        for j in range(SWA_Q_HEADS // 2):
            sl = slice(C_QA + 128 * j, C_QA + 128 * (j + 1))
            _, xhat, r = _norm_pair(p_ref[:, sl], g(G_SWA_Q), lo)
            dx, dgj = _norm_pair_bwd(dqa_ref[:, 128 * j:128 * (j + 1)], g(G_SWA_Q), xhat, r, lo)
            dp_ref[:, sl] = dx.astype(BF16)
            dg[G_SWA_Q] = dg[G_SWA_Q] + dgj
        _, xhat, r = _norm_pair(p_ref[:, C_KA:C_KA + 128], g(G_SWA_K), lo)
        dx, dgj = _norm_pair_bwd(dka_ref[...], g(G_SWA_K), xhat, r, lo)
        dp_ref[:, C_KA:C_KA + 128] = dx.astype(BF16)
        dg[G_SWA_K] = dgj
        dp_ref[:, C_VA:C_VA + 128] = dva_ref[...].astype(BF16)

        dqb_parts = [None] * 6
        for h in range(MLA_HEADS):
            _, xhat, r = _norm_full(q_b[:, 128 * h:128 * (h + 1)], g(G_QN))
            dx, dgj = _norm_full_bwd(dqcat_ref[h, :, 0:128], g(G_QN), xhat, r)
            dqb_parts[h] = dx
            dg[G_QN] = dg[G_QN] + dgj
        for j in range(MLA_HEADS // 2):
            _, xhat, r = _norm_pair(q_b[:, 512 + 128 * j:512 + 128 * (j + 1)], g(G_QR), lo)
            d_rot = jnp.where(lo, dqcat_ref[2 * j, :, 128:256], dqcat_ref[2 * j + 1, :, 128:256])
            d_y = _rope_bwd(d_rot, cos_v, sin_v, first32)
            dx, dgj = _norm_pair_bwd(d_y, g(G_QR), xhat, r, lo)
            dqb_parts[4 + j] = dx
            dg[G_QR] = dg[G_QR] + dgj
        d_qb = jnp.concatenate(dqb_parts, axis=1).astype(BF16)
        dwuq = lax.dot_general(c["cq_b"], d_qb, (((0,), (0,)), ((), ())), preferred_element_type=F32)
        d_cqn = lax.dot_general(d_qb, wuq_ref[...], (((1,), (1,)), ((), ())), preferred_element_type=F32)
        dx, dgcq = _norm_full_bwd(d_cqn, gcq_ref[...], c["cq_hat"], c["cq_r"])
        dp_ref[:, C_CQ:C_CQ + MLA_RANK] = dx.astype(BF16)

        dkv_parts = []
        d_krkr = jnp.zeros((p_ref.shape[0], LANES), F32)
        for h in range(MLA_HEADS):
            _, xhat, r = _norm_full(kv_b[:, 128 * h:128 * (h + 1)], g(G_KN))
            dx, dgj = _norm_full_bwd(dkcat_ref[h, :, 0:128], g(G_KN), xhat, r)
            dkv_parts.append(dx)
            dg[G_KN] = dg[G_KN] + dgj
            d_krkr = d_krkr + dkcat_ref[h, :, 128:256]
        d_kvb = jnp.concatenate(dkv_parts + [dvb_ref[...]], axis=1).astype(BF16)
        dwukv = lax.dot_general(c["ckv_b"], d_kvb, (((0,), (0,)), ((), ())), preferred_element_type=F32)
        d_ckvn = lax.dot_general(d_kvb, wukv_ref[...], (((1,), (1,)), ((), ())), preferred_element_type=F32)
        dx, dgckv = _norm_full_bwd(d_ckvn, gckv_ref[...], c["ckv_hat"], c["ckv_r"])
        dp_ref[:, C_CKV:C_CKV + MLA_RANK] = dx.astype(BF16)

        _, xhat, r = _norm_pair(p_ref[:, C_KR:C_KR + 128], g(G_KR), lo)
        d_kr = jnp.where(lo, d_krkr + pltpu.roll(d_krkr, 64, axis=1), 0.0)
        d_y = jnp.where(lo, _rope_bwd(d_kr, cos_v, sin_v, first32), 0.0)
        dx, dgj = _norm_pair_bwd(d_y, g(G_KR), xhat, r, lo)
        dp_ref[:, C_KR:C_KR + 128] = jnp.where(lo, dx, 0.0).astype(BF16)
        dp_ref[:, C_KR + 128:] = jnp.zeros((p_ref.shape[0], IN_PAD - C_KR - 128), BF16)
        dg[G_KR] = dgj

        for h in range(MEM_HEADS):
            sl = slice(C_QM + 128 * h, C_QM + 128 * (h + 1))
            _, xhat, r = _norm_full(p_ref[:, sl], g(G_MQ))
            dx, dgj = _norm_full_bwd(dqm_ref[:, 128 * h:128 * (h + 1)], g(G_MQ), xhat, r)
            dp_ref[:, sl] = dx.astype(BF16)
            dg[G_MQ] = dg[G_MQ] + dgj

        dg_tile = jnp.concatenate([dg[k] for k in range(7)] + [zero_row], axis=0)

        @pl.when(pl.program_id(0) == 0)
        def _():
            dwuq_ref[...] = dwuq
            dwukv_ref[...] = dwukv
            dg128_ref[...] = dg_tile
            dgcq_ref[...] = dgcq
            dgckv_ref[...] = dgckv

        @pl.when(pl.program_id(0) > 0)
        def _():
            dwuq_ref[...] += dwuq
            dwukv_ref[...] += dwukv
            dg128_ref[...] += dg_tile
            dgcq_ref[...] += dgcq
            dgckv_ref[...] += dgckv

    row = lambda w: pl.BlockSpec((tm, w), lambda i: (i, 0))
    full = lambda shape: pl.BlockSpec(shape, lambda i: tuple(0 for _ in shape))
    cat = pl.BlockSpec((MLA_HEADS, tm, 256), lambda i: (0, i, 0))
    return _pcall(
        body, (proj, g128, gcq, gckv, wuq, wukv, cos_t, sin_t, d_qa, d_ka, d_va, d_qcat, d_kcat, d_vb, d_qm),
        name="attn_prep_bwd", grid=(s // tm,),
        in_specs=[row(IN_PAD), full((8, 128)), full((1, 512)), full((1, 512)), full((512, 768)), full((512, 1024)),
                  row(128), row(128),
                  row(1024), row(128), row(128), cat, cat, row(512), row(512)],
        out_specs=[row(IN_PAD), full((512, 768)), full((512, 1024)), full((8, 128)), full((1, 512)), full((1, 512))],
        out_shape=[jax.ShapeDtypeStruct((s, IN_PAD), BF16), jax.ShapeDtypeStruct((512, 768), F32),
                   jax.ShapeDtypeStruct((512, 1024), F32), jax.ShapeDtypeStruct((8, 128), F32),
                   jax.ShapeDtypeStruct((1, 512), F32), jax.ShapeDtypeStruct((1, 512), F32)],
        sem=("arbitrary",), comm=comm)


SWA_SLOPES = tuple(2.0 ** (-8.0 * h / SWA_Q_HEADS) for h in range(1, SWA_Q_HEADS + 1))
SWA_SCALE = SWA_HEAD_DIM ** -0.5
NT_DIMS = (((1,), (1,)), ((), ()))
TN_DIMS = (((0,), (0,)), ((), ()))


def _swa_span(n, kp_ref, kc_ref, vp_ref, vc_ref, pcol_ref, pprow_ref, pcrow_ref):
    k_span = jnp.concatenate([kp_ref[...], kc_ref[...]], axis=0).astype(F32)
    v_span = jnp.concatenate([vp_ref[...], vc_ref[...]], axis=0).astype(F32)
    lo = _lane((2 * BLOCK, LANES)) < 64
    k_sw = pltpu.roll(k_span, 64, axis=1)
    v_sw = pltpu.roll(v_span, 64, axis=1)
    kk = (jnp.where(lo, k_span, k_sw).astype(BF16), jnp.where(lo, k_sw, k_span).astype(BF16))
    vv_lo = (jnp.where(lo, v_span, 0.0).astype(BF16), jnp.where(lo, v_sw, 0.0).astype(BF16))
    vv_hi = (jnp.where(lo, 0.0, v_sw).astype(BF16), jnp.where(lo, 0.0, v_span).astype(BF16))
    pk = jnp.concatenate([pprow_ref[...], pcrow_ref[...]], axis=1)
    dist = jnp.abs(pcol_ref[...] - pk)
    qi = lax.broadcasted_iota(jnp.int32, (BLOCK, 2 * BLOCK), 0)
    ki = lax.broadcasted_iota(jnp.int32, (BLOCK, 2 * BLOCK), 1)
    first_key = jnp.where(n > 0, qi + 1, jnp.maximum(qi + 1, BLOCK))
    valid = jnp.logical_and(ki >= first_key, ki <= qi + BLOCK)
    mask_add = jnp.where(valid, 0.0, NEG_INF)
    return kk, vv_lo, vv_hi, dist, mask_add


def _swa_heads(q_ref, lo):
    heads = []
    for j in range(SWA_Q_HEADS // 2):
        q_pair = q_ref[:, 128 * j:128 * (j + 1)].astype(F32)
        for par in (0, 1):
            q_h = jnp.where(lo if par == 0 else jnp.logical_not(lo), q_pair, 0.0).astype(BF16)
            heads.append((2 * j + par, (2 * j) // (SWA_Q_HEADS // SWA_KV_HEADS), par, q_h))
    return heads


def _swa_probs(raw, dist, mask_add, slope, sink):
    s = raw * SWA_SCALE - slope * dist + mask_add
    m = jnp.maximum(jnp.max(s, axis=-1, keepdims=True), sink)
    e = jnp.exp(s - m)
    e_sink = jnp.exp(sink - m)
    inv = 1.0 / (jnp.sum(e, axis=-1, keepdims=True) + e_sink)
    return e * inv, e_sink * inv


def _swa_specs():
    blk = lambda w: pl.BlockSpec((BLOCK, w), lambda n: (n, 0))
    prev = lambda w: pl.BlockSpec((BLOCK, w), lambda n: (jnp.maximum(n - 1, 0), 0))
    prow_c = pl.BlockSpec((1, BLOCK), lambda n: (0, n))
    prow_p = pl.BlockSpec((1, BLOCK), lambda n: (0, jnp.maximum(n - 1, 0)))
    smem = pl.BlockSpec(memory_space=pltpu.SMEM)
    return [blk(1024), prev(128), blk(128), prev(128), blk(128), blk(1), prow_p, prow_c, smem], blk


def _swa_fwd(qa, ka, va, pos_col, pos_row, sinks, *, comm=()):
    s = qa.shape[0]
    in_specs, blk = _swa_specs()

    def body(q_ref, kp_ref, kc_ref, vp_ref, vc_ref, pcol_ref, pprow_ref, pcrow_ref, sink_ref, o_ref, yb_ref):
        n = pl.program_id(0)
        kk, vv_lo, vv_hi, dist, mask_add = _swa_span(n, kp_ref, kc_ref, vp_ref, vc_ref, pcol_ref, pprow_ref, pcrow_ref)
        lo = _lane((BLOCK, LANES)) < 64
        heads = _swa_heads(q_ref, lo)
        raws = [lax.dot_general(q_h, kk[kv], NT_DIMS, preferred_element_type=F32) for _, kv, _, q_h in heads]
        probs = [_swa_probs(raw, dist, mask_add, SWA_SLOPES[h], sink_ref[h])[0].astype(BF16)
                 for raw, (h, _, _, _) in zip(raws, heads)]
        for j in range(SWA_Q_HEADS // 2):
            kv = heads[2 * j][1]
            out = (jnp.dot(probs[2 * j], vv_lo[kv], preferred_element_type=F32)
                   + jnp.dot(probs[2 * j + 1], vv_hi[kv], preferred_element_type=F32))
            o_ref[:, 128 * j:128 * (j + 1)] = out
            yb_ref[:, 128 * j:128 * (j + 1)] = out.astype(BF16)

    return _pcall(body, (qa, ka, ka, va, va, pos_col, pos_row, pos_row, sinks), name="swa_fwd", grid=(s // BLOCK,),
                  in_specs=in_specs, out_specs=[blk(1024), blk(1024)],
                  out_shape=[jax.ShapeDtypeStruct((s, 1024), F32), jax.ShapeDtypeStruct((s, D_MODEL), BF16)],
                  sem=("parallel",), comm=comm)


def _swa_bwd(qa, ka, va, pos_col, pos_row, sinks, y_a, d_y, *, comm=()):
    s = qa.shape[0]
    in_specs, blk = _swa_specs()
    whole = pl.BlockSpec((s, 128), lambda n: (0, 0))

    def body(q_ref, kp_ref, kc_ref, vp_ref, vc_ref, pcol_ref, pprow_ref, pcrow_ref, sink_ref, y_ref, dy_ref,
             dq_ref, dk_ref, dv_ref, dsink_ref):
        n = pl.program_id(0)

        @pl.when(n == 0)
        def _():
            dk_ref[...] = jnp.zeros_like(dk_ref)
            dv_ref[...] = jnp.zeros_like(dv_ref)
            dsink_ref[...] = jnp.zeros_like(dsink_ref)

        kk, vv_lo, vv_hi, dist, mask_add = _swa_span(n, kp_ref, kc_ref, vp_ref, vc_ref, pcol_ref, pprow_ref, pcrow_ref)
        lo = _lane((BLOCK, LANES)) < 64
        lo2 = _lane((2 * BLOCK, LANES)) < 64
        lane1 = _lane((1, LANES))
        dsink = jnp.zeros((1, LANES), F32)
        dkk = [jnp.zeros((2 * BLOCK, LANES), F32) for _ in range(SWA_KV_HEADS)]
        dvv = [jnp.zeros((2 * BLOCK, LANES), F32) for _ in range(SWA_KV_HEADS)]
        heads = _swa_heads(q_ref, lo)
        do_b, deltas = [], []
        for j in range(SWA_Q_HEADS // 2):
            do_pair = dy_ref[:, 128 * j:128 * (j + 1)]
            doy = do_pair * y_ref[:, 128 * j:128 * (j + 1)]
            do_b.append(do_pair.astype(BF16))
            deltas.append(jnp.sum(jnp.where(lo, doy, 0.0), axis=-1, keepdims=True))
            deltas.append(jnp.sum(jnp.where(lo, 0.0, doy), axis=-1, keepdims=True))
        raws = [lax.dot_general(q_h, kk[kv], NT_DIMS, preferred_element_type=F32) for _, kv, _, q_h in heads]
        dps = [lax.dot_general(do_b[h // 2], (vv_lo, vv_hi)[par][kv], NT_DIMS, preferred_element_type=F32)
               for h, kv, par, _ in heads]
        p_b, ds_b = [], []
        for h, kv, par, _ in heads:
            p, p_sink = _swa_probs(raws[h], dist, mask_add, SWA_SLOPES[h], sink_ref[h])
            ds = p * (dps[h] - deltas[h])
            dsink = dsink + jnp.where(lane1 == h, -jnp.sum(p_sink * deltas[h], axis=0, keepdims=True), 0.0)
            p_b.append(p.astype(BF16))
            ds_b.append((ds * SWA_SCALE).astype(BF16))
        dq_halves = []
        for h, kv, par, q_h in heads:
            dq_halves.append(jnp.dot(ds_b[h], kk[kv], preferred_element_type=F32))
            dkk[kv] = dkk[kv] + lax.dot_general(ds_b[h], q_h, TN_DIMS, preferred_element_type=F32)
            pv = lax.dot_general(p_b[h], do_b[h // 2], TN_DIMS, preferred_element_type=F32)
            dvv[kv] = dvv[kv] + jnp.where(lo2 if par == 0 else jnp.logical_not(lo2), pv, 0.0)
        for j in range(SWA_Q_HEADS // 2):
            dq_ref[:, 128 * j:128 * (j + 1)] = jnp.where(lo, dq_halves[2 * j], dq_halves[2 * j + 1])
        fold = lambda t: t + pltpu.roll(t, 64, axis=1)
        dk_span = jnp.where(lo2, fold(dkk[0]), fold(dkk[1]))
        dv_span = jnp.where(lo2, fold(dvv[0]), fold(dvv[1]))
        prev0 = pl.multiple_of(jnp.maximum(n - 1, 0) * BLOCK, BLOCK)
        cur0 = pl.multiple_of(n * BLOCK, BLOCK)
        dk_ref[pl.ds(prev0, BLOCK), :] += dk_span[0:BLOCK]
        dk_ref[pl.ds(cur0, BLOCK), :] += dk_span[BLOCK:]
        dv_ref[pl.ds(prev0, BLOCK), :] += dv_span[0:BLOCK]
        dv_ref[pl.ds(cur0, BLOCK), :] += dv_span[BLOCK:]
        dsink_ref[...] += dsink

    return _pcall(
        body, (qa, ka, ka, va, va, pos_col, pos_row, pos_row, sinks, y_a, d_y), name="swa_bwd", grid=(s // BLOCK,),
        in_specs=in_specs + [blk(1024), blk(1024)],
        out_specs=[blk(1024), whole, whole, pl.BlockSpec((1, LANES), lambda n: (0, 0))],
        out_shape=[jax.ShapeDtypeStruct((s, 1024), F32), jax.ShapeDtypeStruct((s, 128), F32),
                   jax.ShapeDtypeStruct((s, 128), F32), jax.ShapeDtypeStruct((1, LANES), F32)],
        sem=("arbitrary",), comm=comm)


MLA_SCALE = (MLA_NOPE + MLA_ROPE) ** -0.5
LOG2_E = math.log2(math.e)
MLA_TILE = 1024
MLA_ROW_GROUP = 256


def _tile_pairs(nt, q_major):
    pairs = [(i, j) for i in range(nt) for j in range(i + 1)] if q_major else \
            [(i, j) for j in range(nt) for i in range(j, nt)]
    return jnp.asarray([p[0] for p in pairs], jnp.int32), jnp.asarray([p[1] for p in pairs], jnp.int32)


def _diag_mask(t):
    return lax.broadcasted_iota(jnp.int32, (t, t), 1) <= lax.broadcasted_iota(jnp.int32, (t, t), 0)


def _mla_fwd(q_cat, k_cat, v_b, y_all, *, comm=()):
    nh, s, _ = q_cat.shape
    t = _tile(s, MLA_TILE)
    qi, kj = _tile_pairs(s // t, True)
    ycol = (SWA_Q_HEADS * SWA_HEAD_DIM) // (nh * MLA_V)

    def body(qi_ref, kj_ref, q_ref, k_ref, v_ref, _, o_ref, lse_ref, yb_ref, m_sc, l_sc, acc_sc):
        i, j = qi_ref[pl.program_id(0)], kj_ref[pl.program_id(0)]

        @pl.when(j == 0)
        def _():
            m_sc[...] = jnp.full_like(m_sc, NEG_INF)
            l_sc[...] = jnp.zeros_like(l_sc)
            acc_sc[...] = jnp.zeros_like(acc_sc)

        def update(diagonal):
            rg = min(MLA_ROW_GROUP, t)
            units = [(h, slice(r0, r0 + rg), r0, (r0 + rg) if diagonal else t) for h in range(nh) for r0 in range(0, t, rg)]
            scores = [lax.dot_general(q_ref[h, rows, :], k_ref[h, 0:nk, :], NT_DIMS, preferred_element_type=F32)
                      for h, rows, _, nk in units]
            probs, alphas = [], []
            for (h, rows, r0, nk), raw in zip(units, scores):
                if diagonal:
                    row = r0 + lax.broadcasted_iota(jnp.int32, (rg, nk), 0)
                    raw = jnp.where(lax.broadcasted_iota(jnp.int32, (rg, nk), 1) <= row, raw, NEG_INF)
                m_old = m_sc[h, rows]
                m_new = jnp.maximum(m_old, jnp.max(raw, axis=-1, keepdims=True))
                alpha = jnp.exp2((m_old - m_new) * (MLA_SCALE * LOG2_E))
                p = jnp.exp2((raw - m_new) * (MLA_SCALE * LOG2_E))
                l_sc[h, rows] = alpha * l_sc[h, rows] + jnp.sum(p, axis=-1, keepdims=True)
                m_sc[h, rows] = m_new
                probs.append(p.astype(BF16))
                alphas.append(alpha)
            for (h, rows, _, nk), p, alpha in zip(units, probs, alphas):
                acc_sc[h, rows] = alpha * acc_sc[h, rows] + jnp.dot(p, v_ref[0:nk, MLA_V * h:MLA_V * (h + 1)],
                                                                    preferred_element_type=F32)

        @pl.when(j < i)
        def _():
            update(False)

        @pl.when(j == i)
        def _():
            update(True)
            for h in range(nh):
                out = acc_sc[h] * (1.0 / l_sc[h])
                o_ref[:, MLA_V * h:MLA_V * (h + 1)] = out
                yb_ref[:, MLA_V * h:MLA_V * (h + 1)] = out.astype(BF16)
                lse_ref[h] = m_sc[h] * MLA_SCALE + jnp.log(l_sc[h])

    return _pcall(
        body, (q_cat, k_cat, v_b, y_all), name="mla_fwd", grid=(qi.shape[0],), prefetch=(qi, kj),
        in_specs=[pl.BlockSpec((nh, t, 256), lambda p, qi, kj: (0, qi[p], 0)),
                  pl.BlockSpec((nh, t, 256), lambda p, qi, kj: (0, kj[p], 0)),
                  pl.BlockSpec((t, nh * MLA_V), lambda p, qi, kj: (kj[p], 0)), ANY],
        out_specs=[pl.BlockSpec((t, nh * MLA_V), lambda p, qi, kj: (qi[p], 0)),
                   pl.BlockSpec((nh, t, 1), lambda p, qi, kj: (0, qi[p], 0)),
                   pl.BlockSpec((t, nh * MLA_V), lambda p, qi, kj: (qi[p], ycol))],
        out_shape=[jax.ShapeDtypeStruct((s, nh * MLA_V), F32), jax.ShapeDtypeStruct((nh, s, 1), F32),
                   jax.ShapeDtypeStruct(y_all.shape, y_all.dtype)],
        scratch_shapes=[pltpu.VMEM((nh, t, 1), F32), pltpu.VMEM((nh, t, 1), F32), pltpu.VMEM((nh, t, MLA_V), F32)],
        sem=("arbitrary",), comm=comm, io_alias={3: 2})


def _mla_bwd(q_cat, k_cat, v_b, y_b, lse, d_y, *, comm=()):
    nh, s, _ = q_cat.shape
    t = _tile(s, MLA_TILE)
    nt = s // t
    hp = 2
    wv = hp * MLA_V
    col0 = (SWA_Q_HEADS * SWA_HEAD_DIM) // wv
    qi, kj = _tile_pairs(nt, False)

    def body(qi_ref, kj_ref, q_ref, k_ref, v_ref, y_ref, lse_ref, dy_ref, dq_ref, dk_ref, dv_ref, dk_sc, dv_sc):
        step = pl.program_id(1)
        i, j = qi_ref[step], kj_ref[step]

        @pl.when(step == 0)
        def _():
            dq_ref[...] = jnp.zeros_like(dq_ref)

        @pl.when(i == j)
        def _():
            dk_sc[...] = jnp.zeros_like(dk_sc)
            dv_sc[...] = jnp.zeros_like(dv_sc)

        def update(diagonal):
            rg = min(MLA_ROW_GROUP, t) if diagonal else t
            units = [(h, r0, (r0 + rg) if diagonal else t) for h in range(hp) for r0 in range(0, t, rg)]
            cols = [slice(MLA_V * h, MLA_V * (h + 1)) for h in range(hp)]
            do_b = [dy_ref[r0:r0 + rg, cols[h]].astype(BF16) for h, r0, _ in units]
            scores = [lax.dot_general(q_ref[h, r0:r0 + rg, :], k_ref[h, 0:nk, :], NT_DIMS, preferred_element_type=F32)
                      for h, r0, nk in units]
            dps = [lax.dot_general(do_b[u], v_ref[0:nk, cols[h]], NT_DIMS, preferred_element_type=F32)
                   for u, (h, r0, nk) in enumerate(units)]
            p_b, ds_b = [], []
            for u, (h, r0, nk) in enumerate(units):
                p = jnp.exp(scores[u] * MLA_SCALE - lse_ref[h, r0:r0 + rg])
                if diagonal:
                    row = r0 + lax.broadcasted_iota(jnp.int32, (rg, nk), 0)
                    p = jnp.where(lax.broadcasted_iota(jnp.int32, (rg, nk), 1) <= row, p, 0.0)
                delta = jnp.sum(dy_ref[r0:r0 + rg, cols[h]] * y_ref[r0:r0 + rg, cols[h]], axis=-1, keepdims=True)
                p_b.append(p.astype(BF16))
                ds_b.append((p * (dps[u] - delta) * MLA_SCALE).astype(BF16))
            for u, (h, r0, nk) in enumerate(units):
                dv_sc[h, 0:nk] += lax.dot_general(p_b[u], do_b[u], TN_DIMS, preferred_element_type=F32)
                dk_sc[h, 0:nk] += lax.dot_general(ds_b[u], q_ref[h, r0:r0 + rg, :], TN_DIMS, preferred_element_type=F32)
                rows = pl.ds(pl.multiple_of(i * t + r0, rg), rg)
                dq_ref[h, rows, :] += jnp.dot(ds_b[u], k_ref[h, 0:nk, :], preferred_element_type=F32)

        @pl.when(i > j)
        def _():
            update(False)

        @pl.when(i == j)
        def _():
            update(True)

        @pl.when(i == nt - 1)
        def _():
            dk_ref[...] = dk_sc[...]
            for h in range(hp):
                dv_ref[:, MLA_V * h:MLA_V * (h + 1)] = dv_sc[h]

    return _pcall(
        body, (q_cat, k_cat, v_b, y_b, lse, d_y), name="mla_bwd", grid=(nh // hp, qi.shape[0]), prefetch=(qi, kj),
        in_specs=[pl.BlockSpec((hp, t, 256), lambda g, p, qi, kj: (g, qi[p], 0)),
                  pl.BlockSpec((hp, t, 256), lambda g, p, qi, kj: (g, kj[p], 0)),
                  pl.BlockSpec((t, wv), lambda g, p, qi, kj: (kj[p], g)),
                  pl.BlockSpec((t, wv), lambda g, p, qi, kj: (qi[p], g)),
                  pl.BlockSpec((hp, t, 1), lambda g, p, qi, kj: (g, qi[p], 0)),
                  pl.BlockSpec((t, wv), lambda g, p, qi, kj: (qi[p], col0 + g))],
        out_specs=[pl.BlockSpec((hp, s, 256), lambda g, p, qi, kj: (g, 0, 0)),
                   pl.BlockSpec((hp, t, 256), lambda g, p, qi, kj: (g, kj[p], 0)),
                   pl.BlockSpec((t, wv), lambda g, p, qi, kj: (kj[p], g))],
        out_shape=[jax.ShapeDtypeStruct((nh, s, 256), F32), jax.ShapeDtypeStruct((nh, s, 256), F32),
                   jax.ShapeDtypeStruct((s, nh * MLA_V), F32)],
        scratch_shapes=[pltpu.VMEM((hp, t, 256), F32), pltpu.VMEM((hp, t, MLA_V), F32)],
        sem=("arbitrary", "arbitrary"), comm=comm)


MEM_SCALE = MEM_DIM ** -0.5


def _mem_kv_fwd(mem, g_mem, w_memkv, g_mk):
    m_len = mem.shape[0]

    def body(mem_ref, g_ref, w_ref, gk_ref, mn_ref, kv_ref, kn_ref, v_ref):
        mn, _, _ = _norm_full(mem_ref[...], g_ref[...])
        mn_b = mn.astype(BF16)
        mn_ref[...] = mn_b
        kv = jnp.dot(mn_b, w_ref[...], preferred_element_type=F32)
        kv_ref[...] = kv
        for h in range(MEM_HEADS):
            kn, _, _ = _norm_full(kv[:, 128 * h:128 * (h + 1)], gk_ref[...])
            kn_ref[:, 128 * h:128 * (h + 1)] = kn.astype(BF16)
        v_ref[...] = kv[:, 512:1024].astype(BF16)

    return pl.pallas_call(
        body, name="mem_kv_fwd",
        out_shape=[jax.ShapeDtypeStruct((m_len, D_MODEL), BF16), jax.ShapeDtypeStruct((m_len, 1024), F32),
                   jax.ShapeDtypeStruct((m_len, 512), BF16), jax.ShapeDtypeStruct((m_len, 512), BF16)],
        compiler_params=_params(),
    )(mem, g_mem, w_memkv, g_mk)


def _mem_kv_bwd(mem, g_mem, w_memkv, g_mk, mn_b, kv, d_kn, d_v):
    m_len = mem.shape[0]

    def body(mem_ref, g_ref, w_ref, gk_ref, mn_ref, kv_ref, dkn_ref, dv_ref, dw_ref, dgmem_ref, dgk_ref):
        parts = []
        dgk = jnp.zeros((1, LANES), F32)
        for h in range(MEM_HEADS):
            _, xhat, r = _norm_full(kv_ref[:, 128 * h:128 * (h + 1)], gk_ref[...])
            dx, dgh = _norm_full_bwd(dkn_ref[:, 128 * h:128 * (h + 1)], gk_ref[...], xhat, r)
            parts.append(dx)
            dgk = dgk + dgh
        d_kv = jnp.concatenate(parts + [dv_ref[...]], axis=1).astype(BF16)
        dw_ref[...] = lax.dot_general(mn_ref[...], d_kv, TN_DIMS, preferred_element_type=F32)
        d_mn = lax.dot_general(d_kv, w_ref[...], NT_DIMS, preferred_element_type=F32)
        _, xhat, _ = _norm_full(mem_ref[...], g_ref[...])
        dgmem_ref[...] = jnp.sum(d_mn * xhat, axis=0, keepdims=True)
        dgk_ref[...] = dgk

    return pl.pallas_call(
        body, name="mem_kv_bwd",
        out_shape=[jax.ShapeDtypeStruct((D_MODEL, 1024), F32), jax.ShapeDtypeStruct((1, D_MODEL), F32),
                   jax.ShapeDtypeStruct((1, LANES), F32)],
        compiler_params=_params(),
    )(mem, g_mem, w_memkv, g_mk, mn_b, kv, d_kn, d_v)


def _mem_softmax(raw):
    sc = raw * MEM_SCALE
    e = jnp.exp(sc - jnp.max(sc, axis=-1, keepdims=True))
    return e * (1.0 / jnp.sum(e, axis=-1, keepdims=True))


def _mem_attn_fwd(qm, km, vm, y_all, *, tm=512):
    s = qm.shape[0]
    tm = _tile(s, tm)
    m_len = km.shape[0]
    ycol = (SWA_Q_HEADS * SWA_HEAD_DIM + MLA_HEADS * MLA_V) // 512

    def body(q_ref, k_ref, v_ref, _, o_ref, yb_ref):
        cols = [slice(128 * h, 128 * (h + 1)) for h in range(MEM_HEADS)]
        raws = [lax.dot_general(q_ref[:, sl], k_ref[:, sl], NT_DIMS, preferred_element_type=F32) for sl in cols]
        probs = [_mem_softmax(raw).astype(BF16) for raw in raws]
        for p, sl in zip(probs, cols):
            out = jnp.dot(p, v_ref[:, sl], preferred_element_type=F32)
            o_ref[:, sl] = out
            yb_ref[:, sl] = out.astype(BF16)

    kvspec = pl.BlockSpec((m_len, 512), lambda i: (0, 0))
    return _pcall(
        body, (qm, km, vm, y_all), name="mem_attn_fwd", grid=(s // tm,),
        in_specs=[pl.BlockSpec((tm, 512), lambda i: (i, 0)), kvspec, kvspec, ANY],
        out_specs=[pl.BlockSpec((tm, 512), lambda i: (i, 0)), pl.BlockSpec((tm, 512), lambda i: (i, ycol))],
        out_shape=[jax.ShapeDtypeStruct((s, 512), F32), jax.ShapeDtypeStruct(y_all.shape, y_all.dtype)],
        sem=("parallel",), io_alias={3: 1})


def _mem_attn_bwd(qm, km, vm, y_m, d_y, *, tm=1024):
    s = qm.shape[0]
    tm = _tile(s, tm)
    m_len = km.shape[0]
    col0 = (SWA_Q_HEADS * SWA_HEAD_DIM + MLA_HEADS * MLA_V) // 512

    def body(q_ref, k_ref, v_ref, y_ref, dy_ref, dq_ref, dk_ref, dv_ref):
        @pl.when(pl.program_id(0) == 0)
        def _():
            dk_ref[...] = jnp.zeros_like(dk_ref)
            dv_ref[...] = jnp.zeros_like(dv_ref)

        cols = [slice(128 * h, 128 * (h + 1)) for h in range(MEM_HEADS)]
        do_b = [dy_ref[:, sl].astype(BF16) for sl in cols]
        raws = [lax.dot_general(q_ref[:, sl], k_ref[:, sl], NT_DIMS, preferred_element_type=F32) for sl in cols]
        dps = [lax.dot_general(do_b[h], v_ref[:, sl], NT_DIMS, preferred_element_type=F32) for h, sl in enumerate(cols)]
        p_b, ds_b = [], []
        for h, sl in enumerate(cols):
            p = _mem_softmax(raws[h])
            delta = jnp.sum(dy_ref[:, sl] * y_ref[:, sl], axis=-1, keepdims=True)
            p_b.append(p.astype(BF16))
            ds_b.append((p * (dps[h] - delta) * MEM_SCALE).astype(BF16))
        for h, sl in enumerate(cols):
            dv_ref[:, sl] += lax.dot_general(p_b[h], do_b[h], TN_DIMS, preferred_element_type=F32)
            dq_ref[:, sl] = jnp.dot(ds_b[h], k_ref[:, sl], preferred_element_type=F32)
            dk_ref[:, sl] += lax.dot_general(ds_b[h], q_ref[:, sl], TN_DIMS, preferred_element_type=F32)

    kvspec = pl.BlockSpec((m_len, 512), lambda i: (0, 0))
    row = pl.BlockSpec((tm, 512), lambda i: (i, 0))
    return pl.pallas_call(
        body, name="mem_attn_bwd", grid=(s // tm,),
        in_specs=[row, kvspec, kvspec, row, pl.BlockSpec((tm, 512), lambda i: (i, col0))],
        out_specs=[row, kvspec, kvspec],
        out_shape=[jax.ShapeDtypeStruct((s, 512), F32), jax.ShapeDtypeStruct((m_len, 512), F32),
                   jax.ShapeDtypeStruct((m_len, 512), F32)],
        compiler_params=_params(("arbitrary",)),
    )(qm, km, vm, y_m, d_y)


def _ffn_gate_up(fn, w_gate, w_up, *, tm=512, comm=()):
    s, d = fn.shape
    nsp, _, tf = w_gate.shape
    f = nsp * tf
    tm = _tile(s, tm)

    def body(x_ref, wg_ref, wu_ref, g_ref, u_ref, a_ref):
        x = x_ref[...]
        gate = jnp.dot(x, wg_ref[...], preferred_element_type=F32)
        up = jnp.dot(x, wu_ref[...], preferred_element_type=F32)
        g_ref[...] = gate.astype(BF16)
        u_ref[...] = up.astype(BF16)
        a_ref[...] = (gate * (1.0 / (1.0 + jnp.exp(-gate))) * up).astype(BF16)

    wspec = pl.BlockSpec((None, d, tf), lambda j, i: (j, 0, 0))
    ospec = pl.BlockSpec((tm, tf), lambda j, i: (i, j))
    osh = jax.ShapeDtypeStruct((s, f), BF16)
    return _pcall(body, (fn, w_gate, w_up), name="ffn_gate_up", grid=(nsp, s // tm),
                  in_specs=[pl.BlockSpec((tm, d), lambda j, i: (i, 0)), wspec, wspec],
                  out_specs=[ospec, ospec, ospec], out_shape=[osh, osh, osh], sem=("parallel", "parallel"), comm=comm)


def _ffn_bwd_act(d_out, w_down, gate, up, *, tm=1024, tf=1408, comm=()):
    s, d = d_out.shape
    f = w_down.shape[0]
    tm, tf = _tile(s, tm), _tile(f, tf)

    sub = tm // 4 if tm % 1024 == 0 else tm

    def body(do_ref, wd_ref, g_ref, u_ref, dg_ref, du_ref):
        groups = [slice(r, r + sub) for r in range(0, tm, sub)]
        parts = [lax.dot_general(do_ref[rows, :].astype(BF16), wd_ref[...], NT_DIMS, preferred_element_type=F32)
                 for rows in groups]
        for rows, d_act in zip(groups, parts):
            gate = g_ref[rows, :].astype(F32)
            sig = 1.0 / (1.0 + jnp.exp(-gate))
            du_ref[rows, :] = (d_act * (gate * sig)).astype(BF16)
            dg_ref[rows, :] = (d_act * u_ref[rows, :].astype(F32) * (sig * (1.0 + gate * (1.0 - sig)))).astype(BF16)

    ospec = pl.BlockSpec((tm, tf), lambda j, i: (i, j))
    osh = jax.ShapeDtypeStruct((s, f), BF16)
    return _pcall(
        body, (d_out, w_down, gate, up), name="ffn_bwd_act", grid=(f // tf, s // tm),
        in_specs=[pl.BlockSpec((tm, d), lambda j, i: (i, 0)), pl.BlockSpec((tf, d), lambda j, i: (j, 0)), ospec, ospec],
        out_specs=[ospec, ospec], out_shape=[osh, osh], sem=("parallel", "parallel"), comm=comm)


def _cols(g4):
    return jnp.concatenate([g4[k] for k in range(N_CHIPS)], axis=1)


def _full_w_in(g4, *, comm=()):
    rows = g4.shape[1]
    per = IN_WIDTH // N_CHIPS
    kr0 = 2304 - (N_CHIPS - 1) * per
    tr = _row_tile(rows, 256, 16)

    def body(g_ref, o_ref):
        for k in range(N_CHIPS - 1):
            o_ref[:, per * k:per * (k + 1)] = g_ref[k]
        o_ref[:, 2304 - kr0:2304] = g_ref[N_CHIPS - 1, :, 0:kr0]
        o_ref[:, 2304:C_KR] = g_ref[N_CHIPS - 1, :, kr0 + 64:per]
        o_ref[:, C_KR:C_KR + 64] = g_ref[N_CHIPS - 1, :, kr0:kr0 + 64]
        o_ref[:, C_KR + 64:] = jnp.zeros((tr, IN_PAD - C_KR - 64), BF16)

    return _pcall(body, (g4,), name="full_w_in", grid=(rows // tr,),
                  in_specs=[pl.BlockSpec((N_CHIPS, tr, per), lambda i: (0, i, 0))],
                  out_specs=pl.BlockSpec((tr, IN_PAD), lambda i: (i, 0)),
                  out_shape=jax.ShapeDtypeStruct((rows, IN_PAD), BF16), sem=("parallel",), comm=comm)


def _shards_w_in(dwp):
    per = IN_WIDTH // N_CHIPS
    kr0 = 2304 - (N_CHIPS - 1) * per
    last = jnp.concatenate([dwp[:, (N_CHIPS - 1) * per:2304], dwp[:, C_KR:C_KR + 64], dwp[:, 2304:C_KR]], axis=1)
    assert last.shape[1] == per and kr0 == 144
    return jnp.stack([dwp[:, per * k:per * (k + 1)] for k in range(N_CHIPS - 1)] + [last])


def _full_heads(g4, first):
    return jnp.concatenate([g4[k][:, :first] for k in range(N_CHIPS)] + [g4[k][:, first:] for k in range(N_CHIPS)], axis=1)


def _shards_heads(dwp, first, rest):
    base = N_CHIPS * first
    return jnp.stack([jnp.concatenate([dwp[:, first * k:first * (k + 1)], dwp[:, base + rest * k:base + rest * (k + 1)]], axis=1)
                      for k in range(N_CHIPS)])


def _rope_tables(pos):
    inv_freq = ROPE_THETA ** (-jnp.arange(0, MLA_ROPE, 2, dtype=F32) / MLA_ROPE)
    ang = pos.astype(F32)[:, None] * inv_freq
    cos, sin = jnp.cos(ang), jnp.sin(ang)
    return jnp.tile(cos, (1, 4)), jnp.concatenate([-sin, sin, -sin, sin], axis=1)


def _gain_table(sp):
    two = lambda v: jnp.tile(v, (1, 2))
    rows = [two(sp["swa_q_norm_g"]), two(sp["swa_k_norm_g"]), sp["mla_qn_norm_g"], two(sp["mla_qr_norm_g"]),
            sp["mla_kn_norm_g"], two(sp["mla_kr_norm_g"]), sp["mem_q_norm_g"], jnp.zeros((1, LANES), F32)]
    return jnp.concatenate(rows, axis=0)


CHIP_DISTANCES = (1, 2, 3)


def _place():
    x, y, c = lax.axis_index("x"), lax.axis_index("y"), lax.axis_index("c")
    return x, y, c, 2 * x + y


def _chip_at(x, y, d):
    px = 1 - x if d & 2 else x
    py = 1 - y if d & 1 else y
    return px, py, 2 * px + py


def _row_tile(rows, want=512, mult=8):
    t = min(rows, want)
    t -= t % mult
    while rows % t:
        t -= mult
    return t


def _cast_into_slot(w, meta, *, name, comm=()):
    rows, cols = w.shape
    tr = _row_tile(rows, 512, 16)

    def body(meta_ref, w_ref, o_ref):
        o_ref[...] = w_ref[...].astype(BF16)

    return _pcall(body, (w,), name=name, grid=(rows // tr,), prefetch=(meta,),
                  in_specs=[pl.BlockSpec((tr, cols), lambda i, m: (i, 0))],
                  out_specs=pl.BlockSpec((None, tr, cols), lambda i, m: (m[0], i, 0)),
                  out_shape=jax.ShapeDtypeStruct((N_CHIPS, rows, cols), BF16), sem=("parallel",), comm=comm)


def _remote(src, dst, ssem, rsem, i, device):
    return pltpu.make_async_remote_copy(src_ref=src, dst_ref=dst, send_sem=ssem.at[i], recv_sem=rsem.at[i],
                                        device_id=device, device_id_type=MESH)


def _symmetric_stage(ins, out_shapes, aliases, n_sem, copies):
    def issue(i_refs, o_refs, ssem, rsem):
        for send, _ in copies(i_refs, o_refs, ssem, rsem):
            send.start()

    def wait(i_refs, o_refs, ssem, rsem):
        pairs = copies(i_refs, o_refs, ssem, rsem)
        for _, arrival in pairs:
            arrival.wait_recv()
        for send, _ in pairs:
            send.wait_send()

    return _Stage(ins, out_shapes, aliases, n_sem, issue, wait)


def _gather_stage(slots, leg, part=(0, 1)):
    n = len(slots)
    shapes = [jax.ShapeDtypeStruct(s.shape, s.dtype) for s in slots]
    in_place = {w: w for w in range(n)}
    if not isinstance(leg, str):
        legs = list(leg)

        def copies(i_refs, o_refs, ssem, rsem):
            return [pr for k, (which, prt) in enumerate(legs)
                    for pr in _gather_stage(slots, which, prt).leg_copies(which, 3 * n * k)(i_refs, o_refs, ssem, rsem)]

        return _symmetric_stage(slots, shapes, in_place, 3 * n * len(legs), copies)

    def leg_copies(which, base):
        def copies(_, outs, ssem, rsem):
            x, y, c, k_me = _place()
            pairs = []
            for w in range(n):
                half = outs[w].shape[1] // 2
                r0, size = _window(half, part)
                slab = lambda k, cc, w=w, half=half, r0=r0, size=size: outs[w].at[k, pl.ds(cc * half + r0, size)]
                for d in CHIP_DISTANCES:
                    px, py, k_src = _chip_at(x, y, d)
                    i = base + 3 * w + d - 1
                    if which == "ici":
                        pairs.append((_remote(slab(k_me, c), slab(k_me, c), ssem, rsem, i, (px, py, c)),
                                      _remote(slab(k_src, c), slab(k_src, c), ssem, rsem, i, (x, y, c))))
                    else:
                        pairs.append((_remote(slab(k_src, c), slab(k_src, c), ssem, rsem, i, (x, y, 1 - c)),
                                      _remote(slab(k_src, 1 - c), slab(k_src, 1 - c), ssem, rsem, i, (x, y, c))))
            return pairs
        return copies

    if leg != "both":
        st = _symmetric_stage(slots, shapes, in_place, 3 * n, leg_copies(leg, 0))
        st.leg_copies = leg_copies
        return st
    ici = _symmetric_stage(slots, shapes, in_place, 6 * n, leg_copies("ici", 0))
    d2d = _symmetric_stage(slots, shapes, in_place, 6 * n, leg_copies("d2d", 3 * n))

    def mid(*refs):
        ici.wait(*refs)
        d2d.issue(*refs)

    return _Stage(slots, shapes, in_place, 6 * n, ici.issue, d2d.wait, mid)


def _halves_stage(grads):
    n = len(grads)

    def copies(ins, outs, ssem, rsem):
        x, y, c, _ = _place()
        pairs = []
        for w in range(n):
            half = ins[w].shape[1] // 2
            pairs.append((_remote(ins[w].at[:, pl.ds((1 - c) * half, half)], outs[w], ssem, rsem, w, (x, y, 1 - c)),
                          _remote(outs[w], outs[w], ssem, rsem, w, (x, y, c))))
        return pairs

    shapes = [jax.ShapeDtypeStruct((N_CHIPS, g.shape[1] // 2, g.shape[2]), g.dtype) for g in grads]
    return _symmetric_stage(grads, shapes, {}, n, copies)


def _window(rows, part):
    idx, count = part
    size = rows // count
    assert size * count == rows and size % 16 == 0, (rows, part)
    return idx * size, size


def _chips_stage(parts, part=(0, 1), into=None):
    n = len(parts)

    def copies(ins, outs, ssem, rsem):
        x, y, c, _ = _place()
        pairs = []
        for w in range(n):
            r0, size = _window(ins[w].shape[1], part)
            for d in CHIP_DISTANCES:
                px, py, _ = _chip_at(x, y, d)
                i = 3 * w + d - 1
                land = outs[w].at[d - 1, pl.ds(r0, size)]
                pairs.append((_remote(ins[w].at[d - 1, pl.ds(r0, size)], land, ssem, rsem, i, (px, py, c)),
                              _remote(land, land, ssem, rsem, i, (x, y, c))))
        return pairs

    shapes = [jax.ShapeDtypeStruct(p.shape, p.dtype) for p in parts]
    if into is None:
        return _symmetric_stage(parts, shapes, {}, 3 * n, copies)
    return _symmetric_stage(list(parts) + list(into), shapes, {n + w: w for w in range(n)}, 3 * n, copies)


def _swap_stage(totals):
    n = len(totals)

    def copies(ins, outs, ssem, rsem):
        x, y, c, _ = _place()
        return [(_remote(ins[w], outs[w], ssem, rsem, w, (x, y, 1 - c)),
                 _remote(outs[w], outs[w], ssem, rsem, w, (x, y, c))) for w in range(n)]

    shapes = [jax.ShapeDtypeStruct(t.shape, t.dtype) for t in totals]
    return _symmetric_stage(totals, shapes, {}, n, copies)


def _run_stages(stages, *, name):
    n_ins = [len(st.ins) for st in stages]
    n_outs = [len(st.out_shapes) for st in stages]
    tot_in, tot_out = sum(n_ins), sum(n_outs)
    aliases, i0, o0 = {}, 0, 0
    for st, ni, no in zip(stages, n_ins, n_outs):
        aliases.update({i0 + a: o0 + b for a, b in st.aliases.items()})
        i0, o0 = i0 + ni, o0 + no

    def body(*refs):
        sems = refs[tot_in + tot_out:]
        for what in ("issue", "wait"):
            i0, o0 = 0, tot_in
            for k, (st, ni, no) in enumerate(zip(stages, n_ins, n_outs)):
                getattr(st, what)(refs[i0:i0 + ni], refs[o0:o0 + no], sems[2 * k], sems[2 * k + 1])
                i0, o0 = i0 + ni, o0 + no

    sem = pltpu.SemaphoreType.DMA
    res = pl.pallas_call(
        body, name=name, in_specs=[ANY] * tot_in, out_specs=[ANY] * tot_out,
        out_shape=[s for st in stages for s in st.out_shapes], input_output_aliases=aliases,
        scratch_shapes=[sem((st.n_sem,)) for st in stages for _ in range(2)],
    )(*[a for st in stages for a in st.ins])
    outs, o0 = [], 0
    for no in n_outs:
        outs.append(list(res[o0:o0 + no]))
        o0 += no
    return outs


def _add_pair(meta, g4, recv, *, name):
    nsh, rows, cols = g4.shape
    half = rows // 2
    tr = _row_tile(half, 128 if cols > 1024 else 256, 16)
    nt = half // tr

    def body(meta_ref, g0, g1, g2, g3, r0, r1, r2, r3, own_ref, oth_ref):
        own_ref[...] = g0[...] + r0[...]
        for d, (g, r) in enumerate(((g1, r1), (g2, r2), (g3, r3))):
            oth_ref[d] = (g[...] + r[...]).astype(BF16)

    blk = (None, tr, cols)
    gspec = lambda d: pl.BlockSpec(blk, lambda i, m: (jnp.bitwise_xor(m[0], d), m[1] * nt + i, 0))
    rspec = lambda d: pl.BlockSpec(blk, lambda i, m: (jnp.bitwise_xor(m[0], d), i, 0))
    grid_spec = pltpu.PrefetchScalarGridSpec(
        num_scalar_prefetch=1, grid=(nt,),
        in_specs=[gspec(d) for d in range(nsh)] + [rspec(d) for d in range(nsh)],
        out_specs=[pl.BlockSpec((tr, cols), lambda i, m: (i, 0)), pl.BlockSpec((3, tr, cols), lambda i, m: (0, i, 0))])
    return pl.pallas_call(
        body, name=name, grid_spec=grid_spec,
        out_shape=[jax.ShapeDtypeStruct((half, cols), F32), jax.ShapeDtypeStruct((3, half, cols), BF16)],
        compiler_params=_params(("parallel",)),
    )(meta, g4, g4, g4, g4, recv, recv, recv, recv)


def _add_chips(own, recv, *, name):
    half, cols = own.shape
    tr = _row_tile(half, 256, 16)

    def body(p_ref, r_ref, o_ref):
        o_ref[...] = ((p_ref[...] + r_ref[0].astype(F32)) + r_ref[1].astype(F32)) + r_ref[2].astype(F32)

    return pl.pallas_call(
        body, name=name, grid=(half // tr,),
        in_specs=[pl.BlockSpec((tr, cols), lambda i: (i, 0)), pl.BlockSpec((3, tr, cols), lambda i: (0, i, 0))],
        out_specs=pl.BlockSpec((tr, cols), lambda i: (i, 0)),
        out_shape=jax.ShapeDtypeStruct((half, cols), F32),
        compiler_params=_params(("parallel",)),
    )(own, recv)


def _adamw_math(w, g, m, v):
    m = ADAM_B1 * m + (1.0 - ADAM_B1) * g
    v = ADAM_B2 * v + (1.0 - ADAM_B2) * (g * g)
    m_hat = m / (1.0 - ADAM_B1 ** ADAM_STEP)
    v_hat = v / (1.0 - ADAM_B2 ** ADAM_STEP)
    delta = -ADAM_LR * (m_hat / (jnp.sqrt(v_hat) + ADAM_EPS) + ADAM_WD * w)
    return delta, m, v


def _adamw(meta, w, g_mine, g_theirs, m, v, *, name):
    rows, cols = w.shape
    half = rows // 2
    tr = _row_tile(half, 256)
    nt = half // tr

    def body(meta_ref, w_ref, a_ref, b_ref, m_ref, v_ref, g_ref, d_ref, mo_ref, vo_ref):
        is_mine = (pl.program_id(0) // nt) == meta_ref[1]
        g = jnp.where(is_mine, a_ref[...], b_ref[...])
        g_ref[...] = g
        d_ref[...], mo_ref[...], vo_ref[...] = _adamw_math(w_ref[...], g, m_ref[...], v_ref[...])

    blk = pl.BlockSpec((tr, cols), lambda i, mt: (i, 0))
    mine = pl.BlockSpec((tr, cols), lambda i, mt: (jnp.where(i // nt == mt[1], i % nt, 0), 0))
    theirs = pl.BlockSpec((tr, cols), lambda i, mt: (jnp.where(i // nt == mt[1], 0, i % nt), 0))
    sh = jax.ShapeDtypeStruct((rows, cols), F32)
    grid_spec = pltpu.PrefetchScalarGridSpec(
        num_scalar_prefetch=1, grid=(rows // tr,),
        in_specs=[blk, mine, theirs, blk, blk], out_specs=[blk] * 4)
    return pl.pallas_call(
        body, name=name, grid_spec=grid_spec, out_shape=[sh] * 4,
        compiler_params=_params(("arbitrary",)),
    )(meta, w, g_mine, g_theirs, m, v)


N_DEVICES = 8


def _small_step(g_pack, w_pack, m_pack, v_pack):
    rows = g_pack.shape[0]

    def body(g_ref, w_ref, m_ref, v_ref, sum_ref, d_ref, mo_ref, vo_ref, slots, ssem, rsem):
        x, y, c, _ = _place()
        me = 4 * x + 2 * y + c
        slots[me] = g_ref[...]
        copies = []
        for r in range(1, N_DEVICES):
            px = 1 - x if r & 4 else x
            py = 1 - y if r & 2 else y
            pc = 1 - c if r & 1 else c
            copies.append(pltpu.make_async_remote_copy(
                src_ref=g_ref, dst_ref=slots.at[me], send_sem=ssem.at[r - 1], recv_sem=rsem.at[r - 1],
                device_id=(px, py, pc), device_id_type=MESH))
        for cp in copies:
            cp.start()
        for r in range(1, N_DEVICES):
            src = jnp.bitwise_xor(me, r)
            pltpu.make_async_remote_copy(
                src_ref=g_ref, dst_ref=slots.at[src], send_sem=ssem.at[r - 1], recv_sem=rsem.at[r - 1],
                device_id=(x, y, c), device_id_type=MESH).wait_recv()
        for cp in copies:
            cp.wait_send()
        total = slots[0]
        for k in range(1, N_DEVICES):
            total = total + slots[k]
        sum_ref[...] = total
        d_ref[...], mo_ref[...], vo_ref[...] = _adamw_math(w_ref[...], total, m_ref[...], v_ref[...])

    sh = jax.ShapeDtypeStruct((rows, LANES), F32)
    vm = pl.BlockSpec(memory_space=pltpu.VMEM)
    return pl.pallas_call(
        body, name="small_allreduce_adamw",
        in_specs=[vm] * 4, out_specs=[vm] * 4, out_shape=[sh] * 4,
        scratch_shapes=[pltpu.VMEM((N_DEVICES, rows, LANES), F32),
                        pltpu.SemaphoreType.DMA((N_DEVICES - 1,)), pltpu.SemaphoreType.DMA((N_DEVICES - 1,))],
    )(g_pack, w_pack, m_pack, v_pack)


WEIGHTS = ("attn_norm_g", "w_in", "swa_q_norm_g", "swa_k_norm_g", "swa_sinks", "mla_cq_norm_g", "mla_ckv_norm_g",
           "w_uq", "w_ukv", "mla_qn_norm_g", "mla_qr_norm_g", "mla_kn_norm_g", "mla_kr_norm_g", "mem_norm_g",
           "w_mem_kv", "mem_q_norm_g", "mem_k_norm_g", "w_out", "ffn_norm_g", "w_gate", "w_up", "w_down")
BIG = ("w_in", "w_uq", "w_ukv", "w_mem_kv", "w_out", "w_gate", "w_up", "w_down")
SMALL = tuple(n for n in WEIGHTS if n not in BIG)
PACK_UNIT = 8 * LANES


def _pack(parts):
    flat = jnp.concatenate(parts, axis=1)
    total = flat.shape[1]
    padded = -(-total // PACK_UNIT) * PACK_UNIT
    return jnp.pad(flat, ((0, 0), (0, padded - total))).reshape(padded // LANES, LANES)


def _unpack(buf, sizes):
    flat = buf.reshape(1, buf.shape[0] * LANES)
    out, at = [], 0
    for n in sizes:
        out.append(flat[:, at:at + n])
        at += n
    return out


def kernel(x, mem, positions, attn_norm_g, w_in, swa_q_norm_g, swa_k_norm_g, swa_sinks, mla_cq_norm_g, mla_ckv_norm_g, w_uq, w_ukv, mla_qn_norm_g, mla_qr_norm_g, mla_kn_norm_g, mla_kr_norm_g, mem_norm_g, w_mem_kv, mem_q_norm_g, mem_k_norm_g, w_out, ffn_norm_g, w_gate, w_up, w_down, loss_target, m_attn_norm_g, m_w_in, m_swa_q_norm_g, m_swa_k_norm_g, m_swa_sinks, m_mla_cq_norm_g, m_mla_ckv_norm_g, m_w_uq, m_w_ukv, m_mla_qn_norm_g, m_mla_qr_norm_g, m_mla_kn_norm_g, m_mla_kr_norm_g, m_mem_norm_g, m_w_mem_kv, m_mem_q_norm_g, m_mem_k_norm_g, m_w_out, m_ffn_norm_g, m_w_gate, m_w_up, m_w_down, v_attn_norm_g, v_w_in, v_swa_q_norm_g, v_swa_k_norm_g, v_swa_sinks, v_mla_cq_norm_g, v_mla_ckv_norm_g, v_w_uq, v_w_ukv, v_mla_qn_norm_g, v_mla_qr_norm_g, v_mla_kn_norm_g, v_mla_kr_norm_g, v_mem_norm_g, v_w_mem_kv, v_mem_q_norm_g, v_mem_k_norm_g, v_w_out, v_ffn_norm_g, v_w_gate, v_w_up, v_w_down):
    given = dict(locals())
    wts = {n: given[n] for n in WEIGHTS}
    mom_m = {n: given["m_" + n] for n in WEIGHTS}
    mom_v = {n: given["v_" + n] for n in WEIGHTS}

    mx, my, mc = lax.axis_index("x"), lax.axis_index("y"), lax.axis_index("c")
    meta = jnp.stack([2 * mx + my, mc]).astype(jnp.int32)
    x, mem, pos, target = x[0], mem[0], positions[0], loss_target[0]
    sp = {n: wts[n] for n in SMALL}
    s = x.shape[0]
    cos_t, sin_t = _rope_tables(pos)
    pos_f = pos.astype(F32)
    pos_col, pos_row = pos_f.reshape(s, 1), pos_f.reshape(1, s)
    g128 = _gain_table(sp)
    sinks = sp["swa_sinks"].reshape(SWA_Q_HEADS)
    gcq, gckv = sp["mla_cq_norm_g"], sp["mla_ckv_norm_g"]
    gs = {}

    slot = {n: _cast_into_slot(wts[n][0], meta, name="cast_" + n) for n in BIG if n not in ("w_gate", "w_up", "w_down")}
    first = [slot["w_in"], slot["w_uq"], slot["w_ukv"]]
    slot["w_gate"], [first] = _cast_into_slot(wts["w_gate"][0], meta, name="cast_w_gate",
                                              comm=[_gather_stage(first, "ici", (0, 4))])
    slot["w_up"], [first] = _cast_into_slot(wts["w_up"][0], meta, name="cast_w_up",
                                            comm=[_gather_stage(first, [("ici", (1, 4)), ("d2d", (0, 4))])])
    slot["w_down"], [first] = _cast_into_slot(wts["w_down"][0], meta, name="cast_w_down",
                                              comm=[_gather_stage(first, [("ici", (2, 4)), ("d2d", (1, 4))])])
    hn, [first] = _rms_fwd(x, sp["attn_norm_g"], name="attn_norm_fwd",
                           comm=[_gather_stage(first, [("ici", (3, 4)), ("d2d", (2, 4))])])
    [first] = _run_stages([_gather_stage(first, "d2d", (3, 4))], name="gather_first_last_d2d")
    w_in_f, [wm] = _full_w_in(first[0], comm=[_gather_stage([slot["w_mem_kv"]], "ici")])
    w_uq_f, w_ukv_f = _full_heads(first[1], MLA_NOPE), _full_heads(first[2], MLA_NOPE)

    eighths = lambda leg, ks: [(leg, (k, 8)) for k in ks]
    proj, [wm, wo, wg] = _matmul(hn, w_in_f, name="in_proj",
                                 comm=[_gather_stage(wm, "d2d"), _gather_stage([slot["w_out"]], "ici"),
                                       _gather_stage([slot["w_gate"]], eighths("ici", (0,)))])
    (qa, ka, va, q_cat, k_cat, v_b, qm), [wo, wg] = _attn_prep_fwd(
        proj, g128, gcq, gckv, w_uq_f, w_ukv_f, cos_t, sin_t,
        comm=[_gather_stage(wo, "d2d"), _gather_stage(wg, eighths("ici", (1, 2)))])
    w_mem_kv_f = wm[0].reshape(D_MODEL, 2 * MEM_HEADS * MEM_DIM)
    w_out_f = wo[0].reshape(D_MODEL, D_MODEL)
    mn_b, kv_m, km, vm = _mem_kv_fwd(mem, sp["mem_norm_g"], w_mem_kv_f, sp["mem_k_norm_g"])
    (y_a, y), [wg] = _swa_fwd(qa, ka, va, pos_col, pos_row, sinks, comm=[_gather_stage(wg, eighths("ici", (3, 4, 5, 6)))])
    (y_b, lse, y), [wg, wu] = _mla_fwd(
        q_cat, k_cat, v_b, y, comm=[_gather_stage(wg, eighths("ici", (7,))),
                                    _gather_stage([slot["w_up"]], eighths("ici", (0, 1, 2, 3, 4)))])
    y_m, y = _mem_attn_fwd(qm, km, vm, y)
    h1, [wu, wg] = _matmul(y, w_out_f, add=x, name="out_proj",
                           comm=[_gather_stage(wu, eighths("ici", (5, 6, 7)) + eighths("d2d", (0, 1, 2, 3, 4))),
                                 _gather_stage(wg, "d2d")])
    fn, [wu] = _rms_fwd(h1, sp["ffn_norm_g"], name="ffn_norm_fwd",
                        comm=[_gather_stage(wu, eighths("d2d", (5, 6, 7)))])
    w_gate_f, w_up_f = wg[0], wu[0]
    (gate, up, act), [wd] = _ffn_gate_up(fn, w_gate_f, w_up_f, comm=[_gather_stage([slot["w_down"]], "both")])
    w_down_f = wd[0].reshape(D_FF, D_MODEL)
    d_out, d_out_b, loss_tile = _matmul(act, w_down_f, add=h1, name="down_proj", tm=512, tk=D_FF, loss_target=target)

    add_pair = lambda n, g4, r: _add_pair(meta, g4, r, name="grad_add_pair_" + n)
    add_chips = lambda n, own, r: _add_chips(own, r, name="grad_add_chips_" + n)
    mine, theirs = {}, {}

    dw_down = _matmul(act, d_out_b, ta=True, name="dw_down", tm=512, tn=1024, tk=s)
    dw_down = dw_down.reshape(N_CHIPS, D_FF // N_CHIPS, D_MODEL)
    (d_gate, d_up), [[r]] = _ffn_bwd_act(d_out_b, w_down_f, gate, up, comm=[_halves_stage([dw_down])])
    own_d, oth_d = add_pair("w_down", dw_down, r)
    dw_gate, [rd] = _matmul(fn, d_gate, ta=True, name="dw_gate", tm=512, tk=s, tn=D_FF // N_CHIPS, out_split=N_CHIPS,
                            comm=[_chips_stage([oth_d], (0, 2))])
    dw_up, [[r], rd] = _matmul(fn, d_up, ta=True, name="dw_up", tm=512, tk=s, tn=D_FF // N_CHIPS, out_split=N_CHIPS,
                               comm=[_halves_stage([dw_gate]), _chips_stage([oth_d], (1, 2), into=rd)])
    mine["w_down"] = add_chips("w_down", own_d, rd[0])
    own_g, oth_g = add_pair("w_gate", dw_gate, r)
    d_fn, [[r], rg, [theirs["w_down"]]] = _matmul(
        d_gate, w_gate_f, tb=True, b_split=True, pair2=(d_up, w_up_f), name="dfn", tm=512, tn=512,
        comm=[_halves_stage([dw_up]), _chips_stage([oth_g]), _swap_stage([mine["w_down"]])])
    mine["w_gate"] = add_chips("w_gate", own_g, rg[0])
    own_u, oth_u = add_pair("w_up", dw_up, r)
    d_h1, d_h1_b, gs["ffn_norm_g"] = _rms_bwd(d_fn, h1, sp["ffn_norm_g"], d_out, name="ffn_norm_bwd")
    dw_out, [[theirs["w_gate"]]] = _matmul(y, d_h1_b, ta=True, name="dw_out", tm=512, tk=s,
                                           comm=[_swap_stage([mine["w_gate"]])])
    dw_out = dw_out.reshape(N_CHIPS, D_MODEL // N_CHIPS, D_MODEL)
    d_y, [[r]] = _matmul(d_h1_b, w_out_f, tb=True, name="dy", comm=[_halves_stage([dw_out])])
    own_o, oth_o = add_pair("w_out", dw_out, r)
    (d_qa, d_ka, d_va, d_sink), [ru] = _swa_bwd(qa, ka, va, pos_col, pos_row, sinks, y_a, d_y,
                                                comm=[_chips_stage([oth_u], (0, 2))])
    (d_qcat, d_kcat, d_vb), [ru, [r]] = _mla_bwd(
        q_cat, k_cat, v_b, y_b, lse, d_y, comm=[_chips_stage([oth_u], (1, 2), into=ru), _chips_stage([oth_o])])
    mine["w_up"] = add_chips("w_up", own_u, ru[0])
    mine["w_out"] = add_chips("w_out", own_o, r)
    d_qm, d_km, d_vm = _mem_attn_bwd(qm, km, vm, y_m, d_y)
    (d_proj, dw_uq, dw_ukv, dg128, gs["mla_cq_norm_g"], gs["mla_ckv_norm_g"]), [[theirs["w_up"], theirs["w_out"]]] = \
        _attn_prep_bwd(proj, g128, gcq, gckv, w_uq_f, w_ukv_f, cos_t, sin_t, d_qa, d_ka, d_va, d_qcat, d_kcat, d_vb,
                       d_qm, comm=[_swap_stage([mine["w_up"], mine["w_out"]])])
    dw_mem_kv, gs["mem_norm_g"], gs["mem_k_norm_g"] = _mem_kv_bwd(
        mem, sp["mem_norm_g"], w_mem_kv_f, sp["mem_k_norm_g"], mn_b, kv_m, d_km, d_vm)
    late = ("w_uq", "w_ukv", "w_mem_kv")
    late_g = [_shards_heads(dw_uq, MLA_NOPE, MLA_ROPE), _shards_heads(dw_ukv, MLA_NOPE, MLA_V),
              dw_mem_kv.reshape(N_CHIPS, D_MODEL // N_CHIPS, -1)]
    dw_in, [rs] = _matmul(hn, d_proj, ta=True, name="dw_in", tm=512, tk=s, comm=[_halves_stage(late_g)])
    late_sums = [add_pair(n, g4, r) for n, g4, r in zip(late, late_g, rs)]
    dw_in = _shards_w_in(dw_in)
    d_hn, [rs, [r]] = _matmul(d_proj, w_in_f, tb=True, name="dhn", tk=1536,
                              comm=[_chips_stage([oth for _, oth in late_sums]), _halves_stage([dw_in])])
    for n, (own, _), r_n in zip(late, late_sums, rs):
        mine[n] = add_chips(n, own, r_n)
    own_i, oth_i = add_pair("w_in", dw_in, r)
    (grad_x, _, gs["attn_norm_g"]), [[r], late_theirs] = _rms_bwd(
        d_hn, x, sp["attn_norm_g"], d_h1, name="attn_norm_bwd",
        comm=[_chips_stage([oth_i]), _swap_stage([mine[n] for n in late])])
    theirs.update(zip(late, late_theirs))
    mine["w_in"] = add_chips("w_in", own_i, r)
    [[theirs["w_in"]]] = _run_stages([_swap_stage([mine["w_in"]])], name="grad_swap_w_in")

    fold = lambda r: r[:, :64] + r[:, 64:]
    gs["swa_q_norm_g"] = fold(dg128[G_SWA_Q:G_SWA_Q + 1])
    gs["swa_k_norm_g"] = fold(dg128[G_SWA_K:G_SWA_K + 1])
    gs["mla_qn_norm_g"] = dg128[G_QN:G_QN + 1]
    gs["mla_qr_norm_g"] = fold(dg128[G_QR:G_QR + 1])
    gs["mla_kn_norm_g"] = dg128[G_KN:G_KN + 1]
    gs["mla_kr_norm_g"] = fold(dg128[G_KR:G_KR + 1])
    gs["mem_q_norm_g"] = dg128[G_MQ:G_MQ + 1]
    gs["swa_sinks"] = d_sink[:, :SWA_Q_HEADS]

    grad, delta, new_m, new_v = {}, {}, {}, {}
    for n in BIG:
        g2, d, m2, v2 = _adamw(meta, wts[n][0], mine[n], theirs[n], mom_m[n][0], mom_v[n][0], name="adamw_" + n)
        grad[n], delta[n], new_m[n], new_v[n] = g2[None], d[None], m2[None], v2[None]

    sizes = [wts[n].shape[1] for n in SMALL]
    zero = jnp.zeros((1, LANES), F32)
    packs = _small_step(_pack([gs[n] for n in SMALL] + [loss_tile]), _pack([wts[n] for n in SMALL] + [zero]),
                        _pack([mom_m[n] for n in SMALL] + [zero]), _pack([mom_v[n] for n in SMALL] + [zero]))
    for store, buf in zip((grad, delta, new_m, new_v), packs):
        for n, val in zip(SMALL, _unpack(buf, sizes)):
            store[n] = val
    loss = _unpack(packs[0], sizes + [LANES])[-1][0, 0]

    return (loss, grad_x[None], *[grad[n] for n in WEIGHTS], *[delta[n] for n in WEIGHTS],
            *[new_m[n] for n in WEIGHTS], *[new_v[n] for n in WEIGHTS])
```

```python
import functools
import math

import jax
import jax.numpy as jnp
from jax import lax
from jax.experimental import pallas as pl
from jax.experimental.pallas import tpu as pltpu

F32 = jnp.float32
BF16 = jnp.bfloat16

D_MODEL = 2048
BLOCK = 128
EPS = 1e-6
NEG_INF = -1e30
SWA_Q_HEADS = 16
SWA_KV_HEADS = 2
SWA_HEAD_DIM = 64
MLA_HEADS = 4
MLA_RANK = 512
MLA_NOPE = 128
MLA_ROPE = 64
MLA_V = 128
ROPE_THETA = 10000.0
MEM_HEADS = 4
MEM_DIM = 128
D_FF = 5632
IN_WIDTH = 2880
IN_PAD = 3072
N_CHIPS = 4

ADAM_LR = 0.001
ADAM_B1 = 0.9
ADAM_B2 = 0.999
ADAM_EPS = 1e-08
ADAM_WD = 0.01
ADAM_STEP = 10

VMEM_LIMIT_BYTES = 56 * 1024 * 1024
LANES = 128

MESH = pl.DeviceIdType.MESH


def _params(sem=None, **kw):
    return pltpu.CompilerParams(dimension_semantics=sem, vmem_limit_bytes=VMEM_LIMIT_BYTES, **kw)


def _tile(n, want):
    if n <= want:
        return n
    t = want - want % LANES
    while t > 0:
        if n % t == 0:
            return t
        t -= LANES
    return n


ANY = pl.BlockSpec(memory_space=pl.ANY)


class _Stage:
    def __init__(self, ins, out_shapes, aliases, n_sem, issue, wait, mid=None):
        self.ins, self.out_shapes, self.aliases, self.n_sem = list(ins), list(out_shapes), dict(aliases), n_sem
        self.issue, self.wait, self.mid = issue, wait, mid


def _pcall(body, args, *, name, grid, in_specs, out_specs, out_shape, scratch_shapes=(), sem=None, comm=(),
           prefetch=(), io_alias=None):
    multi = isinstance(out_shape, (list, tuple))
    out_specs_l = list(out_specs) if multi else [out_specs]
    out_shape_l = list(out_shape) if multi else [out_shape]
    npf = len(prefetch)
    own_aliases = {npf + a: o for a, o in (io_alias or {}).items()}

    def call(fn, in_specs_, out_specs_, out_shape_, scratch_, operands, sem_, aliases=None):
        kw = dict(name=name, out_shape=out_shape_, compiler_params=_params(sem_))
        if aliases:
            kw["input_output_aliases"] = aliases
        if npf:
            spec = pltpu.PrefetchScalarGridSpec(num_scalar_prefetch=npf, grid=grid, in_specs=in_specs_,
                                                out_specs=out_specs_, scratch_shapes=scratch_)
            return pl.pallas_call(fn, grid_spec=spec, **kw)(*prefetch, *operands)
        return pl.pallas_call(fn, grid=grid, in_specs=in_specs_, out_specs=out_specs_, scratch_shapes=scratch_,
                              **kw)(*operands)

    if not comm:
        return call(body, list(in_specs), out_specs, out_shape, list(scratch_shapes), args, sem, own_aliases)
    n_in, n_out, n_scr = len(in_specs), len(out_specs_l), len(scratch_shapes)
    cins = [a for st in comm for a in st.ins]
    couts = [s for st in comm for s in st.out_shapes]
    aliases, ci, co = dict(own_aliases), 0, 0
    for st in comm:
        for a_i, o_i in st.aliases.items():
            aliases[npf + n_in + ci + a_i] = n_out + co + o_i
        ci, co = ci + len(st.ins), co + len(st.out_shapes)

    def wrapped(*refs):
        pre = refs[:npf]
        p = npf
        ins = refs[p:p + n_in]; p += n_in
        cin_refs = refs[p:p + len(cins)]; p += len(cins)
        outs = refs[p:p + n_out]; p += n_out
        cout_refs = refs[p:p + len(couts)]; p += len(couts)
        scr = refs[p:p + n_scr]; p += n_scr
        sems = refs[p:]
        first = functools.reduce(jnp.logical_and, [pl.program_id(a) == 0 for a in range(len(grid))])
        last = functools.reduce(jnp.logical_and, [pl.program_id(a) == grid[a] - 1 for a in range(len(grid))])

        def each(what):
            i, o = 0, 0
            for k, st in enumerate(comm):
                fn = getattr(st, what)
                if fn is not None:
                    fn(cin_refs[i:i + len(st.ins)], cout_refs[o:o + len(st.out_shapes)], sems[2 * k], sems[2 * k + 1])
                i, o = i + len(st.ins), o + len(st.out_shapes)

        @pl.when(first)
        def _():
            each("issue")

        if any(st.mid is not None for st in comm):
            n_steps = math.prod(grid)
            assert n_steps >= 4, "a two-leg stage needs a carrier with several grid steps"
            lin = functools.reduce(lambda acc, a: acc * grid[a] + pl.program_id(a), range(len(grid)), 0)

            @pl.when(lin == (3 * n_steps) // 4)
            def _():
                each("mid")

        body(*pre, *ins, *outs, *scr)

        @pl.when(last)
        def _():
            each("wait")

    sem_scr = [pltpu.SemaphoreType.DMA((st.n_sem,)) for st in comm for _ in range(2)]
    res = call(wrapped, list(in_specs) + [ANY] * len(cins), out_specs_l + [ANY] * len(couts), out_shape_l + couts,
               list(scratch_shapes) + sem_scr, (*args, *cins), ("arbitrary",) * len(grid), aliases)
    normal = list(res[:n_out])
    stage_outs, o = [], n_out
    for st in comm:
        stage_outs.append(list(res[o:o + len(st.out_shapes)]))
        o += len(st.out_shapes)
    return (normal if multi else normal[0]), stage_outs


def _matmul(a, b, *, name, ta=False, tb=False, add=None, out_dtype=F32, tm=1024, tn=1024, tk=2048,
            b_split=False, out_split=0, comm=(), loss_target=None, pair2=None):
    if ta:
        kdim, m = a.shape
    else:
        m, kdim = a.shape
    if b_split:
        assert tb
        nsp, n, kb = b.shape
        kb = kb * nsp
    elif tb:
        n, kb = b.shape
    else:
        kb, n = b.shape
    assert kb == kdim, (a.shape, b.shape, ta, tb)
    if b_split:
        tk = kdim
    if out_split:
        tn = _tile(n // out_split, tn)
    tm, tn, tk = _tile(m, tm), _tile(n, tn), _tile(kdim, tk)
    nk = kdim // tk
    dims = (((0 if ta else 1,), (1 if tb else 0,)), ((), ()))

    def product(a_ref, b_ref):
        if not b_split:
            return lax.dot_general(a_ref[...].astype(BF16), b_ref[...].astype(BF16), dims, preferred_element_type=F32)
        per = kdim // nsp
        return sum(lax.dot_general(a_ref[:, per * c:per * (c + 1)].astype(BF16), b_ref[c].astype(BF16), dims,
                                   preferred_element_type=F32) for c in range(nsp))

    def body(*refs):
        a_ref, b_ref = refs[:2]
        n_ab = 4 if pair2 is not None else 2
        add_ref = refs[n_ab] if add is not None else None
        n_in = n_ab + (add is not None) + (loss_target is not None)

        def products():
            r = product(a_ref, b_ref)
            return r if pair2 is None else r + product(refs[2], refs[3])
        o_ref = refs[n_in]

        def finish(r):
            if add_ref is not None:
                r = r + add_ref[...].astype(F32)
            if loss_target is None:
                o_ref[...] = r.astype(o_ref.dtype)
                return
            db_ref, l_ref = refs[n_in + 1], refs[n_in + 2]
            err = r - refs[n_in - 1][...]
            d_out = err * (1.0 / n)
            o_ref[...] = d_out
            db_ref[...] = d_out.astype(BF16)
            part = jnp.broadcast_to((0.5 / n) * jnp.sum(jnp.sum(err * err, axis=-1, keepdims=True), axis=0, keepdims=True),
                                    (1, LANES))
            first = jnp.logical_and(pl.program_id(0) == 0, pl.program_id(1) == 0)

            @pl.when(first)
            def _():
                l_ref[...] = part

            @pl.when(jnp.logical_not(first))
            def _():
                l_ref[...] += part

        if nk == 1:
            finish(products())
            return
        acc_ref = refs[-1]
        k = pl.program_id(2)
        part = products()

        @pl.when(k == 0)
        def _():
            acc_ref[...] = part

        @pl.when(k > 0)
        def _():
            acc_ref[...] += part

        @pl.when(k == nk - 1)
        def _():
            finish(acc_ref[...])

    a_spec = pl.BlockSpec((tk, tm), lambda i, j, k: (k, i)) if ta else pl.BlockSpec((tm, tk), lambda i, j, k: (i, k))
    if b_split:
        b_spec = pl.BlockSpec((nsp, tn, kdim // nsp), lambda i, j, k: (0, j, 0))
    elif tb:
        b_spec = pl.BlockSpec((tn, tk), lambda i, j, k: (j, k))
    else:
        b_spec = pl.BlockSpec((tk, tn), lambda i, j, k: (k, j))
    in_specs = [a_spec, b_spec]
    args = [a, b]
    if pair2 is not None:
        assert pair2[0].shape == a.shape and pair2[1].shape == b.shape
        in_specs += [a_spec, b_spec]
        args += list(pair2)
    if add is not None:
        in_specs.append(pl.BlockSpec((tm, tn), lambda i, j, k: (i, j)))
        args.append(add)
    tile = pl.BlockSpec((tm, tn), lambda i, j, k: (i, j))
    sem = ("parallel", "parallel", "arbitrary")
    if out_split:
        per = (n // out_split) // tn
        out_spec = pl.BlockSpec((None, tm, tn), lambda i, j, k: (j // per, i, j % per))
        out_shape = jax.ShapeDtypeStruct((out_split, m, n // out_split), out_dtype)
    elif loss_target is not None:
        in_specs.append(tile)
        args.append(loss_target)
        out_spec = [tile, tile, pl.BlockSpec((1, LANES), lambda i, j, k: (0, 0))]
        out_shape = [jax.ShapeDtypeStruct((m, n), F32), jax.ShapeDtypeStruct((m, n), BF16),
                     jax.ShapeDtypeStruct((1, LANES), F32)]
        sem = ("arbitrary",) * 3
    else:
        out_spec = tile
        out_shape = jax.ShapeDtypeStruct((m, n), out_dtype)
    return _pcall(body, args, name=name, grid=(m // tm, n // tn, nk), in_specs=in_specs, out_specs=out_spec,
                  out_shape=out_shape, scratch_shapes=[pltpu.VMEM((tm, tn), F32)] if nk > 1 else [],
                  sem=sem, comm=comm)


def _rms_fwd(x, g, *, name, tm=512, comm=()):
    s, d = x.shape
    tm = _tile(s, tm)

    def body(x_ref, g_ref, o_ref):
        xv = x_ref[...]
        r = lax.rsqrt(jnp.mean(xv * xv, axis=-1, keepdims=True) + EPS)
        o_ref[...] = (xv * r * g_ref[...]).astype(o_ref.dtype)

    return _pcall(body, (x, g), name=name, grid=(s // tm,),
                  in_specs=[pl.BlockSpec((tm, d), lambda i: (i, 0)), pl.BlockSpec((1, d), lambda i: (0, 0))],
                  out_specs=pl.BlockSpec((tm, d), lambda i: (i, 0)),
                  out_shape=jax.ShapeDtypeStruct((s, d), BF16), sem=("parallel",), comm=comm)


def _rms_bwd(dy, x, g, res, *, name, tm=512, comm=()):
    s, d = x.shape
    tm = _tile(s, tm)

    def body(dy_ref, x_ref, g_ref, res_ref, dx_ref, dxb_ref, dg_ref):
        xv = x_ref[...]
        dyv = dy_ref[...]
        r = lax.rsqrt(jnp.mean(xv * xv, axis=-1, keepdims=True) + EPS)
        xhat = xv * r
        dyg = dyv * g_ref[...]
        mt = jnp.mean(dyg * xhat, axis=-1, keepdims=True)
        dx = res_ref[...] + r * (dyg - xhat * mt)
        dx_ref[...] = dx
        dxb_ref[...] = dx.astype(BF16)
        part = jnp.sum(dyv * xhat, axis=0, keepdims=True)

        @pl.when(pl.program_id(0) == 0)
        def _():
            dg_ref[...] = part

        @pl.when(pl.program_id(0) > 0)
        def _():
            dg_ref[...] += part

    row = pl.BlockSpec((tm, d), lambda i: (i, 0))
    vec = pl.BlockSpec((1, d), lambda i: (0, 0))
    return _pcall(body, (dy, x, g, res), name=name, grid=(s // tm,), in_specs=[row, row, vec, row],
                  out_specs=[row, row, vec],
                  out_shape=[jax.ShapeDtypeStruct((s, d), F32), jax.ShapeDtypeStruct((s, d), BF16),
                             jax.ShapeDtypeStruct((1, d), F32)],
                  sem=("arbitrary",), comm=comm)


def _lane(shape):
    return lax.broadcasted_iota(jnp.int32, shape, 1)


def _halfsum(t, lo):
    s_lo = jnp.sum(jnp.where(lo, t, 0.0), axis=-1, keepdims=True)
    s_hi = jnp.sum(jnp.where(lo, 0.0, t), axis=-1, keepdims=True)
    return jnp.where(lo, s_lo, s_hi)


def _norm_pair(x, g, lo):
    r = lax.rsqrt(_halfsum(x * x, lo) * (1.0 / 64.0) + EPS)
    xhat = x * r
    return xhat * g, xhat, r


def _norm_pair_bwd(dy, g, xhat, r, lo):
    dyg = dy * g
    mt = _halfsum(dyg * xhat, lo) * (1.0 / 64.0)
    return r * (dyg - xhat * mt), jnp.sum(dy * xhat, axis=0, keepdims=True)


def _norm_full(x, g):
    r = lax.rsqrt(jnp.mean(x * x, axis=-1, keepdims=True) + EPS)
    xhat = x * r
    return xhat * g, xhat, r


def _norm_full_bwd(dy, g, xhat, r):
    dyg = dy * g
    mt = jnp.mean(dyg * xhat, axis=-1, keepdims=True)
    return r * (dyg - xhat * mt), jnp.sum(dy * xhat, axis=0, keepdims=True)


def _rot(x, first32):
    return jnp.where(first32, pltpu.roll(x, 96, axis=1), pltpu.roll(x, 32, axis=1))


def _rope(x, cos_t, sin_t, first32):
    return x * cos_t + _rot(x, first32) * sin_t


def _rope_bwd(dy, cos_t, sin_t, first32):
    return dy * cos_t + _rot(dy * sin_t, first32)


G_SWA_Q, G_SWA_K, G_QN, G_QR, G_KN, G_KR, G_MQ = range(7)

C_QA, C_KA, C_VA, C_CQ, C_CKV, C_QM, C_KR = 0, 1024, 1152, 1280, 1792, 2304, 2816


def _prep_common(p_ref, g128_ref, gcq_ref, gckv_ref, wuq_ref, wukv_ref, cos_ref, sin_ref):
    tm = p_ref.shape[0]
    lane = _lane((tm, LANES))
    lo = lane < 64
    first32 = (lane % 64) < 32
    cos_t = cos_ref[...]
    sin_t = sin_ref[...]
    g = lambda row: g128_ref[row:row + 1, :]
    out = dict(lo=lo, first32=first32, cos_t=cos_t, sin_t=sin_t, lane=lane)
    cq_n, cq_hat, cq_r = _norm_full(p_ref[:, C_CQ:C_CQ + MLA_RANK], gcq_ref[...])
    ckv_n, ckv_hat, ckv_r = _norm_full(p_ref[:, C_CKV:C_CKV + MLA_RANK], gckv_ref[...])
    cq_b = cq_n.astype(BF16)
    ckv_b = ckv_n.astype(BF16)
    q_b = jnp.dot(cq_b, wuq_ref[...], preferred_element_type=F32)
    kv_b = jnp.dot(ckv_b, wukv_ref[...], preferred_element_type=F32)
    out.update(cq_b=cq_b, cq_hat=cq_hat, cq_r=cq_r, ckv_b=ckv_b, ckv_hat=ckv_hat, ckv_r=ckv_r, q_b=q_b, kv_b=kv_b, g=g)
    return out


def _attn_prep_fwd(proj, g128, gcq, gckv, wuq, wukv, cos_t, sin_t, *, tm=512, comm=()):
    s = proj.shape[0]
    tm = _tile(s, tm)

    def body(p_ref, g128_ref, gcq_ref, gckv_ref, wuq_ref, wukv_ref, cos_ref, sin_ref,
             qa_ref, ka_ref, va_ref, qcat_ref, kcat_ref, vb_ref, qm_ref):
        c = _prep_common(p_ref, g128_ref, gcq_ref, gckv_ref, wuq_ref, wukv_ref, cos_ref, sin_ref)
        lo, first32, g = c["lo"], c["first32"], c["g"]
        for j in range(SWA_Q_HEADS // 2):
            y, _, _ = _norm_pair(p_ref[:, C_QA + 128 * j:C_QA + 128 * (j + 1)], g(G_SWA_Q), lo)
            qa_ref[:, 128 * j:128 * (j + 1)] = y.astype(BF16)
        y, _, _ = _norm_pair(p_ref[:, C_KA:C_KA + 128], g(G_SWA_K), lo)
        ka_ref[...] = y.astype(BF16)
        va_ref[...] = p_ref[:, C_VA:C_VA + 128].astype(BF16)
        kr, _, _ = _norm_pair(p_ref[:, C_KR:C_KR + 128], g(G_KR), lo)
        kr = jnp.where(lo, _rope(kr, c["cos_t"], c["sin_t"], first32), 0.0)
        krkr = (kr + pltpu.roll(kr, 64, axis=1)).astype(BF16)
        q_b, kv_b = c["q_b"], c["kv_b"]
        qr = []
        for j in range(MLA_HEADS // 2):
            y, _, _ = _norm_pair(q_b[:, 512 + 128 * j:512 + 128 * (j + 1)], g(G_QR), lo)
            qr.append(_rope(y, c["cos_t"], c["sin_t"], first32))
        for h in range(MLA_HEADS):
            qn, _, _ = _norm_full(q_b[:, 128 * h:128 * (h + 1)], g(G_QN))
            keep = lo if h % 2 == 0 else jnp.logical_not(lo)
            qcat_ref[h, :, 0:128] = qn.astype(BF16)
            qcat_ref[h, :, 128:256] = jnp.where(keep, qr[h // 2], 0.0).astype(BF16)
            kn, _, _ = _norm_full(kv_b[:, 128 * h:128 * (h + 1)], g(G_KN))
            kcat_ref[h, :, 0:128] = kn.astype(BF16)
            kcat_ref[h, :, 128:256] = krkr
        vb_ref[...] = kv_b[:, 512:1024].astype(BF16)
        for h in range(MEM_HEADS):
            y, _, _ = _norm_full(p_ref[:, C_QM + 128 * h:C_QM + 128 * (h + 1)], g(G_MQ))
            qm_ref[:, 128 * h:128 * (h + 1)] = y.astype(BF16)

    row = lambda w: pl.BlockSpec((tm, w), lambda i: (i, 0))
    full = lambda shape: pl.BlockSpec(shape, lambda i: tuple(0 for _ in shape))
    cat = pl.BlockSpec((MLA_HEADS, tm, 256), lambda i: (0, i, 0))
    return _pcall(
        body, (proj, g128, gcq, gckv, wuq, wukv, cos_t, sin_t), name="attn_prep_fwd", grid=(s // tm,),
        in_specs=[row(IN_PAD), full((8, 128)), full((1, 512)), full((1, 512)), full((512, 768)), full((512, 1024)),
                  row(128), row(128)],
        out_specs=[row(1024), row(128), row(128), cat, cat, row(512), row(512)],
        out_shape=[jax.ShapeDtypeStruct((s, 1024), BF16), jax.ShapeDtypeStruct((s, 128), BF16),
                   jax.ShapeDtypeStruct((s, 128), BF16), jax.ShapeDtypeStruct((MLA_HEADS, s, 256), BF16),
                   jax.ShapeDtypeStruct((MLA_HEADS, s, 256), BF16), jax.ShapeDtypeStruct((s, 512), BF16),
                   jax.ShapeDtypeStruct((s, 512), BF16)],
        sem=("parallel",), comm=comm)


def _attn_prep_bwd(proj, g128, gcq, gckv, wuq, wukv, cos_t, sin_t,
                   d_qa, d_ka, d_va, d_qcat, d_kcat, d_vb, d_qm, *, tm=512, comm=()):
    s = proj.shape[0]
    tm = _tile(s, tm)

    def body(p_ref, g128_ref, gcq_ref, gckv_ref, wuq_ref, wukv_ref, cos_ref, sin_ref,
             dqa_ref, dka_ref, dva_ref, dqcat_ref, dkcat_ref, dvb_ref, dqm_ref,
             dp_ref, dwuq_ref, dwukv_ref, dg128_ref, dgcq_ref, dgckv_ref):
        c = _prep_common(p_ref, g128_ref, gcq_ref, gckv_ref, wuq_ref, wukv_ref, cos_ref, sin_ref)
        lo, first32, g = c["lo"], c["first32"], c["g"]
        cos_v, sin_v = c["cos_t"], c["sin_t"]
        q_b, kv_b = c["q_b"], c["kv_b"]
        zero_row = jnp.zeros((1, LANES), F32)
        dg = {k: zero_row for k in range(7)}

        for j in range(SWA_Q_HEADS // 2):
            sl = slice(C_QA + 128 * j, C_QA + 128 * (j + 1))
            _, xhat, r = _norm_pair(p_ref[:, sl], g(G_SWA_Q), lo)
            dx, dgj = _norm_pair_bwd(dqa_ref[:, 128 * j:128 * (j + 1)], g(G_SWA_Q), xhat, r, lo)
            dp_ref[:, sl] = dx.astype(BF16)
            dg[G_SWA_Q] = dg[G_SWA_Q] + dgj
        _, xhat, r = _norm_pair(p_ref[:, C_KA:C_KA + 128], g(G_SWA_K), lo)
        dx, dgj = _norm_pair_bwd(dka_ref[...], g(G_SWA_K), xhat, r, lo)
        dp_ref[:, C_KA:C_KA + 128] = dx.astype(BF16)
        dg[G_SWA_K] = dgj
        dp_ref[:, C_VA:C_VA + 128] = dva_ref[...].astype(BF16)

        dqb_parts = [None] * 6
        for h in range(MLA_HEADS):
            _, xhat, r = _norm_full(q_b[:, 128 * h:128 * (h + 1)], g(G_QN))
            dx, dgj = _norm_full_bwd(dqcat_ref[h, :, 0:128], g(G_QN), xhat, r)
            dqb_parts[h] = dx
            dg[G_QN] = dg[G_QN] + dgj
        for j in range(MLA_HEADS // 2):
            _, xhat, r = _norm_pair(q_b[:, 512 + 128 * j:512 + 128 * (j + 1)], g(G_QR), lo)
            d_rot = jnp.where(lo, dqcat_ref[2 * j, :, 128:256], dqcat_ref[2 * j + 1, :, 128:256])
            d_y = _rope_bwd(d_rot, cos_v, sin_v, first32)
            dx, dgj = _norm_pair_bwd(d_y, g(G_QR), xhat, r, lo)
            dqb_parts[4 + j] = dx
            dg[G_QR] = dg[G_QR] + dgj
        d_qb = jnp.concatenate(dqb_parts, axis=1).astype(BF16)
        dwuq = lax.dot_general(c["cq_b"], d_qb, (((0,), (0,)), ((), ())), preferred_element_type=F32)
        d_cqn = lax.dot_general(d_qb, wuq_ref[...], (((1,), (1,)), ((), ())), preferred_element_type=F32)
        dx, dgcq = _norm_full_bwd(d_cqn, gcq_ref[...], c["cq_hat"], c["cq_r"])
        dp_ref[:, C_CQ:C_CQ + MLA_RANK] = dx.astype(BF16)

        dkv_parts = []
        d_krkr = jnp.zeros((p_ref.shape[0], LANES), F32)
        for h in range(MLA_HEADS):
            _, xhat, r = _norm_full(kv_b[:, 128 * h:128 * (h + 1)], g(G_KN))
            dx, dgj = _norm_full_bwd(dkcat_ref[h, :, 0:128], g(G_KN), xhat, r)
            dkv_parts.append(dx)
            dg[G_KN] = dg[G_KN] + dgj
            d_krkr = d_krkr + dkcat_ref[h, :, 128:256]
        d_kvb = jnp.concatenate(dkv_parts + [dvb_ref[...]], axis=1).astype(BF16)
        dwukv = lax.dot_general(c["ckv_b"], d_kvb, (((0,), (0,)), ((), ())), preferred_element_type=F32)
        d_ckvn = lax.dot_general(d_kvb, wukv_ref[...], (((1,), (1,)), ((), ())), preferred_element_type=F32)
        dx, dgckv = _norm_full_bwd(d_ckvn, gckv_ref[...], c["ckv_hat"], c["ckv_r"])
        dp_ref[:, C_CKV:C_CKV + MLA_RANK] = dx.astype(BF16)

        _, xhat, r = _norm_pair(p_ref[:, C_KR:C_KR + 128], g(G_KR), lo)
        d_kr = jnp.where(lo, d_krkr + pltpu.roll(d_krkr, 64, axis=1), 0.0)
        d_y = jnp.where(lo, _rope_bwd(d_kr, cos_v, sin_v, first32), 0.0)
        dx, dgj = _norm_pair_bwd(d_y, g(G_KR), xhat, r, lo)
        dp_ref[:, C_KR:C_KR + 128] = jnp.where(lo, dx, 0.0).astype(BF16)
        dp_ref[:, C_KR + 128:] = jnp.zeros((p_ref.shape[0], IN_PAD - C_KR - 128), BF16)
        dg[G_KR] = dgj

        for h in range(MEM_HEADS):
            sl = slice(C_QM + 128 * h, C_QM + 128 * (h + 1))
            _, xhat, r = _norm_full(p_ref[:, sl], g(G_MQ))
            dx, dgj = _norm_full_bwd(dqm_ref[:, 128 * h:128 * (h + 1)], g(G_MQ), xhat, r)
            dp_ref[:, sl] = dx.astype(BF16)
            dg[G_MQ] = dg[G_MQ] + dgj

        dg_tile = jnp.concatenate([dg[k] for k in range(7)] + [zero_row], axis=0)

        @pl.when(pl.program_id(0) == 0)
        def _():
            dwuq_ref[...] = dwuq
            dwukv_ref[...] = dwukv
            dg128_ref[...] = dg_tile
            dgcq_ref[...] = dgcq
            dgckv_ref[...] = dgckv

        @pl.when(pl.program_id(0) > 0)
        def _():
            dwuq_ref[...] += dwuq
            dwukv_ref[...] += dwukv
            dg128_ref[...] += dg_tile
            dgcq_ref[...] += dgcq
            dgckv_ref[...] += dgckv

    row = lambda w: pl.BlockSpec((tm, w), lambda i: (i, 0))
    full = lambda shape: pl.BlockSpec(shape, lambda i: tuple(0 for _ in shape))
    cat = pl.BlockSpec((MLA_HEADS, tm, 256), lambda i: (0, i, 0))
    return _pcall(
        body, (proj, g128, gcq, gckv, wuq, wukv, cos_t, sin_t, d_qa, d_ka, d_va, d_qcat, d_kcat, d_vb, d_qm),
        name="attn_prep_bwd", grid=(s // tm,),
        in_specs=[row(IN_PAD), full((8, 128)), full((1, 512)), full((1, 512)), full((512, 768)), full((512, 1024)),
                  row(128), row(128),
                  row(1024), row(128), row(128), cat, cat, row(512), row(512)],
        out_specs=[row(IN_PAD), full((512, 768)), full((512, 1024)), full((8, 128)), full((1, 512)), full((1, 512))],
        out_shape=[jax.ShapeDtypeStruct((s, IN_PAD), BF16), jax.ShapeDtypeStruct((512, 768), F32),
                   jax.ShapeDtypeStruct((512, 1024), F32), jax.ShapeDtypeStruct((8, 128), F32),
                   jax.ShapeDtypeStruct((1, 512), F32), jax.ShapeDtypeStruct((1, 512), F32)],
        sem=("arbitrary",), comm=comm)


SWA_SLOPES = tuple(2.0 ** (-8.0 * h / SWA_Q_HEADS) for h in range(1, SWA_Q_HEADS + 1))
SWA_SCALE = SWA_HEAD_DIM ** -0.5
NT_DIMS = (((1,), (1,)), ((), ()))
TN_DIMS = (((0,), (0,)), ((), ()))


def _swa_span(n, kp_ref, kc_ref, vp_ref, vc_ref, pcol_ref, pprow_ref, pcrow_ref):
    k_span = jnp.concatenate([kp_ref[...], kc_ref[...]], axis=0).astype(F32)
    v_span = jnp.concatenate([vp_ref[...], vc_ref[...]], axis=0).astype(F32)
    lo = _lane((2 * BLOCK, LANES)) < 64
    k_sw = pltpu.roll(k_span, 64, axis=1)
    v_sw = pltpu.roll(v_span, 64, axis=1)
    kk = (jnp.where(lo, k_span, k_sw).astype(BF16), jnp.where(lo, k_sw, k_span).astype(BF16))
    vv_lo = (jnp.where(lo, v_span, 0.0).astype(BF16), jnp.where(lo, v_sw, 0.0).astype(BF16))
    vv_hi = (jnp.where(lo, 0.0, v_sw).astype(BF16), jnp.where(lo, 0.0, v_span).astype(BF16))
    pk = jnp.concatenate([pprow_ref[...], pcrow_ref[...]], axis=1)
    dist = jnp.abs(pcol_ref[...] - pk)
    qi = lax.broadcasted_iota(jnp.int32, (BLOCK, 2 * BLOCK), 0)
    ki = lax.broadcasted_iota(jnp.int32, (BLOCK, 2 * BLOCK), 1)
    first_key = jnp.where(n > 0, qi + 1, jnp.maximum(qi + 1, BLOCK))
    valid = jnp.logical_and(ki >= first_key, ki <= qi + BLOCK)
    mask_add = jnp.where(valid, 0.0, NEG_INF)
    return kk, vv_lo, vv_hi, dist, mask_add


def _swa_heads(q_ref, lo):
    heads = []
    for j in range(SWA_Q_HEADS // 2):
        q_pair = q_ref[:, 128 * j:128 * (j + 1)].astype(F32)
        for par in (0, 1):
            q_h = jnp.where(lo if par == 0 else jnp.logical_not(lo), q_pair, 0.0).astype(BF16)
            heads.append((2 * j + par, (2 * j) // (SWA_Q_HEADS // SWA_KV_HEADS), par, q_h))
    return heads


def _swa_probs(raw, dist, mask_add, slope, sink):
    s = raw * SWA_SCALE - slope * dist + mask_add
    m = jnp.maximum(jnp.max(s, axis=-1, keepdims=True), sink)
    e = jnp.exp(s - m)
    e_sink = jnp.exp(sink - m)
    inv = 1.0 / (jnp.sum(e, axis=-1, keepdims=True) + e_sink)
    return e * inv, e_sink * inv


def _swa_specs():
    blk = lambda w: pl.BlockSpec((BLOCK, w), lambda n: (n, 0))
    prev = lambda w: pl.BlockSpec((BLOCK, w), lambda n: (jnp.maximum(n - 1, 0), 0))
    prow_c = pl.BlockSpec((1, BLOCK), lambda n: (0, n))
    prow_p = pl.BlockSpec((1, BLOCK), lambda n: (0, jnp.maximum(n - 1, 0)))
    smem = pl.BlockSpec(memory_space=pltpu.SMEM)
    return [blk(1024), prev(128), blk(128), prev(128), blk(128), blk(1), prow_p, prow_c, smem], blk


def _swa_fwd(qa, ka, va, pos_col, pos_row, sinks, *, comm=()):
    s = qa.shape[0]
    in_specs, blk = _swa_specs()

    def body(q_ref, kp_ref, kc_ref, vp_ref, vc_ref, pcol_ref, pprow_ref, pcrow_ref, sink_ref, o_ref, yb_ref):
        n = pl.program_id(0)
        kk, vv_lo, vv_hi, dist, mask_add = _swa_span(n, kp_ref, kc_ref, vp_ref, vc_ref, pcol_ref, pprow_ref, pcrow_ref)
        lo = _lane((BLOCK, LANES)) < 64
        heads = _swa_heads(q_ref, lo)
        raws = [lax.dot_general(q_h, kk[kv], NT_DIMS, preferred_element_type=F32) for _, kv, _, q_h in heads]
        probs = [_swa_probs(raw, dist, mask_add, SWA_SLOPES[h], sink_ref[h])[0].astype(BF16)
                 for raw, (h, _, _, _) in zip(raws, heads)]
        for j in range(SWA_Q_HEADS // 2):
            kv = heads[2 * j][1]
            out = (jnp.dot(probs[2 * j], vv_lo[kv], preferred_element_type=F32)
                   + jnp.dot(probs[2 * j + 1], vv_hi[kv], preferred_element_type=F32))
            o_ref[:, 128 * j:128 * (j + 1)] = out
            yb_ref[:, 128 * j:128 * (j + 1)] = out.astype(BF16)

    return _pcall(body, (qa, ka, ka, va, va, pos_col, pos_row, pos_row, sinks), name="swa_fwd", grid=(s // BLOCK,),
                  in_specs=in_specs, out_specs=[blk(1024), blk(1024)],
                  out_shape=[jax.ShapeDtypeStruct((s, 1024), F32), jax.ShapeDtypeStruct((s, D_MODEL), BF16)],
                  sem=("parallel",), comm=comm)


def _swa_bwd(qa, ka, va, pos_col, pos_row, sinks, y_a, d_y, *, comm=()):
    s = qa.shape[0]
    in_specs, blk = _swa_specs()
    whole = pl.BlockSpec((s, 128), lambda n: (0, 0))

    def body(q_ref, kp_ref, kc_ref, vp_ref, vc_ref, pcol_ref, pprow_ref, pcrow_ref, sink_ref, y_ref, dy_ref,
             dq_ref, dk_ref, dv_ref, dsink_ref):
        n = pl.program_id(0)

        @pl.when(n == 0)
        def _():
            dk_ref[...] = jnp.zeros_like(dk_ref)
            dv_ref[...] = jnp.zeros_like(dv_ref)
            dsink_ref[...] = jnp.zeros_like(dsink_ref)

        kk, vv_lo, vv_hi, dist, mask_add = _swa_span(n, kp_ref, kc_ref, vp_ref, vc_ref, pcol_ref, pprow_ref, pcrow_ref)
        lo = _lane((BLOCK, LANES)) < 64
        lo2 = _lane((2 * BLOCK, LANES)) < 64
        lane1 = _lane((1, LANES))
        dsink = jnp.zeros((1, LANES), F32)
        dkk = [jnp.zeros((2 * BLOCK, LANES), F32) for _ in range(SWA_KV_HEADS)]
        dvv = [jnp.zeros((2 * BLOCK, LANES), F32) for _ in range(SWA_KV_HEADS)]
        heads = _swa_heads(q_ref, lo)
        do_b, deltas = [], []
        for j in range(SWA_Q_HEADS // 2):
            do_pair = dy_ref[:, 128 * j:128 * (j + 1)]
            doy = do_pair * y_ref[:, 128 * j:128 * (j + 1)]
            do_b.append(do_pair.astype(BF16))
            deltas.append(jnp.sum(jnp.where(lo, doy, 0.0), axis=-1, keepdims=True))
            deltas.append(jnp.sum(jnp.where(lo, 0.0, doy), axis=-1, keepdims=True))
        raws = [lax.dot_general(q_h, kk[kv], NT_DIMS, preferred_element_type=F32) for _, kv, _, q_h in heads]
        dps = [lax.dot_general(do_b[h // 2], (vv_lo, vv_hi)[par][kv], NT_DIMS, preferred_element_type=F32)
               for h, kv, par, _ in heads]
        p_b, ds_b = [], []
        for h, kv, par, _ in heads:
            p, p_sink = _swa_probs(raws[h], dist, mask_add, SWA_SLOPES[h], sink_ref[h])
            ds = p * (dps[h] - deltas[h])
            dsink = dsink + jnp.where(lane1 == h, -jnp.sum(p_sink * deltas[h], axis=0, keepdims=True), 0.0)
            p_b.append(p.astype(BF16))
            ds_b.append((ds * SWA_SCALE).astype(BF16))
        dq_halves = []
        for h, kv, par, q_h in heads:
            dq_halves.append(jnp.dot(ds_b[h], kk[kv], preferred_element_type=F32))
            dkk[kv] = dkk[kv] + lax.dot_general(ds_b[h], q_h, TN_DIMS, preferred_element_type=F32)
            pv = lax.dot_general(p_b[h], do_b[h // 2], TN_DIMS, preferred_element_type=F32)
            dvv[kv] = dvv[kv] + jnp.where(lo2 if par == 0 else jnp.logical_not(lo2), pv, 0.0)
        for j in range(SWA_Q_HEADS // 2):
            dq_ref[:, 128 * j:128 * (j + 1)] = jnp.where(lo, dq_halves[2 * j], dq_halves[2 * j + 1])
        fold = lambda t: t + pltpu.roll(t, 64, axis=1)
        dk_span = jnp.where(lo2, fold(dkk[0]), fold(dkk[1]))
        dv_span = jnp.where(lo2, fold(dvv[0]), fold(dvv[1]))
        prev0 = pl.multiple_of(jnp.maximum(n - 1, 0) * BLOCK, BLOCK)
        cur0 = pl.multiple_of(n * BLOCK, BLOCK)
        dk_ref[pl.ds(prev0, BLOCK), :] += dk_span[0:BLOCK]
        dk_ref[pl.ds(cur0, BLOCK), :] += dk_span[BLOCK:]
        dv_ref[pl.ds(prev0, BLOCK), :] += dv_span[0:BLOCK]
        dv_ref[pl.ds(cur0, BLOCK), :] += dv_span[BLOCK:]
        dsink_ref[...] += dsink

    return _pcall(
        body, (qa, ka, ka, va, va, pos_col, pos_row, pos_row, sinks, y_a, d_y), name="swa_bwd", grid=(s // BLOCK,),
        in_specs=in_specs + [blk(1024), blk(1024)],
        out_specs=[blk(1024), whole, whole, pl.BlockSpec((1, LANES), lambda n: (0, 0))],
        out_shape=[jax.ShapeDtypeStruct((s, 1024), F32), jax.ShapeDtypeStruct((s, 128), F32),
                   jax.ShapeDtypeStruct((s, 128), F32), jax.ShapeDtypeStruct((1, LANES), F32)],
        sem=("arbitrary",), comm=comm)


MLA_SCALE = (MLA_NOPE + MLA_ROPE) ** -0.5
LOG2_E = math.log2(math.e)
MLA_TILE = 1024
MLA_ROW_GROUP = 256


def _tile_pairs(nt, q_major):
    pairs = [(i, j) for i in range(nt) for j in range(i + 1)] if q_major else \
            [(i, j) for j in range(nt) for i in range(j, nt)]
    return jnp.asarray([p[0] for p in pairs], jnp.int32), jnp.asarray([p[1] for p in pairs], jnp.int32)


def _diag_mask(t):
    return lax.broadcasted_iota(jnp.int32, (t, t), 1) <= lax.broadcasted_iota(jnp.int32, (t, t), 0)


def _mla_fwd(q_cat, k_cat, v_b, y_all, *, comm=()):
    nh, s, _ = q_cat.shape
    t = _tile(s, MLA_TILE)
    qi, kj = _tile_pairs(s // t, True)
    ycol = (SWA_Q_HEADS * SWA_HEAD_DIM) // (nh * MLA_V)

    def body(qi_ref, kj_ref, q_ref, k_ref, v_ref, _, o_ref, lse_ref, yb_ref, m_sc, l_sc, acc_sc):
        i, j = qi_ref[pl.program_id(0)], kj_ref[pl.program_id(0)]

        @pl.when(j == 0)
        def _():
            m_sc[...] = jnp.full_like(m_sc, NEG_INF)
            l_sc[...] = jnp.zeros_like(l_sc)
            acc_sc[...] = jnp.zeros_like(acc_sc)

        def update(diagonal):
            rg = min(MLA_ROW_GROUP, t)
            units = [(h, slice(r0, r0 + rg), r0, (r0 + rg) if diagonal else t) for h in range(nh) for r0 in range(0, t, rg)]
            scores = [lax.dot_general(q_ref[h, rows, :], k_ref[h, 0:nk, :], NT_DIMS, preferred_element_type=F32)
                      for h, rows, _, nk in units]
            probs, alphas = [], []
            for (h, rows, r0, nk), raw in zip(units, scores):
                if diagonal:
                    row = r0 + lax.broadcasted_iota(jnp.int32, (rg, nk), 0)
                    raw = jnp.where(lax.broadcasted_iota(jnp.int32, (rg, nk), 1) <= row, raw, NEG_INF)
                m_old = m_sc[h, rows]
                m_new = jnp.maximum(m_old, jnp.max(raw, axis=-1, keepdims=True))
                alpha = jnp.exp2((m_old - m_new) * (MLA_SCALE * LOG2_E))
                p = jnp.exp2((raw - m_new) * (MLA_SCALE * LOG2_E))
                l_sc[h, rows] = alpha * l_sc[h, rows] + jnp.sum(p, axis=-1, keepdims=True)
                m_sc[h, rows] = m_new
                probs.append(p.astype(BF16))
                alphas.append(alpha)
            for (h, rows, _, nk), p, alpha in zip(units, probs, alphas):
                acc_sc[h, rows] = alpha * acc_sc[h, rows] + jnp.dot(p, v_ref[0:nk, MLA_V * h:MLA_V * (h + 1)],
                                                                    preferred_element_type=F32)

        @pl.when(j < i)
        def _():
            update(False)

        @pl.when(j == i)
        def _():
            update(True)
            for h in range(nh):
                out = acc_sc[h] * (1.0 / l_sc[h])
                o_ref[:, MLA_V * h:MLA_V * (h + 1)] = out
                yb_ref[:, MLA_V * h:MLA_V * (h + 1)] = out.astype(BF16)
                lse_ref[h] = m_sc[h] * MLA_SCALE + jnp.log(l_sc[h])

    return _pcall(
        body, (q_cat, k_cat, v_b, y_all), name="mla_fwd", grid=(qi.shape[0],), prefetch=(qi, kj),
        in_specs=[pl.BlockSpec((nh, t, 256), lambda p, qi, kj: (0, qi[p], 0)),
                  pl.BlockSpec((nh, t, 256), lambda p, qi, kj: (0, kj[p], 0)),
                  pl.BlockSpec((t, nh * MLA_V), lambda p, qi, kj: (kj[p], 0)), ANY],
        out_specs=[pl.BlockSpec((t, nh * MLA_V), lambda p, qi, kj: (qi[p], 0)),
                   pl.BlockSpec((nh, t, 1), lambda p, qi, kj: (0, qi[p], 0)),
                   pl.BlockSpec((t, nh * MLA_V), lambda p, qi, kj: (qi[p], ycol))],
        out_shape=[jax.ShapeDtypeStruct((s, nh * MLA_V), F32), jax.ShapeDtypeStruct((nh, s, 1), F32),
                   jax.ShapeDtypeStruct(y_all.shape, y_all.dtype)],
        scratch_shapes=[pltpu.VMEM((nh, t, 1), F32), pltpu.VMEM((nh, t, 1), F32), pltpu.VMEM((nh, t, MLA_V), F32)],
        sem=("arbitrary",), comm=comm, io_alias={3: 2})


def _mla_bwd(q_cat, k_cat, v_b, y_b, lse, d_y, *, comm=()):
    nh, s, _ = q_cat.shape
    t = _tile(s, MLA_TILE)
    nt = s // t
    hp = 2
    wv = hp * MLA_V
    col0 = (SWA_Q_HEADS * SWA_HEAD_DIM) // wv
    qi, kj = _tile_pairs(nt, False)

    def body(qi_ref, kj_ref, q_ref, k_ref, v_ref, y_ref, lse_ref, dy_ref, dq_ref, dk_ref, dv_ref, dk_sc, dv_sc):
        step = pl.program_id(1)
        i, j = qi_ref[step], kj_ref[step]

        @pl.when(step == 0)
        def _():
            dq_ref[...] = jnp.zeros_like(dq_ref)

        @pl.when(i == j)
        def _():
            dk_sc[...] = jnp.zeros_like(dk_sc)
            dv_sc[...] = jnp.zeros_like(dv_sc)

        def update(diagonal):
            rg = min(MLA_ROW_GROUP, t) if diagonal else t
            units = [(h, r0, (r0 + rg) if diagonal else t) for h in range(hp) for r0 in range(0, t, rg)]
            cols = [slice(MLA_V * h, MLA_V * (h + 1)) for h in range(hp)]
            do_b = [dy_ref[r0:r0 + rg, cols[h]].astype(BF16) for h, r0, _ in units]
            scores = [lax.dot_general(q_ref[h, r0:r0 + rg, :], k_ref[h, 0:nk, :], NT_DIMS, preferred_element_type=F32)
                      for h, r0, nk in units]
            dps = [lax.dot_general(do_b[u], v_ref[0:nk, cols[h]], NT_DIMS, preferred_element_type=F32)
                   for u, (h, r0, nk) in enumerate(units)]
            p_b, ds_b = [], []
            for u, (h, r0, nk) in enumerate(units):
                p = jnp.exp(scores[u] * MLA_SCALE - lse_ref[h, r0:r0 + rg])
                if diagonal:
                    row = r0 + lax.broadcasted_iota(jnp.int32, (rg, nk), 0)
                    p = jnp.where(lax.broadcasted_iota(jnp.int32, (rg, nk), 1) <= row, p, 0.0)
                delta = jnp.sum(dy_ref[r0:r0 + rg, cols[h]] * y_ref[r0:r0 + rg, cols[h]], axis=-1, keepdims=True)
                p_b.append(p.astype(BF16))
                ds_b.append((p * (dps[u] - delta) * MLA_SCALE).astype(BF16))
            for u, (h, r0, nk) in enumerate(units):
                dv_sc[h, 0:nk] += lax.dot_general(p_b[u], do_b[u], TN_DIMS, preferred_element_type=F32)
                dk_sc[h, 0:nk] += lax.dot_general(ds_b[u], q_ref[h, r0:r0 + rg, :], TN_DIMS, preferred_element_type=F32)
                rows = pl.ds(pl.multiple_of(i * t + r0, rg), rg)
                dq_ref[h, rows, :] += jnp.dot(ds_b[u], k_ref[h, 0:nk, :], preferred_element_type=F32)

        @pl.when(i > j)
        def _():
            update(False)

        @pl.when(i == j)
        def _():
            update(True)

        @pl.when(i == nt - 1)
        def _():
            dk_ref[...] = dk_sc[...]
            for h in range(hp):
                dv_ref[:, MLA_V * h:MLA_V * (h + 1)] = dv_sc[h]

    return _pcall(
        body, (q_cat, k_cat, v_b, y_b, lse, d_y), name="mla_bwd", grid=(nh // hp, qi.shape[0]), prefetch=(qi, kj),
        in_specs=[pl.BlockSpec((hp, t, 256), lambda g, p, qi, kj: (g, qi[p], 0)),
                  pl.BlockSpec((hp, t, 256), lambda g, p, qi, kj: (g, kj[p], 0)),
                  pl.BlockSpec((t, wv), lambda g, p, qi, kj: (kj[p], g)),
                  pl.BlockSpec((t, wv), lambda g, p, qi, kj: (qi[p], g)),
                  pl.BlockSpec((hp, t, 1), lambda g, p, qi, kj: (g, qi[p], 0)),
                  pl.BlockSpec((t, wv), lambda g, p, qi, kj: (qi[p], col0 + g))],
        out_specs=[pl.BlockSpec((hp, s, 256), lambda g, p, qi, kj: (g, 0, 0)),
                   pl.BlockSpec((hp, t, 256), lambda g, p, qi, kj: (g, kj[p], 0)),
                   pl.BlockSpec((t, wv), lambda g, p, qi, kj: (kj[p], g))],
        out_shape=[jax.ShapeDtypeStruct((nh, s, 256), F32), jax.ShapeDtypeStruct((nh, s, 256), F32),
                   jax.ShapeDtypeStruct((s, nh * MLA_V), F32)],
        scratch_shapes=[pltpu.VMEM((hp, t, 256), F32), pltpu.VMEM((hp, t, MLA_V), F32)],
        sem=("arbitrary", "arbitrary"), comm=comm)


MEM_SCALE = MEM_DIM ** -0.5


def _mem_kv_fwd(mem, g_mem, w_memkv, g_mk):
    m_len = mem.shape[0]

    def body(mem_ref, g_ref, w_ref, gk_ref, mn_ref, kv_ref, kn_ref, v_ref):
        mn, _, _ = _norm_full(mem_ref[...], g_ref[...])
        mn_b = mn.astype(BF16)
        mn_ref[...] = mn_b
        kv = jnp.dot(mn_b, w_ref[...], preferred_element_type=F32)
        kv_ref[...] = kv
        for h in range(MEM_HEADS):
            kn, _, _ = _norm_full(kv[:, 128 * h:128 * (h + 1)], gk_ref[...])
            kn_ref[:, 128 * h:128 * (h + 1)] = kn.astype(BF16)
        v_ref[...] = kv[:, 512:1024].astype(BF16)

    return pl.pallas_call(
        body, name="mem_kv_fwd",
        out_shape=[jax.ShapeDtypeStruct((m_len, D_MODEL), BF16), jax.ShapeDtypeStruct((m_len, 1024), F32),
                   jax.ShapeDtypeStruct((m_len, 512), BF16), jax.ShapeDtypeStruct((m_len, 512), BF16)],
        compiler_params=_params(),
    )(mem, g_mem, w_memkv, g_mk)


def _mem_kv_bwd(mem, g_mem, w_memkv, g_mk, mn_b, kv, d_kn, d_v):
    m_len = mem.shape[0]

    def body(mem_ref, g_ref, w_ref, gk_ref, mn_ref, kv_ref, dkn_ref, dv_ref, dw_ref, dgmem_ref, dgk_ref):
        parts = []
        dgk = jnp.zeros((1, LANES), F32)
        for h in range(MEM_HEADS):
            _, xhat, r = _norm_full(kv_ref[:, 128 * h:128 * (h + 1)], gk_ref[...])
            dx, dgh = _norm_full_bwd(dkn_ref[:, 128 * h:128 * (h + 1)], gk_ref[...], xhat, r)
            parts.append(dx)
            dgk = dgk + dgh
        d_kv = jnp.concatenate(parts + [dv_ref[...]], axis=1).astype(BF16)
        dw_ref[...] = lax.dot_general(mn_ref[...], d_kv, TN_DIMS, preferred_element_type=F32)
        d_mn = lax.dot_general(d_kv, w_ref[...], NT_DIMS, preferred_element_type=F32)
        _, xhat, _ = _norm_full(mem_ref[...], g_ref[...])
        dgmem_ref[...] = jnp.sum(d_mn * xhat, axis=0, keepdims=True)
        dgk_ref[...] = dgk

    return pl.pallas_call(
        body, name="mem_kv_bwd",
        out_shape=[jax.ShapeDtypeStruct((D_MODEL, 1024), F32), jax.ShapeDtypeStruct((1, D_MODEL), F32),
                   jax.ShapeDtypeStruct((1, LANES), F32)],
        compiler_params=_params(),
    )(mem, g_mem, w_memkv, g_mk, mn_b, kv, d_kn, d_v)


def _mem_softmax(raw):
    sc = raw * MEM_SCALE
    e = jnp.exp(sc - jnp.max(sc, axis=-1, keepdims=True))
    return e * (1.0 / jnp.sum(e, axis=-1, keepdims=True))


def _mem_attn_fwd(qm, km, vm, y_all, *, tm=512):
    s = qm.shape[0]
    tm = _tile(s, tm)
    m_len = km.shape[0]
    ycol = (SWA_Q_HEADS * SWA_HEAD_DIM + MLA_HEADS * MLA_V) // 512

    def body(q_ref, k_ref, v_ref, _, o_ref, yb_ref):
        cols = [slice(128 * h, 128 * (h + 1)) for h in range(MEM_HEADS)]
        raws = [lax.dot_general(q_ref[:, sl], k_ref[:, sl], NT_DIMS, preferred_element_type=F32) for sl in cols]
        probs = [_mem_softmax(raw).astype(BF16) for raw in raws]
        for p, sl in zip(probs, cols):
            out = jnp.dot(p, v_ref[:, sl], preferred_element_type=F32)
            o_ref[:, sl] = out
            yb_ref[:, sl] = out.astype(BF16)

    kvspec = pl.BlockSpec((m_len, 512), lambda i: (0, 0))
    return _pcall(
        body, (qm, km, vm, y_all), name="mem_attn_fwd", grid=(s // tm,),
        in_specs=[pl.BlockSpec((tm, 512), lambda i: (i, 0)), kvspec, kvspec, ANY],
        out_specs=[pl.BlockSpec((tm, 512), lambda i: (i, 0)), pl.BlockSpec((tm, 512), lambda i: (i, ycol))],
        out_shape=[jax.ShapeDtypeStruct((s, 512), F32), jax.ShapeDtypeStruct(y_all.shape, y_all.dtype)],
        sem=("parallel",), io_alias={3: 1})


def _mem_attn_bwd(qm, km, vm, y_m, d_y, *, tm=1024):
    s = qm.shape[0]
    tm = _tile(s, tm)
    m_len = km.shape[0]
    col0 = (SWA_Q_HEADS * SWA_HEAD_DIM + MLA_HEADS * MLA_V) // 512

    def body(q_ref, k_ref, v_ref, y_ref, dy_ref, dq_ref, dk_ref, dv_ref):
        @pl.when(pl.program_id(0) == 0)
        def _():
            dk_ref[...] = jnp.zeros_like(dk_ref)
            dv_ref[...] = jnp.zeros_like(dv_ref)

        cols = [slice(128 * h, 128 * (h + 1)) for h in range(MEM_HEADS)]
        do_b = [dy_ref[:, sl].astype(BF16) for sl in cols]
        raws = [lax.dot_general(q_ref[:, sl], k_ref[:, sl], NT_DIMS, preferred_element_type=F32) for sl in cols]
        dps = [lax.dot_general(do_b[h], v_ref[:, sl], NT_DIMS, preferred_element_type=F32) for h, sl in enumerate(cols)]
        p_b, ds_b = [], []
        for h, sl in enumerate(cols):
            p = _mem_softmax(raws[h])
            delta = jnp.sum(dy_ref[:, sl] * y_ref[:, sl], axis=-1, keepdims=True)
            p_b.append(p.astype(BF16))
            ds_b.append((p * (dps[h] - delta) * MEM_SCALE).astype(BF16))
        for h, sl in enumerate(cols):
            dv_ref[:, sl] += lax.dot_general(p_b[h], do_b[h], TN_DIMS, preferred_element_type=F32)
            dq_ref[:, sl] = jnp.dot(ds_b[h], k_ref[:, sl], preferred_element_type=F32)
            dk_ref[:, sl] += lax.dot_general(ds_b[h], q_ref[:, sl], TN_DIMS, preferred_element_type=F32)

    kvspec = pl.BlockSpec((m_len, 512), lambda i: (0, 0))
    row = pl.BlockSpec((tm, 512), lambda i: (i, 0))
    return pl.pallas_call(
        body, name="mem_attn_bwd", grid=(s // tm,),
        in_specs=[row, kvspec, kvspec, row, pl.BlockSpec((tm, 512), lambda i: (i, col0))],
        out_specs=[row, kvspec, kvspec],
        out_shape=[jax.ShapeDtypeStruct((s, 512), F32), jax.ShapeDtypeStruct((m_len, 512), F32),
                   jax.ShapeDtypeStruct((m_len, 512), F32)],
        compiler_params=_params(("arbitrary",)),
    )(qm, km, vm, y_m, d_y)


def _ffn_gate_up(fn, w_gate, w_up, *, tm=512, comm=()):
    s, d = fn.shape
    nsp, _, tf = w_gate.shape
    f = nsp * tf
    tm = _tile(s, tm)

    def body(x_ref, wg_ref, wu_ref, g_ref, u_ref, a_ref):
        x = x_ref[...]
        gate = jnp.dot(x, wg_ref[...], preferred_element_type=F32)
        up = jnp.dot(x, wu_ref[...], preferred_element_type=F32)
        g_ref[...] = gate.astype(BF16)
        u_ref[...] = up.astype(BF16)
        a_ref[...] = (gate * (1.0 / (1.0 + jnp.exp(-gate))) * up).astype(BF16)

    wspec = pl.BlockSpec((None, d, tf), lambda j, i: (j, 0, 0))
    ospec = pl.BlockSpec((tm, tf), lambda j, i: (i, j))
    osh = jax.ShapeDtypeStruct((s, f), BF16)
    return _pcall(body, (fn, w_gate, w_up), name="ffn_gate_up", grid=(nsp, s // tm),
                  in_specs=[pl.BlockSpec((tm, d), lambda j, i: (i, 0)), wspec, wspec],
                  out_specs=[ospec, ospec, ospec], out_shape=[osh, osh, osh], sem=("parallel", "parallel"), comm=comm)


def _ffn_bwd_act(d_out, w_down, gate, up, *, tm=1024, tf=1408, comm=()):
    s, d = d_out.shape
    f = w_down.shape[0]
    tm, tf = _tile(s, tm), _tile(f, tf)

    sub = tm // 4 if tm % 1024 == 0 else tm

    def body(do_ref, wd_ref, g_ref, u_ref, dg_ref, du_ref):
        groups = [slice(r, r + sub) for r in range(0, tm, sub)]
        parts = [lax.dot_general(do_ref[rows, :].astype(BF16), wd_ref[...], NT_DIMS, preferred_element_type=F32)
                 for rows in groups]
        for rows, d_act in zip(groups, parts):
            gate = g_ref[rows, :].astype(F32)
            sig = 1.0 / (1.0 + jnp.exp(-gate))
            du_ref[rows, :] = (d_act * (gate * sig)).astype(BF16)
            dg_ref[rows, :] = (d_act * u_ref[rows, :].astype(F32) * (sig * (1.0 + gate * (1.0 - sig)))).astype(BF16)

    ospec = pl.BlockSpec((tm, tf), lambda j, i: (i, j))
    osh = jax.ShapeDtypeStruct((s, f), BF16)
    return _pcall(
        body, (d_out, w_down, gate, up), name="ffn_bwd_act", grid=(f // tf, s // tm),
        in_specs=[pl.BlockSpec((tm, d), lambda j, i: (i, 0)), pl.BlockSpec((tf, d), lambda j, i: (j, 0)), ospec, ospec],
        out_specs=[ospec, ospec], out_shape=[osh, osh], sem=("parallel", "parallel"), comm=comm)


def _cols(g4):
    return jnp.concatenate([g4[k] for k in range(N_CHIPS)], axis=1)


def _full_w_in(g4, *, comm=()):
    rows = g4.shape[1]
    per = IN_WIDTH // N_CHIPS
    kr0 = 2304 - (N_CHIPS - 1) * per
    tr = _row_tile(rows, 256, 16)

    def body(g_ref, o_ref):
        for k in range(N_CHIPS - 1):
            o_ref[:, per * k:per * (k + 1)] = g_ref[k]
        o_ref[:, 2304 - kr0:2304] = g_ref[N_CHIPS - 1, :, 0:kr0]
        o_ref[:, 2304:C_KR] = g_ref[N_CHIPS - 1, :, kr0 + 64:per]
        o_ref[:, C_KR:C_KR + 64] = g_ref[N_CHIPS - 1, :, kr0:kr0 + 64]
        o_ref[:, C_KR + 64:] = jnp.zeros((tr, IN_PAD - C_KR - 64), BF16)

    return _pcall(body, (g4,), name="full_w_in", grid=(rows // tr,),
                  in_specs=[pl.BlockSpec((N_CHIPS, tr, per), lambda i: (0, i, 0))],
                  out_specs=pl.BlockSpec((tr, IN_PAD), lambda i: (i, 0)),
                  out_shape=jax.ShapeDtypeStruct((rows, IN_PAD), BF16), sem=("parallel",), comm=comm)


def _shards_w_in(dwp):
    per = IN_WIDTH // N_CHIPS
    kr0 = 2304 - (N_CHIPS - 1) * per
    last = jnp.concatenate([dwp[:, (N_CHIPS - 1) * per:2304], dwp[:, C_KR:C_KR + 64], dwp[:, 2304:C_KR]], axis=1)
    assert last.shape[1] == per and kr0 == 144
    return jnp.stack([dwp[:, per * k:per * (k + 1)] for k in range(N_CHIPS - 1)] + [last])


def _full_heads(g4, first):
    return jnp.concatenate([g4[k][:, :first] for k in range(N_CHIPS)] + [g4[k][:, first:] for k in range(N_CHIPS)], axis=1)


def _shards_heads(dwp, first, rest):
    base = N_CHIPS * first
    return jnp.stack([jnp.concatenate([dwp[:, first * k:first * (k + 1)], dwp[:, base + rest * k:base + rest * (k + 1)]], axis=1)
                      for k in range(N_CHIPS)])


def _rope_tables(pos):
    inv_freq = ROPE_THETA ** (-jnp.arange(0, MLA_ROPE, 2, dtype=F32) / MLA_ROPE)
    ang = pos.astype(F32)[:, None] * inv_freq
    cos, sin = jnp.cos(ang), jnp.sin(ang)
    return jnp.tile(cos, (1, 4)), jnp.concatenate([-sin, sin, -sin, sin], axis=1)


def _gain_table(sp):
    two = lambda v: jnp.tile(v, (1, 2))
    rows = [two(sp["swa_q_norm_g"]), two(sp["swa_k_norm_g"]), sp["mla_qn_norm_g"], two(sp["mla_qr_norm_g"]),
            sp["mla_kn_norm_g"], two(sp["mla_kr_norm_g"]), sp["mem_q_norm_g"], jnp.zeros((1, LANES), F32)]
    return jnp.concatenate(rows, axis=0)


CHIP_DISTANCES = (1, 2, 3)


def _place():
    x, y, c = lax.axis_index("x"), lax.axis_index("y"), lax.axis_index("c")
    return x, y, c, 2 * x + y


def _chip_at(x, y, d):
    px = 1 - x if d & 2 else x
    py = 1 - y if d & 1 else y
    return px, py, 2 * px + py


def _row_tile(rows, want=512, mult=8):
    t = min(rows, want)
    t -= t % mult
    while rows % t:
        t -= mult
    return t


def _cast_into_slot(w, meta, *, name, comm=()):
    rows, cols = w.shape
    tr = _row_tile(rows, 512, 16)

    def body(meta_ref, w_ref, o_ref):
        o_ref[...] = w_ref[...].astype(BF16)

    return _pcall(body, (w,), name=name, grid=(rows // tr,), prefetch=(meta,),
                  in_specs=[pl.BlockSpec((tr, cols), lambda i, m: (i, 0))],
                  out_specs=pl.BlockSpec((None, tr, cols), lambda i, m: (m[0], i, 0)),
                  out_shape=jax.ShapeDtypeStruct((N_CHIPS, rows, cols), BF16), sem=("parallel",), comm=comm)


def _remote(src, dst, ssem, rsem, i, device):
    return pltpu.make_async_remote_copy(src_ref=src, dst_ref=dst, send_sem=ssem.at[i], recv_sem=rsem.at[i],
                                        device_id=device, device_id_type=MESH)


def _symmetric_stage(ins, out_shapes, aliases, n_sem, copies):
    def issue(i_refs, o_refs, ssem, rsem):
        for send, _ in copies(i_refs, o_refs, ssem, rsem):
            send.start()

    def wait(i_refs, o_refs, ssem, rsem):
        pairs = copies(i_refs, o_refs, ssem, rsem)
        for _, arrival in pairs:
            arrival.wait_recv()
        for send, _ in pairs:
            send.wait_send()

    return _Stage(ins, out_shapes, aliases, n_sem, issue, wait)


def _gather_stage(slots, leg, part=(0, 1)):
    n = len(slots)
    shapes = [jax.ShapeDtypeStruct(s.shape, s.dtype) for s in slots]
    in_place = {w: w for w in range(n)}
    if not isinstance(leg, str):
        legs = list(leg)

        def copies(i_refs, o_refs, ssem, rsem):
            return [pr for k, (which, prt) in enumerate(legs)
                    for pr in _gather_stage(slots, which, prt).leg_copies(which, 3 * n * k)(i_refs, o_refs, ssem, rsem)]

        return _symmetric_stage(slots, shapes, in_place, 3 * n * len(legs), copies)

    def leg_copies(which, base):
        def copies(_, outs, ssem, rsem):
            x, y, c, k_me = _place()
            pairs = []
            for w in range(n):
                half = outs[w].shape[1] // 2
                r0, size = _window(half, part)
                slab = lambda k, cc, w=w, half=half, r0=r0, size=size: outs[w].at[k, pl.ds(cc * half + r0, size)]
                for d in CHIP_DISTANCES:
                    px, py, k_src = _chip_at(x, y, d)
                    i = base + 3 * w + d - 1
                    if which == "ici":
                        pairs.append((_remote(slab(k_me, c), slab(k_me, c), ssem, rsem, i, (px, py, c)),
                                      _remote(slab(k_src, c), slab(k_src, c), ssem, rsem, i, (x, y, c))))
                    else:
                        pairs.append((_remote(slab(k_src, c), slab(k_src, c), ssem, rsem, i, (x, y, 1 - c)),
                                      _remote(slab(k_src, 1 - c), slab(k_src, 1 - c), ssem, rsem, i, (x, y, c))))
            return pairs
        return copies

    if leg != "both":
        st = _symmetric_stage(slots, shapes, in_place, 3 * n, leg_copies(leg, 0))
        st.leg_copies = leg_copies
        return st
    ici = _symmetric_stage(slots, shapes, in_place, 6 * n, leg_copies("ici", 0))
    d2d = _symmetric_stage(slots, shapes, in_place, 6 * n, leg_copies("d2d", 3 * n))

    def mid(*refs):
        ici.wait(*refs)
        d2d.issue(*refs)

    return _Stage(slots, shapes, in_place, 6 * n, ici.issue, d2d.wait, mid)


def _halves_stage(grads):
    n = len(grads)

    def copies(ins, outs, ssem, rsem):
        x, y, c, _ = _place()
        pairs = []
        for w in range(n):
            half = ins[w].shape[1] // 2
            pairs.append((_remote(ins[w].at[:, pl.ds((1 - c) * half, half)], outs[w], ssem, rsem, w, (x, y, 1 - c)),
                          _remote(outs[w], outs[w], ssem, rsem, w, (x, y, c))))
        return pairs

    shapes = [jax.ShapeDtypeStruct((N_CHIPS, g.shape[1] // 2, g.shape[2]), g.dtype) for g in grads]
    return _symmetric_stage(grads, shapes, {}, n, copies)


def _window(rows, part):
    idx, count = part
    size = rows // count
    assert size * count == rows and size % 16 == 0, (rows, part)
    return idx * size, size


def _chips_stage(parts, part=(0, 1), into=None):
    n = len(parts)

    def copies(ins, outs, ssem, rsem):
        x, y, c, _ = _place()
        pairs = []
        for w in range(n):
            r0, size = _window(ins[w].shape[1], part)
            for d in CHIP_DISTANCES:
                px, py, _ = _chip_at(x, y, d)
                i = 3 * w + d - 1
                land = outs[w].at[d - 1, pl.ds(r0, size)]
                pairs.append((_remote(ins[w].at[d - 1, pl.ds(r0, size)], land, ssem, rsem, i, (px, py, c)),
                              _remote(land, land, ssem, rsem, i, (x, y, c))))
        return pairs

    shapes = [jax.ShapeDtypeStruct(p.shape, p.dtype) for p in parts]
    if into is None:
        return _symmetric_stage(parts, shapes, {}, 3 * n, copies)
    return _symmetric_stage(list(parts) + list(into), shapes, {n + w: w for w in range(n)}, 3 * n, copies)


def _swap_stage(totals):
    n = len(totals)

    def copies(ins, outs, ssem, rsem):
        x, y, c, _ = _place()
        return [(_remote(ins[w], outs[w], ssem, rsem, w, (x, y, 1 - c)),
                 _remote(outs[w], outs[w], ssem, rsem, w, (x, y, c))) for w in range(n)]

    shapes = [jax.ShapeDtypeStruct(t.shape, t.dtype) for t in totals]
    return _symmetric_stage(totals, shapes, {}, n, copies)


def _run_stages(stages, *, name):
    n_ins = [len(st.ins) for st in stages]
    n_outs = [len(st.out_shapes) for st in stages]
    tot_in, tot_out = sum(n_ins), sum(n_outs)
    aliases, i0, o0 = {}, 0, 0
    for st, ni, no in zip(stages, n_ins, n_outs):
        aliases.update({i0 + a: o0 + b for a, b in st.aliases.items()})
        i0, o0 = i0 + ni, o0 + no

    def body(*refs):
        sems = refs[tot_in + tot_out:]
        for what in ("issue", "wait"):
            i0, o0 = 0, tot_in
            for k, (st, ni, no) in enumerate(zip(stages, n_ins, n_outs)):
                getattr(st, what)(refs[i0:i0 + ni], refs[o0:o0 + no], sems[2 * k], sems[2 * k + 1])
                i0, o0 = i0 + ni, o0 + no

    sem = pltpu.SemaphoreType.DMA
    res = pl.pallas_call(
        body, name=name, in_specs=[ANY] * tot_in, out_specs=[ANY] * tot_out,
        out_shape=[s for st in stages for s in st.out_shapes], input_output_aliases=aliases,
        scratch_shapes=[sem((st.n_sem,)) for st in stages for _ in range(2)],
    )(*[a for st in stages for a in st.ins])
    outs, o0 = [], 0
    for no in n_outs:
        outs.append(list(res[o0:o0 + no]))
        o0 += no
    return outs


def _add_pair(meta, g4, recv, *, name):
    nsh, rows, cols = g4.shape
    half = rows // 2
    tr = _row_tile(half, 128 if cols > 1024 else 256, 16)
    nt = half // tr

    def body(meta_ref, g0, g1, g2, g3, r0, r1, r2, r3, own_ref, oth_ref):
        own_ref[...] = g0[...] + r0[...]
        for d, (g, r) in enumerate(((g1, r1), (g2, r2), (g3, r3))):
            oth_ref[d] = (g[...] + r[...]).astype(BF16)

    blk = (None, tr, cols)
    gspec = lambda d: pl.BlockSpec(blk, lambda i, m: (jnp.bitwise_xor(m[0], d), m[1] * nt + i, 0))
    rspec = lambda d: pl.BlockSpec(blk, lambda i, m: (jnp.bitwise_xor(m[0], d), i, 0))
    grid_spec = pltpu.PrefetchScalarGridSpec(
        num_scalar_prefetch=1, grid=(nt,),
        in_specs=[gspec(d) for d in range(nsh)] + [rspec(d) for d in range(nsh)],
        out_specs=[pl.BlockSpec((tr, cols), lambda i, m: (i, 0)), pl.BlockSpec((3, tr, cols), lambda i, m: (0, i, 0))])
    return pl.pallas_call(
        body, name=name, grid_spec=grid_spec,
        out_shape=[jax.ShapeDtypeStruct((half, cols), F32), jax.ShapeDtypeStruct((3, half, cols), BF16)],
        compiler_params=_params(("parallel",)),
    )(meta, g4, g4, g4, g4, recv, recv, recv, recv)


def _add_chips(own, recv, *, name):
    half, cols = own.shape
    tr = _row_tile(half, 256, 16)

    def body(p_ref, r_ref, o_ref):
        o_ref[...] = ((p_ref[...] + r_ref[0].astype(F32)) + r_ref[1].astype(F32)) + r_ref[2].astype(F32)

    return pl.pallas_call(
        body, name=name, grid=(half // tr,),
        in_specs=[pl.BlockSpec((tr, cols), lambda i: (i, 0)), pl.BlockSpec((3, tr, cols), lambda i: (0, i, 0))],
        out_specs=pl.BlockSpec((tr, cols), lambda i: (i, 0)),
        out_shape=jax.ShapeDtypeStruct((half, cols), F32),
        compiler_params=_params(("parallel",)),
    )(own, recv)


def _adamw_math(w, g, m, v):
    m = ADAM_B1 * m + (1.0 - ADAM_B1) * g
    v = ADAM_B2 * v + (1.0 - ADAM_B2) * (g * g)
    m_hat = m / (1.0 - ADAM_B1 ** ADAM_STEP)
    v_hat = v / (1.0 - ADAM_B2 ** ADAM_STEP)
    delta = -ADAM_LR * (m_hat / (jnp.sqrt(v_hat) + ADAM_EPS) + ADAM_WD * w)
    return delta, m, v


def _adamw(meta, w, g_mine, g_theirs, m, v, *, name):
    rows, cols = w.shape
    half = rows // 2
    tr = _row_tile(half, 256)
    nt = half // tr

    def body(meta_ref, w_ref, a_ref, b_ref, m_ref, v_ref, g_ref, d_ref, mo_ref, vo_ref):
        is_mine = (pl.program_id(0) // nt) == meta_ref[1]
        g = jnp.where(is_mine, a_ref[...], b_ref[...])
        g_ref[...] = g
        d_ref[...], mo_ref[...], vo_ref[...] = _adamw_math(w_ref[...], g, m_ref[...], v_ref[...])

    blk = pl.BlockSpec((tr, cols), lambda i, mt: (i, 0))
    mine = pl.BlockSpec((tr, cols), lambda i, mt: (jnp.where(i // nt == mt[1], i % nt, 0), 0))
    theirs = pl.BlockSpec((tr, cols), lambda i, mt: (jnp.where(i // nt == mt[1], 0, i % nt), 0))
    sh = jax.ShapeDtypeStruct((rows, cols), F32)
    grid_spec = pltpu.PrefetchScalarGridSpec(
        num_scalar_prefetch=1, grid=(rows // tr,),
        in_specs=[blk, mine, theirs, blk, blk], out_specs=[blk] * 4)
    return pl.pallas_call(
        body, name=name, grid_spec=grid_spec, out_shape=[sh] * 4,
        compiler_params=_params(("arbitrary",)),
    )(meta, w, g_mine, g_theirs, m, v)


N_DEVICES = 8


def _small_step(g_pack, w_pack, m_pack, v_pack):
    rows = g_pack.shape[0]

    def body(g_ref, w_ref, m_ref, v_ref, sum_ref, d_ref, mo_ref, vo_ref, slots, ssem, rsem):
        x, y, c, _ = _place()
        me = 4 * x + 2 * y + c
        slots[me] = g_ref[...]
        copies = []
        for r in range(1, N_DEVICES):
            px = 1 - x if r & 4 else x
            py = 1 - y if r & 2 else y
            pc = 1 - c if r & 1 else c
            copies.append(pltpu.make_async_remote_copy(
                src_ref=g_ref, dst_ref=slots.at[me], send_sem=ssem.at[r - 1], recv_sem=rsem.at[r - 1],
                device_id=(px, py, pc), device_id_type=MESH))
        for cp in copies:
            cp.start()
        for r in range(1, N_DEVICES):
            src = jnp.bitwise_xor(me, r)
            pltpu.make_async_remote_copy(
                src_ref=g_ref, dst_ref=slots.at[src], send_sem=ssem.at[r - 1], recv_sem=rsem.at[r - 1],
                device_id=(x, y, c), device_id_type=MESH).wait_recv()
        for cp in copies:
            cp.wait_send()
        total = slots[0]
        for k in range(1, N_DEVICES):
            total = total + slots[k]
        sum_ref[...] = total
        d_ref[...], mo_ref[...], vo_ref[...] = _adamw_math(w_ref[...], total, m_ref[...], v_ref[...])

    sh = jax.ShapeDtypeStruct((rows, LANES), F32)
    vm = pl.BlockSpec(memory_space=pltpu.VMEM)
    return pl.pallas_call(
        body, name="small_allreduce_adamw",
        in_specs=[vm] * 4, out_specs=[vm] * 4, out_shape=[sh] * 4,
        scratch_shapes=[pltpu.VMEM((N_DEVICES, rows, LANES), F32),
                        pltpu.SemaphoreType.DMA((N_DEVICES - 1,)), pltpu.SemaphoreType.DMA((N_DEVICES - 1,))],
    )(g_pack, w_pack, m_pack, v_pack)


WEIGHTS = ("attn_norm_g", "w_in", "swa_q_norm_g", "swa_k_norm_g", "swa_sinks", "mla_cq_norm_g", "mla_ckv_norm_g",
           "w_uq", "w_ukv", "mla_qn_norm_g", "mla_qr_norm_g", "mla_kn_norm_g", "mla_kr_norm_g", "mem_norm_g",
           "w_mem_kv", "mem_q_norm_g", "mem_k_norm_g", "w_out", "ffn_norm_g", "w_gate", "w_up", "w_down")
BIG = ("w_in", "w_uq", "w_ukv", "w_mem_kv", "w_out", "w_gate", "w_up", "w_down")
SMALL = tuple(n for n in WEIGHTS if n not in BIG)
PACK_UNIT = 8 * LANES


def _pack(parts):
    flat = jnp.concatenate(parts, axis=1)
    total = flat.shape[1]
    padded = -(-total // PACK_UNIT) * PACK_UNIT
    return jnp.pad(flat, ((0, 0), (0, padded - total))).reshape(padded // LANES, LANES)


def _unpack(buf, sizes):
    flat = buf.reshape(1, buf.shape[0] * LANES)
    out, at = [], 0
    for n in sizes:
        out.append(flat[:, at:at + n])
        at += n
    return out


def kernel(x, mem, positions, attn_norm_g, w_in, swa_q_norm_g, swa_k_norm_g, swa_sinks, mla_cq_norm_g, mla_ckv_norm_g, w_uq, w_ukv, mla_qn_norm_g, mla_qr_norm_g, mla_kn_norm_g, mla_kr_norm_g, mem_norm_g, w_mem_kv, mem_q_norm_g, mem_k_norm_g, w_out, ffn_norm_g, w_gate, w_up, w_down, loss_target, m_attn_norm_g, m_w_in, m_swa_q_norm_g, m_swa_k_norm_g, m_swa_sinks, m_mla_cq_norm_g, m_mla_ckv_norm_g, m_w_uq, m_w_ukv, m_mla_qn_norm_g, m_mla_qr_norm_g, m_mla_kn_norm_g, m_mla_kr_norm_g, m_mem_norm_g, m_w_mem_kv, m_mem_q_norm_g, m_mem_k_norm_g, m_w_out, m_ffn_norm_g, m_w_gate, m_w_up, m_w_down, v_attn_norm_g, v_w_in, v_swa_q_norm_g, v_swa_k_norm_g, v_swa_sinks, v_mla_cq_norm_g, v_mla_ckv_norm_g, v_w_uq, v_w_ukv, v_mla_qn_norm_g, v_mla_qr_norm_g, v_mla_kn_norm_g, v_mla_kr_norm_g, v_mem_norm_g, v_w_mem_kv, v_mem_q_norm_g, v_mem_k_norm_g, v_w_out, v_ffn_norm_g, v_w_gate, v_w_up, v_w_down):
    given = dict(locals())
    wts = {n: given[n] for n in WEIGHTS}
    mom_m = {n: given["m_" + n] for n in WEIGHTS}
    mom_v = {n: given["v_" + n] for n in WEIGHTS}

    mx, my, mc = lax.axis_index("x"), lax.axis_index("y"), lax.axis_index("c")
    meta = jnp.stack([2 * mx + my, mc]).astype(jnp.int32)
    x, mem, pos, target = x[0], mem[0], positions[0], loss_target[0]
    sp = {n: wts[n] for n in SMALL}
    s = x.shape[0]
    cos_t, sin_t = _rope_tables(pos)
    pos_f = pos.astype(F32)
    pos_col, pos_row = pos_f.reshape(s, 1), pos_f.reshape(1, s)
    g128 = _gain_table(sp)
    sinks = sp["swa_sinks"].reshape(SWA_Q_HEADS)
    gcq, gckv = sp["mla_cq_norm_g"], sp["mla_ckv_norm_g"]
    gs = {}

    slot = {n: _cast_into_slot(wts[n][0], meta, name="cast_" + n) for n in BIG if n not in ("w_gate", "w_up", "w_down")}
    first = [slot["w_in"], slot["w_uq"], slot["w_ukv"]]
    slot["w_gate"], [first] = _cast_into_slot(wts["w_gate"][0], meta, name="cast_w_gate",
                                              comm=[_gather_stage(first, "ici", (0, 4))])
    slot["w_up"], [first] = _cast_into_slot(wts["w_up"][0], meta, name="cast_w_up",
                                            comm=[_gather_stage(first, [("ici", (1, 4)), ("d2d", (0, 4))])])
    slot["w_down"], [first] = _cast_into_slot(wts["w_down"][0], meta, name="cast_w_down",
                                              comm=[_gather_stage(first, [("ici", (2, 4)), ("d2d", (1, 4))])])
    hn, [first] = _rms_fwd(x, sp["attn_norm_g"], name="attn_norm_fwd",
                           comm=[_gather_stage(first, [("ici", (3, 4)), ("d2d", (2, 4))])])
    [first] = _run_stages([_gather_stage(first, "d2d", (3, 4))], name="gather_first_last_d2d")
    eighths = lambda leg, ks: [(leg, (k, 8)) for k in ks]
    w_in_f, [wm, wg] = _full_w_in(first[0], comm=[_gather_stage([slot["w_mem_kv"]], "ici"),
                                                  _gather_stage([slot["w_gate"]], eighths("ici", (0,)))])
    w_uq_f, w_ukv_f = _full_heads(first[1], MLA_NOPE), _full_heads(first[2], MLA_NOPE)

    proj, [wm, wo, wg] = _matmul(hn, w_in_f, name="in_proj",
                                 comm=[_gather_stage(wm, "d2d"), _gather_stage([slot["w_out"]], "ici"),
                                       _gather_stage(wg, eighths("ici", (1, 2)))])
    (qa, ka, va, q_cat, k_cat, v_b, qm), [wo, wg] = _attn_prep_fwd(
        proj, g128, gcq, gckv, w_uq_f, w_ukv_f, cos_t, sin_t,
        comm=[_gather_stage(wo, "d2d"), _gather_stage(wg, eighths("ici", (3, 4)))])
    w_mem_kv_f = wm[0].reshape(D_MODEL, 2 * MEM_HEADS * MEM_DIM)
    w_out_f = wo[0].reshape(D_MODEL, D_MODEL)
    mn_b, kv_m, km, vm = _mem_kv_fwd(mem, sp["mem_norm_g"], w_mem_kv_f, sp["mem_k_norm_g"])
    (y_a, y), [wg, wu] = _swa_fwd(qa, ka, va, pos_col, pos_row, sinks,
                                  comm=[_gather_stage(wg, eighths("ici", (5, 6, 7))),
                                        _gather_stage([slot["w_up"]], eighths("ici", (0,)))])
    (y_b, lse, y), [wu] = _mla_fwd(q_cat, k_cat, v_b, y, comm=[_gather_stage(wu, eighths("ici", (1, 2, 3, 4, 5, 6, 7)))])
    y_m, y = _mem_attn_fwd(qm, km, vm, y)
    h1, [wu, wg] = _matmul(y, w_out_f, add=x, name="out_proj", comm=[_gather_stage(wu, "d2d"), _gather_stage(wg, "d2d")])
    fn = _rms_fwd(h1, sp["ffn_norm_g"], name="ffn_norm_fwd")
    w_gate_f, w_up_f = wg[0], wu[0]
    (gate, up, act), [wd] = _ffn_gate_up(fn, w_gate_f, w_up_f, comm=[_gather_stage([slot["w_down"]], "both")])
    w_down_f = wd[0].reshape(D_FF, D_MODEL)
    d_out, d_out_b, loss_tile = _matmul(act, w_down_f, add=h1, name="down_proj", tm=512, tk=D_FF, loss_target=target)

    add_pair = lambda n, g4, r: _add_pair(meta, g4, r, name="grad_add_pair_" + n)
    add_chips = lambda n, own, r: _add_chips(own, r, name="grad_add_chips_" + n)
    mine, theirs = {}, {}

    dw_down = _matmul(act, d_out_b, ta=True, name="dw_down", tm=512, tn=1024, tk=s)
    dw_down = dw_down.reshape(N_CHIPS, D_FF // N_CHIPS, D_MODEL)
    (d_gate, d_up), [[r]] = _ffn_bwd_act(d_out_b, w_down_f, gate, up, comm=[_halves_stage([dw_down])])
    own_d, oth_d = add_pair("w_down", dw_down, r)
    dw_gate, [rd] = _matmul(fn, d_gate, ta=True, name="dw_gate", tm=512, tk=s, tn=D_FF // N_CHIPS, out_split=N_CHIPS,
                            comm=[_chips_stage([oth_d], (0, 2))])
    dw_up, [[r], rd] = _matmul(fn, d_up, ta=True, name="dw_up", tm=512, tk=s, tn=D_FF // N_CHIPS, out_split=N_CHIPS,
                               comm=[_halves_stage([dw_gate]), _chips_stage([oth_d], (1, 2), into=rd)])
    mine["w_down"] = add_chips("w_down", own_d, rd[0])
    own_g, oth_g = add_pair("w_gate", dw_gate, r)
    d_fn, [[r], rg, [theirs["w_down"]]] = _matmul(
        d_gate, w_gate_f, tb=True, b_split=True, pair2=(d_up, w_up_f), name="dfn", tm=512, tn=512,
        comm=[_halves_stage([dw_up]), _chips_stage([oth_g]), _swap_stage([mine["w_down"]])])
    mine["w_gate"] = add_chips("w_gate", own_g, rg[0])
    own_u, oth_u = add_pair("w_up", dw_up, r)
    d_h1, d_h1_b, gs["ffn_norm_g"] = _rms_bwd(d_fn, h1, sp["ffn_norm_g"], d_out, name="ffn_norm_bwd")
    dw_out, [[theirs["w_gate"]]] = _matmul(y, d_h1_b, ta=True, name="dw_out", tm=512, tk=s,
                                           comm=[_swap_stage([mine["w_gate"]])])
    dw_out = dw_out.reshape(N_CHIPS, D_MODEL // N_CHIPS, D_MODEL)
    d_y, [[r]] = _matmul(d_h1_b, w_out_f, tb=True, name="dy", comm=[_halves_stage([dw_out])])
    own_o, oth_o = add_pair("w_out", dw_out, r)
    (d_qa, d_ka, d_va, d_sink), [ru] = _swa_bwd(qa, ka, va, pos_col, pos_row, sinks, y_a, d_y,
                                                comm=[_chips_stage([oth_u], (0, 2))])
    (d_qcat, d_kcat, d_vb), [ru, [r]] = _mla_bwd(
        q_cat, k_cat, v_b, y_b, lse, d_y, comm=[_chips_stage([oth_u], (1, 2), into=ru), _chips_stage([oth_o])])
    mine["w_up"] = add_chips("w_up", own_u, ru[0])
    mine["w_out"] = add_chips("w_out", own_o, r)
    d_qm, d_km, d_vm = _mem_attn_bwd(qm, km, vm, y_m, d_y)
    (d_proj, dw_uq, dw_ukv, dg128, gs["mla_cq_norm_g"], gs["mla_ckv_norm_g"]), [[theirs["w_up"], theirs["w_out"]]] = \
        _attn_prep_bwd(proj, g128, gcq, gckv, w_uq_f, w_ukv_f, cos_t, sin_t, d_qa, d_ka, d_va, d_qcat, d_kcat, d_vb,
                       d_qm, comm=[_swap_stage([mine["w_up"], mine["w_out"]])])
    dw_mem_kv, gs["mem_norm_g"], gs["mem_k_norm_g"] = _mem_kv_bwd(
        mem, sp["mem_norm_g"], w_mem_kv_f, sp["mem_k_norm_g"], mn_b, kv_m, d_km, d_vm)
    late = ("w_uq", "w_ukv", "w_mem_kv")
    late_g = [_shards_heads(dw_uq, MLA_NOPE, MLA_ROPE), _shards_heads(dw_ukv, MLA_NOPE, MLA_V),
              dw_mem_kv.reshape(N_CHIPS, D_MODEL // N_CHIPS, -1)]
    dw_in, [rs] = _matmul(hn, d_proj, ta=True, name="dw_in", tm=512, tk=s, comm=[_halves_stage(late_g)])
    late_sums = [add_pair(n, g4, r) for n, g4, r in zip(late, late_g, rs)]
    dw_in = _shards_w_in(dw_in)
    d_hn, [rs, [r]] = _matmul(d_proj, w_in_f, tb=True, name="dhn", tk=1536,
                              comm=[_chips_stage([oth for _, oth in late_sums]), _halves_stage([dw_in])])
    for n, (own, _), r_n in zip(late, late_sums, rs):
        mine[n] = add_chips(n, own, r_n)
    own_i, oth_i = add_pair("w_in", dw_in, r)
    (grad_x, _, gs["attn_norm_g"]), [[r], late_theirs] = _rms_bwd(
        d_hn, x, sp["attn_norm_g"], d_h1, name="attn_norm_bwd",
        comm=[_chips_stage([oth_i]), _swap_stage([mine[n] for n in late])])
    theirs.update(zip(late, late_theirs))
    mine["w_in"] = add_chips("w_in", own_i, r)
    [[theirs["w_in"]]] = _run_stages([_swap_stage([mine["w_in"]])], name="grad_swap_w_in")

    fold = lambda r: r[:, :64] + r[:, 64:]
    gs["swa_q_norm_g"] = fold(dg128[G_SWA_Q:G_SWA_Q + 1])
    gs["swa_k_norm_g"] = fold(dg128[G_SWA_K:G_SWA_K + 1])
    gs["mla_qn_norm_g"] = dg128[G_QN:G_QN + 1]
    gs["mla_qr_norm_g"] = fold(dg128[G_QR:G_QR + 1])
    gs["mla_kn_norm_g"] = dg128[G_KN:G_KN + 1]
    gs["mla_kr_norm_g"] = fold(dg128[G_KR:G_KR + 1])
    gs["mem_q_norm_g"] = dg128[G_MQ:G_MQ + 1]
    gs["swa_sinks"] = d_sink[:, :SWA_Q_HEADS]

    grad, delta, new_m, new_v = {}, {}, {}, {}
    for n in BIG:
        g2, d, m2, v2 = _adamw(meta, wts[n][0], mine[n], theirs[n], mom_m[n][0], mom_v[n][0], name="adamw_" + n)
        grad[n], delta[n], new_m[n], new_v[n] = g2[None], d[None], m2[None], v2[None]

    sizes = [wts[n].shape[1] for n in SMALL]
    zero = jnp.zeros((1, LANES), F32)
    packs = _small_step(_pack([gs[n] for n in SMALL] + [loss_tile]), _pack([wts[n] for n in SMALL] + [zero]),
                        _pack([mom_m[n] for n in SMALL] + [zero]), _pack([mom_v[n] for n in SMALL] + [zero]))
    for store, buf in zip((grad, delta, new_m, new_v), packs):
        for n, val in zip(SMALL, _unpack(buf, sizes)):
            store[n] = val
    loss = _unpack(packs[0], sizes + [LANES])[-1][0, 0]

    return (loss, grad_x[None], *[grad[n] for n in WEIGHTS], *[delta[n] for n in WEIGHTS],
            *[new_m[n] for n in WEIGHTS], *[new_v[n] for n in WEIGHTS])
```

```python
import functools
import math

import jax
import jax.numpy as jnp
from jax import lax
from jax.experimental import pallas as pl
from jax.experimental.pallas import tpu as pltpu

F32 = jnp.float32
BF16 = jnp.bfloat16

D_MODEL = 2048
BLOCK = 128
EPS = 1e-6
NEG_INF = -1e30
SWA_Q_HEADS = 16
SWA_KV_HEADS = 2
SWA_HEAD_DIM = 64
MLA_HEADS = 4
MLA_RANK = 512
MLA_NOPE = 128
MLA_ROPE = 64
MLA_V = 128
ROPE_THETA = 10000.0
MEM_HEADS = 4
MEM_DIM = 128
D_FF = 5632
IN_WIDTH = 2880
IN_PAD = 3072
N_CHIPS = 4

ADAM_LR = 0.001
ADAM_B1 = 0.9
ADAM_B2 = 0.999
ADAM_EPS = 1e-08
ADAM_WD = 0.01
ADAM_STEP = 10

VMEM_LIMIT_BYTES = 56 * 1024 * 1024
LANES = 128

MESH = pl.DeviceIdType.MESH


def _params(sem=None, **kw):
    return pltpu.CompilerParams(dimension_semantics=sem, vmem_limit_bytes=VMEM_LIMIT_BYTES, **kw)


def _tile(n, want):
    if n <= want:
        return n
    t = want - want % LANES
    while t > 0:
        if n % t == 0:
            return t
        t -= LANES
    return n


ANY = pl.BlockSpec(memory_space=pl.ANY)


class _Stage:
    def __init__(self, ins, out_shapes, aliases, n_sem, issue, wait, mid=None):
        self.ins, self.out_shapes, self.aliases, self.n_sem = list(ins), list(out_shapes), dict(aliases), n_sem
        self.issue, self.wait, self.mid = issue, wait, mid


def _pcall(body, args, *, name, grid, in_specs, out_specs, out_shape, scratch_shapes=(), sem=None, comm=(),
           prefetch=(), io_alias=None):
    multi = isinstance(out_shape, (list, tuple))
    out_specs_l = list(out_specs) if multi else [out_specs]
    out_shape_l = list(out_shape) if multi else [out_shape]
    npf = len(prefetch)
    own_aliases = {npf + a: o for a, o in (io_alias or {}).items()}

    def call(fn, in_specs_, out_specs_, out_shape_, scratch_, operands, sem_, aliases=None):
        kw = dict(name=name, out_shape=out_shape_, compiler_params=_params(sem_))
        if aliases:
            kw["input_output_aliases"] = aliases
        if npf:
            spec = pltpu.PrefetchScalarGridSpec(num_scalar_prefetch=npf, grid=grid, in_specs=in_specs_,
                                                out_specs=out_specs_, scratch_shapes=scratch_)
            return pl.pallas_call(fn, grid_spec=spec, **kw)(*prefetch, *operands)
        return pl.pallas_call(fn, grid=grid, in_specs=in_specs_, out_specs=out_specs_, scratch_shapes=scratch_,
                              **kw)(*operands)

    if not comm:
        return call(body, list(in_specs), out_specs, out_shape, list(scratch_shapes), args, sem, own_aliases)
    n_in, n_out, n_scr = len(in_specs), len(out_specs_l), len(scratch_shapes)
    cins = [a for st in comm for a in st.ins]
    couts = [s for st in comm for s in st.out_shapes]
    aliases, ci, co = dict(own_aliases), 0, 0
    for st in comm:
        for a_i, o_i in st.aliases.items():
            aliases[npf + n_in + ci + a_i] = n_out + co + o_i
        ci, co = ci + len(st.ins), co + len(st.out_shapes)

    def wrapped(*refs):
        pre = refs[:npf]
        p = npf
        ins = refs[p:p + n_in]; p += n_in
        cin_refs = refs[p:p + len(cins)]; p += len(cins)
        outs = refs[p:p + n_out]; p += n_out
        cout_refs = refs[p:p + len(couts)]; p += len(couts)
        scr = refs[p:p + n_scr]; p += n_scr
        sems = refs[p:]
        first = functools.reduce(jnp.logical_and, [pl.program_id(a) == 0 for a in range(len(grid))])
        last = functools.reduce(jnp.logical_and, [pl.program_id(a) == grid[a] - 1 for a in range(len(grid))])

        def each(what):
            i, o = 0, 0
            for k, st in enumerate(comm):
                fn = getattr(st, what)
                if fn is not None:
                    fn(cin_refs[i:i + len(st.ins)], cout_refs[o:o + len(st.out_shapes)], sems[2 * k], sems[2 * k + 1])
                i, o = i + len(st.ins), o + len(st.out_shapes)

        @pl.when(first)
        def _():
            each("issue")

        if any(st.mid is not None for st in comm):
            n_steps = math.prod(grid)
            assert n_steps >= 4, "a two-leg stage needs a carrier with several grid steps"
            lin = functools.reduce(lambda acc, a: acc * grid[a] + pl.program_id(a), range(len(grid)), 0)

            @pl.when(lin == (3 * n_steps) // 4)
            def _():
                each("mid")

        body(*pre, *ins, *outs, *scr)

        @pl.when(last)
        def _():
            each("wait")

    sem_scr = [pltpu.SemaphoreType.DMA((st.n_sem,)) for st in comm for _ in range(2)]
    res = call(wrapped, list(in_specs) + [ANY] * len(cins), out_specs_l + [ANY] * len(couts), out_shape_l + couts,
               list(scratch_shapes) + sem_scr, (*args, *cins), ("arbitrary",) * len(grid), aliases)
    normal = list(res[:n_out])
    stage_outs, o = [], n_out
    for st in comm:
        stage_outs.append(list(res[o:o + len(st.out_shapes)]))
        o += len(st.out_shapes)
    return (normal if multi else normal[0]), stage_outs


def _matmul(a, b, *, name, ta=False, tb=False, add=None, out_dtype=F32, tm=1024, tn=1024, tk=2048,
            b_split=False, out_split=0, comm=(), loss_target=None, pair2=None, norm_g=None):
    if ta:
        kdim, m = a.shape
    else:
        m, kdim = a.shape
    if b_split:
        assert tb
        nsp, n, kb = b.shape
        kb = kb * nsp
    elif tb:
        n, kb = b.shape
    else:
        kb, n = b.shape
    assert kb == kdim, (a.shape, b.shape, ta, tb)
    if b_split:
        tk = kdim
    if out_split:
        tn = _tile(n // out_split, tn)
    tm, tn, tk = _tile(m, tm), _tile(n, tn), _tile(kdim, tk)
    nk = kdim // tk
    dims = (((0 if ta else 1,), (1 if tb else 0,)), ((), ()))

    def product(a_ref, b_ref):
        if not b_split:
            return lax.dot_general(a_ref[...].astype(BF16), b_ref[...].astype(BF16), dims, preferred_element_type=F32)
        per = kdim // nsp
        return sum(lax.dot_general(a_ref[:, per * c:per * (c + 1)].astype(BF16), b_ref[c].astype(BF16), dims,
                                   preferred_element_type=F32) for c in range(nsp))

    def body(*refs):
        a_ref, b_ref = refs[:2]
        n_ab = 4 if pair2 is not None else 2
        add_ref = refs[n_ab] if add is not None else None
        n_in = n_ab + (add is not None) + (loss_target is not None) + (norm_g is not None)

        def products():
            r = product(a_ref, b_ref)
            return r if pair2 is None else r + product(refs[2], refs[3])
        o_ref = refs[n_in]

        def finish(r):
            if add_ref is not None:
                r = r + add_ref[...].astype(F32)
            if norm_g is not None:
                o_ref[...] = r
                scale = lax.rsqrt(jnp.mean(r * r, axis=-1, keepdims=True) + EPS)
                refs[n_in + 1][...] = (r * scale * refs[n_in - 1][...]).astype(BF16)
                return
            if loss_target is None:
                o_ref[...] = r.astype(o_ref.dtype)
                return
            db_ref, l_ref = refs[n_in + 1], refs[n_in + 2]
            err = r - refs[n_in - 1][...]
            d_out = err * (1.0 / n)
            o_ref[...] = d_out
            db_ref[...] = d_out.astype(BF16)
            part = jnp.broadcast_to((0.5 / n) * jnp.sum(jnp.sum(err * err, axis=-1, keepdims=True), axis=0, keepdims=True),
                                    (1, LANES))
            first = jnp.logical_and(pl.program_id(0) == 0, pl.program_id(1) == 0)

            @pl.when(first)
            def _():
                l_ref[...] = part

            @pl.when(jnp.logical_not(first))
            def _():
                l_ref[...] += part

        if nk == 1:
            finish(products())
            return
        acc_ref = refs[-1]
        k = pl.program_id(2)
        part = products()

        @pl.when(k == 0)
        def _():
            acc_ref[...] = part

        @pl.when(k > 0)
        def _():
            acc_ref[...] += part

        @pl.when(k == nk - 1)
        def _():
            finish(acc_ref[...])

    a_spec = pl.BlockSpec((tk, tm), lambda i, j, k: (k, i)) if ta else pl.BlockSpec((tm, tk), lambda i, j, k: (i, k))
    if b_split:
        b_spec = pl.BlockSpec((nsp, tn, kdim // nsp), lambda i, j, k: (0, j, 0))
    elif tb:
        b_spec = pl.BlockSpec((tn, tk), lambda i, j, k: (j, k))
    else:
        b_spec = pl.BlockSpec((tk, tn), lambda i, j, k: (k, j))
    in_specs = [a_spec, b_spec]
    args = [a, b]
    if pair2 is not None:
        assert pair2[0].shape == a.shape and pair2[1].shape == b.shape
        in_specs += [a_spec, b_spec]
        args += list(pair2)
    if add is not None:
        in_specs.append(pl.BlockSpec((tm, tn), lambda i, j, k: (i, j)))
        args.append(add)
    tile = pl.BlockSpec((tm, tn), lambda i, j, k: (i, j))
    sem = ("parallel", "parallel", "arbitrary")
    if out_split:
        per = (n // out_split) // tn
        out_spec = pl.BlockSpec((None, tm, tn), lambda i, j, k: (j // per, i, j % per))
        out_shape = jax.ShapeDtypeStruct((out_split, m, n // out_split), out_dtype)
    elif norm_g is not None:
        assert tn == n and nk == 1 and out_dtype == F32
        in_specs.append(pl.BlockSpec((1, n), lambda i, j, k: (0, 0)))
        args.append(norm_g)
        out_spec = [tile, tile]
        out_shape = [jax.ShapeDtypeStruct((m, n), F32), jax.ShapeDtypeStruct((m, n), BF16)]
    elif loss_target is not None:
        in_specs.append(tile)
        args.append(loss_target)
        out_spec = [tile, tile, pl.BlockSpec((1, LANES), lambda i, j, k: (0, 0))]
        out_shape = [jax.ShapeDtypeStruct((m, n), F32), jax.ShapeDtypeStruct((m, n), BF16),
                     jax.ShapeDtypeStruct((1, LANES), F32)]
        sem = ("arbitrary",) * 3
    else:
        out_spec = tile
        out_shape = jax.ShapeDtypeStruct((m, n), out_dtype)
    return _pcall(body, args, name=name, grid=(m // tm, n // tn, nk), in_specs=in_specs, out_specs=out_spec,
                  out_shape=out_shape, scratch_shapes=[pltpu.VMEM((tm, tn), F32)] if nk > 1 else [],
                  sem=sem, comm=comm)


def _rms_fwd(x, g, *, name, tm=512, comm=()):
    s, d = x.shape
    tm = _tile(s, tm)

    def body(x_ref, g_ref, o_ref):
        xv = x_ref[...]
        r = lax.rsqrt(jnp.mean(xv * xv, axis=-1, keepdims=True) + EPS)
        o_ref[...] = (xv * r * g_ref[...]).astype(o_ref.dtype)

    return _pcall(body, (x, g), name=name, grid=(s // tm,),
                  in_specs=[pl.BlockSpec((tm, d), lambda i: (i, 0)), pl.BlockSpec((1, d), lambda i: (0, 0))],
                  out_specs=pl.BlockSpec((tm, d), lambda i: (i, 0)),
                  out_shape=jax.ShapeDtypeStruct((s, d), BF16), sem=("parallel",), comm=comm)


def _rms_bwd(dy, x, g, res, *, name, tm=512, comm=()):
    s, d = x.shape
    tm = _tile(s, tm)

    def body(dy_ref, x_ref, g_ref, res_ref, dx_ref, dxb_ref, dg_ref):
        xv = x_ref[...]
        dyv = dy_ref[...]
        r = lax.rsqrt(jnp.mean(xv * xv, axis=-1, keepdims=True) + EPS)
        xhat = xv * r
        dyg = dyv * g_ref[...]
        mt = jnp.mean(dyg * xhat, axis=-1, keepdims=True)
        dx = res_ref[...] + r * (dyg - xhat * mt)
        dx_ref[...] = dx
        dxb_ref[...] = dx.astype(BF16)
        part = jnp.sum(dyv * xhat, axis=0, keepdims=True)

        @pl.when(pl.program_id(0) == 0)
        def _():
            dg_ref[...] = part

        @pl.when(pl.program_id(0) > 0)
        def _():
            dg_ref[...] += part

    row = pl.BlockSpec((tm, d), lambda i: (i, 0))
    vec = pl.BlockSpec((1, d), lambda i: (0, 0))
    return _pcall(body, (dy, x, g, res), name=name, grid=(s // tm,), in_specs=[row, row, vec, row],
                  out_specs=[row, row, vec],
                  out_shape=[jax.ShapeDtypeStruct((s, d), F32), jax.ShapeDtypeStruct((s, d), BF16),
                             jax.ShapeDtypeStruct((1, d), F32)],
                  sem=("arbitrary",), comm=comm)


def _lane(shape):
    return lax.broadcasted_iota(jnp.int32, shape, 1)


def _halfsum(t, lo):
    s_lo = jnp.sum(jnp.where(lo, t, 0.0), axis=-1, keepdims=True)
    s_hi = jnp.sum(jnp.where(lo, 0.0, t), axis=-1, keepdims=True)
    return jnp.where(lo, s_lo, s_hi)


def _norm_pair(x, g, lo):
    r = lax.rsqrt(_halfsum(x * x, lo) * (1.0 / 64.0) + EPS)
    xhat = x * r
    return xhat * g, xhat, r


def _norm_pair_bwd(dy, g, xhat, r, lo):
    dyg = dy * g
    mt = _halfsum(dyg * xhat, lo) * (1.0 / 64.0)
    return r * (dyg - xhat * mt), jnp.sum(dy * xhat, axis=0, keepdims=True)


def _norm_full(x, g):
    r = lax.rsqrt(jnp.mean(x * x, axis=-1, keepdims=True) + EPS)
    xhat = x * r
    return xhat * g, xhat, r


def _norm_full_bwd(dy, g, xhat, r):
    dyg = dy * g
    mt = jnp.mean(dyg * xhat, axis=-1, keepdims=True)
    return r * (dyg - xhat * mt), jnp.sum(dy * xhat, axis=0, keepdims=True)


def _rot(x, first32):
    return jnp.where(first32, pltpu.roll(x, 96, axis=1), pltpu.roll(x, 32, axis=1))


def _rope(x, cos_t, sin_t, first32):
    return x * cos_t + _rot(x, first32) * sin_t


def _rope_bwd(dy, cos_t, sin_t, first32):
    return dy * cos_t + _rot(dy * sin_t, first32)


G_SWA_Q, G_SWA_K, G_QN, G_QR, G_KN, G_KR, G_MQ = range(7)

C_QA, C_KA, C_VA, C_CQ, C_CKV, C_QM, C_KR = 0, 1024, 1152, 1280, 1792, 2304, 2816


def _prep_common(p_ref, g128_ref, gcq_ref, gckv_ref, wuq_ref, wukv_ref, cos_ref, sin_ref):
    tm = p_ref.shape[0]
    lane = _lane((tm, LANES))
    lo = lane < 64
    first32 = (lane % 64) < 32
    cos_t = cos_ref[...]
    sin_t = sin_ref[...]
    g = lambda row: g128_ref[row:row + 1, :]
    out = dict(lo=lo, first32=first32, cos_t=cos_t, sin_t=sin_t, lane=lane)
    cq_n, cq_hat, cq_r = _norm_full(p_ref[:, C_CQ:C_CQ + MLA_RANK], gcq_ref[...])
    ckv_n, ckv_hat, ckv_r = _norm_full(p_ref[:, C_CKV:C_CKV + MLA_RANK], gckv_ref[...])
    cq_b = cq_n.astype(BF16)
    ckv_b = ckv_n.astype(BF16)
    q_b = jnp.dot(cq_b, wuq_ref[...], preferred_element_type=F32)
    kv_b = jnp.dot(ckv_b, wukv_ref[...], preferred_element_type=F32)
    out.update(cq_b=cq_b, cq_hat=cq_hat, cq_r=cq_r, ckv_b=ckv_b, ckv_hat=ckv_hat, ckv_r=ckv_r, q_b=q_b, kv_b=kv_b, g=g)
    return out


def _attn_prep_fwd(proj, g128, gcq, gckv, wuq, wukv, cos_t, sin_t, *, tm=512, comm=()):
    s = proj.shape[0]
    tm = _tile(s, tm)

    def body(p_ref, g128_ref, gcq_ref, gckv_ref, wuq_ref, wukv_ref, cos_ref, sin_ref,
             qa_ref, ka_ref, va_ref, qcat_ref, kcat_ref, vb_ref, qm_ref):
        c = _prep_common(p_ref, g128_ref, gcq_ref, gckv_ref, wuq_ref, wukv_ref, cos_ref, sin_ref)
        lo, first32, g = c["lo"], c["first32"], c["g"]
        for j in range(SWA_Q_HEADS // 2):
            y, _, _ = _norm_pair(p_ref[:, C_QA + 128 * j:C_QA + 128 * (j + 1)], g(G_SWA_Q), lo)
            qa_ref[:, 128 * j:128 * (j + 1)] = y.astype(BF16)
        y, _, _ = _norm_pair(p_ref[:, C_KA:C_KA + 128], g(G_SWA_K), lo)
        ka_ref[...] = y.astype(BF16)
        va_ref[...] = p_ref[:, C_VA:C_VA + 128].astype(BF16)
        kr, _, _ = _norm_pair(p_ref[:, C_KR:C_KR + 128], g(G_KR), lo)
        kr = jnp.where(lo, _rope(kr, c["cos_t"], c["sin_t"], first32), 0.0)
        krkr = (kr + pltpu.roll(kr, 64, axis=1)).astype(BF16)
        q_b, kv_b = c["q_b"], c["kv_b"]
        qr = []
        for j in range(MLA_HEADS // 2):
            y, _, _ = _norm_pair(q_b[:, 512 + 128 * j:512 + 128 * (j + 1)], g(G_QR), lo)
            qr.append(_rope(y, c["cos_t"], c["sin_t"], first32))
        for h in range(MLA_HEADS):
            qn, _, _ = _norm_full(q_b[:, 128 * h:128 * (h + 1)], g(G_QN))
            keep = lo if h % 2 == 0 else jnp.logical_not(lo)
            qcat_ref[h, :, 0:128] = qn.astype(BF16)
            qcat_ref[h, :, 128:256] = jnp.where(keep, qr[h // 2], 0.0).astype(BF16)
            kn, _, _ = _norm_full(kv_b[:, 128 * h:128 * (h + 1)], g(G_KN))
            kcat_ref[h, :, 0:128] = kn.astype(BF16)
            kcat_ref[h, :, 128:256] = krkr
        vb_ref[...] = kv_b[:, 512:1024].astype(BF16)
        for h in range(MEM_HEADS):
            y, _, _ = _norm_full(p_ref[:, C_QM + 128 * h:C_QM + 128 * (h + 1)], g(G_MQ))
            qm_ref[:, 128 * h:128 * (h + 1)] = y.astype(BF16)

    row = lambda w: pl.BlockSpec((tm, w), lambda i: (i, 0))
    full = lambda shape: pl.BlockSpec(shape, lambda i: tuple(0 for _ in shape))
    cat = pl.BlockSpec((MLA_HEADS, tm, 256), lambda i: (0, i, 0))
    return _pcall(
        body, (proj, g128, gcq, gckv, wuq, wukv, cos_t, sin_t), name="attn_prep_fwd", grid=(s // tm,),
        in_specs=[row(IN_PAD), full((8, 128)), full((1, 512)), full((1, 512)), full((512, 768)), full((512, 1024)),
                  row(128), row(128)],
        out_specs=[row(1024), row(128), row(128), cat, cat, row(512), row(512)],
        out_shape=[jax.ShapeDtypeStruct((s, 1024), BF16), jax.ShapeDtypeStruct((s, 128), BF16),
                   jax.ShapeDtypeStruct((s, 128), BF16), jax.ShapeDtypeStruct((MLA_HEADS, s, 256), BF16),
                   jax.ShapeDtypeStruct((MLA_HEADS, s, 256), BF16), jax.ShapeDtypeStruct((s, 512), BF16),
                   jax.ShapeDtypeStruct((s, 512), BF16)],
        sem=("parallel",), comm=comm)


def _attn_prep_bwd(proj, g128, gcq, gckv, wuq, wukv, cos_t, sin_t,
                   d_qa, d_ka, d_va, d_qcat, d_kcat, d_vb, d_qm, *, tm=512, comm=()):
    s = proj.shape[0]
    tm = _tile(s, tm)

    def body(p_ref, g128_ref, gcq_ref, gckv_ref, wuq_ref, wukv_ref, cos_ref, sin_ref,
             dqa_ref, dka_ref, dva_ref, dqcat_ref, dkcat_ref, dvb_ref, dqm_ref,
             dp_ref, dwuq_ref, dwukv_ref, dg128_ref, dgcq_ref, dgckv_ref):
        c = _prep_common(p_ref, g128_ref, gcq_ref, gckv_ref, wuq_ref, wukv_ref, cos_ref, sin_ref)
        lo, first32, g = c["lo"], c["first32"], c["g"]
        cos_v, sin_v = c["cos_t"], c["sin_t"]
        q_b, kv_b = c["q_b"], c["kv_b"]
        zero_row = jnp.zeros((1, LANES), F32)
        dg = {k: zero_row for k in range(7)}

        for j in range(SWA_Q_HEADS // 2):
            sl = slice(C_QA + 128 * j, C_QA + 128 * (j + 1))
            _, xhat, r = _norm_pair(p_ref[:, sl], g(G_SWA_Q), lo)
            dx, dgj = _norm_pair_bwd(dqa_ref[:, 128 * j:128 * (j + 1)], g(G_SWA_Q), xhat, r, lo)
            dp_ref[:, sl] = dx.astype(BF16)
            dg[G_SWA_Q] = dg[G_SWA_Q] + dgj
        _, xhat, r = _norm_pair(p_ref[:, C_KA:C_KA + 128], g(G_SWA_K), lo)
        dx, dgj = _norm_pair_bwd(dka_ref[...], g(G_SWA_K), xhat, r, lo)
        dp_ref[:, C_KA:C_KA + 128] = dx.astype(BF16)
        dg[G_SWA_K] = dgj
        dp_ref[:, C_VA:C_VA + 128] = dva_ref[...].astype(BF16)

        dqb_parts = [None] * 6
        for h in range(MLA_HEADS):
            _, xhat, r = _norm_full(q_b[:, 128 * h:128 * (h + 1)], g(G_QN))
            dx, dgj = _norm_full_bwd(dqcat_ref[h, :, 0:128], g(G_QN), xhat, r)
            dqb_parts[h] = dx
            dg[G_QN] = dg[G_QN] + dgj
        for j in range(MLA_HEADS // 2):
            _, xhat, r = _norm_pair(q_b[:, 512 + 128 * j:512 + 128 * (j + 1)], g(G_QR), lo)
            d_rot = jnp.where(lo, dqcat_ref[2 * j, :, 128:256], dqcat_ref[2 * j + 1, :, 128:256])
            d_y = _rope_bwd(d_rot, cos_v, sin_v, first32)
            dx, dgj = _norm_pair_bwd(d_y, g(G_QR), xhat, r, lo)
            dqb_parts[4 + j] = dx
            dg[G_QR] = dg[G_QR] + dgj
        d_qb = jnp.concatenate(dqb_parts, axis=1).astype(BF16)
        dwuq = lax.dot_general(c["cq_b"], d_qb, (((0,), (0,)), ((), ())), preferred_element_type=F32)
        d_cqn = lax.dot_general(d_qb, wuq_ref[...], (((1,), (1,)), ((), ())), preferred_element_type=F32)
        dx, dgcq = _norm_full_bwd(d_cqn, gcq_ref[...], c["cq_hat"], c["cq_r"])
        dp_ref[:, C_CQ:C_CQ + MLA_RANK] = dx.astype(BF16)

        dkv_parts = []
        d_krkr = jnp.zeros((p_ref.shape[0], LANES), F32)
        for h in range(MLA_HEADS):
            _, xhat, r = _norm_full(kv_b[:, 128 * h:128 * (h + 1)], g(G_KN))
            dx, dgj = _norm_full_bwd(dkcat_ref[h, :, 0:128], g(G_KN), xhat, r)
            dkv_parts.append(dx)
            dg[G_KN] = dg[G_KN] + dgj
            d_krkr = d_krkr + dkcat_ref[h, :, 128:256]
        d_kvb = jnp.concatenate(dkv_parts + [dvb_ref[...]], axis=1).astype(BF16)
        dwukv = lax.dot_general(c["ckv_b"], d_kvb, (((0,), (0,)), ((), ())), preferred_element_type=F32)
        d_ckvn = lax.dot_general(d_kvb, wukv_ref[...], (((1,), (1,)), ((), ())), preferred_element_type=F32)
        dx, dgckv = _norm_full_bwd(d_ckvn, gckv_ref[...], c["ckv_hat"], c["ckv_r"])
        dp_ref[:, C_CKV:C_CKV + MLA_RANK] = dx.astype(BF16)

        _, xhat, r = _norm_pair(p_ref[:, C_KR:C_KR + 128], g(G_KR), lo)
        d_kr = jnp.where(lo, d_krkr + pltpu.roll(d_krkr, 64, axis=1), 0.0)
        d_y = jnp.where(lo, _rope_bwd(d_kr, cos_v, sin_v, first32), 0.0)
        dx, dgj = _norm_pair_bwd(d_y, g(G_KR), xhat, r, lo)
        dp_ref[:, C_KR:C_KR + 128] = jnp.where(lo, dx, 0.0).astype(BF16)
        dp_ref[:, C_KR + 128:] = jnp.zeros((p_ref.shape[0], IN_PAD - C_KR - 128), BF16)
        dg[G_KR] = dgj

        for h in range(MEM_HEADS):
            sl = slice(C_QM + 128 * h, C_QM + 128 * (h + 1))
            _, xhat, r = _norm_full(p_ref[:, sl], g(G_MQ))
            dx, dgj = _norm_full_bwd(dqm_ref[:, 128 * h:128 * (h + 1)], g(G_MQ), xhat, r)
            dp_ref[:, sl] = dx.astype(BF16)
            dg[G_MQ] = dg[G_MQ] + dgj

        dg_tile = jnp.concatenate([dg[k] for k in range(7)] + [zero_row], axis=0)

        @pl.when(pl.program_id(0) == 0)
        def _():
            dwuq_ref[...] = dwuq
            dwukv_ref[...] = dwukv
            dg128_ref[...] = dg_tile
            dgcq_ref[...] = dgcq
            dgckv_ref[...] = dgckv

        @pl.when(pl.program_id(0) > 0)
        def _():
            dwuq_ref[...] += dwuq
            dwukv_ref[...] += dwukv
            dg128_ref[...] += dg_tile
            dgcq_ref[...] += dgcq
            dgckv_ref[...] += dgckv

    row = lambda w: pl.BlockSpec((tm, w), lambda i: (i, 0))
    full = lambda shape: pl.BlockSpec(shape, lambda i: tuple(0 for _ in shape))
    cat = pl.BlockSpec((MLA_HEADS, tm, 256), lambda i: (0, i, 0))
    return _pcall(
        body, (proj, g128, gcq, gckv, wuq, wukv, cos_t, sin_t, d_qa, d_ka, d_va, d_qcat, d_kcat, d_vb, d_qm),
        name="attn_prep_bwd", grid=(s // tm,),
        in_specs=[row(IN_PAD), full((8, 128)), full((1, 512)), full((1, 512)), full((512, 768)), full((512, 1024)),
                  row(128), row(128),
                  row(1024), row(128), row(128), cat, cat, row(512), row(512)],
        out_specs=[row(IN_PAD), full((512, 768)), full((512, 1024)), full((8, 128)), full((1, 512)), full((1, 512))],
        out_shape=[jax.ShapeDtypeStruct((s, IN_PAD), BF16), jax.ShapeDtypeStruct((512, 768), F32),
                   jax.ShapeDtypeStruct((512, 1024), F32), jax.ShapeDtypeStruct((8, 128), F32),
                   jax.ShapeDtypeStruct((1, 512), F32), jax.ShapeDtypeStruct((1, 512), F32)],
        sem=("arbitrary",), comm=comm)


SWA_SLOPES = tuple(2.0 ** (-8.0 * h / SWA_Q_HEADS) for h in range(1, SWA_Q_HEADS + 1))
SWA_SCALE = SWA_HEAD_DIM ** -0.5
NT_DIMS = (((1,), (1,)), ((), ()))
TN_DIMS = (((0,), (0,)), ((), ()))


def _swa_span(n, kp_ref, kc_ref, vp_ref, vc_ref, pcol_ref, pprow_ref, pcrow_ref):
    k_span = jnp.concatenate([kp_ref[...], kc_ref[...]], axis=0).astype(F32)
    v_span = jnp.concatenate([vp_ref[...], vc_ref[...]], axis=0).astype(F32)
    lo = _lane((2 * BLOCK, LANES)) < 64
    k_sw = pltpu.roll(k_span, 64, axis=1)
    v_sw = pltpu.roll(v_span, 64, axis=1)
    kk = (jnp.where(lo, k_span, k_sw).astype(BF16), jnp.where(lo, k_sw, k_span).astype(BF16))
    vv_lo = (jnp.where(lo, v_span, 0.0).astype(BF16), jnp.where(lo, v_sw, 0.0).astype(BF16))
    vv_hi = (jnp.where(lo, 0.0, v_sw).astype(BF16), jnp.where(lo, 0.0, v_span).astype(BF16))
    pk = jnp.concatenate([pprow_ref[...], pcrow_ref[...]], axis=1)
    dist = jnp.abs(pcol_ref[...] - pk)
    qi = lax.broadcasted_iota(jnp.int32, (BLOCK, 2 * BLOCK), 0)
    ki = lax.broadcasted_iota(jnp.int32, (BLOCK, 2 * BLOCK), 1)
    first_key = jnp.where(n > 0, qi + 1, jnp.maximum(qi + 1, BLOCK))
    valid = jnp.logical_and(ki >= first_key, ki <= qi + BLOCK)
    mask_add = jnp.where(valid, 0.0, NEG_INF)
    return kk, vv_lo, vv_hi, dist, mask_add


def _swa_heads(q_ref, lo):
    heads = []
    for j in range(SWA_Q_HEADS // 2):
        q_pair = q_ref[:, 128 * j:128 * (j + 1)].astype(F32)
        for par in (0, 1):
            q_h = jnp.where(lo if par == 0 else jnp.logical_not(lo), q_pair, 0.0).astype(BF16)
            heads.append((2 * j + par, (2 * j) // (SWA_Q_HEADS // SWA_KV_HEADS), par, q_h))
    return heads


def _swa_probs(raw, dist, mask_add, slope, sink):
    s = raw * SWA_SCALE - slope * dist + mask_add
    m = jnp.maximum(jnp.max(s, axis=-1, keepdims=True), sink)
    e = jnp.exp(s - m)
    e_sink = jnp.exp(sink - m)
    inv = 1.0 / (jnp.sum(e, axis=-1, keepdims=True) + e_sink)
    return e * inv, e_sink * inv


def _swa_specs():
    blk = lambda w: pl.BlockSpec((BLOCK, w), lambda n: (n, 0))
    prev = lambda w: pl.BlockSpec((BLOCK, w), lambda n: (jnp.maximum(n - 1, 0), 0))
    prow_c = pl.BlockSpec((1, BLOCK), lambda n: (0, n))
    prow_p = pl.BlockSpec((1, BLOCK), lambda n: (0, jnp.maximum(n - 1, 0)))
    smem = pl.BlockSpec(memory_space=pltpu.SMEM)
    return [blk(1024), prev(128), blk(128), prev(128), blk(128), blk(1), prow_p, prow_c, smem], blk


def _swa_fwd(qa, ka, va, pos_col, pos_row, sinks, *, comm=()):
    s = qa.shape[0]
    in_specs, blk = _swa_specs()

    def body(q_ref, kp_ref, kc_ref, vp_ref, vc_ref, pcol_ref, pprow_ref, pcrow_ref, sink_ref, o_ref, yb_ref):
        n = pl.program_id(0)
        kk, vv_lo, vv_hi, dist, mask_add = _swa_span(n, kp_ref, kc_ref, vp_ref, vc_ref, pcol_ref, pprow_ref, pcrow_ref)
        lo = _lane((BLOCK, LANES)) < 64
        heads = _swa_heads(q_ref, lo)
        raws = [lax.dot_general(q_h, kk[kv], NT_DIMS, preferred_element_type=F32) for _, kv, _, q_h in heads]
        probs = [_swa_probs(raw, dist, mask_add, SWA_SLOPES[h], sink_ref[h])[0].astype(BF16)
                 for raw, (h, _, _, _) in zip(raws, heads)]
        for j in range(SWA_Q_HEADS // 2):
            kv = heads[2 * j][1]
            out = (jnp.dot(probs[2 * j], vv_lo[kv], preferred_element_type=F32)
                   + jnp.dot(probs[2 * j + 1], vv_hi[kv], preferred_element_type=F32))
            o_ref[:, 128 * j:128 * (j + 1)] = out
            yb_ref[:, 128 * j:128 * (j + 1)] = out.astype(BF16)

    return _pcall(body, (qa, ka, ka, va, va, pos_col, pos_row, pos_row, sinks), name="swa_fwd", grid=(s // BLOCK,),
                  in_specs=in_specs, out_specs=[blk(1024), blk(1024)],
                  out_shape=[jax.ShapeDtypeStruct((s, 1024), F32), jax.ShapeDtypeStruct((s, D_MODEL), BF16)],
                  sem=("parallel",), comm=comm)


def _swa_bwd(qa, ka, va, pos_col, pos_row, sinks, y_a, d_y, *, comm=()):
    s = qa.shape[0]
    in_specs, blk = _swa_specs()
    whole = pl.BlockSpec((s, 128), lambda n: (0, 0))

    def body(q_ref, kp_ref, kc_ref, vp_ref, vc_ref, pcol_ref, pprow_ref, pcrow_ref, sink_ref, y_ref, dy_ref,
             dq_ref, dk_ref, dv_ref, dsink_ref):
        n = pl.program_id(0)

        @pl.when(n == 0)
        def _():
            dk_ref[...] = jnp.zeros_like(dk_ref)
            dv_ref[...] = jnp.zeros_like(dv_ref)
            dsink_ref[...] = jnp.zeros_like(dsink_ref)

        kk, vv_lo, vv_hi, dist, mask_add = _swa_span(n, kp_ref, kc_ref, vp_ref, vc_ref, pcol_ref, pprow_ref, pcrow_ref)
        lo = _lane((BLOCK, LANES)) < 64
        lo2 = _lane((2 * BLOCK, LANES)) < 64
        lane1 = _lane((1, LANES))
        dsink = jnp.zeros((1, LANES), F32)
        dkk = [jnp.zeros((2 * BLOCK, LANES), F32) for _ in range(SWA_KV_HEADS)]
        dvv = [jnp.zeros((2 * BLOCK, LANES), F32) for _ in range(SWA_KV_HEADS)]
        heads = _swa_heads(q_ref, lo)
        do_b, deltas = [], []
        for j in range(SWA_Q_HEADS // 2):
            do_pair = dy_ref[:, 128 * j:128 * (j + 1)]
            doy = do_pair * y_ref[:, 128 * j:128 * (j + 1)]
            do_b.append(do_pair.astype(BF16))
            deltas.append(jnp.sum(jnp.where(lo, doy, 0.0), axis=-1, keepdims=True))
            deltas.append(jnp.sum(jnp.where(lo, 0.0, doy), axis=-1, keepdims=True))
        raws = [lax.dot_general(q_h, kk[kv], NT_DIMS, preferred_element_type=F32) for _, kv, _, q_h in heads]
        dps = [lax.dot_general(do_b[h // 2], (vv_lo, vv_hi)[par][kv], NT_DIMS, preferred_element_type=F32)
               for h, kv, par, _ in heads]
        p_b, ds_b = [], []
        for h, kv, par, _ in heads:
            p, p_sink = _swa_probs(raws[h], dist, mask_add, SWA_SLOPES[h], sink_ref[h])
            ds = p * (dps[h] - deltas[h])
            dsink = dsink + jnp.where(lane1 == h, -jnp.sum(p_sink * deltas[h], axis=0, keepdims=True), 0.0)
            p_b.append(p.astype(BF16))
            ds_b.append((ds * SWA_SCALE).astype(BF16))
        dq_halves = []
        for h, kv, par, q_h in heads:
            dq_halves.append(jnp.dot(ds_b[h], kk[kv], preferred_element_type=F32))
            dkk[kv] = dkk[kv] + lax.dot_general(ds_b[h], q_h, TN_DIMS, preferred_element_type=F32)
            pv = lax.dot_general(p_b[h], do_b[h // 2], TN_DIMS, preferred_element_type=F32)
            dvv[kv] = dvv[kv] + jnp.where(lo2 if par == 0 else jnp.logical_not(lo2), pv, 0.0)
        for j in range(SWA_Q_HEADS // 2):
            dq_ref[:, 128 * j:128 * (j + 1)] = jnp.where(lo, dq_halves[2 * j], dq_halves[2 * j + 1])
        fold = lambda t: t + pltpu.roll(t, 64, axis=1)
        dk_span = jnp.where(lo2, fold(dkk[0]), fold(dkk[1]))
        dv_span = jnp.where(lo2, fold(dvv[0]), fold(dvv[1]))
        prev0 = pl.multiple_of(jnp.maximum(n - 1, 0) * BLOCK, BLOCK)
        cur0 = pl.multiple_of(n * BLOCK, BLOCK)
        dk_ref[pl.ds(prev0, BLOCK), :] += dk_span[0:BLOCK]
        dk_ref[pl.ds(cur0, BLOCK), :] += dk_span[BLOCK:]
        dv_ref[pl.ds(prev0, BLOCK), :] += dv_span[0:BLOCK]
        dv_ref[pl.ds(cur0, BLOCK), :] += dv_span[BLOCK:]
        dsink_ref[...] += dsink

    return _pcall(
        body, (qa, ka, ka, va, va, pos_col, pos_row, pos_row, sinks, y_a, d_y), name="swa_bwd", grid=(s // BLOCK,),
        in_specs=in_specs + [blk(1024), blk(1024)],
        out_specs=[blk(1024), whole, whole, pl.BlockSpec((1, LANES), lambda n: (0, 0))],
        out_shape=[jax.ShapeDtypeStruct((s, 1024), F32), jax.ShapeDtypeStruct((s, 128), F32),
                   jax.ShapeDtypeStruct((s, 128), F32), jax.ShapeDtypeStruct((1, LANES), F32)],
        sem=("arbitrary",), comm=comm)


MLA_SCALE = (MLA_NOPE + MLA_ROPE) ** -0.5
LOG2_E = math.log2(math.e)
MLA_TILE = 1024
MLA_ROW_GROUP = 256


def _tile_pairs(nt, q_major):
    pairs = [(i, j) for i in range(nt) for j in range(i + 1)] if q_major else \
            [(i, j) for j in range(nt) for i in range(j, nt)]
    return jnp.asarray([p[0] for p in pairs], jnp.int32), jnp.asarray([p[1] for p in pairs], jnp.int32)


def _diag_mask(t):
    return lax.broadcasted_iota(jnp.int32, (t, t), 1) <= lax.broadcasted_iota(jnp.int32, (t, t), 0)


def _mla_fwd(q_cat, k_cat, v_b, y_all, *, comm=()):
    nh, s, _ = q_cat.shape
    t = _tile(s, MLA_TILE)
    qi, kj = _tile_pairs(s // t, True)
    ycol = (SWA_Q_HEADS * SWA_HEAD_DIM) // (nh * MLA_V)

    def body(qi_ref, kj_ref, q_ref, k_ref, v_ref, _, o_ref, lse_ref, yb_ref, m_sc, l_sc, acc_sc):
        i, j = qi_ref[pl.program_id(0)], kj_ref[pl.program_id(0)]

        @pl.when(j == 0)
        def _():
            m_sc[...] = jnp.full_like(m_sc, NEG_INF)
            l_sc[...] = jnp.zeros_like(l_sc)
            acc_sc[...] = jnp.zeros_like(acc_sc)

        def update(diagonal):
            rg = min(MLA_ROW_GROUP, t)
            units = [(h, slice(r0, r0 + rg), r0, (r0 + rg) if diagonal else t) for h in range(nh) for r0 in range(0, t, rg)]
            scores = [lax.dot_general(q_ref[h, rows, :], k_ref[h, 0:nk, :], NT_DIMS, preferred_element_type=F32)
                      for h, rows, _, nk in units]
            probs, alphas = [], []
            for (h, rows, r0, nk), raw in zip(units, scores):
                if diagonal:
                    row = r0 + lax.broadcasted_iota(jnp.int32, (rg, nk), 0)
                    raw = jnp.where(lax.broadcasted_iota(jnp.int32, (rg, nk), 1) <= row, raw, NEG_INF)
                m_old = m_sc[h, rows]
                m_new = jnp.maximum(m_old, jnp.max(raw, axis=-1, keepdims=True))
                alpha = jnp.exp2((m_old - m_new) * (MLA_SCALE * LOG2_E))
                p = jnp.exp2((raw - m_new) * (MLA_SCALE * LOG2_E))
                l_sc[h, rows] = alpha * l_sc[h, rows] + jnp.sum(p, axis=-1, keepdims=True)
                m_sc[h, rows] = m_new
                probs.append(p.astype(BF16))
                alphas.append(alpha)
            for (h, rows, _, nk), p, alpha in zip(units, probs, alphas):
                acc_sc[h, rows] = alpha * acc_sc[h, rows] + jnp.dot(p, v_ref[0:nk, MLA_V * h:MLA_V * (h + 1)],
                                                                    preferred_element_type=F32)

        @pl.when(j < i)
        def _():
            update(False)

        @pl.when(j == i)
        def _():
            update(True)
            for h in range(nh):
                out = acc_sc[h] * (1.0 / l_sc[h])
                o_ref[:, MLA_V * h:MLA_V * (h + 1)] = out
                yb_ref[:, MLA_V * h:MLA_V * (h + 1)] = out.astype(BF16)
                lse_ref[h] = m_sc[h] * MLA_SCALE + jnp.log(l_sc[h])

    return _pcall(
        body, (q_cat, k_cat, v_b, y_all), name="mla_fwd", grid=(qi.shape[0],), prefetch=(qi, kj),
        in_specs=[pl.BlockSpec((nh, t, 256), lambda p, qi, kj: (0, qi[p], 0)),
                  pl.BlockSpec((nh, t, 256), lambda p, qi, kj: (0, kj[p], 0)),
                  pl.BlockSpec((t, nh * MLA_V), lambda p, qi, kj: (kj[p], 0)), ANY],
        out_specs=[pl.BlockSpec((t, nh * MLA_V), lambda p, qi, kj: (qi[p], 0)),
                   pl.BlockSpec((nh, t, 1), lambda p, qi, kj: (0, qi[p], 0)),
                   pl.BlockSpec((t, nh * MLA_V), lambda p, qi, kj: (qi[p], ycol))],
        out_shape=[jax.ShapeDtypeStruct((s, nh * MLA_V), F32), jax.ShapeDtypeStruct((nh, s, 1), F32),
                   jax.ShapeDtypeStruct(y_all.shape, y_all.dtype)],
        scratch_shapes=[pltpu.VMEM((nh, t, 1), F32), pltpu.VMEM((nh, t, 1), F32), pltpu.VMEM((nh, t, MLA_V), F32)],
        sem=("arbitrary",), comm=comm, io_alias={3: 2})


def _mla_bwd(q_cat, k_cat, v_b, y_b, lse, d_y, *, comm=()):
    nh, s, _ = q_cat.shape
    t = _tile(s, MLA_TILE)
    nt = s // t
    hp = 2
    wv = hp * MLA_V
    col0 = (SWA_Q_HEADS * SWA_HEAD_DIM) // wv
    qi, kj = _tile_pairs(nt, False)

    def body(qi_ref, kj_ref, q_ref, k_ref, v_ref, y_ref, lse_ref, dy_ref, dq_ref, dk_ref, dv_ref, dk_sc, dv_sc):
        step = pl.program_id(1)
        i, j = qi_ref[step], kj_ref[step]

        @pl.when(step == 0)
        def _():
            dq_ref[...] = jnp.zeros_like(dq_ref)

        @pl.when(i == j)
        def _():
            dk_sc[...] = jnp.zeros_like(dk_sc)
            dv_sc[...] = jnp.zeros_like(dv_sc)

        def update(diagonal):
            rg = min(MLA_ROW_GROUP, t) if diagonal else t
            units = [(h, r0, (r0 + rg) if diagonal else t) for h in range(hp) for r0 in range(0, t, rg)]
            cols = [slice(MLA_V * h, MLA_V * (h + 1)) for h in range(hp)]
            do_b = [dy_ref[r0:r0 + rg, cols[h]].astype(BF16) for h, r0, _ in units]
            scores = [lax.dot_general(q_ref[h, r0:r0 + rg, :], k_ref[h, 0:nk, :], NT_DIMS, preferred_element_type=F32)
                      for h, r0, nk in units]
            dps = [lax.dot_general(do_b[u], v_ref[0:nk, cols[h]], NT_DIMS, preferred_element_type=F32)
                   for u, (h, r0, nk) in enumerate(units)]
            p_b, ds_b = [], []
            for u, (h, r0, nk) in enumerate(units):
                p = jnp.exp(scores[u] * MLA_SCALE - lse_ref[h, r0:r0 + rg])
                if diagonal:
                    row = r0 + lax.broadcasted_iota(jnp.int32, (rg, nk), 0)
                    p = jnp.where(lax.broadcasted_iota(jnp.int32, (rg, nk), 1) <= row, p, 0.0)
                delta = jnp.sum(dy_ref[r0:r0 + rg, cols[h]] * y_ref[r0:r0 + rg, cols[h]], axis=-1, keepdims=True)
                p_b.append(p.astype(BF16))
                ds_b.append((p * (dps[u] - delta) * MLA_SCALE).astype(BF16))
            for u, (h, r0, nk) in enumerate(units):
                dv_sc[h, 0:nk] += lax.dot_general(p_b[u], do_b[u], TN_DIMS, preferred_element_type=F32)
                dk_sc[h, 0:nk] += lax.dot_general(ds_b[u], q_ref[h, r0:r0 + rg, :], TN_DIMS, preferred_element_type=F32)
                rows = pl.ds(pl.multiple_of(i * t + r0, rg), rg)
                dq_ref[h, rows, :] += jnp.dot(ds_b[u], k_ref[h, 0:nk, :], preferred_element_type=F32)

        @pl.when(i > j)
        def _():
            update(False)

        @pl.when(i == j)
        def _():
            update(True)

        @pl.when(i == nt - 1)
        def _():
            dk_ref[...] = dk_sc[...]
            for h in range(hp):
                dv_ref[:, MLA_V * h:MLA_V * (h + 1)] = dv_sc[h]

    return _pcall(
        body, (q_cat, k_cat, v_b, y_b, lse, d_y), name="mla_bwd", grid=(nh // hp, qi.shape[0]), prefetch=(qi, kj),
        in_specs=[pl.BlockSpec((hp, t, 256), lambda g, p, qi, kj: (g, qi[p], 0)),
                  pl.BlockSpec((hp, t, 256), lambda g, p, qi, kj: (g, kj[p], 0)),
                  pl.BlockSpec((t, wv), lambda g, p, qi, kj: (kj[p], g)),
                  pl.BlockSpec((t, wv), lambda g, p, qi, kj: (qi[p], g)),
                  pl.BlockSpec((hp, t, 1), lambda g, p, qi, kj: (g, qi[p], 0)),
                  pl.BlockSpec((t, wv), lambda g, p, qi, kj: (qi[p], col0 + g))],
        out_specs=[pl.BlockSpec((hp, s, 256), lambda g, p, qi, kj: (g, 0, 0)),
                   pl.BlockSpec((hp, t, 256), lambda g, p, qi, kj: (g, kj[p], 0)),
                   pl.BlockSpec((t, wv), lambda g, p, qi, kj: (kj[p], g))],
        out_shape=[jax.ShapeDtypeStruct((nh, s, 256), F32), jax.ShapeDtypeStruct((nh, s, 256), F32),
                   jax.ShapeDtypeStruct((s, nh * MLA_V), F32)],
        scratch_shapes=[pltpu.VMEM((hp, t, 256), F32), pltpu.VMEM((hp, t, MLA_V), F32)],
        sem=("arbitrary", "arbitrary"), comm=comm)


MEM_SCALE = MEM_DIM ** -0.5


def _mem_kv_fwd(mem, g_mem, w_memkv, g_mk):
    m_len = mem.shape[0]

    def body(mem_ref, g_ref, w_ref, gk_ref, mn_ref, kv_ref, kn_ref, v_ref):
        mn, _, _ = _norm_full(mem_ref[...], g_ref[...])
        mn_b = mn.astype(BF16)
        mn_ref[...] = mn_b
        kv = jnp.dot(mn_b, w_ref[...], preferred_element_type=F32)
        kv_ref[...] = kv
        for h in range(MEM_HEADS):
            kn, _, _ = _norm_full(kv[:, 128 * h:128 * (h + 1)], gk_ref[...])
            kn_ref[:, 128 * h:128 * (h + 1)] = kn.astype(BF16)
        v_ref[...] = kv[:, 512:1024].astype(BF16)

    return pl.pallas_call(
        body, name="mem_kv_fwd",
        out_shape=[jax.ShapeDtypeStruct((m_len, D_MODEL), BF16), jax.ShapeDtypeStruct((m_len, 1024), F32),
                   jax.ShapeDtypeStruct((m_len, 512), BF16), jax.ShapeDtypeStruct((m_len, 512), BF16)],
        compiler_params=_params(),
    )(mem, g_mem, w_memkv, g_mk)


def _mem_kv_bwd(mem, g_mem, w_memkv, g_mk, mn_b, kv, d_kn, d_v):
    m_len = mem.shape[0]

    def body(mem_ref, g_ref, w_ref, gk_ref, mn_ref, kv_ref, dkn_ref, dv_ref, dw_ref, dgmem_ref, dgk_ref):
        parts = []
        dgk = jnp.zeros((1, LANES), F32)
        for h in range(MEM_HEADS):
            _, xhat, r = _norm_full(kv_ref[:, 128 * h:128 * (h + 1)], gk_ref[...])
            dx, dgh = _norm_full_bwd(dkn_ref[:, 128 * h:128 * (h + 1)], gk_ref[...], xhat, r)
            parts.append(dx)
            dgk = dgk + dgh
        d_kv = jnp.concatenate(parts + [dv_ref[...]], axis=1).astype(BF16)
        dw_ref[...] = lax.dot_general(mn_ref[...], d_kv, TN_DIMS, preferred_element_type=F32)
        d_mn = lax.dot_general(d_kv, w_ref[...], NT_DIMS, preferred_element_type=F32)
        _, xhat, _ = _norm_full(mem_ref[...], g_ref[...])
        dgmem_ref[...] = jnp.sum(d_mn * xhat, axis=0, keepdims=True)
        dgk_ref[...] = dgk

    return pl.pallas_call(
        body, name="mem_kv_bwd",
        out_shape=[jax.ShapeDtypeStruct((D_MODEL, 1024), F32), jax.ShapeDtypeStruct((1, D_MODEL), F32),
                   jax.ShapeDtypeStruct((1, LANES), F32)],
        compiler_params=_params(),
    )(mem, g_mem, w_memkv, g_mk, mn_b, kv, d_kn, d_v)


def _mem_softmax(raw):
    sc = raw * MEM_SCALE
    e = jnp.exp(sc - jnp.max(sc, axis=-1, keepdims=True))
    return e * (1.0 / jnp.sum(e, axis=-1, keepdims=True))


def _mem_attn_fwd(qm, km, vm, y_all, *, tm=512):
    s = qm.shape[0]
    tm = _tile(s, tm)
    m_len = km.shape[0]
    ycol = (SWA_Q_HEADS * SWA_HEAD_DIM + MLA_HEADS * MLA_V) // 512

    def body(q_ref, k_ref, v_ref, _, o_ref, yb_ref):
        cols = [slice(128 * h, 128 * (h + 1)) for h in range(MEM_HEADS)]
        raws = [lax.dot_general(q_ref[:, sl], k_ref[:, sl], NT_DIMS, preferred_element_type=F32) for sl in cols]
        probs = [_mem_softmax(raw).astype(BF16) for raw in raws]
        for p, sl in zip(probs, cols):
            out = jnp.dot(p, v_ref[:, sl], preferred_element_type=F32)
            o_ref[:, sl] = out
            yb_ref[:, sl] = out.astype(BF16)

    kvspec = pl.BlockSpec((m_len, 512), lambda i: (0, 0))
    return _pcall(
        body, (qm, km, vm, y_all), name="mem_attn_fwd", grid=(s // tm,),
        in_specs=[pl.BlockSpec((tm, 512), lambda i: (i, 0)), kvspec, kvspec, ANY],
        out_specs=[pl.BlockSpec((tm, 512), lambda i: (i, 0)), pl.BlockSpec((tm, 512), lambda i: (i, ycol))],
        out_shape=[jax.ShapeDtypeStruct((s, 512), F32), jax.ShapeDtypeStruct(y_all.shape, y_all.dtype)],
        sem=("parallel",), io_alias={3: 1})


def _mem_attn_bwd(qm, km, vm, y_m, d_y, *, tm=1024):
    s = qm.shape[0]
    tm = _tile(s, tm)
    m_len = km.shape[0]
    col0 = (SWA_Q_HEADS * SWA_HEAD_DIM + MLA_HEADS * MLA_V) // 512

    def body(q_ref, k_ref, v_ref, y_ref, dy_ref, dq_ref, dk_ref, dv_ref):
        @pl.when(pl.program_id(0) == 0)
        def _():
            dk_ref[...] = jnp.zeros_like(dk_ref)
            dv_ref[...] = jnp.zeros_like(dv_ref)

        cols = [slice(128 * h, 128 * (h + 1)) for h in range(MEM_HEADS)]
        do_b = [dy_ref[:, sl].astype(BF16) for sl in cols]
        raws = [lax.dot_general(q_ref[:, sl], k_ref[:, sl], NT_DIMS, preferred_element_type=F32) for sl in cols]
        dps = [lax.dot_general(do_b[h], v_ref[:, sl], NT_DIMS, preferred_element_type=F32) for h, sl in enumerate(cols)]
        p_b, ds_b = [], []
        for h, sl in enumerate(cols):
            p = _mem_softmax(raws[h])
            delta = jnp.sum(dy_ref[:, sl] * y_ref[:, sl], axis=-1, keepdims=True)
            p_b.append(p.astype(BF16))
            ds_b.append((p * (dps[h] - delta) * MEM_SCALE).astype(BF16))
        for h, sl in enumerate(cols):
            dv_ref[:, sl] += lax.dot_general(p_b[h], do_b[h], TN_DIMS, preferred_element_type=F32)
            dq_ref[:, sl] = jnp.dot(ds_b[h], k_ref[:, sl], preferred_element_type=F32)
            dk_ref[:, sl] += lax.dot_general(ds_b[h], q_ref[:, sl], TN_DIMS, preferred_element_type=F32)

    kvspec = pl.BlockSpec((m_len, 512), lambda i: (0, 0))
    row = pl.BlockSpec((tm, 512), lambda i: (i, 0))
    return pl.pallas_call(
        body, name="mem_attn_bwd", grid=(s // tm,),
        in_specs=[row, kvspec, kvspec, row, pl.BlockSpec((tm, 512), lambda i: (i, col0))],
        out_specs=[row, kvspec, kvspec],
        out_shape=[jax.ShapeDtypeStruct((s, 512), F32), jax.ShapeDtypeStruct((m_len, 512), F32),
                   jax.ShapeDtypeStruct((m_len, 512), F32)],
        compiler_params=_params(("arbitrary",)),
    )(qm, km, vm, y_m, d_y)


def _ffn_gate_up(fn, w_gate, w_up, *, tm=512, comm=()):
    s, d = fn.shape
    nsp, _, tf = w_gate.shape
    f = nsp * tf
    tm = _tile(s, tm)

    def body(x_ref, wg_ref, wu_ref, g_ref, u_ref, a_ref):
        x = x_ref[...]
        gate = jnp.dot(x, wg_ref[...], preferred_element_type=F32)
        up = jnp.dot(x, wu_ref[...], preferred_element_type=F32)
        g_ref[...] = gate.astype(BF16)
        u_ref[...] = up.astype(BF16)
        a_ref[...] = (gate * (1.0 / (1.0 + jnp.exp(-gate))) * up).astype(BF16)

    wspec = pl.BlockSpec((None, d, tf), lambda j, i: (j, 0, 0))
    ospec = pl.BlockSpec((tm, tf), lambda j, i: (i, j))
    osh = jax.ShapeDtypeStruct((s, f), BF16)
    return _pcall(body, (fn, w_gate, w_up), name="ffn_gate_up", grid=(nsp, s // tm),
                  in_specs=[pl.BlockSpec((tm, d), lambda j, i: (i, 0)), wspec, wspec],
                  out_specs=[ospec, ospec, ospec], out_shape=[osh, osh, osh], sem=("parallel", "parallel"), comm=comm)


def _ffn_bwd_act(d_out, w_down, gate, up, *, tm=1024, tf=1408, comm=()):
    s, d = d_out.shape
    f = w_down.shape[0]
    tm, tf = _tile(s, tm), _tile(f, tf)

    sub = tm // 4 if tm % 1024 == 0 else tm

    def body(do_ref, wd_ref, g_ref, u_ref, dg_ref, du_ref):
        groups = [slice(r, r + sub) for r in range(0, tm, sub)]
        parts = [lax.dot_general(do_ref[rows, :].astype(BF16), wd_ref[...], NT_DIMS, preferred_element_type=F32)
                 for rows in groups]
        for rows, d_act in zip(groups, parts):
            gate = g_ref[rows, :].astype(F32)
            sig = 1.0 / (1.0 + jnp.exp(-gate))
            du_ref[rows, :] = (d_act * (gate * sig)).astype(BF16)
            dg_ref[rows, :] = (d_act * u_ref[rows, :].astype(F32) * (sig * (1.0 + gate * (1.0 - sig)))).astype(BF16)

    ospec = pl.BlockSpec((tm, tf), lambda j, i: (i, j))
    osh = jax.ShapeDtypeStruct((s, f), BF16)
    return _pcall(
        body, (d_out, w_down, gate, up), name="ffn_bwd_act", grid=(f // tf, s // tm),
        in_specs=[pl.BlockSpec((tm, d), lambda j, i: (i, 0)), pl.BlockSpec((tf, d), lambda j, i: (j, 0)), ospec, ospec],
        out_specs=[ospec, ospec], out_shape=[osh, osh], sem=("parallel", "parallel"), comm=comm)


def _cols(g4):
    return jnp.concatenate([g4[k] for k in range(N_CHIPS)], axis=1)


def _full_w_in(g4, *, comm=()):
    rows = g4.shape[1]
    per = IN_WIDTH // N_CHIPS
    kr0 = 2304 - (N_CHIPS - 1) * per
    tr = _row_tile(rows, 256, 16)

    def body(g_ref, o_ref):
        for k in range(N_CHIPS - 1):
            o_ref[:, per * k:per * (k + 1)] = g_ref[k]
        o_ref[:, 2304 - kr0:2304] = g_ref[N_CHIPS - 1, :, 0:kr0]
        o_ref[:, 2304:C_KR] = g_ref[N_CHIPS - 1, :, kr0 + 64:per]
        o_ref[:, C_KR:C_KR + 64] = g_ref[N_CHIPS - 1, :, kr0:kr0 + 64]
        o_ref[:, C_KR + 64:] = jnp.zeros((tr, IN_PAD - C_KR - 64), BF16)

    return _pcall(body, (g4,), name="full_w_in", grid=(rows // tr,),
                  in_specs=[pl.BlockSpec((N_CHIPS, tr, per), lambda i: (0, i, 0))],
                  out_specs=pl.BlockSpec((tr, IN_PAD), lambda i: (i, 0)),
                  out_shape=jax.ShapeDtypeStruct((rows, IN_PAD), BF16), sem=("parallel",), comm=comm)


def _shards_w_in(dwp):
    per = IN_WIDTH // N_CHIPS
    kr0 = 2304 - (N_CHIPS - 1) * per
    last = jnp.concatenate([dwp[:, (N_CHIPS - 1) * per:2304], dwp[:, C_KR:C_KR + 64], dwp[:, 2304:C_KR]], axis=1)
    assert last.shape[1] == per and kr0 == 144
    return jnp.stack([dwp[:, per * k:per * (k + 1)] for k in range(N_CHIPS - 1)] + [last])


def _full_heads(g4, first):
    return jnp.concatenate([g4[k][:, :first] for k in range(N_CHIPS)] + [g4[k][:, first:] for k in range(N_CHIPS)], axis=1)


def _shards_heads(dwp, first, rest):
    base = N_CHIPS * first
    return jnp.stack([jnp.concatenate([dwp[:, first * k:first * (k + 1)], dwp[:, base + rest * k:base + rest * (k + 1)]], axis=1)
                      for k in range(N_CHIPS)])


def _rope_tables(pos):
    inv_freq = ROPE_THETA ** (-jnp.arange(0, MLA_ROPE, 2, dtype=F32) / MLA_ROPE)
    ang = pos.astype(F32)[:, None] * inv_freq
    cos, sin = jnp.cos(ang), jnp.sin(ang)
    return jnp.tile(cos, (1, 4)), jnp.concatenate([-sin, sin, -sin, sin], axis=1)


def _gain_table(sp):
    two = lambda v: jnp.tile(v, (1, 2))
    rows = [two(sp["swa_q_norm_g"]), two(sp["swa_k_norm_g"]), sp["mla_qn_norm_g"], two(sp["mla_qr_norm_g"]),
            sp["mla_kn_norm_g"], two(sp["mla_kr_norm_g"]), sp["mem_q_norm_g"], jnp.zeros((1, LANES), F32)]
    return jnp.concatenate(rows, axis=0)


CHIP_DISTANCES = (1, 2, 3)


def _place():
    x, y, c = lax.axis_index("x"), lax.axis_index("y"), lax.axis_index("c")
    return x, y, c, 2 * x + y


def _chip_at(x, y, d):
    px = 1 - x if d & 2 else x
    py = 1 - y if d & 1 else y
    return px, py, 2 * px + py


def _row_tile(rows, want=512, mult=8):
    t = min(rows, want)
    t -= t % mult
    while rows % t:
        t -= mult
    return t


def _cast_into_slot(w, meta, *, name, comm=()):
    rows, cols = w.shape
    tr = _row_tile(rows, 512, 16)

    def body(meta_ref, w_ref, o_ref):
        o_ref[...] = w_ref[...].astype(BF16)

    return _pcall(body, (w,), name=name, grid=(rows // tr,), prefetch=(meta,),
                  in_specs=[pl.BlockSpec((tr, cols), lambda i, m: (i, 0))],
                  out_specs=pl.BlockSpec((None, tr, cols), lambda i, m: (m[0], i, 0)),
                  out_shape=jax.ShapeDtypeStruct((N_CHIPS, rows, cols), BF16), sem=("parallel",), comm=comm)


def _remote(src, dst, ssem, rsem, i, device):
    return pltpu.make_async_remote_copy(src_ref=src, dst_ref=dst, send_sem=ssem.at[i], recv_sem=rsem.at[i],
                                        device_id=device, device_id_type=MESH)


def _symmetric_stage(ins, out_shapes, aliases, n_sem, copies):
    def issue(i_refs, o_refs, ssem, rsem):
        for send, _ in copies(i_refs, o_refs, ssem, rsem):
            send.start()

    def wait(i_refs, o_refs, ssem, rsem):
        pairs = copies(i_refs, o_refs, ssem, rsem)
        for _, arrival in pairs:
            arrival.wait_recv()
        for send, _ in pairs:
            send.wait_send()

    return _Stage(ins, out_shapes, aliases, n_sem, issue, wait)


def _gather_stage(slots, leg, part=(0, 1)):
    n = len(slots)
    shapes = [jax.ShapeDtypeStruct(s.shape, s.dtype) for s in slots]
    in_place = {w: w for w in range(n)}
    if not isinstance(leg, str):
        legs = list(leg)

        def copies(i_refs, o_refs, ssem, rsem):
            return [pr for k, (which, prt) in enumerate(legs)
                    for pr in _gather_stage(slots, which, prt).leg_copies(which, 3 * n * k)(i_refs, o_refs, ssem, rsem)]

        return _symmetric_stage(slots, shapes, in_place, 3 * n * len(legs), copies)

    def leg_copies(which, base):
        def copies(_, outs, ssem, rsem):
            x, y, c, k_me = _place()
            pairs = []
            for w in range(n):
                half = outs[w].shape[1] // 2
                r0, size = _window(half, part)
                slab = lambda k, cc, w=w, half=half, r0=r0, size=size: outs[w].at[k, pl.ds(cc * half + r0, size)]
                for d in CHIP_DISTANCES:
                    px, py, k_src = _chip_at(x, y, d)
                    i = base + 3 * w + d - 1
                    if which == "ici":
                        pairs.append((_remote(slab(k_me, c), slab(k_me, c), ssem, rsem, i, (px, py, c)),
                                      _remote(slab(k_src, c), slab(k_src, c), ssem, rsem, i, (x, y, c))))
                    else:
                        pairs.append((_remote(slab(k_src, c), slab(k_src, c), ssem, rsem, i, (x, y, 1 - c)),
                                      _remote(slab(k_src, 1 - c), slab(k_src, 1 - c), ssem, rsem, i, (x, y, c))))
            return pairs
        return copies

    if leg != "both":
        st = _symmetric_stage(slots, shapes, in_place, 3 * n, leg_copies(leg, 0))
        st.leg_copies = leg_copies
        return st
    ici = _symmetric_stage(slots, shapes, in_place, 6 * n, leg_copies("ici", 0))
    d2d = _symmetric_stage(slots, shapes, in_place, 6 * n, leg_copies("d2d", 3 * n))

    def mid(*refs):
        ici.wait(*refs)
        d2d.issue(*refs)

    return _Stage(slots, shapes, in_place, 6 * n, ici.issue, d2d.wait, mid)


def _halves_stage(grads):
    n = len(grads)

    def copies(ins, outs, ssem, rsem):
        x, y, c, _ = _place()
        pairs = []
        for w in range(n):
            half = ins[w].shape[1] // 2
            pairs.append((_remote(ins[w].at[:, pl.ds((1 - c) * half, half)], outs[w], ssem, rsem, w, (x, y, 1 - c)),
                          _remote(outs[w], outs[w], ssem, rsem, w, (x, y, c))))
        return pairs

    shapes = [jax.ShapeDtypeStruct((N_CHIPS, g.shape[1] // 2, g.shape[2]), g.dtype) for g in grads]
    return _symmetric_stage(grads, shapes, {}, n, copies)


def _window(rows, part):
    idx, count = part
    size = rows // count
    assert size * count == rows and size % 16 == 0, (rows, part)
    return idx * size, size


def _chips_stage(parts, part=(0, 1), into=None):
    n = len(parts)

    def copies(ins, outs, ssem, rsem):
        x, y, c, _ = _place()
        pairs = []
        for w in range(n):
            r0, size = _window(ins[w].shape[1], part)
            for d in CHIP_DISTANCES:
                px, py, _ = _chip_at(x, y, d)
                i = 3 * w + d - 1
                land = outs[w].at[d - 1, pl.ds(r0, size)]
                pairs.append((_remote(ins[w].at[d - 1, pl.ds(r0, size)], land, ssem, rsem, i, (px, py, c)),
                              _remote(land, land, ssem, rsem, i, (x, y, c))))
        return pairs

    shapes = [jax.ShapeDtypeStruct(p.shape, p.dtype) for p in parts]
    if into is None:
        return _symmetric_stage(parts, shapes, {}, 3 * n, copies)
    return _symmetric_stage(list(parts) + list(into), shapes, {n + w: w for w in range(n)}, 3 * n, copies)


def _swap_stage(totals):
    n = len(totals)

    def copies(ins, outs, ssem, rsem):
        x, y, c, _ = _place()
        return [(_remote(ins[w], outs[w], ssem, rsem, w, (x, y, 1 - c)),
                 _remote(outs[w], outs[w], ssem, rsem, w, (x, y, c))) for w in range(n)]

    shapes = [jax.ShapeDtypeStruct(t.shape, t.dtype) for t in totals]
    return _symmetric_stage(totals, shapes, {}, n, copies)


def _run_stages(stages, *, name):
    n_ins = [len(st.ins) for st in stages]
    n_outs = [len(st.out_shapes) for st in stages]
    tot_in, tot_out = sum(n_ins), sum(n_outs)
    aliases, i0, o0 = {}, 0, 0
    for st, ni, no in zip(stages, n_ins, n_outs):
        aliases.update({i0 + a: o0 + b for a, b in st.aliases.items()})
        i0, o0 = i0 + ni, o0 + no

    def body(*refs):
        sems = refs[tot_in + tot_out:]
        for what in ("issue", "wait"):
            i0, o0 = 0, tot_in
            for k, (st, ni, no) in enumerate(zip(stages, n_ins, n_outs)):
                getattr(st, what)(refs[i0:i0 + ni], refs[o0:o0 + no], sems[2 * k], sems[2 * k + 1])
                i0, o0 = i0 + ni, o0 + no

    sem = pltpu.SemaphoreType.DMA
    res = pl.pallas_call(
        body, name=name, in_specs=[ANY] * tot_in, out_specs=[ANY] * tot_out,
        out_shape=[s for st in stages for s in st.out_shapes], input_output_aliases=aliases,
        scratch_shapes=[sem((st.n_sem,)) for st in stages for _ in range(2)],
    )(*[a for st in stages for a in st.ins])
    outs, o0 = [], 0
    for no in n_outs:
        outs.append(list(res[o0:o0 + no]))
        o0 += no
    return outs


def _add_pair(meta, g4, recv, *, name):
    nsh, rows, cols = g4.shape
    half = rows // 2
    tr = _row_tile(half, 128 if cols > 1024 else 256, 16)
    nt = half // tr

    def body(meta_ref, g0, g1, g2, g3, r0, r1, r2, r3, own_ref, oth_ref):
        own_ref[...] = g0[...] + r0[...]
        for d, (g, r) in enumerate(((g1, r1), (g2, r2), (g3, r3))):
            oth_ref[d] = (g[...] + r[...]).astype(BF16)

    blk = (None, tr, cols)
    gspec = lambda d: pl.BlockSpec(blk, lambda i, m: (jnp.bitwise_xor(m[0], d), m[1] * nt + i, 0))
    rspec = lambda d: pl.BlockSpec(blk, lambda i, m: (jnp.bitwise_xor(m[0], d), i, 0))
    grid_spec = pltpu.PrefetchScalarGridSpec(
        num_scalar_prefetch=1, grid=(nt,),
        in_specs=[gspec(d) for d in range(nsh)] + [rspec(d) for d in range(nsh)],
        out_specs=[pl.BlockSpec((tr, cols), lambda i, m: (i, 0)), pl.BlockSpec((3, tr, cols), lambda i, m: (0, i, 0))])
    return pl.pallas_call(
        body, name=name, grid_spec=grid_spec,
        out_shape=[jax.ShapeDtypeStruct((half, cols), F32), jax.ShapeDtypeStruct((3, half, cols), BF16)],
        compiler_params=_params(("parallel",)),
    )(meta, g4, g4, g4, g4, recv, recv, recv, recv)


def _add_chips(own, recv, *, name):
    half, cols = own.shape
    tr = _row_tile(half, 256, 16)

    def body(p_ref, r_ref, o_ref):
        o_ref[...] = ((p_ref[...] + r_ref[0].astype(F32)) + r_ref[1].astype(F32)) + r_ref[2].astype(F32)

    return pl.pallas_call(
        body, name=name, grid=(half // tr,),
        in_specs=[pl.BlockSpec((tr, cols), lambda i: (i, 0)), pl.BlockSpec((3, tr, cols), lambda i: (0, i, 0))],
        out_specs=pl.BlockSpec((tr, cols), lambda i: (i, 0)),
        out_shape=jax.ShapeDtypeStruct((half, cols), F32),
        compiler_params=_params(("parallel",)),
    )(own, recv)


def _adamw_math(w, g, m, v):
    m = ADAM_B1 * m + (1.0 - ADAM_B1) * g
    v = ADAM_B2 * v + (1.0 - ADAM_B2) * (g * g)
    m_hat = m / (1.0 - ADAM_B1 ** ADAM_STEP)
    v_hat = v / (1.0 - ADAM_B2 ** ADAM_STEP)
    delta = -ADAM_LR * (m_hat / (jnp.sqrt(v_hat) + ADAM_EPS) + ADAM_WD * w)
    return delta, m, v


def _adamw(meta, w, g_mine, g_theirs, m, v, *, name):
    rows, cols = w.shape
    half = rows // 2
    tr = _row_tile(half, 256)
    nt = half // tr

    def body(meta_ref, w_ref, a_ref, b_ref, m_ref, v_ref, g_ref, d_ref, mo_ref, vo_ref):
        is_mine = (pl.program_id(0) // nt) == meta_ref[1]
        g = jnp.where(is_mine, a_ref[...], b_ref[...])
        g_ref[...] = g
        d_ref[...], mo_ref[...], vo_ref[...] = _adamw_math(w_ref[...], g, m_ref[...], v_ref[...])

    blk = pl.BlockSpec((tr, cols), lambda i, mt: (i, 0))
    mine = pl.BlockSpec((tr, cols), lambda i, mt: (jnp.where(i // nt == mt[1], i % nt, 0), 0))
    theirs = pl.BlockSpec((tr, cols), lambda i, mt: (jnp.where(i // nt == mt[1], 0, i % nt), 0))
    sh = jax.ShapeDtypeStruct((rows, cols), F32)
    grid_spec = pltpu.PrefetchScalarGridSpec(
        num_scalar_prefetch=1, grid=(rows // tr,),
        in_specs=[blk, mine, theirs, blk, blk], out_specs=[blk] * 4)
    return pl.pallas_call(
        body, name=name, grid_spec=grid_spec, out_shape=[sh] * 4,
        compiler_params=_params(("arbitrary",)),
    )(meta, w, g_mine, g_theirs, m, v)


N_DEVICES = 8


def _small_step(g_pack, w_pack, m_pack, v_pack):
    rows = g_pack.shape[0]

    def body(g_ref, w_ref, m_ref, v_ref, sum_ref, d_ref, mo_ref, vo_ref, slots, ssem, rsem):
        x, y, c, _ = _place()
        me = 4 * x + 2 * y + c
        slots[me] = g_ref[...]
        copies = []
        for r in range(1, N_DEVICES):
            px = 1 - x if r & 4 else x
            py = 1 - y if r & 2 else y
            pc = 1 - c if r & 1 else c
            copies.append(pltpu.make_async_remote_copy(
                src_ref=g_ref, dst_ref=slots.at[me], send_sem=ssem.at[r - 1], recv_sem=rsem.at[r - 1],
                device_id=(px, py, pc), device_id_type=MESH))
        for cp in copies:
            cp.start()
        for r in range(1, N_DEVICES):
            src = jnp.bitwise_xor(me, r)
            pltpu.make_async_remote_copy(
                src_ref=g_ref, dst_ref=slots.at[src], send_sem=ssem.at[r - 1], recv_sem=rsem.at[r - 1],
                device_id=(x, y, c), device_id_type=MESH).wait_recv()
        for cp in copies:
            cp.wait_send()
        total = slots[0]
        for k in range(1, N_DEVICES):
            total = total + slots[k]
        sum_ref[...] = total
        d_ref[...], mo_ref[...], vo_ref[...] = _adamw_math(w_ref[...], total, m_ref[...], v_ref[...])

    sh = jax.ShapeDtypeStruct((rows, LANES), F32)
    vm = pl.BlockSpec(memory_space=pltpu.VMEM)
    return pl.pallas_call(
        body, name="small_allreduce_adamw",
        in_specs=[vm] * 4, out_specs=[vm] * 4, out_shape=[sh] * 4,
        scratch_shapes=[pltpu.VMEM((N_DEVICES, rows, LANES), F32),
                        pltpu.SemaphoreType.DMA((N_DEVICES - 1,)), pltpu.SemaphoreType.DMA((N_DEVICES - 1,))],
    )(g_pack, w_pack, m_pack, v_pack)


WEIGHTS = ("attn_norm_g", "w_in", "swa_q_norm_g", "swa_k_norm_g", "swa_sinks", "mla_cq_norm_g", "mla_ckv_norm_g",
           "w_uq", "w_ukv", "mla_qn_norm_g", "mla_qr_norm_g", "mla_kn_norm_g", "mla_kr_norm_g", "mem_norm_g",
           "w_mem_kv", "mem_q_norm_g", "mem_k_norm_g", "w_out", "ffn_norm_g", "w_gate", "w_up", "w_down")
BIG = ("w_in", "w_uq", "w_ukv", "w_mem_kv", "w_out", "w_gate", "w_up", "w_down")
SMALL = tuple(n for n in WEIGHTS if n not in BIG)
PACK_UNIT = 8 * LANES


def _pack(parts):
    flat = jnp.concatenate(parts, axis=1)
    total = flat.shape[1]
    padded = -(-total // PACK_UNIT) * PACK_UNIT
    return jnp.pad(flat, ((0, 0), (0, padded - total))).reshape(padded // LANES, LANES)


def _unpack(buf, sizes):
    flat = buf.reshape(1, buf.shape[0] * LANES)
    out, at = [], 0
    for n in sizes:
        out.append(flat[:, at:at + n])
        at += n
    return out


def kernel(x, mem, positions, attn_norm_g, w_in, swa_q_norm_g, swa_k_norm_g, swa_sinks, mla_cq_norm_g, mla_ckv_norm_g, w_uq, w_ukv, mla_qn_norm_g, mla_qr_norm_g, mla_kn_norm_g, mla_kr_norm_g, mem_norm_g, w_mem_kv, mem_q_norm_g, mem_k_norm_g, w_out, ffn_norm_g, w_gate, w_up, w_down, loss_target, m_attn_norm_g, m_w_in, m_swa_q_norm_g, m_swa_k_norm_g, m_swa_sinks, m_mla_cq_norm_g, m_mla_ckv_norm_g, m_w_uq, m_w_ukv, m_mla_qn_norm_g, m_mla_qr_norm_g, m_mla_kn_norm_g, m_mla_kr_norm_g, m_mem_norm_g, m_w_mem_kv, m_mem_q_norm_g, m_mem_k_norm_g, m_w_out, m_ffn_norm_g, m_w_gate, m_w_up, m_w_down, v_attn_norm_g, v_w_in, v_swa_q_norm_g, v_swa_k_norm_g, v_swa_sinks, v_mla_cq_norm_g, v_mla_ckv_norm_g, v_w_uq, v_w_ukv, v_mla_qn_norm_g, v_mla_qr_norm_g, v_mla_kn_norm_g, v_mla_kr_norm_g, v_mem_norm_g, v_w_mem_kv, v_mem_q_norm_g, v_mem_k_norm_g, v_w_out, v_ffn_norm_g, v_w_gate, v_w_up, v_w_down):
    given = dict(locals())
    wts = {n: given[n] for n in WEIGHTS}
    mom_m = {n: given["m_" + n] for n in WEIGHTS}
    mom_v = {n: given["v_" + n] for n in WEIGHTS}

    mx, my, mc = lax.axis_index("x"), lax.axis_index("y"), lax.axis_index("c")
    meta = jnp.stack([2 * mx + my, mc]).astype(jnp.int32)
    x, mem, pos, target = x[0], mem[0], positions[0], loss_target[0]
    sp = {n: wts[n] for n in SMALL}
    s = x.shape[0]
    cos_t, sin_t = _rope_tables(pos)
    pos_f = pos.astype(F32)
    pos_col, pos_row = pos_f.reshape(s, 1), pos_f.reshape(1, s)
    g128 = _gain_table(sp)
    sinks = sp["swa_sinks"].reshape(SWA_Q_HEADS)
    gcq, gckv = sp["mla_cq_norm_g"], sp["mla_ckv_norm_g"]
    gs = {}

    slot = {n: _cast_into_slot(wts[n][0], meta, name="cast_" + n) for n in BIG if n not in ("w_gate", "w_up", "w_down")}
    first = [slot["w_in"], slot["w_uq"], slot["w_ukv"]]
    slot["w_gate"], [first] = _cast_into_slot(wts["w_gate"][0], meta, name="cast_w_gate",
                                              comm=[_gather_stage(first, "ici", (0, 4))])
    slot["w_up"], [first] = _cast_into_slot(wts["w_up"][0], meta, name="cast_w_up",
                                            comm=[_gather_stage(first, [("ici", (1, 4)), ("d2d", (0, 4))])])
    slot["w_down"], [first] = _cast_into_slot(wts["w_down"][0], meta, name="cast_w_down",
                                              comm=[_gather_stage(first, [("ici", (2, 4)), ("d2d", (1, 4))])])
    hn, [first] = _rms_fwd(x, sp["attn_norm_g"], name="attn_norm_fwd",
                           comm=[_gather_stage(first, [("ici", (3, 4)), ("d2d", (2, 4))])])
    [first] = _run_stages([_gather_stage(first, "d2d", (3, 4))], name="gather_first_last_d2d")
    w_in_f, [wm] = _full_w_in(first[0], comm=[_gather_stage([slot["w_mem_kv"]], "ici")])
    w_uq_f, w_ukv_f = _full_heads(first[1], MLA_NOPE), _full_heads(first[2], MLA_NOPE)

    eighths = lambda leg, ks: [(leg, (k, 8)) for k in ks]
    proj, [wm, wo, wg] = _matmul(hn, w_in_f, name="in_proj",
                                 comm=[_gather_stage(wm, "d2d"), _gather_stage([slot["w_out"]], "ici"),
                                       _gather_stage([slot["w_gate"]], eighths("ici", (0,)))])
    (qa, ka, va, q_cat, k_cat, v_b, qm), [wo, wg] = _attn_prep_fwd(
        proj, g128, gcq, gckv, w_uq_f, w_ukv_f, cos_t, sin_t,
        comm=[_gather_stage(wo, "d2d"), _gather_stage(wg, eighths("ici", (1, 2)))])
    w_mem_kv_f = wm[0].reshape(D_MODEL, 2 * MEM_HEADS * MEM_DIM)
    w_out_f = wo[0].reshape(D_MODEL, D_MODEL)
    mn_b, kv_m, km, vm = _mem_kv_fwd(mem, sp["mem_norm_g"], w_mem_kv_f, sp["mem_k_norm_g"])
    (y_a, y), [wg] = _swa_fwd(qa, ka, va, pos_col, pos_row, sinks, comm=[_gather_stage(wg, eighths("ici", (3, 4, 5, 6)))])
    (y_b, lse, y), [wg, wu] = _mla_fwd(
        q_cat, k_cat, v_b, y, comm=[_gather_stage(wg, eighths("ici", (7,))),
                                    _gather_stage([slot["w_up"]], eighths("ici", (0, 1, 2, 3, 4)))])
    y_m, y = _mem_attn_fwd(qm, km, vm, y)
    (h1, fn), [wu, wg] = _matmul(y, w_out_f, add=x, norm_g=sp["ffn_norm_g"], name="out_proj", tm=512, tn=D_MODEL,
                                 comm=[_gather_stage(wu, eighths("ici", (5, 6, 7)) + eighths("d2d", (0, 1, 2, 3, 4))),
                                       _gather_stage(wg, "d2d")])
    [wu] = _run_stages([_gather_stage(wu, eighths("d2d", (5, 6, 7)))], name="gather_w_up_last_d2d")
    w_gate_f, w_up_f = wg[0], wu[0]
    (gate, up, act), [wd] = _ffn_gate_up(fn, w_gate_f, w_up_f, comm=[_gather_stage([slot["w_down"]], "both")])
    w_down_f = wd[0].reshape(D_FF, D_MODEL)
    d_out, d_out_b, loss_tile = _matmul(act, w_down_f, add=h1, name="down_proj", tm=512, tk=D_FF, loss_target=target)

    add_pair = lambda n, g4, r: _add_pair(meta, g4, r, name="grad_add_pair_" + n)
    add_chips = lambda n, own, r: _add_chips(own, r, name="grad_add_chips_" + n)
    mine, theirs = {}, {}

    dw_down = _matmul(act, d_out_b, ta=True, name="dw_down", tm=512, tn=1024, tk=s)
    dw_down = dw_down.reshape(N_CHIPS, D_FF // N_CHIPS, D_MODEL)
    (d_gate, d_up), [[r]] = _ffn_bwd_act(d_out_b, w_down_f, gate, up, comm=[_halves_stage([dw_down])])
    own_d, oth_d = add_pair("w_down", dw_down, r)
    dw_gate, [rd] = _matmul(fn, d_gate, ta=True, name="dw_gate", tm=512, tk=s, tn=D_FF // N_CHIPS, out_split=N_CHIPS,
                            comm=[_chips_stage([oth_d], (0, 2))])
    dw_up, [[r], rd] = _matmul(fn, d_up, ta=True, name="dw_up", tm=512, tk=s, tn=D_FF // N_CHIPS, out_split=N_CHIPS,
                               comm=[_halves_stage([dw_gate]), _chips_stage([oth_d], (1, 2), into=rd)])
    mine["w_down"] = add_chips("w_down", own_d, rd[0])
    own_g, oth_g = add_pair("w_gate", dw_gate, r)
    d_fn, [[r], rg, [theirs["w_down"]]] = _matmul(
        d_gate, w_gate_f, tb=True, b_split=True, pair2=(d_up, w_up_f), name="dfn", tm=512, tn=512,
        comm=[_halves_stage([dw_up]), _chips_stage([oth_g]), _swap_stage([mine["w_down"]])])
    mine["w_gate"] = add_chips("w_gate", own_g, rg[0])
    own_u, oth_u = add_pair("w_up", dw_up, r)
    d_h1, d_h1_b, gs["ffn_norm_g"] = _rms_bwd(d_fn, h1, sp["ffn_norm_g"], d_out, name="ffn_norm_bwd")
    dw_out, [[theirs["w_gate"]]] = _matmul(y, d_h1_b, ta=True, name="dw_out", tm=512, tk=s,
                                           comm=[_swap_stage([mine["w_gate"]])])
    dw_out = dw_out.reshape(N_CHIPS, D_MODEL // N_CHIPS, D_MODEL)
    d_y, [[r]] = _matmul(d_h1_b, w_out_f, tb=True, name="dy", comm=[_halves_stage([dw_out])])
    own_o, oth_o = add_pair("w_out", dw_out, r)
    (d_qa, d_ka, d_va, d_sink), [ru] = _swa_bwd(qa, ka, va, pos_col, pos_row, sinks, y_a, d_y,
                                                comm=[_chips_stage([oth_u], (0, 2))])
    (d_qcat, d_kcat, d_vb), [ru, [r]] = _mla_bwd(
        q_cat, k_cat, v_b, y_b, lse, d_y, comm=[_chips_stage([oth_u], (1, 2), into=ru), _chips_stage([oth_o])])
    mine["w_up"] = add_chips("w_up", own_u, ru[0])
    mine["w_out"] = add_chips("w_out", own_o, r)
    d_qm, d_km, d_vm = _mem_attn_bwd(qm, km, vm, y_m, d_y)
    (d_proj, dw_uq, dw_ukv, dg128, gs["mla_cq_norm_g"], gs["mla_ckv_norm_g"]), [[theirs["w_up"], theirs["w_out"]]] = \
        _attn_prep_bwd(proj, g128, gcq, gckv, w_uq_f, w_ukv_f, cos_t, sin_t, d_qa, d_ka, d_va, d_qcat, d_kcat, d_vb,
                       d_qm, comm=[_swap_stage([mine["w_up"], mine["w_out"]])])
    dw_mem_kv, gs["mem_norm_g"], gs["mem_k_norm_g"] = _mem_kv_bwd(
        mem, sp["mem_norm_g"], w_mem_kv_f, sp["mem_k_norm_g"], mn_b, kv_m, d_km, d_vm)
    late = ("w_uq", "w_ukv", "w_mem_kv")
    late_g = [_shards_heads(dw_uq, MLA_NOPE, MLA_ROPE), _shards_heads(dw_ukv, MLA_NOPE, MLA_V),
              dw_mem_kv.reshape(N_CHIPS, D_MODEL // N_CHIPS, -1)]
    dw_in, [rs] = _matmul(hn, d_proj, ta=True, name="dw_in", tm=512, tk=s, comm=[_halves_stage(late_g)])
    late_sums = [add_pair(n, g4, r) for n, g4, r in zip(late, late_g, rs)]
    dw_in = _shards_w_in(dw_in)
    d_hn, [rs, [r]] = _matmul(d_proj, w_in_f, tb=True, name="dhn", tk=1536,
                              comm=[_chips_stage([oth for _, oth in late_sums]), _halves_stage([dw_in])])
    for n, (own, _), r_n in zip(late, late_sums, rs):
        mine[n] = add_chips(n, own, r_n)
    own_i, oth_i = add_pair("w_in", dw_in, r)
    (grad_x, _, gs["attn_norm_g"]), [[r], late_theirs] = _rms_bwd(
        d_hn, x, sp["attn_norm_g"], d_h1, name="attn_norm_bwd",
        comm=[_chips_stage([oth_i]), _swap_stage([mine[n] for n in late])])
    theirs.update(zip(late, late_theirs))
    mine["w_in"] = add_chips("w_in", own_i, r)
    [[theirs["w_in"]]] = _run_stages([_swap_stage([mine["w_in"]])], name="grad_swap_w_in")

    fold = lambda r: r[:, :64] + r[:, 64:]
    gs["swa_q_norm_g"] = fold(dg128[G_SWA_Q:G_SWA_Q + 1])
    gs["swa_k_norm_g"] = fold(dg128[G_SWA_K:G_SWA_K + 1])
    gs["mla_qn_norm_g"] = dg128[G_QN:G_QN + 1]
    gs["mla_qr_norm_g"] = fold(dg128[G_QR:G_QR + 1])
    gs["mla_kn_norm_g"] = dg128[G_KN:G_KN + 1]
    gs["mla_kr_norm_g"] = fold(dg128[G_KR:G_KR + 1])
    gs["mem_q_norm_g"] = dg128[G_MQ:G_MQ + 1]
    gs["swa_sinks"] = d_sink[:, :SWA_Q_HEADS]

    grad, delta, new_m, new_v = {}, {}, {}, {}
    for n in BIG:
        g2, d, m2, v2 = _adamw(meta, wts[n][0], mine[n], theirs[n], mom_m[n][0], mom_v[n][0], name="adamw_" + n)
        grad[n], delta[n], new_m[n], new_v[n] = g2[None], d[None], m2[None], v2[None]

    sizes = [wts[n].shape[1] for n in SMALL]
    zero = jnp.zeros((1, LANES), F32)
    packs = _small_step(_pack([gs[n] for n in SMALL] + [loss_tile]), _pack([wts[n] for n in SMALL] + [zero]),
                        _pack([mom_m[n] for n in SMALL] + [zero]), _pack([mom_v[n] for n in SMALL] + [zero]))
    for store, buf in zip((grad, delta, new_m, new_v), packs):
        for n, val in zip(SMALL, _unpack(buf, sizes)):
            store[n] = val
    loss = _unpack(packs[0], sizes + [LANES])[-1][0, 0]

    return (loss, grad_x[None], *[grad[n] for n in WEIGHTS], *[delta[n] for n in WEIGHTS],
            *[new_m[n] for n in WEIGHTS], *[new_v[n] for n in WEIGHTS])
```
